```python
import math
import jax, jax.numpy as jnp
from jax import lax
import numpy as np

D_MODEL = 1024
BATCH = 8
SEQ = 4096
DEPTH = 2

RG_HEADS = 10
RG_HEAD_DIM = 64
RG_WIDTH = RG_HEADS * RG_HEAD_DIM
S5_GROUP_DIM = 16
S5_GROUPS = 24
S5_WIDTH = S5_GROUPS * S5_GROUP_DIM
S5_STATE = 64
MIX_WIDTH = RG_WIDTH + S5_WIDTH
IN_PROJ_WIDTH = 2 * RG_WIDTH + 2 * S5_WIDTH
CONV_WIDTH = 4
RG_C = 8.0
PLE_DIM = 256
LN_EPS = 1e-5
DEEPNORM_ALPHA = (2.0 * DEPTH) ** 0.25
DEEPNORM_BETA = (8.0 * DEPTH) ** -0.25

kernel_name = "hymba_style_rglru_s5_deepnorm"


def _layer_norm(x, g, b):
    xf = x.astype(jnp.float32)
    mu = jnp.mean(xf, axis=-1, keepdims=True)
    var = jnp.mean(jnp.square(xf - mu), axis=-1, keepdims=True)
    return ((xf - mu) * lax.rsqrt(var + LN_EPS) * g + b).astype(x.dtype)


def _causal_depthwise_conv(x, w, b):
    L = x.shape[1]
    xp = jnp.pad(x, ((0, 0), (CONV_WIDTH - 1, 0), (0, 0)))
    y = xp[:, 0:L] * w[0]
    for k in range(1, CONV_WIDTH):
        y = y + xp[:, k:k + L] * w[k]
    return y + b


def _rg_lru(x, wa, ba, wx, bx, lam):
    bsz, L, W = x.shape
    f32 = jnp.float32
    xh = x.reshape(bsz, L, RG_HEADS, RG_HEAD_DIM)
    r = jax.nn.sigmoid(jnp.einsum('blhi,hij->blhj', xh, wa).reshape(bsz, L, W) + ba)
    i = jax.nn.sigmoid(jnp.einsum('blhi,hij->blhj', xh, wx).reshape(bsz, L, W) + bx)
    log_a = -RG_C * r.astype(f32) * jax.nn.softplus(-lam.astype(f32))
    a = jnp.exp(log_a)
    mult = jnp.sqrt(-jnp.expm1(2.0 * log_a))
    u = mult * (i * x).astype(f32)

    def step(h, inp):
        a_t, u_t = inp
        h = a_t * h + u_t
        return h, h

    h0 = jnp.zeros((bsz, W), f32)
    _, hs = lax.scan(step, h0, (jnp.swapaxes(a, 0, 1), jnp.swapaxes(u, 0, 1)))
    return jnp.swapaxes(hs, 0, 1).astype(x.dtype)


def _s5(u, a_re, a_im, b_re, b_im, c_re, c_im, d, log_step, w_glu, b_glu):
    bsz, L, _ = u.shape
    f32 = jnp.float32
    uh = u.reshape(bsz, L, S5_GROUPS, S5_GROUP_DIM).astype(f32)
    dt = jnp.exp(log_step.astype(f32))[:, None]
    ar, ai = a_re.astype(f32), a_im.astype(f32)
    mag = jnp.exp(ar * dt)
    lb_re = mag * jnp.cos(ai * dt)
    lb_im = mag * jnp.sin(ai * dt)
    den = ar * ar + ai * ai
    coef_re = ((lb_re - 1.0) * ar + lb_im * ai) / den
    coef_im = (lb_im * ar - (lb_re - 1.0) * ai) / den
    br, bi = b_re.astype(f32), b_im.astype(f32)
    bb_re = coef_re[..., None] * br - coef_im[..., None] * bi
    bb_im = coef_re[..., None] * bi + coef_im[..., None] * br
    bu_re = jnp.einsum('blgh,gph->blgp', uh, bb_re)
    bu_im = jnp.einsum('blgh,gph->blgp', uh, bb_im)
    shape = bu_re.shape
    elems = (jnp.broadcast_to(lb_re, shape), jnp.broadcast_to(lb_im, shape), bu_re, bu_im)

    def combine(e1, e2):
        a1r, a1i, b1r, b1i = e1
        a2r, a2i, b2r, b2i = e2
        return (a2r * a1r - a2i * a1i,
                a2r * a1i + a2i * a1r,
                a2r * b1r - a2i * b1i + b2r,
                a2r * b1i + a2i * b1r + b2i)

    _, _, s_re, s_im = lax.associative_scan(combine, elems, axis=1)
    y = (jnp.einsum('blgp,ghp->blgh', s_re, c_re.astype(f32))
         - jnp.einsum('blgp,ghp->blgh', s_im, c_im.astype(f32))
         + d.astype(f32) * uh)
    y = jax.nn.gelu(y.reshape(bsz, L, S5_WIDTH))
    y = y * jax.nn.sigmoid(y @ w_glu.astype(f32) + b_glu.astype(f32))
    return y.astype(u.dtype)


def _fwd_setup_inputs(seed: int = 0) -> dict:
    key = jax.random.key(seed)
    ks = iter(jax.random.split(key, 40))
    f32 = jnp.float32

    def nrm(shape, scale):
        return scale * jax.random.normal(next(ks), shape, f32)

    x = nrm((BATCH, SEQ, D_MODEL), 1.0)
    p = nrm((DEPTH, BATCH, SEQ, PLE_DIM), 1.0)
    w_in = nrm((DEPTH, D_MODEL, IN_PROJ_WIDTH), D_MODEL ** -0.5)
    conv_w = nrm((DEPTH, CONV_WIDTH, RG_WIDTH), CONV_WIDTH ** -0.5)
    conv_b = nrm((DEPTH, RG_WIDTH), 0.01)
    rg_wa = nrm((DEPTH, RG_HEADS, RG_HEAD_DIM, RG_HEAD_DIM), RG_HEAD_DIM ** -0.5)
    rg_ba = nrm((DEPTH, RG_WIDTH), 0.01)
    rg_wx = nrm((DEPTH, RG_HEADS, RG_HEAD_DIM, RG_HEAD_DIM), RG_HEAD_DIM ** -0.5)
    rg_bx = nrm((DEPTH, RG_WIDTH), 0.01)
    a_c = jax.random.uniform(next(ks), (DEPTH, RG_WIDTH), f32, minval=0.9, maxval=0.999)
    a_base = a_c ** (1.0 / RG_C)
    rg_lambda = jnp.log(a_base) - jnp.log1p(-a_base)
    n = jnp.arange(S5_STATE, dtype=f32)
    s5_a_re = -0.5 + nrm((DEPTH, S5_GROUPS, S5_STATE), 0.01)
    s5_a_im = jnp.pi * n + nrm((DEPTH, S5_GROUPS, S5_STATE), 0.01)
    s5_b_re = nrm((DEPTH, S5_GROUPS, S5_STATE, S5_GROUP_DIM), (2.0 * S5_GROUP_DIM) ** -0.5)
    s5_b_im = nrm((DEPTH, S5_GROUPS, S5_STATE, S5_GROUP_DIM), (2.0 * S5_GROUP_DIM) ** -0.5)
    s5_c_re = nrm((DEPTH, S5_GROUPS, S5_GROUP_DIM, S5_STATE), (2.0 * S5_STATE) ** -0.5)
    s5_c_im = nrm((DEPTH, S5_GROUPS, S5_GROUP_DIM, S5_STATE), (2.0 * S5_STATE) ** -0.5)
    s5_d = nrm((DEPTH, S5_GROUPS, S5_GROUP_DIM), 1.0)
    s5_log_step = jax.random.uniform(next(ks), (DEPTH, S5_GROUPS), f32,
                                     minval=math.log(1e-3), maxval=math.log(1e-1))
    s5_w_glu = nrm((DEPTH, S5_WIDTH, S5_WIDTH), S5_WIDTH ** -0.5)
    s5_b_glu = nrm((DEPTH, S5_WIDTH), 0.01)
    w_out = nrm((DEPTH, MIX_WIDTH, D_MODEL), MIX_WIDTH ** -0.5 * DEEPNORM_BETA)
    ln1_g = 1.0 + nrm((DEPTH, D_MODEL), 0.01)
    ln1_b = nrm((DEPTH, D_MODEL), 0.01)
    ple_w = nrm((DEPTH, PLE_DIM, D_MODEL), PLE_DIM ** -0.5 * DEEPNORM_BETA)
    ple_gate_w = nrm((DEPTH, D_MODEL, D_MODEL), D_MODEL ** -0.5)
    ple_gate_b = nrm((DEPTH, D_MODEL), 0.01)
    ln2_g = 1.0 + nrm((DEPTH, D_MODEL), 0.01)
    ln2_b = nrm((DEPTH, D_MODEL), 0.01)
    return {"x": x, "p": p, "w_in": w_in, "conv_w": conv_w, "conv_b": conv_b,
            "rg_wa": rg_wa, "rg_ba": rg_ba, "rg_wx": rg_wx, "rg_bx": rg_bx,
            "rg_lambda": rg_lambda, "s5_a_re": s5_a_re, "s5_a_im": s5_a_im,
            "s5_b_re": s5_b_re, "s5_b_im": s5_b_im, "s5_c_re": s5_c_re,
            "s5_c_im": s5_c_im, "s5_d": s5_d, "s5_log_step": s5_log_step,
            "s5_w_glu": s5_w_glu, "s5_b_glu": s5_b_glu, "w_out": w_out,
            "ln1_g": ln1_g, "ln1_b": ln1_b, "ple_w": ple_w, "ple_gate_w": ple_gate_w,
            "ple_gate_b": ple_gate_b, "ln2_g": ln2_g, "ln2_b": ln2_b}


def _fwd_reference(x, p, w_in, conv_w, conv_b, rg_wa, rg_ba, rg_wx, rg_bx, rg_lambda,
              s5_a_re, s5_a_im, s5_b_re, s5_b_im, s5_c_re, s5_c_im, s5_d, s5_log_step,
              s5_w_glu, s5_b_glu, w_out, ln1_g, ln1_b, ple_w, ple_gate_w, ple_gate_b,
              ln2_g, ln2_b):
    for i in range(DEPTH):
        z = x @ w_in[i]
        c0 = RG_WIDTH
        c1 = 2 * RG_WIDTH
        c2 = c1 + S5_WIDTH
        rg_x, rg_gate = z[..., :c0], z[..., c0:c1]
        s5_u, s5_gate = z[..., c1:c2], z[..., c2:]
        rg_h = _causal_depthwise_conv(rg_x, conv_w[i], conv_b[i])
        rg_y = _rg_lru(rg_h, rg_wa[i], rg_ba[i], rg_wx[i], rg_bx[i], rg_lambda[i])
        rg_y = rg_y * jax.nn.silu(rg_gate)
        s5_y = _s5(s5_u, s5_a_re[i], s5_a_im[i], s5_b_re[i], s5_b_im[i], s5_c_re[i],
                   s5_c_im[i], s5_d[i], s5_log_step[i], s5_w_glu[i], s5_b_glu[i])
        s5_y = s5_y * jax.nn.silu(s5_gate)
        mix = jnp.concatenate([rg_y, s5_y], axis=-1) @ w_out[i]
        x = _layer_norm(DEEPNORM_ALPHA * x + mix, ln1_g[i], ln1_b[i])
        e = (p[i] @ ple_w[i]) * jax.nn.sigmoid(x @ ple_gate_w[i] + ple_gate_b[i])
        x = _layer_norm(DEEPNORM_ALPHA * x + e, ln2_g[i], ln2_b[i])
    return x


import jax as _jax
import jax.numpy as _jnp

TWIN_FORMAT = 'train_step'
FWD_PARAMS = ['x', 'p', 'w_in', 'conv_w', 'conv_b', 'rg_wa', 'rg_ba', 'rg_wx', 'rg_bx', 'rg_lambda', 's5_a_re', 's5_a_im', 's5_b_re', 's5_b_im', 's5_c_re', 's5_c_im', 's5_d', 's5_log_step', 's5_w_glu', 's5_b_glu', 'w_out', 'ln1_g', 'ln1_b', 'ple_w', 'ple_gate_w', 'ple_gate_b', 'ln2_g', 'ln2_b']
TWIN_WEIGHTS = ['w_in', 'conv_w', 'conv_b', 'rg_wa', 'rg_ba', 'rg_wx', 'rg_bx', 'rg_lambda', 's5_a_re', 's5_a_im', 's5_b_re', 's5_b_im', 's5_c_re', 's5_c_im', 's5_d', 's5_log_step', 's5_w_glu', 's5_b_glu', 'w_out', 'ln1_g', 'ln1_b', 'ple_w', 'ple_gate_w', 'ple_gate_b', 'ln2_g', 'ln2_b']
TWIN_DIFF_INPUT = 'x'
TWIN_INPUTS = ['x', 'p', 'w_in', 'conv_w', 'conv_b', 'rg_wa', 'rg_ba', 'rg_wx', 'rg_bx', 'rg_lambda', 's5_a_re', 's5_a_im', 's5_b_re', 's5_b_im', 's5_c_re', 's5_c_im', 's5_d', 's5_log_step', 's5_w_glu', 's5_b_glu', 'w_out', 'ln1_g', 'ln1_b', 'ple_w', 'ple_gate_w', 'ple_gate_b', 'ln2_g', 'ln2_b', 'loss_target', 'm_w_in', 'm_conv_w', 'm_conv_b', 'm_rg_wa', 'm_rg_ba', 'm_rg_wx', 'm_rg_bx', 'm_rg_lambda', 'm_s5_a_re', 'm_s5_a_im', 'm_s5_b_re', 'm_s5_b_im', 'm_s5_c_re', 'm_s5_c_im', 'm_s5_d', 'm_s5_log_step', 'm_s5_w_glu', 'm_s5_b_glu', 'm_w_out', 'm_ln1_g', 'm_ln1_b', 'm_ple_w', 'm_ple_gate_w', 'm_ple_gate_b', 'm_ln2_g', 'm_ln2_b', 'v_w_in', 'v_conv_w', 'v_conv_b', 'v_rg_wa', 'v_rg_ba', 'v_rg_wx', 'v_rg_bx', 'v_rg_lambda', 'v_s5_a_re', 'v_s5_a_im', 'v_s5_b_re', 'v_s5_b_im', 'v_s5_c_re', 'v_s5_c_im', 'v_s5_d', 'v_s5_log_step', 'v_s5_w_glu', 'v_s5_b_glu', 'v_w_out', 'v_ln1_g', 'v_ln1_b', 'v_ple_w', 'v_ple_gate_w', 'v_ple_gate_b', 'v_ln2_g', 'v_ln2_b']
TWIN_OUTPUTS = ['loss', 'grad_x', 'grad_w_in', 'grad_conv_w', 'grad_conv_b', 'grad_rg_wa', 'grad_rg_ba', 'grad_rg_wx', 'grad_rg_bx', 'grad_rg_lambda', 'grad_s5_a_re', 'grad_s5_a_im', 'grad_s5_b_re', 'grad_s5_b_im', 'grad_s5_c_re', 'grad_s5_c_im', 'grad_s5_d', 'grad_s5_log_step', 'grad_s5_w_glu', 'grad_s5_b_glu', 'grad_w_out', 'grad_ln1_g', 'grad_ln1_b', 'grad_ple_w', 'grad_ple_gate_w', 'grad_ple_gate_b', 'grad_ln2_g', 'grad_ln2_b', 'delta_w_in', 'delta_conv_w', 'delta_conv_b', 'delta_rg_wa', 'delta_rg_ba', 'delta_rg_wx', 'delta_rg_bx', 'delta_rg_lambda', 'delta_s5_a_re', 'delta_s5_a_im', 'delta_s5_b_re', 'delta_s5_b_im', 'delta_s5_c_re', 'delta_s5_c_im', 'delta_s5_d', 'delta_s5_log_step', 'delta_s5_w_glu', 'delta_s5_b_glu', 'delta_w_out', 'delta_ln1_g', 'delta_ln1_b', 'delta_ple_w', 'delta_ple_gate_w', 'delta_ple_gate_b', 'delta_ln2_g', 'delta_ln2_b', 'new_m_w_in', 'new_m_conv_w', 'new_m_conv_b', 'new_m_rg_wa', 'new_m_rg_ba', 'new_m_rg_wx', 'new_m_rg_bx', 'new_m_rg_lambda', 'new_m_s5_a_re', 'new_m_s5_a_im', 'new_m_s5_b_re', 'new_m_s5_b_im', 'new_m_s5_c_re', 'new_m_s5_c_im', 'new_m_s5_d', 'new_m_s5_log_step', 'new_m_s5_w_glu', 'new_m_s5_b_glu', 'new_m_w_out', 'new_m_ln1_g', 'new_m_ln1_b', 'new_m_ple_w', 'new_m_ple_gate_w', 'new_m_ple_gate_b', 'new_m_ln2_g', 'new_m_ln2_b', 'new_v_w_in', 'new_v_conv_w', 'new_v_conv_b', 'new_v_rg_wa', 'new_v_rg_ba', 'new_v_rg_wx', 'new_v_rg_bx', 'new_v_rg_lambda', 'new_v_s5_a_re', 'new_v_s5_a_im', 'new_v_s5_b_re', 'new_v_s5_b_im', 'new_v_s5_c_re', 'new_v_s5_c_im', 'new_v_s5_d', 'new_v_s5_log_step', 'new_v_s5_w_glu', 'new_v_s5_b_glu', 'new_v_w_out', 'new_v_ln1_g', 'new_v_ln1_b', 'new_v_ple_w', 'new_v_ple_gate_w', 'new_v_ple_gate_b', 'new_v_ln2_g', 'new_v_ln2_b']
TWIN_LEAF_KINDS = {'loss': 'loss', 'grad_x': 'grad_x', 'grad_w_in': 'grad_w', 'grad_conv_w': 'grad_w', 'grad_conv_b': 'grad_w', 'grad_rg_wa': 'grad_w', 'grad_rg_ba': 'grad_w', 'grad_rg_wx': 'grad_w', 'grad_rg_bx': 'grad_w', 'grad_rg_lambda': 'grad_w', 'grad_s5_a_re': 'grad_w', 'grad_s5_a_im': 'grad_w', 'grad_s5_b_re': 'grad_w', 'grad_s5_b_im': 'grad_w', 'grad_s5_c_re': 'grad_w', 'grad_s5_c_im': 'grad_w', 'grad_s5_d': 'grad_w', 'grad_s5_log_step': 'grad_w', 'grad_s5_w_glu': 'grad_w', 'grad_s5_b_glu': 'grad_w', 'grad_w_out': 'grad_w', 'grad_ln1_g': 'grad_w', 'grad_ln1_b': 'grad_w', 'grad_ple_w': 'grad_w', 'grad_ple_gate_w': 'grad_w', 'grad_ple_gate_b': 'grad_w', 'grad_ln2_g': 'grad_w', 'grad_ln2_b': 'grad_w', 'delta_w_in': 'delta_w', 'delta_conv_w': 'delta_w', 'delta_conv_b': 'delta_w', 'delta_rg_wa': 'delta_w', 'delta_rg_ba': 'delta_w', 'delta_rg_wx': 'delta_w', 'delta_rg_bx': 'delta_w', 'delta_rg_lambda': 'delta_w', 'delta_s5_a_re': 'delta_w', 'delta_s5_a_im': 'delta_w', 'delta_s5_b_re': 'delta_w', 'delta_s5_b_im': 'delta_w', 'delta_s5_c_re': 'delta_w', 'delta_s5_c_im': 'delta_w', 'delta_s5_d': 'delta_w', 'delta_s5_log_step': 'delta_w', 'delta_s5_w_glu': 'delta_w', 'delta_s5_b_glu': 'delta_w', 'delta_w_out': 'delta_w', 'delta_ln1_g': 'delta_w', 'delta_ln1_b': 'delta_w', 'delta_ple_w': 'delta_w', 'delta_ple_gate_w': 'delta_w', 'delta_ple_gate_b': 'delta_w', 'delta_ln2_g': 'delta_w', 'delta_ln2_b': 'delta_w', 'new_m_w_in': 'new_m', 'new_m_conv_w': 'new_m', 'new_m_conv_b': 'new_m', 'new_m_rg_wa': 'new_m', 'new_m_rg_ba': 'new_m', 'new_m_rg_wx': 'new_m', 'new_m_rg_bx': 'new_m', 'new_m_rg_lambda': 'new_m', 'new_m_s5_a_re': 'new_m', 'new_m_s5_a_im': 'new_m', 'new_m_s5_b_re': 'new_m', 'new_m_s5_b_im': 'new_m', 'new_m_s5_c_re': 'new_m', 'new_m_s5_c_im': 'new_m', 'new_m_s5_d': 'new_m', 'new_m_s5_log_step': 'new_m', 'new_m_s5_w_glu': 'new_m', 'new_m_s5_b_glu': 'new_m', 'new_m_w_out': 'new_m', 'new_m_ln1_g': 'new_m', 'new_m_ln1_b': 'new_m', 'new_m_ple_w': 'new_m', 'new_m_ple_gate_w': 'new_m', 'new_m_ple_gate_b': 'new_m', 'new_m_ln2_g': 'new_m', 'new_m_ln2_b': 'new_m', 'new_v_w_in': 'new_v', 'new_v_conv_w': 'new_v', 'new_v_conv_b': 'new_v', 'new_v_rg_wa': 'new_v', 'new_v_rg_ba': 'new_v', 'new_v_rg_wx': 'new_v', 'new_v_rg_bx': 'new_v', 'new_v_rg_lambda': 'new_v', 'new_v_s5_a_re': 'new_v', 'new_v_s5_a_im': 'new_v', 'new_v_s5_b_re': 'new_v', 'new_v_s5_b_im': 'new_v', 'new_v_s5_c_re': 'new_v', 'new_v_s5_c_im': 'new_v', 'new_v_s5_d': 'new_v', 'new_v_s5_log_step': 'new_v', 'new_v_s5_w_glu': 'new_v', 'new_v_s5_b_glu': 'new_v', 'new_v_w_out': 'new_v', 'new_v_ln1_g': 'new_v', 'new_v_ln1_b': 'new_v', 'new_v_ple_w': 'new_v', 'new_v_ple_gate_w': 'new_v', 'new_v_ple_gate_b': 'new_v', 'new_v_ln2_g': 'new_v', 'new_v_ln2_b': 'new_v'}


def _forward(args):
    return _fwd_reference(*[args[k] for k in FWD_PARAMS])


def _output_shape():
    def fwd():
        inp = _fwd_setup_inputs(0)
        return _fwd_reference(*[inp[k] for k in FWD_PARAMS])
    out = _jax.eval_shape(fwd)
    return out.shape, out.dtype

N_MICROBATCH = 1
ADAM_LR = 0.001
ADAM_B1 = 0.9
ADAM_B2 = 0.999
ADAM_EPS = 1e-08
ADAM_WD = 0.01
ADAM_STEP = 10
PER_EXAMPLE_BATCH_AXIS = {'x': 0, 'p': 1, 'loss_target': 0}
SHARED_INPUTS = []
_WEIGHT_DTYPES = {'w_in': _jnp.float32, 'conv_w': _jnp.float32, 'conv_b': _jnp.float32, 'rg_wa': _jnp.float32, 'rg_ba': _jnp.float32, 'rg_wx': _jnp.float32, 'rg_bx': _jnp.float32, 'rg_lambda': _jnp.float32, 's5_a_re': _jnp.float32, 's5_a_im': _jnp.float32, 's5_b_re': _jnp.float32, 's5_b_im': _jnp.float32, 's5_c_re': _jnp.float32, 's5_c_im': _jnp.float32, 's5_d': _jnp.float32, 's5_log_step': _jnp.float32, 's5_w_glu': _jnp.float32, 's5_b_glu': _jnp.float32, 'w_out': _jnp.float32, 'ln1_g': _jnp.float32, 'ln1_b': _jnp.float32, 'ple_w': _jnp.float32, 'ple_gate_w': _jnp.float32, 'ple_gate_b': _jnp.float32, 'ln2_g': _jnp.float32, 'ln2_b': _jnp.float32}
MOMENT_SCALE = {'w_in': 2.198306e-02, 'conv_w': 2.718378e-02, 'conv_b': 2.836929e-01, 'rg_wa': 1.075138e-02, 'rg_ba': 7.690632e-03, 'rg_wx': 1.951277e-02, 'rg_bx': 8.892848e-03, 'rg_lambda': 1.377245e-02, 's5_a_re': 6.274005e-04, 's5_a_im': 6.280903e-04, 's5_b_re': 4.267307e-04, 's5_b_im': 4.276595e-04, 's5_c_re': 8.615289e-04, 's5_c_im': 8.704419e-04, 's5_d': 1.399820e-02, 's5_log_step': 3.929786e-01, 's5_w_glu': 3.740190e-03, 's5_b_glu': 5.541708e-03, 'w_out': 4.756275e-02, 'ln1_g': 4.961127e-01, 'ln1_b': 3.447291e-01, 'ple_w': 6.620970e-02, 'ple_gate_w': 1.296337e-02, 'ple_gate_b': 1.389051e-02, 'ln2_g': 2.264254e+01, 'ln2_b': 8.267653e-01}


def _to_microbatches(a, axis):
    t = _jnp.moveaxis(a, axis, 0)
    t = t.reshape((N_MICROBATCH, t.shape[0] // N_MICROBATCH) + t.shape[1:])
    return _jnp.moveaxis(t, 1, axis + 1)


def setup_inputs(seed: int = 0) -> dict:
    inp = _fwd_setup_inputs(seed)
    key = _jax.random.fold_in(_jax.random.key(seed), 7919)
    shape, _ = _output_shape()
    out = dict(inp)
    out["loss_target"] = _jax.random.normal(_jax.random.fold_in(key, 0), shape, _jnp.float32)
    for i, name in enumerate(TWIN_WEIGHTS):
        w = inp[name].astype(_jnp.float32)
        if MOMENT_SCALE is None:
            s = _jnp.sqrt(_jnp.mean(_jnp.square(w)) + 1e-30)
        else:
            s = MOMENT_SCALE[name]
        km, kv = _jax.random.split(_jax.random.fold_in(key, i + 1))
        out[name] = w
        out["m_" + name] = s * _jax.random.normal(km, w.shape, _jnp.float32)
        out["v_" + name] = (s * s) * _jax.random.uniform(kv, w.shape, _jnp.float32, 0.5, 1.5)
    if N_MICROBATCH > 1:
        for name, axis in PER_EXAMPLE_BATCH_AXIS.items():
            out[name] = _to_microbatches(out[name], axis)
    return {'x': out['x'], 'p': out['p'], 'w_in': out['w_in'], 'conv_w': out['conv_w'], 'conv_b': out['conv_b'], 'rg_wa': out['rg_wa'], 'rg_ba': out['rg_ba'], 'rg_wx': out['rg_wx'], 'rg_bx': out['rg_bx'], 'rg_lambda': out['rg_lambda'], 's5_a_re': out['s5_a_re'], 's5_a_im': out['s5_a_im'], 's5_b_re': out['s5_b_re'], 's5_b_im': out['s5_b_im'], 's5_c_re': out['s5_c_re'], 's5_c_im': out['s5_c_im'], 's5_d': out['s5_d'], 's5_log_step': out['s5_log_step'], 's5_w_glu': out['s5_w_glu'], 's5_b_glu': out['s5_b_glu'], 'w_out': out['w_out'], 'ln1_g': out['ln1_g'], 'ln1_b': out['ln1_b'], 'ple_w': out['ple_w'], 'ple_gate_w': out['ple_gate_w'], 'ple_gate_b': out['ple_gate_b'], 'ln2_g': out['ln2_g'], 'ln2_b': out['ln2_b'], 'loss_target': out['loss_target'], 'm_w_in': out['m_w_in'], 'm_conv_w': out['m_conv_w'], 'm_conv_b': out['m_conv_b'], 'm_rg_wa': out['m_rg_wa'], 'm_rg_ba': out['m_rg_ba'], 'm_rg_wx': out['m_rg_wx'], 'm_rg_bx': out['m_rg_bx'], 'm_rg_lambda': out['m_rg_lambda'], 'm_s5_a_re': out['m_s5_a_re'], 'm_s5_a_im': out['m_s5_a_im'], 'm_s5_b_re': out['m_s5_b_re'], 'm_s5_b_im': out['m_s5_b_im'], 'm_s5_c_re': out['m_s5_c_re'], 'm_s5_c_im': out['m_s5_c_im'], 'm_s5_d': out['m_s5_d'], 'm_s5_log_step': out['m_s5_log_step'], 'm_s5_w_glu': out['m_s5_w_glu'], 'm_s5_b_glu': out['m_s5_b_glu'], 'm_w_out': out['m_w_out'], 'm_ln1_g': out['m_ln1_g'], 'm_ln1_b': out['m_ln1_b'], 'm_ple_w': out['m_ple_w'], 'm_ple_gate_w': out['m_ple_gate_w'], 'm_ple_gate_b': out['m_ple_gate_b'], 'm_ln2_g': out['m_ln2_g'], 'm_ln2_b': out['m_ln2_b'], 'v_w_in': out['v_w_in'], 'v_conv_w': out['v_conv_w'], 'v_conv_b': out['v_conv_b'], 'v_rg_wa': out['v_rg_wa'], 'v_rg_ba': out['v_rg_ba'], 'v_rg_wx': out['v_rg_wx'], 'v_rg_bx': out['v_rg_bx'], 'v_rg_lambda': out['v_rg_lambda'], 'v_s5_a_re': out['v_s5_a_re'], 'v_s5_a_im': out['v_s5_a_im'], 'v_s5_b_re': out['v_s5_b_re'], 'v_s5_b_im': out['v_s5_b_im'], 'v_s5_c_re': out['v_s5_c_re'], 'v_s5_c_im': out['v_s5_c_im'], 'v_s5_d': out['v_s5_d'], 'v_s5_log_step': out['v_s5_log_step'], 'v_s5_w_glu': out['v_s5_w_glu'], 'v_s5_b_glu': out['v_s5_b_glu'], 'v_w_out': out['v_w_out'], 'v_ln1_g': out['v_ln1_g'], 'v_ln1_b': out['v_ln1_b'], 'v_ple_w': out['v_ple_w'], 'v_ple_gate_w': out['v_ple_gate_w'], 'v_ple_gate_b': out['v_ple_gate_b'], 'v_ln2_g': out['v_ln2_g'], 'v_ln2_b': out['v_ln2_b']}


def _loss(weights, diff, rest, loss_target):
    with _jax.named_scope("forward"):
        args = {**rest, TWIN_DIFF_INPUT: diff, **{k: w.astype(_WEIGHT_DTYPES[k]) for k, w in weights.items()}}
        y = _forward(args)
    with _jax.named_scope("loss_head"):
        err = _jnp.square(y.astype(_jnp.float32) - loss_target)
        return 0.5 * _jnp.sum(_jnp.mean(err, axis=-1)) if err.ndim else 0.5 * err


def _adamw(w, g, m, v):
    m = ADAM_B1 * m + (1.0 - ADAM_B1) * g
    v = ADAM_B2 * v + (1.0 - ADAM_B2) * _jnp.square(g)
    m_hat = m / (1.0 - ADAM_B1 ** ADAM_STEP)
    v_hat = v / (1.0 - ADAM_B2 ** ADAM_STEP)
    delta = -ADAM_LR * (m_hat / (_jnp.sqrt(v_hat) + ADAM_EPS) + ADAM_WD * w)
    return delta, m, v


def reference(x, p, w_in, conv_w, conv_b, rg_wa, rg_ba, rg_wx, rg_bx, rg_lambda, s5_a_re, s5_a_im, s5_b_re, s5_b_im, s5_c_re, s5_c_im, s5_d, s5_log_step, s5_w_glu, s5_b_glu, w_out, ln1_g, ln1_b, ple_w, ple_gate_w, ple_gate_b, ln2_g, ln2_b, loss_target, m_w_in, m_conv_w, m_conv_b, m_rg_wa, m_rg_ba, m_rg_wx, m_rg_bx, m_rg_lambda, m_s5_a_re, m_s5_a_im, m_s5_b_re, m_s5_b_im, m_s5_c_re, m_s5_c_im, m_s5_d, m_s5_log_step, m_s5_w_glu, m_s5_b_glu, m_w_out, m_ln1_g, m_ln1_b, m_ple_w, m_ple_gate_w, m_ple_gate_b, m_ln2_g, m_ln2_b, v_w_in, v_conv_w, v_conv_b, v_rg_wa, v_rg_ba, v_rg_wx, v_rg_bx, v_rg_lambda, v_s5_a_re, v_s5_a_im, v_s5_b_re, v_s5_b_im, v_s5_c_re, v_s5_c_im, v_s5_d, v_s5_log_step, v_s5_w_glu, v_s5_b_glu, v_w_out, v_ln1_g, v_ln1_b, v_ple_w, v_ple_gate_w, v_ple_gate_b, v_ln2_g, v_ln2_b):
    given = dict(x=x, p=p, w_in=w_in, conv_w=conv_w, conv_b=conv_b, rg_wa=rg_wa, rg_ba=rg_ba, rg_wx=rg_wx, rg_bx=rg_bx, rg_lambda=rg_lambda, s5_a_re=s5_a_re, s5_a_im=s5_a_im, s5_b_re=s5_b_re, s5_b_im=s5_b_im, s5_c_re=s5_c_re, s5_c_im=s5_c_im, s5_d=s5_d, s5_log_step=s5_log_step, s5_w_glu=s5_w_glu, s5_b_glu=s5_b_glu, w_out=w_out, ln1_g=ln1_g, ln1_b=ln1_b, ple_w=ple_w, ple_gate_w=ple_gate_w, ple_gate_b=ple_gate_b, ln2_g=ln2_g, ln2_b=ln2_b, loss_target=loss_target, m_w_in=m_w_in, m_conv_w=m_conv_w, m_conv_b=m_conv_b, m_rg_wa=m_rg_wa, m_rg_ba=m_rg_ba, m_rg_wx=m_rg_wx, m_rg_bx=m_rg_bx, m_rg_lambda=m_rg_lambda, m_s5_a_re=m_s5_a_re, m_s5_a_im=m_s5_a_im, m_s5_b_re=m_s5_b_re, m_s5_b_im=m_s5_b_im, m_s5_c_re=m_s5_c_re, m_s5_c_im=m_s5_c_im, m_s5_d=m_s5_d, m_s5_log_step=m_s5_log_step, m_s5_w_glu=m_s5_w_glu, m_s5_b_glu=m_s5_b_glu, m_w_out=m_w_out, m_ln1_g=m_ln1_g, m_ln1_b=m_ln1_b, m_ple_w=m_ple_w, m_ple_gate_w=m_ple_gate_w, m_ple_gate_b=m_ple_gate_b, m_ln2_g=m_ln2_g, m_ln2_b=m_ln2_b, v_w_in=v_w_in, v_conv_w=v_conv_w, v_conv_b=v_conv_b, v_rg_wa=v_rg_wa, v_rg_ba=v_rg_ba, v_rg_wx=v_rg_wx, v_rg_bx=v_rg_bx, v_rg_lambda=v_rg_lambda, v_s5_a_re=v_s5_a_re, v_s5_a_im=v_s5_a_im, v_s5_b_re=v_s5_b_re, v_s5_b_im=v_s5_b_im, v_s5_c_re=v_s5_c_re, v_s5_c_im=v_s5_c_im, v_s5_d=v_s5_d, v_s5_log_step=v_s5_log_step, v_s5_w_glu=v_s5_w_glu, v_s5_b_glu=v_s5_b_glu, v_w_out=v_w_out, v_ln1_g=v_ln1_g, v_ln1_b=v_ln1_b, v_ple_w=v_ple_w, v_ple_gate_w=v_ple_gate_w, v_ple_gate_b=v_ple_gate_b, v_ln2_g=v_ln2_g, v_ln2_b=v_ln2_b)
    weights = {n: given[n] for n in TWIN_WEIGHTS}
    shared = {n: given[n] for n in SHARED_INPUTS}
    per_example = {n: given[n] for n in ['x', 'p']}
    grad_fn = _jax.value_and_grad(_loss, argnums=(0, 1))

    def one_microbatch(ex, loss_target):
        ex = dict(ex)
        diff = ex.pop(TWIN_DIFF_INPUT)
        return grad_fn(weights, diff, {**shared, **ex}, loss_target)

    if N_MICROBATCH == 1:
        loss, (grad_w, grad_x) = one_microbatch(per_example, given["loss_target"])
    else:
        def body(carry, xs):
            loss_sum, grad_sum = carry
            l_k, (gw_k, gx_k) = one_microbatch(xs[0], xs[1])
            with _jax.named_scope("update"):
                return (loss_sum + l_k, _jax.tree.map(_jnp.add, grad_sum, gw_k)), gx_k

        init = (_jnp.zeros((), _jnp.float32), _jax.tree.map(_jnp.zeros_like, weights))
        (loss, grad_w), grad_x = _jax.lax.scan(body, init, (per_example, given["loss_target"]))
    with _jax.named_scope("update"):
        delta_w, new_m, new_v = {}, {}, {}
        for n in TWIN_WEIGHTS:
            delta_w[n], new_m[n], new_v[n] = _adamw(weights[n], grad_w[n], given["m_" + n], given["v_" + n])
    return (loss, grad_x, *[grad_w[n] for n in TWIN_WEIGHTS], *[delta_w[n] for n in TWIN_WEIGHTS],
            *[new_m[n] for n in TWIN_WEIGHTS], *[new_v[n] for n in TWIN_WEIGHTS])
```

```python
import functools
import math

import jax
import jax.numpy as jnp
from jax import lax
from jax.experimental import pallas as pl
from jax.experimental.pallas import tpu as pltpu

F32 = jnp.float32
MXU = jnp.bfloat16
WIRE = jnp.bfloat16

N_DEV = 8
D_MODEL = 1024
RG_W = 640
S5_W = 384
S5_P = 64
S5_N = 24 * S5_P
Z_W = 2 * RG_W + 2 * S5_W
C_RGG = RG_W
C_S5U = 2 * RG_W
C_S5G = 2 * RG_W + S5_W
LANE = 128
N_RG_T = RG_W // LANE
N_S5_T = S5_W // LANE
N_S5_J = 4
ALPHA = (2.0 * 2) ** 0.25
LN_EPS = 1e-5
RG_C = 8.0
LR, B1, B2, EPS, WD, STEP = 0.001, 0.9, 0.999, 1e-08, 0.01, 10
BC1 = 1.0 - B1 ** STEP
BC2 = 1.0 - B2 ** STEP
RC = 256
TM = 256
VMEM_LIMIT = 56 * 1024 * 1024

MESH = pl.DeviceIdType.MESH
ANY = pl.BlockSpec(memory_space=pl.ANY)


def _params(n_grid_axes, vmem=VMEM_LIMIT):
    return pltpu.CompilerParams(dimension_semantics=("arbitrary",) * n_grid_axes, vmem_limit_bytes=vmem)


def _S(shape, dtype=F32):
    return jax.ShapeDtypeStruct(tuple(shape), dtype)


def _sigmoid(x):
    return 1.0 / (1.0 + jnp.exp(-x))


def _silu_and_grad(x):
    s = _sigmoid(x)
    return x * s, s * (1.0 + x * (1.0 - s))


_GELU_C = math.sqrt(2.0 / math.pi)


def _gelu(x):
    return 0.5 * x * (1.0 + jnp.tanh(_GELU_C * (x + 0.044715 * (x * x * x))))


def _gelu_grad(x):
    th = jnp.tanh(_GELU_C * (x + 0.044715 * (x * x * x)))
    return 0.5 * (1.0 + th) + 0.5 * x * (1.0 - th * th) * (_GELU_C * (1.0 + 3.0 * 0.044715 * (x * x)))


def _mm(a, b):
    return jnp.dot(a.astype(MXU), b.astype(MXU), preferred_element_type=F32)


def _mm_nt(a, b):
    return lax.dot_general(a.astype(MXU), b.astype(MXU), (((1,), (1,)), ((), ())), preferred_element_type=F32)


def _mm_tn(a, b):
    return lax.dot_general(a.astype(MXU), b.astype(MXU), (((0,), (0,)), ((), ())), preferred_element_type=F32)


def _ln_fwd(t, g, b):
    mu = jnp.mean(t, axis=-1, keepdims=True)
    tc = t - mu
    var = jnp.mean(tc * tc, axis=-1, keepdims=True)
    rstd = lax.rsqrt(var + LN_EPS)
    xhat = tc * rstd
    return xhat * g + b, xhat, rstd


def _ln_bwd(dy, xhat, rstd, g):
    dxh = dy * g
    m1 = jnp.mean(dxh, axis=-1, keepdims=True)
    m2 = jnp.mean(dxh * xhat, axis=-1, keepdims=True)
    return rstd * (dxh - m1 - xhat * m2)


def _colsum(a):
    return jnp.sum(a, axis=0, keepdims=True)


def _down(x, d, rows, fill):
    return jnp.where(rows >= d, pltpu.roll(x, d, 0), fill)


def _up(x, d, rows, fill):
    n = x.shape[0]
    return jnp.where(rows < n - d, pltpu.roll(x, n - d, 0), fill)


def _scan_steps(n):
    d = 1
    while d < n:
        yield d
        d *= 2


def _halo(ref, c, r0):
    rp = pl.multiple_of(jnp.maximum(r0 - 8, 0), 8)
    return jnp.where(c > 0, ref[pl.ds(rp, 8), :], 0.0)


def _conv_taps(xe):
    return [pltpu.roll(xe, 3, 0)[8:, :], pltpu.roll(xe, 2, 0)[8:, :], pltpu.roll(xe, 1, 0)[8:, :], xe[8:, :]]


def _rg_gates(h, wa, wx, ba, bx, sp):
    r = _sigmoid(_mm(h, wa) + ba)
    i = _sigmoid(_mm(h, wx) + bx)
    log_a = (-RG_C) * r * sp
    a = jnp.exp(log_a)
    mult = jnp.sqrt(-jnp.tanh(log_a) * (a * a + 1.0))
    return r, i, a, mult


def _softplus(y):
    return jnp.maximum(y, 0.0) + jnp.log1p(jnp.exp(-jnp.abs(y)))


def _inproj_fwd(x, w_in):
    L = x.shape[0]

    def body(x_ref, w_ref, z_ref):
        z_ref[...] = _mm(x_ref[...], w_ref[...])

    return pl.pallas_call(
        body, name="inproj_fwd", grid=(L // TM,),
        in_specs=[pl.BlockSpec((TM, D_MODEL), lambda i: (i, 0)), pl.BlockSpec((D_MODEL, Z_W), lambda i: (0, 0))],
        out_specs=pl.BlockSpec((TM, Z_W), lambda i: (i, 0)),
        out_shape=_S((L, Z_W)), compiler_params=_params(1))(x, w_in)


def _inproj_bwd(dt1, x, dzx, dzg, dzu, w_in):
    L = x.shape[0]

    def body(dt1_ref, x_ref, dzx_ref, dzg_ref, dzu_ref, w_ref, dx_ref, dw_ref):
        @pl.when(pl.program_id(0) == 0)
        def _():
            dw_ref[...] = jnp.zeros_like(dw_ref)
        dzg = dzg_ref[...]
        dz = jnp.concatenate([dzx_ref[...], dzg[:, :RG_W], dzu_ref[...], dzg[:, RG_W:]], axis=1).astype(MXU)
        dx_ref[...] = ALPHA * dt1_ref[...] + _mm_nt(dz, w_ref[...])
        dw_ref[...] += _mm_tn(x_ref[...], dz)

    row = lambda w: pl.BlockSpec((TM, w), lambda i: (i, 0))
    return pl.pallas_call(
        body, name="inproj_bwd", grid=(L // TM,),
        in_specs=[row(D_MODEL), row(D_MODEL), row(RG_W), row(D_MODEL), row(S5_W),
                  pl.BlockSpec((D_MODEL, Z_W), lambda i: (0, 0))],
        out_specs=[row(D_MODEL), pl.BlockSpec((D_MODEL, Z_W), lambda i: (0, 0))],
        out_shape=[_S((L, D_MODEL)), _S((D_MODEL, Z_W))], compiler_params=_params(1))(dt1, x, dzx, dzg, dzu, w_in)


def _rg_specs(L):
    tile = lambda rows: pl.BlockSpec((rows, LANE), lambda c: (0, c))
    return tile, pl.BlockSpec((None, LANE, LANE), lambda c: (c, 0, 0))


def _rg_fwd(z, cw, cb, wa_bd, wx_bd, ba, bx, lam):
    L = z.shape[0]

    def body(x_ref, cw_ref, cb_ref, wa_ref, wx_ref, ba_ref, bx_ref, lam_ref, hs_ref):
        w, b = cw_ref[...], cb_ref[...]
        wa, wx, ba_, bx_ = wa_ref[...].astype(MXU), wx_ref[...].astype(MXU), ba_ref[...], bx_ref[...]
        sp = _softplus(-lam_ref[...])
        rows = lax.broadcasted_iota(jnp.int32, (RC, LANE), 0)

        def step(c, carry):
            r0 = pl.multiple_of(c * RC, RC)
            xe = jnp.concatenate([_halo(x_ref, c, r0), x_ref[pl.ds(r0, RC), :]], axis=0)
            t = _conv_taps(xe)
            h = t[0] * w[0:1] + t[1] * w[1:2] + t[2] * w[2:3] + t[3] * w[3:4] + b
            _, i, a, mult = _rg_gates(h, wa, wx, ba_, bx_, sp)
            A, U = a, mult * (i * h)
            for d in _scan_steps(RC):
                U = A * _down(U, d, rows, 0.0) + U
                A = A * _down(A, d, rows, 1.0)
            hs = A * carry + U
            hs_ref[pl.ds(r0, RC), :] = hs
            return hs[RC - 1:RC, :]

        lax.fori_loop(0, L // RC, step, jnp.zeros((1, LANE), F32))

    tile, bd = _rg_specs(L)
    return pl.pallas_call(
        body, name="rg_fwd", grid=(N_RG_T,),
        in_specs=[tile(L), tile(4), tile(1), bd, bd, tile(1), tile(1), tile(1)],
        out_specs=tile(L), out_shape=_S((L, RG_W)), compiler_params=_params(1))(z, cw, cb, wa_bd, wx_bd, ba, bx, lam)


def _rg_bwd(dhs, z, hs, cw, cb, wa_bd, wx_bd, ba, bx, lam):
    L = z.shape[0]

    def body(g_ref, x_ref, hs_ref, cw_ref, cb_ref, wa_ref, wx_ref, ba_ref, bx_ref, lam_ref,
             dx_ref, dcw_ref, dcb_ref, dwa_ref, dwx_ref, dba_ref, dbx_ref, dlam_ref):
        w, b = cw_ref[...], cb_ref[...]
        wa, wx, ba_, bx_ = wa_ref[...].astype(MXU), wx_ref[...].astype(MXU), ba_ref[...], bx_ref[...]
        lam = lam_ref[...]
        sp = _softplus(-lam)
        rows = lax.broadcasted_iota(jnp.int32, (RC, LANE), 0)
        for ref in (dcw_ref, dcb_ref, dwa_ref, dwx_ref, dba_ref, dbx_ref, dlam_ref):
            ref[...] = jnp.zeros_like(ref)
        nch = L // RC

        def step(k, carry):
            cin, nxt = carry
            c = nch - 1 - k
            r0 = pl.multiple_of(c * RC, RC)
            xe = jnp.concatenate([_halo(x_ref, c, r0), x_ref[pl.ds(r0, RC), :]], axis=0)
            t = _conv_taps(xe)
            h = t[0] * w[0:1] + t[1] * w[1:2] + t[2] * w[2:3] + t[3] * w[3:4] + b
            r, i, a, mult = _rg_gates(h, wa, wx, ba_, bx_, sp)
            hs_e = jnp.concatenate([_halo(hs_ref, c, r0), hs_ref[pl.ds(r0, RC), :]], axis=0)
            hs_prev = pltpu.roll(hs_e, 1, 0)[8:, :]
            U = g_ref[pl.ds(r0, RC), :] + jnp.where(rows == RC - 1, cin, 0.0)
            A = _up(a, 1, rows, 0.0)
            for d in _scan_steps(RC):
                U = A * _up(U, d, rows, 0.0) + U
                A = A * _up(A, d, rows, 1.0)
            dh = U
            cin_new = a[0:1, :] * dh[0:1, :]
            ih = i * h
            dlog_a = dh * hs_prev * a - (dh * ih) * (a * a) / mult
            di = dh * mult * h
            dhin = dh * mult * i
            dr = dlog_a * ((-RG_C) * sp)
            dlam_ref[...] += _colsum(dlog_a * r)
            dra = dr * r * (1.0 - r)
            dia = di * i * (1.0 - i)
            dwa_ref[...] += _mm_tn(h, dra)
            dwx_ref[...] += _mm_tn(h, dia)
            dba_ref[...] += _colsum(dra)
            dbx_ref[...] += _colsum(dia)
            dhin = dhin + _mm_nt(dra, wa) + _mm_nt(dia, wx)
            de = jnp.concatenate([dhin, nxt], axis=0)
            n = RC + 8
            dx = (dhin * w[3:4] + pltpu.roll(de, n - 1, 0)[:RC, :] * w[2:3]
                  + pltpu.roll(de, n - 2, 0)[:RC, :] * w[1:2] + pltpu.roll(de, n - 3, 0)[:RC, :] * w[0:1])
            dx_ref[pl.ds(r0, RC), :] = dx
            for kk in range(4):
                dcw_ref[kk:kk + 1, :] += _colsum(dhin * t[kk])
            dcb_ref[...] += _colsum(dhin)
            return cin_new, dhin[0:8, :]

        lax.fori_loop(0, nch, step, (jnp.zeros((1, LANE), F32), jnp.zeros((8, LANE), F32)))
        dlam_ref[...] = dlam_ref[...] * (RG_C * _sigmoid(-lam))

    tile, bd = _rg_specs(L)
    return pl.pallas_call(
        body, name="rg_bwd", grid=(N_RG_T,),
        in_specs=[tile(L), tile(L), tile(L), tile(4), tile(1), bd, bd, tile(1), tile(1), tile(1)],
        out_specs=[tile(L), tile(4), tile(1), bd, bd, tile(1), tile(1), tile(1)],
        out_shape=[_S((L, RG_W)), _S((4, RG_W)), _S((1, RG_W)), _S((N_RG_T, LANE, LANE)), _S((N_RG_T, LANE, LANE)),
                   _S((1, RG_W)), _S((1, RG_W)), _S((1, RG_W))],
        compiler_params=_params(1))(dhs, z, hs, cw, cb, wa_bd, wx_bd, ba, bx, lam)


def _cmul(ar, ai, br, bi):
    return ar * br - ai * bi, ar * bi + ai * br


def _s5_powers(lr, li):
    out = []
    for _ in _scan_steps(RC):
        out.append((lr, li))
        lr, li = _cmul(lr, li, lr, li)
    return out


def _s5_specs(L):
    in_tile = pl.BlockSpec((L, LANE), lambda t, j: (0, t))
    st = pl.BlockSpec((L, LANE), lambda t, j: (0, t * N_S5_J + j))
    bb = pl.BlockSpec((None, LANE, LANE), lambda t, j: (t, 0, j))
    cc = pl.BlockSpec((None, LANE, LANE), lambda t, j: (t, j, 0))
    lb = pl.BlockSpec((1, LANE), lambda t, j: (0, t * N_S5_J + j))
    dv = pl.BlockSpec((1, LANE), lambda t, j: (0, t))
    return in_tile, st, bb, cc, lb, dv


def _s5_fwd(z, bb_re, bb_im, lb_re, lb_im, c_re, c_im, dvec):
    L = z.shape[0]

    def body(u_ref, bbr_ref, bbi_ref, lr_ref, li_ref, cr_ref, ci_ref, d_ref, y_ref, sr_ref, si_ref):
        j = pl.program_id(1)
        bbr, bbi = bbr_ref[...].astype(MXU), bbi_ref[...].astype(MXU)
        cr, ci = cr_ref[...].astype(MXU), ci_ref[...].astype(MXU)
        lr, li = lr_ref[...], li_ref[...]
        dv = d_ref[...]
        rows = lax.broadcasted_iota(jnp.int32, (RC, LANE), 0)
        pw = _s5_powers(lr, li)
        er, ei = jnp.broadcast_to(lr, (RC, LANE)), jnp.broadcast_to(li, (RC, LANE))
        for d in _scan_steps(RC):
            er, ei = _cmul(er, ei, _down(er, d, rows, 1.0), _down(ei, d, rows, 0.0))

        def step(c, carry):
            kr, ki = carry
            r0 = pl.multiple_of(c * RC, RC)
            u = u_ref[pl.ds(r0, RC), :]
            ub = u.astype(MXU)
            sr = jnp.dot(ub, bbr, preferred_element_type=F32)
            si = jnp.dot(ub, bbi, preferred_element_type=F32)
            for d, (pr, pi) in zip(_scan_steps(RC), pw):
                shr, shi = _down(sr, d, rows, 0.0), _down(si, d, rows, 0.0)
                sr, si = sr + (pr * shr - pi * shi), si + (pr * shi + pi * shr)
            sr = sr + (er * kr - ei * ki)
            si = si + (er * ki + ei * kr)
            sr_ref[pl.ds(r0, RC), :] = sr
            si_ref[pl.ds(r0, RC), :] = si
            y = _mm(sr, cr) - _mm(si, ci)

            @pl.when(j == 0)
            def _():
                y_ref[pl.ds(r0, RC), :] = y + dv * u

            @pl.when(j > 0)
            def _():
                y_ref[pl.ds(r0, RC), :] += y

            return sr[RC - 1:RC, :], si[RC - 1:RC, :]

        zero = jnp.zeros((1, LANE), F32)
        lax.fori_loop(0, L // RC, step, (zero, zero))

    in_tile, st, bb, cc, lb, dv = _s5_specs(L)
    u_tile = pl.BlockSpec((L, LANE), lambda t, j: (0, C_S5U // LANE + t))
    return pl.pallas_call(
        body, name="s5_fwd", grid=(N_S5_T, N_S5_J),
        in_specs=[u_tile, bb, bb, lb, lb, cc, cc, dv],
        out_specs=[in_tile, st, st],
        out_shape=[_S((L, S5_W)), _S((L, S5_N)), _S((L, S5_N))],
        compiler_params=_params(2))(z, bb_re, bb_im, lb_re, lb_im, c_re, c_im, dvec)


def _s5_bwd(dy0, z, s_re, s_im, bb_re, bb_im, lb_re, lb_im, c_re, c_im, dvec):
    L = z.shape[0]

    def body(dy_ref, u_ref, sr_ref, si_ref, bbr_ref, bbi_ref, lr_ref, li_ref, cr_ref, ci_ref, d_ref,
             du_ref, dbbr_ref, dbbi_ref, dlr_ref, dli_ref, dcr_ref, dci_ref, dd_ref):
        j = pl.program_id(1)
        bbr, bbi = bbr_ref[...].astype(MXU), bbi_ref[...].astype(MXU)
        cr, ci = cr_ref[...].astype(MXU), ci_ref[...].astype(MXU)
        lr, li = lr_ref[...], -li_ref[...]
        dv = d_ref[...]
        rows = lax.broadcasted_iota(jnp.int32, (RC, LANE), 0)
        pw = _s5_powers(lr, li)
        er, ei = jnp.broadcast_to(lr, (RC, LANE)), jnp.broadcast_to(li, (RC, LANE))
        for d in _scan_steps(RC):
            er, ei = _cmul(er, ei, _up(er, d, rows, 1.0), _up(ei, d, rows, 0.0))
        for ref in (dbbr_ref, dbbi_ref, dlr_ref, dli_ref, dcr_ref, dci_ref):
            ref[...] = jnp.zeros_like(ref)

        @pl.when(j == 0)
        def _():
            dd_ref[...] = jnp.zeros_like(dd_ref)

        nch = L // RC

        def step(k, carry):
            kr, ki = carry
            c = nch - 1 - k
            r0 = pl.multiple_of(c * RC, RC)
            dy = dy_ref[pl.ds(r0, RC), :]
            u = u_ref[pl.ds(r0, RC), :]
            dyb, ub = dy.astype(MXU), u.astype(MXU)
            sr, si = sr_ref[pl.ds(r0, RC), :], si_ref[pl.ds(r0, RC), :]
            dcr_ref[...] += _mm_tn(sr, dyb)
            dci_ref[...] -= _mm_tn(si, dyb)
            gr = _mm_nt(dyb, cr)
            gi = -_mm_nt(dyb, ci)
            for d, (pr, pi) in zip(_scan_steps(RC), pw):
                shr, shi = _up(gr, d, rows, 0.0), _up(gi, d, rows, 0.0)
                gr, gi = gr + (pr * shr - pi * shi), gi + (pr * shi + pi * shr)
            gr = gr + (er * kr - ei * ki)
            gi = gi + (er * ki + ei * kr)
            pr_ = pltpu.roll(jnp.concatenate([_halo(sr_ref, c, r0), sr], axis=0), 1, 0)[8:, :]
            pi_ = pltpu.roll(jnp.concatenate([_halo(si_ref, c, r0), si], axis=0), 1, 0)[8:, :]
            dlr_ref[...] += _colsum(pr_ * gr + pi_ * gi)
            dli_ref[...] += _colsum(pr_ * gi - pi_ * gr)
            grb, gib = gr.astype(MXU), gi.astype(MXU)
            dbbr_ref[...] += _mm_tn(ub, grb)
            dbbi_ref[...] += _mm_tn(ub, gib)
            du = _mm_nt(grb, bbr) + _mm_nt(gib, bbi)

            @pl.when(j == 0)
            def _():
                du_ref[pl.ds(r0, RC), :] = du + dv * dy
                dd_ref[...] += _colsum(dy * u)

            @pl.when(j > 0)
            def _():
                du_ref[pl.ds(r0, RC), :] += du

            return gr[0:1, :], gi[0:1, :]

        zero = jnp.zeros((1, LANE), F32)
        lax.fori_loop(0, nch, step, (zero, zero))

    in_tile, st, bb, cc, lb, dv = _s5_specs(L)
    u_tile = pl.BlockSpec((L, LANE), lambda t, j: (0, C_S5U // LANE + t))
    return pl.pallas_call(
        body, name="s5_bwd", grid=(N_S5_T, N_S5_J),
        in_specs=[in_tile, u_tile, st, st, bb, bb, lb, lb, cc, cc, dv],
        out_specs=[in_tile, bb, bb, lb, lb, cc, cc, dv],
        out_shape=[_S((L, S5_W)), _S((N_S5_T, LANE, 4 * LANE)), _S((N_S5_T, LANE, 4 * LANE)), _S((1, S5_N)), _S((1, S5_N)),
                   _S((N_S5_T, 4 * LANE, LANE)), _S((N_S5_T, 4 * LANE, LANE)), _S((1, S5_W))],
        compiler_params=_params(2))(dy0, z, s_re, s_im, bb_re, bb_im, lb_re, lb_im, c_re, c_im, dvec)


def _disc(ar, ai, ls):
    dt = jnp.exp(ls)
    mag = jnp.exp(ar * dt)
    lr = mag * jnp.cos(ai * dt)
    li = mag * jnp.sin(ai * dt)
    den = ar * ar + ai * ai
    cr = ((lr - 1.0) * ar + li * ai) / den
    ci = (li * ar - (lr - 1.0) * ai) / den
    return lr, li, cr, ci


def _s5_disc_fwd(ar, ai, ls):
    def body(ar_ref, ai_ref, ls_ref, lr_ref, li_ref, cr_ref, ci_ref):
        lr, li, cr, ci = _disc(ar_ref[...], ai_ref[...], ls_ref[...])
        lr_ref[...], li_ref[...], cr_ref[...], ci_ref[...] = lr, li, cr, ci

    sh = _S(ar.shape)
    return pl.pallas_call(body, name="s5_disc_fwd", out_shape=[sh, sh, sh, sh])(ar, ai, ls)


def _s5_disc_bwd(ar, ai, ls, dlr, dli, dcr, dci):
    def body(ar_ref, ai_ref, ls_ref, dlr_ref, dli_ref, dcr_ref, dci_ref, dar_ref, dai_ref, dls_ref):
        _, vjp = jax.vjp(_disc, ar_ref[...], ai_ref[...], jnp.broadcast_to(ls_ref[...], ar_ref.shape))
        dar, dai, dls = vjp((dlr_ref[...], dli_ref[...], dcr_ref[...], dci_ref[...]))
        dar_ref[...], dai_ref[...] = dar, dai
        dls_ref[...] = jnp.sum(dls, axis=1, keepdims=True)

    return pl.pallas_call(body, name="s5_disc_bwd", out_shape=[_S(ar.shape), _S(ar.shape), _S(ls.shape)])(
        ar, ai, ls, dlr, dli, dcr, dci)


def _s5_bscale_fwd(cr, ci, br, bi):
    def body(cr_ref, ci_ref, br_ref, bi_ref, or_ref, oi_ref):
        or_ref[...], oi_ref[...] = _cmul(cr_ref[...], ci_ref[...], br_ref[...], bi_ref[...])

    return pl.pallas_call(body, name="s5_bscale_fwd", out_shape=[_S(br.shape), _S(br.shape)])(cr, ci, br, bi)


def _s5_bscale_bwd(cr, ci, br, bi, gr, gi):
    def body(cr_ref, ci_ref, br_ref, bi_ref, gr_ref, gi_ref, dbr_ref, dbi_ref, dcr_ref, dci_ref):
        cr_, ci_, br_, bi_, gr_, gi_ = (r[...] for r in (cr_ref, ci_ref, br_ref, bi_ref, gr_ref, gi_ref))
        dbr_ref[...] = cr_ * gr_ + ci_ * gi_
        dbi_ref[...] = cr_ * gi_ - ci_ * gr_
        dcr_ref[...] = jnp.sum(gr_ * br_ + gi_ * bi_, axis=1, keepdims=True)
        dci_ref[...] = jnp.sum(gi_ * br_ - gr_ * bi_, axis=1, keepdims=True)

    return pl.pallas_call(body, name="s5_bscale_bwd",
                          out_shape=[_S(br.shape), _S(br.shape), _S(cr.shape), _S(cr.shape)])(cr, ci, br, bi, gr, gi)


def _row(w):
    return pl.BlockSpec((TM, w), lambda i: (i, 0))


def _full(shape):
    return pl.BlockSpec(tuple(shape), lambda i: (0,) * len(shape))


def _post_fwd(x, hs, z, y0, p, w_glu, b_glu, w_out, g1, b1, ple_w, w_pg, b_pg, g2, b2):
    L = x.shape[0]

    def body(x_ref, hs_ref, z_ref, y0_ref, p_ref, wg_ref, bg_ref, wo_ref, g1_ref, b1_ref, pw_ref, wpg_ref, bpg_ref,
             g2_ref, b2_ref, x2_ref, t1_ref, t2_ref, m_ref):
        rg_gate = z_ref[:, C_RGG:C_RGG + RG_W]
        s5_gate = z_ref[:, C_S5G:C_S5G + S5_W]
        rg_y = hs_ref[...] * _silu_and_grad(rg_gate)[0]
        y1 = _gelu(y0_ref[...])
        gl = _sigmoid(_mm(y1, wg_ref[...]) + bg_ref[...])
        s5_y = (y1 * gl) * _silu_and_grad(s5_gate)[0]
        m_ref[:, :RG_W] = rg_y
        m_ref[:, RG_W:] = s5_y
        mix = _mm(m_ref[...], wo_ref[...])
        t1 = ALPHA * x_ref[...] + mix
        x1, _, _ = _ln_fwd(t1, g1_ref[...], b1_ref[...])
        e = _mm(p_ref[...], pw_ref[...]) * _sigmoid(_mm(x1, wpg_ref[...]) + bpg_ref[...])
        t2 = ALPHA * x1 + e
        x2, _, _ = _ln_fwd(t2, g2_ref[...], b2_ref[...])
        t1_ref[...], t2_ref[...], x2_ref[...] = t1, t2, x2

    vec = _full((1, D_MODEL))
    return pl.pallas_call(
        body, name="post_fwd", grid=(L // TM,),
        in_specs=[_row(D_MODEL), _row(RG_W), _row(Z_W), _row(S5_W), _row(256), _full((S5_W, S5_W)), _full((1, S5_W)),
                  _full((D_MODEL, D_MODEL)), vec, vec, _full((256, D_MODEL)), _full((D_MODEL, D_MODEL)), vec, vec, vec],
        out_specs=[_row(D_MODEL)] * 4, out_shape=[_S((L, D_MODEL))] * 4,
        compiler_params=_params(1))(x, hs, z, y0, p, w_glu, b_glu, w_out, g1, b1, ple_w, w_pg, b_pg, g2, b2)


def _post_bwd_a(dx2_or_target, is_top, t2, t1, p, ple_w, w_pg, b_pg, g1, b1, g2, b2):
    L = t1.shape[0]

    def body(d_ref, t2_ref, t1_ref, p_ref, pw_ref, wpg_ref, bpg_ref, g1_ref, b1_ref, g2_ref, b2_ref,
             dt1_ref, dpw_ref, dwpg_ref, dbpg_ref, dg1_ref, db1_ref, dg2_ref, db2_ref, loss_ref):
        @pl.when(pl.program_id(0) == 0)
        def _():
            for ref in (dpw_ref, dwpg_ref, dbpg_ref, dg1_ref, db1_ref, dg2_ref, db2_ref, loss_ref):
                ref[...] = jnp.zeros_like(ref)

        g1, g2 = g1_ref[...], g2_ref[...]
        x1, xh1, rstd1 = _ln_fwd(t1_ref[...], g1, b1_ref[...])
        x2, xh2, rstd2 = _ln_fwd(t2_ref[...], g2, b2_ref[...])
        if is_top:
            err = x2 - d_ref[...]
            loss_ref[...] += _colsum(err * err)
            dx2 = err * (1.0 / D_MODEL)
        else:
            dx2 = d_ref[...]
        p = p_ref[...]
        q = _mm(p, pw_ref[...])
        gt = _sigmoid(_mm(x1, wpg_ref[...]) + bpg_ref[...])
        dg2_ref[...] += _colsum(dx2 * xh2)
        db2_ref[...] += _colsum(dx2)
        dt2 = _ln_bwd(dx2, xh2, rstd2, g2)
        dq = dt2 * gt
        dgpre = (dt2 * q) * gt * (1.0 - gt)
        dpw_ref[...] += _mm_tn(p, dq)
        dwpg_ref[...] += _mm_tn(x1, dgpre)
        dbpg_ref[...] += _colsum(dgpre)
        dx1 = ALPHA * dt2 + _mm_nt(dgpre, wpg_ref[...])
        dg1_ref[...] += _colsum(dx1 * xh1)
        db1_ref[...] += _colsum(dx1)
        dt1_ref[...] = _ln_bwd(dx1, xh1, rstd1, g1)

    vec = _full((1, D_MODEL))
    return pl.pallas_call(
        body, name="post_bwd_a_top" if is_top else "post_bwd_a", grid=(L // TM,),
        in_specs=[_row(D_MODEL), _row(D_MODEL), _row(D_MODEL), _row(256), _full((256, D_MODEL)),
                  _full((D_MODEL, D_MODEL)), vec, vec, vec, vec, vec],
        out_specs=[_row(D_MODEL), _full((256, D_MODEL)), _full((D_MODEL, D_MODEL)), vec, vec, vec, vec, vec, vec],
        out_shape=[_S((L, D_MODEL)), _S((256, D_MODEL)), _S((D_MODEL, D_MODEL))] + [_S((1, D_MODEL))] * 6,
        compiler_params=_params(1))(dx2_or_target, t2, t1, p, ple_w, w_pg, b_pg, g1, b1, g2, b2)


def _post_bwd_b(dt1, m, z, hs, y0, w_out, w_glu, b_glu):
    L = dt1.shape[0]

    def body(dt1_ref, m_ref, z_ref, hs_ref, y0_ref, wo_ref, wg_ref, bg_ref,
             dhs_ref, dy0_ref, dzg_ref, dwo_ref, dwg_ref, dbg_ref):
        @pl.when(pl.program_id(0) == 0)
        def _():
            for ref in (dwo_ref, dwg_ref, dbg_ref):
                ref[...] = jnp.zeros_like(ref)

        dt1b = dt1_ref[...].astype(MXU)
        dm = _mm_nt(dt1b, wo_ref[...])
        dwo_ref[...] += _mm_tn(m_ref[...], dt1b)
        d_rgy, d_s5y = dm[:, :RG_W], dm[:, RG_W:]
        rg_gate = z_ref[:, C_RGG:C_RGG + RG_W]
        s5_gate = z_ref[:, C_S5G:C_S5G + S5_W]
        sl, dsl = _silu_and_grad(rg_gate)
        dhs_ref[...] = d_rgy * sl
        dzg_ref[:, :RG_W] = d_rgy * hs_ref[...] * dsl
        y0 = y0_ref[...]
        y1 = _gelu(y0)
        gl = _sigmoid(_mm(y1, wg_ref[...]) + bg_ref[...])
        sl, dsl = _silu_and_grad(s5_gate)
        dy2 = d_s5y * sl
        dzg_ref[:, RG_W:] = d_s5y * (y1 * gl) * dsl
        dglpre = (dy2 * y1) * gl * (1.0 - gl)
        dwg_ref[...] += _mm_tn(y1, dglpre)
        dbg_ref[...] += _colsum(dglpre)
        dy1 = dy2 * gl + _mm_nt(dglpre, wg_ref[...])
        dy0_ref[...] = dy1 * _gelu_grad(y0)

    return pl.pallas_call(
        body, name="post_bwd_b", grid=(L // TM,),
        in_specs=[_row(D_MODEL), _row(D_MODEL), _row(Z_W), _row(RG_W), _row(S5_W), _full((D_MODEL, D_MODEL)),
                  _full((S5_W, S5_W)), _full((1, S5_W))],
        out_specs=[_row(RG_W), _row(S5_W), _row(D_MODEL), _full((D_MODEL, D_MODEL)), _full((S5_W, S5_W)), _full((1, S5_W))],
        out_shape=[_S((L, RG_W)), _S((L, S5_W)), _S((L, D_MODEL)), _S((D_MODEL, D_MODEL)), _S((S5_W, S5_W)), _S((1, S5_W))],
        compiler_params=_params(1))(dt1, m, z, hs, y0, w_out, w_glu, b_glu)


def _adamw(parts, w, m, v):
    n, R, C = parts.shape
    tr = R
    for cand in (512, 256, 128, 64, 32, 16, 8):
        if R % cand == 0 and n * cand * C * 4 <= 4 * 1024 * 1024:
            tr = cand
            break

    def body(p_ref, w_ref, m_ref, v_ref, g_ref, d_ref, nm_ref, nv_ref):
        g = p_ref[0]
        for k in range(1, n):
            g = g + p_ref[k]
        nm = B1 * m_ref[...] + (1.0 - B1) * g
        nv = B2 * v_ref[...] + (1.0 - B2) * (g * g)
        d_ref[...] = (-LR) * ((nm / BC1) / (jnp.sqrt(nv / BC2) + EPS) + WD * w_ref[...])
        g_ref[...], nm_ref[...], nv_ref[...] = g, nm, nv

    blk = pl.BlockSpec((tr, C), lambda i: (i, 0))
    return pl.pallas_call(
        body, name="adamw", grid=(R // tr,),
        in_specs=[pl.BlockSpec((n, tr, C), lambda i: (0, i, 0)), blk, blk, blk],
        out_specs=[blk] * 4, out_shape=[_S((R, C))] * 4, compiler_params=_params(1))(parts, w, m, v)


def _me():
    return lax.axis_index("x"), lax.axis_index("y"), lax.axis_index("c")


def _lin(dev):
    return 4 * dev[0] + 2 * dev[1] + dev[2]


def _blk(ref, axis, size, idx):
    nd = len(ref.shape)
    start = idx * size
    if axis == nd - 1 and size % LANE == 0:
        start = pl.multiple_of(start, LANE)
    elif axis == nd - 2 and size % 16 == 0:
        start = pl.multiple_of(start, 16)
    ix = [slice(None)] * nd
    ix[axis] = pl.ds(start, size)
    return ref.at[tuple(ix)]


def _all_gather(shards, axes, name):
    n = len(shards)
    sizes = [s.shape[a] for s, a in zip(shards, axes)]
    out_shapes = [_S(s.shape[:a] + (N_DEV * s.shape[a],) + s.shape[a + 1:], s.dtype) for s, a in zip(shards, axes)]

    def body(*refs):
        ins, outs = refs[:n], refs[n:2 * n]
        send_sems, recv_sems, local_sems = refs[2 * n:]
        x, y, c = _me()
        me, sibling = (x, y, c), (x, y, 1 - c)
        chips = [(1 - x, y), (x, 1 - y), (1 - x, 1 - y)]

        def copy(a, k, block, to, from_input=False):
            dst = _blk(outs[a], axes[a], sizes[a], _lin(block))
            return pltpu.make_async_remote_copy(
                src_ref=ins[a] if from_input else dst, dst_ref=dst, send_sem=send_sems.at[a, k],
                recv_sem=recv_sems.at[a, k], device_id=to, device_id_type=MESH)

        mine = [pltpu.make_async_copy(ins[a], _blk(outs[a], axes[a], sizes[a], _lin(me)), local_sems.at[a]) for a in range(n)]
        for cp in mine:
            cp.start()
        first = []
        for a in range(n):
            first.append(copy(a, 0, me, sibling, True))
            first += [copy(a, 1 + j, me, (*chip, c), True) for j, chip in enumerate(chips)]
        for cp in first:
            cp.start()
        passed = []
        for j, chip in enumerate(chips):
            for a in range(n):
                copy(a, 1 + j, (*chip, c), me).wait_recv()
                cp = copy(a, 4 + j, (*chip, c), sibling)
                cp.start()
                passed.append(cp)
        for a in range(n):
            copy(a, 0, sibling, me).wait_recv()
            for j, chip in enumerate(chips):
                copy(a, 4 + j, (*chip, 1 - c), me).wait_recv()
        for cp in first + passed:
            cp.wait_send()
        for cp in mine:
            cp.wait()

    return pl.pallas_call(
        body, name=name, out_shape=out_shapes, in_specs=[ANY] * n, out_specs=[ANY] * n,
        scratch_shapes=[pltpu.SemaphoreType.DMA((n, 7)), pltpu.SemaphoreType.DMA((n, 7)), pltpu.SemaphoreType.DMA((n,))],
    )(*shards)


def _exchange(arrays, axes, name):
    n = len(arrays)
    sizes = [s.shape[a] // N_DEV for s, a in zip(arrays, axes)]
    out_shapes = [_S((N_DEV,) + s.shape[:a] + (sz,) + s.shape[a + 1:], s.dtype) for s, a, sz in zip(arrays, axes, sizes)]

    def body(*refs):
        ins, outs = refs[:n], refs[n:2 * n]
        send_sems, recv_sems, local_sems = refs[2 * n:]
        x, y, c = _me()
        me = (x, y, c)
        flip = lambda v, f: 1 - v if f else v
        peers = [(flip(x, k & 4), flip(y, k & 2), flip(c, k & 1)) for k in range(1, N_DEV)]

        def copy(a, k, src_dev_block, to):
            return pltpu.make_async_remote_copy(
                src_ref=_blk(ins[a], axes[a], sizes[a], _lin(src_dev_block)), dst_ref=outs[a].at[_lin(me)],
                send_sem=send_sems.at[a, k], recv_sem=recv_sems.at[a, k], device_id=to, device_id_type=MESH)

        mine = [pltpu.make_async_copy(_blk(ins[a], axes[a], sizes[a], _lin(me)), outs[a].at[_lin(me)], local_sems.at[a])
                for a in range(n)]
        for cp in mine:
            cp.start()
        sends = [copy(a, k, peer, peer) for a in range(n) for k, peer in enumerate(peers)]
        for cp in sends:
            cp.start()
        for a in range(n):
            for k, peer in enumerate(peers):
                pltpu.make_async_remote_copy(
                    src_ref=outs[a].at[_lin(peer)], dst_ref=outs[a].at[_lin(peer)], send_sem=send_sems.at[a, k],
                    recv_sem=recv_sems.at[a, k], device_id=peer, device_id_type=MESH).wait_recv()
        for cp in sends:
            cp.wait_send()
        for cp in mine:
            cp.wait()

    return pl.pallas_call(
        body, name=name, out_shape=out_shapes, in_specs=[ANY] * n, out_specs=[ANY] * n,
        scratch_shapes=[pltpu.SemaphoreType.DMA((n, 7)), pltpu.SemaphoreType.DMA((n, 7)), pltpu.SemaphoreType.DMA((n,))],
    )(*arrays)


def _block_diag(w, nb):
    tn, r, c = w.shape
    w = w.reshape(tn // nb, nb, r, c)
    return jnp.einsum('tarc,ab->tarbc', w, jnp.eye(nb, dtype=w.dtype)).reshape(tn // nb, nb * r, nb * c)


def _block_diag_extract(w, nb):
    t, R, C = w.shape
    w = w.reshape(t, nb, R // nb, nb, C // nb)
    return jnp.einsum('tarbc,ab->tarc', w, jnp.eye(nb, dtype=w.dtype)).reshape(t * nb, R // nb, C // nb)


SMALL = ['conv_b', 'rg_wa', 'rg_ba', 'rg_wx', 'rg_bx', 'rg_lambda', 's5_a_re', 's5_a_im', 's5_b_re', 's5_b_im',
         's5_c_re', 's5_c_im', 's5_d', 's5_log_step', 's5_b_glu', 'ln1_g', 'ln1_b', 'ple_gate_b', 'ln2_g', 'ln2_b']
WEIGHTS = ['w_in', 'conv_w', 'conv_b', 'rg_wa', 'rg_ba', 'rg_wx', 'rg_bx', 'rg_lambda', 's5_a_re', 's5_a_im', 's5_b_re',
           's5_b_im', 's5_c_re', 's5_c_im', 's5_d', 's5_log_step', 's5_w_glu', 's5_b_glu', 'w_out', 'ln1_g', 'ln1_b',
           'ple_w', 'ple_gate_w', 'ple_gate_b', 'ln2_g', 'ln2_b']
PACK_ROWS_MULT = 64


def _pack(tree):
    flat = jnp.concatenate([tree[k].reshape(-1) for k in SMALL])
    rows = -(-flat.shape[0] // (LANE * PACK_ROWS_MULT)) * PACK_ROWS_MULT
    return jnp.pad(flat, (0, rows * LANE - flat.shape[0])).reshape(rows, LANE)


def _unpack(packed, like):
    flat, out, o = packed.reshape(-1), {}, 0
    for k in SMALL:
        n = math.prod(like[k].shape)
        out[k] = flat[o:o + n].reshape(like[k].shape)
        o += n
    return out


def _local_grads(x, p, target, W, disc):
    depth = 2
    saved = []
    for i in range(depth):
        w = W[i]
        z = _inproj_fwd(x, w['w_in'])
        hs = _rg_fwd(z, w['conv_w'], w['conv_b'], w['wa_bd'], w['wx_bd'], w['rg_ba'], w['rg_bx'], w['rg_lambda'])
        d = disc[i]
        y0, s_re, s_im = _s5_fwd(z, d['bb_re'], d['bb_im'], d['lb_re'], d['lb_im'], d['c_re'], d['c_im'], w['s5_d'])
        x2, t1, t2, m = _post_fwd(x, hs, z, y0, p[i], w['s5_w_glu'], w['s5_b_glu'], w['w_out'], w['ln1_g'], w['ln1_b'],
                                  w['ple_w'], w['ple_gate_w'], w['ple_gate_b'], w['ln2_g'], w['ln2_b'])
        saved.append((x, z, hs, y0, s_re, s_im, t1, t2, m))
        x = x2

    grads = [None] * depth
    dx = target
    loss = None
    for i in reversed(range(depth)):
        w, d = W[i], disc[i]
        xin, z, hs, y0, s_re, s_im, t1, t2, m = saved[i]
        g = {}
        (dt1, g['ple_w'], g['ple_gate_w'], g['ple_gate_b'], g['ln1_g'], g['ln1_b'], g['ln2_g'], g['ln2_b'], lrow) = _post_bwd_a(
            dx, i == depth - 1, t2, t1, p[i], w['ple_w'], w['ple_gate_w'], w['ple_gate_b'], w['ln1_g'], w['ln1_b'],
            w['ln2_g'], w['ln2_b'])
        if i == depth - 1:
            loss = 0.5 / D_MODEL * jnp.sum(lrow)
        dhs, dy0, dzg, g['w_out'], g['s5_w_glu'], g['s5_b_glu'] = _post_bwd_b(dt1, m, z, hs, y0, w['w_out'], w['s5_w_glu'],
                                                                           w['s5_b_glu'])
        (dzu, g['bb_re'], g['bb_im'], g['lb_re'], g['lb_im'], g['c_re'], g['c_im'], g['s5_d']) = _s5_bwd(
            dy0, z, s_re, s_im, d['bb_re'], d['bb_im'], d['lb_re'], d['lb_im'], d['c_re'], d['c_im'], w['s5_d'])
        (dzx, g['conv_w'], g['conv_b'], g['wa_bd'], g['wx_bd'], g['rg_ba'], g['rg_bx'], g['rg_lambda']) = _rg_bwd(
            dhs, z, hs, w['conv_w'], w['conv_b'], w['wa_bd'], w['wx_bd'], w['rg_ba'], w['rg_bx'], w['rg_lambda'])
        dx, g['w_in'] = _inproj_bwd(dt1, xin, dzx, dzg, dzu, w['w_in'])
        grads[i] = g
    return loss, dx, grads


def _s5_layouts_fwd(s5_a_re, s5_a_im, s5_log_step, s5_b_re, s5_b_im, s5_c_re, s5_c_im):
    depth = s5_a_re.shape[0]
    ar, ai = s5_a_re.reshape(depth * 24, S5_P), s5_a_im.reshape(depth * 24, S5_P)
    ls = s5_log_step.reshape(depth * 24, 1)
    lr, li, cr, ci = _s5_disc_fwd(ar, ai, ls)
    col = lambda a: a.reshape(depth * S5_N, 1)
    br, bi = s5_b_re.reshape(depth * S5_N, 16), s5_b_im.reshape(depth * S5_N, 16)
    bbr, bbi = _s5_bscale_fwd(col(cr), col(ci), br, bi)
    disc = []
    for i in range(depth):
        gph = lambda a: a.reshape(depth, 24, S5_P, 16)[i]
        disc.append(dict(
            bb_re=_block_diag(jnp.swapaxes(gph(bbr), 1, 2), 8), bb_im=_block_diag(jnp.swapaxes(gph(bbi), 1, 2), 8),
            lb_re=lr.reshape(depth, 1, S5_N)[i], lb_im=li.reshape(depth, 1, S5_N)[i],
            c_re=_block_diag(jnp.swapaxes(s5_c_re[i], 1, 2), 8), c_im=_block_diag(jnp.swapaxes(s5_c_im[i], 1, 2), 8)))
    return disc, (ar, ai, ls, col(cr), col(ci), br, bi)


def _s5_layouts_bwd(grads, res):
    ar, ai, ls, cr, ci, br, bi = res
    depth = len(grads)
    stack = lambda f: jnp.stack([f(g) for g in grads])
    dbbr = stack(lambda g: jnp.swapaxes(_block_diag_extract(g['bb_re'], 8), 1, 2)).reshape(depth * S5_N, 16)
    dbbi = stack(lambda g: jnp.swapaxes(_block_diag_extract(g['bb_im'], 8), 1, 2)).reshape(depth * S5_N, 16)
    dbr, dbi, dcr, dci = _s5_bscale_bwd(cr, ci, br, bi, dbbr, dbbi)
    gp = lambda a: a.reshape(depth * 24, S5_P)
    dar, dai, dls = _s5_disc_bwd(ar, ai, ls, gp(stack(lambda g: g['lb_re'])), gp(stack(lambda g: g['lb_im'])), gp(dcr), gp(dci))
    return dict(
        s5_a_re=dar.reshape(depth, 24, S5_P), s5_a_im=dai.reshape(depth, 24, S5_P), s5_log_step=dls.reshape(depth, 24),
        s5_b_re=dbr.reshape(depth, 24, S5_P, 16), s5_b_im=dbi.reshape(depth, 24, S5_P, 16),
        s5_c_re=stack(lambda g: jnp.swapaxes(_block_diag_extract(g['c_re'], 8), 1, 2)),
        s5_c_im=stack(lambda g: jnp.swapaxes(_block_diag_extract(g['c_im'], 8), 1, 2)))


def _layer_weights(full, i):
    row = lambda a: a[i].reshape(1, -1)
    return dict(
        w_in=full['w_in'][i], w_out=full['w_out'][i], ple_w=full['ple_w'][i], ple_gate_w=full['ple_gate_w'][i],
        s5_w_glu=full['s5_w_glu'][i], conv_w=full['conv_w'][i], conv_b=row(full['conv_b']),
        wa_bd=_block_diag(full['rg_wa'][i], 2), wx_bd=_block_diag(full['rg_wx'][i], 2),
        rg_ba=row(full['rg_ba']), rg_bx=row(full['rg_bx']), rg_lambda=row(full['rg_lambda']),
        s5_d=row(full['s5_d']), s5_b_glu=row(full['s5_b_glu']), ln1_g=row(full['ln1_g']), ln1_b=row(full['ln1_b']),
        ple_gate_b=row(full['ple_gate_b']), ln2_g=row(full['ln2_g']), ln2_b=row(full['ln2_b']))


def _full_grads(full, x, p, target):
    disc, res = _s5_layouts_fwd(full['s5_a_re'], full['s5_a_im'], full['s5_log_step'], full['s5_b_re'], full['s5_b_im'],
                                full['s5_c_re'], full['s5_c_im'])
    W = [_layer_weights(full, i) for i in range(2)]
    loss, gx, grads = _local_grads(x, p, target, W, disc)
    stack = lambda f: jnp.stack([f(g) for g in grads])
    out = _s5_layouts_bwd(grads, res)
    for k in ('w_in', 'w_out', 'ple_w', 'ple_gate_w', 's5_w_glu', 'conv_w'):
        out[k] = stack(lambda g: g[k])
    for k in ('conv_b', 'rg_ba', 'rg_bx', 'rg_lambda', 's5_b_glu', 'ln1_g', 'ln1_b', 'ple_gate_b', 'ln2_g', 'ln2_b'):
        out[k] = stack(lambda g: g[k][0])
    out['s5_d'] = stack(lambda g: g['s5_d'][0]).reshape(2, 24, 16)
    out['rg_wa'] = stack(lambda g: _block_diag_extract(g['wa_bd'], 2))
    out['rg_wx'] = stack(lambda g: _block_diag_extract(g['wx_bd'], 2))
    return loss, gx, out


SHARD_AXIS = {'w_in': 2, 'w_out': 1, 'ple_w': 2, 'ple_gate_w': 1, 's5_w_glu': 1}


def kernel(x, p, w_in, conv_w, conv_b, rg_wa, rg_ba, rg_wx, rg_bx, rg_lambda, s5_a_re, s5_a_im, s5_b_re, s5_b_im, s5_c_re, s5_c_im, s5_d, s5_log_step, s5_w_glu, s5_b_glu, w_out, ln1_g, ln1_b, ple_w, ple_gate_w, ple_gate_b, ln2_g, ln2_b, loss_target, m_w_in, m_conv_w, m_conv_b, m_rg_wa, m_rg_ba, m_rg_wx, m_rg_bx, m_rg_lambda, m_s5_a_re, m_s5_a_im, m_s5_b_re, m_s5_b_im, m_s5_c_re, m_s5_c_im, m_s5_d, m_s5_log_step, m_s5_w_glu, m_s5_b_glu, m_w_out, m_ln1_g, m_ln1_b, m_ple_w, m_ple_gate_w, m_ple_gate_b, m_ln2_g, m_ln2_b, v_w_in, v_conv_w, v_conv_b, v_rg_wa, v_rg_ba, v_rg_wx, v_rg_bx, v_rg_lambda, v_s5_a_re, v_s5_a_im, v_s5_b_re, v_s5_b_im, v_s5_c_re, v_s5_c_im, v_s5_d, v_s5_log_step, v_s5_w_glu, v_s5_b_glu, v_w_out, v_ln1_g, v_ln1_b, v_ple_w, v_ple_gate_w, v_ple_gate_b, v_ln2_g, v_ln2_b):
    local = dict(locals())
    w = {k: local[k] for k in WEIGHTS}
    mom = {k: local['m_' + k] for k in WEIGHTS}
    var = {k: local['v_' + k] for k in WEIGHTS}

    big = list(SHARD_AXIS)
    gathered = _all_gather([w[k].astype(WIRE) for k in big] + [conv_w[None]], [SHARD_AXIS[k] for k in big] + [0], "gather_weights")
    full = dict(w)
    full.update(dict(zip(big, gathered[:-1])))
    full['conv_w'] = jnp.moveaxis(gathered[-1], 0, 2).reshape(2, 4, RG_W)

    loss, grad_x, g = _full_grads(full, x[0], p[:, 0], loss_target[0])
    loss = lax.psum(loss, ("x", "y", "c"))

    conv_blocks = jnp.moveaxis(g['conv_w'].reshape(2, 4, N_DEV, RG_W // N_DEV), 2, 0).reshape(N_DEV, 8, RG_W // N_DEV)
    recv = _exchange([g[k] for k in big] + [conv_blocks], [SHARD_AXIS[k] for k in big] + [0], "exchange_grads")
    outs = {}
    for k, r in zip(big + ['conv_w'], recv):
        shard = w[k].shape
        c = shard[-1]
        two = lambda a: a.reshape(-1, c)
        outs[k] = [o.reshape(shard) for o in _adamw(r.reshape(N_DEV, -1, c), two(w[k]), two(mom[k]), two(var[k]))]

    parts = _all_gather([_pack(g)], [0], "gather_small_grads")[0]
    rows = parts.shape[0] // N_DEV
    small = _adamw(parts.reshape(N_DEV, rows, LANE), _pack(w), _pack(mom), _pack(var))
    small = [_unpack(o, w) for o in small]
    for k in SMALL:
        outs[k] = [o[k] for o in small]

    res = [loss, grad_x[None]]
    for j in range(4):
        res += [outs[k][j] for k in WEIGHTS]
    return tuple(res)
```

```python
import functools
import math

import jax
import jax.numpy as jnp
from jax import lax
from jax.experimental import pallas as pl
from jax.experimental.pallas import tpu as pltpu

F32 = jnp.float32
MXU = jnp.bfloat16
WIRE = jnp.bfloat16

N_DEV = 8
D_MODEL = 1024
RG_W = 640
S5_W = 384
S5_P = 64
S5_N = 24 * S5_P
Z_W = 2 * RG_W + 2 * S5_W
C_RGG = RG_W
C_S5U = 2 * RG_W
C_S5G = 2 * RG_W + S5_W
LANE = 128
N_RG_T = RG_W // LANE
N_S5_T = S5_W // LANE
N_S5_J = 4
ALPHA = (2.0 * 2) ** 0.25
LN_EPS = 1e-5
RG_C = 8.0
LR, B1, B2, EPS, WD, STEP = 0.001, 0.9, 0.999, 1e-08, 0.01, 10
BC1 = 1.0 - B1 ** STEP
BC2 = 1.0 - B2 ** STEP
RC = 256
TM = 256
VMEM_LIMIT = 56 * 1024 * 1024

MESH = pl.DeviceIdType.MESH
ANY = pl.BlockSpec(memory_space=pl.ANY)


def _params(n_grid_axes, vmem=VMEM_LIMIT):
    return pltpu.CompilerParams(dimension_semantics=("arbitrary",) * n_grid_axes, vmem_limit_bytes=vmem)


def _S(shape, dtype=F32):
    return jax.ShapeDtypeStruct(tuple(shape), dtype)


def _sigmoid(x):
    return 1.0 / (1.0 + jnp.exp(-x))


def _silu_and_grad(x):
    s = _sigmoid(x)
    return x * s, s * (1.0 + x * (1.0 - s))


_GELU_C = math.sqrt(2.0 / math.pi)


def _gelu(x):
    return 0.5 * x * (1.0 + jnp.tanh(_GELU_C * (x + 0.044715 * (x * x * x))))


def _gelu_grad(x):
    th = jnp.tanh(_GELU_C * (x + 0.044715 * (x * x * x)))
    return 0.5 * (1.0 + th) + 0.5 * x * (1.0 - th * th) * (_GELU_C * (1.0 + 3.0 * 0.044715 * (x * x)))


def _mm(a, b):
    return jnp.dot(a.astype(MXU), b.astype(MXU), preferred_element_type=F32)


def _mm_nt(a, b):
    return lax.dot_general(a.astype(MXU), b.astype(MXU), (((1,), (1,)), ((), ())), preferred_element_type=F32)


def _mm_tn(a, b):
    return lax.dot_general(a.astype(MXU), b.astype(MXU), (((0,), (0,)), ((), ())), preferred_element_type=F32)


def _ln_fwd(t, g, b):
    mu = jnp.mean(t, axis=-1, keepdims=True)
    tc = t - mu
    var = jnp.mean(tc * tc, axis=-1, keepdims=True)
    rstd = lax.rsqrt(var + LN_EPS)
    xhat = tc * rstd
    return xhat * g + b, xhat, rstd


def _ln_bwd(dy, xhat, rstd, g):
    dxh = dy * g
    m1 = jnp.mean(dxh, axis=-1, keepdims=True)
    m2 = jnp.mean(dxh * xhat, axis=-1, keepdims=True)
    return rstd * (dxh - m1 - xhat * m2)


def _colsum(a):
    return jnp.sum(a, axis=0, keepdims=True)


def _down(x, d, rows, fill):
    return jnp.where(rows >= d, pltpu.roll(x, d, 0), fill)


def _up(x, d, rows, fill):
    n = x.shape[0]
    return jnp.where(rows < n - d, pltpu.roll(x, n - d, 0), fill)


def _scan_steps(n):
    d = 1
    while d < n:
        yield d
        d *= 2


def _halo(ref, c, r0):
    rp = pl.multiple_of(jnp.maximum(r0 - 8, 0), 8)
    return jnp.where(c > 0, ref[pl.ds(rp, 8), :], 0.0)


def _conv_taps(xe):
    return [pltpu.roll(xe, 3, 0)[8:, :], pltpu.roll(xe, 2, 0)[8:, :], pltpu.roll(xe, 1, 0)[8:, :], xe[8:, :]]


def _rg_gates(h, wa, wx, ba, bx, sp):
    r = _sigmoid(_mm(h, wa) + ba)
    i = _sigmoid(_mm(h, wx) + bx)
    log_a = (-RG_C) * r * sp
    a = jnp.exp(log_a)
    mult = jnp.sqrt(-jnp.tanh(log_a) * (a * a + 1.0))
    return r, i, a, mult


def _softplus(y):
    return jnp.maximum(y, 0.0) + jnp.log1p(jnp.exp(-jnp.abs(y)))


def _inproj_fwd(x, w_in):
    L = x.shape[0]

    def body(x_ref, w_ref, z_ref):
        z_ref[...] = _mm(x_ref[...], w_ref[...])

    return pl.pallas_call(
        body, name="inproj_fwd", grid=(L // TM,),
        in_specs=[pl.BlockSpec((TM, D_MODEL), lambda i: (i, 0)), pl.BlockSpec((D_MODEL, Z_W), lambda i: (0, 0))],
        out_specs=pl.BlockSpec((TM, Z_W), lambda i: (i, 0)),
        out_shape=_S((L, Z_W)), compiler_params=_params(1))(x, w_in)


def _inproj_bwd(dt1, x, dzx, dzg, dzu, w_in):
    L = x.shape[0]

    def body(dt1_ref, x_ref, dzx_ref, dzg_ref, dzu_ref, w_ref, dx_ref, dw_ref, acc_ref):
        @pl.when(pl.program_id(0) == 0)
        def _():
            acc_ref[...] = jnp.zeros_like(acc_ref)
        dzg = dzg_ref[...]
        dz = jnp.concatenate([dzx_ref[...], dzg[:, :RG_W], dzu_ref[...], dzg[:, RG_W:]], axis=1).astype(MXU)
        dx_ref[...] = ALPHA * dt1_ref[...] + _mm_nt(dz, w_ref[...])
        acc_ref[...] += _mm_tn(x_ref[...], dz)

        @pl.when(pl.program_id(0) == L // TM - 1)
        def _():
            dw_ref[...] = acc_ref[...].astype(WIRE)

    row = lambda w: pl.BlockSpec((TM, w), lambda i: (i, 0))
    return pl.pallas_call(
        body, name="inproj_bwd", grid=(L // TM,),
        in_specs=[row(D_MODEL), row(D_MODEL), row(RG_W), row(D_MODEL), row(S5_W),
                  pl.BlockSpec((D_MODEL, Z_W), lambda i: (0, 0))],
        out_specs=[row(D_MODEL), pl.BlockSpec((D_MODEL, Z_W), lambda i: (0, 0))],
        out_shape=[_S((L, D_MODEL)), _S((D_MODEL, Z_W), WIRE)], scratch_shapes=[pltpu.VMEM((D_MODEL, Z_W), F32)],
        compiler_params=_params(1))(dt1, x, dzx, dzg, dzu, w_in)


def _rg_specs(L):
    tile = lambda rows: pl.BlockSpec((rows, LANE), lambda c: (0, c))
    return tile, pl.BlockSpec((None, LANE, LANE), lambda c: (c, 0, 0))


def _rg_fwd(z, cw, cb, wa_bd, wx_bd, ba, bx, lam):
    L = z.shape[0]

    def body(x_ref, cw_ref, cb_ref, wa_ref, wx_ref, ba_ref, bx_ref, lam_ref, hs_ref):
        w, b = cw_ref[...], cb_ref[...]
        wa, wx, ba_, bx_ = wa_ref[...].astype(MXU), wx_ref[...].astype(MXU), ba_ref[...], bx_ref[...]
        sp = _softplus(-lam_ref[...])
        rows = lax.broadcasted_iota(jnp.int32, (RC, LANE), 0)

        def step(c, carry):
            r0 = pl.multiple_of(c * RC, RC)
            xe = jnp.concatenate([_halo(x_ref, c, r0), x_ref[pl.ds(r0, RC), :]], axis=0)
            t = _conv_taps(xe)
            h = t[0] * w[0:1] + t[1] * w[1:2] + t[2] * w[2:3] + t[3] * w[3:4] + b
            _, i, a, mult = _rg_gates(h, wa, wx, ba_, bx_, sp)
            A, U = a, mult * (i * h)
            for d in _scan_steps(RC):
                U = A * _down(U, d, rows, 0.0) + U
                A = A * _down(A, d, rows, 1.0)
            hs = A * carry + U
            hs_ref[pl.ds(r0, RC), :] = hs
            return hs[RC - 1:RC, :]

        lax.fori_loop(0, L // RC, step, jnp.zeros((1, LANE), F32))

    tile, bd = _rg_specs(L)
    return pl.pallas_call(
        body, name="rg_fwd", grid=(N_RG_T,),
        in_specs=[tile(L), tile(4), tile(1), bd, bd, tile(1), tile(1), tile(1)],
        out_specs=tile(L), out_shape=_S((L, RG_W)), compiler_params=_params(1))(z, cw, cb, wa_bd, wx_bd, ba, bx, lam)


def _rg_bwd(dhs, z, hs, cw, cb, wa_bd, wx_bd, ba, bx, lam):
    L = z.shape[0]

    def body(g_ref, x_ref, hs_ref, cw_ref, cb_ref, wa_ref, wx_ref, ba_ref, bx_ref, lam_ref,
             dx_ref, dcw_ref, dcb_ref, dwa_ref, dwx_ref, dba_ref, dbx_ref, dlam_ref):
        w, b = cw_ref[...], cb_ref[...]
        wa, wx, ba_, bx_ = wa_ref[...].astype(MXU), wx_ref[...].astype(MXU), ba_ref[...], bx_ref[...]
        lam = lam_ref[...]
        sp = _softplus(-lam)
        rows = lax.broadcasted_iota(jnp.int32, (RC, LANE), 0)
        for ref in (dcw_ref, dcb_ref, dwa_ref, dwx_ref, dba_ref, dbx_ref, dlam_ref):
            ref[...] = jnp.zeros_like(ref)
        nch = L // RC

        def step(k, carry):
            cin, nxt = carry
            c = nch - 1 - k
            r0 = pl.multiple_of(c * RC, RC)
            xe = jnp.concatenate([_halo(x_ref, c, r0), x_ref[pl.ds(r0, RC), :]], axis=0)
            t = _conv_taps(xe)
            h = t[0] * w[0:1] + t[1] * w[1:2] + t[2] * w[2:3] + t[3] * w[3:4] + b
            r, i, a, mult = _rg_gates(h, wa, wx, ba_, bx_, sp)
            hs_e = jnp.concatenate([_halo(hs_ref, c, r0), hs_ref[pl.ds(r0, RC), :]], axis=0)
            hs_prev = pltpu.roll(hs_e, 1, 0)[8:, :]
            U = g_ref[pl.ds(r0, RC), :] + jnp.where(rows == RC - 1, cin, 0.0)
            A = _up(a, 1, rows, 0.0)
            for d in _scan_steps(RC):
                U = A * _up(U, d, rows, 0.0) + U
                A = A * _up(A, d, rows, 1.0)
            dh = U
            cin_new = a[0:1, :] * dh[0:1, :]
            ih = i * h
            dlog_a = dh * hs_prev * a - (dh * ih) * (a * a) / mult
            di = dh * mult * h
            dhin = dh * mult * i
            dr = dlog_a * ((-RG_C) * sp)
            dlam_ref[...] += _colsum(dlog_a * r)
            dra = dr * r * (1.0 - r)
            dia = di * i * (1.0 - i)
            dwa_ref[...] += _mm_tn(h, dra)
            dwx_ref[...] += _mm_tn(h, dia)
            dba_ref[...] += _colsum(dra)
            dbx_ref[...] += _colsum(dia)
            dhin = dhin + _mm_nt(dra, wa) + _mm_nt(dia, wx)
            de = jnp.concatenate([dhin, nxt], axis=0)
            n = RC + 8
            dx = (dhin * w[3:4] + pltpu.roll(de, n - 1, 0)[:RC, :] * w[2:3]
                  + pltpu.roll(de, n - 2, 0)[:RC, :] * w[1:2] + pltpu.roll(de, n - 3, 0)[:RC, :] * w[0:1])
            dx_ref[pl.ds(r0, RC), :] = dx
            for kk in range(4):
                dcw_ref[kk:kk + 1, :] += _colsum(dhin * t[kk])
            dcb_ref[...] += _colsum(dhin)
            return cin_new, dhin[0:8, :]

        lax.fori_loop(0, nch, step, (jnp.zeros((1, LANE), F32), jnp.zeros((8, LANE), F32)))
        dlam_ref[...] = dlam_ref[...] * (RG_C * _sigmoid(-lam))

    tile, bd = _rg_specs(L)
    return pl.pallas_call(
        body, name="rg_bwd", grid=(N_RG_T,),
        in_specs=[tile(L), tile(L), tile(L), tile(4), tile(1), bd, bd, tile(1), tile(1), tile(1)],
        out_specs=[tile(L), tile(4), tile(1), bd, bd, tile(1), tile(1), tile(1)],
        out_shape=[_S((L, RG_W)), _S((4, RG_W)), _S((1, RG_W)), _S((N_RG_T, LANE, LANE)), _S((N_RG_T, LANE, LANE)),
                   _S((1, RG_W)), _S((1, RG_W)), _S((1, RG_W))],
        compiler_params=_params(1))(dhs, z, hs, cw, cb, wa_bd, wx_bd, ba, bx, lam)


def _cmul(ar, ai, br, bi):
    return ar * br - ai * bi, ar * bi + ai * br


def _s5_powers(lr, li):
    out = []
    for _ in _scan_steps(RC):
        out.append((lr, li))
        lr, li = _cmul(lr, li, lr, li)
    return out


def _s5_specs(L):
    in_tile = pl.BlockSpec((L, LANE), lambda t, j: (0, t))
    st = pl.BlockSpec((L, LANE), lambda t, j: (0, t * N_S5_J + j))
    bb = pl.BlockSpec((None, LANE, LANE), lambda t, j: (t, 0, j))
    cc = pl.BlockSpec((None, LANE, LANE), lambda t, j: (t, j, 0))
    lb = pl.BlockSpec((1, LANE), lambda t, j: (0, t * N_S5_J + j))
    dv = pl.BlockSpec((1, LANE), lambda t, j: (0, t))
    return in_tile, st, bb, cc, lb, dv


def _s5_fwd(z, bb_re, bb_im, lb_re, lb_im, c_re, c_im, dvec):
    L = z.shape[0]

    def body(u_ref, bbr_ref, bbi_ref, lr_ref, li_ref, cr_ref, ci_ref, d_ref, y_ref, sr_ref, si_ref):
        j = pl.program_id(1)
        bbr, bbi = bbr_ref[...].astype(MXU), bbi_ref[...].astype(MXU)
        cr, ci = cr_ref[...].astype(MXU), ci_ref[...].astype(MXU)
        lr, li = lr_ref[...], li_ref[...]
        dv = d_ref[...]
        rows = lax.broadcasted_iota(jnp.int32, (RC, LANE), 0)
        pw = _s5_powers(lr, li)
        er, ei = jnp.broadcast_to(lr, (RC, LANE)), jnp.broadcast_to(li, (RC, LANE))
        for d in _scan_steps(RC):
            er, ei = _cmul(er, ei, _down(er, d, rows, 1.0), _down(ei, d, rows, 0.0))

        def step(c, carry):
            kr, ki = carry
            r0 = pl.multiple_of(c * RC, RC)
            u = u_ref[pl.ds(r0, RC), :]
            ub = u.astype(MXU)
            sr = jnp.dot(ub, bbr, preferred_element_type=F32)
            si = jnp.dot(ub, bbi, preferred_element_type=F32)
            for d, (pr, pi) in zip(_scan_steps(RC), pw):
                shr, shi = _down(sr, d, rows, 0.0), _down(si, d, rows, 0.0)
                sr, si = sr + (pr * shr - pi * shi), si + (pr * shi + pi * shr)
            sr = sr + (er * kr - ei * ki)
            si = si + (er * ki + ei * kr)
            sr_ref[pl.ds(r0, RC), :] = sr
            si_ref[pl.ds(r0, RC), :] = si
            y = _mm(sr, cr) - _mm(si, ci)

            @pl.when(j == 0)
            def _():
                y_ref[pl.ds(r0, RC), :] = y + dv * u

            @pl.when(j > 0)
            def _():
                y_ref[pl.ds(r0, RC), :] += y

            return sr[RC - 1:RC, :], si[RC - 1:RC, :]

        zero = jnp.zeros((1, LANE), F32)
        lax.fori_loop(0, L // RC, step, (zero, zero))

    in_tile, st, bb, cc, lb, dv = _s5_specs(L)
    u_tile = pl.BlockSpec((L, LANE), lambda t, j: (0, C_S5U // LANE + t))
    return pl.pallas_call(
        body, name="s5_fwd", grid=(N_S5_T, N_S5_J),
        in_specs=[u_tile, bb, bb, lb, lb, cc, cc, dv],
        out_specs=[in_tile, st, st],
        out_shape=[_S((L, S5_W)), _S((L, S5_N)), _S((L, S5_N))],
        compiler_params=_params(2))(z, bb_re, bb_im, lb_re, lb_im, c_re, c_im, dvec)


def _s5_bwd(dy0, z, s_re, s_im, bb_re, bb_im, lb_re, lb_im, c_re, c_im, dvec):
    L = z.shape[0]

    def body(dy_ref, u_ref, sr_ref, si_ref, bbr_ref, bbi_ref, lr_ref, li_ref, cr_ref, ci_ref, d_ref,
             du_ref, dbbr_ref, dbbi_ref, dlr_ref, dli_ref, dcr_ref, dci_ref, dd_ref):
        j = pl.program_id(1)
        bbr, bbi = bbr_ref[...].astype(MXU), bbi_ref[...].astype(MXU)
        cr, ci = cr_ref[...].astype(MXU), ci_ref[...].astype(MXU)
        lr, li = lr_ref[...], -li_ref[...]
        dv = d_ref[...]
        rows = lax.broadcasted_iota(jnp.int32, (RC, LANE), 0)
        pw = _s5_powers(lr, li)
        er, ei = jnp.broadcast_to(lr, (RC, LANE)), jnp.broadcast_to(li, (RC, LANE))
        for d in _scan_steps(RC):
            er, ei = _cmul(er, ei, _up(er, d, rows, 1.0), _up(ei, d, rows, 0.0))
        for ref in (dbbr_ref, dbbi_ref, dlr_ref, dli_ref, dcr_ref, dci_ref):
            ref[...] = jnp.zeros_like(ref)

        @pl.when(j == 0)
        def _():
            dd_ref[...] = jnp.zeros_like(dd_ref)

        nch = L // RC

        def step(k, carry):
            kr, ki = carry
            c = nch - 1 - k
            r0 = pl.multiple_of(c * RC, RC)
            dy = dy_ref[pl.ds(r0, RC), :]
            u = u_ref[pl.ds(r0, RC), :]
            dyb, ub = dy.astype(MXU), u.astype(MXU)
            sr, si = sr_ref[pl.ds(r0, RC), :], si_ref[pl.ds(r0, RC), :]
            dcr_ref[...] += _mm_tn(sr, dyb)
            dci_ref[...] -= _mm_tn(si, dyb)
            gr = _mm_nt(dyb, cr)
            gi = -_mm_nt(dyb, ci)
            for d, (pr, pi) in zip(_scan_steps(RC), pw):
                shr, shi = _up(gr, d, rows, 0.0), _up(gi, d, rows, 0.0)
                gr, gi = gr + (pr * shr - pi * shi), gi + (pr * shi + pi * shr)
            gr = gr + (er * kr - ei * ki)
            gi = gi + (er * ki + ei * kr)
            pr_ = pltpu.roll(jnp.concatenate([_halo(sr_ref, c, r0), sr], axis=0), 1, 0)[8:, :]
            pi_ = pltpu.roll(jnp.concatenate([_halo(si_ref, c, r0), si], axis=0), 1, 0)[8:, :]
            dlr_ref[...] += _colsum(pr_ * gr + pi_ * gi)
            dli_ref[...] += _colsum(pr_ * gi - pi_ * gr)
            grb, gib = gr.astype(MXU), gi.astype(MXU)
            dbbr_ref[...] += _mm_tn(ub, grb)
            dbbi_ref[...] += _mm_tn(ub, gib)
            du = _mm_nt(grb, bbr) + _mm_nt(gib, bbi)

            @pl.when(j == 0)
            def _():
                du_ref[pl.ds(r0, RC), :] = du + dv * dy
                dd_ref[...] += _colsum(dy * u)

            @pl.when(j > 0)
            def _():
                du_ref[pl.ds(r0, RC), :] += du

            return gr[0:1, :], gi[0:1, :]

        zero = jnp.zeros((1, LANE), F32)
        lax.fori_loop(0, nch, step, (zero, zero))

    in_tile, st, bb, cc, lb, dv = _s5_specs(L)
    u_tile = pl.BlockSpec((L, LANE), lambda t, j: (0, C_S5U // LANE + t))
    return pl.pallas_call(
        body, name="s5_bwd", grid=(N_S5_T, N_S5_J),
        in_specs=[in_tile, u_tile, st, st, bb, bb, lb, lb, cc, cc, dv],
        out_specs=[in_tile, bb, bb, lb, lb, cc, cc, dv],
        out_shape=[_S((L, S5_W)), _S((N_S5_T, LANE, 4 * LANE)), _S((N_S5_T, LANE, 4 * LANE)), _S((1, S5_N)), _S((1, S5_N)),
                   _S((N_S5_T, 4 * LANE, LANE)), _S((N_S5_T, 4 * LANE, LANE)), _S((1, S5_W))],
        compiler_params=_params(2))(dy0, z, s_re, s_im, bb_re, bb_im, lb_re, lb_im, c_re, c_im, dvec)


def _disc(ar, ai, ls):
    dt = jnp.exp(ls)
    mag = jnp.exp(ar * dt)
    lr = mag * jnp.cos(ai * dt)
    li = mag * jnp.sin(ai * dt)
    den = ar * ar + ai * ai
    cr = ((lr - 1.0) * ar + li * ai) / den
    ci = (li * ar - (lr - 1.0) * ai) / den
    return lr, li, cr, ci


def _s5_disc_fwd(ar, ai, ls):
    def body(ar_ref, ai_ref, ls_ref, lr_ref, li_ref, cr_ref, ci_ref):
        lr, li, cr, ci = _disc(ar_ref[...], ai_ref[...], ls_ref[...])
        lr_ref[...], li_ref[...], cr_ref[...], ci_ref[...] = lr, li, cr, ci

    sh = _S(ar.shape)
    return pl.pallas_call(body, name="s5_disc_fwd", out_shape=[sh, sh, sh, sh])(ar, ai, ls)


def _s5_disc_bwd(ar, ai, ls, dlr, dli, dcr, dci):
    def body(ar_ref, ai_ref, ls_ref, dlr_ref, dli_ref, dcr_ref, dci_ref, dar_ref, dai_ref, dls_ref):
        _, vjp = jax.vjp(_disc, ar_ref[...], ai_ref[...], jnp.broadcast_to(ls_ref[...], ar_ref.shape))
        dar, dai, dls = vjp((dlr_ref[...], dli_ref[...], dcr_ref[...], dci_ref[...]))
        dar_ref[...], dai_ref[...] = dar, dai
        dls_ref[...] = jnp.sum(dls, axis=1, keepdims=True)

    return pl.pallas_call(body, name="s5_disc_bwd", out_shape=[_S(ar.shape), _S(ar.shape), _S(ls.shape)])(
        ar, ai, ls, dlr, dli, dcr, dci)


def _s5_bscale_fwd(cr, ci, br, bi):
    def body(cr_ref, ci_ref, br_ref, bi_ref, or_ref, oi_ref):
        or_ref[...], oi_ref[...] = _cmul(cr_ref[...], ci_ref[...], br_ref[...], bi_ref[...])

    return pl.pallas_call(body, name="s5_bscale_fwd", out_shape=[_S(br.shape), _S(br.shape)])(cr, ci, br, bi)


def _s5_bscale_bwd(cr, ci, br, bi, gr, gi):
    def body(cr_ref, ci_ref, br_ref, bi_ref, gr_ref, gi_ref, dbr_ref, dbi_ref, dcr_ref, dci_ref):
        cr_, ci_, br_, bi_, gr_, gi_ = (r[...] for r in (cr_ref, ci_ref, br_ref, bi_ref, gr_ref, gi_ref))
        dbr_ref[...] = cr_ * gr_ + ci_ * gi_
        dbi_ref[...] = cr_ * gi_ - ci_ * gr_
        dcr_ref[...] = jnp.sum(gr_ * br_ + gi_ * bi_, axis=1, keepdims=True)
        dci_ref[...] = jnp.sum(gi_ * br_ - gr_ * bi_, axis=1, keepdims=True)

    return pl.pallas_call(body, name="s5_bscale_bwd",
                          out_shape=[_S(br.shape), _S(br.shape), _S(cr.shape), _S(cr.shape)])(cr, ci, br, bi, gr, gi)


def _row(w):
    return pl.BlockSpec((TM, w), lambda i: (i, 0))


def _full(shape):
    return pl.BlockSpec(tuple(shape), lambda i: (0,) * len(shape))


def _post_fwd(x, hs, z, y0, p, w_glu, b_glu, w_out, g1, b1, ple_w, w_pg, b_pg, g2, b2):
    L = x.shape[0]

    def body(x_ref, hs_ref, z_ref, y0_ref, p_ref, wg_ref, bg_ref, wo_ref, g1_ref, b1_ref, pw_ref, wpg_ref, bpg_ref,
             g2_ref, b2_ref, x2_ref, t1_ref, t2_ref, m_ref):
        rg_gate = z_ref[:, C_RGG:C_RGG + RG_W]
        s5_gate = z_ref[:, C_S5G:C_S5G + S5_W]
        rg_y = hs_ref[...] * _silu_and_grad(rg_gate)[0]
        y1 = _gelu(y0_ref[...])
        gl = _sigmoid(_mm(y1, wg_ref[...]) + bg_ref[...])
        s5_y = (y1 * gl) * _silu_and_grad(s5_gate)[0]
        m_ref[:, :RG_W] = rg_y
        m_ref[:, RG_W:] = s5_y
        mix = _mm(m_ref[...], wo_ref[...])
        t1 = ALPHA * x_ref[...] + mix
        x1, _, _ = _ln_fwd(t1, g1_ref[...], b1_ref[...])
        e = _mm(p_ref[...], pw_ref[...]) * _sigmoid(_mm(x1, wpg_ref[...]) + bpg_ref[...])
        t2 = ALPHA * x1 + e
        x2, _, _ = _ln_fwd(t2, g2_ref[...], b2_ref[...])
        t1_ref[...], t2_ref[...], x2_ref[...] = t1, t2, x2

    vec = _full((1, D_MODEL))
    return pl.pallas_call(
        body, name="post_fwd", grid=(L // TM,),
        in_specs=[_row(D_MODEL), _row(RG_W), _row(Z_W), _row(S5_W), _row(256), _full((S5_W, S5_W)), _full((1, S5_W)),
                  _full((D_MODEL, D_MODEL)), vec, vec, _full((256, D_MODEL)), _full((D_MODEL, D_MODEL)), vec, vec, vec],
        out_specs=[_row(D_MODEL)] * 4, out_shape=[_S((L, D_MODEL))] * 4,
        compiler_params=_params(1))(x, hs, z, y0, p, w_glu, b_glu, w_out, g1, b1, ple_w, w_pg, b_pg, g2, b2)


def _post_bwd_a(dx2_or_target, is_top, t2, t1, p, ple_w, w_pg, b_pg, g1, b1, g2, b2):
    L = t1.shape[0]

    def body(d_ref, t2_ref, t1_ref, p_ref, pw_ref, wpg_ref, bpg_ref, g1_ref, b1_ref, g2_ref, b2_ref,
             dt1_ref, dpw_out, dwpg_out, dbpg_ref, dg1_ref, db1_ref, dg2_ref, db2_ref, loss_ref, dpw_ref, dwpg_ref):
        @pl.when(pl.program_id(0) == 0)
        def _():
            for ref in (dpw_ref, dwpg_ref, dbpg_ref, dg1_ref, db1_ref, dg2_ref, db2_ref, loss_ref):
                ref[...] = jnp.zeros_like(ref)

        g1, g2 = g1_ref[...], g2_ref[...]
        x1, xh1, rstd1 = _ln_fwd(t1_ref[...], g1, b1_ref[...])
        x2, xh2, rstd2 = _ln_fwd(t2_ref[...], g2, b2_ref[...])
        if is_top:
            err = x2 - d_ref[...]
            loss_ref[...] += _colsum(err * err)
            dx2 = err * (1.0 / D_MODEL)
        else:
            dx2 = d_ref[...]
        p = p_ref[...]
        q = _mm(p, pw_ref[...])
        gt = _sigmoid(_mm(x1, wpg_ref[...]) + bpg_ref[...])
        dg2_ref[...] += _colsum(dx2 * xh2)
        db2_ref[...] += _colsum(dx2)
        dt2 = _ln_bwd(dx2, xh2, rstd2, g2)
        dq = dt2 * gt
        dgpre = (dt2 * q) * gt * (1.0 - gt)
        dpw_ref[...] += _mm_tn(p, dq)
        dwpg_ref[...] += _mm_tn(x1, dgpre)
        dbpg_ref[...] += _colsum(dgpre)
        dx1 = ALPHA * dt2 + _mm_nt(dgpre, wpg_ref[...])
        dg1_ref[...] += _colsum(dx1 * xh1)
        db1_ref[...] += _colsum(dx1)
        dt1_ref[...] = _ln_bwd(dx1, xh1, rstd1, g1)

        @pl.when(pl.program_id(0) == L // TM - 1)
        def _():
            dpw_out[...] = dpw_ref[...].astype(WIRE)
            dwpg_out[...] = dwpg_ref[...].astype(WIRE)

    vec = _full((1, D_MODEL))
    return pl.pallas_call(
        body, name="post_bwd_a_top" if is_top else "post_bwd_a", grid=(L // TM,),
        in_specs=[_row(D_MODEL), _row(D_MODEL), _row(D_MODEL), _row(256), _full((256, D_MODEL)),
                  _full((D_MODEL, D_MODEL)), vec, vec, vec, vec, vec],
        out_specs=[_row(D_MODEL), _full((256, D_MODEL)), _full((D_MODEL, D_MODEL)), vec, vec, vec, vec, vec, vec],
        out_shape=[_S((L, D_MODEL)), _S((256, D_MODEL), WIRE), _S((D_MODEL, D_MODEL), WIRE)] + [_S((1, D_MODEL))] * 6,
        scratch_shapes=[pltpu.VMEM((256, D_MODEL), F32), pltpu.VMEM((D_MODEL, D_MODEL), F32)],
        compiler_params=_params(1))(dx2_or_target, t2, t1, p, ple_w, w_pg, b_pg, g1, b1, g2, b2)


def _post_bwd_b(dt1, m, z, hs, y0, w_out, w_glu, b_glu):
    L = dt1.shape[0]

    def body(dt1_ref, m_ref, z_ref, hs_ref, y0_ref, wo_ref, wg_ref, bg_ref,
             dhs_ref, dy0_ref, dzg_ref, dwo_out, dwg_out, dbg_ref, dwo_ref, dwg_ref):
        @pl.when(pl.program_id(0) == 0)
        def _():
            for ref in (dwo_ref, dwg_ref, dbg_ref):
                ref[...] = jnp.zeros_like(ref)

        dt1b = dt1_ref[...].astype(MXU)
        dm = _mm_nt(dt1b, wo_ref[...])
        dwo_ref[...] += _mm_tn(m_ref[...], dt1b)
        d_rgy, d_s5y = dm[:, :RG_W], dm[:, RG_W:]
        rg_gate = z_ref[:, C_RGG:C_RGG + RG_W]
        s5_gate = z_ref[:, C_S5G:C_S5G + S5_W]
        sl, dsl = _silu_and_grad(rg_gate)
        dhs_ref[...] = d_rgy * sl
        dzg_ref[:, :RG_W] = d_rgy * hs_ref[...] * dsl
        y0 = y0_ref[...]
        y1 = _gelu(y0)
        gl = _sigmoid(_mm(y1, wg_ref[...]) + bg_ref[...])
        sl, dsl = _silu_and_grad(s5_gate)
        dy2 = d_s5y * sl
        dzg_ref[:, RG_W:] = d_s5y * (y1 * gl) * dsl
        dglpre = (dy2 * y1) * gl * (1.0 - gl)
        dwg_ref[...] += _mm_tn(y1, dglpre)
        dbg_ref[...] += _colsum(dglpre)
        dy1 = dy2 * gl + _mm_nt(dglpre, wg_ref[...])
        dy0_ref[...] = dy1 * _gelu_grad(y0)

        @pl.when(pl.program_id(0) == L // TM - 1)
        def _():
            dwo_out[...] = dwo_ref[...].astype(WIRE)
            dwg_out[...] = dwg_ref[...].astype(WIRE)

    return pl.pallas_call(
        body, name="post_bwd_b", grid=(L // TM,),
        in_specs=[_row(D_MODEL), _row(D_MODEL), _row(Z_W), _row(RG_W), _row(S5_W), _full((D_MODEL, D_MODEL)),
                  _full((S5_W, S5_W)), _full((1, S5_W))],
        out_specs=[_row(RG_W), _row(S5_W), _row(D_MODEL), _full((D_MODEL, D_MODEL)), _full((S5_W, S5_W)), _full((1, S5_W))],
        out_shape=[_S((L, RG_W)), _S((L, S5_W)), _S((L, D_MODEL)), _S((D_MODEL, D_MODEL), WIRE), _S((S5_W, S5_W), WIRE),
                   _S((1, S5_W))],
        scratch_shapes=[pltpu.VMEM((D_MODEL, D_MODEL), F32), pltpu.VMEM((S5_W, S5_W), F32)],
        compiler_params=_params(1))(dt1, m, z, hs, y0, w_out, w_glu, b_glu)


def _adamw(parts, w, m, v):
    n, R, C = parts.shape
    tr = R
    for cand in (512, 256, 128, 64, 32, 16, 8):
        if R % cand == 0 and n * cand * C * 4 <= 4 * 1024 * 1024:
            tr = cand
            break

    def body(p_ref, w_ref, m_ref, v_ref, g_ref, d_ref, nm_ref, nv_ref):
        g = p_ref[0].astype(F32)
        for k in range(1, n):
            g = g + p_ref[k].astype(F32)
        nm = B1 * m_ref[...] + (1.0 - B1) * g
        nv = B2 * v_ref[...] + (1.0 - B2) * (g * g)
        d_ref[...] = (-LR) * ((nm / BC1) / (jnp.sqrt(nv / BC2) + EPS) + WD * w_ref[...])
        g_ref[...], nm_ref[...], nv_ref[...] = g, nm, nv

    blk = pl.BlockSpec((tr, C), lambda i: (i, 0))
    return pl.pallas_call(
        body, name="adamw", grid=(R // tr,),
        in_specs=[pl.BlockSpec((n, tr, C), lambda i: (0, i, 0)), blk, blk, blk],
        out_specs=[blk] * 4, out_shape=[_S((R, C))] * 4, compiler_params=_params(1))(parts, w, m, v)


def _me():
    return lax.axis_index("x"), lax.axis_index("y"), lax.axis_index("c")


def _lin(dev):
    return 4 * dev[0] + 2 * dev[1] + dev[2]


def _blk(ref, axis, size, idx):
    nd = len(ref.shape)
    start = idx * size
    if axis == nd - 1 and size % LANE == 0:
        start = pl.multiple_of(start, LANE)
    elif axis == nd - 2 and size % 16 == 0:
        start = pl.multiple_of(start, 16)
    ix = [slice(None)] * nd
    ix[axis] = pl.ds(start, size)
    return ref.at[tuple(ix)]


def _all_gather(shards, axes, name):
    n = len(shards)
    sizes = [s.shape[a] for s, a in zip(shards, axes)]
    out_shapes = [_S(s.shape[:a] + (N_DEV * s.shape[a],) + s.shape[a + 1:], s.dtype) for s, a in zip(shards, axes)]

    def body(*refs):
        ins, outs = refs[:n], refs[n:2 * n]
        send_sems, recv_sems, local_sems = refs[2 * n:]
        x, y, c = _me()
        me, sibling = (x, y, c), (x, y, 1 - c)
        chips = [(1 - x, y), (x, 1 - y), (1 - x, 1 - y)]

        def copy(a, k, block, to, from_input=False):
            dst = _blk(outs[a], axes[a], sizes[a], _lin(block))
            return pltpu.make_async_remote_copy(
                src_ref=ins[a] if from_input else dst, dst_ref=dst, send_sem=send_sems.at[a, k],
                recv_sem=recv_sems.at[a, k], device_id=to, device_id_type=MESH)

        mine = [pltpu.make_async_copy(ins[a], _blk(outs[a], axes[a], sizes[a], _lin(me)), local_sems.at[a]) for a in range(n)]
        for cp in mine:
            cp.start()
        first = []
        for a in range(n):
            first.append(copy(a, 0, me, sibling, True))
            first += [copy(a, 1 + j, me, (*chip, c), True) for j, chip in enumerate(chips)]
        for cp in first:
            cp.start()
        passed = []
        for j, chip in enumerate(chips):
            for a in range(n):
                copy(a, 1 + j, (*chip, c), me).wait_recv()
                cp = copy(a, 4 + j, (*chip, c), sibling)
                cp.start()
                passed.append(cp)
        for a in range(n):
            copy(a, 0, sibling, me).wait_recv()
            for j, chip in enumerate(chips):
                copy(a, 4 + j, (*chip, 1 - c), me).wait_recv()
        for cp in first + passed:
            cp.wait_send()
        for cp in mine:
            cp.wait()

    return pl.pallas_call(
        body, name=name, out_shape=out_shapes, in_specs=[ANY] * n, out_specs=[ANY] * n,
        scratch_shapes=[pltpu.SemaphoreType.DMA((n, 7)), pltpu.SemaphoreType.DMA((n, 7)), pltpu.SemaphoreType.DMA((n,))],
    )(*shards)


def _exchange(groups, axes, name):
    arrays = [a for g in groups for a in g]
    where = [(o, i) for o, g in enumerate(groups) for i in range(len(g))]
    ax = [axes[o] for o, _ in where]
    n = len(arrays)
    sizes = [s.shape[a] // N_DEV for s, a in zip(arrays, ax)]
    out_shapes = []
    for g, a in zip(groups, axes):
        s = g[0].shape
        out_shapes.append(_S((N_DEV, len(g)) + s[:a] + (s[a] // N_DEV,) + s[a + 1:], g[0].dtype))

    def body(*refs):
        ins, outs = refs[:n], refs[n:n + len(groups)]
        send_sems, recv_sems, local_sems = refs[n + len(groups):]
        x, y, c = _me()
        me = (x, y, c)
        flip = lambda v, f: 1 - v if f else v
        peers = [(flip(x, k & 4), flip(y, k & 2), flip(c, k & 1)) for k in range(1, N_DEV)]

        def land(a, sender):
            o, i = where[a]
            return outs[o].at[_lin(sender), i]

        def copy(a, k, to):
            return pltpu.make_async_remote_copy(
                src_ref=_blk(ins[a], ax[a], sizes[a], _lin(to)), dst_ref=land(a, me),
                send_sem=send_sems.at[a, k], recv_sem=recv_sems.at[a, k], device_id=to, device_id_type=MESH)

        mine = [pltpu.make_async_copy(_blk(ins[a], ax[a], sizes[a], _lin(me)), land(a, me), local_sems.at[a]) for a in range(n)]
        for cp in mine:
            cp.start()
        sends = [copy(a, k, peer) for a in range(n) for k, peer in enumerate(peers)]
        for cp in sends:
            cp.start()
        for a in range(n):
            for k, peer in enumerate(peers):
                pltpu.make_async_remote_copy(
                    src_ref=land(a, peer), dst_ref=land(a, peer), send_sem=send_sems.at[a, k],
                    recv_sem=recv_sems.at[a, k], device_id=peer, device_id_type=MESH).wait_recv()
        for cp in sends:
            cp.wait_send()
        for cp in mine:
            cp.wait()

    return pl.pallas_call(
        body, name=name, out_shape=out_shapes, in_specs=[ANY] * n, out_specs=[ANY] * len(groups),
        scratch_shapes=[pltpu.SemaphoreType.DMA((n, 7)), pltpu.SemaphoreType.DMA((n, 7)), pltpu.SemaphoreType.DMA((n,))],
    )(*arrays)


def _sum_parts(parts):
    n, R, C = parts.shape

    def body(p_ref, o_ref):
        g = p_ref[0]
        for k in range(1, n):
            g = g + p_ref[k]
        o_ref[...] = g

    return pl.pallas_call(body, name="sum_parts", out_shape=_S((R, C)))(parts)


def _block_diag(w, nb):
    tn, r, c = w.shape
    w = w.reshape(tn // nb, nb, r, c)
    return jnp.einsum('tarc,ab->tarbc', w, jnp.eye(nb, dtype=w.dtype)).reshape(tn // nb, nb * r, nb * c)


def _block_diag_extract(w, nb):
    t, R, C = w.shape
    w = w.reshape(t, nb, R // nb, nb, C // nb)
    return jnp.einsum('tarbc,ab->tarc', w, jnp.eye(nb, dtype=w.dtype)).reshape(t * nb, R // nb, C // nb)


SMALL = ['conv_b', 'rg_wa', 'rg_ba', 'rg_wx', 'rg_bx', 'rg_lambda', 's5_a_re', 's5_a_im', 's5_b_re', 's5_b_im',
         's5_c_re', 's5_c_im', 's5_d', 's5_log_step', 's5_b_glu', 'ln1_g', 'ln1_b', 'ple_gate_b', 'ln2_g', 'ln2_b']
WEIGHTS = ['w_in', 'conv_w', 'conv_b', 'rg_wa', 'rg_ba', 'rg_wx', 'rg_bx', 'rg_lambda', 's5_a_re', 's5_a_im', 's5_b_re',
           's5_b_im', 's5_c_re', 's5_c_im', 's5_d', 's5_log_step', 's5_w_glu', 's5_b_glu', 'w_out', 'ln1_g', 'ln1_b',
           'ple_w', 'ple_gate_w', 'ple_gate_b', 'ln2_g', 'ln2_b']
PACK_ROWS_MULT = 64


def _pack(tree):
    flat = jnp.concatenate([tree[k].reshape(-1) for k in SMALL])
    rows = -(-flat.shape[0] // (LANE * PACK_ROWS_MULT)) * PACK_ROWS_MULT
    return jnp.pad(flat, (0, rows * LANE - flat.shape[0])).reshape(rows, LANE)


def _unpack(packed, like):
    flat, out, o = packed.reshape(-1), {}, 0
    for k in SMALL:
        n = math.prod(like[k].shape)
        out[k] = flat[o:o + n].reshape(like[k].shape)
        o += n
    return out


def _local_grads(x, p, target, W, disc):
    depth = 2
    saved = []
    for i in range(depth):
        w = W[i]
        z = _inproj_fwd(x, w['w_in'])
        hs = _rg_fwd(z, w['conv_w'], w['conv_b'], w['wa_bd'], w['wx_bd'], w['rg_ba'], w['rg_bx'], w['rg_lambda'])
        d = disc[i]
        y0, s_re, s_im = _s5_fwd(z, d['bb_re'], d['bb_im'], d['lb_re'], d['lb_im'], d['c_re'], d['c_im'], w['s5_d'])
        x2, t1, t2, m = _post_fwd(x, hs, z, y0, p[i], w['s5_w_glu'], w['s5_b_glu'], w['w_out'], w['ln1_g'], w['ln1_b'],
                                  w['ple_w'], w['ple_gate_w'], w['ple_gate_b'], w['ln2_g'], w['ln2_b'])
        saved.append((x, z, hs, y0, s_re, s_im, t1, t2, m))
        x = x2

    grads = [None] * depth
    dx = target
    loss = None
    for i in reversed(range(depth)):
        w, d = W[i], disc[i]
        xin, z, hs, y0, s_re, s_im, t1, t2, m = saved[i]
        g = {}
        (dt1, g['ple_w'], g['ple_gate_w'], g['ple_gate_b'], g['ln1_g'], g['ln1_b'], g['ln2_g'], g['ln2_b'], lrow) = _post_bwd_a(
            dx, i == depth - 1, t2, t1, p[i], w['ple_w'], w['ple_gate_w'], w['ple_gate_b'], w['ln1_g'], w['ln1_b'],
            w['ln2_g'], w['ln2_b'])
        if i == depth - 1:
            loss = 0.5 / D_MODEL * jnp.sum(lrow)
        dhs, dy0, dzg, g['w_out'], g['s5_w_glu'], g['s5_b_glu'] = _post_bwd_b(dt1, m, z, hs, y0, w['w_out'], w['s5_w_glu'],
                                                                           w['s5_b_glu'])
        (dzu, g['bb_re'], g['bb_im'], g['lb_re'], g['lb_im'], g['c_re'], g['c_im'], g['s5_d']) = _s5_bwd(
            dy0, z, s_re, s_im, d['bb_re'], d['bb_im'], d['lb_re'], d['lb_im'], d['c_re'], d['c_im'], w['s5_d'])
        (dzx, g['conv_w'], g['conv_b'], g['wa_bd'], g['wx_bd'], g['rg_ba'], g['rg_bx'], g['rg_lambda']) = _rg_bwd(
            dhs, z, hs, w['conv_w'], w['conv_b'], w['wa_bd'], w['wx_bd'], w['rg_ba'], w['rg_bx'], w['rg_lambda'])
        dx, g['w_in'] = _inproj_bwd(dt1, xin, dzx, dzg, dzu, w['w_in'])
        grads[i] = g
    return loss, dx, grads


def _s5_layouts_fwd(s5_a_re, s5_a_im, s5_log_step, s5_b_re, s5_b_im, s5_c_re, s5_c_im):
    depth = s5_a_re.shape[0]
    ar, ai = s5_a_re.reshape(depth * 24, S5_P), s5_a_im.reshape(depth * 24, S5_P)
    ls = s5_log_step.reshape(depth * 24, 1)
    lr, li, cr, ci = _s5_disc_fwd(ar, ai, ls)
    col = lambda a: a.reshape(depth * S5_N, 1)
    br, bi = s5_b_re.reshape(depth * S5_N, 16), s5_b_im.reshape(depth * S5_N, 16)
    bbr, bbi = _s5_bscale_fwd(col(cr), col(ci), br, bi)
    disc = []
    for i in range(depth):
        gph = lambda a: a.reshape(depth, 24, S5_P, 16)[i]
        disc.append(dict(
            bb_re=_block_diag(jnp.swapaxes(gph(bbr), 1, 2), 8), bb_im=_block_diag(jnp.swapaxes(gph(bbi), 1, 2), 8),
            lb_re=lr.reshape(depth, 1, S5_N)[i], lb_im=li.reshape(depth, 1, S5_N)[i],
            c_re=_block_diag(jnp.swapaxes(s5_c_re[i], 1, 2), 8), c_im=_block_diag(jnp.swapaxes(s5_c_im[i], 1, 2), 8)))
    return disc, (ar, ai, ls, col(cr), col(ci), br, bi)


def _s5_layouts_bwd(grads, res):
    ar, ai, ls, cr, ci, br, bi = res
    depth = len(grads)
    stack = lambda f: jnp.stack([f(g) for g in grads])
    dbbr = stack(lambda g: jnp.swapaxes(_block_diag_extract(g['bb_re'], 8), 1, 2)).reshape(depth * S5_N, 16)
    dbbi = stack(lambda g: jnp.swapaxes(_block_diag_extract(g['bb_im'], 8), 1, 2)).reshape(depth * S5_N, 16)
    dbr, dbi, dcr, dci = _s5_bscale_bwd(cr, ci, br, bi, dbbr, dbbi)
    gp = lambda a: a.reshape(depth * 24, S5_P)
    dar, dai, dls = _s5_disc_bwd(ar, ai, ls, gp(stack(lambda g: g['lb_re'])), gp(stack(lambda g: g['lb_im'])), gp(dcr), gp(dci))
    return dict(
        s5_a_re=dar.reshape(depth, 24, S5_P), s5_a_im=dai.reshape(depth, 24, S5_P), s5_log_step=dls.reshape(depth, 24),
        s5_b_re=dbr.reshape(depth, 24, S5_P, 16), s5_b_im=dbi.reshape(depth, 24, S5_P, 16),
        s5_c_re=stack(lambda g: jnp.swapaxes(_block_diag_extract(g['c_re'], 8), 1, 2)),
        s5_c_im=stack(lambda g: jnp.swapaxes(_block_diag_extract(g['c_im'], 8), 1, 2)))


def _layer_weights(full, i):
    row = lambda a: a[i].reshape(1, -1)
    return dict(
        w_in=full['w_in'][i], w_out=full['w_out'][i], ple_w=full['ple_w'][i], ple_gate_w=full['ple_gate_w'][i],
        s5_w_glu=full['s5_w_glu'][i], conv_w=full['conv_w'][i], conv_b=row(full['conv_b']),
        wa_bd=_block_diag(full['rg_wa'][i], 2), wx_bd=_block_diag(full['rg_wx'][i], 2),
        rg_ba=row(full['rg_ba']), rg_bx=row(full['rg_bx']), rg_lambda=row(full['rg_lambda']),
        s5_d=row(full['s5_d']), s5_b_glu=row(full['s5_b_glu']), ln1_g=row(full['ln1_g']), ln1_b=row(full['ln1_b']),
        ple_gate_b=row(full['ple_gate_b']), ln2_g=row(full['ln2_g']), ln2_b=row(full['ln2_b']))


def _full_grads(full, x, p, target):
    disc, res = _s5_layouts_fwd(full['s5_a_re'], full['s5_a_im'], full['s5_log_step'], full['s5_b_re'], full['s5_b_im'],
                                full['s5_c_re'], full['s5_c_im'])
    W = [_layer_weights(full, i) for i in range(2)]
    loss, gx, grads = _local_grads(x, p, target, W, disc)
    stack = lambda f: jnp.stack([f(g) for g in grads])
    out = _s5_layouts_bwd(grads, res)
    for k in SHARD_AXIS:
        out[k] = [g[k] for g in grads]
    out['conv_w'] = stack(lambda g: g['conv_w'])
    for k in ('conv_b', 'rg_ba', 'rg_bx', 'rg_lambda', 's5_b_glu', 'ln1_g', 'ln1_b', 'ple_gate_b', 'ln2_g', 'ln2_b'):
        out[k] = stack(lambda g: g[k][0])
    out['s5_d'] = stack(lambda g: g['s5_d'][0]).reshape(2, 24, 16)
    out['rg_wa'] = stack(lambda g: _block_diag_extract(g['wa_bd'], 2))
    out['rg_wx'] = stack(lambda g: _block_diag_extract(g['wx_bd'], 2))
    return loss, gx, out


SHARD_AXIS = {'w_in': 2, 'w_out': 1, 'ple_w': 2, 'ple_gate_w': 1, 's5_w_glu': 1}


def kernel(x, p, w_in, conv_w, conv_b, rg_wa, rg_ba, rg_wx, rg_bx, rg_lambda, s5_a_re, s5_a_im, s5_b_re, s5_b_im, s5_c_re, s5_c_im, s5_d, s5_log_step, s5_w_glu, s5_b_glu, w_out, ln1_g, ln1_b, ple_w, ple_gate_w, ple_gate_b, ln2_g, ln2_b, loss_target, m_w_in, m_conv_w, m_conv_b, m_rg_wa, m_rg_ba, m_rg_wx, m_rg_bx, m_rg_lambda, m_s5_a_re, m_s5_a_im, m_s5_b_re, m_s5_b_im, m_s5_c_re, m_s5_c_im, m_s5_d, m_s5_log_step, m_s5_w_glu, m_s5_b_glu, m_w_out, m_ln1_g, m_ln1_b, m_ple_w, m_ple_gate_w, m_ple_gate_b, m_ln2_g, m_ln2_b, v_w_in, v_conv_w, v_conv_b, v_rg_wa, v_rg_ba, v_rg_wx, v_rg_bx, v_rg_lambda, v_s5_a_re, v_s5_a_im, v_s5_b_re, v_s5_b_im, v_s5_c_re, v_s5_c_im, v_s5_d, v_s5_log_step, v_s5_w_glu, v_s5_b_glu, v_w_out, v_ln1_g, v_ln1_b, v_ple_w, v_ple_gate_w, v_ple_gate_b, v_ln2_g, v_ln2_b):
    local = dict(locals())
    w = {k: local[k] for k in WEIGHTS}
    mom = {k: local['m_' + k] for k in WEIGHTS}
    var = {k: local['v_' + k] for k in WEIGHTS}

    big = list(SHARD_AXIS)
    gathered = _all_gather([w[k].astype(WIRE) for k in big] + [conv_w[None]], [SHARD_AXIS[k] for k in big] + [0], "gather_weights")
    full = dict(w)
    full.update(dict(zip(big, gathered[:-1])))
    full['conv_w'] = jnp.moveaxis(gathered[-1], 0, 2).reshape(2, 4, RG_W)

    loss, grad_x, g = _full_grads(full, x[0], p[:, 0], loss_target[0])
    loss = lax.psum(loss, ("x", "y", "c"))

    conv_blocks = jnp.moveaxis(g['conv_w'].reshape(2, 4, N_DEV, RG_W // N_DEV), 2, 0).reshape(N_DEV, 8, RG_W // N_DEV)
    packed = _pack(g)
    recv = _exchange([g[k] for k in big] + [[conv_blocks], [packed]], [SHARD_AXIS[k] - 1 for k in big] + [0, 0],
                     "exchange_grads")
    outs = {}
    for k, r in zip(big + ['conv_w'], recv[:-1]):
        shard = w[k].shape
        c = shard[-1]
        two = lambda a: a.reshape(-1, c)
        outs[k] = [o.reshape(shard) for o in _adamw(r.reshape(N_DEV, -1, c), two(w[k]), two(mom[k]), two(var[k]))]

    rows = packed.shape[0] // N_DEV
    mine = _sum_parts(recv[-1].reshape(N_DEV, rows, LANE))
    summed = _all_gather([mine], [0], "gather_small_grads")[0]
    small = _adamw(summed[None], _pack(w), _pack(mom), _pack(var))
    small = [_unpack(o, w) for o in small]
    for k in SMALL:
        outs[k] = [o[k] for o in small]

    res = [loss, grad_x[None]]
    for j in range(4):
        res += [outs[k][j] for k in WEIGHTS]
    return tuple(res)
```

```python
import functools
import math

import jax
import jax.numpy as jnp
from jax import lax
from jax.experimental import pallas as pl
from jax.experimental.pallas import tpu as pltpu

F32 = jnp.float32
MXU = jnp.bfloat16
WIRE = jnp.bfloat16

N_DEV = 8
D_MODEL = 1024
RG_W = 640
S5_W = 384
S5_P = 64
S5_N = 24 * S5_P
Z_W = 2 * RG_W + 2 * S5_W
C_RGG = RG_W
C_S5U = 2 * RG_W
C_S5G = 2 * RG_W + S5_W
LANE = 128
N_RG_T = RG_W // LANE
N_S5_T = S5_W // LANE
N_S5_J = 4
ALPHA = (2.0 * 2) ** 0.25
LN_EPS = 1e-5
RG_C = 8.0
LR, B1, B2, EPS, WD, STEP = 0.001, 0.9, 0.999, 1e-08, 0.01, 10
BC1 = 1.0 - B1 ** STEP
BC2 = 1.0 - B2 ** STEP
RC = 256
TM = 256
VMEM_LIMIT = 56 * 1024 * 1024

MESH = pl.DeviceIdType.MESH
ANY = pl.BlockSpec(memory_space=pl.ANY)


def _params(n_grid_axes, vmem=VMEM_LIMIT):
    return pltpu.CompilerParams(dimension_semantics=("arbitrary",) * n_grid_axes, vmem_limit_bytes=vmem)


def _S(shape, dtype=F32):
    return jax.ShapeDtypeStruct(tuple(shape), dtype)


def _sigmoid(x):
    return 1.0 / (1.0 + jnp.exp(-x))


def _silu_and_grad(x):
    s = _sigmoid(x)
    return x * s, s * (1.0 + x * (1.0 - s))


_GELU_C = math.sqrt(2.0 / math.pi)


def _gelu(x):
    return 0.5 * x * (1.0 + jnp.tanh(_GELU_C * (x + 0.044715 * (x * x * x))))


def _gelu_grad(x):
    th = jnp.tanh(_GELU_C * (x + 0.044715 * (x * x * x)))
    return 0.5 * (1.0 + th) + 0.5 * x * (1.0 - th * th) * (_GELU_C * (1.0 + 3.0 * 0.044715 * (x * x)))


def _mm(a, b):
    return jnp.dot(a.astype(MXU), b.astype(MXU), preferred_element_type=F32)


def _mm_nt(a, b):
    return lax.dot_general(a.astype(MXU), b.astype(MXU), (((1,), (1,)), ((), ())), preferred_element_type=F32)


def _mm_tn(a, b):
    return lax.dot_general(a.astype(MXU), b.astype(MXU), (((0,), (0,)), ((), ())), preferred_element_type=F32)


def _ln_fwd(t, g, b):
    mu = jnp.mean(t, axis=-1, keepdims=True)
    tc = t - mu
    var = jnp.mean(tc * tc, axis=-1, keepdims=True)
    rstd = lax.rsqrt(var + LN_EPS)
    xhat = tc * rstd
    return xhat * g + b, xhat, rstd


def _ln_bwd(dy, xhat, rstd, g):
    dxh = dy * g
    m1 = jnp.mean(dxh, axis=-1, keepdims=True)
    m2 = jnp.mean(dxh * xhat, axis=-1, keepdims=True)
    return rstd * (dxh - m1 - xhat * m2)


def _colsum(a):
    return jnp.sum(a, axis=0, keepdims=True)


def _up(x, d, rows, fill):
    n = x.shape[0]
    return jnp.where(rows < n - d, pltpu.roll(x, n - d, 0), fill)


SUB = 8
TILE_STEPS = (1, 2, 4)


def _r8(width):
    return lax.broadcasted_iota(jnp.int32, (SUB, width), 0)


def _scan_real(a, u, carry, reverse=False):
    r8 = _r8(a.shape[1])
    n = a.shape[0] // SUB
    outs = [None] * n
    for k in (reversed(range(n)) if reverse else range(n)):
        A, U = a[SUB * k:SUB * k + SUB], u[SUB * k:SUB * k + SUB]
        for d in TILE_STEPS:
            m = (r8 < SUB - d) if reverse else (r8 >= d)
            sh = SUB - d if reverse else d
            U = A * jnp.where(m, pltpu.roll(U, sh, 0), 0.0) + U
            A = A * jnp.where(m, pltpu.roll(A, sh, 0), 1.0)
        h = A * carry + U
        outs[k] = h
        carry = h[0:1] if reverse else h[SUB - 1:SUB]
    return jnp.concatenate(outs, axis=0), carry


def _tile_powers(lr, li, reverse=False):
    width = lr.shape[1]
    r8 = _r8(width)
    steps = []
    pr, pi = lr, li
    er, ei = jnp.broadcast_to(lr, (SUB, width)), jnp.broadcast_to(li, (SUB, width))
    for d in TILE_STEPS:
        m = (r8 < SUB - d) if reverse else (r8 >= d)
        sh = SUB - d if reverse else d
        steps.append((sh, jnp.where(m, pr, 0.0), jnp.where(m, pi, 0.0)))
        er, ei = _cmul(er, ei, jnp.where(m, pltpu.roll(er, sh, 0), 1.0), jnp.where(m, pltpu.roll(ei, sh, 0), 0.0))
        pr, pi = _cmul(pr, pi, pr, pi)
    return steps, (er, ei)


def _scan_lti(xr, xi, carry, steps, e, reverse=False):
    er, ei = e
    kr, ki = carry
    n = xr.shape[0] // SUB
    outr, outi = [None] * n, [None] * n
    for k in (reversed(range(n)) if reverse else range(n)):
        sr, si = xr[SUB * k:SUB * k + SUB], xi[SUB * k:SUB * k + SUB]
        for sh, pr, pi in steps:
            shr, shi = pltpu.roll(sr, sh, 0), pltpu.roll(si, sh, 0)
            sr, si = sr + (pr * shr - pi * shi), si + (pr * shi + pi * shr)
        sr = sr + (er * kr - ei * ki)
        si = si + (er * ki + ei * kr)
        outr[k], outi[k] = sr, si
        kr, ki = (sr[0:1], si[0:1]) if reverse else (sr[SUB - 1:SUB], si[SUB - 1:SUB])
    return jnp.concatenate(outr, axis=0), jnp.concatenate(outi, axis=0), (kr, ki)


def _halo(ref, c, r0):
    rp = pl.multiple_of(jnp.maximum(r0 - 8, 0), 8)
    return jnp.where(c > 0, ref[pl.ds(rp, 8), :], 0.0)


def _conv_taps(xe):
    return [pltpu.roll(xe, 3, 0)[8:, :], pltpu.roll(xe, 2, 0)[8:, :], pltpu.roll(xe, 1, 0)[8:, :], xe[8:, :]]


def _rg_gates(h, wa, wx, ba, bx, sp):
    r = _sigmoid(_mm(h, wa) + ba)
    i = _sigmoid(_mm(h, wx) + bx)
    log_a = (-RG_C) * r * sp
    a = jnp.exp(log_a)
    mult = jnp.sqrt(-jnp.tanh(log_a) * (a * a + 1.0))
    return r, i, a, mult


def _softplus(y):
    return jnp.maximum(y, 0.0) + jnp.log1p(jnp.exp(-jnp.abs(y)))


def _inproj_fwd(x, w_in):
    L = x.shape[0]

    def body(x_ref, w_ref, z_ref):
        z_ref[...] = _mm(x_ref[...], w_ref[...])

    return pl.pallas_call(
        body, name="inproj_fwd", grid=(L // TM,),
        in_specs=[pl.BlockSpec((TM, D_MODEL), lambda i: (i, 0)), pl.BlockSpec((D_MODEL, Z_W), lambda i: (0, 0))],
        out_specs=pl.BlockSpec((TM, Z_W), lambda i: (i, 0)),
        out_shape=_S((L, Z_W)), compiler_params=_params(1))(x, w_in)


def _inproj_bwd(dt1, x, dzx, dzg, dzu, w_in):
    L = x.shape[0]

    def body(dt1_ref, x_ref, dzx_ref, dzg_ref, dzu_ref, w_ref, dx_ref, dw_ref, acc_ref):
        @pl.when(pl.program_id(0) == 0)
        def _():
            acc_ref[...] = jnp.zeros_like(acc_ref)
        dzg = dzg_ref[...]
        dz = jnp.concatenate([dzx_ref[...], dzg[:, :RG_W], dzu_ref[...], dzg[:, RG_W:]], axis=1).astype(MXU)
        dx_ref[...] = ALPHA * dt1_ref[...] + _mm_nt(dz, w_ref[...])
        acc_ref[...] += _mm_tn(x_ref[...], dz)

        @pl.when(pl.program_id(0) == L // TM - 1)
        def _():
            dw_ref[...] = acc_ref[...].astype(WIRE)

    row = lambda w: pl.BlockSpec((TM, w), lambda i: (i, 0))
    return pl.pallas_call(
        body, name="inproj_bwd", grid=(L // TM,),
        in_specs=[row(D_MODEL), row(D_MODEL), row(RG_W), row(D_MODEL), row(S5_W),
                  pl.BlockSpec((D_MODEL, Z_W), lambda i: (0, 0))],
        out_specs=[row(D_MODEL), pl.BlockSpec((D_MODEL, Z_W), lambda i: (0, 0))],
        out_shape=[_S((L, D_MODEL)), _S((D_MODEL, Z_W), WIRE)], scratch_shapes=[pltpu.VMEM((D_MODEL, Z_W), F32)],
        compiler_params=_params(1))(dt1, x, dzx, dzg, dzu, w_in)


def _rg_specs(L):
    tile = lambda rows: pl.BlockSpec((rows, LANE), lambda c: (0, c))
    return tile, pl.BlockSpec((None, LANE, LANE), lambda c: (c, 0, 0))


def _rg_fwd(z, cw, cb, wa_bd, wx_bd, ba, bx, lam):
    L = z.shape[0]

    def body(x_ref, cw_ref, cb_ref, wa_ref, wx_ref, ba_ref, bx_ref, lam_ref, hs_ref):
        w, b = cw_ref[...], cb_ref[...]
        wa, wx, ba_, bx_ = wa_ref[...].astype(MXU), wx_ref[...].astype(MXU), ba_ref[...], bx_ref[...]
        sp = _softplus(-lam_ref[...])

        def step(c, carry):
            r0 = pl.multiple_of(c * RC, RC)
            xe = jnp.concatenate([_halo(x_ref, c, r0), x_ref[pl.ds(r0, RC), :]], axis=0)
            t = _conv_taps(xe)
            h = t[0] * w[0:1] + t[1] * w[1:2] + t[2] * w[2:3] + t[3] * w[3:4] + b
            _, i, a, mult = _rg_gates(h, wa, wx, ba_, bx_, sp)
            hs, carry = _scan_real(a, mult * (i * h), carry)
            hs_ref[pl.ds(r0, RC), :] = hs
            return carry

        lax.fori_loop(0, L // RC, step, jnp.zeros((1, LANE), F32))

    tile, bd = _rg_specs(L)
    return pl.pallas_call(
        body, name="rg_fwd", grid=(N_RG_T,),
        in_specs=[tile(L), tile(4), tile(1), bd, bd, tile(1), tile(1), tile(1)],
        out_specs=tile(L), out_shape=_S((L, RG_W)), compiler_params=_params(1))(z, cw, cb, wa_bd, wx_bd, ba, bx, lam)


def _rg_bwd(dhs, z, hs, cw, cb, wa_bd, wx_bd, ba, bx, lam):
    L = z.shape[0]

    def body(g_ref, x_ref, hs_ref, cw_ref, cb_ref, wa_ref, wx_ref, ba_ref, bx_ref, lam_ref,
             dx_ref, dcw_ref, dcb_ref, dwa_ref, dwx_ref, dba_ref, dbx_ref, dlam_ref):
        w, b = cw_ref[...], cb_ref[...]
        wa, wx, ba_, bx_ = wa_ref[...].astype(MXU), wx_ref[...].astype(MXU), ba_ref[...], bx_ref[...]
        lam = lam_ref[...]
        sp = _softplus(-lam)
        rows = lax.broadcasted_iota(jnp.int32, (RC, LANE), 0)
        for ref in (dcw_ref, dcb_ref, dwa_ref, dwx_ref, dba_ref, dbx_ref, dlam_ref):
            ref[...] = jnp.zeros_like(ref)
        nch = L // RC

        def step(k, carry):
            cin, nxt = carry
            c = nch - 1 - k
            r0 = pl.multiple_of(c * RC, RC)
            xe = jnp.concatenate([_halo(x_ref, c, r0), x_ref[pl.ds(r0, RC), :]], axis=0)
            t = _conv_taps(xe)
            h = t[0] * w[0:1] + t[1] * w[1:2] + t[2] * w[2:3] + t[3] * w[3:4] + b
            r, i, a, mult = _rg_gates(h, wa, wx, ba_, bx_, sp)
            hs_e = jnp.concatenate([_halo(hs_ref, c, r0), hs_ref[pl.ds(r0, RC), :]], axis=0)
            hs_prev = pltpu.roll(hs_e, 1, 0)[8:, :]
            g = g_ref[pl.ds(r0, RC), :]
            cc, cin_new = _scan_real(a, a * g, cin, reverse=True)
            dh = g + _up(cc, 1, rows, cin)
            ih = i * h
            dlog_a = dh * hs_prev * a - (dh * ih) * (a * a) / mult
            di = dh * mult * h
            dhin = dh * mult * i
            dr = dlog_a * ((-RG_C) * sp)
            dlam_ref[...] += _colsum(dlog_a * r)
            dra = dr * r * (1.0 - r)
            dia = di * i * (1.0 - i)
            dwa_ref[...] += _mm_tn(h, dra)
            dwx_ref[...] += _mm_tn(h, dia)
            dba_ref[...] += _colsum(dra)
            dbx_ref[...] += _colsum(dia)
            dhin = dhin + _mm_nt(dra, wa) + _mm_nt(dia, wx)
            de = jnp.concatenate([dhin, nxt], axis=0)
            n = RC + 8
            dx = (dhin * w[3:4] + pltpu.roll(de, n - 1, 0)[:RC, :] * w[2:3]
                  + pltpu.roll(de, n - 2, 0)[:RC, :] * w[1:2] + pltpu.roll(de, n - 3, 0)[:RC, :] * w[0:1])
            dx_ref[pl.ds(r0, RC), :] = dx
            for kk in range(4):
                dcw_ref[kk:kk + 1, :] += _colsum(dhin * t[kk])
            dcb_ref[...] += _colsum(dhin)
            return cin_new, dhin[0:8, :]

        lax.fori_loop(0, nch, step, (jnp.zeros((1, LANE), F32), jnp.zeros((8, LANE), F32)))
        dlam_ref[...] = dlam_ref[...] * (RG_C * _sigmoid(-lam))

    tile, bd = _rg_specs(L)
    return pl.pallas_call(
        body, name="rg_bwd", grid=(N_RG_T,),
        in_specs=[tile(L), tile(L), tile(L), tile(4), tile(1), bd, bd, tile(1), tile(1), tile(1)],
        out_specs=[tile(L), tile(4), tile(1), bd, bd, tile(1), tile(1), tile(1)],
        out_shape=[_S((L, RG_W)), _S((4, RG_W)), _S((1, RG_W)), _S((N_RG_T, LANE, LANE)), _S((N_RG_T, LANE, LANE)),
                   _S((1, RG_W)), _S((1, RG_W)), _S((1, RG_W))],
        compiler_params=_params(1))(dhs, z, hs, cw, cb, wa_bd, wx_bd, ba, bx, lam)


def _cmul(ar, ai, br, bi):
    return ar * br - ai * bi, ar * bi + ai * br


S5_TW = S5_N // N_S5_T


def _s5_specs(L):
    in_tile = pl.BlockSpec((L, LANE), lambda t: (0, t))
    st = pl.BlockSpec((L, S5_TW), lambda t: (0, t))
    bb = pl.BlockSpec((None, LANE, S5_TW), lambda t: (t, 0, 0))
    cc = pl.BlockSpec((None, S5_TW, LANE), lambda t: (t, 0, 0))
    lb = pl.BlockSpec((1, S5_TW), lambda t: (0, t))
    dv = pl.BlockSpec((1, LANE), lambda t: (0, t))
    return in_tile, st, bb, cc, lb, dv


def _s5_fwd(z, bb_re, bb_im, lb_re, lb_im, c_re, c_im, dvec):
    L = z.shape[0]

    def body(u_ref, bbr_ref, bbi_ref, lr_ref, li_ref, cr_ref, ci_ref, d_ref, y_ref, sr_ref, si_ref):
        bbr, bbi = bbr_ref[...].astype(MXU), bbi_ref[...].astype(MXU)
        cr, ci = cr_ref[...].astype(MXU), ci_ref[...].astype(MXU)
        dv = d_ref[...]
        steps, e = _tile_powers(lr_ref[...], li_ref[...])

        def step(c, carry):
            r0 = pl.multiple_of(c * RC, RC)
            u = u_ref[pl.ds(r0, RC), :]
            ub = u.astype(MXU)
            sr = jnp.dot(ub, bbr, preferred_element_type=F32)
            si = jnp.dot(ub, bbi, preferred_element_type=F32)
            sr, si, carry = _scan_lti(sr, si, carry, steps, e)
            sr_ref[pl.ds(r0, RC), :] = sr
            si_ref[pl.ds(r0, RC), :] = si
            y_ref[pl.ds(r0, RC), :] = dv * u + (_mm(sr, cr) - _mm(si, ci))
            return carry

        zero = jnp.zeros((1, S5_TW), F32)
        lax.fori_loop(0, L // RC, step, (zero, zero))

    in_tile, st, bb, cc, lb, dv = _s5_specs(L)
    u_tile = pl.BlockSpec((L, LANE), lambda t: (0, C_S5U // LANE + t))
    return pl.pallas_call(
        body, name="s5_fwd", grid=(N_S5_T,),
        in_specs=[u_tile, bb, bb, lb, lb, cc, cc, dv],
        out_specs=[in_tile, st, st],
        out_shape=[_S((L, S5_W)), _S((L, S5_N)), _S((L, S5_N))],
        compiler_params=_params(1))(z, bb_re, bb_im, lb_re, lb_im, c_re, c_im, dvec)


def _s5_bwd(dy0, z, s_re, s_im, bb_re, bb_im, lb_re, lb_im, c_re, c_im, dvec):
    L = z.shape[0]

    def body(dy_ref, u_ref, sr_ref, si_ref, bbr_ref, bbi_ref, lr_ref, li_ref, cr_ref, ci_ref, d_ref,
             du_ref, dbbr_ref, dbbi_ref, dlr_ref, dli_ref, dcr_ref, dci_ref, dd_ref):
        bbr, bbi = bbr_ref[...].astype(MXU), bbi_ref[...].astype(MXU)
        cr, ci = cr_ref[...].astype(MXU), ci_ref[...].astype(MXU)
        lr, li = lr_ref[...], -li_ref[...]
        dv = d_ref[...]
        steps, e = _tile_powers(lr, li, reverse=True)
        for ref in (dbbr_ref, dbbi_ref, dlr_ref, dli_ref, dcr_ref, dci_ref, dd_ref):
            ref[...] = jnp.zeros_like(ref)
        nch = L // RC

        def step(k, carry):
            c = nch - 1 - k
            r0 = pl.multiple_of(c * RC, RC)
            dy = dy_ref[pl.ds(r0, RC), :]
            u = u_ref[pl.ds(r0, RC), :]
            dyb, ub = dy.astype(MXU), u.astype(MXU)
            sr, si = sr_ref[pl.ds(r0, RC), :], si_ref[pl.ds(r0, RC), :]
            dcr_ref[...] += _mm_tn(sr, dyb)
            dci_ref[...] -= _mm_tn(si, dyb)
            gr = _mm_nt(dyb, cr)
            gi = -_mm_nt(dyb, ci)
            gr, gi, carry = _scan_lti(gr, gi, carry, steps, e, reverse=True)
            pr_ = pltpu.roll(jnp.concatenate([_halo(sr_ref, c, r0), sr], axis=0), 1, 0)[8:, :]
            pi_ = pltpu.roll(jnp.concatenate([_halo(si_ref, c, r0), si], axis=0), 1, 0)[8:, :]
            dlr_ref[...] += _colsum(pr_ * gr + pi_ * gi)
            dli_ref[...] += _colsum(pr_ * gi - pi_ * gr)
            grb, gib = gr.astype(MXU), gi.astype(MXU)
            dbbr_ref[...] += _mm_tn(ub, grb)
            dbbi_ref[...] += _mm_tn(ub, gib)
            du_ref[pl.ds(r0, RC), :] = dv * dy + (_mm_nt(grb, bbr) + _mm_nt(gib, bbi))
            dd_ref[...] += _colsum(dy * u)
            return carry

        zero = jnp.zeros((1, S5_TW), F32)
        lax.fori_loop(0, nch, step, (zero, zero))

    in_tile, st, bb, cc, lb, dv = _s5_specs(L)
    u_tile = pl.BlockSpec((L, LANE), lambda t: (0, C_S5U // LANE + t))
    return pl.pallas_call(
        body, name="s5_bwd", grid=(N_S5_T,),
        in_specs=[in_tile, u_tile, st, st, bb, bb, lb, lb, cc, cc, dv],
        out_specs=[in_tile, bb, bb, lb, lb, cc, cc, dv],
        out_shape=[_S((L, S5_W)), _S((N_S5_T, LANE, S5_TW)), _S((N_S5_T, LANE, S5_TW)), _S((1, S5_N)), _S((1, S5_N)),
                   _S((N_S5_T, S5_TW, LANE)), _S((N_S5_T, S5_TW, LANE)), _S((1, S5_W))],
        compiler_params=_params(1))(dy0, z, s_re, s_im, bb_re, bb_im, lb_re, lb_im, c_re, c_im, dvec)


def _disc(ar, ai, ls):
    dt = jnp.exp(ls)
    mag = jnp.exp(ar * dt)
    lr = mag * jnp.cos(ai * dt)
    li = mag * jnp.sin(ai * dt)
    den = ar * ar + ai * ai
    cr = ((lr - 1.0) * ar + li * ai) / den
    ci = (li * ar - (lr - 1.0) * ai) / den
    return lr, li, cr, ci


def _s5_disc_fwd(ar, ai, ls):
    def body(ar_ref, ai_ref, ls_ref, lr_ref, li_ref, cr_ref, ci_ref):
        lr, li, cr, ci = _disc(ar_ref[...], ai_ref[...], ls_ref[...])
        lr_ref[...], li_ref[...], cr_ref[...], ci_ref[...] = lr, li, cr, ci

    sh = _S(ar.shape)
    return pl.pallas_call(body, name="s5_disc_fwd", out_shape=[sh, sh, sh, sh])(ar, ai, ls)


def _s5_disc_bwd(ar, ai, ls, dlr, dli, dcr, dci):
    def body(ar_ref, ai_ref, ls_ref, dlr_ref, dli_ref, dcr_ref, dci_ref, dar_ref, dai_ref, dls_ref):
        _, vjp = jax.vjp(_disc, ar_ref[...], ai_ref[...], jnp.broadcast_to(ls_ref[...], ar_ref.shape))
        dar, dai, dls = vjp((dlr_ref[...], dli_ref[...], dcr_ref[...], dci_ref[...]))
        dar_ref[...], dai_ref[...] = dar, dai
        dls_ref[...] = jnp.sum(dls, axis=1, keepdims=True)

    return pl.pallas_call(body, name="s5_disc_bwd", out_shape=[_S(ar.shape), _S(ar.shape), _S(ls.shape)])(
        ar, ai, ls, dlr, dli, dcr, dci)


def _s5_bscale_fwd(cr, ci, br, bi):
    def body(cr_ref, ci_ref, br_ref, bi_ref, or_ref, oi_ref):
        or_ref[...], oi_ref[...] = _cmul(cr_ref[...], ci_ref[...], br_ref[...], bi_ref[...])

    return pl.pallas_call(body, name="s5_bscale_fwd", out_shape=[_S(br.shape), _S(br.shape)])(cr, ci, br, bi)


def _s5_bscale_bwd(cr, ci, br, bi, gr, gi):
    def body(cr_ref, ci_ref, br_ref, bi_ref, gr_ref, gi_ref, dbr_ref, dbi_ref, dcr_ref, dci_ref):
        cr_, ci_, br_, bi_, gr_, gi_ = (r[...] for r in (cr_ref, ci_ref, br_ref, bi_ref, gr_ref, gi_ref))
        dbr_ref[...] = cr_ * gr_ + ci_ * gi_
        dbi_ref[...] = cr_ * gi_ - ci_ * gr_
        dcr_ref[...] = jnp.sum(gr_ * br_ + gi_ * bi_, axis=1, keepdims=True)
        dci_ref[...] = jnp.sum(gi_ * br_ - gr_ * bi_, axis=1, keepdims=True)

    return pl.pallas_call(body, name="s5_bscale_bwd",
                          out_shape=[_S(br.shape), _S(br.shape), _S(cr.shape), _S(cr.shape)])(cr, ci, br, bi, gr, gi)


def _row(w):
    return pl.BlockSpec((TM, w), lambda i: (i, 0))


def _full(shape):
    return pl.BlockSpec(tuple(shape), lambda i: (0,) * len(shape))


def _post_fwd(x, hs, z, y0, p, w_glu, b_glu, w_out, g1, b1, ple_w, w_pg, b_pg, g2, b2):
    L = x.shape[0]

    def body(x_ref, hs_ref, z_ref, y0_ref, p_ref, wg_ref, bg_ref, wo_ref, g1_ref, b1_ref, pw_ref, wpg_ref, bpg_ref,
             g2_ref, b2_ref, x2_ref, t1_ref, t2_ref, m_ref):
        rg_gate = z_ref[:, C_RGG:C_RGG + RG_W]
        s5_gate = z_ref[:, C_S5G:C_S5G + S5_W]
        rg_y = hs_ref[...] * _silu_and_grad(rg_gate)[0]
        y1 = _gelu(y0_ref[...])
        gl = _sigmoid(_mm(y1, wg_ref[...]) + bg_ref[...])
        s5_y = (y1 * gl) * _silu_and_grad(s5_gate)[0]
        m_ref[:, :RG_W] = rg_y
        m_ref[:, RG_W:] = s5_y
        mix = _mm(m_ref[...], wo_ref[...])
        t1 = ALPHA * x_ref[...] + mix
        x1, _, _ = _ln_fwd(t1, g1_ref[...], b1_ref[...])
        e = _mm(p_ref[...], pw_ref[...]) * _sigmoid(_mm(x1, wpg_ref[...]) + bpg_ref[...])
        t2 = ALPHA * x1 + e
        x2, _, _ = _ln_fwd(t2, g2_ref[...], b2_ref[...])
        t1_ref[...], t2_ref[...], x2_ref[...] = t1, t2, x2

    vec = _full((1, D_MODEL))
    return pl.pallas_call(
        body, name="post_fwd", grid=(L // TM,),
        in_specs=[_row(D_MODEL), _row(RG_W), _row(Z_W), _row(S5_W), _row(256), _full((S5_W, S5_W)), _full((1, S5_W)),
                  _full((D_MODEL, D_MODEL)), vec, vec, _full((256, D_MODEL)), _full((D_MODEL, D_MODEL)), vec, vec, vec],
        out_specs=[_row(D_MODEL)] * 4, out_shape=[_S((L, D_MODEL))] * 4,
        compiler_params=_params(1))(x, hs, z, y0, p, w_glu, b_glu, w_out, g1, b1, ple_w, w_pg, b_pg, g2, b2)


def _post_bwd_a(dx2_or_target, is_top, t2, t1, p, ple_w, w_pg, b_pg, g1, b1, g2, b2):
    L = t1.shape[0]

    def body(d_ref, t2_ref, t1_ref, p_ref, pw_ref, wpg_ref, bpg_ref, g1_ref, b1_ref, g2_ref, b2_ref,
             dt1_ref, dpw_out, dwpg_out, dbpg_ref, dg1_ref, db1_ref, dg2_ref, db2_ref, loss_ref, dpw_ref, dwpg_ref):
        @pl.when(pl.program_id(0) == 0)
        def _():
            for ref in (dpw_ref, dwpg_ref, dbpg_ref, dg1_ref, db1_ref, dg2_ref, db2_ref, loss_ref):
                ref[...] = jnp.zeros_like(ref)

        g1, g2 = g1_ref[...], g2_ref[...]
        x1, xh1, rstd1 = _ln_fwd(t1_ref[...], g1, b1_ref[...])
        x2, xh2, rstd2 = _ln_fwd(t2_ref[...], g2, b2_ref[...])
        if is_top:
            err = x2 - d_ref[...]
            loss_ref[...] += _colsum(err * err)
            dx2 = err * (1.0 / D_MODEL)
        else:
            dx2 = d_ref[...]
        p = p_ref[...]
        q = _mm(p, pw_ref[...])
        gt = _sigmoid(_mm(x1, wpg_ref[...]) + bpg_ref[...])
        dg2_ref[...] += _colsum(dx2 * xh2)
        db2_ref[...] += _colsum(dx2)
        dt2 = _ln_bwd(dx2, xh2, rstd2, g2)
        dq = dt2 * gt
        dgpre = (dt2 * q) * gt * (1.0 - gt)
        dpw_ref[...] += _mm_tn(p, dq)
        dwpg_ref[...] += _mm_tn(x1, dgpre)
        dbpg_ref[...] += _colsum(dgpre)
        dx1 = ALPHA * dt2 + _mm_nt(dgpre, wpg_ref[...])
        dg1_ref[...] += _colsum(dx1 * xh1)
        db1_ref[...] += _colsum(dx1)
        dt1_ref[...] = _ln_bwd(dx1, xh1, rstd1, g1)

        @pl.when(pl.program_id(0) == L // TM - 1)
        def _():
            dpw_out[...] = dpw_ref[...].astype(WIRE)
            dwpg_out[...] = dwpg_ref[...].astype(WIRE)

    vec = _full((1, D_MODEL))
    return pl.pallas_call(
        body, name="post_bwd_a_top" if is_top else "post_bwd_a", grid=(L // TM,),
        in_specs=[_row(D_MODEL), _row(D_MODEL), _row(D_MODEL), _row(256), _full((256, D_MODEL)),
                  _full((D_MODEL, D_MODEL)), vec, vec, vec, vec, vec],
        out_specs=[_row(D_MODEL), _full((256, D_MODEL)), _full((D_MODEL, D_MODEL)), vec, vec, vec, vec, vec, vec],
        out_shape=[_S((L, D_MODEL)), _S((256, D_MODEL), WIRE), _S((D_MODEL, D_MODEL), WIRE)] + [_S((1, D_MODEL))] * 6,
        scratch_shapes=[pltpu.VMEM((256, D_MODEL), F32), pltpu.VMEM((D_MODEL, D_MODEL), F32)],
        compiler_params=_params(1))(dx2_or_target, t2, t1, p, ple_w, w_pg, b_pg, g1, b1, g2, b2)


def _post_bwd_b(dt1, m, z, hs, y0, w_out, w_glu, b_glu):
    L = dt1.shape[0]

    def body(dt1_ref, m_ref, z_ref, hs_ref, y0_ref, wo_ref, wg_ref, bg_ref,
             dhs_ref, dy0_ref, dzg_ref, dwo_out, dwg_out, dbg_ref, dwo_ref, dwg_ref):
        @pl.when(pl.program_id(0) == 0)
        def _():
            for ref in (dwo_ref, dwg_ref, dbg_ref):
                ref[...] = jnp.zeros_like(ref)

        dt1b = dt1_ref[...].astype(MXU)
        dm = _mm_nt(dt1b, wo_ref[...])
        dwo_ref[...] += _mm_tn(m_ref[...], dt1b)
        d_rgy, d_s5y = dm[:, :RG_W], dm[:, RG_W:]
        rg_gate = z_ref[:, C_RGG:C_RGG + RG_W]
        s5_gate = z_ref[:, C_S5G:C_S5G + S5_W]
        sl, dsl = _silu_and_grad(rg_gate)
        dhs_ref[...] = d_rgy * sl
        dzg_ref[:, :RG_W] = d_rgy * hs_ref[...] * dsl
        y0 = y0_ref[...]
        y1 = _gelu(y0)
        gl = _sigmoid(_mm(y1, wg_ref[...]) + bg_ref[...])
        sl, dsl = _silu_and_grad(s5_gate)
        dy2 = d_s5y * sl
        dzg_ref[:, RG_W:] = d_s5y * (y1 * gl) * dsl
        dglpre = (dy2 * y1) * gl * (1.0 - gl)
        dwg_ref[...] += _mm_tn(y1, dglpre)
        dbg_ref[...] += _colsum(dglpre)
        dy1 = dy2 * gl + _mm_nt(dglpre, wg_ref[...])
        dy0_ref[...] = dy1 * _gelu_grad(y0)

        @pl.when(pl.program_id(0) == L // TM - 1)
        def _():
            dwo_out[...] = dwo_ref[...].astype(WIRE)
            dwg_out[...] = dwg_ref[...].astype(WIRE)

    return pl.pallas_call(
        body, name="post_bwd_b", grid=(L // TM,),
        in_specs=[_row(D_MODEL), _row(D_MODEL), _row(Z_W), _row(RG_W), _row(S5_W), _full((D_MODEL, D_MODEL)),
                  _full((S5_W, S5_W)), _full((1, S5_W))],
        out_specs=[_row(RG_W), _row(S5_W), _row(D_MODEL), _full((D_MODEL, D_MODEL)), _full((S5_W, S5_W)), _full((1, S5_W))],
        out_shape=[_S((L, RG_W)), _S((L, S5_W)), _S((L, D_MODEL)), _S((D_MODEL, D_MODEL), WIRE), _S((S5_W, S5_W), WIRE),
                   _S((1, S5_W))],
        scratch_shapes=[pltpu.VMEM((D_MODEL, D_MODEL), F32), pltpu.VMEM((S5_W, S5_W), F32)],
        compiler_params=_params(1))(dt1, m, z, hs, y0, w_out, w_glu, b_glu)


def _adamw(parts, w, m, v):
    n, R, C = parts.shape
    tr = R
    for cand in (512, 256, 128, 64, 32, 16, 8):
        if R % cand == 0 and n * cand * C * 4 <= 4 * 1024 * 1024:
            tr = cand
            break

    def body(p_ref, w_ref, m_ref, v_ref, g_ref, d_ref, nm_ref, nv_ref):
        g = p_ref[0].astype(F32)
        for k in range(1, n):
            g = g + p_ref[k].astype(F32)
        nm = B1 * m_ref[...] + (1.0 - B1) * g
        nv = B2 * v_ref[...] + (1.0 - B2) * (g * g)
        d_ref[...] = (-LR) * ((nm / BC1) / (jnp.sqrt(nv / BC2) + EPS) + WD * w_ref[...])
        g_ref[...], nm_ref[...], nv_ref[...] = g, nm, nv

    blk = pl.BlockSpec((tr, C), lambda i: (i, 0))
    return pl.pallas_call(
        body, name="adamw", grid=(R // tr,),
        in_specs=[pl.BlockSpec((n, tr, C), lambda i: (0, i, 0)), blk, blk, blk],
        out_specs=[blk] * 4, out_shape=[_S((R, C))] * 4, compiler_params=_params(1))(parts, w, m, v)


def _me():
    return lax.axis_index("x"), lax.axis_index("y"), lax.axis_index("c")


def _lin(dev):
    return 4 * dev[0] + 2 * dev[1] + dev[2]


def _blk(ref, axis, size, idx):
    nd = len(ref.shape)
    start = idx * size
    if axis == nd - 1 and size % LANE == 0:
        start = pl.multiple_of(start, LANE)
    elif axis == nd - 2 and size % 16 == 0:
        start = pl.multiple_of(start, 16)
    ix = [slice(None)] * nd
    ix[axis] = pl.ds(start, size)
    return ref.at[tuple(ix)]


def _all_gather(shards, axes, name):
    n = len(shards)
    sizes = [s.shape[a] for s, a in zip(shards, axes)]
    out_shapes = [_S(s.shape[:a] + (N_DEV * s.shape[a],) + s.shape[a + 1:], s.dtype) for s, a in zip(shards, axes)]

    def body(*refs):
        ins, outs = refs[:n], refs[n:2 * n]
        send_sems, recv_sems, local_sems = refs[2 * n:]
        x, y, c = _me()
        me, sibling = (x, y, c), (x, y, 1 - c)
        chips = [(1 - x, y), (x, 1 - y), (1 - x, 1 - y)]

        def copy(a, k, block, to, from_input=False):
            dst = _blk(outs[a], axes[a], sizes[a], _lin(block))
            return pltpu.make_async_remote_copy(
                src_ref=ins[a] if from_input else dst, dst_ref=dst, send_sem=send_sems.at[a, k],
                recv_sem=recv_sems.at[a, k], device_id=to, device_id_type=MESH)

        mine = [pltpu.make_async_copy(ins[a], _blk(outs[a], axes[a], sizes[a], _lin(me)), local_sems.at[a]) for a in range(n)]
        for cp in mine:
            cp.start()
        first = []
        for a in range(n):
            first.append(copy(a, 0, me, sibling, True))
            first += [copy(a, 1 + j, me, (*chip, c), True) for j, chip in enumerate(chips)]
        for cp in first:
            cp.start()
        passed = []
        for j, chip in enumerate(chips):
            for a in range(n):
                copy(a, 1 + j, (*chip, c), me).wait_recv()
                cp = copy(a, 4 + j, (*chip, c), sibling)
                cp.start()
                passed.append(cp)
        for a in range(n):
            copy(a, 0, sibling, me).wait_recv()
            for j, chip in enumerate(chips):
                copy(a, 4 + j, (*chip, 1 - c), me).wait_recv()
        for cp in first + passed:
            cp.wait_send()
        for cp in mine:
            cp.wait()

    return pl.pallas_call(
        body, name=name, out_shape=out_shapes, in_specs=[ANY] * n, out_specs=[ANY] * n,
        scratch_shapes=[pltpu.SemaphoreType.DMA((n, 7)), pltpu.SemaphoreType.DMA((n, 7)), pltpu.SemaphoreType.DMA((n,))],
    )(*shards)


def _exchange(groups, axes, name):
    arrays = [a for g in groups for a in g]
    where = [(o, i) for o, g in enumerate(groups) for i in range(len(g))]
    ax = [axes[o] for o, _ in where]
    n = len(arrays)
    sizes = [s.shape[a] // N_DEV for s, a in zip(arrays, ax)]
    out_shapes = []
    for g, a in zip(groups, axes):
        s = g[0].shape
        out_shapes.append(_S((N_DEV, len(g)) + s[:a] + (s[a] // N_DEV,) + s[a + 1:], g[0].dtype))

    def body(*refs):
        ins, outs = refs[:n], refs[n:n + len(groups)]
        send_sems, recv_sems, local_sems = refs[n + len(groups):]
        x, y, c = _me()
        me = (x, y, c)
        flip = lambda v, f: 1 - v if f else v
        peers = [(flip(x, k & 4), flip(y, k & 2), flip(c, k & 1)) for k in range(1, N_DEV)]

        def land(a, sender):
            o, i = where[a]
            return outs[o].at[_lin(sender), i]

        def copy(a, k, to):
            return pltpu.make_async_remote_copy(
                src_ref=_blk(ins[a], ax[a], sizes[a], _lin(to)), dst_ref=land(a, me),
                send_sem=send_sems.at[a, k], recv_sem=recv_sems.at[a, k], device_id=to, device_id_type=MESH)

        mine = [pltpu.make_async_copy(_blk(ins[a], ax[a], sizes[a], _lin(me)), land(a, me), local_sems.at[a]) for a in range(n)]
        for cp in mine:
            cp.start()
        sends = [copy(a, k, peer) for a in range(n) for k, peer in enumerate(peers)]
        for cp in sends:
            cp.start()
        for a in range(n):
            for k, peer in enumerate(peers):
                pltpu.make_async_remote_copy(
                    src_ref=land(a, peer), dst_ref=land(a, peer), send_sem=send_sems.at[a, k],
                    recv_sem=recv_sems.at[a, k], device_id=peer, device_id_type=MESH).wait_recv()
        for cp in sends:
            cp.wait_send()
        for cp in mine:
            cp.wait()

    return pl.pallas_call(
        body, name=name, out_shape=out_shapes, in_specs=[ANY] * n, out_specs=[ANY] * len(groups),
        scratch_shapes=[pltpu.SemaphoreType.DMA((n, 7)), pltpu.SemaphoreType.DMA((n, 7)), pltpu.SemaphoreType.DMA((n,))],
    )(*arrays)


def _sum_parts(parts):
    n, R, C = parts.shape

    def body(p_ref, o_ref):
        g = p_ref[0]
        for k in range(1, n):
            g = g + p_ref[k]
        o_ref[...] = g

    return pl.pallas_call(body, name="sum_parts", out_shape=_S((R, C)))(parts)


def _block_diag(w, nb):
    tn, r, c = w.shape
    w = w.reshape(tn // nb, nb, r, c)
    return jnp.einsum('tarc,ab->tarbc', w, jnp.eye(nb, dtype=w.dtype)).reshape(tn // nb, nb * r, nb * c)


def _block_diag_extract(w, nb):
    t, R, C = w.shape
    w = w.reshape(t, nb, R // nb, nb, C // nb)
    return jnp.einsum('tarbc,ab->tarc', w, jnp.eye(nb, dtype=w.dtype)).reshape(t * nb, R // nb, C // nb)


SMALL = ['conv_b', 'rg_wa', 'rg_ba', 'rg_wx', 'rg_bx', 'rg_lambda', 's5_a_re', 's5_a_im', 's5_b_re', 's5_b_im',
         's5_c_re', 's5_c_im', 's5_d', 's5_log_step', 's5_b_glu', 'ln1_g', 'ln1_b', 'ple_gate_b', 'ln2_g', 'ln2_b']
WEIGHTS = ['w_in', 'conv_w', 'conv_b', 'rg_wa', 'rg_ba', 'rg_wx', 'rg_bx', 'rg_lambda', 's5_a_re', 's5_a_im', 's5_b_re',
           's5_b_im', 's5_c_re', 's5_c_im', 's5_d', 's5_log_step', 's5_w_glu', 's5_b_glu', 'w_out', 'ln1_g', 'ln1_b',
           'ple_w', 'ple_gate_w', 'ple_gate_b', 'ln2_g', 'ln2_b']
PACK_ROWS_MULT = 64


def _pack(tree):
    flat = jnp.concatenate([tree[k].reshape(-1) for k in SMALL])
    rows = -(-flat.shape[0] // (LANE * PACK_ROWS_MULT)) * PACK_ROWS_MULT
    return jnp.pad(flat, (0, rows * LANE - flat.shape[0])).reshape(rows, LANE)


def _unpack(packed, like):
    flat, out, o = packed.reshape(-1), {}, 0
    for k in SMALL:
        n = math.prod(like[k].shape)
        out[k] = flat[o:o + n].reshape(like[k].shape)
        o += n
    return out


def _local_grads(x, p, target, W, disc):
    depth = 2
    saved = []
    for i in range(depth):
        w = W[i]
        z = _inproj_fwd(x, w['w_in'])
        hs = _rg_fwd(z, w['conv_w'], w['conv_b'], w['wa_bd'], w['wx_bd'], w['rg_ba'], w['rg_bx'], w['rg_lambda'])
        d = disc[i]
        y0, s_re, s_im = _s5_fwd(z, d['bb_re'], d['bb_im'], d['lb_re'], d['lb_im'], d['c_re'], d['c_im'], w['s5_d'])
        x2, t1, t2, m = _post_fwd(x, hs, z, y0, p[i], w['s5_w_glu'], w['s5_b_glu'], w['w_out'], w['ln1_g'], w['ln1_b'],
                                  w['ple_w'], w['ple_gate_w'], w['ple_gate_b'], w['ln2_g'], w['ln2_b'])
        saved.append((x, z, hs, y0, s_re, s_im, t1, t2, m))
        x = x2

    grads = [None] * depth
    dx = target
    loss = None
    for i in reversed(range(depth)):
        w, d = W[i], disc[i]
        xin, z, hs, y0, s_re, s_im, t1, t2, m = saved[i]
        g = {}
        (dt1, g['ple_w'], g['ple_gate_w'], g['ple_gate_b'], g['ln1_g'], g['ln1_b'], g['ln2_g'], g['ln2_b'], lrow) = _post_bwd_a(
            dx, i == depth - 1, t2, t1, p[i], w['ple_w'], w['ple_gate_w'], w['ple_gate_b'], w['ln1_g'], w['ln1_b'],
            w['ln2_g'], w['ln2_b'])
        if i == depth - 1:
            loss = 0.5 / D_MODEL * jnp.sum(lrow)
        dhs, dy0, dzg, g['w_out'], g['s5_w_glu'], g['s5_b_glu'] = _post_bwd_b(dt1, m, z, hs, y0, w['w_out'], w['s5_w_glu'],
                                                                           w['s5_b_glu'])
        (dzu, g['bb_re'], g['bb_im'], g['lb_re'], g['lb_im'], g['c_re'], g['c_im'], g['s5_d']) = _s5_bwd(
            dy0, z, s_re, s_im, d['bb_re'], d['bb_im'], d['lb_re'], d['lb_im'], d['c_re'], d['c_im'], w['s5_d'])
        (dzx, g['conv_w'], g['conv_b'], g['wa_bd'], g['wx_bd'], g['rg_ba'], g['rg_bx'], g['rg_lambda']) = _rg_bwd(
            dhs, z, hs, w['conv_w'], w['conv_b'], w['wa_bd'], w['wx_bd'], w['rg_ba'], w['rg_bx'], w['rg_lambda'])
        dx, g['w_in'] = _inproj_bwd(dt1, xin, dzx, dzg, dzu, w['w_in'])
        grads[i] = g
    return loss, dx, grads


def _s5_layouts_fwd(s5_a_re, s5_a_im, s5_log_step, s5_b_re, s5_b_im, s5_c_re, s5_c_im):
    depth = s5_a_re.shape[0]
    ar, ai = s5_a_re.reshape(depth * 24, S5_P), s5_a_im.reshape(depth * 24, S5_P)
    ls = s5_log_step.reshape(depth * 24, 1)
    lr, li, cr, ci = _s5_disc_fwd(ar, ai, ls)
    col = lambda a: a.reshape(depth * S5_N, 1)
    br, bi = s5_b_re.reshape(depth * S5_N, 16), s5_b_im.reshape(depth * S5_N, 16)
    bbr, bbi = _s5_bscale_fwd(col(cr), col(ci), br, bi)
    disc = []
    for i in range(depth):
        gph = lambda a: a.reshape(depth, 24, S5_P, 16)[i]
        disc.append(dict(
            bb_re=_block_diag(jnp.swapaxes(gph(bbr), 1, 2), 8), bb_im=_block_diag(jnp.swapaxes(gph(bbi), 1, 2), 8),
            lb_re=lr.reshape(depth, 1, S5_N)[i], lb_im=li.reshape(depth, 1, S5_N)[i],
            c_re=_block_diag(jnp.swapaxes(s5_c_re[i], 1, 2), 8), c_im=_block_diag(jnp.swapaxes(s5_c_im[i], 1, 2), 8)))
    return disc, (ar, ai, ls, col(cr), col(ci), br, bi)


def _s5_layouts_bwd(grads, res):
    ar, ai, ls, cr, ci, br, bi = res
    depth = len(grads)
    stack = lambda f: jnp.stack([f(g) for g in grads])
    dbbr = stack(lambda g: jnp.swapaxes(_block_diag_extract(g['bb_re'], 8), 1, 2)).reshape(depth * S5_N, 16)
    dbbi = stack(lambda g: jnp.swapaxes(_block_diag_extract(g['bb_im'], 8), 1, 2)).reshape(depth * S5_N, 16)
    dbr, dbi, dcr, dci = _s5_bscale_bwd(cr, ci, br, bi, dbbr, dbbi)
    gp = lambda a: a.reshape(depth * 24, S5_P)
    dar, dai, dls = _s5_disc_bwd(ar, ai, ls, gp(stack(lambda g: g['lb_re'])), gp(stack(lambda g: g['lb_im'])), gp(dcr), gp(dci))
    return dict(
        s5_a_re=dar.reshape(depth, 24, S5_P), s5_a_im=dai.reshape(depth, 24, S5_P), s5_log_step=dls.reshape(depth, 24),
        s5_b_re=dbr.reshape(depth, 24, S5_P, 16), s5_b_im=dbi.reshape(depth, 24, S5_P, 16),
        s5_c_re=stack(lambda g: jnp.swapaxes(_block_diag_extract(g['c_re'], 8), 1, 2)),
        s5_c_im=stack(lambda g: jnp.swapaxes(_block_diag_extract(g['c_im'], 8), 1, 2)))


def _layer_weights(full, i):
    row = lambda a: a[i].reshape(1, -1)
    return dict(
        w_in=full['w_in'][i], w_out=full['w_out'][i], ple_w=full['ple_w'][i], ple_gate_w=full['ple_gate_w'][i],
        s5_w_glu=full['s5_w_glu'][i], conv_w=full['conv_w'][i], conv_b=row(full['conv_b']),
        wa_bd=_block_diag(full['rg_wa'][i], 2), wx_bd=_block_diag(full['rg_wx'][i], 2),
        rg_ba=row(full['rg_ba']), rg_bx=row(full['rg_bx']), rg_lambda=row(full['rg_lambda']),
        s5_d=row(full['s5_d']), s5_b_glu=row(full['s5_b_glu']), ln1_g=row(full['ln1_g']), ln1_b=row(full['ln1_b']),
        ple_gate_b=row(full['ple_gate_b']), ln2_g=row(full['ln2_g']), ln2_b=row(full['ln2_b']))


def _full_grads(full, x, p, target):
    disc, res = _s5_layouts_fwd(full['s5_a_re'], full['s5_a_im'], full['s5_log_step'], full['s5_b_re'], full['s5_b_im'],
                                full['s5_c_re'], full['s5_c_im'])
    W = [_layer_weights(full, i) for i in range(2)]
    loss, gx, grads = _local_grads(x, p, target, W, disc)
    stack = lambda f: jnp.stack([f(g) for g in grads])
    out = _s5_layouts_bwd(grads, res)
    for k in SHARD_AXIS:
        out[k] = [g[k] for g in grads]
    out['conv_w'] = stack(lambda g: g['conv_w'])
    for k in ('conv_b', 'rg_ba', 'rg_bx', 'rg_lambda', 's5_b_glu', 'ln1_g', 'ln1_b', 'ple_gate_b', 'ln2_g', 'ln2_b'):
        out[k] = stack(lambda g: g[k][0])
    out['s5_d'] = stack(lambda g: g['s5_d'][0]).reshape(2, 24, 16)
    out['rg_wa'] = stack(lambda g: _block_diag_extract(g['wa_bd'], 2))
    out['rg_wx'] = stack(lambda g: _block_diag_extract(g['wx_bd'], 2))
    return loss, gx, out


SHARD_AXIS = {'w_in': 2, 'w_out': 1, 'ple_w': 2, 'ple_gate_w': 1, 's5_w_glu': 1}


def kernel(x, p, w_in, conv_w, conv_b, rg_wa, rg_ba, rg_wx, rg_bx, rg_lambda, s5_a_re, s5_a_im, s5_b_re, s5_b_im, s5_c_re, s5_c_im, s5_d, s5_log_step, s5_w_glu, s5_b_glu, w_out, ln1_g, ln1_b, ple_w, ple_gate_w, ple_gate_b, ln2_g, ln2_b, loss_target, m_w_in, m_conv_w, m_conv_b, m_rg_wa, m_rg_ba, m_rg_wx, m_rg_bx, m_rg_lambda, m_s5_a_re, m_s5_a_im, m_s5_b_re, m_s5_b_im, m_s5_c_re, m_s5_c_im, m_s5_d, m_s5_log_step, m_s5_w_glu, m_s5_b_glu, m_w_out, m_ln1_g, m_ln1_b, m_ple_w, m_ple_gate_w, m_ple_gate_b, m_ln2_g, m_ln2_b, v_w_in, v_conv_w, v_conv_b, v_rg_wa, v_rg_ba, v_rg_wx, v_rg_bx, v_rg_lambda, v_s5_a_re, v_s5_a_im, v_s5_b_re, v_s5_b_im, v_s5_c_re, v_s5_c_im, v_s5_d, v_s5_log_step, v_s5_w_glu, v_s5_b_glu, v_w_out, v_ln1_g, v_ln1_b, v_ple_w, v_ple_gate_w, v_ple_gate_b, v_ln2_g, v_ln2_b):
    local = dict(locals())
    w = {k: local[k] for k in WEIGHTS}
    mom = {k: local['m_' + k] for k in WEIGHTS}
    var = {k: local['v_' + k] for k in WEIGHTS}

    big = list(SHARD_AXIS)
    gathered = _all_gather([w[k].astype(WIRE) for k in big] + [conv_w[None]], [SHARD_AXIS[k] for k in big] + [0], "gather_weights")
    full = dict(w)
    full.update(dict(zip(big, gathered[:-1])))
    full['conv_w'] = jnp.moveaxis(gathered[-1], 0, 2).reshape(2, 4, RG_W)

    loss, grad_x, g = _full_grads(full, x[0], p[:, 0], loss_target[0])
    loss = lax.psum(loss, ("x", "y", "c"))

    conv_blocks = jnp.moveaxis(g['conv_w'].reshape(2, 4, N_DEV, RG_W // N_DEV), 2, 0).reshape(N_DEV, 8, RG_W // N_DEV)
    packed = _pack(g)
    recv = _exchange([g[k] for k in big] + [[conv_blocks], [packed]], [SHARD_AXIS[k] - 1 for k in big] + [0, 0],
                     "exchange_grads")
    outs = {}
    for k, r in zip(big + ['conv_w'], recv[:-1]):
        shard = w[k].shape
        c = shard[-1]
        two = lambda a: a.reshape(-1, c)
        outs[k] = [o.reshape(shard) for o in _adamw(r.reshape(N_DEV, -1, c), two(w[k]), two(mom[k]), two(var[k]))]

    rows = packed.shape[0] // N_DEV
    mine = _sum_parts(recv[-1].reshape(N_DEV, rows, LANE))
    summed = _all_gather([mine], [0], "gather_small_grads")[0]
    small = _adamw(summed[None], _pack(w), _pack(mom), _pack(var))
    small = [_unpack(o, w) for o in small]
    for k in SMALL:
        outs[k] = [o[k] for o in small]

    res = [loss, grad_x[None]]
    for j in range(4):
        res += [outs[k][j] for k in WEIGHTS]
    return tuple(res)
```

```python
import functools
import math

import jax
import jax.numpy as jnp
from jax import lax
from jax.experimental import pallas as pl
from jax.experimental.pallas import tpu as pltpu

F32 = jnp.float32
MXU = jnp.bfloat16
WIRE = jnp.bfloat16

N_DEV = 8
D_MODEL = 1024
RG_W = 640
S5_W = 384
S5_P = 64
S5_N = 24 * S5_P
Z_W = 2 * RG_W + 2 * S5_W
C_RGG = RG_W
C_S5U = 2 * RG_W
C_S5G = 2 * RG_W + S5_W
LANE = 128
N_RG_T = RG_W // LANE
N_S5_T = S5_W // LANE
N_S5_J = 4
ALPHA = (2.0 * 2) ** 0.25
LN_EPS = 1e-5
RG_C = 8.0
LR, B1, B2, EPS, WD, STEP = 0.001, 0.9, 0.999, 1e-08, 0.01, 10
BC1 = 1.0 - B1 ** STEP
BC2 = 1.0 - B2 ** STEP
RC = 256
TM = 256
VMEM_LIMIT = 56 * 1024 * 1024

MESH = pl.DeviceIdType.MESH
ANY = pl.BlockSpec(memory_space=pl.ANY)


def _params(n_grid_axes, vmem=VMEM_LIMIT):
    return pltpu.CompilerParams(dimension_semantics=("arbitrary",) * n_grid_axes, vmem_limit_bytes=vmem)


def _S(shape, dtype=F32):
    return jax.ShapeDtypeStruct(tuple(shape), dtype)


def _sigmoid(x):
    return 1.0 / (1.0 + jnp.exp(-x))


def _silu_and_grad(x):
    s = _sigmoid(x)
    return x * s, s * (1.0 + x * (1.0 - s))


_GELU_C = math.sqrt(2.0 / math.pi)


def _gelu(x):
    return 0.5 * x * (1.0 + jnp.tanh(_GELU_C * (x + 0.044715 * (x * x * x))))


def _gelu_grad(x):
    th = jnp.tanh(_GELU_C * (x + 0.044715 * (x * x * x)))
    return 0.5 * (1.0 + th) + 0.5 * x * (1.0 - th * th) * (_GELU_C * (1.0 + 3.0 * 0.044715 * (x * x)))


def _mm(a, b):
    return jnp.dot(a.astype(MXU), b.astype(MXU), preferred_element_type=F32)


def _mm_nt(a, b):
    return lax.dot_general(a.astype(MXU), b.astype(MXU), (((1,), (1,)), ((), ())), preferred_element_type=F32)


def _mm_tn(a, b):
    return lax.dot_general(a.astype(MXU), b.astype(MXU), (((0,), (0,)), ((), ())), preferred_element_type=F32)


def _ln_fwd(t, g, b):
    mu = jnp.mean(t, axis=-1, keepdims=True)
    tc = t - mu
    var = jnp.mean(tc * tc, axis=-1, keepdims=True)
    rstd = lax.rsqrt(var + LN_EPS)
    xhat = tc * rstd
    return xhat * g + b, xhat, rstd


def _ln_bwd(dy, xhat, rstd, g):
    dxh = dy * g
    m1 = jnp.mean(dxh, axis=-1, keepdims=True)
    m2 = jnp.mean(dxh * xhat, axis=-1, keepdims=True)
    return rstd * (dxh - m1 - xhat * m2)


def _colsum(a):
    return jnp.sum(a, axis=0, keepdims=True)


def _up(x, d, rows, fill):
    n = x.shape[0]
    return jnp.where(rows < n - d, pltpu.roll(x, n - d, 0), fill)


SUB = 8
TILE_STEPS = (1, 2, 4)


def _r8(width):
    return lax.broadcasted_iota(jnp.int32, (SUB, width), 0)


def _scan_real(a, u, carry, reverse=False):
    r8 = _r8(a.shape[1])
    n = a.shape[0] // SUB
    outs = [None] * n
    for k in (reversed(range(n)) if reverse else range(n)):
        A, U = a[SUB * k:SUB * k + SUB], u[SUB * k:SUB * k + SUB]
        for d in TILE_STEPS:
            m = (r8 < SUB - d) if reverse else (r8 >= d)
            sh = SUB - d if reverse else d
            U = A * jnp.where(m, pltpu.roll(U, sh, 0), 0.0) + U
            A = A * jnp.where(m, pltpu.roll(A, sh, 0), 1.0)
        h = A * carry + U
        outs[k] = h
        carry = h[0:1] if reverse else h[SUB - 1:SUB]
    return jnp.concatenate(outs, axis=0), carry


def _tile_powers(lr, li, reverse=False):
    width = lr.shape[1]
    r8 = _r8(width)
    steps = []
    pr, pi = lr, li
    er, ei = jnp.broadcast_to(lr, (SUB, width)), jnp.broadcast_to(li, (SUB, width))
    for d in TILE_STEPS:
        m = (r8 < SUB - d) if reverse else (r8 >= d)
        sh = SUB - d if reverse else d
        steps.append((sh, jnp.where(m, pr, 0.0), jnp.where(m, pi, 0.0)))
        er, ei = _cmul(er, ei, jnp.where(m, pltpu.roll(er, sh, 0), 1.0), jnp.where(m, pltpu.roll(ei, sh, 0), 0.0))
        pr, pi = _cmul(pr, pi, pr, pi)
    return steps, (er, ei)


def _scan_lti(xr, xi, carry, steps, e, reverse=False):
    er, ei = e
    kr, ki = carry
    n = xr.shape[0] // SUB
    outr, outi = [None] * n, [None] * n
    for k in (reversed(range(n)) if reverse else range(n)):
        sr, si = xr[SUB * k:SUB * k + SUB], xi[SUB * k:SUB * k + SUB]
        for sh, pr, pi in steps:
            shr, shi = pltpu.roll(sr, sh, 0), pltpu.roll(si, sh, 0)
            sr, si = sr + (pr * shr - pi * shi), si + (pr * shi + pi * shr)
        sr = sr + (er * kr - ei * ki)
        si = si + (er * ki + ei * kr)
        outr[k], outi[k] = sr, si
        kr, ki = (sr[0:1], si[0:1]) if reverse else (sr[SUB - 1:SUB], si[SUB - 1:SUB])
    return jnp.concatenate(outr, axis=0), jnp.concatenate(outi, axis=0), (kr, ki)


def _halo(ref, c, r0):
    rp = pl.multiple_of(jnp.maximum(r0 - 8, 0), 8)
    return jnp.where(c > 0, ref[pl.ds(rp, 8), :], 0.0)


def _conv_taps(xe):
    return [pltpu.roll(xe, 3, 0)[8:, :], pltpu.roll(xe, 2, 0)[8:, :], pltpu.roll(xe, 1, 0)[8:, :], xe[8:, :]]


def _rg_gates(h, wa, wx, ba, bx, sp):
    r = _sigmoid(_mm(h, wa) + ba)
    i = _sigmoid(_mm(h, wx) + bx)
    log_a = (-RG_C) * r * sp
    a = jnp.exp(log_a)
    mult = jnp.sqrt(-jnp.tanh(log_a) * (a * a + 1.0))
    return r, i, a, mult


def _softplus(y):
    return jnp.maximum(y, 0.0) + jnp.log1p(jnp.exp(-jnp.abs(y)))


def _after(token):
    return ([], []) if token is None else ([token], [ANY])


def _inproj_fwd(x, w_in, token=None):
    L = x.shape[0]

    def body(x_ref, w_ref, *rest):
        rest[-1][...] = _mm(x_ref[...], w_ref[...])

    extra, extra_specs = _after(token)
    return pl.pallas_call(
        body, name="inproj_fwd", grid=(L // TM,),
        in_specs=[pl.BlockSpec((TM, D_MODEL), lambda i: (i, 0)), pl.BlockSpec((D_MODEL, Z_W), lambda i: (0, 0))] + extra_specs,
        out_specs=pl.BlockSpec((TM, Z_W), lambda i: (i, 0)),
        out_shape=_S((L, Z_W)), compiler_params=_params(1))(x, w_in, *extra)


def _inproj_bwd(dt1, x, dzx, dzg, dzu, w_in):
    L = x.shape[0]

    def body(dt1_ref, x_ref, dzx_ref, dzg_ref, dzu_ref, w_ref, dx_ref, dw_ref, acc_ref):
        @pl.when(pl.program_id(0) == 0)
        def _():
            acc_ref[...] = jnp.zeros_like(acc_ref)
        dzg = dzg_ref[...]
        dz = jnp.concatenate([dzx_ref[...], dzg[:, :RG_W], dzu_ref[...], dzg[:, RG_W:]], axis=1).astype(MXU)
        dx_ref[...] = ALPHA * dt1_ref[...] + _mm_nt(dz, w_ref[...])
        acc_ref[...] += _mm_tn(x_ref[...], dz)

        @pl.when(pl.program_id(0) == L // TM - 1)
        def _():
            dw_ref[...] = acc_ref[...].astype(WIRE)

    row = lambda w: pl.BlockSpec((TM, w), lambda i: (i, 0))
    return pl.pallas_call(
        body, name="inproj_bwd", grid=(L // TM,),
        in_specs=[row(D_MODEL), row(D_MODEL), row(RG_W), row(D_MODEL), row(S5_W),
                  pl.BlockSpec((D_MODEL, Z_W), lambda i: (0, 0))],
        out_specs=[row(D_MODEL), pl.BlockSpec((D_MODEL, Z_W), lambda i: (0, 0))],
        out_shape=[_S((L, D_MODEL)), _S((D_MODEL, Z_W), WIRE)], scratch_shapes=[pltpu.VMEM((D_MODEL, Z_W), F32)],
        compiler_params=_params(1))(dt1, x, dzx, dzg, dzu, w_in)


def _rg_specs(L):
    tile = lambda rows: pl.BlockSpec((rows, LANE), lambda c: (0, c))
    return tile, pl.BlockSpec((None, LANE, LANE), lambda c: (c, 0, 0))


def _rg_fwd(z, cw, cb, wa_bd, wx_bd, ba, bx, lam):
    L = z.shape[0]

    def body(x_ref, cw_ref, cb_ref, wa_ref, wx_ref, ba_ref, bx_ref, lam_ref, hs_ref):
        w, b = cw_ref[...], cb_ref[...]
        wa, wx, ba_, bx_ = wa_ref[...].astype(MXU), wx_ref[...].astype(MXU), ba_ref[...], bx_ref[...]
        sp = _softplus(-lam_ref[...])

        def step(c, carry):
            r0 = pl.multiple_of(c * RC, RC)
            xe = jnp.concatenate([_halo(x_ref, c, r0), x_ref[pl.ds(r0, RC), :]], axis=0)
            t = _conv_taps(xe)
            h = t[0] * w[0:1] + t[1] * w[1:2] + t[2] * w[2:3] + t[3] * w[3:4] + b
            _, i, a, mult = _rg_gates(h, wa, wx, ba_, bx_, sp)
            hs, carry = _scan_real(a, mult * (i * h), carry)
            hs_ref[pl.ds(r0, RC), :] = hs
            return carry

        lax.fori_loop(0, L // RC, step, jnp.zeros((1, LANE), F32))

    tile, bd = _rg_specs(L)
    return pl.pallas_call(
        body, name="rg_fwd", grid=(N_RG_T,),
        in_specs=[tile(L), tile(4), tile(1), bd, bd, tile(1), tile(1), tile(1)],
        out_specs=tile(L), out_shape=_S((L, RG_W)), compiler_params=_params(1))(z, cw, cb, wa_bd, wx_bd, ba, bx, lam)


def _rg_bwd(dhs, z, hs, cw, cb, wa_bd, wx_bd, ba, bx, lam):
    L = z.shape[0]

    def body(g_ref, x_ref, hs_ref, cw_ref, cb_ref, wa_ref, wx_ref, ba_ref, bx_ref, lam_ref,
             dx_ref, dcw_ref, dcb_ref, dwa_ref, dwx_ref, dba_ref, dbx_ref, dlam_ref):
        w, b = cw_ref[...], cb_ref[...]
        wa, wx, ba_, bx_ = wa_ref[...].astype(MXU), wx_ref[...].astype(MXU), ba_ref[...], bx_ref[...]
        lam = lam_ref[...]
        sp = _softplus(-lam)
        rows = lax.broadcasted_iota(jnp.int32, (RC, LANE), 0)
        for ref in (dcw_ref, dcb_ref, dwa_ref, dwx_ref, dba_ref, dbx_ref, dlam_ref):
            ref[...] = jnp.zeros_like(ref)
        nch = L // RC

        def step(k, carry):
            cin, nxt = carry
            c = nch - 1 - k
            r0 = pl.multiple_of(c * RC, RC)
            xe = jnp.concatenate([_halo(x_ref, c, r0), x_ref[pl.ds(r0, RC), :]], axis=0)
            t = _conv_taps(xe)
            h = t[0] * w[0:1] + t[1] * w[1:2] + t[2] * w[2:3] + t[3] * w[3:4] + b
            r, i, a, mult = _rg_gates(h, wa, wx, ba_, bx_, sp)
            hs_e = jnp.concatenate([_halo(hs_ref, c, r0), hs_ref[pl.ds(r0, RC), :]], axis=0)
            hs_prev = pltpu.roll(hs_e, 1, 0)[8:, :]
            g = g_ref[pl.ds(r0, RC), :]
            cc, cin_new = _scan_real(a, a * g, cin, reverse=True)
            dh = g + _up(cc, 1, rows, cin)
            ih = i * h
            dlog_a = dh * hs_prev * a - (dh * ih) * (a * a) / mult
            di = dh * mult * h
            dhin = dh * mult * i
            dr = dlog_a * ((-RG_C) * sp)
            dlam_ref[...] += _colsum(dlog_a * r)
            dra = dr * r * (1.0 - r)
            dia = di * i * (1.0 - i)
            dwa_ref[...] += _mm_tn(h, dra)
            dwx_ref[...] += _mm_tn(h, dia)
            dba_ref[...] += _colsum(dra)
            dbx_ref[...] += _colsum(dia)
            dhin = dhin + _mm_nt(dra, wa) + _mm_nt(dia, wx)
            de = jnp.concatenate([dhin, nxt], axis=0)
            n = RC + 8
            dx = (dhin * w[3:4] + pltpu.roll(de, n - 1, 0)[:RC, :] * w[2:3]
                  + pltpu.roll(de, n - 2, 0)[:RC, :] * w[1:2] + pltpu.roll(de, n - 3, 0)[:RC, :] * w[0:1])
            dx_ref[pl.ds(r0, RC), :] = dx
            for kk in range(4):
                dcw_ref[kk:kk + 1, :] += _colsum(dhin * t[kk])
            dcb_ref[...] += _colsum(dhin)
            return cin_new, dhin[0:8, :]

        lax.fori_loop(0, nch, step, (jnp.zeros((1, LANE), F32), jnp.zeros((8, LANE), F32)))
        dlam_ref[...] = dlam_ref[...] * (RG_C * _sigmoid(-lam))

    tile, bd = _rg_specs(L)
    return pl.pallas_call(
        body, name="rg_bwd", grid=(N_RG_T,),
        in_specs=[tile(L), tile(L), tile(L), tile(4), tile(1), bd, bd, tile(1), tile(1), tile(1)],
        out_specs=[tile(L), tile(4), tile(1), bd, bd, tile(1), tile(1), tile(1)],
        out_shape=[_S((L, RG_W)), _S((4, RG_W)), _S((1, RG_W)), _S((N_RG_T, LANE, LANE)), _S((N_RG_T, LANE, LANE)),
                   _S((1, RG_W)), _S((1, RG_W)), _S((1, RG_W))],
        compiler_params=_params(1))(dhs, z, hs, cw, cb, wa_bd, wx_bd, ba, bx, lam)


def _cmul(ar, ai, br, bi):
    return ar * br - ai * bi, ar * bi + ai * br


S5_TW = S5_N // N_S5_T


def _s5_specs(L):
    in_tile = pl.BlockSpec((L, LANE), lambda t: (0, t))
    st = pl.BlockSpec((L, S5_TW), lambda t: (0, t))
    bb = pl.BlockSpec((None, LANE, S5_TW), lambda t: (t, 0, 0))
    cc = pl.BlockSpec((None, S5_TW, LANE), lambda t: (t, 0, 0))
    lb = pl.BlockSpec((1, S5_TW), lambda t: (0, t))
    dv = pl.BlockSpec((1, LANE), lambda t: (0, t))
    return in_tile, st, bb, cc, lb, dv


def _s5_fwd(z, bb_re, bb_im, lb_re, lb_im, c_re, c_im, dvec):
    L = z.shape[0]

    def body(u_ref, bbr_ref, bbi_ref, lr_ref, li_ref, cr_ref, ci_ref, d_ref, y_ref, sr_ref, si_ref):
        bbr, bbi = bbr_ref[...].astype(MXU), bbi_ref[...].astype(MXU)
        cr, ci = cr_ref[...].astype(MXU), ci_ref[...].astype(MXU)
        dv = d_ref[...]
        steps, e = _tile_powers(lr_ref[...], li_ref[...])

        def step(c, carry):
            r0 = pl.multiple_of(c * RC, RC)
            u = u_ref[pl.ds(r0, RC), :]
            ub = u.astype(MXU)
            sr = jnp.dot(ub, bbr, preferred_element_type=F32)
            si = jnp.dot(ub, bbi, preferred_element_type=F32)
            sr, si, carry = _scan_lti(sr, si, carry, steps, e)
            sr_ref[pl.ds(r0, RC), :] = sr
            si_ref[pl.ds(r0, RC), :] = si
            y_ref[pl.ds(r0, RC), :] = dv * u + (_mm(sr, cr) - _mm(si, ci))
            return carry

        zero = jnp.zeros((1, S5_TW), F32)
        lax.fori_loop(0, L // RC, step, (zero, zero))

    in_tile, st, bb, cc, lb, dv = _s5_specs(L)
    u_tile = pl.BlockSpec((L, LANE), lambda t: (0, C_S5U // LANE + t))
    return pl.pallas_call(
        body, name="s5_fwd", grid=(N_S5_T,),
        in_specs=[u_tile, bb, bb, lb, lb, cc, cc, dv],
        out_specs=[in_tile, st, st],
        out_shape=[_S((L, S5_W)), _S((L, S5_N)), _S((L, S5_N))],
        compiler_params=_params(1))(z, bb_re, bb_im, lb_re, lb_im, c_re, c_im, dvec)


def _s5_bwd(dy0, z, s_re, s_im, bb_re, bb_im, lb_re, lb_im, c_re, c_im, dvec):
    L = z.shape[0]

    def body(dy_ref, u_ref, sr_ref, si_ref, bbr_ref, bbi_ref, lr_ref, li_ref, cr_ref, ci_ref, d_ref,
             du_ref, dbbr_ref, dbbi_ref, dlr_ref, dli_ref, dcr_ref, dci_ref, dd_ref):
        bbr, bbi = bbr_ref[...].astype(MXU), bbi_ref[...].astype(MXU)
        cr, ci = cr_ref[...].astype(MXU), ci_ref[...].astype(MXU)
        lr, li = lr_ref[...], -li_ref[...]
        dv = d_ref[...]
        steps, e = _tile_powers(lr, li, reverse=True)
        for ref in (dbbr_ref, dbbi_ref, dlr_ref, dli_ref, dcr_ref, dci_ref, dd_ref):
            ref[...] = jnp.zeros_like(ref)
        nch = L // RC

        def step(k, carry):
            c = nch - 1 - k
            r0 = pl.multiple_of(c * RC, RC)
            dy = dy_ref[pl.ds(r0, RC), :]
            u = u_ref[pl.ds(r0, RC), :]
            dyb, ub = dy.astype(MXU), u.astype(MXU)
            sr, si = sr_ref[pl.ds(r0, RC), :], si_ref[pl.ds(r0, RC), :]
            dcr_ref[...] += _mm_tn(sr, dyb)
            dci_ref[...] -= _mm_tn(si, dyb)
            gr = _mm_nt(dyb, cr)
            gi = -_mm_nt(dyb, ci)
            gr, gi, carry = _scan_lti(gr, gi, carry, steps, e, reverse=True)
            pr_ = pltpu.roll(jnp.concatenate([_halo(sr_ref, c, r0), sr], axis=0), 1, 0)[8:, :]
            pi_ = pltpu.roll(jnp.concatenate([_halo(si_ref, c, r0), si], axis=0), 1, 0)[8:, :]
            dlr_ref[...] += _colsum(pr_ * gr + pi_ * gi)
            dli_ref[...] += _colsum(pr_ * gi - pi_ * gr)
            grb, gib = gr.astype(MXU), gi.astype(MXU)
            dbbr_ref[...] += _mm_tn(ub, grb)
            dbbi_ref[...] += _mm_tn(ub, gib)
            du_ref[pl.ds(r0, RC), :] = dv * dy + (_mm_nt(grb, bbr) + _mm_nt(gib, bbi))
            dd_ref[...] += _colsum(dy * u)
            return carry

        zero = jnp.zeros((1, S5_TW), F32)
        lax.fori_loop(0, nch, step, (zero, zero))

    in_tile, st, bb, cc, lb, dv = _s5_specs(L)
    u_tile = pl.BlockSpec((L, LANE), lambda t: (0, C_S5U // LANE + t))
    return pl.pallas_call(
        body, name="s5_bwd", grid=(N_S5_T,),
        in_specs=[in_tile, u_tile, st, st, bb, bb, lb, lb, cc, cc, dv],
        out_specs=[in_tile, bb, bb, lb, lb, cc, cc, dv],
        out_shape=[_S((L, S5_W)), _S((N_S5_T, LANE, S5_TW)), _S((N_S5_T, LANE, S5_TW)), _S((1, S5_N)), _S((1, S5_N)),
                   _S((N_S5_T, S5_TW, LANE)), _S((N_S5_T, S5_TW, LANE)), _S((1, S5_W))],
        compiler_params=_params(1))(dy0, z, s_re, s_im, bb_re, bb_im, lb_re, lb_im, c_re, c_im, dvec)


def _disc(ar, ai, ls):
    dt = jnp.exp(ls)
    mag = jnp.exp(ar * dt)
    lr = mag * jnp.cos(ai * dt)
    li = mag * jnp.sin(ai * dt)
    den = ar * ar + ai * ai
    cr = ((lr - 1.0) * ar + li * ai) / den
    ci = (li * ar - (lr - 1.0) * ai) / den
    return lr, li, cr, ci


def _s5_disc_fwd(ar, ai, ls):
    def body(ar_ref, ai_ref, ls_ref, lr_ref, li_ref, cr_ref, ci_ref):
        lr, li, cr, ci = _disc(ar_ref[...], ai_ref[...], ls_ref[...])
        lr_ref[...], li_ref[...], cr_ref[...], ci_ref[...] = lr, li, cr, ci

    sh = _S(ar.shape)
    return pl.pallas_call(body, name="s5_disc_fwd", out_shape=[sh, sh, sh, sh])(ar, ai, ls)


def _s5_disc_bwd(ar, ai, ls, dlr, dli, dcr, dci):
    def body(ar_ref, ai_ref, ls_ref, dlr_ref, dli_ref, dcr_ref, dci_ref, dar_ref, dai_ref, dls_ref):
        _, vjp = jax.vjp(_disc, ar_ref[...], ai_ref[...], jnp.broadcast_to(ls_ref[...], ar_ref.shape))
        dar, dai, dls = vjp((dlr_ref[...], dli_ref[...], dcr_ref[...], dci_ref[...]))
        dar_ref[...], dai_ref[...] = dar, dai
        dls_ref[...] = jnp.sum(dls, axis=1, keepdims=True)

    return pl.pallas_call(body, name="s5_disc_bwd", out_shape=[_S(ar.shape), _S(ar.shape), _S(ls.shape)])(
        ar, ai, ls, dlr, dli, dcr, dci)


def _s5_bscale_fwd(cr, ci, br, bi):
    def body(cr_ref, ci_ref, br_ref, bi_ref, or_ref, oi_ref):
        or_ref[...], oi_ref[...] = _cmul(cr_ref[...], ci_ref[...], br_ref[...], bi_ref[...])

    return pl.pallas_call(body, name="s5_bscale_fwd", out_shape=[_S(br.shape), _S(br.shape)])(cr, ci, br, bi)


def _s5_bscale_bwd(cr, ci, br, bi, gr, gi):
    def body(cr_ref, ci_ref, br_ref, bi_ref, gr_ref, gi_ref, dbr_ref, dbi_ref, dcr_ref, dci_ref):
        cr_, ci_, br_, bi_, gr_, gi_ = (r[...] for r in (cr_ref, ci_ref, br_ref, bi_ref, gr_ref, gi_ref))
        dbr_ref[...] = cr_ * gr_ + ci_ * gi_
        dbi_ref[...] = cr_ * gi_ - ci_ * gr_
        dcr_ref[...] = jnp.sum(gr_ * br_ + gi_ * bi_, axis=1, keepdims=True)
        dci_ref[...] = jnp.sum(gi_ * br_ - gr_ * bi_, axis=1, keepdims=True)

    return pl.pallas_call(body, name="s5_bscale_bwd",
                          out_shape=[_S(br.shape), _S(br.shape), _S(cr.shape), _S(cr.shape)])(cr, ci, br, bi, gr, gi)


def _row(w):
    return pl.BlockSpec((TM, w), lambda i: (i, 0))


def _full(shape):
    return pl.BlockSpec(tuple(shape), lambda i: (0,) * len(shape))


def _post_fwd(x, hs, z, y0, p, w_glu, b_glu, w_out, g1, b1, ple_w, w_pg, b_pg, g2, b2):
    L = x.shape[0]

    def body(x_ref, hs_ref, z_ref, y0_ref, p_ref, wg_ref, bg_ref, wo_ref, g1_ref, b1_ref, pw_ref, wpg_ref, bpg_ref,
             g2_ref, b2_ref, x2_ref, t1_ref, t2_ref, m_ref):
        rg_gate = z_ref[:, C_RGG:C_RGG + RG_W]
        s5_gate = z_ref[:, C_S5G:C_S5G + S5_W]
        rg_y = hs_ref[...] * _silu_and_grad(rg_gate)[0]
        y1 = _gelu(y0_ref[...])
        gl = _sigmoid(_mm(y1, wg_ref[...]) + bg_ref[...])
        s5_y = (y1 * gl) * _silu_and_grad(s5_gate)[0]
        m_ref[:, :RG_W] = rg_y
        m_ref[:, RG_W:] = s5_y
        mix = _mm(m_ref[...], wo_ref[...])
        t1 = ALPHA * x_ref[...] + mix
        x1, _, _ = _ln_fwd(t1, g1_ref[...], b1_ref[...])
        e = _mm(p_ref[...], pw_ref[...]) * _sigmoid(_mm(x1, wpg_ref[...]) + bpg_ref[...])
        t2 = ALPHA * x1 + e
        x2, _, _ = _ln_fwd(t2, g2_ref[...], b2_ref[...])
        t1_ref[...], t2_ref[...], x2_ref[...] = t1, t2, x2

    vec = _full((1, D_MODEL))
    return pl.pallas_call(
        body, name="post_fwd", grid=(L // TM,),
        in_specs=[_row(D_MODEL), _row(RG_W), _row(Z_W), _row(S5_W), _row(256), _full((S5_W, S5_W)), _full((1, S5_W)),
                  _full((D_MODEL, D_MODEL)), vec, vec, _full((256, D_MODEL)), _full((D_MODEL, D_MODEL)), vec, vec, vec],
        out_specs=[_row(D_MODEL)] * 4, out_shape=[_S((L, D_MODEL))] * 4,
        compiler_params=_params(1))(x, hs, z, y0, p, w_glu, b_glu, w_out, g1, b1, ple_w, w_pg, b_pg, g2, b2)


def _post_bwd_a(dx2_or_target, is_top, t2, t1, p, ple_w, w_pg, b_pg, g1, b1, g2, b2, token=None):
    L = t1.shape[0]
    extra, extra_specs = _after(token)

    def body(d_ref, t2_ref, t1_ref, p_ref, pw_ref, wpg_ref, bpg_ref, g1_ref, b1_ref, g2_ref, b2_ref, *rest):
        (dt1_ref, dpw_out, dwpg_out, dbpg_ref, dg1_ref, db1_ref, dg2_ref, db2_ref, loss_ref, dpw_ref,
         dwpg_ref) = rest[len(extra):]
        @pl.when(pl.program_id(0) == 0)
        def _():
            for ref in (dpw_ref, dwpg_ref, dbpg_ref, dg1_ref, db1_ref, dg2_ref, db2_ref, loss_ref):
                ref[...] = jnp.zeros_like(ref)

        g1, g2 = g1_ref[...], g2_ref[...]
        x1, xh1, rstd1 = _ln_fwd(t1_ref[...], g1, b1_ref[...])
        x2, xh2, rstd2 = _ln_fwd(t2_ref[...], g2, b2_ref[...])
        if is_top:
            err = x2 - d_ref[...]
            loss_ref[...] += _colsum(err * err)
            dx2 = err * (1.0 / D_MODEL)
        else:
            dx2 = d_ref[...]
        p = p_ref[...]
        q = _mm(p, pw_ref[...])
        gt = _sigmoid(_mm(x1, wpg_ref[...]) + bpg_ref[...])
        dg2_ref[...] += _colsum(dx2 * xh2)
        db2_ref[...] += _colsum(dx2)
        dt2 = _ln_bwd(dx2, xh2, rstd2, g2)
        dq = dt2 * gt
        dgpre = (dt2 * q) * gt * (1.0 - gt)
        dpw_ref[...] += _mm_tn(p, dq)
        dwpg_ref[...] += _mm_tn(x1, dgpre)
        dbpg_ref[...] += _colsum(dgpre)
        dx1 = ALPHA * dt2 + _mm_nt(dgpre, wpg_ref[...])
        dg1_ref[...] += _colsum(dx1 * xh1)
        db1_ref[...] += _colsum(dx1)
        dt1_ref[...] = _ln_bwd(dx1, xh1, rstd1, g1)

        @pl.when(pl.program_id(0) == L // TM - 1)
        def _():
            dpw_out[...] = dpw_ref[...].astype(WIRE)
            dwpg_out[...] = dwpg_ref[...].astype(WIRE)

    vec = _full((1, D_MODEL))
    return pl.pallas_call(
        body, name="post_bwd_a_top" if is_top else "post_bwd_a", grid=(L // TM,),
        in_specs=[_row(D_MODEL), _row(D_MODEL), _row(D_MODEL), _row(256), _full((256, D_MODEL)),
                  _full((D_MODEL, D_MODEL)), vec, vec, vec, vec, vec] + extra_specs,
        out_specs=[_row(D_MODEL), _full((256, D_MODEL)), _full((D_MODEL, D_MODEL)), vec, vec, vec, vec, vec, vec],
        out_shape=[_S((L, D_MODEL)), _S((256, D_MODEL), WIRE), _S((D_MODEL, D_MODEL), WIRE)] + [_S((1, D_MODEL))] * 6,
        scratch_shapes=[pltpu.VMEM((256, D_MODEL), F32), pltpu.VMEM((D_MODEL, D_MODEL), F32)],
        compiler_params=_params(1))(dx2_or_target, t2, t1, p, ple_w, w_pg, b_pg, g1, b1, g2, b2, *extra)


def _post_bwd_b(dt1, m, z, hs, y0, w_out, w_glu, b_glu):
    L = dt1.shape[0]

    def body(dt1_ref, m_ref, z_ref, hs_ref, y0_ref, wo_ref, wg_ref, bg_ref,
             dhs_ref, dy0_ref, dzg_ref, dwo_out, dwg_out, dbg_ref, dwo_ref, dwg_ref):
        @pl.when(pl.program_id(0) == 0)
        def _():
            for ref in (dwo_ref, dwg_ref, dbg_ref):
                ref[...] = jnp.zeros_like(ref)

        dt1b = dt1_ref[...].astype(MXU)
        dm = _mm_nt(dt1b, wo_ref[...])
        dwo_ref[...] += _mm_tn(m_ref[...], dt1b)
        d_rgy, d_s5y = dm[:, :RG_W], dm[:, RG_W:]
        rg_gate = z_ref[:, C_RGG:C_RGG + RG_W]
        s5_gate = z_ref[:, C_S5G:C_S5G + S5_W]
        sl, dsl = _silu_and_grad(rg_gate)
        dhs_ref[...] = d_rgy * sl
        dzg_ref[:, :RG_W] = d_rgy * hs_ref[...] * dsl
        y0 = y0_ref[...]
        y1 = _gelu(y0)
        gl = _sigmoid(_mm(y1, wg_ref[...]) + bg_ref[...])
        sl, dsl = _silu_and_grad(s5_gate)
        dy2 = d_s5y * sl
        dzg_ref[:, RG_W:] = d_s5y * (y1 * gl) * dsl
        dglpre = (dy2 * y1) * gl * (1.0 - gl)
        dwg_ref[...] += _mm_tn(y1, dglpre)
        dbg_ref[...] += _colsum(dglpre)
        dy1 = dy2 * gl + _mm_nt(dglpre, wg_ref[...])
        dy0_ref[...] = dy1 * _gelu_grad(y0)

        @pl.when(pl.program_id(0) == L // TM - 1)
        def _():
            dwo_out[...] = dwo_ref[...].astype(WIRE)
            dwg_out[...] = dwg_ref[...].astype(WIRE)

    return pl.pallas_call(
        body, name="post_bwd_b", grid=(L // TM,),
        in_specs=[_row(D_MODEL), _row(D_MODEL), _row(Z_W), _row(RG_W), _row(S5_W), _full((D_MODEL, D_MODEL)),
                  _full((S5_W, S5_W)), _full((1, S5_W))],
        out_specs=[_row(RG_W), _row(S5_W), _row(D_MODEL), _full((D_MODEL, D_MODEL)), _full((S5_W, S5_W)), _full((1, S5_W))],
        out_shape=[_S((L, RG_W)), _S((L, S5_W)), _S((L, D_MODEL)), _S((D_MODEL, D_MODEL), WIRE), _S((S5_W, S5_W), WIRE),
                   _S((1, S5_W))],
        scratch_shapes=[pltpu.VMEM((D_MODEL, D_MODEL), F32), pltpu.VMEM((S5_W, S5_W), F32)],
        compiler_params=_params(1))(dt1, m, z, hs, y0, w_out, w_glu, b_glu)


def _adamw(parts, w, m, v):
    nl = len(parts)
    n, R, C = parts[0].shape
    tr = R
    for cand in (512, 256, 128, 64, 32, 16, 8):
        if R % cand == 0 and n * cand * C * 4 <= 4 * 1024 * 1024:
            tr = cand
            break
    nblk = R // tr

    def body(*refs):
        p_refs = refs[:nl]
        w_ref, m_ref, v_ref, g_ref, d_ref, nm_ref, nv_ref = refs[nl:]
        layer = pl.program_id(0)
        g = None
        for li, p_ref in enumerate(p_refs):
            s = p_ref[0].astype(F32)
            for k in range(1, n):
                s = s + p_ref[k].astype(F32)
            g = s if g is None else jnp.where(layer == li, s, g)
        nm = B1 * m_ref[...] + (1.0 - B1) * g
        nv = B2 * v_ref[...] + (1.0 - B2) * (g * g)
        d_ref[...] = (-LR) * ((nm / BC1) / (jnp.sqrt(nv / BC2) + EPS) + WD * w_ref[...])
        g_ref[...], nm_ref[...], nv_ref[...] = g, nm, nv

    def part_spec(li):
        return pl.BlockSpec((n, tr, C), lambda l, i: (0, jnp.where(l == li, i, jnp.where(l < li, 0, nblk - 1)), 0))

    blk = pl.BlockSpec((tr, C), lambda l, i: (l * nblk + i, 0))
    return pl.pallas_call(
        body, name="adamw", grid=(nl, nblk),
        in_specs=[part_spec(li) for li in range(nl)] + [blk, blk, blk],
        out_specs=[blk] * 4, out_shape=[_S((nl * R, C))] * 4, compiler_params=_params(2))(*parts, w, m, v)


def _me():
    return lax.axis_index("x"), lax.axis_index("y"), lax.axis_index("c")


def _lin(dev):
    return 4 * dev[0] + 2 * dev[1] + dev[2]


def _blk(ref, axis, size, idx):
    nd = len(ref.shape)
    start = idx * size
    if axis == nd - 1 and size % LANE == 0:
        start = pl.multiple_of(start, LANE)
    elif axis == nd - 2 and size % 16 == 0:
        start = pl.multiple_of(start, 16)
    ix = [slice(None)] * nd
    ix[axis] = pl.ds(start, size)
    return ref.at[tuple(ix)]


def _all_gather(shards, axes, name):
    n = len(shards)
    sizes = [s.shape[a] for s, a in zip(shards, axes)]
    out_shapes = [_S(s.shape[:a] + (N_DEV * s.shape[a],) + s.shape[a + 1:], s.dtype) for s, a in zip(shards, axes)]

    def body(*refs):
        ins, outs = refs[:n], refs[n:2 * n]
        send_sems, recv_sems, local_sems = refs[2 * n:]
        x, y, c = _me()
        me, sibling = (x, y, c), (x, y, 1 - c)
        chips = [(1 - x, y), (x, 1 - y), (1 - x, 1 - y)]

        def copy(a, k, block, to, from_input=False):
            dst = _blk(outs[a], axes[a], sizes[a], _lin(block))
            return pltpu.make_async_remote_copy(
                src_ref=ins[a] if from_input else dst, dst_ref=dst, send_sem=send_sems.at[a, k],
                recv_sem=recv_sems.at[a, k], device_id=to, device_id_type=MESH)

        mine = [pltpu.make_async_copy(ins[a], _blk(outs[a], axes[a], sizes[a], _lin(me)), local_sems.at[a]) for a in range(n)]
        for cp in mine:
            cp.start()
        first = []
        for a in range(n):
            first.append(copy(a, 0, me, sibling, True))
            first += [copy(a, 1 + j, me, (*chip, c), True) for j, chip in enumerate(chips)]
        for cp in first:
            cp.start()
        passed = []
        for j, chip in enumerate(chips):
            for a in range(n):
                copy(a, 1 + j, (*chip, c), me).wait_recv()
                cp = copy(a, 4 + j, (*chip, c), sibling)
                cp.start()
                passed.append(cp)
        for a in range(n):
            copy(a, 0, sibling, me).wait_recv()
            for j, chip in enumerate(chips):
                copy(a, 4 + j, (*chip, 1 - c), me).wait_recv()
        for cp in first + passed:
            cp.wait_send()
        for cp in mine:
            cp.wait()

    return pl.pallas_call(
        body, name=name, out_shape=out_shapes, in_specs=[ANY] * n, out_specs=[ANY] * n,
        scratch_shapes=[pltpu.SemaphoreType.DMA((n, 7)), pltpu.SemaphoreType.DMA((n, 7)), pltpu.SemaphoreType.DMA((n,))],
    )(*shards)


def _exchange(groups, axes, name):
    arrays = [a for g in groups for a in g]
    where = [(o, i) for o, g in enumerate(groups) for i in range(len(g))]
    ax = [axes[o] for o, _ in where]
    n = len(arrays)
    sizes = [s.shape[a] // N_DEV for s, a in zip(arrays, ax)]
    out_shapes = []
    for g, a in zip(groups, axes):
        s = g[0].shape
        out_shapes.append(_S((N_DEV, len(g)) + s[:a] + (s[a] // N_DEV,) + s[a + 1:], g[0].dtype))

    def body(*refs):
        ins, outs = refs[:n], refs[n:n + len(groups)]
        send_sems, recv_sems, local_sems = refs[n + len(groups):]
        x, y, c = _me()
        me = (x, y, c)
        flip = lambda v, f: 1 - v if f else v
        peers = [(flip(x, k & 4), flip(y, k & 2), flip(c, k & 1)) for k in range(1, N_DEV)]

        def land(a, sender):
            o, i = where[a]
            return outs[o].at[_lin(sender), i]

        def copy(a, k, to):
            return pltpu.make_async_remote_copy(
                src_ref=_blk(ins[a], ax[a], sizes[a], _lin(to)), dst_ref=land(a, me),
                send_sem=send_sems.at[a, k], recv_sem=recv_sems.at[a, k], device_id=to, device_id_type=MESH)

        mine = [pltpu.make_async_copy(_blk(ins[a], ax[a], sizes[a], _lin(me)), land(a, me), local_sems.at[a]) for a in range(n)]
        for cp in mine:
            cp.start()
        sends = [copy(a, k, peer) for a in range(n) for k, peer in enumerate(peers)]
        for cp in sends:
            cp.start()
        for a in range(n):
            for k, peer in enumerate(peers):
                pltpu.make_async_remote_copy(
                    src_ref=land(a, peer), dst_ref=land(a, peer), send_sem=send_sems.at[a, k],
                    recv_sem=recv_sems.at[a, k], device_id=peer, device_id_type=MESH).wait_recv()
        for cp in sends:
            cp.wait_send()
        for cp in mine:
            cp.wait()

    return pl.pallas_call(
        body, name=name, out_shape=out_shapes, in_specs=[ANY] * n, out_specs=[ANY] * len(groups),
        scratch_shapes=[pltpu.SemaphoreType.DMA((n, 7)), pltpu.SemaphoreType.DMA((n, 7)), pltpu.SemaphoreType.DMA((n,))],
    )(*arrays)


HBM_SPEC = pl.BlockSpec(memory_space=pltpu.HBM)
SEM_SPEC = pl.BlockSpec(memory_space=pltpu.SEMAPHORE)
EFFECT = pltpu.SideEffectType.DATAFLOW_SIDE_EFFECTING


def _peers(x, y, c):
    flip = lambda v, f: 1 - v if f else v
    return [(flip(x, k & 4), flip(y, k & 2), flip(c, k & 1)) for k in range(1, N_DEV)]


def _land_shape(mode, s, axis):
    if mode == "gather":
        return s.shape[:axis] + (N_DEV * s.shape[axis],) + s.shape[axis + 1:]
    return (N_DEV,) + s.shape[:axis] + (s.shape[axis] // N_DEV,) + s.shape[axis + 1:]


def _src_view(mode, ref, axis, peer):
    return ref if mode == "gather" else _blk(ref, axis, ref.shape[axis] // N_DEV, peer)


def _dst_view(mode, land, axis, sender):
    return _blk(land, axis, land.shape[axis] // N_DEV, sender) if mode == "gather" else land.at[sender]


def _seven_blocks(mode, land, axis):
    if mode == "gather":
        ix = [slice(None)] * len(land.shape)
        ix[axis] = pl.ds(0, (N_DEV - 1) * (land.shape[axis] // N_DEV))
        return land.at[tuple(ix)]
    return land.at[pl.ds(0, N_DEV - 1)]


def _place_own(mode, srcs, axes, name):
    n = len(srcs)

    def body(*refs):
        me = _lin(_me())
        cps = [pltpu.make_async_copy(_src_view(mode, refs[a], axes[a], me), _dst_view(mode, refs[n + a], axes[a], me),
                                     refs[2 * n].at[a]) for a in range(n)]
        for cp in cps:
            cp.start()
        for cp in cps:
            cp.wait()

    return pl.pallas_call(
        body, name=name, out_shape=[_S(_land_shape(mode, s, a), s.dtype) for s, a in zip(srcs, axes)],
        in_specs=[ANY] * n, out_specs=[ANY] * n, scratch_shapes=[pltpu.SemaphoreType.DMA((n,))])(*srcs)


def _push_start(mode, srcs, lands, axes, name):
    n = len(srcs)

    def body(*refs):
        src_refs, land_refs = refs[:n], refs[n:2 * n]
        send_sems, recv_sems = refs[2 * n], refs[2 * n + 1]
        token = refs[-1]
        x, y, c = _me()
        me = _lin((x, y, c))
        for a in range(n):
            for peer in _peers(x, y, c):
                pltpu.make_async_remote_copy(
                    src_ref=_src_view(mode, src_refs[a], axes[a], _lin(peer)),
                    dst_ref=_dst_view(mode, land_refs[a], axes[a], me),
                    send_sem=send_sems.at[a], recv_sem=recv_sems.at[a], device_id=peer, device_id_type=MESH).start()
        token[...] = jnp.zeros_like(token)

    hbm = lambda s: pltpu.HBM(s.shape, s.dtype)
    outs = pl.pallas_call(
        body, name=name,
        out_shape=(pltpu.SemaphoreType.DMA((n,)), pltpu.SemaphoreType.DMA((n,)), *[hbm(s) for s in srcs], *[hbm(s) for s in lands],
                   _S((SUB, LANE))),
        in_specs=[HBM_SPEC] * (2 * n),
        out_specs=(SEM_SPEC, SEM_SPEC, *[HBM_SPEC] * (2 * n), pl.BlockSpec(memory_space=pltpu.VMEM)),
        input_output_aliases={i: 2 + i for i in range(2 * n)},
        compiler_params=pltpu.CompilerParams(has_side_effects=EFFECT),
    )(*[pltpu.with_memory_space_constraint(s, pltpu.HBM) for s in list(srcs) + list(lands)])
    return outs[0], outs[1], outs[2:2 + n], outs[2 + n:2 + 2 * n], outs[-1]


def _push_wait(mode, send_sems, recv_sems, srcs, lands, axes, after, name):
    n = len(srcs)

    def body(*refs):
        land_refs = refs[n:2 * n]
        send_sems, recv_sems = refs[2 * n], refs[2 * n + 1]
        x, y, c = _me()
        for a in range(n):
            seven = _seven_blocks(mode, land_refs[a], axes[a])
            cp = pltpu.make_async_remote_copy(src_ref=seven, dst_ref=seven, send_sem=send_sems.at[a], recv_sem=recv_sems.at[a],
                                              device_id=(x, y, 1 - c), device_id_type=MESH)
            cp.wait_send()
            cp.wait_recv()

    hbm = lambda s: pltpu.HBM(s.shape, s.dtype)
    outs = pl.pallas_call(
        body, name=name, out_shape=tuple(hbm(s) for s in list(srcs) + list(lands)),
        in_specs=[HBM_SPEC] * (2 * n) + [SEM_SPEC, SEM_SPEC, ANY], out_specs=tuple([HBM_SPEC] * (2 * n)),
        input_output_aliases={i: i for i in range(2 * n)},
        compiler_params=pltpu.CompilerParams(has_side_effects=EFFECT),
    )(*srcs, *lands, send_sems, recv_sems, after)
    return outs[n:]


def _sum_parts(parts):
    n, R, C = parts.shape

    def body(p_ref, o_ref):
        g = p_ref[0]
        for k in range(1, n):
            g = g + p_ref[k]
        o_ref[...] = g

    return pl.pallas_call(body, name="sum_parts", out_shape=_S((R, C)))(parts)


def _block_diag(w, nb):
    tn, r, c = w.shape
    w = w.reshape(tn // nb, nb, r, c)
    return jnp.einsum('tarc,ab->tarbc', w, jnp.eye(nb, dtype=w.dtype)).reshape(tn // nb, nb * r, nb * c)


def _block_diag_extract(w, nb):
    t, R, C = w.shape
    w = w.reshape(t, nb, R // nb, nb, C // nb)
    return jnp.einsum('tarbc,ab->tarc', w, jnp.eye(nb, dtype=w.dtype)).reshape(t * nb, R // nb, C // nb)


SMALL = ['conv_b', 'rg_wa', 'rg_ba', 'rg_wx', 'rg_bx', 'rg_lambda', 's5_a_re', 's5_a_im', 's5_b_re', 's5_b_im',
         's5_c_re', 's5_c_im', 's5_d', 's5_log_step', 's5_b_glu', 'ln1_g', 'ln1_b', 'ple_gate_b', 'ln2_g', 'ln2_b']
WEIGHTS = ['w_in', 'conv_w', 'conv_b', 'rg_wa', 'rg_ba', 'rg_wx', 'rg_bx', 'rg_lambda', 's5_a_re', 's5_a_im', 's5_b_re',
           's5_b_im', 's5_c_re', 's5_c_im', 's5_d', 's5_log_step', 's5_w_glu', 's5_b_glu', 'w_out', 'ln1_g', 'ln1_b',
           'ple_w', 'ple_gate_w', 'ple_gate_b', 'ln2_g', 'ln2_b']
PACK_ROWS_MULT = 64


def _pack(tree):
    flat = jnp.concatenate([tree[k].reshape(-1) for k in SMALL])
    rows = -(-flat.shape[0] // (LANE * PACK_ROWS_MULT)) * PACK_ROWS_MULT
    return jnp.pad(flat, (0, rows * LANE - flat.shape[0])).reshape(rows, LANE)


def _unpack(packed, like):
    flat, out, o = packed.reshape(-1), {}, 0
    for k in SMALL:
        n = math.prod(like[k].shape)
        out[k] = flat[o:o + n].reshape(like[k].shape)
        o += n
    return out


class _NoHooks:
    token = None

    def late_weights(self, W, after):
        return W

    def layer_done(self, i, g, dx):
        return None


def _local_grads(x, p, target, W, disc, hooks):
    depth = 2
    saved = []
    for i in range(depth):
        w = W[i]
        z = _inproj_fwd(x, w['w_in'], hooks.token if i == 0 else None)
        hs = _rg_fwd(z, w['conv_w'], w['conv_b'], w['wa_bd'], w['wx_bd'], w['rg_ba'], w['rg_bx'], w['rg_lambda'])
        d = disc[i]
        y0, s_re, s_im = _s5_fwd(z, d['bb_re'], d['bb_im'], d['lb_re'], d['lb_im'], d['c_re'], d['c_im'], w['s5_d'])
        if i == 0:
            W = hooks.late_weights(W, y0)
            w = W[i]
        x2, t1, t2, m = _post_fwd(x, hs, z, y0, p[i], w['s5_w_glu'], w['s5_b_glu'], w['w_out'], w['ln1_g'], w['ln1_b'],
                                  w['ple_w'], w['ple_gate_w'], w['ple_gate_b'], w['ln2_g'], w['ln2_b'])
        saved.append((x, z, hs, y0, s_re, s_im, t1, t2, m))
        x = x2

    grads = [None] * depth
    dx = target
    loss = None
    token = None
    for i in reversed(range(depth)):
        w, d = W[i], disc[i]
        xin, z, hs, y0, s_re, s_im, t1, t2, m = saved[i]
        g = {}
        (dt1, g['ple_w'], g['ple_gate_w'], g['ple_gate_b'], g['ln1_g'], g['ln1_b'], g['ln2_g'], g['ln2_b'], lrow) = _post_bwd_a(
            dx, i == depth - 1, t2, t1, p[i], w['ple_w'], w['ple_gate_w'], w['ple_gate_b'], w['ln1_g'], w['ln1_b'],
            w['ln2_g'], w['ln2_b'], token)
        if i == depth - 1:
            loss = 0.5 / D_MODEL * jnp.sum(lrow)
        dhs, dy0, dzg, g['w_out'], g['s5_w_glu'], g['s5_b_glu'] = _post_bwd_b(dt1, m, z, hs, y0, w['w_out'], w['s5_w_glu'],
                                                                           w['s5_b_glu'])
        (dzu, g['bb_re'], g['bb_im'], g['lb_re'], g['lb_im'], g['c_re'], g['c_im'], g['s5_d']) = _s5_bwd(
            dy0, z, s_re, s_im, d['bb_re'], d['bb_im'], d['lb_re'], d['lb_im'], d['c_re'], d['c_im'], w['s5_d'])
        (dzx, g['conv_w'], g['conv_b'], g['wa_bd'], g['wx_bd'], g['rg_ba'], g['rg_bx'], g['rg_lambda']) = _rg_bwd(
            dhs, z, hs, w['conv_w'], w['conv_b'], w['wa_bd'], w['wx_bd'], w['rg_ba'], w['rg_bx'], w['rg_lambda'])
        dx, g['w_in'] = _inproj_bwd(dt1, xin, dzx, dzg, dzu, w['w_in'])
        grads[i] = g
        token = hooks.layer_done(i, g, dx)
    return loss, dx, grads


def _s5_layouts_fwd(s5_a_re, s5_a_im, s5_log_step, s5_b_re, s5_b_im, s5_c_re, s5_c_im):
    depth = s5_a_re.shape[0]
    ar, ai = s5_a_re.reshape(depth * 24, S5_P), s5_a_im.reshape(depth * 24, S5_P)
    ls = s5_log_step.reshape(depth * 24, 1)
    lr, li, cr, ci = _s5_disc_fwd(ar, ai, ls)
    col = lambda a: a.reshape(depth * S5_N, 1)
    br, bi = s5_b_re.reshape(depth * S5_N, 16), s5_b_im.reshape(depth * S5_N, 16)
    bbr, bbi = _s5_bscale_fwd(col(cr), col(ci), br, bi)
    disc = []
    for i in range(depth):
        gph = lambda a: a.reshape(depth, 24, S5_P, 16)[i]
        disc.append(dict(
            bb_re=_block_diag(jnp.swapaxes(gph(bbr), 1, 2), 8), bb_im=_block_diag(jnp.swapaxes(gph(bbi), 1, 2), 8),
            lb_re=lr.reshape(depth, 1, S5_N)[i], lb_im=li.reshape(depth, 1, S5_N)[i],
            c_re=_block_diag(jnp.swapaxes(s5_c_re[i], 1, 2), 8), c_im=_block_diag(jnp.swapaxes(s5_c_im[i], 1, 2), 8)))
    return disc, (ar, ai, ls, col(cr), col(ci), br, bi)


def _s5_layouts_bwd(grads, res):
    ar, ai, ls, cr, ci, br, bi = res
    depth = len(grads)
    stack = lambda f: jnp.stack([f(g) for g in grads])
    dbbr = stack(lambda g: jnp.swapaxes(_block_diag_extract(g['bb_re'], 8), 1, 2)).reshape(depth * S5_N, 16)
    dbbi = stack(lambda g: jnp.swapaxes(_block_diag_extract(g['bb_im'], 8), 1, 2)).reshape(depth * S5_N, 16)
    dbr, dbi, dcr, dci = _s5_bscale_bwd(cr, ci, br, bi, dbbr, dbbi)
    gp = lambda a: a.reshape(depth * 24, S5_P)
    dar, dai, dls = _s5_disc_bwd(ar, ai, ls, gp(stack(lambda g: g['lb_re'])), gp(stack(lambda g: g['lb_im'])), gp(dcr), gp(dci))
    return dict(
        s5_a_re=dar.reshape(depth, 24, S5_P), s5_a_im=dai.reshape(depth, 24, S5_P), s5_log_step=dls.reshape(depth, 24),
        s5_b_re=dbr.reshape(depth, 24, S5_P, 16), s5_b_im=dbi.reshape(depth, 24, S5_P, 16),
        s5_c_re=stack(lambda g: jnp.swapaxes(_block_diag_extract(g['c_re'], 8), 1, 2)),
        s5_c_im=stack(lambda g: jnp.swapaxes(_block_diag_extract(g['c_im'], 8), 1, 2)))


LATE = ('w_out', 'ple_w', 'ple_gate_w', 's5_w_glu')


def _layer_weights(full, i):
    row = lambda a: a[i].reshape(1, -1)
    return dict(
        w_in=full['w_in'][i], conv_w=full['conv_w'][i], conv_b=row(full['conv_b']),
        wa_bd=_block_diag(full['rg_wa'][i], 2), wx_bd=_block_diag(full['rg_wx'][i], 2),
        rg_ba=row(full['rg_ba']), rg_bx=row(full['rg_bx']), rg_lambda=row(full['rg_lambda']),
        s5_d=row(full['s5_d']), s5_b_glu=row(full['s5_b_glu']), ln1_g=row(full['ln1_g']), ln1_b=row(full['ln1_b']),
        ple_gate_b=row(full['ple_gate_b']), ln2_g=row(full['ln2_g']), ln2_b=row(full['ln2_b']))


class _AllLocal(_NoHooks):
    def __init__(self, full):
        self.full = full

    def late_weights(self, W, after):
        for i, w in enumerate(W):
            w.update({k: self.full[k][i] for k in LATE})
        return W


def _full_grads(full, x, p, target, hooks=None):
    disc, res = _s5_layouts_fwd(full['s5_a_re'], full['s5_a_im'], full['s5_log_step'], full['s5_b_re'], full['s5_b_im'],
                                full['s5_c_re'], full['s5_c_im'])
    W = [_layer_weights(full, i) for i in range(2)]
    loss, gx, grads = _local_grads(x, p, target, W, disc, hooks or _AllLocal(full))
    stack = lambda f: jnp.stack([f(g) for g in grads])
    out = _s5_layouts_bwd(grads, res)
    for k in SHARD_AXIS:
        out[k] = [g[k] for g in grads]
    out['conv_w'] = stack(lambda g: g['conv_w'])
    for k in ('conv_b', 'rg_ba', 'rg_bx', 'rg_lambda', 's5_b_glu', 'ln1_g', 'ln1_b', 'ple_gate_b', 'ln2_g', 'ln2_b'):
        out[k] = stack(lambda g: g[k][0])
    out['s5_d'] = stack(lambda g: g['s5_d'][0]).reshape(2, 24, 16)
    out['rg_wa'] = stack(lambda g: _block_diag_extract(g['wa_bd'], 2))
    out['rg_wx'] = stack(lambda g: _block_diag_extract(g['wx_bd'], 2))
    return loss, gx, out


SHARD_AXIS = {'w_in': 2, 'w_out': 1, 'ple_w': 2, 'ple_gate_w': 1, 's5_w_glu': 1}


def kernel(x, p, w_in, conv_w, conv_b, rg_wa, rg_ba, rg_wx, rg_bx, rg_lambda, s5_a_re, s5_a_im, s5_b_re, s5_b_im, s5_c_re, s5_c_im, s5_d, s5_log_step, s5_w_glu, s5_b_glu, w_out, ln1_g, ln1_b, ple_w, ple_gate_w, ple_gate_b, ln2_g, ln2_b, loss_target, m_w_in, m_conv_w, m_conv_b, m_rg_wa, m_rg_ba, m_rg_wx, m_rg_bx, m_rg_lambda, m_s5_a_re, m_s5_a_im, m_s5_b_re, m_s5_b_im, m_s5_c_re, m_s5_c_im, m_s5_d, m_s5_log_step, m_s5_w_glu, m_s5_b_glu, m_w_out, m_ln1_g, m_ln1_b, m_ple_w, m_ple_gate_w, m_ple_gate_b, m_ln2_g, m_ln2_b, v_w_in, v_conv_w, v_conv_b, v_rg_wa, v_rg_ba, v_rg_wx, v_rg_bx, v_rg_lambda, v_s5_a_re, v_s5_a_im, v_s5_b_re, v_s5_b_im, v_s5_c_re, v_s5_c_im, v_s5_d, v_s5_log_step, v_s5_w_glu, v_s5_b_glu, v_w_out, v_ln1_g, v_ln1_b, v_ple_w, v_ple_gate_w, v_ple_gate_b, v_ln2_g, v_ln2_b):
    local = dict(locals())
    w = {k: local[k] for k in WEIGHTS}
    mom = {k: local['m_' + k] for k in WEIGHTS}
    var = {k: local['v_' + k] for k in WEIGHTS}

    big = list(SHARD_AXIS)
    first = _all_gather([w_in[0].astype(WIRE), conv_w[None]], [1, 0], "gather_first_weights")
    rest_src = [w_in[1].astype(WIRE)] + [w[k].astype(WIRE) for k in LATE]
    rest_axes = [1] + [SHARD_AXIS[k] for k in LATE]
    rest = _push_start("gather", rest_src, _place_own("gather", rest_src, rest_axes, "place_weights"), rest_axes, "push_weights")
    grad_axes = [SHARD_AXIS[k] - 1 for k in big]

    class Overlap(_NoHooks):
        token = rest[4]

        def late_weights(self, W, after):
            lands = _push_wait("gather", rest[0], rest[1], rest[2], rest[3], rest_axes, after, "await_weights")
            W[1]['w_in'] = lands[0]
            for i, wl in enumerate(W):
                wl.update({k: land[i] for k, land in zip(LATE, lands[1:])})
            return W

        def layer_done(self, i, g, dx):
            if i != 1:
                return None
            srcs = [g[k] for k in big]
            self.pushed = _push_start("scatter", srcs, _place_own("scatter", srcs, grad_axes, "place_grads"), grad_axes,
                                      "push_grads")
            return self.pushed[4]

    hooks = Overlap()
    full = dict(w)
    full['w_in'] = [first[0], None]
    full['conv_w'] = jnp.moveaxis(first[1], 0, 2).reshape(2, 4, RG_W)

    loss, grad_x, g = _full_grads(full, x[0], p[:, 0], loss_target[0], hooks)
    loss = lax.psum(loss, ("x", "y", "c"))

    pushed = hooks.pushed
    recv1 = _push_wait("scatter", pushed[0], pushed[1], pushed[2], pushed[3], grad_axes, grad_x, "await_grads")
    conv_blocks = jnp.moveaxis(g['conv_w'].reshape(2, 4, N_DEV, RG_W // N_DEV), 2, 0).reshape(N_DEV, 8, RG_W // N_DEV)
    packed = _pack(g)
    recv0 = _exchange([[g[k][0]] for k in big] + [[conv_blocks], [packed]], grad_axes + [0, 0], "exchange_grads")
    outs = {}
    for j, k in enumerate(big + ['conv_w']):
        shard = w[k].shape
        c = shard[-1]
        two = lambda a: a.reshape(-1, c)
        parts = [recv0[j].reshape(N_DEV, -1, c)] + ([recv1[j].reshape(N_DEV, -1, c)] if k in big else [])
        outs[k] = [o.reshape(shard) for o in _adamw(parts, two(w[k]), two(mom[k]), two(var[k]))]

    rows = packed.shape[0] // N_DEV
    mine = _sum_parts(recv0[-1].reshape(N_DEV, rows, LANE))
    summed = _all_gather([mine], [0], "gather_small_grads")[0]
    small = _adamw([summed[None]], _pack(w), _pack(mom), _pack(var))
    small = [_unpack(o, w) for o in small]
    for k in SMALL:
        outs[k] = [o[k] for o in small]

    res = [loss, grad_x[None]]
    for j in range(4):
        res += [outs[k][j] for k in WEIGHTS]
    return tuple(res)
```

```python
import functools
import math

import jax
import jax.numpy as jnp
from jax import lax
from jax.experimental import pallas as pl
from jax.experimental.pallas import tpu as pltpu

F32 = jnp.float32
MXU = jnp.bfloat16
WIRE = jnp.bfloat16

N_DEV = 8
D_MODEL = 1024
RG_W = 640
S5_W = 384
S5_P = 64
S5_N = 24 * S5_P
Z_W = 2 * RG_W + 2 * S5_W
C_RGG = RG_W
C_S5U = 2 * RG_W
C_S5G = 2 * RG_W + S5_W
LANE = 128
N_RG_T = RG_W // LANE
N_S5_T = S5_W // LANE
W_BLK = Z_W // N_DEV
ALPHA = (2.0 * 2) ** 0.25
LN_EPS = 1e-5
RG_C = 8.0
LR, B1, B2, EPS, WD, STEP = 0.001, 0.9, 0.999, 1e-08, 0.01, 10
BC1 = 1.0 - B1 ** STEP
BC2 = 1.0 - B2 ** STEP
RC = 256
TM = 256
VMEM_LIMIT = 56 * 1024 * 1024

MESH = pl.DeviceIdType.MESH
ANY = pl.BlockSpec(memory_space=pl.ANY)


def _params(n_grid_axes, vmem=VMEM_LIMIT):
    return pltpu.CompilerParams(dimension_semantics=("arbitrary",) * n_grid_axes, vmem_limit_bytes=vmem)


def _S(shape, dtype=F32):
    return jax.ShapeDtypeStruct(tuple(shape), dtype)


def _sigmoid(x):
    return 1.0 / (1.0 + jnp.exp(-x))


def _silu_and_grad(x):
    s = _sigmoid(x)
    return x * s, s * (1.0 + x * (1.0 - s))


_GELU_C = math.sqrt(2.0 / math.pi)


def _gelu(x):
    return 0.5 * x * (1.0 + jnp.tanh(_GELU_C * (x + 0.044715 * (x * x * x))))


def _gelu_grad(x):
    th = jnp.tanh(_GELU_C * (x + 0.044715 * (x * x * x)))
    return 0.5 * (1.0 + th) + 0.5 * x * (1.0 - th * th) * (_GELU_C * (1.0 + 3.0 * 0.044715 * (x * x)))


def _mm(a, b):
    return jnp.dot(a.astype(MXU), b.astype(MXU), preferred_element_type=F32)


def _mm_nt(a, b):
    return lax.dot_general(a.astype(MXU), b.astype(MXU), (((1,), (1,)), ((), ())), preferred_element_type=F32)


def _mm_tn(a, b):
    return lax.dot_general(a.astype(MXU), b.astype(MXU), (((0,), (0,)), ((), ())), preferred_element_type=F32)


def _ln_fwd(t, g, b):
    mu = jnp.mean(t, axis=-1, keepdims=True)
    tc = t - mu
    var = jnp.mean(tc * tc, axis=-1, keepdims=True)
    rstd = lax.rsqrt(var + LN_EPS)
    xhat = tc * rstd
    return xhat * g + b, xhat, rstd


def _ln_bwd(dy, xhat, rstd, g):
    dxh = dy * g
    m1 = jnp.mean(dxh, axis=-1, keepdims=True)
    m2 = jnp.mean(dxh * xhat, axis=-1, keepdims=True)
    return rstd * (dxh - m1 - xhat * m2)


def _colsum(a):
    return jnp.sum(a, axis=0, keepdims=True)


def _up(x, d, rows, fill):
    n = x.shape[0]
    return jnp.where(rows < n - d, pltpu.roll(x, n - d, 0), fill)


SUB = 8
TILE_STEPS = (1, 2, 4)


def _r8(width):
    return lax.broadcasted_iota(jnp.int32, (SUB, width), 0)


def _scan_real(a, u, carry, reverse=False):
    r8 = _r8(a.shape[1])
    n = a.shape[0] // SUB
    outs = [None] * n
    for k in (reversed(range(n)) if reverse else range(n)):
        A, U = a[SUB * k:SUB * k + SUB], u[SUB * k:SUB * k + SUB]
        for d in TILE_STEPS:
            m = (r8 < SUB - d) if reverse else (r8 >= d)
            sh = SUB - d if reverse else d
            U = A * jnp.where(m, pltpu.roll(U, sh, 0), 0.0) + U
            A = A * jnp.where(m, pltpu.roll(A, sh, 0), 1.0)
        h = A * carry + U
        outs[k] = h
        carry = h[0:1] if reverse else h[SUB - 1:SUB]
    return jnp.concatenate(outs, axis=0), carry


def _tile_powers(lr, li, reverse=False):
    width = lr.shape[1]
    r8 = _r8(width)
    steps = []
    pr, pi = lr, li
    er, ei = jnp.broadcast_to(lr, (SUB, width)), jnp.broadcast_to(li, (SUB, width))
    for d in TILE_STEPS:
        m = (r8 < SUB - d) if reverse else (r8 >= d)
        sh = SUB - d if reverse else d
        steps.append((sh, jnp.where(m, pr, 0.0), jnp.where(m, pi, 0.0)))
        er, ei = _cmul(er, ei, jnp.where(m, pltpu.roll(er, sh, 0), 1.0), jnp.where(m, pltpu.roll(ei, sh, 0), 0.0))
        pr, pi = _cmul(pr, pi, pr, pi)
    return steps, (er, ei)


def _scan_lti(xr, xi, carry, steps, e, reverse=False):
    er, ei = e
    kr, ki = carry
    n = xr.shape[0] // SUB
    outr, outi = [None] * n, [None] * n
    for k in (reversed(range(n)) if reverse else range(n)):
        sr, si = xr[SUB * k:SUB * k + SUB], xi[SUB * k:SUB * k + SUB]
        for sh, pr, pi in steps:
            shr, shi = pltpu.roll(sr, sh, 0), pltpu.roll(si, sh, 0)
            sr, si = sr + (pr * shr - pi * shi), si + (pr * shi + pi * shr)
        sr = sr + (er * kr - ei * ki)
        si = si + (er * ki + ei * kr)
        outr[k], outi[k] = sr, si
        kr, ki = (sr[0:1], si[0:1]) if reverse else (sr[SUB - 1:SUB], si[SUB - 1:SUB])
    return jnp.concatenate(outr, axis=0), jnp.concatenate(outi, axis=0), (kr, ki)


def _halo(ref, c, r0):
    rp = pl.multiple_of(jnp.maximum(r0 - 8, 0), 8)
    return jnp.where(c > 0, ref[pl.ds(rp, 8), :], 0.0)


def _conv_taps(xe):
    return [pltpu.roll(xe, 3, 0)[8:, :], pltpu.roll(xe, 2, 0)[8:, :], pltpu.roll(xe, 1, 0)[8:, :], xe[8:, :]]


def _rg_gates(h, wa, wx, ba, bx, sp):
    r = _sigmoid(_mm(h, wa) + ba)
    i = _sigmoid(_mm(h, wx) + bx)
    log_a = (-RG_C) * r * sp
    a = jnp.exp(log_a)
    mult = jnp.sqrt(-jnp.tanh(log_a) * (a * a + 1.0))
    return r, i, a, mult


def _softplus(y):
    return jnp.maximum(y, 0.0) + jnp.log1p(jnp.exp(-jnp.abs(y)))


def _after(token):
    return ([], []) if token is None else ([token], [ANY])


def _inproj_fwd(x, w_in, token=None):
    L = x.shape[0]

    def body(x_ref, w_ref, *rest):
        xb = x_ref[...].astype(MXU)
        for j in range(N_DEV):
            rest[-1][:, j * W_BLK:(j + 1) * W_BLK] = jnp.dot(xb, w_ref[j].astype(MXU), preferred_element_type=F32)

    extra, extra_specs = _after(token)
    return pl.pallas_call(
        body, name="inproj_fwd", grid=(L // TM,),
        in_specs=[pl.BlockSpec((TM, D_MODEL), lambda i: (i, 0)),
                  pl.BlockSpec((N_DEV, D_MODEL, W_BLK), lambda i: (0, 0, 0))] + extra_specs,
        out_specs=pl.BlockSpec((TM, Z_W), lambda i: (i, 0)),
        out_shape=_S((L, Z_W)), compiler_params=_params(1))(x, w_in, *extra)


def _inproj_bwd(dt1, x, dzx, dzg, dzu, w_in):
    L = x.shape[0]

    def body(dt1_ref, x_ref, dzx_ref, dzg_ref, dzu_ref, w_ref, dx_ref, dw_ref, acc_ref):
        @pl.when(pl.program_id(0) == 0)
        def _():
            acc_ref[...] = jnp.zeros_like(acc_ref)
        dzg = dzg_ref[...]
        dz = jnp.concatenate([dzx_ref[...], dzg[:, :RG_W], dzu_ref[...], dzg[:, RG_W:]], axis=1).astype(MXU)
        xb = x_ref[...].astype(MXU)
        dx = ALPHA * dt1_ref[...]
        for j in range(N_DEV):
            dzj = dz[:, j * W_BLK:(j + 1) * W_BLK]
            dx = dx + _mm_nt(dzj, w_ref[j])
            acc_ref[j] += _mm_tn(xb, dzj)
        dx_ref[...] = dx

        @pl.when(pl.program_id(0) == L // TM - 1)
        def _():
            dw_ref[...] = acc_ref[...].astype(WIRE)

    row = lambda w: pl.BlockSpec((TM, w), lambda i: (i, 0))
    wspec = pl.BlockSpec((N_DEV, D_MODEL, W_BLK), lambda i: (0, 0, 0))
    return pl.pallas_call(
        body, name="inproj_bwd", grid=(L // TM,),
        in_specs=[row(D_MODEL), row(D_MODEL), row(RG_W), row(D_MODEL), row(S5_W), wspec],
        out_specs=[row(D_MODEL), wspec],
        out_shape=[_S((L, D_MODEL)), _S((N_DEV, D_MODEL, W_BLK), WIRE)],
        scratch_shapes=[pltpu.VMEM((N_DEV, D_MODEL, W_BLK), F32)],
        compiler_params=_params(1))(dt1, x, dzx, dzg, dzu, w_in)


def _rg_specs(L):
    tile = lambda rows: pl.BlockSpec((rows, LANE), lambda c: (0, c))
    return tile, pl.BlockSpec((None, LANE, LANE), lambda c: (c, 0, 0))


def _rg_fwd(z, cw, cb, wa_bd, wx_bd, ba, bx, lam):
    L = z.shape[0]

    def body(x_ref, cw_ref, cb_ref, wa_ref, wx_ref, ba_ref, bx_ref, lam_ref, hs_ref):
        w, b = cw_ref[...], cb_ref[...]
        wa, wx, ba_, bx_ = wa_ref[...].astype(MXU), wx_ref[...].astype(MXU), ba_ref[...], bx_ref[...]
        sp = _softplus(-lam_ref[...])

        def step(c, carry):
            r0 = pl.multiple_of(c * RC, RC)
            xe = jnp.concatenate([_halo(x_ref, c, r0), x_ref[pl.ds(r0, RC), :]], axis=0)
            t = _conv_taps(xe)
            h = t[0] * w[0:1] + t[1] * w[1:2] + t[2] * w[2:3] + t[3] * w[3:4] + b
            _, i, a, mult = _rg_gates(h, wa, wx, ba_, bx_, sp)
            hs, carry = _scan_real(a, mult * (i * h), carry)
            hs_ref[pl.ds(r0, RC), :] = hs
            return carry

        lax.fori_loop(0, L // RC, step, jnp.zeros((1, LANE), F32))

    tile, bd = _rg_specs(L)
    return pl.pallas_call(
        body, name="rg_fwd", grid=(N_RG_T,),
        in_specs=[tile(L), tile(4), tile(1), bd, bd, tile(1), tile(1), tile(1)],
        out_specs=tile(L), out_shape=_S((L, RG_W)), compiler_params=_params(1))(z, cw, cb, wa_bd, wx_bd, ba, bx, lam)


def _rg_bwd(dhs, z, hs, cw, cb, wa_bd, wx_bd, ba, bx, lam):
    L = z.shape[0]

    def body(g_ref, x_ref, hs_ref, cw_ref, cb_ref, wa_ref, wx_ref, ba_ref, bx_ref, lam_ref,
             dx_ref, dcw_ref, dcb_ref, dwa_ref, dwx_ref, dba_ref, dbx_ref, dlam_ref):
        w, b = cw_ref[...], cb_ref[...]
        wa, wx, ba_, bx_ = wa_ref[...].astype(MXU), wx_ref[...].astype(MXU), ba_ref[...], bx_ref[...]
        lam = lam_ref[...]
        sp = _softplus(-lam)
        rows = lax.broadcasted_iota(jnp.int32, (RC, LANE), 0)
        for ref in (dcw_ref, dcb_ref, dwa_ref, dwx_ref, dba_ref, dbx_ref, dlam_ref):
            ref[...] = jnp.zeros_like(ref)
        nch = L // RC

        def step(k, carry):
            cin, nxt = carry
            c = nch - 1 - k
            r0 = pl.multiple_of(c * RC, RC)
            xe = jnp.concatenate([_halo(x_ref, c, r0), x_ref[pl.ds(r0, RC), :]], axis=0)
            t = _conv_taps(xe)
            h = t[0] * w[0:1] + t[1] * w[1:2] + t[2] * w[2:3] + t[3] * w[3:4] + b
            r, i, a, mult = _rg_gates(h, wa, wx, ba_, bx_, sp)
            hs_e = jnp.concatenate([_halo(hs_ref, c, r0), hs_ref[pl.ds(r0, RC), :]], axis=0)
            hs_prev = pltpu.roll(hs_e, 1, 0)[8:, :]
            g = g_ref[pl.ds(r0, RC), :]
            cc, cin_new = _scan_real(a, a * g, cin, reverse=True)
            dh = g + _up(cc, 1, rows, cin)
            ih = i * h
            dlog_a = dh * hs_prev * a - (dh * ih) * (a * a) / mult
            di = dh * mult * h
            dhin = dh * mult * i
            dr = dlog_a * ((-RG_C) * sp)
            dlam_ref[...] += _colsum(dlog_a * r)
            dra = dr * r * (1.0 - r)
            dia = di * i * (1.0 - i)
            dwa_ref[...] += _mm_tn(h, dra)
            dwx_ref[...] += _mm_tn(h, dia)
            dba_ref[...] += _colsum(dra)
            dbx_ref[...] += _colsum(dia)
            dhin = dhin + _mm_nt(dra, wa) + _mm_nt(dia, wx)
            de = jnp.concatenate([dhin, nxt], axis=0)
            n = RC + 8
            dx = (dhin * w[3:4] + pltpu.roll(de, n - 1, 0)[:RC, :] * w[2:3]
                  + pltpu.roll(de, n - 2, 0)[:RC, :] * w[1:2] + pltpu.roll(de, n - 3, 0)[:RC, :] * w[0:1])
            dx_ref[pl.ds(r0, RC), :] = dx
            for kk in range(4):
                dcw_ref[kk:kk + 1, :] += _colsum(dhin * t[kk])
            dcb_ref[...] += _colsum(dhin)
            return cin_new, dhin[0:8, :]

        lax.fori_loop(0, nch, step, (jnp.zeros((1, LANE), F32), jnp.zeros((8, LANE), F32)))
        dlam_ref[...] = dlam_ref[...] * (RG_C * _sigmoid(-lam))

    tile, bd = _rg_specs(L)
    return pl.pallas_call(
        body, name="rg_bwd", grid=(N_RG_T,),
        in_specs=[tile(L), tile(L), tile(L), tile(4), tile(1), bd, bd, tile(1), tile(1), tile(1)],
        out_specs=[tile(L), tile(4), tile(1), bd, bd, tile(1), tile(1), tile(1)],
        out_shape=[_S((L, RG_W)), _S((4, RG_W)), _S((1, RG_W)), _S((N_RG_T, LANE, LANE)), _S((N_RG_T, LANE, LANE)),
                   _S((1, RG_W)), _S((1, RG_W)), _S((1, RG_W))],
        compiler_params=_params(1))(dhs, z, hs, cw, cb, wa_bd, wx_bd, ba, bx, lam)


def _cmul(ar, ai, br, bi):
    return ar * br - ai * bi, ar * bi + ai * br


S5_TW = S5_N // N_S5_T


def _s5_specs(L):
    in_tile = pl.BlockSpec((L, LANE), lambda t: (0, t))
    st = pl.BlockSpec((L, S5_TW), lambda t: (0, t))
    bb = pl.BlockSpec((None, LANE, S5_TW), lambda t: (t, 0, 0))
    cc = pl.BlockSpec((None, S5_TW, LANE), lambda t: (t, 0, 0))
    lb = pl.BlockSpec((1, S5_TW), lambda t: (0, t))
    dv = pl.BlockSpec((1, LANE), lambda t: (0, t))
    return in_tile, st, bb, cc, lb, dv


def _s5_fwd(z, bb_re, bb_im, lb_re, lb_im, c_re, c_im, dvec):
    L = z.shape[0]

    def body(u_ref, bbr_ref, bbi_ref, lr_ref, li_ref, cr_ref, ci_ref, d_ref, y_ref, sr_ref, si_ref):
        bbr, bbi = bbr_ref[...].astype(MXU), bbi_ref[...].astype(MXU)
        cr, ci = cr_ref[...].astype(MXU), ci_ref[...].astype(MXU)
        dv = d_ref[...]
        steps, e = _tile_powers(lr_ref[...], li_ref[...])

        def step(c, carry):
            r0 = pl.multiple_of(c * RC, RC)
            u = u_ref[pl.ds(r0, RC), :]
            ub = u.astype(MXU)
            sr = jnp.dot(ub, bbr, preferred_element_type=F32)
            si = jnp.dot(ub, bbi, preferred_element_type=F32)
            sr, si, carry = _scan_lti(sr, si, carry, steps, e)
            sr_ref[pl.ds(r0, RC), :] = sr
            si_ref[pl.ds(r0, RC), :] = si
            y_ref[pl.ds(r0, RC), :] = dv * u + (_mm(sr, cr) - _mm(si, ci))
            return carry

        zero = jnp.zeros((1, S5_TW), F32)
        lax.fori_loop(0, L // RC, step, (zero, zero))

    in_tile, st, bb, cc, lb, dv = _s5_specs(L)
    u_tile = pl.BlockSpec((L, LANE), lambda t: (0, C_S5U // LANE + t))
    return pl.pallas_call(
        body, name="s5_fwd", grid=(N_S5_T,),
        in_specs=[u_tile, bb, bb, lb, lb, cc, cc, dv],
        out_specs=[in_tile, st, st],
        out_shape=[_S((L, S5_W)), _S((L, S5_N)), _S((L, S5_N))],
        compiler_params=_params(1))(z, bb_re, bb_im, lb_re, lb_im, c_re, c_im, dvec)


def _s5_bwd(dy0, z, s_re, s_im, bb_re, bb_im, lb_re, lb_im, c_re, c_im, dvec, token=None):
    L = z.shape[0]
    extra, extra_specs = _after(token)

    def body(dy_ref, u_ref, sr_ref, si_ref, bbr_ref, bbi_ref, lr_ref, li_ref, cr_ref, ci_ref, d_ref, *rest):
        du_ref, dbbr_ref, dbbi_ref, dlr_ref, dli_ref, dcr_ref, dci_ref, dd_ref = rest[len(extra):]
        bbr, bbi = bbr_ref[...].astype(MXU), bbi_ref[...].astype(MXU)
        cr, ci = cr_ref[...].astype(MXU), ci_ref[...].astype(MXU)
        lr, li = lr_ref[...], -li_ref[...]
        dv = d_ref[...]
        steps, e = _tile_powers(lr, li, reverse=True)
        for ref in (dbbr_ref, dbbi_ref, dlr_ref, dli_ref, dcr_ref, dci_ref, dd_ref):
            ref[...] = jnp.zeros_like(ref)
        nch = L // RC

        def step(k, carry):
            c = nch - 1 - k
            r0 = pl.multiple_of(c * RC, RC)
            dy = dy_ref[pl.ds(r0, RC), :]
            u = u_ref[pl.ds(r0, RC), :]
            dyb, ub = dy.astype(MXU), u.astype(MXU)
            sr, si = sr_ref[pl.ds(r0, RC), :], si_ref[pl.ds(r0, RC), :]
            dcr_ref[...] += _mm_tn(sr, dyb)
            dci_ref[...] -= _mm_tn(si, dyb)
            gr = _mm_nt(dyb, cr)
            gi = -_mm_nt(dyb, ci)
            gr, gi, carry = _scan_lti(gr, gi, carry, steps, e, reverse=True)
            pr_ = pltpu.roll(jnp.concatenate([_halo(sr_ref, c, r0), sr], axis=0), 1, 0)[8:, :]
            pi_ = pltpu.roll(jnp.concatenate([_halo(si_ref, c, r0), si], axis=0), 1, 0)[8:, :]
            dlr_ref[...] += _colsum(pr_ * gr + pi_ * gi)
            dli_ref[...] += _colsum(pr_ * gi - pi_ * gr)
            grb, gib = gr.astype(MXU), gi.astype(MXU)
            dbbr_ref[...] += _mm_tn(ub, grb)
            dbbi_ref[...] += _mm_tn(ub, gib)
            du_ref[pl.ds(r0, RC), :] = dv * dy + (_mm_nt(grb, bbr) + _mm_nt(gib, bbi))
            dd_ref[...] += _colsum(dy * u)
            return carry

        zero = jnp.zeros((1, S5_TW), F32)
        lax.fori_loop(0, nch, step, (zero, zero))

    in_tile, st, bb, cc, lb, dv = _s5_specs(L)
    u_tile = pl.BlockSpec((L, LANE), lambda t: (0, C_S5U // LANE + t))
    return pl.pallas_call(
        body, name="s5_bwd", grid=(N_S5_T,),
        in_specs=[in_tile, u_tile, st, st, bb, bb, lb, lb, cc, cc, dv] + extra_specs,
        out_specs=[in_tile, bb, bb, lb, lb, cc, cc, dv],
        out_shape=[_S((L, S5_W)), _S((N_S5_T, LANE, S5_TW)), _S((N_S5_T, LANE, S5_TW)), _S((1, S5_N)), _S((1, S5_N)),
                   _S((N_S5_T, S5_TW, LANE)), _S((N_S5_T, S5_TW, LANE)), _S((1, S5_W))],
        compiler_params=_params(1))(dy0, z, s_re, s_im, bb_re, bb_im, lb_re, lb_im, c_re, c_im, dvec, *extra)


def _disc(ar, ai, ls):
    dt = jnp.exp(ls)
    mag = jnp.exp(ar * dt)
    lr = mag * jnp.cos(ai * dt)
    li = mag * jnp.sin(ai * dt)
    den = ar * ar + ai * ai
    cr = ((lr - 1.0) * ar + li * ai) / den
    ci = (li * ar - (lr - 1.0) * ai) / den
    return lr, li, cr, ci


def _s5_disc_fwd(ar, ai, ls):
    def body(ar_ref, ai_ref, ls_ref, lr_ref, li_ref, cr_ref, ci_ref):
        lr, li, cr, ci = _disc(ar_ref[...], ai_ref[...], ls_ref[...])
        lr_ref[...], li_ref[...], cr_ref[...], ci_ref[...] = lr, li, cr, ci

    sh = _S(ar.shape)
    return pl.pallas_call(body, name="s5_disc_fwd", out_shape=[sh, sh, sh, sh])(ar, ai, ls)


def _s5_disc_bwd(ar, ai, ls, dlr, dli, dcr, dci):
    def body(ar_ref, ai_ref, ls_ref, dlr_ref, dli_ref, dcr_ref, dci_ref, dar_ref, dai_ref, dls_ref):
        _, vjp = jax.vjp(_disc, ar_ref[...], ai_ref[...], jnp.broadcast_to(ls_ref[...], ar_ref.shape))
        dar, dai, dls = vjp((dlr_ref[...], dli_ref[...], dcr_ref[...], dci_ref[...]))
        dar_ref[...], dai_ref[...] = dar, dai
        dls_ref[...] = jnp.sum(dls, axis=1, keepdims=True)

    return pl.pallas_call(body, name="s5_disc_bwd", out_shape=[_S(ar.shape), _S(ar.shape), _S(ls.shape)])(
        ar, ai, ls, dlr, dli, dcr, dci)


def _s5_bscale_fwd(cr, ci, br, bi):
    def body(cr_ref, ci_ref, br_ref, bi_ref, or_ref, oi_ref):
        or_ref[...], oi_ref[...] = _cmul(cr_ref[...], ci_ref[...], br_ref[...], bi_ref[...])

    return pl.pallas_call(body, name="s5_bscale_fwd", out_shape=[_S(br.shape), _S(br.shape)])(cr, ci, br, bi)


def _s5_bscale_bwd(cr, ci, br, bi, gr, gi):
    def body(cr_ref, ci_ref, br_ref, bi_ref, gr_ref, gi_ref, dbr_ref, dbi_ref, dcr_ref, dci_ref):
        cr_, ci_, br_, bi_, gr_, gi_ = (r[...] for r in (cr_ref, ci_ref, br_ref, bi_ref, gr_ref, gi_ref))
        dbr_ref[...] = cr_ * gr_ + ci_ * gi_
        dbi_ref[...] = cr_ * gi_ - ci_ * gr_
        dcr_ref[...] = jnp.sum(gr_ * br_ + gi_ * bi_, axis=1, keepdims=True)
        dci_ref[...] = jnp.sum(gi_ * br_ - gr_ * bi_, axis=1, keepdims=True)

    return pl.pallas_call(body, name="s5_bscale_bwd",
                          out_shape=[_S(br.shape), _S(br.shape), _S(cr.shape), _S(cr.shape)])(cr, ci, br, bi, gr, gi)


def _row(w):
    return pl.BlockSpec((TM, w), lambda i: (i, 0))


def _full(shape):
    return pl.BlockSpec(tuple(shape), lambda i: (0,) * len(shape))


def _post_fwd(x, hs, z, y0, p, w_glu, b_glu, w_out, g1, b1, ple_w, w_pg, b_pg, g2, b2):
    L = x.shape[0]

    def body(x_ref, hs_ref, z_ref, y0_ref, p_ref, wg_ref, bg_ref, wo_ref, g1_ref, b1_ref, pw_ref, wpg_ref, bpg_ref,
             g2_ref, b2_ref, x2_ref, t1_ref, t2_ref, m_ref):
        rg_gate = z_ref[:, C_RGG:C_RGG + RG_W]
        s5_gate = z_ref[:, C_S5G:C_S5G + S5_W]
        rg_y = hs_ref[...] * _silu_and_grad(rg_gate)[0]
        y1 = _gelu(y0_ref[...])
        gl = _sigmoid(_mm(y1, wg_ref[...]) + bg_ref[...])
        s5_y = (y1 * gl) * _silu_and_grad(s5_gate)[0]
        m_ref[:, :RG_W] = rg_y
        m_ref[:, RG_W:] = s5_y
        mix = _mm(m_ref[...], wo_ref[...])
        t1 = ALPHA * x_ref[...] + mix
        x1, _, _ = _ln_fwd(t1, g1_ref[...], b1_ref[...])
        e = _mm(p_ref[...], pw_ref[...]) * _sigmoid(_mm(x1, wpg_ref[...]) + bpg_ref[...])
        t2 = ALPHA * x1 + e
        x2, _, _ = _ln_fwd(t2, g2_ref[...], b2_ref[...])
        t1_ref[...], t2_ref[...], x2_ref[...] = t1, t2, x2

    vec = _full((1, D_MODEL))
    return pl.pallas_call(
        body, name="post_fwd", grid=(L // TM,),
        in_specs=[_row(D_MODEL), _row(RG_W), _row(Z_W), _row(S5_W), _row(256), _full((S5_W, S5_W)), _full((1, S5_W)),
                  _full((D_MODEL, D_MODEL)), vec, vec, _full((256, D_MODEL)), _full((D_MODEL, D_MODEL)), vec, vec, vec],
        out_specs=[_row(D_MODEL)] * 4, out_shape=[_S((L, D_MODEL))] * 4,
        compiler_params=_params(1))(x, hs, z, y0, p, w_glu, b_glu, w_out, g1, b1, ple_w, w_pg, b_pg, g2, b2)


def _post_bwd_a(dx2_or_target, is_top, t2, t1, p, ple_w, w_pg, b_pg, g1, b1, g2, b2, token=None):
    L = t1.shape[0]
    extra, extra_specs = _after(token)

    def body(d_ref, t2_ref, t1_ref, p_ref, pw_ref, wpg_ref, bpg_ref, g1_ref, b1_ref, g2_ref, b2_ref, *rest):
        (dt1_ref, dpw_out, dwpg_out, dbpg_ref, dg1_ref, db1_ref, dg2_ref, db2_ref, loss_ref, dpw_ref,
         dwpg_ref) = rest[len(extra):]
        @pl.when(pl.program_id(0) == 0)
        def _():
            for ref in (dpw_ref, dwpg_ref, dbpg_ref, dg1_ref, db1_ref, dg2_ref, db2_ref, loss_ref):
                ref[...] = jnp.zeros_like(ref)

        g1, g2 = g1_ref[...], g2_ref[...]
        x1, xh1, rstd1 = _ln_fwd(t1_ref[...], g1, b1_ref[...])
        x2, xh2, rstd2 = _ln_fwd(t2_ref[...], g2, b2_ref[...])
        if is_top:
            err = x2 - d_ref[...]
            loss_ref[...] += _colsum(err * err)
            dx2 = err * (1.0 / D_MODEL)
        else:
            dx2 = d_ref[...]
        p = p_ref[...]
        q = _mm(p, pw_ref[...])
        gt = _sigmoid(_mm(x1, wpg_ref[...]) + bpg_ref[...])
        dg2_ref[...] += _colsum(dx2 * xh2)
        db2_ref[...] += _colsum(dx2)
        dt2 = _ln_bwd(dx2, xh2, rstd2, g2)
        dq = dt2 * gt
        dgpre = (dt2 * q) * gt * (1.0 - gt)
        dpw_ref[...] += _mm_tn(p, dq)
        dwpg_ref[...] += _mm_tn(x1, dgpre)
        dbpg_ref[...] += _colsum(dgpre)
        dx1 = ALPHA * dt2 + _mm_nt(dgpre, wpg_ref[...])
        dg1_ref[...] += _colsum(dx1 * xh1)
        db1_ref[...] += _colsum(dx1)
        dt1_ref[...] = _ln_bwd(dx1, xh1, rstd1, g1)

        @pl.when(pl.program_id(0) == L // TM - 1)
        def _():
            dpw_out[...] = dpw_ref[...].astype(WIRE)
            dwpg_out[...] = dwpg_ref[...].astype(WIRE)

    vec = _full((1, D_MODEL))
    return pl.pallas_call(
        body, name="post_bwd_a_top" if is_top else "post_bwd_a", grid=(L // TM,),
        in_specs=[_row(D_MODEL), _row(D_MODEL), _row(D_MODEL), _row(256), _full((256, D_MODEL)),
                  _full((D_MODEL, D_MODEL)), vec, vec, vec, vec, vec] + extra_specs,
        out_specs=[_row(D_MODEL), _full((256, D_MODEL)), _full((D_MODEL, D_MODEL)), vec, vec, vec, vec, vec, vec],
        out_shape=[_S((L, D_MODEL)), _S((256, D_MODEL), WIRE), _S((D_MODEL, D_MODEL), WIRE)] + [_S((1, D_MODEL))] * 6,
        scratch_shapes=[pltpu.VMEM((256, D_MODEL), F32), pltpu.VMEM((D_MODEL, D_MODEL), F32)],
        compiler_params=_params(1))(dx2_or_target, t2, t1, p, ple_w, w_pg, b_pg, g1, b1, g2, b2, *extra)


def _post_bwd_b(dt1, m, z, hs, y0, w_out, w_glu, b_glu):
    L = dt1.shape[0]

    def body(dt1_ref, m_ref, z_ref, hs_ref, y0_ref, wo_ref, wg_ref, bg_ref,
             dhs_ref, dy0_ref, dzg_ref, dwo_out, dwg_out, dbg_ref, dwo_ref, dwg_ref):
        @pl.when(pl.program_id(0) == 0)
        def _():
            for ref in (dwo_ref, dwg_ref, dbg_ref):
                ref[...] = jnp.zeros_like(ref)

        dt1b = dt1_ref[...].astype(MXU)
        dm = _mm_nt(dt1b, wo_ref[...])
        dwo_ref[...] += _mm_tn(m_ref[...], dt1b)
        d_rgy, d_s5y = dm[:, :RG_W], dm[:, RG_W:]
        rg_gate = z_ref[:, C_RGG:C_RGG + RG_W]
        s5_gate = z_ref[:, C_S5G:C_S5G + S5_W]
        sl, dsl = _silu_and_grad(rg_gate)
        dhs_ref[...] = d_rgy * sl
        dzg_ref[:, :RG_W] = d_rgy * hs_ref[...] * dsl
        y0 = y0_ref[...]
        y1 = _gelu(y0)
        gl = _sigmoid(_mm(y1, wg_ref[...]) + bg_ref[...])
        sl, dsl = _silu_and_grad(s5_gate)
        dy2 = d_s5y * sl
        dzg_ref[:, RG_W:] = d_s5y * (y1 * gl) * dsl
        dglpre = (dy2 * y1) * gl * (1.0 - gl)
        dwg_ref[...] += _mm_tn(y1, dglpre)
        dbg_ref[...] += _colsum(dglpre)
        dy1 = dy2 * gl + _mm_nt(dglpre, wg_ref[...])
        dy0_ref[...] = dy1 * _gelu_grad(y0)

        @pl.when(pl.program_id(0) == L // TM - 1)
        def _():
            dwo_out[...] = dwo_ref[...].astype(WIRE)
            dwg_out[...] = dwg_ref[...].astype(WIRE)

    return pl.pallas_call(
        body, name="post_bwd_b", grid=(L // TM,),
        in_specs=[_row(D_MODEL), _row(D_MODEL), _row(Z_W), _row(RG_W), _row(S5_W), _full((D_MODEL, D_MODEL)),
                  _full((S5_W, S5_W)), _full((1, S5_W))],
        out_specs=[_row(RG_W), _row(S5_W), _row(D_MODEL), _full((D_MODEL, D_MODEL)), _full((S5_W, S5_W)), _full((1, S5_W))],
        out_shape=[_S((L, RG_W)), _S((L, S5_W)), _S((L, D_MODEL)), _S((D_MODEL, D_MODEL), WIRE), _S((S5_W, S5_W), WIRE),
                   _S((1, S5_W))],
        scratch_shapes=[pltpu.VMEM((D_MODEL, D_MODEL), F32), pltpu.VMEM((S5_W, S5_W), F32)],
        compiler_params=_params(1))(dt1, m, z, hs, y0, w_out, w_glu, b_glu)


def _adamw(parts, w, m, v):
    nl = len(parts)
    n, R, C = parts[0].shape
    tr = R
    for cand in (512, 256, 128, 64, 32, 16, 8):
        if R % cand == 0 and n * cand * C * 4 <= 4 * 1024 * 1024:
            tr = cand
            break
    nblk = R // tr

    def body(*refs):
        p_refs = refs[:nl]
        w_ref, m_ref, v_ref, g_ref, d_ref, nm_ref, nv_ref = refs[nl:]
        layer = pl.program_id(0)
        g = None
        for li, p_ref in enumerate(p_refs):
            s = p_ref[0].astype(F32)
            for k in range(1, n):
                s = s + p_ref[k].astype(F32)
            g = s if g is None else jnp.where(layer == li, s, g)
        nm = B1 * m_ref[...] + (1.0 - B1) * g
        nv = B2 * v_ref[...] + (1.0 - B2) * (g * g)
        d_ref[...] = (-LR) * ((nm / BC1) / (jnp.sqrt(nv / BC2) + EPS) + WD * w_ref[...])
        g_ref[...], nm_ref[...], nv_ref[...] = g, nm, nv

    def part_spec(li):
        return pl.BlockSpec((n, tr, C), lambda l, i: (0, jnp.where(l == li, i, jnp.where(l < li, 0, nblk - 1)), 0))

    blk = pl.BlockSpec((tr, C), lambda l, i: (l * nblk + i, 0))
    return pl.pallas_call(
        body, name="adamw", grid=(nl, nblk),
        in_specs=[part_spec(li) for li in range(nl)] + [blk, blk, blk],
        out_specs=[blk] * 4, out_shape=[_S((nl * R, C))] * 4, compiler_params=_params(2))(*parts, w, m, v)


def _me():
    return lax.axis_index("x"), lax.axis_index("y"), lax.axis_index("c")


def _lin(dev):
    return 4 * dev[0] + 2 * dev[1] + dev[2]


def _blk(ref, axis, size, idx):
    nd = len(ref.shape)
    start = idx * size
    if axis == nd - 1 and size % LANE == 0:
        start = pl.multiple_of(start, LANE)
    elif axis == nd - 2 and size % 16 == 0:
        start = pl.multiple_of(start, 16)
    ix = [slice(None)] * nd
    ix[axis] = pl.ds(start, size)
    return ref.at[tuple(ix)]


def _all_gather(shards, axes, name):
    n = len(shards)
    sizes = [s.shape[a] for s, a in zip(shards, axes)]
    out_shapes = [_S(s.shape[:a] + (N_DEV * s.shape[a],) + s.shape[a + 1:], s.dtype) for s, a in zip(shards, axes)]

    def body(*refs):
        ins, outs = refs[:n], refs[n:2 * n]
        send_sems, recv_sems, local_sems = refs[2 * n:]
        x, y, c = _me()
        me, sibling = (x, y, c), (x, y, 1 - c)
        chips = [(1 - x, y), (x, 1 - y), (1 - x, 1 - y)]

        def copy(a, k, block, to, from_input=False):
            dst = _blk(outs[a], axes[a], sizes[a], _lin(block))
            return pltpu.make_async_remote_copy(
                src_ref=ins[a] if from_input else dst, dst_ref=dst, send_sem=send_sems.at[a, k],
                recv_sem=recv_sems.at[a, k], device_id=to, device_id_type=MESH)

        mine = [pltpu.make_async_copy(ins[a], _blk(outs[a], axes[a], sizes[a], _lin(me)), local_sems.at[a]) for a in range(n)]
        for cp in mine:
            cp.start()
        first = []
        for a in range(n):
            first.append(copy(a, 0, me, sibling, True))
            first += [copy(a, 1 + j, me, (*chip, c), True) for j, chip in enumerate(chips)]
        for cp in first:
            cp.start()
        passed = []
        for j, chip in enumerate(chips):
            for a in range(n):
                copy(a, 1 + j, (*chip, c), me).wait_recv()
                cp = copy(a, 4 + j, (*chip, c), sibling)
                cp.start()
                passed.append(cp)
        for a in range(n):
            copy(a, 0, sibling, me).wait_recv()
            for j, chip in enumerate(chips):
                copy(a, 4 + j, (*chip, 1 - c), me).wait_recv()
        for cp in first + passed:
            cp.wait_send()
        for cp in mine:
            cp.wait()

    return pl.pallas_call(
        body, name=name, out_shape=out_shapes, in_specs=[ANY] * n, out_specs=[ANY] * n,
        scratch_shapes=[pltpu.SemaphoreType.DMA((n, 7)), pltpu.SemaphoreType.DMA((n, 7)), pltpu.SemaphoreType.DMA((n,))],
    )(*shards)


def _exchange(groups, axes, name):
    arrays = [a for g in groups for a in g]
    where = [(o, i) for o, g in enumerate(groups) for i in range(len(g))]
    ax = [axes[o] for o, _ in where]
    n = len(arrays)
    sizes = [s.shape[a] // N_DEV for s, a in zip(arrays, ax)]
    out_shapes = []
    for g, a in zip(groups, axes):
        s = g[0].shape
        out_shapes.append(_S((N_DEV, len(g)) + s[:a] + (s[a] // N_DEV,) + s[a + 1:], g[0].dtype))

    def body(*refs):
        ins, outs = refs[:n], refs[n:n + len(groups)]
        send_sems, recv_sems, local_sems = refs[n + len(groups):]
        x, y, c = _me()
        me = (x, y, c)
        flip = lambda v, f: 1 - v if f else v
        peers = [(flip(x, k & 4), flip(y, k & 2), flip(c, k & 1)) for k in range(1, N_DEV)]

        def land(a, sender):
            o, i = where[a]
            return outs[o].at[_lin(sender), i]

        def copy(a, k, to):
            return pltpu.make_async_remote_copy(
                src_ref=_blk(ins[a], ax[a], sizes[a], _lin(to)), dst_ref=land(a, me),
                send_sem=send_sems.at[a, k], recv_sem=recv_sems.at[a, k], device_id=to, device_id_type=MESH)

        mine = [pltpu.make_async_copy(_blk(ins[a], ax[a], sizes[a], _lin(me)), land(a, me), local_sems.at[a]) for a in range(n)]
        for cp in mine:
            cp.start()
        sends = [copy(a, k, peer) for a in range(n) for k, peer in enumerate(peers)]
        for cp in sends:
            cp.start()
        for a in range(n):
            for k, peer in enumerate(peers):
                pltpu.make_async_remote_copy(
                    src_ref=land(a, peer), dst_ref=land(a, peer), send_sem=send_sems.at[a, k],
                    recv_sem=recv_sems.at[a, k], device_id=peer, device_id_type=MESH).wait_recv()
        for cp in sends:
            cp.wait_send()
        for cp in mine:
            cp.wait()

    return pl.pallas_call(
        body, name=name, out_shape=out_shapes, in_specs=[ANY] * n, out_specs=[ANY] * len(groups),
        scratch_shapes=[pltpu.SemaphoreType.DMA((n, 7)), pltpu.SemaphoreType.DMA((n, 7)), pltpu.SemaphoreType.DMA((n,))],
    )(*arrays)


HBM_SPEC = pl.BlockSpec(memory_space=pltpu.HBM)
SEM_SPEC = pl.BlockSpec(memory_space=pltpu.SEMAPHORE)
EFFECT = pltpu.SideEffectType.DATAFLOW_SIDE_EFFECTING


def _peers(x, y, c):
    flip = lambda v, f: 1 - v if f else v
    return [(flip(x, k & 4), flip(y, k & 2), flip(c, k & 1)) for k in range(1, N_DEV)]


def _land_shape(mode, s, axis):
    if mode == "gather":
        return s.shape[:axis] + (N_DEV * s.shape[axis],) + s.shape[axis + 1:]
    return (N_DEV,) + s.shape[:axis] + (s.shape[axis] // N_DEV,) + s.shape[axis + 1:]


def _src_view(mode, ref, axis, peer):
    return ref if mode == "gather" else _blk(ref, axis, ref.shape[axis] // N_DEV, peer)


def _dst_view(mode, land, axis, sender):
    return _blk(land, axis, land.shape[axis] // N_DEV, sender) if mode == "gather" else land.at[sender]


def _seven_blocks(mode, land, axis):
    if mode == "gather":
        ix = [slice(None)] * len(land.shape)
        ix[axis] = pl.ds(0, (N_DEV - 1) * (land.shape[axis] // N_DEV))
        return land.at[tuple(ix)]
    return land.at[pl.ds(0, N_DEV - 1)]


def _place_own(mode, srcs, axes, name, after=None):
    n = len(srcs)
    extra, extra_specs = _after(after)
    k = n + len(extra)

    def body(*refs):
        me = _lin(_me())
        cps = [pltpu.make_async_copy(_src_view(mode, refs[a], axes[a], me), _dst_view(mode, refs[k + a], axes[a], me),
                                     refs[k + n].at[a]) for a in range(n)]
        for cp in cps:
            cp.start()
        for cp in cps:
            cp.wait()

    return pl.pallas_call(
        body, name=name, out_shape=[_S(_land_shape(mode, s, a), s.dtype) for s, a in zip(srcs, axes)],
        in_specs=[ANY] * n + extra_specs, out_specs=[ANY] * n, scratch_shapes=[pltpu.SemaphoreType.DMA((n,))])(*srcs, *extra)


def _push_start(mode, srcs, lands, axes, name):
    n = len(srcs)

    def body(*refs):
        src_refs, land_refs = refs[:n], refs[n:2 * n]
        send_sems, recv_sems = refs[2 * n], refs[2 * n + 1]
        token = refs[-1]
        x, y, c = _me()
        me = _lin((x, y, c))
        for a in range(n):
            for peer in _peers(x, y, c):
                pltpu.make_async_remote_copy(
                    src_ref=_src_view(mode, src_refs[a], axes[a], _lin(peer)),
                    dst_ref=_dst_view(mode, land_refs[a], axes[a], me),
                    send_sem=send_sems.at[a], recv_sem=recv_sems.at[a], device_id=peer, device_id_type=MESH).start()
        token[...] = jnp.zeros_like(token)

    hbm = lambda s: pltpu.HBM(s.shape, s.dtype)
    outs = pl.pallas_call(
        body, name=name,
        out_shape=(pltpu.SemaphoreType.DMA((n,)), pltpu.SemaphoreType.DMA((n,)), *[hbm(s) for s in srcs], *[hbm(s) for s in lands],
                   _S((SUB, LANE))),
        in_specs=[HBM_SPEC] * (2 * n),
        out_specs=(SEM_SPEC, SEM_SPEC, *[HBM_SPEC] * (2 * n), pl.BlockSpec(memory_space=pltpu.VMEM)),
        input_output_aliases={i: 2 + i for i in range(2 * n)},
        compiler_params=pltpu.CompilerParams(has_side_effects=EFFECT),
    )(*[pltpu.with_memory_space_constraint(s, pltpu.HBM) for s in list(srcs) + list(lands)])
    return outs[0], outs[1], outs[2:2 + n], outs[2 + n:2 + 2 * n], outs[-1]


def _push_wait(mode, send_sems, recv_sems, srcs, lands, axes, after, name):
    n = len(srcs)

    def body(*refs):
        land_refs = refs[n:2 * n]
        send_sems, recv_sems = refs[2 * n], refs[2 * n + 1]
        x, y, c = _me()
        for a in range(n):
            seven = _seven_blocks(mode, land_refs[a], axes[a])
            cp = pltpu.make_async_remote_copy(src_ref=seven, dst_ref=seven, send_sem=send_sems.at[a], recv_sem=recv_sems.at[a],
                                              device_id=(x, y, 1 - c), device_id_type=MESH)
            cp.wait_send()
            cp.wait_recv()

    hbm = lambda s: pltpu.HBM(s.shape, s.dtype)
    outs = pl.pallas_call(
        body, name=name, out_shape=tuple(hbm(s) for s in list(srcs) + list(lands)),
        in_specs=[HBM_SPEC] * (2 * n) + [SEM_SPEC, SEM_SPEC, ANY], out_specs=tuple([HBM_SPEC] * (2 * n)),
        input_output_aliases={i: i for i in range(2 * n)},
        compiler_params=pltpu.CompilerParams(has_side_effects=EFFECT),
    )(*srcs, *lands, send_sems, recv_sems, after)
    return outs[n:]


def _sum_parts(parts):
    n, R, C = parts.shape

    def body(p_ref, o_ref):
        g = p_ref[0]
        for k in range(1, n):
            g = g + p_ref[k]
        o_ref[...] = g

    return pl.pallas_call(body, name="sum_parts", out_shape=_S((R, C)))(parts)


def _block_diag(w, nb):
    tn, r, c = w.shape
    w = w.reshape(tn // nb, nb, r, c)
    return jnp.einsum('tarc,ab->tarbc', w, jnp.eye(nb, dtype=w.dtype)).reshape(tn // nb, nb * r, nb * c)


def _block_diag_extract(w, nb):
    t, R, C = w.shape
    w = w.reshape(t, nb, R // nb, nb, C // nb)
    return jnp.einsum('tarbc,ab->tarc', w, jnp.eye(nb, dtype=w.dtype)).reshape(t * nb, R // nb, C // nb)


SMALL = ['conv_b', 'rg_wa', 'rg_ba', 'rg_wx', 'rg_bx', 'rg_lambda', 's5_a_re', 's5_a_im', 's5_b_re', 's5_b_im',
         's5_c_re', 's5_c_im', 's5_d', 's5_log_step', 's5_b_glu', 'ln1_g', 'ln1_b', 'ple_gate_b', 'ln2_g', 'ln2_b']
WEIGHTS = ['w_in', 'conv_w', 'conv_b', 'rg_wa', 'rg_ba', 'rg_wx', 'rg_bx', 'rg_lambda', 's5_a_re', 's5_a_im', 's5_b_re',
           's5_b_im', 's5_c_re', 's5_c_im', 's5_d', 's5_log_step', 's5_w_glu', 's5_b_glu', 'w_out', 'ln1_g', 'ln1_b',
           'ple_w', 'ple_gate_w', 'ple_gate_b', 'ln2_g', 'ln2_b']
PACK_ROWS_MULT = 64


def _pack(tree):
    flat = jnp.concatenate([tree[k].reshape(-1) for k in SMALL])
    rows = -(-flat.shape[0] // (LANE * PACK_ROWS_MULT)) * PACK_ROWS_MULT
    return jnp.pad(flat, (0, rows * LANE - flat.shape[0])).reshape(rows, LANE)


def _unpack(packed, like):
    flat, out, o = packed.reshape(-1), {}, 0
    for k in SMALL:
        n = math.prod(like[k].shape)
        out[k] = flat[o:o + n].reshape(like[k].shape)
        o += n
    return out


class _NoHooks:
    token = None

    def late_weights(self, W, after):
        return W

    def post_done(self, i, g):
        return None

    def layer_done(self, i, g, dx):
        return None


def _local_grads(x, p, target, W, disc, hooks):
    depth = 2
    saved = []
    for i in range(depth):
        w = W[i]
        z = _inproj_fwd(x, w['w_in'], hooks.token if i == 0 else None)
        hs = _rg_fwd(z, w['conv_w'], w['conv_b'], w['wa_bd'], w['wx_bd'], w['rg_ba'], w['rg_bx'], w['rg_lambda'])
        d = disc[i]
        y0, s_re, s_im = _s5_fwd(z, d['bb_re'], d['bb_im'], d['lb_re'], d['lb_im'], d['c_re'], d['c_im'], w['s5_d'])
        if i == 0:
            W = hooks.late_weights(W, y0)
            w = W[i]
        x2, t1, t2, m = _post_fwd(x, hs, z, y0, p[i], w['s5_w_glu'], w['s5_b_glu'], w['w_out'], w['ln1_g'], w['ln1_b'],
                                  w['ple_w'], w['ple_gate_w'], w['ple_gate_b'], w['ln2_g'], w['ln2_b'])
        saved.append((x, z, hs, y0, s_re, s_im, t1, t2, m))
        x = x2

    grads = [None] * depth
    dx = target
    loss = None
    token = None
    for i in reversed(range(depth)):
        w, d = W[i], disc[i]
        xin, z, hs, y0, s_re, s_im, t1, t2, m = saved[i]
        g = {}
        (dt1, g['ple_w'], g['ple_gate_w'], g['ple_gate_b'], g['ln1_g'], g['ln1_b'], g['ln2_g'], g['ln2_b'], lrow) = _post_bwd_a(
            dx, i == depth - 1, t2, t1, p[i], w['ple_w'], w['ple_gate_w'], w['ple_gate_b'], w['ln1_g'], w['ln1_b'],
            w['ln2_g'], w['ln2_b'], token)
        if i == depth - 1:
            loss = 0.5 / D_MODEL * jnp.sum(lrow)
        dhs, dy0, dzg, g['w_out'], g['s5_w_glu'], g['s5_b_glu'] = _post_bwd_b(dt1, m, z, hs, y0, w['w_out'], w['s5_w_glu'],
                                                                           w['s5_b_glu'])
        (dzu, g['bb_re'], g['bb_im'], g['lb_re'], g['lb_im'], g['c_re'], g['c_im'], g['s5_d']) = _s5_bwd(
            dy0, z, s_re, s_im, d['bb_re'], d['bb_im'], d['lb_re'], d['lb_im'], d['c_re'], d['c_im'], w['s5_d'],
            hooks.post_done(i, g))
        (dzx, g['conv_w'], g['conv_b'], g['wa_bd'], g['wx_bd'], g['rg_ba'], g['rg_bx'], g['rg_lambda']) = _rg_bwd(
            dhs, z, hs, w['conv_w'], w['conv_b'], w['wa_bd'], w['wx_bd'], w['rg_ba'], w['rg_bx'], w['rg_lambda'])
        dx, g['w_in'] = _inproj_bwd(dt1, xin, dzx, dzg, dzu, w['w_in'])
        grads[i] = g
        token = hooks.layer_done(i, g, dx)
    return loss, dx, grads


def _s5_layouts_fwd(s5_a_re, s5_a_im, s5_log_step, s5_b_re, s5_b_im, s5_c_re, s5_c_im):
    depth = s5_a_re.shape[0]
    ar, ai = s5_a_re.reshape(depth * 24, S5_P), s5_a_im.reshape(depth * 24, S5_P)
    ls = s5_log_step.reshape(depth * 24, 1)
    lr, li, cr, ci = _s5_disc_fwd(ar, ai, ls)
    col = lambda a: a.reshape(depth * S5_N, 1)
    br, bi = s5_b_re.reshape(depth * S5_N, 16), s5_b_im.reshape(depth * S5_N, 16)
    bbr, bbi = _s5_bscale_fwd(col(cr), col(ci), br, bi)
    disc = []
    for i in range(depth):
        gph = lambda a: a.reshape(depth, 24, S5_P, 16)[i]
        disc.append(dict(
            bb_re=_block_diag(jnp.swapaxes(gph(bbr), 1, 2), 8), bb_im=_block_diag(jnp.swapaxes(gph(bbi), 1, 2), 8),
            lb_re=lr.reshape(depth, 1, S5_N)[i], lb_im=li.reshape(depth, 1, S5_N)[i],
            c_re=_block_diag(jnp.swapaxes(s5_c_re[i], 1, 2), 8), c_im=_block_diag(jnp.swapaxes(s5_c_im[i], 1, 2), 8)))
    return disc, (ar, ai, ls, col(cr), col(ci), br, bi)


def _s5_layouts_bwd(grads, res):
    ar, ai, ls, cr, ci, br, bi = res
    depth = len(grads)
    stack = lambda f: jnp.stack([f(g) for g in grads])
    dbbr = stack(lambda g: jnp.swapaxes(_block_diag_extract(g['bb_re'], 8), 1, 2)).reshape(depth * S5_N, 16)
    dbbi = stack(lambda g: jnp.swapaxes(_block_diag_extract(g['bb_im'], 8), 1, 2)).reshape(depth * S5_N, 16)
    dbr, dbi, dcr, dci = _s5_bscale_bwd(cr, ci, br, bi, dbbr, dbbi)
    gp = lambda a: a.reshape(depth * 24, S5_P)
    dar, dai, dls = _s5_disc_bwd(ar, ai, ls, gp(stack(lambda g: g['lb_re'])), gp(stack(lambda g: g['lb_im'])), gp(dcr), gp(dci))
    return dict(
        s5_a_re=dar.reshape(depth, 24, S5_P), s5_a_im=dai.reshape(depth, 24, S5_P), s5_log_step=dls.reshape(depth, 24),
        s5_b_re=dbr.reshape(depth, 24, S5_P, 16), s5_b_im=dbi.reshape(depth, 24, S5_P, 16),
        s5_c_re=stack(lambda g: jnp.swapaxes(_block_diag_extract(g['c_re'], 8), 1, 2)),
        s5_c_im=stack(lambda g: jnp.swapaxes(_block_diag_extract(g['c_im'], 8), 1, 2)))


LATE = ('w_out', 'ple_w', 'ple_gate_w', 's5_w_glu')


def _layer_weights(full, i):
    row = lambda a: a[i].reshape(1, -1)
    return dict(
        w_in=full['w_in'][i], conv_w=full['conv_w'][i], conv_b=row(full['conv_b']),
        wa_bd=_block_diag(full['rg_wa'][i], 2), wx_bd=_block_diag(full['rg_wx'][i], 2),
        rg_ba=row(full['rg_ba']), rg_bx=row(full['rg_bx']), rg_lambda=row(full['rg_lambda']),
        s5_d=row(full['s5_d']), s5_b_glu=row(full['s5_b_glu']), ln1_g=row(full['ln1_g']), ln1_b=row(full['ln1_b']),
        ple_gate_b=row(full['ple_gate_b']), ln2_g=row(full['ln2_g']), ln2_b=row(full['ln2_b']))


class _AllLocal(_NoHooks):
    def __init__(self, full):
        self.full = full

    def late_weights(self, W, after):
        for i, w in enumerate(W):
            w.update({k: self.full[k][i] for k in LATE})
        return W


def _full_grads(full, x, p, target, hooks=None):
    disc, res = _s5_layouts_fwd(full['s5_a_re'], full['s5_a_im'], full['s5_log_step'], full['s5_b_re'], full['s5_b_im'],
                                full['s5_c_re'], full['s5_c_im'])
    W = [_layer_weights(full, i) for i in range(2)]
    loss, gx, grads = _local_grads(x, p, target, W, disc, hooks or _AllLocal(full))
    stack = lambda f: jnp.stack([f(g) for g in grads])
    out = _s5_layouts_bwd(grads, res)
    for k in SHARD_AXIS:
        out[k] = [g[k] for g in grads]
    out['conv_w'] = stack(lambda g: g['conv_w'])
    for k in ('conv_b', 'rg_ba', 'rg_bx', 'rg_lambda', 's5_b_glu', 'ln1_g', 'ln1_b', 'ple_gate_b', 'ln2_g', 'ln2_b'):
        out[k] = stack(lambda g: g[k][0])
    out['s5_d'] = stack(lambda g: g['s5_d'][0]).reshape(2, 24, 16)
    out['rg_wa'] = stack(lambda g: _block_diag_extract(g['wa_bd'], 2))
    out['rg_wx'] = stack(lambda g: _block_diag_extract(g['wx_bd'], 2))
    return loss, gx, out


SHARD_AXIS = {'w_in': 2, 'w_out': 1, 'ple_w': 2, 'ple_gate_w': 1, 's5_w_glu': 1}


def kernel(x, p, w_in, conv_w, conv_b, rg_wa, rg_ba, rg_wx, rg_bx, rg_lambda, s5_a_re, s5_a_im, s5_b_re, s5_b_im, s5_c_re, s5_c_im, s5_d, s5_log_step, s5_w_glu, s5_b_glu, w_out, ln1_g, ln1_b, ple_w, ple_gate_w, ple_gate_b, ln2_g, ln2_b, loss_target, m_w_in, m_conv_w, m_conv_b, m_rg_wa, m_rg_ba, m_rg_wx, m_rg_bx, m_rg_lambda, m_s5_a_re, m_s5_a_im, m_s5_b_re, m_s5_b_im, m_s5_c_re, m_s5_c_im, m_s5_d, m_s5_log_step, m_s5_w_glu, m_s5_b_glu, m_w_out, m_ln1_g, m_ln1_b, m_ple_w, m_ple_gate_w, m_ple_gate_b, m_ln2_g, m_ln2_b, v_w_in, v_conv_w, v_conv_b, v_rg_wa, v_rg_ba, v_rg_wx, v_rg_bx, v_rg_lambda, v_s5_a_re, v_s5_a_im, v_s5_b_re, v_s5_b_im, v_s5_c_re, v_s5_c_im, v_s5_d, v_s5_log_step, v_s5_w_glu, v_s5_b_glu, v_w_out, v_ln1_g, v_ln1_b, v_ple_w, v_ple_gate_w, v_ple_gate_b, v_ln2_g, v_ln2_b):
    local = dict(locals())
    w = {k: local[k] for k in WEIGHTS}
    mom = {k: local['m_' + k] for k in WEIGHTS}
    var = {k: local['v_' + k] for k in WEIGHTS}

    big = list(SHARD_AXIS)
    wire = {k: w[k].astype(WIRE) for k in big}
    first = _all_gather([wire['w_in'][0][None], conv_w[None]], [0, 0], "gather_first_weights")
    rest_src = [wire['w_in'][1][None]] + [wire[k] for k in LATE]
    rest_axes = [0] + [SHARD_AXIS[k] for k in LATE]
    rest = _push_start("gather", rest_src, _place_own("gather", rest_src, rest_axes, "place_weights", after=first[0]),
                       rest_axes, "push_weights")
    late_axes = [SHARD_AXIS[k] - 1 for k in LATE]
    pushed = {}

    def push_grads(key, g, names, axes):
        srcs = [g[k] for k in names]
        pushed[key] = _push_start("scatter", srcs, _place_own("scatter", srcs, axes, "place_grads_" + key), axes,
                                  "push_grads_" + key)
        return pushed[key][4]

    def await_grads(key, axes, after):
        s = pushed[key]
        return _push_wait("scatter", s[0], s[1], s[2], s[3], axes, after, "await_grads_" + key)

    class Overlap(_NoHooks):
        token = rest[4]

        def late_weights(self, W, after):
            lands = _push_wait("gather", rest[0], rest[1], rest[2], rest[3], rest_axes, after, "await_weights")
            W[1]['w_in'] = lands[0]
            for i, wl in enumerate(W):
                wl.update({k: land[i] for k, land in zip(LATE, lands[1:])})
            return W

        def post_done(self, i, g):
            return push_grads("late0", g, LATE, late_axes) if i == 0 else None

        def layer_done(self, i, g, dx):
            return push_grads("all1", g, ['w_in'] + list(LATE), [0] + late_axes) if i == 1 else None

    hooks = Overlap()
    full = dict(w)
    full['w_in'] = [first[0], None]
    full['conv_w'] = jnp.moveaxis(first[1], 0, 2).reshape(2, 4, RG_W)

    loss, grad_x, g = _full_grads(full, x[0], p[:, 0], loss_target[0], hooks)
    loss = lax.psum(loss, ("x", "y", "c"))

    conv_blocks = jnp.moveaxis(g['conv_w'].reshape(2, 4, N_DEV, RG_W // N_DEV), 2, 0).reshape(N_DEV, 8, RG_W // N_DEV)
    packed = _pack(g)
    w_in0, conv_parts, small_parts = _exchange([[g['w_in'][0]], [conv_blocks], [packed]], [0, 0, 0], "exchange_grads")
    recv1 = dict(zip(['w_in'] + list(LATE), await_grads("all1", [0] + late_axes, grad_x)))
    recv0 = dict(zip(LATE, await_grads("late0", late_axes, grad_x)), w_in=w_in0)
    outs = {}
    for k in big + ['conv_w']:
        shard = w[k].shape
        c = shard[-1]
        two = lambda a: a.reshape(-1, c)
        parts = [conv_parts.reshape(N_DEV, -1, c)] if k == 'conv_w' else [r[k].reshape(N_DEV, -1, c) for r in (recv0, recv1)]
        outs[k] = [o.reshape(shard) for o in _adamw(parts, two(w[k]), two(mom[k]), two(var[k]))]

    rows = packed.shape[0] // N_DEV
    mine = _sum_parts(small_parts.reshape(N_DEV, rows, LANE))
    summed = _all_gather([mine], [0], "gather_small_grads")[0]
    small = _adamw([summed[None]], _pack(w), _pack(mom), _pack(var))
    small = [_unpack(o, w) for o in small]
    for k in SMALL:
        outs[k] = [o[k] for o in small]

    res = [loss, grad_x[None]]
    for j in range(4):
        res += [outs[k][j] for k in WEIGHTS]
    return tuple(res)
```

```python
import functools
import math

import jax
import jax.numpy as jnp
from jax import lax
from jax.experimental import pallas as pl
from jax.experimental.pallas import tpu as pltpu

F32 = jnp.float32
MXU = jnp.bfloat16
WIRE = jnp.bfloat16

N_DEV = 8
D_MODEL = 1024
RG_W = 640
S5_W = 384
S5_P = 64
S5_N = 24 * S5_P
Z_W = 2 * RG_W + 2 * S5_W
C_RGG = RG_W
C_S5U = 2 * RG_W
C_S5G = 2 * RG_W + S5_W
LANE = 128
N_RG_T = RG_W // LANE
N_S5_T = S5_W // LANE
W_BLK = Z_W // N_DEV
ALPHA = (2.0 * 2) ** 0.25
LN_EPS = 1e-5
RG_C = 8.0
LR, B1, B2, EPS, WD, STEP = 0.001, 0.9, 0.999, 1e-08, 0.01, 10
BC1 = 1.0 - B1 ** STEP
BC2 = 1.0 - B2 ** STEP
RC = 256
TM = 256
VMEM_LIMIT = 56 * 1024 * 1024

MESH = pl.DeviceIdType.MESH
ANY = pl.BlockSpec(memory_space=pl.ANY)


def _params(n_grid_axes, vmem=VMEM_LIMIT):
    return pltpu.CompilerParams(dimension_semantics=("arbitrary",) * n_grid_axes, vmem_limit_bytes=vmem)


def _S(shape, dtype=F32):
    return jax.ShapeDtypeStruct(tuple(shape), dtype)


def _sigmoid(x):
    return 1.0 / (1.0 + jnp.exp(-x))


def _silu_and_grad(x):
    s = _sigmoid(x)
    return x * s, s * (1.0 + x * (1.0 - s))


_GELU_C = math.sqrt(2.0 / math.pi)


def _gelu(x):
    return 0.5 * x * (1.0 + jnp.tanh(_GELU_C * (x + 0.044715 * (x * x * x))))


def _gelu_grad(x):
    th = jnp.tanh(_GELU_C * (x + 0.044715 * (x * x * x)))
    return 0.5 * (1.0 + th) + 0.5 * x * (1.0 - th * th) * (_GELU_C * (1.0 + 3.0 * 0.044715 * (x * x)))


def _mm(a, b):
    return jnp.dot(a.astype(MXU), b.astype(MXU), preferred_element_type=F32)


def _mm_nt(a, b):
    return lax.dot_general(a.astype(MXU), b.astype(MXU), (((1,), (1,)), ((), ())), preferred_element_type=F32)


def _mm_tn(a, b):
    return lax.dot_general(a.astype(MXU), b.astype(MXU), (((0,), (0,)), ((), ())), preferred_element_type=F32)


def _ln_fwd(t, g, b):
    mu = jnp.mean(t, axis=-1, keepdims=True)
    tc = t - mu
    var = jnp.mean(tc * tc, axis=-1, keepdims=True)
    rstd = lax.rsqrt(var + LN_EPS)
    xhat = tc * rstd
    return xhat * g + b, xhat, rstd


def _ln_bwd(dy, xhat, rstd, g):
    dxh = dy * g
    m1 = jnp.mean(dxh, axis=-1, keepdims=True)
    m2 = jnp.mean(dxh * xhat, axis=-1, keepdims=True)
    return rstd * (dxh - m1 - xhat * m2)


def _colsum(a):
    return jnp.sum(a, axis=0, keepdims=True)


def _up(x, d, rows, fill):
    n = x.shape[0]
    return jnp.where(rows < n - d, pltpu.roll(x, n - d, 0), fill)


SUB = 8
TILE_STEPS = (1, 2, 4)


def _r8(width):
    return lax.broadcasted_iota(jnp.int32, (SUB, width), 0)


def _scan_real(a, u, carry, reverse=False):
    r8 = _r8(a.shape[1])
    n = a.shape[0] // SUB
    outs = [None] * n
    for k in (reversed(range(n)) if reverse else range(n)):
        A, U = a[SUB * k:SUB * k + SUB], u[SUB * k:SUB * k + SUB]
        for d in TILE_STEPS:
            m = (r8 < SUB - d) if reverse else (r8 >= d)
            sh = SUB - d if reverse else d
            U = A * jnp.where(m, pltpu.roll(U, sh, 0), 0.0) + U
            A = A * jnp.where(m, pltpu.roll(A, sh, 0), 1.0)
        h = A * carry + U
        outs[k] = h
        carry = h[0:1] if reverse else h[SUB - 1:SUB]
    return jnp.concatenate(outs, axis=0), carry


def _tile_powers(lr, li, reverse=False):
    width = lr.shape[1]
    r8 = _r8(width)
    steps = []
    pr, pi = lr, li
    er, ei = jnp.broadcast_to(lr, (SUB, width)), jnp.broadcast_to(li, (SUB, width))
    for d in TILE_STEPS:
        m = (r8 < SUB - d) if reverse else (r8 >= d)
        sh = SUB - d if reverse else d
        steps.append((sh, jnp.where(m, pr, 0.0), jnp.where(m, pi, 0.0)))
        er, ei = _cmul(er, ei, jnp.where(m, pltpu.roll(er, sh, 0), 1.0), jnp.where(m, pltpu.roll(ei, sh, 0), 0.0))
        pr, pi = _cmul(pr, pi, pr, pi)
    return steps, (er, ei)


def _scan_lti(xr, xi, carry, steps, e, reverse=False):
    er, ei = e
    kr, ki = carry
    n = xr.shape[0] // SUB
    outr, outi = [None] * n, [None] * n
    for k in (reversed(range(n)) if reverse else range(n)):
        sr, si = xr[SUB * k:SUB * k + SUB], xi[SUB * k:SUB * k + SUB]
        for sh, pr, pi in steps:
            shr, shi = pltpu.roll(sr, sh, 0), pltpu.roll(si, sh, 0)
            sr, si = sr + (pr * shr - pi * shi), si + (pr * shi + pi * shr)
        sr = sr + (er * kr - ei * ki)
        si = si + (er * ki + ei * kr)
        outr[k], outi[k] = sr, si
        kr, ki = (sr[0:1], si[0:1]) if reverse else (sr[SUB - 1:SUB], si[SUB - 1:SUB])
    return jnp.concatenate(outr, axis=0), jnp.concatenate(outi, axis=0), (kr, ki)


def _halo(ref, c, r0):
    rp = pl.multiple_of(jnp.maximum(r0 - 8, 0), 8)
    return jnp.where(c > 0, ref[pl.ds(rp, 8), :], 0.0)


def _conv_taps(xe):
    return [pltpu.roll(xe, 3, 0)[8:, :], pltpu.roll(xe, 2, 0)[8:, :], pltpu.roll(xe, 1, 0)[8:, :], xe[8:, :]]


def _rg_gates(h, wa, wx, ba, bx, sp):
    r = _sigmoid(_mm(h, wa) + ba)
    i = _sigmoid(_mm(h, wx) + bx)
    log_a = (-RG_C) * r * sp
    a = jnp.exp(log_a)
    mult = jnp.sqrt(-jnp.tanh(log_a) * (a * a + 1.0))
    return r, i, a, mult


def _softplus(y):
    return jnp.maximum(y, 0.0) + jnp.log1p(jnp.exp(-jnp.abs(y)))


def _after(token):
    return ([], []) if token is None else ([token], [ANY])


def _inproj_fwd(x, w_in, token=None):
    L = x.shape[0]

    def body(x_ref, w_ref, *rest):
        xb = x_ref[...].astype(MXU)
        for j in range(N_DEV):
            rest[-1][:, j * W_BLK:(j + 1) * W_BLK] = jnp.dot(xb, w_ref[j].astype(MXU), preferred_element_type=F32)

    extra, extra_specs = _after(token)
    return pl.pallas_call(
        body, name="inproj_fwd", grid=(L // TM,),
        in_specs=[pl.BlockSpec((TM, D_MODEL), lambda i: (i, 0)),
                  pl.BlockSpec((N_DEV, D_MODEL, W_BLK), lambda i: (0, 0, 0))] + extra_specs,
        out_specs=pl.BlockSpec((TM, Z_W), lambda i: (i, 0)),
        out_shape=_S((L, Z_W)), compiler_params=_params(1))(x, w_in, *extra)


def _inproj_bwd(dt1, x, dzx, dzg, dzu, w_in):
    L = x.shape[0]

    def body(dt1_ref, x_ref, dzx_ref, dzg_ref, dzu_ref, w_ref, dx_ref, dw_ref, acc_ref):
        @pl.when(pl.program_id(0) == 0)
        def _():
            acc_ref[...] = jnp.zeros_like(acc_ref)
        dzg = dzg_ref[...]
        dz = jnp.concatenate([dzx_ref[...], dzg[:, :RG_W], dzu_ref[...], dzg[:, RG_W:]], axis=1).astype(MXU)
        xb = x_ref[...].astype(MXU)
        dx = ALPHA * dt1_ref[...]
        for j in range(N_DEV):
            dzj = dz[:, j * W_BLK:(j + 1) * W_BLK]
            dx = dx + _mm_nt(dzj, w_ref[j])
            acc_ref[j] += _mm_tn(xb, dzj)
        dx_ref[...] = dx

        @pl.when(pl.program_id(0) == L // TM - 1)
        def _():
            dw_ref[...] = acc_ref[...].astype(WIRE)

    row = lambda w: pl.BlockSpec((TM, w), lambda i: (i, 0))
    wspec = pl.BlockSpec((N_DEV, D_MODEL, W_BLK), lambda i: (0, 0, 0))
    return pl.pallas_call(
        body, name="inproj_bwd", grid=(L // TM,),
        in_specs=[row(D_MODEL), row(D_MODEL), row(RG_W), row(D_MODEL), row(S5_W), wspec],
        out_specs=[row(D_MODEL), wspec],
        out_shape=[_S((L, D_MODEL)), _S((N_DEV, D_MODEL, W_BLK), WIRE)],
        scratch_shapes=[pltpu.VMEM((N_DEV, D_MODEL, W_BLK), F32)],
        compiler_params=_params(1))(dt1, x, dzx, dzg, dzu, w_in)


def _rg_specs(L):
    tile = lambda rows: pl.BlockSpec((rows, LANE), lambda c: (0, c))
    return tile, pl.BlockSpec((None, LANE, LANE), lambda c: (c, 0, 0))


def _rg_fwd(z, cw, cb, wa_bd, wx_bd, ba, bx, lam):
    L = z.shape[0]

    def body(x_ref, cw_ref, cb_ref, wa_ref, wx_ref, ba_ref, bx_ref, lam_ref, hs_ref):
        w, b = cw_ref[...], cb_ref[...]
        wa, wx, ba_, bx_ = wa_ref[...].astype(MXU), wx_ref[...].astype(MXU), ba_ref[...], bx_ref[...]
        sp = _softplus(-lam_ref[...])

        def step(c, carry):
            r0 = pl.multiple_of(c * RC, RC)
            xe = jnp.concatenate([_halo(x_ref, c, r0), x_ref[pl.ds(r0, RC), :]], axis=0)
            t = _conv_taps(xe)
            h = t[0] * w[0:1] + t[1] * w[1:2] + t[2] * w[2:3] + t[3] * w[3:4] + b
            _, i, a, mult = _rg_gates(h, wa, wx, ba_, bx_, sp)
            hs, carry = _scan_real(a, mult * (i * h), carry)
            hs_ref[pl.ds(r0, RC), :] = hs
            return carry

        lax.fori_loop(0, L // RC, step, jnp.zeros((1, LANE), F32))

    tile, bd = _rg_specs(L)
    return pl.pallas_call(
        body, name="rg_fwd", grid=(N_RG_T,),
        in_specs=[tile(L), tile(4), tile(1), bd, bd, tile(1), tile(1), tile(1)],
        out_specs=tile(L), out_shape=_S((L, RG_W)), compiler_params=_params(1))(z, cw, cb, wa_bd, wx_bd, ba, bx, lam)


def _rg_bwd(dhs, z, hs, cw, cb, wa_bd, wx_bd, ba, bx, lam):
    L = z.shape[0]

    def body(g_ref, x_ref, hs_ref, cw_ref, cb_ref, wa_ref, wx_ref, ba_ref, bx_ref, lam_ref,
             dx_ref, dcw_ref, dcb_ref, dwa_ref, dwx_ref, dba_ref, dbx_ref, dlam_ref):
        w, b = cw_ref[...], cb_ref[...]
        wa, wx, ba_, bx_ = wa_ref[...].astype(MXU), wx_ref[...].astype(MXU), ba_ref[...], bx_ref[...]
        lam = lam_ref[...]
        sp = _softplus(-lam)
        rows = lax.broadcasted_iota(jnp.int32, (RC, LANE), 0)
        for ref in (dcw_ref, dcb_ref, dwa_ref, dwx_ref, dba_ref, dbx_ref, dlam_ref):
            ref[...] = jnp.zeros_like(ref)
        nch = L // RC

        def step(k, carry):
            cin, nxt = carry
            c = nch - 1 - k
            r0 = pl.multiple_of(c * RC, RC)
            xe = jnp.concatenate([_halo(x_ref, c, r0), x_ref[pl.ds(r0, RC), :]], axis=0)
            t = _conv_taps(xe)
            h = t[0] * w[0:1] + t[1] * w[1:2] + t[2] * w[2:3] + t[3] * w[3:4] + b
            r, i, a, mult = _rg_gates(h, wa, wx, ba_, bx_, sp)
            hs_e = jnp.concatenate([_halo(hs_ref, c, r0), hs_ref[pl.ds(r0, RC), :]], axis=0)
            hs_prev = pltpu.roll(hs_e, 1, 0)[8:, :]
            g = g_ref[pl.ds(r0, RC), :]
            cc, cin_new = _scan_real(a, a * g, cin, reverse=True)
            dh = g + _up(cc, 1, rows, cin)
            ih = i * h
            dlog_a = dh * hs_prev * a - (dh * ih) * (a * a) / mult
            di = dh * mult * h
            dhin = dh * mult * i
            dr = dlog_a * ((-RG_C) * sp)
            dlam_ref[...] += _colsum(dlog_a * r)
            dra = dr * r * (1.0 - r)
            dia = di * i * (1.0 - i)
            dwa_ref[...] += _mm_tn(h, dra)
            dwx_ref[...] += _mm_tn(h, dia)
            dba_ref[...] += _colsum(dra)
            dbx_ref[...] += _colsum(dia)
            dhin = dhin + _mm_nt(dra, wa) + _mm_nt(dia, wx)
            de = jnp.concatenate([dhin, nxt], axis=0)
            n = RC + 8
            dx = (dhin * w[3:4] + pltpu.roll(de, n - 1, 0)[:RC, :] * w[2:3]
                  + pltpu.roll(de, n - 2, 0)[:RC, :] * w[1:2] + pltpu.roll(de, n - 3, 0)[:RC, :] * w[0:1])
            dx_ref[pl.ds(r0, RC), :] = dx
            for kk in range(4):
                dcw_ref[kk:kk + 1, :] += _colsum(dhin * t[kk])
            dcb_ref[...] += _colsum(dhin)
            return cin_new, dhin[0:8, :]

        lax.fori_loop(0, nch, step, (jnp.zeros((1, LANE), F32), jnp.zeros((8, LANE), F32)))
        dlam_ref[...] = dlam_ref[...] * (RG_C * _sigmoid(-lam))

    tile, bd = _rg_specs(L)
    return pl.pallas_call(
        body, name="rg_bwd", grid=(N_RG_T,),
        in_specs=[tile(L), tile(L), tile(L), tile(4), tile(1), bd, bd, tile(1), tile(1), tile(1)],
        out_specs=[tile(L), tile(4), tile(1), bd, bd, tile(1), tile(1), tile(1)],
        out_shape=[_S((L, RG_W)), _S((4, RG_W)), _S((1, RG_W)), _S((N_RG_T, LANE, LANE)), _S((N_RG_T, LANE, LANE)),
                   _S((1, RG_W)), _S((1, RG_W)), _S((1, RG_W))],
        compiler_params=_params(1))(dhs, z, hs, cw, cb, wa_bd, wx_bd, ba, bx, lam)


def _cmul(ar, ai, br, bi):
    return ar * br - ai * bi, ar * bi + ai * br


S5_TW = S5_N // N_S5_T


def _s5_specs(L):
    in_tile = pl.BlockSpec((L, LANE), lambda t: (0, t))
    st = pl.BlockSpec((L, S5_TW), lambda t: (0, t))
    bb = pl.BlockSpec((None, LANE, S5_TW), lambda t: (t, 0, 0))
    cc = pl.BlockSpec((None, S5_TW, LANE), lambda t: (t, 0, 0))
    lb = pl.BlockSpec((1, S5_TW), lambda t: (0, t))
    dv = pl.BlockSpec((1, LANE), lambda t: (0, t))
    return in_tile, st, bb, cc, lb, dv


def _s5_fwd(z, bb_re, bb_im, lb_re, lb_im, c_re, c_im, dvec):
    L = z.shape[0]

    def body(u_ref, bbr_ref, bbi_ref, lr_ref, li_ref, cr_ref, ci_ref, d_ref, y_ref, sr_ref, si_ref):
        bbr, bbi = bbr_ref[...].astype(MXU), bbi_ref[...].astype(MXU)
        cr, ci = cr_ref[...].astype(MXU), ci_ref[...].astype(MXU)
        dv = d_ref[...]
        steps, e = _tile_powers(lr_ref[...], li_ref[...])

        def step(c, carry):
            r0 = pl.multiple_of(c * RC, RC)
            u = u_ref[pl.ds(r0, RC), :]
            ub = u.astype(MXU)
            sr = jnp.dot(ub, bbr, preferred_element_type=F32)
            si = jnp.dot(ub, bbi, preferred_element_type=F32)
            sr, si, carry = _scan_lti(sr, si, carry, steps, e)
            sr_ref[pl.ds(r0, RC), :] = sr
            si_ref[pl.ds(r0, RC), :] = si
            y_ref[pl.ds(r0, RC), :] = dv * u + (_mm(sr, cr) - _mm(si, ci))
            return carry

        zero = jnp.zeros((1, S5_TW), F32)
        lax.fori_loop(0, L // RC, step, (zero, zero))

    in_tile, st, bb, cc, lb, dv = _s5_specs(L)
    u_tile = pl.BlockSpec((L, LANE), lambda t: (0, C_S5U // LANE + t))
    return pl.pallas_call(
        body, name="s5_fwd", grid=(N_S5_T,),
        in_specs=[u_tile, bb, bb, lb, lb, cc, cc, dv],
        out_specs=[in_tile, st, st],
        out_shape=[_S((L, S5_W)), _S((L, S5_N)), _S((L, S5_N))],
        compiler_params=_params(1))(z, bb_re, bb_im, lb_re, lb_im, c_re, c_im, dvec)


def _s5_bwd(dy0, z, s_re, s_im, bb_re, bb_im, lb_re, lb_im, c_re, c_im, dvec, token=None):
    L = z.shape[0]
    extra, extra_specs = _after(token)

    def body(dy_ref, u_ref, sr_ref, si_ref, bbr_ref, bbi_ref, lr_ref, li_ref, cr_ref, ci_ref, d_ref, *rest):
        du_ref, dbbr_ref, dbbi_ref, dlr_ref, dli_ref, dcr_ref, dci_ref, dd_ref = rest[len(extra):]
        bbr, bbi = bbr_ref[...].astype(MXU), bbi_ref[...].astype(MXU)
        cr, ci = cr_ref[...].astype(MXU), ci_ref[...].astype(MXU)
        lr, li = lr_ref[...], -li_ref[...]
        dv = d_ref[...]
        steps, e = _tile_powers(lr, li, reverse=True)
        for ref in (dbbr_ref, dbbi_ref, dlr_ref, dli_ref, dcr_ref, dci_ref, dd_ref):
            ref[...] = jnp.zeros_like(ref)
        nch = L // RC

        def step(k, carry):
            c = nch - 1 - k
            r0 = pl.multiple_of(c * RC, RC)
            dy = dy_ref[pl.ds(r0, RC), :]
            u = u_ref[pl.ds(r0, RC), :]
            dyb, ub = dy.astype(MXU), u.astype(MXU)
            sr, si = sr_ref[pl.ds(r0, RC), :], si_ref[pl.ds(r0, RC), :]
            dcr_ref[...] += _mm_tn(sr, dyb)
            dci_ref[...] -= _mm_tn(si, dyb)
            gr = _mm_nt(dyb, cr)
            gi = -_mm_nt(dyb, ci)
            gr, gi, carry = _scan_lti(gr, gi, carry, steps, e, reverse=True)
            pr_ = pltpu.roll(jnp.concatenate([_halo(sr_ref, c, r0), sr], axis=0), 1, 0)[8:, :]
            pi_ = pltpu.roll(jnp.concatenate([_halo(si_ref, c, r0), si], axis=0), 1, 0)[8:, :]
            dlr_ref[...] += _colsum(pr_ * gr + pi_ * gi)
            dli_ref[...] += _colsum(pr_ * gi - pi_ * gr)
            grb, gib = gr.astype(MXU), gi.astype(MXU)
            dbbr_ref[...] += _mm_tn(ub, grb)
            dbbi_ref[...] += _mm_tn(ub, gib)
            du_ref[pl.ds(r0, RC), :] = dv * dy + (_mm_nt(grb, bbr) + _mm_nt(gib, bbi))
            dd_ref[...] += _colsum(dy * u)
            return carry

        zero = jnp.zeros((1, S5_TW), F32)
        lax.fori_loop(0, nch, step, (zero, zero))

    in_tile, st, bb, cc, lb, dv = _s5_specs(L)
    u_tile = pl.BlockSpec((L, LANE), lambda t: (0, C_S5U // LANE + t))
    return pl.pallas_call(
        body, name="s5_bwd", grid=(N_S5_T,),
        in_specs=[in_tile, u_tile, st, st, bb, bb, lb, lb, cc, cc, dv] + extra_specs,
        out_specs=[in_tile, bb, bb, lb, lb, cc, cc, dv],
        out_shape=[_S((L, S5_W)), _S((N_S5_T, LANE, S5_TW)), _S((N_S5_T, LANE, S5_TW)), _S((1, S5_N)), _S((1, S5_N)),
                   _S((N_S5_T, S5_TW, LANE)), _S((N_S5_T, S5_TW, LANE)), _S((1, S5_W))],
        compiler_params=_params(1))(dy0, z, s_re, s_im, bb_re, bb_im, lb_re, lb_im, c_re, c_im, dvec, *extra)


def _disc(ar, ai, ls):
    dt = jnp.exp(ls)
    mag = jnp.exp(ar * dt)
    lr = mag * jnp.cos(ai * dt)
    li = mag * jnp.sin(ai * dt)
    den = ar * ar + ai * ai
    cr = ((lr - 1.0) * ar + li * ai) / den
    ci = (li * ar - (lr - 1.0) * ai) / den
    return lr, li, cr, ci


def _s5_disc_fwd(ar, ai, ls):
    def body(ar_ref, ai_ref, ls_ref, lr_ref, li_ref, cr_ref, ci_ref):
        lr, li, cr, ci = _disc(ar_ref[...], ai_ref[...], ls_ref[...])
        lr_ref[...], li_ref[...], cr_ref[...], ci_ref[...] = lr, li, cr, ci

    sh = _S(ar.shape)
    return pl.pallas_call(body, name="s5_disc_fwd", out_shape=[sh, sh, sh, sh])(ar, ai, ls)


def _s5_disc_bwd(ar, ai, ls, dlr, dli, dcr, dci):
    def body(ar_ref, ai_ref, ls_ref, dlr_ref, dli_ref, dcr_ref, dci_ref, dar_ref, dai_ref, dls_ref):
        _, vjp = jax.vjp(_disc, ar_ref[...], ai_ref[...], jnp.broadcast_to(ls_ref[...], ar_ref.shape))
        dar, dai, dls = vjp((dlr_ref[...], dli_ref[...], dcr_ref[...], dci_ref[...]))
        dar_ref[...], dai_ref[...] = dar, dai
        dls_ref[...] = jnp.sum(dls, axis=1, keepdims=True)

    return pl.pallas_call(body, name="s5_disc_bwd", out_shape=[_S(ar.shape), _S(ar.shape), _S(ls.shape)])(
        ar, ai, ls, dlr, dli, dcr, dci)


def _s5_bscale_fwd(cr, ci, br, bi):
    def body(cr_ref, ci_ref, br_ref, bi_ref, or_ref, oi_ref):
        or_ref[...], oi_ref[...] = _cmul(cr_ref[...], ci_ref[...], br_ref[...], bi_ref[...])

    return pl.pallas_call(body, name="s5_bscale_fwd", out_shape=[_S(br.shape), _S(br.shape)])(cr, ci, br, bi)


def _s5_bscale_bwd(cr, ci, br, bi, gr, gi):
    def body(cr_ref, ci_ref, br_ref, bi_ref, gr_ref, gi_ref, dbr_ref, dbi_ref, dcr_ref, dci_ref):
        cr_, ci_, br_, bi_, gr_, gi_ = (r[...] for r in (cr_ref, ci_ref, br_ref, bi_ref, gr_ref, gi_ref))
        dbr_ref[...] = cr_ * gr_ + ci_ * gi_
        dbi_ref[...] = cr_ * gi_ - ci_ * gr_
        dcr_ref[...] = jnp.sum(gr_ * br_ + gi_ * bi_, axis=1, keepdims=True)
        dci_ref[...] = jnp.sum(gi_ * br_ - gr_ * bi_, axis=1, keepdims=True)

    return pl.pallas_call(body, name="s5_bscale_bwd",
                          out_shape=[_S(br.shape), _S(br.shape), _S(cr.shape), _S(cr.shape)])(cr, ci, br, bi, gr, gi)


def _row(w):
    return pl.BlockSpec((TM, w), lambda i: (i, 0))


def _full(shape):
    return pl.BlockSpec(tuple(shape), lambda i: (0,) * len(shape))


def _post_fwd(x, hs, z, y0, p, w_glu, b_glu, w_out, g1, b1, ple_w, w_pg, b_pg, g2, b2):
    L = x.shape[0]

    def body(x_ref, hs_ref, z_ref, y0_ref, p_ref, wg_ref, bg_ref, wo_ref, g1_ref, b1_ref, pw_ref, wpg_ref, bpg_ref,
             g2_ref, b2_ref, x2_ref, t1_ref, t2_ref, m_ref):
        rg_gate = z_ref[:, C_RGG:C_RGG + RG_W]
        s5_gate = z_ref[:, C_S5G:C_S5G + S5_W]
        rg_y = hs_ref[...] * _silu_and_grad(rg_gate)[0]
        y1 = _gelu(y0_ref[...])
        gl = _sigmoid(_mm(y1, wg_ref[...]) + bg_ref[...])
        s5_y = (y1 * gl) * _silu_and_grad(s5_gate)[0]
        m_ref[:, :RG_W] = rg_y
        m_ref[:, RG_W:] = s5_y
        mix = _mm(m_ref[...], wo_ref[...])
        t1 = ALPHA * x_ref[...] + mix
        x1, _, _ = _ln_fwd(t1, g1_ref[...], b1_ref[...])
        e = _mm(p_ref[...], pw_ref[...]) * _sigmoid(_mm(x1, wpg_ref[...]) + bpg_ref[...])
        t2 = ALPHA * x1 + e
        x2, _, _ = _ln_fwd(t2, g2_ref[...], b2_ref[...])
        t1_ref[...], t2_ref[...], x2_ref[...] = t1, t2, x2

    vec = _full((1, D_MODEL))
    return pl.pallas_call(
        body, name="post_fwd", grid=(L // TM,),
        in_specs=[_row(D_MODEL), _row(RG_W), _row(Z_W), _row(S5_W), _row(256), _full((S5_W, S5_W)), _full((1, S5_W)),
                  _full((D_MODEL, D_MODEL)), vec, vec, _full((256, D_MODEL)), _full((D_MODEL, D_MODEL)), vec, vec, vec],
        out_specs=[_row(D_MODEL)] * 4, out_shape=[_S((L, D_MODEL))] * 4,
        compiler_params=_params(1))(x, hs, z, y0, p, w_glu, b_glu, w_out, g1, b1, ple_w, w_pg, b_pg, g2, b2)


def _post_bwd_a(dx2_or_target, is_top, t2, t1, p, ple_w, w_pg, b_pg, g1, b1, g2, b2, token=None):
    L = t1.shape[0]
    extra, extra_specs = _after(token)

    def body(d_ref, t2_ref, t1_ref, p_ref, pw_ref, wpg_ref, bpg_ref, g1_ref, b1_ref, g2_ref, b2_ref, *rest):
        (dt1_ref, dpw_out, dwpg_out, dbpg_ref, dg1_ref, db1_ref, dg2_ref, db2_ref, loss_ref, dpw_ref,
         dwpg_ref) = rest[len(extra):]
        @pl.when(pl.program_id(0) == 0)
        def _():
            for ref in (dpw_ref, dwpg_ref, dbpg_ref, dg1_ref, db1_ref, dg2_ref, db2_ref, loss_ref):
                ref[...] = jnp.zeros_like(ref)

        g1, g2 = g1_ref[...], g2_ref[...]
        x1, xh1, rstd1 = _ln_fwd(t1_ref[...], g1, b1_ref[...])
        x2, xh2, rstd2 = _ln_fwd(t2_ref[...], g2, b2_ref[...])
        if is_top:
            err = x2 - d_ref[...]
            loss_ref[...] += _colsum(err * err)
            dx2 = err * (1.0 / D_MODEL)
        else:
            dx2 = d_ref[...]
        p = p_ref[...]
        q = _mm(p, pw_ref[...])
        gt = _sigmoid(_mm(x1, wpg_ref[...]) + bpg_ref[...])
        dg2_ref[...] += _colsum(dx2 * xh2)
        db2_ref[...] += _colsum(dx2)
        dt2 = _ln_bwd(dx2, xh2, rstd2, g2)
        dq = dt2 * gt
        dgpre = (dt2 * q) * gt * (1.0 - gt)
        dpw_ref[...] += _mm_tn(p, dq)
        dwpg_ref[...] += _mm_tn(x1, dgpre)
        dbpg_ref[...] += _colsum(dgpre)
        dx1 = ALPHA * dt2 + _mm_nt(dgpre, wpg_ref[...])
        dg1_ref[...] += _colsum(dx1 * xh1)
        db1_ref[...] += _colsum(dx1)
        dt1_ref[...] = _ln_bwd(dx1, xh1, rstd1, g1)

        @pl.when(pl.program_id(0) == L // TM - 1)
        def _():
            dpw_out[...] = dpw_ref[...].astype(WIRE)
            dwpg_out[...] = dwpg_ref[...].astype(WIRE)

    vec = _full((1, D_MODEL))
    return pl.pallas_call(
        body, name="post_bwd_a_top" if is_top else "post_bwd_a", grid=(L // TM,),
        in_specs=[_row(D_MODEL), _row(D_MODEL), _row(D_MODEL), _row(256), _full((256, D_MODEL)),
                  _full((D_MODEL, D_MODEL)), vec, vec, vec, vec, vec] + extra_specs,
        out_specs=[_row(D_MODEL), _full((256, D_MODEL)), _full((D_MODEL, D_MODEL)), vec, vec, vec, vec, vec, vec],
        out_shape=[_S((L, D_MODEL)), _S((256, D_MODEL), WIRE), _S((D_MODEL, D_MODEL), WIRE)] + [_S((1, D_MODEL))] * 6,
        scratch_shapes=[pltpu.VMEM((256, D_MODEL), F32), pltpu.VMEM((D_MODEL, D_MODEL), F32)],
        compiler_params=_params(1))(dx2_or_target, t2, t1, p, ple_w, w_pg, b_pg, g1, b1, g2, b2, *extra)


def _post_bwd_b(dt1, m, z, hs, y0, w_out, w_glu, b_glu):
    L = dt1.shape[0]

    def body(dt1_ref, m_ref, z_ref, hs_ref, y0_ref, wo_ref, wg_ref, bg_ref,
             dhs_ref, dy0_ref, dzg_ref, dwo_out, dwg_out, dbg_ref, dwo_ref, dwg_ref):
        @pl.when(pl.program_id(0) == 0)
        def _():
            for ref in (dwo_ref, dwg_ref, dbg_ref):
                ref[...] = jnp.zeros_like(ref)

        dt1b = dt1_ref[...].astype(MXU)
        dm = _mm_nt(dt1b, wo_ref[...])
        dwo_ref[...] += _mm_tn(m_ref[...], dt1b)
        d_rgy, d_s5y = dm[:, :RG_W], dm[:, RG_W:]
        rg_gate = z_ref[:, C_RGG:C_RGG + RG_W]
        s5_gate = z_ref[:, C_S5G:C_S5G + S5_W]
        sl, dsl = _silu_and_grad(rg_gate)
        dhs_ref[...] = d_rgy * sl
        dzg_ref[:, :RG_W] = d_rgy * hs_ref[...] * dsl
        y0 = y0_ref[...]
        y1 = _gelu(y0)
        gl = _sigmoid(_mm(y1, wg_ref[...]) + bg_ref[...])
        sl, dsl = _silu_and_grad(s5_gate)
        dy2 = d_s5y * sl
        dzg_ref[:, RG_W:] = d_s5y * (y1 * gl) * dsl
        dglpre = (dy2 * y1) * gl * (1.0 - gl)
        dwg_ref[...] += _mm_tn(y1, dglpre)
        dbg_ref[...] += _colsum(dglpre)
        dy1 = dy2 * gl + _mm_nt(dglpre, wg_ref[...])
        dy0_ref[...] = dy1 * _gelu_grad(y0)

        @pl.when(pl.program_id(0) == L // TM - 1)
        def _():
            dwo_out[...] = dwo_ref[...].astype(WIRE)
            dwg_out[...] = dwg_ref[...].astype(WIRE)

    return pl.pallas_call(
        body, name="post_bwd_b", grid=(L // TM,),
        in_specs=[_row(D_MODEL), _row(D_MODEL), _row(Z_W), _row(RG_W), _row(S5_W), _full((D_MODEL, D_MODEL)),
                  _full((S5_W, S5_W)), _full((1, S5_W))],
        out_specs=[_row(RG_W), _row(S5_W), _row(D_MODEL), _full((D_MODEL, D_MODEL)), _full((S5_W, S5_W)), _full((1, S5_W))],
        out_shape=[_S((L, RG_W)), _S((L, S5_W)), _S((L, D_MODEL)), _S((D_MODEL, D_MODEL), WIRE), _S((S5_W, S5_W), WIRE),
                   _S((1, S5_W))],
        scratch_shapes=[pltpu.VMEM((D_MODEL, D_MODEL), F32), pltpu.VMEM((S5_W, S5_W), F32)],
        compiler_params=_params(1))(dt1, m, z, hs, y0, w_out, w_glu, b_glu)


def _adamw(parts, w, m, v):
    nl = len(parts)
    n, R, C = parts[0].shape
    tr = R
    for cand in (512, 256, 128, 64, 32, 16, 8):
        if R % cand == 0 and n * cand * C * 4 <= 4 * 1024 * 1024:
            tr = cand
            break
    nblk = R // tr

    def body(*refs):
        p_refs = refs[:nl]
        w_ref, m_ref, v_ref, g_ref, d_ref, nm_ref, nv_ref = refs[nl:]
        layer = pl.program_id(0)
        g = None
        for li, p_ref in enumerate(p_refs):
            s = p_ref[0].astype(F32)
            for k in range(1, n):
                s = s + p_ref[k].astype(F32)
            g = s if g is None else jnp.where(layer == li, s, g)
        nm = B1 * m_ref[...] + (1.0 - B1) * g
        nv = B2 * v_ref[...] + (1.0 - B2) * (g * g)
        d_ref[...] = (-LR) * ((nm / BC1) / (jnp.sqrt(nv / BC2) + EPS) + WD * w_ref[...])
        g_ref[...], nm_ref[...], nv_ref[...] = g, nm, nv

    def part_spec(li):
        return pl.BlockSpec((n, tr, C), lambda l, i: (0, jnp.where(l == li, i, jnp.where(l < li, 0, nblk - 1)), 0))

    blk = pl.BlockSpec((tr, C), lambda l, i: (l * nblk + i, 0))
    return pl.pallas_call(
        body, name="adamw", grid=(nl, nblk),
        in_specs=[part_spec(li) for li in range(nl)] + [blk, blk, blk],
        out_specs=[blk] * 4, out_shape=[_S((nl * R, C))] * 4, compiler_params=_params(2))(*parts, w, m, v)


def _me():
    return lax.axis_index("x"), lax.axis_index("y"), lax.axis_index("c")


def _lin(dev):
    return 4 * dev[0] + 2 * dev[1] + dev[2]


def _blk(ref, axis, size, idx):
    nd = len(ref.shape)
    start = idx * size
    if axis == nd - 1 and size % LANE == 0:
        start = pl.multiple_of(start, LANE)
    elif axis == nd - 2 and size % 16 == 0:
        start = pl.multiple_of(start, 16)
    ix = [slice(None)] * nd
    ix[axis] = pl.ds(start, size)
    return ref.at[tuple(ix)]


def _all_gather(shards, axes, name):
    n = len(shards)
    sizes = [s.shape[a] for s, a in zip(shards, axes)]
    out_shapes = [_S(s.shape[:a] + (N_DEV * s.shape[a],) + s.shape[a + 1:], s.dtype) for s, a in zip(shards, axes)]

    def body(*refs):
        ins, outs = refs[:n], refs[n:2 * n]
        send_sems, recv_sems, local_sems = refs[2 * n:]
        x, y, c = _me()
        me, sibling = (x, y, c), (x, y, 1 - c)
        chips = [(1 - x, y), (x, 1 - y), (1 - x, 1 - y)]

        def copy(a, k, block, to, from_input=False):
            dst = _blk(outs[a], axes[a], sizes[a], _lin(block))
            return pltpu.make_async_remote_copy(
                src_ref=ins[a] if from_input else dst, dst_ref=dst, send_sem=send_sems.at[a, k],
                recv_sem=recv_sems.at[a, k], device_id=to, device_id_type=MESH)

        mine = [pltpu.make_async_copy(ins[a], _blk(outs[a], axes[a], sizes[a], _lin(me)), local_sems.at[a]) for a in range(n)]
        for cp in mine:
            cp.start()
        first = []
        for a in range(n):
            first.append(copy(a, 0, me, sibling, True))
            first += [copy(a, 1 + j, me, (*chip, c), True) for j, chip in enumerate(chips)]
        for cp in first:
            cp.start()
        passed = []
        for j, chip in enumerate(chips):
            for a in range(n):
                copy(a, 1 + j, (*chip, c), me).wait_recv()
                cp = copy(a, 4 + j, (*chip, c), sibling)
                cp.start()
                passed.append(cp)
        for a in range(n):
            copy(a, 0, sibling, me).wait_recv()
            for j, chip in enumerate(chips):
                copy(a, 4 + j, (*chip, 1 - c), me).wait_recv()
        for cp in first + passed:
            cp.wait_send()
        for cp in mine:
            cp.wait()

    return pl.pallas_call(
        body, name=name, out_shape=out_shapes, in_specs=[ANY] * n, out_specs=[ANY] * n,
        scratch_shapes=[pltpu.SemaphoreType.DMA((n, 7)), pltpu.SemaphoreType.DMA((n, 7)), pltpu.SemaphoreType.DMA((n,))],
    )(*shards)


def _exchange(groups, axes, name):
    arrays = [a for g in groups for a in g]
    where = [(o, i) for o, g in enumerate(groups) for i in range(len(g))]
    ax = [axes[o] for o, _ in where]
    n = len(arrays)
    sizes = [s.shape[a] // N_DEV for s, a in zip(arrays, ax)]
    out_shapes = []
    for g, a in zip(groups, axes):
        s = g[0].shape
        out_shapes.append(_S((N_DEV, len(g)) + s[:a] + (s[a] // N_DEV,) + s[a + 1:], g[0].dtype))

    def body(*refs):
        ins, outs = refs[:n], refs[n:n + len(groups)]
        send_sems, recv_sems, local_sems = refs[n + len(groups):]
        x, y, c = _me()
        me = (x, y, c)
        flip = lambda v, f: 1 - v if f else v
        peers = [(flip(x, k & 4), flip(y, k & 2), flip(c, k & 1)) for k in range(1, N_DEV)]

        def land(a, sender):
            o, i = where[a]
            return outs[o].at[_lin(sender), i]

        def copy(a, k, to):
            return pltpu.make_async_remote_copy(
                src_ref=_blk(ins[a], ax[a], sizes[a], _lin(to)), dst_ref=land(a, me),
                send_sem=send_sems.at[a, k], recv_sem=recv_sems.at[a, k], device_id=to, device_id_type=MESH)

        mine = [pltpu.make_async_copy(_blk(ins[a], ax[a], sizes[a], _lin(me)), land(a, me), local_sems.at[a]) for a in range(n)]
        for cp in mine:
            cp.start()
        sends = [copy(a, k, peer) for a in range(n) for k, peer in enumerate(peers)]
        for cp in sends:
            cp.start()
        for a in range(n):
            for k, peer in enumerate(peers):
                pltpu.make_async_remote_copy(
                    src_ref=land(a, peer), dst_ref=land(a, peer), send_sem=send_sems.at[a, k],
                    recv_sem=recv_sems.at[a, k], device_id=peer, device_id_type=MESH).wait_recv()
        for cp in sends:
            cp.wait_send()
        for cp in mine:
            cp.wait()

    return pl.pallas_call(
        body, name=name, out_shape=out_shapes, in_specs=[ANY] * n, out_specs=[ANY] * len(groups),
        scratch_shapes=[pltpu.SemaphoreType.DMA((n, 7)), pltpu.SemaphoreType.DMA((n, 7)), pltpu.SemaphoreType.DMA((n,))],
    )(*arrays)


HBM_SPEC = pl.BlockSpec(memory_space=pltpu.HBM)
SEM_SPEC = pl.BlockSpec(memory_space=pltpu.SEMAPHORE)
EFFECT = pltpu.SideEffectType.DATAFLOW_SIDE_EFFECTING


def _peers(x, y, c):
    flip = lambda v, f: 1 - v if f else v
    return [(flip(x, k & 4), flip(y, k & 2), flip(c, k & 1)) for k in range(1, N_DEV)]


def _land_shape(mode, s, axis):
    if mode == "gather":
        return s.shape[:axis] + (N_DEV * s.shape[axis],) + s.shape[axis + 1:]
    return (N_DEV,) + s.shape[:axis] + (s.shape[axis] // N_DEV,) + s.shape[axis + 1:]


def _src_view(mode, ref, axis, peer):
    return ref if mode == "gather" else _blk(ref, axis, ref.shape[axis] // N_DEV, peer)


def _dst_view(mode, land, axis, sender):
    return _blk(land, axis, land.shape[axis] // N_DEV, sender) if mode == "gather" else land.at[sender]


def _seven_blocks(mode, land, axis):
    if mode == "gather":
        ix = [slice(None)] * len(land.shape)
        ix[axis] = pl.ds(0, (N_DEV - 1) * (land.shape[axis] // N_DEV))
        return land.at[tuple(ix)]
    return land.at[pl.ds(0, N_DEV - 1)]


def _place_own(mode, srcs, axes, name, after=None):
    n = len(srcs)
    extra, extra_specs = _after(after)

    def body(me_ref, *refs):
        for a in range(n):
            out = refs[n + len(extra) + a]
            out[...] = refs[a][...].reshape(out.shape)

    def at_me(shape, axis):
        return lambda i, me: tuple(me[0] if d == axis else 0 for d in range(len(shape)))

    in_specs, out_specs = [], []
    for s, axis in zip(srcs, axes):
        if mode == "gather":
            in_specs.append(pl.BlockSpec(s.shape, lambda i, me, nd=len(s.shape): (0,) * nd))
            out_specs.append(pl.BlockSpec(s.shape, at_me(s.shape, axis)))
        else:
            blk = s.shape[:axis] + (s.shape[axis] // N_DEV,) + s.shape[axis + 1:]
            in_specs.append(pl.BlockSpec(blk, at_me(blk, axis)))
            out_specs.append(pl.BlockSpec((1,) + blk, at_me((1,) + blk, 0)))
    me = _lin(_me()).astype(jnp.int32).reshape(1)
    return pl.pallas_call(
        body, name=name, out_shape=[_S(_land_shape(mode, s, a), s.dtype) for s, a in zip(srcs, axes)],
        grid_spec=pltpu.PrefetchScalarGridSpec(num_scalar_prefetch=1, grid=(1,), in_specs=in_specs + extra_specs,
                                               out_specs=out_specs),
        compiler_params=_params(1))(me, *srcs, *extra)


def _push_start(mode, srcs, lands, axes, name):
    n = len(srcs)

    def body(*refs):
        src_refs, land_refs = refs[:n], refs[n:2 * n]
        send_sems, recv_sems = refs[2 * n], refs[2 * n + 1]
        token = refs[-1]
        x, y, c = _me()
        me = _lin((x, y, c))
        for a in range(n):
            for peer in _peers(x, y, c):
                pltpu.make_async_remote_copy(
                    src_ref=_src_view(mode, src_refs[a], axes[a], _lin(peer)),
                    dst_ref=_dst_view(mode, land_refs[a], axes[a], me),
                    send_sem=send_sems.at[a], recv_sem=recv_sems.at[a], device_id=peer, device_id_type=MESH).start()
        token[...] = jnp.zeros_like(token)

    hbm = lambda s: pltpu.HBM(s.shape, s.dtype)
    outs = pl.pallas_call(
        body, name=name,
        out_shape=(pltpu.SemaphoreType.DMA((n,)), pltpu.SemaphoreType.DMA((n,)), *[hbm(s) for s in srcs], *[hbm(s) for s in lands],
                   _S((SUB, LANE))),
        in_specs=[HBM_SPEC] * (2 * n),
        out_specs=(SEM_SPEC, SEM_SPEC, *[HBM_SPEC] * (2 * n), pl.BlockSpec(memory_space=pltpu.VMEM)),
        input_output_aliases={i: 2 + i for i in range(2 * n)},
        compiler_params=pltpu.CompilerParams(has_side_effects=EFFECT),
    )(*[pltpu.with_memory_space_constraint(s, pltpu.HBM) for s in list(srcs) + list(lands)])
    return outs[0], outs[1], outs[2:2 + n], outs[2 + n:2 + 2 * n], outs[-1]


def _push_wait(mode, send_sems, recv_sems, srcs, lands, axes, after, name):
    n = len(srcs)

    def body(*refs):
        land_refs = refs[n:2 * n]
        send_sems, recv_sems = refs[2 * n], refs[2 * n + 1]
        x, y, c = _me()
        for a in range(n):
            seven = _seven_blocks(mode, land_refs[a], axes[a])
            cp = pltpu.make_async_remote_copy(src_ref=seven, dst_ref=seven, send_sem=send_sems.at[a], recv_sem=recv_sems.at[a],
                                              device_id=(x, y, 1 - c), device_id_type=MESH)
            cp.wait_send()
            cp.wait_recv()

    hbm = lambda s: pltpu.HBM(s.shape, s.dtype)
    outs = pl.pallas_call(
        body, name=name, out_shape=tuple(hbm(s) for s in list(srcs) + list(lands)),
        in_specs=[HBM_SPEC] * (2 * n) + [SEM_SPEC, SEM_SPEC, ANY], out_specs=tuple([HBM_SPEC] * (2 * n)),
        input_output_aliases={i: i for i in range(2 * n)},
        compiler_params=pltpu.CompilerParams(has_side_effects=EFFECT),
    )(*srcs, *lands, send_sems, recv_sems, after)
    return outs[n:]


def _sum_parts(parts):
    n, R, C = parts.shape

    def body(p_ref, o_ref):
        g = p_ref[0]
        for k in range(1, n):
            g = g + p_ref[k]
        o_ref[...] = g

    return pl.pallas_call(body, name="sum_parts", out_shape=_S((R, C)))(parts)


def _block_diag(w, nb):
    tn, r, c = w.shape
    w = w.reshape(tn // nb, nb, r, c)
    return jnp.einsum('tarc,ab->tarbc', w, jnp.eye(nb, dtype=w.dtype)).reshape(tn // nb, nb * r, nb * c)


def _block_diag_extract(w, nb):
    t, R, C = w.shape
    w = w.reshape(t, nb, R // nb, nb, C // nb)
    return jnp.einsum('tarbc,ab->tarc', w, jnp.eye(nb, dtype=w.dtype)).reshape(t * nb, R // nb, C // nb)


SMALL = ['conv_b', 'rg_wa', 'rg_ba', 'rg_wx', 'rg_bx', 'rg_lambda', 's5_a_re', 's5_a_im', 's5_b_re', 's5_b_im',
         's5_c_re', 's5_c_im', 's5_d', 's5_log_step', 's5_b_glu', 'ln1_g', 'ln1_b', 'ple_gate_b', 'ln2_g', 'ln2_b']
WEIGHTS = ['w_in', 'conv_w', 'conv_b', 'rg_wa', 'rg_ba', 'rg_wx', 'rg_bx', 'rg_lambda', 's5_a_re', 's5_a_im', 's5_b_re',
           's5_b_im', 's5_c_re', 's5_c_im', 's5_d', 's5_log_step', 's5_w_glu', 's5_b_glu', 'w_out', 'ln1_g', 'ln1_b',
           'ple_w', 'ple_gate_w', 'ple_gate_b', 'ln2_g', 'ln2_b']
PACK_ROWS_MULT = 64


def _pack(tree):
    flat = jnp.concatenate([tree[k].reshape(-1) for k in SMALL])
    rows = -(-flat.shape[0] // (LANE * PACK_ROWS_MULT)) * PACK_ROWS_MULT
    return jnp.pad(flat, (0, rows * LANE - flat.shape[0])).reshape(rows, LANE)


def _unpack(packed, like):
    flat, out, o = packed.reshape(-1), {}, 0
    for k in SMALL:
        n = math.prod(like[k].shape)
        out[k] = flat[o:o + n].reshape(like[k].shape)
        o += n
    return out


class _NoHooks:
    token = None

    def layer_start(self, i, W, after):
        return W

    def late_weights(self, i, W, after):
        return W

    def post_done(self, i, g):
        return None

    def layer_done(self, i, g, dx):
        return None


def _local_grads(x, p, target, W, disc, hooks):
    depth = 2
    saved = []
    for i in range(depth):
        if i > 0:
            W = hooks.layer_start(i, W, x)
        w = W[i]
        z = _inproj_fwd(x, w['w_in'], hooks.token if i == 0 else None)
        hs = _rg_fwd(z, w['conv_w'], w['conv_b'], w['wa_bd'], w['wx_bd'], w['rg_ba'], w['rg_bx'], w['rg_lambda'])
        d = disc[i]
        y0, s_re, s_im = _s5_fwd(z, d['bb_re'], d['bb_im'], d['lb_re'], d['lb_im'], d['c_re'], d['c_im'], w['s5_d'])
        W = hooks.late_weights(i, W, y0)
        w = W[i]
        x2, t1, t2, m = _post_fwd(x, hs, z, y0, p[i], w['s5_w_glu'], w['s5_b_glu'], w['w_out'], w['ln1_g'], w['ln1_b'],
                                  w['ple_w'], w['ple_gate_w'], w['ple_gate_b'], w['ln2_g'], w['ln2_b'])
        saved.append((x, z, hs, y0, s_re, s_im, t1, t2, m))
        x = x2

    grads = [None] * depth
    dx = target
    loss = None
    token = None
    for i in reversed(range(depth)):
        w, d = W[i], disc[i]
        xin, z, hs, y0, s_re, s_im, t1, t2, m = saved[i]
        g = {}
        (dt1, g['ple_w'], g['ple_gate_w'], g['ple_gate_b'], g['ln1_g'], g['ln1_b'], g['ln2_g'], g['ln2_b'], lrow) = _post_bwd_a(
            dx, i == depth - 1, t2, t1, p[i], w['ple_w'], w['ple_gate_w'], w['ple_gate_b'], w['ln1_g'], w['ln1_b'],
            w['ln2_g'], w['ln2_b'], token)
        if i == depth - 1:
            loss = 0.5 / D_MODEL * jnp.sum(lrow)
        dhs, dy0, dzg, g['w_out'], g['s5_w_glu'], g['s5_b_glu'] = _post_bwd_b(dt1, m, z, hs, y0, w['w_out'], w['s5_w_glu'],
                                                                           w['s5_b_glu'])
        (dzu, g['bb_re'], g['bb_im'], g['lb_re'], g['lb_im'], g['c_re'], g['c_im'], g['s5_d']) = _s5_bwd(
            dy0, z, s_re, s_im, d['bb_re'], d['bb_im'], d['lb_re'], d['lb_im'], d['c_re'], d['c_im'], w['s5_d'],
            hooks.post_done(i, g))
        (dzx, g['conv_w'], g['conv_b'], g['wa_bd'], g['wx_bd'], g['rg_ba'], g['rg_bx'], g['rg_lambda']) = _rg_bwd(
            dhs, z, hs, w['conv_w'], w['conv_b'], w['wa_bd'], w['wx_bd'], w['rg_ba'], w['rg_bx'], w['rg_lambda'])
        dx, g['w_in'] = _inproj_bwd(dt1, xin, dzx, dzg, dzu, w['w_in'])
        grads[i] = g
        token = hooks.layer_done(i, g, dx)
    return loss, dx, grads


def _s5_layouts_fwd(s5_a_re, s5_a_im, s5_log_step, s5_b_re, s5_b_im, s5_c_re, s5_c_im):
    depth = s5_a_re.shape[0]
    ar, ai = s5_a_re.reshape(depth * 24, S5_P), s5_a_im.reshape(depth * 24, S5_P)
    ls = s5_log_step.reshape(depth * 24, 1)
    lr, li, cr, ci = _s5_disc_fwd(ar, ai, ls)
    col = lambda a: a.reshape(depth * S5_N, 1)
    br, bi = s5_b_re.reshape(depth * S5_N, 16), s5_b_im.reshape(depth * S5_N, 16)
    bbr, bbi = _s5_bscale_fwd(col(cr), col(ci), br, bi)
    disc = []
    for i in range(depth):
        gph = lambda a: a.reshape(depth, 24, S5_P, 16)[i]
        disc.append(dict(
            bb_re=_block_diag(jnp.swapaxes(gph(bbr), 1, 2), 8), bb_im=_block_diag(jnp.swapaxes(gph(bbi), 1, 2), 8),
            lb_re=lr.reshape(depth, 1, S5_N)[i], lb_im=li.reshape(depth, 1, S5_N)[i],
            c_re=_block_diag(jnp.swapaxes(s5_c_re[i], 1, 2), 8), c_im=_block_diag(jnp.swapaxes(s5_c_im[i], 1, 2), 8)))
    return disc, (ar, ai, ls, col(cr), col(ci), br, bi)


def _s5_layouts_bwd(grads, res):
    ar, ai, ls, cr, ci, br, bi = res
    depth = len(grads)
    stack = lambda f: jnp.stack([f(g) for g in grads])
    dbbr = stack(lambda g: jnp.swapaxes(_block_diag_extract(g['bb_re'], 8), 1, 2)).reshape(depth * S5_N, 16)
    dbbi = stack(lambda g: jnp.swapaxes(_block_diag_extract(g['bb_im'], 8), 1, 2)).reshape(depth * S5_N, 16)
    dbr, dbi, dcr, dci = _s5_bscale_bwd(cr, ci, br, bi, dbbr, dbbi)
    gp = lambda a: a.reshape(depth * 24, S5_P)
    dar, dai, dls = _s5_disc_bwd(ar, ai, ls, gp(stack(lambda g: g['lb_re'])), gp(stack(lambda g: g['lb_im'])), gp(dcr), gp(dci))
    return dict(
        s5_a_re=dar.reshape(depth, 24, S5_P), s5_a_im=dai.reshape(depth, 24, S5_P), s5_log_step=dls.reshape(depth, 24),
        s5_b_re=dbr.reshape(depth, 24, S5_P, 16), s5_b_im=dbi.reshape(depth, 24, S5_P, 16),
        s5_c_re=stack(lambda g: jnp.swapaxes(_block_diag_extract(g['c_re'], 8), 1, 2)),
        s5_c_im=stack(lambda g: jnp.swapaxes(_block_diag_extract(g['c_im'], 8), 1, 2)))


LATE = ('w_out', 'ple_w', 'ple_gate_w', 's5_w_glu')


def _layer_weights(full, i):
    row = lambda a: a[i].reshape(1, -1)
    return dict(
        w_in=full['w_in'][i], conv_w=full['conv_w'][i], conv_b=row(full['conv_b']),
        wa_bd=_block_diag(full['rg_wa'][i], 2), wx_bd=_block_diag(full['rg_wx'][i], 2),
        rg_ba=row(full['rg_ba']), rg_bx=row(full['rg_bx']), rg_lambda=row(full['rg_lambda']),
        s5_d=row(full['s5_d']), s5_b_glu=row(full['s5_b_glu']), ln1_g=row(full['ln1_g']), ln1_b=row(full['ln1_b']),
        ple_gate_b=row(full['ple_gate_b']), ln2_g=row(full['ln2_g']), ln2_b=row(full['ln2_b']))


class _AllLocal(_NoHooks):
    def __init__(self, full):
        self.full = full

    def late_weights(self, i, W, after):
        W[i].update({k: self.full[k][i] for k in LATE})
        return W


def _full_grads(full, x, p, target, hooks=None):
    disc, res = _s5_layouts_fwd(full['s5_a_re'], full['s5_a_im'], full['s5_log_step'], full['s5_b_re'], full['s5_b_im'],
                                full['s5_c_re'], full['s5_c_im'])
    W = [_layer_weights(full, i) for i in range(2)]
    loss, gx, grads = _local_grads(x, p, target, W, disc, hooks or _AllLocal(full))
    stack = lambda f: jnp.stack([f(g) for g in grads])
    out = _s5_layouts_bwd(grads, res)
    for k in SHARD_AXIS:
        out[k] = [g[k] for g in grads]
    out['conv_w'] = stack(lambda g: g['conv_w'])
    for k in ('conv_b', 'rg_ba', 'rg_bx', 'rg_lambda', 's5_b_glu', 'ln1_g', 'ln1_b', 'ple_gate_b', 'ln2_g', 'ln2_b'):
        out[k] = stack(lambda g: g[k][0])
    out['s5_d'] = stack(lambda g: g['s5_d'][0]).reshape(2, 24, 16)
    out['rg_wa'] = stack(lambda g: _block_diag_extract(g['wa_bd'], 2))
    out['rg_wx'] = stack(lambda g: _block_diag_extract(g['wx_bd'], 2))
    return loss, gx, out


SHARD_AXIS = {'w_in': 2, 'w_out': 1, 'ple_w': 2, 'ple_gate_w': 1, 's5_w_glu': 1}


def kernel(x, p, w_in, conv_w, conv_b, rg_wa, rg_ba, rg_wx, rg_bx, rg_lambda, s5_a_re, s5_a_im, s5_b_re, s5_b_im, s5_c_re, s5_c_im, s5_d, s5_log_step, s5_w_glu, s5_b_glu, w_out, ln1_g, ln1_b, ple_w, ple_gate_w, ple_gate_b, ln2_g, ln2_b, loss_target, m_w_in, m_conv_w, m_conv_b, m_rg_wa, m_rg_ba, m_rg_wx, m_rg_bx, m_rg_lambda, m_s5_a_re, m_s5_a_im, m_s5_b_re, m_s5_b_im, m_s5_c_re, m_s5_c_im, m_s5_d, m_s5_log_step, m_s5_w_glu, m_s5_b_glu, m_w_out, m_ln1_g, m_ln1_b, m_ple_w, m_ple_gate_w, m_ple_gate_b, m_ln2_g, m_ln2_b, v_w_in, v_conv_w, v_conv_b, v_rg_wa, v_rg_ba, v_rg_wx, v_rg_bx, v_rg_lambda, v_s5_a_re, v_s5_a_im, v_s5_b_re, v_s5_b_im, v_s5_c_re, v_s5_c_im, v_s5_d, v_s5_log_step, v_s5_w_glu, v_s5_b_glu, v_w_out, v_ln1_g, v_ln1_b, v_ple_w, v_ple_gate_w, v_ple_gate_b, v_ln2_g, v_ln2_b):
    local = dict(locals())
    w = {k: local[k] for k in WEIGHTS}
    mom = {k: local['m_' + k] for k in WEIGHTS}
    var = {k: local['v_' + k] for k in WEIGHTS}

    big = list(SHARD_AXIS)
    wire = {k: w[k].astype(WIRE) for k in big}
    first = _all_gather([wire['w_in'][0][None], conv_w[None]], [0, 0], "gather_first_weights")
    late_axes = [SHARD_AXIS[k] - 1 for k in LATE]
    pushed = {}

    def push_weights(key, srcs, axes, after):
        pushed[key] = _push_start("gather", srcs, _place_own("gather", srcs, axes, "place_weights_" + key, after=after), axes,
                                  "push_weights_" + key)
        return pushed[key][4]

    def await_weights(key, axes, after):
        s = pushed[key]
        return _push_wait("gather", s[0], s[1], s[2], s[3], axes, after, "await_weights_" + key)

    token0 = push_weights("l0", [wire[k][0] for k in LATE], late_axes, first[0])
    push_weights("l1", [wire['w_in'][1][None]] + [wire[k][1] for k in LATE], [0] + late_axes, token0)

    def push_grads(key, g, names, axes):
        srcs = [g[k] for k in names]
        pushed[key] = _push_start("scatter", srcs, _place_own("scatter", srcs, axes, "place_grads_" + key), axes,
                                  "push_grads_" + key)
        return pushed[key][4]

    def await_grads(key, axes, after):
        s = pushed[key]
        return _push_wait("scatter", s[0], s[1], s[2], s[3], axes, after, "await_grads_" + key)

    class Overlap(_NoHooks):
        token = pushed["l1"][4]

        def late_weights(self, i, W, after):
            if i == 0:
                W[0].update(zip(LATE, await_weights("l0", late_axes, after)))
            return W

        def layer_start(self, i, W, after):
            lands = await_weights("l1", [0] + late_axes, after)
            W[1].update(zip(LATE, lands[1:]), w_in=lands[0])
            return W

        def post_done(self, i, g):
            return push_grads("late0", g, LATE, late_axes) if i == 0 else None

        def layer_done(self, i, g, dx):
            return push_grads("all1", g, ['w_in'] + list(LATE), [0] + late_axes) if i == 1 else None

    hooks = Overlap()
    full = dict(w)
    full['w_in'] = [first[0], None]
    full['conv_w'] = jnp.moveaxis(first[1], 0, 2).reshape(2, 4, RG_W)

    loss, grad_x, g = _full_grads(full, x[0], p[:, 0], loss_target[0], hooks)
    loss = lax.psum(loss, ("x", "y", "c"))

    conv_blocks = jnp.moveaxis(g['conv_w'].reshape(2, 4, N_DEV, RG_W // N_DEV), 2, 0).reshape(N_DEV, 8, RG_W // N_DEV)
    packed = _pack(g)
    w_in0, conv_parts, small_parts = _exchange([[g['w_in'][0]], [conv_blocks], [packed]], [0, 0, 0], "exchange_grads")
    recv1 = dict(zip(['w_in'] + list(LATE), await_grads("all1", [0] + late_axes, grad_x)))
    recv0 = dict(zip(LATE, await_grads("late0", late_axes, grad_x)), w_in=w_in0)
    outs = {}
    for k in big + ['conv_w']:
        shard = w[k].shape
        c = shard[-1]
        two = lambda a: a.reshape(-1, c)
        parts = [conv_parts.reshape(N_DEV, -1, c)] if k == 'conv_w' else [r[k].reshape(N_DEV, -1, c) for r in (recv0, recv1)]
        outs[k] = [o.reshape(shard) for o in _adamw(parts, two(w[k]), two(mom[k]), two(var[k]))]

    rows = packed.shape[0] // N_DEV
    mine = _sum_parts(small_parts.reshape(N_DEV, rows, LANE))
    summed = _all_gather([mine], [0], "gather_small_grads")[0]
    small = _adamw([summed[None]], _pack(w), _pack(mom), _pack(var))
    small = [_unpack(o, w) for o in small]
    for k in SMALL:
        outs[k] = [o[k] for o in small]

    res = [loss, grad_x[None]]
    for j in range(4):
        res += [outs[k][j] for k in WEIGHTS]
    return tuple(res)
```

```python
import functools
import math

import jax
import jax.numpy as jnp
from jax import lax
from jax.experimental import pallas as pl
from jax.experimental.pallas import tpu as pltpu

F32 = jnp.float32
MXU = jnp.bfloat16
WIRE = jnp.bfloat16

N_DEV = 8
D_MODEL = 1024
RG_W = 640
S5_W = 384
S5_P = 64
S5_N = 24 * S5_P
Z_W = 2 * RG_W + 2 * S5_W
C_RGG = RG_W
C_S5U = 2 * RG_W
C_S5G = 2 * RG_W + S5_W
LANE = 128
N_RG_T = RG_W // LANE
N_S5_T = S5_W // LANE
W_BLK = Z_W // N_DEV
ALPHA = (2.0 * 2) ** 0.25
LN_EPS = 1e-5
RG_C = 8.0
LR, B1, B2, EPS, WD, STEP = 0.001, 0.9, 0.999, 1e-08, 0.01, 10
BC1 = 1.0 - B1 ** STEP
BC2 = 1.0 - B2 ** STEP
RC = 256
TM = 256
VMEM_LIMIT = 56 * 1024 * 1024

MESH = pl.DeviceIdType.MESH
ANY = pl.BlockSpec(memory_space=pl.ANY)


def _params(n_grid_axes, vmem=VMEM_LIMIT):
    return pltpu.CompilerParams(dimension_semantics=("arbitrary",) * n_grid_axes, vmem_limit_bytes=vmem)


def _S(shape, dtype=F32):
    return jax.ShapeDtypeStruct(tuple(shape), dtype)


def _sigmoid(x):
    return 1.0 / (1.0 + jnp.exp(-x))


def _silu_and_grad(x):
    s = _sigmoid(x)
    return x * s, s * (1.0 + x * (1.0 - s))


_GELU_C = math.sqrt(2.0 / math.pi)


def _gelu(x):
    return 0.5 * x * (1.0 + jnp.tanh(_GELU_C * (x + 0.044715 * (x * x * x))))


def _gelu_grad(x):
    th = jnp.tanh(_GELU_C * (x + 0.044715 * (x * x * x)))
    return 0.5 * (1.0 + th) + 0.5 * x * (1.0 - th * th) * (_GELU_C * (1.0 + 3.0 * 0.044715 * (x * x)))


def _mm(a, b):
    return jnp.dot(a.astype(MXU), b.astype(MXU), preferred_element_type=F32)


def _mm_nt(a, b):
    return lax.dot_general(a.astype(MXU), b.astype(MXU), (((1,), (1,)), ((), ())), preferred_element_type=F32)


def _mm_tn(a, b):
    return lax.dot_general(a.astype(MXU), b.astype(MXU), (((0,), (0,)), ((), ())), preferred_element_type=F32)


def _ln_fwd(t, g, b):
    mu = jnp.mean(t, axis=-1, keepdims=True)
    tc = t - mu
    var = jnp.mean(tc * tc, axis=-1, keepdims=True)
    rstd = lax.rsqrt(var + LN_EPS)
    xhat = tc * rstd
    return xhat * g + b, xhat, rstd


def _ln_bwd(dy, xhat, rstd, g):
    dxh = dy * g
    m1 = jnp.mean(dxh, axis=-1, keepdims=True)
    m2 = jnp.mean(dxh * xhat, axis=-1, keepdims=True)
    return rstd * (dxh - m1 - xhat * m2)


def _colsum(a):
    return jnp.sum(a, axis=0, keepdims=True)


def _up(x, d, rows, fill):
    n = x.shape[0]
    return jnp.where(rows < n - d, pltpu.roll(x, n - d, 0), fill)


SUB = 8
TILE_STEPS = (1, 2, 4)


def _r8(width):
    return lax.broadcasted_iota(jnp.int32, (SUB, width), 0)


def _scan_real(a, u, carry, reverse=False):
    r8 = _r8(a.shape[1])
    n = a.shape[0] // SUB
    outs = [None] * n
    for k in (reversed(range(n)) if reverse else range(n)):
        A, U = a[SUB * k:SUB * k + SUB], u[SUB * k:SUB * k + SUB]
        for d in TILE_STEPS:
            m = (r8 < SUB - d) if reverse else (r8 >= d)
            sh = SUB - d if reverse else d
            U = A * jnp.where(m, pltpu.roll(U, sh, 0), 0.0) + U
            A = A * jnp.where(m, pltpu.roll(A, sh, 0), 1.0)
        h = A * carry + U
        outs[k] = h
        carry = h[0:1] if reverse else h[SUB - 1:SUB]
    return jnp.concatenate(outs, axis=0), carry


def _tile_powers(lr, li, reverse=False):
    width = lr.shape[1]
    r8 = _r8(width)
    steps = []
    pr, pi = lr, li
    er, ei = jnp.broadcast_to(lr, (SUB, width)), jnp.broadcast_to(li, (SUB, width))
    for d in TILE_STEPS:
        m = (r8 < SUB - d) if reverse else (r8 >= d)
        sh = SUB - d if reverse else d
        steps.append((sh, jnp.where(m, pr, 0.0), jnp.where(m, pi, 0.0)))
        er, ei = _cmul(er, ei, jnp.where(m, pltpu.roll(er, sh, 0), 1.0), jnp.where(m, pltpu.roll(ei, sh, 0), 0.0))
        pr, pi = _cmul(pr, pi, pr, pi)
    return steps, (er, ei)


def _scan_lti(xr, xi, carry, steps, e, reverse=False):
    er, ei = e
    kr, ki = carry
    n = xr.shape[0] // SUB
    outr, outi = [None] * n, [None] * n
    for k in (reversed(range(n)) if reverse else range(n)):
        sr, si = xr[SUB * k:SUB * k + SUB], xi[SUB * k:SUB * k + SUB]
        for sh, pr, pi in steps:
            shr, shi = pltpu.roll(sr, sh, 0), pltpu.roll(si, sh, 0)
            sr, si = sr + (pr * shr - pi * shi), si + (pr * shi + pi * shr)
        sr = sr + (er * kr - ei * ki)
        si = si + (er * ki + ei * kr)
        outr[k], outi[k] = sr, si
        kr, ki = (sr[0:1], si[0:1]) if reverse else (sr[SUB - 1:SUB], si[SUB - 1:SUB])
    return jnp.concatenate(outr, axis=0), jnp.concatenate(outi, axis=0), (kr, ki)


def _halo(ref, c, r0):
    rp = pl.multiple_of(jnp.maximum(r0 - 8, 0), 8)
    return jnp.where(c > 0, ref[pl.ds(rp, 8), :], 0.0)


def _conv_taps(xe):
    return [pltpu.roll(xe, 3, 0)[8:, :], pltpu.roll(xe, 2, 0)[8:, :], pltpu.roll(xe, 1, 0)[8:, :], xe[8:, :]]


def _rg_gates(h, wa, wx, ba, bx, sp):
    r = _sigmoid(_mm(h, wa) + ba)
    i = _sigmoid(_mm(h, wx) + bx)
    log_a = (-RG_C) * r * sp
    a = jnp.exp(log_a)
    mult = jnp.sqrt(-jnp.tanh(log_a) * (a * a + 1.0))
    return r, i, a, mult


def _softplus(y):
    return jnp.maximum(y, 0.0) + jnp.log1p(jnp.exp(-jnp.abs(y)))


def _after(token):
    return ([], []) if token is None else ([token], [ANY])


def _inproj_fwd(x, w_in, token=None):
    L = x.shape[0]

    def body(x_ref, w_ref, *rest):
        xb = x_ref[...].astype(MXU)
        for j in range(N_DEV):
            rest[-1][:, j * W_BLK:(j + 1) * W_BLK] = jnp.dot(xb, w_ref[j].astype(MXU), preferred_element_type=F32)

    extra, extra_specs = _after(token)
    return pl.pallas_call(
        body, name="inproj_fwd", grid=(L // TM,),
        in_specs=[pl.BlockSpec((TM, D_MODEL), lambda i: (i, 0)),
                  pl.BlockSpec((N_DEV, D_MODEL, W_BLK), lambda i: (0, 0, 0))] + extra_specs,
        out_specs=pl.BlockSpec((TM, Z_W), lambda i: (i, 0)),
        out_shape=_S((L, Z_W)), compiler_params=_params(1))(x, w_in, *extra)


def _inproj_bwd(dt1, x, dzx, dzg, dzu, w_in):
    L = x.shape[0]

    def body(dt1_ref, x_ref, dzx_ref, dzg_ref, dzu_ref, w_ref, dx_ref, dw_ref, acc_ref):
        @pl.when(pl.program_id(0) == 0)
        def _():
            acc_ref[...] = jnp.zeros_like(acc_ref)
        dzg = dzg_ref[...]
        dz = jnp.concatenate([dzx_ref[...], dzg[:, :RG_W], dzu_ref[...], dzg[:, RG_W:]], axis=1).astype(MXU)
        xb = x_ref[...].astype(MXU)
        dx = ALPHA * dt1_ref[...]
        for j in range(N_DEV):
            dzj = dz[:, j * W_BLK:(j + 1) * W_BLK]
            dx = dx + _mm_nt(dzj, w_ref[j])
            acc_ref[j] += _mm_tn(xb, dzj)
        dx_ref[...] = dx

        @pl.when(pl.program_id(0) == L // TM - 1)
        def _():
            dw_ref[...] = acc_ref[...].astype(WIRE)

    row = lambda w: pl.BlockSpec((TM, w), lambda i: (i, 0))
    wspec = pl.BlockSpec((N_DEV, D_MODEL, W_BLK), lambda i: (0, 0, 0))
    return pl.pallas_call(
        body, name="inproj_bwd", grid=(L // TM,),
        in_specs=[row(D_MODEL), row(D_MODEL), row(RG_W), row(D_MODEL), row(S5_W), wspec],
        out_specs=[row(D_MODEL), wspec],
        out_shape=[_S((L, D_MODEL)), _S((N_DEV, D_MODEL, W_BLK), WIRE)],
        scratch_shapes=[pltpu.VMEM((N_DEV, D_MODEL, W_BLK), F32)],
        compiler_params=_params(1))(dt1, x, dzx, dzg, dzu, w_in)


def _rg_specs(layer):
    tile = lambda rows: pl.BlockSpec((rows, LANE), lambda c: (0, c))
    ptile = lambda rows: pl.BlockSpec((None, rows, LANE), lambda c: (layer, 0, c))
    pbd = pl.BlockSpec((None, LANE, LANE), lambda c: (layer * N_RG_T + c, 0, 0))
    return tile, ptile, pbd, pl.BlockSpec((None, LANE, LANE), lambda c: (c, 0, 0))


def _rg_fwd(z, cw, cb, wa_bd, wx_bd, ba, bx, lam, layer):
    L = z.shape[0]

    def body(x_ref, cw_ref, cb_ref, wa_ref, wx_ref, ba_ref, bx_ref, lam_ref, hs_ref):
        w, b = cw_ref[...], cb_ref[...]
        wa, wx, ba_, bx_ = wa_ref[...].astype(MXU), wx_ref[...].astype(MXU), ba_ref[...], bx_ref[...]
        sp = _softplus(-lam_ref[...])

        def step(c, carry):
            r0 = pl.multiple_of(c * RC, RC)
            xe = jnp.concatenate([_halo(x_ref, c, r0), x_ref[pl.ds(r0, RC), :]], axis=0)
            t = _conv_taps(xe)
            h = t[0] * w[0:1] + t[1] * w[1:2] + t[2] * w[2:3] + t[3] * w[3:4] + b
            _, i, a, mult = _rg_gates(h, wa, wx, ba_, bx_, sp)
            hs, carry = _scan_real(a, mult * (i * h), carry)
            hs_ref[pl.ds(r0, RC), :] = hs
            return carry

        lax.fori_loop(0, L // RC, step, jnp.zeros((1, LANE), F32))

    tile, ptile, pbd, _ = _rg_specs(layer)
    return pl.pallas_call(
        body, name="rg_fwd", grid=(N_RG_T,),
        in_specs=[tile(L), ptile(4), ptile(1), pbd, pbd, ptile(1), ptile(1), ptile(1)],
        out_specs=tile(L), out_shape=_S((L, RG_W)), compiler_params=_params(1))(z, cw, cb, wa_bd, wx_bd, ba, bx, lam)


def _rg_bwd(dhs, z, hs, cw, cb, wa_bd, wx_bd, ba, bx, lam, layer):
    L = z.shape[0]

    def body(g_ref, x_ref, hs_ref, cw_ref, cb_ref, wa_ref, wx_ref, ba_ref, bx_ref, lam_ref,
             dx_ref, dcw_ref, dcb_ref, dwa_ref, dwx_ref, dba_ref, dbx_ref, dlam_ref):
        w, b = cw_ref[...], cb_ref[...]
        wa, wx, ba_, bx_ = wa_ref[...].astype(MXU), wx_ref[...].astype(MXU), ba_ref[...], bx_ref[...]
        lam = lam_ref[...]
        sp = _softplus(-lam)
        rows = lax.broadcasted_iota(jnp.int32, (RC, LANE), 0)
        for ref in (dcw_ref, dcb_ref, dwa_ref, dwx_ref, dba_ref, dbx_ref, dlam_ref):
            ref[...] = jnp.zeros_like(ref)
        nch = L // RC

        def step(k, carry):
            cin, nxt = carry
            c = nch - 1 - k
            r0 = pl.multiple_of(c * RC, RC)
            xe = jnp.concatenate([_halo(x_ref, c, r0), x_ref[pl.ds(r0, RC), :]], axis=0)
            t = _conv_taps(xe)
            h = t[0] * w[0:1] + t[1] * w[1:2] + t[2] * w[2:3] + t[3] * w[3:4] + b
            r, i, a, mult = _rg_gates(h, wa, wx, ba_, bx_, sp)
            hs_e = jnp.concatenate([_halo(hs_ref, c, r0), hs_ref[pl.ds(r0, RC), :]], axis=0)
            hs_prev = pltpu.roll(hs_e, 1, 0)[8:, :]
            g = g_ref[pl.ds(r0, RC), :]
            cc, cin_new = _scan_real(a, a * g, cin, reverse=True)
            dh = g + _up(cc, 1, rows, cin)
            ih = i * h
            dlog_a = dh * hs_prev * a - (dh * ih) * (a * a) / mult
            di = dh * mult * h
            dhin = dh * mult * i
            dr = dlog_a * ((-RG_C) * sp)
            dlam_ref[...] += _colsum(dlog_a * r)
            dra = dr * r * (1.0 - r)
            dia = di * i * (1.0 - i)
            dwa_ref[...] += _mm_tn(h, dra)
            dwx_ref[...] += _mm_tn(h, dia)
            dba_ref[...] += _colsum(dra)
            dbx_ref[...] += _colsum(dia)
            dhin = dhin + _mm_nt(dra, wa) + _mm_nt(dia, wx)
            de = jnp.concatenate([dhin, nxt], axis=0)
            n = RC + 8
            dx = (dhin * w[3:4] + pltpu.roll(de, n - 1, 0)[:RC, :] * w[2:3]
                  + pltpu.roll(de, n - 2, 0)[:RC, :] * w[1:2] + pltpu.roll(de, n - 3, 0)[:RC, :] * w[0:1])
            dx_ref[pl.ds(r0, RC), :] = dx
            for kk in range(4):
                dcw_ref[kk:kk + 1, :] += _colsum(dhin * t[kk])
            dcb_ref[...] += _colsum(dhin)
            return cin_new, dhin[0:8, :]

        lax.fori_loop(0, nch, step, (jnp.zeros((1, LANE), F32), jnp.zeros((8, LANE), F32)))
        dlam_ref[...] = dlam_ref[...] * (RG_C * _sigmoid(-lam))

    tile, ptile, pbd, bd = _rg_specs(layer)
    return pl.pallas_call(
        body, name="rg_bwd", grid=(N_RG_T,),
        in_specs=[tile(L), tile(L), tile(L), ptile(4), ptile(1), pbd, pbd, ptile(1), ptile(1), ptile(1)],
        out_specs=[tile(L), tile(4), tile(1), bd, bd, tile(1), tile(1), tile(1)],
        out_shape=[_S((L, RG_W)), _S((4, RG_W)), _S((1, RG_W)), _S((N_RG_T, LANE, LANE)), _S((N_RG_T, LANE, LANE)),
                   _S((1, RG_W)), _S((1, RG_W)), _S((1, RG_W))],
        compiler_params=_params(1))(dhs, z, hs, cw, cb, wa_bd, wx_bd, ba, bx, lam)


def _cmul(ar, ai, br, bi):
    return ar * br - ai * bi, ar * bi + ai * br


S5_TW = S5_N // N_S5_T


def _s5_specs(L):
    in_tile = pl.BlockSpec((L, LANE), lambda t: (0, t))
    st = pl.BlockSpec((L, S5_TW), lambda t: (0, t))
    bb = pl.BlockSpec((None, LANE, S5_TW), lambda t: (t, 0, 0))
    cc = pl.BlockSpec((None, S5_TW, LANE), lambda t: (t, 0, 0))
    lb = pl.BlockSpec((1, S5_TW), lambda t: (0, t))
    dv = pl.BlockSpec((1, LANE), lambda t: (0, t))
    return in_tile, st, bb, cc, lb, dv


def _layer_row_tile(layer):
    return pl.BlockSpec((None, 1, LANE), lambda t: (layer, 0, t))


def _s5_fwd(z, bb_re, bb_im, lb_re, lb_im, c_re, c_im, dvec, layer):
    L = z.shape[0]

    def body(u_ref, bbr_ref, bbi_ref, lr_ref, li_ref, cr_ref, ci_ref, d_ref, y_ref, sr_ref, si_ref):
        bbr, bbi = bbr_ref[...].astype(MXU), bbi_ref[...].astype(MXU)
        cr, ci = cr_ref[...].astype(MXU), ci_ref[...].astype(MXU)
        dv = d_ref[...]
        steps, e = _tile_powers(lr_ref[...], li_ref[...])

        def step(c, carry):
            r0 = pl.multiple_of(c * RC, RC)
            u = u_ref[pl.ds(r0, RC), :]
            ub = u.astype(MXU)
            sr = jnp.dot(ub, bbr, preferred_element_type=F32)
            si = jnp.dot(ub, bbi, preferred_element_type=F32)
            sr, si, carry = _scan_lti(sr, si, carry, steps, e)
            sr_ref[pl.ds(r0, RC), :] = sr
            si_ref[pl.ds(r0, RC), :] = si
            y_ref[pl.ds(r0, RC), :] = dv * u + (_mm(sr, cr) - _mm(si, ci))
            return carry

        zero = jnp.zeros((1, S5_TW), F32)
        lax.fori_loop(0, L // RC, step, (zero, zero))

    in_tile, st, bb, cc, lb, dv = _s5_specs(L)
    u_tile = pl.BlockSpec((L, LANE), lambda t: (0, C_S5U // LANE + t))
    return pl.pallas_call(
        body, name="s5_fwd", grid=(N_S5_T,),
        in_specs=[u_tile, bb, bb, lb, lb, cc, cc, _layer_row_tile(layer)],
        out_specs=[in_tile, st, st],
        out_shape=[_S((L, S5_W)), _S((L, S5_N)), _S((L, S5_N))],
        compiler_params=_params(1))(z, bb_re, bb_im, lb_re, lb_im, c_re, c_im, dvec)


def _s5_bwd(dy0, z, s_re, s_im, bb_re, bb_im, lb_re, lb_im, c_re, c_im, dvec, layer, token=None):
    L = z.shape[0]
    extra, extra_specs = _after(token)

    def body(dy_ref, u_ref, sr_ref, si_ref, bbr_ref, bbi_ref, lr_ref, li_ref, cr_ref, ci_ref, d_ref, *rest):
        du_ref, dbbr_ref, dbbi_ref, dlr_ref, dli_ref, dcr_ref, dci_ref, dd_ref = rest[len(extra):]
        bbr, bbi = bbr_ref[...].astype(MXU), bbi_ref[...].astype(MXU)
        cr, ci = cr_ref[...].astype(MXU), ci_ref[...].astype(MXU)
        lr, li = lr_ref[...], -li_ref[...]
        dv = d_ref[...]
        steps, e = _tile_powers(lr, li, reverse=True)
        for ref in (dbbr_ref, dbbi_ref, dlr_ref, dli_ref, dcr_ref, dci_ref, dd_ref):
            ref[...] = jnp.zeros_like(ref)
        nch = L // RC

        def step(k, carry):
            c = nch - 1 - k
            r0 = pl.multiple_of(c * RC, RC)
            dy = dy_ref[pl.ds(r0, RC), :]
            u = u_ref[pl.ds(r0, RC), :]
            dyb, ub = dy.astype(MXU), u.astype(MXU)
            sr, si = sr_ref[pl.ds(r0, RC), :], si_ref[pl.ds(r0, RC), :]
            dcr_ref[...] += _mm_tn(sr, dyb)
            dci_ref[...] -= _mm_tn(si, dyb)
            gr = _mm_nt(dyb, cr)
            gi = -_mm_nt(dyb, ci)
            gr, gi, carry = _scan_lti(gr, gi, carry, steps, e, reverse=True)
            pr_ = pltpu.roll(jnp.concatenate([_halo(sr_ref, c, r0), sr], axis=0), 1, 0)[8:, :]
            pi_ = pltpu.roll(jnp.concatenate([_halo(si_ref, c, r0), si], axis=0), 1, 0)[8:, :]
            dlr_ref[...] += _colsum(pr_ * gr + pi_ * gi)
            dli_ref[...] += _colsum(pr_ * gi - pi_ * gr)
            grb, gib = gr.astype(MXU), gi.astype(MXU)
            dbbr_ref[...] += _mm_tn(ub, grb)
            dbbi_ref[...] += _mm_tn(ub, gib)
            du_ref[pl.ds(r0, RC), :] = dv * dy + (_mm_nt(grb, bbr) + _mm_nt(gib, bbi))
            dd_ref[...] += _colsum(dy * u)
            return carry

        zero = jnp.zeros((1, S5_TW), F32)
        lax.fori_loop(0, nch, step, (zero, zero))

    in_tile, st, bb, cc, lb, dv = _s5_specs(L)
    u_tile = pl.BlockSpec((L, LANE), lambda t: (0, C_S5U // LANE + t))
    return pl.pallas_call(
        body, name="s5_bwd", grid=(N_S5_T,),
        in_specs=[in_tile, u_tile, st, st, bb, bb, lb, lb, cc, cc, _layer_row_tile(layer)] + extra_specs,
        out_specs=[in_tile, bb, bb, lb, lb, cc, cc, dv],
        out_shape=[_S((L, S5_W)), _S((N_S5_T, LANE, S5_TW)), _S((N_S5_T, LANE, S5_TW)), _S((1, S5_N)), _S((1, S5_N)),
                   _S((N_S5_T, S5_TW, LANE)), _S((N_S5_T, S5_TW, LANE)), _S((1, S5_W))],
        compiler_params=_params(1))(dy0, z, s_re, s_im, bb_re, bb_im, lb_re, lb_im, c_re, c_im, dvec, *extra)


def _disc(ar, ai, ls):
    dt = jnp.exp(ls)
    mag = jnp.exp(ar * dt)
    lr = mag * jnp.cos(ai * dt)
    li = mag * jnp.sin(ai * dt)
    den = ar * ar + ai * ai
    cr = ((lr - 1.0) * ar + li * ai) / den
    ci = (li * ar - (lr - 1.0) * ai) / den
    return lr, li, cr, ci


def _s5_disc_fwd(ar, ai, ls):
    def body(ar_ref, ai_ref, ls_ref, lr_ref, li_ref, cr_ref, ci_ref):
        lr, li, cr, ci = _disc(ar_ref[...], ai_ref[...], ls_ref[...])
        lr_ref[...], li_ref[...], cr_ref[...], ci_ref[...] = lr, li, cr, ci

    sh = _S(ar.shape)
    return pl.pallas_call(body, name="s5_disc_fwd", out_shape=[sh, sh, sh, sh])(ar, ai, ls)


def _s5_disc_bwd(ar, ai, ls, dlr, dli, dcr, dci):
    def body(ar_ref, ai_ref, ls_ref, dlr_ref, dli_ref, dcr_ref, dci_ref, dar_ref, dai_ref, dls_ref):
        _, vjp = jax.vjp(_disc, ar_ref[...], ai_ref[...], jnp.broadcast_to(ls_ref[...], ar_ref.shape))
        dar, dai, dls = vjp((dlr_ref[...], dli_ref[...], dcr_ref[...], dci_ref[...]))
        dar_ref[...], dai_ref[...] = dar, dai
        dls_ref[...] = jnp.sum(dls, axis=1, keepdims=True)

    return pl.pallas_call(body, name="s5_disc_bwd", out_shape=[_S(ar.shape), _S(ar.shape), _S(ls.shape)])(
        ar, ai, ls, dlr, dli, dcr, dci)


def _s5_bscale_fwd(cr, ci, br, bi):
    def body(cr_ref, ci_ref, br_ref, bi_ref, or_ref, oi_ref):
        or_ref[...], oi_ref[...] = _cmul(cr_ref[...], ci_ref[...], br_ref[...], bi_ref[...])

    return pl.pallas_call(body, name="s5_bscale_fwd", out_shape=[_S(br.shape), _S(br.shape)])(cr, ci, br, bi)


def _s5_bscale_bwd(cr, ci, br, bi, gr, gi):
    def body(cr_ref, ci_ref, br_ref, bi_ref, gr_ref, gi_ref, dbr_ref, dbi_ref, dcr_ref, dci_ref):
        cr_, ci_, br_, bi_, gr_, gi_ = (r[...] for r in (cr_ref, ci_ref, br_ref, bi_ref, gr_ref, gi_ref))
        dbr_ref[...] = cr_ * gr_ + ci_ * gi_
        dbi_ref[...] = cr_ * gi_ - ci_ * gr_
        dcr_ref[...] = jnp.sum(gr_ * br_ + gi_ * bi_, axis=1, keepdims=True)
        dci_ref[...] = jnp.sum(gi_ * br_ - gr_ * bi_, axis=1, keepdims=True)

    return pl.pallas_call(body, name="s5_bscale_bwd",
                          out_shape=[_S(br.shape), _S(br.shape), _S(cr.shape), _S(cr.shape)])(cr, ci, br, bi, gr, gi)


def _row(w):
    return pl.BlockSpec((TM, w), lambda i: (i, 0))


def _full(shape):
    return pl.BlockSpec(tuple(shape), lambda i: (0,) * len(shape))


def _lrow(layer, width):
    return pl.BlockSpec((None, 1, width), lambda i: (layer, 0, 0))


def _post_fwd(x, hs, z, y0, p, w_glu, b_glu, w_out, g1, b1, ple_w, w_pg, b_pg, g2, b2, layer):
    L = x.shape[0]

    def body(x_ref, hs_ref, z_ref, y0_ref, p_ref, wg_ref, bg_ref, wo_ref, g1_ref, b1_ref, pw_ref, wpg_ref, bpg_ref,
             g2_ref, b2_ref, x2_ref, t1_ref, t2_ref, m_ref):
        rg_gate = z_ref[:, C_RGG:C_RGG + RG_W]
        s5_gate = z_ref[:, C_S5G:C_S5G + S5_W]
        rg_y = hs_ref[...] * _silu_and_grad(rg_gate)[0]
        y1 = _gelu(y0_ref[...])
        gl = _sigmoid(_mm(y1, wg_ref[...]) + bg_ref[...])
        s5_y = (y1 * gl) * _silu_and_grad(s5_gate)[0]
        m_ref[:, :RG_W] = rg_y
        m_ref[:, RG_W:] = s5_y
        mix = _mm(m_ref[...], wo_ref[...])
        t1 = ALPHA * x_ref[...] + mix
        x1, _, _ = _ln_fwd(t1, g1_ref[...], b1_ref[...])
        e = _mm(p_ref[...], pw_ref[...]) * _sigmoid(_mm(x1, wpg_ref[...]) + bpg_ref[...])
        t2 = ALPHA * x1 + e
        x2, _, _ = _ln_fwd(t2, g2_ref[...], b2_ref[...])
        t1_ref[...], t2_ref[...], x2_ref[...] = t1, t2, x2

    vec = _lrow(layer, D_MODEL)
    return pl.pallas_call(
        body, name="post_fwd", grid=(L // TM,),
        in_specs=[_row(D_MODEL), _row(RG_W), _row(Z_W), _row(S5_W), _row(256), _full((S5_W, S5_W)), _lrow(layer, S5_W),
                  _full((D_MODEL, D_MODEL)), vec, vec, _full((256, D_MODEL)), _full((D_MODEL, D_MODEL)), vec, vec, vec],
        out_specs=[_row(D_MODEL)] * 4, out_shape=[_S((L, D_MODEL))] * 4,
        compiler_params=_params(1))(x, hs, z, y0, p, w_glu, b_glu, w_out, g1, b1, ple_w, w_pg, b_pg, g2, b2)


def _post_bwd_a(dx2_or_target, is_top, t2, t1, p, ple_w, w_pg, b_pg, g1, b1, g2, b2, layer, token=None):
    L = t1.shape[0]
    extra, extra_specs = _after(token)

    def body(d_ref, t2_ref, t1_ref, p_ref, pw_ref, wpg_ref, bpg_ref, g1_ref, b1_ref, g2_ref, b2_ref, *rest):
        (dt1_ref, dpw_out, dwpg_out, dbpg_ref, dg1_ref, db1_ref, dg2_ref, db2_ref, loss_ref, dpw_ref,
         dwpg_ref) = rest[len(extra):]
        @pl.when(pl.program_id(0) == 0)
        def _():
            for ref in (dpw_ref, dwpg_ref, dbpg_ref, dg1_ref, db1_ref, dg2_ref, db2_ref, loss_ref):
                ref[...] = jnp.zeros_like(ref)

        g1, g2 = g1_ref[...], g2_ref[...]
        x1, xh1, rstd1 = _ln_fwd(t1_ref[...], g1, b1_ref[...])
        x2, xh2, rstd2 = _ln_fwd(t2_ref[...], g2, b2_ref[...])
        if is_top:
            err = x2 - d_ref[...]
            loss_ref[...] += _colsum(err * err)
            dx2 = err * (1.0 / D_MODEL)
        else:
            dx2 = d_ref[...]
        p = p_ref[...]
        q = _mm(p, pw_ref[...])
        gt = _sigmoid(_mm(x1, wpg_ref[...]) + bpg_ref[...])
        dg2_ref[...] += _colsum(dx2 * xh2)
        db2_ref[...] += _colsum(dx2)
        dt2 = _ln_bwd(dx2, xh2, rstd2, g2)
        dq = dt2 * gt
        dgpre = (dt2 * q) * gt * (1.0 - gt)
        dpw_ref[...] += _mm_tn(p, dq)
        dwpg_ref[...] += _mm_tn(x1, dgpre)
        dbpg_ref[...] += _colsum(dgpre)
        dx1 = ALPHA * dt2 + _mm_nt(dgpre, wpg_ref[...])
        dg1_ref[...] += _colsum(dx1 * xh1)
        db1_ref[...] += _colsum(dx1)
        dt1_ref[...] = _ln_bwd(dx1, xh1, rstd1, g1)

        @pl.when(pl.program_id(0) == L // TM - 1)
        def _():
            dpw_out[...] = dpw_ref[...].astype(WIRE)
            dwpg_out[...] = dwpg_ref[...].astype(WIRE)

    vec, lvec = _full((1, D_MODEL)), _lrow(layer, D_MODEL)
    return pl.pallas_call(
        body, name="post_bwd_a_top" if is_top else "post_bwd_a", grid=(L // TM,),
        in_specs=[_row(D_MODEL), _row(D_MODEL), _row(D_MODEL), _row(256), _full((256, D_MODEL)),
                  _full((D_MODEL, D_MODEL)), lvec, lvec, lvec, lvec, lvec] + extra_specs,
        out_specs=[_row(D_MODEL), _full((256, D_MODEL)), _full((D_MODEL, D_MODEL)), vec, vec, vec, vec, vec, vec],
        out_shape=[_S((L, D_MODEL)), _S((256, D_MODEL), WIRE), _S((D_MODEL, D_MODEL), WIRE)] + [_S((1, D_MODEL))] * 6,
        scratch_shapes=[pltpu.VMEM((256, D_MODEL), F32), pltpu.VMEM((D_MODEL, D_MODEL), F32)],
        compiler_params=_params(1))(dx2_or_target, t2, t1, p, ple_w, w_pg, b_pg, g1, b1, g2, b2, *extra)


def _post_bwd_b(dt1, m, z, hs, y0, w_out, w_glu, b_glu, layer):
    L = dt1.shape[0]

    def body(dt1_ref, m_ref, z_ref, hs_ref, y0_ref, wo_ref, wg_ref, bg_ref,
             dhs_ref, dy0_ref, dzg_ref, dwo_out, dwg_out, dbg_ref, dwo_ref, dwg_ref):
        @pl.when(pl.program_id(0) == 0)
        def _():
            for ref in (dwo_ref, dwg_ref, dbg_ref):
                ref[...] = jnp.zeros_like(ref)

        dt1b = dt1_ref[...].astype(MXU)
        dm = _mm_nt(dt1b, wo_ref[...])
        dwo_ref[...] += _mm_tn(m_ref[...], dt1b)
        d_rgy, d_s5y = dm[:, :RG_W], dm[:, RG_W:]
        rg_gate = z_ref[:, C_RGG:C_RGG + RG_W]
        s5_gate = z_ref[:, C_S5G:C_S5G + S5_W]
        sl, dsl = _silu_and_grad(rg_gate)
        dhs_ref[...] = d_rgy * sl
        dzg_ref[:, :RG_W] = d_rgy * hs_ref[...] * dsl
        y0 = y0_ref[...]
        y1 = _gelu(y0)
        gl = _sigmoid(_mm(y1, wg_ref[...]) + bg_ref[...])
        sl, dsl = _silu_and_grad(s5_gate)
        dy2 = d_s5y * sl
        dzg_ref[:, RG_W:] = d_s5y * (y1 * gl) * dsl
        dglpre = (dy2 * y1) * gl * (1.0 - gl)
        dwg_ref[...] += _mm_tn(y1, dglpre)
        dbg_ref[...] += _colsum(dglpre)
        dy1 = dy2 * gl + _mm_nt(dglpre, wg_ref[...])
        dy0_ref[...] = dy1 * _gelu_grad(y0)

        @pl.when(pl.program_id(0) == L // TM - 1)
        def _():
            dwo_out[...] = dwo_ref[...].astype(WIRE)
            dwg_out[...] = dwg_ref[...].astype(WIRE)

    return pl.pallas_call(
        body, name="post_bwd_b", grid=(L // TM,),
        in_specs=[_row(D_MODEL), _row(D_MODEL), _row(Z_W), _row(RG_W), _row(S5_W), _full((D_MODEL, D_MODEL)),
                  _full((S5_W, S5_W)), _lrow(layer, S5_W)],
        out_specs=[_row(RG_W), _row(S5_W), _row(D_MODEL), _full((D_MODEL, D_MODEL)), _full((S5_W, S5_W)), _full((1, S5_W))],
        out_shape=[_S((L, RG_W)), _S((L, S5_W)), _S((L, D_MODEL)), _S((D_MODEL, D_MODEL), WIRE), _S((S5_W, S5_W), WIRE),
                   _S((1, S5_W))],
        scratch_shapes=[pltpu.VMEM((D_MODEL, D_MODEL), F32), pltpu.VMEM((S5_W, S5_W), F32)],
        compiler_params=_params(1))(dt1, m, z, hs, y0, w_out, w_glu, b_glu)


def _adamw(parts, w, m, v):
    nl = len(parts)
    n, R, C = parts[0].shape
    tr = R
    for cand in (512, 256, 128, 64, 32, 16, 8):
        if R % cand == 0 and n * cand * C * 4 <= 4 * 1024 * 1024:
            tr = cand
            break
    nblk = R // tr

    def body(*refs):
        p_refs = refs[:nl]
        w_ref, m_ref, v_ref, g_ref, d_ref, nm_ref, nv_ref = refs[nl:]
        layer = pl.program_id(0)
        g = None
        for li, p_ref in enumerate(p_refs):
            s = p_ref[0].astype(F32)
            for k in range(1, n):
                s = s + p_ref[k].astype(F32)
            g = s if g is None else jnp.where(layer == li, s, g)
        nm = B1 * m_ref[...] + (1.0 - B1) * g
        nv = B2 * v_ref[...] + (1.0 - B2) * (g * g)
        d_ref[...] = (-LR) * ((nm / BC1) / (jnp.sqrt(nv / BC2) + EPS) + WD * w_ref[...])
        g_ref[...], nm_ref[...], nv_ref[...] = g, nm, nv

    def part_spec(li):
        return pl.BlockSpec((n, tr, C), lambda l, i: (0, jnp.where(l == li, i, jnp.where(l < li, 0, nblk - 1)), 0))

    blk = pl.BlockSpec((tr, C), lambda l, i: (l * nblk + i, 0))
    return pl.pallas_call(
        body, name="adamw", grid=(nl, nblk),
        in_specs=[part_spec(li) for li in range(nl)] + [blk, blk, blk],
        out_specs=[blk] * 4, out_shape=[_S((nl * R, C))] * 4, compiler_params=_params(2))(*parts, w, m, v)


def _adamw_natural(names, g, w, m, v, name):
    n = len(names)
    two = lambda a: a if a.ndim == 2 else a.reshape(-1, a.shape[-1])

    def body(*refs):
        for j in range(n):
            g_ref, w_ref, m_ref, v_ref, d_ref, nm_ref, nv_ref = (refs[k * n + j] for k in range(7))
            gj = g_ref[...]
            nm = B1 * m_ref[...] + (1.0 - B1) * gj
            nv = B2 * v_ref[...] + (1.0 - B2) * (gj * gj)
            d_ref[...] = (-LR) * ((nm / BC1) / (jnp.sqrt(nv / BC2) + EPS) + WD * w_ref[...])
            nm_ref[...], nv_ref[...] = nm, nv

    ins = [two(t[k]) for t in (g, w, m, v) for k in names]
    outs = pl.pallas_call(body, name=name, out_shape=[_S(two(w[k]).shape) for _ in range(3) for k in names],
                          compiler_params=pltpu.CompilerParams(vmem_limit_bytes=VMEM_LIMIT))(*ins)
    return [{k: outs[t * n + j].reshape(w[k].shape) for j, k in enumerate(names)} for t in range(3)]


def _me():
    return lax.axis_index("x"), lax.axis_index("y"), lax.axis_index("c")


def _lin(dev):
    return 4 * dev[0] + 2 * dev[1] + dev[2]


def _blk(ref, axis, size, idx):
    nd = len(ref.shape)
    start = idx * size
    if axis == nd - 1 and size % LANE == 0:
        start = pl.multiple_of(start, LANE)
    elif axis == nd - 2 and size % 16 == 0:
        start = pl.multiple_of(start, 16)
    ix = [slice(None)] * nd
    ix[axis] = pl.ds(start, size)
    return ref.at[tuple(ix)]


def _all_gather(shards, axes, name):
    n = len(shards)
    sizes = [s.shape[a] for s, a in zip(shards, axes)]
    out_shapes = [_S(s.shape[:a] + (N_DEV * s.shape[a],) + s.shape[a + 1:], s.dtype) for s, a in zip(shards, axes)]

    def body(*refs):
        ins, outs = refs[:n], refs[n:2 * n]
        send_sems, recv_sems, local_sems = refs[2 * n:]
        x, y, c = _me()
        me, sibling = (x, y, c), (x, y, 1 - c)
        chips = [(1 - x, y), (x, 1 - y), (1 - x, 1 - y)]

        def copy(a, k, block, to, from_input=False):
            dst = _blk(outs[a], axes[a], sizes[a], _lin(block))
            return pltpu.make_async_remote_copy(
                src_ref=ins[a] if from_input else dst, dst_ref=dst, send_sem=send_sems.at[a, k],
                recv_sem=recv_sems.at[a, k], device_id=to, device_id_type=MESH)

        mine = [pltpu.make_async_copy(ins[a], _blk(outs[a], axes[a], sizes[a], _lin(me)), local_sems.at[a]) for a in range(n)]
        for cp in mine:
            cp.start()
        first = []
        for a in range(n):
            first.append(copy(a, 0, me, sibling, True))
            first += [copy(a, 1 + j, me, (*chip, c), True) for j, chip in enumerate(chips)]
        for cp in first:
            cp.start()
        passed = []
        for j, chip in enumerate(chips):
            for a in range(n):
                copy(a, 1 + j, (*chip, c), me).wait_recv()
                cp = copy(a, 4 + j, (*chip, c), sibling)
                cp.start()
                passed.append(cp)
        for a in range(n):
            copy(a, 0, sibling, me).wait_recv()
            for j, chip in enumerate(chips):
                copy(a, 4 + j, (*chip, 1 - c), me).wait_recv()
        for cp in first + passed:
            cp.wait_send()
        for cp in mine:
            cp.wait()

    return pl.pallas_call(
        body, name=name, out_shape=out_shapes, in_specs=[ANY] * n, out_specs=[ANY] * n,
        scratch_shapes=[pltpu.SemaphoreType.DMA((n, 7)), pltpu.SemaphoreType.DMA((n, 7)), pltpu.SemaphoreType.DMA((n,))],
    )(*shards)


def _exchange(groups, axes, name):
    arrays = [a for g in groups for a in g]
    where = [(o, i) for o, g in enumerate(groups) for i in range(len(g))]
    ax = [axes[o] for o, _ in where]
    n = len(arrays)
    sizes = [s.shape[a] // N_DEV for s, a in zip(arrays, ax)]
    out_shapes = []
    for g, a in zip(groups, axes):
        s = g[0].shape
        out_shapes.append(_S((N_DEV, len(g)) + s[:a] + (s[a] // N_DEV,) + s[a + 1:], g[0].dtype))

    def body(*refs):
        ins, outs = refs[:n], refs[n:n + len(groups)]
        send_sems, recv_sems, local_sems = refs[n + len(groups):]
        x, y, c = _me()
        me = (x, y, c)
        flip = lambda v, f: 1 - v if f else v
        peers = [(flip(x, k & 4), flip(y, k & 2), flip(c, k & 1)) for k in range(1, N_DEV)]

        def land(a, sender):
            o, i = where[a]
            return outs[o].at[_lin(sender), i]

        def copy(a, k, to):
            return pltpu.make_async_remote_copy(
                src_ref=_blk(ins[a], ax[a], sizes[a], _lin(to)), dst_ref=land(a, me),
                send_sem=send_sems.at[a, k], recv_sem=recv_sems.at[a, k], device_id=to, device_id_type=MESH)

        mine = [pltpu.make_async_copy(_blk(ins[a], ax[a], sizes[a], _lin(me)), land(a, me), local_sems.at[a]) for a in range(n)]
        for cp in mine:
            cp.start()
        sends = [copy(a, k, peer) for a in range(n) for k, peer in enumerate(peers)]
        for cp in sends:
            cp.start()
        for a in range(n):
            for k, peer in enumerate(peers):
                pltpu.make_async_remote_copy(
                    src_ref=land(a, peer), dst_ref=land(a, peer), send_sem=send_sems.at[a, k],
                    recv_sem=recv_sems.at[a, k], device_id=peer, device_id_type=MESH).wait_recv()
        for cp in sends:
            cp.wait_send()
        for cp in mine:
            cp.wait()

    return pl.pallas_call(
        body, name=name, out_shape=out_shapes, in_specs=[ANY] * n, out_specs=[ANY] * len(groups),
        scratch_shapes=[pltpu.SemaphoreType.DMA((n, 7)), pltpu.SemaphoreType.DMA((n, 7)), pltpu.SemaphoreType.DMA((n,))],
    )(*arrays)


HBM_SPEC = pl.BlockSpec(memory_space=pltpu.HBM)
SEM_SPEC = pl.BlockSpec(memory_space=pltpu.SEMAPHORE)
EFFECT = pltpu.SideEffectType.DATAFLOW_SIDE_EFFECTING


def _peers(x, y, c):
    flip = lambda v, f: 1 - v if f else v
    return [(flip(x, k & 4), flip(y, k & 2), flip(c, k & 1)) for k in range(1, N_DEV)]


def _land_shape(mode, s, axis):
    if mode == "gather":
        return s.shape[:axis] + (N_DEV * s.shape[axis],) + s.shape[axis + 1:]
    return (N_DEV,) + s.shape[:axis] + (s.shape[axis] // N_DEV,) + s.shape[axis + 1:]


def _src_view(mode, ref, axis, peer):
    return ref if mode == "gather" else _blk(ref, axis, ref.shape[axis] // N_DEV, peer)


def _dst_view(mode, land, axis, sender):
    return _blk(land, axis, land.shape[axis] // N_DEV, sender) if mode == "gather" else land.at[sender]


def _seven_blocks(mode, land, axis):
    if mode == "gather":
        ix = [slice(None)] * len(land.shape)
        ix[axis] = pl.ds(0, (N_DEV - 1) * (land.shape[axis] // N_DEV))
        return land.at[tuple(ix)]
    return land.at[pl.ds(0, N_DEV - 1)]


def _place_own(mode, srcs, axes, name, after=None):
    n = len(srcs)
    extra, extra_specs = _after(after)

    def body(me_ref, *refs):
        for a in range(n):
            out = refs[n + len(extra) + a]
            out[...] = refs[a][...].reshape(out.shape)

    def at_me(shape, axis):
        return lambda i, me: tuple(me[0] if d == axis else 0 for d in range(len(shape)))

    in_specs, out_specs = [], []
    for s, axis in zip(srcs, axes):
        if mode == "gather":
            in_specs.append(pl.BlockSpec(s.shape, lambda i, me, nd=len(s.shape): (0,) * nd))
            out_specs.append(pl.BlockSpec(s.shape, at_me(s.shape, axis)))
        else:
            blk = s.shape[:axis] + (s.shape[axis] // N_DEV,) + s.shape[axis + 1:]
            in_specs.append(pl.BlockSpec(blk, at_me(blk, axis)))
            out_specs.append(pl.BlockSpec((1,) + blk, at_me((1,) + blk, 0)))
    me = _lin(_me()).astype(jnp.int32).reshape(1)
    return pl.pallas_call(
        body, name=name, out_shape=[_S(_land_shape(mode, s, a), s.dtype) for s, a in zip(srcs, axes)],
        grid_spec=pltpu.PrefetchScalarGridSpec(num_scalar_prefetch=1, grid=(1,), in_specs=in_specs + extra_specs,
                                               out_specs=out_specs),
        compiler_params=_params(1))(me, *srcs, *extra)


def _push_start(mode, srcs, lands, axes, name):
    n = len(srcs)

    def body(*refs):
        src_refs, land_refs = refs[:n], refs[n:2 * n]
        send_sems, recv_sems = refs[2 * n], refs[2 * n + 1]
        token = refs[-1]
        x, y, c = _me()
        me = _lin((x, y, c))
        for a in range(n):
            for peer in _peers(x, y, c):
                pltpu.make_async_remote_copy(
                    src_ref=_src_view(mode, src_refs[a], axes[a], _lin(peer)),
                    dst_ref=_dst_view(mode, land_refs[a], axes[a], me),
                    send_sem=send_sems.at[a], recv_sem=recv_sems.at[a], device_id=peer, device_id_type=MESH).start()
        token[...] = jnp.zeros_like(token)

    hbm = lambda s: pltpu.HBM(s.shape, s.dtype)
    outs = pl.pallas_call(
        body, name=name,
        out_shape=(pltpu.SemaphoreType.DMA((n,)), pltpu.SemaphoreType.DMA((n,)), *[hbm(s) for s in srcs], *[hbm(s) for s in lands],
                   _S((SUB, LANE))),
        in_specs=[HBM_SPEC] * (2 * n),
        out_specs=(SEM_SPEC, SEM_SPEC, *[HBM_SPEC] * (2 * n), pl.BlockSpec(memory_space=pltpu.VMEM)),
        input_output_aliases={i: 2 + i for i in range(2 * n)},
        compiler_params=pltpu.CompilerParams(has_side_effects=EFFECT),
    )(*[pltpu.with_memory_space_constraint(s, pltpu.HBM) for s in list(srcs) + list(lands)])
    return outs[0], outs[1], outs[2:2 + n], outs[2 + n:2 + 2 * n], outs[-1]


def _push_wait(mode, send_sems, recv_sems, srcs, lands, axes, after, name):
    n = len(srcs)

    def body(*refs):
        land_refs = refs[n:2 * n]
        send_sems, recv_sems = refs[2 * n], refs[2 * n + 1]
        x, y, c = _me()
        for a in range(n):
            seven = _seven_blocks(mode, land_refs[a], axes[a])
            cp = pltpu.make_async_remote_copy(src_ref=seven, dst_ref=seven, send_sem=send_sems.at[a], recv_sem=recv_sems.at[a],
                                              device_id=(x, y, 1 - c), device_id_type=MESH)
            cp.wait_send()
            cp.wait_recv()

    hbm = lambda s: pltpu.HBM(s.shape, s.dtype)
    outs = pl.pallas_call(
        body, name=name, out_shape=tuple(hbm(s) for s in list(srcs) + list(lands)),
        in_specs=[HBM_SPEC] * (2 * n) + [SEM_SPEC, SEM_SPEC, ANY], out_specs=tuple([HBM_SPEC] * (2 * n)),
        input_output_aliases={i: i for i in range(2 * n)},
        compiler_params=pltpu.CompilerParams(has_side_effects=EFFECT),
    )(*srcs, *lands, send_sems, recv_sems, after)
    return outs[n:]


def _sum_parts(parts):
    n, R, C = parts.shape

    def body(p_ref, o_ref):
        g = p_ref[0]
        for k in range(1, n):
            g = g + p_ref[k]
        o_ref[...] = g

    return pl.pallas_call(body, name="sum_parts", out_shape=_S((R, C)))(parts)


def _block_diag(w, nb):
    tn, r, c = w.shape
    w = w.reshape(tn // nb, nb, r, c)
    return jnp.einsum('tarc,ab->tarbc', w, jnp.eye(nb, dtype=w.dtype)).reshape(tn // nb, nb * r, nb * c)


def _block_diag_extract(w, nb):
    t, R, C = w.shape
    w = w.reshape(t, nb, R // nb, nb, C // nb)
    return jnp.einsum('tarbc,ab->tarc', w, jnp.eye(nb, dtype=w.dtype)).reshape(t * nb, R // nb, C // nb)


SMALL = ['conv_b', 'rg_wa', 'rg_ba', 'rg_wx', 'rg_bx', 'rg_lambda', 's5_a_re', 's5_a_im', 's5_b_re', 's5_b_im',
         's5_c_re', 's5_c_im', 's5_d', 's5_log_step', 's5_b_glu', 'ln1_g', 'ln1_b', 'ple_gate_b', 'ln2_g', 'ln2_b']
WEIGHTS = ['w_in', 'conv_w', 'conv_b', 'rg_wa', 'rg_ba', 'rg_wx', 'rg_bx', 'rg_lambda', 's5_a_re', 's5_a_im', 's5_b_re',
           's5_b_im', 's5_c_re', 's5_c_im', 's5_d', 's5_log_step', 's5_w_glu', 's5_b_glu', 'w_out', 'ln1_g', 'ln1_b',
           'ple_w', 'ple_gate_w', 'ple_gate_b', 'ln2_g', 'ln2_b']
PACK_ROWS_MULT = 64


def _pack(tree):
    flat = jnp.concatenate([tree[k].reshape(-1) for k in SMALL])
    rows = -(-flat.shape[0] // (LANE * PACK_ROWS_MULT)) * PACK_ROWS_MULT
    return jnp.pad(flat, (0, rows * LANE - flat.shape[0])).reshape(rows, LANE)


def _unpack(packed, like):
    flat, out, o = packed.reshape(-1), {}, 0
    for k in SMALL:
        n = math.prod(like[k].shape)
        out[k] = flat[o:o + n].reshape(like[k].shape)
        o += n
    return out


class _NoHooks:
    token = None

    def layer_start(self, i, W, after):
        return W

    def late_weights(self, i, W, after):
        return W

    def post_done(self, i, g):
        return None

    def layer_done(self, i, g, dx):
        return None


def _local_grads(x, p, target, W, disc, hooks):
    depth = 2
    saved = []
    for i in range(depth):
        if i > 0:
            W = hooks.layer_start(i, W, x)
        w = W[i]
        z = _inproj_fwd(x, w['w_in'], hooks.token if i == 0 else None)
        hs = _rg_fwd(z, w['conv_w'], w['conv_b'], w['wa_bd'], w['wx_bd'], w['rg_ba'], w['rg_bx'], w['rg_lambda'], i)
        d = disc[i]
        y0, s_re, s_im = _s5_fwd(z, d['bb_re'], d['bb_im'], d['lb_re'], d['lb_im'], d['c_re'], d['c_im'], w['s5_d'], i)
        W = hooks.late_weights(i, W, y0)
        w = W[i]
        x2, t1, t2, m = _post_fwd(x, hs, z, y0, p[i], w['s5_w_glu'], w['s5_b_glu'], w['w_out'], w['ln1_g'], w['ln1_b'],
                                  w['ple_w'], w['ple_gate_w'], w['ple_gate_b'], w['ln2_g'], w['ln2_b'], i)
        saved.append((x, z, hs, y0, s_re, s_im, t1, t2, m))
        x = x2

    grads = [None] * depth
    dx = target
    loss = None
    token = None
    for i in reversed(range(depth)):
        w, d = W[i], disc[i]
        xin, z, hs, y0, s_re, s_im, t1, t2, m = saved[i]
        g = {}
        (dt1, g['ple_w'], g['ple_gate_w'], g['ple_gate_b'], g['ln1_g'], g['ln1_b'], g['ln2_g'], g['ln2_b'], lrow) = _post_bwd_a(
            dx, i == depth - 1, t2, t1, p[i], w['ple_w'], w['ple_gate_w'], w['ple_gate_b'], w['ln1_g'], w['ln1_b'],
            w['ln2_g'], w['ln2_b'], i, token)
        if i == depth - 1:
            loss = 0.5 / D_MODEL * jnp.sum(lrow)
        dhs, dy0, dzg, g['w_out'], g['s5_w_glu'], g['s5_b_glu'] = _post_bwd_b(dt1, m, z, hs, y0, w['w_out'], w['s5_w_glu'],
                                                                           w['s5_b_glu'], i)
        (dzu, g['bb_re'], g['bb_im'], g['lb_re'], g['lb_im'], g['c_re'], g['c_im'], g['s5_d']) = _s5_bwd(
            dy0, z, s_re, s_im, d['bb_re'], d['bb_im'], d['lb_re'], d['lb_im'], d['c_re'], d['c_im'], w['s5_d'], i,
            hooks.post_done(i, g))
        (dzx, g['conv_w'], g['conv_b'], g['wa_bd'], g['wx_bd'], g['rg_ba'], g['rg_bx'], g['rg_lambda']) = _rg_bwd(
            dhs, z, hs, w['conv_w'], w['conv_b'], w['wa_bd'], w['wx_bd'], w['rg_ba'], w['rg_bx'], w['rg_lambda'], i)
        dx, g['w_in'] = _inproj_bwd(dt1, xin, dzx, dzg, dzu, w['w_in'])
        grads[i] = g
        token = hooks.layer_done(i, g, dx)
    return loss, dx, grads


def _s5_layouts_fwd(s5_a_re, s5_a_im, s5_log_step, s5_b_re, s5_b_im, s5_c_re, s5_c_im):
    depth = s5_a_re.shape[0]
    ar, ai = s5_a_re.reshape(depth * 24, S5_P), s5_a_im.reshape(depth * 24, S5_P)
    ls = s5_log_step.reshape(depth * 24, 1)
    lr, li, cr, ci = _s5_disc_fwd(ar, ai, ls)
    col = lambda a: a.reshape(depth * S5_N, 1)
    br, bi = s5_b_re.reshape(depth * S5_N, 16), s5_b_im.reshape(depth * S5_N, 16)
    bbr, bbi = _s5_bscale_fwd(col(cr), col(ci), br, bi)
    disc = []
    for i in range(depth):
        gph = lambda a: a.reshape(depth, 24, S5_P, 16)[i]
        disc.append(dict(
            bb_re=_block_diag(jnp.swapaxes(gph(bbr), 1, 2), 8), bb_im=_block_diag(jnp.swapaxes(gph(bbi), 1, 2), 8),
            lb_re=lr.reshape(depth, 1, S5_N)[i], lb_im=li.reshape(depth, 1, S5_N)[i],
            c_re=_block_diag(jnp.swapaxes(s5_c_re[i], 1, 2), 8), c_im=_block_diag(jnp.swapaxes(s5_c_im[i], 1, 2), 8)))
    return disc, (ar, ai, ls, col(cr), col(ci), br, bi)


def _s5_layouts_bwd(grads, res):
    ar, ai, ls, cr, ci, br, bi = res
    depth = len(grads)
    stack = lambda f: jnp.stack([f(g) for g in grads])
    dbbr = stack(lambda g: jnp.swapaxes(_block_diag_extract(g['bb_re'], 8), 1, 2)).reshape(depth * S5_N, 16)
    dbbi = stack(lambda g: jnp.swapaxes(_block_diag_extract(g['bb_im'], 8), 1, 2)).reshape(depth * S5_N, 16)
    dbr, dbi, dcr, dci = _s5_bscale_bwd(cr, ci, br, bi, dbbr, dbbi)
    gp = lambda a: a.reshape(depth * 24, S5_P)
    dar, dai, dls = _s5_disc_bwd(ar, ai, ls, gp(stack(lambda g: g['lb_re'])), gp(stack(lambda g: g['lb_im'])), gp(dcr), gp(dci))
    return dict(
        s5_a_re=dar.reshape(depth, 24, S5_P), s5_a_im=dai.reshape(depth, 24, S5_P), s5_log_step=dls.reshape(depth, 24),
        s5_b_re=dbr.reshape(depth, 24, S5_P, 16), s5_b_im=dbi.reshape(depth, 24, S5_P, 16),
        s5_c_re=stack(lambda g: jnp.swapaxes(_block_diag_extract(g['c_re'], 8), 1, 2)),
        s5_c_im=stack(lambda g: jnp.swapaxes(_block_diag_extract(g['c_im'], 8), 1, 2)))


LATE = ('w_out', 'ple_w', 'ple_gate_w', 's5_w_glu')


ROWS = ('conv_b', 'rg_ba', 'rg_bx', 'rg_lambda', 's5_d', 's5_b_glu', 'ln1_g', 'ln1_b', 'ple_gate_b', 'ln2_g', 'ln2_b')


def _shared_weights(full):
    depth = full['conv_b'].shape[0]
    shared = {k: full[k].reshape(depth, 1, -1) for k in ROWS}
    shared['conv_w'] = full['conv_w']
    shared['wa_bd'] = _block_diag(full['rg_wa'].reshape(depth * 10, 64, 64), 2)
    shared['wx_bd'] = _block_diag(full['rg_wx'].reshape(depth * 10, 64, 64), 2)
    return shared


def _layer_weights(full, shared, i):
    return dict(shared, w_in=full['w_in'][i])


class _AllLocal(_NoHooks):
    def __init__(self, full):
        self.full = full

    def late_weights(self, i, W, after):
        W[i].update({k: self.full[k][i] for k in LATE})
        return W


def _full_grads(full, x, p, target, hooks=None):
    disc, res = _s5_layouts_fwd(full['s5_a_re'], full['s5_a_im'], full['s5_log_step'], full['s5_b_re'], full['s5_b_im'],
                                full['s5_c_re'], full['s5_c_im'])
    shared = _shared_weights(full)
    W = [_layer_weights(full, shared, i) for i in range(2)]
    loss, gx, grads = _local_grads(x, p, target, W, disc, hooks or _AllLocal(full))
    stack = lambda f: jnp.stack([f(g) for g in grads])
    out = _s5_layouts_bwd(grads, res)
    for k in SHARD_AXIS:
        out[k] = [g[k] for g in grads]
    out['conv_w'] = stack(lambda g: g['conv_w'])
    for k in ('conv_b', 'rg_ba', 'rg_bx', 'rg_lambda', 's5_b_glu', 'ln1_g', 'ln1_b', 'ple_gate_b', 'ln2_g', 'ln2_b'):
        out[k] = stack(lambda g: g[k][0])
    out['s5_d'] = stack(lambda g: g['s5_d'][0]).reshape(2, 24, 16)
    out['rg_wa'] = stack(lambda g: _block_diag_extract(g['wa_bd'], 2))
    out['rg_wx'] = stack(lambda g: _block_diag_extract(g['wx_bd'], 2))
    return loss, gx, out


SHARD_AXIS = {'w_in': 2, 'w_out': 1, 'ple_w': 2, 'ple_gate_w': 1, 's5_w_glu': 1}


def kernel(x, p, w_in, conv_w, conv_b, rg_wa, rg_ba, rg_wx, rg_bx, rg_lambda, s5_a_re, s5_a_im, s5_b_re, s5_b_im, s5_c_re, s5_c_im, s5_d, s5_log_step, s5_w_glu, s5_b_glu, w_out, ln1_g, ln1_b, ple_w, ple_gate_w, ple_gate_b, ln2_g, ln2_b, loss_target, m_w_in, m_conv_w, m_conv_b, m_rg_wa, m_rg_ba, m_rg_wx, m_rg_bx, m_rg_lambda, m_s5_a_re, m_s5_a_im, m_s5_b_re, m_s5_b_im, m_s5_c_re, m_s5_c_im, m_s5_d, m_s5_log_step, m_s5_w_glu, m_s5_b_glu, m_w_out, m_ln1_g, m_ln1_b, m_ple_w, m_ple_gate_w, m_ple_gate_b, m_ln2_g, m_ln2_b, v_w_in, v_conv_w, v_conv_b, v_rg_wa, v_rg_ba, v_rg_wx, v_rg_bx, v_rg_lambda, v_s5_a_re, v_s5_a_im, v_s5_b_re, v_s5_b_im, v_s5_c_re, v_s5_c_im, v_s5_d, v_s5_log_step, v_s5_w_glu, v_s5_b_glu, v_w_out, v_ln1_g, v_ln1_b, v_ple_w, v_ple_gate_w, v_ple_gate_b, v_ln2_g, v_ln2_b):
    local = dict(locals())
    w = {k: local[k] for k in WEIGHTS}
    mom = {k: local['m_' + k] for k in WEIGHTS}
    var = {k: local['v_' + k] for k in WEIGHTS}

    big = list(SHARD_AXIS)
    wire = {k: w[k].astype(WIRE) for k in big}
    first = _all_gather([wire['w_in'][0][None], conv_w[None]], [0, 0], "gather_first_weights")
    late_axes = [SHARD_AXIS[k] - 1 for k in LATE]
    pushed = {}

    def push_weights(key, srcs, axes, after):
        pushed[key] = _push_start("gather", srcs, _place_own("gather", srcs, axes, "place_weights_" + key, after=after), axes,
                                  "push_weights_" + key)
        return pushed[key][4]

    def await_weights(key, axes, after):
        s = pushed[key]
        return _push_wait("gather", s[0], s[1], s[2], s[3], axes, after, "await_weights_" + key)

    token0 = push_weights("l0", [wire[k][0] for k in LATE], late_axes, first[0])
    push_weights("l1", [wire['w_in'][1][None]] + [wire[k][1] for k in LATE], [0] + late_axes, token0)

    def push_grads(key, g, names, axes):
        srcs = [g[k] for k in names]
        pushed[key] = _push_start("scatter", srcs, _place_own("scatter", srcs, axes, "place_grads_" + key), axes,
                                  "push_grads_" + key)
        return pushed[key][4]

    def await_grads(key, axes, after):
        s = pushed[key]
        return _push_wait("scatter", s[0], s[1], s[2], s[3], axes, after, "await_grads_" + key)

    class Overlap(_NoHooks):
        token = pushed["l1"][4]

        def late_weights(self, i, W, after):
            if i == 0:
                W[0].update(zip(LATE, await_weights("l0", late_axes, after)))
            return W

        def layer_start(self, i, W, after):
            lands = await_weights("l1", [0] + late_axes, after)
            W[1].update(zip(LATE, lands[1:]), w_in=lands[0])
            return W

        def post_done(self, i, g):
            return push_grads("late0", g, LATE, late_axes) if i == 0 else None

        def layer_done(self, i, g, dx):
            return push_grads("all1", g, ['w_in'] + list(LATE), [0] + late_axes) if i == 1 else None

    hooks = Overlap()
    full = dict(w)
    full['w_in'] = [first[0], None]
    full['conv_w'] = jnp.moveaxis(first[1], 0, 2).reshape(2, 4, RG_W)

    loss, grad_x, g = _full_grads(full, x[0], p[:, 0], loss_target[0], hooks)
    loss = lax.psum(loss, ("x", "y", "c"))

    conv_blocks = jnp.moveaxis(g['conv_w'].reshape(2, 4, N_DEV, RG_W // N_DEV), 2, 0).reshape(N_DEV, 8, RG_W // N_DEV)
    packed = _pack(g)
    w_in0, conv_parts, small_parts = _exchange([[g['w_in'][0]], [conv_blocks], [packed]], [0, 0, 0], "exchange_grads")
    recv1 = dict(zip(['w_in'] + list(LATE), await_grads("all1", [0] + late_axes, grad_x)))
    recv0 = dict(zip(LATE, await_grads("late0", late_axes, grad_x)), w_in=w_in0)
    outs = {}
    for k in big + ['conv_w']:
        shard = w[k].shape
        c = shard[-1]
        two = lambda a: a.reshape(-1, c)
        parts = [conv_parts.reshape(N_DEV, -1, c)] if k == 'conv_w' else [r[k].reshape(N_DEV, -1, c) for r in (recv0, recv1)]
        outs[k] = [o.reshape(shard) for o in _adamw(parts, two(w[k]), two(mom[k]), two(var[k]))]

    rows = packed.shape[0] // N_DEV
    mine = _sum_parts(small_parts.reshape(N_DEV, rows, LANE))
    summed = _unpack(_all_gather([mine], [0], "gather_small_grads")[0], w)
    narrow = ['s5_b_re', 's5_b_im']
    for names, name in ((narrow, "adamw_s5_b"), ([k for k in SMALL if k not in narrow], "adamw_small")):
        delta, new_m, new_v = _adamw_natural(names, summed, w, mom, var, name)
        for k in names:
            outs[k] = [summed[k], delta[k], new_m[k], new_v[k]]

    res = [loss, grad_x[None]]
    for j in range(4):
        res += [outs[k][j] for k in WEIGHTS]
    return tuple(res)
```

```python
import functools
import math

import jax
import jax.numpy as jnp
from jax import lax
from jax.experimental import pallas as pl
from jax.experimental.pallas import tpu as pltpu

F32 = jnp.float32
MXU = jnp.bfloat16
WIRE = jnp.bfloat16

N_DEV = 8
D_MODEL = 1024
RG_W = 640
S5_W = 384
S5_P = 64
S5_N = 24 * S5_P
Z_W = 2 * RG_W + 2 * S5_W
C_RGG = RG_W
C_S5U = 2 * RG_W
C_S5G = 2 * RG_W + S5_W
LANE = 128
N_RG_T = RG_W // LANE
N_S5_T = S5_W // LANE
W_BLK = Z_W // N_DEV
ALPHA = (2.0 * 2) ** 0.25
LN_EPS = 1e-5
RG_C = 8.0
LR, B1, B2, EPS, WD, STEP = 0.001, 0.9, 0.999, 1e-08, 0.01, 10
BC1 = 1.0 - B1 ** STEP
BC2 = 1.0 - B2 ** STEP
RC = 256
TM = 256
VMEM_LIMIT = 56 * 1024 * 1024

MESH = pl.DeviceIdType.MESH
ANY = pl.BlockSpec(memory_space=pl.ANY)


def _params(n_grid_axes, vmem=VMEM_LIMIT):
    return pltpu.CompilerParams(dimension_semantics=("arbitrary",) * n_grid_axes, vmem_limit_bytes=vmem)


def _S(shape, dtype=F32):
    return jax.ShapeDtypeStruct(tuple(shape), dtype)


def _sigmoid(x):
    return 0.5 * jnp.tanh(0.5 * x) + 0.5


def _silu_and_grad(x):
    s = _sigmoid(x)
    return x * s, s * (1.0 + x * (1.0 - s))


_GELU_C = math.sqrt(2.0 / math.pi)


def _gelu(x):
    return 0.5 * x * (1.0 + jnp.tanh(_GELU_C * (x + 0.044715 * (x * x * x))))


def _gelu_grad(x):
    th = jnp.tanh(_GELU_C * (x + 0.044715 * (x * x * x)))
    return 0.5 * (1.0 + th) + 0.5 * x * (1.0 - th * th) * (_GELU_C * (1.0 + 3.0 * 0.044715 * (x * x)))


def _mm(a, b):
    return jnp.dot(a.astype(MXU), b.astype(MXU), preferred_element_type=F32)


def _mm_nt(a, b):
    return lax.dot_general(a.astype(MXU), b.astype(MXU), (((1,), (1,)), ((), ())), preferred_element_type=F32)


def _mm_tn(a, b):
    return lax.dot_general(a.astype(MXU), b.astype(MXU), (((0,), (0,)), ((), ())), preferred_element_type=F32)


def _ln_fwd(t, g, b):
    mu = jnp.mean(t, axis=-1, keepdims=True)
    tc = t - mu
    var = jnp.mean(tc * tc, axis=-1, keepdims=True)
    rstd = lax.rsqrt(var + LN_EPS)
    xhat = tc * rstd
    return xhat * g + b, xhat, rstd


def _ln_bwd(dy, xhat, rstd, g):
    dxh = dy * g
    m1 = jnp.mean(dxh, axis=-1, keepdims=True)
    m2 = jnp.mean(dxh * xhat, axis=-1, keepdims=True)
    return rstd * (dxh - m1 - xhat * m2)


def _colsum(a):
    return jnp.sum(a, axis=0, keepdims=True)


def _up(x, d, rows, fill):
    n = x.shape[0]
    return jnp.where(rows < n - d, pltpu.roll(x, n - d, 0), fill)


SUB = 8
TILE_STEPS = (1, 2, 4)


def _r8(width):
    return lax.broadcasted_iota(jnp.int32, (SUB, width), 0)


def _scan_real(a, u, carry, reverse=False):
    r8 = _r8(a.shape[1])
    n = a.shape[0] // SUB
    outs = [None] * n
    for k in (reversed(range(n)) if reverse else range(n)):
        A, U = a[SUB * k:SUB * k + SUB], u[SUB * k:SUB * k + SUB]
        for d in TILE_STEPS:
            m = (r8 < SUB - d) if reverse else (r8 >= d)
            sh = SUB - d if reverse else d
            U = A * jnp.where(m, pltpu.roll(U, sh, 0), 0.0) + U
            A = A * jnp.where(m, pltpu.roll(A, sh, 0), 1.0)
        h = A * carry + U
        outs[k] = h
        carry = h[0:1] if reverse else h[SUB - 1:SUB]
    return jnp.concatenate(outs, axis=0), carry


def _tile_powers(lr, li, reverse=False):
    width = lr.shape[1]
    r8 = _r8(width)
    steps = []
    pr, pi = lr, li
    er, ei = jnp.broadcast_to(lr, (SUB, width)), jnp.broadcast_to(li, (SUB, width))
    for d in TILE_STEPS:
        m = (r8 < SUB - d) if reverse else (r8 >= d)
        sh = SUB - d if reverse else d
        steps.append((sh, jnp.where(m, pr, 0.0), jnp.where(m, pi, 0.0)))
        er, ei = _cmul(er, ei, jnp.where(m, pltpu.roll(er, sh, 0), 1.0), jnp.where(m, pltpu.roll(ei, sh, 0), 0.0))
        pr, pi = _cmul(pr, pi, pr, pi)
    return steps, (er, ei)


def _scan_lti(xr, xi, carry, steps, e, reverse=False):
    er, ei = e
    kr, ki = carry
    n = xr.shape[0] // SUB
    outr, outi = [None] * n, [None] * n
    for k in (reversed(range(n)) if reverse else range(n)):
        sr, si = xr[SUB * k:SUB * k + SUB], xi[SUB * k:SUB * k + SUB]
        for sh, pr, pi in steps:
            shr, shi = pltpu.roll(sr, sh, 0), pltpu.roll(si, sh, 0)
            sr, si = sr + (pr * shr - pi * shi), si + (pr * shi + pi * shr)
        sr = sr + (er * kr - ei * ki)
        si = si + (er * ki + ei * kr)
        outr[k], outi[k] = sr, si
        kr, ki = (sr[0:1], si[0:1]) if reverse else (sr[SUB - 1:SUB], si[SUB - 1:SUB])
    return jnp.concatenate(outr, axis=0), jnp.concatenate(outi, axis=0), (kr, ki)


def _halo(ref, c, r0):
    rp = pl.multiple_of(jnp.maximum(r0 - 8, 0), 8)
    return jnp.where(c > 0, ref[pl.ds(rp, 8), :], 0.0)


def _conv_taps(xe):
    return [pltpu.roll(xe, 3, 0)[8:, :], pltpu.roll(xe, 2, 0)[8:, :], pltpu.roll(xe, 1, 0)[8:, :], xe[8:, :]]


def _rg_gates(h, wa, wx, ba, bx, sp):
    r = _sigmoid(_mm(h, wa) + ba)
    i = _sigmoid(_mm(h, wx) + bx)
    log_a = (-RG_C) * r * sp
    a = jnp.exp(log_a)
    mult = jnp.sqrt(-jnp.tanh(log_a) * (a * a + 1.0))
    return r, i, a, mult


def _softplus(y):
    return jnp.maximum(y, 0.0) + jnp.log1p(jnp.exp(-jnp.abs(y)))


def _after(token):
    return ([], []) if token is None else ([token], [ANY])


def _inproj_fwd(x, w_in, token=None):
    L = x.shape[0]

    def body(x_ref, w_ref, *rest):
        xb = x_ref[...].astype(MXU)
        for j in range(N_DEV):
            rest[-1][:, j * W_BLK:(j + 1) * W_BLK] = jnp.dot(xb, w_ref[j].astype(MXU), preferred_element_type=F32)

    extra, extra_specs = _after(token)
    return pl.pallas_call(
        body, name="inproj_fwd", grid=(L // TM,),
        in_specs=[pl.BlockSpec((TM, D_MODEL), lambda i: (i, 0)),
                  pl.BlockSpec((N_DEV, D_MODEL, W_BLK), lambda i: (0, 0, 0))] + extra_specs,
        out_specs=pl.BlockSpec((TM, Z_W), lambda i: (i, 0)),
        out_shape=_S((L, Z_W)), compiler_params=_params(1))(x, w_in, *extra)


def _inproj_bwd(dt1, x, dzx, dzg, dzu, w_in):
    L = x.shape[0]

    def body(dt1_ref, x_ref, dzx_ref, dzg_ref, dzu_ref, w_ref, dx_ref, dw_ref, acc_ref):
        @pl.when(pl.program_id(0) == 0)
        def _():
            acc_ref[...] = jnp.zeros_like(acc_ref)
        dzg = dzg_ref[...]
        dz = jnp.concatenate([dzx_ref[...], dzg[:, :RG_W], dzu_ref[...], dzg[:, RG_W:]], axis=1).astype(MXU)
        xb = x_ref[...].astype(MXU)
        dx = ALPHA * dt1_ref[...]
        for j in range(N_DEV):
            dzj = dz[:, j * W_BLK:(j + 1) * W_BLK]
            dx = dx + _mm_nt(dzj, w_ref[j])
            acc_ref[j] += _mm_tn(xb, dzj)
        dx_ref[...] = dx

        @pl.when(pl.program_id(0) == L // TM - 1)
        def _():
            dw_ref[...] = acc_ref[...].astype(WIRE)

    row = lambda w: pl.BlockSpec((TM, w), lambda i: (i, 0))
    wspec = pl.BlockSpec((N_DEV, D_MODEL, W_BLK), lambda i: (0, 0, 0))
    return pl.pallas_call(
        body, name="inproj_bwd", grid=(L // TM,),
        in_specs=[row(D_MODEL), row(D_MODEL), row(RG_W), row(D_MODEL), row(S5_W), wspec],
        out_specs=[row(D_MODEL), wspec],
        out_shape=[_S((L, D_MODEL)), _S((N_DEV, D_MODEL, W_BLK), WIRE)],
        scratch_shapes=[pltpu.VMEM((N_DEV, D_MODEL, W_BLK), F32)],
        compiler_params=_params(1))(dt1, x, dzx, dzg, dzu, w_in)


def _rg_specs(layer):
    tile = lambda rows: pl.BlockSpec((rows, LANE), lambda c: (0, c))
    ptile = lambda rows: pl.BlockSpec((None, rows, LANE), lambda c: (layer, 0, c))
    pbd = pl.BlockSpec((None, LANE, LANE), lambda c: (layer * N_RG_T + c, 0, 0))
    return tile, ptile, pbd, pl.BlockSpec((None, LANE, LANE), lambda c: (c, 0, 0))


def _rg_fwd(z, cw, cb, wa_bd, wx_bd, ba, bx, lam, layer):
    L = z.shape[0]

    def body(x_ref, cw_ref, cb_ref, wa_ref, wx_ref, ba_ref, bx_ref, lam_ref, hs_ref):
        w, b = cw_ref[...], cb_ref[...]
        wa, wx, ba_, bx_ = wa_ref[...].astype(MXU), wx_ref[...].astype(MXU), ba_ref[...], bx_ref[...]
        sp = _softplus(-lam_ref[...])

        def step(c, carry):
            r0 = pl.multiple_of(c * RC, RC)
            xe = jnp.concatenate([_halo(x_ref, c, r0), x_ref[pl.ds(r0, RC), :]], axis=0)
            t = _conv_taps(xe)
            h = t[0] * w[0:1] + t[1] * w[1:2] + t[2] * w[2:3] + t[3] * w[3:4] + b
            _, i, a, mult = _rg_gates(h, wa, wx, ba_, bx_, sp)
            hs, carry = _scan_real(a, mult * (i * h), carry)
            hs_ref[pl.ds(r0, RC), :] = hs
            return carry

        lax.fori_loop(0, L // RC, step, jnp.zeros((1, LANE), F32))

    tile, ptile, pbd, _ = _rg_specs(layer)
    return pl.pallas_call(
        body, name="rg_fwd", grid=(N_RG_T,),
        in_specs=[tile(L), ptile(4), ptile(1), pbd, pbd, ptile(1), ptile(1), ptile(1)],
        out_specs=tile(L), out_shape=_S((L, RG_W)), compiler_params=_params(1))(z, cw, cb, wa_bd, wx_bd, ba, bx, lam)


def _rg_bwd(dhs, z, hs, cw, cb, wa_bd, wx_bd, ba, bx, lam, layer):
    L = z.shape[0]

    def body(g_ref, x_ref, hs_ref, cw_ref, cb_ref, wa_ref, wx_ref, ba_ref, bx_ref, lam_ref,
             dx_ref, dcw_ref, dcb_ref, dwa_ref, dwx_ref, dba_ref, dbx_ref, dlam_ref):
        w, b = cw_ref[...], cb_ref[...]
        wa, wx, ba_, bx_ = wa_ref[...].astype(MXU), wx_ref[...].astype(MXU), ba_ref[...], bx_ref[...]
        lam = lam_ref[...]
        sp = _softplus(-lam)
        rows = lax.broadcasted_iota(jnp.int32, (RC, LANE), 0)
        for ref in (dcw_ref, dcb_ref, dwa_ref, dwx_ref, dba_ref, dbx_ref, dlam_ref):
            ref[...] = jnp.zeros_like(ref)
        nch = L // RC

        def step(k, carry):
            cin, nxt = carry
            c = nch - 1 - k
            r0 = pl.multiple_of(c * RC, RC)
            xe = jnp.concatenate([_halo(x_ref, c, r0), x_ref[pl.ds(r0, RC), :]], axis=0)
            t = _conv_taps(xe)
            h = t[0] * w[0:1] + t[1] * w[1:2] + t[2] * w[2:3] + t[3] * w[3:4] + b
            r, i, a, mult = _rg_gates(h, wa, wx, ba_, bx_, sp)
            hs_e = jnp.concatenate([_halo(hs_ref, c, r0), hs_ref[pl.ds(r0, RC), :]], axis=0)
            hs_prev = pltpu.roll(hs_e, 1, 0)[8:, :]
            g = g_ref[pl.ds(r0, RC), :]
            cc, cin_new = _scan_real(a, a * g, cin, reverse=True)
            dh = g + _up(cc, 1, rows, cin)
            ih = i * h
            dlog_a = dh * hs_prev * a - (dh * ih) * (a * a) / mult
            di = dh * mult * h
            dhin = dh * mult * i
            dr = dlog_a * ((-RG_C) * sp)
            dlam_ref[...] += _colsum(dlog_a * r)
            dra = dr * r * (1.0 - r)
            dia = di * i * (1.0 - i)
            dwa_ref[...] += _mm_tn(h, dra)
            dwx_ref[...] += _mm_tn(h, dia)
            dba_ref[...] += _colsum(dra)
            dbx_ref[...] += _colsum(dia)
            dhin = dhin + _mm_nt(dra, wa) + _mm_nt(dia, wx)
            de = jnp.concatenate([dhin, nxt], axis=0)
            n = RC + 8
            dx = (dhin * w[3:4] + pltpu.roll(de, n - 1, 0)[:RC, :] * w[2:3]
                  + pltpu.roll(de, n - 2, 0)[:RC, :] * w[1:2] + pltpu.roll(de, n - 3, 0)[:RC, :] * w[0:1])
            dx_ref[pl.ds(r0, RC), :] = dx
            for kk in range(4):
                dcw_ref[kk:kk + 1, :] += _colsum(dhin * t[kk])
            dcb_ref[...] += _colsum(dhin)
            return cin_new, dhin[0:8, :]

        lax.fori_loop(0, nch, step, (jnp.zeros((1, LANE), F32), jnp.zeros((8, LANE), F32)))
        dlam_ref[...] = dlam_ref[...] * (RG_C * _sigmoid(-lam))

    tile, ptile, pbd, bd = _rg_specs(layer)
    return pl.pallas_call(
        body, name="rg_bwd", grid=(N_RG_T,),
        in_specs=[tile(L), tile(L), tile(L), ptile(4), ptile(1), pbd, pbd, ptile(1), ptile(1), ptile(1)],
        out_specs=[tile(L), tile(4), tile(1), bd, bd, tile(1), tile(1), tile(1)],
        out_shape=[_S((L, RG_W)), _S((4, RG_W)), _S((1, RG_W)), _S((N_RG_T, LANE, LANE)), _S((N_RG_T, LANE, LANE)),
                   _S((1, RG_W)), _S((1, RG_W)), _S((1, RG_W))],
        compiler_params=_params(1))(dhs, z, hs, cw, cb, wa_bd, wx_bd, ba, bx, lam)


def _cmul(ar, ai, br, bi):
    return ar * br - ai * bi, ar * bi + ai * br


S5_TW = S5_N // N_S5_T


def _s5_specs(L):
    in_tile = pl.BlockSpec((L, LANE), lambda t: (0, t))
    st = pl.BlockSpec((L, S5_TW), lambda t: (0, t))
    bb = pl.BlockSpec((None, LANE, S5_TW), lambda t: (t, 0, 0))
    cc = pl.BlockSpec((None, S5_TW, LANE), lambda t: (t, 0, 0))
    lb = pl.BlockSpec((1, S5_TW), lambda t: (0, t))
    dv = pl.BlockSpec((1, LANE), lambda t: (0, t))
    return in_tile, st, bb, cc, lb, dv


def _layer_row_tile(layer):
    return pl.BlockSpec((None, 1, LANE), lambda t: (layer, 0, t))


def _s5_fwd(z, bb_re, bb_im, lb_re, lb_im, c_re, c_im, dvec, layer):
    L = z.shape[0]

    def body(u_ref, bbr_ref, bbi_ref, lr_ref, li_ref, cr_ref, ci_ref, d_ref, y_ref, sr_ref, si_ref):
        bbr, bbi = bbr_ref[...].astype(MXU), bbi_ref[...].astype(MXU)
        cr, ci = cr_ref[...].astype(MXU), ci_ref[...].astype(MXU)
        dv = d_ref[...]
        steps, e = _tile_powers(lr_ref[...], li_ref[...])

        def step(c, carry):
            r0 = pl.multiple_of(c * RC, RC)
            u = u_ref[pl.ds(r0, RC), :]
            ub = u.astype(MXU)
            sr = jnp.dot(ub, bbr, preferred_element_type=F32)
            si = jnp.dot(ub, bbi, preferred_element_type=F32)
            sr, si, carry = _scan_lti(sr, si, carry, steps, e)
            sr_ref[pl.ds(r0, RC), :] = sr
            si_ref[pl.ds(r0, RC), :] = si
            y_ref[pl.ds(r0, RC), :] = dv * u + (_mm(sr, cr) - _mm(si, ci))
            return carry

        zero = jnp.zeros((1, S5_TW), F32)
        lax.fori_loop(0, L // RC, step, (zero, zero))

    in_tile, st, bb, cc, lb, dv = _s5_specs(L)
    u_tile = pl.BlockSpec((L, LANE), lambda t: (0, C_S5U // LANE + t))
    return pl.pallas_call(
        body, name="s5_fwd", grid=(N_S5_T,),
        in_specs=[u_tile, bb, bb, lb, lb, cc, cc, _layer_row_tile(layer)],
        out_specs=[in_tile, st, st],
        out_shape=[_S((L, S5_W)), _S((L, S5_N)), _S((L, S5_N))],
        compiler_params=_params(1))(z, bb_re, bb_im, lb_re, lb_im, c_re, c_im, dvec)


def _s5_bwd(dy0, z, s_re, s_im, bb_re, bb_im, lb_re, lb_im, c_re, c_im, dvec, layer, token=None):
    L = z.shape[0]
    extra, extra_specs = _after(token)

    def body(dy_ref, u_ref, sr_ref, si_ref, bbr_ref, bbi_ref, lr_ref, li_ref, cr_ref, ci_ref, d_ref, *rest):
        du_ref, dbbr_ref, dbbi_ref, dlr_ref, dli_ref, dcr_ref, dci_ref, dd_ref = rest[len(extra):]
        bbr, bbi = bbr_ref[...].astype(MXU), bbi_ref[...].astype(MXU)
        cr, ci = cr_ref[...].astype(MXU), ci_ref[...].astype(MXU)
        lr, li = lr_ref[...], -li_ref[...]
        dv = d_ref[...]
        steps, e = _tile_powers(lr, li, reverse=True)
        for ref in (dbbr_ref, dbbi_ref, dlr_ref, dli_ref, dcr_ref, dci_ref, dd_ref):
            ref[...] = jnp.zeros_like(ref)
        nch = L // RC

        def step(k, carry):
            c = nch - 1 - k
            r0 = pl.multiple_of(c * RC, RC)
            dy = dy_ref[pl.ds(r0, RC), :]
            u = u_ref[pl.ds(r0, RC), :]
            dyb, ub = dy.astype(MXU), u.astype(MXU)
            sr, si = sr_ref[pl.ds(r0, RC), :], si_ref[pl.ds(r0, RC), :]
            dcr_ref[...] += _mm_tn(sr, dyb)
            dci_ref[...] -= _mm_tn(si, dyb)
            gr = _mm_nt(dyb, cr)
            gi = -_mm_nt(dyb, ci)
            gr, gi, carry = _scan_lti(gr, gi, carry, steps, e, reverse=True)
            pr_ = pltpu.roll(jnp.concatenate([_halo(sr_ref, c, r0), sr], axis=0), 1, 0)[8:, :]
            pi_ = pltpu.roll(jnp.concatenate([_halo(si_ref, c, r0), si], axis=0), 1, 0)[8:, :]
            dlr_ref[...] += _colsum(pr_ * gr + pi_ * gi)
            dli_ref[...] += _colsum(pr_ * gi - pi_ * gr)
            grb, gib = gr.astype(MXU), gi.astype(MXU)
            dbbr_ref[...] += _mm_tn(ub, grb)
            dbbi_ref[...] += _mm_tn(ub, gib)
            du_ref[pl.ds(r0, RC), :] = dv * dy + (_mm_nt(grb, bbr) + _mm_nt(gib, bbi))
            dd_ref[...] += _colsum(dy * u)
            return carry

        zero = jnp.zeros((1, S5_TW), F32)
        lax.fori_loop(0, nch, step, (zero, zero))

    in_tile, st, bb, cc, lb, dv = _s5_specs(L)
    u_tile = pl.BlockSpec((L, LANE), lambda t: (0, C_S5U // LANE + t))
    return pl.pallas_call(
        body, name="s5_bwd", grid=(N_S5_T,),
        in_specs=[in_tile, u_tile, st, st, bb, bb, lb, lb, cc, cc, _layer_row_tile(layer)] + extra_specs,
        out_specs=[in_tile, bb, bb, lb, lb, cc, cc, dv],
        out_shape=[_S((L, S5_W)), _S((N_S5_T, LANE, S5_TW)), _S((N_S5_T, LANE, S5_TW)), _S((1, S5_N)), _S((1, S5_N)),
                   _S((N_S5_T, S5_TW, LANE)), _S((N_S5_T, S5_TW, LANE)), _S((1, S5_W))],
        compiler_params=_params(1))(dy0, z, s_re, s_im, bb_re, bb_im, lb_re, lb_im, c_re, c_im, dvec, *extra)


def _disc(ar, ai, ls):
    dt = jnp.exp(ls)
    mag = jnp.exp(ar * dt)
    lr = mag * jnp.cos(ai * dt)
    li = mag * jnp.sin(ai * dt)
    den = ar * ar + ai * ai
    cr = ((lr - 1.0) * ar + li * ai) / den
    ci = (li * ar - (lr - 1.0) * ai) / den
    return lr, li, cr, ci


def _s5_disc_fwd(ar, ai, ls):
    def body(ar_ref, ai_ref, ls_ref, lr_ref, li_ref, cr_ref, ci_ref):
        lr, li, cr, ci = _disc(ar_ref[...], ai_ref[...], ls_ref[...])
        lr_ref[...], li_ref[...], cr_ref[...], ci_ref[...] = lr, li, cr, ci

    sh = _S(ar.shape)
    return pl.pallas_call(body, name="s5_disc_fwd", out_shape=[sh, sh, sh, sh])(ar, ai, ls)


def _s5_disc_bwd(ar, ai, ls, dlr, dli, dcr, dci):
    def body(ar_ref, ai_ref, ls_ref, dlr_ref, dli_ref, dcr_ref, dci_ref, dar_ref, dai_ref, dls_ref):
        _, vjp = jax.vjp(_disc, ar_ref[...], ai_ref[...], jnp.broadcast_to(ls_ref[...], ar_ref.shape))
        dar, dai, dls = vjp((dlr_ref[...], dli_ref[...], dcr_ref[...], dci_ref[...]))
        dar_ref[...], dai_ref[...] = dar, dai
        dls_ref[...] = jnp.sum(dls, axis=1, keepdims=True)

    return pl.pallas_call(body, name="s5_disc_bwd", out_shape=[_S(ar.shape), _S(ar.shape), _S(ls.shape)])(
        ar, ai, ls, dlr, dli, dcr, dci)


def _s5_bscale_fwd(cr, ci, br, bi):
    def body(cr_ref, ci_ref, br_ref, bi_ref, or_ref, oi_ref):
        or_ref[...], oi_ref[...] = _cmul(cr_ref[...], ci_ref[...], br_ref[...], bi_ref[...])

    return pl.pallas_call(body, name="s5_bscale_fwd", out_shape=[_S(br.shape), _S(br.shape)])(cr, ci, br, bi)


def _s5_bscale_bwd(cr, ci, br, bi, gr, gi):
    def body(cr_ref, ci_ref, br_ref, bi_ref, gr_ref, gi_ref, dbr_ref, dbi_ref, dcr_ref, dci_ref):
        cr_, ci_, br_, bi_, gr_, gi_ = (r[...] for r in (cr_ref, ci_ref, br_ref, bi_ref, gr_ref, gi_ref))
        dbr_ref[...] = cr_ * gr_ + ci_ * gi_
        dbi_ref[...] = cr_ * gi_ - ci_ * gr_
        dcr_ref[...] = jnp.sum(gr_ * br_ + gi_ * bi_, axis=1, keepdims=True)
        dci_ref[...] = jnp.sum(gi_ * br_ - gr_ * bi_, axis=1, keepdims=True)

    return pl.pallas_call(body, name="s5_bscale_bwd",
                          out_shape=[_S(br.shape), _S(br.shape), _S(cr.shape), _S(cr.shape)])(cr, ci, br, bi, gr, gi)


def _row(w):
    return pl.BlockSpec((TM, w), lambda i: (i, 0))


def _full(shape):
    return pl.BlockSpec(tuple(shape), lambda i: (0,) * len(shape))


def _lrow(layer, width):
    return pl.BlockSpec((None, 1, width), lambda i: (layer, 0, 0))


def _post_fwd(x, hs, z, y0, p, w_glu, b_glu, w_out, g1, b1, ple_w, w_pg, b_pg, g2, b2, layer):
    L = x.shape[0]

    def body(x_ref, hs_ref, z_ref, y0_ref, p_ref, wg_ref, bg_ref, wo_ref, g1_ref, b1_ref, pw_ref, wpg_ref, bpg_ref,
             g2_ref, b2_ref, x2_ref, t1_ref, t2_ref, m_ref):
        rg_gate = z_ref[:, C_RGG:C_RGG + RG_W]
        s5_gate = z_ref[:, C_S5G:C_S5G + S5_W]
        rg_y = hs_ref[...] * _silu_and_grad(rg_gate)[0]
        y1 = _gelu(y0_ref[...])
        gl = _sigmoid(_mm(y1, wg_ref[...]) + bg_ref[...])
        s5_y = (y1 * gl) * _silu_and_grad(s5_gate)[0]
        m_ref[:, :RG_W] = rg_y
        m_ref[:, RG_W:] = s5_y
        mix = _mm(m_ref[...], wo_ref[...])
        t1 = ALPHA * x_ref[...] + mix
        x1, _, _ = _ln_fwd(t1, g1_ref[...], b1_ref[...])
        e = _mm(p_ref[...], pw_ref[...]) * _sigmoid(_mm(x1, wpg_ref[...]) + bpg_ref[...])
        t2 = ALPHA * x1 + e
        x2, _, _ = _ln_fwd(t2, g2_ref[...], b2_ref[...])
        t1_ref[...], t2_ref[...], x2_ref[...] = t1, t2, x2

    vec = _lrow(layer, D_MODEL)
    return pl.pallas_call(
        body, name="post_fwd", grid=(L // TM,),
        in_specs=[_row(D_MODEL), _row(RG_W), _row(Z_W), _row(S5_W), _row(256), _full((S5_W, S5_W)), _lrow(layer, S5_W),
                  _full((D_MODEL, D_MODEL)), vec, vec, _full((256, D_MODEL)), _full((D_MODEL, D_MODEL)), vec, vec, vec],
        out_specs=[_row(D_MODEL)] * 4, out_shape=[_S((L, D_MODEL))] * 4,
        compiler_params=_params(1))(x, hs, z, y0, p, w_glu, b_glu, w_out, g1, b1, ple_w, w_pg, b_pg, g2, b2)


def _post_bwd_a(dx2_or_target, is_top, t2, t1, p, ple_w, w_pg, b_pg, g1, b1, g2, b2, layer, token=None):
    L = t1.shape[0]
    extra, extra_specs = _after(token)

    def body(d_ref, t2_ref, t1_ref, p_ref, pw_ref, wpg_ref, bpg_ref, g1_ref, b1_ref, g2_ref, b2_ref, *rest):
        (dt1_ref, dpw_out, dwpg_out, dbpg_ref, dg1_ref, db1_ref, dg2_ref, db2_ref, loss_ref, dpw_ref,
         dwpg_ref) = rest[len(extra):]
        @pl.when(pl.program_id(0) == 0)
        def _():
            for ref in (dpw_ref, dwpg_ref, dbpg_ref, dg1_ref, db1_ref, dg2_ref, db2_ref, loss_ref):
                ref[...] = jnp.zeros_like(ref)

        g1, g2 = g1_ref[...], g2_ref[...]
        x1, xh1, rstd1 = _ln_fwd(t1_ref[...], g1, b1_ref[...])
        x2, xh2, rstd2 = _ln_fwd(t2_ref[...], g2, b2_ref[...])
        if is_top:
            err = x2 - d_ref[...]
            loss_ref[...] += _colsum(err * err)
            dx2 = err * (1.0 / D_MODEL)
        else:
            dx2 = d_ref[...]
        p = p_ref[...]
        q = _mm(p, pw_ref[...])
        gt = _sigmoid(_mm(x1, wpg_ref[...]) + bpg_ref[...])
        dg2_ref[...] += _colsum(dx2 * xh2)
        db2_ref[...] += _colsum(dx2)
        dt2 = _ln_bwd(dx2, xh2, rstd2, g2)
        dq = dt2 * gt
        dgpre = (dt2 * q) * gt * (1.0 - gt)
        dpw_ref[...] += _mm_tn(p, dq)
        dwpg_ref[...] += _mm_tn(x1, dgpre)
        dbpg_ref[...] += _colsum(dgpre)
        dx1 = ALPHA * dt2 + _mm_nt(dgpre, wpg_ref[...])
        dg1_ref[...] += _colsum(dx1 * xh1)
        db1_ref[...] += _colsum(dx1)
        dt1_ref[...] = _ln_bwd(dx1, xh1, rstd1, g1)

        @pl.when(pl.program_id(0) == L // TM - 1)
        def _():
            dpw_out[...] = dpw_ref[...].astype(WIRE)
            dwpg_out[...] = dwpg_ref[...].astype(WIRE)

    vec, lvec = _full((1, D_MODEL)), _lrow(layer, D_MODEL)
    return pl.pallas_call(
        body, name="post_bwd_a_top" if is_top else "post_bwd_a", grid=(L // TM,),
        in_specs=[_row(D_MODEL), _row(D_MODEL), _row(D_MODEL), _row(256), _full((256, D_MODEL)),
                  _full((D_MODEL, D_MODEL)), lvec, lvec, lvec, lvec, lvec] + extra_specs,
        out_specs=[_row(D_MODEL), _full((256, D_MODEL)), _full((D_MODEL, D_MODEL)), vec, vec, vec, vec, vec, vec],
        out_shape=[_S((L, D_MODEL)), _S((256, D_MODEL), WIRE), _S((D_MODEL, D_MODEL), WIRE)] + [_S((1, D_MODEL))] * 6,
        scratch_shapes=[pltpu.VMEM((256, D_MODEL), F32), pltpu.VMEM((D_MODEL, D_MODEL), F32)],
        compiler_params=_params(1))(dx2_or_target, t2, t1, p, ple_w, w_pg, b_pg, g1, b1, g2, b2, *extra)


def _post_bwd_b(dt1, m, z, hs, y0, w_out, w_glu, b_glu, layer):
    L = dt1.shape[0]

    def body(dt1_ref, m_ref, z_ref, hs_ref, y0_ref, wo_ref, wg_ref, bg_ref,
             dhs_ref, dy0_ref, dzg_ref, dwo_out, dwg_out, dbg_ref, dwo_ref, dwg_ref):
        @pl.when(pl.program_id(0) == 0)
        def _():
            for ref in (dwo_ref, dwg_ref, dbg_ref):
                ref[...] = jnp.zeros_like(ref)

        dt1b = dt1_ref[...].astype(MXU)
        dm = _mm_nt(dt1b, wo_ref[...])
        dwo_ref[...] += _mm_tn(m_ref[...], dt1b)
        d_rgy, d_s5y = dm[:, :RG_W], dm[:, RG_W:]
        rg_gate = z_ref[:, C_RGG:C_RGG + RG_W]
        s5_gate = z_ref[:, C_S5G:C_S5G + S5_W]
        sl, dsl = _silu_and_grad(rg_gate)
        dhs_ref[...] = d_rgy * sl
        dzg_ref[:, :RG_W] = d_rgy * hs_ref[...] * dsl
        y0 = y0_ref[...]
        y1 = _gelu(y0)
        gl = _sigmoid(_mm(y1, wg_ref[...]) + bg_ref[...])
        sl, dsl = _silu_and_grad(s5_gate)
        dy2 = d_s5y * sl
        dzg_ref[:, RG_W:] = d_s5y * (y1 * gl) * dsl
        dglpre = (dy2 * y1) * gl * (1.0 - gl)
        dwg_ref[...] += _mm_tn(y1, dglpre)
        dbg_ref[...] += _colsum(dglpre)
        dy1 = dy2 * gl + _mm_nt(dglpre, wg_ref[...])
        dy0_ref[...] = dy1 * _gelu_grad(y0)

        @pl.when(pl.program_id(0) == L // TM - 1)
        def _():
            dwo_out[...] = dwo_ref[...].astype(WIRE)
            dwg_out[...] = dwg_ref[...].astype(WIRE)

    return pl.pallas_call(
        body, name="post_bwd_b", grid=(L // TM,),
        in_specs=[_row(D_MODEL), _row(D_MODEL), _row(Z_W), _row(RG_W), _row(S5_W), _full((D_MODEL, D_MODEL)),
                  _full((S5_W, S5_W)), _lrow(layer, S5_W)],
        out_specs=[_row(RG_W), _row(S5_W), _row(D_MODEL), _full((D_MODEL, D_MODEL)), _full((S5_W, S5_W)), _full((1, S5_W))],
        out_shape=[_S((L, RG_W)), _S((L, S5_W)), _S((L, D_MODEL)), _S((D_MODEL, D_MODEL), WIRE), _S((S5_W, S5_W), WIRE),
                   _S((1, S5_W))],
        scratch_shapes=[pltpu.VMEM((D_MODEL, D_MODEL), F32), pltpu.VMEM((S5_W, S5_W), F32)],
        compiler_params=_params(1))(dt1, m, z, hs, y0, w_out, w_glu, b_glu)


def _adamw(parts, w, m, v):
    nl = len(parts)
    n, R, C = parts[0].shape
    tr = R
    for cand in (512, 256, 128, 64, 32, 16, 8):
        if R % cand == 0 and n * cand * C * 4 <= 4 * 1024 * 1024:
            tr = cand
            break
    nblk = R // tr

    def body(*refs):
        p_refs = refs[:nl]
        w_ref, m_ref, v_ref, g_ref, d_ref, nm_ref, nv_ref = refs[nl:]
        layer = pl.program_id(0)
        g = None
        for li, p_ref in enumerate(p_refs):
            s = p_ref[0].astype(F32)
            for k in range(1, n):
                s = s + p_ref[k].astype(F32)
            g = s if g is None else jnp.where(layer == li, s, g)
        nm = B1 * m_ref[...] + (1.0 - B1) * g
        nv = B2 * v_ref[...] + (1.0 - B2) * (g * g)
        d_ref[...] = (-LR) * ((nm / BC1) / (jnp.sqrt(nv / BC2) + EPS) + WD * w_ref[...])
        g_ref[...], nm_ref[...], nv_ref[...] = g, nm, nv

    def part_spec(li):
        return pl.BlockSpec((n, tr, C), lambda l, i: (0, jnp.where(l == li, i, jnp.where(l < li, 0, nblk - 1)), 0))

    blk = pl.BlockSpec((tr, C), lambda l, i: (l * nblk + i, 0))
    return pl.pallas_call(
        body, name="adamw", grid=(nl, nblk),
        in_specs=[part_spec(li) for li in range(nl)] + [blk, blk, blk],
        out_specs=[blk] * 4, out_shape=[_S((nl * R, C))] * 4, compiler_params=_params(2))(*parts, w, m, v)


def _adamw_natural(names, g, w, m, v, name):
    n = len(names)

    def body(*refs):
        for j in range(n):
            g_ref, w_ref, m_ref, v_ref, d_ref, nm_ref, nv_ref = (refs[k * n + j] for k in range(7))
            gj = g_ref[...]
            nm = B1 * m_ref[...] + (1.0 - B1) * gj
            nv = B2 * v_ref[...] + (1.0 - B2) * (gj * gj)
            d_ref[...] = (-LR) * ((nm / BC1) / (jnp.sqrt(nv / BC2) + EPS) + WD * w_ref[...])
            nm_ref[...], nv_ref[...] = nm, nv

    ins = [t[k] for t in (g, w, m, v) for k in names]
    outs = pl.pallas_call(body, name=name, out_shape=[_S(w[k].shape) for _ in range(3) for k in names],
                          compiler_params=pltpu.CompilerParams(vmem_limit_bytes=VMEM_LIMIT))(*ins)
    return [{k: outs[t * n + j] for j, k in enumerate(names)} for t in range(3)]


def _me():
    return lax.axis_index("x"), lax.axis_index("y"), lax.axis_index("c")


def _lin(dev):
    return 4 * dev[0] + 2 * dev[1] + dev[2]


def _blk(ref, axis, size, idx):
    nd = len(ref.shape)
    start = idx * size
    if axis == nd - 1 and size % LANE == 0:
        start = pl.multiple_of(start, LANE)
    elif axis == nd - 2 and size % 16 == 0:
        start = pl.multiple_of(start, 16)
    ix = [slice(None)] * nd
    ix[axis] = pl.ds(start, size)
    return ref.at[tuple(ix)]


def _all_gather(shards, axes, name):
    n = len(shards)
    sizes = [s.shape[a] for s, a in zip(shards, axes)]
    out_shapes = [_S(s.shape[:a] + (N_DEV * s.shape[a],) + s.shape[a + 1:], s.dtype) for s, a in zip(shards, axes)]

    def body(*refs):
        ins, outs = refs[:n], refs[n:2 * n]
        send_sems, recv_sems, local_sems = refs[2 * n:]
        x, y, c = _me()
        me, sibling = (x, y, c), (x, y, 1 - c)
        chips = [(1 - x, y), (x, 1 - y), (1 - x, 1 - y)]

        def copy(a, k, block, to, from_input=False):
            dst = _blk(outs[a], axes[a], sizes[a], _lin(block))
            return pltpu.make_async_remote_copy(
                src_ref=ins[a] if from_input else dst, dst_ref=dst, send_sem=send_sems.at[a, k],
                recv_sem=recv_sems.at[a, k], device_id=to, device_id_type=MESH)

        mine = [pltpu.make_async_copy(ins[a], _blk(outs[a], axes[a], sizes[a], _lin(me)), local_sems.at[a]) for a in range(n)]
        for cp in mine:
            cp.start()
        first = []
        for a in range(n):
            first.append(copy(a, 0, me, sibling, True))
            first += [copy(a, 1 + j, me, (*chip, c), True) for j, chip in enumerate(chips)]
        for cp in first:
            cp.start()
        passed = []
        for j, chip in enumerate(chips):
            for a in range(n):
                copy(a, 1 + j, (*chip, c), me).wait_recv()
                cp = copy(a, 4 + j, (*chip, c), sibling)
                cp.start()
                passed.append(cp)
        for a in range(n):
            copy(a, 0, sibling, me).wait_recv()
            for j, chip in enumerate(chips):
                copy(a, 4 + j, (*chip, 1 - c), me).wait_recv()
        for cp in first + passed:
            cp.wait_send()
        for cp in mine:
            cp.wait()

    return pl.pallas_call(
        body, name=name, out_shape=out_shapes, in_specs=[ANY] * n, out_specs=[ANY] * n,
        scratch_shapes=[pltpu.SemaphoreType.DMA((n, 7)), pltpu.SemaphoreType.DMA((n, 7)), pltpu.SemaphoreType.DMA((n,))],
    )(*shards)


def _exchange(groups, axes, name):
    arrays = [a for g in groups for a in g]
    where = [(o, i) for o, g in enumerate(groups) for i in range(len(g))]
    ax = [axes[o] for o, _ in where]
    n = len(arrays)
    sizes = [s.shape[a] // N_DEV for s, a in zip(arrays, ax)]
    out_shapes = []
    for g, a in zip(groups, axes):
        s = g[0].shape
        out_shapes.append(_S((N_DEV, len(g)) + s[:a] + (s[a] // N_DEV,) + s[a + 1:], g[0].dtype))

    def body(*refs):
        ins, outs = refs[:n], refs[n:n + len(groups)]
        send_sems, recv_sems, local_sems = refs[n + len(groups):]
        x, y, c = _me()
        me = (x, y, c)
        flip = lambda v, f: 1 - v if f else v
        peers = [(flip(x, k & 4), flip(y, k & 2), flip(c, k & 1)) for k in range(1, N_DEV)]

        def land(a, sender):
            o, i = where[a]
            return outs[o].at[_lin(sender), i]

        def copy(a, k, to):
            return pltpu.make_async_remote_copy(
                src_ref=_blk(ins[a], ax[a], sizes[a], _lin(to)), dst_ref=land(a, me),
                send_sem=send_sems.at[a, k], recv_sem=recv_sems.at[a, k], device_id=to, device_id_type=MESH)

        mine = [pltpu.make_async_copy(_blk(ins[a], ax[a], sizes[a], _lin(me)), land(a, me), local_sems.at[a]) for a in range(n)]
        for cp in mine:
            cp.start()
        sends = [copy(a, k, peer) for a in range(n) for k, peer in enumerate(peers)]
        for cp in sends:
            cp.start()
        for a in range(n):
            for k, peer in enumerate(peers):
                pltpu.make_async_remote_copy(
                    src_ref=land(a, peer), dst_ref=land(a, peer), send_sem=send_sems.at[a, k],
                    recv_sem=recv_sems.at[a, k], device_id=peer, device_id_type=MESH).wait_recv()
        for cp in sends:
            cp.wait_send()
        for cp in mine:
            cp.wait()

    return pl.pallas_call(
        body, name=name, out_shape=out_shapes, in_specs=[ANY] * n, out_specs=[ANY] * len(groups),
        scratch_shapes=[pltpu.SemaphoreType.DMA((n, 7)), pltpu.SemaphoreType.DMA((n, 7)), pltpu.SemaphoreType.DMA((n,))],
    )(*arrays)


HBM_SPEC = pl.BlockSpec(memory_space=pltpu.HBM)
SEM_SPEC = pl.BlockSpec(memory_space=pltpu.SEMAPHORE)
EFFECT = pltpu.SideEffectType.DATAFLOW_SIDE_EFFECTING


def _peers(x, y, c):
    flip = lambda v, f: 1 - v if f else v
    return [(flip(x, k & 4), flip(y, k & 2), flip(c, k & 1)) for k in range(1, N_DEV)]


def _land_shape(mode, s, axis):
    if mode == "gather":
        return s.shape[:axis] + (N_DEV * s.shape[axis],) + s.shape[axis + 1:]
    return (N_DEV,) + s.shape[:axis] + (s.shape[axis] // N_DEV,) + s.shape[axis + 1:]


def _src_view(mode, ref, axis, peer):
    return ref if mode == "gather" else _blk(ref, axis, ref.shape[axis] // N_DEV, peer)


def _dst_view(mode, land, axis, sender):
    return _blk(land, axis, land.shape[axis] // N_DEV, sender) if mode == "gather" else land.at[sender]


def _seven_blocks(mode, land, axis):
    if mode == "gather":
        ix = [slice(None)] * len(land.shape)
        ix[axis] = pl.ds(0, (N_DEV - 1) * (land.shape[axis] // N_DEV))
        return land.at[tuple(ix)]
    return land.at[pl.ds(0, N_DEV - 1)]


def _place_own(mode, srcs, axes, name, after=None):
    n = len(srcs)
    extra, extra_specs = _after(after)

    def body(me_ref, *refs):
        for a in range(n):
            out = refs[n + len(extra) + a]
            out[...] = refs[a][...].reshape(out.shape)

    def at_me(shape, axis):
        return lambda i, me: tuple(me[0] if d == axis else 0 for d in range(len(shape)))

    in_specs, out_specs = [], []
    for s, axis in zip(srcs, axes):
        if mode == "gather":
            in_specs.append(pl.BlockSpec(s.shape, lambda i, me, nd=len(s.shape): (0,) * nd))
            out_specs.append(pl.BlockSpec(s.shape, at_me(s.shape, axis)))
        else:
            blk = s.shape[:axis] + (s.shape[axis] // N_DEV,) + s.shape[axis + 1:]
            in_specs.append(pl.BlockSpec(blk, at_me(blk, axis)))
            out_specs.append(pl.BlockSpec((1,) + blk, at_me((1,) + blk, 0)))
    me = _lin(_me()).astype(jnp.int32).reshape(1)
    return pl.pallas_call(
        body, name=name, out_shape=[_S(_land_shape(mode, s, a), s.dtype) for s, a in zip(srcs, axes)],
        grid_spec=pltpu.PrefetchScalarGridSpec(num_scalar_prefetch=1, grid=(1,), in_specs=in_specs + extra_specs,
                                               out_specs=out_specs),
        compiler_params=_params(1))(me, *srcs, *extra)


def _push_start(mode, srcs, lands, axes, name):
    n = len(srcs)

    def body(*refs):
        src_refs, land_refs = refs[:n], refs[n:2 * n]
        send_sems, recv_sems = refs[2 * n], refs[2 * n + 1]
        token = refs[-1]
        x, y, c = _me()
        me = _lin((x, y, c))
        for a in range(n):
            for peer in _peers(x, y, c):
                pltpu.make_async_remote_copy(
                    src_ref=_src_view(mode, src_refs[a], axes[a], _lin(peer)),
                    dst_ref=_dst_view(mode, land_refs[a], axes[a], me),
                    send_sem=send_sems.at[a], recv_sem=recv_sems.at[a], device_id=peer, device_id_type=MESH).start()
        token[...] = jnp.zeros_like(token)

    hbm = lambda s: pltpu.HBM(s.shape, s.dtype)
    outs = pl.pallas_call(
        body, name=name,
        out_shape=(pltpu.SemaphoreType.DMA((n,)), pltpu.SemaphoreType.DMA((n,)), *[hbm(s) for s in srcs], *[hbm(s) for s in lands],
                   _S((SUB, LANE))),
        in_specs=[HBM_SPEC] * (2 * n),
        out_specs=(SEM_SPEC, SEM_SPEC, *[HBM_SPEC] * (2 * n), pl.BlockSpec(memory_space=pltpu.VMEM)),
        input_output_aliases={i: 2 + i for i in range(2 * n)},
        compiler_params=pltpu.CompilerParams(has_side_effects=EFFECT),
    )(*[pltpu.with_memory_space_constraint(s, pltpu.HBM) for s in list(srcs) + list(lands)])
    return outs[0], outs[1], outs[2:2 + n], outs[2 + n:2 + 2 * n], outs[-1]


def _push_wait(mode, send_sems, recv_sems, srcs, lands, axes, after, name):
    n = len(srcs)

    def body(*refs):
        land_refs = refs[n:2 * n]
        send_sems, recv_sems = refs[2 * n], refs[2 * n + 1]
        x, y, c = _me()
        for a in range(n):
            seven = _seven_blocks(mode, land_refs[a], axes[a])
            cp = pltpu.make_async_remote_copy(src_ref=seven, dst_ref=seven, send_sem=send_sems.at[a], recv_sem=recv_sems.at[a],
                                              device_id=(x, y, 1 - c), device_id_type=MESH)
            cp.wait_send()
            cp.wait_recv()

    hbm = lambda s: pltpu.HBM(s.shape, s.dtype)
    outs = pl.pallas_call(
        body, name=name, out_shape=tuple(hbm(s) for s in list(srcs) + list(lands)),
        in_specs=[HBM_SPEC] * (2 * n) + [SEM_SPEC, SEM_SPEC, ANY], out_specs=tuple([HBM_SPEC] * (2 * n)),
        input_output_aliases={i: i for i in range(2 * n)},
        compiler_params=pltpu.CompilerParams(has_side_effects=EFFECT),
    )(*srcs, *lands, send_sems, recv_sems, after)
    return outs[n:]


def _sum_parts(parts):
    n, R, C = parts.shape

    def body(p_ref, o_ref):
        g = p_ref[0]
        for k in range(1, n):
            g = g + p_ref[k]
        o_ref[...] = g

    return pl.pallas_call(body, name="sum_parts", out_shape=_S((R, C)))(parts)


def _block_diag(w, nb):
    tn, r, c = w.shape
    w = w.reshape(tn // nb, nb, r, c)
    return jnp.einsum('tarc,ab->tarbc', w, jnp.eye(nb, dtype=w.dtype)).reshape(tn // nb, nb * r, nb * c)


def _block_diag_extract(w, nb):
    t, R, C = w.shape
    w = w.reshape(t, nb, R // nb, nb, C // nb)
    return jnp.einsum('tarbc,ab->tarc', w, jnp.eye(nb, dtype=w.dtype)).reshape(t * nb, R // nb, C // nb)


SMALL = ['conv_b', 'rg_wa', 'rg_ba', 'rg_wx', 'rg_bx', 'rg_lambda', 's5_a_re', 's5_a_im', 's5_b_re', 's5_b_im',
         's5_c_re', 's5_c_im', 's5_d', 's5_log_step', 's5_b_glu', 'ln1_g', 'ln1_b', 'ple_gate_b', 'ln2_g', 'ln2_b']
WEIGHTS = ['w_in', 'conv_w', 'conv_b', 'rg_wa', 'rg_ba', 'rg_wx', 'rg_bx', 'rg_lambda', 's5_a_re', 's5_a_im', 's5_b_re',
           's5_b_im', 's5_c_re', 's5_c_im', 's5_d', 's5_log_step', 's5_w_glu', 's5_b_glu', 'w_out', 'ln1_g', 'ln1_b',
           'ple_w', 'ple_gate_w', 'ple_gate_b', 'ln2_g', 'ln2_b']
PACK_ROWS_MULT = 64


def _pack(tree):
    flat = jnp.concatenate([tree[k].reshape(-1) for k in SMALL])
    rows = -(-flat.shape[0] // (LANE * PACK_ROWS_MULT)) * PACK_ROWS_MULT
    return jnp.pad(flat, (0, rows * LANE - flat.shape[0])).reshape(rows, LANE)


def _unpack(packed, like):
    flat, out, o = packed.reshape(-1), {}, 0
    for k in SMALL:
        n = math.prod(like[k].shape)
        out[k] = flat[o:o + n].reshape(like[k].shape)
        o += n
    return out


class _NoHooks:
    token = None

    def layer_start(self, i, W, after):
        return W

    def late_weights(self, i, W, after):
        return W

    def post_done(self, i, g):
        return None

    def layer_done(self, i, g, dx):
        return None


def _local_grads(x, p, target, W, disc, hooks):
    depth = 2
    saved = []
    for i in range(depth):
        if i > 0:
            W = hooks.layer_start(i, W, x)
        w = W[i]
        z = _inproj_fwd(x, w['w_in'], hooks.token if i == 0 else None)
        hs = _rg_fwd(z, w['conv_w'], w['conv_b'], w['wa_bd'], w['wx_bd'], w['rg_ba'], w['rg_bx'], w['rg_lambda'], i)
        d = disc[i]
        y0, s_re, s_im = _s5_fwd(z, d['bb_re'], d['bb_im'], d['lb_re'], d['lb_im'], d['c_re'], d['c_im'], w['s5_d'], i)
        W = hooks.late_weights(i, W, y0)
        w = W[i]
        x2, t1, t2, m = _post_fwd(x, hs, z, y0, p[i], w['s5_w_glu'], w['s5_b_glu'], w['w_out'], w['ln1_g'], w['ln1_b'],
                                  w['ple_w'], w['ple_gate_w'], w['ple_gate_b'], w['ln2_g'], w['ln2_b'], i)
        saved.append((x, z, hs, y0, s_re, s_im, t1, t2, m))
        x = x2

    grads = [None] * depth
    dx = target
    loss = None
    token = None
    for i in reversed(range(depth)):
        w, d = W[i], disc[i]
        xin, z, hs, y0, s_re, s_im, t1, t2, m = saved[i]
        g = {}
        (dt1, g['ple_w'], g['ple_gate_w'], g['ple_gate_b'], g['ln1_g'], g['ln1_b'], g['ln2_g'], g['ln2_b'], lrow) = _post_bwd_a(
            dx, i == depth - 1, t2, t1, p[i], w['ple_w'], w['ple_gate_w'], w['ple_gate_b'], w['ln1_g'], w['ln1_b'],
            w['ln2_g'], w['ln2_b'], i, token)
        if i == depth - 1:
            loss = 0.5 / D_MODEL * jnp.sum(lrow)
        dhs, dy0, dzg, g['w_out'], g['s5_w_glu'], g['s5_b_glu'] = _post_bwd_b(dt1, m, z, hs, y0, w['w_out'], w['s5_w_glu'],
                                                                           w['s5_b_glu'], i)
        (dzu, g['bb_re'], g['bb_im'], g['lb_re'], g['lb_im'], g['c_re'], g['c_im'], g['s5_d']) = _s5_bwd(
            dy0, z, s_re, s_im, d['bb_re'], d['bb_im'], d['lb_re'], d['lb_im'], d['c_re'], d['c_im'], w['s5_d'], i,
            hooks.post_done(i, g))
        (dzx, g['conv_w'], g['conv_b'], g['wa_bd'], g['wx_bd'], g['rg_ba'], g['rg_bx'], g['rg_lambda']) = _rg_bwd(
            dhs, z, hs, w['conv_w'], w['conv_b'], w['wa_bd'], w['wx_bd'], w['rg_ba'], w['rg_bx'], w['rg_lambda'], i)
        dx, g['w_in'] = _inproj_bwd(dt1, xin, dzx, dzg, dzu, w['w_in'])
        grads[i] = g
        token = hooks.layer_done(i, g, dx)
    return loss, dx, grads


def _s5_layouts_fwd(s5_a_re, s5_a_im, s5_log_step, s5_b_re, s5_b_im, s5_c_re, s5_c_im):
    depth = s5_a_re.shape[0]
    ar, ai = s5_a_re.reshape(depth * 24, S5_P), s5_a_im.reshape(depth * 24, S5_P)
    ls = s5_log_step.reshape(depth * 24, 1)
    lr, li, cr, ci = _s5_disc_fwd(ar, ai, ls)
    col = lambda a: a.reshape(depth * S5_N, 1)
    br, bi = s5_b_re.reshape(depth * S5_N, 16), s5_b_im.reshape(depth * S5_N, 16)
    bbr, bbi = _s5_bscale_fwd(col(cr), col(ci), br, bi)
    disc = []
    for i in range(depth):
        gph = lambda a: a.reshape(depth, 24, S5_P, 16)[i]
        disc.append(dict(
            bb_re=_block_diag(jnp.swapaxes(gph(bbr), 1, 2), 8), bb_im=_block_diag(jnp.swapaxes(gph(bbi), 1, 2), 8),
            lb_re=lr.reshape(depth, 1, S5_N)[i], lb_im=li.reshape(depth, 1, S5_N)[i],
            c_re=_block_diag(jnp.swapaxes(s5_c_re[i], 1, 2), 8), c_im=_block_diag(jnp.swapaxes(s5_c_im[i], 1, 2), 8)))
    return disc, (ar, ai, ls, col(cr), col(ci), br, bi)


def _s5_layouts_bwd(grads, res):
    ar, ai, ls, cr, ci, br, bi = res
    depth = len(grads)
    stack = lambda f: jnp.stack([f(g) for g in grads])
    dbbr = stack(lambda g: jnp.swapaxes(_block_diag_extract(g['bb_re'], 8), 1, 2)).reshape(depth * S5_N, 16)
    dbbi = stack(lambda g: jnp.swapaxes(_block_diag_extract(g['bb_im'], 8), 1, 2)).reshape(depth * S5_N, 16)
    dbr, dbi, dcr, dci = _s5_bscale_bwd(cr, ci, br, bi, dbbr, dbbi)
    gp = lambda a: a.reshape(depth * 24, S5_P)
    dar, dai, dls = _s5_disc_bwd(ar, ai, ls, gp(stack(lambda g: g['lb_re'])), gp(stack(lambda g: g['lb_im'])), gp(dcr), gp(dci))
    return dict(
        s5_a_re=dar.reshape(depth, 24, S5_P), s5_a_im=dai.reshape(depth, 24, S5_P), s5_log_step=dls.reshape(depth, 24),
        s5_b_re=dbr.reshape(depth, 24, S5_P, 16), s5_b_im=dbi.reshape(depth, 24, S5_P, 16),
        s5_c_re=stack(lambda g: jnp.swapaxes(_block_diag_extract(g['c_re'], 8), 1, 2)),
        s5_c_im=stack(lambda g: jnp.swapaxes(_block_diag_extract(g['c_im'], 8), 1, 2)))


LATE = ('w_out', 'ple_w', 'ple_gate_w', 's5_w_glu')


ROWS = ('conv_b', 'rg_ba', 'rg_bx', 'rg_lambda', 's5_d', 's5_b_glu', 'ln1_g', 'ln1_b', 'ple_gate_b', 'ln2_g', 'ln2_b')


def _shared_weights(full):
    depth = full['conv_b'].shape[0]
    shared = {k: full[k].reshape(depth, 1, -1) for k in ROWS}
    shared['conv_w'] = full['conv_w']
    shared['wa_bd'] = _block_diag(full['rg_wa'].reshape(depth * 10, 64, 64), 2)
    shared['wx_bd'] = _block_diag(full['rg_wx'].reshape(depth * 10, 64, 64), 2)
    return shared


def _layer_weights(full, shared, i):
    return dict(shared, w_in=full['w_in'][i])


class _AllLocal(_NoHooks):
    def __init__(self, full):
        self.full = full

    def late_weights(self, i, W, after):
        W[i].update({k: self.full[k][i] for k in LATE})
        return W


def _full_grads(full, x, p, target, hooks=None):
    disc, res = _s5_layouts_fwd(full['s5_a_re'], full['s5_a_im'], full['s5_log_step'], full['s5_b_re'], full['s5_b_im'],
                                full['s5_c_re'], full['s5_c_im'])
    shared = _shared_weights(full)
    W = [_layer_weights(full, shared, i) for i in range(2)]
    loss, gx, grads = _local_grads(x, p, target, W, disc, hooks or _AllLocal(full))
    stack = lambda f: jnp.stack([f(g) for g in grads])
    out = _s5_layouts_bwd(grads, res)
    for k in SHARD_AXIS:
        out[k] = [g[k] for g in grads]
    out['conv_w'] = stack(lambda g: g['conv_w'])
    for k in ('conv_b', 'rg_ba', 'rg_bx', 'rg_lambda', 's5_b_glu', 'ln1_g', 'ln1_b', 'ple_gate_b', 'ln2_g', 'ln2_b'):
        out[k] = stack(lambda g: g[k][0])
    out['s5_d'] = stack(lambda g: g['s5_d'][0]).reshape(2, 24, 16)
    out['rg_wa'] = stack(lambda g: _block_diag_extract(g['wa_bd'], 2))
    out['rg_wx'] = stack(lambda g: _block_diag_extract(g['wx_bd'], 2))
    return loss, gx, out


SHARD_AXIS = {'w_in': 2, 'w_out': 1, 'ple_w': 2, 'ple_gate_w': 1, 's5_w_glu': 1}


def kernel(x, p, w_in, conv_w, conv_b, rg_wa, rg_ba, rg_wx, rg_bx, rg_lambda, s5_a_re, s5_a_im, s5_b_re, s5_b_im, s5_c_re, s5_c_im, s5_d, s5_log_step, s5_w_glu, s5_b_glu, w_out, ln1_g, ln1_b, ple_w, ple_gate_w, ple_gate_b, ln2_g, ln2_b, loss_target, m_w_in, m_conv_w, m_conv_b, m_rg_wa, m_rg_ba, m_rg_wx, m_rg_bx, m_rg_lambda, m_s5_a_re, m_s5_a_im, m_s5_b_re, m_s5_b_im, m_s5_c_re, m_s5_c_im, m_s5_d, m_s5_log_step, m_s5_w_glu, m_s5_b_glu, m_w_out, m_ln1_g, m_ln1_b, m_ple_w, m_ple_gate_w, m_ple_gate_b, m_ln2_g, m_ln2_b, v_w_in, v_conv_w, v_conv_b, v_rg_wa, v_rg_ba, v_rg_wx, v_rg_bx, v_rg_lambda, v_s5_a_re, v_s5_a_im, v_s5_b_re, v_s5_b_im, v_s5_c_re, v_s5_c_im, v_s5_d, v_s5_log_step, v_s5_w_glu, v_s5_b_glu, v_w_out, v_ln1_g, v_ln1_b, v_ple_w, v_ple_gate_w, v_ple_gate_b, v_ln2_g, v_ln2_b):
    local = dict(locals())
    w = {k: local[k] for k in WEIGHTS}
    mom = {k: local['m_' + k] for k in WEIGHTS}
    var = {k: local['v_' + k] for k in WEIGHTS}

    big = list(SHARD_AXIS)
    wire = {k: w[k].astype(WIRE) for k in big}
    first = _all_gather([wire['w_in'][0][None], conv_w[None]], [0, 0], "gather_first_weights")
    late_axes = [SHARD_AXIS[k] - 1 for k in LATE]
    pushed = {}

    def push_weights(key, srcs, axes, after):
        pushed[key] = _push_start("gather", srcs, _place_own("gather", srcs, axes, "place_weights_" + key, after=after), axes,
                                  "push_weights_" + key)
        return pushed[key][4]

    def await_weights(key, axes, after):
        s = pushed[key]
        return _push_wait("gather", s[0], s[1], s[2], s[3], axes, after, "await_weights_" + key)

    token0 = push_weights("l0", [wire[k][0] for k in LATE], late_axes, first[0])
    push_weights("l1", [wire['w_in'][1][None]] + [wire[k][1] for k in LATE], [0] + late_axes, token0)

    def push_grads(key, g, names, axes):
        srcs = [g[k] for k in names]
        pushed[key] = _push_start("scatter", srcs, _place_own("scatter", srcs, axes, "place_grads_" + key), axes,
                                  "push_grads_" + key)
        return pushed[key][4]

    def await_grads(key, axes, after):
        s = pushed[key]
        return _push_wait("scatter", s[0], s[1], s[2], s[3], axes, after, "await_grads_" + key)

    class Overlap(_NoHooks):
        token = pushed["l1"][4]

        def late_weights(self, i, W, after):
            if i == 0:
                W[0].update(zip(LATE, await_weights("l0", late_axes, after)))
            return W

        def layer_start(self, i, W, after):
            lands = await_weights("l1", [0] + late_axes, after)
            W[1].update(zip(LATE, lands[1:]), w_in=lands[0])
            return W

        def post_done(self, i, g):
            return push_grads("late0", g, LATE, late_axes) if i == 0 else None

        def layer_done(self, i, g, dx):
            return push_grads("all1", g, ['w_in'] + list(LATE), [0] + late_axes) if i == 1 else None

    hooks = Overlap()
    full = dict(w)
    full['w_in'] = [first[0], None]
    full['conv_w'] = jnp.moveaxis(first[1], 0, 2).reshape(2, 4, RG_W)

    loss, grad_x, g = _full_grads(full, x[0], p[:, 0], loss_target[0], hooks)
    loss = lax.psum(loss, ("x", "y", "c"))

    conv_blocks = jnp.moveaxis(g['conv_w'].reshape(2, 4, N_DEV, RG_W // N_DEV), 2, 0).reshape(N_DEV, 8, RG_W // N_DEV)
    packed = _pack(g)
    w_in0, conv_parts, small_parts = _exchange([[g['w_in'][0]], [conv_blocks], [packed]], [0, 0, 0], "exchange_grads")
    recv1 = dict(zip(['w_in'] + list(LATE), await_grads("all1", [0] + late_axes, grad_x)))
    recv0 = dict(zip(LATE, await_grads("late0", late_axes, grad_x)), w_in=w_in0)
    outs = {}
    for k in big + ['conv_w']:
        shard = w[k].shape
        c = shard[-1]
        two = lambda a: a.reshape(-1, c)
        parts = [conv_parts.reshape(N_DEV, -1, c)] if k == 'conv_w' else [r[k].reshape(N_DEV, -1, c) for r in (recv0, recv1)]
        outs[k] = [o.reshape(shard) for o in _adamw(parts, two(w[k]), two(mom[k]), two(var[k]))]

    rows = packed.shape[0] // N_DEV
    mine = _sum_parts(small_parts.reshape(N_DEV, rows, LANE))
    summed = _unpack(_all_gather([mine], [0], "gather_small_grads")[0], w)
    narrow = ['s5_b_re', 's5_b_im']
    for names, name in ((narrow, "adamw_s5_b"), ([k for k in SMALL if k not in narrow], "adamw_small")):
        delta, new_m, new_v = _adamw_natural(names, summed, w, mom, var, name)
        for k in names:
            outs[k] = [summed[k], delta[k], new_m[k], new_v[k]]

    res = [loss, grad_x[None]]
    for j in range(4):
        res += [outs[k][j] for k in WEIGHTS]
    return tuple(res)
```

```python
import functools
import math

import jax
import jax.numpy as jnp
from jax import lax
from jax.experimental import pallas as pl
from jax.experimental.pallas import tpu as pltpu

F32 = jnp.float32
MXU = jnp.bfloat16
WIRE = jnp.bfloat16

N_DEV = 8
D_MODEL = 1024
RG_W = 640
S5_W = 384
S5_P = 64
S5_N = 24 * S5_P
Z_W = 2 * RG_W + 2 * S5_W
C_RGG = RG_W
C_S5U = 2 * RG_W
C_S5G = 2 * RG_W + S5_W
LANE = 128
N_RG_T = RG_W // LANE
N_S5_T = S5_W // LANE
W_BLK = Z_W // N_DEV
ALPHA = (2.0 * 2) ** 0.25
LN_EPS = 1e-5
RG_C = 8.0
LR, B1, B2, EPS, WD, STEP = 0.001, 0.9, 0.999, 1e-08, 0.01, 10
BC1 = 1.0 - B1 ** STEP
BC2 = 1.0 - B2 ** STEP
RC = 256
TM = 256
VMEM_LIMIT = 56 * 1024 * 1024

MESH = pl.DeviceIdType.MESH
ANY = pl.BlockSpec(memory_space=pl.ANY)


def _params(n_grid_axes, vmem=VMEM_LIMIT):
    return pltpu.CompilerParams(dimension_semantics=("arbitrary",) * n_grid_axes, vmem_limit_bytes=vmem)


def _S(shape, dtype=F32):
    return jax.ShapeDtypeStruct(tuple(shape), dtype)


def _sigmoid(x):
    return 0.5 * jnp.tanh(0.5 * x) + 0.5


def _silu_and_grad(x):
    s = _sigmoid(x)
    return x * s, s * (1.0 + x * (1.0 - s))


_GELU_C = math.sqrt(2.0 / math.pi)


def _gelu(x):
    return 0.5 * x * (1.0 + jnp.tanh(_GELU_C * (x + 0.044715 * (x * x * x))))


def _gelu_grad(x):
    th = jnp.tanh(_GELU_C * (x + 0.044715 * (x * x * x)))
    return 0.5 * (1.0 + th) + 0.5 * x * (1.0 - th * th) * (_GELU_C * (1.0 + 3.0 * 0.044715 * (x * x)))


def _mm(a, b):
    return jnp.dot(a.astype(MXU), b.astype(MXU), preferred_element_type=F32)


def _mm_nt(a, b):
    return lax.dot_general(a.astype(MXU), b.astype(MXU), (((1,), (1,)), ((), ())), preferred_element_type=F32)


def _mm_tn(a, b):
    return lax.dot_general(a.astype(MXU), b.astype(MXU), (((0,), (0,)), ((), ())), preferred_element_type=F32)


def _ln_fwd(t, g, b):
    mu = jnp.mean(t, axis=-1, keepdims=True)
    tc = t - mu
    var = jnp.mean(tc * tc, axis=-1, keepdims=True)
    rstd = lax.rsqrt(var + LN_EPS)
    xhat = tc * rstd
    return xhat * g + b, xhat, rstd


def _ln_bwd(dy, xhat, rstd, g):
    dxh = dy * g
    m1 = jnp.mean(dxh, axis=-1, keepdims=True)
    m2 = jnp.mean(dxh * xhat, axis=-1, keepdims=True)
    return rstd * (dxh - m1 - xhat * m2)


def _colsum(a):
    return jnp.sum(a, axis=0, keepdims=True)


def _up(x, d, rows, fill):
    n = x.shape[0]
    return jnp.where(rows < n - d, pltpu.roll(x, n - d, 0), fill)


SUB = 8
TILE_STEPS = (1, 2, 4)


def _r8(width):
    return lax.broadcasted_iota(jnp.int32, (SUB, width), 0)


def _scan_real(a, u, carry, reverse=False):
    r8 = _r8(a.shape[1])
    n = a.shape[0] // SUB
    outs = [None] * n
    for k in (reversed(range(n)) if reverse else range(n)):
        A, U = a[SUB * k:SUB * k + SUB], u[SUB * k:SUB * k + SUB]
        for d in TILE_STEPS:
            m = (r8 < SUB - d) if reverse else (r8 >= d)
            sh = SUB - d if reverse else d
            U = A * jnp.where(m, pltpu.roll(U, sh, 0), 0.0) + U
            A = A * jnp.where(m, pltpu.roll(A, sh, 0), 1.0)
        h = A * carry + U
        outs[k] = h
        carry = h[0:1] if reverse else h[SUB - 1:SUB]
    return jnp.concatenate(outs, axis=0), carry


def _tile_powers(lr, li, reverse=False):
    width = lr.shape[1]
    r8 = _r8(width)
    steps = []
    pr, pi = lr, li
    er, ei = jnp.broadcast_to(lr, (SUB, width)), jnp.broadcast_to(li, (SUB, width))
    for d in TILE_STEPS:
        m = (r8 < SUB - d) if reverse else (r8 >= d)
        sh = SUB - d if reverse else d
        steps.append((sh, jnp.where(m, pr, 0.0), jnp.where(m, pi, 0.0)))
        er, ei = _cmul(er, ei, jnp.where(m, pltpu.roll(er, sh, 0), 1.0), jnp.where(m, pltpu.roll(ei, sh, 0), 0.0))
        pr, pi = _cmul(pr, pi, pr, pi)
    return steps, (er, ei)


def _scan_lti(xr, xi, carry, steps, e, reverse=False):
    er, ei = e
    kr, ki = carry
    n = xr.shape[0] // SUB
    outr, outi = [None] * n, [None] * n
    for k in (reversed(range(n)) if reverse else range(n)):
        sr, si = xr[SUB * k:SUB * k + SUB], xi[SUB * k:SUB * k + SUB]
        for sh, pr, pi in steps:
            shr, shi = pltpu.roll(sr, sh, 0), pltpu.roll(si, sh, 0)
            sr, si = sr + (pr * shr - pi * shi), si + (pr * shi + pi * shr)
        sr = sr + (er * kr - ei * ki)
        si = si + (er * ki + ei * kr)
        outr[k], outi[k] = sr, si
        kr, ki = (sr[0:1], si[0:1]) if reverse else (sr[SUB - 1:SUB], si[SUB - 1:SUB])
    return jnp.concatenate(outr, axis=0), jnp.concatenate(outi, axis=0), (kr, ki)


def _halo(ref, c, r0):
    rp = pl.multiple_of(jnp.maximum(r0 - 8, 0), 8)
    return jnp.where(c > 0, ref[pl.ds(rp, 8), :], 0.0)


def _conv_taps(xe):
    return [pltpu.roll(xe, 3, 0)[8:, :], pltpu.roll(xe, 2, 0)[8:, :], pltpu.roll(xe, 1, 0)[8:, :], xe[8:, :]]


def _rg_gates(h, wa, wx, ba, bx, sp):
    r = _sigmoid(_mm(h, wa) + ba)
    i = _sigmoid(_mm(h, wx) + bx)
    log_a = (-RG_C) * r * sp
    a = jnp.exp(log_a)
    mult = jnp.sqrt(-jnp.tanh(log_a) * (a * a + 1.0))
    return r, i, a, mult


def _softplus(y):
    return jnp.maximum(y, 0.0) + jnp.log1p(jnp.exp(-jnp.abs(y)))


def _after(token):
    return ([], []) if token is None else ([token], [ANY])


def _inproj_fwd(x, w_in, token=None):
    L = x.shape[0]

    def body(x_ref, w_ref, *rest):
        xb = x_ref[...].astype(MXU)
        for j in range(N_DEV):
            rest[-1][:, j * W_BLK:(j + 1) * W_BLK] = jnp.dot(xb, w_ref[j].astype(MXU), preferred_element_type=F32)

    extra, extra_specs = _after(token)
    return pl.pallas_call(
        body, name="inproj_fwd", grid=(L // TM,),
        in_specs=[pl.BlockSpec((TM, D_MODEL), lambda i: (i, 0)),
                  pl.BlockSpec((N_DEV, D_MODEL, W_BLK), lambda i: (0, 0, 0))] + extra_specs,
        out_specs=pl.BlockSpec((TM, Z_W), lambda i: (i, 0)),
        out_shape=_S((L, Z_W)), compiler_params=_params(1))(x, w_in, *extra)


def _inproj_bwd(dt1, x, dzx, dzg, dzu, w_in):
    L = x.shape[0]

    def body(dt1_ref, x_ref, dzx_ref, dzg_ref, dzu_ref, w_ref, dx_ref, dw_ref, acc_ref):
        @pl.when(pl.program_id(0) == 0)
        def _():
            acc_ref[...] = jnp.zeros_like(acc_ref)
        dzg = dzg_ref[...]
        dz = jnp.concatenate([dzx_ref[...], dzg[:, :RG_W], dzu_ref[...], dzg[:, RG_W:]], axis=1).astype(MXU)
        xb = x_ref[...].astype(MXU)
        dx = ALPHA * dt1_ref[...]
        for j in range(N_DEV):
            dzj = dz[:, j * W_BLK:(j + 1) * W_BLK]
            dx = dx + _mm_nt(dzj, w_ref[j])
            acc_ref[j] += _mm_tn(xb, dzj)
        dx_ref[...] = dx

        @pl.when(pl.program_id(0) == L // TM - 1)
        def _():
            dw_ref[...] = acc_ref[...].astype(WIRE)

    row = lambda w: pl.BlockSpec((TM, w), lambda i: (i, 0))
    wspec = pl.BlockSpec((N_DEV, D_MODEL, W_BLK), lambda i: (0, 0, 0))
    return pl.pallas_call(
        body, name="inproj_bwd", grid=(L // TM,),
        in_specs=[row(D_MODEL), row(D_MODEL), row(RG_W), row(D_MODEL), row(S5_W), wspec],
        out_specs=[row(D_MODEL), wspec],
        out_shape=[_S((L, D_MODEL)), _S((N_DEV, D_MODEL, W_BLK), WIRE)],
        scratch_shapes=[pltpu.VMEM((N_DEV, D_MODEL, W_BLK), F32)],
        compiler_params=_params(1))(dt1, x, dzx, dzg, dzu, w_in)


def _rg_specs(layer):
    tile = lambda rows: pl.BlockSpec((rows, LANE), lambda c: (0, c))
    ptile = lambda rows: pl.BlockSpec((None, rows, LANE), lambda c: (layer, 0, c))
    pbd = pl.BlockSpec((None, LANE, LANE), lambda c: (layer * N_RG_T + c, 0, 0))
    return tile, ptile, pbd, pl.BlockSpec((None, LANE, LANE), lambda c: (c, 0, 0))


def _rg_fwd(z, cw, cb, wa_bd, wx_bd, ba, bx, lam, layer):
    L = z.shape[0]

    def body(x_ref, cw_ref, cb_ref, wa_ref, wx_ref, ba_ref, bx_ref, lam_ref, hs_ref):
        w, b = cw_ref[...], cb_ref[...]
        wa, wx, ba_, bx_ = wa_ref[...].astype(MXU), wx_ref[...].astype(MXU), ba_ref[...], bx_ref[...]
        sp = _softplus(-lam_ref[...])

        def step(c, carry):
            r0 = pl.multiple_of(c * RC, RC)
            xe = jnp.concatenate([_halo(x_ref, c, r0), x_ref[pl.ds(r0, RC), :]], axis=0)
            t = _conv_taps(xe)
            h = t[0] * w[0:1] + t[1] * w[1:2] + t[2] * w[2:3] + t[3] * w[3:4] + b
            _, i, a, mult = _rg_gates(h, wa, wx, ba_, bx_, sp)
            hs, carry = _scan_real(a, mult * (i * h), carry)
            hs_ref[pl.ds(r0, RC), :] = hs
            return carry

        lax.fori_loop(0, L // RC, step, jnp.zeros((1, LANE), F32))

    tile, ptile, pbd, _ = _rg_specs(layer)
    return pl.pallas_call(
        body, name="rg_fwd", grid=(N_RG_T,),
        in_specs=[tile(L), ptile(4), ptile(1), pbd, pbd, ptile(1), ptile(1), ptile(1)],
        out_specs=tile(L), out_shape=_S((L, RG_W)), compiler_params=_params(1))(z, cw, cb, wa_bd, wx_bd, ba, bx, lam)


def _rg_bwd(dhs, z, hs, cw, cb, wa_bd, wx_bd, ba, bx, lam, layer):
    L = z.shape[0]

    def body(g_ref, x_ref, hs_ref, cw_ref, cb_ref, wa_ref, wx_ref, ba_ref, bx_ref, lam_ref,
             dx_ref, dcw_ref, dcb_ref, dwa_ref, dwx_ref, dba_ref, dbx_ref, dlam_ref):
        w, b = cw_ref[...], cb_ref[...]
        wa, wx, ba_, bx_ = wa_ref[...].astype(MXU), wx_ref[...].astype(MXU), ba_ref[...], bx_ref[...]
        lam = lam_ref[...]
        sp = _softplus(-lam)
        rows = lax.broadcasted_iota(jnp.int32, (RC, LANE), 0)
        for ref in (dcw_ref, dcb_ref, dwa_ref, dwx_ref, dba_ref, dbx_ref, dlam_ref):
            ref[...] = jnp.zeros_like(ref)
        nch = L // RC

        def step(k, carry):
            cin, nxt = carry
            c = nch - 1 - k
            r0 = pl.multiple_of(c * RC, RC)
            xe = jnp.concatenate([_halo(x_ref, c, r0), x_ref[pl.ds(r0, RC), :]], axis=0)
            t = _conv_taps(xe)
            h = t[0] * w[0:1] + t[1] * w[1:2] + t[2] * w[2:3] + t[3] * w[3:4] + b
            r, i, a, mult = _rg_gates(h, wa, wx, ba_, bx_, sp)
            hs_e = jnp.concatenate([_halo(hs_ref, c, r0), hs_ref[pl.ds(r0, RC), :]], axis=0)
            hs_prev = pltpu.roll(hs_e, 1, 0)[8:, :]
            g = g_ref[pl.ds(r0, RC), :]
            cc, cin_new = _scan_real(a, a * g, cin, reverse=True)
            dh = g + _up(cc, 1, rows, cin)
            ih = i * h
            dlog_a = dh * hs_prev * a - (dh * ih) * (a * a) / mult
            di = dh * mult * h
            dhin = dh * mult * i
            dr = dlog_a * ((-RG_C) * sp)
            dlam_ref[...] += _colsum(dlog_a * r)
            dra = dr * r * (1.0 - r)
            dia = di * i * (1.0 - i)
            dwa_ref[...] += _mm_tn(h, dra)
            dwx_ref[...] += _mm_tn(h, dia)
            dba_ref[...] += _colsum(dra)
            dbx_ref[...] += _colsum(dia)
            dhin = dhin + _mm_nt(dra, wa) + _mm_nt(dia, wx)
            de = jnp.concatenate([dhin, nxt], axis=0)
            n = RC + 8
            dx = (dhin * w[3:4] + pltpu.roll(de, n - 1, 0)[:RC, :] * w[2:3]
                  + pltpu.roll(de, n - 2, 0)[:RC, :] * w[1:2] + pltpu.roll(de, n - 3, 0)[:RC, :] * w[0:1])
            dx_ref[pl.ds(r0, RC), :] = dx
            for kk in range(4):
                dcw_ref[kk:kk + 1, :] += _colsum(dhin * t[kk])
            dcb_ref[...] += _colsum(dhin)
            return cin_new, dhin[0:8, :]

        lax.fori_loop(0, nch, step, (jnp.zeros((1, LANE), F32), jnp.zeros((8, LANE), F32)))
        dlam_ref[...] = dlam_ref[...] * (RG_C * _sigmoid(-lam))

    tile, ptile, pbd, bd = _rg_specs(layer)
    return pl.pallas_call(
        body, name="rg_bwd", grid=(N_RG_T,),
        in_specs=[tile(L), tile(L), tile(L), ptile(4), ptile(1), pbd, pbd, ptile(1), ptile(1), ptile(1)],
        out_specs=[tile(L), tile(4), tile(1), bd, bd, tile(1), tile(1), tile(1)],
        out_shape=[_S((L, RG_W)), _S((4, RG_W)), _S((1, RG_W)), _S((N_RG_T, LANE, LANE)), _S((N_RG_T, LANE, LANE)),
                   _S((1, RG_W)), _S((1, RG_W)), _S((1, RG_W))],
        compiler_params=_params(1))(dhs, z, hs, cw, cb, wa_bd, wx_bd, ba, bx, lam)


def _cmul(ar, ai, br, bi):
    return ar * br - ai * bi, ar * bi + ai * br


S5_TW = S5_N // N_S5_T


def _s5_specs(L):
    in_tile = pl.BlockSpec((L, LANE), lambda t: (0, t))
    st = pl.BlockSpec((L, S5_TW), lambda t: (0, t))
    bb = pl.BlockSpec((None, LANE, S5_TW), lambda t: (t, 0, 0))
    cc = pl.BlockSpec((None, S5_TW, LANE), lambda t: (t, 0, 0))
    lb = pl.BlockSpec((1, S5_TW), lambda t: (0, t))
    dv = pl.BlockSpec((1, LANE), lambda t: (0, t))
    return in_tile, st, bb, cc, lb, dv


def _layer_row_tile(layer):
    return pl.BlockSpec((None, 1, LANE), lambda t: (layer, 0, t))


def _s5_fwd(z, bb_re, bb_im, lb_re, lb_im, c_re, c_im, dvec, layer):
    L = z.shape[0]

    def body(u_ref, bbr_ref, bbi_ref, lr_ref, li_ref, cr_ref, ci_ref, d_ref, y_ref, sr_ref, si_ref):
        bbr, bbi = bbr_ref[...].astype(MXU), bbi_ref[...].astype(MXU)
        cr, ci = cr_ref[...].astype(MXU), ci_ref[...].astype(MXU)
        dv = d_ref[...]
        steps, e = _tile_powers(lr_ref[...], li_ref[...])

        def step(c, carry):
            r0 = pl.multiple_of(c * RC, RC)
            u = u_ref[pl.ds(r0, RC), :]
            ub = u.astype(MXU)
            sr = jnp.dot(ub, bbr, preferred_element_type=F32)
            si = jnp.dot(ub, bbi, preferred_element_type=F32)
            sr, si, carry = _scan_lti(sr, si, carry, steps, e)
            sr_ref[pl.ds(r0, RC), :] = sr
            si_ref[pl.ds(r0, RC), :] = si
            y_ref[pl.ds(r0, RC), :] = dv * u + (_mm(sr, cr) - _mm(si, ci))
            return carry

        zero = jnp.zeros((1, S5_TW), F32)
        lax.fori_loop(0, L // RC, step, (zero, zero))

    in_tile, st, bb, cc, lb, dv = _s5_specs(L)
    u_tile = pl.BlockSpec((L, LANE), lambda t: (0, C_S5U // LANE + t))
    return pl.pallas_call(
        body, name="s5_fwd", grid=(N_S5_T,),
        in_specs=[u_tile, bb, bb, lb, lb, cc, cc, _layer_row_tile(layer)],
        out_specs=[in_tile, st, st],
        out_shape=[_S((L, S5_W)), _S((L, S5_N)), _S((L, S5_N))],
        compiler_params=_params(1))(z, bb_re, bb_im, lb_re, lb_im, c_re, c_im, dvec)


def _s5_bwd(dy0, z, s_re, s_im, bb_re, bb_im, lb_re, lb_im, c_re, c_im, dvec, layer, token=None):
    L = z.shape[0]
    extra, extra_specs = _after(token)

    def body(dy_ref, u_ref, sr_ref, si_ref, bbr_ref, bbi_ref, lr_ref, li_ref, cr_ref, ci_ref, d_ref, *rest):
        du_ref, dbbr_ref, dbbi_ref, dlr_ref, dli_ref, dcr_ref, dci_ref, dd_ref = rest[len(extra):]
        bbr, bbi = bbr_ref[...].astype(MXU), bbi_ref[...].astype(MXU)
        cr, ci = cr_ref[...].astype(MXU), ci_ref[...].astype(MXU)
        lr, li = lr_ref[...], -li_ref[...]
        dv = d_ref[...]
        steps, e = _tile_powers(lr, li, reverse=True)
        for ref in (dbbr_ref, dbbi_ref, dlr_ref, dli_ref, dcr_ref, dci_ref, dd_ref):
            ref[...] = jnp.zeros_like(ref)
        nch = L // RC

        def step(k, carry):
            c = nch - 1 - k
            r0 = pl.multiple_of(c * RC, RC)
            dy = dy_ref[pl.ds(r0, RC), :]
            u = u_ref[pl.ds(r0, RC), :]
            dyb, ub = dy.astype(MXU), u.astype(MXU)
            sr, si = sr_ref[pl.ds(r0, RC), :], si_ref[pl.ds(r0, RC), :]
            dcr_ref[...] += _mm_tn(sr, dyb)
            dci_ref[...] -= _mm_tn(si, dyb)
            gr = _mm_nt(dyb, cr)
            gi = -_mm_nt(dyb, ci)
            gr, gi, carry = _scan_lti(gr, gi, carry, steps, e, reverse=True)
            pr_ = pltpu.roll(jnp.concatenate([_halo(sr_ref, c, r0), sr], axis=0), 1, 0)[8:, :]
            pi_ = pltpu.roll(jnp.concatenate([_halo(si_ref, c, r0), si], axis=0), 1, 0)[8:, :]
            dlr_ref[...] += _colsum(pr_ * gr + pi_ * gi)
            dli_ref[...] += _colsum(pr_ * gi - pi_ * gr)
            grb, gib = gr.astype(MXU), gi.astype(MXU)
            dbbr_ref[...] += _mm_tn(ub, grb)
            dbbi_ref[...] += _mm_tn(ub, gib)
            du_ref[pl.ds(r0, RC), :] = dv * dy + (_mm_nt(grb, bbr) + _mm_nt(gib, bbi))
            dd_ref[...] += _colsum(dy * u)
            return carry

        zero = jnp.zeros((1, S5_TW), F32)
        lax.fori_loop(0, nch, step, (zero, zero))

    in_tile, st, bb, cc, lb, dv = _s5_specs(L)
    u_tile = pl.BlockSpec((L, LANE), lambda t: (0, C_S5U // LANE + t))
    return pl.pallas_call(
        body, name="s5_bwd", grid=(N_S5_T,),
        in_specs=[in_tile, u_tile, st, st, bb, bb, lb, lb, cc, cc, _layer_row_tile(layer)] + extra_specs,
        out_specs=[in_tile, bb, bb, lb, lb, cc, cc, dv],
        out_shape=[_S((L, S5_W)), _S((N_S5_T, LANE, S5_TW)), _S((N_S5_T, LANE, S5_TW)), _S((1, S5_N)), _S((1, S5_N)),
                   _S((N_S5_T, S5_TW, LANE)), _S((N_S5_T, S5_TW, LANE)), _S((1, S5_W))],
        compiler_params=_params(1))(dy0, z, s_re, s_im, bb_re, bb_im, lb_re, lb_im, c_re, c_im, dvec, *extra)


def _disc(ar, ai, ls):
    dt = jnp.exp(ls)
    mag = jnp.exp(ar * dt)
    lr = mag * jnp.cos(ai * dt)
    li = mag * jnp.sin(ai * dt)
    den = ar * ar + ai * ai
    cr = ((lr - 1.0) * ar + li * ai) / den
    ci = (li * ar - (lr - 1.0) * ai) / den
    return lr, li, cr, ci


def _s5_disc_fwd(ar, ai, ls):
    def body(ar_ref, ai_ref, ls_ref, lr_ref, li_ref, cr_ref, ci_ref):
        lr, li, cr, ci = _disc(ar_ref[...], ai_ref[...], ls_ref[...])
        lr_ref[...], li_ref[...], cr_ref[...], ci_ref[...] = lr, li, cr, ci

    sh = _S(ar.shape)
    return pl.pallas_call(body, name="s5_disc_fwd", out_shape=[sh, sh, sh, sh])(ar, ai, ls)


def _s5_disc_bwd(ar, ai, ls, dlr, dli, dcr, dci):
    def body(ar_ref, ai_ref, ls_ref, dlr_ref, dli_ref, dcr_ref, dci_ref, dar_ref, dai_ref, dls_ref):
        _, vjp = jax.vjp(_disc, ar_ref[...], ai_ref[...], jnp.broadcast_to(ls_ref[...], ar_ref.shape))
        dar, dai, dls = vjp((dlr_ref[...], dli_ref[...], dcr_ref[...], dci_ref[...]))
        dar_ref[...], dai_ref[...] = dar, dai
        dls_ref[...] = jnp.sum(dls, axis=1, keepdims=True)

    return pl.pallas_call(body, name="s5_disc_bwd", out_shape=[_S(ar.shape), _S(ar.shape), _S(ls.shape)])(
        ar, ai, ls, dlr, dli, dcr, dci)


def _s5_bscale_fwd(cr, ci, br, bi):
    def body(cr_ref, ci_ref, br_ref, bi_ref, or_ref, oi_ref):
        or_ref[...], oi_ref[...] = _cmul(cr_ref[...], ci_ref[...], br_ref[...], bi_ref[...])

    return pl.pallas_call(body, name="s5_bscale_fwd", out_shape=[_S(br.shape), _S(br.shape)])(cr, ci, br, bi)


def _s5_bscale_bwd(cr, ci, br, bi, gr, gi):
    def body(cr_ref, ci_ref, br_ref, bi_ref, gr_ref, gi_ref, dbr_ref, dbi_ref, dcr_ref, dci_ref):
        cr_, ci_, br_, bi_, gr_, gi_ = (r[...] for r in (cr_ref, ci_ref, br_ref, bi_ref, gr_ref, gi_ref))
        dbr_ref[...] = cr_ * gr_ + ci_ * gi_
        dbi_ref[...] = cr_ * gi_ - ci_ * gr_
        dcr_ref[...] = jnp.sum(gr_ * br_ + gi_ * bi_, axis=1, keepdims=True)
        dci_ref[...] = jnp.sum(gi_ * br_ - gr_ * bi_, axis=1, keepdims=True)

    return pl.pallas_call(body, name="s5_bscale_bwd",
                          out_shape=[_S(br.shape), _S(br.shape), _S(cr.shape), _S(cr.shape)])(cr, ci, br, bi, gr, gi)


def _row(w):
    return pl.BlockSpec((TM, w), lambda i: (i, 0))


def _full(shape):
    return pl.BlockSpec(tuple(shape), lambda i: (0,) * len(shape))


def _lrow(layer, width):
    return pl.BlockSpec((None, 1, width), lambda i: (layer, 0, 0))


def _post_fwd(x, hs, z, y0, p, w_glu, b_glu, w_out, g1, b1, ple_w, w_pg, b_pg, g2, b2, layer):
    L = x.shape[0]

    def body(x_ref, hs_ref, z_ref, y0_ref, p_ref, wg_ref, bg_ref, wo_ref, g1_ref, b1_ref, pw_ref, wpg_ref, bpg_ref,
             g2_ref, b2_ref, x2_ref, xh1_ref, xh2_ref, m_ref, q_ref, gt_ref, rstd1_ref, rstd2_ref):
        rg_gate = z_ref[:, C_RGG:C_RGG + RG_W]
        s5_gate = z_ref[:, C_S5G:C_S5G + S5_W]
        rg_y = hs_ref[...] * _silu_and_grad(rg_gate)[0]
        y1 = _gelu(y0_ref[...])
        gl = _sigmoid(_mm(y1, wg_ref[...]) + bg_ref[...])
        s5_y = (y1 * gl) * _silu_and_grad(s5_gate)[0]
        m_ref[:, :RG_W] = rg_y
        m_ref[:, RG_W:] = s5_y
        mix = _mm(m_ref[...], wo_ref[...])
        t1 = ALPHA * x_ref[...] + mix
        x1, xh1, rstd1 = _ln_fwd(t1, g1_ref[...], b1_ref[...])
        q = _mm(p_ref[...], pw_ref[...])
        gt = _sigmoid(_mm(x1, wpg_ref[...]) + bpg_ref[...])
        t2 = ALPHA * x1 + q * gt
        x2, xh2, rstd2 = _ln_fwd(t2, g2_ref[...], b2_ref[...])
        x2_ref[...], xh1_ref[...], xh2_ref[...], q_ref[...], gt_ref[...] = x2, xh1, xh2, q, gt
        rstd1_ref[...], rstd2_ref[...] = rstd1, rstd2

    vec = _lrow(layer, D_MODEL)
    return pl.pallas_call(
        body, name="post_fwd", grid=(L // TM,),
        in_specs=[_row(D_MODEL), _row(RG_W), _row(Z_W), _row(S5_W), _row(256), _full((S5_W, S5_W)), _lrow(layer, S5_W),
                  _full((D_MODEL, D_MODEL)), vec, vec, _full((256, D_MODEL)), _full((D_MODEL, D_MODEL)), vec, vec, vec],
        out_specs=[_row(D_MODEL)] * 6 + [_row(1)] * 2, out_shape=[_S((L, D_MODEL))] * 6 + [_S((L, 1))] * 2,
        compiler_params=_params(1))(x, hs, z, y0, p, w_glu, b_glu, w_out, g1, b1, ple_w, w_pg, b_pg, g2, b2)


def _post_bwd_a(dx2_or_target, is_top, xh2, xh1, rstd2, rstd1, q, gt, p, w_pg, g1, b1, g2, b2, layer, token=None):
    L = xh1.shape[0]
    extra, extra_specs = _after(token)

    def body(d_ref, xh2_ref, xh1_ref, rstd2_ref, rstd1_ref, q_ref, gt_ref, p_ref, wpg_ref, g1_ref, b1_ref, g2_ref,
             b2_ref, *rest):
        (dt1_ref, dpw_out, dwpg_out, dbpg_ref, dg1_ref, db1_ref, dg2_ref, db2_ref, loss_ref, dpw_ref,
         dwpg_ref) = rest[len(extra):]
        @pl.when(pl.program_id(0) == 0)
        def _():
            for ref in (dpw_ref, dwpg_ref, dbpg_ref, dg1_ref, db1_ref, dg2_ref, db2_ref, loss_ref):
                ref[...] = jnp.zeros_like(ref)

        g1, g2 = g1_ref[...], g2_ref[...]
        xh1, xh2, rstd1, rstd2 = xh1_ref[...], xh2_ref[...], rstd1_ref[...], rstd2_ref[...]
        x1 = xh1 * g1 + b1_ref[...]
        if is_top:
            err = (xh2 * g2 + b2_ref[...]) - d_ref[...]
            loss_ref[...] += _colsum(err * err)
            dx2 = err * (1.0 / D_MODEL)
        else:
            dx2 = d_ref[...]
        p = p_ref[...]
        q, gt = q_ref[...], gt_ref[...]
        dg2_ref[...] += _colsum(dx2 * xh2)
        db2_ref[...] += _colsum(dx2)
        dt2 = _ln_bwd(dx2, xh2, rstd2, g2)
        dq = dt2 * gt
        dgpre = (dt2 * q) * gt * (1.0 - gt)
        dpw_ref[...] += _mm_tn(p, dq)
        dwpg_ref[...] += _mm_tn(x1, dgpre)
        dbpg_ref[...] += _colsum(dgpre)
        dx1 = ALPHA * dt2 + _mm_nt(dgpre, wpg_ref[...])
        dg1_ref[...] += _colsum(dx1 * xh1)
        db1_ref[...] += _colsum(dx1)
        dt1_ref[...] = _ln_bwd(dx1, xh1, rstd1, g1)

        @pl.when(pl.program_id(0) == L // TM - 1)
        def _():
            dpw_out[...] = dpw_ref[...].astype(WIRE)
            dwpg_out[...] = dwpg_ref[...].astype(WIRE)

    vec, lvec = _full((1, D_MODEL)), _lrow(layer, D_MODEL)
    return pl.pallas_call(
        body, name="post_bwd_a_top" if is_top else "post_bwd_a", grid=(L // TM,),
        in_specs=[_row(D_MODEL), _row(D_MODEL), _row(D_MODEL), _row(1), _row(1), _row(D_MODEL), _row(D_MODEL), _row(256),
                  _full((D_MODEL, D_MODEL)), lvec, lvec, lvec, lvec] + extra_specs,
        out_specs=[_row(D_MODEL), _full((256, D_MODEL)), _full((D_MODEL, D_MODEL)), vec, vec, vec, vec, vec, vec],
        out_shape=[_S((L, D_MODEL)), _S((256, D_MODEL), WIRE), _S((D_MODEL, D_MODEL), WIRE)] + [_S((1, D_MODEL))] * 6,
        scratch_shapes=[pltpu.VMEM((256, D_MODEL), F32), pltpu.VMEM((D_MODEL, D_MODEL), F32)],
        compiler_params=_params(1))(dx2_or_target, xh2, xh1, rstd2, rstd1, q, gt, p, w_pg, g1, b1, g2, b2, *extra)


def _post_bwd_b(dt1, m, z, hs, y0, w_out, w_glu, b_glu, layer):
    L = dt1.shape[0]

    def body(dt1_ref, m_ref, z_ref, hs_ref, y0_ref, wo_ref, wg_ref, bg_ref,
             dhs_ref, dy0_ref, dzg_ref, dwo_out, dwg_out, dbg_ref, dwo_ref, dwg_ref):
        @pl.when(pl.program_id(0) == 0)
        def _():
            for ref in (dwo_ref, dwg_ref, dbg_ref):
                ref[...] = jnp.zeros_like(ref)

        dt1b = dt1_ref[...].astype(MXU)
        dm = _mm_nt(dt1b, wo_ref[...])
        dwo_ref[...] += _mm_tn(m_ref[...], dt1b)
        d_rgy, d_s5y = dm[:, :RG_W], dm[:, RG_W:]
        rg_gate = z_ref[:, C_RGG:C_RGG + RG_W]
        s5_gate = z_ref[:, C_S5G:C_S5G + S5_W]
        sl, dsl = _silu_and_grad(rg_gate)
        dhs_ref[...] = d_rgy * sl
        dzg_ref[:, :RG_W] = d_rgy * hs_ref[...] * dsl
        y0 = y0_ref[...]
        y1 = _gelu(y0)
        gl = _sigmoid(_mm(y1, wg_ref[...]) + bg_ref[...])
        sl, dsl = _silu_and_grad(s5_gate)
        dy2 = d_s5y * sl
        dzg_ref[:, RG_W:] = d_s5y * (y1 * gl) * dsl
        dglpre = (dy2 * y1) * gl * (1.0 - gl)
        dwg_ref[...] += _mm_tn(y1, dglpre)
        dbg_ref[...] += _colsum(dglpre)
        dy1 = dy2 * gl + _mm_nt(dglpre, wg_ref[...])
        dy0_ref[...] = dy1 * _gelu_grad(y0)

        @pl.when(pl.program_id(0) == L // TM - 1)
        def _():
            dwo_out[...] = dwo_ref[...].astype(WIRE)
            dwg_out[...] = dwg_ref[...].astype(WIRE)

    return pl.pallas_call(
        body, name="post_bwd_b", grid=(L // TM,),
        in_specs=[_row(D_MODEL), _row(D_MODEL), _row(Z_W), _row(RG_W), _row(S5_W), _full((D_MODEL, D_MODEL)),
                  _full((S5_W, S5_W)), _lrow(layer, S5_W)],
        out_specs=[_row(RG_W), _row(S5_W), _row(D_MODEL), _full((D_MODEL, D_MODEL)), _full((S5_W, S5_W)), _full((1, S5_W))],
        out_shape=[_S((L, RG_W)), _S((L, S5_W)), _S((L, D_MODEL)), _S((D_MODEL, D_MODEL), WIRE), _S((S5_W, S5_W), WIRE),
                   _S((1, S5_W))],
        scratch_shapes=[pltpu.VMEM((D_MODEL, D_MODEL), F32), pltpu.VMEM((S5_W, S5_W), F32)],
        compiler_params=_params(1))(dt1, m, z, hs, y0, w_out, w_glu, b_glu)


def _adamw(parts, w, m, v, token=None):
    nl = len(parts)
    extra, extra_specs = _after(token)
    n, R, C = parts[0].shape
    tr = R
    for cand in (512, 256, 128, 64, 32, 16, 8):
        if R % cand == 0 and n * cand * C * 4 <= 4 * 1024 * 1024:
            tr = cand
            break
    nblk = R // tr

    def body(*refs):
        p_refs = refs[:nl]
        w_ref, m_ref, v_ref = refs[nl:nl + 3]
        g_ref, d_ref, nm_ref, nv_ref = refs[nl + 3 + len(extra):]
        layer = pl.program_id(0)
        g = None
        for li, p_ref in enumerate(p_refs):
            s = p_ref[0].astype(F32)
            for k in range(1, n):
                s = s + p_ref[k].astype(F32)
            g = s if g is None else jnp.where(layer == li, s, g)
        nm = B1 * m_ref[...] + (1.0 - B1) * g
        nv = B2 * v_ref[...] + (1.0 - B2) * (g * g)
        d_ref[...] = (-LR) * ((nm / BC1) / (jnp.sqrt(nv / BC2) + EPS) + WD * w_ref[...])
        g_ref[...], nm_ref[...], nv_ref[...] = g, nm, nv

    def part_spec(li):
        return pl.BlockSpec((n, tr, C), lambda l, i: (0, jnp.where(l == li, i, jnp.where(l < li, 0, nblk - 1)), 0))

    blk = pl.BlockSpec((tr, C), lambda l, i: (l * nblk + i, 0))
    return pl.pallas_call(
        body, name="adamw", grid=(nl, nblk),
        in_specs=[part_spec(li) for li in range(nl)] + [blk, blk, blk] + extra_specs,
        out_specs=[blk] * 4, out_shape=[_S((nl * R, C))] * 4, compiler_params=_params(2))(*parts, w, m, v, *extra)


def _adamw_natural(names, g, w, m, v, name):
    n = len(names)

    def body(*refs):
        for j in range(n):
            g_ref, w_ref, m_ref, v_ref, d_ref, nm_ref, nv_ref = (refs[k * n + j] for k in range(7))
            gj = g_ref[...]
            nm = B1 * m_ref[...] + (1.0 - B1) * gj
            nv = B2 * v_ref[...] + (1.0 - B2) * (gj * gj)
            d_ref[...] = (-LR) * ((nm / BC1) / (jnp.sqrt(nv / BC2) + EPS) + WD * w_ref[...])
            nm_ref[...], nv_ref[...] = nm, nv

    ins = [t[k] for t in (g, w, m, v) for k in names]
    outs = pl.pallas_call(body, name=name, out_shape=[_S(w[k].shape) for _ in range(3) for k in names],
                          compiler_params=pltpu.CompilerParams(vmem_limit_bytes=VMEM_LIMIT))(*ins)
    return [{k: outs[t * n + j] for j, k in enumerate(names)} for t in range(3)]


def _me():
    return lax.axis_index("x"), lax.axis_index("y"), lax.axis_index("c")


def _lin(dev):
    return 4 * dev[0] + 2 * dev[1] + dev[2]


def _blk(ref, axis, size, idx):
    nd = len(ref.shape)
    start = idx * size
    if axis == nd - 1 and size % LANE == 0:
        start = pl.multiple_of(start, LANE)
    elif axis == nd - 2 and size % 16 == 0:
        start = pl.multiple_of(start, 16)
    ix = [slice(None)] * nd
    ix[axis] = pl.ds(start, size)
    return ref.at[tuple(ix)]


def _all_gather(shards, axes, name):
    n = len(shards)
    sizes = [s.shape[a] for s, a in zip(shards, axes)]
    out_shapes = [_S(s.shape[:a] + (N_DEV * s.shape[a],) + s.shape[a + 1:], s.dtype) for s, a in zip(shards, axes)]

    def body(*refs):
        ins, outs = refs[:n], refs[n:2 * n]
        send_sems, recv_sems, local_sems = refs[2 * n:]
        x, y, c = _me()
        me, sibling = (x, y, c), (x, y, 1 - c)
        chips = [(1 - x, y), (x, 1 - y), (1 - x, 1 - y)]

        def copy(a, k, block, to, from_input=False):
            dst = _blk(outs[a], axes[a], sizes[a], _lin(block))
            return pltpu.make_async_remote_copy(
                src_ref=ins[a] if from_input else dst, dst_ref=dst, send_sem=send_sems.at[a, k],
                recv_sem=recv_sems.at[a, k], device_id=to, device_id_type=MESH)

        mine = [pltpu.make_async_copy(ins[a], _blk(outs[a], axes[a], sizes[a], _lin(me)), local_sems.at[a]) for a in range(n)]
        for cp in mine:
            cp.start()
        first = []
        for a in range(n):
            first.append(copy(a, 0, me, sibling, True))
            first += [copy(a, 1 + j, me, (*chip, c), True) for j, chip in enumerate(chips)]
        for cp in first:
            cp.start()
        passed = []
        for j, chip in enumerate(chips):
            for a in range(n):
                copy(a, 1 + j, (*chip, c), me).wait_recv()
                cp = copy(a, 4 + j, (*chip, c), sibling)
                cp.start()
                passed.append(cp)
        for a in range(n):
            copy(a, 0, sibling, me).wait_recv()
            for j, chip in enumerate(chips):
                copy(a, 4 + j, (*chip, 1 - c), me).wait_recv()
        for cp in first + passed:
            cp.wait_send()
        for cp in mine:
            cp.wait()

    return pl.pallas_call(
        body, name=name, out_shape=out_shapes, in_specs=[ANY] * n, out_specs=[ANY] * n,
        scratch_shapes=[pltpu.SemaphoreType.DMA((n, 7)), pltpu.SemaphoreType.DMA((n, 7)), pltpu.SemaphoreType.DMA((n,))],
    )(*shards)


HBM_SPEC = pl.BlockSpec(memory_space=pltpu.HBM)
SEM_SPEC = pl.BlockSpec(memory_space=pltpu.SEMAPHORE)
EFFECT = pltpu.SideEffectType.DATAFLOW_SIDE_EFFECTING


def _peers(x, y, c):
    flip = lambda v, f: 1 - v if f else v
    return [(flip(x, k & 4), flip(y, k & 2), flip(c, k & 1)) for k in range(1, N_DEV)]


def _land_shape(mode, s, axis):
    if mode == "gather":
        return s.shape[:axis] + (N_DEV * s.shape[axis],) + s.shape[axis + 1:]
    return (N_DEV,) + s.shape[:axis] + (s.shape[axis] // N_DEV,) + s.shape[axis + 1:]


def _src_view(mode, ref, axis, peer):
    return ref if mode == "gather" else _blk(ref, axis, ref.shape[axis] // N_DEV, peer)


def _dst_view(mode, land, axis, sender):
    return _blk(land, axis, land.shape[axis] // N_DEV, sender) if mode == "gather" else land.at[sender]


def _seven_blocks(mode, land, axis):
    if mode == "gather":
        ix = [slice(None)] * len(land.shape)
        ix[axis] = pl.ds(0, (N_DEV - 1) * (land.shape[axis] // N_DEV))
        return land.at[tuple(ix)]
    return land.at[pl.ds(0, N_DEV - 1)]


def _place_own(mode, srcs, axes, name, after=None):
    n = len(srcs)
    extra, extra_specs = _after(after)

    def body(me_ref, *refs):
        for a in range(n):
            out = refs[n + len(extra) + a]
            out[...] = refs[a][...].reshape(out.shape)

    def at_me(shape, axis):
        return lambda i, me: tuple(me[0] if d == axis else 0 for d in range(len(shape)))

    in_specs, out_specs = [], []
    for s, axis in zip(srcs, axes):
        if mode == "gather":
            in_specs.append(pl.BlockSpec(s.shape, lambda i, me, nd=len(s.shape): (0,) * nd))
            out_specs.append(pl.BlockSpec(s.shape, at_me(s.shape, axis)))
        else:
            blk = s.shape[:axis] + (s.shape[axis] // N_DEV,) + s.shape[axis + 1:]
            in_specs.append(pl.BlockSpec(blk, at_me(blk, axis)))
            out_specs.append(pl.BlockSpec((1,) + blk, at_me((1,) + blk, 0)))
    me = _lin(_me()).astype(jnp.int32).reshape(1)
    return pl.pallas_call(
        body, name=name, out_shape=[_S(_land_shape(mode, s, a), s.dtype) for s, a in zip(srcs, axes)],
        grid_spec=pltpu.PrefetchScalarGridSpec(num_scalar_prefetch=1, grid=(1,), in_specs=in_specs + extra_specs,
                                               out_specs=out_specs),
        compiler_params=_params(1))(me, *srcs, *extra)


def _push_start(mode, srcs, lands, axes, name):
    n = len(srcs)

    def body(*refs):
        src_refs, land_refs = refs[:n], refs[n:2 * n]
        send_sems, recv_sems = refs[2 * n], refs[2 * n + 1]
        token = refs[-1]
        x, y, c = _me()
        me = _lin((x, y, c))
        for a in range(n):
            for peer in _peers(x, y, c):
                pltpu.make_async_remote_copy(
                    src_ref=_src_view(mode, src_refs[a], axes[a], _lin(peer)),
                    dst_ref=_dst_view(mode, land_refs[a], axes[a], me),
                    send_sem=send_sems.at[a], recv_sem=recv_sems.at[a], device_id=peer, device_id_type=MESH).start()
        token[...] = jnp.zeros_like(token)

    hbm = lambda s: pltpu.HBM(s.shape, s.dtype)
    outs = pl.pallas_call(
        body, name=name,
        out_shape=(pltpu.SemaphoreType.DMA((n,)), pltpu.SemaphoreType.DMA((n,)), *[hbm(s) for s in srcs], *[hbm(s) for s in lands],
                   _S((SUB, LANE))),
        in_specs=[HBM_SPEC] * (2 * n),
        out_specs=(SEM_SPEC, SEM_SPEC, *[HBM_SPEC] * (2 * n), pl.BlockSpec(memory_space=pltpu.VMEM)),
        input_output_aliases={i: 2 + i for i in range(2 * n)},
        compiler_params=pltpu.CompilerParams(has_side_effects=EFFECT),
    )(*[pltpu.with_memory_space_constraint(s, pltpu.HBM) for s in list(srcs) + list(lands)])
    return outs[0], outs[1], outs[2:2 + n], outs[2 + n:2 + 2 * n], outs[-1]


def _push_wait(mode, send_sems, recv_sems, srcs, lands, axes, after, name):
    n = len(srcs)

    def body(*refs):
        land_refs = refs[n:2 * n]
        send_sems, recv_sems = refs[2 * n], refs[2 * n + 1]
        x, y, c = _me()
        for a in range(n):
            seven = _seven_blocks(mode, land_refs[a], axes[a])
            cp = pltpu.make_async_remote_copy(src_ref=seven, dst_ref=seven, send_sem=send_sems.at[a], recv_sem=recv_sems.at[a],
                                              device_id=(x, y, 1 - c), device_id_type=MESH)
            cp.wait_send()
            cp.wait_recv()

    hbm = lambda s: pltpu.HBM(s.shape, s.dtype)
    outs = pl.pallas_call(
        body, name=name, out_shape=tuple(hbm(s) for s in list(srcs) + list(lands)),
        in_specs=[HBM_SPEC] * (2 * n) + [SEM_SPEC, SEM_SPEC, ANY], out_specs=tuple([HBM_SPEC] * (2 * n)),
        input_output_aliases={i: i for i in range(2 * n)},
        compiler_params=pltpu.CompilerParams(has_side_effects=EFFECT),
    )(*srcs, *lands, send_sems, recv_sems, after)
    return outs[n:]


def _sum_parts(parts):
    n, R, C = parts.shape

    def body(p_ref, o_ref):
        g = p_ref[0]
        for k in range(1, n):
            g = g + p_ref[k]
        o_ref[...] = g

    return pl.pallas_call(body, name="sum_parts", out_shape=_S((R, C)))(parts)


def _block_diag(w, nb):
    tn, r, c = w.shape
    w = w.reshape(tn // nb, nb, r, c)
    return jnp.einsum('tarc,ab->tarbc', w, jnp.eye(nb, dtype=w.dtype)).reshape(tn // nb, nb * r, nb * c)


def _block_diag_extract(w, nb):
    t, R, C = w.shape
    w = w.reshape(t, nb, R // nb, nb, C // nb)
    return jnp.einsum('tarbc,ab->tarc', w, jnp.eye(nb, dtype=w.dtype)).reshape(t * nb, R // nb, C // nb)


SMALL = ['conv_b', 'rg_wa', 'rg_ba', 'rg_wx', 'rg_bx', 'rg_lambda', 's5_a_re', 's5_a_im', 's5_b_re', 's5_b_im',
         's5_c_re', 's5_c_im', 's5_d', 's5_log_step', 's5_b_glu', 'ln1_g', 'ln1_b', 'ple_gate_b', 'ln2_g', 'ln2_b']
WEIGHTS = ['w_in', 'conv_w', 'conv_b', 'rg_wa', 'rg_ba', 'rg_wx', 'rg_bx', 'rg_lambda', 's5_a_re', 's5_a_im', 's5_b_re',
           's5_b_im', 's5_c_re', 's5_c_im', 's5_d', 's5_log_step', 's5_w_glu', 's5_b_glu', 'w_out', 'ln1_g', 'ln1_b',
           'ple_w', 'ple_gate_w', 'ple_gate_b', 'ln2_g', 'ln2_b']
PACK_ROWS_MULT = 64


def _pack(tree):
    flat = jnp.concatenate([tree[k].reshape(-1) for k in SMALL])
    rows = -(-flat.shape[0] // (LANE * PACK_ROWS_MULT)) * PACK_ROWS_MULT
    return jnp.pad(flat, (0, rows * LANE - flat.shape[0])).reshape(rows, LANE)


def _unpack(packed, like):
    flat, out, o = packed.reshape(-1), {}, 0
    for k in SMALL:
        n = math.prod(like[k].shape)
        out[k] = flat[o:o + n].reshape(like[k].shape)
        o += n
    return out


class _NoHooks:
    token = None

    def layer_start(self, i, W, after):
        return W

    def late_weights(self, i, W, after):
        return W

    def post_done(self, i, g):
        return None

    def layer_done(self, i, g, dx):
        return None


def _local_grads(x, p, target, W, disc, hooks):
    depth = 2
    saved = []
    for i in range(depth):
        if i > 0:
            W = hooks.layer_start(i, W, x)
        w = W[i]
        z = _inproj_fwd(x, w['w_in'], hooks.token if i == 0 else None)
        hs = _rg_fwd(z, w['conv_w'], w['conv_b'], w['wa_bd'], w['wx_bd'], w['rg_ba'], w['rg_bx'], w['rg_lambda'], i)
        d = disc[i]
        y0, s_re, s_im = _s5_fwd(z, d['bb_re'], d['bb_im'], d['lb_re'], d['lb_im'], d['c_re'], d['c_im'], w['s5_d'], i)
        W = hooks.late_weights(i, W, y0)
        w = W[i]
        x2, *norms = _post_fwd(x, hs, z, y0, p[i], w['s5_w_glu'], w['s5_b_glu'], w['w_out'], w['ln1_g'], w['ln1_b'],
                               w['ple_w'], w['ple_gate_w'], w['ple_gate_b'], w['ln2_g'], w['ln2_b'], i)
        saved.append((x, z, hs, y0, s_re, s_im, norms))
        x = x2

    grads = [None] * depth
    dx = target
    loss = None
    token = None
    for i in reversed(range(depth)):
        w, d = W[i], disc[i]
        xin, z, hs, y0, s_re, s_im, (xh1, xh2, m, q, gt, rstd1, rstd2) = saved[i]
        g = {}
        (dt1, g['ple_w'], g['ple_gate_w'], g['ple_gate_b'], g['ln1_g'], g['ln1_b'], g['ln2_g'], g['ln2_b'], lrow) = _post_bwd_a(
            dx, i == depth - 1, xh2, xh1, rstd2, rstd1, q, gt, p[i], w['ple_gate_w'], w['ln1_g'], w['ln1_b'],
            w['ln2_g'], w['ln2_b'], i, token)
        if i == depth - 1:
            loss = 0.5 / D_MODEL * jnp.sum(lrow)
        dhs, dy0, dzg, g['w_out'], g['s5_w_glu'], g['s5_b_glu'] = _post_bwd_b(dt1, m, z, hs, y0, w['w_out'], w['s5_w_glu'],
                                                                           w['s5_b_glu'], i)
        (dzu, g['bb_re'], g['bb_im'], g['lb_re'], g['lb_im'], g['c_re'], g['c_im'], g['s5_d']) = _s5_bwd(
            dy0, z, s_re, s_im, d['bb_re'], d['bb_im'], d['lb_re'], d['lb_im'], d['c_re'], d['c_im'], w['s5_d'], i,
            hooks.post_done(i, g))
        (dzx, g['conv_w'], g['conv_b'], g['wa_bd'], g['wx_bd'], g['rg_ba'], g['rg_bx'], g['rg_lambda']) = _rg_bwd(
            dhs, z, hs, w['conv_w'], w['conv_b'], w['wa_bd'], w['wx_bd'], w['rg_ba'], w['rg_bx'], w['rg_lambda'], i)
        dx, g['w_in'] = _inproj_bwd(dt1, xin, dzx, dzg, dzu, w['w_in'])
        grads[i] = g
        token = hooks.layer_done(i, g, dx)
    return loss, dx, grads


def _s5_layouts_fwd(s5_a_re, s5_a_im, s5_log_step, s5_b_re, s5_b_im, s5_c_re, s5_c_im):
    depth = s5_a_re.shape[0]
    ar, ai = s5_a_re.reshape(depth * 24, S5_P), s5_a_im.reshape(depth * 24, S5_P)
    ls = s5_log_step.reshape(depth * 24, 1)
    lr, li, cr, ci = _s5_disc_fwd(ar, ai, ls)
    col = lambda a: a.reshape(depth * S5_N, 1)
    br, bi = s5_b_re.reshape(depth * S5_N, 16), s5_b_im.reshape(depth * S5_N, 16)
    bbr, bbi = _s5_bscale_fwd(col(cr), col(ci), br, bi)
    disc = []
    for i in range(depth):
        gph = lambda a: a.reshape(depth, 24, S5_P, 16)[i]
        disc.append(dict(
            bb_re=_block_diag(jnp.swapaxes(gph(bbr), 1, 2), 8), bb_im=_block_diag(jnp.swapaxes(gph(bbi), 1, 2), 8),
            lb_re=lr.reshape(depth, 1, S5_N)[i], lb_im=li.reshape(depth, 1, S5_N)[i],
            c_re=_block_diag(jnp.swapaxes(s5_c_re[i], 1, 2), 8), c_im=_block_diag(jnp.swapaxes(s5_c_im[i], 1, 2), 8)))
    return disc, (ar, ai, ls, col(cr), col(ci), br, bi)


def _s5_layouts_bwd(grads, res):
    ar, ai, ls, cr, ci, br, bi = res
    depth = len(grads)
    stack = lambda f: jnp.stack([f(g) for g in grads])
    dbbr = stack(lambda g: jnp.swapaxes(_block_diag_extract(g['bb_re'], 8), 1, 2)).reshape(depth * S5_N, 16)
    dbbi = stack(lambda g: jnp.swapaxes(_block_diag_extract(g['bb_im'], 8), 1, 2)).reshape(depth * S5_N, 16)
    dbr, dbi, dcr, dci = _s5_bscale_bwd(cr, ci, br, bi, dbbr, dbbi)
    gp = lambda a: a.reshape(depth * 24, S5_P)
    dar, dai, dls = _s5_disc_bwd(ar, ai, ls, gp(stack(lambda g: g['lb_re'])), gp(stack(lambda g: g['lb_im'])), gp(dcr), gp(dci))
    return dict(
        s5_a_re=dar.reshape(depth, 24, S5_P), s5_a_im=dai.reshape(depth, 24, S5_P), s5_log_step=dls.reshape(depth, 24),
        s5_b_re=dbr.reshape(depth, 24, S5_P, 16), s5_b_im=dbi.reshape(depth, 24, S5_P, 16),
        s5_c_re=stack(lambda g: jnp.swapaxes(_block_diag_extract(g['c_re'], 8), 1, 2)),
        s5_c_im=stack(lambda g: jnp.swapaxes(_block_diag_extract(g['c_im'], 8), 1, 2)))


LATE = ('w_out', 'ple_w', 'ple_gate_w', 's5_w_glu')


ROWS = ('conv_b', 'rg_ba', 'rg_bx', 'rg_lambda', 's5_d', 's5_b_glu', 'ln1_g', 'ln1_b', 'ple_gate_b', 'ln2_g', 'ln2_b')


def _shared_weights(full):
    depth = full['conv_b'].shape[0]
    shared = {k: full[k].reshape(depth, 1, -1) for k in ROWS}
    shared['conv_w'] = full['conv_w']
    shared['wa_bd'] = _block_diag(full['rg_wa'].reshape(depth * 10, 64, 64), 2)
    shared['wx_bd'] = _block_diag(full['rg_wx'].reshape(depth * 10, 64, 64), 2)
    return shared


def _layer_weights(full, shared, i):
    return dict(shared, w_in=full['w_in'][i])


class _AllLocal(_NoHooks):
    def __init__(self, full):
        self.full = full

    def late_weights(self, i, W, after):
        W[i].update({k: self.full[k][i] for k in LATE})
        return W


def _full_grads(full, x, p, target, hooks=None):
    disc, res = _s5_layouts_fwd(full['s5_a_re'], full['s5_a_im'], full['s5_log_step'], full['s5_b_re'], full['s5_b_im'],
                                full['s5_c_re'], full['s5_c_im'])
    shared = _shared_weights(full)
    W = [_layer_weights(full, shared, i) for i in range(2)]
    loss, gx, grads = _local_grads(x, p, target, W, disc, hooks or _AllLocal(full))
    stack = lambda f: jnp.stack([f(g) for g in grads])
    out = _s5_layouts_bwd(grads, res)
    for k in SHARD_AXIS:
        out[k] = [g[k] for g in grads]
    out['conv_w'] = stack(lambda g: g['conv_w'])
    for k in ('conv_b', 'rg_ba', 'rg_bx', 'rg_lambda', 's5_b_glu', 'ln1_g', 'ln1_b', 'ple_gate_b', 'ln2_g', 'ln2_b'):
        out[k] = stack(lambda g: g[k][0])
    out['s5_d'] = stack(lambda g: g['s5_d'][0]).reshape(2, 24, 16)
    out['rg_wa'] = stack(lambda g: _block_diag_extract(g['wa_bd'], 2))
    out['rg_wx'] = stack(lambda g: _block_diag_extract(g['wx_bd'], 2))
    return loss, gx, out


SHARD_AXIS = {'w_in': 2, 'w_out': 1, 'ple_w': 2, 'ple_gate_w': 1, 's5_w_glu': 1}


def kernel(x, p, w_in, conv_w, conv_b, rg_wa, rg_ba, rg_wx, rg_bx, rg_lambda, s5_a_re, s5_a_im, s5_b_re, s5_b_im, s5_c_re, s5_c_im, s5_d, s5_log_step, s5_w_glu, s5_b_glu, w_out, ln1_g, ln1_b, ple_w, ple_gate_w, ple_gate_b, ln2_g, ln2_b, loss_target, m_w_in, m_conv_w, m_conv_b, m_rg_wa, m_rg_ba, m_rg_wx, m_rg_bx, m_rg_lambda, m_s5_a_re, m_s5_a_im, m_s5_b_re, m_s5_b_im, m_s5_c_re, m_s5_c_im, m_s5_d, m_s5_log_step, m_s5_w_glu, m_s5_b_glu, m_w_out, m_ln1_g, m_ln1_b, m_ple_w, m_ple_gate_w, m_ple_gate_b, m_ln2_g, m_ln2_b, v_w_in, v_conv_w, v_conv_b, v_rg_wa, v_rg_ba, v_rg_wx, v_rg_bx, v_rg_lambda, v_s5_a_re, v_s5_a_im, v_s5_b_re, v_s5_b_im, v_s5_c_re, v_s5_c_im, v_s5_d, v_s5_log_step, v_s5_w_glu, v_s5_b_glu, v_w_out, v_ln1_g, v_ln1_b, v_ple_w, v_ple_gate_w, v_ple_gate_b, v_ln2_g, v_ln2_b):
    local = dict(locals())
    w = {k: local[k] for k in WEIGHTS}
    mom = {k: local['m_' + k] for k in WEIGHTS}
    var = {k: local['v_' + k] for k in WEIGHTS}

    big = list(SHARD_AXIS)
    wire = {k: w[k].astype(WIRE) for k in big}
    first = _all_gather([wire['w_in'][0][None], conv_w[None]], [0, 0], "gather_first_weights")
    late_axes = [SHARD_AXIS[k] - 1 for k in LATE]
    pushed = {}

    def push_weights(key, srcs, axes, after):
        pushed[key] = _push_start("gather", srcs, _place_own("gather", srcs, axes, "place_weights_" + key, after=after), axes,
                                  "push_weights_" + key)
        return pushed[key][4]

    def await_weights(key, axes, after):
        s = pushed[key]
        return _push_wait("gather", s[0], s[1], s[2], s[3], axes, after, "await_weights_" + key)

    token0 = push_weights("l0", [wire[k][0] for k in LATE], late_axes, first[0])
    push_weights("l1", [wire['w_in'][1][None]] + [wire[k][1] for k in LATE], [0] + late_axes, token0)

    def push_grads(key, g, names, axes):
        srcs = [g[k] for k in names]
        pushed[key] = _push_start("scatter", srcs, _place_own("scatter", srcs, axes, "place_grads_" + key), axes,
                                  "push_grads_" + key)
        return pushed[key][4]

    def await_grads(key, axes, after):
        s = pushed[key]
        return _push_wait("scatter", s[0], s[1], s[2], s[3], axes, after, "await_grads_" + key)

    class Overlap(_NoHooks):
        token = pushed["l1"][4]

        def late_weights(self, i, W, after):
            if i == 0:
                W[0].update(zip(LATE, await_weights("l0", late_axes, after)))
            return W

        def layer_start(self, i, W, after):
            lands = await_weights("l1", [0] + late_axes, after)
            W[1].update(zip(LATE, lands[1:]), w_in=lands[0])
            return W

        def post_done(self, i, g):
            return push_grads("late0", g, LATE, late_axes) if i == 0 else None

        def layer_done(self, i, g, dx):
            return push_grads("all1", g, ['w_in'] + list(LATE), [0] + late_axes) if i == 1 else None

    hooks = Overlap()
    full = dict(w)
    full['w_in'] = [first[0], None]
    full['conv_w'] = jnp.moveaxis(first[1], 0, 2).reshape(2, 4, RG_W)

    loss, grad_x, g = _full_grads(full, x[0], p[:, 0], loss_target[0], hooks)
    loss = lax.psum(loss, ("x", "y", "c"))

    conv_blocks = jnp.moveaxis(g['conv_w'].reshape(2, 4, N_DEV, RG_W // N_DEV), 2, 0).reshape(N_DEV, 8, RG_W // N_DEV)
    packed = _pack(g)
    token = push_grads("final", dict(w_in=g['w_in'][0], conv_w=conv_blocks, small=packed), ['w_in', 'conv_w', 'small'],
                       [0, 0, 0])
    recv1 = dict(zip(['w_in'] + list(LATE), await_grads("all1", [0] + late_axes, grad_x)))
    recv0 = dict(zip(LATE, await_grads("late0", late_axes, grad_x)))
    outs = {}

    def update(k, parts, token=None):
        shard = w[k].shape
        c = shard[-1]
        two = lambda a: a.reshape(-1, c)
        res = _adamw([r.reshape(N_DEV, -1, c) for r in parts], two(w[k]), two(mom[k]), two(var[k]), token)
        outs[k] = [o.reshape(shard) for o in res]

    for k in LATE:
        update(k, [recv0[k], recv1[k]], token)
        token = None
    w_in0, conv_parts, small_parts = await_grads("final", [0, 0, 0], outs[LATE[-1]][1])
    update('w_in', [w_in0, recv1['w_in']])
    update('conv_w', [conv_parts])

    rows = packed.shape[0] // N_DEV
    mine = _sum_parts(small_parts.reshape(N_DEV, rows, LANE))
    summed = _unpack(_all_gather([mine], [0], "gather_small_grads")[0], w)
    narrow = ['s5_b_re', 's5_b_im']
    for names, name in ((narrow, "adamw_s5_b"), ([k for k in SMALL if k not in narrow], "adamw_small")):
        delta, new_m, new_v = _adamw_natural(names, summed, w, mom, var, name)
        for k in names:
            outs[k] = [summed[k], delta[k], new_m[k], new_v[k]]

    res = [loss, grad_x[None]]
    for j in range(4):
        res += [outs[k][j] for k in WEIGHTS]
    return tuple(res)
```

```python
import functools
import math

import jax
import jax.numpy as jnp
from jax import lax
from jax.experimental import pallas as pl
from jax.experimental.pallas import tpu as pltpu

F32 = jnp.float32
MXU = jnp.bfloat16
WIRE = jnp.bfloat16

N_DEV = 8
D_MODEL = 1024
RG_W = 640
S5_W = 384
S5_P = 64
S5_N = 24 * S5_P
Z_W = 2 * RG_W + 2 * S5_W
C_RGG = RG_W
C_S5U = 2 * RG_W
C_S5G = 2 * RG_W + S5_W
LANE = 128
N_RG_T = RG_W // LANE
N_S5_T = S5_W // LANE
W_BLK = Z_W // N_DEV
ALPHA = (2.0 * 2) ** 0.25
LN_EPS = 1e-5
RG_C = 8.0
LR, B1, B2, EPS, WD, STEP = 0.001, 0.9, 0.999, 1e-08, 0.01, 10
BC1 = 1.0 - B1 ** STEP
BC2 = 1.0 - B2 ** STEP
RC = 256
TM = 256
VMEM_LIMIT = 56 * 1024 * 1024

MESH = pl.DeviceIdType.MESH
ANY = pl.BlockSpec(memory_space=pl.ANY)


def _params(n_grid_axes, vmem=VMEM_LIMIT):
    return pltpu.CompilerParams(dimension_semantics=("arbitrary",) * n_grid_axes, vmem_limit_bytes=vmem)


def _S(shape, dtype=F32):
    return jax.ShapeDtypeStruct(tuple(shape), dtype)


def _sigmoid(x):
    return 0.5 * jnp.tanh(0.5 * x) + 0.5


def _silu_and_grad(x):
    s = _sigmoid(x)
    return x * s, s * (1.0 + x * (1.0 - s))


_GELU_C = math.sqrt(2.0 / math.pi)


def _gelu(x):
    return 0.5 * x * (1.0 + jnp.tanh(_GELU_C * (x + 0.044715 * (x * x * x))))


def _gelu_grad(x):
    th = jnp.tanh(_GELU_C * (x + 0.044715 * (x * x * x)))
    return 0.5 * (1.0 + th) + 0.5 * x * (1.0 - th * th) * (_GELU_C * (1.0 + 3.0 * 0.044715 * (x * x)))


def _mm(a, b):
    return jnp.dot(a.astype(MXU), b.astype(MXU), preferred_element_type=F32)


def _mm_nt(a, b):
    return lax.dot_general(a.astype(MXU), b.astype(MXU), (((1,), (1,)), ((), ())), preferred_element_type=F32)


def _mm_tn(a, b):
    return lax.dot_general(a.astype(MXU), b.astype(MXU), (((0,), (0,)), ((), ())), preferred_element_type=F32)


def _ln_fwd(t, g, b):
    mu = jnp.mean(t, axis=-1, keepdims=True)
    tc = t - mu
    var = jnp.mean(tc * tc, axis=-1, keepdims=True)
    rstd = lax.rsqrt(var + LN_EPS)
    xhat = tc * rstd
    return xhat * g + b, xhat, rstd


def _ln_bwd(dy, xhat, rstd, g):
    dxh = dy * g
    m1 = jnp.mean(dxh, axis=-1, keepdims=True)
    m2 = jnp.mean(dxh * xhat, axis=-1, keepdims=True)
    return rstd * (dxh - m1 - xhat * m2)


def _colsum(a):
    return jnp.sum(a, axis=0, keepdims=True)


def _up(x, d, rows, fill):
    n = x.shape[0]
    return jnp.where(rows < n - d, pltpu.roll(x, n - d, 0), fill)


SUB = 8
TILE_STEPS = (1, 2, 4)


def _r8(width):
    return lax.broadcasted_iota(jnp.int32, (SUB, width), 0)


def _scan_real(a, u, carry, reverse=False):
    r8 = _r8(a.shape[1])
    n = a.shape[0] // SUB
    outs = [None] * n
    for k in (reversed(range(n)) if reverse else range(n)):
        A, U = a[SUB * k:SUB * k + SUB], u[SUB * k:SUB * k + SUB]
        for d in TILE_STEPS:
            m = (r8 < SUB - d) if reverse else (r8 >= d)
            sh = SUB - d if reverse else d
            U = A * jnp.where(m, pltpu.roll(U, sh, 0), 0.0) + U
            A = A * jnp.where(m, pltpu.roll(A, sh, 0), 1.0)
        h = A * carry + U
        outs[k] = h
        carry = h[0:1] if reverse else h[SUB - 1:SUB]
    return jnp.concatenate(outs, axis=0), carry


def _tile_powers(lr, li, reverse=False):
    width = lr.shape[1]
    r8 = _r8(width)
    steps = []
    pr, pi = lr, li
    er, ei = jnp.broadcast_to(lr, (SUB, width)), jnp.broadcast_to(li, (SUB, width))
    for d in TILE_STEPS:
        m = (r8 < SUB - d) if reverse else (r8 >= d)
        sh = SUB - d if reverse else d
        steps.append((sh, jnp.where(m, pr, 0.0), jnp.where(m, pi, 0.0)))
        er, ei = _cmul(er, ei, jnp.where(m, pltpu.roll(er, sh, 0), 1.0), jnp.where(m, pltpu.roll(ei, sh, 0), 0.0))
        pr, pi = _cmul(pr, pi, pr, pi)
    return steps, (er, ei)


def _scan_lti(xr, xi, carry, steps, e, reverse=False):
    er, ei = e
    kr, ki = carry
    n = xr.shape[0] // SUB
    outr, outi = [None] * n, [None] * n
    for k in (reversed(range(n)) if reverse else range(n)):
        sr, si = xr[SUB * k:SUB * k + SUB], xi[SUB * k:SUB * k + SUB]
        for sh, pr, pi in steps:
            shr, shi = pltpu.roll(sr, sh, 0), pltpu.roll(si, sh, 0)
            sr, si = sr + (pr * shr - pi * shi), si + (pr * shi + pi * shr)
        sr = sr + (er * kr - ei * ki)
        si = si + (er * ki + ei * kr)
        outr[k], outi[k] = sr, si
        kr, ki = (sr[0:1], si[0:1]) if reverse else (sr[SUB - 1:SUB], si[SUB - 1:SUB])
    return jnp.concatenate(outr, axis=0), jnp.concatenate(outi, axis=0), (kr, ki)


def _halo(ref, c, r0):
    rp = pl.multiple_of(jnp.maximum(r0 - 8, 0), 8)
    return jnp.where(c > 0, ref[pl.ds(rp, 8), :], 0.0)


def _conv_taps(xe):
    return [pltpu.roll(xe, 3, 0)[8:, :], pltpu.roll(xe, 2, 0)[8:, :], pltpu.roll(xe, 1, 0)[8:, :], xe[8:, :]]


def _rg_gates(h, wa, wx, ba, bx, sp):
    r = _sigmoid(_mm(h, wa) + ba)
    i = _sigmoid(_mm(h, wx) + bx)
    log_a = (-RG_C) * r * sp
    a = jnp.exp(log_a)
    mult = jnp.sqrt(-jnp.tanh(log_a) * (a * a + 1.0))
    return r, i, a, mult


def _softplus(y):
    return jnp.maximum(y, 0.0) + jnp.log1p(jnp.exp(-jnp.abs(y)))


def _after(token):
    return ([], []) if token is None else ([token], [ANY])


def _inproj_fwd(x, w_in, token=None):
    L = x.shape[0]

    def body(x_ref, w_ref, *rest):
        xb = x_ref[...].astype(MXU)
        for j in range(N_DEV):
            rest[-1][:, j * W_BLK:(j + 1) * W_BLK] = jnp.dot(xb, w_ref[j].astype(MXU), preferred_element_type=F32)

    extra, extra_specs = _after(token)
    return pl.pallas_call(
        body, name="inproj_fwd", grid=(L // TM,),
        in_specs=[pl.BlockSpec((TM, D_MODEL), lambda i: (i, 0)),
                  pl.BlockSpec((N_DEV, D_MODEL, W_BLK), lambda i: (0, 0, 0))] + extra_specs,
        out_specs=pl.BlockSpec((TM, Z_W), lambda i: (i, 0)),
        out_shape=_S((L, Z_W)), compiler_params=_params(1))(x, w_in, *extra)


def _inproj_bwd(dt1, x, dzx, dzg, dzu, w_in):
    L = x.shape[0]

    def body(dt1_ref, x_ref, dzx_ref, dzg_ref, dzu_ref, w_ref, dx_ref, dw_ref, acc_ref):
        @pl.when(pl.program_id(0) == 0)
        def _():
            acc_ref[...] = jnp.zeros_like(acc_ref)
        dzg = dzg_ref[...]
        dz = jnp.concatenate([dzx_ref[...], dzg[:, :RG_W], dzu_ref[...], dzg[:, RG_W:]], axis=1).astype(MXU)
        xb = x_ref[...].astype(MXU)
        dx = ALPHA * dt1_ref[...]
        for j in range(N_DEV):
            dzj = dz[:, j * W_BLK:(j + 1) * W_BLK]
            dx = dx + _mm_nt(dzj, w_ref[j])
            acc_ref[j] += _mm_tn(xb, dzj)
        dx_ref[...] = dx

        @pl.when(pl.program_id(0) == L // TM - 1)
        def _():
            dw_ref[...] = acc_ref[...].astype(WIRE)

    row = lambda w: pl.BlockSpec((TM, w), lambda i: (i, 0))
    wspec = pl.BlockSpec((N_DEV, D_MODEL, W_BLK), lambda i: (0, 0, 0))
    return pl.pallas_call(
        body, name="inproj_bwd", grid=(L // TM,),
        in_specs=[row(D_MODEL), row(D_MODEL), row(RG_W), row(D_MODEL), row(S5_W), wspec],
        out_specs=[row(D_MODEL), wspec],
        out_shape=[_S((L, D_MODEL)), _S((N_DEV, D_MODEL, W_BLK), WIRE)],
        scratch_shapes=[pltpu.VMEM((N_DEV, D_MODEL, W_BLK), F32)],
        compiler_params=_params(1))(dt1, x, dzx, dzg, dzu, w_in)


TM2 = 512


def _dz_block(dzx_ref, dzg_ref, dzu_ref):
    dzg = dzg_ref[...]
    return jnp.concatenate([dzx_ref[...], dzg[:, :RG_W], dzu_ref[...], dzg[:, RG_W:]], axis=1).astype(MXU)


def _inproj_bwd_dw(x, dzx, dzg, dzu):
    L = x.shape[0]

    def body(x_ref, dzx_ref, dzg_ref, dzu_ref, dw_ref, acc_ref):
        @pl.when(pl.program_id(0) == 0)
        def _():
            acc_ref[...] = jnp.zeros_like(acc_ref)
        dz = _dz_block(dzx_ref, dzg_ref, dzu_ref)
        xb = x_ref[...].astype(MXU)
        for j in range(N_DEV):
            acc_ref[j] += _mm_tn(xb, dz[:, j * W_BLK:(j + 1) * W_BLK])

        @pl.when(pl.program_id(0) == L // TM2 - 1)
        def _():
            dw_ref[...] = acc_ref[...].astype(WIRE)

    row = lambda w: pl.BlockSpec((TM2, w), lambda i: (i, 0))
    wspec = pl.BlockSpec((N_DEV, D_MODEL, W_BLK), lambda i: (0, 0, 0))
    return pl.pallas_call(
        body, name="inproj_bwd_dw", grid=(L // TM2,),
        in_specs=[row(D_MODEL), row(RG_W), row(D_MODEL), row(S5_W)], out_specs=wspec,
        out_shape=_S((N_DEV, D_MODEL, W_BLK), WIRE), scratch_shapes=[pltpu.VMEM((N_DEV, D_MODEL, W_BLK), F32)],
        compiler_params=_params(1))(x, dzx, dzg, dzu)


def _inproj_bwd_dx(dt1, dzx, dzg, dzu, w_in, token=None):
    L = dt1.shape[0]
    extra, extra_specs = _after(token)

    def body(dt1_ref, dzx_ref, dzg_ref, dzu_ref, w_ref, *rest):
        dz = _dz_block(dzx_ref, dzg_ref, dzu_ref)
        dx = ALPHA * dt1_ref[...]
        for j in range(N_DEV):
            dx = dx + _mm_nt(dz[:, j * W_BLK:(j + 1) * W_BLK], w_ref[j])
        rest[-1][...] = dx

    row = lambda w: pl.BlockSpec((TM2, w), lambda i: (i, 0))
    wspec = pl.BlockSpec((N_DEV, D_MODEL, W_BLK), lambda i: (0, 0, 0))
    return pl.pallas_call(
        body, name="inproj_bwd_dx", grid=(L // TM2,),
        in_specs=[row(D_MODEL), row(RG_W), row(D_MODEL), row(S5_W), wspec] + extra_specs, out_specs=row(D_MODEL),
        out_shape=_S((L, D_MODEL)), compiler_params=_params(1))(dt1, dzx, dzg, dzu, w_in, *extra)


def _rg_specs(layer):
    tile = lambda rows: pl.BlockSpec((rows, LANE), lambda c: (0, c))
    ptile = lambda rows: pl.BlockSpec((None, rows, LANE), lambda c: (layer, 0, c))
    pbd = pl.BlockSpec((None, LANE, LANE), lambda c: (layer * N_RG_T + c, 0, 0))
    return tile, ptile, pbd, pl.BlockSpec((None, LANE, LANE), lambda c: (c, 0, 0))


def _rg_fwd(z, cw, cb, wa_bd, wx_bd, ba, bx, lam, layer):
    L = z.shape[0]

    def body(x_ref, cw_ref, cb_ref, wa_ref, wx_ref, ba_ref, bx_ref, lam_ref, hs_ref):
        w, b = cw_ref[...], cb_ref[...]
        wa, wx, ba_, bx_ = wa_ref[...].astype(MXU), wx_ref[...].astype(MXU), ba_ref[...], bx_ref[...]
        sp = _softplus(-lam_ref[...])

        def step(c, carry):
            r0 = pl.multiple_of(c * RC, RC)
            xe = jnp.concatenate([_halo(x_ref, c, r0), x_ref[pl.ds(r0, RC), :]], axis=0)
            t = _conv_taps(xe)
            h = t[0] * w[0:1] + t[1] * w[1:2] + t[2] * w[2:3] + t[3] * w[3:4] + b
            _, i, a, mult = _rg_gates(h, wa, wx, ba_, bx_, sp)
            hs, carry = _scan_real(a, mult * (i * h), carry)
            hs_ref[pl.ds(r0, RC), :] = hs
            return carry

        lax.fori_loop(0, L // RC, step, jnp.zeros((1, LANE), F32))

    tile, ptile, pbd, _ = _rg_specs(layer)
    return pl.pallas_call(
        body, name="rg_fwd", grid=(N_RG_T,),
        in_specs=[tile(L), ptile(4), ptile(1), pbd, pbd, ptile(1), ptile(1), ptile(1)],
        out_specs=tile(L), out_shape=_S((L, RG_W)), compiler_params=_params(1))(z, cw, cb, wa_bd, wx_bd, ba, bx, lam)


def _rg_bwd(dhs, z, hs, cw, cb, wa_bd, wx_bd, ba, bx, lam, layer):
    L = z.shape[0]

    def body(g_ref, x_ref, hs_ref, cw_ref, cb_ref, wa_ref, wx_ref, ba_ref, bx_ref, lam_ref,
             dx_ref, dcw_ref, dcb_ref, dwa_ref, dwx_ref, dba_ref, dbx_ref, dlam_ref):
        w, b = cw_ref[...], cb_ref[...]
        wa, wx, ba_, bx_ = wa_ref[...].astype(MXU), wx_ref[...].astype(MXU), ba_ref[...], bx_ref[...]
        lam = lam_ref[...]
        sp = _softplus(-lam)
        rows = lax.broadcasted_iota(jnp.int32, (RC, LANE), 0)
        for ref in (dcw_ref, dcb_ref, dwa_ref, dwx_ref, dba_ref, dbx_ref, dlam_ref):
            ref[...] = jnp.zeros_like(ref)
        nch = L // RC

        def step(k, carry):
            cin, nxt = carry
            c = nch - 1 - k
            r0 = pl.multiple_of(c * RC, RC)
            xe = jnp.concatenate([_halo(x_ref, c, r0), x_ref[pl.ds(r0, RC), :]], axis=0)
            t = _conv_taps(xe)
            h = t[0] * w[0:1] + t[1] * w[1:2] + t[2] * w[2:3] + t[3] * w[3:4] + b
            r, i, a, mult = _rg_gates(h, wa, wx, ba_, bx_, sp)
            hs_e = jnp.concatenate([_halo(hs_ref, c, r0), hs_ref[pl.ds(r0, RC), :]], axis=0)
            hs_prev = pltpu.roll(hs_e, 1, 0)[8:, :]
            g = g_ref[pl.ds(r0, RC), :]
            cc, cin_new = _scan_real(a, a * g, cin, reverse=True)
            dh = g + _up(cc, 1, rows, cin)
            ih = i * h
            dlog_a = dh * hs_prev * a - (dh * ih) * (a * a) / mult
            di = dh * mult * h
            dhin = dh * mult * i
            dr = dlog_a * ((-RG_C) * sp)
            dlam_ref[...] += _colsum(dlog_a * r)
            dra = dr * r * (1.0 - r)
            dia = di * i * (1.0 - i)
            dwa_ref[...] += _mm_tn(h, dra)
            dwx_ref[...] += _mm_tn(h, dia)
            dba_ref[...] += _colsum(dra)
            dbx_ref[...] += _colsum(dia)
            dhin = dhin + _mm_nt(dra, wa) + _mm_nt(dia, wx)
            de = jnp.concatenate([dhin, nxt], axis=0)
            n = RC + 8
            dx = (dhin * w[3:4] + pltpu.roll(de, n - 1, 0)[:RC, :] * w[2:3]
                  + pltpu.roll(de, n - 2, 0)[:RC, :] * w[1:2] + pltpu.roll(de, n - 3, 0)[:RC, :] * w[0:1])
            dx_ref[pl.ds(r0, RC), :] = dx
            for kk in range(4):
                dcw_ref[kk:kk + 1, :] += _colsum(dhin * t[kk])
            dcb_ref[...] += _colsum(dhin)
            return cin_new, dhin[0:8, :]

        lax.fori_loop(0, nch, step, (jnp.zeros((1, LANE), F32), jnp.zeros((8, LANE), F32)))
        dlam_ref[...] = dlam_ref[...] * (RG_C * _sigmoid(-lam))

    tile, ptile, pbd, bd = _rg_specs(layer)
    return pl.pallas_call(
        body, name="rg_bwd", grid=(N_RG_T,),
        in_specs=[tile(L), tile(L), tile(L), ptile(4), ptile(1), pbd, pbd, ptile(1), ptile(1), ptile(1)],
        out_specs=[tile(L), tile(4), tile(1), bd, bd, tile(1), tile(1), tile(1)],
        out_shape=[_S((L, RG_W)), _S((4, RG_W)), _S((1, RG_W)), _S((N_RG_T, LANE, LANE)), _S((N_RG_T, LANE, LANE)),
                   _S((1, RG_W)), _S((1, RG_W)), _S((1, RG_W))],
        compiler_params=_params(1))(dhs, z, hs, cw, cb, wa_bd, wx_bd, ba, bx, lam)


def _cmul(ar, ai, br, bi):
    return ar * br - ai * bi, ar * bi + ai * br


S5_TW = S5_N // N_S5_T


def _s5_specs(L):
    in_tile = pl.BlockSpec((L, LANE), lambda t: (0, t))
    st = pl.BlockSpec((L, S5_TW), lambda t: (0, t))
    bb = pl.BlockSpec((None, LANE, S5_TW), lambda t: (t, 0, 0))
    cc = pl.BlockSpec((None, S5_TW, LANE), lambda t: (t, 0, 0))
    lb = pl.BlockSpec((1, S5_TW), lambda t: (0, t))
    dv = pl.BlockSpec((1, LANE), lambda t: (0, t))
    return in_tile, st, bb, cc, lb, dv


def _layer_row_tile(layer):
    return pl.BlockSpec((None, 1, LANE), lambda t: (layer, 0, t))


def _s5_fwd(z, bb_re, bb_im, lb_re, lb_im, c_re, c_im, dvec, layer):
    L = z.shape[0]

    def body(u_ref, bbr_ref, bbi_ref, lr_ref, li_ref, cr_ref, ci_ref, d_ref, y_ref, sr_ref, si_ref):
        bbr, bbi = bbr_ref[...].astype(MXU), bbi_ref[...].astype(MXU)
        cr, ci = cr_ref[...].astype(MXU), ci_ref[...].astype(MXU)
        dv = d_ref[...]
        steps, e = _tile_powers(lr_ref[...], li_ref[...])

        def step(c, carry):
            r0 = pl.multiple_of(c * RC, RC)
            u = u_ref[pl.ds(r0, RC), :]
            ub = u.astype(MXU)
            sr = jnp.dot(ub, bbr, preferred_element_type=F32)
            si = jnp.dot(ub, bbi, preferred_element_type=F32)
            sr, si, carry = _scan_lti(sr, si, carry, steps, e)
            sr_ref[pl.ds(r0, RC), :] = sr
            si_ref[pl.ds(r0, RC), :] = si
            y_ref[pl.ds(r0, RC), :] = dv * u + (_mm(sr, cr) - _mm(si, ci))
            return carry

        zero = jnp.zeros((1, S5_TW), F32)
        lax.fori_loop(0, L // RC, step, (zero, zero))

    in_tile, st, bb, cc, lb, dv = _s5_specs(L)
    u_tile = pl.BlockSpec((L, LANE), lambda t: (0, C_S5U // LANE + t))
    return pl.pallas_call(
        body, name="s5_fwd", grid=(N_S5_T,),
        in_specs=[u_tile, bb, bb, lb, lb, cc, cc, _layer_row_tile(layer)],
        out_specs=[in_tile, st, st],
        out_shape=[_S((L, S5_W)), _S((L, S5_N)), _S((L, S5_N))],
        compiler_params=_params(1))(z, bb_re, bb_im, lb_re, lb_im, c_re, c_im, dvec)


def _s5_bwd(dy0, z, s_re, s_im, bb_re, bb_im, lb_re, lb_im, c_re, c_im, dvec, layer, token=None):
    L = z.shape[0]
    extra, extra_specs = _after(token)

    def body(dy_ref, u_ref, sr_ref, si_ref, bbr_ref, bbi_ref, lr_ref, li_ref, cr_ref, ci_ref, d_ref, *rest):
        du_ref, dbbr_ref, dbbi_ref, dlr_ref, dli_ref, dcr_ref, dci_ref, dd_ref = rest[len(extra):]
        bbr, bbi = bbr_ref[...].astype(MXU), bbi_ref[...].astype(MXU)
        cr, ci = cr_ref[...].astype(MXU), ci_ref[...].astype(MXU)
        lr, li = lr_ref[...], -li_ref[...]
        dv = d_ref[...]
        steps, e = _tile_powers(lr, li, reverse=True)
        for ref in (dbbr_ref, dbbi_ref, dlr_ref, dli_ref, dcr_ref, dci_ref, dd_ref):
            ref[...] = jnp.zeros_like(ref)
        nch = L // RC

        def step(k, carry):
            c = nch - 1 - k
            r0 = pl.multiple_of(c * RC, RC)
            dy = dy_ref[pl.ds(r0, RC), :]
            u = u_ref[pl.ds(r0, RC), :]
            dyb, ub = dy.astype(MXU), u.astype(MXU)
            sr, si = sr_ref[pl.ds(r0, RC), :], si_ref[pl.ds(r0, RC), :]
            dcr_ref[...] += _mm_tn(sr, dyb)
            dci_ref[...] -= _mm_tn(si, dyb)
            gr = _mm_nt(dyb, cr)
            gi = -_mm_nt(dyb, ci)
            gr, gi, carry = _scan_lti(gr, gi, carry, steps, e, reverse=True)
            pr_ = pltpu.roll(jnp.concatenate([_halo(sr_ref, c, r0), sr], axis=0), 1, 0)[8:, :]
            pi_ = pltpu.roll(jnp.concatenate([_halo(si_ref, c, r0), si], axis=0), 1, 0)[8:, :]
            dlr_ref[...] += _colsum(pr_ * gr + pi_ * gi)
            dli_ref[...] += _colsum(pr_ * gi - pi_ * gr)
            grb, gib = gr.astype(MXU), gi.astype(MXU)
            dbbr_ref[...] += _mm_tn(ub, grb)
            dbbi_ref[...] += _mm_tn(ub, gib)
            du_ref[pl.ds(r0, RC), :] = dv * dy + (_mm_nt(grb, bbr) + _mm_nt(gib, bbi))
            dd_ref[...] += _colsum(dy * u)
            return carry

        zero = jnp.zeros((1, S5_TW), F32)
        lax.fori_loop(0, nch, step, (zero, zero))

    in_tile, st, bb, cc, lb, dv = _s5_specs(L)
    u_tile = pl.BlockSpec((L, LANE), lambda t: (0, C_S5U // LANE + t))
    return pl.pallas_call(
        body, name="s5_bwd", grid=(N_S5_T,),
        in_specs=[in_tile, u_tile, st, st, bb, bb, lb, lb, cc, cc, _layer_row_tile(layer)] + extra_specs,
        out_specs=[in_tile, bb, bb, lb, lb, cc, cc, dv],
        out_shape=[_S((L, S5_W)), _S((N_S5_T, LANE, S5_TW)), _S((N_S5_T, LANE, S5_TW)), _S((1, S5_N)), _S((1, S5_N)),
                   _S((N_S5_T, S5_TW, LANE)), _S((N_S5_T, S5_TW, LANE)), _S((1, S5_W))],
        compiler_params=_params(1))(dy0, z, s_re, s_im, bb_re, bb_im, lb_re, lb_im, c_re, c_im, dvec, *extra)


def _disc(ar, ai, ls):
    dt = jnp.exp(ls)
    mag = jnp.exp(ar * dt)
    lr = mag * jnp.cos(ai * dt)
    li = mag * jnp.sin(ai * dt)
    den = ar * ar + ai * ai
    cr = ((lr - 1.0) * ar + li * ai) / den
    ci = (li * ar - (lr - 1.0) * ai) / den
    return lr, li, cr, ci


def _s5_disc_fwd(ar, ai, ls):
    def body(ar_ref, ai_ref, ls_ref, lr_ref, li_ref, cr_ref, ci_ref):
        lr, li, cr, ci = _disc(ar_ref[...], ai_ref[...], ls_ref[...])
        lr_ref[...], li_ref[...], cr_ref[...], ci_ref[...] = lr, li, cr, ci

    sh = _S(ar.shape)
    return pl.pallas_call(body, name="s5_disc_fwd", out_shape=[sh, sh, sh, sh])(ar, ai, ls)


def _s5_disc_bwd(ar, ai, ls, dlr, dli, dcr, dci):
    def body(ar_ref, ai_ref, ls_ref, dlr_ref, dli_ref, dcr_ref, dci_ref, dar_ref, dai_ref, dls_ref):
        _, vjp = jax.vjp(_disc, ar_ref[...], ai_ref[...], jnp.broadcast_to(ls_ref[...], ar_ref.shape))
        dar, dai, dls = vjp((dlr_ref[...], dli_ref[...], dcr_ref[...], dci_ref[...]))
        dar_ref[...], dai_ref[...] = dar, dai
        dls_ref[...] = jnp.sum(dls, axis=1, keepdims=True)

    return pl.pallas_call(body, name="s5_disc_bwd", out_shape=[_S(ar.shape), _S(ar.shape), _S(ls.shape)])(
        ar, ai, ls, dlr, dli, dcr, dci)


def _s5_bscale_fwd(cr, ci, br, bi):
    def body(cr_ref, ci_ref, br_ref, bi_ref, or_ref, oi_ref):
        or_ref[...], oi_ref[...] = _cmul(cr_ref[...], ci_ref[...], br_ref[...], bi_ref[...])

    return pl.pallas_call(body, name="s5_bscale_fwd", out_shape=[_S(br.shape), _S(br.shape)])(cr, ci, br, bi)


def _s5_bscale_bwd(cr, ci, br, bi, gr, gi):
    def body(cr_ref, ci_ref, br_ref, bi_ref, gr_ref, gi_ref, dbr_ref, dbi_ref, dcr_ref, dci_ref):
        cr_, ci_, br_, bi_, gr_, gi_ = (r[...] for r in (cr_ref, ci_ref, br_ref, bi_ref, gr_ref, gi_ref))
        dbr_ref[...] = cr_ * gr_ + ci_ * gi_
        dbi_ref[...] = cr_ * gi_ - ci_ * gr_
        dcr_ref[...] = jnp.sum(gr_ * br_ + gi_ * bi_, axis=1, keepdims=True)
        dci_ref[...] = jnp.sum(gi_ * br_ - gr_ * bi_, axis=1, keepdims=True)

    return pl.pallas_call(body, name="s5_bscale_bwd",
                          out_shape=[_S(br.shape), _S(br.shape), _S(cr.shape), _S(cr.shape)])(cr, ci, br, bi, gr, gi)


def _row(w):
    return pl.BlockSpec((TM, w), lambda i: (i, 0))


def _full(shape):
    return pl.BlockSpec(tuple(shape), lambda i: (0,) * len(shape))


def _lrow(layer, width):
    return pl.BlockSpec((None, 1, width), lambda i: (layer, 0, 0))


def _post_fwd(x, hs, z, y0, p, w_glu, b_glu, w_out, g1, b1, ple_w, w_pg, b_pg, g2, b2, layer):
    L = x.shape[0]

    def body(x_ref, hs_ref, z_ref, y0_ref, p_ref, wg_ref, bg_ref, wo_ref, g1_ref, b1_ref, pw_ref, wpg_ref, bpg_ref,
             g2_ref, b2_ref, x2_ref, xh1_ref, xh2_ref, m_ref, q_ref, gt_ref, rstd1_ref, rstd2_ref):
        rg_gate = z_ref[:, C_RGG:C_RGG + RG_W]
        s5_gate = z_ref[:, C_S5G:C_S5G + S5_W]
        rg_y = hs_ref[...] * _silu_and_grad(rg_gate)[0]
        y1 = _gelu(y0_ref[...])
        gl = _sigmoid(_mm(y1, wg_ref[...]) + bg_ref[...])
        s5_y = (y1 * gl) * _silu_and_grad(s5_gate)[0]
        m_ref[:, :RG_W] = rg_y
        m_ref[:, RG_W:] = s5_y
        mix = _mm(m_ref[...], wo_ref[...])
        t1 = ALPHA * x_ref[...] + mix
        x1, xh1, rstd1 = _ln_fwd(t1, g1_ref[...], b1_ref[...])
        q = _mm(p_ref[...], pw_ref[...])
        gt = _sigmoid(_mm(x1, wpg_ref[...]) + bpg_ref[...])
        t2 = ALPHA * x1 + q * gt
        x2, xh2, rstd2 = _ln_fwd(t2, g2_ref[...], b2_ref[...])
        x2_ref[...], xh1_ref[...], xh2_ref[...], q_ref[...], gt_ref[...] = x2, xh1, xh2, q, gt
        rstd1_ref[...], rstd2_ref[...] = rstd1, rstd2

    vec = _lrow(layer, D_MODEL)
    return pl.pallas_call(
        body, name="post_fwd", grid=(L // TM,),
        in_specs=[_row(D_MODEL), _row(RG_W), _row(Z_W), _row(S5_W), _row(256), _full((S5_W, S5_W)), _lrow(layer, S5_W),
                  _full((D_MODEL, D_MODEL)), vec, vec, _full((256, D_MODEL)), _full((D_MODEL, D_MODEL)), vec, vec, vec],
        out_specs=[_row(D_MODEL)] * 6 + [_row(1)] * 2, out_shape=[_S((L, D_MODEL))] * 6 + [_S((L, 1))] * 2,
        compiler_params=_params(1))(x, hs, z, y0, p, w_glu, b_glu, w_out, g1, b1, ple_w, w_pg, b_pg, g2, b2)


def _post_bwd_a(dx2_or_target, is_top, xh2, xh1, rstd2, rstd1, q, gt, p, w_pg, g1, b1, g2, b2, layer, token=None):
    L = xh1.shape[0]
    extra, extra_specs = _after(token)

    def body(d_ref, xh2_ref, xh1_ref, rstd2_ref, rstd1_ref, q_ref, gt_ref, p_ref, wpg_ref, g1_ref, b1_ref, g2_ref,
             b2_ref, *rest):
        (dt1_ref, dpw_out, dwpg_out, dbpg_ref, dg1_ref, db1_ref, dg2_ref, db2_ref, loss_ref, dpw_ref,
         dwpg_ref) = rest[len(extra):]
        @pl.when(pl.program_id(0) == 0)
        def _():
            for ref in (dpw_ref, dwpg_ref, dbpg_ref, dg1_ref, db1_ref, dg2_ref, db2_ref, loss_ref):
                ref[...] = jnp.zeros_like(ref)

        g1, g2 = g1_ref[...], g2_ref[...]
        xh1, xh2, rstd1, rstd2 = xh1_ref[...], xh2_ref[...], rstd1_ref[...], rstd2_ref[...]
        x1 = xh1 * g1 + b1_ref[...]
        if is_top:
            err = (xh2 * g2 + b2_ref[...]) - d_ref[...]
            loss_ref[...] += _colsum(err * err)
            dx2 = err * (1.0 / D_MODEL)
        else:
            dx2 = d_ref[...]
        p = p_ref[...]
        q, gt = q_ref[...], gt_ref[...]
        dg2_ref[...] += _colsum(dx2 * xh2)
        db2_ref[...] += _colsum(dx2)
        dt2 = _ln_bwd(dx2, xh2, rstd2, g2)
        dq = dt2 * gt
        dgpre = (dt2 * q) * gt * (1.0 - gt)
        dpw_ref[...] += _mm_tn(p, dq)
        dwpg_ref[...] += _mm_tn(x1, dgpre)
        dbpg_ref[...] += _colsum(dgpre)
        dx1 = ALPHA * dt2 + _mm_nt(dgpre, wpg_ref[...])
        dg1_ref[...] += _colsum(dx1 * xh1)
        db1_ref[...] += _colsum(dx1)
        dt1_ref[...] = _ln_bwd(dx1, xh1, rstd1, g1)

        @pl.when(pl.program_id(0) == L // TM - 1)
        def _():
            dpw_out[...] = dpw_ref[...].astype(WIRE)
            dwpg_out[...] = dwpg_ref[...].astype(WIRE)

    vec, lvec = _full((1, D_MODEL)), _lrow(layer, D_MODEL)
    return pl.pallas_call(
        body, name="post_bwd_a_top" if is_top else "post_bwd_a", grid=(L // TM,),
        in_specs=[_row(D_MODEL), _row(D_MODEL), _row(D_MODEL), _row(1), _row(1), _row(D_MODEL), _row(D_MODEL), _row(256),
                  _full((D_MODEL, D_MODEL)), lvec, lvec, lvec, lvec] + extra_specs,
        out_specs=[_row(D_MODEL), _full((256, D_MODEL)), _full((D_MODEL, D_MODEL)), vec, vec, vec, vec, vec, vec],
        out_shape=[_S((L, D_MODEL)), _S((256, D_MODEL), WIRE), _S((D_MODEL, D_MODEL), WIRE)] + [_S((1, D_MODEL))] * 6,
        scratch_shapes=[pltpu.VMEM((256, D_MODEL), F32), pltpu.VMEM((D_MODEL, D_MODEL), F32)],
        compiler_params=_params(1))(dx2_or_target, xh2, xh1, rstd2, rstd1, q, gt, p, w_pg, g1, b1, g2, b2, *extra)


def _post_bwd_b(dt1, m, z, hs, y0, w_out, w_glu, b_glu, layer):
    L = dt1.shape[0]

    def body(dt1_ref, m_ref, z_ref, hs_ref, y0_ref, wo_ref, wg_ref, bg_ref,
             dhs_ref, dy0_ref, dzg_ref, dwo_out, dwg_out, dbg_ref, dwo_ref, dwg_ref):
        @pl.when(pl.program_id(0) == 0)
        def _():
            for ref in (dwo_ref, dwg_ref, dbg_ref):
                ref[...] = jnp.zeros_like(ref)

        dt1b = dt1_ref[...].astype(MXU)
        dm = _mm_nt(dt1b, wo_ref[...])
        dwo_ref[...] += _mm_tn(m_ref[...], dt1b)
        d_rgy, d_s5y = dm[:, :RG_W], dm[:, RG_W:]
        rg_gate = z_ref[:, C_RGG:C_RGG + RG_W]
        s5_gate = z_ref[:, C_S5G:C_S5G + S5_W]
        sl, dsl = _silu_and_grad(rg_gate)
        dhs_ref[...] = d_rgy * sl
        dzg_ref[:, :RG_W] = d_rgy * hs_ref[...] * dsl
        y0 = y0_ref[...]
        y1 = _gelu(y0)
        gl = _sigmoid(_mm(y1, wg_ref[...]) + bg_ref[...])
        sl, dsl = _silu_and_grad(s5_gate)
        dy2 = d_s5y * sl
        dzg_ref[:, RG_W:] = d_s5y * (y1 * gl) * dsl
        dglpre = (dy2 * y1) * gl * (1.0 - gl)
        dwg_ref[...] += _mm_tn(y1, dglpre)
        dbg_ref[...] += _colsum(dglpre)
        dy1 = dy2 * gl + _mm_nt(dglpre, wg_ref[...])
        dy0_ref[...] = dy1 * _gelu_grad(y0)

        @pl.when(pl.program_id(0) == L // TM - 1)
        def _():
            dwo_out[...] = dwo_ref[...].astype(WIRE)
            dwg_out[...] = dwg_ref[...].astype(WIRE)

    return pl.pallas_call(
        body, name="post_bwd_b", grid=(L // TM,),
        in_specs=[_row(D_MODEL), _row(D_MODEL), _row(Z_W), _row(RG_W), _row(S5_W), _full((D_MODEL, D_MODEL)),
                  _full((S5_W, S5_W)), _lrow(layer, S5_W)],
        out_specs=[_row(RG_W), _row(S5_W), _row(D_MODEL), _full((D_MODEL, D_MODEL)), _full((S5_W, S5_W)), _full((1, S5_W))],
        out_shape=[_S((L, RG_W)), _S((L, S5_W)), _S((L, D_MODEL)), _S((D_MODEL, D_MODEL), WIRE), _S((S5_W, S5_W), WIRE),
                   _S((1, S5_W))],
        scratch_shapes=[pltpu.VMEM((D_MODEL, D_MODEL), F32), pltpu.VMEM((S5_W, S5_W), F32)],
        compiler_params=_params(1))(dt1, m, z, hs, y0, w_out, w_glu, b_glu)


def _adamw(parts, w, m, v, token=None):
    nl = len(parts)
    extra, extra_specs = _after(token)
    n, R, C = parts[0].shape
    tr = R
    for cand in (512, 256, 128, 64, 32, 16, 8):
        if R % cand == 0 and n * cand * C * 4 <= 4 * 1024 * 1024:
            tr = cand
            break
    nblk = R // tr

    def body(*refs):
        p_refs = refs[:nl]
        w_ref, m_ref, v_ref = refs[nl:nl + 3]
        g_ref, d_ref, nm_ref, nv_ref = refs[nl + 3 + len(extra):]
        layer = pl.program_id(0)
        g = None
        for li, p_ref in enumerate(p_refs):
            s = p_ref[0].astype(F32)
            for k in range(1, n):
                s = s + p_ref[k].astype(F32)
            g = s if g is None else jnp.where(layer == li, s, g)
        nm = B1 * m_ref[...] + (1.0 - B1) * g
        nv = B2 * v_ref[...] + (1.0 - B2) * (g * g)
        d_ref[...] = (-LR) * ((nm / BC1) / (jnp.sqrt(nv / BC2) + EPS) + WD * w_ref[...])
        g_ref[...], nm_ref[...], nv_ref[...] = g, nm, nv

    def part_spec(li):
        return pl.BlockSpec((n, tr, C), lambda l, i: (0, jnp.where(l == li, i, jnp.where(l < li, 0, nblk - 1)), 0))

    blk = pl.BlockSpec((tr, C), lambda l, i: (l * nblk + i, 0))
    return pl.pallas_call(
        body, name="adamw", grid=(nl, nblk),
        in_specs=[part_spec(li) for li in range(nl)] + [blk, blk, blk] + extra_specs,
        out_specs=[blk] * 4, out_shape=[_S((nl * R, C))] * 4, compiler_params=_params(2))(*parts, w, m, v, *extra)


def _adamw_natural(names, g, w, m, v, name):
    n = len(names)

    def body(*refs):
        for j in range(n):
            g_ref, w_ref, m_ref, v_ref, d_ref, nm_ref, nv_ref = (refs[k * n + j] for k in range(7))
            gj = g_ref[...]
            nm = B1 * m_ref[...] + (1.0 - B1) * gj
            nv = B2 * v_ref[...] + (1.0 - B2) * (gj * gj)
            d_ref[...] = (-LR) * ((nm / BC1) / (jnp.sqrt(nv / BC2) + EPS) + WD * w_ref[...])
            nm_ref[...], nv_ref[...] = nm, nv

    ins = [t[k] for t in (g, w, m, v) for k in names]
    outs = pl.pallas_call(body, name=name, out_shape=[_S(w[k].shape) for _ in range(3) for k in names],
                          compiler_params=pltpu.CompilerParams(vmem_limit_bytes=VMEM_LIMIT))(*ins)
    return [{k: outs[t * n + j] for j, k in enumerate(names)} for t in range(3)]


def _me():
    return lax.axis_index("x"), lax.axis_index("y"), lax.axis_index("c")


def _lin(dev):
    return 4 * dev[0] + 2 * dev[1] + dev[2]


def _blk(ref, axis, size, idx):
    nd = len(ref.shape)
    start = idx * size
    if axis == nd - 1 and size % LANE == 0:
        start = pl.multiple_of(start, LANE)
    elif axis == nd - 2 and size % 16 == 0:
        start = pl.multiple_of(start, 16)
    ix = [slice(None)] * nd
    ix[axis] = pl.ds(start, size)
    return ref.at[tuple(ix)]


def _all_gather(shards, axes, name):
    n = len(shards)
    sizes = [s.shape[a] for s, a in zip(shards, axes)]
    out_shapes = [_S(s.shape[:a] + (N_DEV * s.shape[a],) + s.shape[a + 1:], s.dtype) for s, a in zip(shards, axes)]

    def body(*refs):
        ins, outs = refs[:n], refs[n:2 * n]
        send_sems, recv_sems, local_sems = refs[2 * n:]
        x, y, c = _me()
        me, sibling = (x, y, c), (x, y, 1 - c)
        chips = [(1 - x, y), (x, 1 - y), (1 - x, 1 - y)]

        def copy(a, k, block, to, from_input=False):
            dst = _blk(outs[a], axes[a], sizes[a], _lin(block))
            return pltpu.make_async_remote_copy(
                src_ref=ins[a] if from_input else dst, dst_ref=dst, send_sem=send_sems.at[a, k],
                recv_sem=recv_sems.at[a, k], device_id=to, device_id_type=MESH)

        mine = [pltpu.make_async_copy(ins[a], _blk(outs[a], axes[a], sizes[a], _lin(me)), local_sems.at[a]) for a in range(n)]
        for cp in mine:
            cp.start()
        first = []
        for a in range(n):
            first.append(copy(a, 0, me, sibling, True))
            first += [copy(a, 1 + j, me, (*chip, c), True) for j, chip in enumerate(chips)]
        for cp in first:
            cp.start()
        passed = []
        for j, chip in enumerate(chips):
            for a in range(n):
                copy(a, 1 + j, (*chip, c), me).wait_recv()
                cp = copy(a, 4 + j, (*chip, c), sibling)
                cp.start()
                passed.append(cp)
        for a in range(n):
            copy(a, 0, sibling, me).wait_recv()
            for j, chip in enumerate(chips):
                copy(a, 4 + j, (*chip, 1 - c), me).wait_recv()
        for cp in first + passed:
            cp.wait_send()
        for cp in mine:
            cp.wait()

    return pl.pallas_call(
        body, name=name, out_shape=out_shapes, in_specs=[ANY] * n, out_specs=[ANY] * n,
        scratch_shapes=[pltpu.SemaphoreType.DMA((n, 7)), pltpu.SemaphoreType.DMA((n, 7)), pltpu.SemaphoreType.DMA((n,))],
    )(*shards)


HBM_SPEC = pl.BlockSpec(memory_space=pltpu.HBM)
SEM_SPEC = pl.BlockSpec(memory_space=pltpu.SEMAPHORE)
EFFECT = pltpu.SideEffectType.DATAFLOW_SIDE_EFFECTING


def _peers(x, y, c):
    flip = lambda v, f: 1 - v if f else v
    return [(flip(x, k & 4), flip(y, k & 2), flip(c, k & 1)) for k in range(1, N_DEV)]


def _land_shape(mode, s, axis):
    if mode == "gather":
        return s.shape[:axis] + (N_DEV * s.shape[axis],) + s.shape[axis + 1:]
    return (N_DEV,) + s.shape[:axis] + (s.shape[axis] // N_DEV,) + s.shape[axis + 1:]


def _src_view(mode, ref, axis, peer):
    return ref if mode == "gather" else _blk(ref, axis, ref.shape[axis] // N_DEV, peer)


def _dst_view(mode, land, axis, sender):
    return _blk(land, axis, land.shape[axis] // N_DEV, sender) if mode == "gather" else land.at[sender]


def _seven_blocks(mode, land, axis):
    if mode == "gather":
        ix = [slice(None)] * len(land.shape)
        ix[axis] = pl.ds(0, (N_DEV - 1) * (land.shape[axis] // N_DEV))
        return land.at[tuple(ix)]
    return land.at[pl.ds(0, N_DEV - 1)]


def _place_own(mode, srcs, axes, name, after=None):
    n = len(srcs)
    extra, extra_specs = _after(after)

    def body(me_ref, *refs):
        for a in range(n):
            out = refs[n + len(extra) + a]
            out[...] = refs[a][...].reshape(out.shape)

    def at_me(shape, axis):
        return lambda i, me: tuple(me[0] if d == axis else 0 for d in range(len(shape)))

    in_specs, out_specs = [], []
    for s, axis in zip(srcs, axes):
        if mode == "gather":
            in_specs.append(pl.BlockSpec(s.shape, lambda i, me, nd=len(s.shape): (0,) * nd))
            out_specs.append(pl.BlockSpec(s.shape, at_me(s.shape, axis)))
        else:
            blk = s.shape[:axis] + (s.shape[axis] // N_DEV,) + s.shape[axis + 1:]
            in_specs.append(pl.BlockSpec(blk, at_me(blk, axis)))
            out_specs.append(pl.BlockSpec((1,) + blk, at_me((1,) + blk, 0)))
    me = _lin(_me()).astype(jnp.int32).reshape(1)
    return pl.pallas_call(
        body, name=name, out_shape=[_S(_land_shape(mode, s, a), s.dtype) for s, a in zip(srcs, axes)],
        grid_spec=pltpu.PrefetchScalarGridSpec(num_scalar_prefetch=1, grid=(1,), in_specs=in_specs + extra_specs,
                                               out_specs=out_specs),
        compiler_params=_params(1))(me, *srcs, *extra)


def _push_start(mode, srcs, lands, axes, name):
    n = len(srcs)

    def body(*refs):
        src_refs, land_refs = refs[:n], refs[n:2 * n]
        send_sems, recv_sems = refs[2 * n], refs[2 * n + 1]
        token = refs[-1]
        x, y, c = _me()
        me = _lin((x, y, c))
        for a in range(n):
            for peer in _peers(x, y, c):
                pltpu.make_async_remote_copy(
                    src_ref=_src_view(mode, src_refs[a], axes[a], _lin(peer)),
                    dst_ref=_dst_view(mode, land_refs[a], axes[a], me),
                    send_sem=send_sems.at[a], recv_sem=recv_sems.at[a], device_id=peer, device_id_type=MESH).start()
        token[...] = jnp.zeros_like(token)

    hbm = lambda s: pltpu.HBM(s.shape, s.dtype)
    outs = pl.pallas_call(
        body, name=name,
        out_shape=(pltpu.SemaphoreType.DMA((n,)), pltpu.SemaphoreType.DMA((n,)), *[hbm(s) for s in srcs], *[hbm(s) for s in lands],
                   _S((SUB, LANE))),
        in_specs=[HBM_SPEC] * (2 * n),
        out_specs=(SEM_SPEC, SEM_SPEC, *[HBM_SPEC] * (2 * n), pl.BlockSpec(memory_space=pltpu.VMEM)),
        input_output_aliases={i: 2 + i for i in range(2 * n)},
        compiler_params=pltpu.CompilerParams(has_side_effects=EFFECT),
    )(*[pltpu.with_memory_space_constraint(s, pltpu.HBM) for s in list(srcs) + list(lands)])
    return outs[0], outs[1], outs[2:2 + n], outs[2 + n:2 + 2 * n], outs[-1]


def _push_wait(mode, send_sems, recv_sems, srcs, lands, axes, after, name):
    n = len(srcs)

    def body(*refs):
        land_refs = refs[n:2 * n]
        send_sems, recv_sems = refs[2 * n], refs[2 * n + 1]
        x, y, c = _me()
        for a in range(n):
            seven = _seven_blocks(mode, land_refs[a], axes[a])
            cp = pltpu.make_async_remote_copy(src_ref=seven, dst_ref=seven, send_sem=send_sems.at[a], recv_sem=recv_sems.at[a],
                                              device_id=(x, y, 1 - c), device_id_type=MESH)
            cp.wait_send()
            cp.wait_recv()

    hbm = lambda s: pltpu.HBM(s.shape, s.dtype)
    outs = pl.pallas_call(
        body, name=name, out_shape=tuple(hbm(s) for s in list(srcs) + list(lands)),
        in_specs=[HBM_SPEC] * (2 * n) + [SEM_SPEC, SEM_SPEC, ANY], out_specs=tuple([HBM_SPEC] * (2 * n)),
        input_output_aliases={i: i for i in range(2 * n)},
        compiler_params=pltpu.CompilerParams(has_side_effects=EFFECT),
    )(*srcs, *lands, send_sems, recv_sems, after)
    return outs[n:]


def _sum_parts(parts):
    n, R, C = parts.shape

    def body(p_ref, o_ref):
        g = p_ref[0]
        for k in range(1, n):
            g = g + p_ref[k]
        o_ref[...] = g

    return pl.pallas_call(body, name="sum_parts", out_shape=_S((R, C)))(parts)


def _block_diag(w, nb):
    tn, r, c = w.shape
    w = w.reshape(tn // nb, nb, r, c)
    return jnp.einsum('tarc,ab->tarbc', w, jnp.eye(nb, dtype=w.dtype)).reshape(tn // nb, nb * r, nb * c)


def _block_diag_extract(w, nb):
    t, R, C = w.shape
    w = w.reshape(t, nb, R // nb, nb, C // nb)
    return jnp.einsum('tarbc,ab->tarc', w, jnp.eye(nb, dtype=w.dtype)).reshape(t * nb, R // nb, C // nb)


SMALL = ['conv_b', 'rg_wa', 'rg_ba', 'rg_wx', 'rg_bx', 'rg_lambda', 's5_a_re', 's5_a_im', 's5_b_re', 's5_b_im',
         's5_c_re', 's5_c_im', 's5_d', 's5_log_step', 's5_b_glu', 'ln1_g', 'ln1_b', 'ple_gate_b', 'ln2_g', 'ln2_b']
WEIGHTS = ['w_in', 'conv_w', 'conv_b', 'rg_wa', 'rg_ba', 'rg_wx', 'rg_bx', 'rg_lambda', 's5_a_re', 's5_a_im', 's5_b_re',
           's5_b_im', 's5_c_re', 's5_c_im', 's5_d', 's5_log_step', 's5_w_glu', 's5_b_glu', 'w_out', 'ln1_g', 'ln1_b',
           'ple_w', 'ple_gate_w', 'ple_gate_b', 'ln2_g', 'ln2_b']
PACK_ROWS_MULT = 64


def _pack(tree):
    flat = jnp.concatenate([tree[k].reshape(-1) for k in SMALL])
    rows = -(-flat.shape[0] // (LANE * PACK_ROWS_MULT)) * PACK_ROWS_MULT
    return jnp.pad(flat, (0, rows * LANE - flat.shape[0])).reshape(rows, LANE)


def _unpack(packed, like):
    flat, out, o = packed.reshape(-1), {}, 0
    for k in SMALL:
        n = math.prod(like[k].shape)
        out[k] = flat[o:o + n].reshape(like[k].shape)
        o += n
    return out


class _NoHooks:
    token = None

    def layer_start(self, i, W, after):
        return W

    def late_weights(self, i, W, after):
        return W

    def post_done(self, i, g):
        return None

    def w_in_done(self, i, g):
        return None

    def layer_done(self, i, g, dx):
        return None


def _local_grads(x, p, target, W, disc, hooks):
    depth = 2
    saved = []
    for i in range(depth):
        if i > 0:
            W = hooks.layer_start(i, W, x)
        w = W[i]
        z = _inproj_fwd(x, w['w_in'], hooks.token if i == 0 else None)
        hs = _rg_fwd(z, w['conv_w'], w['conv_b'], w['wa_bd'], w['wx_bd'], w['rg_ba'], w['rg_bx'], w['rg_lambda'], i)
        d = disc[i]
        y0, s_re, s_im = _s5_fwd(z, d['bb_re'], d['bb_im'], d['lb_re'], d['lb_im'], d['c_re'], d['c_im'], w['s5_d'], i)
        W = hooks.late_weights(i, W, y0)
        w = W[i]
        x2, *norms = _post_fwd(x, hs, z, y0, p[i], w['s5_w_glu'], w['s5_b_glu'], w['w_out'], w['ln1_g'], w['ln1_b'],
                               w['ple_w'], w['ple_gate_w'], w['ple_gate_b'], w['ln2_g'], w['ln2_b'], i)
        saved.append((x, z, hs, y0, s_re, s_im, norms))
        x = x2

    grads = [None] * depth
    dx = target
    loss = None
    token = None
    for i in reversed(range(depth)):
        w, d = W[i], disc[i]
        xin, z, hs, y0, s_re, s_im, (xh1, xh2, m, q, gt, rstd1, rstd2) = saved[i]
        g = {}
        (dt1, g['ple_w'], g['ple_gate_w'], g['ple_gate_b'], g['ln1_g'], g['ln1_b'], g['ln2_g'], g['ln2_b'], lrow) = _post_bwd_a(
            dx, i == depth - 1, xh2, xh1, rstd2, rstd1, q, gt, p[i], w['ple_gate_w'], w['ln1_g'], w['ln1_b'],
            w['ln2_g'], w['ln2_b'], i, token)
        if i == depth - 1:
            loss = 0.5 / D_MODEL * jnp.sum(lrow)
        dhs, dy0, dzg, g['w_out'], g['s5_w_glu'], g['s5_b_glu'] = _post_bwd_b(dt1, m, z, hs, y0, w['w_out'], w['s5_w_glu'],
                                                                           w['s5_b_glu'], i)
        (dzu, g['bb_re'], g['bb_im'], g['lb_re'], g['lb_im'], g['c_re'], g['c_im'], g['s5_d']) = _s5_bwd(
            dy0, z, s_re, s_im, d['bb_re'], d['bb_im'], d['lb_re'], d['lb_im'], d['c_re'], d['c_im'], w['s5_d'], i,
            hooks.post_done(i, g))
        (dzx, g['conv_w'], g['conv_b'], g['wa_bd'], g['wx_bd'], g['rg_ba'], g['rg_bx'], g['rg_lambda']) = _rg_bwd(
            dhs, z, hs, w['conv_w'], w['conv_b'], w['wa_bd'], w['wx_bd'], w['rg_ba'], w['rg_bx'], w['rg_lambda'], i)
        if i == 0:
            g['w_in'] = _inproj_bwd_dw(xin, dzx, dzg, dzu)
            dx = _inproj_bwd_dx(dt1, dzx, dzg, dzu, w['w_in'], hooks.w_in_done(i, g))
        else:
            dx, g['w_in'] = _inproj_bwd(dt1, xin, dzx, dzg, dzu, w['w_in'])
        grads[i] = g
        token = hooks.layer_done(i, g, dx)
    return loss, dx, grads


def _s5_layouts_fwd(s5_a_re, s5_a_im, s5_log_step, s5_b_re, s5_b_im, s5_c_re, s5_c_im):
    depth = s5_a_re.shape[0]
    ar, ai = s5_a_re.reshape(depth * 24, S5_P), s5_a_im.reshape(depth * 24, S5_P)
    ls = s5_log_step.reshape(depth * 24, 1)
    lr, li, cr, ci = _s5_disc_fwd(ar, ai, ls)
    col = lambda a: a.reshape(depth * S5_N, 1)
    br, bi = s5_b_re.reshape(depth * S5_N, 16), s5_b_im.reshape(depth * S5_N, 16)
    bbr, bbi = _s5_bscale_fwd(col(cr), col(ci), br, bi)
    disc = []
    for i in range(depth):
        gph = lambda a: a.reshape(depth, 24, S5_P, 16)[i]
        disc.append(dict(
            bb_re=_block_diag(jnp.swapaxes(gph(bbr), 1, 2), 8), bb_im=_block_diag(jnp.swapaxes(gph(bbi), 1, 2), 8),
            lb_re=lr.reshape(depth, 1, S5_N)[i], lb_im=li.reshape(depth, 1, S5_N)[i],
            c_re=_block_diag(jnp.swapaxes(s5_c_re[i], 1, 2), 8), c_im=_block_diag(jnp.swapaxes(s5_c_im[i], 1, 2), 8)))
    return disc, (ar, ai, ls, col(cr), col(ci), br, bi)


def _s5_layouts_bwd(grads, res):
    ar, ai, ls, cr, ci, br, bi = res
    depth = len(grads)
    stack = lambda f: jnp.stack([f(g) for g in grads])
    dbbr = stack(lambda g: jnp.swapaxes(_block_diag_extract(g['bb_re'], 8), 1, 2)).reshape(depth * S5_N, 16)
    dbbi = stack(lambda g: jnp.swapaxes(_block_diag_extract(g['bb_im'], 8), 1, 2)).reshape(depth * S5_N, 16)
    dbr, dbi, dcr, dci = _s5_bscale_bwd(cr, ci, br, bi, dbbr, dbbi)
    gp = lambda a: a.reshape(depth * 24, S5_P)
    dar, dai, dls = _s5_disc_bwd(ar, ai, ls, gp(stack(lambda g: g['lb_re'])), gp(stack(lambda g: g['lb_im'])), gp(dcr), gp(dci))
    return dict(
        s5_a_re=dar.reshape(depth, 24, S5_P), s5_a_im=dai.reshape(depth, 24, S5_P), s5_log_step=dls.reshape(depth, 24),
        s5_b_re=dbr.reshape(depth, 24, S5_P, 16), s5_b_im=dbi.reshape(depth, 24, S5_P, 16),
        s5_c_re=stack(lambda g: jnp.swapaxes(_block_diag_extract(g['c_re'], 8), 1, 2)),
        s5_c_im=stack(lambda g: jnp.swapaxes(_block_diag_extract(g['c_im'], 8), 1, 2)))


LATE = ('w_out', 'ple_w', 'ple_gate_w', 's5_w_glu')


ROWS = ('conv_b', 'rg_ba', 'rg_bx', 'rg_lambda', 's5_d', 's5_b_glu', 'ln1_g', 'ln1_b', 'ple_gate_b', 'ln2_g', 'ln2_b')


def _shared_weights(full):
    depth = full['conv_b'].shape[0]
    shared = {k: full[k].reshape(depth, 1, -1) for k in ROWS}
    shared['conv_w'] = full['conv_w']
    shared['wa_bd'] = _block_diag(full['rg_wa'].reshape(depth * 10, 64, 64), 2)
    shared['wx_bd'] = _block_diag(full['rg_wx'].reshape(depth * 10, 64, 64), 2)
    return shared


def _layer_weights(full, shared, i):
    return dict(shared, w_in=full['w_in'][i])


class _AllLocal(_NoHooks):
    def __init__(self, full):
        self.full = full

    def late_weights(self, i, W, after):
        W[i].update({k: self.full[k][i] for k in LATE})
        return W


def _full_grads(full, x, p, target, hooks=None):
    disc, res = _s5_layouts_fwd(full['s5_a_re'], full['s5_a_im'], full['s5_log_step'], full['s5_b_re'], full['s5_b_im'],
                                full['s5_c_re'], full['s5_c_im'])
    shared = _shared_weights(full)
    W = [_layer_weights(full, shared, i) for i in range(2)]
    loss, gx, grads = _local_grads(x, p, target, W, disc, hooks or _AllLocal(full))
    stack = lambda f: jnp.stack([f(g) for g in grads])
    out = _s5_layouts_bwd(grads, res)
    for k in SHARD_AXIS:
        out[k] = [g[k] for g in grads]
    out['conv_w'] = stack(lambda g: g['conv_w'])
    for k in ('conv_b', 'rg_ba', 'rg_bx', 'rg_lambda', 's5_b_glu', 'ln1_g', 'ln1_b', 'ple_gate_b', 'ln2_g', 'ln2_b'):
        out[k] = stack(lambda g: g[k][0])
    out['s5_d'] = stack(lambda g: g['s5_d'][0]).reshape(2, 24, 16)
    out['rg_wa'] = stack(lambda g: _block_diag_extract(g['wa_bd'], 2))
    out['rg_wx'] = stack(lambda g: _block_diag_extract(g['wx_bd'], 2))
    return loss, gx, out


SHARD_AXIS = {'w_in': 2, 'w_out': 1, 'ple_w': 2, 'ple_gate_w': 1, 's5_w_glu': 1}


def kernel(x, p, w_in, conv_w, conv_b, rg_wa, rg_ba, rg_wx, rg_bx, rg_lambda, s5_a_re, s5_a_im, s5_b_re, s5_b_im, s5_c_re, s5_c_im, s5_d, s5_log_step, s5_w_glu, s5_b_glu, w_out, ln1_g, ln1_b, ple_w, ple_gate_w, ple_gate_b, ln2_g, ln2_b, loss_target, m_w_in, m_conv_w, m_conv_b, m_rg_wa, m_rg_ba, m_rg_wx, m_rg_bx, m_rg_lambda, m_s5_a_re, m_s5_a_im, m_s5_b_re, m_s5_b_im, m_s5_c_re, m_s5_c_im, m_s5_d, m_s5_log_step, m_s5_w_glu, m_s5_b_glu, m_w_out, m_ln1_g, m_ln1_b, m_ple_w, m_ple_gate_w, m_ple_gate_b, m_ln2_g, m_ln2_b, v_w_in, v_conv_w, v_conv_b, v_rg_wa, v_rg_ba, v_rg_wx, v_rg_bx, v_rg_lambda, v_s5_a_re, v_s5_a_im, v_s5_b_re, v_s5_b_im, v_s5_c_re, v_s5_c_im, v_s5_d, v_s5_log_step, v_s5_w_glu, v_s5_b_glu, v_w_out, v_ln1_g, v_ln1_b, v_ple_w, v_ple_gate_w, v_ple_gate_b, v_ln2_g, v_ln2_b):
    local = dict(locals())
    w = {k: local[k] for k in WEIGHTS}
    mom = {k: local['m_' + k] for k in WEIGHTS}
    var = {k: local['v_' + k] for k in WEIGHTS}

    big = list(SHARD_AXIS)
    wire = {k: w[k].astype(WIRE) for k in big}
    first = _all_gather([wire['w_in'][0][None], conv_w[None]], [0, 0], "gather_first_weights")
    late_axes = [SHARD_AXIS[k] - 1 for k in LATE]
    pushed = {}

    def push_weights(key, srcs, axes, after):
        pushed[key] = _push_start("gather", srcs, _place_own("gather", srcs, axes, "place_weights_" + key, after=after), axes,
                                  "push_weights_" + key)
        return pushed[key][4]

    def await_weights(key, axes, after):
        s = pushed[key]
        return _push_wait("gather", s[0], s[1], s[2], s[3], axes, after, "await_weights_" + key)

    token0 = push_weights("l0", [wire[k][0] for k in LATE], late_axes, first[0])
    push_weights("l1", [wire['w_in'][1][None]] + [wire[k][1] for k in LATE], [0] + late_axes, token0)

    def push_grads(key, g, names, axes):
        srcs = [g[k] for k in names]
        pushed[key] = _push_start("scatter", srcs, _place_own("scatter", srcs, axes, "place_grads_" + key), axes,
                                  "push_grads_" + key)
        return pushed[key][4]

    def await_grads(key, axes, after):
        s = pushed[key]
        return _push_wait("scatter", s[0], s[1], s[2], s[3], axes, after, "await_grads_" + key)

    class Overlap(_NoHooks):
        token = pushed["l1"][4]

        def late_weights(self, i, W, after):
            if i == 0:
                W[0].update(zip(LATE, await_weights("l0", late_axes, after)))
            return W

        def layer_start(self, i, W, after):
            lands = await_weights("l1", [0] + late_axes, after)
            W[1].update(zip(LATE, lands[1:]), w_in=lands[0])
            return W

        def post_done(self, i, g):
            return push_grads("late0", g, LATE, late_axes) if i == 0 else None

        def w_in_done(self, i, g):
            return push_grads("w_in0", g, ['w_in'], [0])

        def layer_done(self, i, g, dx):
            return push_grads("all1", g, ['w_in'] + list(LATE), [0] + late_axes) if i == 1 else None

    hooks = Overlap()
    full = dict(w)
    full['w_in'] = [first[0], None]
    full['conv_w'] = jnp.moveaxis(first[1], 0, 2).reshape(2, 4, RG_W)

    loss, grad_x, g = _full_grads(full, x[0], p[:, 0], loss_target[0], hooks)
    loss = lax.psum(loss, ("x", "y", "c"))

    conv_blocks = jnp.moveaxis(g['conv_w'].reshape(2, 4, N_DEV, RG_W // N_DEV), 2, 0).reshape(N_DEV, 8, RG_W // N_DEV)
    packed = _pack(g)
    token = push_grads("small", dict(conv_w=conv_blocks, small=packed), ['conv_w', 'small'], [0, 0])
    recv1 = dict(zip(['w_in'] + list(LATE), await_grads("all1", [0] + late_axes, grad_x)))
    recv0 = dict(zip(LATE, await_grads("late0", late_axes, grad_x)))
    outs = {}

    def update(k, parts, token=None):
        shard = w[k].shape
        c = shard[-1]
        two = lambda a: a.reshape(-1, c)
        res = _adamw([r.reshape(N_DEV, -1, c) for r in parts], two(w[k]), two(mom[k]), two(var[k]), token)
        outs[k] = [o.reshape(shard) for o in res]

    for k in LATE:
        update(k, [recv0[k], recv1[k]], token)
        token = None
    w_in0, = await_grads("w_in0", [0], outs[LATE[-1]][1])
    update('w_in', [w_in0, recv1['w_in']])
    conv_parts, small_parts = await_grads("small", [0, 0], outs['w_in'][1])
    update('conv_w', [conv_parts])

    rows = packed.shape[0] // N_DEV
    mine = _sum_parts(small_parts.reshape(N_DEV, rows, LANE))
    summed = _unpack(_all_gather([mine], [0], "gather_small_grads")[0], w)
    narrow = ['s5_b_re', 's5_b_im']
    for names, name in ((narrow, "adamw_s5_b"), ([k for k in SMALL if k not in narrow], "adamw_small")):
        delta, new_m, new_v = _adamw_natural(names, summed, w, mom, var, name)
        for k in names:
            outs[k] = [summed[k], delta[k], new_m[k], new_v[k]]

    res = [loss, grad_x[None]]
    for j in range(4):
        res += [outs[k][j] for k in WEIGHTS]
    return tuple(res)
```

```python
import math

import jax
import jax.numpy as jnp
from jax import lax
from jax.experimental import pallas as pl
from jax.experimental.pallas import tpu as pltpu

F32 = jnp.float32
MXU = jnp.bfloat16
WIRE = jnp.bfloat16

N_DEV = 8
D_MODEL = 1024
PLE_D = 256
RG_W = 640
S5_W = 384
S5_P = 64
S5_N = 24 * S5_P
Z_W = 2 * RG_W + 2 * S5_W
C_RGG = RG_W
C_S5U = 2 * RG_W
C_S5G = 2 * RG_W + S5_W
LANE = 128
N_RG_T = RG_W // LANE
N_S5_T = S5_W // LANE
W_BLK = Z_W // N_DEV
ALPHA = (2.0 * 2) ** 0.25
LN_EPS = 1e-5
RG_C = 8.0
LR, B1, B2, EPS, WD, STEP = 0.001, 0.9, 0.999, 1e-08, 0.01, 10
BC1 = 1.0 - B1 ** STEP
BC2 = 1.0 - B2 ** STEP
RC = 256
TM = 256
VMEM_LIMIT = 56 * 1024 * 1024

MESH = pl.DeviceIdType.MESH
ANY = pl.BlockSpec(memory_space=pl.ANY)


def _params(n_grid_axes, vmem=VMEM_LIMIT):
    return pltpu.CompilerParams(dimension_semantics=("arbitrary",) * n_grid_axes, vmem_limit_bytes=vmem)


def _S(shape, dtype=F32):
    return jax.ShapeDtypeStruct(tuple(shape), dtype)


def _sigmoid(x):
    return 0.5 * jnp.tanh(0.5 * x) + 0.5


def _silu_and_grad(x):
    s = _sigmoid(x)
    return x * s, s * (1.0 + x * (1.0 - s))


_GELU_C = math.sqrt(2.0 / math.pi)


def _gelu(x):
    return 0.5 * x * (1.0 + jnp.tanh(_GELU_C * (x + 0.044715 * (x * x * x))))


def _gelu_grad(x):
    th = jnp.tanh(_GELU_C * (x + 0.044715 * (x * x * x)))
    return 0.5 * (1.0 + th) + 0.5 * x * (1.0 - th * th) * (_GELU_C * (1.0 + 3.0 * 0.044715 * (x * x)))


def _mm(a, b):
    return jnp.dot(a.astype(MXU), b.astype(MXU), preferred_element_type=F32)


def _mm_nt(a, b):
    return lax.dot_general(a.astype(MXU), b.astype(MXU), (((1,), (1,)), ((), ())), preferred_element_type=F32)


def _mm_tn(a, b):
    return lax.dot_general(a.astype(MXU), b.astype(MXU), (((0,), (0,)), ((), ())), preferred_element_type=F32)


def _ln_fwd(t, g, b):
    mu = jnp.mean(t, axis=-1, keepdims=True)
    tc = t - mu
    var = jnp.mean(tc * tc, axis=-1, keepdims=True)
    rstd = lax.rsqrt(var + LN_EPS)
    xhat = tc * rstd
    return xhat * g + b, xhat, rstd


def _ln_bwd(dy, xhat, rstd, g):
    dxh = dy * g
    m1 = jnp.mean(dxh, axis=-1, keepdims=True)
    m2 = jnp.mean(dxh * xhat, axis=-1, keepdims=True)
    return rstd * (dxh - m1 - xhat * m2)


def _colsum(a):
    return jnp.sum(a, axis=0, keepdims=True)


def _up(x, d, rows, fill):
    n = x.shape[0]
    return jnp.where(rows < n - d, pltpu.roll(x, n - d, 0), fill)


SUB = 8
TILE_STEPS = (1, 2, 4)


def _r8(width):
    return lax.broadcasted_iota(jnp.int32, (SUB, width), 0)


def _scan_real(a, u, carry, reverse=False):
    r8 = _r8(a.shape[1])
    n = a.shape[0] // SUB
    outs = [None] * n
    for k in (reversed(range(n)) if reverse else range(n)):
        A, U = a[SUB * k:SUB * k + SUB], u[SUB * k:SUB * k + SUB]
        for d in TILE_STEPS:
            m = (r8 < SUB - d) if reverse else (r8 >= d)
            sh = SUB - d if reverse else d
            U = A * jnp.where(m, pltpu.roll(U, sh, 0), 0.0) + U
            A = A * jnp.where(m, pltpu.roll(A, sh, 0), 1.0)
        h = A * carry + U
        outs[k] = h
        carry = h[0:1] if reverse else h[SUB - 1:SUB]
    return jnp.concatenate(outs, axis=0), carry


def _tile_powers(lr, li, reverse=False):
    width = lr.shape[1]
    r8 = _r8(width)
    steps = []
    pr, pi = lr, li
    er, ei = jnp.broadcast_to(lr, (SUB, width)), jnp.broadcast_to(li, (SUB, width))
    for d in TILE_STEPS:
        m = (r8 < SUB - d) if reverse else (r8 >= d)
        sh = SUB - d if reverse else d
        steps.append((sh, jnp.where(m, pr, 0.0), jnp.where(m, pi, 0.0)))
        er, ei = _cmul(er, ei, jnp.where(m, pltpu.roll(er, sh, 0), 1.0), jnp.where(m, pltpu.roll(ei, sh, 0), 0.0))
        pr, pi = _cmul(pr, pi, pr, pi)
    return steps, (er, ei)


def _scan_lti(xr, xi, carry, steps, e, reverse=False):
    er, ei = e
    kr, ki = carry
    n = xr.shape[0] // SUB
    outr, outi = [None] * n, [None] * n
    for k in (reversed(range(n)) if reverse else range(n)):
        sr, si = xr[SUB * k:SUB * k + SUB], xi[SUB * k:SUB * k + SUB]
        for sh, pr, pi in steps:
            shr, shi = pltpu.roll(sr, sh, 0), pltpu.roll(si, sh, 0)
            sr, si = sr + (pr * shr - pi * shi), si + (pr * shi + pi * shr)
        sr = sr + (er * kr - ei * ki)
        si = si + (er * ki + ei * kr)
        outr[k], outi[k] = sr, si
        kr, ki = (sr[0:1], si[0:1]) if reverse else (sr[SUB - 1:SUB], si[SUB - 1:SUB])
    return jnp.concatenate(outr, axis=0), jnp.concatenate(outi, axis=0), (kr, ki)


def _halo(ref, c, r0):
    rp = pl.multiple_of(jnp.maximum(r0 - 8, 0), 8)
    return jnp.where(c > 0, ref[pl.ds(rp, 8), :], 0.0)


def _conv_taps(xe):
    return [pltpu.roll(xe, 3, 0)[8:, :], pltpu.roll(xe, 2, 0)[8:, :], pltpu.roll(xe, 1, 0)[8:, :], xe[8:, :]]


def _rg_gates(h, wa, wx, ba, bx, sp):
    r = _sigmoid(_mm(h, wa) + ba)
    i = _sigmoid(_mm(h, wx) + bx)
    log_a = (-RG_C) * r * sp
    a = jnp.exp(log_a)
    mult = jnp.sqrt(-jnp.tanh(log_a) * (a * a + 1.0))
    return r, i, a, mult


def _softplus(y):
    return jnp.maximum(y, 0.0) + jnp.log1p(jnp.exp(-jnp.abs(y)))


def _after(token):
    return ([], []) if token is None else ([token], [ANY])


def _inproj_fwd(x, w_in, token=None):
    L = x.shape[0]

    def body(x_ref, w_ref, *rest):
        xb = x_ref[...].astype(MXU)
        for j in range(N_DEV):
            rest[-1][:, j * W_BLK:(j + 1) * W_BLK] = jnp.dot(xb, w_ref[j].astype(MXU), preferred_element_type=F32)

    extra, extra_specs = _after(token)
    return pl.pallas_call(
        body, name="inproj_fwd", grid=(L // TM,),
        in_specs=[pl.BlockSpec((TM, D_MODEL), lambda i: (i, 0)),
                  pl.BlockSpec((N_DEV, D_MODEL, W_BLK), lambda i: (0, 0, 0))] + extra_specs,
        out_specs=pl.BlockSpec((TM, Z_W), lambda i: (i, 0)),
        out_shape=_S((L, Z_W)), compiler_params=_params(1))(x, w_in, *extra)


def _inproj_bwd(dt1, x, dzx, dzg, dzu, w_in):
    L = x.shape[0]

    def body(dt1_ref, x_ref, dzx_ref, dzg_ref, dzu_ref, w_ref, dx_ref, dw_ref, acc_ref):
        @pl.when(pl.program_id(0) == 0)
        def _():
            acc_ref[...] = jnp.zeros_like(acc_ref)
        dzg = dzg_ref[...]
        dz = jnp.concatenate([dzx_ref[...], dzg[:, :RG_W], dzu_ref[...], dzg[:, RG_W:]], axis=1).astype(MXU)
        xb = x_ref[...].astype(MXU)
        dx = ALPHA * dt1_ref[...]
        for j in range(N_DEV):
            dzj = dz[:, j * W_BLK:(j + 1) * W_BLK]
            dx = dx + _mm_nt(dzj, w_ref[j])
            acc_ref[j] += _mm_tn(xb, dzj)
        dx_ref[...] = dx

        @pl.when(pl.program_id(0) == L // TM - 1)
        def _():
            dw_ref[...] = acc_ref[...].astype(WIRE)

    row = lambda w: pl.BlockSpec((TM, w), lambda i: (i, 0))
    wspec = pl.BlockSpec((N_DEV, D_MODEL, W_BLK), lambda i: (0, 0, 0))
    return pl.pallas_call(
        body, name="inproj_bwd", grid=(L // TM,),
        in_specs=[row(D_MODEL), row(D_MODEL), row(RG_W), row(D_MODEL), row(S5_W), wspec],
        out_specs=[row(D_MODEL), wspec],
        out_shape=[_S((L, D_MODEL)), _S((N_DEV, D_MODEL, W_BLK), WIRE)],
        scratch_shapes=[pltpu.VMEM((N_DEV, D_MODEL, W_BLK), F32)],
        compiler_params=_params(1))(dt1, x, dzx, dzg, dzu, w_in)


TM2 = 512


def _dz_block(dzx_ref, dzg_ref, dzu_ref):
    dzg = dzg_ref[...]
    return jnp.concatenate([dzx_ref[...], dzg[:, :RG_W], dzu_ref[...], dzg[:, RG_W:]], axis=1).astype(MXU)


def _inproj_bwd_dw(x, dzx, dzg, dzu):
    L = x.shape[0]

    def body(x_ref, dzx_ref, dzg_ref, dzu_ref, dw_ref, acc_ref):
        @pl.when(pl.program_id(0) == 0)
        def _():
            acc_ref[...] = jnp.zeros_like(acc_ref)
        dz = _dz_block(dzx_ref, dzg_ref, dzu_ref)
        xb = x_ref[...].astype(MXU)
        for j in range(N_DEV):
            acc_ref[j] += _mm_tn(xb, dz[:, j * W_BLK:(j + 1) * W_BLK])

        @pl.when(pl.program_id(0) == L // TM2 - 1)
        def _():
            dw_ref[...] = acc_ref[...].astype(WIRE)

    row = lambda w: pl.BlockSpec((TM2, w), lambda i: (i, 0))
    wspec = pl.BlockSpec((N_DEV, D_MODEL, W_BLK), lambda i: (0, 0, 0))
    return pl.pallas_call(
        body, name="inproj_bwd_dw", grid=(L // TM2,),
        in_specs=[row(D_MODEL), row(RG_W), row(D_MODEL), row(S5_W)], out_specs=wspec,
        out_shape=_S((N_DEV, D_MODEL, W_BLK), WIRE), scratch_shapes=[pltpu.VMEM((N_DEV, D_MODEL, W_BLK), F32)],
        compiler_params=_params(1))(x, dzx, dzg, dzu)


def _inproj_bwd_dx(dt1, dzx, dzg, dzu, w_in, token=None):
    L = dt1.shape[0]
    extra, extra_specs = _after(token)

    def body(dt1_ref, dzx_ref, dzg_ref, dzu_ref, w_ref, *rest):
        dz = _dz_block(dzx_ref, dzg_ref, dzu_ref)
        dx = ALPHA * dt1_ref[...]
        for j in range(N_DEV):
            dx = dx + _mm_nt(dz[:, j * W_BLK:(j + 1) * W_BLK], w_ref[j])
        rest[-1][...] = dx

    row = lambda w: pl.BlockSpec((TM2, w), lambda i: (i, 0))
    wspec = pl.BlockSpec((N_DEV, D_MODEL, W_BLK), lambda i: (0, 0, 0))
    return pl.pallas_call(
        body, name="inproj_bwd_dx", grid=(L // TM2,),
        in_specs=[row(D_MODEL), row(RG_W), row(D_MODEL), row(S5_W), wspec] + extra_specs, out_specs=row(D_MODEL),
        out_shape=_S((L, D_MODEL)), compiler_params=_params(1))(dt1, dzx, dzg, dzu, w_in, *extra)


def _rg_specs(layer):
    tile = lambda rows: pl.BlockSpec((rows, LANE), lambda c: (0, c))
    ptile = lambda rows: pl.BlockSpec((None, rows, LANE), lambda c: (layer, 0, c))
    pbd = pl.BlockSpec((None, LANE, LANE), lambda c: (layer * N_RG_T + c, 0, 0))
    return tile, ptile, pbd, pl.BlockSpec((None, LANE, LANE), lambda c: (c, 0, 0))


def _rg_fwd(z, cw, cb, wa_bd, wx_bd, ba, bx, lam, layer):
    L = z.shape[0]

    def body(x_ref, cw_ref, cb_ref, wa_ref, wx_ref, ba_ref, bx_ref, lam_ref, hs_ref):
        w, b = cw_ref[...], cb_ref[...]
        wa, wx, ba_, bx_ = wa_ref[...].astype(MXU), wx_ref[...].astype(MXU), ba_ref[...], bx_ref[...]
        sp = _softplus(-lam_ref[...])

        def step(c, carry):
            r0 = pl.multiple_of(c * RC, RC)
            xe = jnp.concatenate([_halo(x_ref, c, r0), x_ref[pl.ds(r0, RC), :]], axis=0)
            t = _conv_taps(xe)
            h = t[0] * w[0:1] + t[1] * w[1:2] + t[2] * w[2:3] + t[3] * w[3:4] + b
            _, i, a, mult = _rg_gates(h, wa, wx, ba_, bx_, sp)
            hs, carry = _scan_real(a, mult * (i * h), carry)
            hs_ref[pl.ds(r0, RC), :] = hs
            return carry

        lax.fori_loop(0, L // RC, step, jnp.zeros((1, LANE), F32))

    tile, ptile, pbd, _ = _rg_specs(layer)
    return pl.pallas_call(
        body, name="rg_fwd", grid=(N_RG_T,),
        in_specs=[tile(L), ptile(4), ptile(1), pbd, pbd, ptile(1), ptile(1), ptile(1)],
        out_specs=tile(L), out_shape=_S((L, RG_W)), compiler_params=_params(1))(z, cw, cb, wa_bd, wx_bd, ba, bx, lam)


def _rg_bwd(dhs, z, hs, cw, cb, wa_bd, wx_bd, ba, bx, lam, layer):
    L = z.shape[0]

    def body(g_ref, x_ref, hs_ref, cw_ref, cb_ref, wa_ref, wx_ref, ba_ref, bx_ref, lam_ref,
             dx_ref, dcw_ref, dcb_ref, dwa_ref, dwx_ref, dba_ref, dbx_ref, dlam_ref):
        w, b = cw_ref[...], cb_ref[...]
        wa, wx, ba_, bx_ = wa_ref[...].astype(MXU), wx_ref[...].astype(MXU), ba_ref[...], bx_ref[...]
        lam = lam_ref[...]
        sp = _softplus(-lam)
        rows = lax.broadcasted_iota(jnp.int32, (RC, LANE), 0)
        for ref in (dcw_ref, dcb_ref, dwa_ref, dwx_ref, dba_ref, dbx_ref, dlam_ref):
            ref[...] = jnp.zeros_like(ref)
        nch = L // RC

        def step(k, carry):
            cin, nxt = carry
            c = nch - 1 - k
            r0 = pl.multiple_of(c * RC, RC)
            xe = jnp.concatenate([_halo(x_ref, c, r0), x_ref[pl.ds(r0, RC), :]], axis=0)
            t = _conv_taps(xe)
            h = t[0] * w[0:1] + t[1] * w[1:2] + t[2] * w[2:3] + t[3] * w[3:4] + b
            r, i, a, mult = _rg_gates(h, wa, wx, ba_, bx_, sp)
            hs_e = jnp.concatenate([_halo(hs_ref, c, r0), hs_ref[pl.ds(r0, RC), :]], axis=0)
            hs_prev = pltpu.roll(hs_e, 1, 0)[8:, :]
            g = g_ref[pl.ds(r0, RC), :]
            cc, cin_new = _scan_real(a, a * g, cin, reverse=True)
            dh = g + _up(cc, 1, rows, cin)
            ih = i * h
            dlog_a = dh * hs_prev * a - (dh * ih) * (a * a) / mult
            di = dh * mult * h
            dhin = dh * mult * i
            dr = dlog_a * ((-RG_C) * sp)
            dlam_ref[...] += _colsum(dlog_a * r)
            dra = dr * r * (1.0 - r)
            dia = di * i * (1.0 - i)
            dwa_ref[...] += _mm_tn(h, dra)
            dwx_ref[...] += _mm_tn(h, dia)
            dba_ref[...] += _colsum(dra)
            dbx_ref[...] += _colsum(dia)
            dhin = dhin + _mm_nt(dra, wa) + _mm_nt(dia, wx)
            de = jnp.concatenate([dhin, nxt], axis=0)
            n = RC + 8
            dx = (dhin * w[3:4] + pltpu.roll(de, n - 1, 0)[:RC, :] * w[2:3]
                  + pltpu.roll(de, n - 2, 0)[:RC, :] * w[1:2] + pltpu.roll(de, n - 3, 0)[:RC, :] * w[0:1])
            dx_ref[pl.ds(r0, RC), :] = dx
            for kk in range(4):
                dcw_ref[kk:kk + 1, :] += _colsum(dhin * t[kk])
            dcb_ref[...] += _colsum(dhin)
            return cin_new, dhin[0:8, :]

        lax.fori_loop(0, nch, step, (jnp.zeros((1, LANE), F32), jnp.zeros((8, LANE), F32)))
        dlam_ref[...] = dlam_ref[...] * (RG_C * _sigmoid(-lam))

    tile, ptile, pbd, bd = _rg_specs(layer)
    return pl.pallas_call(
        body, name="rg_bwd", grid=(N_RG_T,),
        in_specs=[tile(L), tile(L), tile(L), ptile(4), ptile(1), pbd, pbd, ptile(1), ptile(1), ptile(1)],
        out_specs=[tile(L), tile(4), tile(1), bd, bd, tile(1), tile(1), tile(1)],
        out_shape=[_S((L, RG_W)), _S((4, RG_W)), _S((1, RG_W)), _S((N_RG_T, LANE, LANE)), _S((N_RG_T, LANE, LANE)),
                   _S((1, RG_W)), _S((1, RG_W)), _S((1, RG_W))],
        compiler_params=_params(1))(dhs, z, hs, cw, cb, wa_bd, wx_bd, ba, bx, lam)


def _cmul(ar, ai, br, bi):
    return ar * br - ai * bi, ar * bi + ai * br


S5_TW = S5_N // N_S5_T


def _s5_specs(L):
    in_tile = pl.BlockSpec((L, LANE), lambda t: (0, t))
    st = pl.BlockSpec((L, S5_TW), lambda t: (0, t))
    bb = pl.BlockSpec((None, LANE, S5_TW), lambda t: (t, 0, 0))
    cc = pl.BlockSpec((None, S5_TW, LANE), lambda t: (t, 0, 0))
    lb = pl.BlockSpec((1, S5_TW), lambda t: (0, t))
    dv = pl.BlockSpec((1, LANE), lambda t: (0, t))
    return in_tile, st, bb, cc, lb, dv


def _layer_row_tile(layer):
    return pl.BlockSpec((None, 1, LANE), lambda t: (layer, 0, t))


def _s5_fwd(z, bb_re, bb_im, lb_re, lb_im, c_re, c_im, dvec, layer):
    L = z.shape[0]

    def body(u_ref, bbr_ref, bbi_ref, lr_ref, li_ref, cr_ref, ci_ref, d_ref, y_ref, sr_ref, si_ref):
        bbr, bbi = bbr_ref[...].astype(MXU), bbi_ref[...].astype(MXU)
        cr, ci = cr_ref[...].astype(MXU), ci_ref[...].astype(MXU)
        dv = d_ref[...]
        steps, e = _tile_powers(lr_ref[...], li_ref[...])

        def step(c, carry):
            r0 = pl.multiple_of(c * RC, RC)
            u = u_ref[pl.ds(r0, RC), :]
            ub = u.astype(MXU)
            sr = jnp.dot(ub, bbr, preferred_element_type=F32)
            si = jnp.dot(ub, bbi, preferred_element_type=F32)
            sr, si, carry = _scan_lti(sr, si, carry, steps, e)
            sr_ref[pl.ds(r0, RC), :] = sr
            si_ref[pl.ds(r0, RC), :] = si
            y_ref[pl.ds(r0, RC), :] = dv * u + (_mm(sr, cr) - _mm(si, ci))
            return carry

        zero = jnp.zeros((1, S5_TW), F32)
        lax.fori_loop(0, L // RC, step, (zero, zero))

    in_tile, st, bb, cc, lb, dv = _s5_specs(L)
    u_tile = pl.BlockSpec((L, LANE), lambda t: (0, C_S5U // LANE + t))
    return pl.pallas_call(
        body, name="s5_fwd", grid=(N_S5_T,),
        in_specs=[u_tile, bb, bb, lb, lb, cc, cc, _layer_row_tile(layer)],
        out_specs=[in_tile, st, st],
        out_shape=[_S((L, S5_W)), _S((L, S5_N)), _S((L, S5_N))],
        compiler_params=_params(1))(z, bb_re, bb_im, lb_re, lb_im, c_re, c_im, dvec)


def _s5_bwd(dy0, z, s_re, s_im, bb_re, bb_im, lb_re, lb_im, c_re, c_im, dvec, layer, token=None):
    L = z.shape[0]
    extra, extra_specs = _after(token)

    def body(dy_ref, u_ref, sr_ref, si_ref, bbr_ref, bbi_ref, lr_ref, li_ref, cr_ref, ci_ref, d_ref, *rest):
        du_ref, dbbr_ref, dbbi_ref, dlr_ref, dli_ref, dcr_ref, dci_ref, dd_ref = rest[len(extra):]
        bbr, bbi = bbr_ref[...].astype(MXU), bbi_ref[...].astype(MXU)
        cr, ci = cr_ref[...].astype(MXU), ci_ref[...].astype(MXU)
        lr, li = lr_ref[...], -li_ref[...]
        dv = d_ref[...]
        steps, e = _tile_powers(lr, li, reverse=True)
        for ref in (dbbr_ref, dbbi_ref, dlr_ref, dli_ref, dcr_ref, dci_ref, dd_ref):
            ref[...] = jnp.zeros_like(ref)
        nch = L // RC

        def step(k, carry):
            c = nch - 1 - k
            r0 = pl.multiple_of(c * RC, RC)
            dy = dy_ref[pl.ds(r0, RC), :]
            u = u_ref[pl.ds(r0, RC), :]
            dyb, ub = dy.astype(MXU), u.astype(MXU)
            sr, si = sr_ref[pl.ds(r0, RC), :], si_ref[pl.ds(r0, RC), :]
            dcr_ref[...] += _mm_tn(sr, dyb)
            dci_ref[...] -= _mm_tn(si, dyb)
            gr = _mm_nt(dyb, cr)
            gi = -_mm_nt(dyb, ci)
            gr, gi, carry = _scan_lti(gr, gi, carry, steps, e, reverse=True)
            pr_ = pltpu.roll(jnp.concatenate([_halo(sr_ref, c, r0), sr], axis=0), 1, 0)[8:, :]
            pi_ = pltpu.roll(jnp.concatenate([_halo(si_ref, c, r0), si], axis=0), 1, 0)[8:, :]
            dlr_ref[...] += _colsum(pr_ * gr + pi_ * gi)
            dli_ref[...] += _colsum(pr_ * gi - pi_ * gr)
            grb, gib = gr.astype(MXU), gi.astype(MXU)
            dbbr_ref[...] += _mm_tn(ub, grb)
            dbbi_ref[...] += _mm_tn(ub, gib)
            du_ref[pl.ds(r0, RC), :] = dv * dy + (_mm_nt(grb, bbr) + _mm_nt(gib, bbi))
            dd_ref[...] += _colsum(dy * u)
            return carry

        zero = jnp.zeros((1, S5_TW), F32)
        lax.fori_loop(0, nch, step, (zero, zero))

    in_tile, st, bb, cc, lb, dv = _s5_specs(L)
    u_tile = pl.BlockSpec((L, LANE), lambda t: (0, C_S5U // LANE + t))
    return pl.pallas_call(
        body, name="s5_bwd", grid=(N_S5_T,),
        in_specs=[in_tile, u_tile, st, st, bb, bb, lb, lb, cc, cc, _layer_row_tile(layer)] + extra_specs,
        out_specs=[in_tile, bb, bb, lb, lb, cc, cc, dv],
        out_shape=[_S((L, S5_W)), _S((N_S5_T, LANE, S5_TW)), _S((N_S5_T, LANE, S5_TW)), _S((1, S5_N)), _S((1, S5_N)),
                   _S((N_S5_T, S5_TW, LANE)), _S((N_S5_T, S5_TW, LANE)), _S((1, S5_W))],
        compiler_params=_params(1))(dy0, z, s_re, s_im, bb_re, bb_im, lb_re, lb_im, c_re, c_im, dvec, *extra)


def _disc(ar, ai, ls):
    dt = jnp.exp(ls)
    mag = jnp.exp(ar * dt)
    lr = mag * jnp.cos(ai * dt)
    li = mag * jnp.sin(ai * dt)
    den = ar * ar + ai * ai
    cr = ((lr - 1.0) * ar + li * ai) / den
    ci = (li * ar - (lr - 1.0) * ai) / den
    return lr, li, cr, ci


def _s5_disc_fwd(ar, ai, ls, token=None):
    extra, extra_specs = _after(token)

    def body(ar_ref, ai_ref, ls_ref, *rest):
        lr_ref, li_ref, cr_ref, ci_ref = rest[len(extra):]
        lr, li, cr, ci = _disc(ar_ref[...], ai_ref[...], ls_ref[...])
        lr_ref[...], li_ref[...], cr_ref[...], ci_ref[...] = lr, li, cr, ci

    sh = _S(ar.shape)
    vm = pl.BlockSpec(memory_space=pltpu.VMEM)
    return pl.pallas_call(body, name="s5_disc_fwd", in_specs=[vm, vm, vm] + extra_specs, out_shape=[sh, sh, sh, sh])(
        ar, ai, ls, *extra)


def _s5_disc_bwd(ar, ai, ls, dlr, dli, dcr, dci):
    def body(ar_ref, ai_ref, ls_ref, dlr_ref, dli_ref, dcr_ref, dci_ref, dar_ref, dai_ref, dls_ref):
        _, vjp = jax.vjp(_disc, ar_ref[...], ai_ref[...], jnp.broadcast_to(ls_ref[...], ar_ref.shape))
        dar, dai, dls = vjp((dlr_ref[...], dli_ref[...], dcr_ref[...], dci_ref[...]))
        dar_ref[...], dai_ref[...] = dar, dai
        dls_ref[...] = jnp.sum(dls, axis=1, keepdims=True)

    return pl.pallas_call(body, name="s5_disc_bwd", out_shape=[_S(ar.shape), _S(ar.shape), _S(ls.shape)])(
        ar, ai, ls, dlr, dli, dcr, dci)


def _s5_bscale_fwd(cr, ci, br, bi):
    def body(cr_ref, ci_ref, br_ref, bi_ref, or_ref, oi_ref):
        or_ref[...], oi_ref[...] = _cmul(cr_ref[...], ci_ref[...], br_ref[...], bi_ref[...])

    return pl.pallas_call(body, name="s5_bscale_fwd", out_shape=[_S(br.shape), _S(br.shape)])(cr, ci, br, bi)


def _s5_bscale_bwd(cr, ci, br, bi, gr, gi):
    def body(cr_ref, ci_ref, br_ref, bi_ref, gr_ref, gi_ref, dbr_ref, dbi_ref, dcr_ref, dci_ref):
        cr_, ci_, br_, bi_, gr_, gi_ = (r[...] for r in (cr_ref, ci_ref, br_ref, bi_ref, gr_ref, gi_ref))
        dbr_ref[...] = cr_ * gr_ + ci_ * gi_
        dbi_ref[...] = cr_ * gi_ - ci_ * gr_
        dcr_ref[...] = jnp.sum(gr_ * br_ + gi_ * bi_, axis=1, keepdims=True)
        dci_ref[...] = jnp.sum(gi_ * br_ - gr_ * bi_, axis=1, keepdims=True)

    return pl.pallas_call(body, name="s5_bscale_bwd",
                          out_shape=[_S(br.shape), _S(br.shape), _S(cr.shape), _S(cr.shape)])(cr, ci, br, bi, gr, gi)


def _row(w):
    return pl.BlockSpec((TM, w), lambda i: (i, 0))


def _full(shape):
    return pl.BlockSpec(tuple(shape), lambda i: (0,) * len(shape))


def _lrow(layer, width):
    return pl.BlockSpec((None, 1, width), lambda i: (layer, 0, 0))


def _post_fwd(x, hs, z, y0, p, w_glu, b_glu, w_out, g1, b1, ple_w, w_pg, b_pg, g2, b2, layer):
    L = x.shape[0]

    def body(x_ref, hs_ref, z_ref, y0_ref, p_ref, wg_ref, bg_ref, wo_ref, g1_ref, b1_ref, pw_ref, wpg_ref, bpg_ref,
             g2_ref, b2_ref, x2_ref, xh1_ref, xh2_ref, m_ref, q_ref, gt_ref, rstd1_ref, rstd2_ref):
        rg_gate = z_ref[:, C_RGG:C_RGG + RG_W]
        s5_gate = z_ref[:, C_S5G:C_S5G + S5_W]
        rg_y = hs_ref[...] * _silu_and_grad(rg_gate)[0]
        y1 = _gelu(y0_ref[...])
        gl = _sigmoid(_mm(y1, wg_ref[...]) + bg_ref[...])
        s5_y = (y1 * gl) * _silu_and_grad(s5_gate)[0]
        m_ref[:, :RG_W] = rg_y
        m_ref[:, RG_W:] = s5_y
        mix = _mm(m_ref[...], wo_ref[...])
        t1 = ALPHA * x_ref[...] + mix
        x1, xh1, rstd1 = _ln_fwd(t1, g1_ref[...], b1_ref[...])
        q = _mm(p_ref[...], pw_ref[...])
        gt = _sigmoid(_mm(x1, wpg_ref[...]) + bpg_ref[...])
        t2 = ALPHA * x1 + q * gt
        x2, xh2, rstd2 = _ln_fwd(t2, g2_ref[...], b2_ref[...])
        x2_ref[...], xh1_ref[...], xh2_ref[...], q_ref[...], gt_ref[...] = x2, xh1, xh2, q, gt
        rstd1_ref[...], rstd2_ref[...] = rstd1, rstd2

    vec = _lrow(layer, D_MODEL)
    return pl.pallas_call(
        body, name="post_fwd", grid=(L // TM,),
        in_specs=[_row(D_MODEL), _row(RG_W), _row(Z_W), _row(S5_W), _row(PLE_D), _full((S5_W, S5_W)), _lrow(layer, S5_W),
                  _full((D_MODEL, D_MODEL)), vec, vec, _full((PLE_D, D_MODEL)), _full((D_MODEL, D_MODEL)), vec, vec, vec],
        out_specs=[_row(D_MODEL)] * 6 + [_row(1)] * 2, out_shape=[_S((L, D_MODEL))] * 6 + [_S((L, 1))] * 2,
        compiler_params=_params(1))(x, hs, z, y0, p, w_glu, b_glu, w_out, g1, b1, ple_w, w_pg, b_pg, g2, b2)


def _post_bwd_a(dx2_or_target, is_top, xh2, xh1, rstd2, rstd1, q, gt, p, w_pg, g1, b1, g2, b2, layer, token=None):
    L = xh1.shape[0]
    extra, extra_specs = _after(token)

    def body(d_ref, xh2_ref, xh1_ref, rstd2_ref, rstd1_ref, q_ref, gt_ref, p_ref, wpg_ref, g1_ref, b1_ref, g2_ref,
             b2_ref, *rest):
        (dt1_ref, dpw_out, dwpg_out, dbpg_ref, dg1_ref, db1_ref, dg2_ref, db2_ref, loss_ref, dpw_ref,
         dwpg_ref) = rest[len(extra):]
        @pl.when(pl.program_id(0) == 0)
        def _():
            for ref in (dpw_ref, dwpg_ref, dbpg_ref, dg1_ref, db1_ref, dg2_ref, db2_ref, loss_ref):
                ref[...] = jnp.zeros_like(ref)

        g1, g2 = g1_ref[...], g2_ref[...]
        xh1, xh2, rstd1, rstd2 = xh1_ref[...], xh2_ref[...], rstd1_ref[...], rstd2_ref[...]
        x1 = xh1 * g1 + b1_ref[...]
        if is_top:
            err = (xh2 * g2 + b2_ref[...]) - d_ref[...]
            loss_ref[...] += _colsum(err * err)
            dx2 = err * (1.0 / D_MODEL)
        else:
            dx2 = d_ref[...]
        p = p_ref[...]
        q, gt = q_ref[...], gt_ref[...]
        dg2_ref[...] += _colsum(dx2 * xh2)
        db2_ref[...] += _colsum(dx2)
        dt2 = _ln_bwd(dx2, xh2, rstd2, g2)
        dq = dt2 * gt
        dgpre = (dt2 * q) * gt * (1.0 - gt)
        dpw_ref[...] += _mm_tn(p, dq)
        dwpg_ref[...] += _mm_tn(x1, dgpre)
        dbpg_ref[...] += _colsum(dgpre)
        dx1 = ALPHA * dt2 + _mm_nt(dgpre, wpg_ref[...])
        dg1_ref[...] += _colsum(dx1 * xh1)
        db1_ref[...] += _colsum(dx1)
        dt1_ref[...] = _ln_bwd(dx1, xh1, rstd1, g1)

        @pl.when(pl.program_id(0) == L // TM - 1)
        def _():
            dpw_out[...] = dpw_ref[...].astype(WIRE)
            dwpg_out[...] = dwpg_ref[...].astype(WIRE)

    vec, lvec = _full((1, D_MODEL)), _lrow(layer, D_MODEL)
    return pl.pallas_call(
        body, name="post_bwd_a_top" if is_top else "post_bwd_a", grid=(L // TM,),
        in_specs=[_row(D_MODEL), _row(D_MODEL), _row(D_MODEL), _row(1), _row(1), _row(D_MODEL), _row(D_MODEL), _row(PLE_D),
                  _full((D_MODEL, D_MODEL)), lvec, lvec, lvec, lvec] + extra_specs,
        out_specs=[_row(D_MODEL), _full((PLE_D, D_MODEL)), _full((D_MODEL, D_MODEL)), vec, vec, vec, vec, vec, vec],
        out_shape=[_S((L, D_MODEL)), _S((PLE_D, D_MODEL), WIRE), _S((D_MODEL, D_MODEL), WIRE)] + [_S((1, D_MODEL))] * 6,
        scratch_shapes=[pltpu.VMEM((PLE_D, D_MODEL), F32), pltpu.VMEM((D_MODEL, D_MODEL), F32)],
        compiler_params=_params(1))(dx2_or_target, xh2, xh1, rstd2, rstd1, q, gt, p, w_pg, g1, b1, g2, b2, *extra)


def _post_bwd_b(dt1, m, z, hs, y0, w_out, w_glu, b_glu, layer):
    L = dt1.shape[0]

    def body(dt1_ref, m_ref, z_ref, hs_ref, y0_ref, wo_ref, wg_ref, bg_ref,
             dhs_ref, dy0_ref, dzg_ref, dwo_out, dwg_out, dbg_ref, dwo_ref, dwg_ref):
        @pl.when(pl.program_id(0) == 0)
        def _():
            for ref in (dwo_ref, dwg_ref, dbg_ref):
                ref[...] = jnp.zeros_like(ref)

        dt1b = dt1_ref[...].astype(MXU)
        dm = _mm_nt(dt1b, wo_ref[...])
        dwo_ref[...] += _mm_tn(m_ref[...], dt1b)
        d_rgy, d_s5y = dm[:, :RG_W], dm[:, RG_W:]
        rg_gate = z_ref[:, C_RGG:C_RGG + RG_W]
        s5_gate = z_ref[:, C_S5G:C_S5G + S5_W]
        sl, dsl = _silu_and_grad(rg_gate)
        dhs_ref[...] = d_rgy * sl
        dzg_ref[:, :RG_W] = d_rgy * hs_ref[...] * dsl
        y0 = y0_ref[...]
        y1 = _gelu(y0)
        gl = _sigmoid(_mm(y1, wg_ref[...]) + bg_ref[...])
        sl, dsl = _silu_and_grad(s5_gate)
        dy2 = d_s5y * sl
        dzg_ref[:, RG_W:] = d_s5y * (y1 * gl) * dsl
        dglpre = (dy2 * y1) * gl * (1.0 - gl)
        dwg_ref[...] += _mm_tn(y1, dglpre)
        dbg_ref[...] += _colsum(dglpre)
        dy1 = dy2 * gl + _mm_nt(dglpre, wg_ref[...])
        dy0_ref[...] = dy1 * _gelu_grad(y0)

        @pl.when(pl.program_id(0) == L // TM - 1)
        def _():
            dwo_out[...] = dwo_ref[...].astype(WIRE)
            dwg_out[...] = dwg_ref[...].astype(WIRE)

    return pl.pallas_call(
        body, name="post_bwd_b", grid=(L // TM,),
        in_specs=[_row(D_MODEL), _row(D_MODEL), _row(Z_W), _row(RG_W), _row(S5_W), _full((D_MODEL, D_MODEL)),
                  _full((S5_W, S5_W)), _lrow(layer, S5_W)],
        out_specs=[_row(RG_W), _row(S5_W), _row(D_MODEL), _full((D_MODEL, D_MODEL)), _full((S5_W, S5_W)), _full((1, S5_W))],
        out_shape=[_S((L, RG_W)), _S((L, S5_W)), _S((L, D_MODEL)), _S((D_MODEL, D_MODEL), WIRE), _S((S5_W, S5_W), WIRE),
                   _S((1, S5_W))],
        scratch_shapes=[pltpu.VMEM((D_MODEL, D_MODEL), F32), pltpu.VMEM((S5_W, S5_W), F32)],
        compiler_params=_params(1))(dt1, m, z, hs, y0, w_out, w_glu, b_glu)


def _adamw(parts, w, m, v, token=None):
    nl = len(parts)
    extra, extra_specs = _after(token)
    n, R, C = parts[0].shape
    tr = R
    for cand in (512, 256, 128, 64, 32, 16, 8):
        if R % cand == 0 and n * cand * C * 4 <= 4 * 1024 * 1024:
            tr = cand
            break
    nblk = R // tr

    def body(*refs):
        p_refs = refs[:nl]
        w_ref, m_ref, v_ref = refs[nl:nl + 3]
        g_ref, d_ref, nm_ref, nv_ref = refs[nl + 3 + len(extra):]
        layer = pl.program_id(0)
        g = None
        for li, p_ref in enumerate(p_refs):
            s = p_ref[0].astype(F32)
            for k in range(1, n):
                s = s + p_ref[k].astype(F32)
            g = s if g is None else jnp.where(layer == li, s, g)
        nm = B1 * m_ref[...] + (1.0 - B1) * g
        nv = B2 * v_ref[...] + (1.0 - B2) * (g * g)
        d_ref[...] = (-LR) * ((nm / BC1) / (jnp.sqrt(nv / BC2) + EPS) + WD * w_ref[...])
        g_ref[...], nm_ref[...], nv_ref[...] = g, nm, nv

    def part_spec(li):
        return pl.BlockSpec((n, tr, C), lambda l, i: (0, jnp.where(l == li, i, jnp.where(l < li, 0, nblk - 1)), 0))

    blk = pl.BlockSpec((tr, C), lambda l, i: (l * nblk + i, 0))
    return pl.pallas_call(
        body, name="adamw", grid=(nl, nblk),
        in_specs=[part_spec(li) for li in range(nl)] + [blk, blk, blk] + extra_specs,
        out_specs=[blk] * 4, out_shape=[_S((nl * R, C))] * 4, compiler_params=_params(2))(*parts, w, m, v, *extra)


def _adamw_natural(names, g, w, m, v, name):
    n = len(names)

    def body(*refs):
        for j in range(n):
            g_ref, w_ref, m_ref, v_ref, d_ref, nm_ref, nv_ref = (refs[k * n + j] for k in range(7))
            gj = g_ref[...]
            nm = B1 * m_ref[...] + (1.0 - B1) * gj
            nv = B2 * v_ref[...] + (1.0 - B2) * (gj * gj)
            d_ref[...] = (-LR) * ((nm / BC1) / (jnp.sqrt(nv / BC2) + EPS) + WD * w_ref[...])
            nm_ref[...], nv_ref[...] = nm, nv

    ins = [t[k] for t in (g, w, m, v) for k in names]
    outs = pl.pallas_call(body, name=name, out_shape=[_S(w[k].shape) for _ in range(3) for k in names],
                          compiler_params=pltpu.CompilerParams(vmem_limit_bytes=VMEM_LIMIT))(*ins)
    return [{k: outs[t * n + j] for j, k in enumerate(names)} for t in range(3)]


def _me():
    return lax.axis_index("x"), lax.axis_index("y"), lax.axis_index("c")


def _lin(dev):
    return 4 * dev[0] + 2 * dev[1] + dev[2]


def _blk(ref, axis, size, idx):
    nd = len(ref.shape)
    start = idx * size
    if axis == nd - 1 and size % LANE == 0:
        start = pl.multiple_of(start, LANE)
    elif axis == nd - 2 and size % 16 == 0:
        start = pl.multiple_of(start, 16)
    ix = [slice(None)] * nd
    ix[axis] = pl.ds(start, size)
    return ref.at[tuple(ix)]


def _all_gather(shards, axes, name):
    n = len(shards)
    sizes = [s.shape[a] for s, a in zip(shards, axes)]
    out_shapes = [_S(s.shape[:a] + (N_DEV * s.shape[a],) + s.shape[a + 1:], s.dtype) for s, a in zip(shards, axes)]

    def body(*refs):
        ins, outs = refs[:n], refs[n:2 * n]
        send_sems, recv_sems, local_sems = refs[2 * n:]
        x, y, c = _me()
        me, sibling = (x, y, c), (x, y, 1 - c)
        chips = [(1 - x, y), (x, 1 - y), (1 - x, 1 - y)]

        def copy(a, k, block, to, from_input=False):
            dst = _blk(outs[a], axes[a], sizes[a], _lin(block))
            return pltpu.make_async_remote_copy(
                src_ref=ins[a] if from_input else dst, dst_ref=dst, send_sem=send_sems.at[a, k],
                recv_sem=recv_sems.at[a, k], device_id=to, device_id_type=MESH)

        mine = [pltpu.make_async_copy(ins[a], _blk(outs[a], axes[a], sizes[a], _lin(me)), local_sems.at[a]) for a in range(n)]
        for cp in mine:
            cp.start()
        first = []
        for a in range(n):
            first.append(copy(a, 0, me, sibling, True))
            first += [copy(a, 1 + j, me, (*chip, c), True) for j, chip in enumerate(chips)]
        for cp in first:
            cp.start()
        passed = []
        for j, chip in enumerate(chips):
            for a in range(n):
                copy(a, 1 + j, (*chip, c), me).wait_recv()
                cp = copy(a, 4 + j, (*chip, c), sibling)
                cp.start()
                passed.append(cp)
        for a in range(n):
            copy(a, 0, sibling, me).wait_recv()
            for j, chip in enumerate(chips):
                copy(a, 4 + j, (*chip, 1 - c), me).wait_recv()
        for cp in first + passed:
            cp.wait_send()
        for cp in mine:
            cp.wait()

    return pl.pallas_call(
        body, name=name, out_shape=out_shapes, in_specs=[ANY] * n, out_specs=[ANY] * n,
        scratch_shapes=[pltpu.SemaphoreType.DMA((n, 7)), pltpu.SemaphoreType.DMA((n, 7)), pltpu.SemaphoreType.DMA((n,))],
    )(*shards)


HBM_SPEC = pl.BlockSpec(memory_space=pltpu.HBM)
SEM_SPEC = pl.BlockSpec(memory_space=pltpu.SEMAPHORE)
EFFECT = pltpu.SideEffectType.DATAFLOW_SIDE_EFFECTING


def _peers(x, y, c):
    flip = lambda v, f: 1 - v if f else v
    return [(flip(x, k & 4), flip(y, k & 2), flip(c, k & 1)) for k in range(1, N_DEV)]


def _land_shape(mode, s, axis):
    if mode == "gather":
        return s.shape[:axis] + (N_DEV * s.shape[axis],) + s.shape[axis + 1:]
    return (N_DEV,) + s.shape[:axis] + (s.shape[axis] // N_DEV,) + s.shape[axis + 1:]


def _src_view(mode, ref, axis, peer):
    return ref if mode == "gather" else _blk(ref, axis, ref.shape[axis] // N_DEV, peer)


def _dst_view(mode, land, axis, sender):
    return _blk(land, axis, land.shape[axis] // N_DEV, sender) if mode == "gather" else land.at[sender]


def _seven_blocks(mode, land, axis):
    if mode == "gather":
        ix = [slice(None)] * len(land.shape)
        ix[axis] = pl.ds(0, (N_DEV - 1) * (land.shape[axis] // N_DEV))
        return land.at[tuple(ix)]
    return land.at[pl.ds(0, N_DEV - 1)]


def _place_own(mode, srcs, axes, name, after=None):
    n = len(srcs)
    extra, extra_specs = _after(after)

    def body(me_ref, *refs):
        for a in range(n):
            out = refs[n + len(extra) + a]
            out[...] = refs[a][...].reshape(out.shape)

    def at_me(shape, axis):
        return lambda i, me: tuple(me[0] if d == axis else 0 for d in range(len(shape)))

    in_specs, out_specs = [], []
    for s, axis in zip(srcs, axes):
        if mode == "gather":
            in_specs.append(pl.BlockSpec(s.shape, lambda i, me, nd=len(s.shape): (0,) * nd))
            out_specs.append(pl.BlockSpec(s.shape, at_me(s.shape, axis)))
        else:
            blk = s.shape[:axis] + (s.shape[axis] // N_DEV,) + s.shape[axis + 1:]
            in_specs.append(pl.BlockSpec(blk, at_me(blk, axis)))
            out_specs.append(pl.BlockSpec((1,) + blk, at_me((1,) + blk, 0)))
    me = _lin(_me()).astype(jnp.int32).reshape(1)
    return pl.pallas_call(
        body, name=name, out_shape=[_S(_land_shape(mode, s, a), s.dtype) for s, a in zip(srcs, axes)],
        grid_spec=pltpu.PrefetchScalarGridSpec(num_scalar_prefetch=1, grid=(1,), in_specs=in_specs + extra_specs,
                                               out_specs=out_specs),
        compiler_params=_params(1))(me, *srcs, *extra)


def _push_start(mode, srcs, lands, axes, name):
    n = len(srcs)

    def body(*refs):
        src_refs, land_refs = refs[:n], refs[n:2 * n]
        send_sems, recv_sems = refs[2 * n], refs[2 * n + 1]
        token = refs[-1]
        x, y, c = _me()
        me = _lin((x, y, c))
        for a in range(n):
            for peer in _peers(x, y, c):
                pltpu.make_async_remote_copy(
                    src_ref=_src_view(mode, src_refs[a], axes[a], _lin(peer)),
                    dst_ref=_dst_view(mode, land_refs[a], axes[a], me),
                    send_sem=send_sems.at[a], recv_sem=recv_sems.at[a], device_id=peer, device_id_type=MESH).start()
        token[...] = jnp.zeros_like(token)

    hbm = lambda s: pltpu.HBM(s.shape, s.dtype)
    outs = pl.pallas_call(
        body, name=name,
        out_shape=(pltpu.SemaphoreType.DMA((n,)), pltpu.SemaphoreType.DMA((n,)), *[hbm(s) for s in srcs], *[hbm(s) for s in lands],
                   _S((SUB, LANE))),
        in_specs=[HBM_SPEC] * (2 * n),
        out_specs=(SEM_SPEC, SEM_SPEC, *[HBM_SPEC] * (2 * n), pl.BlockSpec(memory_space=pltpu.VMEM)),
        input_output_aliases={i: 2 + i for i in range(2 * n)},
        compiler_params=pltpu.CompilerParams(has_side_effects=EFFECT),
    )(*[pltpu.with_memory_space_constraint(s, pltpu.HBM) for s in list(srcs) + list(lands)])
    return outs[0], outs[1], outs[2:2 + n], outs[2 + n:2 + 2 * n], outs[-1]


def _push_wait(mode, send_sems, recv_sems, srcs, lands, axes, after, name):
    n = len(srcs)

    def body(*refs):
        land_refs = refs[n:2 * n]
        send_sems, recv_sems = refs[2 * n], refs[2 * n + 1]
        x, y, c = _me()
        for a in range(n):
            seven = _seven_blocks(mode, land_refs[a], axes[a])
            cp = pltpu.make_async_remote_copy(src_ref=seven, dst_ref=seven, send_sem=send_sems.at[a], recv_sem=recv_sems.at[a],
                                              device_id=(x, y, 1 - c), device_id_type=MESH)
            cp.wait_send()
            cp.wait_recv()

    hbm = lambda s: pltpu.HBM(s.shape, s.dtype)
    outs = pl.pallas_call(
        body, name=name, out_shape=tuple(hbm(s) for s in list(srcs) + list(lands)),
        in_specs=[HBM_SPEC] * (2 * n) + [SEM_SPEC, SEM_SPEC, ANY], out_specs=tuple([HBM_SPEC] * (2 * n)),
        input_output_aliases={i: i for i in range(2 * n)},
        compiler_params=pltpu.CompilerParams(has_side_effects=EFFECT),
    )(*srcs, *lands, send_sems, recv_sems, after)
    return outs[n:]


def _sum_parts(parts):
    n, R, C = parts.shape

    def body(p_ref, o_ref):
        g = p_ref[0]
        for k in range(1, n):
            g = g + p_ref[k]
        o_ref[...] = g

    return pl.pallas_call(body, name="sum_parts", out_shape=_S((R, C)))(parts)


def _block_diag(w, nb):
    tn, r, c = w.shape
    w = w.reshape(tn // nb, nb, r, c)
    return jnp.einsum('tarc,ab->tarbc', w, jnp.eye(nb, dtype=w.dtype)).reshape(tn // nb, nb * r, nb * c)


def _block_diag_extract(w, nb):
    t, R, C = w.shape
    w = w.reshape(t, nb, R // nb, nb, C // nb)
    return jnp.einsum('tarbc,ab->tarc', w, jnp.eye(nb, dtype=w.dtype)).reshape(t * nb, R // nb, C // nb)


SMALL = ['conv_b', 'rg_wa', 'rg_ba', 'rg_wx', 'rg_bx', 'rg_lambda', 's5_a_re', 's5_a_im', 's5_b_re', 's5_b_im',
         's5_c_re', 's5_c_im', 's5_d', 's5_log_step', 's5_b_glu', 'ln1_g', 'ln1_b', 'ple_gate_b', 'ln2_g', 'ln2_b']
WEIGHTS = ['w_in', 'conv_w', 'conv_b', 'rg_wa', 'rg_ba', 'rg_wx', 'rg_bx', 'rg_lambda', 's5_a_re', 's5_a_im', 's5_b_re',
           's5_b_im', 's5_c_re', 's5_c_im', 's5_d', 's5_log_step', 's5_w_glu', 's5_b_glu', 'w_out', 'ln1_g', 'ln1_b',
           'ple_w', 'ple_gate_w', 'ple_gate_b', 'ln2_g', 'ln2_b']
PACK_ROWS_MULT = 64


def _pack(tree, scalar):
    flat = jnp.concatenate([tree[k].reshape(-1) for k in SMALL] + [scalar.reshape(1)])
    rows = -(-flat.shape[0] // (LANE * PACK_ROWS_MULT)) * PACK_ROWS_MULT
    return jnp.pad(flat, (0, rows * LANE - flat.shape[0])).reshape(rows, LANE)


def _unpack(packed, like):
    flat, out, o = packed.reshape(-1), {}, 0
    for k in SMALL:
        n = math.prod(like[k].shape)
        out[k] = flat[o:o + n].reshape(like[k].shape)
        o += n
    return out, flat[o]


class _NoHooks:
    token = None
    first_token = None

    def first_weights(self, full, after):
        return full

    def layer_start(self, i, W, after):
        return W

    def late_weights(self, i, W, after):
        return W

    def post_done(self, i, g):
        return None

    def w_in_done(self, i, g):
        return None

    def layer_done(self, i, g, dx):
        return None


def _local_grads(x, p, target, W, disc, hooks):
    depth = 2
    saved = []
    for i in range(depth):
        if i > 0:
            W = hooks.layer_start(i, W, x)
        w = W[i]
        z = _inproj_fwd(x, w['w_in'], hooks.token if i == 0 else None)
        hs = _rg_fwd(z, w['conv_w'], w['conv_b'], w['wa_bd'], w['wx_bd'], w['rg_ba'], w['rg_bx'], w['rg_lambda'], i)
        d = disc[i]
        y0, s_re, s_im = _s5_fwd(z, d['bb_re'], d['bb_im'], d['lb_re'], d['lb_im'], d['c_re'], d['c_im'], w['s5_d'], i)
        W = hooks.late_weights(i, W, y0)
        w = W[i]
        x2, *norms = _post_fwd(x, hs, z, y0, p[i], w['s5_w_glu'], w['s5_b_glu'], w['w_out'], w['ln1_g'], w['ln1_b'],
                               w['ple_w'], w['ple_gate_w'], w['ple_gate_b'], w['ln2_g'], w['ln2_b'], i)
        saved.append((x, z, hs, y0, s_re, s_im, norms))
        x = x2

    grads = [None] * depth
    dx = target
    loss = None
    token = None
    for i in reversed(range(depth)):
        w, d = W[i], disc[i]
        xin, z, hs, y0, s_re, s_im, (xh1, xh2, m, q, gt, rstd1, rstd2) = saved[i]
        g = {}
        (dt1, g['ple_w'], g['ple_gate_w'], g['ple_gate_b'], g['ln1_g'], g['ln1_b'], g['ln2_g'], g['ln2_b'], lrow) = _post_bwd_a(
            dx, i == depth - 1, xh2, xh1, rstd2, rstd1, q, gt, p[i], w['ple_gate_w'], w['ln1_g'], w['ln1_b'],
            w['ln2_g'], w['ln2_b'], i, token)
        if i == depth - 1:
            loss = 0.5 / D_MODEL * jnp.sum(lrow)
        dhs, dy0, dzg, g['w_out'], g['s5_w_glu'], g['s5_b_glu'] = _post_bwd_b(dt1, m, z, hs, y0, w['w_out'], w['s5_w_glu'],
                                                                           w['s5_b_glu'], i)
        (dzu, g['bb_re'], g['bb_im'], g['lb_re'], g['lb_im'], g['c_re'], g['c_im'], g['s5_d']) = _s5_bwd(
            dy0, z, s_re, s_im, d['bb_re'], d['bb_im'], d['lb_re'], d['lb_im'], d['c_re'], d['c_im'], w['s5_d'], i,
            hooks.post_done(i, g))
        (dzx, g['conv_w'], g['conv_b'], g['wa_bd'], g['wx_bd'], g['rg_ba'], g['rg_bx'], g['rg_lambda']) = _rg_bwd(
            dhs, z, hs, w['conv_w'], w['conv_b'], w['wa_bd'], w['wx_bd'], w['rg_ba'], w['rg_bx'], w['rg_lambda'], i)
        if i == 0:
            g['w_in'] = _inproj_bwd_dw(xin, dzx, dzg, dzu)
            dx = _inproj_bwd_dx(dt1, dzx, dzg, dzu, w['w_in'], hooks.w_in_done(i, g))
        else:
            dx, g['w_in'] = _inproj_bwd(dt1, xin, dzx, dzg, dzu, w['w_in'])
        grads[i] = g
        token = hooks.layer_done(i, g, dx)
    return loss, dx, grads


def _s5_layouts_fwd(s5_a_re, s5_a_im, s5_log_step, s5_b_re, s5_b_im, s5_c_re, s5_c_im, token=None):
    depth = s5_a_re.shape[0]
    ar, ai = s5_a_re.reshape(depth * 24, S5_P), s5_a_im.reshape(depth * 24, S5_P)
    ls = s5_log_step.reshape(depth * 24, 1)
    lr, li, cr, ci = _s5_disc_fwd(ar, ai, ls, token)
    col = lambda a: a.reshape(depth * S5_N, 1)
    br, bi = s5_b_re.reshape(depth * S5_N, 16), s5_b_im.reshape(depth * S5_N, 16)
    bbr, bbi = _s5_bscale_fwd(col(cr), col(ci), br, bi)
    disc = []
    for i in range(depth):
        gph = lambda a: a.reshape(depth, 24, S5_P, 16)[i]
        disc.append(dict(
            bb_re=_block_diag(jnp.swapaxes(gph(bbr), 1, 2), 8), bb_im=_block_diag(jnp.swapaxes(gph(bbi), 1, 2), 8),
            lb_re=lr.reshape(depth, 1, S5_N)[i], lb_im=li.reshape(depth, 1, S5_N)[i],
            c_re=_block_diag(jnp.swapaxes(s5_c_re[i], 1, 2), 8), c_im=_block_diag(jnp.swapaxes(s5_c_im[i], 1, 2), 8)))
    return disc, (ar, ai, ls, col(cr), col(ci), br, bi)


def _s5_layouts_bwd(grads, res):
    ar, ai, ls, cr, ci, br, bi = res
    depth = len(grads)
    stack = lambda f: jnp.stack([f(g) for g in grads])
    dbbr = stack(lambda g: jnp.swapaxes(_block_diag_extract(g['bb_re'], 8), 1, 2)).reshape(depth * S5_N, 16)
    dbbi = stack(lambda g: jnp.swapaxes(_block_diag_extract(g['bb_im'], 8), 1, 2)).reshape(depth * S5_N, 16)
    dbr, dbi, dcr, dci = _s5_bscale_bwd(cr, ci, br, bi, dbbr, dbbi)
    gp = lambda a: a.reshape(depth * 24, S5_P)
    dar, dai, dls = _s5_disc_bwd(ar, ai, ls, gp(stack(lambda g: g['lb_re'])), gp(stack(lambda g: g['lb_im'])), gp(dcr), gp(dci))
    return dict(
        s5_a_re=dar.reshape(depth, 24, S5_P), s5_a_im=dai.reshape(depth, 24, S5_P), s5_log_step=dls.reshape(depth, 24),
        s5_b_re=dbr.reshape(depth, 24, S5_P, 16), s5_b_im=dbi.reshape(depth, 24, S5_P, 16),
        s5_c_re=stack(lambda g: jnp.swapaxes(_block_diag_extract(g['c_re'], 8), 1, 2)),
        s5_c_im=stack(lambda g: jnp.swapaxes(_block_diag_extract(g['c_im'], 8), 1, 2)))


LATE = ('w_out', 'ple_w', 'ple_gate_w', 's5_w_glu')


ROWS = ('conv_b', 'rg_ba', 'rg_bx', 'rg_lambda', 's5_d', 's5_b_glu', 'ln1_g', 'ln1_b', 'ple_gate_b', 'ln2_g', 'ln2_b')


def _shared_weights(full):
    depth = full['conv_b'].shape[0]
    shared = {k: full[k].reshape(depth, 1, -1) for k in ROWS}
    shared['conv_w'] = full['conv_w']
    shared['wa_bd'] = _block_diag(full['rg_wa'].reshape(depth * 10, 64, 64), 2)
    shared['wx_bd'] = _block_diag(full['rg_wx'].reshape(depth * 10, 64, 64), 2)
    return shared


def _layer_weights(full, shared, i):
    return dict(shared, w_in=full['w_in'][i])


class _AllLocal(_NoHooks):
    def __init__(self, full):
        self.full = full

    def late_weights(self, i, W, after):
        W[i].update({k: self.full[k][i] for k in LATE})
        return W


def _full_grads(full, x, p, target, hooks=None):
    hooks = hooks or _AllLocal(full)
    disc, res = _s5_layouts_fwd(full['s5_a_re'], full['s5_a_im'], full['s5_log_step'], full['s5_b_re'], full['s5_b_im'],
                                full['s5_c_re'], full['s5_c_im'], hooks.first_token)
    full = hooks.first_weights(full, disc[-1]['bb_im'])
    shared = _shared_weights(full)
    W = [_layer_weights(full, shared, i) for i in range(2)]
    loss, gx, grads = _local_grads(x, p, target, W, disc, hooks)
    stack = lambda f: jnp.stack([f(g) for g in grads])
    out = _s5_layouts_bwd(grads, res)
    for k in SHARD_AXIS:
        out[k] = [g[k] for g in grads]
    out['conv_w'] = stack(lambda g: g['conv_w'])
    for k in ('conv_b', 'rg_ba', 'rg_bx', 'rg_lambda', 's5_b_glu', 'ln1_g', 'ln1_b', 'ple_gate_b', 'ln2_g', 'ln2_b'):
        out[k] = stack(lambda g: g[k][0])
    out['s5_d'] = stack(lambda g: g['s5_d'][0]).reshape(2, 24, 16)
    out['rg_wa'] = stack(lambda g: _block_diag_extract(g['wa_bd'], 2))
    out['rg_wx'] = stack(lambda g: _block_diag_extract(g['wx_bd'], 2))
    return loss, gx, out


SHARD_AXIS = {'w_in': 2, 'w_out': 1, 'ple_w': 2, 'ple_gate_w': 1, 's5_w_glu': 1}


def kernel(x, p, w_in, conv_w, conv_b, rg_wa, rg_ba, rg_wx, rg_bx, rg_lambda, s5_a_re, s5_a_im, s5_b_re, s5_b_im, s5_c_re, s5_c_im, s5_d, s5_log_step, s5_w_glu, s5_b_glu, w_out, ln1_g, ln1_b, ple_w, ple_gate_w, ple_gate_b, ln2_g, ln2_b, loss_target, m_w_in, m_conv_w, m_conv_b, m_rg_wa, m_rg_ba, m_rg_wx, m_rg_bx, m_rg_lambda, m_s5_a_re, m_s5_a_im, m_s5_b_re, m_s5_b_im, m_s5_c_re, m_s5_c_im, m_s5_d, m_s5_log_step, m_s5_w_glu, m_s5_b_glu, m_w_out, m_ln1_g, m_ln1_b, m_ple_w, m_ple_gate_w, m_ple_gate_b, m_ln2_g, m_ln2_b, v_w_in, v_conv_w, v_conv_b, v_rg_wa, v_rg_ba, v_rg_wx, v_rg_bx, v_rg_lambda, v_s5_a_re, v_s5_a_im, v_s5_b_re, v_s5_b_im, v_s5_c_re, v_s5_c_im, v_s5_d, v_s5_log_step, v_s5_w_glu, v_s5_b_glu, v_w_out, v_ln1_g, v_ln1_b, v_ple_w, v_ple_gate_w, v_ple_gate_b, v_ln2_g, v_ln2_b):
    local = dict(locals())
    w = {k: local[k] for k in WEIGHTS}
    mom = {k: local['m_' + k] for k in WEIGHTS}
    var = {k: local['v_' + k] for k in WEIGHTS}

    big = list(SHARD_AXIS)
    wire = {k: w[k].astype(WIRE) for k in big}
    late_axes = [SHARD_AXIS[k] - 1 for k in LATE]
    pushed = {}

    def push_weights(key, srcs, axes, after):
        pushed[key] = _push_start("gather", srcs, _place_own("gather", srcs, axes, "place_weights_" + key, after=after), axes,
                                  "push_weights_" + key)
        return pushed[key][4]

    def await_weights(key, axes, after):
        s = pushed[key]
        return _push_wait("gather", s[0], s[1], s[2], s[3], axes, after, "await_weights_" + key)

    token_first = push_weights("first", [wire['w_in'][0][None], conv_w[None]], [0, 0], None)
    token0 = push_weights("l0", [wire[k][0] for k in LATE], late_axes, token_first)
    push_weights("l1", [wire['w_in'][1][None]] + [wire[k][1] for k in LATE], [0] + late_axes, token0)

    def push_grads(key, g, names, axes):
        srcs = [g[k] for k in names]
        pushed[key] = _push_start("scatter", srcs, _place_own("scatter", srcs, axes, "place_grads_" + key), axes,
                                  "push_grads_" + key)
        return pushed[key][4]

    def await_grads(key, axes, after):
        s = pushed[key]
        return _push_wait("scatter", s[0], s[1], s[2], s[3], axes, after, "await_grads_" + key)

    class Overlap(_NoHooks):
        token = pushed["l1"][4]
        first_token = token

        def first_weights(self, full, after):
            w_in0, conv = await_weights("first", [0, 0], after)
            return dict(full, w_in=[w_in0, None], conv_w=jnp.moveaxis(conv, 0, 2).reshape(2, 4, RG_W))

        def late_weights(self, i, W, after):
            if i == 0:
                W[0].update(zip(LATE, await_weights("l0", late_axes, after)))
            return W

        def layer_start(self, i, W, after):
            lands = await_weights("l1", [0] + late_axes, after)
            W[1].update(zip(LATE, lands[1:]), w_in=lands[0])
            return W

        def post_done(self, i, g):
            return push_grads("late0", g, LATE, late_axes) if i == 0 else None

        def w_in_done(self, i, g):
            return push_grads("w_in0", g, ['w_in'], [0])

        def layer_done(self, i, g, dx):
            return push_grads("all1", g, ['w_in'] + list(LATE), [0] + late_axes) if i == 1 else None

    local_loss, grad_x, g = _full_grads(dict(w), x[0], p[:, 0], loss_target[0], Overlap())

    conv_blocks = jnp.moveaxis(g['conv_w'].reshape(2, 4, N_DEV, RG_W // N_DEV), 2, 0).reshape(N_DEV, 8, RG_W // N_DEV)
    packed = _pack(g, local_loss)
    token = push_grads("small", dict(conv_w=conv_blocks, small=packed), ['conv_w', 'small'], [0, 0])
    recv1 = dict(zip(['w_in'] + list(LATE), await_grads("all1", [0] + late_axes, grad_x)))
    recv0 = dict(zip(LATE, await_grads("late0", late_axes, grad_x)))
    outs = {}

    def update(k, parts, token=None):
        shard = w[k].shape
        c = shard[-1]
        two = lambda a: a.reshape(-1, c)
        res = _adamw([r.reshape(N_DEV, -1, c) for r in parts], two(w[k]), two(mom[k]), two(var[k]), token)
        outs[k] = [o.reshape(shard) for o in res]

    for k in LATE:
        update(k, [recv0[k], recv1[k]], token)
        token = None
    w_in0, = await_grads("w_in0", [0], outs[LATE[-1]][1])
    update('w_in', [w_in0, recv1['w_in']])
    conv_parts, small_parts = await_grads("small", [0, 0], outs['w_in'][1])
    update('conv_w', [conv_parts])

    rows = packed.shape[0] // N_DEV
    mine = _sum_parts(small_parts.reshape(N_DEV, rows, LANE))
    summed, loss = _unpack(_all_gather([mine], [0], "gather_small_grads")[0], w)
    narrow = ['s5_b_re', 's5_b_im']
    for names, name in ((narrow, "adamw_s5_b"), ([k for k in SMALL if k not in narrow], "adamw_small")):
        delta, new_m, new_v = _adamw_natural(names, summed, w, mom, var, name)
        for k in names:
            outs[k] = [summed[k], delta[k], new_m[k], new_v[k]]

    res = [loss, grad_x[None]]
    for j in range(4):
        res += [outs[k][j] for k in WEIGHTS]
    return tuple(res)
```

```python
import math

import jax
import jax.numpy as jnp
from jax import lax
from jax.experimental import pallas as pl
from jax.experimental.pallas import tpu as pltpu

F32 = jnp.float32
MXU = jnp.bfloat16
WIRE = jnp.bfloat16

N_DEV = 8
D_MODEL = 1024
PLE_D = 256
RG_W = 640
S5_W = 384
S5_P = 64
S5_N = 24 * S5_P
Z_W = 2 * RG_W + 2 * S5_W
C_RGG = RG_W
C_S5U = 2 * RG_W
C_S5G = 2 * RG_W + S5_W
LANE = 128
N_RG_T = RG_W // LANE
N_S5_T = S5_W // LANE
W_BLK = Z_W // N_DEV
ALPHA = (2.0 * 2) ** 0.25
LN_EPS = 1e-5
RG_C = 8.0
LR, B1, B2, EPS, WD, STEP = 0.001, 0.9, 0.999, 1e-08, 0.01, 10
BC1 = 1.0 - B1 ** STEP
BC2 = 1.0 - B2 ** STEP
RC = 256
TM = 256
TM_MM = 1024
VMEM_LIMIT = 56 * 1024 * 1024

MESH = pl.DeviceIdType.MESH
ANY = pl.BlockSpec(memory_space=pl.ANY)


def _params(n_grid_axes, vmem=VMEM_LIMIT):
    return pltpu.CompilerParams(dimension_semantics=("arbitrary",) * n_grid_axes, vmem_limit_bytes=vmem)


def _S(shape, dtype=F32):
    return jax.ShapeDtypeStruct(tuple(shape), dtype)


def _sigmoid(x):
    return 0.5 * jnp.tanh(0.5 * x) + 0.5


def _silu_and_grad(x):
    s = _sigmoid(x)
    return x * s, s * (1.0 + x * (1.0 - s))


_GELU_C = math.sqrt(2.0 / math.pi)


def _gelu(x):
    return 0.5 * x * (1.0 + jnp.tanh(_GELU_C * (x + 0.044715 * (x * x * x))))


def _gelu_grad(x):
    th = jnp.tanh(_GELU_C * (x + 0.044715 * (x * x * x)))
    return 0.5 * (1.0 + th) + 0.5 * x * (1.0 - th * th) * (_GELU_C * (1.0 + 3.0 * 0.044715 * (x * x)))


def _mm(a, b):
    return jnp.dot(a.astype(MXU), b.astype(MXU), preferred_element_type=F32)


def _mm_nt(a, b):
    return lax.dot_general(a.astype(MXU), b.astype(MXU), (((1,), (1,)), ((), ())), preferred_element_type=F32)


def _mm_tn(a, b):
    return lax.dot_general(a.astype(MXU), b.astype(MXU), (((0,), (0,)), ((), ())), preferred_element_type=F32)


def _ln_fwd(t, g, b):
    mu = jnp.mean(t, axis=-1, keepdims=True)
    tc = t - mu
    var = jnp.mean(tc * tc, axis=-1, keepdims=True)
    rstd = lax.rsqrt(var + LN_EPS)
    xhat = tc * rstd
    return xhat * g + b, xhat, rstd


def _ln_bwd(dy, xhat, rstd, g):
    dxh = dy * g
    m1 = jnp.mean(dxh, axis=-1, keepdims=True)
    m2 = jnp.mean(dxh * xhat, axis=-1, keepdims=True)
    return rstd * (dxh - m1 - xhat * m2)


def _colsum(a):
    return jnp.sum(a, axis=0, keepdims=True)


def _up(x, d, rows, fill):
    n = x.shape[0]
    return jnp.where(rows < n - d, pltpu.roll(x, n - d, 0), fill)


SUB = 8
TILE_STEPS = (1, 2, 4)


def _r8(width):
    return lax.broadcasted_iota(jnp.int32, (SUB, width), 0)


def _scan_real(a, u, carry, reverse=False):
    r8 = _r8(a.shape[1])
    n = a.shape[0] // SUB
    outs = [None] * n
    for k in (reversed(range(n)) if reverse else range(n)):
        A, U = a[SUB * k:SUB * k + SUB], u[SUB * k:SUB * k + SUB]
        for d in TILE_STEPS:
            m = (r8 < SUB - d) if reverse else (r8 >= d)
            sh = SUB - d if reverse else d
            U = A * jnp.where(m, pltpu.roll(U, sh, 0), 0.0) + U
            A = A * jnp.where(m, pltpu.roll(A, sh, 0), 1.0)
        h = A * carry + U
        outs[k] = h
        carry = h[0:1] if reverse else h[SUB - 1:SUB]
    return jnp.concatenate(outs, axis=0), carry


def _tile_powers(lr, li, reverse=False):
    width = lr.shape[1]
    r8 = _r8(width)
    steps = []
    pr, pi = lr, li
    er, ei = jnp.broadcast_to(lr, (SUB, width)), jnp.broadcast_to(li, (SUB, width))
    for d in TILE_STEPS:
        m = (r8 < SUB - d) if reverse else (r8 >= d)
        sh = SUB - d if reverse else d
        steps.append((sh, jnp.where(m, pr, 0.0), jnp.where(m, pi, 0.0)))
        er, ei = _cmul(er, ei, jnp.where(m, pltpu.roll(er, sh, 0), 1.0), jnp.where(m, pltpu.roll(ei, sh, 0), 0.0))
        pr, pi = _cmul(pr, pi, pr, pi)
    return steps, (er, ei)


def _scan_lti(xr, xi, carry, steps, e, reverse=False):
    er, ei = e
    kr, ki = carry
    n = xr.shape[0] // SUB
    outr, outi = [None] * n, [None] * n
    for k in (reversed(range(n)) if reverse else range(n)):
        sr, si = xr[SUB * k:SUB * k + SUB], xi[SUB * k:SUB * k + SUB]
        for sh, pr, pi in steps:
            shr, shi = pltpu.roll(sr, sh, 0), pltpu.roll(si, sh, 0)
            sr, si = sr + (pr * shr - pi * shi), si + (pr * shi + pi * shr)
        sr = sr + (er * kr - ei * ki)
        si = si + (er * ki + ei * kr)
        outr[k], outi[k] = sr, si
        kr, ki = (sr[0:1], si[0:1]) if reverse else (sr[SUB - 1:SUB], si[SUB - 1:SUB])
    return jnp.concatenate(outr, axis=0), jnp.concatenate(outi, axis=0), (kr, ki)


def _halo(ref, c, r0):
    rp = pl.multiple_of(jnp.maximum(r0 - 8, 0), 8)
    return jnp.where(c > 0, ref[pl.ds(rp, 8), :], 0.0)


def _conv_taps(xe):
    return [pltpu.roll(xe, 3, 0)[8:, :], pltpu.roll(xe, 2, 0)[8:, :], pltpu.roll(xe, 1, 0)[8:, :], xe[8:, :]]


def _rg_gates(h, wa, wx, ba, bx, sp):
    r = _sigmoid(_mm(h, wa) + ba)
    i = _sigmoid(_mm(h, wx) + bx)
    log_a = (-RG_C) * r * sp
    a = jnp.exp(log_a)
    mult = jnp.sqrt(-jnp.tanh(log_a) * (a * a + 1.0))
    return r, i, a, mult


def _softplus(y):
    return jnp.maximum(y, 0.0) + jnp.log1p(jnp.exp(-jnp.abs(y)))


def _after(token):
    return ([], []) if token is None else ([token], [ANY])


def _inproj_fwd(x, w_in, token=None):
    L = x.shape[0]

    def body(x_ref, w_ref, *rest):
        rest[-1][...] = _mm(x_ref[...], w_ref[...])

    extra, extra_specs = _after(token)
    tm = min(TM_MM, L)
    return pl.pallas_call(
        body, name="inproj_fwd", grid=(L // tm,),
        in_specs=[pl.BlockSpec((tm, D_MODEL), lambda i: (i, 0)), pl.BlockSpec((D_MODEL, Z_W), lambda i: (0, 0))] + extra_specs,
        out_specs=pl.BlockSpec((tm, Z_W), lambda i: (i, 0)),
        out_shape=_S((L, Z_W)), compiler_params=_params(1))(x, w_in, *extra)


def _inproj_bwd(dt1, x, dzx, dzg, dzu, w_in):
    L = x.shape[0]

    def body(dt1_ref, x_ref, dzx_ref, dzg_ref, dzu_ref, w_ref, dx_ref, dw_ref, acc_ref):
        @pl.when(pl.program_id(0) == 0)
        def _():
            acc_ref[...] = jnp.zeros_like(acc_ref)
        dzg = dzg_ref[...]
        dz = jnp.concatenate([dzx_ref[...], dzg[:, :RG_W], dzu_ref[...], dzg[:, RG_W:]], axis=1).astype(MXU)
        xb = x_ref[...].astype(MXU)
        dx_ref[...] = ALPHA * dt1_ref[...] + _mm_nt(dz, w_ref[...])
        for j in range(N_DEV):
            acc_ref[j] += _mm_tn(xb, dz[:, j * W_BLK:(j + 1) * W_BLK])

        @pl.when(pl.program_id(0) == L // TM - 1)
        def _():
            dw_ref[...] = acc_ref[...].astype(WIRE)

    row = lambda w: pl.BlockSpec((TM, w), lambda i: (i, 0))
    wspec = pl.BlockSpec((N_DEV, D_MODEL, W_BLK), lambda i: (0, 0, 0))
    return pl.pallas_call(
        body, name="inproj_bwd", grid=(L // TM,),
        in_specs=[row(D_MODEL), row(D_MODEL), row(RG_W), row(D_MODEL), row(S5_W),
                  pl.BlockSpec((D_MODEL, Z_W), lambda i: (0, 0))],
        out_specs=[row(D_MODEL), wspec],
        out_shape=[_S((L, D_MODEL)), _S((N_DEV, D_MODEL, W_BLK), WIRE)],
        scratch_shapes=[pltpu.VMEM((N_DEV, D_MODEL, W_BLK), F32)],
        compiler_params=_params(1))(dt1, x, dzx, dzg, dzu, w_in)


TM2 = 512


def _dz_block(dzx_ref, dzg_ref, dzu_ref):
    dzg = dzg_ref[...]
    return jnp.concatenate([dzx_ref[...], dzg[:, :RG_W], dzu_ref[...], dzg[:, RG_W:]], axis=1).astype(MXU)


def _inproj_bwd_dw(x, dzx, dzg, dzu):
    L = x.shape[0]

    def body(x_ref, dzx_ref, dzg_ref, dzu_ref, dw_ref, acc_ref):
        @pl.when(pl.program_id(0) == 0)
        def _():
            acc_ref[...] = jnp.zeros_like(acc_ref)
        dz = _dz_block(dzx_ref, dzg_ref, dzu_ref)
        xb = x_ref[...].astype(MXU)
        for j in range(N_DEV):
            acc_ref[j] += _mm_tn(xb, dz[:, j * W_BLK:(j + 1) * W_BLK])

        @pl.when(pl.program_id(0) == L // TM2 - 1)
        def _():
            dw_ref[...] = acc_ref[...].astype(WIRE)

    row = lambda w: pl.BlockSpec((TM2, w), lambda i: (i, 0))
    wspec = pl.BlockSpec((N_DEV, D_MODEL, W_BLK), lambda i: (0, 0, 0))
    return pl.pallas_call(
        body, name="inproj_bwd_dw", grid=(L // TM2,),
        in_specs=[row(D_MODEL), row(RG_W), row(D_MODEL), row(S5_W)], out_specs=wspec,
        out_shape=_S((N_DEV, D_MODEL, W_BLK), WIRE), scratch_shapes=[pltpu.VMEM((N_DEV, D_MODEL, W_BLK), F32)],
        compiler_params=_params(1))(x, dzx, dzg, dzu)


def _inproj_bwd_dx(dt1, dzx, dzg, dzu, w_in, token=None):
    L = dt1.shape[0]
    extra, extra_specs = _after(token)

    def body(dt1_ref, dzx_ref, dzg_ref, dzu_ref, w_ref, *rest):
        rest[-1][...] = ALPHA * dt1_ref[...] + _mm_nt(_dz_block(dzx_ref, dzg_ref, dzu_ref), w_ref[...])

    tm = min(TM_MM, L)
    row = lambda w: pl.BlockSpec((tm, w), lambda i: (i, 0))
    return pl.pallas_call(
        body, name="inproj_bwd_dx", grid=(L // tm,),
        in_specs=[row(D_MODEL), row(RG_W), row(D_MODEL), row(S5_W), _full((D_MODEL, Z_W))] + extra_specs,
        out_specs=row(D_MODEL), out_shape=_S((L, D_MODEL)), compiler_params=_params(1))(dt1, dzx, dzg, dzu, w_in, *extra)


def _rg_specs(layer):
    tile = lambda rows: pl.BlockSpec((rows, LANE), lambda c: (0, c))
    ptile = lambda rows: pl.BlockSpec((None, rows, LANE), lambda c: (layer, 0, c))
    pbd = pl.BlockSpec((None, LANE, LANE), lambda c: (layer * N_RG_T + c, 0, 0))
    return tile, ptile, pbd, pl.BlockSpec((None, LANE, LANE), lambda c: (c, 0, 0))


def _rg_fwd(z, cw, cb, wa_bd, wx_bd, ba, bx, lam, layer):
    L = z.shape[0]

    def body(x_ref, cw_ref, cb_ref, wa_ref, wx_ref, ba_ref, bx_ref, lam_ref, hs_ref):
        w, b = cw_ref[...], cb_ref[...]
        wa, wx, ba_, bx_ = wa_ref[...].astype(MXU), wx_ref[...].astype(MXU), ba_ref[...], bx_ref[...]
        sp = _softplus(-lam_ref[...])

        def step(c, carry):
            r0 = pl.multiple_of(c * RC, RC)
            xe = jnp.concatenate([_halo(x_ref, c, r0), x_ref[pl.ds(r0, RC), :]], axis=0)
            t = _conv_taps(xe)
            h = t[0] * w[0:1] + t[1] * w[1:2] + t[2] * w[2:3] + t[3] * w[3:4] + b
            _, i, a, mult = _rg_gates(h, wa, wx, ba_, bx_, sp)
            hs, carry = _scan_real(a, mult * (i * h), carry)
            hs_ref[pl.ds(r0, RC), :] = hs
            return carry

        lax.fori_loop(0, L // RC, step, jnp.zeros((1, LANE), F32))

    tile, ptile, pbd, _ = _rg_specs(layer)
    return pl.pallas_call(
        body, name="rg_fwd", grid=(N_RG_T,),
        in_specs=[tile(L), ptile(4), ptile(1), pbd, pbd, ptile(1), ptile(1), ptile(1)],
        out_specs=tile(L), out_shape=_S((L, RG_W)), compiler_params=_params(1))(z, cw, cb, wa_bd, wx_bd, ba, bx, lam)


def _rg_bwd(dhs, z, hs, cw, cb, wa_bd, wx_bd, ba, bx, lam, layer):
    L = z.shape[0]

    def body(g_ref, x_ref, hs_ref, cw_ref, cb_ref, wa_ref, wx_ref, ba_ref, bx_ref, lam_ref,
             dx_ref, dcw_ref, dcb_ref, dwa_ref, dwx_ref, dba_ref, dbx_ref, dlam_ref):
        w, b = cw_ref[...], cb_ref[...]
        wa, wx, ba_, bx_ = wa_ref[...].astype(MXU), wx_ref[...].astype(MXU), ba_ref[...], bx_ref[...]
        lam = lam_ref[...]
        sp = _softplus(-lam)
        rows = lax.broadcasted_iota(jnp.int32, (RC, LANE), 0)
        for ref in (dcw_ref, dcb_ref, dwa_ref, dwx_ref, dba_ref, dbx_ref, dlam_ref):
            ref[...] = jnp.zeros_like(ref)
        nch = L // RC

        def step(k, carry):
            cin, nxt = carry
            c = nch - 1 - k
            r0 = pl.multiple_of(c * RC, RC)
            xe = jnp.concatenate([_halo(x_ref, c, r0), x_ref[pl.ds(r0, RC), :]], axis=0)
            t = _conv_taps(xe)
            h = t[0] * w[0:1] + t[1] * w[1:2] + t[2] * w[2:3] + t[3] * w[3:4] + b
            r, i, a, mult = _rg_gates(h, wa, wx, ba_, bx_, sp)
            hs_e = jnp.concatenate([_halo(hs_ref, c, r0), hs_ref[pl.ds(r0, RC), :]], axis=0)
            hs_prev = pltpu.roll(hs_e, 1, 0)[8:, :]
            g = g_ref[pl.ds(r0, RC), :]
            cc, cin_new = _scan_real(a, a * g, cin, reverse=True)
            dh = g + _up(cc, 1, rows, cin)
            ih = i * h
            dlog_a = dh * hs_prev * a - (dh * ih) * (a * a) / mult
            di = dh * mult * h
            dhin = dh * mult * i
            dr = dlog_a * ((-RG_C) * sp)
            dlam_ref[...] += _colsum(dlog_a * r)
            dra = dr * r * (1.0 - r)
            dia = di * i * (1.0 - i)
            dwa_ref[...] += _mm_tn(h, dra)
            dwx_ref[...] += _mm_tn(h, dia)
            dba_ref[...] += _colsum(dra)
            dbx_ref[...] += _colsum(dia)
            dhin = dhin + _mm_nt(dra, wa) + _mm_nt(dia, wx)
            de = jnp.concatenate([dhin, nxt], axis=0)
            n = RC + 8
            dx = (dhin * w[3:4] + pltpu.roll(de, n - 1, 0)[:RC, :] * w[2:3]
                  + pltpu.roll(de, n - 2, 0)[:RC, :] * w[1:2] + pltpu.roll(de, n - 3, 0)[:RC, :] * w[0:1])
            dx_ref[pl.ds(r0, RC), :] = dx
            for kk in range(4):
                dcw_ref[kk:kk + 1, :] += _colsum(dhin * t[kk])
            dcb_ref[...] += _colsum(dhin)
            return cin_new, dhin[0:8, :]

        lax.fori_loop(0, nch, step, (jnp.zeros((1, LANE), F32), jnp.zeros((8, LANE), F32)))
        dlam_ref[...] = dlam_ref[...] * (RG_C * _sigmoid(-lam))

    tile, ptile, pbd, bd = _rg_specs(layer)
    return pl.pallas_call(
        body, name="rg_bwd", grid=(N_RG_T,),
        in_specs=[tile(L), tile(L), tile(L), ptile(4), ptile(1), pbd, pbd, ptile(1), ptile(1), ptile(1)],
        out_specs=[tile(L), tile(4), tile(1), bd, bd, tile(1), tile(1), tile(1)],
        out_shape=[_S((L, RG_W)), _S((4, RG_W)), _S((1, RG_W)), _S((N_RG_T, LANE, LANE)), _S((N_RG_T, LANE, LANE)),
                   _S((1, RG_W)), _S((1, RG_W)), _S((1, RG_W))],
        compiler_params=_params(1))(dhs, z, hs, cw, cb, wa_bd, wx_bd, ba, bx, lam)


def _cmul(ar, ai, br, bi):
    return ar * br - ai * bi, ar * bi + ai * br


S5_TW = S5_N // N_S5_T


def _s5_specs(L):
    in_tile = pl.BlockSpec((L, LANE), lambda t: (0, t))
    st = pl.BlockSpec((L, S5_TW), lambda t: (0, t))
    bb = pl.BlockSpec((None, LANE, S5_TW), lambda t: (t, 0, 0))
    cc = pl.BlockSpec((None, S5_TW, LANE), lambda t: (t, 0, 0))
    lb = pl.BlockSpec((1, S5_TW), lambda t: (0, t))
    dv = pl.BlockSpec((1, LANE), lambda t: (0, t))
    return in_tile, st, bb, cc, lb, dv


def _layer_row_tile(layer):
    return pl.BlockSpec((None, 1, LANE), lambda t: (layer, 0, t))


def _s5_fwd(z, bb_re, bb_im, lb_re, lb_im, c_re, c_im, dvec, layer):
    L = z.shape[0]

    def body(u_ref, bbr_ref, bbi_ref, lr_ref, li_ref, cr_ref, ci_ref, d_ref, y_ref, sr_ref, si_ref):
        bbr, bbi = bbr_ref[...].astype(MXU), bbi_ref[...].astype(MXU)
        cr, ci = cr_ref[...].astype(MXU), ci_ref[...].astype(MXU)
        dv = d_ref[...]
        steps, e = _tile_powers(lr_ref[...], li_ref[...])

        def step(c, carry):
            r0 = pl.multiple_of(c * RC, RC)
            u = u_ref[pl.ds(r0, RC), :]
            ub = u.astype(MXU)
            sr = jnp.dot(ub, bbr, preferred_element_type=F32)
            si = jnp.dot(ub, bbi, preferred_element_type=F32)
            sr, si, carry = _scan_lti(sr, si, carry, steps, e)
            sr_ref[pl.ds(r0, RC), :] = sr
            si_ref[pl.ds(r0, RC), :] = si
            y_ref[pl.ds(r0, RC), :] = dv * u + (_mm(sr, cr) - _mm(si, ci))
            return carry

        zero = jnp.zeros((1, S5_TW), F32)
        lax.fori_loop(0, L // RC, step, (zero, zero))

    in_tile, st, bb, cc, lb, dv = _s5_specs(L)
    u_tile = pl.BlockSpec((L, LANE), lambda t: (0, C_S5U // LANE + t))
    return pl.pallas_call(
        body, name="s5_fwd", grid=(N_S5_T,),
        in_specs=[u_tile, bb, bb, lb, lb, cc, cc, _layer_row_tile(layer)],
        out_specs=[in_tile, st, st],
        out_shape=[_S((L, S5_W)), _S((L, S5_N)), _S((L, S5_N))],
        compiler_params=_params(1))(z, bb_re, bb_im, lb_re, lb_im, c_re, c_im, dvec)


def _s5_bwd(dy0, z, s_re, s_im, bb_re, bb_im, lb_re, lb_im, c_re, c_im, dvec, layer, token=None):
    L = z.shape[0]
    extra, extra_specs = _after(token)

    def body(dy_ref, u_ref, sr_ref, si_ref, bbr_ref, bbi_ref, lr_ref, li_ref, cr_ref, ci_ref, d_ref, *rest):
        du_ref, dbbr_ref, dbbi_ref, dlr_ref, dli_ref, dcr_ref, dci_ref, dd_ref = rest[len(extra):]
        bbr, bbi = bbr_ref[...].astype(MXU), bbi_ref[...].astype(MXU)
        cr, ci = cr_ref[...].astype(MXU), ci_ref[...].astype(MXU)
        lr, li = lr_ref[...], -li_ref[...]
        dv = d_ref[...]
        steps, e = _tile_powers(lr, li, reverse=True)
        for ref in (dbbr_ref, dbbi_ref, dlr_ref, dli_ref, dcr_ref, dci_ref, dd_ref):
            ref[...] = jnp.zeros_like(ref)
        nch = L // RC

        def step(k, carry):
            c = nch - 1 - k
            r0 = pl.multiple_of(c * RC, RC)
            dy = dy_ref[pl.ds(r0, RC), :]
            u = u_ref[pl.ds(r0, RC), :]
            dyb, ub = dy.astype(MXU), u.astype(MXU)
            sr, si = sr_ref[pl.ds(r0, RC), :], si_ref[pl.ds(r0, RC), :]
            dcr_ref[...] += _mm_tn(sr, dyb)
            dci_ref[...] -= _mm_tn(si, dyb)
            gr = _mm_nt(dyb, cr)
            gi = -_mm_nt(dyb, ci)
            gr, gi, carry = _scan_lti(gr, gi, carry, steps, e, reverse=True)
            pr_ = pltpu.roll(jnp.concatenate([_halo(sr_ref, c, r0), sr], axis=0), 1, 0)[8:, :]
            pi_ = pltpu.roll(jnp.concatenate([_halo(si_ref, c, r0), si], axis=0), 1, 0)[8:, :]
            dlr_ref[...] += _colsum(pr_ * gr + pi_ * gi)
            dli_ref[...] += _colsum(pr_ * gi - pi_ * gr)
            grb, gib = gr.astype(MXU), gi.astype(MXU)
            dbbr_ref[...] += _mm_tn(ub, grb)
            dbbi_ref[...] += _mm_tn(ub, gib)
            du_ref[pl.ds(r0, RC), :] = dv * dy + (_mm_nt(grb, bbr) + _mm_nt(gib, bbi))
            dd_ref[...] += _colsum(dy * u)
            return carry

        zero = jnp.zeros((1, S5_TW), F32)
        lax.fori_loop(0, nch, step, (zero, zero))

    in_tile, st, bb, cc, lb, dv = _s5_specs(L)
    u_tile = pl.BlockSpec((L, LANE), lambda t: (0, C_S5U // LANE + t))
    return pl.pallas_call(
        body, name="s5_bwd", grid=(N_S5_T,),
        in_specs=[in_tile, u_tile, st, st, bb, bb, lb, lb, cc, cc, _layer_row_tile(layer)] + extra_specs,
        out_specs=[in_tile, bb, bb, lb, lb, cc, cc, dv],
        out_shape=[_S((L, S5_W)), _S((N_S5_T, LANE, S5_TW)), _S((N_S5_T, LANE, S5_TW)), _S((1, S5_N)), _S((1, S5_N)),
                   _S((N_S5_T, S5_TW, LANE)), _S((N_S5_T, S5_TW, LANE)), _S((1, S5_W))],
        compiler_params=_params(1))(dy0, z, s_re, s_im, bb_re, bb_im, lb_re, lb_im, c_re, c_im, dvec, *extra)


def _disc(ar, ai, ls):
    dt = jnp.exp(ls)
    mag = jnp.exp(ar * dt)
    lr = mag * jnp.cos(ai * dt)
    li = mag * jnp.sin(ai * dt)
    den = ar * ar + ai * ai
    cr = ((lr - 1.0) * ar + li * ai) / den
    ci = (li * ar - (lr - 1.0) * ai) / den
    return lr, li, cr, ci


def _s5_disc_fwd(ar, ai, ls, token=None):
    extra, extra_specs = _after(token)

    def body(ar_ref, ai_ref, ls_ref, *rest):
        lr_ref, li_ref, cr_ref, ci_ref = rest[len(extra):]
        lr, li, cr, ci = _disc(ar_ref[...], ai_ref[...], ls_ref[...])
        lr_ref[...], li_ref[...], cr_ref[...], ci_ref[...] = lr, li, cr, ci

    sh = _S(ar.shape)
    vm = pl.BlockSpec(memory_space=pltpu.VMEM)
    return pl.pallas_call(body, name="s5_disc_fwd", in_specs=[vm, vm, vm] + extra_specs, out_shape=[sh, sh, sh, sh])(
        ar, ai, ls, *extra)


def _s5_disc_bwd(ar, ai, ls, dlr, dli, dcr, dci):
    def body(ar_ref, ai_ref, ls_ref, dlr_ref, dli_ref, dcr_ref, dci_ref, dar_ref, dai_ref, dls_ref):
        _, vjp = jax.vjp(_disc, ar_ref[...], ai_ref[...], jnp.broadcast_to(ls_ref[...], ar_ref.shape))
        dar, dai, dls = vjp((dlr_ref[...], dli_ref[...], dcr_ref[...], dci_ref[...]))
        dar_ref[...], dai_ref[...] = dar, dai
        dls_ref[...] = jnp.sum(dls, axis=1, keepdims=True)

    return pl.pallas_call(body, name="s5_disc_bwd", out_shape=[_S(ar.shape), _S(ar.shape), _S(ls.shape)])(
        ar, ai, ls, dlr, dli, dcr, dci)


def _s5_bscale_fwd(cr, ci, br, bi):
    def body(cr_ref, ci_ref, br_ref, bi_ref, or_ref, oi_ref):
        or_ref[...], oi_ref[...] = _cmul(cr_ref[...], ci_ref[...], br_ref[...], bi_ref[...])

    return pl.pallas_call(body, name="s5_bscale_fwd", out_shape=[_S(br.shape), _S(br.shape)])(cr, ci, br, bi)


def _s5_bscale_bwd(cr, ci, br, bi, gr, gi):
    def body(cr_ref, ci_ref, br_ref, bi_ref, gr_ref, gi_ref, dbr_ref, dbi_ref, dcr_ref, dci_ref):
        cr_, ci_, br_, bi_, gr_, gi_ = (r[...] for r in (cr_ref, ci_ref, br_ref, bi_ref, gr_ref, gi_ref))
        dbr_ref[...] = cr_ * gr_ + ci_ * gi_
        dbi_ref[...] = cr_ * gi_ - ci_ * gr_
        dcr_ref[...] = jnp.sum(gr_ * br_ + gi_ * bi_, axis=1, keepdims=True)
        dci_ref[...] = jnp.sum(gi_ * br_ - gr_ * bi_, axis=1, keepdims=True)

    return pl.pallas_call(body, name="s5_bscale_bwd",
                          out_shape=[_S(br.shape), _S(br.shape), _S(cr.shape), _S(cr.shape)])(cr, ci, br, bi, gr, gi)


def _row(w):
    return pl.BlockSpec((TM, w), lambda i: (i, 0))


def _full(shape):
    return pl.BlockSpec(tuple(shape), lambda i: (0,) * len(shape))


def _lrow(layer, width):
    return pl.BlockSpec((None, 1, width), lambda i: (layer, 0, 0))


def _post_fwd(x, hs, z, y0, p, w_glu, b_glu, w_out, g1, b1, ple_w, w_pg, b_pg, g2, b2, layer):
    L = x.shape[0]

    def body(x_ref, hs_ref, z_ref, y0_ref, p_ref, wg_ref, bg_ref, wo_ref, g1_ref, b1_ref, pw_ref, wpg_ref, bpg_ref,
             g2_ref, b2_ref, x2_ref, xh1_ref, xh2_ref, m_ref, q_ref, gt_ref, rstd1_ref, rstd2_ref):
        rg_gate = z_ref[:, C_RGG:C_RGG + RG_W]
        s5_gate = z_ref[:, C_S5G:C_S5G + S5_W]
        rg_y = hs_ref[...] * _silu_and_grad(rg_gate)[0]
        y1 = _gelu(y0_ref[...])
        gl = _sigmoid(_mm(y1, wg_ref[...]) + bg_ref[...])
        s5_y = (y1 * gl) * _silu_and_grad(s5_gate)[0]
        m_ref[:, :RG_W] = rg_y
        m_ref[:, RG_W:] = s5_y
        mix = _mm(m_ref[...], wo_ref[...])
        t1 = ALPHA * x_ref[...] + mix
        x1, xh1, rstd1 = _ln_fwd(t1, g1_ref[...], b1_ref[...])
        q = _mm(p_ref[...], pw_ref[...])
        gt = _sigmoid(_mm(x1, wpg_ref[...]) + bpg_ref[...])
        t2 = ALPHA * x1 + q * gt
        x2, xh2, rstd2 = _ln_fwd(t2, g2_ref[...], b2_ref[...])
        x2_ref[...], xh1_ref[...], xh2_ref[...], q_ref[...], gt_ref[...] = x2, xh1, xh2, q, gt
        rstd1_ref[...], rstd2_ref[...] = rstd1, rstd2

    vec = _lrow(layer, D_MODEL)
    return pl.pallas_call(
        body, name="post_fwd", grid=(L // TM,),
        in_specs=[_row(D_MODEL), _row(RG_W), _row(Z_W), _row(S5_W), _row(PLE_D), _full((S5_W, S5_W)), _lrow(layer, S5_W),
                  _full((D_MODEL, D_MODEL)), vec, vec, _full((PLE_D, D_MODEL)), _full((D_MODEL, D_MODEL)), vec, vec, vec],
        out_specs=[_row(D_MODEL)] * 6 + [_row(1)] * 2, out_shape=[_S((L, D_MODEL))] * 6 + [_S((L, 1))] * 2,
        compiler_params=_params(1))(x, hs, z, y0, p, w_glu, b_glu, w_out, g1, b1, ple_w, w_pg, b_pg, g2, b2)


def _post_bwd_a(dx2_or_target, is_top, xh2, xh1, rstd2, rstd1, q, gt, p, w_pg, g1, b1, g2, b2, layer, token=None):
    L = xh1.shape[0]
    extra, extra_specs = _after(token)

    def body(d_ref, xh2_ref, xh1_ref, rstd2_ref, rstd1_ref, q_ref, gt_ref, p_ref, wpg_ref, g1_ref, b1_ref, g2_ref,
             b2_ref, *rest):
        (dt1_ref, dpw_out, dwpg_out, dbpg_ref, dg1_ref, db1_ref, dg2_ref, db2_ref, loss_ref, dpw_ref,
         dwpg_ref) = rest[len(extra):]
        @pl.when(pl.program_id(0) == 0)
        def _():
            for ref in (dpw_ref, dwpg_ref, dbpg_ref, dg1_ref, db1_ref, dg2_ref, db2_ref, loss_ref):
                ref[...] = jnp.zeros_like(ref)

        g1, g2 = g1_ref[...], g2_ref[...]
        xh1, xh2, rstd1, rstd2 = xh1_ref[...], xh2_ref[...], rstd1_ref[...], rstd2_ref[...]
        x1 = xh1 * g1 + b1_ref[...]
        if is_top:
            err = (xh2 * g2 + b2_ref[...]) - d_ref[...]
            loss_ref[...] += _colsum(err * err)
            dx2 = err * (1.0 / D_MODEL)
        else:
            dx2 = d_ref[...]
        p = p_ref[...]
        q, gt = q_ref[...], gt_ref[...]
        dg2_ref[...] += _colsum(dx2 * xh2)
        db2_ref[...] += _colsum(dx2)
        dt2 = _ln_bwd(dx2, xh2, rstd2, g2)
        dq = dt2 * gt
        dgpre = (dt2 * q) * gt * (1.0 - gt)
        dpw_ref[...] += _mm_tn(p, dq)
        dwpg_ref[...] += _mm_tn(x1, dgpre)
        dbpg_ref[...] += _colsum(dgpre)
        dx1 = ALPHA * dt2 + _mm_nt(dgpre, wpg_ref[...])
        dg1_ref[...] += _colsum(dx1 * xh1)
        db1_ref[...] += _colsum(dx1)
        dt1_ref[...] = _ln_bwd(dx1, xh1, rstd1, g1)

        @pl.when(pl.program_id(0) == L // TM - 1)
        def _():
            dpw_out[...] = dpw_ref[...].astype(WIRE)
            dwpg_out[...] = dwpg_ref[...].astype(WIRE)

    vec, lvec = _full((1, D_MODEL)), _lrow(layer, D_MODEL)
    return pl.pallas_call(
        body, name="post_bwd_a_top" if is_top else "post_bwd_a", grid=(L // TM,),
        in_specs=[_row(D_MODEL), _row(D_MODEL), _row(D_MODEL), _row(1), _row(1), _row(D_MODEL), _row(D_MODEL), _row(PLE_D),
                  _full((D_MODEL, D_MODEL)), lvec, lvec, lvec, lvec] + extra_specs,
        out_specs=[_row(D_MODEL), _full((PLE_D, D_MODEL)), _full((D_MODEL, D_MODEL)), vec, vec, vec, vec, vec, vec],
        out_shape=[_S((L, D_MODEL)), _S((PLE_D, D_MODEL), WIRE), _S((D_MODEL, D_MODEL), WIRE)] + [_S((1, D_MODEL))] * 6,
        scratch_shapes=[pltpu.VMEM((PLE_D, D_MODEL), F32), pltpu.VMEM((D_MODEL, D_MODEL), F32)],
        compiler_params=_params(1))(dx2_or_target, xh2, xh1, rstd2, rstd1, q, gt, p, w_pg, g1, b1, g2, b2, *extra)


def _post_bwd_b(dt1, m, z, hs, y0, w_out, w_glu, b_glu, layer):
    L = dt1.shape[0]

    def body(dt1_ref, m_ref, z_ref, hs_ref, y0_ref, wo_ref, wg_ref, bg_ref,
             dhs_ref, dy0_ref, dzg_ref, dwo_out, dwg_out, dbg_ref, dwo_ref, dwg_ref):
        @pl.when(pl.program_id(0) == 0)
        def _():
            for ref in (dwo_ref, dwg_ref, dbg_ref):
                ref[...] = jnp.zeros_like(ref)

        dt1b = dt1_ref[...].astype(MXU)
        dm = _mm_nt(dt1b, wo_ref[...])
        dwo_ref[...] += _mm_tn(m_ref[...], dt1b)
        d_rgy, d_s5y = dm[:, :RG_W], dm[:, RG_W:]
        rg_gate = z_ref[:, C_RGG:C_RGG + RG_W]
        s5_gate = z_ref[:, C_S5G:C_S5G + S5_W]
        sl, dsl = _silu_and_grad(rg_gate)
        dhs_ref[...] = d_rgy * sl
        dzg_ref[:, :RG_W] = d_rgy * hs_ref[...] * dsl
        y0 = y0_ref[...]
        y1 = _gelu(y0)
        gl = _sigmoid(_mm(y1, wg_ref[...]) + bg_ref[...])
        sl, dsl = _silu_and_grad(s5_gate)
        dy2 = d_s5y * sl
        dzg_ref[:, RG_W:] = d_s5y * (y1 * gl) * dsl
        dglpre = (dy2 * y1) * gl * (1.0 - gl)
        dwg_ref[...] += _mm_tn(y1, dglpre)
        dbg_ref[...] += _colsum(dglpre)
        dy1 = dy2 * gl + _mm_nt(dglpre, wg_ref[...])
        dy0_ref[...] = dy1 * _gelu_grad(y0)

        @pl.when(pl.program_id(0) == L // TM - 1)
        def _():
            dwo_out[...] = dwo_ref[...].astype(WIRE)
            dwg_out[...] = dwg_ref[...].astype(WIRE)

    return pl.pallas_call(
        body, name="post_bwd_b", grid=(L // TM,),
        in_specs=[_row(D_MODEL), _row(D_MODEL), _row(Z_W), _row(RG_W), _row(S5_W), _full((D_MODEL, D_MODEL)),
                  _full((S5_W, S5_W)), _lrow(layer, S5_W)],
        out_specs=[_row(RG_W), _row(S5_W), _row(D_MODEL), _full((D_MODEL, D_MODEL)), _full((S5_W, S5_W)), _full((1, S5_W))],
        out_shape=[_S((L, RG_W)), _S((L, S5_W)), _S((L, D_MODEL)), _S((D_MODEL, D_MODEL), WIRE), _S((S5_W, S5_W), WIRE),
                   _S((1, S5_W))],
        scratch_shapes=[pltpu.VMEM((D_MODEL, D_MODEL), F32), pltpu.VMEM((S5_W, S5_W), F32)],
        compiler_params=_params(1))(dt1, m, z, hs, y0, w_out, w_glu, b_glu)


def _adamw(parts, w, m, v, token=None):
    nl = len(parts)
    extra, extra_specs = _after(token)
    n, R, C = parts[0].shape
    tr = R
    for cand in (512, 256, 128, 64, 32, 16, 8):
        if R % cand == 0 and n * cand * C * 4 <= 4 * 1024 * 1024:
            tr = cand
            break
    nblk = R // tr

    def body(*refs):
        p_refs = refs[:nl]
        w_ref, m_ref, v_ref = refs[nl:nl + 3]
        g_ref, d_ref, nm_ref, nv_ref = refs[nl + 3 + len(extra):]
        layer = pl.program_id(0)
        g = None
        for li, p_ref in enumerate(p_refs):
            s = p_ref[0].astype(F32)
            for k in range(1, n):
                s = s + p_ref[k].astype(F32)
            g = s if g is None else jnp.where(layer == li, s, g)
        nm = B1 * m_ref[...] + (1.0 - B1) * g
        nv = B2 * v_ref[...] + (1.0 - B2) * (g * g)
        d_ref[...] = (-LR) * ((nm / BC1) / (jnp.sqrt(nv / BC2) + EPS) + WD * w_ref[...])
        g_ref[...], nm_ref[...], nv_ref[...] = g, nm, nv

    def part_spec(li):
        return pl.BlockSpec((n, tr, C), lambda l, i: (0, jnp.where(l == li, i, jnp.where(l < li, 0, nblk - 1)), 0))

    blk = pl.BlockSpec((tr, C), lambda l, i: (l * nblk + i, 0))
    return pl.pallas_call(
        body, name="adamw", grid=(nl, nblk),
        in_specs=[part_spec(li) for li in range(nl)] + [blk, blk, blk] + extra_specs,
        out_specs=[blk] * 4, out_shape=[_S((nl * R, C))] * 4, compiler_params=_params(2))(*parts, w, m, v, *extra)


def _adamw_natural(names, g, w, m, v, name):
    n = len(names)

    def body(*refs):
        for j in range(n):
            g_ref, w_ref, m_ref, v_ref, d_ref, nm_ref, nv_ref = (refs[k * n + j] for k in range(7))
            gj = g_ref[...]
            nm = B1 * m_ref[...] + (1.0 - B1) * gj
            nv = B2 * v_ref[...] + (1.0 - B2) * (gj * gj)
            d_ref[...] = (-LR) * ((nm / BC1) / (jnp.sqrt(nv / BC2) + EPS) + WD * w_ref[...])
            nm_ref[...], nv_ref[...] = nm, nv

    ins = [t[k] for t in (g, w, m, v) for k in names]
    outs = pl.pallas_call(body, name=name, out_shape=[_S(w[k].shape) for _ in range(3) for k in names],
                          compiler_params=pltpu.CompilerParams(vmem_limit_bytes=VMEM_LIMIT))(*ins)
    return [{k: outs[t * n + j] for j, k in enumerate(names)} for t in range(3)]


def _me():
    return lax.axis_index("x"), lax.axis_index("y"), lax.axis_index("c")


def _lin(dev):
    return 4 * dev[0] + 2 * dev[1] + dev[2]


def _blk(ref, axis, size, idx):
    nd = len(ref.shape)
    start = idx * size
    if axis == nd - 1 and size % LANE == 0:
        start = pl.multiple_of(start, LANE)
    elif axis == nd - 2 and size % 16 == 0:
        start = pl.multiple_of(start, 16)
    ix = [slice(None)] * nd
    ix[axis] = pl.ds(start, size)
    return ref.at[tuple(ix)]


def _all_gather(shards, axes, name):
    n = len(shards)
    sizes = [s.shape[a] for s, a in zip(shards, axes)]
    out_shapes = [_S(s.shape[:a] + (N_DEV * s.shape[a],) + s.shape[a + 1:], s.dtype) for s, a in zip(shards, axes)]

    def body(*refs):
        ins, outs = refs[:n], refs[n:2 * n]
        send_sems, recv_sems, local_sems = refs[2 * n:]
        x, y, c = _me()
        me, sibling = (x, y, c), (x, y, 1 - c)
        chips = [(1 - x, y), (x, 1 - y), (1 - x, 1 - y)]

        def copy(a, k, block, to, from_input=False):
            dst = _blk(outs[a], axes[a], sizes[a], _lin(block))
            return pltpu.make_async_remote_copy(
                src_ref=ins[a] if from_input else dst, dst_ref=dst, send_sem=send_sems.at[a, k],
                recv_sem=recv_sems.at[a, k], device_id=to, device_id_type=MESH)

        mine = [pltpu.make_async_copy(ins[a], _blk(outs[a], axes[a], sizes[a], _lin(me)), local_sems.at[a]) for a in range(n)]
        for cp in mine:
            cp.start()
        first = []
        for a in range(n):
            first.append(copy(a, 0, me, sibling, True))
            first += [copy(a, 1 + j, me, (*chip, c), True) for j, chip in enumerate(chips)]
        for cp in first:
            cp.start()
        passed = []
        for j, chip in enumerate(chips):
            for a in range(n):
                copy(a, 1 + j, (*chip, c), me).wait_recv()
                cp = copy(a, 4 + j, (*chip, c), sibling)
                cp.start()
                passed.append(cp)
        for a in range(n):
            copy(a, 0, sibling, me).wait_recv()
            for j, chip in enumerate(chips):
                copy(a, 4 + j, (*chip, 1 - c), me).wait_recv()
        for cp in first + passed:
            cp.wait_send()
        for cp in mine:
            cp.wait()

    return pl.pallas_call(
        body, name=name, out_shape=out_shapes, in_specs=[ANY] * n, out_specs=[ANY] * n,
        scratch_shapes=[pltpu.SemaphoreType.DMA((n, 7)), pltpu.SemaphoreType.DMA((n, 7)), pltpu.SemaphoreType.DMA((n,))],
    )(*shards)


HBM_SPEC = pl.BlockSpec(memory_space=pltpu.HBM)
SEM_SPEC = pl.BlockSpec(memory_space=pltpu.SEMAPHORE)
EFFECT = pltpu.SideEffectType.DATAFLOW_SIDE_EFFECTING


def _peers(x, y, c):
    flip = lambda v, f: 1 - v if f else v
    return [(flip(x, k & 4), flip(y, k & 2), flip(c, k & 1)) for k in range(1, N_DEV)]


def _land_shape(mode, s, axis):
    if mode == "gather":
        return s.shape[:axis] + (N_DEV * s.shape[axis],) + s.shape[axis + 1:]
    return (N_DEV,) + s.shape[:axis] + (s.shape[axis] // N_DEV,) + s.shape[axis + 1:]


def _src_view(mode, ref, axis, peer):
    return ref if mode == "gather" else _blk(ref, axis, ref.shape[axis] // N_DEV, peer)


def _dst_view(mode, land, axis, sender):
    return _blk(land, axis, land.shape[axis] // N_DEV, sender) if mode == "gather" else land.at[sender]


def _seven_blocks(mode, land, axis):
    if mode == "gather":
        ix = [slice(None)] * len(land.shape)
        ix[axis] = pl.ds(0, (N_DEV - 1) * (land.shape[axis] // N_DEV))
        return land.at[tuple(ix)]
    return land.at[pl.ds(0, N_DEV - 1)]


def _place_own(mode, srcs, axes, name, after=None):
    n = len(srcs)
    extra, extra_specs = _after(after)

    def body(me_ref, *refs):
        for a in range(n):
            out = refs[n + len(extra) + a]
            out[...] = refs[a][...].reshape(out.shape)

    def at_me(shape, axis):
        return lambda i, me: tuple(me[0] if d == axis else 0 for d in range(len(shape)))

    in_specs, out_specs = [], []
    for s, axis in zip(srcs, axes):
        if mode == "gather":
            in_specs.append(pl.BlockSpec(s.shape, lambda i, me, nd=len(s.shape): (0,) * nd))
            out_specs.append(pl.BlockSpec(s.shape, at_me(s.shape, axis)))
        else:
            blk = s.shape[:axis] + (s.shape[axis] // N_DEV,) + s.shape[axis + 1:]
            in_specs.append(pl.BlockSpec(blk, at_me(blk, axis)))
            out_specs.append(pl.BlockSpec((1,) + blk, at_me((1,) + blk, 0)))
    me = _lin(_me()).astype(jnp.int32).reshape(1)
    return pl.pallas_call(
        body, name=name, out_shape=[_S(_land_shape(mode, s, a), s.dtype) for s, a in zip(srcs, axes)],
        grid_spec=pltpu.PrefetchScalarGridSpec(num_scalar_prefetch=1, grid=(1,), in_specs=in_specs + extra_specs,
                                               out_specs=out_specs),
        compiler_params=_params(1))(me, *srcs, *extra)


def _push_start(mode, srcs, lands, axes, name):
    n = len(srcs)

    def body(*refs):
        src_refs, land_refs = refs[:n], refs[n:2 * n]
        send_sems, recv_sems = refs[2 * n], refs[2 * n + 1]
        token = refs[-1]
        x, y, c = _me()
        me = _lin((x, y, c))
        for a in range(n):
            for peer in _peers(x, y, c):
                pltpu.make_async_remote_copy(
                    src_ref=_src_view(mode, src_refs[a], axes[a], _lin(peer)),
                    dst_ref=_dst_view(mode, land_refs[a], axes[a], me),
                    send_sem=send_sems.at[a], recv_sem=recv_sems.at[a], device_id=peer, device_id_type=MESH).start()
        token[...] = jnp.zeros_like(token)

    hbm = lambda s: pltpu.HBM(s.shape, s.dtype)
    outs = pl.pallas_call(
        body, name=name,
        out_shape=(pltpu.SemaphoreType.DMA((n,)), pltpu.SemaphoreType.DMA((n,)), *[hbm(s) for s in srcs], *[hbm(s) for s in lands],
                   _S((SUB, LANE))),
        in_specs=[HBM_SPEC] * (2 * n),
        out_specs=(SEM_SPEC, SEM_SPEC, *[HBM_SPEC] * (2 * n), pl.BlockSpec(memory_space=pltpu.VMEM)),
        input_output_aliases={i: 2 + i for i in range(2 * n)},
        compiler_params=pltpu.CompilerParams(has_side_effects=EFFECT),
    )(*[pltpu.with_memory_space_constraint(s, pltpu.HBM) for s in list(srcs) + list(lands)])
    return outs[0], outs[1], outs[2:2 + n], outs[2 + n:2 + 2 * n], outs[-1]


def _push_wait(mode, send_sems, recv_sems, srcs, lands, axes, after, name):
    n = len(srcs)

    def body(*refs):
        land_refs = refs[n:2 * n]
        send_sems, recv_sems = refs[2 * n], refs[2 * n + 1]
        x, y, c = _me()
        for a in range(n):
            seven = _seven_blocks(mode, land_refs[a], axes[a])
            cp = pltpu.make_async_remote_copy(src_ref=seven, dst_ref=seven, send_sem=send_sems.at[a], recv_sem=recv_sems.at[a],
                                              device_id=(x, y, 1 - c), device_id_type=MESH)
            cp.wait_send()
            cp.wait_recv()

    hbm = lambda s: pltpu.HBM(s.shape, s.dtype)
    outs = pl.pallas_call(
        body, name=name, out_shape=tuple(hbm(s) for s in list(srcs) + list(lands)),
        in_specs=[HBM_SPEC] * (2 * n) + [SEM_SPEC, SEM_SPEC, ANY], out_specs=tuple([HBM_SPEC] * (2 * n)),
        input_output_aliases={i: i for i in range(2 * n)},
        compiler_params=pltpu.CompilerParams(has_side_effects=EFFECT),
    )(*srcs, *lands, send_sems, recv_sems, after)
    return outs[n:]


def _sum_parts(parts):
    n, R, C = parts.shape

    def body(p_ref, o_ref):
        g = p_ref[0]
        for k in range(1, n):
            g = g + p_ref[k]
        o_ref[...] = g

    return pl.pallas_call(body, name="sum_parts", out_shape=_S((R, C)))(parts)


def _block_diag(w, nb):
    tn, r, c = w.shape
    w = w.reshape(tn // nb, nb, r, c)
    return jnp.einsum('tarc,ab->tarbc', w, jnp.eye(nb, dtype=w.dtype)).reshape(tn // nb, nb * r, nb * c)


def _block_diag_extract(w, nb):
    t, R, C = w.shape
    w = w.reshape(t, nb, R // nb, nb, C // nb)
    return jnp.einsum('tarbc,ab->tarc', w, jnp.eye(nb, dtype=w.dtype)).reshape(t * nb, R // nb, C // nb)


SMALL = ['conv_b', 'rg_wa', 'rg_ba', 'rg_wx', 'rg_bx', 'rg_lambda', 's5_a_re', 's5_a_im', 's5_b_re', 's5_b_im',
         's5_c_re', 's5_c_im', 's5_d', 's5_log_step', 's5_b_glu', 'ln1_g', 'ln1_b', 'ple_gate_b', 'ln2_g', 'ln2_b']
WEIGHTS = ['w_in', 'conv_w', 'conv_b', 'rg_wa', 'rg_ba', 'rg_wx', 'rg_bx', 'rg_lambda', 's5_a_re', 's5_a_im', 's5_b_re',
           's5_b_im', 's5_c_re', 's5_c_im', 's5_d', 's5_log_step', 's5_w_glu', 's5_b_glu', 'w_out', 'ln1_g', 'ln1_b',
           'ple_w', 'ple_gate_w', 'ple_gate_b', 'ln2_g', 'ln2_b']
PACK_ROWS_MULT = 64


def _pack(tree, scalar):
    flat = jnp.concatenate([tree[k].reshape(-1) for k in SMALL] + [scalar.reshape(1)])
    rows = -(-flat.shape[0] // (LANE * PACK_ROWS_MULT)) * PACK_ROWS_MULT
    return jnp.pad(flat, (0, rows * LANE - flat.shape[0])).reshape(rows, LANE)


def _unpack(packed, like):
    flat, out, o = packed.reshape(-1), {}, 0
    for k in SMALL:
        n = math.prod(like[k].shape)
        out[k] = flat[o:o + n].reshape(like[k].shape)
        o += n
    return out, flat[o]


class _NoHooks:
    token = None
    first_token = None

    def first_weights(self, full, after):
        return full

    def layer_start(self, i, W, after):
        return W

    def late_weights(self, i, W, after):
        return W

    def post_done(self, i, g):
        return None

    def w_in_done(self, i, g):
        return None

    def layer_done(self, i, g, dx):
        return None


def _local_grads(x, p, target, W, disc, hooks):
    depth = 2
    saved = []
    for i in range(depth):
        if i > 0:
            W = hooks.layer_start(i, W, x)
        w = W[i]
        z = _inproj_fwd(x, w['w_in'], hooks.token if i == 0 else None)
        hs = _rg_fwd(z, w['conv_w'], w['conv_b'], w['wa_bd'], w['wx_bd'], w['rg_ba'], w['rg_bx'], w['rg_lambda'], i)
        d = disc[i]
        y0, s_re, s_im = _s5_fwd(z, d['bb_re'], d['bb_im'], d['lb_re'], d['lb_im'], d['c_re'], d['c_im'], w['s5_d'], i)
        W = hooks.late_weights(i, W, y0)
        w = W[i]
        x2, *norms = _post_fwd(x, hs, z, y0, p[i], w['s5_w_glu'], w['s5_b_glu'], w['w_out'], w['ln1_g'], w['ln1_b'],
                               w['ple_w'], w['ple_gate_w'], w['ple_gate_b'], w['ln2_g'], w['ln2_b'], i)
        saved.append((x, z, hs, y0, s_re, s_im, norms))
        x = x2

    grads = [None] * depth
    dx = target
    loss = None
    token = None
    for i in reversed(range(depth)):
        w, d = W[i], disc[i]
        xin, z, hs, y0, s_re, s_im, (xh1, xh2, m, q, gt, rstd1, rstd2) = saved[i]
        g = {}
        (dt1, g['ple_w'], g['ple_gate_w'], g['ple_gate_b'], g['ln1_g'], g['ln1_b'], g['ln2_g'], g['ln2_b'], lrow) = _post_bwd_a(
            dx, i == depth - 1, xh2, xh1, rstd2, rstd1, q, gt, p[i], w['ple_gate_w'], w['ln1_g'], w['ln1_b'],
            w['ln2_g'], w['ln2_b'], i, token)
        if i == depth - 1:
            loss = 0.5 / D_MODEL * jnp.sum(lrow)
        dhs, dy0, dzg, g['w_out'], g['s5_w_glu'], g['s5_b_glu'] = _post_bwd_b(dt1, m, z, hs, y0, w['w_out'], w['s5_w_glu'],
                                                                           w['s5_b_glu'], i)
        (dzu, g['bb_re'], g['bb_im'], g['lb_re'], g['lb_im'], g['c_re'], g['c_im'], g['s5_d']) = _s5_bwd(
            dy0, z, s_re, s_im, d['bb_re'], d['bb_im'], d['lb_re'], d['lb_im'], d['c_re'], d['c_im'], w['s5_d'], i,
            hooks.post_done(i, g))
        (dzx, g['conv_w'], g['conv_b'], g['wa_bd'], g['wx_bd'], g['rg_ba'], g['rg_bx'], g['rg_lambda']) = _rg_bwd(
            dhs, z, hs, w['conv_w'], w['conv_b'], w['wa_bd'], w['wx_bd'], w['rg_ba'], w['rg_bx'], w['rg_lambda'], i)
        if i == 0:
            g['w_in'] = _inproj_bwd_dw(xin, dzx, dzg, dzu)
            dx = _inproj_bwd_dx(dt1, dzx, dzg, dzu, w['w_in'], hooks.w_in_done(i, g))
        else:
            dx, g['w_in'] = _inproj_bwd(dt1, xin, dzx, dzg, dzu, w['w_in'])
        grads[i] = g
        token = hooks.layer_done(i, g, dx)
    return loss, dx, grads


def _s5_layouts_fwd(s5_a_re, s5_a_im, s5_log_step, s5_b_re, s5_b_im, s5_c_re, s5_c_im, token=None):
    depth = s5_a_re.shape[0]
    ar, ai = s5_a_re.reshape(depth * 24, S5_P), s5_a_im.reshape(depth * 24, S5_P)
    ls = s5_log_step.reshape(depth * 24, 1)
    lr, li, cr, ci = _s5_disc_fwd(ar, ai, ls, token)
    col = lambda a: a.reshape(depth * S5_N, 1)
    br, bi = s5_b_re.reshape(depth * S5_N, 16), s5_b_im.reshape(depth * S5_N, 16)
    bbr, bbi = _s5_bscale_fwd(col(cr), col(ci), br, bi)
    disc = []
    for i in range(depth):
        gph = lambda a: a.reshape(depth, 24, S5_P, 16)[i]
        disc.append(dict(
            bb_re=_block_diag(jnp.swapaxes(gph(bbr), 1, 2), 8), bb_im=_block_diag(jnp.swapaxes(gph(bbi), 1, 2), 8),
            lb_re=lr.reshape(depth, 1, S5_N)[i], lb_im=li.reshape(depth, 1, S5_N)[i],
            c_re=_block_diag(jnp.swapaxes(s5_c_re[i], 1, 2), 8), c_im=_block_diag(jnp.swapaxes(s5_c_im[i], 1, 2), 8)))
    return disc, (ar, ai, ls, col(cr), col(ci), br, bi)


def _s5_layouts_bwd(grads, res):
    ar, ai, ls, cr, ci, br, bi = res
    depth = len(grads)
    stack = lambda f: jnp.stack([f(g) for g in grads])
    dbbr = stack(lambda g: jnp.swapaxes(_block_diag_extract(g['bb_re'], 8), 1, 2)).reshape(depth * S5_N, 16)
    dbbi = stack(lambda g: jnp.swapaxes(_block_diag_extract(g['bb_im'], 8), 1, 2)).reshape(depth * S5_N, 16)
    dbr, dbi, dcr, dci = _s5_bscale_bwd(cr, ci, br, bi, dbbr, dbbi)
    gp = lambda a: a.reshape(depth * 24, S5_P)
    dar, dai, dls = _s5_disc_bwd(ar, ai, ls, gp(stack(lambda g: g['lb_re'])), gp(stack(lambda g: g['lb_im'])), gp(dcr), gp(dci))
    return dict(
        s5_a_re=dar.reshape(depth, 24, S5_P), s5_a_im=dai.reshape(depth, 24, S5_P), s5_log_step=dls.reshape(depth, 24),
        s5_b_re=dbr.reshape(depth, 24, S5_P, 16), s5_b_im=dbi.reshape(depth, 24, S5_P, 16),
        s5_c_re=stack(lambda g: jnp.swapaxes(_block_diag_extract(g['c_re'], 8), 1, 2)),
        s5_c_im=stack(lambda g: jnp.swapaxes(_block_diag_extract(g['c_im'], 8), 1, 2)))


LATE = ('w_out', 'ple_w', 'ple_gate_w', 's5_w_glu')


ROWS = ('conv_b', 'rg_ba', 'rg_bx', 'rg_lambda', 's5_d', 's5_b_glu', 'ln1_g', 'ln1_b', 'ple_gate_b', 'ln2_g', 'ln2_b')


def _shared_weights(full):
    depth = full['conv_b'].shape[0]
    shared = {k: full[k].reshape(depth, 1, -1) for k in ROWS}
    shared['conv_w'] = full['conv_w']
    shared['wa_bd'] = _block_diag(full['rg_wa'].reshape(depth * 10, 64, 64), 2)
    shared['wx_bd'] = _block_diag(full['rg_wx'].reshape(depth * 10, 64, 64), 2)
    return shared


def _layer_weights(full, shared, i):
    return dict(shared, w_in=full['w_in'][i])


class _AllLocal(_NoHooks):
    def __init__(self, full):
        self.full = full

    def late_weights(self, i, W, after):
        W[i].update({k: self.full[k][i] for k in LATE})
        return W


def _full_grads(full, x, p, target, hooks=None):
    hooks = hooks or _AllLocal(full)
    disc, res = _s5_layouts_fwd(full['s5_a_re'], full['s5_a_im'], full['s5_log_step'], full['s5_b_re'], full['s5_b_im'],
                                full['s5_c_re'], full['s5_c_im'], hooks.first_token)
    full = hooks.first_weights(full, disc[-1]['bb_im'])
    shared = _shared_weights(full)
    W = [_layer_weights(full, shared, i) for i in range(2)]
    loss, gx, grads = _local_grads(x, p, target, W, disc, hooks)
    stack = lambda f: jnp.stack([f(g) for g in grads])
    out = _s5_layouts_bwd(grads, res)
    for k in SHARD_AXIS:
        out[k] = [g[k] for g in grads]
    out['conv_w'] = stack(lambda g: g['conv_w'])
    for k in ('conv_b', 'rg_ba', 'rg_bx', 'rg_lambda', 's5_b_glu', 'ln1_g', 'ln1_b', 'ple_gate_b', 'ln2_g', 'ln2_b'):
        out[k] = stack(lambda g: g[k][0])
    out['s5_d'] = stack(lambda g: g['s5_d'][0]).reshape(2, 24, 16)
    out['rg_wa'] = stack(lambda g: _block_diag_extract(g['wa_bd'], 2))
    out['rg_wx'] = stack(lambda g: _block_diag_extract(g['wx_bd'], 2))
    return loss, gx, out


SHARD_AXIS = {'w_in': 2, 'w_out': 1, 'ple_w': 2, 'ple_gate_w': 1, 's5_w_glu': 1}


def kernel(x, p, w_in, conv_w, conv_b, rg_wa, rg_ba, rg_wx, rg_bx, rg_lambda, s5_a_re, s5_a_im, s5_b_re, s5_b_im, s5_c_re, s5_c_im, s5_d, s5_log_step, s5_w_glu, s5_b_glu, w_out, ln1_g, ln1_b, ple_w, ple_gate_w, ple_gate_b, ln2_g, ln2_b, loss_target, m_w_in, m_conv_w, m_conv_b, m_rg_wa, m_rg_ba, m_rg_wx, m_rg_bx, m_rg_lambda, m_s5_a_re, m_s5_a_im, m_s5_b_re, m_s5_b_im, m_s5_c_re, m_s5_c_im, m_s5_d, m_s5_log_step, m_s5_w_glu, m_s5_b_glu, m_w_out, m_ln1_g, m_ln1_b, m_ple_w, m_ple_gate_w, m_ple_gate_b, m_ln2_g, m_ln2_b, v_w_in, v_conv_w, v_conv_b, v_rg_wa, v_rg_ba, v_rg_wx, v_rg_bx, v_rg_lambda, v_s5_a_re, v_s5_a_im, v_s5_b_re, v_s5_b_im, v_s5_c_re, v_s5_c_im, v_s5_d, v_s5_log_step, v_s5_w_glu, v_s5_b_glu, v_w_out, v_ln1_g, v_ln1_b, v_ple_w, v_ple_gate_w, v_ple_gate_b, v_ln2_g, v_ln2_b):
    local = dict(locals())
    w = {k: local[k] for k in WEIGHTS}
    mom = {k: local['m_' + k] for k in WEIGHTS}
    var = {k: local['v_' + k] for k in WEIGHTS}

    big = list(SHARD_AXIS)
    wire = {k: w[k].astype(WIRE) for k in big}
    late_axes = [SHARD_AXIS[k] - 1 for k in LATE]
    pushed = {}

    def push_weights(key, srcs, axes, after):
        pushed[key] = _push_start("gather", srcs, _place_own("gather", srcs, axes, "place_weights_" + key, after=after), axes,
                                  "push_weights_" + key)
        return pushed[key][4]

    def await_weights(key, axes, after):
        s = pushed[key]
        return _push_wait("gather", s[0], s[1], s[2], s[3], axes, after, "await_weights_" + key)

    token_first = push_weights("first", [wire['w_in'][0][None], conv_w[None]], [0, 0], None)
    token0 = push_weights("l0", [wire[k][0] for k in LATE], late_axes, token_first)
    push_weights("l1", [wire['w_in'][1][None]] + [wire[k][1] for k in LATE], [0] + late_axes, token0)

    def push_grads(key, g, names, axes):
        srcs = [g[k] for k in names]
        pushed[key] = _push_start("scatter", srcs, _place_own("scatter", srcs, axes, "place_grads_" + key), axes,
                                  "push_grads_" + key)
        return pushed[key][4]

    def await_grads(key, axes, after):
        s = pushed[key]
        return _push_wait("scatter", s[0], s[1], s[2], s[3], axes, after, "await_grads_" + key)

    def columns(blocks):
        return jnp.swapaxes(blocks, 0, 1).reshape(D_MODEL, Z_W)

    class Overlap(_NoHooks):
        token = pushed["l1"][4]
        first_token = token

        def first_weights(self, full, after):
            w_in0, conv = await_weights("first", [0, 0], after)
            return dict(full, w_in=[columns(w_in0), None], conv_w=jnp.moveaxis(conv, 0, 2).reshape(2, 4, RG_W))

        def late_weights(self, i, W, after):
            if i == 0:
                W[0].update(zip(LATE, await_weights("l0", late_axes, after)))
            return W

        def layer_start(self, i, W, after):
            lands = await_weights("l1", [0] + late_axes, after)
            W[1].update(zip(LATE, lands[1:]), w_in=columns(lands[0]))
            return W

        def post_done(self, i, g):
            return push_grads("late0", g, LATE, late_axes) if i == 0 else None

        def w_in_done(self, i, g):
            return push_grads("w_in0", g, ['w_in'], [0])

        def layer_done(self, i, g, dx):
            return push_grads("all1", g, ['w_in'] + list(LATE), [0] + late_axes) if i == 1 else None

    local_loss, grad_x, g = _full_grads(dict(w), x[0], p[:, 0], loss_target[0], Overlap())

    conv_blocks = jnp.moveaxis(g['conv_w'].reshape(2, 4, N_DEV, RG_W // N_DEV), 2, 0).reshape(N_DEV, 8, RG_W // N_DEV)
    packed = _pack(g, local_loss)
    token = push_grads("small", dict(conv_w=conv_blocks, small=packed), ['conv_w', 'small'], [0, 0])
    recv1 = dict(zip(['w_in'] + list(LATE), await_grads("all1", [0] + late_axes, grad_x)))
    recv0 = dict(zip(LATE, await_grads("late0", late_axes, grad_x)))
    outs = {}

    def update(k, parts, token=None):
        shard = w[k].shape
        c = shard[-1]
        two = lambda a: a.reshape(-1, c)
        res = _adamw([r.reshape(N_DEV, -1, c) for r in parts], two(w[k]), two(mom[k]), two(var[k]), token)
        outs[k] = [o.reshape(shard) for o in res]

    for k in LATE:
        update(k, [recv0[k], recv1[k]], token)
        token = None
    w_in0, = await_grads("w_in0", [0], outs[LATE[-1]][1])
    update('w_in', [w_in0, recv1['w_in']])
    conv_parts, small_parts = await_grads("small", [0, 0], outs['w_in'][1])
    update('conv_w', [conv_parts])

    rows = packed.shape[0] // N_DEV
    mine = _sum_parts(small_parts.reshape(N_DEV, rows, LANE))
    summed, loss = _unpack(_all_gather([mine], [0], "gather_small_grads")[0], w)
    narrow = ['s5_b_re', 's5_b_im']
    for names, name in ((narrow, "adamw_s5_b"), ([k for k in SMALL if k not in narrow], "adamw_small")):
        delta, new_m, new_v = _adamw_natural(names, summed, w, mom, var, name)
        for k in names:
            outs[k] = [summed[k], delta[k], new_m[k], new_v[k]]

    res = [loss, grad_x[None]]
    for j in range(4):
        res += [outs[k][j] for k in WEIGHTS]
    return tuple(res)
```

```python
import math

import jax
import jax.numpy as jnp
from jax import lax
from jax.experimental import pallas as pl
from jax.experimental.pallas import tpu as pltpu

F32 = jnp.float32
MXU = jnp.bfloat16
WIRE = jnp.bfloat16

N_DEV = 8
D_MODEL = 1024
PLE_D = 256
RG_W = 640
S5_W = 384
S5_P = 64
S5_N = 24 * S5_P
Z_W = 2 * RG_W + 2 * S5_W
C_RGG = RG_W
C_S5U = 2 * RG_W
C_S5G = 2 * RG_W + S5_W
LANE = 128
N_RG_T = RG_W // LANE
N_S5_T = S5_W // LANE
W_BLK = Z_W // N_DEV
ALPHA = (2.0 * 2) ** 0.25
LN_EPS = 1e-5
RG_C = 8.0
LR, B1, B2, EPS, WD, STEP = 0.001, 0.9, 0.999, 1e-08, 0.01, 10
BC1 = 1.0 - B1 ** STEP
BC2 = 1.0 - B2 ** STEP
RC = 256
TM = 256
TM_MM = 1024
VMEM_LIMIT = 56 * 1024 * 1024

MESH = pl.DeviceIdType.MESH
ANY = pl.BlockSpec(memory_space=pl.ANY)


def _params(n_grid_axes, vmem=VMEM_LIMIT):
    return pltpu.CompilerParams(dimension_semantics=("arbitrary",) * n_grid_axes, vmem_limit_bytes=vmem)


def _S(shape, dtype=F32):
    return jax.ShapeDtypeStruct(tuple(shape), dtype)


def _sigmoid(x):
    return 0.5 * jnp.tanh(0.5 * x) + 0.5


def _silu_and_grad(x):
    s = _sigmoid(x)
    return x * s, s * (1.0 + x * (1.0 - s))


_GELU_C = math.sqrt(2.0 / math.pi)


def _gelu(x):
    return 0.5 * x * (1.0 + jnp.tanh(_GELU_C * (x + 0.044715 * (x * x * x))))


def _gelu_grad(x):
    th = jnp.tanh(_GELU_C * (x + 0.044715 * (x * x * x)))
    return 0.5 * (1.0 + th) + 0.5 * x * (1.0 - th * th) * (_GELU_C * (1.0 + 3.0 * 0.044715 * (x * x)))


def _mm(a, b):
    return jnp.dot(a.astype(MXU), b.astype(MXU), preferred_element_type=F32)


def _mm_nt(a, b):
    return lax.dot_general(a.astype(MXU), b.astype(MXU), (((1,), (1,)), ((), ())), preferred_element_type=F32)


def _mm_tn(a, b):
    return lax.dot_general(a.astype(MXU), b.astype(MXU), (((0,), (0,)), ((), ())), preferred_element_type=F32)


def _ln_fwd(t, g, b):
    mu = jnp.mean(t, axis=-1, keepdims=True)
    tc = t - mu
    var = jnp.mean(tc * tc, axis=-1, keepdims=True)
    rstd = lax.rsqrt(var + LN_EPS)
    xhat = tc * rstd
    return xhat * g + b, xhat, rstd


def _ln_bwd(dy, xhat, rstd, g):
    dxh = dy * g
    m1 = jnp.mean(dxh, axis=-1, keepdims=True)
    m2 = jnp.mean(dxh * xhat, axis=-1, keepdims=True)
    return rstd * (dxh - m1 - xhat * m2)


def _colsum(a):
    return jnp.sum(a, axis=0, keepdims=True)


def _up(x, d, rows, fill):
    n = x.shape[0]
    return jnp.where(rows < n - d, pltpu.roll(x, n - d, 0), fill)


SUB = 8
TILE_STEPS = (1, 2, 4)


def _r8(width):
    return lax.broadcasted_iota(jnp.int32, (SUB, width), 0)


def _scan_real(a, u, carry, reverse=False):
    r8 = _r8(a.shape[1])
    n = a.shape[0] // SUB
    outs = [None] * n
    for k in (reversed(range(n)) if reverse else range(n)):
        A, U = a[SUB * k:SUB * k + SUB], u[SUB * k:SUB * k + SUB]
        for d in TILE_STEPS:
            m = (r8 < SUB - d) if reverse else (r8 >= d)
            sh = SUB - d if reverse else d
            U = A * jnp.where(m, pltpu.roll(U, sh, 0), 0.0) + U
            A = A * jnp.where(m, pltpu.roll(A, sh, 0), 1.0)
        h = A * carry + U
        outs[k] = h
        carry = h[0:1] if reverse else h[SUB - 1:SUB]
    return jnp.concatenate(outs, axis=0), carry


def _tile_powers(lr, li, reverse=False):
    width = lr.shape[1]
    r8 = _r8(width)
    steps = []
    pr, pi = lr, li
    er, ei = jnp.broadcast_to(lr, (SUB, width)), jnp.broadcast_to(li, (SUB, width))
    for d in TILE_STEPS:
        m = (r8 < SUB - d) if reverse else (r8 >= d)
        sh = SUB - d if reverse else d
        steps.append((sh, jnp.where(m, pr, 0.0), jnp.where(m, pi, 0.0)))
        er, ei = _cmul(er, ei, jnp.where(m, pltpu.roll(er, sh, 0), 1.0), jnp.where(m, pltpu.roll(ei, sh, 0), 0.0))
        pr, pi = _cmul(pr, pi, pr, pi)
    return steps, (er, ei)


def _scan_lti(xr, xi, carry, steps, e, reverse=False):
    er, ei = e
    kr, ki = carry
    n = xr.shape[0] // SUB
    outr, outi = [None] * n, [None] * n
    for k in (reversed(range(n)) if reverse else range(n)):
        sr, si = xr[SUB * k:SUB * k + SUB], xi[SUB * k:SUB * k + SUB]
        for sh, pr, pi in steps:
            shr, shi = pltpu.roll(sr, sh, 0), pltpu.roll(si, sh, 0)
            sr, si = sr + (pr * shr - pi * shi), si + (pr * shi + pi * shr)
        sr = sr + (er * kr - ei * ki)
        si = si + (er * ki + ei * kr)
        outr[k], outi[k] = sr, si
        kr, ki = (sr[0:1], si[0:1]) if reverse else (sr[SUB - 1:SUB], si[SUB - 1:SUB])
    return jnp.concatenate(outr, axis=0), jnp.concatenate(outi, axis=0), (kr, ki)


def _halo(ref, c, r0):
    rp = pl.multiple_of(jnp.maximum(r0 - 8, 0), 8)
    return jnp.where(c > 0, ref[pl.ds(rp, 8), :], 0.0)


def _conv_taps(xe):
    return [pltpu.roll(xe, 3, 0)[8:, :], pltpu.roll(xe, 2, 0)[8:, :], pltpu.roll(xe, 1, 0)[8:, :], xe[8:, :]]


def _rg_gates(h, wa, wx, ba, bx, sp):
    r = _sigmoid(_mm(h, wa) + ba)
    i = _sigmoid(_mm(h, wx) + bx)
    log_a = (-RG_C) * r * sp
    a = jnp.exp(log_a)
    mult = jnp.sqrt(-jnp.tanh(log_a) * (a * a + 1.0))
    return r, i, a, mult


def _softplus(y):
    return jnp.maximum(y, 0.0) + jnp.log1p(jnp.exp(-jnp.abs(y)))


def _after(token):
    return ([], []) if token is None else ([token], [ANY])


def _inproj_fwd(x, w_in, token=None):
    L = x.shape[0]

    def body(x_ref, w_ref, *rest):
        rest[-1][...] = _mm(x_ref[...], w_ref[...])

    extra, extra_specs = _after(token)
    tm = min(TM_MM, L)
    return pl.pallas_call(
        body, name="inproj_fwd", grid=(L // tm,),
        in_specs=[pl.BlockSpec((tm, D_MODEL), lambda i: (i, 0)), pl.BlockSpec((D_MODEL, Z_W), lambda i: (0, 0))] + extra_specs,
        out_specs=pl.BlockSpec((tm, Z_W), lambda i: (i, 0)),
        out_shape=_S((L, Z_W)), compiler_params=_params(1))(x, w_in, *extra)


def _inproj_bwd(dt1, x, dzx, dzg, dzu, w_in):
    L = x.shape[0]

    def body(dt1_ref, x_ref, dzx_ref, dzg_ref, dzu_ref, w_ref, dx_ref, dw_ref, acc_ref):
        @pl.when(pl.program_id(0) == 0)
        def _():
            acc_ref[...] = jnp.zeros_like(acc_ref)
        dzg = dzg_ref[...]
        dz = jnp.concatenate([dzx_ref[...], dzg[:, :RG_W], dzu_ref[...], dzg[:, RG_W:]], axis=1).astype(MXU)
        xb = x_ref[...].astype(MXU)
        dx_ref[...] = ALPHA * dt1_ref[...] + _mm_nt(dz, w_ref[...])
        for j in range(N_DEV):
            acc_ref[j] += _mm_tn(xb, dz[:, j * W_BLK:(j + 1) * W_BLK])

        @pl.when(pl.program_id(0) == L // TM - 1)
        def _():
            dw_ref[...] = acc_ref[...].astype(WIRE)

    row = lambda w: pl.BlockSpec((TM, w), lambda i: (i, 0))
    wspec = pl.BlockSpec((N_DEV, D_MODEL, W_BLK), lambda i: (0, 0, 0))
    return pl.pallas_call(
        body, name="inproj_bwd", grid=(L // TM,),
        in_specs=[row(D_MODEL), row(D_MODEL), row(RG_W), row(D_MODEL), row(S5_W),
                  pl.BlockSpec((D_MODEL, Z_W), lambda i: (0, 0))],
        out_specs=[row(D_MODEL), wspec],
        out_shape=[_S((L, D_MODEL)), _S((N_DEV, D_MODEL, W_BLK), WIRE)],
        scratch_shapes=[pltpu.VMEM((N_DEV, D_MODEL, W_BLK), F32)],
        compiler_params=_params(1))(dt1, x, dzx, dzg, dzu, w_in)


TM2 = 512


def _dz_block(dzx_ref, dzg_ref, dzu_ref):
    dzg = dzg_ref[...]
    return jnp.concatenate([dzx_ref[...], dzg[:, :RG_W], dzu_ref[...], dzg[:, RG_W:]], axis=1).astype(MXU)


def _inproj_bwd_dw(x, dzx, dzg, dzu, token=None):
    L = x.shape[0]
    extra, extra_specs = _after(token)

    def body(x_ref, dzx_ref, dzg_ref, dzu_ref, *rest):
        dw_ref, acc_ref = rest[len(extra):]
        @pl.when(pl.program_id(0) == 0)
        def _():
            acc_ref[...] = jnp.zeros_like(acc_ref)
        dz = _dz_block(dzx_ref, dzg_ref, dzu_ref)
        xb = x_ref[...].astype(MXU)
        for j in range(N_DEV):
            acc_ref[j] += _mm_tn(xb, dz[:, j * W_BLK:(j + 1) * W_BLK])

        @pl.when(pl.program_id(0) == L // TM2 - 1)
        def _():
            dw_ref[...] = acc_ref[...].astype(WIRE)

    row = lambda w: pl.BlockSpec((TM2, w), lambda i: (i, 0))
    wspec = pl.BlockSpec((N_DEV, D_MODEL, W_BLK), lambda i: (0, 0, 0))
    return pl.pallas_call(
        body, name="inproj_bwd_dw", grid=(L // TM2,),
        in_specs=[row(D_MODEL), row(RG_W), row(D_MODEL), row(S5_W)] + extra_specs, out_specs=wspec,
        out_shape=_S((N_DEV, D_MODEL, W_BLK), WIRE), scratch_shapes=[pltpu.VMEM((N_DEV, D_MODEL, W_BLK), F32)],
        compiler_params=_params(1))(x, dzx, dzg, dzu, *extra)


def _inproj_bwd_dx(dt1, dzx, dzg, dzu, w_in, token=None):
    L = dt1.shape[0]
    extra, extra_specs = _after(token)

    def body(dt1_ref, dzx_ref, dzg_ref, dzu_ref, w_ref, *rest):
        rest[-1][...] = ALPHA * dt1_ref[...] + _mm_nt(_dz_block(dzx_ref, dzg_ref, dzu_ref), w_ref[...])

    tm = min(TM_MM, L)
    row = lambda w: pl.BlockSpec((tm, w), lambda i: (i, 0))
    return pl.pallas_call(
        body, name="inproj_bwd_dx", grid=(L // tm,),
        in_specs=[row(D_MODEL), row(RG_W), row(D_MODEL), row(S5_W), _full((D_MODEL, Z_W))] + extra_specs,
        out_specs=row(D_MODEL), out_shape=_S((L, D_MODEL)), compiler_params=_params(1))(dt1, dzx, dzg, dzu, w_in, *extra)


def _rg_specs(layer):
    tile = lambda rows: pl.BlockSpec((rows, LANE), lambda c: (0, c))
    ptile = lambda rows: pl.BlockSpec((None, rows, LANE), lambda c: (layer, 0, c))
    pbd = pl.BlockSpec((None, LANE, LANE), lambda c: (layer * N_RG_T + c, 0, 0))
    return tile, ptile, pbd, pl.BlockSpec((None, LANE, LANE), lambda c: (c, 0, 0))


def _rg_fwd(z, cw, cb, wa_bd, wx_bd, ba, bx, lam, layer):
    L = z.shape[0]

    def body(x_ref, cw_ref, cb_ref, wa_ref, wx_ref, ba_ref, bx_ref, lam_ref, hs_ref):
        w, b = cw_ref[...], cb_ref[...]
        wa, wx, ba_, bx_ = wa_ref[...].astype(MXU), wx_ref[...].astype(MXU), ba_ref[...], bx_ref[...]
        sp = _softplus(-lam_ref[...])

        def step(c, carry):
            r0 = pl.multiple_of(c * RC, RC)
            xe = jnp.concatenate([_halo(x_ref, c, r0), x_ref[pl.ds(r0, RC), :]], axis=0)
            t = _conv_taps(xe)
            h = t[0] * w[0:1] + t[1] * w[1:2] + t[2] * w[2:3] + t[3] * w[3:4] + b
            _, i, a, mult = _rg_gates(h, wa, wx, ba_, bx_, sp)
            hs, carry = _scan_real(a, mult * (i * h), carry)
            hs_ref[pl.ds(r0, RC), :] = hs
            return carry

        lax.fori_loop(0, L // RC, step, jnp.zeros((1, LANE), F32))

    tile, ptile, pbd, _ = _rg_specs(layer)
    return pl.pallas_call(
        body, name="rg_fwd", grid=(N_RG_T,),
        in_specs=[tile(L), ptile(4), ptile(1), pbd, pbd, ptile(1), ptile(1), ptile(1)],
        out_specs=tile(L), out_shape=_S((L, RG_W)), compiler_params=_params(1))(z, cw, cb, wa_bd, wx_bd, ba, bx, lam)


def _rg_bwd(dhs, z, hs, cw, cb, wa_bd, wx_bd, ba, bx, lam, layer):
    L = z.shape[0]

    def body(g_ref, x_ref, hs_ref, cw_ref, cb_ref, wa_ref, wx_ref, ba_ref, bx_ref, lam_ref,
             dx_ref, dcw_ref, dcb_ref, dwa_ref, dwx_ref, dba_ref, dbx_ref, dlam_ref):
        w, b = cw_ref[...], cb_ref[...]
        wa, wx, ba_, bx_ = wa_ref[...].astype(MXU), wx_ref[...].astype(MXU), ba_ref[...], bx_ref[...]
        lam = lam_ref[...]
        sp = _softplus(-lam)
        rows = lax.broadcasted_iota(jnp.int32, (RC, LANE), 0)
        for ref in (dcw_ref, dcb_ref, dwa_ref, dwx_ref, dba_ref, dbx_ref, dlam_ref):
            ref[...] = jnp.zeros_like(ref)
        nch = L // RC

        def step(k, carry):
            cin, nxt = carry
            c = nch - 1 - k
            r0 = pl.multiple_of(c * RC, RC)
            xe = jnp.concatenate([_halo(x_ref, c, r0), x_ref[pl.ds(r0, RC), :]], axis=0)
            t = _conv_taps(xe)
            h = t[0] * w[0:1] + t[1] * w[1:2] + t[2] * w[2:3] + t[3] * w[3:4] + b
            r, i, a, mult = _rg_gates(h, wa, wx, ba_, bx_, sp)
            hs_e = jnp.concatenate([_halo(hs_ref, c, r0), hs_ref[pl.ds(r0, RC), :]], axis=0)
            hs_prev = pltpu.roll(hs_e, 1, 0)[8:, :]
            g = g_ref[pl.ds(r0, RC), :]
            cc, cin_new = _scan_real(a, a * g, cin, reverse=True)
            dh = g + _up(cc, 1, rows, cin)
            ih = i * h
            dlog_a = dh * hs_prev * a - (dh * ih) * (a * a) / mult
            di = dh * mult * h
            dhin = dh * mult * i
            dr = dlog_a * ((-RG_C) * sp)
            dlam_ref[...] += _colsum(dlog_a * r)
            dra = dr * r * (1.0 - r)
            dia = di * i * (1.0 - i)
            dwa_ref[...] += _mm_tn(h, dra)
            dwx_ref[...] += _mm_tn(h, dia)
            dba_ref[...] += _colsum(dra)
            dbx_ref[...] += _colsum(dia)
            dhin = dhin + _mm_nt(dra, wa) + _mm_nt(dia, wx)
            de = jnp.concatenate([dhin, nxt], axis=0)
            n = RC + 8
            dx = (dhin * w[3:4] + pltpu.roll(de, n - 1, 0)[:RC, :] * w[2:3]
                  + pltpu.roll(de, n - 2, 0)[:RC, :] * w[1:2] + pltpu.roll(de, n - 3, 0)[:RC, :] * w[0:1])
            dx_ref[pl.ds(r0, RC), :] = dx
            for kk in range(4):
                dcw_ref[kk:kk + 1, :] += _colsum(dhin * t[kk])
            dcb_ref[...] += _colsum(dhin)
            return cin_new, dhin[0:8, :]

        lax.fori_loop(0, nch, step, (jnp.zeros((1, LANE), F32), jnp.zeros((8, LANE), F32)))
        dlam_ref[...] = dlam_ref[...] * (RG_C * _sigmoid(-lam))

    tile, ptile, pbd, bd = _rg_specs(layer)
    return pl.pallas_call(
        body, name="rg_bwd", grid=(N_RG_T,),
        in_specs=[tile(L), tile(L), tile(L), ptile(4), ptile(1), pbd, pbd, ptile(1), ptile(1), ptile(1)],
        out_specs=[tile(L), tile(4), tile(1), bd, bd, tile(1), tile(1), tile(1)],
        out_shape=[_S((L, RG_W)), _S((4, RG_W)), _S((1, RG_W)), _S((N_RG_T, LANE, LANE)), _S((N_RG_T, LANE, LANE)),
                   _S((1, RG_W)), _S((1, RG_W)), _S((1, RG_W))],
        compiler_params=_params(1))(dhs, z, hs, cw, cb, wa_bd, wx_bd, ba, bx, lam)


def _cmul(ar, ai, br, bi):
    return ar * br - ai * bi, ar * bi + ai * br


S5_TW = S5_N // N_S5_T


def _s5_specs(L):
    in_tile = pl.BlockSpec((L, LANE), lambda t: (0, t))
    st = pl.BlockSpec((L, S5_TW), lambda t: (0, t))
    bb = pl.BlockSpec((None, LANE, S5_TW), lambda t: (t, 0, 0))
    cc = pl.BlockSpec((None, S5_TW, LANE), lambda t: (t, 0, 0))
    lb = pl.BlockSpec((1, S5_TW), lambda t: (0, t))
    dv = pl.BlockSpec((1, LANE), lambda t: (0, t))
    return in_tile, st, bb, cc, lb, dv


def _layer_row_tile(layer):
    return pl.BlockSpec((None, 1, LANE), lambda t: (layer, 0, t))


def _s5_fwd(z, bb_re, bb_im, lb_re, lb_im, c_re, c_im, dvec, layer):
    L = z.shape[0]

    def body(u_ref, bbr_ref, bbi_ref, lr_ref, li_ref, cr_ref, ci_ref, d_ref, y_ref, sr_ref, si_ref):
        bbr, bbi = bbr_ref[...].astype(MXU), bbi_ref[...].astype(MXU)
        cr, ci = cr_ref[...].astype(MXU), ci_ref[...].astype(MXU)
        dv = d_ref[...]
        steps, e = _tile_powers(lr_ref[...], li_ref[...])

        def step(c, carry):
            r0 = pl.multiple_of(c * RC, RC)
            u = u_ref[pl.ds(r0, RC), :]
            ub = u.astype(MXU)
            sr = jnp.dot(ub, bbr, preferred_element_type=F32)
            si = jnp.dot(ub, bbi, preferred_element_type=F32)
            sr, si, carry = _scan_lti(sr, si, carry, steps, e)
            sr_ref[pl.ds(r0, RC), :] = sr
            si_ref[pl.ds(r0, RC), :] = si
            y_ref[pl.ds(r0, RC), :] = dv * u + (_mm(sr, cr) - _mm(si, ci))
            return carry

        zero = jnp.zeros((1, S5_TW), F32)
        lax.fori_loop(0, L // RC, step, (zero, zero))

    in_tile, st, bb, cc, lb, dv = _s5_specs(L)
    u_tile = pl.BlockSpec((L, LANE), lambda t: (0, C_S5U // LANE + t))
    return pl.pallas_call(
        body, name="s5_fwd", grid=(N_S5_T,),
        in_specs=[u_tile, bb, bb, lb, lb, cc, cc, _layer_row_tile(layer)],
        out_specs=[in_tile, st, st],
        out_shape=[_S((L, S5_W)), _S((L, S5_N)), _S((L, S5_N))],
        compiler_params=_params(1))(z, bb_re, bb_im, lb_re, lb_im, c_re, c_im, dvec)


def _s5_bwd(dy0, z, s_re, s_im, bb_re, bb_im, lb_re, lb_im, c_re, c_im, dvec, layer, token=None):
    L = z.shape[0]
    extra, extra_specs = _after(token)

    def body(dy_ref, u_ref, sr_ref, si_ref, bbr_ref, bbi_ref, lr_ref, li_ref, cr_ref, ci_ref, d_ref, *rest):
        du_ref, dbbr_ref, dbbi_ref, dlr_ref, dli_ref, dcr_ref, dci_ref, dd_ref = rest[len(extra):]
        bbr, bbi = bbr_ref[...].astype(MXU), bbi_ref[...].astype(MXU)
        cr, ci = cr_ref[...].astype(MXU), ci_ref[...].astype(MXU)
        lr, li = lr_ref[...], -li_ref[...]
        dv = d_ref[...]
        steps, e = _tile_powers(lr, li, reverse=True)
        for ref in (dbbr_ref, dbbi_ref, dlr_ref, dli_ref, dcr_ref, dci_ref, dd_ref):
            ref[...] = jnp.zeros_like(ref)
        nch = L // RC

        def step(k, carry):
            c = nch - 1 - k
            r0 = pl.multiple_of(c * RC, RC)
            dy = dy_ref[pl.ds(r0, RC), :]
            u = u_ref[pl.ds(r0, RC), :]
            dyb, ub = dy.astype(MXU), u.astype(MXU)
            sr, si = sr_ref[pl.ds(r0, RC), :], si_ref[pl.ds(r0, RC), :]
            dcr_ref[...] += _mm_tn(sr, dyb)
            dci_ref[...] -= _mm_tn(si, dyb)
            gr = _mm_nt(dyb, cr)
            gi = -_mm_nt(dyb, ci)
            gr, gi, carry = _scan_lti(gr, gi, carry, steps, e, reverse=True)
            pr_ = pltpu.roll(jnp.concatenate([_halo(sr_ref, c, r0), sr], axis=0), 1, 0)[8:, :]
            pi_ = pltpu.roll(jnp.concatenate([_halo(si_ref, c, r0), si], axis=0), 1, 0)[8:, :]
            dlr_ref[...] += _colsum(pr_ * gr + pi_ * gi)
            dli_ref[...] += _colsum(pr_ * gi - pi_ * gr)
            grb, gib = gr.astype(MXU), gi.astype(MXU)
            dbbr_ref[...] += _mm_tn(ub, grb)
            dbbi_ref[...] += _mm_tn(ub, gib)
            du_ref[pl.ds(r0, RC), :] = dv * dy + (_mm_nt(grb, bbr) + _mm_nt(gib, bbi))
            dd_ref[...] += _colsum(dy * u)
            return carry

        zero = jnp.zeros((1, S5_TW), F32)
        lax.fori_loop(0, nch, step, (zero, zero))

    in_tile, st, bb, cc, lb, dv = _s5_specs(L)
    u_tile = pl.BlockSpec((L, LANE), lambda t: (0, C_S5U // LANE + t))
    return pl.pallas_call(
        body, name="s5_bwd", grid=(N_S5_T,),
        in_specs=[in_tile, u_tile, st, st, bb, bb, lb, lb, cc, cc, _layer_row_tile(layer)] + extra_specs,
        out_specs=[in_tile, bb, bb, lb, lb, cc, cc, dv],
        out_shape=[_S((L, S5_W)), _S((N_S5_T, LANE, S5_TW)), _S((N_S5_T, LANE, S5_TW)), _S((1, S5_N)), _S((1, S5_N)),
                   _S((N_S5_T, S5_TW, LANE)), _S((N_S5_T, S5_TW, LANE)), _S((1, S5_W))],
        compiler_params=_params(1))(dy0, z, s_re, s_im, bb_re, bb_im, lb_re, lb_im, c_re, c_im, dvec, *extra)


def _disc(ar, ai, ls):
    dt = jnp.exp(ls)
    mag = jnp.exp(ar * dt)
    lr = mag * jnp.cos(ai * dt)
    li = mag * jnp.sin(ai * dt)
    den = ar * ar + ai * ai
    cr = ((lr - 1.0) * ar + li * ai) / den
    ci = (li * ar - (lr - 1.0) * ai) / den
    return lr, li, cr, ci


def _s5_disc_fwd(ar, ai, ls, token=None):
    extra, extra_specs = _after(token)

    def body(ar_ref, ai_ref, ls_ref, *rest):
        lr_ref, li_ref, cr_ref, ci_ref = rest[len(extra):]
        lr, li, cr, ci = _disc(ar_ref[...], ai_ref[...], ls_ref[...])
        lr_ref[...], li_ref[...], cr_ref[...], ci_ref[...] = lr, li, cr, ci

    sh = _S(ar.shape)
    vm = pl.BlockSpec(memory_space=pltpu.VMEM)
    return pl.pallas_call(body, name="s5_disc_fwd", in_specs=[vm, vm, vm] + extra_specs, out_shape=[sh, sh, sh, sh])(
        ar, ai, ls, *extra)


def _s5_disc_bwd(ar, ai, ls, dlr, dli, dcr, dci):
    def body(ar_ref, ai_ref, ls_ref, dlr_ref, dli_ref, dcr_ref, dci_ref, dar_ref, dai_ref, dls_ref):
        _, vjp = jax.vjp(_disc, ar_ref[...], ai_ref[...], jnp.broadcast_to(ls_ref[...], ar_ref.shape))
        dar, dai, dls = vjp((dlr_ref[...], dli_ref[...], dcr_ref[...], dci_ref[...]))
        dar_ref[...], dai_ref[...] = dar, dai
        dls_ref[...] = jnp.sum(dls, axis=1, keepdims=True)

    return pl.pallas_call(body, name="s5_disc_bwd", out_shape=[_S(ar.shape), _S(ar.shape), _S(ls.shape)])(
        ar, ai, ls, dlr, dli, dcr, dci)


def _s5_bscale_fwd(cr, ci, br, bi):
    def body(cr_ref, ci_ref, br_ref, bi_ref, or_ref, oi_ref):
        or_ref[...], oi_ref[...] = _cmul(cr_ref[...], ci_ref[...], br_ref[...], bi_ref[...])

    return pl.pallas_call(body, name="s5_bscale_fwd", out_shape=[_S(br.shape), _S(br.shape)])(cr, ci, br, bi)


def _s5_bscale_bwd(cr, ci, br, bi, gr, gi):
    def body(cr_ref, ci_ref, br_ref, bi_ref, gr_ref, gi_ref, dbr_ref, dbi_ref, dcr_ref, dci_ref):
        cr_, ci_, br_, bi_, gr_, gi_ = (r[...] for r in (cr_ref, ci_ref, br_ref, bi_ref, gr_ref, gi_ref))
        dbr_ref[...] = cr_ * gr_ + ci_ * gi_
        dbi_ref[...] = cr_ * gi_ - ci_ * gr_
        dcr_ref[...] = jnp.sum(gr_ * br_ + gi_ * bi_, axis=1, keepdims=True)
        dci_ref[...] = jnp.sum(gi_ * br_ - gr_ * bi_, axis=1, keepdims=True)

    return pl.pallas_call(body, name="s5_bscale_bwd",
                          out_shape=[_S(br.shape), _S(br.shape), _S(cr.shape), _S(cr.shape)])(cr, ci, br, bi, gr, gi)


def _row(w):
    return pl.BlockSpec((TM, w), lambda i: (i, 0))


def _full(shape):
    return pl.BlockSpec(tuple(shape), lambda i: (0,) * len(shape))


def _p_rows(layer):
    return pl.BlockSpec((None, None, TM, PLE_D), lambda i: (layer, 0, i, 0))


def _lrow(layer, width):
    return pl.BlockSpec((None, 1, width), lambda i: (layer, 0, 0))


def _post_fwd(x, hs, z, y0, p, w_glu, b_glu, w_out, g1, b1, ple_w, w_pg, b_pg, g2, b2, layer):
    L = x.shape[0]

    def body(x_ref, hs_ref, z_ref, y0_ref, p_ref, wg_ref, bg_ref, wo_ref, g1_ref, b1_ref, pw_ref, wpg_ref, bpg_ref,
             g2_ref, b2_ref, x2_ref, xh1_ref, xh2_ref, m_ref, q_ref, gt_ref, rstd1_ref, rstd2_ref):
        rg_gate = z_ref[:, C_RGG:C_RGG + RG_W]
        s5_gate = z_ref[:, C_S5G:C_S5G + S5_W]
        rg_y = hs_ref[...] * _silu_and_grad(rg_gate)[0]
        y1 = _gelu(y0_ref[...])
        gl = _sigmoid(_mm(y1, wg_ref[...]) + bg_ref[...])
        s5_y = (y1 * gl) * _silu_and_grad(s5_gate)[0]
        m_ref[:, :RG_W] = rg_y
        m_ref[:, RG_W:] = s5_y
        mix = _mm(m_ref[...], wo_ref[...])
        t1 = ALPHA * x_ref[...] + mix
        x1, xh1, rstd1 = _ln_fwd(t1, g1_ref[...], b1_ref[...])
        q = _mm(p_ref[...], pw_ref[...])
        gt = _sigmoid(_mm(x1, wpg_ref[...]) + bpg_ref[...])
        t2 = ALPHA * x1 + q * gt
        x2, xh2, rstd2 = _ln_fwd(t2, g2_ref[...], b2_ref[...])
        x2_ref[...], xh1_ref[...], xh2_ref[...], q_ref[...], gt_ref[...] = x2, xh1, xh2, q, gt
        rstd1_ref[...], rstd2_ref[...] = rstd1, rstd2

    vec = _lrow(layer, D_MODEL)
    return pl.pallas_call(
        body, name="post_fwd", grid=(L // TM,),
        in_specs=[_row(D_MODEL), _row(RG_W), _row(Z_W), _row(S5_W), _p_rows(layer), _full((S5_W, S5_W)), _lrow(layer, S5_W),
                  _full((D_MODEL, D_MODEL)), vec, vec, _full((PLE_D, D_MODEL)), _full((D_MODEL, D_MODEL)), vec, vec, vec],
        out_specs=[_row(D_MODEL)] * 6 + [_row(1)] * 2, out_shape=[_S((L, D_MODEL))] * 6 + [_S((L, 1))] * 2,
        compiler_params=_params(1))(x, hs, z, y0, p, w_glu, b_glu, w_out, g1, b1, ple_w, w_pg, b_pg, g2, b2)


def _post_bwd_a(dx2_or_target, is_top, xh2, xh1, rstd2, rstd1, q, gt, p, w_pg, g1, b1, g2, b2, layer, token=None):
    L = xh1.shape[0]
    extra, extra_specs = _after(token)

    def body(d_ref, xh2_ref, xh1_ref, rstd2_ref, rstd1_ref, q_ref, gt_ref, p_ref, wpg_ref, g1_ref, b1_ref, g2_ref,
             b2_ref, *rest):
        (dt1_ref, dpw_out, dwpg_out, dbpg_ref, dg1_ref, db1_ref, dg2_ref, db2_ref, loss_ref, dpw_ref,
         dwpg_ref) = rest[len(extra):]
        @pl.when(pl.program_id(0) == 0)
        def _():
            for ref in (dpw_ref, dwpg_ref, dbpg_ref, dg1_ref, db1_ref, dg2_ref, db2_ref, loss_ref):
                ref[...] = jnp.zeros_like(ref)

        g1, g2 = g1_ref[...], g2_ref[...]
        xh1, xh2, rstd1, rstd2 = xh1_ref[...], xh2_ref[...], rstd1_ref[...], rstd2_ref[...]
        x1 = xh1 * g1 + b1_ref[...]
        if is_top:
            err = (xh2 * g2 + b2_ref[...]) - d_ref[...]
            loss_ref[...] += _colsum(err * err)
            dx2 = err * (1.0 / D_MODEL)
        else:
            dx2 = d_ref[...]
        p = p_ref[...]
        q, gt = q_ref[...], gt_ref[...]
        dg2_ref[...] += _colsum(dx2 * xh2)
        db2_ref[...] += _colsum(dx2)
        dt2 = _ln_bwd(dx2, xh2, rstd2, g2)
        dq = dt2 * gt
        dgpre = (dt2 * q) * gt * (1.0 - gt)
        dpw_ref[...] += _mm_tn(p, dq)
        dwpg_ref[...] += _mm_tn(x1, dgpre)
        dbpg_ref[...] += _colsum(dgpre)
        dx1 = ALPHA * dt2 + _mm_nt(dgpre, wpg_ref[...])
        dg1_ref[...] += _colsum(dx1 * xh1)
        db1_ref[...] += _colsum(dx1)
        dt1_ref[...] = _ln_bwd(dx1, xh1, rstd1, g1)

        @pl.when(pl.program_id(0) == L // TM - 1)
        def _():
            dpw_out[...] = dpw_ref[...].astype(WIRE)
            dwpg_out[...] = dwpg_ref[...].astype(WIRE)

    vec, lvec = _full((1, D_MODEL)), _lrow(layer, D_MODEL)
    return pl.pallas_call(
        body, name="post_bwd_a_top" if is_top else "post_bwd_a", grid=(L // TM,),
        in_specs=[_row(D_MODEL), _row(D_MODEL), _row(D_MODEL), _row(1), _row(1), _row(D_MODEL), _row(D_MODEL), _p_rows(layer),
                  _full((D_MODEL, D_MODEL)), lvec, lvec, lvec, lvec] + extra_specs,
        out_specs=[_row(D_MODEL), _full((PLE_D, D_MODEL)), _full((D_MODEL, D_MODEL)), vec, vec, vec, vec, vec, vec],
        out_shape=[_S((L, D_MODEL)), _S((PLE_D, D_MODEL), WIRE), _S((D_MODEL, D_MODEL), WIRE)] + [_S((1, D_MODEL))] * 6,
        scratch_shapes=[pltpu.VMEM((PLE_D, D_MODEL), F32), pltpu.VMEM((D_MODEL, D_MODEL), F32)],
        compiler_params=_params(1))(dx2_or_target, xh2, xh1, rstd2, rstd1, q, gt, p, w_pg, g1, b1, g2, b2, *extra)


def _post_bwd_b(dt1, m, z, hs, y0, w_out, w_glu, b_glu, layer):
    L = dt1.shape[0]

    def body(dt1_ref, m_ref, z_ref, hs_ref, y0_ref, wo_ref, wg_ref, bg_ref,
             dhs_ref, dy0_ref, dzg_ref, dwo_out, dwg_out, dbg_ref, dwo_ref, dwg_ref):
        @pl.when(pl.program_id(0) == 0)
        def _():
            for ref in (dwo_ref, dwg_ref, dbg_ref):
                ref[...] = jnp.zeros_like(ref)

        dt1b = dt1_ref[...].astype(MXU)
        dm = _mm_nt(dt1b, wo_ref[...])
        dwo_ref[...] += _mm_tn(m_ref[...], dt1b)
        d_rgy, d_s5y = dm[:, :RG_W], dm[:, RG_W:]
        rg_gate = z_ref[:, C_RGG:C_RGG + RG_W]
        s5_gate = z_ref[:, C_S5G:C_S5G + S5_W]
        sl, dsl = _silu_and_grad(rg_gate)
        dhs_ref[...] = d_rgy * sl
        dzg_ref[:, :RG_W] = d_rgy * hs_ref[...] * dsl
        y0 = y0_ref[...]
        y1 = _gelu(y0)
        gl = _sigmoid(_mm(y1, wg_ref[...]) + bg_ref[...])
        sl, dsl = _silu_and_grad(s5_gate)
        dy2 = d_s5y * sl
        dzg_ref[:, RG_W:] = d_s5y * (y1 * gl) * dsl
        dglpre = (dy2 * y1) * gl * (1.0 - gl)
        dwg_ref[...] += _mm_tn(y1, dglpre)
        dbg_ref[...] += _colsum(dglpre)
        dy1 = dy2 * gl + _mm_nt(dglpre, wg_ref[...])
        dy0_ref[...] = dy1 * _gelu_grad(y0)

        @pl.when(pl.program_id(0) == L // TM - 1)
        def _():
            dwo_out[...] = dwo_ref[...].astype(WIRE)
            dwg_out[...] = dwg_ref[...].astype(WIRE)

    return pl.pallas_call(
        body, name="post_bwd_b", grid=(L // TM,),
        in_specs=[_row(D_MODEL), _row(D_MODEL), _row(Z_W), _row(RG_W), _row(S5_W), _full((D_MODEL, D_MODEL)),
                  _full((S5_W, S5_W)), _lrow(layer, S5_W)],
        out_specs=[_row(RG_W), _row(S5_W), _row(D_MODEL), _full((D_MODEL, D_MODEL)), _full((S5_W, S5_W)), _full((1, S5_W))],
        out_shape=[_S((L, RG_W)), _S((L, S5_W)), _S((L, D_MODEL)), _S((D_MODEL, D_MODEL), WIRE), _S((S5_W, S5_W), WIRE),
                   _S((1, S5_W))],
        scratch_shapes=[pltpu.VMEM((D_MODEL, D_MODEL), F32), pltpu.VMEM((S5_W, S5_W), F32)],
        compiler_params=_params(1))(dt1, m, z, hs, y0, w_out, w_glu, b_glu)


def _adamw(parts, w, m, v, token=None):
    nl = len(parts)
    extra, extra_specs = _after(token)
    n, R, C = parts[0].shape
    tr = R
    for cand in (512, 256, 128, 64, 32, 16, 8):
        if R % cand == 0 and n * cand * C * 4 <= 4 * 1024 * 1024:
            tr = cand
            break
    nblk = R // tr

    def body(*refs):
        p_refs = refs[:nl]
        w_ref, m_ref, v_ref = refs[nl:nl + 3]
        g_ref, d_ref, nm_ref, nv_ref = refs[nl + 3 + len(extra):]
        layer = pl.program_id(0)
        g = None
        for li, p_ref in enumerate(p_refs):
            s = p_ref[0].astype(F32)
            for k in range(1, n):
                s = s + p_ref[k].astype(F32)
            g = s if g is None else jnp.where(layer == li, s, g)
        nm = B1 * m_ref[...] + (1.0 - B1) * g
        nv = B2 * v_ref[...] + (1.0 - B2) * (g * g)
        d_ref[...] = (-LR) * ((nm / BC1) / (jnp.sqrt(nv / BC2) + EPS) + WD * w_ref[...])
        g_ref[...], nm_ref[...], nv_ref[...] = g, nm, nv

    def part_spec(li):
        return pl.BlockSpec((n, tr, C), lambda l, i: (0, jnp.where(l == li, i, jnp.where(l < li, 0, nblk - 1)), 0))

    blk = pl.BlockSpec((tr, C), lambda l, i: (l * nblk + i, 0))
    return pl.pallas_call(
        body, name="adamw", grid=(nl, nblk),
        in_specs=[part_spec(li) for li in range(nl)] + [blk, blk, blk] + extra_specs,
        out_specs=[blk] * 4, out_shape=[_S((nl * R, C))] * 4, compiler_params=_params(2))(*parts, w, m, v, *extra)


def _adamw_natural(names, g, w, m, v, name):
    n = len(names)

    def body(*refs):
        for j in range(n):
            g_ref, w_ref, m_ref, v_ref, d_ref, nm_ref, nv_ref = (refs[k * n + j] for k in range(7))
            gj = g_ref[...]
            nm = B1 * m_ref[...] + (1.0 - B1) * gj
            nv = B2 * v_ref[...] + (1.0 - B2) * (gj * gj)
            d_ref[...] = (-LR) * ((nm / BC1) / (jnp.sqrt(nv / BC2) + EPS) + WD * w_ref[...])
            nm_ref[...], nv_ref[...] = nm, nv

    ins = [t[k] for t in (g, w, m, v) for k in names]
    outs = pl.pallas_call(body, name=name, out_shape=[_S(w[k].shape) for _ in range(3) for k in names],
                          compiler_params=pltpu.CompilerParams(vmem_limit_bytes=VMEM_LIMIT))(*ins)
    return [{k: outs[t * n + j] for j, k in enumerate(names)} for t in range(3)]


def _me():
    return lax.axis_index("x"), lax.axis_index("y"), lax.axis_index("c")


def _lin(dev):
    return 4 * dev[0] + 2 * dev[1] + dev[2]


def _blk(ref, axis, size, idx):
    nd = len(ref.shape)
    start = idx * size
    if axis == nd - 1 and size % LANE == 0:
        start = pl.multiple_of(start, LANE)
    elif axis == nd - 2 and size % 16 == 0:
        start = pl.multiple_of(start, 16)
    ix = [slice(None)] * nd
    ix[axis] = pl.ds(start, size)
    return ref.at[tuple(ix)]


def _all_gather(shards, axes, name):
    n = len(shards)
    sizes = [s.shape[a] for s, a in zip(shards, axes)]
    out_shapes = [_S(s.shape[:a] + (N_DEV * s.shape[a],) + s.shape[a + 1:], s.dtype) for s, a in zip(shards, axes)]

    def body(*refs):
        ins, outs = refs[:n], refs[n:2 * n]
        send_sems, recv_sems, local_sems = refs[2 * n:]
        x, y, c = _me()
        me, sibling = (x, y, c), (x, y, 1 - c)
        chips = [(1 - x, y), (x, 1 - y), (1 - x, 1 - y)]

        def copy(a, k, block, to, from_input=False):
            dst = _blk(outs[a], axes[a], sizes[a], _lin(block))
            return pltpu.make_async_remote_copy(
                src_ref=ins[a] if from_input else dst, dst_ref=dst, send_sem=send_sems.at[a, k],
                recv_sem=recv_sems.at[a, k], device_id=to, device_id_type=MESH)

        mine = [pltpu.make_async_copy(ins[a], _blk(outs[a], axes[a], sizes[a], _lin(me)), local_sems.at[a]) for a in range(n)]
        for cp in mine:
            cp.start()
        first = []
        for a in range(n):
            first.append(copy(a, 0, me, sibling, True))
            first += [copy(a, 1 + j, me, (*chip, c), True) for j, chip in enumerate(chips)]
        for cp in first:
            cp.start()
        passed = []
        for j, chip in enumerate(chips):
            for a in range(n):
                copy(a, 1 + j, (*chip, c), me).wait_recv()
                cp = copy(a, 4 + j, (*chip, c), sibling)
                cp.start()
                passed.append(cp)
        for a in range(n):
            copy(a, 0, sibling, me).wait_recv()
            for j, chip in enumerate(chips):
                copy(a, 4 + j, (*chip, 1 - c), me).wait_recv()
        for cp in first + passed:
            cp.wait_send()
        for cp in mine:
            cp.wait()

    return pl.pallas_call(
        body, name=name, out_shape=out_shapes, in_specs=[ANY] * n, out_specs=[ANY] * n,
        scratch_shapes=[pltpu.SemaphoreType.DMA((n, 7)), pltpu.SemaphoreType.DMA((n, 7)), pltpu.SemaphoreType.DMA((n,))],
    )(*shards)


HBM_SPEC = pl.BlockSpec(memory_space=pltpu.HBM)
SEM_SPEC = pl.BlockSpec(memory_space=pltpu.SEMAPHORE)
EFFECT = pltpu.SideEffectType.DATAFLOW_SIDE_EFFECTING


def _peers(x, y, c):
    flip = lambda v, f: 1 - v if f else v
    return [(flip(x, k & 4), flip(y, k & 2), flip(c, k & 1)) for k in range(1, N_DEV)]


def _land_shape(mode, s, axis):
    if mode == "gather":
        return s.shape[:axis] + (N_DEV * s.shape[axis],) + s.shape[axis + 1:]
    return (N_DEV,) + s.shape[:axis] + (s.shape[axis] // N_DEV,) + s.shape[axis + 1:]


def _src_view(mode, ref, axis, peer):
    return ref if mode == "gather" else _blk(ref, axis, ref.shape[axis] // N_DEV, peer)


def _dst_view(mode, land, axis, sender):
    return _blk(land, axis, land.shape[axis] // N_DEV, sender) if mode == "gather" else land.at[sender]


def _seven_blocks(mode, land, axis):
    if mode == "gather":
        ix = [slice(None)] * len(land.shape)
        ix[axis] = pl.ds(0, (N_DEV - 1) * (land.shape[axis] // N_DEV))
        return land.at[tuple(ix)]
    return land.at[pl.ds(0, N_DEV - 1)]


def _place_own(mode, srcs, axes, name, after=None):
    n = len(srcs)
    extra, extra_specs = _after(after)

    def body(me_ref, *refs):
        for a in range(n):
            out = refs[n + len(extra) + a]
            out[...] = refs[a][...].reshape(out.shape)

    def at_me(shape, axis):
        return lambda i, me: tuple(me[0] if d == axis else 0 for d in range(len(shape)))

    in_specs, out_specs = [], []
    for s, axis in zip(srcs, axes):
        if mode == "gather":
            in_specs.append(pl.BlockSpec(s.shape, lambda i, me, nd=len(s.shape): (0,) * nd))
            out_specs.append(pl.BlockSpec(s.shape, at_me(s.shape, axis)))
        else:
            blk = s.shape[:axis] + (s.shape[axis] // N_DEV,) + s.shape[axis + 1:]
            in_specs.append(pl.BlockSpec(blk, at_me(blk, axis)))
            out_specs.append(pl.BlockSpec((1,) + blk, at_me((1,) + blk, 0)))
    me = _lin(_me()).astype(jnp.int32).reshape(1)
    return pl.pallas_call(
        body, name=name, out_shape=[_S(_land_shape(mode, s, a), s.dtype) for s, a in zip(srcs, axes)],
        grid_spec=pltpu.PrefetchScalarGridSpec(num_scalar_prefetch=1, grid=(1,), in_specs=in_specs + extra_specs,
                                               out_specs=out_specs),
        compiler_params=_params(1))(me, *srcs, *extra)


def _push_start(mode, srcs, lands, axes, name):
    n = len(srcs)

    def body(*refs):
        src_refs, land_refs = refs[:n], refs[n:2 * n]
        send_sems, recv_sems = refs[2 * n], refs[2 * n + 1]
        token = refs[-1]
        x, y, c = _me()
        me = _lin((x, y, c))
        for a in range(n):
            for peer in _peers(x, y, c):
                pltpu.make_async_remote_copy(
                    src_ref=_src_view(mode, src_refs[a], axes[a], _lin(peer)),
                    dst_ref=_dst_view(mode, land_refs[a], axes[a], me),
                    send_sem=send_sems.at[a], recv_sem=recv_sems.at[a], device_id=peer, device_id_type=MESH).start()
        token[...] = jnp.zeros_like(token)

    hbm = lambda s: pltpu.HBM(s.shape, s.dtype)
    outs = pl.pallas_call(
        body, name=name,
        out_shape=(pltpu.SemaphoreType.DMA((n,)), pltpu.SemaphoreType.DMA((n,)), *[hbm(s) for s in srcs], *[hbm(s) for s in lands],
                   _S((SUB, LANE))),
        in_specs=[HBM_SPEC] * (2 * n),
        out_specs=(SEM_SPEC, SEM_SPEC, *[HBM_SPEC] * (2 * n), pl.BlockSpec(memory_space=pltpu.VMEM)),
        input_output_aliases={i: 2 + i for i in range(2 * n)},
        compiler_params=pltpu.CompilerParams(has_side_effects=EFFECT),
    )(*[pltpu.with_memory_space_constraint(s, pltpu.HBM) for s in list(srcs) + list(lands)])
    return outs[0], outs[1], outs[2:2 + n], outs[2 + n:2 + 2 * n], outs[-1]


def _push_wait(mode, send_sems, recv_sems, srcs, lands, axes, after, name):
    n = len(srcs)
    after = list(after) if isinstance(after, (list, tuple)) else [after]

    def body(*refs):
        land_refs = refs[n:2 * n]
        send_sems, recv_sems = refs[2 * n], refs[2 * n + 1]
        x, y, c = _me()
        for a in range(n):
            seven = _seven_blocks(mode, land_refs[a], axes[a])
            cp = pltpu.make_async_remote_copy(src_ref=seven, dst_ref=seven, send_sem=send_sems.at[a], recv_sem=recv_sems.at[a],
                                              device_id=(x, y, 1 - c), device_id_type=MESH)
            cp.wait_send()
            cp.wait_recv()

    hbm = lambda s: pltpu.HBM(s.shape, s.dtype)
    outs = pl.pallas_call(
        body, name=name, out_shape=tuple(hbm(s) for s in list(srcs) + list(lands)),
        in_specs=[HBM_SPEC] * (2 * n) + [SEM_SPEC, SEM_SPEC] + [ANY] * len(after), out_specs=tuple([HBM_SPEC] * (2 * n)),
        input_output_aliases={i: i for i in range(2 * n)},
        compiler_params=pltpu.CompilerParams(has_side_effects=EFFECT),
    )(*srcs, *lands, send_sems, recv_sems, *after)
    return outs[n:]


def _sum_parts(parts):
    n, R, C = parts.shape

    def body(p_ref, o_ref):
        g = p_ref[0]
        for k in range(1, n):
            g = g + p_ref[k]
        o_ref[...] = g

    return pl.pallas_call(body, name="sum_parts", out_shape=_S((R, C)))(parts)


def _block_diag(w, nb):
    tn, r, c = w.shape
    w = w.reshape(tn // nb, nb, r, c)
    return jnp.einsum('tarc,ab->tarbc', w, jnp.eye(nb, dtype=w.dtype)).reshape(tn // nb, nb * r, nb * c)


def _block_diag_extract(w, nb):
    t, R, C = w.shape
    w = w.reshape(t, nb, R // nb, nb, C // nb)
    return jnp.einsum('tarbc,ab->tarc', w, jnp.eye(nb, dtype=w.dtype)).reshape(t * nb, R // nb, C // nb)


SMALL = ['conv_b', 'rg_wa', 'rg_ba', 'rg_wx', 'rg_bx', 'rg_lambda', 's5_a_re', 's5_a_im', 's5_b_re', 's5_b_im',
         's5_c_re', 's5_c_im', 's5_d', 's5_log_step', 's5_b_glu', 'ln1_g', 'ln1_b', 'ple_gate_b', 'ln2_g', 'ln2_b']
WEIGHTS = ['w_in', 'conv_w', 'conv_b', 'rg_wa', 'rg_ba', 'rg_wx', 'rg_bx', 'rg_lambda', 's5_a_re', 's5_a_im', 's5_b_re',
           's5_b_im', 's5_c_re', 's5_c_im', 's5_d', 's5_log_step', 's5_w_glu', 's5_b_glu', 'w_out', 'ln1_g', 'ln1_b',
           'ple_w', 'ple_gate_w', 'ple_gate_b', 'ln2_g', 'ln2_b']
PACK_ROWS_MULT = 64


def _pack(tree, scalar):
    flat = jnp.concatenate([tree[k].reshape(-1) for k in SMALL] + [scalar.reshape(1)])
    rows = -(-flat.shape[0] // (LANE * PACK_ROWS_MULT)) * PACK_ROWS_MULT
    return jnp.pad(flat, (0, rows * LANE - flat.shape[0])).reshape(rows, LANE)


def _unpack(packed, like):
    flat, out, o = packed.reshape(-1), {}, 0
    for k in SMALL:
        n = math.prod(like[k].shape)
        out[k] = flat[o:o + n].reshape(like[k].shape)
        o += n
    return out, flat[o]


class _NoHooks:
    token = None
    first_token = None

    def first_weights(self, full, after):
        return full

    def layer_start(self, i, W, after):
        return W

    def late_weights(self, i, W, after):
        return W

    def post_done(self, i, g):
        return None

    def smalls_done(self, grads, loss):
        self.small = _small_grads(grads, self.res)
        return None

    def w_in_done(self, i, g):
        return None

    def layer_done(self, i, g, dx):
        return None


def _local_grads(x, p, target, W, disc, hooks):
    depth = 2
    saved = []
    for i in range(depth):
        if i > 0:
            W = hooks.layer_start(i, W, x)
        w = W[i]
        z = _inproj_fwd(x, w['w_in'], hooks.token if i == 0 else None)
        hs = _rg_fwd(z, w['conv_w'], w['conv_b'], w['wa_bd'], w['wx_bd'], w['rg_ba'], w['rg_bx'], w['rg_lambda'], i)
        d = disc[i]
        y0, s_re, s_im = _s5_fwd(z, d['bb_re'], d['bb_im'], d['lb_re'], d['lb_im'], d['c_re'], d['c_im'], w['s5_d'], i)
        W = hooks.late_weights(i, W, y0)
        w = W[i]
        x2, *norms = _post_fwd(x, hs, z, y0, p, w['s5_w_glu'], w['s5_b_glu'], w['w_out'], w['ln1_g'], w['ln1_b'],
                               w['ple_w'], w['ple_gate_w'], w['ple_gate_b'], w['ln2_g'], w['ln2_b'], i)
        saved.append((x, z, hs, y0, s_re, s_im, norms))
        x = x2

    grads = [None] * depth
    dx = target
    loss = None
    token = None
    for i in reversed(range(depth)):
        w, d = W[i], disc[i]
        xin, z, hs, y0, s_re, s_im, (xh1, xh2, m, q, gt, rstd1, rstd2) = saved[i]
        g = {}
        (dt1, g['ple_w'], g['ple_gate_w'], g['ple_gate_b'], g['ln1_g'], g['ln1_b'], g['ln2_g'], g['ln2_b'], lrow) = _post_bwd_a(
            dx, i == depth - 1, xh2, xh1, rstd2, rstd1, q, gt, p, w['ple_gate_w'], w['ln1_g'], w['ln1_b'],
            w['ln2_g'], w['ln2_b'], i, token)
        if i == depth - 1:
            loss = 0.5 / D_MODEL * jnp.sum(lrow)
        dhs, dy0, dzg, g['w_out'], g['s5_w_glu'], g['s5_b_glu'] = _post_bwd_b(dt1, m, z, hs, y0, w['w_out'], w['s5_w_glu'],
                                                                           w['s5_b_glu'], i)
        (dzu, g['bb_re'], g['bb_im'], g['lb_re'], g['lb_im'], g['c_re'], g['c_im'], g['s5_d']) = _s5_bwd(
            dy0, z, s_re, s_im, d['bb_re'], d['bb_im'], d['lb_re'], d['lb_im'], d['c_re'], d['c_im'], w['s5_d'], i,
            hooks.post_done(i, g))
        (dzx, g['conv_w'], g['conv_b'], g['wa_bd'], g['wx_bd'], g['rg_ba'], g['rg_bx'], g['rg_lambda']) = _rg_bwd(
            dhs, z, hs, w['conv_w'], w['conv_b'], w['wa_bd'], w['wx_bd'], w['rg_ba'], w['rg_bx'], w['rg_lambda'], i)
        if i == 0:
            g['w_in'] = _inproj_bwd_dw(xin, dzx, dzg, dzu, hooks.smalls_done([g, grads[1]], loss))
            dx = _inproj_bwd_dx(dt1, dzx, dzg, dzu, w['w_in'], hooks.w_in_done(i, g))
        else:
            dx, g['w_in'] = _inproj_bwd(dt1, xin, dzx, dzg, dzu, w['w_in'])
        grads[i] = g
        token = hooks.layer_done(i, g, dx)
    return loss, dx, grads


def _s5_layouts_fwd(s5_a_re, s5_a_im, s5_log_step, s5_b_re, s5_b_im, s5_c_re, s5_c_im, token=None):
    depth = s5_a_re.shape[0]
    ar, ai = s5_a_re.reshape(depth * 24, S5_P), s5_a_im.reshape(depth * 24, S5_P)
    ls = s5_log_step.reshape(depth * 24, 1)
    lr, li, cr, ci = _s5_disc_fwd(ar, ai, ls, token)
    col = lambda a: a.reshape(depth * S5_N, 1)
    br, bi = s5_b_re.reshape(depth * S5_N, 16), s5_b_im.reshape(depth * S5_N, 16)
    bbr, bbi = _s5_bscale_fwd(col(cr), col(ci), br, bi)
    disc = []
    for i in range(depth):
        gph = lambda a: a.reshape(depth, 24, S5_P, 16)[i]
        disc.append(dict(
            bb_re=_block_diag(jnp.swapaxes(gph(bbr), 1, 2), 8), bb_im=_block_diag(jnp.swapaxes(gph(bbi), 1, 2), 8),
            lb_re=lr.reshape(depth, 1, S5_N)[i], lb_im=li.reshape(depth, 1, S5_N)[i],
            c_re=_block_diag(jnp.swapaxes(s5_c_re[i], 1, 2), 8), c_im=_block_diag(jnp.swapaxes(s5_c_im[i], 1, 2), 8)))
    return disc, (ar, ai, ls, col(cr), col(ci), br, bi)


def _s5_layouts_bwd(grads, res):
    ar, ai, ls, cr, ci, br, bi = res
    depth = len(grads)
    stack = lambda f: jnp.stack([f(g) for g in grads])
    dbbr = stack(lambda g: jnp.swapaxes(_block_diag_extract(g['bb_re'], 8), 1, 2)).reshape(depth * S5_N, 16)
    dbbi = stack(lambda g: jnp.swapaxes(_block_diag_extract(g['bb_im'], 8), 1, 2)).reshape(depth * S5_N, 16)
    dbr, dbi, dcr, dci = _s5_bscale_bwd(cr, ci, br, bi, dbbr, dbbi)
    gp = lambda a: a.reshape(depth * 24, S5_P)
    dar, dai, dls = _s5_disc_bwd(ar, ai, ls, gp(stack(lambda g: g['lb_re'])), gp(stack(lambda g: g['lb_im'])), gp(dcr), gp(dci))
    return dict(
        s5_a_re=dar.reshape(depth, 24, S5_P), s5_a_im=dai.reshape(depth, 24, S5_P), s5_log_step=dls.reshape(depth, 24),
        s5_b_re=dbr.reshape(depth, 24, S5_P, 16), s5_b_im=dbi.reshape(depth, 24, S5_P, 16),
        s5_c_re=stack(lambda g: jnp.swapaxes(_block_diag_extract(g['c_re'], 8), 1, 2)),
        s5_c_im=stack(lambda g: jnp.swapaxes(_block_diag_extract(g['c_im'], 8), 1, 2)))


LATE = ('w_out', 'ple_w', 'ple_gate_w', 's5_w_glu')


ROWS = ('conv_b', 'rg_ba', 'rg_bx', 'rg_lambda', 's5_d', 's5_b_glu', 'ln1_g', 'ln1_b', 'ple_gate_b', 'ln2_g', 'ln2_b')


def _shared_weights(full):
    depth = full['conv_b'].shape[0]
    shared = {k: full[k].reshape(depth, 1, -1) for k in ROWS}
    shared['conv_w'] = full['conv_w']
    shared['wa_bd'] = _block_diag(full['rg_wa'].reshape(depth * 10, 64, 64), 2)
    shared['wx_bd'] = _block_diag(full['rg_wx'].reshape(depth * 10, 64, 64), 2)
    return shared


def _layer_weights(full, shared, i):
    return dict(shared, w_in=full['w_in'][i])


class _AllLocal(_NoHooks):
    def __init__(self, full):
        self.full = full

    def late_weights(self, i, W, after):
        W[i].update({k: self.full[k][i] for k in LATE})
        return W


def _full_grads(full, x, p, target, hooks=None):
    hooks = hooks or _AllLocal(full)
    disc, res = _s5_layouts_fwd(full['s5_a_re'], full['s5_a_im'], full['s5_log_step'], full['s5_b_re'], full['s5_b_im'],
                                full['s5_c_re'], full['s5_c_im'], hooks.first_token)
    full = hooks.first_weights(full, disc[-1]['bb_im'])
    shared = _shared_weights(full)
    W = [_layer_weights(full, shared, i) for i in range(2)]
    hooks.res = res
    loss, gx, grads = _local_grads(x, p, target, W, disc, hooks)
    out = dict(hooks.small)
    for k in SHARD_AXIS:
        out[k] = [g[k] for g in grads]
    return loss, gx, out


def _small_grads(grads, res):
    stack = lambda f: jnp.stack([f(g) for g in grads])
    out = _s5_layouts_bwd(grads, res)
    out['conv_w'] = stack(lambda g: g['conv_w'])
    for k in ('conv_b', 'rg_ba', 'rg_bx', 'rg_lambda', 's5_b_glu', 'ln1_g', 'ln1_b', 'ple_gate_b', 'ln2_g', 'ln2_b'):
        out[k] = stack(lambda g: g[k][0])
    out['s5_d'] = stack(lambda g: g['s5_d'][0]).reshape(2, 24, 16)
    out['rg_wa'] = stack(lambda g: _block_diag_extract(g['wa_bd'], 2))
    out['rg_wx'] = stack(lambda g: _block_diag_extract(g['wx_bd'], 2))
    return out


SHARD_AXIS = {'w_in': 2, 'w_out': 1, 'ple_w': 2, 'ple_gate_w': 1, 's5_w_glu': 1}


def kernel(x, p, w_in, conv_w, conv_b, rg_wa, rg_ba, rg_wx, rg_bx, rg_lambda, s5_a_re, s5_a_im, s5_b_re, s5_b_im, s5_c_re, s5_c_im, s5_d, s5_log_step, s5_w_glu, s5_b_glu, w_out, ln1_g, ln1_b, ple_w, ple_gate_w, ple_gate_b, ln2_g, ln2_b, loss_target, m_w_in, m_conv_w, m_conv_b, m_rg_wa, m_rg_ba, m_rg_wx, m_rg_bx, m_rg_lambda, m_s5_a_re, m_s5_a_im, m_s5_b_re, m_s5_b_im, m_s5_c_re, m_s5_c_im, m_s5_d, m_s5_log_step, m_s5_w_glu, m_s5_b_glu, m_w_out, m_ln1_g, m_ln1_b, m_ple_w, m_ple_gate_w, m_ple_gate_b, m_ln2_g, m_ln2_b, v_w_in, v_conv_w, v_conv_b, v_rg_wa, v_rg_ba, v_rg_wx, v_rg_bx, v_rg_lambda, v_s5_a_re, v_s5_a_im, v_s5_b_re, v_s5_b_im, v_s5_c_re, v_s5_c_im, v_s5_d, v_s5_log_step, v_s5_w_glu, v_s5_b_glu, v_w_out, v_ln1_g, v_ln1_b, v_ple_w, v_ple_gate_w, v_ple_gate_b, v_ln2_g, v_ln2_b):
    local = dict(locals())
    w = {k: local[k] for k in WEIGHTS}
    mom = {k: local['m_' + k] for k in WEIGHTS}
    var = {k: local['v_' + k] for k in WEIGHTS}

    big = list(SHARD_AXIS)
    wire = {k: w[k].astype(WIRE) for k in big}
    late_axes = [SHARD_AXIS[k] - 1 for k in LATE]
    pushed = {}

    def push_weights(key, srcs, axes, after):
        pushed[key] = _push_start("gather", srcs, _place_own("gather", srcs, axes, "place_weights_" + key, after=after), axes,
                                  "push_weights_" + key)
        return pushed[key][4]

    def await_weights(key, axes, after):
        s = pushed[key]
        return _push_wait("gather", s[0], s[1], s[2], s[3], axes, after, "await_weights_" + key)

    token_first = push_weights("first", [wire['w_in'][0], conv_w[None]], [1, 0], None)
    token0 = push_weights("l0", [wire[k][0] for k in LATE], late_axes, token_first)
    push_weights("l1", [wire['w_in'][1]] + [wire[k][1] for k in LATE], [1] + late_axes, token0)

    def push_grads(key, g, names, axes):
        srcs = [g[k] for k in names]
        pushed[key] = _push_start("scatter", srcs, _place_own("scatter", srcs, axes, "place_grads_" + key), axes,
                                  "push_grads_" + key)
        return pushed[key][4]

    def await_grads(key, axes, after):
        s = pushed[key]
        return _push_wait("scatter", s[0], s[1], s[2], s[3], axes, after, "await_grads_" + key)

    class Overlap(_NoHooks):
        token = pushed["l1"][4]
        first_token = token

        def first_weights(self, full, after):
            w_in0, conv = await_weights("first", [1, 0], after)
            return dict(full, w_in=[w_in0, None], conv_w=jnp.moveaxis(conv, 0, 2).reshape(2, 4, RG_W))

        def late_weights(self, i, W, after):
            if i == 0:
                W[0].update(zip(LATE, await_weights("l0", late_axes, after)))
            return W

        def layer_start(self, i, W, after):
            lands = await_weights("l1", [1] + late_axes, after)
            W[1].update(zip(LATE, lands[1:]), w_in=lands[0])
            return W

        def post_done(self, i, g):
            return push_grads("late0", g, LATE, late_axes) if i == 0 else None

        def smalls_done(self, grads, loss):
            super().smalls_done(grads, loss)
            conv = jnp.moveaxis(self.small['conv_w'].reshape(2, 4, N_DEV, RG_W // N_DEV), 2, 0)
            self.packed = _pack(self.small, loss)
            return push_grads("small", dict(conv_w=conv.reshape(N_DEV, 8, RG_W // N_DEV), small=self.packed),
                              ['conv_w', 'small'], [0, 0])

        def w_in_done(self, i, g):
            return push_grads("w_in0", g, ['w_in'], [0])

        def layer_done(self, i, g, dx):
            return push_grads("all1", g, ['w_in'] + list(LATE), [0] + late_axes) if i == 1 else None

    hooks = Overlap()
    _, grad_x, g = _full_grads(dict(w), x[0], p, loss_target[0], hooks)

    recv1 = dict(zip(['w_in'] + list(LATE), await_grads("all1", [0] + late_axes, grad_x)))
    recv0 = dict(zip(LATE, await_grads("late0", late_axes, grad_x)))
    outs = {}

    def update(k, parts):
        shard = w[k].shape
        c = shard[-1]
        two = lambda a: a.reshape(-1, c)
        res = _adamw([r.reshape(N_DEV, -1, c) for r in parts], two(w[k]), two(mom[k]), two(var[k]))
        outs[k] = [o.reshape(shard) for o in res]

    for k in LATE:
        update(k, [recv0[k], recv1[k]])
    done = [outs[k][1] for k in LATE]
    conv_parts, small_parts = await_grads("small", [0, 0], done)

    rows = hooks.packed.shape[0] // N_DEV
    mine = _sum_parts(small_parts.reshape(N_DEV, rows, LANE))
    gathered = _all_gather([mine], [0], "gather_small_grads")[0]
    w_in0, = await_grads("w_in0", [0], gathered)
    update('w_in', [w_in0, recv1['w_in']])
    update('conv_w', [conv_parts])
    summed, loss = _unpack(gathered, w)
    narrow = ['s5_b_re', 's5_b_im']
    for names, name in ((narrow, "adamw_s5_b"), ([k for k in SMALL if k not in narrow], "adamw_small")):
        delta, new_m, new_v = _adamw_natural(names, summed, w, mom, var, name)
        for k in names:
            outs[k] = [summed[k], delta[k], new_m[k], new_v[k]]

    res = [loss, grad_x[None]]
    for j in range(4):
        res += [outs[k][j] for k in WEIGHTS]
    return tuple(res)
```

```python
import math

import jax
import jax.numpy as jnp
from jax import lax
from jax.experimental import pallas as pl
from jax.experimental.pallas import tpu as pltpu

F32 = jnp.float32
MXU = jnp.bfloat16
WIRE = jnp.bfloat16

N_DEV = 8
D_MODEL = 1024
PLE_D = 256
RG_W = 640
S5_W = 384
S5_P = 64
S5_N = 24 * S5_P
Z_W = 2 * RG_W + 2 * S5_W
C_RGG = RG_W
C_S5U = 2 * RG_W
C_S5G = 2 * RG_W + S5_W
LANE = 128
N_RG_T = RG_W // LANE
N_S5_T = S5_W // LANE
W_BLK = Z_W // N_DEV
ALPHA = (2.0 * 2) ** 0.25
LN_EPS = 1e-5
RG_C = 8.0
LR, B1, B2, EPS, WD, STEP = 0.001, 0.9, 0.999, 1e-08, 0.01, 10
BC1 = 1.0 - B1 ** STEP
BC2 = 1.0 - B2 ** STEP
RC = 256
TM = 256
TM_MM = 1024
VMEM_LIMIT = 56 * 1024 * 1024

MESH = pl.DeviceIdType.MESH
ANY = pl.BlockSpec(memory_space=pl.ANY)


def _params(n_grid_axes, vmem=VMEM_LIMIT):
    return pltpu.CompilerParams(dimension_semantics=("arbitrary",) * n_grid_axes, vmem_limit_bytes=vmem)


def _S(shape, dtype=F32):
    return jax.ShapeDtypeStruct(tuple(shape), dtype)


def _sigmoid(x):
    return 0.5 * jnp.tanh(0.5 * x) + 0.5


def _silu_and_grad(x):
    s = _sigmoid(x)
    return x * s, s * (1.0 + x * (1.0 - s))


_GELU_C = math.sqrt(2.0 / math.pi)


def _gelu(x):
    return 0.5 * x * (1.0 + jnp.tanh(_GELU_C * (x + 0.044715 * (x * x * x))))


def _gelu_grad(x):
    th = jnp.tanh(_GELU_C * (x + 0.044715 * (x * x * x)))
    return 0.5 * (1.0 + th) + 0.5 * x * (1.0 - th * th) * (_GELU_C * (1.0 + 3.0 * 0.044715 * (x * x)))


def _mm(a, b):
    return jnp.dot(a.astype(MXU), b.astype(MXU), preferred_element_type=F32)


def _mm_nt(a, b):
    return lax.dot_general(a.astype(MXU), b.astype(MXU), (((1,), (1,)), ((), ())), preferred_element_type=F32)


def _mm_tn(a, b):
    return lax.dot_general(a.astype(MXU), b.astype(MXU), (((0,), (0,)), ((), ())), preferred_element_type=F32)


def _ln_fwd(t, g, b):
    mu = jnp.mean(t, axis=-1, keepdims=True)
    tc = t - mu
    var = jnp.mean(tc * tc, axis=-1, keepdims=True)
    rstd = lax.rsqrt(var + LN_EPS)
    xhat = tc * rstd
    return xhat * g + b, xhat, rstd


def _ln_bwd(dy, xhat, rstd, g):
    dxh = dy * g
    m1 = jnp.mean(dxh, axis=-1, keepdims=True)
    m2 = jnp.mean(dxh * xhat, axis=-1, keepdims=True)
    return rstd * (dxh - m1 - xhat * m2)


def _colsum(a):
    return jnp.sum(a, axis=0, keepdims=True)


def _up(x, d, rows, fill):
    n = x.shape[0]
    return jnp.where(rows < n - d, pltpu.roll(x, n - d, 0), fill)


SUB = 8
TILE_STEPS = (1, 2, 4)


def _r8(width):
    return lax.broadcasted_iota(jnp.int32, (SUB, width), 0)


def _scan_real(a, u, carry, reverse=False):
    r8 = _r8(a.shape[1])
    n = a.shape[0] // SUB
    outs = [None] * n
    for k in (reversed(range(n)) if reverse else range(n)):
        A, U = a[SUB * k:SUB * k + SUB], u[SUB * k:SUB * k + SUB]
        for d in TILE_STEPS:
            m = (r8 < SUB - d) if reverse else (r8 >= d)
            sh = SUB - d if reverse else d
            U = A * jnp.where(m, pltpu.roll(U, sh, 0), 0.0) + U
            A = A * jnp.where(m, pltpu.roll(A, sh, 0), 1.0)
        h = A * carry + U
        outs[k] = h
        carry = h[0:1] if reverse else h[SUB - 1:SUB]
    return jnp.concatenate(outs, axis=0), carry


def _tile_powers(lr, li, reverse=False):
    width = lr.shape[1]
    r8 = _r8(width)
    steps = []
    pr, pi = lr, li
    er, ei = jnp.broadcast_to(lr, (SUB, width)), jnp.broadcast_to(li, (SUB, width))
    for d in TILE_STEPS:
        m = (r8 < SUB - d) if reverse else (r8 >= d)
        sh = SUB - d if reverse else d
        steps.append((sh, jnp.where(m, pr, 0.0), jnp.where(m, pi, 0.0)))
        er, ei = _cmul(er, ei, jnp.where(m, pltpu.roll(er, sh, 0), 1.0), jnp.where(m, pltpu.roll(ei, sh, 0), 0.0))
        pr, pi = _cmul(pr, pi, pr, pi)
    return steps, (er, ei)


def _scan_lti(xr, xi, carry, steps, e, reverse=False):
    er, ei = e
    kr, ki = carry
    n = xr.shape[0] // SUB
    outr, outi = [None] * n, [None] * n
    for k in (reversed(range(n)) if reverse else range(n)):
        sr, si = xr[SUB * k:SUB * k + SUB], xi[SUB * k:SUB * k + SUB]
        for sh, pr, pi in steps:
            shr, shi = pltpu.roll(sr, sh, 0), pltpu.roll(si, sh, 0)
            sr, si = sr + (pr * shr - pi * shi), si + (pr * shi + pi * shr)
        sr = sr + (er * kr - ei * ki)
        si = si + (er * ki + ei * kr)
        outr[k], outi[k] = sr, si
        kr, ki = (sr[0:1], si[0:1]) if reverse else (sr[SUB - 1:SUB], si[SUB - 1:SUB])
    return jnp.concatenate(outr, axis=0), jnp.concatenate(outi, axis=0), (kr, ki)


def _halo(ref, c, r0):
    rp = pl.multiple_of(jnp.maximum(r0 - 8, 0), 8)
    return jnp.where(c > 0, ref[pl.ds(rp, 8), :], 0.0)


def _conv_taps(xe):
    return [pltpu.roll(xe, 3, 0)[8:, :], pltpu.roll(xe, 2, 0)[8:, :], pltpu.roll(xe, 1, 0)[8:, :], xe[8:, :]]


def _rg_gates(h, wa, wx, ba, bx, sp):
    r = _sigmoid(_mm(h, wa) + ba)
    i = _sigmoid(_mm(h, wx) + bx)
    log_a = (-RG_C) * r * sp
    a = jnp.exp(log_a)
    mult = jnp.sqrt(-jnp.tanh(log_a) * (a * a + 1.0))
    return r, i, a, mult


def _softplus(y):
    return jnp.maximum(y, 0.0) + jnp.log1p(jnp.exp(-jnp.abs(y)))


def _after(token):
    return ([], []) if token is None else ([token], [ANY])


def _inproj_fwd(x, w_in, token=None):
    L = x.shape[0]

    def body(x_ref, w_ref, *rest):
        rest[-1][...] = _mm(x_ref[...], w_ref[...])

    extra, extra_specs = _after(token)
    tm = min(TM_MM, L)
    return pl.pallas_call(
        body, name="inproj_fwd", grid=(L // tm,),
        in_specs=[pl.BlockSpec((tm, D_MODEL), lambda i: (i, 0)), pl.BlockSpec((D_MODEL, Z_W), lambda i: (0, 0))] + extra_specs,
        out_specs=pl.BlockSpec((tm, Z_W), lambda i: (i, 0)),
        out_shape=_S((L, Z_W)), compiler_params=_params(1))(x, w_in, *extra)


def _inproj_bwd(dt1, x, dzx, dzg, dzu, w_in):
    L = x.shape[0]

    def body(dt1_ref, x_ref, dzx_ref, dzg_ref, dzu_ref, w_ref, dx_ref, dw_ref, acc_ref):
        @pl.when(pl.program_id(0) == 0)
        def _():
            acc_ref[...] = jnp.zeros_like(acc_ref)
        dzg = dzg_ref[...]
        dz = jnp.concatenate([dzx_ref[...], dzg[:, :RG_W], dzu_ref[...], dzg[:, RG_W:]], axis=1).astype(MXU)
        xb = x_ref[...].astype(MXU)
        dx_ref[...] = ALPHA * dt1_ref[...] + _mm_nt(dz, w_ref[...])
        for j in range(N_DEV):
            acc_ref[j] += _mm_tn(xb, dz[:, j * W_BLK:(j + 1) * W_BLK])

        @pl.when(pl.program_id(0) == L // TM - 1)
        def _():
            dw_ref[...] = acc_ref[...].astype(WIRE)

    row = lambda w: pl.BlockSpec((TM, w), lambda i: (i, 0))
    wspec = pl.BlockSpec((N_DEV, D_MODEL, W_BLK), lambda i: (0, 0, 0))
    return pl.pallas_call(
        body, name="inproj_bwd", grid=(L // TM,),
        in_specs=[row(D_MODEL), row(D_MODEL), row(RG_W), row(D_MODEL), row(S5_W),
                  pl.BlockSpec((D_MODEL, Z_W), lambda i: (0, 0))],
        out_specs=[row(D_MODEL), wspec],
        out_shape=[_S((L, D_MODEL)), _S((N_DEV, D_MODEL, W_BLK), WIRE)],
        scratch_shapes=[pltpu.VMEM((N_DEV, D_MODEL, W_BLK), F32)],
        compiler_params=_params(1))(dt1, x, dzx, dzg, dzu, w_in)


TM2 = 512


def _dz_block(dzx_ref, dzg_ref, dzu_ref):
    dzg = dzg_ref[...]
    return jnp.concatenate([dzx_ref[...], dzg[:, :RG_W], dzu_ref[...], dzg[:, RG_W:]], axis=1).astype(MXU)


def _inproj_bwd_dw(x, dzx, dzg, dzu, token=None):
    L = x.shape[0]
    extra, extra_specs = _after(token)

    def body(x_ref, dzx_ref, dzg_ref, dzu_ref, *rest):
        dw_ref, acc_ref = rest[len(extra):]
        @pl.when(pl.program_id(0) == 0)
        def _():
            acc_ref[...] = jnp.zeros_like(acc_ref)
        dz = _dz_block(dzx_ref, dzg_ref, dzu_ref)
        xb = x_ref[...].astype(MXU)
        for j in range(N_DEV):
            acc_ref[j] += _mm_tn(xb, dz[:, j * W_BLK:(j + 1) * W_BLK])

        @pl.when(pl.program_id(0) == L // TM2 - 1)
        def _():
            dw_ref[...] = acc_ref[...].astype(WIRE)

    row = lambda w: pl.BlockSpec((TM2, w), lambda i: (i, 0))
    wspec = pl.BlockSpec((N_DEV, D_MODEL, W_BLK), lambda i: (0, 0, 0))
    return pl.pallas_call(
        body, name="inproj_bwd_dw", grid=(L // TM2,),
        in_specs=[row(D_MODEL), row(RG_W), row(D_MODEL), row(S5_W)] + extra_specs, out_specs=wspec,
        out_shape=_S((N_DEV, D_MODEL, W_BLK), WIRE), scratch_shapes=[pltpu.VMEM((N_DEV, D_MODEL, W_BLK), F32)],
        compiler_params=_params(1))(x, dzx, dzg, dzu, *extra)


def _inproj_bwd_dx(dt1, dzx, dzg, dzu, w_in, token=None):
    L = dt1.shape[0]
    extra, extra_specs = _after(token)

    def body(dt1_ref, dzx_ref, dzg_ref, dzu_ref, w_ref, *rest):
        rest[-1][...] = ALPHA * dt1_ref[...] + _mm_nt(_dz_block(dzx_ref, dzg_ref, dzu_ref), w_ref[...])

    tm = min(TM_MM, L)
    row = lambda w: pl.BlockSpec((tm, w), lambda i: (i, 0))
    return pl.pallas_call(
        body, name="inproj_bwd_dx", grid=(L // tm,),
        in_specs=[row(D_MODEL), row(RG_W), row(D_MODEL), row(S5_W), _full((D_MODEL, Z_W))] + extra_specs,
        out_specs=row(D_MODEL), out_shape=_S((L, D_MODEL)), compiler_params=_params(1))(dt1, dzx, dzg, dzu, w_in, *extra)


def _rg_specs(layer):
    tile = lambda rows: pl.BlockSpec((rows, LANE), lambda c: (0, c))
    ptile = lambda rows: pl.BlockSpec((None, rows, LANE), lambda c: (layer, 0, c))
    pheads = pl.BlockSpec((None, 2, RG_HD, RG_HD), lambda c: (layer, c, 0, 0))
    return tile, ptile, pheads, pl.BlockSpec((2, RG_HD, RG_HD), lambda c: (c, 0, 0))


RG_HD = 64


def _bd2(w):
    z = jnp.zeros((RG_HD, RG_HD), w.dtype)
    return jnp.concatenate([jnp.concatenate([w[0], z], axis=1), jnp.concatenate([z, w[1]], axis=1)], axis=0)


def _bd2_diag(m):
    return jnp.stack([m[:RG_HD, :RG_HD], m[RG_HD:, RG_HD:]])


def _rg_fwd(z, cw, cb, wa_bd, wx_bd, ba, bx, lam, layer):
    L = z.shape[0]

    def body(x_ref, cw_ref, cb_ref, wa_ref, wx_ref, ba_ref, bx_ref, lam_ref, hs_ref):
        w, b = cw_ref[...], cb_ref[...]
        wa, wx, ba_, bx_ = _bd2(wa_ref[...]).astype(MXU), _bd2(wx_ref[...]).astype(MXU), ba_ref[...], bx_ref[...]
        sp = _softplus(-lam_ref[...])

        def step(c, carry):
            r0 = pl.multiple_of(c * RC, RC)
            xe = jnp.concatenate([_halo(x_ref, c, r0), x_ref[pl.ds(r0, RC), :]], axis=0)
            t = _conv_taps(xe)
            h = t[0] * w[0:1] + t[1] * w[1:2] + t[2] * w[2:3] + t[3] * w[3:4] + b
            _, i, a, mult = _rg_gates(h, wa, wx, ba_, bx_, sp)
            hs, carry = _scan_real(a, mult * (i * h), carry)
            hs_ref[pl.ds(r0, RC), :] = hs
            return carry

        lax.fori_loop(0, L // RC, step, jnp.zeros((1, LANE), F32))

    tile, ptile, pheads, _ = _rg_specs(layer)
    return pl.pallas_call(
        body, name="rg_fwd", grid=(N_RG_T,),
        in_specs=[tile(L), ptile(4), ptile(1), pheads, pheads, ptile(1), ptile(1), ptile(1)],
        out_specs=tile(L), out_shape=_S((L, RG_W)), compiler_params=_params(1))(z, cw, cb, wa_bd, wx_bd, ba, bx, lam)


def _rg_bwd(dhs, z, hs, cw, cb, wa_bd, wx_bd, ba, bx, lam, layer):
    L = z.shape[0]

    def body(g_ref, x_ref, hs_ref, cw_ref, cb_ref, wa_ref, wx_ref, ba_ref, bx_ref, lam_ref,
             dx_ref, dcw_ref, dcb_ref, dwa_out, dwx_out, dba_ref, dbx_ref, dlam_ref, dwa_ref, dwx_ref):
        w, b = cw_ref[...], cb_ref[...]
        wa, wx, ba_, bx_ = _bd2(wa_ref[...]).astype(MXU), _bd2(wx_ref[...]).astype(MXU), ba_ref[...], bx_ref[...]
        lam = lam_ref[...]
        sp = _softplus(-lam)
        rows = lax.broadcasted_iota(jnp.int32, (RC, LANE), 0)
        for ref in (dcw_ref, dcb_ref, dwa_ref, dwx_ref, dba_ref, dbx_ref, dlam_ref):
            ref[...] = jnp.zeros_like(ref)
        nch = L // RC

        def step(k, carry):
            cin, nxt = carry
            c = nch - 1 - k
            r0 = pl.multiple_of(c * RC, RC)
            xe = jnp.concatenate([_halo(x_ref, c, r0), x_ref[pl.ds(r0, RC), :]], axis=0)
            t = _conv_taps(xe)
            h = t[0] * w[0:1] + t[1] * w[1:2] + t[2] * w[2:3] + t[3] * w[3:4] + b
            r, i, a, mult = _rg_gates(h, wa, wx, ba_, bx_, sp)
            hs_e = jnp.concatenate([_halo(hs_ref, c, r0), hs_ref[pl.ds(r0, RC), :]], axis=0)
            hs_prev = pltpu.roll(hs_e, 1, 0)[8:, :]
            g = g_ref[pl.ds(r0, RC), :]
            cc, cin_new = _scan_real(a, a * g, cin, reverse=True)
            dh = g + _up(cc, 1, rows, cin)
            ih = i * h
            dlog_a = dh * hs_prev * a - (dh * ih) * (a * a) / mult
            di = dh * mult * h
            dhin = dh * mult * i
            dr = dlog_a * ((-RG_C) * sp)
            dlam_ref[...] += _colsum(dlog_a * r)
            dra = dr * r * (1.0 - r)
            dia = di * i * (1.0 - i)
            dwa_ref[...] += _mm_tn(h, dra)
            dwx_ref[...] += _mm_tn(h, dia)
            dba_ref[...] += _colsum(dra)
            dbx_ref[...] += _colsum(dia)
            dhin = dhin + _mm_nt(dra, wa) + _mm_nt(dia, wx)
            de = jnp.concatenate([dhin, nxt], axis=0)
            n = RC + 8
            dx = (dhin * w[3:4] + pltpu.roll(de, n - 1, 0)[:RC, :] * w[2:3]
                  + pltpu.roll(de, n - 2, 0)[:RC, :] * w[1:2] + pltpu.roll(de, n - 3, 0)[:RC, :] * w[0:1])
            dx_ref[pl.ds(r0, RC), :] = dx
            for kk in range(4):
                dcw_ref[kk:kk + 1, :] += _colsum(dhin * t[kk])
            dcb_ref[...] += _colsum(dhin)
            return cin_new, dhin[0:8, :]

        lax.fori_loop(0, nch, step, (jnp.zeros((1, LANE), F32), jnp.zeros((8, LANE), F32)))
        dlam_ref[...] = dlam_ref[...] * (RG_C * _sigmoid(-lam))
        dwa_out[...], dwx_out[...] = _bd2_diag(dwa_ref[...]), _bd2_diag(dwx_ref[...])

    tile, ptile, pheads, gheads = _rg_specs(layer)
    heads = _S((2 * N_RG_T, RG_HD, RG_HD))
    return pl.pallas_call(
        body, name="rg_bwd", grid=(N_RG_T,),
        in_specs=[tile(L), tile(L), tile(L), ptile(4), ptile(1), pheads, pheads, ptile(1), ptile(1), ptile(1)],
        out_specs=[tile(L), tile(4), tile(1), gheads, gheads, tile(1), tile(1), tile(1)],
        out_shape=[_S((L, RG_W)), _S((4, RG_W)), _S((1, RG_W)), heads, heads, _S((1, RG_W)), _S((1, RG_W)), _S((1, RG_W))],
        scratch_shapes=[pltpu.VMEM((LANE, LANE), F32), pltpu.VMEM((LANE, LANE), F32)],
        compiler_params=_params(1))(dhs, z, hs, cw, cb, wa_bd, wx_bd, ba, bx, lam)


def _cmul(ar, ai, br, bi):
    return ar * br - ai * bi, ar * bi + ai * br


S5_TW = S5_N // N_S5_T


S5_H = 16
S5_GT = LANE // S5_H


def _s5_specs(L, layer):
    in_tile = pl.BlockSpec((L, LANE), lambda t: (0, t))
    st = pl.BlockSpec((L, S5_TW), lambda t: (0, t))
    pg = pl.BlockSpec((None, S5_GT, S5_H, S5_P), lambda t: (layer * N_S5_T + t, 0, 0, 0))
    plb = pl.BlockSpec((None, S5_GT, S5_P), lambda t: (layer * N_S5_T + t, 0, 0))
    gg = pl.BlockSpec((None, S5_GT, S5_H, S5_P), lambda t: (t, 0, 0, 0))
    glb = pl.BlockSpec((None, S5_GT, S5_P), lambda t: (t, 0, 0))
    dv = pl.BlockSpec((1, LANE), lambda t: (0, t))
    return in_tile, st, pg, plb, gg, glb, dv


def _bd8(blocks):
    rows = []
    for g in range(S5_GT):
        pieces = [blocks[g]]
        if g:
            pieces.insert(0, jnp.zeros((S5_H, S5_P * g), blocks.dtype))
        if g < S5_GT - 1:
            pieces.append(jnp.zeros((S5_H, S5_P * (S5_GT - 1 - g)), blocks.dtype))
        rows.append(jnp.concatenate(pieces, axis=1))
    return jnp.concatenate(rows, axis=0)


def _bd8_diag(m):
    return jnp.stack([m[S5_H * g:S5_H * (g + 1), S5_P * g:S5_P * (g + 1)] for g in range(S5_GT)])


def _row8(v):
    return jnp.concatenate([v[g:g + 1] for g in range(S5_GT)], axis=1)


def _row8_split(r):
    return jnp.concatenate([r[:, S5_P * g:S5_P * (g + 1)] for g in range(S5_GT)], axis=0)


def _layer_row_tile(layer):
    return pl.BlockSpec((None, 1, LANE), lambda t: (layer, 0, t))


def _s5_fwd(z, bb_re, bb_im, lb_re, lb_im, c_re, c_im, dvec, layer):
    L = z.shape[0]

    def body(u_ref, bbr_ref, bbi_ref, lr_ref, li_ref, cr_ref, ci_ref, d_ref, y_ref, sr_ref, si_ref):
        bbr, bbi = _bd8(bbr_ref[...]).astype(MXU), _bd8(bbi_ref[...]).astype(MXU)
        cr, ci = _bd8(cr_ref[...]).astype(MXU), _bd8(ci_ref[...]).astype(MXU)
        dv = d_ref[...]
        steps, e = _tile_powers(_row8(lr_ref[...]), _row8(li_ref[...]))

        def step(c, carry):
            r0 = pl.multiple_of(c * RC, RC)
            u = u_ref[pl.ds(r0, RC), :]
            ub = u.astype(MXU)
            sr = jnp.dot(ub, bbr, preferred_element_type=F32)
            si = jnp.dot(ub, bbi, preferred_element_type=F32)
            sr, si, carry = _scan_lti(sr, si, carry, steps, e)
            sr_ref[pl.ds(r0, RC), :] = sr
            si_ref[pl.ds(r0, RC), :] = si
            y_ref[pl.ds(r0, RC), :] = dv * u + (_mm_nt(sr, cr) - _mm_nt(si, ci))
            return carry

        zero = jnp.zeros((1, S5_TW), F32)
        lax.fori_loop(0, L // RC, step, (zero, zero))

    in_tile, st, pg, plb, _, _, _ = _s5_specs(L, layer)
    u_tile = pl.BlockSpec((L, LANE), lambda t: (0, C_S5U // LANE + t))
    return pl.pallas_call(
        body, name="s5_fwd", grid=(N_S5_T,),
        in_specs=[u_tile, pg, pg, plb, plb, pg, pg, _layer_row_tile(layer)],
        out_specs=[in_tile, st, st],
        out_shape=[_S((L, S5_W)), _S((L, S5_N)), _S((L, S5_N))],
        compiler_params=_params(1))(z, bb_re, bb_im, lb_re, lb_im, c_re, c_im, dvec)


def _s5_bwd(dy0, z, s_re, s_im, bb_re, bb_im, lb_re, lb_im, c_re, c_im, dvec, layer, token=None):
    L = z.shape[0]
    extra, extra_specs = _after(token)

    def body(dy_ref, u_ref, sr_ref, si_ref, bbr_ref, bbi_ref, lr_ref, li_ref, cr_ref, ci_ref, d_ref, *rest):
        (du_ref, dbbr_out, dbbi_out, dlr_out, dli_out, dcr_out, dci_out, dd_ref,
         dbbr_ref, dbbi_ref, dcr_ref, dci_ref, dlr_ref, dli_ref) = rest[len(extra):]
        bbr, bbi = _bd8(bbr_ref[...]).astype(MXU), _bd8(bbi_ref[...]).astype(MXU)
        cr, ci = _bd8(cr_ref[...]).astype(MXU), _bd8(ci_ref[...]).astype(MXU)
        lr, li = _row8(lr_ref[...]), -_row8(li_ref[...])
        dv = d_ref[...]
        steps, e = _tile_powers(lr, li, reverse=True)
        for ref in (dbbr_ref, dbbi_ref, dlr_ref, dli_ref, dcr_ref, dci_ref, dd_ref):
            ref[...] = jnp.zeros_like(ref)
        nch = L // RC

        def step(k, carry):
            c = nch - 1 - k
            r0 = pl.multiple_of(c * RC, RC)
            dy = dy_ref[pl.ds(r0, RC), :]
            u = u_ref[pl.ds(r0, RC), :]
            dyb, ub = dy.astype(MXU), u.astype(MXU)
            sr, si = sr_ref[pl.ds(r0, RC), :], si_ref[pl.ds(r0, RC), :]
            dcr_ref[...] += _mm_tn(dyb, sr)
            dci_ref[...] -= _mm_tn(dyb, si)
            gr = jnp.dot(dyb, cr, preferred_element_type=F32)
            gi = -jnp.dot(dyb, ci, preferred_element_type=F32)
            gr, gi, carry = _scan_lti(gr, gi, carry, steps, e, reverse=True)
            pr_ = pltpu.roll(jnp.concatenate([_halo(sr_ref, c, r0), sr], axis=0), 1, 0)[8:, :]
            pi_ = pltpu.roll(jnp.concatenate([_halo(si_ref, c, r0), si], axis=0), 1, 0)[8:, :]
            dlr_ref[...] += _colsum(pr_ * gr + pi_ * gi)
            dli_ref[...] += _colsum(pr_ * gi - pi_ * gr)
            grb, gib = gr.astype(MXU), gi.astype(MXU)
            dbbr_ref[...] += _mm_tn(ub, grb)
            dbbi_ref[...] += _mm_tn(ub, gib)
            du_ref[pl.ds(r0, RC), :] = dv * dy + (_mm_nt(grb, bbr) + _mm_nt(gib, bbi))
            dd_ref[...] += _colsum(dy * u)
            return carry

        zero = jnp.zeros((1, S5_TW), F32)
        lax.fori_loop(0, nch, step, (zero, zero))
        dbbr_out[...], dbbi_out[...] = _bd8_diag(dbbr_ref[...]), _bd8_diag(dbbi_ref[...])
        dcr_out[...], dci_out[...] = _bd8_diag(dcr_ref[...]), _bd8_diag(dci_ref[...])
        dlr_out[...], dli_out[...] = _row8_split(dlr_ref[...]), _row8_split(dli_ref[...])

    in_tile, st, pg, plb, gg, glb, dv = _s5_specs(L, layer)
    u_tile = pl.BlockSpec((L, LANE), lambda t: (0, C_S5U // LANE + t))
    groups, rows = _S((N_S5_T, S5_GT, S5_H, S5_P)), _S((N_S5_T, S5_GT, S5_P))
    wide = pltpu.VMEM((LANE, S5_TW), F32)
    return pl.pallas_call(
        body, name="s5_bwd", grid=(N_S5_T,),
        in_specs=[in_tile, u_tile, st, st, pg, pg, plb, plb, pg, pg, _layer_row_tile(layer)] + extra_specs,
        out_specs=[in_tile, gg, gg, glb, glb, gg, gg, dv],
        out_shape=[_S((L, S5_W)), groups, groups, rows, rows, groups, groups, _S((1, S5_W))],
        scratch_shapes=[wide, wide, wide, wide, pltpu.VMEM((1, S5_TW), F32), pltpu.VMEM((1, S5_TW), F32)],
        compiler_params=_params(1))(dy0, z, s_re, s_im, bb_re, bb_im, lb_re, lb_im, c_re, c_im, dvec, *extra)


def _disc(ar, ai, ls):
    dt = jnp.exp(ls)
    mag = jnp.exp(ar * dt)
    lr = mag * jnp.cos(ai * dt)
    li = mag * jnp.sin(ai * dt)
    den = ar * ar + ai * ai
    cr = ((lr - 1.0) * ar + li * ai) / den
    ci = (li * ar - (lr - 1.0) * ai) / den
    return lr, li, cr, ci


def _s5_disc_fwd(ar, ai, ls, token=None):
    extra, extra_specs = _after(token)

    def body(ar_ref, ai_ref, ls_ref, *rest):
        lr_ref, li_ref, cr_ref, ci_ref = rest[len(extra):]
        lr, li, cr, ci = _disc(ar_ref[...], ai_ref[...], ls_ref[...])
        lr_ref[...], li_ref[...], cr_ref[...], ci_ref[...] = lr, li, cr, ci

    sh = _S(ar.shape)
    vm = pl.BlockSpec(memory_space=pltpu.VMEM)
    return pl.pallas_call(body, name="s5_disc_fwd", in_specs=[vm, vm, vm] + extra_specs, out_shape=[sh, sh, sh, sh])(
        ar, ai, ls, *extra)


def _s5_disc_bwd(ar, ai, ls, dlr, dli, dcr, dci):
    def body(ar_ref, ai_ref, ls_ref, dlr_ref, dli_ref, dcr_ref, dci_ref, dar_ref, dai_ref, dls_ref):
        _, vjp = jax.vjp(_disc, ar_ref[...], ai_ref[...], jnp.broadcast_to(ls_ref[...], ar_ref.shape))
        dar, dai, dls = vjp((dlr_ref[...], dli_ref[...], dcr_ref[...], dci_ref[...]))
        dar_ref[...], dai_ref[...] = dar, dai
        dls_ref[...] = jnp.sum(dls, axis=1, keepdims=True)

    return pl.pallas_call(body, name="s5_disc_bwd", out_shape=[_S(ar.shape), _S(ar.shape), _S(ls.shape)])(
        ar, ai, ls, dlr, dli, dcr, dci)


def _s5_bscale_fwd(cr, ci, br, bi):
    def body(cr_ref, ci_ref, br_ref, bi_ref, or_ref, oi_ref):
        or_ref[...], oi_ref[...] = _cmul(cr_ref[...], ci_ref[...], br_ref[...], bi_ref[...])

    return pl.pallas_call(body, name="s5_bscale_fwd", out_shape=[_S(br.shape), _S(br.shape)])(cr, ci, br, bi)


def _s5_bscale_bwd(cr, ci, br, bi, gr, gi):
    def body(cr_ref, ci_ref, br_ref, bi_ref, gr_ref, gi_ref, dbr_ref, dbi_ref, dcr_ref, dci_ref):
        cr_, ci_, br_, bi_, gr_, gi_ = (r[...] for r in (cr_ref, ci_ref, br_ref, bi_ref, gr_ref, gi_ref))
        dbr_ref[...] = cr_ * gr_ + ci_ * gi_
        dbi_ref[...] = cr_ * gi_ - ci_ * gr_
        dcr_ref[...] = jnp.sum(gr_ * br_ + gi_ * bi_, axis=1, keepdims=True)
        dci_ref[...] = jnp.sum(gi_ * br_ - gr_ * bi_, axis=1, keepdims=True)

    return pl.pallas_call(body, name="s5_bscale_bwd",
                          out_shape=[_S(br.shape), _S(br.shape), _S(cr.shape), _S(cr.shape)])(cr, ci, br, bi, gr, gi)


def _row(w):
    return pl.BlockSpec((TM, w), lambda i: (i, 0))


def _full(shape):
    return pl.BlockSpec(tuple(shape), lambda i: (0,) * len(shape))


def _p_rows(layer):
    return pl.BlockSpec((None, None, TM, PLE_D), lambda i: (layer, 0, i, 0))


def _lrow(layer, width):
    return pl.BlockSpec((None, 1, width), lambda i: (layer, 0, 0))


def _post_fwd(x, hs, z, y0, p, w_glu, b_glu, w_out, g1, b1, ple_w, w_pg, b_pg, g2, b2, layer):
    L = x.shape[0]

    def body(x_ref, hs_ref, z_ref, y0_ref, p_ref, wg_ref, bg_ref, wo_ref, g1_ref, b1_ref, pw_ref, wpg_ref, bpg_ref,
             g2_ref, b2_ref, x2_ref, xh1_ref, xh2_ref, m_ref, q_ref, gt_ref, rstd1_ref, rstd2_ref):
        rg_gate = z_ref[:, C_RGG:C_RGG + RG_W]
        s5_gate = z_ref[:, C_S5G:C_S5G + S5_W]
        rg_y = hs_ref[...] * _silu_and_grad(rg_gate)[0]
        y1 = _gelu(y0_ref[...])
        gl = _sigmoid(_mm(y1, wg_ref[...]) + bg_ref[...])
        s5_y = (y1 * gl) * _silu_and_grad(s5_gate)[0]
        m_ref[:, :RG_W] = rg_y
        m_ref[:, RG_W:] = s5_y
        mix = _mm(m_ref[...], wo_ref[...])
        t1 = ALPHA * x_ref[...] + mix
        x1, xh1, rstd1 = _ln_fwd(t1, g1_ref[...], b1_ref[...])
        q = _mm(p_ref[...], pw_ref[...])
        gt = _sigmoid(_mm(x1, wpg_ref[...]) + bpg_ref[...])
        t2 = ALPHA * x1 + q * gt
        x2, xh2, rstd2 = _ln_fwd(t2, g2_ref[...], b2_ref[...])
        x2_ref[...], xh1_ref[...], xh2_ref[...], q_ref[...], gt_ref[...] = x2, xh1, xh2, q, gt
        rstd1_ref[...], rstd2_ref[...] = rstd1, rstd2

    vec = _lrow(layer, D_MODEL)
    return pl.pallas_call(
        body, name="post_fwd", grid=(L // TM,),
        in_specs=[_row(D_MODEL), _row(RG_W), _row(Z_W), _row(S5_W), _p_rows(layer), _full((S5_W, S5_W)), _lrow(layer, S5_W),
                  _full((D_MODEL, D_MODEL)), vec, vec, _full((PLE_D, D_MODEL)), _full((D_MODEL, D_MODEL)), vec, vec, vec],
        out_specs=[_row(D_MODEL)] * 6 + [_row(1)] * 2, out_shape=[_S((L, D_MODEL))] * 6 + [_S((L, 1))] * 2,
        compiler_params=_params(1))(x, hs, z, y0, p, w_glu, b_glu, w_out, g1, b1, ple_w, w_pg, b_pg, g2, b2)


def _post_bwd_a(dx2_or_target, is_top, xh2, xh1, rstd2, rstd1, q, gt, p, w_pg, g1, b1, g2, b2, layer, token=None):
    L = xh1.shape[0]
    extra, extra_specs = _after(token)

    def body(d_ref, xh2_ref, xh1_ref, rstd2_ref, rstd1_ref, q_ref, gt_ref, p_ref, wpg_ref, g1_ref, b1_ref, g2_ref,
             b2_ref, *rest):
        (dt1_ref, dpw_out, dwpg_out, dbpg_ref, dg1_ref, db1_ref, dg2_ref, db2_ref, loss_ref, dpw_ref,
         dwpg_ref) = rest[len(extra):]
        @pl.when(pl.program_id(0) == 0)
        def _():
            for ref in (dpw_ref, dwpg_ref, dbpg_ref, dg1_ref, db1_ref, dg2_ref, db2_ref, loss_ref):
                ref[...] = jnp.zeros_like(ref)

        g1, g2 = g1_ref[...], g2_ref[...]
        xh1, xh2, rstd1, rstd2 = xh1_ref[...], xh2_ref[...], rstd1_ref[...], rstd2_ref[...]
        x1 = xh1 * g1 + b1_ref[...]
        if is_top:
            err = (xh2 * g2 + b2_ref[...]) - d_ref[...]
            loss_ref[...] += _colsum(err * err)
            dx2 = err * (1.0 / D_MODEL)
        else:
            dx2 = d_ref[...]
        p = p_ref[...]
        q, gt = q_ref[...], gt_ref[...]
        dg2_ref[...] += _colsum(dx2 * xh2)
        db2_ref[...] += _colsum(dx2)
        dt2 = _ln_bwd(dx2, xh2, rstd2, g2)
        dq = dt2 * gt
        dgpre = (dt2 * q) * gt * (1.0 - gt)
        dpw_ref[...] += _mm_tn(p, dq)
        dwpg_ref[...] += _mm_tn(x1, dgpre)
        dbpg_ref[...] += _colsum(dgpre)
        dx1 = ALPHA * dt2 + _mm_nt(dgpre, wpg_ref[...])
        dg1_ref[...] += _colsum(dx1 * xh1)
        db1_ref[...] += _colsum(dx1)
        dt1_ref[...] = _ln_bwd(dx1, xh1, rstd1, g1)

        @pl.when(pl.program_id(0) == L // TM - 1)
        def _():
            dpw_out[...] = dpw_ref[...].astype(WIRE)
            dwpg_out[...] = dwpg_ref[...].astype(WIRE)

    vec, lvec = _full((1, D_MODEL)), _lrow(layer, D_MODEL)
    return pl.pallas_call(
        body, name="post_bwd_a_top" if is_top else "post_bwd_a", grid=(L // TM,),
        in_specs=[_row(D_MODEL), _row(D_MODEL), _row(D_MODEL), _row(1), _row(1), _row(D_MODEL), _row(D_MODEL), _p_rows(layer),
                  _full((D_MODEL, D_MODEL)), lvec, lvec, lvec, lvec] + extra_specs,
        out_specs=[_row(D_MODEL), _full((PLE_D, D_MODEL)), _full((D_MODEL, D_MODEL)), vec, vec, vec, vec, vec, vec],
        out_shape=[_S((L, D_MODEL)), _S((PLE_D, D_MODEL), WIRE), _S((D_MODEL, D_MODEL), WIRE)] + [_S((1, D_MODEL))] * 6,
        scratch_shapes=[pltpu.VMEM((PLE_D, D_MODEL), F32), pltpu.VMEM((D_MODEL, D_MODEL), F32)],
        compiler_params=_params(1))(dx2_or_target, xh2, xh1, rstd2, rstd1, q, gt, p, w_pg, g1, b1, g2, b2, *extra)


def _post_bwd_b(dt1, m, z, hs, y0, w_out, w_glu, b_glu, layer):
    L = dt1.shape[0]

    def body(dt1_ref, m_ref, z_ref, hs_ref, y0_ref, wo_ref, wg_ref, bg_ref,
             dhs_ref, dy0_ref, dzg_ref, dwo_out, dwg_out, dbg_ref, dwo_ref, dwg_ref):
        @pl.when(pl.program_id(0) == 0)
        def _():
            for ref in (dwo_ref, dwg_ref, dbg_ref):
                ref[...] = jnp.zeros_like(ref)

        dt1b = dt1_ref[...].astype(MXU)
        dm = _mm_nt(dt1b, wo_ref[...])
        dwo_ref[...] += _mm_tn(m_ref[...], dt1b)
        d_rgy, d_s5y = dm[:, :RG_W], dm[:, RG_W:]
        rg_gate = z_ref[:, C_RGG:C_RGG + RG_W]
        s5_gate = z_ref[:, C_S5G:C_S5G + S5_W]
        sl, dsl = _silu_and_grad(rg_gate)
        dhs_ref[...] = d_rgy * sl
        dzg_ref[:, :RG_W] = d_rgy * hs_ref[...] * dsl
        y0 = y0_ref[...]
        y1 = _gelu(y0)
        gl = _sigmoid(_mm(y1, wg_ref[...]) + bg_ref[...])
        sl, dsl = _silu_and_grad(s5_gate)
        dy2 = d_s5y * sl
        dzg_ref[:, RG_W:] = d_s5y * (y1 * gl) * dsl
        dglpre = (dy2 * y1) * gl * (1.0 - gl)
        dwg_ref[...] += _mm_tn(y1, dglpre)
        dbg_ref[...] += _colsum(dglpre)
        dy1 = dy2 * gl + _mm_nt(dglpre, wg_ref[...])
        dy0_ref[...] = dy1 * _gelu_grad(y0)

        @pl.when(pl.program_id(0) == L // TM - 1)
        def _():
            dwo_out[...] = dwo_ref[...].astype(WIRE)
            dwg_out[...] = dwg_ref[...].astype(WIRE)

    return pl.pallas_call(
        body, name="post_bwd_b", grid=(L // TM,),
        in_specs=[_row(D_MODEL), _row(D_MODEL), _row(Z_W), _row(RG_W), _row(S5_W), _full((D_MODEL, D_MODEL)),
                  _full((S5_W, S5_W)), _lrow(layer, S5_W)],
        out_specs=[_row(RG_W), _row(S5_W), _row(D_MODEL), _full((D_MODEL, D_MODEL)), _full((S5_W, S5_W)), _full((1, S5_W))],
        out_shape=[_S((L, RG_W)), _S((L, S5_W)), _S((L, D_MODEL)), _S((D_MODEL, D_MODEL), WIRE), _S((S5_W, S5_W), WIRE),
                   _S((1, S5_W))],
        scratch_shapes=[pltpu.VMEM((D_MODEL, D_MODEL), F32), pltpu.VMEM((S5_W, S5_W), F32)],
        compiler_params=_params(1))(dt1, m, z, hs, y0, w_out, w_glu, b_glu)


def _adamw(parts, w, m, v, token=None):
    nl = len(parts)
    extra, extra_specs = _after(token)
    n, R, C = parts[0].shape
    tr = R
    for cand in (512, 256, 128, 64, 32, 16, 8):
        if R % cand == 0 and n * cand * C * 4 <= 4 * 1024 * 1024:
            tr = cand
            break
    nblk = R // tr

    def body(*refs):
        p_refs = refs[:nl]
        w_ref, m_ref, v_ref = refs[nl:nl + 3]
        g_ref, d_ref, nm_ref, nv_ref = refs[nl + 3 + len(extra):]
        layer = pl.program_id(0)
        g = None
        for li, p_ref in enumerate(p_refs):
            s = p_ref[0].astype(F32)
            for k in range(1, n):
                s = s + p_ref[k].astype(F32)
            g = s if g is None else jnp.where(layer == li, s, g)
        nm = B1 * m_ref[...] + (1.0 - B1) * g
        nv = B2 * v_ref[...] + (1.0 - B2) * (g * g)
        d_ref[...] = (-LR) * ((nm / BC1) / (jnp.sqrt(nv / BC2) + EPS) + WD * w_ref[...])
        g_ref[...], nm_ref[...], nv_ref[...] = g, nm, nv

    def part_spec(li):
        return pl.BlockSpec((n, tr, C), lambda l, i: (0, jnp.where(l == li, i, jnp.where(l < li, 0, nblk - 1)), 0))

    blk = pl.BlockSpec((tr, C), lambda l, i: (l * nblk + i, 0))
    return pl.pallas_call(
        body, name="adamw", grid=(nl, nblk),
        in_specs=[part_spec(li) for li in range(nl)] + [blk, blk, blk] + extra_specs,
        out_specs=[blk] * 4, out_shape=[_S((nl * R, C))] * 4, compiler_params=_params(2))(*parts, w, m, v, *extra)


def _adamw_natural(names, g, w, m, v, name):
    n = len(names)

    def body(*refs):
        for j in range(n):
            g_ref, w_ref, m_ref, v_ref, d_ref, nm_ref, nv_ref = (refs[k * n + j] for k in range(7))
            gj = g_ref[...]
            nm = B1 * m_ref[...] + (1.0 - B1) * gj
            nv = B2 * v_ref[...] + (1.0 - B2) * (gj * gj)
            d_ref[...] = (-LR) * ((nm / BC1) / (jnp.sqrt(nv / BC2) + EPS) + WD * w_ref[...])
            nm_ref[...], nv_ref[...] = nm, nv

    ins = [t[k] for t in (g, w, m, v) for k in names]
    outs = pl.pallas_call(body, name=name, out_shape=[_S(w[k].shape) for _ in range(3) for k in names],
                          compiler_params=pltpu.CompilerParams(vmem_limit_bytes=VMEM_LIMIT))(*ins)
    return [{k: outs[t * n + j] for j, k in enumerate(names)} for t in range(3)]


def _me():
    return lax.axis_index("x"), lax.axis_index("y"), lax.axis_index("c")


def _lin(dev):
    return 4 * dev[0] + 2 * dev[1] + dev[2]


def _blk(ref, axis, size, idx):
    nd = len(ref.shape)
    start = idx * size
    if axis == nd - 1 and size % LANE == 0:
        start = pl.multiple_of(start, LANE)
    elif axis == nd - 2 and size % 16 == 0:
        start = pl.multiple_of(start, 16)
    ix = [slice(None)] * nd
    ix[axis] = pl.ds(start, size)
    return ref.at[tuple(ix)]


def _all_gather(shards, axes, name):
    n = len(shards)
    sizes = [s.shape[a] for s, a in zip(shards, axes)]
    out_shapes = [_S(s.shape[:a] + (N_DEV * s.shape[a],) + s.shape[a + 1:], s.dtype) for s, a in zip(shards, axes)]

    def body(*refs):
        ins, outs = refs[:n], refs[n:2 * n]
        send_sems, recv_sems, local_sems = refs[2 * n:]
        x, y, c = _me()
        me, sibling = (x, y, c), (x, y, 1 - c)
        chips = [(1 - x, y), (x, 1 - y), (1 - x, 1 - y)]

        def copy(a, k, block, to, from_input=False):
            dst = _blk(outs[a], axes[a], sizes[a], _lin(block))
            return pltpu.make_async_remote_copy(
                src_ref=ins[a] if from_input else dst, dst_ref=dst, send_sem=send_sems.at[a, k],
                recv_sem=recv_sems.at[a, k], device_id=to, device_id_type=MESH)

        mine = [pltpu.make_async_copy(ins[a], _blk(outs[a], axes[a], sizes[a], _lin(me)), local_sems.at[a]) for a in range(n)]
        for cp in mine:
            cp.start()
        first = []
        for a in range(n):
            first.append(copy(a, 0, me, sibling, True))
            first += [copy(a, 1 + j, me, (*chip, c), True) for j, chip in enumerate(chips)]
        for cp in first:
            cp.start()
        passed = []
        for j, chip in enumerate(chips):
            for a in range(n):
                copy(a, 1 + j, (*chip, c), me).wait_recv()
                cp = copy(a, 4 + j, (*chip, c), sibling)
                cp.start()
                passed.append(cp)
        for a in range(n):
            copy(a, 0, sibling, me).wait_recv()
            for j, chip in enumerate(chips):
                copy(a, 4 + j, (*chip, 1 - c), me).wait_recv()
        for cp in first + passed:
            cp.wait_send()
        for cp in mine:
            cp.wait()

    return pl.pallas_call(
        body, name=name, out_shape=out_shapes, in_specs=[ANY] * n, out_specs=[ANY] * n,
        scratch_shapes=[pltpu.SemaphoreType.DMA((n, 7)), pltpu.SemaphoreType.DMA((n, 7)), pltpu.SemaphoreType.DMA((n,))],
    )(*shards)


HBM_SPEC = pl.BlockSpec(memory_space=pltpu.HBM)
SEM_SPEC = pl.BlockSpec(memory_space=pltpu.SEMAPHORE)
EFFECT = pltpu.SideEffectType.DATAFLOW_SIDE_EFFECTING


def _peers(x, y, c):
    flip = lambda v, f: 1 - v if f else v
    return [(flip(x, k & 4), flip(y, k & 2), flip(c, k & 1)) for k in range(1, N_DEV)]


def _land_shape(mode, s, axis):
    if mode == "gather":
        return s.shape[:axis] + (N_DEV * s.shape[axis],) + s.shape[axis + 1:]
    return (N_DEV,) + s.shape[:axis] + (s.shape[axis] // N_DEV,) + s.shape[axis + 1:]


def _src_view(mode, ref, axis, peer):
    return ref if mode == "gather" else _blk(ref, axis, ref.shape[axis] // N_DEV, peer)


def _dst_view(mode, land, axis, sender):
    return _blk(land, axis, land.shape[axis] // N_DEV, sender) if mode == "gather" else land.at[sender]


def _seven_blocks(mode, land, axis):
    if mode == "gather":
        ix = [slice(None)] * len(land.shape)
        ix[axis] = pl.ds(0, (N_DEV - 1) * (land.shape[axis] // N_DEV))
        return land.at[tuple(ix)]
    return land.at[pl.ds(0, N_DEV - 1)]


def _place_own(mode, srcs, axes, name, after=None):
    n = len(srcs)
    extra, extra_specs = _after(after)

    def body(me_ref, *refs):
        for a in range(n):
            out = refs[n + len(extra) + a]
            out[...] = refs[a][...].reshape(out.shape)

    def at_me(shape, axis):
        return lambda i, me: tuple(me[0] if d == axis else 0 for d in range(len(shape)))

    in_specs, out_specs = [], []
    for s, axis in zip(srcs, axes):
        if mode == "gather":
            in_specs.append(pl.BlockSpec(s.shape, lambda i, me, nd=len(s.shape): (0,) * nd))
            out_specs.append(pl.BlockSpec(s.shape, at_me(s.shape, axis)))
        else:
            blk = s.shape[:axis] + (s.shape[axis] // N_DEV,) + s.shape[axis + 1:]
            in_specs.append(pl.BlockSpec(blk, at_me(blk, axis)))
            out_specs.append(pl.BlockSpec((1,) + blk, at_me((1,) + blk, 0)))
    me = _lin(_me()).astype(jnp.int32).reshape(1)
    return pl.pallas_call(
        body, name=name, out_shape=[_S(_land_shape(mode, s, a), s.dtype) for s, a in zip(srcs, axes)],
        grid_spec=pltpu.PrefetchScalarGridSpec(num_scalar_prefetch=1, grid=(1,), in_specs=in_specs + extra_specs,
                                               out_specs=out_specs),
        compiler_params=_params(1))(me, *srcs, *extra)


def _push_start(mode, srcs, lands, axes, name):
    n = len(srcs)

    def body(*refs):
        src_refs, land_refs = refs[:n], refs[n:2 * n]
        send_sems, recv_sems = refs[2 * n], refs[2 * n + 1]
        token = refs[-1]
        x, y, c = _me()
        me = _lin((x, y, c))
        for a in range(n):
            for peer in _peers(x, y, c):
                pltpu.make_async_remote_copy(
                    src_ref=_src_view(mode, src_refs[a], axes[a], _lin(peer)),
                    dst_ref=_dst_view(mode, land_refs[a], axes[a], me),
                    send_sem=send_sems.at[a], recv_sem=recv_sems.at[a], device_id=peer, device_id_type=MESH).start()
        token[...] = jnp.zeros_like(token)

    hbm = lambda s: pltpu.HBM(s.shape, s.dtype)
    outs = pl.pallas_call(
        body, name=name,
        out_shape=(pltpu.SemaphoreType.DMA((n,)), pltpu.SemaphoreType.DMA((n,)), *[hbm(s) for s in srcs], *[hbm(s) for s in lands],
                   _S((SUB, LANE))),
        in_specs=[HBM_SPEC] * (2 * n),
        out_specs=(SEM_SPEC, SEM_SPEC, *[HBM_SPEC] * (2 * n), pl.BlockSpec(memory_space=pltpu.VMEM)),
        input_output_aliases={i: 2 + i for i in range(2 * n)},
        compiler_params=pltpu.CompilerParams(has_side_effects=EFFECT),
    )(*[pltpu.with_memory_space_constraint(s, pltpu.HBM) for s in list(srcs) + list(lands)])
    return outs[0], outs[1], outs[2:2 + n], outs[2 + n:2 + 2 * n], outs[-1]


def _push_wait(mode, send_sems, recv_sems, srcs, lands, axes, after, name):
    n = len(srcs)
    after = list(after) if isinstance(after, (list, tuple)) else [after]

    def body(*refs):
        land_refs = refs[n:2 * n]
        send_sems, recv_sems = refs[2 * n], refs[2 * n + 1]
        x, y, c = _me()
        for a in range(n):
            seven = _seven_blocks(mode, land_refs[a], axes[a])
            cp = pltpu.make_async_remote_copy(src_ref=seven, dst_ref=seven, send_sem=send_sems.at[a], recv_sem=recv_sems.at[a],
                                              device_id=(x, y, 1 - c), device_id_type=MESH)
            cp.wait_send()
            cp.wait_recv()

    hbm = lambda s: pltpu.HBM(s.shape, s.dtype)
    outs = pl.pallas_call(
        body, name=name, out_shape=tuple(hbm(s) for s in list(srcs) + list(lands)),
        in_specs=[HBM_SPEC] * (2 * n) + [SEM_SPEC, SEM_SPEC] + [ANY] * len(after), out_specs=tuple([HBM_SPEC] * (2 * n)),
        input_output_aliases={i: i for i in range(2 * n)},
        compiler_params=pltpu.CompilerParams(has_side_effects=EFFECT),
    )(*srcs, *lands, send_sems, recv_sems, *after)
    return outs[n:]


def _sum_parts(parts):
    n, R, C = parts.shape

    def body(p_ref, o_ref):
        g = p_ref[0]
        for k in range(1, n):
            g = g + p_ref[k]
        o_ref[...] = g

    return pl.pallas_call(body, name="sum_parts", out_shape=_S((R, C)))(parts)


SMALL =['conv_b', 'rg_wa', 'rg_ba', 'rg_wx', 'rg_bx', 'rg_lambda', 's5_a_re', 's5_a_im', 's5_b_re', 's5_b_im',
         's5_c_re', 's5_c_im', 's5_d', 's5_log_step', 's5_b_glu', 'ln1_g', 'ln1_b', 'ple_gate_b', 'ln2_g', 'ln2_b']
WEIGHTS = ['w_in', 'conv_w', 'conv_b', 'rg_wa', 'rg_ba', 'rg_wx', 'rg_bx', 'rg_lambda', 's5_a_re', 's5_a_im', 's5_b_re',
           's5_b_im', 's5_c_re', 's5_c_im', 's5_d', 's5_log_step', 's5_w_glu', 's5_b_glu', 'w_out', 'ln1_g', 'ln1_b',
           'ple_w', 'ple_gate_w', 'ple_gate_b', 'ln2_g', 'ln2_b']
PACK_ROWS_MULT = 64


def _pack(tree, scalar):
    flat = jnp.concatenate([tree[k].reshape(-1) for k in SMALL] + [scalar.reshape(1)])
    rows = -(-flat.shape[0] // (LANE * PACK_ROWS_MULT)) * PACK_ROWS_MULT
    return jnp.pad(flat, (0, rows * LANE - flat.shape[0])).reshape(rows, LANE)


def _unpack(packed, like):
    flat, out, o = packed.reshape(-1), {}, 0
    for k in SMALL:
        n = math.prod(like[k].shape)
        out[k] = flat[o:o + n].reshape(like[k].shape)
        o += n
    return out, flat[o]


class _NoHooks:
    token = None
    first_token = None

    def first_weights(self, full, after):
        return full

    def layer_start(self, i, W, after):
        return W

    def late_weights(self, i, W, after):
        return W

    def post_done(self, i, g):
        return None

    def smalls_done(self, grads, loss):
        self.small = _small_grads(grads, self.res)
        return None

    def w_in_done(self, i, g):
        return None

    def layer_done(self, i, g, dx):
        return None


def _local_grads(x, p, target, W, disc, hooks):
    depth = 2
    saved = []
    for i in range(depth):
        if i > 0:
            W = hooks.layer_start(i, W, x)
        w = W[i]
        z = _inproj_fwd(x, w['w_in'], hooks.token if i == 0 else None)
        hs = _rg_fwd(z, w['conv_w'], w['conv_b'], w['wa_bd'], w['wx_bd'], w['rg_ba'], w['rg_bx'], w['rg_lambda'], i)
        d = disc[i]
        y0, s_re, s_im = _s5_fwd(z, d['bb_re'], d['bb_im'], d['lb_re'], d['lb_im'], d['c_re'], d['c_im'], w['s5_d'], i)
        W = hooks.late_weights(i, W, y0)
        w = W[i]
        x2, *norms = _post_fwd(x, hs, z, y0, p, w['s5_w_glu'], w['s5_b_glu'], w['w_out'], w['ln1_g'], w['ln1_b'],
                               w['ple_w'], w['ple_gate_w'], w['ple_gate_b'], w['ln2_g'], w['ln2_b'], i)
        saved.append((x, z, hs, y0, s_re, s_im, norms))
        x = x2

    grads = [None] * depth
    dx = target
    loss = None
    token = None
    for i in reversed(range(depth)):
        w, d = W[i], disc[i]
        xin, z, hs, y0, s_re, s_im, (xh1, xh2, m, q, gt, rstd1, rstd2) = saved[i]
        g = {}
        (dt1, g['ple_w'], g['ple_gate_w'], g['ple_gate_b'], g['ln1_g'], g['ln1_b'], g['ln2_g'], g['ln2_b'], lrow) = _post_bwd_a(
            dx, i == depth - 1, xh2, xh1, rstd2, rstd1, q, gt, p, w['ple_gate_w'], w['ln1_g'], w['ln1_b'],
            w['ln2_g'], w['ln2_b'], i, token)
        if i == depth - 1:
            loss = 0.5 / D_MODEL * jnp.sum(lrow)
        dhs, dy0, dzg, g['w_out'], g['s5_w_glu'], g['s5_b_glu'] = _post_bwd_b(dt1, m, z, hs, y0, w['w_out'], w['s5_w_glu'],
                                                                           w['s5_b_glu'], i)
        (dzu, g['bb_re'], g['bb_im'], g['lb_re'], g['lb_im'], g['c_re'], g['c_im'], g['s5_d']) = _s5_bwd(
            dy0, z, s_re, s_im, d['bb_re'], d['bb_im'], d['lb_re'], d['lb_im'], d['c_re'], d['c_im'], w['s5_d'], i,
            hooks.post_done(i, g))
        (dzx, g['conv_w'], g['conv_b'], g['wa_bd'], g['wx_bd'], g['rg_ba'], g['rg_bx'], g['rg_lambda']) = _rg_bwd(
            dhs, z, hs, w['conv_w'], w['conv_b'], w['wa_bd'], w['wx_bd'], w['rg_ba'], w['rg_bx'], w['rg_lambda'], i)
        if i == 0:
            g['w_in'] = _inproj_bwd_dw(xin, dzx, dzg, dzu, hooks.smalls_done([g, grads[1]], loss))
            dx = _inproj_bwd_dx(dt1, dzx, dzg, dzu, w['w_in'], hooks.w_in_done(i, g))
        else:
            dx, g['w_in'] = _inproj_bwd(dt1, xin, dzx, dzg, dzu, w['w_in'])
        grads[i] = g
        token = hooks.layer_done(i, g, dx)
    return loss, dx, grads


def _s5_layouts_fwd(s5_a_re, s5_a_im, s5_log_step, s5_b_re, s5_b_im, s5_c_re, s5_c_im, token=None):
    depth = s5_a_re.shape[0]
    ar, ai = s5_a_re.reshape(depth * 24, S5_P), s5_a_im.reshape(depth * 24, S5_P)
    ls = s5_log_step.reshape(depth * 24, 1)
    lr, li, cr, ci = _s5_disc_fwd(ar, ai, ls, token)
    per_group = lambda a: a.reshape(depth * 24, 1, S5_P)
    as_c = lambda b: jnp.swapaxes(b, 2, 3).reshape(depth * 24, S5_H, S5_P)
    res = (ar, ai, ls, per_group(cr), per_group(ci), as_c(s5_b_re), as_c(s5_b_im))
    bbr, bbi = _s5_bscale_fwd(*res[3:])
    tiles = lambda a: a.reshape(depth * N_S5_T, S5_GT, S5_H, S5_P)
    rows = lambda a: a.reshape(depth * N_S5_T, S5_GT, S5_P)
    disc = dict(bb_re=tiles(bbr), bb_im=tiles(bbi), lb_re=rows(lr), lb_im=rows(li), c_re=tiles(s5_c_re), c_im=tiles(s5_c_im))
    return [disc] * depth, res


def _s5_layouts_bwd(grads, res):
    ar, ai, ls, cr, ci, br, bi = res
    depth = len(grads)
    stack = lambda k, shape: jnp.stack([g[k] for g in grads]).reshape(shape)
    groups, shape_c = (depth * 24, S5_H, S5_P), (depth, 24, S5_H, S5_P)
    dbr, dbi, dcr, dci = _s5_bscale_bwd(cr, ci, br, bi, stack('bb_re', groups), stack('bb_im', groups))
    gp = (depth * 24, S5_P)
    dar, dai, dls = _s5_disc_bwd(ar, ai, ls, stack('lb_re', gp), stack('lb_im', gp), dcr.reshape(gp), dci.reshape(gp))
    return dict(
        s5_a_re=dar.reshape(depth, 24, S5_P), s5_a_im=dai.reshape(depth, 24, S5_P), s5_log_step=dls.reshape(depth, 24),
        s5_b_re=jnp.swapaxes(dbr.reshape(shape_c), 2, 3), s5_b_im=jnp.swapaxes(dbi.reshape(shape_c), 2, 3),
        s5_c_re=stack('c_re', shape_c), s5_c_im=stack('c_im', shape_c))


LATE = ('w_out', 'ple_w', 'ple_gate_w', 's5_w_glu')


ROWS = ('conv_b', 'rg_ba', 'rg_bx', 'rg_lambda', 's5_d', 's5_b_glu', 'ln1_g', 'ln1_b', 'ple_gate_b', 'ln2_g', 'ln2_b')


def _shared_weights(full):
    depth = full['conv_b'].shape[0]
    shared = {k: full[k].reshape(depth, 1, -1) for k in ROWS}
    shared.update(conv_w=full['conv_w'], wa_bd=full['rg_wa'], wx_bd=full['rg_wx'])
    return shared


def _layer_weights(full, shared, i):
    return dict(shared, w_in=full['w_in'][i])


class _AllLocal(_NoHooks):
    def __init__(self, full):
        self.full = full

    def late_weights(self, i, W, after):
        W[i].update({k: self.full[k][i] for k in LATE})
        return W


def _full_grads(full, x, p, target, hooks=None):
    hooks = hooks or _AllLocal(full)
    disc, res = _s5_layouts_fwd(full['s5_a_re'], full['s5_a_im'], full['s5_log_step'], full['s5_b_re'], full['s5_b_im'],
                                full['s5_c_re'], full['s5_c_im'], hooks.first_token)
    full = hooks.first_weights(full, disc[-1]['bb_im'])
    shared = _shared_weights(full)
    W = [_layer_weights(full, shared, i) for i in range(2)]
    hooks.res = res
    loss, gx, grads = _local_grads(x, p, target, W, disc, hooks)
    out = dict(hooks.small)
    for k in SHARD_AXIS:
        out[k] = [g[k] for g in grads]
    return loss, gx, out


def _small_grads(grads, res):
    stack = lambda f: jnp.stack([f(g) for g in grads])
    out = _s5_layouts_bwd(grads, res)
    out['conv_w'] = stack(lambda g: g['conv_w'])
    for k in ('conv_b', 'rg_ba', 'rg_bx', 'rg_lambda', 's5_b_glu', 'ln1_g', 'ln1_b', 'ple_gate_b', 'ln2_g', 'ln2_b'):
        out[k] = stack(lambda g: g[k][0])
    out['s5_d'] = stack(lambda g: g['s5_d'][0]).reshape(2, 24, 16)
    out['rg_wa'] = stack(lambda g: g['wa_bd'])
    out['rg_wx'] = stack(lambda g: g['wx_bd'])
    return out


SHARD_AXIS = {'w_in': 2, 'w_out': 1, 'ple_w': 2, 'ple_gate_w': 1, 's5_w_glu': 1}


def kernel(x, p, w_in, conv_w, conv_b, rg_wa, rg_ba, rg_wx, rg_bx, rg_lambda, s5_a_re, s5_a_im, s5_b_re, s5_b_im, s5_c_re, s5_c_im, s5_d, s5_log_step, s5_w_glu, s5_b_glu, w_out, ln1_g, ln1_b, ple_w, ple_gate_w, ple_gate_b, ln2_g, ln2_b, loss_target, m_w_in, m_conv_w, m_conv_b, m_rg_wa, m_rg_ba, m_rg_wx, m_rg_bx, m_rg_lambda, m_s5_a_re, m_s5_a_im, m_s5_b_re, m_s5_b_im, m_s5_c_re, m_s5_c_im, m_s5_d, m_s5_log_step, m_s5_w_glu, m_s5_b_glu, m_w_out, m_ln1_g, m_ln1_b, m_ple_w, m_ple_gate_w, m_ple_gate_b, m_ln2_g, m_ln2_b, v_w_in, v_conv_w, v_conv_b, v_rg_wa, v_rg_ba, v_rg_wx, v_rg_bx, v_rg_lambda, v_s5_a_re, v_s5_a_im, v_s5_b_re, v_s5_b_im, v_s5_c_re, v_s5_c_im, v_s5_d, v_s5_log_step, v_s5_w_glu, v_s5_b_glu, v_w_out, v_ln1_g, v_ln1_b, v_ple_w, v_ple_gate_w, v_ple_gate_b, v_ln2_g, v_ln2_b):
    local = dict(locals())
    w = {k: local[k] for k in WEIGHTS}
    mom = {k: local['m_' + k] for k in WEIGHTS}
    var = {k: local['v_' + k] for k in WEIGHTS}

    big = list(SHARD_AXIS)
    wire = {k: w[k].astype(WIRE) for k in big}
    late_axes = [SHARD_AXIS[k] - 1 for k in LATE]
    pushed = {}

    def push_weights(key, srcs, axes, after):
        pushed[key] = _push_start("gather", srcs, _place_own("gather", srcs, axes, "place_weights_" + key, after=after), axes,
                                  "push_weights_" + key)
        return pushed[key][4]

    def await_weights(key, axes, after):
        s = pushed[key]
        return _push_wait("gather", s[0], s[1], s[2], s[3], axes, after, "await_weights_" + key)

    token_first = push_weights("first", [wire['w_in'][0], conv_w[None]], [1, 0], None)
    token0 = push_weights("l0", [wire[k][0] for k in LATE], late_axes, token_first)
    push_weights("l1", [wire['w_in'][1]] + [wire[k][1] for k in LATE], [1] + late_axes, token0)

    def push_grads(key, g, names, axes):
        srcs = [g[k] for k in names]
        pushed[key] = _push_start("scatter", srcs, _place_own("scatter", srcs, axes, "place_grads_" + key), axes,
                                  "push_grads_" + key)
        return pushed[key][4]

    def await_grads(key, axes, after):
        s = pushed[key]
        return _push_wait("scatter", s[0], s[1], s[2], s[3], axes, after, "await_grads_" + key)

    class Overlap(_NoHooks):
        token = pushed["l1"][4]
        first_token = token

        def first_weights(self, full, after):
            w_in0, conv = await_weights("first", [1, 0], after)
            return dict(full, w_in=[w_in0, None], conv_w=jnp.moveaxis(conv, 0, 2).reshape(2, 4, RG_W))

        def late_weights(self, i, W, after):
            if i == 0:
                W[0].update(zip(LATE, await_weights("l0", late_axes, after)))
            return W

        def layer_start(self, i, W, after):
            lands = await_weights("l1", [1] + late_axes, after)
            W[1].update(zip(LATE, lands[1:]), w_in=lands[0])
            return W

        def post_done(self, i, g):
            return push_grads("late0", g, LATE, late_axes) if i == 0 else None

        def smalls_done(self, grads, loss):
            super().smalls_done(grads, loss)
            conv = jnp.moveaxis(self.small['conv_w'].reshape(2, 4, N_DEV, RG_W // N_DEV), 2, 0)
            self.packed = _pack(self.small, loss)
            return push_grads("small", dict(conv_w=conv.reshape(N_DEV, 8, RG_W // N_DEV), small=self.packed),
                              ['conv_w', 'small'], [0, 0])

        def w_in_done(self, i, g):
            return push_grads("w_in0", g, ['w_in'], [0])

        def layer_done(self, i, g, dx):
            return push_grads("all1", g, ['w_in'] + list(LATE), [0] + late_axes) if i == 1 else None

    hooks = Overlap()
    _, grad_x, g = _full_grads(dict(w), x[0], p, loss_target[0], hooks)

    recv1 = dict(zip(['w_in'] + list(LATE), await_grads("all1", [0] + late_axes, grad_x)))
    recv0 = dict(zip(LATE, await_grads("late0", late_axes, grad_x)))
    outs = {}

    def update(k, parts):
        shard = w[k].shape
        c = shard[-1]
        two = lambda a: a.reshape(-1, c)
        res = _adamw([r.reshape(N_DEV, -1, c) for r in parts], two(w[k]), two(mom[k]), two(var[k]))
        outs[k] = [o.reshape(shard) for o in res]

    for k in LATE:
        update(k, [recv0[k], recv1[k]])
    done = [outs[k][1] for k in LATE]
    conv_parts, small_parts = await_grads("small", [0, 0], done)

    rows = hooks.packed.shape[0] // N_DEV
    mine = _sum_parts(small_parts.reshape(N_DEV, rows, LANE))
    gathered = _all_gather([mine], [0], "gather_small_grads")[0]
    w_in0, = await_grads("w_in0", [0], gathered)
    update('w_in', [w_in0, recv1['w_in']])
    update('conv_w', [conv_parts])
    summed, loss = _unpack(gathered, w)
    narrow = ['s5_b_re', 's5_b_im']
    for names, name in ((narrow, "adamw_s5_b"), ([k for k in SMALL if k not in narrow], "adamw_small")):
        delta, new_m, new_v = _adamw_natural(names, summed, w, mom, var, name)
        for k in names:
            outs[k] = [summed[k], delta[k], new_m[k], new_v[k]]

    res = [loss, grad_x[None]]
    for j in range(4):
        res += [outs[k][j] for k in WEIGHTS]
    return tuple(res)
```

```python
import math

import jax
import jax.numpy as jnp
from jax import lax
from jax.experimental import pallas as pl
from jax.experimental.pallas import tpu as pltpu

F32 = jnp.float32
MXU = jnp.bfloat16
WIRE = jnp.bfloat16

N_DEV = 8
D_MODEL = 1024
PLE_D = 256
RG_W = 640
S5_W = 384
S5_P = 64
S5_N = 24 * S5_P
Z_W = 2 * RG_W + 2 * S5_W
C_RGG = RG_W
C_S5U = 2 * RG_W
C_S5G = 2 * RG_W + S5_W
LANE = 128
N_RG_T = RG_W // LANE
N_S5_T = S5_W // LANE
W_BLK = Z_W // N_DEV
ALPHA = (2.0 * 2) ** 0.25
LN_EPS = 1e-5
RG_C = 8.0
LR, B1, B2, EPS, WD, STEP = 0.001, 0.9, 0.999, 1e-08, 0.01, 10
BC1 = 1.0 - B1 ** STEP
BC2 = 1.0 - B2 ** STEP
RC = 256
TM = 256
TM_MM = 1024
VMEM_LIMIT = 56 * 1024 * 1024

MESH = pl.DeviceIdType.MESH
ANY = pl.BlockSpec(memory_space=pl.ANY)


def _params(n_grid_axes, vmem=VMEM_LIMIT):
    return pltpu.CompilerParams(dimension_semantics=("arbitrary",) * n_grid_axes, vmem_limit_bytes=vmem)


def _S(shape, dtype=F32):
    return jax.ShapeDtypeStruct(tuple(shape), dtype)


def _sigmoid(x):
    return 0.5 * jnp.tanh(0.5 * x) + 0.5


def _silu_and_grad(x):
    s = _sigmoid(x)
    return x * s, s * (1.0 + x * (1.0 - s))


_GELU_C = math.sqrt(2.0 / math.pi)


def _gelu(x):
    return 0.5 * x * (1.0 + jnp.tanh(_GELU_C * (x + 0.044715 * (x * x * x))))


def _gelu_grad(x):
    th = jnp.tanh(_GELU_C * (x + 0.044715 * (x * x * x)))
    return 0.5 * (1.0 + th) + 0.5 * x * (1.0 - th * th) * (_GELU_C * (1.0 + 3.0 * 0.044715 * (x * x)))


def _mm(a, b):
    return jnp.dot(a.astype(MXU), b.astype(MXU), preferred_element_type=F32)


def _mm_nt(a, b):
    return lax.dot_general(a.astype(MXU), b.astype(MXU), (((1,), (1,)), ((), ())), preferred_element_type=F32)


def _mm_tn(a, b):
    return lax.dot_general(a.astype(MXU), b.astype(MXU), (((0,), (0,)), ((), ())), preferred_element_type=F32)


def _ln_fwd(t, g, b):
    mu = jnp.mean(t, axis=-1, keepdims=True)
    tc = t - mu
    var = jnp.mean(tc * tc, axis=-1, keepdims=True)
    rstd = lax.rsqrt(var + LN_EPS)
    xhat = tc * rstd
    return xhat * g + b, xhat, rstd


def _ln_bwd(dy, xhat, rstd, g):
    dxh = dy * g
    m1 = jnp.mean(dxh, axis=-1, keepdims=True)
    m2 = jnp.mean(dxh * xhat, axis=-1, keepdims=True)
    return rstd * (dxh - m1 - xhat * m2)


def _colsum(a):
    return jnp.sum(a, axis=0, keepdims=True)


def _up(x, d, rows, fill):
    n = x.shape[0]
    return jnp.where(rows < n - d, pltpu.roll(x, n - d, 0), fill)


SUB = 8
TILE_STEPS = (1, 2, 4)


def _r8(width):
    return lax.broadcasted_iota(jnp.int32, (SUB, width), 0)


def _scan_real(a, u, carry, reverse=False):
    r8 = _r8(a.shape[1])
    n = a.shape[0] // SUB
    outs = [None] * n
    for k in (reversed(range(n)) if reverse else range(n)):
        A, U = a[SUB * k:SUB * k + SUB], u[SUB * k:SUB * k + SUB]
        for d in TILE_STEPS:
            m = (r8 < SUB - d) if reverse else (r8 >= d)
            sh = SUB - d if reverse else d
            U = A * jnp.where(m, pltpu.roll(U, sh, 0), 0.0) + U
            A = A * jnp.where(m, pltpu.roll(A, sh, 0), 1.0)
        h = A * carry + U
        outs[k] = h
        carry = h[0:1] if reverse else h[SUB - 1:SUB]
    return jnp.concatenate(outs, axis=0), carry


def _tile_powers(lr, li, reverse=False):
    width = lr.shape[1]
    r8 = _r8(width)
    steps = []
    pr, pi = lr, li
    er, ei = jnp.broadcast_to(lr, (SUB, width)), jnp.broadcast_to(li, (SUB, width))
    for d in TILE_STEPS:
        m = (r8 < SUB - d) if reverse else (r8 >= d)
        sh = SUB - d if reverse else d
        steps.append((sh, jnp.where(m, pr, 0.0), jnp.where(m, pi, 0.0)))
        er, ei = _cmul(er, ei, jnp.where(m, pltpu.roll(er, sh, 0), 1.0), jnp.where(m, pltpu.roll(ei, sh, 0), 0.0))
        pr, pi = _cmul(pr, pi, pr, pi)
    return steps, (er, ei)


def _scan_lti(xr, xi, carry, steps, e, reverse=False):
    er, ei = e
    kr, ki = carry
    n = xr.shape[0] // SUB
    outr, outi = [None] * n, [None] * n
    for k in (reversed(range(n)) if reverse else range(n)):
        sr, si = xr[SUB * k:SUB * k + SUB], xi[SUB * k:SUB * k + SUB]
        for sh, pr, pi in steps:
            shr, shi = pltpu.roll(sr, sh, 0), pltpu.roll(si, sh, 0)
            sr, si = sr + (pr * shr - pi * shi), si + (pr * shi + pi * shr)
        sr = sr + (er * kr - ei * ki)
        si = si + (er * ki + ei * kr)
        outr[k], outi[k] = sr, si
        kr, ki = (sr[0:1], si[0:1]) if reverse else (sr[SUB - 1:SUB], si[SUB - 1:SUB])
    return jnp.concatenate(outr, axis=0), jnp.concatenate(outi, axis=0), (kr, ki)


def _halo(ref, c, r0):
    rp = pl.multiple_of(jnp.maximum(r0 - 8, 0), 8)
    return jnp.where(c > 0, ref[pl.ds(rp, 8), :], 0.0)


def _conv_taps(xe):
    return [pltpu.roll(xe, 3, 0)[8:, :], pltpu.roll(xe, 2, 0)[8:, :], pltpu.roll(xe, 1, 0)[8:, :], xe[8:, :]]


def _rg_gates(h, wa, wx, ba, bx, sp):
    r = _sigmoid(_mm(h, wa) + ba)
    i = _sigmoid(_mm(h, wx) + bx)
    log_a = (-RG_C) * r * sp
    a = jnp.exp(log_a)
    mult = jnp.sqrt(-jnp.tanh(log_a) * (a * a + 1.0))
    return r, i, a, mult


def _softplus(y):
    return jnp.maximum(y, 0.0) + jnp.log1p(jnp.exp(-jnp.abs(y)))


def _after(token):
    return ([], []) if token is None else ([token], [ANY])


def _inproj_fwd(x, w_in, token=None):
    L = x.shape[0]

    def body(x_ref, w_ref, *rest):
        rest[-1][...] = _mm(x_ref[...], w_ref[...])

    extra, extra_specs = _after(token)
    tm = min(TM_MM, L)
    return pl.pallas_call(
        body, name="inproj_fwd", grid=(L // tm,),
        in_specs=[pl.BlockSpec((tm, D_MODEL), lambda i: (i, 0)), pl.BlockSpec((D_MODEL, Z_W), lambda i: (0, 0))] + extra_specs,
        out_specs=pl.BlockSpec((tm, Z_W), lambda i: (i, 0)),
        out_shape=_S((L, Z_W)), compiler_params=_params(1))(x, w_in, *extra)


def _inproj_bwd(dt1, x, dzx, dzg, dzu, w_in):
    L = x.shape[0]

    def body(dt1_ref, x_ref, dzx_ref, dzg_ref, dzu_ref, w_ref, dx_ref, dw_ref, acc_ref):
        @pl.when(pl.program_id(0) == 0)
        def _():
            acc_ref[...] = jnp.zeros_like(acc_ref)
        dzg = dzg_ref[...]
        dz = jnp.concatenate([dzx_ref[...], dzg[:, :RG_W], dzu_ref[...], dzg[:, RG_W:]], axis=1).astype(MXU)
        xb = x_ref[...].astype(MXU)
        dx_ref[...] = ALPHA * dt1_ref[...] + _mm_nt(dz, w_ref[...])
        for j in range(N_DEV):
            acc_ref[j] += _mm_tn(xb, dz[:, j * W_BLK:(j + 1) * W_BLK])

        @pl.when(pl.program_id(0) == L // TM - 1)
        def _():
            dw_ref[...] = acc_ref[...].astype(WIRE)

    row = lambda w: pl.BlockSpec((TM, w), lambda i: (i, 0))
    wspec = pl.BlockSpec((N_DEV, D_MODEL, W_BLK), lambda i: (0, 0, 0))
    return pl.pallas_call(
        body, name="inproj_bwd", grid=(L // TM,),
        in_specs=[row(D_MODEL), row(D_MODEL), row(RG_W), row(D_MODEL), row(S5_W),
                  pl.BlockSpec((D_MODEL, Z_W), lambda i: (0, 0))],
        out_specs=[row(D_MODEL), wspec],
        out_shape=[_S((L, D_MODEL)), _S((N_DEV, D_MODEL, W_BLK), WIRE)],
        scratch_shapes=[pltpu.VMEM((N_DEV, D_MODEL, W_BLK), F32)],
        compiler_params=_params(1))(dt1, x, dzx, dzg, dzu, w_in)


TM2 = 512


def _dz_block(dzx_ref, dzg_ref, dzu_ref):
    dzg = dzg_ref[...]
    return jnp.concatenate([dzx_ref[...], dzg[:, :RG_W], dzu_ref[...], dzg[:, RG_W:]], axis=1).astype(MXU)


def _inproj_bwd_dw(x, dzx, dzg, dzu, token=None):
    L = x.shape[0]
    extra, extra_specs = _after(token)

    def body(x_ref, dzx_ref, dzg_ref, dzu_ref, *rest):
        dw_ref, acc_ref = rest[len(extra):]
        @pl.when(pl.program_id(0) == 0)
        def _():
            acc_ref[...] = jnp.zeros_like(acc_ref)
        dz = _dz_block(dzx_ref, dzg_ref, dzu_ref)
        xb = x_ref[...].astype(MXU)
        for j in range(N_DEV):
            acc_ref[j] += _mm_tn(xb, dz[:, j * W_BLK:(j + 1) * W_BLK])

        @pl.when(pl.program_id(0) == L // TM2 - 1)
        def _():
            dw_ref[...] = acc_ref[...].astype(WIRE)

    row = lambda w: pl.BlockSpec((TM2, w), lambda i: (i, 0))
    wspec = pl.BlockSpec((N_DEV, D_MODEL, W_BLK), lambda i: (0, 0, 0))
    return pl.pallas_call(
        body, name="inproj_bwd_dw", grid=(L // TM2,),
        in_specs=[row(D_MODEL), row(RG_W), row(D_MODEL), row(S5_W)] + extra_specs, out_specs=wspec,
        out_shape=_S((N_DEV, D_MODEL, W_BLK), WIRE), scratch_shapes=[pltpu.VMEM((N_DEV, D_MODEL, W_BLK), F32)],
        compiler_params=_params(1))(x, dzx, dzg, dzu, *extra)


def _inproj_bwd_dx(dt1, dzx, dzg, dzu, w_in, token=None):
    L = dt1.shape[0]
    extra, extra_specs = _after(token)

    def body(dt1_ref, dzx_ref, dzg_ref, dzu_ref, w_ref, *rest):
        rest[-1][...] = ALPHA * dt1_ref[...] + _mm_nt(_dz_block(dzx_ref, dzg_ref, dzu_ref), w_ref[...])

    tm = min(TM_MM, L)
    row = lambda w: pl.BlockSpec((tm, w), lambda i: (i, 0))
    return pl.pallas_call(
        body, name="inproj_bwd_dx", grid=(L // tm,),
        in_specs=[row(D_MODEL), row(RG_W), row(D_MODEL), row(S5_W), _full((D_MODEL, Z_W))] + extra_specs,
        out_specs=row(D_MODEL), out_shape=_S((L, D_MODEL)), compiler_params=_params(1))(dt1, dzx, dzg, dzu, w_in, *extra)


def _rg_specs(layer):
    tile = lambda rows: pl.BlockSpec((rows, LANE), lambda c: (0, c))
    ptile = lambda rows: pl.BlockSpec((None, rows, LANE), lambda c: (layer, 0, c))
    pheads = pl.BlockSpec((None, 2, RG_HD, RG_HD), lambda c: (layer, c, 0, 0))
    return tile, ptile, pheads, pl.BlockSpec((2, RG_HD, RG_HD), lambda c: (c, 0, 0))


RG_HD = 64


def _bd2(w):
    z = jnp.zeros((RG_HD, RG_HD), w.dtype)
    return jnp.concatenate([jnp.concatenate([w[0], z], axis=1), jnp.concatenate([z, w[1]], axis=1)], axis=0)


def _bd2_diag(m):
    return jnp.stack([m[:RG_HD, :RG_HD], m[RG_HD:, RG_HD:]])


def _rg_fwd(z, cw, cb, wa_bd, wx_bd, ba, bx, lam, layer):
    L = z.shape[0]

    def body(x_ref, cw_ref, cb_ref, wa_ref, wx_ref, ba_ref, bx_ref, lam_ref, hs_ref):
        w, b = cw_ref[...], cb_ref[...]
        wa, wx, ba_, bx_ = _bd2(wa_ref[...]).astype(MXU), _bd2(wx_ref[...]).astype(MXU), ba_ref[...], bx_ref[...]
        sp = _softplus(-lam_ref[...])

        def step(c, carry):
            r0 = pl.multiple_of(c * RC, RC)
            xe = jnp.concatenate([_halo(x_ref, c, r0), x_ref[pl.ds(r0, RC), :]], axis=0)
            t = _conv_taps(xe)
            h = t[0] * w[0:1] + t[1] * w[1:2] + t[2] * w[2:3] + t[3] * w[3:4] + b
            _, i, a, mult = _rg_gates(h, wa, wx, ba_, bx_, sp)
            hs, carry = _scan_real(a, mult * (i * h), carry)
            hs_ref[pl.ds(r0, RC), :] = hs
            return carry

        lax.fori_loop(0, L // RC, step, jnp.zeros((1, LANE), F32))

    tile, ptile, pheads, _ = _rg_specs(layer)
    return pl.pallas_call(
        body, name="rg_fwd", grid=(N_RG_T,),
        in_specs=[tile(L), ptile(4), ptile(1), pheads, pheads, ptile(1), ptile(1), ptile(1)],
        out_specs=tile(L), out_shape=_S((L, RG_W)), compiler_params=_params(1))(z, cw, cb, wa_bd, wx_bd, ba, bx, lam)


def _rg_bwd(dhs, z, hs, cw, cb, wa_bd, wx_bd, ba, bx, lam, layer):
    L = z.shape[0]

    def body(g_ref, x_ref, hs_ref, cw_ref, cb_ref, wa_ref, wx_ref, ba_ref, bx_ref, lam_ref,
             dx_ref, dcw_ref, dcb_ref, dwa_out, dwx_out, dba_ref, dbx_ref, dlam_ref, dwa_ref, dwx_ref):
        w, b = cw_ref[...], cb_ref[...]
        wa, wx, ba_, bx_ = _bd2(wa_ref[...]).astype(MXU), _bd2(wx_ref[...]).astype(MXU), ba_ref[...], bx_ref[...]
        lam = lam_ref[...]
        sp = _softplus(-lam)
        rows = lax.broadcasted_iota(jnp.int32, (RC, LANE), 0)
        for ref in (dcw_ref, dcb_ref, dwa_ref, dwx_ref, dba_ref, dbx_ref, dlam_ref):
            ref[...] = jnp.zeros_like(ref)
        nch = L // RC

        def step(k, carry):
            cin, nxt = carry
            c = nch - 1 - k
            r0 = pl.multiple_of(c * RC, RC)
            xe = jnp.concatenate([_halo(x_ref, c, r0), x_ref[pl.ds(r0, RC), :]], axis=0)
            t = _conv_taps(xe)
            h = t[0] * w[0:1] + t[1] * w[1:2] + t[2] * w[2:3] + t[3] * w[3:4] + b
            r, i, a, mult = _rg_gates(h, wa, wx, ba_, bx_, sp)
            hs_e = jnp.concatenate([_halo(hs_ref, c, r0), hs_ref[pl.ds(r0, RC), :]], axis=0)
            hs_prev = pltpu.roll(hs_e, 1, 0)[8:, :]
            g = g_ref[pl.ds(r0, RC), :]
            cc, cin_new = _scan_real(a, a * g, cin, reverse=True)
            dh = g + _up(cc, 1, rows, cin)
            ih = i * h
            dlog_a = dh * hs_prev * a - (dh * ih) * (a * a) / mult
            di = dh * mult * h
            dhin = dh * mult * i
            dr = dlog_a * ((-RG_C) * sp)
            dlam_ref[...] += _colsum(dlog_a * r)
            dra = dr * r * (1.0 - r)
            dia = di * i * (1.0 - i)
            dwa_ref[...] += _mm_tn(h, dra)
            dwx_ref[...] += _mm_tn(h, dia)
            dba_ref[...] += _colsum(dra)
            dbx_ref[...] += _colsum(dia)
            dhin = dhin + _mm_nt(dra, wa) + _mm_nt(dia, wx)
            de = jnp.concatenate([dhin, nxt], axis=0)
            n = RC + 8
            dx = (dhin * w[3:4] + pltpu.roll(de, n - 1, 0)[:RC, :] * w[2:3]
                  + pltpu.roll(de, n - 2, 0)[:RC, :] * w[1:2] + pltpu.roll(de, n - 3, 0)[:RC, :] * w[0:1])
            dx_ref[pl.ds(r0, RC), :] = dx
            for kk in range(4):
                dcw_ref[kk:kk + 1, :] += _colsum(dhin * t[kk])
            dcb_ref[...] += _colsum(dhin)
            return cin_new, dhin[0:8, :]

        lax.fori_loop(0, nch, step, (jnp.zeros((1, LANE), F32), jnp.zeros((8, LANE), F32)))
        dlam_ref[...] = dlam_ref[...] * (RG_C * _sigmoid(-lam))
        dwa_out[...], dwx_out[...] = _bd2_diag(dwa_ref[...]), _bd2_diag(dwx_ref[...])

    tile, ptile, pheads, gheads = _rg_specs(layer)
    heads = _S((2 * N_RG_T, RG_HD, RG_HD))
    return pl.pallas_call(
        body, name="rg_bwd", grid=(N_RG_T,),
        in_specs=[tile(L), tile(L), tile(L), ptile(4), ptile(1), pheads, pheads, ptile(1), ptile(1), ptile(1)],
        out_specs=[tile(L), tile(4), tile(1), gheads, gheads, tile(1), tile(1), tile(1)],
        out_shape=[_S((L, RG_W)), _S((4, RG_W)), _S((1, RG_W)), heads, heads, _S((1, RG_W)), _S((1, RG_W)), _S((1, RG_W))],
        scratch_shapes=[pltpu.VMEM((LANE, LANE), F32), pltpu.VMEM((LANE, LANE), F32)],
        compiler_params=_params(1))(dhs, z, hs, cw, cb, wa_bd, wx_bd, ba, bx, lam)


def _cmul(ar, ai, br, bi):
    return ar * br - ai * bi, ar * bi + ai * br


S5_TW = S5_N // N_S5_T


S5_H = 16
S5_GT = LANE // S5_H


def _s5_specs(L, layer):
    in_tile = pl.BlockSpec((L, LANE), lambda t: (0, t))
    st = pl.BlockSpec((L, S5_TW), lambda t: (0, t))
    pg = pl.BlockSpec((None, S5_GT, S5_H, S5_P), lambda t: (layer * N_S5_T + t, 0, 0, 0))
    plb = pl.BlockSpec((None, S5_GT, S5_P), lambda t: (layer * N_S5_T + t, 0, 0))
    gg = pl.BlockSpec((None, S5_GT, S5_H, S5_P), lambda t: (t, 0, 0, 0))
    glb = pl.BlockSpec((None, S5_GT, S5_P), lambda t: (t, 0, 0))
    dv = pl.BlockSpec((1, LANE), lambda t: (0, t))
    return in_tile, st, pg, plb, gg, glb, dv


def _bd8(blocks):
    rows = []
    for g in range(S5_GT):
        pieces = [blocks[g]]
        if g:
            pieces.insert(0, jnp.zeros((S5_H, S5_P * g), blocks.dtype))
        if g < S5_GT - 1:
            pieces.append(jnp.zeros((S5_H, S5_P * (S5_GT - 1 - g)), blocks.dtype))
        rows.append(jnp.concatenate(pieces, axis=1))
    return jnp.concatenate(rows, axis=0)


def _bd8_diag(m):
    return jnp.stack([m[S5_H * g:S5_H * (g + 1), S5_P * g:S5_P * (g + 1)] for g in range(S5_GT)])


def _row8(v):
    return jnp.concatenate([v[g:g + 1] for g in range(S5_GT)], axis=1)


def _row8_split(r):
    return jnp.concatenate([r[:, S5_P * g:S5_P * (g + 1)] for g in range(S5_GT)], axis=0)


def _layer_row_tile(layer):
    return pl.BlockSpec((None, 1, LANE), lambda t: (layer, 0, t))


def _s5_fwd(z, bb_re, bb_im, lb_re, lb_im, c_re, c_im, dvec, layer):
    L = z.shape[0]

    def body(u_ref, bbr_ref, bbi_ref, lr_ref, li_ref, cr_ref, ci_ref, d_ref, y_ref, sr_ref, si_ref):
        bbr, bbi = _bd8(bbr_ref[...]).astype(MXU), _bd8(bbi_ref[...]).astype(MXU)
        cr, ci = _bd8(cr_ref[...]).astype(MXU), _bd8(ci_ref[...]).astype(MXU)
        dv = d_ref[...]
        steps, e = _tile_powers(_row8(lr_ref[...]), _row8(li_ref[...]))

        def step(c, carry):
            r0 = pl.multiple_of(c * RC, RC)
            u = u_ref[pl.ds(r0, RC), :]
            ub = u.astype(MXU)
            sr = jnp.dot(ub, bbr, preferred_element_type=F32)
            si = jnp.dot(ub, bbi, preferred_element_type=F32)
            sr, si, carry = _scan_lti(sr, si, carry, steps, e)
            sr_ref[pl.ds(r0, RC), :] = sr
            si_ref[pl.ds(r0, RC), :] = si
            y_ref[pl.ds(r0, RC), :] = dv * u + (_mm_nt(sr, cr) - _mm_nt(si, ci))
            return carry

        zero = jnp.zeros((1, S5_TW), F32)
        lax.fori_loop(0, L // RC, step, (zero, zero))

    in_tile, st, pg, plb, _, _, _ = _s5_specs(L, layer)
    u_tile = pl.BlockSpec((L, LANE), lambda t: (0, C_S5U // LANE + t))
    return pl.pallas_call(
        body, name="s5_fwd", grid=(N_S5_T,),
        in_specs=[u_tile, pg, pg, plb, plb, pg, pg, _layer_row_tile(layer)],
        out_specs=[in_tile, st, st],
        out_shape=[_S((L, S5_W)), _S((L, S5_N)), _S((L, S5_N))],
        compiler_params=_params(1))(z, bb_re, bb_im, lb_re, lb_im, c_re, c_im, dvec)


def _s5_bwd(dy0, z, s_re, s_im, bb_re, bb_im, lb_re, lb_im, c_re, c_im, dvec, layer, token=None):
    L = z.shape[0]
    extra, extra_specs = _after(token)

    def body(dy_ref, u_ref, sr_ref, si_ref, bbr_ref, bbi_ref, lr_ref, li_ref, cr_ref, ci_ref, d_ref, *rest):
        (du_ref, dbbr_out, dbbi_out, dlr_out, dli_out, dcr_out, dci_out, dd_ref,
         dbbr_ref, dbbi_ref, dcr_ref, dci_ref, dlr_ref, dli_ref) = rest[len(extra):]
        bbr, bbi = _bd8(bbr_ref[...]).astype(MXU), _bd8(bbi_ref[...]).astype(MXU)
        cr, ci = _bd8(cr_ref[...]).astype(MXU), _bd8(ci_ref[...]).astype(MXU)
        lr, li = _row8(lr_ref[...]), -_row8(li_ref[...])
        dv = d_ref[...]
        steps, e = _tile_powers(lr, li, reverse=True)
        for ref in (dbbr_ref, dbbi_ref, dlr_ref, dli_ref, dcr_ref, dci_ref, dd_ref):
            ref[...] = jnp.zeros_like(ref)
        nch = L // RC

        def step(k, carry):
            c = nch - 1 - k
            r0 = pl.multiple_of(c * RC, RC)
            dy = dy_ref[pl.ds(r0, RC), :]
            u = u_ref[pl.ds(r0, RC), :]
            dyb, ub = dy.astype(MXU), u.astype(MXU)
            sr, si = sr_ref[pl.ds(r0, RC), :], si_ref[pl.ds(r0, RC), :]
            dcr_ref[...] += _mm_tn(dyb, sr)
            dci_ref[...] -= _mm_tn(dyb, si)
            gr = jnp.dot(dyb, cr, preferred_element_type=F32)
            gi = -jnp.dot(dyb, ci, preferred_element_type=F32)
            gr, gi, carry = _scan_lti(gr, gi, carry, steps, e, reverse=True)
            pr_ = pltpu.roll(jnp.concatenate([_halo(sr_ref, c, r0), sr], axis=0), 1, 0)[8:, :]
            pi_ = pltpu.roll(jnp.concatenate([_halo(si_ref, c, r0), si], axis=0), 1, 0)[8:, :]
            dlr_ref[...] += _colsum(pr_ * gr + pi_ * gi)
            dli_ref[...] += _colsum(pr_ * gi - pi_ * gr)
            grb, gib = gr.astype(MXU), gi.astype(MXU)
            dbbr_ref[...] += _mm_tn(ub, grb)
            dbbi_ref[...] += _mm_tn(ub, gib)
            du_ref[pl.ds(r0, RC), :] = dv * dy + (_mm_nt(grb, bbr) + _mm_nt(gib, bbi))
            dd_ref[...] += _colsum(dy * u)
            return carry

        zero = jnp.zeros((1, S5_TW), F32)
        lax.fori_loop(0, nch, step, (zero, zero))
        dbbr_out[...], dbbi_out[...] = _bd8_diag(dbbr_ref[...]), _bd8_diag(dbbi_ref[...])
        dcr_out[...], dci_out[...] = _bd8_diag(dcr_ref[...]), _bd8_diag(dci_ref[...])
        dlr_out[...], dli_out[...] = _row8_split(dlr_ref[...]), _row8_split(dli_ref[...])

    in_tile, st, pg, plb, gg, glb, dv = _s5_specs(L, layer)
    u_tile = pl.BlockSpec((L, LANE), lambda t: (0, C_S5U // LANE + t))
    groups, rows = _S((N_S5_T, S5_GT, S5_H, S5_P)), _S((N_S5_T, S5_GT, S5_P))
    wide = pltpu.VMEM((LANE, S5_TW), F32)
    return pl.pallas_call(
        body, name="s5_bwd", grid=(N_S5_T,),
        in_specs=[in_tile, u_tile, st, st, pg, pg, plb, plb, pg, pg, _layer_row_tile(layer)] + extra_specs,
        out_specs=[in_tile, gg, gg, glb, glb, gg, gg, dv],
        out_shape=[_S((L, S5_W)), groups, groups, rows, rows, groups, groups, _S((1, S5_W))],
        scratch_shapes=[wide, wide, wide, wide, pltpu.VMEM((1, S5_TW), F32), pltpu.VMEM((1, S5_TW), F32)],
        compiler_params=_params(1))(dy0, z, s_re, s_im, bb_re, bb_im, lb_re, lb_im, c_re, c_im, dvec, *extra)


def _disc(ar, ai, ls):
    dt = jnp.exp(ls)
    mag = jnp.exp(ar * dt)
    lr = mag * jnp.cos(ai * dt)
    li = mag * jnp.sin(ai * dt)
    den = ar * ar + ai * ai
    cr = ((lr - 1.0) * ar + li * ai) / den
    ci = (li * ar - (lr - 1.0) * ai) / den
    return lr, li, cr, ci


def _s5_disc_fwd(ar, ai, ls, token=None):
    extra, extra_specs = _after(token)

    def body(ar_ref, ai_ref, ls_ref, *rest):
        lr_ref, li_ref, cr_ref, ci_ref = rest[len(extra):]
        lr, li, cr, ci = _disc(ar_ref[...], ai_ref[...], ls_ref[...])
        lr_ref[...], li_ref[...], cr_ref[...], ci_ref[...] = lr, li, cr, ci

    sh = _S(ar.shape)
    vm = pl.BlockSpec(memory_space=pltpu.VMEM)
    return pl.pallas_call(body, name="s5_disc_fwd", in_specs=[vm, vm, vm] + extra_specs, out_shape=[sh, sh, sh, sh])(
        ar, ai, ls, *extra)


def _s5_disc_bwd(ar, ai, ls, dlr, dli, dcr, dci):
    def body(ar_ref, ai_ref, ls_ref, dlr_ref, dli_ref, dcr_ref, dci_ref, dar_ref, dai_ref, dls_ref):
        _, vjp = jax.vjp(_disc, ar_ref[...], ai_ref[...], jnp.broadcast_to(ls_ref[...], ar_ref.shape))
        dar, dai, dls = vjp((dlr_ref[...], dli_ref[...], dcr_ref[...], dci_ref[...]))
        dar_ref[...], dai_ref[...] = dar, dai
        dls_ref[...] = jnp.sum(dls, axis=1, keepdims=True)

    return pl.pallas_call(body, name="s5_disc_bwd", out_shape=[_S(ar.shape), _S(ar.shape), _S(ls.shape)])(
        ar, ai, ls, dlr, dli, dcr, dci)


def _s5_bscale_fwd(cr, ci, br, bi):
    def body(cr_ref, ci_ref, br_ref, bi_ref, or_ref, oi_ref):
        or_ref[...], oi_ref[...] = _cmul(cr_ref[...], ci_ref[...], br_ref[...], bi_ref[...])

    return pl.pallas_call(body, name="s5_bscale_fwd", out_shape=[_S(br.shape), _S(br.shape)])(cr, ci, br, bi)


def _s5_bscale_bwd(cr, ci, br, bi, gr, gi):
    def body(cr_ref, ci_ref, br_ref, bi_ref, gr_ref, gi_ref, dbr_ref, dbi_ref, dcr_ref, dci_ref):
        cr_, ci_, br_, bi_, gr_, gi_ = (r[...] for r in (cr_ref, ci_ref, br_ref, bi_ref, gr_ref, gi_ref))
        dbr_ref[...] = cr_ * gr_ + ci_ * gi_
        dbi_ref[...] = cr_ * gi_ - ci_ * gr_
        dcr_ref[...] = jnp.sum(gr_ * br_ + gi_ * bi_, axis=1, keepdims=True)
        dci_ref[...] = jnp.sum(gi_ * br_ - gr_ * bi_, axis=1, keepdims=True)

    return pl.pallas_call(body, name="s5_bscale_bwd",
                          out_shape=[_S(br.shape), _S(br.shape), _S(cr.shape), _S(cr.shape)])(cr, ci, br, bi, gr, gi)


def _row(w):
    return pl.BlockSpec((TM, w), lambda i: (i, 0))


def _full(shape):
    return pl.BlockSpec(tuple(shape), lambda i: (0,) * len(shape))


def _p_rows(layer):
    return pl.BlockSpec((None, None, TM, PLE_D), lambda i: (layer, 0, i, 0))


def _lrow(layer, width):
    return pl.BlockSpec((None, 1, width), lambda i: (layer, 0, 0))


def _post_fwd(x, hs, z, y0, p, w_glu, b_glu, w_out, g1, b1, ple_w, w_pg, b_pg, g2, b2, layer):
    L = x.shape[0]

    def body(x_ref, hs_ref, z_ref, y0_ref, p_ref, wg_ref, bg_ref, wo_ref, g1_ref, b1_ref, pw_ref, wpg_ref, bpg_ref,
             g2_ref, b2_ref, x2_ref, xh1_ref, xh2_ref, m_ref, q_ref, gt_ref, rstd1_ref, rstd2_ref):
        rg_gate = z_ref[:, C_RGG:C_RGG + RG_W]
        s5_gate = z_ref[:, C_S5G:C_S5G + S5_W]
        rg_y = hs_ref[...] * _silu_and_grad(rg_gate)[0]
        y1 = _gelu(y0_ref[...])
        gl = _sigmoid(_mm(y1, wg_ref[...]) + bg_ref[...])
        s5_y = (y1 * gl) * _silu_and_grad(s5_gate)[0]
        m_ref[:, :RG_W] = rg_y
        m_ref[:, RG_W:] = s5_y
        mix = _mm(m_ref[...], wo_ref[...])
        t1 = ALPHA * x_ref[...] + mix
        x1, xh1, rstd1 = _ln_fwd(t1, g1_ref[...], b1_ref[...])
        q = _mm(p_ref[...], pw_ref[...])
        gt = _sigmoid(_mm(x1, wpg_ref[...]) + bpg_ref[...])
        t2 = ALPHA * x1 + q * gt
        x2, xh2, rstd2 = _ln_fwd(t2, g2_ref[...], b2_ref[...])
        x2_ref[...], xh1_ref[...], xh2_ref[...], q_ref[...], gt_ref[...] = x2, xh1, xh2, q, gt
        rstd1_ref[...], rstd2_ref[...] = rstd1, rstd2

    vec = _lrow(layer, D_MODEL)
    return pl.pallas_call(
        body, name="post_fwd", grid=(L // TM,),
        in_specs=[_row(D_MODEL), _row(RG_W), _row(Z_W), _row(S5_W), _p_rows(layer), _full((S5_W, S5_W)), _lrow(layer, S5_W),
                  _full((D_MODEL, D_MODEL)), vec, vec, _full((PLE_D, D_MODEL)), _full((D_MODEL, D_MODEL)), vec, vec, vec],
        out_specs=[_row(D_MODEL)] * 6 + [_row(1)] * 2, out_shape=[_S((L, D_MODEL))] * 6 + [_S((L, 1))] * 2,
        compiler_params=_params(1))(x, hs, z, y0, p, w_glu, b_glu, w_out, g1, b1, ple_w, w_pg, b_pg, g2, b2)


def _post_bwd_a(dx2_or_target, is_top, xh2, xh1, rstd2, rstd1, q, gt, p, w_pg, g1, b1, g2, b2, layer, token=None):
    L = xh1.shape[0]
    extra, extra_specs = _after(token)

    def body(d_ref, xh2_ref, xh1_ref, rstd2_ref, rstd1_ref, q_ref, gt_ref, p_ref, wpg_ref, g1_ref, b1_ref, g2_ref,
             b2_ref, *rest):
        (dt1_ref, dpw_out, dwpg_out, dbpg_ref, dg1_ref, db1_ref, dg2_ref, db2_ref, loss_ref, dpw_ref,
         dwpg_ref) = rest[len(extra):]
        @pl.when(pl.program_id(0) == 0)
        def _():
            for ref in (dpw_ref, dwpg_ref, dbpg_ref, dg1_ref, db1_ref, dg2_ref, db2_ref, loss_ref):
                ref[...] = jnp.zeros_like(ref)

        g1, g2 = g1_ref[...], g2_ref[...]
        xh1, xh2, rstd1, rstd2 = xh1_ref[...], xh2_ref[...], rstd1_ref[...], rstd2_ref[...]
        x1 = xh1 * g1 + b1_ref[...]
        if is_top:
            err = (xh2 * g2 + b2_ref[...]) - d_ref[...]
            loss_ref[...] += _colsum(err * err)
            dx2 = err * (1.0 / D_MODEL)
        else:
            dx2 = d_ref[...]
        p = p_ref[...]
        q, gt = q_ref[...], gt_ref[...]
        dg2_ref[...] += _colsum(dx2 * xh2)
        db2_ref[...] += _colsum(dx2)
        dt2 = _ln_bwd(dx2, xh2, rstd2, g2)
        dq = dt2 * gt
        dgpre = (dt2 * q) * gt * (1.0 - gt)
        dpw_ref[...] += _mm_tn(p, dq)
        dwpg_ref[...] += _mm_tn(x1, dgpre)
        dbpg_ref[...] += _colsum(dgpre)
        dx1 = ALPHA * dt2 + _mm_nt(dgpre, wpg_ref[...])
        dg1_ref[...] += _colsum(dx1 * xh1)
        db1_ref[...] += _colsum(dx1)
        dt1_ref[...] = _ln_bwd(dx1, xh1, rstd1, g1)

        @pl.when(pl.program_id(0) == L // TM - 1)
        def _():
            dpw_out[...] = dpw_ref[...].astype(WIRE)
            dwpg_out[...] = dwpg_ref[...].astype(WIRE)

    vec, lvec = _full((1, D_MODEL)), _lrow(layer, D_MODEL)
    return pl.pallas_call(
        body, name="post_bwd_a_top" if is_top else "post_bwd_a", grid=(L // TM,),
        in_specs=[_row(D_MODEL), _row(D_MODEL), _row(D_MODEL), _row(1), _row(1), _row(D_MODEL), _row(D_MODEL), _p_rows(layer),
                  _full((D_MODEL, D_MODEL)), lvec, lvec, lvec, lvec] + extra_specs,
        out_specs=[_row(D_MODEL), _full((PLE_D, D_MODEL)), _full((D_MODEL, D_MODEL)), vec, vec, vec, vec, vec, vec],
        out_shape=[_S((L, D_MODEL)), _S((PLE_D, D_MODEL), WIRE), _S((D_MODEL, D_MODEL), WIRE)] + [_S((1, D_MODEL))] * 6,
        scratch_shapes=[pltpu.VMEM((PLE_D, D_MODEL), F32), pltpu.VMEM((D_MODEL, D_MODEL), F32)],
        compiler_params=_params(1))(dx2_or_target, xh2, xh1, rstd2, rstd1, q, gt, p, w_pg, g1, b1, g2, b2, *extra)


def _post_bwd_b(dt1, m, z, hs, y0, w_out, w_glu, b_glu, layer):
    L = dt1.shape[0]

    def body(dt1_ref, m_ref, z_ref, hs_ref, y0_ref, wo_ref, wg_ref, bg_ref,
             dhs_ref, dy0_ref, dzg_ref, dwo_out, dwg_out, dbg_ref, dwo_ref, dwg_ref):
        @pl.when(pl.program_id(0) == 0)
        def _():
            for ref in (dwo_ref, dwg_ref, dbg_ref):
                ref[...] = jnp.zeros_like(ref)

        dt1b = dt1_ref[...].astype(MXU)
        dm = _mm_nt(dt1b, wo_ref[...])
        dwo_ref[...] += _mm_tn(m_ref[...], dt1b)
        d_rgy, d_s5y = dm[:, :RG_W], dm[:, RG_W:]
        rg_gate = z_ref[:, C_RGG:C_RGG + RG_W]
        s5_gate = z_ref[:, C_S5G:C_S5G + S5_W]
        sl, dsl = _silu_and_grad(rg_gate)
        dhs_ref[...] = d_rgy * sl
        dzg_ref[:, :RG_W] = d_rgy * hs_ref[...] * dsl
        y0 = y0_ref[...]
        y1 = _gelu(y0)
        gl = _sigmoid(_mm(y1, wg_ref[...]) + bg_ref[...])
        sl, dsl = _silu_and_grad(s5_gate)
        dy2 = d_s5y * sl
        dzg_ref[:, RG_W:] = d_s5y * (y1 * gl) * dsl
        dglpre = (dy2 * y1) * gl * (1.0 - gl)
        dwg_ref[...] += _mm_tn(y1, dglpre)
        dbg_ref[...] += _colsum(dglpre)
        dy1 = dy2 * gl + _mm_nt(dglpre, wg_ref[...])
        dy0_ref[...] = dy1 * _gelu_grad(y0)

        @pl.when(pl.program_id(0) == L // TM - 1)
        def _():
            dwo_out[...] = dwo_ref[...].astype(WIRE)
            dwg_out[...] = dwg_ref[...].astype(WIRE)

    return pl.pallas_call(
        body, name="post_bwd_b", grid=(L // TM,),
        in_specs=[_row(D_MODEL), _row(D_MODEL), _row(Z_W), _row(RG_W), _row(S5_W), _full((D_MODEL, D_MODEL)),
                  _full((S5_W, S5_W)), _lrow(layer, S5_W)],
        out_specs=[_row(RG_W), _row(S5_W), _row(D_MODEL), _full((D_MODEL, D_MODEL)), _full((S5_W, S5_W)), _full((1, S5_W))],
        out_shape=[_S((L, RG_W)), _S((L, S5_W)), _S((L, D_MODEL)), _S((D_MODEL, D_MODEL), WIRE), _S((S5_W, S5_W), WIRE),
                   _S((1, S5_W))],
        scratch_shapes=[pltpu.VMEM((D_MODEL, D_MODEL), F32), pltpu.VMEM((S5_W, S5_W), F32)],
        compiler_params=_params(1))(dt1, m, z, hs, y0, w_out, w_glu, b_glu)


def _adamw(parts, w, m, v, token=None):
    nl = len(parts)
    extra, extra_specs = _after(token)
    n, R, C = parts[0].shape
    tr = R
    for cand in (512, 256, 128, 64, 32, 16, 8):
        if R % cand == 0 and n * cand * C * 4 <= 4 * 1024 * 1024:
            tr = cand
            break
    nblk = R // tr

    def body(*refs):
        p_refs = refs[:nl]
        w_ref, m_ref, v_ref = refs[nl:nl + 3]
        g_ref, d_ref, nm_ref, nv_ref = refs[nl + 3 + len(extra):]
        layer = pl.program_id(0)
        g = None
        for li, p_ref in enumerate(p_refs):
            s = p_ref[0].astype(F32)
            for k in range(1, n):
                s = s + p_ref[k].astype(F32)
            g = s if g is None else jnp.where(layer == li, s, g)
        nm = B1 * m_ref[...] + (1.0 - B1) * g
        nv = B2 * v_ref[...] + (1.0 - B2) * (g * g)
        d_ref[...] = (-LR) * ((nm / BC1) / (jnp.sqrt(nv / BC2) + EPS) + WD * w_ref[...])
        g_ref[...], nm_ref[...], nv_ref[...] = g, nm, nv

    def part_spec(li):
        return pl.BlockSpec((n, tr, C), lambda l, i: (0, jnp.where(l == li, i, jnp.where(l < li, 0, nblk - 1)), 0))

    blk = pl.BlockSpec((tr, C), lambda l, i: (l * nblk + i, 0))
    return pl.pallas_call(
        body, name="adamw", grid=(nl, nblk),
        in_specs=[part_spec(li) for li in range(nl)] + [blk, blk, blk] + extra_specs,
        out_specs=[blk] * 4, out_shape=[_S((nl * R, C))] * 4, compiler_params=_params(2))(*parts, w, m, v, *extra)


def _adamw_natural(names, g, w, m, v, name):
    n = len(names)

    def body(*refs):
        for j in range(n):
            g_ref, w_ref, m_ref, v_ref, d_ref, nm_ref, nv_ref = (refs[k * n + j] for k in range(7))
            gj = g_ref[...]
            nm = B1 * m_ref[...] + (1.0 - B1) * gj
            nv = B2 * v_ref[...] + (1.0 - B2) * (gj * gj)
            d_ref[...] = (-LR) * ((nm / BC1) / (jnp.sqrt(nv / BC2) + EPS) + WD * w_ref[...])
            nm_ref[...], nv_ref[...] = nm, nv

    ins = [t[k] for t in (g, w, m, v) for k in names]
    outs = pl.pallas_call(body, name=name, out_shape=[_S(w[k].shape) for _ in range(3) for k in names],
                          compiler_params=pltpu.CompilerParams(vmem_limit_bytes=VMEM_LIMIT))(*ins)
    return [{k: outs[t * n + j] for j, k in enumerate(names)} for t in range(3)]


def _me():
    return lax.axis_index("x"), lax.axis_index("y"), lax.axis_index("c")


def _lin(dev):
    return 4 * dev[0] + 2 * dev[1] + dev[2]


def _blk(ref, axis, size, idx):
    nd = len(ref.shape)
    start = idx * size
    if axis == nd - 1 and size % LANE == 0:
        start = pl.multiple_of(start, LANE)
    elif axis == nd - 2 and size % 16 == 0:
        start = pl.multiple_of(start, 16)
    ix = [slice(None)] * nd
    ix[axis] = pl.ds(start, size)
    return ref.at[tuple(ix)]


def _all_gather(shards, axes, name):
    n = len(shards)
    sizes = [s.shape[a] for s, a in zip(shards, axes)]
    out_shapes = [_S(s.shape[:a] + (N_DEV * s.shape[a],) + s.shape[a + 1:], s.dtype) for s, a in zip(shards, axes)]

    def body(*refs):
        ins, outs = refs[:n], refs[n:2 * n]
        send_sems, recv_sems, local_sems = refs[2 * n:]
        x, y, c = _me()
        me, sibling = (x, y, c), (x, y, 1 - c)
        chips = [(1 - x, y), (x, 1 - y), (1 - x, 1 - y)]

        def copy(a, k, block, to, from_input=False):
            dst = _blk(outs[a], axes[a], sizes[a], _lin(block))
            return pltpu.make_async_remote_copy(
                src_ref=ins[a] if from_input else dst, dst_ref=dst, send_sem=send_sems.at[a, k],
                recv_sem=recv_sems.at[a, k], device_id=to, device_id_type=MESH)

        mine = [pltpu.make_async_copy(ins[a], _blk(outs[a], axes[a], sizes[a], _lin(me)), local_sems.at[a]) for a in range(n)]
        for cp in mine:
            cp.start()
        first = []
        for a in range(n):
            first.append(copy(a, 0, me, sibling, True))
            first += [copy(a, 1 + j, me, (*chip, c), True) for j, chip in enumerate(chips)]
        for cp in first:
            cp.start()
        passed = []
        for j, chip in enumerate(chips):
            for a in range(n):
                copy(a, 1 + j, (*chip, c), me).wait_recv()
                cp = copy(a, 4 + j, (*chip, c), sibling)
                cp.start()
                passed.append(cp)
        for a in range(n):
            copy(a, 0, sibling, me).wait_recv()
            for j, chip in enumerate(chips):
                copy(a, 4 + j, (*chip, 1 - c), me).wait_recv()
        for cp in first + passed:
            cp.wait_send()
        for cp in mine:
            cp.wait()

    return pl.pallas_call(
        body, name=name, out_shape=out_shapes, in_specs=[ANY] * n, out_specs=[ANY] * n,
        scratch_shapes=[pltpu.SemaphoreType.DMA((n, 7)), pltpu.SemaphoreType.DMA((n, 7)), pltpu.SemaphoreType.DMA((n,))],
    )(*shards)


HBM_SPEC = pl.BlockSpec(memory_space=pltpu.HBM)
SEM_SPEC = pl.BlockSpec(memory_space=pltpu.SEMAPHORE)
EFFECT = pltpu.SideEffectType.DATAFLOW_SIDE_EFFECTING


def _peers(x, y, c):
    flip = lambda v, f: 1 - v if f else v
    return [(flip(x, k & 4), flip(y, k & 2), flip(c, k & 1)) for k in range(1, N_DEV)]


def _land_shape(mode, s, axis):
    if mode == "gather":
        return s.shape[:axis] + (N_DEV * s.shape[axis],) + s.shape[axis + 1:]
    return (N_DEV,) + s.shape[:axis] + (s.shape[axis] // N_DEV,) + s.shape[axis + 1:]


def _src_view(mode, ref, axis, peer):
    return ref if mode == "gather" else _blk(ref, axis, ref.shape[axis] // N_DEV, peer)


def _dst_view(mode, land, axis, sender):
    return _blk(land, axis, land.shape[axis] // N_DEV, sender) if mode == "gather" else land.at[sender]


def _seven_blocks(mode, land, axis):
    if mode == "gather":
        ix = [slice(None)] * len(land.shape)
        ix[axis] = pl.ds(0, (N_DEV - 1) * (land.shape[axis] // N_DEV))
        return land.at[tuple(ix)]
    return land.at[pl.ds(0, N_DEV - 1)]


def _place_own(mode, srcs, axes, name, after=None):
    n = len(srcs)
    extra, extra_specs = _after(after)

    def body(me_ref, *refs):
        for a in range(n):
            out = refs[n + len(extra) + a]
            out[...] = refs[a][...].reshape(out.shape)

    def at_me(shape, axis):
        return lambda i, me: tuple(me[0] if d == axis else 0 for d in range(len(shape)))

    in_specs, out_specs = [], []
    for s, axis in zip(srcs, axes):
        if mode == "gather":
            in_specs.append(pl.BlockSpec(s.shape, lambda i, me, nd=len(s.shape): (0,) * nd))
            out_specs.append(pl.BlockSpec(s.shape, at_me(s.shape, axis)))
        else:
            blk = s.shape[:axis] + (s.shape[axis] // N_DEV,) + s.shape[axis + 1:]
            in_specs.append(pl.BlockSpec(blk, at_me(blk, axis)))
            out_specs.append(pl.BlockSpec((1,) + blk, at_me((1,) + blk, 0)))
    me = _lin(_me()).astype(jnp.int32).reshape(1)
    return pl.pallas_call(
        body, name=name, out_shape=[_S(_land_shape(mode, s, a), s.dtype) for s, a in zip(srcs, axes)],
        grid_spec=pltpu.PrefetchScalarGridSpec(num_scalar_prefetch=1, grid=(1,), in_specs=in_specs + extra_specs,
                                               out_specs=out_specs),
        compiler_params=_params(1))(me, *srcs, *extra)


def _place_shards(shards, layers, axes, dtypes, name, after=None):
    n = len(shards)
    extra, extra_specs = _after(after)

    def body(me_ref, *refs):
        for a in range(n):
            out = refs[n + len(extra) + a]
            out[...] = refs[a][...].astype(out.dtype)

    in_specs, out_specs, out_shape = [], [], []
    for s, layer, axis, dt in zip(shards, layers, axes, dtypes):
        shape = s.shape if layer is None else s.shape[1:]
        nd = len(shape)
        if layer is None:
            in_specs.append(pl.BlockSpec(shape, lambda i, me, nd=nd: (0,) * nd))
        else:
            in_specs.append(pl.BlockSpec((None,) + shape, lambda i, me, nd=nd, layer=layer: (layer,) + (0,) * nd))
        out_specs.append(pl.BlockSpec(shape, lambda i, me, nd=nd, axis=axis: tuple(me[0] if d == axis else 0 for d in range(nd))))
        out_shape.append(_S(shape[:axis] + (N_DEV * shape[axis],) + shape[axis + 1:], dt))
    me = _lin(_me()).astype(jnp.int32).reshape(1)
    return pl.pallas_call(
        body, name=name, out_shape=out_shape,
        grid_spec=pltpu.PrefetchScalarGridSpec(num_scalar_prefetch=1, grid=(1,), in_specs=in_specs + extra_specs,
                                               out_specs=out_specs),
        compiler_params=_params(1))(me, *shards, *extra)


def _push_start(mode, srcs, lands, axes, name):
    n, ns = len(lands), len(srcs)

    def body(*refs):
        src_refs, land_refs = refs[:ns], refs[ns:ns + n]
        send_sems, recv_sems = refs[ns + n], refs[ns + n + 1]
        token = refs[-1]
        x, y, c = _me()
        me = _lin((x, y, c))
        for a in range(n):
            mine = _dst_view(mode, land_refs[a], axes[a], me)
            for peer in _peers(x, y, c):
                pltpu.make_async_remote_copy(
                    src_ref=_src_view(mode, src_refs[a], axes[a], _lin(peer)) if ns else mine, dst_ref=mine,
                    send_sem=send_sems.at[a], recv_sem=recv_sems.at[a], device_id=peer, device_id_type=MESH).start()
        token[...] = jnp.zeros_like(token)

    hbm = lambda s: pltpu.HBM(s.shape, s.dtype)
    outs = pl.pallas_call(
        body, name=name,
        out_shape=(pltpu.SemaphoreType.DMA((n,)), pltpu.SemaphoreType.DMA((n,)), *[hbm(s) for s in srcs], *[hbm(s) for s in lands],
                   _S((SUB, LANE))),
        in_specs=[HBM_SPEC] * (ns + n),
        out_specs=(SEM_SPEC, SEM_SPEC, *[HBM_SPEC] * (ns + n), pl.BlockSpec(memory_space=pltpu.VMEM)),
        input_output_aliases={i: 2 + i for i in range(ns + n)},
        compiler_params=pltpu.CompilerParams(has_side_effects=EFFECT),
    )(*[pltpu.with_memory_space_constraint(s, pltpu.HBM) for s in list(srcs) + list(lands)])
    return outs[0], outs[1], outs[2:2 + ns], outs[2 + ns:2 + ns + n], outs[-1]


def _push_wait(mode, send_sems, recv_sems, srcs, lands, axes, after, name):
    n, ns = len(lands), len(srcs)
    after = list(after) if isinstance(after, (list, tuple)) else [after]

    def body(*refs):
        land_refs = refs[ns:ns + n]
        send_sems, recv_sems = refs[ns + n], refs[ns + n + 1]
        x, y, c = _me()
        for a in range(n):
            seven = _seven_blocks(mode, land_refs[a], axes[a])
            cp = pltpu.make_async_remote_copy(src_ref=seven, dst_ref=seven, send_sem=send_sems.at[a], recv_sem=recv_sems.at[a],
                                              device_id=(x, y, 1 - c), device_id_type=MESH)
            cp.wait_send()
            cp.wait_recv()

    hbm = lambda s: pltpu.HBM(s.shape, s.dtype)
    outs = pl.pallas_call(
        body, name=name, out_shape=tuple(hbm(s) for s in list(srcs) + list(lands)),
        in_specs=[HBM_SPEC] * (ns + n) + [SEM_SPEC, SEM_SPEC] + [ANY] * len(after), out_specs=tuple([HBM_SPEC] * (ns + n)),
        input_output_aliases={i: i for i in range(ns + n)},
        compiler_params=pltpu.CompilerParams(has_side_effects=EFFECT),
    )(*srcs, *lands, send_sems, recv_sems, *after)
    return outs[ns:]


def _sum_parts(parts):
    n, R, C = parts.shape

    def body(p_ref, o_ref):
        g = p_ref[0]
        for k in range(1, n):
            g = g + p_ref[k]
        o_ref[...] = g

    return pl.pallas_call(body, name="sum_parts", out_shape=_S((R, C)))(parts)


SMALL =['conv_b', 'rg_wa', 'rg_ba', 'rg_wx', 'rg_bx', 'rg_lambda', 's5_a_re', 's5_a_im', 's5_b_re', 's5_b_im',
         's5_c_re', 's5_c_im', 's5_d', 's5_log_step', 's5_b_glu', 'ln1_g', 'ln1_b', 'ple_gate_b', 'ln2_g', 'ln2_b']
WEIGHTS = ['w_in', 'conv_w', 'conv_b', 'rg_wa', 'rg_ba', 'rg_wx', 'rg_bx', 'rg_lambda', 's5_a_re', 's5_a_im', 's5_b_re',
           's5_b_im', 's5_c_re', 's5_c_im', 's5_d', 's5_log_step', 's5_w_glu', 's5_b_glu', 'w_out', 'ln1_g', 'ln1_b',
           'ple_w', 'ple_gate_w', 'ple_gate_b', 'ln2_g', 'ln2_b']
PACK_ROWS_MULT = 64


def _pack(tree, scalar):
    flat = jnp.concatenate([tree[k].reshape(-1) for k in SMALL] + [scalar.reshape(1)])
    rows = -(-flat.shape[0] // (LANE * PACK_ROWS_MULT)) * PACK_ROWS_MULT
    return jnp.pad(flat, (0, rows * LANE - flat.shape[0])).reshape(rows, LANE)


def _unpack(packed, like):
    flat, out, o = packed.reshape(-1), {}, 0
    for k in SMALL:
        n = math.prod(like[k].shape)
        out[k] = flat[o:o + n].reshape(like[k].shape)
        o += n
    return out, flat[o]


class _NoHooks:
    token = None
    first_token = None

    def first_weights(self, full, after):
        return full

    def layer_start(self, i, W, after):
        return W

    def late_weights(self, i, W, after):
        return W

    def post_done(self, i, g):
        return None

    def smalls_done(self, grads, loss):
        self.small = _small_grads(grads, self.res)
        return None

    def w_in_done(self, i, g):
        return None

    def layer_done(self, i, g, dx):
        return None


def _local_grads(x, p, target, W, disc, hooks):
    depth = 2
    saved = []
    for i in range(depth):
        if i > 0:
            W = hooks.layer_start(i, W, x)
        w = W[i]
        z = _inproj_fwd(x, w['w_in'], hooks.token if i == 0 else None)
        hs = _rg_fwd(z, w['conv_w'], w['conv_b'], w['wa_bd'], w['wx_bd'], w['rg_ba'], w['rg_bx'], w['rg_lambda'], i)
        d = disc[i]
        y0, s_re, s_im = _s5_fwd(z, d['bb_re'], d['bb_im'], d['lb_re'], d['lb_im'], d['c_re'], d['c_im'], w['s5_d'], i)
        W = hooks.late_weights(i, W, y0)
        w = W[i]
        x2, *norms = _post_fwd(x, hs, z, y0, p, w['s5_w_glu'], w['s5_b_glu'], w['w_out'], w['ln1_g'], w['ln1_b'],
                               w['ple_w'], w['ple_gate_w'], w['ple_gate_b'], w['ln2_g'], w['ln2_b'], i)
        saved.append((x, z, hs, y0, s_re, s_im, norms))
        x = x2

    grads = [None] * depth
    dx = target
    loss = None
    token = None
    for i in reversed(range(depth)):
        w, d = W[i], disc[i]
        xin, z, hs, y0, s_re, s_im, (xh1, xh2, m, q, gt, rstd1, rstd2) = saved[i]
        g = {}
        (dt1, g['ple_w'], g['ple_gate_w'], g['ple_gate_b'], g['ln1_g'], g['ln1_b'], g['ln2_g'], g['ln2_b'], lrow) = _post_bwd_a(
            dx, i == depth - 1, xh2, xh1, rstd2, rstd1, q, gt, p, w['ple_gate_w'], w['ln1_g'], w['ln1_b'],
            w['ln2_g'], w['ln2_b'], i, token)
        if i == depth - 1:
            loss = 0.5 / D_MODEL * jnp.sum(lrow)
        dhs, dy0, dzg, g['w_out'], g['s5_w_glu'], g['s5_b_glu'] = _post_bwd_b(dt1, m, z, hs, y0, w['w_out'], w['s5_w_glu'],
                                                                           w['s5_b_glu'], i)
        (dzu, g['bb_re'], g['bb_im'], g['lb_re'], g['lb_im'], g['c_re'], g['c_im'], g['s5_d']) = _s5_bwd(
            dy0, z, s_re, s_im, d['bb_re'], d['bb_im'], d['lb_re'], d['lb_im'], d['c_re'], d['c_im'], w['s5_d'], i,
            hooks.post_done(i, g))
        (dzx, g['conv_w'], g['conv_b'], g['wa_bd'], g['wx_bd'], g['rg_ba'], g['rg_bx'], g['rg_lambda']) = _rg_bwd(
            dhs, z, hs, w['conv_w'], w['conv_b'], w['wa_bd'], w['wx_bd'], w['rg_ba'], w['rg_bx'], w['rg_lambda'], i)
        if i == 0:
            g['w_in'] = _inproj_bwd_dw(xin, dzx, dzg, dzu, hooks.smalls_done([g, grads[1]], loss))
            dx = _inproj_bwd_dx(dt1, dzx, dzg, dzu, w['w_in'], hooks.w_in_done(i, g))
        else:
            dx, g['w_in'] = _inproj_bwd(dt1, xin, dzx, dzg, dzu, w['w_in'])
        grads[i] = g
        token = hooks.layer_done(i, g, dx)
    return loss, dx, grads


def _s5_layouts_fwd(s5_a_re, s5_a_im, s5_log_step, s5_b_re, s5_b_im, s5_c_re, s5_c_im, token=None):
    depth = s5_a_re.shape[0]
    ar, ai = s5_a_re.reshape(depth * 24, S5_P), s5_a_im.reshape(depth * 24, S5_P)
    ls = s5_log_step.reshape(depth * 24, 1)
    lr, li, cr, ci = _s5_disc_fwd(ar, ai, ls, token)
    per_group = lambda a: a.reshape(depth * 24, 1, S5_P)
    as_c = lambda b: jnp.swapaxes(b, 2, 3).reshape(depth * 24, S5_H, S5_P)
    res = (ar, ai, ls, per_group(cr), per_group(ci), as_c(s5_b_re), as_c(s5_b_im))
    bbr, bbi = _s5_bscale_fwd(*res[3:])
    tiles = lambda a: a.reshape(depth * N_S5_T, S5_GT, S5_H, S5_P)
    rows = lambda a: a.reshape(depth * N_S5_T, S5_GT, S5_P)
    disc = dict(bb_re=tiles(bbr), bb_im=tiles(bbi), lb_re=rows(lr), lb_im=rows(li), c_re=tiles(s5_c_re), c_im=tiles(s5_c_im))
    return [disc] * depth, res


def _s5_layouts_bwd(grads, res):
    ar, ai, ls, cr, ci, br, bi = res
    depth = len(grads)
    stack = lambda k, shape: jnp.stack([g[k] for g in grads]).reshape(shape)
    groups, shape_c = (depth * 24, S5_H, S5_P), (depth, 24, S5_H, S5_P)
    dbr, dbi, dcr, dci = _s5_bscale_bwd(cr, ci, br, bi, stack('bb_re', groups), stack('bb_im', groups))
    gp = (depth * 24, S5_P)
    dar, dai, dls = _s5_disc_bwd(ar, ai, ls, stack('lb_re', gp), stack('lb_im', gp), dcr.reshape(gp), dci.reshape(gp))
    return dict(
        s5_a_re=dar.reshape(depth, 24, S5_P), s5_a_im=dai.reshape(depth, 24, S5_P), s5_log_step=dls.reshape(depth, 24),
        s5_b_re=jnp.swapaxes(dbr.reshape(shape_c), 2, 3), s5_b_im=jnp.swapaxes(dbi.reshape(shape_c), 2, 3),
        s5_c_re=stack('c_re', shape_c), s5_c_im=stack('c_im', shape_c))


LATE = ('w_out', 'ple_w', 'ple_gate_w', 's5_w_glu')


ROWS = ('conv_b', 'rg_ba', 'rg_bx', 'rg_lambda', 's5_d', 's5_b_glu', 'ln1_g', 'ln1_b', 'ple_gate_b', 'ln2_g', 'ln2_b')


def _shared_weights(full):
    depth = full['conv_b'].shape[0]
    shared = {k: full[k].reshape(depth, 1, -1) for k in ROWS}
    shared.update(conv_w=full['conv_w'], wa_bd=full['rg_wa'], wx_bd=full['rg_wx'])
    return shared


def _layer_weights(full, shared, i):
    return dict(shared, w_in=full['w_in'][i])


class _AllLocal(_NoHooks):
    def __init__(self, full):
        self.full = full

    def late_weights(self, i, W, after):
        W[i].update({k: self.full[k][i] for k in LATE})
        return W


def _full_grads(full, x, p, target, hooks=None):
    hooks = hooks or _AllLocal(full)
    disc, res = _s5_layouts_fwd(full['s5_a_re'], full['s5_a_im'], full['s5_log_step'], full['s5_b_re'], full['s5_b_im'],
                                full['s5_c_re'], full['s5_c_im'], hooks.first_token)
    full = hooks.first_weights(full, disc[-1]['bb_im'])
    shared = _shared_weights(full)
    W = [_layer_weights(full, shared, i) for i in range(2)]
    hooks.res = res
    loss, gx, grads = _local_grads(x, p, target, W, disc, hooks)
    out = dict(hooks.small)
    for k in SHARD_AXIS:
        out[k] = [g[k] for g in grads]
    return loss, gx, out


def _small_grads(grads, res):
    stack = lambda f: jnp.stack([f(g) for g in grads])
    out = _s5_layouts_bwd(grads, res)
    out['conv_w'] = stack(lambda g: g['conv_w'])
    for k in ('conv_b', 'rg_ba', 'rg_bx', 'rg_lambda', 's5_b_glu', 'ln1_g', 'ln1_b', 'ple_gate_b', 'ln2_g', 'ln2_b'):
        out[k] = stack(lambda g: g[k][0])
    out['s5_d'] = stack(lambda g: g['s5_d'][0]).reshape(2, 24, 16)
    out['rg_wa'] = stack(lambda g: g['wa_bd'])
    out['rg_wx'] = stack(lambda g: g['wx_bd'])
    return out


SHARD_AXIS = {'w_in': 2, 'w_out': 1, 'ple_w': 2, 'ple_gate_w': 1, 's5_w_glu': 1}


def kernel(x, p, w_in, conv_w, conv_b, rg_wa, rg_ba, rg_wx, rg_bx, rg_lambda, s5_a_re, s5_a_im, s5_b_re, s5_b_im, s5_c_re, s5_c_im, s5_d, s5_log_step, s5_w_glu, s5_b_glu, w_out, ln1_g, ln1_b, ple_w, ple_gate_w, ple_gate_b, ln2_g, ln2_b, loss_target, m_w_in, m_conv_w, m_conv_b, m_rg_wa, m_rg_ba, m_rg_wx, m_rg_bx, m_rg_lambda, m_s5_a_re, m_s5_a_im, m_s5_b_re, m_s5_b_im, m_s5_c_re, m_s5_c_im, m_s5_d, m_s5_log_step, m_s5_w_glu, m_s5_b_glu, m_w_out, m_ln1_g, m_ln1_b, m_ple_w, m_ple_gate_w, m_ple_gate_b, m_ln2_g, m_ln2_b, v_w_in, v_conv_w, v_conv_b, v_rg_wa, v_rg_ba, v_rg_wx, v_rg_bx, v_rg_lambda, v_s5_a_re, v_s5_a_im, v_s5_b_re, v_s5_b_im, v_s5_c_re, v_s5_c_im, v_s5_d, v_s5_log_step, v_s5_w_glu, v_s5_b_glu, v_w_out, v_ln1_g, v_ln1_b, v_ple_w, v_ple_gate_w, v_ple_gate_b, v_ln2_g, v_ln2_b):
    local = dict(locals())
    w = {k: local[k] for k in WEIGHTS}
    mom = {k: local['m_' + k] for k in WEIGHTS}
    var = {k: local['v_' + k] for k in WEIGHTS}

    big = list(SHARD_AXIS)
    late_axes = [SHARD_AXIS[k] - 1 for k in LATE]
    pushed = {}

    def push_weights(key, names, layers, axes, after):
        shards = [w[k] if layer is not None else w[k][None] for k, layer in zip(names, layers)]
        dtypes = [WIRE if k in big else w[k].dtype for k in names]
        lands = _place_shards(shards, layers, axes, dtypes, "place_weights_" + key, after)
        pushed[key] = _push_start("gather", [], lands, axes, "push_weights_" + key)
        return pushed[key][4]

    def await_weights(key, axes, after):
        s = pushed[key]
        return _push_wait("gather", s[0], s[1], s[2], s[3], axes, after, "await_weights_" + key)

    token = push_weights("first", ['w_in', 'conv_w'], [0, None], [1, 0], None)
    token = push_weights("l0", LATE, [0] * len(LATE), late_axes, token)
    push_weights("l1", ['w_in'] + list(LATE), [1] * (1 + len(LATE)), [1] + late_axes, token)

    def push_grads(key, g, names, axes):
        srcs = [g[k] for k in names]
        pushed[key] = _push_start("scatter", srcs, _place_own("scatter", srcs, axes, "place_grads_" + key), axes,
                                  "push_grads_" + key)
        return pushed[key][4]

    def await_grads(key, axes, after):
        s = pushed[key]
        return _push_wait("scatter", s[0], s[1], s[2], s[3], axes, after, "await_grads_" + key)

    class Overlap(_NoHooks):
        token = pushed["l1"][4]
        first_token = token

        def first_weights(self, full, after):
            w_in0, conv = await_weights("first", [1, 0], after)
            return dict(full, w_in=[w_in0, None], conv_w=jnp.moveaxis(conv, 0, 2).reshape(2, 4, RG_W))

        def late_weights(self, i, W, after):
            if i == 0:
                W[0].update(zip(LATE, await_weights("l0", late_axes, after)))
            return W

        def layer_start(self, i, W, after):
            lands = await_weights("l1", [1] + late_axes, after)
            W[1].update(zip(LATE, lands[1:]), w_in=lands[0])
            return W

        def post_done(self, i, g):
            return push_grads("late0", g, LATE, late_axes) if i == 0 else None

        def smalls_done(self, grads, loss):
            super().smalls_done(grads, loss)
            conv = jnp.moveaxis(self.small['conv_w'].reshape(2, 4, N_DEV, RG_W // N_DEV), 2, 0)
            self.packed = _pack(self.small, loss)
            return push_grads("small", dict(conv_w=conv.reshape(N_DEV, 8, RG_W // N_DEV), small=self.packed),
                              ['conv_w', 'small'], [0, 0])

        def w_in_done(self, i, g):
            return push_grads("w_in0", g, ['w_in'], [0])

        def layer_done(self, i, g, dx):
            return push_grads("all1", g, ['w_in'] + list(LATE), [0] + late_axes) if i == 1 else None

    hooks = Overlap()
    _, grad_x, g = _full_grads(dict(w), x[0], p, loss_target[0], hooks)

    recv1 = dict(zip(['w_in'] + list(LATE), await_grads("all1", [0] + late_axes, grad_x)))
    recv0 = dict(zip(LATE, await_grads("late0", late_axes, grad_x)))
    outs = {}

    def update(k, parts):
        shard = w[k].shape
        c = shard[-1]
        two = lambda a: a.reshape(-1, c)
        res = _adamw([r.reshape(N_DEV, -1, c) for r in parts], two(w[k]), two(mom[k]), two(var[k]))
        outs[k] = [o.reshape(shard) for o in res]

    for k in LATE:
        update(k, [recv0[k], recv1[k]])
    done = [outs[k][1] for k in LATE]
    conv_parts, small_parts = await_grads("small", [0, 0], done)

    rows = hooks.packed.shape[0] // N_DEV
    mine = _sum_parts(small_parts.reshape(N_DEV, rows, LANE))
    gathered = _all_gather([mine], [0], "gather_small_grads")[0]
    w_in0, = await_grads("w_in0", [0], gathered)
    update('w_in', [w_in0, recv1['w_in']])
    update('conv_w', [conv_parts])
    summed, loss = _unpack(gathered, w)
    narrow = ['s5_b_re', 's5_b_im']
    for names, name in ((narrow, "adamw_s5_b"), ([k for k in SMALL if k not in narrow], "adamw_small")):
        delta, new_m, new_v = _adamw_natural(names, summed, w, mom, var, name)
        for k in names:
            outs[k] = [summed[k], delta[k], new_m[k], new_v[k]]

    res = [loss, grad_x[None]]
    for j in range(4):
        res += [outs[k][j] for k in WEIGHTS]
    return tuple(res)
```

```python
import math

import jax
import jax.numpy as jnp
from jax import lax
from jax.experimental import pallas as pl
from jax.experimental.pallas import tpu as pltpu

F32 = jnp.float32
MXU = jnp.bfloat16
WIRE = jnp.bfloat16

N_DEV = 8
D_MODEL = 1024
PLE_D = 256
RG_W = 640
S5_W = 384
S5_P = 64
S5_N = 24 * S5_P
Z_W = 2 * RG_W + 2 * S5_W
C_RGG = RG_W
C_S5U = 2 * RG_W
C_S5G = 2 * RG_W + S5_W
LANE = 128
N_RG_T = RG_W // LANE
N_S5_T = S5_W // LANE
W_BLK = Z_W // N_DEV
ALPHA = (2.0 * 2) ** 0.25
LN_EPS = 1e-5
RG_C = 8.0
LR, B1, B2, EPS, WD, STEP = 0.001, 0.9, 0.999, 1e-08, 0.01, 10
BC1 = 1.0 - B1 ** STEP
BC2 = 1.0 - B2 ** STEP
RC = 256
TM = 256
TM_MM = 1024
VMEM_LIMIT = 56 * 1024 * 1024

MESH = pl.DeviceIdType.MESH
ANY = pl.BlockSpec(memory_space=pl.ANY)


def _params(n_grid_axes, vmem=VMEM_LIMIT):
    return pltpu.CompilerParams(dimension_semantics=("arbitrary",) * n_grid_axes, vmem_limit_bytes=vmem)


def _S(shape, dtype=F32):
    return jax.ShapeDtypeStruct(tuple(shape), dtype)


def _sigmoid(x):
    return 0.5 * jnp.tanh(0.5 * x) + 0.5


def _silu_and_grad(x):
    s = _sigmoid(x)
    return x * s, s * (1.0 + x * (1.0 - s))


_GELU_C = math.sqrt(2.0 / math.pi)


def _gelu(x):
    return 0.5 * x * (1.0 + jnp.tanh(_GELU_C * (x + 0.044715 * (x * x * x))))


def _gelu_grad(x):
    th = jnp.tanh(_GELU_C * (x + 0.044715 * (x * x * x)))
    return 0.5 * (1.0 + th) + 0.5 * x * (1.0 - th * th) * (_GELU_C * (1.0 + 3.0 * 0.044715 * (x * x)))


def _mm(a, b):
    return jnp.dot(a.astype(MXU), b.astype(MXU), preferred_element_type=F32)


def _mm_nt(a, b):
    return lax.dot_general(a.astype(MXU), b.astype(MXU), (((1,), (1,)), ((), ())), preferred_element_type=F32)


def _mm_tn(a, b):
    return lax.dot_general(a.astype(MXU), b.astype(MXU), (((0,), (0,)), ((), ())), preferred_element_type=F32)


def _ln_fwd(t, g, b):
    mu = jnp.mean(t, axis=-1, keepdims=True)
    tc = t - mu
    var = jnp.mean(tc * tc, axis=-1, keepdims=True)
    rstd = lax.rsqrt(var + LN_EPS)
    xhat = tc * rstd
    return xhat * g + b, xhat, rstd


def _ln_bwd(dy, xhat, rstd, g):
    dxh = dy * g
    m1 = jnp.mean(dxh, axis=-1, keepdims=True)
    m2 = jnp.mean(dxh * xhat, axis=-1, keepdims=True)
    return rstd * (dxh - m1 - xhat * m2)


def _colsum(a):
    return jnp.sum(a, axis=0, keepdims=True)


def _up(x, d, rows, fill):
    n = x.shape[0]
    return jnp.where(rows < n - d, pltpu.roll(x, n - d, 0), fill)


SUB = 8
TILE_STEPS = (1, 2, 4)


def _r8(width):
    return lax.broadcasted_iota(jnp.int32, (SUB, width), 0)


def _scan_real(a, u, carry, reverse=False):
    r8 = _r8(a.shape[1])
    n = a.shape[0] // SUB
    outs = [None] * n
    for k in (reversed(range(n)) if reverse else range(n)):
        A, U = a[SUB * k:SUB * k + SUB], u[SUB * k:SUB * k + SUB]
        for d in TILE_STEPS:
            m = (r8 < SUB - d) if reverse else (r8 >= d)
            sh = SUB - d if reverse else d
            U = A * jnp.where(m, pltpu.roll(U, sh, 0), 0.0) + U
            A = A * jnp.where(m, pltpu.roll(A, sh, 0), 1.0)
        h = A * carry + U
        outs[k] = h
        carry = h[0:1] if reverse else h[SUB - 1:SUB]
    return jnp.concatenate(outs, axis=0), carry


def _tile_powers(lr, li, reverse=False):
    width = lr.shape[1]
    r8 = _r8(width)
    steps = []
    pr, pi = lr, li
    er, ei = jnp.broadcast_to(lr, (SUB, width)), jnp.broadcast_to(li, (SUB, width))
    for d in TILE_STEPS:
        m = (r8 < SUB - d) if reverse else (r8 >= d)
        sh = SUB - d if reverse else d
        steps.append((sh, jnp.where(m, pr, 0.0), jnp.where(m, pi, 0.0)))
        er, ei = _cmul(er, ei, jnp.where(m, pltpu.roll(er, sh, 0), 1.0), jnp.where(m, pltpu.roll(ei, sh, 0), 0.0))
        pr, pi = _cmul(pr, pi, pr, pi)
    return steps, (er, ei)


def _scan_lti(xr, xi, carry, steps, e, reverse=False):
    er, ei = e
    kr, ki = carry
    n = xr.shape[0] // SUB
    outr, outi = [None] * n, [None] * n
    for k in (reversed(range(n)) if reverse else range(n)):
        sr, si = xr[SUB * k:SUB * k + SUB], xi[SUB * k:SUB * k + SUB]
        for sh, pr, pi in steps:
            shr, shi = pltpu.roll(sr, sh, 0), pltpu.roll(si, sh, 0)
            sr, si = sr + (pr * shr - pi * shi), si + (pr * shi + pi * shr)
        sr = sr + (er * kr - ei * ki)
        si = si + (er * ki + ei * kr)
        outr[k], outi[k] = sr, si
        kr, ki = (sr[0:1], si[0:1]) if reverse else (sr[SUB - 1:SUB], si[SUB - 1:SUB])
    return jnp.concatenate(outr, axis=0), jnp.concatenate(outi, axis=0), (kr, ki)


def _halo(ref, c, r0):
    rp = pl.multiple_of(jnp.maximum(r0 - 8, 0), 8)
    return jnp.where(c > 0, ref[pl.ds(rp, 8), :], 0.0)


def _conv_taps(xe):
    return [pltpu.roll(xe, 3, 0)[8:, :], pltpu.roll(xe, 2, 0)[8:, :], pltpu.roll(xe, 1, 0)[8:, :], xe[8:, :]]


def _rg_gates(h, wa, wx, ba, bx, sp):
    r = _sigmoid(_mm(h, wa) + ba)
    i = _sigmoid(_mm(h, wx) + bx)
    log_a = (-RG_C) * r * sp
    a = jnp.exp(log_a)
    mult = jnp.sqrt(-jnp.tanh(log_a) * (a * a + 1.0))
    return r, i, a, mult


def _softplus(y):
    return jnp.maximum(y, 0.0) + jnp.log1p(jnp.exp(-jnp.abs(y)))


def _after(token):
    return ([], []) if token is None else ([token], [ANY])


def _inproj_fwd(x, w_in, token=None):
    L = x.shape[0]

    def body(x_ref, w_ref, *rest):
        rest[-1][...] = _mm(x_ref[...], w_ref[...])

    extra, extra_specs = _after(token)
    tm = min(TM_MM, L)
    return pl.pallas_call(
        body, name="inproj_fwd", grid=(L // tm,),
        in_specs=[pl.BlockSpec((tm, D_MODEL), lambda i: (i, 0)), pl.BlockSpec((D_MODEL, Z_W), lambda i: (0, 0))] + extra_specs,
        out_specs=pl.BlockSpec((tm, Z_W), lambda i: (i, 0)),
        out_shape=_S((L, Z_W)), compiler_params=_params(1))(x, w_in, *extra)


def _inproj_bwd(dt1, x, dzx, dzg, dzu, w_in):
    L = x.shape[0]

    def body(dt1_ref, x_ref, dzx_ref, dzg_ref, dzu_ref, w_ref, dx_ref, dw_ref, acc_ref):
        @pl.when(pl.program_id(0) == 0)
        def _():
            acc_ref[...] = jnp.zeros_like(acc_ref)
        dzg = dzg_ref[...]
        dz = jnp.concatenate([dzx_ref[...], dzg[:, :RG_W], dzu_ref[...], dzg[:, RG_W:]], axis=1).astype(MXU)
        xb = x_ref[...].astype(MXU)
        dx_ref[...] = ALPHA * dt1_ref[...] + _mm_nt(dz, w_ref[...])
        for j in range(N_DEV):
            acc_ref[j] += _mm_tn(xb, dz[:, j * W_BLK:(j + 1) * W_BLK])

        @pl.when(pl.program_id(0) == L // TM - 1)
        def _():
            dw_ref[...] = acc_ref[...].astype(WIRE)

    row = lambda w: pl.BlockSpec((TM, w), lambda i: (i, 0))
    wspec = pl.BlockSpec((N_DEV, D_MODEL, W_BLK), lambda i: (0, 0, 0))
    return pl.pallas_call(
        body, name="inproj_bwd", grid=(L // TM,),
        in_specs=[row(D_MODEL), row(D_MODEL), row(RG_W), row(D_MODEL), row(S5_W),
                  pl.BlockSpec((D_MODEL, Z_W), lambda i: (0, 0))],
        out_specs=[row(D_MODEL), wspec],
        out_shape=[_S((L, D_MODEL)), _S((N_DEV, D_MODEL, W_BLK), WIRE)],
        scratch_shapes=[pltpu.VMEM((N_DEV, D_MODEL, W_BLK), F32)],
        compiler_params=_params(1))(dt1, x, dzx, dzg, dzu, w_in)


TM2 = 512


def _dz_block(dzx_ref, dzg_ref, dzu_ref):
    dzg = dzg_ref[...]
    return jnp.concatenate([dzx_ref[...], dzg[:, :RG_W], dzu_ref[...], dzg[:, RG_W:]], axis=1).astype(MXU)


def _inproj_bwd_dw(x, dzx, dzg, dzu, token=None):
    L = x.shape[0]
    extra, extra_specs = _after(token)

    def body(x_ref, dzx_ref, dzg_ref, dzu_ref, *rest):
        dw_ref, acc_ref = rest[len(extra):]
        @pl.when(pl.program_id(0) == 0)
        def _():
            acc_ref[...] = jnp.zeros_like(acc_ref)
        dz = _dz_block(dzx_ref, dzg_ref, dzu_ref)
        xb = x_ref[...].astype(MXU)
        for j in range(N_DEV):
            acc_ref[j] += _mm_tn(xb, dz[:, j * W_BLK:(j + 1) * W_BLK])

        @pl.when(pl.program_id(0) == L // TM2 - 1)
        def _():
            dw_ref[...] = acc_ref[...].astype(WIRE)

    row = lambda w: pl.BlockSpec((TM2, w), lambda i: (i, 0))
    wspec = pl.BlockSpec((N_DEV, D_MODEL, W_BLK), lambda i: (0, 0, 0))
    return pl.pallas_call(
        body, name="inproj_bwd_dw", grid=(L // TM2,),
        in_specs=[row(D_MODEL), row(RG_W), row(D_MODEL), row(S5_W)] + extra_specs, out_specs=wspec,
        out_shape=_S((N_DEV, D_MODEL, W_BLK), WIRE), scratch_shapes=[pltpu.VMEM((N_DEV, D_MODEL, W_BLK), F32)],
        compiler_params=_params(1))(x, dzx, dzg, dzu, *extra)


def _inproj_bwd_dx(dt1, dzx, dzg, dzu, w_in, token=None):
    L = dt1.shape[0]
    extra, extra_specs = _after(token)

    def body(dt1_ref, dzx_ref, dzg_ref, dzu_ref, w_ref, *rest):
        rest[-1][...] = ALPHA * dt1_ref[...] + _mm_nt(_dz_block(dzx_ref, dzg_ref, dzu_ref), w_ref[...])

    tm = min(TM_MM, L)
    row = lambda w: pl.BlockSpec((tm, w), lambda i: (i, 0))
    return pl.pallas_call(
        body, name="inproj_bwd_dx", grid=(L // tm,),
        in_specs=[row(D_MODEL), row(RG_W), row(D_MODEL), row(S5_W), _full((D_MODEL, Z_W))] + extra_specs,
        out_specs=row(D_MODEL), out_shape=_S((L, D_MODEL)), compiler_params=_params(1))(dt1, dzx, dzg, dzu, w_in, *extra)


def _rg_specs(layer):
    tile = lambda rows: pl.BlockSpec((rows, LANE), lambda c: (0, c))
    ptile = lambda rows: pl.BlockSpec((None, rows, LANE), lambda c: (layer, 0, c))
    pheads = pl.BlockSpec((None, 2, RG_HD, RG_HD), lambda c: (layer, c, 0, 0))
    return tile, ptile, pheads, pl.BlockSpec((2, RG_HD, RG_HD), lambda c: (c, 0, 0))


RG_HD = 64


def _bd2(w):
    z = jnp.zeros((RG_HD, RG_HD), w.dtype)
    return jnp.concatenate([jnp.concatenate([w[0], z], axis=1), jnp.concatenate([z, w[1]], axis=1)], axis=0)


def _bd2_diag(m):
    return jnp.stack([m[:RG_HD, :RG_HD], m[RG_HD:, RG_HD:]])


def _rg_fwd(z, cw, cb, wa_bd, wx_bd, ba, bx, lam, layer):
    L = z.shape[0]

    def body(x_ref, cw_ref, cb_ref, wa_ref, wx_ref, ba_ref, bx_ref, lam_ref, hs_ref, *saved):
        w, b = cw_ref[...], cb_ref[...]
        wa, wx, ba_, bx_ = _bd2(wa_ref[...]).astype(MXU), _bd2(wx_ref[...]).astype(MXU), ba_ref[...], bx_ref[...]
        sp = _softplus(-lam_ref[...])

        def step(c, carry):
            r0 = pl.multiple_of(c * RC, RC)
            xe = jnp.concatenate([_halo(x_ref, c, r0), x_ref[pl.ds(r0, RC), :]], axis=0)
            t = _conv_taps(xe)
            h = t[0] * w[0:1] + t[1] * w[1:2] + t[2] * w[2:3] + t[3] * w[3:4] + b
            r, i, a, mult = _rg_gates(h, wa, wx, ba_, bx_, sp)
            hs, carry = _scan_real(a, mult * (i * h), carry)
            hs_ref[pl.ds(r0, RC), :] = hs
            for ref, val in zip(saved, (h, r, i, a, mult)):
                ref[pl.ds(r0, RC), :] = val
            return carry

        lax.fori_loop(0, L // RC, step, jnp.zeros((1, LANE), F32))

    tile, ptile, pheads, _ = _rg_specs(layer)
    return pl.pallas_call(
        body, name="rg_fwd", grid=(N_RG_T,),
        in_specs=[tile(L), ptile(4), ptile(1), pheads, pheads, ptile(1), ptile(1), ptile(1)],
        out_specs=[tile(L)] * 6, out_shape=[_S((L, RG_W))] * 6, compiler_params=_params(1))(
            z, cw, cb, wa_bd, wx_bd, ba, bx, lam)


def _rg_bwd(dhs, z, hs, gates, cw, wa_bd, wx_bd, lam, layer):
    L = z.shape[0]

    def body(g_ref, x_ref, hs_ref, h_ref, r_ref, i_ref, a_ref, mult_ref, cw_ref, wa_ref, wx_ref, lam_ref,
             dx_ref, dcw_ref, dcb_ref, dwa_out, dwx_out, dba_ref, dbx_ref, dlam_ref, dwa_ref, dwx_ref):
        w = cw_ref[...]
        wa, wx = _bd2(wa_ref[...]).astype(MXU), _bd2(wx_ref[...]).astype(MXU)
        lam = lam_ref[...]
        sp = _softplus(-lam)
        rows = lax.broadcasted_iota(jnp.int32, (RC, LANE), 0)
        for ref in (dcw_ref, dcb_ref, dwa_ref, dwx_ref, dba_ref, dbx_ref, dlam_ref):
            ref[...] = jnp.zeros_like(ref)
        nch = L // RC

        def step(k, carry):
            cin, nxt = carry
            c = nch - 1 - k
            r0 = pl.multiple_of(c * RC, RC)
            xe = jnp.concatenate([_halo(x_ref, c, r0), x_ref[pl.ds(r0, RC), :]], axis=0)
            t = _conv_taps(xe)
            h, r, i, a, mult = (ref[pl.ds(r0, RC), :] for ref in (h_ref, r_ref, i_ref, a_ref, mult_ref))
            hs_e = jnp.concatenate([_halo(hs_ref, c, r0), hs_ref[pl.ds(r0, RC), :]], axis=0)
            hs_prev = pltpu.roll(hs_e, 1, 0)[8:, :]
            g = g_ref[pl.ds(r0, RC), :]
            cc, cin_new = _scan_real(a, a * g, cin, reverse=True)
            dh = g + _up(cc, 1, rows, cin)
            ih = i * h
            dlog_a = dh * hs_prev * a - (dh * ih) * (a * a) / mult
            di = dh * mult * h
            dhin = dh * mult * i
            dr = dlog_a * ((-RG_C) * sp)
            dlam_ref[...] += _colsum(dlog_a * r)
            dra = dr * r * (1.0 - r)
            dia = di * i * (1.0 - i)
            dwa_ref[...] += _mm_tn(h, dra)
            dwx_ref[...] += _mm_tn(h, dia)
            dba_ref[...] += _colsum(dra)
            dbx_ref[...] += _colsum(dia)
            dhin = dhin + _mm_nt(dra, wa) + _mm_nt(dia, wx)
            de = jnp.concatenate([dhin, nxt], axis=0)
            n = RC + 8
            dx = (dhin * w[3:4] + pltpu.roll(de, n - 1, 0)[:RC, :] * w[2:3]
                  + pltpu.roll(de, n - 2, 0)[:RC, :] * w[1:2] + pltpu.roll(de, n - 3, 0)[:RC, :] * w[0:1])
            dx_ref[pl.ds(r0, RC), :] = dx
            for kk in range(4):
                dcw_ref[kk:kk + 1, :] += _colsum(dhin * t[kk])
            dcb_ref[...] += _colsum(dhin)
            return cin_new, dhin[0:8, :]

        lax.fori_loop(0, nch, step, (jnp.zeros((1, LANE), F32), jnp.zeros((8, LANE), F32)))
        dlam_ref[...] = dlam_ref[...] * (RG_C * _sigmoid(-lam))
        dwa_out[...], dwx_out[...] = _bd2_diag(dwa_ref[...]), _bd2_diag(dwx_ref[...])

    tile, ptile, pheads, gheads = _rg_specs(layer)
    heads = _S((2 * N_RG_T, RG_HD, RG_HD))
    return pl.pallas_call(
        body, name="rg_bwd", grid=(N_RG_T,),
        in_specs=[tile(L)] * 8 + [ptile(4), pheads, pheads, ptile(1)],
        out_specs=[tile(L), tile(4), tile(1), gheads, gheads, tile(1), tile(1), tile(1)],
        out_shape=[_S((L, RG_W)), _S((4, RG_W)), _S((1, RG_W)), heads, heads, _S((1, RG_W)), _S((1, RG_W)), _S((1, RG_W))],
        scratch_shapes=[pltpu.VMEM((LANE, LANE), F32), pltpu.VMEM((LANE, LANE), F32)],
        compiler_params=_params(1))(dhs, z, hs, *gates, cw, wa_bd, wx_bd, lam)


def _cmul(ar, ai, br, bi):
    return ar * br - ai * bi, ar * bi + ai * br


S5_TW = S5_N // N_S5_T


S5_H = 16
S5_GT = LANE // S5_H


def _s5_specs(L, layer):
    in_tile = pl.BlockSpec((L, LANE), lambda t: (0, t))
    st = pl.BlockSpec((L, S5_TW), lambda t: (0, t))
    pg = pl.BlockSpec((None, S5_GT, S5_H, S5_P), lambda t: (layer * N_S5_T + t, 0, 0, 0))
    plb = pl.BlockSpec((None, S5_GT, S5_P), lambda t: (layer * N_S5_T + t, 0, 0))
    gg = pl.BlockSpec((None, S5_GT, S5_H, S5_P), lambda t: (t, 0, 0, 0))
    glb = pl.BlockSpec((None, S5_GT, S5_P), lambda t: (t, 0, 0))
    dv = pl.BlockSpec((1, LANE), lambda t: (0, t))
    return in_tile, st, pg, plb, gg, glb, dv


def _bd8(blocks):
    rows = []
    for g in range(S5_GT):
        pieces = [blocks[g]]
        if g:
            pieces.insert(0, jnp.zeros((S5_H, S5_P * g), blocks.dtype))
        if g < S5_GT - 1:
            pieces.append(jnp.zeros((S5_H, S5_P * (S5_GT - 1 - g)), blocks.dtype))
        rows.append(jnp.concatenate(pieces, axis=1))
    return jnp.concatenate(rows, axis=0)


def _bd8_diag(m):
    return jnp.stack([m[S5_H * g:S5_H * (g + 1), S5_P * g:S5_P * (g + 1)] for g in range(S5_GT)])


def _row8(v):
    return jnp.concatenate([v[g:g + 1] for g in range(S5_GT)], axis=1)


def _row8_split(r):
    return jnp.concatenate([r[:, S5_P * g:S5_P * (g + 1)] for g in range(S5_GT)], axis=0)


def _layer_row_tile(layer):
    return pl.BlockSpec((None, 1, LANE), lambda t: (layer, 0, t))


def _s5_fwd(z, bb_re, bb_im, lb_re, lb_im, c_re, c_im, dvec, layer):
    L = z.shape[0]

    def body(u_ref, bbr_ref, bbi_ref, lr_ref, li_ref, cr_ref, ci_ref, d_ref, y_ref, sr_ref, si_ref):
        bbr, bbi = _bd8(bbr_ref[...]).astype(MXU), _bd8(bbi_ref[...]).astype(MXU)
        cr, ci = _bd8(cr_ref[...]).astype(MXU), _bd8(ci_ref[...]).astype(MXU)
        dv = d_ref[...]
        steps, e = _tile_powers(_row8(lr_ref[...]), _row8(li_ref[...]))

        def step(c, carry):
            r0 = pl.multiple_of(c * RC, RC)
            u = u_ref[pl.ds(r0, RC), :]
            ub = u.astype(MXU)
            sr = jnp.dot(ub, bbr, preferred_element_type=F32)
            si = jnp.dot(ub, bbi, preferred_element_type=F32)
            sr, si, carry = _scan_lti(sr, si, carry, steps, e)
            sr_ref[pl.ds(r0, RC), :] = sr
            si_ref[pl.ds(r0, RC), :] = si
            y_ref[pl.ds(r0, RC), :] = dv * u + (_mm_nt(sr, cr) - _mm_nt(si, ci))
            return carry

        zero = jnp.zeros((1, S5_TW), F32)
        lax.fori_loop(0, L // RC, step, (zero, zero))

    in_tile, st, pg, plb, _, _, _ = _s5_specs(L, layer)
    u_tile = pl.BlockSpec((L, LANE), lambda t: (0, C_S5U // LANE + t))
    return pl.pallas_call(
        body, name="s5_fwd", grid=(N_S5_T,),
        in_specs=[u_tile, pg, pg, plb, plb, pg, pg, _layer_row_tile(layer)],
        out_specs=[in_tile, st, st],
        out_shape=[_S((L, S5_W)), _S((L, S5_N)), _S((L, S5_N))],
        compiler_params=_params(1))(z, bb_re, bb_im, lb_re, lb_im, c_re, c_im, dvec)


def _s5_bwd(dy0, z, s_re, s_im, bb_re, bb_im, lb_re, lb_im, c_re, c_im, dvec, layer, token=None):
    L = z.shape[0]
    extra, extra_specs = _after(token)

    def body(dy_ref, u_ref, sr_ref, si_ref, bbr_ref, bbi_ref, lr_ref, li_ref, cr_ref, ci_ref, d_ref, *rest):
        (du_ref, dbbr_out, dbbi_out, dlr_out, dli_out, dcr_out, dci_out, dd_ref,
         dbbr_ref, dbbi_ref, dcr_ref, dci_ref, dlr_ref, dli_ref) = rest[len(extra):]
        bbr, bbi = _bd8(bbr_ref[...]).astype(MXU), _bd8(bbi_ref[...]).astype(MXU)
        cr, ci = _bd8(cr_ref[...]).astype(MXU), _bd8(ci_ref[...]).astype(MXU)
        lr, li = _row8(lr_ref[...]), -_row8(li_ref[...])
        dv = d_ref[...]
        steps, e = _tile_powers(lr, li, reverse=True)
        for ref in (dbbr_ref, dbbi_ref, dlr_ref, dli_ref, dcr_ref, dci_ref, dd_ref):
            ref[...] = jnp.zeros_like(ref)
        nch = L // RC

        def step(k, carry):
            c = nch - 1 - k
            r0 = pl.multiple_of(c * RC, RC)
            dy = dy_ref[pl.ds(r0, RC), :]
            u = u_ref[pl.ds(r0, RC), :]
            dyb, ub = dy.astype(MXU), u.astype(MXU)
            sr, si = sr_ref[pl.ds(r0, RC), :], si_ref[pl.ds(r0, RC), :]
            dcr_ref[...] += _mm_tn(dyb, sr)
            dci_ref[...] -= _mm_tn(dyb, si)
            gr = jnp.dot(dyb, cr, preferred_element_type=F32)
            gi = -jnp.dot(dyb, ci, preferred_element_type=F32)
            gr, gi, carry = _scan_lti(gr, gi, carry, steps, e, reverse=True)
            pr_ = pltpu.roll(jnp.concatenate([_halo(sr_ref, c, r0), sr], axis=0), 1, 0)[8:, :]
            pi_ = pltpu.roll(jnp.concatenate([_halo(si_ref, c, r0), si], axis=0), 1, 0)[8:, :]
            dlr_ref[...] += _colsum(pr_ * gr + pi_ * gi)
            dli_ref[...] += _colsum(pr_ * gi - pi_ * gr)
            grb, gib = gr.astype(MXU), gi.astype(MXU)
            dbbr_ref[...] += _mm_tn(ub, grb)
            dbbi_ref[...] += _mm_tn(ub, gib)
            du_ref[pl.ds(r0, RC), :] = dv * dy + (_mm_nt(grb, bbr) + _mm_nt(gib, bbi))
            dd_ref[...] += _colsum(dy * u)
            return carry

        zero = jnp.zeros((1, S5_TW), F32)
        lax.fori_loop(0, nch, step, (zero, zero))
        dbbr_out[...], dbbi_out[...] = _bd8_diag(dbbr_ref[...]), _bd8_diag(dbbi_ref[...])
        dcr_out[...], dci_out[...] = _bd8_diag(dcr_ref[...]), _bd8_diag(dci_ref[...])
        dlr_out[...], dli_out[...] = _row8_split(dlr_ref[...]), _row8_split(dli_ref[...])

    in_tile, st, pg, plb, gg, glb, dv = _s5_specs(L, layer)
    u_tile = pl.BlockSpec((L, LANE), lambda t: (0, C_S5U // LANE + t))
    groups, rows = _S((N_S5_T, S5_GT, S5_H, S5_P)), _S((N_S5_T, S5_GT, S5_P))
    wide = pltpu.VMEM((LANE, S5_TW), F32)
    return pl.pallas_call(
        body, name="s5_bwd", grid=(N_S5_T,),
        in_specs=[in_tile, u_tile, st, st, pg, pg, plb, plb, pg, pg, _layer_row_tile(layer)] + extra_specs,
        out_specs=[in_tile, gg, gg, glb, glb, gg, gg, dv],
        out_shape=[_S((L, S5_W)), groups, groups, rows, rows, groups, groups, _S((1, S5_W))],
        scratch_shapes=[wide, wide, wide, wide, pltpu.VMEM((1, S5_TW), F32), pltpu.VMEM((1, S5_TW), F32)],
        compiler_params=_params(1))(dy0, z, s_re, s_im, bb_re, bb_im, lb_re, lb_im, c_re, c_im, dvec, *extra)


def _disc(ar, ai, ls):
    dt = jnp.exp(ls)
    mag = jnp.exp(ar * dt)
    lr = mag * jnp.cos(ai * dt)
    li = mag * jnp.sin(ai * dt)
    den = ar * ar + ai * ai
    cr = ((lr - 1.0) * ar + li * ai) / den
    ci = (li * ar - (lr - 1.0) * ai) / den
    return lr, li, cr, ci


def _s5_disc_fwd(ar, ai, ls, token=None):
    extra, extra_specs = _after(token)

    def body(ar_ref, ai_ref, ls_ref, *rest):
        lr_ref, li_ref, cr_ref, ci_ref = rest[len(extra):]
        lr, li, cr, ci = _disc(ar_ref[...], ai_ref[...], ls_ref[...])
        lr_ref[...], li_ref[...], cr_ref[...], ci_ref[...] = lr, li, cr, ci

    sh = _S(ar.shape)
    vm = pl.BlockSpec(memory_space=pltpu.VMEM)
    return pl.pallas_call(body, name="s5_disc_fwd", in_specs=[vm, vm, vm] + extra_specs, out_shape=[sh, sh, sh, sh])(
        ar, ai, ls, *extra)


def _s5_disc_bwd(ar, ai, ls, dlr, dli, dcr, dci):
    def body(ar_ref, ai_ref, ls_ref, dlr_ref, dli_ref, dcr_ref, dci_ref, dar_ref, dai_ref, dls_ref):
        _, vjp = jax.vjp(_disc, ar_ref[...], ai_ref[...], jnp.broadcast_to(ls_ref[...], ar_ref.shape))
        dar, dai, dls = vjp((dlr_ref[...], dli_ref[...], dcr_ref[...], dci_ref[...]))
        dar_ref[...], dai_ref[...] = dar, dai
        dls_ref[...] = jnp.sum(dls, axis=1, keepdims=True)

    return pl.pallas_call(body, name="s5_disc_bwd", out_shape=[_S(ar.shape), _S(ar.shape), _S(ls.shape)])(
        ar, ai, ls, dlr, dli, dcr, dci)


def _s5_bscale_fwd(cr, ci, br, bi):
    def body(cr_ref, ci_ref, br_ref, bi_ref, or_ref, oi_ref):
        or_ref[...], oi_ref[...] = _cmul(cr_ref[...], ci_ref[...], br_ref[...], bi_ref[...])

    return pl.pallas_call(body, name="s5_bscale_fwd", out_shape=[_S(br.shape), _S(br.shape)])(cr, ci, br, bi)


def _s5_bscale_bwd(cr, ci, br, bi, gr, gi):
    def body(cr_ref, ci_ref, br_ref, bi_ref, gr_ref, gi_ref, dbr_ref, dbi_ref, dcr_ref, dci_ref):
        cr_, ci_, br_, bi_, gr_, gi_ = (r[...] for r in (cr_ref, ci_ref, br_ref, bi_ref, gr_ref, gi_ref))
        dbr_ref[...] = cr_ * gr_ + ci_ * gi_
        dbi_ref[...] = cr_ * gi_ - ci_ * gr_
        dcr_ref[...] = jnp.sum(gr_ * br_ + gi_ * bi_, axis=1, keepdims=True)
        dci_ref[...] = jnp.sum(gi_ * br_ - gr_ * bi_, axis=1, keepdims=True)

    return pl.pallas_call(body, name="s5_bscale_bwd",
                          out_shape=[_S(br.shape), _S(br.shape), _S(cr.shape), _S(cr.shape)])(cr, ci, br, bi, gr, gi)


def _row(w):
    return pl.BlockSpec((TM, w), lambda i: (i, 0))


def _full(shape):
    return pl.BlockSpec(tuple(shape), lambda i: (0,) * len(shape))


def _p_rows(layer):
    return pl.BlockSpec((None, None, TM, PLE_D), lambda i: (layer, 0, i, 0))


def _lrow(layer, width):
    return pl.BlockSpec((None, 1, width), lambda i: (layer, 0, 0))


def _post_fwd(x, hs, z, y0, p, w_glu, b_glu, w_out, g1, b1, ple_w, w_pg, b_pg, g2, b2, layer):
    L = x.shape[0]

    def body(x_ref, hs_ref, z_ref, y0_ref, p_ref, wg_ref, bg_ref, wo_ref, g1_ref, b1_ref, pw_ref, wpg_ref, bpg_ref,
             g2_ref, b2_ref, x2_ref, xh1_ref, xh2_ref, m_ref, q_ref, gt_ref, rstd1_ref, rstd2_ref):
        rg_gate = z_ref[:, C_RGG:C_RGG + RG_W]
        s5_gate = z_ref[:, C_S5G:C_S5G + S5_W]
        rg_y = hs_ref[...] * _silu_and_grad(rg_gate)[0]
        y1 = _gelu(y0_ref[...])
        gl = _sigmoid(_mm(y1, wg_ref[...]) + bg_ref[...])
        s5_y = (y1 * gl) * _silu_and_grad(s5_gate)[0]
        m_ref[:, :RG_W] = rg_y
        m_ref[:, RG_W:] = s5_y
        mix = _mm(m_ref[...], wo_ref[...])
        t1 = ALPHA * x_ref[...] + mix
        x1, xh1, rstd1 = _ln_fwd(t1, g1_ref[...], b1_ref[...])
        q = _mm(p_ref[...], pw_ref[...])
        gt = _sigmoid(_mm(x1, wpg_ref[...]) + bpg_ref[...])
        t2 = ALPHA * x1 + q * gt
        x2, xh2, rstd2 = _ln_fwd(t2, g2_ref[...], b2_ref[...])
        x2_ref[...], xh1_ref[...], xh2_ref[...], q_ref[...], gt_ref[...] = x2, xh1, xh2, q, gt
        rstd1_ref[...], rstd2_ref[...] = rstd1, rstd2

    vec = _lrow(layer, D_MODEL)
    return pl.pallas_call(
        body, name="post_fwd", grid=(L // TM,),
        in_specs=[_row(D_MODEL), _row(RG_W), _row(Z_W), _row(S5_W), _p_rows(layer), _full((S5_W, S5_W)), _lrow(layer, S5_W),
                  _full((D_MODEL, D_MODEL)), vec, vec, _full((PLE_D, D_MODEL)), _full((D_MODEL, D_MODEL)), vec, vec, vec],
        out_specs=[_row(D_MODEL)] * 6 + [_row(1)] * 2, out_shape=[_S((L, D_MODEL))] * 6 + [_S((L, 1))] * 2,
        compiler_params=_params(1))(x, hs, z, y0, p, w_glu, b_glu, w_out, g1, b1, ple_w, w_pg, b_pg, g2, b2)


def _post_bwd_a(dx2_or_target, is_top, xh2, xh1, rstd2, rstd1, q, gt, p, w_pg, g1, b1, g2, b2, layer, token=None):
    L = xh1.shape[0]
    extra, extra_specs = _after(token)

    def body(d_ref, xh2_ref, xh1_ref, rstd2_ref, rstd1_ref, q_ref, gt_ref, p_ref, wpg_ref, g1_ref, b1_ref, g2_ref,
             b2_ref, *rest):
        (dt1_ref, dpw_out, dwpg_out, dbpg_ref, dg1_ref, db1_ref, dg2_ref, db2_ref, loss_ref, dpw_ref,
         dwpg_ref) = rest[len(extra):]
        @pl.when(pl.program_id(0) == 0)
        def _():
            for ref in (dpw_ref, dwpg_ref, dbpg_ref, dg1_ref, db1_ref, dg2_ref, db2_ref, loss_ref):
                ref[...] = jnp.zeros_like(ref)

        g1, g2 = g1_ref[...], g2_ref[...]
        xh1, xh2, rstd1, rstd2 = xh1_ref[...], xh2_ref[...], rstd1_ref[...], rstd2_ref[...]
        x1 = xh1 * g1 + b1_ref[...]
        if is_top:
            err = (xh2 * g2 + b2_ref[...]) - d_ref[...]
            loss_ref[...] += _colsum(err * err)
            dx2 = err * (1.0 / D_MODEL)
        else:
            dx2 = d_ref[...]
        p = p_ref[...]
        q, gt = q_ref[...], gt_ref[...]
        dg2_ref[...] += _colsum(dx2 * xh2)
        db2_ref[...] += _colsum(dx2)
        dt2 = _ln_bwd(dx2, xh2, rstd2, g2)
        dq = dt2 * gt
        dgpre = (dt2 * q) * gt * (1.0 - gt)
        dpw_ref[...] += _mm_tn(p, dq)
        dwpg_ref[...] += _mm_tn(x1, dgpre)
        dbpg_ref[...] += _colsum(dgpre)
        dx1 = ALPHA * dt2 + _mm_nt(dgpre, wpg_ref[...])
        dg1_ref[...] += _colsum(dx1 * xh1)
        db1_ref[...] += _colsum(dx1)
        dt1_ref[...] = _ln_bwd(dx1, xh1, rstd1, g1)

        @pl.when(pl.program_id(0) == L // TM - 1)
        def _():
            dpw_out[...] = dpw_ref[...].astype(WIRE)
            dwpg_out[...] = dwpg_ref[...].astype(WIRE)

    vec, lvec = _full((1, D_MODEL)), _lrow(layer, D_MODEL)
    return pl.pallas_call(
        body, name="post_bwd_a_top" if is_top else "post_bwd_a", grid=(L // TM,),
        in_specs=[_row(D_MODEL), _row(D_MODEL), _row(D_MODEL), _row(1), _row(1), _row(D_MODEL), _row(D_MODEL), _p_rows(layer),
                  _full((D_MODEL, D_MODEL)), lvec, lvec, lvec, lvec] + extra_specs,
        out_specs=[_row(D_MODEL), _full((PLE_D, D_MODEL)), _full((D_MODEL, D_MODEL)), vec, vec, vec, vec, vec, vec],
        out_shape=[_S((L, D_MODEL)), _S((PLE_D, D_MODEL), WIRE), _S((D_MODEL, D_MODEL), WIRE)] + [_S((1, D_MODEL))] * 6,
        scratch_shapes=[pltpu.VMEM((PLE_D, D_MODEL), F32), pltpu.VMEM((D_MODEL, D_MODEL), F32)],
        compiler_params=_params(1))(dx2_or_target, xh2, xh1, rstd2, rstd1, q, gt, p, w_pg, g1, b1, g2, b2, *extra)


def _post_bwd_b(dt1, m, z, hs, y0, w_out, w_glu, b_glu, layer):
    L = dt1.shape[0]

    def body(dt1_ref, m_ref, z_ref, hs_ref, y0_ref, wo_ref, wg_ref, bg_ref,
             dhs_ref, dy0_ref, dzg_ref, dwo_out, dwg_out, dbg_ref, dwo_ref, dwg_ref):
        @pl.when(pl.program_id(0) == 0)
        def _():
            for ref in (dwo_ref, dwg_ref, dbg_ref):
                ref[...] = jnp.zeros_like(ref)

        dt1b = dt1_ref[...].astype(MXU)
        dm = _mm_nt(dt1b, wo_ref[...])
        dwo_ref[...] += _mm_tn(m_ref[...], dt1b)
        d_rgy, d_s5y = dm[:, :RG_W], dm[:, RG_W:]
        rg_gate = z_ref[:, C_RGG:C_RGG + RG_W]
        s5_gate = z_ref[:, C_S5G:C_S5G + S5_W]
        sl, dsl = _silu_and_grad(rg_gate)
        dhs_ref[...] = d_rgy * sl
        dzg_ref[:, :RG_W] = d_rgy * hs_ref[...] * dsl
        y0 = y0_ref[...]
        y1 = _gelu(y0)
        gl = _sigmoid(_mm(y1, wg_ref[...]) + bg_ref[...])
        sl, dsl = _silu_and_grad(s5_gate)
        dy2 = d_s5y * sl
        dzg_ref[:, RG_W:] = d_s5y * (y1 * gl) * dsl
        dglpre = (dy2 * y1) * gl * (1.0 - gl)
        dwg_ref[...] += _mm_tn(y1, dglpre)
        dbg_ref[...] += _colsum(dglpre)
        dy1 = dy2 * gl + _mm_nt(dglpre, wg_ref[...])
        dy0_ref[...] = dy1 * _gelu_grad(y0)

        @pl.when(pl.program_id(0) == L // TM - 1)
        def _():
            dwo_out[...] = dwo_ref[...].astype(WIRE)
            dwg_out[...] = dwg_ref[...].astype(WIRE)

    return pl.pallas_call(
        body, name="post_bwd_b", grid=(L // TM,),
        in_specs=[_row(D_MODEL), _row(D_MODEL), _row(Z_W), _row(RG_W), _row(S5_W), _full((D_MODEL, D_MODEL)),
                  _full((S5_W, S5_W)), _lrow(layer, S5_W)],
        out_specs=[_row(RG_W), _row(S5_W), _row(D_MODEL), _full((D_MODEL, D_MODEL)), _full((S5_W, S5_W)), _full((1, S5_W))],
        out_shape=[_S((L, RG_W)), _S((L, S5_W)), _S((L, D_MODEL)), _S((D_MODEL, D_MODEL), WIRE), _S((S5_W, S5_W), WIRE),
                   _S((1, S5_W))],
        scratch_shapes=[pltpu.VMEM((D_MODEL, D_MODEL), F32), pltpu.VMEM((S5_W, S5_W), F32)],
        compiler_params=_params(1))(dt1, m, z, hs, y0, w_out, w_glu, b_glu)


def _adamw(parts, w, m, v, token=None):
    nl = len(parts)
    extra, extra_specs = _after(token)
    n, R, C = parts[0].shape
    tr = R
    for cand in (512, 256, 128, 64, 32, 16, 8):
        if R % cand == 0 and n * cand * C * 4 <= 4 * 1024 * 1024:
            tr = cand
            break
    nblk = R // tr

    def body(*refs):
        p_refs = refs[:nl]
        w_ref, m_ref, v_ref = refs[nl:nl + 3]
        g_ref, d_ref, nm_ref, nv_ref = refs[nl + 3 + len(extra):]
        layer = pl.program_id(0)
        g = None
        for li, p_ref in enumerate(p_refs):
            s = p_ref[0].astype(F32)
            for k in range(1, n):
                s = s + p_ref[k].astype(F32)
            g = s if g is None else jnp.where(layer == li, s, g)
        nm = B1 * m_ref[...] + (1.0 - B1) * g
        nv = B2 * v_ref[...] + (1.0 - B2) * (g * g)
        d_ref[...] = (-LR) * ((nm / BC1) / (jnp.sqrt(nv / BC2) + EPS) + WD * w_ref[...])
        g_ref[...], nm_ref[...], nv_ref[...] = g, nm, nv

    def part_spec(li):
        return pl.BlockSpec((n, tr, C), lambda l, i: (0, jnp.where(l == li, i, jnp.where(l < li, 0, nblk - 1)), 0))

    blk = pl.BlockSpec((tr, C), lambda l, i: (l * nblk + i, 0))
    return pl.pallas_call(
        body, name="adamw", grid=(nl, nblk),
        in_specs=[part_spec(li) for li in range(nl)] + [blk, blk, blk] + extra_specs,
        out_specs=[blk] * 4, out_shape=[_S((nl * R, C))] * 4, compiler_params=_params(2))(*parts, w, m, v, *extra)


def _adamw_natural(names, g, w, m, v, name):
    n = len(names)

    def body(*refs):
        for j in range(n):
            g_ref, w_ref, m_ref, v_ref, d_ref, nm_ref, nv_ref = (refs[k * n + j] for k in range(7))
            gj = g_ref[...]
            nm = B1 * m_ref[...] + (1.0 - B1) * gj
            nv = B2 * v_ref[...] + (1.0 - B2) * (gj * gj)
            d_ref[...] = (-LR) * ((nm / BC1) / (jnp.sqrt(nv / BC2) + EPS) + WD * w_ref[...])
            nm_ref[...], nv_ref[...] = nm, nv

    ins = [t[k] for t in (g, w, m, v) for k in names]
    outs = pl.pallas_call(body, name=name, out_shape=[_S(w[k].shape) for _ in range(3) for k in names],
                          compiler_params=pltpu.CompilerParams(vmem_limit_bytes=VMEM_LIMIT))(*ins)
    return [{k: outs[t * n + j] for j, k in enumerate(names)} for t in range(3)]


def _me():
    return lax.axis_index("x"), lax.axis_index("y"), lax.axis_index("c")


def _lin(dev):
    return 4 * dev[0] + 2 * dev[1] + dev[2]


def _blk(ref, axis, size, idx):
    nd = len(ref.shape)
    start = idx * size
    if axis == nd - 1 and size % LANE == 0:
        start = pl.multiple_of(start, LANE)
    elif axis == nd - 2 and size % 16 == 0:
        start = pl.multiple_of(start, 16)
    ix = [slice(None)] * nd
    ix[axis] = pl.ds(start, size)
    return ref.at[tuple(ix)]


def _all_gather(shards, axes, name):
    n = len(shards)
    sizes = [s.shape[a] for s, a in zip(shards, axes)]
    out_shapes = [_S(s.shape[:a] + (N_DEV * s.shape[a],) + s.shape[a + 1:], s.dtype) for s, a in zip(shards, axes)]

    def body(*refs):
        ins, outs = refs[:n], refs[n:2 * n]
        send_sems, recv_sems, local_sems = refs[2 * n:]
        x, y, c = _me()
        me, sibling = (x, y, c), (x, y, 1 - c)
        chips = [(1 - x, y), (x, 1 - y), (1 - x, 1 - y)]

        def copy(a, k, block, to, from_input=False):
            dst = _blk(outs[a], axes[a], sizes[a], _lin(block))
            return pltpu.make_async_remote_copy(
                src_ref=ins[a] if from_input else dst, dst_ref=dst, send_sem=send_sems.at[a, k],
                recv_sem=recv_sems.at[a, k], device_id=to, device_id_type=MESH)

        mine = [pltpu.make_async_copy(ins[a], _blk(outs[a], axes[a], sizes[a], _lin(me)), local_sems.at[a]) for a in range(n)]
        for cp in mine:
            cp.start()
        first = []
        for a in range(n):
            first.append(copy(a, 0, me, sibling, True))
            first += [copy(a, 1 + j, me, (*chip, c), True) for j, chip in enumerate(chips)]
        for cp in first:
            cp.start()
        passed = []
        for j, chip in enumerate(chips):
            for a in range(n):
                copy(a, 1 + j, (*chip, c), me).wait_recv()
                cp = copy(a, 4 + j, (*chip, c), sibling)
                cp.start()
                passed.append(cp)
        for a in range(n):
            copy(a, 0, sibling, me).wait_recv()
            for j, chip in enumerate(chips):
                copy(a, 4 + j, (*chip, 1 - c), me).wait_recv()
        for cp in first + passed:
            cp.wait_send()
        for cp in mine:
            cp.wait()

    return pl.pallas_call(
        body, name=name, out_shape=out_shapes, in_specs=[ANY] * n, out_specs=[ANY] * n,
        scratch_shapes=[pltpu.SemaphoreType.DMA((n, 7)), pltpu.SemaphoreType.DMA((n, 7)), pltpu.SemaphoreType.DMA((n,))],
    )(*shards)


HBM_SPEC = pl.BlockSpec(memory_space=pltpu.HBM)
SEM_SPEC = pl.BlockSpec(memory_space=pltpu.SEMAPHORE)
EFFECT = pltpu.SideEffectType.DATAFLOW_SIDE_EFFECTING


def _peers(x, y, c):
    flip = lambda v, f: 1 - v if f else v
    return [(flip(x, k & 4), flip(y, k & 2), flip(c, k & 1)) for k in range(1, N_DEV)]


def _land_shape(mode, s, axis):
    if mode == "gather":
        return s.shape[:axis] + (N_DEV * s.shape[axis],) + s.shape[axis + 1:]
    return (N_DEV,) + s.shape[:axis] + (s.shape[axis] // N_DEV,) + s.shape[axis + 1:]


def _src_view(mode, ref, axis, peer):
    return ref if mode == "gather" else _blk(ref, axis, ref.shape[axis] // N_DEV, peer)


def _dst_view(mode, land, axis, sender):
    return _blk(land, axis, land.shape[axis] // N_DEV, sender) if mode == "gather" else land.at[sender]


def _seven_blocks(mode, land, axis):
    if mode == "gather":
        ix = [slice(None)] * len(land.shape)
        ix[axis] = pl.ds(0, (N_DEV - 1) * (land.shape[axis] // N_DEV))
        return land.at[tuple(ix)]
    return land.at[pl.ds(0, N_DEV - 1)]


def _place_own(mode, srcs, axes, name, after=None):
    n = len(srcs)
    extra, extra_specs = _after(after)

    def body(me_ref, *refs):
        for a in range(n):
            out = refs[n + len(extra) + a]
            out[...] = refs[a][...].reshape(out.shape)

    def at_me(shape, axis):
        return lambda i, me: tuple(me[0] if d == axis else 0 for d in range(len(shape)))

    in_specs, out_specs = [], []
    for s, axis in zip(srcs, axes):
        if mode == "gather":
            in_specs.append(pl.BlockSpec(s.shape, lambda i, me, nd=len(s.shape): (0,) * nd))
            out_specs.append(pl.BlockSpec(s.shape, at_me(s.shape, axis)))
        else:
            blk = s.shape[:axis] + (s.shape[axis] // N_DEV,) + s.shape[axis + 1:]
            in_specs.append(pl.BlockSpec(blk, at_me(blk, axis)))
            out_specs.append(pl.BlockSpec((1,) + blk, at_me((1,) + blk, 0)))
    me = _lin(_me()).astype(jnp.int32).reshape(1)
    return pl.pallas_call(
        body, name=name, out_shape=[_S(_land_shape(mode, s, a), s.dtype) for s, a in zip(srcs, axes)],
        grid_spec=pltpu.PrefetchScalarGridSpec(num_scalar_prefetch=1, grid=(1,), in_specs=in_specs + extra_specs,
                                               out_specs=out_specs),
        compiler_params=_params(1))(me, *srcs, *extra)


def _place_shards(shards, layers, axes, dtypes, name, after=None):
    n = len(shards)
    extra, extra_specs = _after(after)

    def body(me_ref, *refs):
        for a in range(n):
            out = refs[n + len(extra) + a]
            out[...] = refs[a][...].astype(out.dtype)

    in_specs, out_specs, out_shape = [], [], []
    for s, layer, axis, dt in zip(shards, layers, axes, dtypes):
        shape = s.shape if layer is None else s.shape[1:]
        nd = len(shape)
        if layer is None:
            in_specs.append(pl.BlockSpec(shape, lambda i, me, nd=nd: (0,) * nd))
        else:
            in_specs.append(pl.BlockSpec((None,) + shape, lambda i, me, nd=nd, layer=layer: (layer,) + (0,) * nd))
        out_specs.append(pl.BlockSpec(shape, lambda i, me, nd=nd, axis=axis: tuple(me[0] if d == axis else 0 for d in range(nd))))
        out_shape.append(_S(shape[:axis] + (N_DEV * shape[axis],) + shape[axis + 1:], dt))
    me = _lin(_me()).astype(jnp.int32).reshape(1)
    return pl.pallas_call(
        body, name=name, out_shape=out_shape,
        grid_spec=pltpu.PrefetchScalarGridSpec(num_scalar_prefetch=1, grid=(1,), in_specs=in_specs + extra_specs,
                                               out_specs=out_specs),
        compiler_params=_params(1))(me, *shards, *extra)


def _push_start(mode, srcs, lands, axes, name):
    n, ns = len(lands), len(srcs)

    def body(*refs):
        src_refs, land_refs = refs[:ns], refs[ns:ns + n]
        send_sems, recv_sems = refs[ns + n], refs[ns + n + 1]
        token = refs[-1]
        x, y, c = _me()
        me = _lin((x, y, c))
        for a in range(n):
            mine = _dst_view(mode, land_refs[a], axes[a], me)
            for peer in _peers(x, y, c):
                pltpu.make_async_remote_copy(
                    src_ref=_src_view(mode, src_refs[a], axes[a], _lin(peer)) if ns else mine, dst_ref=mine,
                    send_sem=send_sems.at[a], recv_sem=recv_sems.at[a], device_id=peer, device_id_type=MESH).start()
        token[...] = jnp.zeros_like(token)

    hbm = lambda s: pltpu.HBM(s.shape, s.dtype)
    outs = pl.pallas_call(
        body, name=name,
        out_shape=(pltpu.SemaphoreType.DMA((n,)), pltpu.SemaphoreType.DMA((n,)), *[hbm(s) for s in srcs], *[hbm(s) for s in lands],
                   _S((SUB, LANE))),
        in_specs=[HBM_SPEC] * (ns + n),
        out_specs=(SEM_SPEC, SEM_SPEC, *[HBM_SPEC] * (ns + n), pl.BlockSpec(memory_space=pltpu.VMEM)),
        input_output_aliases={i: 2 + i for i in range(ns + n)},
        compiler_params=pltpu.CompilerParams(has_side_effects=EFFECT),
    )(*[pltpu.with_memory_space_constraint(s, pltpu.HBM) for s in list(srcs) + list(lands)])
    return outs[0], outs[1], outs[2:2 + ns], outs[2 + ns:2 + ns + n], outs[-1]


def _push_wait(mode, send_sems, recv_sems, srcs, lands, axes, after, name):
    n, ns = len(lands), len(srcs)
    after = list(after) if isinstance(after, (list, tuple)) else [after]

    def body(*refs):
        land_refs = refs[ns:ns + n]
        send_sems, recv_sems = refs[ns + n], refs[ns + n + 1]
        x, y, c = _me()
        for a in range(n):
            seven = _seven_blocks(mode, land_refs[a], axes[a])
            cp = pltpu.make_async_remote_copy(src_ref=seven, dst_ref=seven, send_sem=send_sems.at[a], recv_sem=recv_sems.at[a],
                                              device_id=(x, y, 1 - c), device_id_type=MESH)
            cp.wait_send()
            cp.wait_recv()

    hbm = lambda s: pltpu.HBM(s.shape, s.dtype)
    outs = pl.pallas_call(
        body, name=name, out_shape=tuple(hbm(s) for s in list(srcs) + list(lands)),
        in_specs=[HBM_SPEC] * (ns + n) + [SEM_SPEC, SEM_SPEC] + [ANY] * len(after), out_specs=tuple([HBM_SPEC] * (ns + n)),
        input_output_aliases={i: i for i in range(ns + n)},
        compiler_params=pltpu.CompilerParams(has_side_effects=EFFECT),
    )(*srcs, *lands, send_sems, recv_sems, *after)
    return outs[ns:]


def _sum_parts(parts):
    n, R, C = parts.shape

    def body(p_ref, o_ref):
        g = p_ref[0]
        for k in range(1, n):
            g = g + p_ref[k]
        o_ref[...] = g

    return pl.pallas_call(body, name="sum_parts", out_shape=_S((R, C)))(parts)


SMALL =['conv_b', 'rg_wa', 'rg_ba', 'rg_wx', 'rg_bx', 'rg_lambda', 's5_a_re', 's5_a_im', 's5_b_re', 's5_b_im',
         's5_c_re', 's5_c_im', 's5_d', 's5_log_step', 's5_b_glu', 'ln1_g', 'ln1_b', 'ple_gate_b', 'ln2_g', 'ln2_b']
WEIGHTS = ['w_in', 'conv_w', 'conv_b', 'rg_wa', 'rg_ba', 'rg_wx', 'rg_bx', 'rg_lambda', 's5_a_re', 's5_a_im', 's5_b_re',
           's5_b_im', 's5_c_re', 's5_c_im', 's5_d', 's5_log_step', 's5_w_glu', 's5_b_glu', 'w_out', 'ln1_g', 'ln1_b',
           'ple_w', 'ple_gate_w', 'ple_gate_b', 'ln2_g', 'ln2_b']
PACK_ROWS_MULT = 64


def _pack(tree, scalar):
    flat = jnp.concatenate([tree[k].reshape(-1) for k in SMALL] + [scalar.reshape(1)])
    rows = -(-flat.shape[0] // (LANE * PACK_ROWS_MULT)) * PACK_ROWS_MULT
    return jnp.pad(flat, (0, rows * LANE - flat.shape[0])).reshape(rows, LANE)


def _unpack(packed, like):
    flat, out, o = packed.reshape(-1), {}, 0
    for k in SMALL:
        n = math.prod(like[k].shape)
        out[k] = flat[o:o + n].reshape(like[k].shape)
        o += n
    return out, flat[o]


class _NoHooks:
    token = None
    first_token = None

    def first_weights(self, full, after):
        return full

    def layer_start(self, i, W, after):
        return W

    def late_weights(self, i, W, after):
        return W

    def post_done(self, i, g):
        return None

    def smalls_done(self, grads, loss):
        self.small = _small_grads(grads, self.res)
        return None

    def w_in_done(self, i, g):
        return None

    def layer_done(self, i, g, dx):
        return None


def _local_grads(x, p, target, W, disc, hooks):
    depth = 2
    saved = []
    for i in range(depth):
        if i > 0:
            W = hooks.layer_start(i, W, x)
        w = W[i]
        z = _inproj_fwd(x, w['w_in'], hooks.token if i == 0 else None)
        hs, *gates = _rg_fwd(z, w['conv_w'], w['conv_b'], w['wa_bd'], w['wx_bd'], w['rg_ba'], w['rg_bx'], w['rg_lambda'], i)
        d = disc[i]
        y0, s_re, s_im = _s5_fwd(z, d['bb_re'], d['bb_im'], d['lb_re'], d['lb_im'], d['c_re'], d['c_im'], w['s5_d'], i)
        W = hooks.late_weights(i, W, y0)
        w = W[i]
        x2, *norms = _post_fwd(x, hs, z, y0, p, w['s5_w_glu'], w['s5_b_glu'], w['w_out'], w['ln1_g'], w['ln1_b'],
                               w['ple_w'], w['ple_gate_w'], w['ple_gate_b'], w['ln2_g'], w['ln2_b'], i)
        saved.append((x, z, hs, gates, y0, s_re, s_im, norms))
        x = x2

    grads = [None] * depth
    dx = target
    loss = None
    token = None
    for i in reversed(range(depth)):
        w, d = W[i], disc[i]
        xin, z, hs, gates, y0, s_re, s_im, (xh1, xh2, m, q, gt, rstd1, rstd2) = saved[i]
        g = {}
        (dt1, g['ple_w'], g['ple_gate_w'], g['ple_gate_b'], g['ln1_g'], g['ln1_b'], g['ln2_g'], g['ln2_b'], lrow) = _post_bwd_a(
            dx, i == depth - 1, xh2, xh1, rstd2, rstd1, q, gt, p, w['ple_gate_w'], w['ln1_g'], w['ln1_b'],
            w['ln2_g'], w['ln2_b'], i, token)
        if i == depth - 1:
            loss = 0.5 / D_MODEL * jnp.sum(lrow)
        dhs, dy0, dzg, g['w_out'], g['s5_w_glu'], g['s5_b_glu'] = _post_bwd_b(dt1, m, z, hs, y0, w['w_out'], w['s5_w_glu'],
                                                                           w['s5_b_glu'], i)
        (dzu, g['bb_re'], g['bb_im'], g['lb_re'], g['lb_im'], g['c_re'], g['c_im'], g['s5_d']) = _s5_bwd(
            dy0, z, s_re, s_im, d['bb_re'], d['bb_im'], d['lb_re'], d['lb_im'], d['c_re'], d['c_im'], w['s5_d'], i,
            hooks.post_done(i, g))
        (dzx, g['conv_w'], g['conv_b'], g['wa_bd'], g['wx_bd'], g['rg_ba'], g['rg_bx'], g['rg_lambda']) = _rg_bwd(
            dhs, z, hs, gates, w['conv_w'], w['wa_bd'], w['wx_bd'], w['rg_lambda'], i)
        if i == 0:
            g['w_in'] = _inproj_bwd_dw(xin, dzx, dzg, dzu, hooks.smalls_done([g, grads[1]], loss))
            dx = _inproj_bwd_dx(dt1, dzx, dzg, dzu, w['w_in'], hooks.w_in_done(i, g))
        else:
            dx, g['w_in'] = _inproj_bwd(dt1, xin, dzx, dzg, dzu, w['w_in'])
        grads[i] = g
        token = hooks.layer_done(i, g, dx)
    return loss, dx, grads


def _s5_layouts_fwd(s5_a_re, s5_a_im, s5_log_step, s5_b_re, s5_b_im, s5_c_re, s5_c_im, token=None):
    depth = s5_a_re.shape[0]
    ar, ai = s5_a_re.reshape(depth * 24, S5_P), s5_a_im.reshape(depth * 24, S5_P)
    ls = s5_log_step.reshape(depth * 24, 1)
    lr, li, cr, ci = _s5_disc_fwd(ar, ai, ls, token)
    per_group = lambda a: a.reshape(depth * 24, 1, S5_P)
    as_c = lambda b: jnp.swapaxes(b, 2, 3).reshape(depth * 24, S5_H, S5_P)
    res = (ar, ai, ls, per_group(cr), per_group(ci), as_c(s5_b_re), as_c(s5_b_im))
    bbr, bbi = _s5_bscale_fwd(*res[3:])
    tiles = lambda a: a.reshape(depth * N_S5_T, S5_GT, S5_H, S5_P)
    rows = lambda a: a.reshape(depth * N_S5_T, S5_GT, S5_P)
    disc = dict(bb_re=tiles(bbr), bb_im=tiles(bbi), lb_re=rows(lr), lb_im=rows(li), c_re=tiles(s5_c_re), c_im=tiles(s5_c_im))
    return [disc] * depth, res


def _s5_layouts_bwd(grads, res):
    ar, ai, ls, cr, ci, br, bi = res
    depth = len(grads)
    stack = lambda k, shape: jnp.stack([g[k] for g in grads]).reshape(shape)
    groups, shape_c = (depth * 24, S5_H, S5_P), (depth, 24, S5_H, S5_P)
    dbr, dbi, dcr, dci = _s5_bscale_bwd(cr, ci, br, bi, stack('bb_re', groups), stack('bb_im', groups))
    gp = (depth * 24, S5_P)
    dar, dai, dls = _s5_disc_bwd(ar, ai, ls, stack('lb_re', gp), stack('lb_im', gp), dcr.reshape(gp), dci.reshape(gp))
    return dict(
        s5_a_re=dar.reshape(depth, 24, S5_P), s5_a_im=dai.reshape(depth, 24, S5_P), s5_log_step=dls.reshape(depth, 24),
        s5_b_re=jnp.swapaxes(dbr.reshape(shape_c), 2, 3), s5_b_im=jnp.swapaxes(dbi.reshape(shape_c), 2, 3),
        s5_c_re=stack('c_re', shape_c), s5_c_im=stack('c_im', shape_c))


LATE = ('w_out', 'ple_w', 'ple_gate_w', 's5_w_glu')


ROWS = ('conv_b', 'rg_ba', 'rg_bx', 'rg_lambda', 's5_d', 's5_b_glu', 'ln1_g', 'ln1_b', 'ple_gate_b', 'ln2_g', 'ln2_b')


def _shared_weights(full):
    depth = full['conv_b'].shape[0]
    shared = {k: full[k].reshape(depth, 1, -1) for k in ROWS}
    shared.update(conv_w=full['conv_w'], wa_bd=full['rg_wa'], wx_bd=full['rg_wx'])
    return shared


def _layer_weights(full, shared, i):
    return dict(shared, w_in=full['w_in'][i])


class _AllLocal(_NoHooks):
    def __init__(self, full):
        self.full = full

    def late_weights(self, i, W, after):
        W[i].update({k: self.full[k][i] for k in LATE})
        return W


def _full_grads(full, x, p, target, hooks=None):
    hooks = hooks or _AllLocal(full)
    disc, res = _s5_layouts_fwd(full['s5_a_re'], full['s5_a_im'], full['s5_log_step'], full['s5_b_re'], full['s5_b_im'],
                                full['s5_c_re'], full['s5_c_im'], hooks.first_token)
    full = hooks.first_weights(full, disc[-1]['bb_im'])
    shared = _shared_weights(full)
    W = [_layer_weights(full, shared, i) for i in range(2)]
    hooks.res = res
    loss, gx, grads = _local_grads(x, p, target, W, disc, hooks)
    out = dict(hooks.small)
    for k in SHARD_AXIS:
        out[k] = [g[k] for g in grads]
    return loss, gx, out


def _small_grads(grads, res):
    stack = lambda f: jnp.stack([f(g) for g in grads])
    out = _s5_layouts_bwd(grads, res)
    out['conv_w'] = stack(lambda g: g['conv_w'])
    for k in ('conv_b', 'rg_ba', 'rg_bx', 'rg_lambda', 's5_b_glu', 'ln1_g', 'ln1_b', 'ple_gate_b', 'ln2_g', 'ln2_b'):
        out[k] = stack(lambda g: g[k][0])
    out['s5_d'] = stack(lambda g: g['s5_d'][0]).reshape(2, 24, 16)
    out['rg_wa'] = stack(lambda g: g['wa_bd'])
    out['rg_wx'] = stack(lambda g: g['wx_bd'])
    return out


SHARD_AXIS = {'w_in': 2, 'w_out': 1, 'ple_w': 2, 'ple_gate_w': 1, 's5_w_glu': 1}


def kernel(x, p, w_in, conv_w, conv_b, rg_wa, rg_ba, rg_wx, rg_bx, rg_lambda, s5_a_re, s5_a_im, s5_b_re, s5_b_im, s5_c_re, s5_c_im, s5_d, s5_log_step, s5_w_glu, s5_b_glu, w_out, ln1_g, ln1_b, ple_w, ple_gate_w, ple_gate_b, ln2_g, ln2_b, loss_target, m_w_in, m_conv_w, m_conv_b, m_rg_wa, m_rg_ba, m_rg_wx, m_rg_bx, m_rg_lambda, m_s5_a_re, m_s5_a_im, m_s5_b_re, m_s5_b_im, m_s5_c_re, m_s5_c_im, m_s5_d, m_s5_log_step, m_s5_w_glu, m_s5_b_glu, m_w_out, m_ln1_g, m_ln1_b, m_ple_w, m_ple_gate_w, m_ple_gate_b, m_ln2_g, m_ln2_b, v_w_in, v_conv_w, v_conv_b, v_rg_wa, v_rg_ba, v_rg_wx, v_rg_bx, v_rg_lambda, v_s5_a_re, v_s5_a_im, v_s5_b_re, v_s5_b_im, v_s5_c_re, v_s5_c_im, v_s5_d, v_s5_log_step, v_s5_w_glu, v_s5_b_glu, v_w_out, v_ln1_g, v_ln1_b, v_ple_w, v_ple_gate_w, v_ple_gate_b, v_ln2_g, v_ln2_b):
    local = dict(locals())
    w = {k: local[k] for k in WEIGHTS}
    mom = {k: local['m_' + k] for k in WEIGHTS}
    var = {k: local['v_' + k] for k in WEIGHTS}

    big = list(SHARD_AXIS)
    late_axes = [SHARD_AXIS[k] - 1 for k in LATE]
    pushed = {}

    def push_weights(key, names, layers, axes, after):
        shards = [w[k] if layer is not None else w[k][None] for k, layer in zip(names, layers)]
        dtypes = [WIRE if k in big else w[k].dtype for k in names]
        lands = _place_shards(shards, layers, axes, dtypes, "place_weights_" + key, after)
        pushed[key] = _push_start("gather", [], lands, axes, "push_weights_" + key)
        return pushed[key][4]

    def await_weights(key, axes, after):
        s = pushed[key]
        return _push_wait("gather", s[0], s[1], s[2], s[3], axes, after, "await_weights_" + key)

    token = push_weights("first", ['w_in', 'conv_w'], [0, None], [1, 0], None)
    token = push_weights("l0", LATE, [0] * len(LATE), late_axes, token)
    push_weights("l1", ['w_in'] + list(LATE), [1] * (1 + len(LATE)), [1] + late_axes, token)

    def push_grads(key, g, names, axes):
        srcs = [g[k] for k in names]
        pushed[key] = _push_start("scatter", srcs, _place_own("scatter", srcs, axes, "place_grads_" + key), axes,
                                  "push_grads_" + key)
        return pushed[key][4]

    def await_grads(key, axes, after):
        s = pushed[key]
        return _push_wait("scatter", s[0], s[1], s[2], s[3], axes, after, "await_grads_" + key)

    class Overlap(_NoHooks):
        token = pushed["l1"][4]
        first_token = token

        def first_weights(self, full, after):
            w_in0, conv = await_weights("first", [1, 0], after)
            return dict(full, w_in=[w_in0, None], conv_w=jnp.moveaxis(conv, 0, 2).reshape(2, 4, RG_W))

        def late_weights(self, i, W, after):
            if i == 0:
                W[0].update(zip(LATE, await_weights("l0", late_axes, after)))
            return W

        def layer_start(self, i, W, after):
            lands = await_weights("l1", [1] + late_axes, after)
            W[1].update(zip(LATE, lands[1:]), w_in=lands[0])
            return W

        def post_done(self, i, g):
            return push_grads("late0", g, LATE, late_axes) if i == 0 else None

        def smalls_done(self, grads, loss):
            super().smalls_done(grads, loss)
            conv = jnp.moveaxis(self.small['conv_w'].reshape(2, 4, N_DEV, RG_W // N_DEV), 2, 0)
            self.packed = _pack(self.small, loss)
            return push_grads("small", dict(conv_w=conv.reshape(N_DEV, 8, RG_W // N_DEV), small=self.packed),
                              ['conv_w', 'small'], [0, 0])

        def w_in_done(self, i, g):
            return push_grads("w_in0", g, ['w_in'], [0])

        def layer_done(self, i, g, dx):
            return push_grads("all1", g, ['w_in'] + list(LATE), [0] + late_axes) if i == 1 else None

    hooks = Overlap()
    _, grad_x, g = _full_grads(dict(w), x[0], p, loss_target[0], hooks)

    recv1 = dict(zip(['w_in'] + list(LATE), await_grads("all1", [0] + late_axes, grad_x)))
    recv0 = dict(zip(LATE, await_grads("late0", late_axes, grad_x)))
    outs = {}

    def update(k, parts):
        shard = w[k].shape
        c = shard[-1]
        two = lambda a: a.reshape(-1, c)
        res = _adamw([r.reshape(N_DEV, -1, c) for r in parts], two(w[k]), two(mom[k]), two(var[k]))
        outs[k] = [o.reshape(shard) for o in res]

    for k in LATE:
        update(k, [recv0[k], recv1[k]])
    done = [outs[k][1] for k in LATE]
    conv_parts, small_parts = await_grads("small", [0, 0], done)

    rows = hooks.packed.shape[0] // N_DEV
    mine = _sum_parts(small_parts.reshape(N_DEV, rows, LANE))
    gathered = _all_gather([mine], [0], "gather_small_grads")[0]
    w_in0, = await_grads("w_in0", [0], gathered)
    update('w_in', [w_in0, recv1['w_in']])
    update('conv_w', [conv_parts])
    summed, loss = _unpack(gathered, w)
    narrow = ['s5_b_re', 's5_b_im']
    for names, name in ((narrow, "adamw_s5_b"), ([k for k in SMALL if k not in narrow], "adamw_small")):
        delta, new_m, new_v = _adamw_natural(names, summed, w, mom, var, name)
        for k in names:
            outs[k] = [summed[k], delta[k], new_m[k], new_v[k]]

    res = [loss, grad_x[None]]
    for j in range(4):
        res += [outs[k][j] for k in WEIGHTS]
    return tuple(res)
```

```python
import math

import jax
import jax.numpy as jnp
from jax import lax
from jax.experimental import pallas as pl
from jax.experimental.pallas import tpu as pltpu

F32 = jnp.float32
MXU = jnp.bfloat16
WIRE = jnp.bfloat16

N_DEV = 8
D_MODEL = 1024
PLE_D = 256
RG_W = 640
S5_W = 384
S5_P = 64
S5_N = 24 * S5_P
Z_W = 2 * RG_W + 2 * S5_W
C_RGG = RG_W
C_S5U = 2 * RG_W
C_S5G = 2 * RG_W + S5_W
LANE = 128
N_RG_T = RG_W // LANE
N_S5_T = S5_W // LANE
W_BLK = Z_W // N_DEV
ALPHA = (2.0 * 2) ** 0.25
LN_EPS = 1e-5
RG_C = 8.0
LR, B1, B2, EPS, WD, STEP = 0.001, 0.9, 0.999, 1e-08, 0.01, 10
BC1 = 1.0 - B1 ** STEP
BC2 = 1.0 - B2 ** STEP
RC = 512
TM = 256
TM_MM = 1024
VMEM_LIMIT = 56 * 1024 * 1024

MESH = pl.DeviceIdType.MESH
ANY = pl.BlockSpec(memory_space=pl.ANY)


def _params(n_grid_axes, vmem=VMEM_LIMIT):
    return pltpu.CompilerParams(dimension_semantics=("arbitrary",) * n_grid_axes, vmem_limit_bytes=vmem)


def _S(shape, dtype=F32):
    return jax.ShapeDtypeStruct(tuple(shape), dtype)


def _sigmoid(x):
    return 0.5 * jnp.tanh(0.5 * x) + 0.5


def _silu_and_grad(x):
    s = _sigmoid(x)
    return x * s, s * (1.0 + x * (1.0 - s))


_GELU_C = math.sqrt(2.0 / math.pi)


def _gelu(x):
    return 0.5 * x * (1.0 + jnp.tanh(_GELU_C * (x + 0.044715 * (x * x * x))))


def _gelu_grad(x):
    th = jnp.tanh(_GELU_C * (x + 0.044715 * (x * x * x)))
    return 0.5 * (1.0 + th) + 0.5 * x * (1.0 - th * th) * (_GELU_C * (1.0 + 3.0 * 0.044715 * (x * x)))


def _mm(a, b):
    return jnp.dot(a.astype(MXU), b.astype(MXU), preferred_element_type=F32)


def _mm_nt(a, b):
    return lax.dot_general(a.astype(MXU), b.astype(MXU), (((1,), (1,)), ((), ())), preferred_element_type=F32)


def _mm_tn(a, b):
    return lax.dot_general(a.astype(MXU), b.astype(MXU), (((0,), (0,)), ((), ())), preferred_element_type=F32)


def _ln_fwd(t, g, b):
    mu = jnp.mean(t, axis=-1, keepdims=True)
    tc = t - mu
    var = jnp.mean(tc * tc, axis=-1, keepdims=True)
    rstd = lax.rsqrt(var + LN_EPS)
    xhat = tc * rstd
    return xhat * g + b, xhat, rstd


def _ln_bwd(dy, xhat, rstd, g):
    dxh = dy * g
    m1 = jnp.mean(dxh, axis=-1, keepdims=True)
    m2 = jnp.mean(dxh * xhat, axis=-1, keepdims=True)
    return rstd * (dxh - m1 - xhat * m2)


def _colsum(a):
    return jnp.sum(a, axis=0, keepdims=True)


def _up(x, d, rows, fill):
    n = x.shape[0]
    return jnp.where(rows < n - d, pltpu.roll(x, n - d, 0), fill)


SUB = 8
TILE_STEPS = (1, 2, 4)


def _r8(width):
    return lax.broadcasted_iota(jnp.int32, (SUB, width), 0)


def _scan_real(a, u, carry, reverse=False):
    r8 = _r8(a.shape[1])
    n = a.shape[0] // SUB
    outs = [None] * n
    for k in (reversed(range(n)) if reverse else range(n)):
        A, U = a[SUB * k:SUB * k + SUB], u[SUB * k:SUB * k + SUB]
        for d in TILE_STEPS:
            m = (r8 < SUB - d) if reverse else (r8 >= d)
            sh = SUB - d if reverse else d
            U = A * jnp.where(m, pltpu.roll(U, sh, 0), 0.0) + U
            A = A * jnp.where(m, pltpu.roll(A, sh, 0), 1.0)
        h = A * carry + U
        outs[k] = h
        carry = h[0:1] if reverse else h[SUB - 1:SUB]
    return jnp.concatenate(outs, axis=0), carry


def _tile_powers(lr, li, reverse=False):
    width = lr.shape[1]
    r8 = _r8(width)
    steps = []
    pr, pi = lr, li
    er, ei = jnp.broadcast_to(lr, (SUB, width)), jnp.broadcast_to(li, (SUB, width))
    for d in TILE_STEPS:
        m = (r8 < SUB - d) if reverse else (r8 >= d)
        sh = SUB - d if reverse else d
        steps.append((sh, jnp.where(m, pr, 0.0), jnp.where(m, pi, 0.0)))
        er, ei = _cmul(er, ei, jnp.where(m, pltpu.roll(er, sh, 0), 1.0), jnp.where(m, pltpu.roll(ei, sh, 0), 0.0))
        pr, pi = _cmul(pr, pi, pr, pi)
    return steps, (er, ei)


def _scan_lti(xr, xi, carry, steps, e, reverse=False):
    er, ei = e
    kr, ki = carry
    n = xr.shape[0] // SUB
    outr, outi = [None] * n, [None] * n
    for k in (reversed(range(n)) if reverse else range(n)):
        sr, si = xr[SUB * k:SUB * k + SUB], xi[SUB * k:SUB * k + SUB]
        for sh, pr, pi in steps:
            shr, shi = pltpu.roll(sr, sh, 0), pltpu.roll(si, sh, 0)
            sr, si = sr + (pr * shr - pi * shi), si + (pr * shi + pi * shr)
        sr = sr + (er * kr - ei * ki)
        si = si + (er * ki + ei * kr)
        outr[k], outi[k] = sr, si
        kr, ki = (sr[0:1], si[0:1]) if reverse else (sr[SUB - 1:SUB], si[SUB - 1:SUB])
    return jnp.concatenate(outr, axis=0), jnp.concatenate(outi, axis=0), (kr, ki)


def _halo(ref, c, r0):
    rp = pl.multiple_of(jnp.maximum(r0 - 8, 0), 8)
    return jnp.where(c > 0, ref[pl.ds(rp, 8), :], 0.0)


def _conv_taps(xe):
    return [pltpu.roll(xe, 3, 0)[8:, :], pltpu.roll(xe, 2, 0)[8:, :], pltpu.roll(xe, 1, 0)[8:, :], xe[8:, :]]


def _rg_gates(h, wa, wx, ba, bx, sp):
    r = _sigmoid(_mm(h, wa) + ba)
    i = _sigmoid(_mm(h, wx) + bx)
    log_a = (-RG_C) * r * sp
    a = jnp.exp(log_a)
    mult = jnp.sqrt(-jnp.tanh(log_a) * (a * a + 1.0))
    return r, i, a, mult


def _softplus(y):
    return jnp.maximum(y, 0.0) + jnp.log1p(jnp.exp(-jnp.abs(y)))


def _after(token):
    return ([], []) if token is None else ([token], [ANY])


def _inproj_fwd(x, w_in, token=None):
    L = x.shape[0]

    def body(x_ref, w_ref, *rest):
        rest[-1][...] = _mm(x_ref[...], w_ref[...])

    extra, extra_specs = _after(token)
    tm = min(TM_MM, L)
    return pl.pallas_call(
        body, name="inproj_fwd", grid=(L // tm,),
        in_specs=[pl.BlockSpec((tm, D_MODEL), lambda i: (i, 0)), pl.BlockSpec((D_MODEL, Z_W), lambda i: (0, 0))] + extra_specs,
        out_specs=pl.BlockSpec((tm, Z_W), lambda i: (i, 0)),
        out_shape=_S((L, Z_W)), compiler_params=_params(1))(x, w_in, *extra)


def _inproj_bwd(dt1, x, dzx, dzg, dzu, w_in):
    L = x.shape[0]

    def body(dt1_ref, x_ref, dzx_ref, dzg_ref, dzu_ref, w_ref, dx_ref, dw_ref, acc_ref):
        @pl.when(pl.program_id(0) == 0)
        def _():
            acc_ref[...] = jnp.zeros_like(acc_ref)
        dzg = dzg_ref[...]
        dz = jnp.concatenate([dzx_ref[...], dzg[:, :RG_W], dzu_ref[...], dzg[:, RG_W:]], axis=1).astype(MXU)
        xb = x_ref[...].astype(MXU)
        dx_ref[...] = ALPHA * dt1_ref[...] + _mm_nt(dz, w_ref[...])
        for j in range(N_DEV):
            acc_ref[j] += _mm_tn(xb, dz[:, j * W_BLK:(j + 1) * W_BLK])

        @pl.when(pl.program_id(0) == L // TM - 1)
        def _():
            dw_ref[...] = acc_ref[...].astype(WIRE)

    row = lambda w: pl.BlockSpec((TM, w), lambda i: (i, 0))
    wspec = pl.BlockSpec((N_DEV, D_MODEL, W_BLK), lambda i: (0, 0, 0))
    return pl.pallas_call(
        body, name="inproj_bwd", grid=(L // TM,),
        in_specs=[row(D_MODEL), row(D_MODEL), row(RG_W), row(D_MODEL), row(S5_W),
                  pl.BlockSpec((D_MODEL, Z_W), lambda i: (0, 0))],
        out_specs=[row(D_MODEL), wspec],
        out_shape=[_S((L, D_MODEL)), _S((N_DEV, D_MODEL, W_BLK), WIRE)],
        scratch_shapes=[pltpu.VMEM((N_DEV, D_MODEL, W_BLK), F32)],
        compiler_params=_params(1))(dt1, x, dzx, dzg, dzu, w_in)


TM2 = 512


def _dz_block(dzx_ref, dzg_ref, dzu_ref):
    dzg = dzg_ref[...]
    return jnp.concatenate([dzx_ref[...], dzg[:, :RG_W], dzu_ref[...], dzg[:, RG_W:]], axis=1).astype(MXU)


def _inproj_bwd_dw(x, dzx, dzg, dzu, token=None):
    L = x.shape[0]
    extra, extra_specs = _after(token)

    def body(x_ref, dzx_ref, dzg_ref, dzu_ref, *rest):
        dw_ref, acc_ref = rest[len(extra):]
        @pl.when(pl.program_id(0) == 0)
        def _():
            acc_ref[...] = jnp.zeros_like(acc_ref)
        dz = _dz_block(dzx_ref, dzg_ref, dzu_ref)
        xb = x_ref[...].astype(MXU)
        for j in range(N_DEV):
            acc_ref[j] += _mm_tn(xb, dz[:, j * W_BLK:(j + 1) * W_BLK])

        @pl.when(pl.program_id(0) == L // TM2 - 1)
        def _():
            dw_ref[...] = acc_ref[...].astype(WIRE)

    row = lambda w: pl.BlockSpec((TM2, w), lambda i: (i, 0))
    wspec = pl.BlockSpec((N_DEV, D_MODEL, W_BLK), lambda i: (0, 0, 0))
    return pl.pallas_call(
        body, name="inproj_bwd_dw", grid=(L // TM2,),
        in_specs=[row(D_MODEL), row(RG_W), row(D_MODEL), row(S5_W)] + extra_specs, out_specs=wspec,
        out_shape=_S((N_DEV, D_MODEL, W_BLK), WIRE), scratch_shapes=[pltpu.VMEM((N_DEV, D_MODEL, W_BLK), F32)],
        compiler_params=_params(1))(x, dzx, dzg, dzu, *extra)


def _inproj_bwd_dx(dt1, dzx, dzg, dzu, w_in, token=None):
    L = dt1.shape[0]
    extra, extra_specs = _after(token)

    def body(dt1_ref, dzx_ref, dzg_ref, dzu_ref, w_ref, *rest):
        rest[-1][...] = ALPHA * dt1_ref[...] + _mm_nt(_dz_block(dzx_ref, dzg_ref, dzu_ref), w_ref[...])

    tm = min(TM_MM, L)
    row = lambda w: pl.BlockSpec((tm, w), lambda i: (i, 0))
    return pl.pallas_call(
        body, name="inproj_bwd_dx", grid=(L // tm,),
        in_specs=[row(D_MODEL), row(RG_W), row(D_MODEL), row(S5_W), _full((D_MODEL, Z_W))] + extra_specs,
        out_specs=row(D_MODEL), out_shape=_S((L, D_MODEL)), compiler_params=_params(1))(dt1, dzx, dzg, dzu, w_in, *extra)


def _rg_specs(layer):
    tile = lambda rows: pl.BlockSpec((rows, LANE), lambda c: (0, c))
    ptile = lambda rows: pl.BlockSpec((None, rows, LANE), lambda c: (layer, 0, c))
    pheads = pl.BlockSpec((None, 2, RG_HD, RG_HD), lambda c: (layer, c, 0, 0))
    return tile, ptile, pheads, pl.BlockSpec((2, RG_HD, RG_HD), lambda c: (c, 0, 0))


RG_HD = 64


def _bd2(w):
    z = jnp.zeros((RG_HD, RG_HD), w.dtype)
    return jnp.concatenate([jnp.concatenate([w[0], z], axis=1), jnp.concatenate([z, w[1]], axis=1)], axis=0)


def _bd2_diag(m):
    return jnp.stack([m[:RG_HD, :RG_HD], m[RG_HD:, RG_HD:]])


def _rg_fwd(z, cw, cb, wa_bd, wx_bd, ba, bx, lam, layer):
    L = z.shape[0]

    def body(x_ref, cw_ref, cb_ref, wa_ref, wx_ref, ba_ref, bx_ref, lam_ref, hs_ref, *saved):
        w, b = cw_ref[...], cb_ref[...]
        wa, wx, ba_, bx_ = _bd2(wa_ref[...]).astype(MXU), _bd2(wx_ref[...]).astype(MXU), ba_ref[...], bx_ref[...]
        sp = _softplus(-lam_ref[...])

        def step(c, carry):
            r0 = pl.multiple_of(c * RC, RC)
            xe = jnp.concatenate([_halo(x_ref, c, r0), x_ref[pl.ds(r0, RC), :]], axis=0)
            t = _conv_taps(xe)
            h = t[0] * w[0:1] + t[1] * w[1:2] + t[2] * w[2:3] + t[3] * w[3:4] + b
            r, i, a, mult = _rg_gates(h, wa, wx, ba_, bx_, sp)
            hs, carry = _scan_real(a, mult * (i * h), carry)
            hs_ref[pl.ds(r0, RC), :] = hs
            for ref, val in zip(saved, (h, r, i, a, mult)):
                ref[pl.ds(r0, RC), :] = val
            return carry

        lax.fori_loop(0, L // RC, step, jnp.zeros((1, LANE), F32))

    tile, ptile, pheads, _ = _rg_specs(layer)
    return pl.pallas_call(
        body, name="rg_fwd", grid=(N_RG_T,),
        in_specs=[tile(L), ptile(4), ptile(1), pheads, pheads, ptile(1), ptile(1), ptile(1)],
        out_specs=[tile(L)] * 6, out_shape=[_S((L, RG_W))] * 6, compiler_params=_params(1))(
            z, cw, cb, wa_bd, wx_bd, ba, bx, lam)


def _rg_bwd(dhs, z, hs, gates, cw, wa_bd, wx_bd, lam, layer):
    L = z.shape[0]

    def body(g_ref, x_ref, hs_ref, h_ref, r_ref, i_ref, a_ref, mult_ref, cw_ref, wa_ref, wx_ref, lam_ref,
             dx_ref, dcw_ref, dcb_ref, dwa_out, dwx_out, dba_ref, dbx_ref, dlam_ref, dwa_ref, dwx_ref):
        w = cw_ref[...]
        wa, wx = _bd2(wa_ref[...]).astype(MXU), _bd2(wx_ref[...]).astype(MXU)
        lam = lam_ref[...]
        sp = _softplus(-lam)
        rows = lax.broadcasted_iota(jnp.int32, (RC, LANE), 0)
        for ref in (dcw_ref, dcb_ref, dwa_ref, dwx_ref, dba_ref, dbx_ref, dlam_ref):
            ref[...] = jnp.zeros_like(ref)
        nch = L // RC

        def step(k, carry):
            cin, nxt = carry
            c = nch - 1 - k
            r0 = pl.multiple_of(c * RC, RC)
            xe = jnp.concatenate([_halo(x_ref, c, r0), x_ref[pl.ds(r0, RC), :]], axis=0)
            t = _conv_taps(xe)
            h, r, i, a, mult = (ref[pl.ds(r0, RC), :] for ref in (h_ref, r_ref, i_ref, a_ref, mult_ref))
            hs_e = jnp.concatenate([_halo(hs_ref, c, r0), hs_ref[pl.ds(r0, RC), :]], axis=0)
            hs_prev = pltpu.roll(hs_e, 1, 0)[8:, :]
            g = g_ref[pl.ds(r0, RC), :]
            cc, cin_new = _scan_real(a, a * g, cin, reverse=True)
            dh = g + _up(cc, 1, rows, cin)
            ih = i * h
            dlog_a = dh * hs_prev * a - (dh * ih) * (a * a) / mult
            di = dh * mult * h
            dhin = dh * mult * i
            dr = dlog_a * ((-RG_C) * sp)
            dlam_ref[...] += _colsum(dlog_a * r)
            dra = dr * r * (1.0 - r)
            dia = di * i * (1.0 - i)
            dwa_ref[...] += _mm_tn(h, dra)
            dwx_ref[...] += _mm_tn(h, dia)
            dba_ref[...] += _colsum(dra)
            dbx_ref[...] += _colsum(dia)
            dhin = dhin + _mm_nt(dra, wa) + _mm_nt(dia, wx)
            de = jnp.concatenate([dhin, nxt], axis=0)
            n = RC + 8
            dx = (dhin * w[3:4] + pltpu.roll(de, n - 1, 0)[:RC, :] * w[2:3]
                  + pltpu.roll(de, n - 2, 0)[:RC, :] * w[1:2] + pltpu.roll(de, n - 3, 0)[:RC, :] * w[0:1])
            dx_ref[pl.ds(r0, RC), :] = dx
            for kk in range(4):
                dcw_ref[kk:kk + 1, :] += _colsum(dhin * t[kk])
            dcb_ref[...] += _colsum(dhin)
            return cin_new, dhin[0:8, :]

        lax.fori_loop(0, nch, step, (jnp.zeros((1, LANE), F32), jnp.zeros((8, LANE), F32)))
        dlam_ref[...] = dlam_ref[...] * (RG_C * _sigmoid(-lam))
        dwa_out[...], dwx_out[...] = _bd2_diag(dwa_ref[...]), _bd2_diag(dwx_ref[...])

    tile, ptile, pheads, gheads = _rg_specs(layer)
    heads = _S((2 * N_RG_T, RG_HD, RG_HD))
    return pl.pallas_call(
        body, name="rg_bwd", grid=(N_RG_T,),
        in_specs=[tile(L)] * 8 + [ptile(4), pheads, pheads, ptile(1)],
        out_specs=[tile(L), tile(4), tile(1), gheads, gheads, tile(1), tile(1), tile(1)],
        out_shape=[_S((L, RG_W)), _S((4, RG_W)), _S((1, RG_W)), heads, heads, _S((1, RG_W)), _S((1, RG_W)), _S((1, RG_W))],
        scratch_shapes=[pltpu.VMEM((LANE, LANE), F32), pltpu.VMEM((LANE, LANE), F32)],
        compiler_params=_params(1))(dhs, z, hs, *gates, cw, wa_bd, wx_bd, lam)


def _cmul(ar, ai, br, bi):
    return ar * br - ai * bi, ar * bi + ai * br


S5_TW = S5_N // N_S5_T


S5_H = 16
S5_GT = LANE // S5_H


def _s5_specs(L, layer):
    in_tile = pl.BlockSpec((L, LANE), lambda t: (0, t))
    st = pl.BlockSpec((L, S5_TW), lambda t: (0, t))
    pg = pl.BlockSpec((None, S5_GT, S5_H, S5_P), lambda t: (layer * N_S5_T + t, 0, 0, 0))
    plb = pl.BlockSpec((None, S5_GT, S5_P), lambda t: (layer * N_S5_T + t, 0, 0))
    gg = pl.BlockSpec((None, S5_GT, S5_H, S5_P), lambda t: (t, 0, 0, 0))
    glb = pl.BlockSpec((None, S5_GT, S5_P), lambda t: (t, 0, 0))
    dv = pl.BlockSpec((1, LANE), lambda t: (0, t))
    return in_tile, st, pg, plb, gg, glb, dv


def _bd8(blocks):
    rows = []
    for g in range(S5_GT):
        pieces = [blocks[g]]
        if g:
            pieces.insert(0, jnp.zeros((S5_H, S5_P * g), blocks.dtype))
        if g < S5_GT - 1:
            pieces.append(jnp.zeros((S5_H, S5_P * (S5_GT - 1 - g)), blocks.dtype))
        rows.append(jnp.concatenate(pieces, axis=1))
    return jnp.concatenate(rows, axis=0)


def _bd8_diag(m):
    return jnp.stack([m[S5_H * g:S5_H * (g + 1), S5_P * g:S5_P * (g + 1)] for g in range(S5_GT)])


def _row8(v):
    return jnp.concatenate([v[g:g + 1] for g in range(S5_GT)], axis=1)


def _row8_split(r):
    return jnp.concatenate([r[:, S5_P * g:S5_P * (g + 1)] for g in range(S5_GT)], axis=0)


def _layer_row_tile(layer):
    return pl.BlockSpec((None, 1, LANE), lambda t: (layer, 0, t))


def _s5_fwd(z, bb_re, bb_im, lb_re, lb_im, c_re, c_im, dvec, layer):
    L = z.shape[0]

    def body(u_ref, bbr_ref, bbi_ref, lr_ref, li_ref, cr_ref, ci_ref, d_ref, y_ref, sr_ref, si_ref):
        bbr, bbi = _bd8(bbr_ref[...]).astype(MXU), _bd8(bbi_ref[...]).astype(MXU)
        cr, ci = _bd8(cr_ref[...]).astype(MXU), _bd8(ci_ref[...]).astype(MXU)
        dv = d_ref[...]
        steps, e = _tile_powers(_row8(lr_ref[...]), _row8(li_ref[...]))

        def step(c, carry):
            r0 = pl.multiple_of(c * RC, RC)
            u = u_ref[pl.ds(r0, RC), :]
            ub = u.astype(MXU)
            sr = jnp.dot(ub, bbr, preferred_element_type=F32)
            si = jnp.dot(ub, bbi, preferred_element_type=F32)
            sr, si, carry = _scan_lti(sr, si, carry, steps, e)
            sr_ref[pl.ds(r0, RC), :] = sr
            si_ref[pl.ds(r0, RC), :] = si
            y_ref[pl.ds(r0, RC), :] = dv * u + (_mm_nt(sr, cr) - _mm_nt(si, ci))
            return carry

        zero = jnp.zeros((1, S5_TW), F32)
        lax.fori_loop(0, L // RC, step, (zero, zero))

    in_tile, st, pg, plb, _, _, _ = _s5_specs(L, layer)
    u_tile = pl.BlockSpec((L, LANE), lambda t: (0, C_S5U // LANE + t))
    return pl.pallas_call(
        body, name="s5_fwd", grid=(N_S5_T,),
        in_specs=[u_tile, pg, pg, plb, plb, pg, pg, _layer_row_tile(layer)],
        out_specs=[in_tile, st, st],
        out_shape=[_S((L, S5_W)), _S((L, S5_N)), _S((L, S5_N))],
        compiler_params=_params(1))(z, bb_re, bb_im, lb_re, lb_im, c_re, c_im, dvec)


def _s5_bwd(dy0, z, s_re, s_im, bb_re, bb_im, lb_re, lb_im, c_re, c_im, dvec, layer, token=None):
    L = z.shape[0]
    extra, extra_specs = _after(token)

    def body(dy_ref, u_ref, sr_ref, si_ref, bbr_ref, bbi_ref, lr_ref, li_ref, cr_ref, ci_ref, d_ref, *rest):
        (du_ref, dbbr_out, dbbi_out, dlr_out, dli_out, dcr_out, dci_out, dd_ref,
         dbbr_ref, dbbi_ref, dcr_ref, dci_ref, dlr_ref, dli_ref) = rest[len(extra):]
        bbr, bbi = _bd8(bbr_ref[...]).astype(MXU), _bd8(bbi_ref[...]).astype(MXU)
        cr, ci = _bd8(cr_ref[...]).astype(MXU), _bd8(ci_ref[...]).astype(MXU)
        lr, li = _row8(lr_ref[...]), -_row8(li_ref[...])
        dv = d_ref[...]
        steps, e = _tile_powers(lr, li, reverse=True)
        for ref in (dbbr_ref, dbbi_ref, dlr_ref, dli_ref, dcr_ref, dci_ref, dd_ref):
            ref[...] = jnp.zeros_like(ref)
        nch = L // RC

        def step(k, carry):
            c = nch - 1 - k
            r0 = pl.multiple_of(c * RC, RC)
            dy = dy_ref[pl.ds(r0, RC), :]
            u = u_ref[pl.ds(r0, RC), :]
            dyb, ub = dy.astype(MXU), u.astype(MXU)
            sr, si = sr_ref[pl.ds(r0, RC), :], si_ref[pl.ds(r0, RC), :]
            dcr_ref[...] += _mm_tn(dyb, sr)
            dci_ref[...] -= _mm_tn(dyb, si)
            gr = jnp.dot(dyb, cr, preferred_element_type=F32)
            gi = -jnp.dot(dyb, ci, preferred_element_type=F32)
            gr, gi, carry = _scan_lti(gr, gi, carry, steps, e, reverse=True)
            pr_ = pltpu.roll(jnp.concatenate([_halo(sr_ref, c, r0), sr], axis=0), 1, 0)[8:, :]
            pi_ = pltpu.roll(jnp.concatenate([_halo(si_ref, c, r0), si], axis=0), 1, 0)[8:, :]
            dlr_ref[...] += _colsum(pr_ * gr + pi_ * gi)
            dli_ref[...] += _colsum(pr_ * gi - pi_ * gr)
            grb, gib = gr.astype(MXU), gi.astype(MXU)
            dbbr_ref[...] += _mm_tn(ub, grb)
            dbbi_ref[...] += _mm_tn(ub, gib)
            du_ref[pl.ds(r0, RC), :] = dv * dy + (_mm_nt(grb, bbr) + _mm_nt(gib, bbi))
            dd_ref[...] += _colsum(dy * u)
            return carry

        zero = jnp.zeros((1, S5_TW), F32)
        lax.fori_loop(0, nch, step, (zero, zero))
        dbbr_out[...], dbbi_out[...] = _bd8_diag(dbbr_ref[...]), _bd8_diag(dbbi_ref[...])
        dcr_out[...], dci_out[...] = _bd8_diag(dcr_ref[...]), _bd8_diag(dci_ref[...])
        dlr_out[...], dli_out[...] = _row8_split(dlr_ref[...]), _row8_split(dli_ref[...])

    in_tile, st, pg, plb, gg, glb, dv = _s5_specs(L, layer)
    u_tile = pl.BlockSpec((L, LANE), lambda t: (0, C_S5U // LANE + t))
    groups, rows = _S((N_S5_T, S5_GT, S5_H, S5_P)), _S((N_S5_T, S5_GT, S5_P))
    wide = pltpu.VMEM((LANE, S5_TW), F32)
    return pl.pallas_call(
        body, name="s5_bwd", grid=(N_S5_T,),
        in_specs=[in_tile, u_tile, st, st, pg, pg, plb, plb, pg, pg, _layer_row_tile(layer)] + extra_specs,
        out_specs=[in_tile, gg, gg, glb, glb, gg, gg, dv],
        out_shape=[_S((L, S5_W)), groups, groups, rows, rows, groups, groups, _S((1, S5_W))],
        scratch_shapes=[wide, wide, wide, wide, pltpu.VMEM((1, S5_TW), F32), pltpu.VMEM((1, S5_TW), F32)],
        compiler_params=_params(1))(dy0, z, s_re, s_im, bb_re, bb_im, lb_re, lb_im, c_re, c_im, dvec, *extra)


def _disc(ar, ai, ls):
    dt = jnp.exp(ls)
    mag = jnp.exp(ar * dt)
    lr = mag * jnp.cos(ai * dt)
    li = mag * jnp.sin(ai * dt)
    den = ar * ar + ai * ai
    cr = ((lr - 1.0) * ar + li * ai) / den
    ci = (li * ar - (lr - 1.0) * ai) / den
    return lr, li, cr, ci


def _s5_disc_fwd(ar, ai, ls, token=None):
    extra, extra_specs = _after(token)

    def body(ar_ref, ai_ref, ls_ref, *rest):
        lr_ref, li_ref, cr_ref, ci_ref = rest[len(extra):]
        lr, li, cr, ci = _disc(ar_ref[...], ai_ref[...], ls_ref[...])
        lr_ref[...], li_ref[...], cr_ref[...], ci_ref[...] = lr, li, cr, ci

    sh = _S(ar.shape)
    vm = pl.BlockSpec(memory_space=pltpu.VMEM)
    return pl.pallas_call(body, name="s5_disc_fwd", in_specs=[vm, vm, vm] + extra_specs, out_shape=[sh, sh, sh, sh])(
        ar, ai, ls, *extra)


def _s5_disc_bwd(ar, ai, ls, dlr, dli, dcr, dci):
    def body(ar_ref, ai_ref, ls_ref, dlr_ref, dli_ref, dcr_ref, dci_ref, dar_ref, dai_ref, dls_ref):
        _, vjp = jax.vjp(_disc, ar_ref[...], ai_ref[...], jnp.broadcast_to(ls_ref[...], ar_ref.shape))
        dar, dai, dls = vjp((dlr_ref[...], dli_ref[...], dcr_ref[...], dci_ref[...]))
        dar_ref[...], dai_ref[...] = dar, dai
        dls_ref[...] = jnp.sum(dls, axis=1, keepdims=True)

    return pl.pallas_call(body, name="s5_disc_bwd", out_shape=[_S(ar.shape), _S(ar.shape), _S(ls.shape)])(
        ar, ai, ls, dlr, dli, dcr, dci)


def _s5_bscale_fwd(cr, ci, br, bi):
    def body(cr_ref, ci_ref, br_ref, bi_ref, or_ref, oi_ref):
        or_ref[...], oi_ref[...] = _cmul(cr_ref[...], ci_ref[...], br_ref[...], bi_ref[...])

    return pl.pallas_call(body, name="s5_bscale_fwd", out_shape=[_S(br.shape), _S(br.shape)])(cr, ci, br, bi)


def _s5_bscale_bwd(cr, ci, br, bi, gr, gi):
    def body(cr_ref, ci_ref, br_ref, bi_ref, gr_ref, gi_ref, dbr_ref, dbi_ref, dcr_ref, dci_ref):
        cr_, ci_, br_, bi_, gr_, gi_ = (r[...] for r in (cr_ref, ci_ref, br_ref, bi_ref, gr_ref, gi_ref))
        dbr_ref[...] = cr_ * gr_ + ci_ * gi_
        dbi_ref[...] = cr_ * gi_ - ci_ * gr_
        dcr_ref[...] = jnp.sum(gr_ * br_ + gi_ * bi_, axis=1, keepdims=True)
        dci_ref[...] = jnp.sum(gi_ * br_ - gr_ * bi_, axis=1, keepdims=True)

    return pl.pallas_call(body, name="s5_bscale_bwd",
                          out_shape=[_S(br.shape), _S(br.shape), _S(cr.shape), _S(cr.shape)])(cr, ci, br, bi, gr, gi)


def _row(w):
    return pl.BlockSpec((TM, w), lambda i: (i, 0))


def _full(shape):
    return pl.BlockSpec(tuple(shape), lambda i: (0,) * len(shape))


def _p_rows(layer):
    return pl.BlockSpec((None, None, TM, PLE_D), lambda i: (layer, 0, i, 0))


def _lrow(layer, width):
    return pl.BlockSpec((None, 1, width), lambda i: (layer, 0, 0))


def _post_fwd(x, hs, z, y0, p, w_glu, b_glu, w_out, g1, b1, ple_w, w_pg, b_pg, g2, b2, layer):
    L = x.shape[0]

    def body(x_ref, hs_ref, z_ref, y0_ref, p_ref, wg_ref, bg_ref, wo_ref, g1_ref, b1_ref, pw_ref, wpg_ref, bpg_ref,
             g2_ref, b2_ref, x2_ref, xh1_ref, xh2_ref, q_ref, gt_ref, rstd1_ref, rstd2_ref):
        rg_gate = z_ref[:, C_RGG:C_RGG + RG_W]
        s5_gate = z_ref[:, C_S5G:C_S5G + S5_W]
        rg_y = hs_ref[...] * _silu_and_grad(rg_gate)[0]
        y1 = _gelu(y0_ref[...])
        gl = _sigmoid(_mm(y1, wg_ref[...]) + bg_ref[...])
        s5_y = (y1 * gl) * _silu_and_grad(s5_gate)[0]
        mix = _mm(jnp.concatenate([rg_y.astype(MXU), s5_y.astype(MXU)], axis=1), wo_ref[...])
        t1 = ALPHA * x_ref[...] + mix
        x1, xh1, rstd1 = _ln_fwd(t1, g1_ref[...], b1_ref[...])
        q = _mm(p_ref[...], pw_ref[...])
        gt = _sigmoid(_mm(x1, wpg_ref[...]) + bpg_ref[...])
        t2 = ALPHA * x1 + q * gt
        x2, xh2, rstd2 = _ln_fwd(t2, g2_ref[...], b2_ref[...])
        x2_ref[...], xh1_ref[...], xh2_ref[...], q_ref[...], gt_ref[...] = x2, xh1, xh2, q, gt
        rstd1_ref[...], rstd2_ref[...] = rstd1, rstd2

    vec = _lrow(layer, D_MODEL)
    return pl.pallas_call(
        body, name="post_fwd", grid=(L // TM,),
        in_specs=[_row(D_MODEL), _row(RG_W), _row(Z_W), _row(S5_W), _p_rows(layer), _full((S5_W, S5_W)), _lrow(layer, S5_W),
                  _full((D_MODEL, D_MODEL)), vec, vec, _full((PLE_D, D_MODEL)), _full((D_MODEL, D_MODEL)), vec, vec, vec],
        out_specs=[_row(D_MODEL)] * 5 + [_row(1)] * 2, out_shape=[_S((L, D_MODEL))] * 5 + [_S((L, 1))] * 2,
        compiler_params=_params(1))(x, hs, z, y0, p, w_glu, b_glu, w_out, g1, b1, ple_w, w_pg, b_pg, g2, b2)


def _post_bwd_a(dx2_or_target, is_top, xh2, xh1, rstd2, rstd1, q, gt, p, w_pg, g1, b1, g2, b2, layer, token=None):
    L = xh1.shape[0]
    extra, extra_specs = _after(token)

    def body(d_ref, xh2_ref, xh1_ref, rstd2_ref, rstd1_ref, q_ref, gt_ref, p_ref, wpg_ref, g1_ref, b1_ref, g2_ref,
             b2_ref, *rest):
        (dt1_ref, dpw_out, dwpg_out, dbpg_ref, dg1_ref, db1_ref, dg2_ref, db2_ref, loss_ref, dpw_ref,
         dwpg_ref) = rest[len(extra):]
        @pl.when(pl.program_id(0) == 0)
        def _():
            for ref in (dpw_ref, dwpg_ref, dbpg_ref, dg1_ref, db1_ref, dg2_ref, db2_ref, loss_ref):
                ref[...] = jnp.zeros_like(ref)

        g1, g2 = g1_ref[...], g2_ref[...]
        xh1, xh2, rstd1, rstd2 = xh1_ref[...], xh2_ref[...], rstd1_ref[...], rstd2_ref[...]
        x1 = xh1 * g1 + b1_ref[...]
        if is_top:
            err = (xh2 * g2 + b2_ref[...]) - d_ref[...]
            loss_ref[...] += _colsum(err * err)
            dx2 = err * (1.0 / D_MODEL)
        else:
            dx2 = d_ref[...]
        p = p_ref[...]
        q, gt = q_ref[...], gt_ref[...]
        dg2_ref[...] += _colsum(dx2 * xh2)
        db2_ref[...] += _colsum(dx2)
        dt2 = _ln_bwd(dx2, xh2, rstd2, g2)
        dq = dt2 * gt
        dgpre = (dt2 * q) * gt * (1.0 - gt)
        dpw_ref[...] += _mm_tn(p, dq)
        dwpg_ref[...] += _mm_tn(x1, dgpre)
        dbpg_ref[...] += _colsum(dgpre)
        dx1 = ALPHA * dt2 + _mm_nt(dgpre, wpg_ref[...])
        dg1_ref[...] += _colsum(dx1 * xh1)
        db1_ref[...] += _colsum(dx1)
        dt1_ref[...] = _ln_bwd(dx1, xh1, rstd1, g1)

        @pl.when(pl.program_id(0) == L // TM - 1)
        def _():
            dpw_out[...] = dpw_ref[...].astype(WIRE)
            dwpg_out[...] = dwpg_ref[...].astype(WIRE)

    vec, lvec = _full((1, D_MODEL)), _lrow(layer, D_MODEL)
    return pl.pallas_call(
        body, name="post_bwd_a_top" if is_top else "post_bwd_a", grid=(L // TM,),
        in_specs=[_row(D_MODEL), _row(D_MODEL), _row(D_MODEL), _row(1), _row(1), _row(D_MODEL), _row(D_MODEL), _p_rows(layer),
                  _full((D_MODEL, D_MODEL)), lvec, lvec, lvec, lvec] + extra_specs,
        out_specs=[_row(D_MODEL), _full((PLE_D, D_MODEL)), _full((D_MODEL, D_MODEL)), vec, vec, vec, vec, vec, vec],
        out_shape=[_S((L, D_MODEL)), _S((PLE_D, D_MODEL), WIRE), _S((D_MODEL, D_MODEL), WIRE)] + [_S((1, D_MODEL))] * 6,
        scratch_shapes=[pltpu.VMEM((PLE_D, D_MODEL), F32), pltpu.VMEM((D_MODEL, D_MODEL), F32)],
        compiler_params=_params(1))(dx2_or_target, xh2, xh1, rstd2, rstd1, q, gt, p, w_pg, g1, b1, g2, b2, *extra)


def _post_bwd_b(dt1, z, hs, y0, w_out, w_glu, b_glu, layer):
    L = dt1.shape[0]

    def body(dt1_ref, z_ref, hs_ref, y0_ref, wo_ref, wg_ref, bg_ref,
             dhs_ref, dy0_ref, dzg_ref, dwo_out, dwg_out, dbg_ref, dwo_ref, dwg_ref):
        @pl.when(pl.program_id(0) == 0)
        def _():
            for ref in (dwo_ref, dwg_ref, dbg_ref):
                ref[...] = jnp.zeros_like(ref)

        dt1b = dt1_ref[...].astype(MXU)
        dm = _mm_nt(dt1b, wo_ref[...])
        d_rgy, d_s5y = dm[:, :RG_W], dm[:, RG_W:]
        rg_gate = z_ref[:, C_RGG:C_RGG + RG_W]
        s5_gate = z_ref[:, C_S5G:C_S5G + S5_W]
        hs = hs_ref[...]
        sl, dsl = _silu_and_grad(rg_gate)
        dhs_ref[...] = d_rgy * sl
        dzg_ref[:, :RG_W] = d_rgy * hs * dsl
        y0 = y0_ref[...]
        y1 = _gelu(y0)
        gl = _sigmoid(_mm(y1, wg_ref[...]) + bg_ref[...])
        y2 = y1 * gl
        sl2, dsl = _silu_and_grad(s5_gate)
        m = jnp.concatenate([(hs * sl).astype(MXU), (y2 * sl2).astype(MXU)], axis=1)
        dwo_ref[...] += _mm_tn(m, dt1b)
        dy2 = d_s5y * sl2
        dzg_ref[:, RG_W:] = d_s5y * y2 * dsl
        dglpre = (dy2 * y1) * gl * (1.0 - gl)
        dwg_ref[...] += _mm_tn(y1, dglpre)
        dbg_ref[...] += _colsum(dglpre)
        dy1 = dy2 * gl + _mm_nt(dglpre, wg_ref[...])
        dy0_ref[...] = dy1 * _gelu_grad(y0)

        @pl.when(pl.program_id(0) == L // TM - 1)
        def _():
            dwo_out[...] = dwo_ref[...].astype(WIRE)
            dwg_out[...] = dwg_ref[...].astype(WIRE)

    return pl.pallas_call(
        body, name="post_bwd_b", grid=(L // TM,),
        in_specs=[_row(D_MODEL), _row(Z_W), _row(RG_W), _row(S5_W), _full((D_MODEL, D_MODEL)),
                  _full((S5_W, S5_W)), _lrow(layer, S5_W)],
        out_specs=[_row(RG_W), _row(S5_W), _row(D_MODEL), _full((D_MODEL, D_MODEL)), _full((S5_W, S5_W)), _full((1, S5_W))],
        out_shape=[_S((L, RG_W)), _S((L, S5_W)), _S((L, D_MODEL)), _S((D_MODEL, D_MODEL), WIRE), _S((S5_W, S5_W), WIRE),
                   _S((1, S5_W))],
        scratch_shapes=[pltpu.VMEM((D_MODEL, D_MODEL), F32), pltpu.VMEM((S5_W, S5_W), F32)],
        compiler_params=_params(1))(dt1, z, hs, y0, w_out, w_glu, b_glu)


def _adamw(parts, w, m, v, token=None):
    nl = len(parts)
    extra, extra_specs = _after(token)
    n, R, C = parts[0].shape
    tr = R
    for cand in (512, 256, 128, 64, 32, 16, 8):
        if R % cand == 0 and n * cand * C * 4 <= 4 * 1024 * 1024:
            tr = cand
            break
    nblk = R // tr

    def body(*refs):
        p_refs = refs[:nl]
        w_ref, m_ref, v_ref = refs[nl:nl + 3]
        g_ref, d_ref, nm_ref, nv_ref = refs[nl + 3 + len(extra):]
        layer = pl.program_id(0)
        g = None
        for li, p_ref in enumerate(p_refs):
            s = p_ref[0].astype(F32)
            for k in range(1, n):
                s = s + p_ref[k].astype(F32)
            g = s if g is None else jnp.where(layer == li, s, g)
        nm = B1 * m_ref[...] + (1.0 - B1) * g
        nv = B2 * v_ref[...] + (1.0 - B2) * (g * g)
        d_ref[...] = (-LR) * ((nm / BC1) / (jnp.sqrt(nv / BC2) + EPS) + WD * w_ref[...])
        g_ref[...], nm_ref[...], nv_ref[...] = g, nm, nv

    def part_spec(li):
        return pl.BlockSpec((n, tr, C), lambda l, i: (0, jnp.where(l == li, i, jnp.where(l < li, 0, nblk - 1)), 0))

    blk = pl.BlockSpec((tr, C), lambda l, i: (l * nblk + i, 0))
    return pl.pallas_call(
        body, name="adamw", grid=(nl, nblk),
        in_specs=[part_spec(li) for li in range(nl)] + [blk, blk, blk] + extra_specs,
        out_specs=[blk] * 4, out_shape=[_S((nl * R, C))] * 4, compiler_params=_params(2))(*parts, w, m, v, *extra)


def _adamw_natural(names, g, w, m, v, name):
    n = len(names)

    def body(*refs):
        for j in range(n):
            g_ref, w_ref, m_ref, v_ref, d_ref, nm_ref, nv_ref = (refs[k * n + j] for k in range(7))
            gj = g_ref[...]
            nm = B1 * m_ref[...] + (1.0 - B1) * gj
            nv = B2 * v_ref[...] + (1.0 - B2) * (gj * gj)
            d_ref[...] = (-LR) * ((nm / BC1) / (jnp.sqrt(nv / BC2) + EPS) + WD * w_ref[...])
            nm_ref[...], nv_ref[...] = nm, nv

    ins = [t[k] for t in (g, w, m, v) for k in names]
    outs = pl.pallas_call(body, name=name, out_shape=[_S(w[k].shape) for _ in range(3) for k in names],
                          compiler_params=pltpu.CompilerParams(vmem_limit_bytes=VMEM_LIMIT))(*ins)
    return [{k: outs[t * n + j] for j, k in enumerate(names)} for t in range(3)]


def _me():
    return lax.axis_index("x"), lax.axis_index("y"), lax.axis_index("c")


def _lin(dev):
    return 4 * dev[0] + 2 * dev[1] + dev[2]


def _blk(ref, axis, size, idx):
    nd = len(ref.shape)
    start = idx * size
    if axis == nd - 1 and size % LANE == 0:
        start = pl.multiple_of(start, LANE)
    elif axis == nd - 2 and size % 16 == 0:
        start = pl.multiple_of(start, 16)
    ix = [slice(None)] * nd
    ix[axis] = pl.ds(start, size)
    return ref.at[tuple(ix)]


def _all_gather(shards, axes, name):
    n = len(shards)
    sizes = [s.shape[a] for s, a in zip(shards, axes)]
    out_shapes = [_S(s.shape[:a] + (N_DEV * s.shape[a],) + s.shape[a + 1:], s.dtype) for s, a in zip(shards, axes)]

    def body(*refs):
        ins, outs = refs[:n], refs[n:2 * n]
        send_sems, recv_sems, local_sems = refs[2 * n:]
        x, y, c = _me()
        me, sibling = (x, y, c), (x, y, 1 - c)
        chips = [(1 - x, y), (x, 1 - y), (1 - x, 1 - y)]

        def copy(a, k, block, to, from_input=False):
            dst = _blk(outs[a], axes[a], sizes[a], _lin(block))
            return pltpu.make_async_remote_copy(
                src_ref=ins[a] if from_input else dst, dst_ref=dst, send_sem=send_sems.at[a, k],
                recv_sem=recv_sems.at[a, k], device_id=to, device_id_type=MESH)

        mine = [pltpu.make_async_copy(ins[a], _blk(outs[a], axes[a], sizes[a], _lin(me)), local_sems.at[a]) for a in range(n)]
        for cp in mine:
            cp.start()
        first = []
        for a in range(n):
            first.append(copy(a, 0, me, sibling, True))
            first += [copy(a, 1 + j, me, (*chip, c), True) for j, chip in enumerate(chips)]
        for cp in first:
            cp.start()
        passed = []
        for j, chip in enumerate(chips):
            for a in range(n):
                copy(a, 1 + j, (*chip, c), me).wait_recv()
                cp = copy(a, 4 + j, (*chip, c), sibling)
                cp.start()
                passed.append(cp)
        for a in range(n):
            copy(a, 0, sibling, me).wait_recv()
            for j, chip in enumerate(chips):
                copy(a, 4 + j, (*chip, 1 - c), me).wait_recv()
        for cp in first + passed:
            cp.wait_send()
        for cp in mine:
            cp.wait()

    return pl.pallas_call(
        body, name=name, out_shape=out_shapes, in_specs=[ANY] * n, out_specs=[ANY] * n,
        scratch_shapes=[pltpu.SemaphoreType.DMA((n, 7)), pltpu.SemaphoreType.DMA((n, 7)), pltpu.SemaphoreType.DMA((n,))],
    )(*shards)


HBM_SPEC = pl.BlockSpec(memory_space=pltpu.HBM)
SEM_SPEC = pl.BlockSpec(memory_space=pltpu.SEMAPHORE)
EFFECT = pltpu.SideEffectType.DATAFLOW_SIDE_EFFECTING


def _peers(x, y, c):
    flip = lambda v, f: 1 - v if f else v
    return [(flip(x, k & 4), flip(y, k & 2), flip(c, k & 1)) for k in range(1, N_DEV)]


def _land_shape(mode, s, axis):
    if mode == "gather":
        return s.shape[:axis] + (N_DEV * s.shape[axis],) + s.shape[axis + 1:]
    return (N_DEV,) + s.shape[:axis] + (s.shape[axis] // N_DEV,) + s.shape[axis + 1:]


def _src_view(mode, ref, axis, peer):
    return ref if mode == "gather" else _blk(ref, axis, ref.shape[axis] // N_DEV, peer)


def _dst_view(mode, land, axis, sender):
    return _blk(land, axis, land.shape[axis] // N_DEV, sender) if mode == "gather" else land.at[sender]


def _seven_blocks(mode, land, axis):
    if mode == "gather":
        ix = [slice(None)] * len(land.shape)
        ix[axis] = pl.ds(0, (N_DEV - 1) * (land.shape[axis] // N_DEV))
        return land.at[tuple(ix)]
    return land.at[pl.ds(0, N_DEV - 1)]


def _place_own(mode, srcs, axes, name, after=None):
    n = len(srcs)
    extra, extra_specs = _after(after)

    def body(me_ref, *refs):
        for a in range(n):
            out = refs[n + len(extra) + a]
            out[...] = refs[a][...].reshape(out.shape)

    def at_me(shape, axis):
        return lambda i, me: tuple(me[0] if d == axis else 0 for d in range(len(shape)))

    in_specs, out_specs = [], []
    for s, axis in zip(srcs, axes):
        if mode == "gather":
            in_specs.append(pl.BlockSpec(s.shape, lambda i, me, nd=len(s.shape): (0,) * nd))
            out_specs.append(pl.BlockSpec(s.shape, at_me(s.shape, axis)))
        else:
            blk = s.shape[:axis] + (s.shape[axis] // N_DEV,) + s.shape[axis + 1:]
            in_specs.append(pl.BlockSpec(blk, at_me(blk, axis)))
            out_specs.append(pl.BlockSpec((1,) + blk, at_me((1,) + blk, 0)))
    me = _lin(_me()).astype(jnp.int32).reshape(1)
    return pl.pallas_call(
        body, name=name, out_shape=[_S(_land_shape(mode, s, a), s.dtype) for s, a in zip(srcs, axes)],
        grid_spec=pltpu.PrefetchScalarGridSpec(num_scalar_prefetch=1, grid=(1,), in_specs=in_specs + extra_specs,
                                               out_specs=out_specs),
        compiler_params=_params(1))(me, *srcs, *extra)


def _place_shards(shards, layers, axes, dtypes, name, after=None):
    n = len(shards)
    extra, extra_specs = _after(after)

    def body(me_ref, *refs):
        for a in range(n):
            out = refs[n + len(extra) + a]
            out[...] = refs[a][...].astype(out.dtype)

    in_specs, out_specs, out_shape = [], [], []
    for s, layer, axis, dt in zip(shards, layers, axes, dtypes):
        shape = s.shape if layer is None else s.shape[1:]
        nd = len(shape)
        if layer is None:
            in_specs.append(pl.BlockSpec(shape, lambda i, me, nd=nd: (0,) * nd))
        else:
            in_specs.append(pl.BlockSpec((None,) + shape, lambda i, me, nd=nd, layer=layer: (layer,) + (0,) * nd))
        out_specs.append(pl.BlockSpec(shape, lambda i, me, nd=nd, axis=axis: tuple(me[0] if d == axis else 0 for d in range(nd))))
        out_shape.append(_S(shape[:axis] + (N_DEV * shape[axis],) + shape[axis + 1:], dt))
    me = _lin(_me()).astype(jnp.int32).reshape(1)
    return pl.pallas_call(
        body, name=name, out_shape=out_shape,
        grid_spec=pltpu.PrefetchScalarGridSpec(num_scalar_prefetch=1, grid=(1,), in_specs=in_specs + extra_specs,
                                               out_specs=out_specs),
        compiler_params=_params(1))(me, *shards, *extra)


def _push_start(mode, srcs, lands, axes, name):
    n, ns = len(lands), len(srcs)

    def body(*refs):
        src_refs, land_refs = refs[:ns], refs[ns:ns + n]
        send_sems, recv_sems = refs[ns + n], refs[ns + n + 1]
        token = refs[-1]
        x, y, c = _me()
        me = _lin((x, y, c))
        for a in range(n):
            mine = _dst_view(mode, land_refs[a], axes[a], me)
            for peer in _peers(x, y, c):
                pltpu.make_async_remote_copy(
                    src_ref=_src_view(mode, src_refs[a], axes[a], _lin(peer)) if ns else mine, dst_ref=mine,
                    send_sem=send_sems.at[a], recv_sem=recv_sems.at[a], device_id=peer, device_id_type=MESH).start()
        token[...] = jnp.zeros_like(token)

    hbm = lambda s: pltpu.HBM(s.shape, s.dtype)
    outs = pl.pallas_call(
        body, name=name,
        out_shape=(pltpu.SemaphoreType.DMA((n,)), pltpu.SemaphoreType.DMA((n,)), *[hbm(s) for s in srcs], *[hbm(s) for s in lands],
                   _S((SUB, LANE))),
        in_specs=[HBM_SPEC] * (ns + n),
        out_specs=(SEM_SPEC, SEM_SPEC, *[HBM_SPEC] * (ns + n), pl.BlockSpec(memory_space=pltpu.VMEM)),
        input_output_aliases={i: 2 + i for i in range(ns + n)},
        compiler_params=pltpu.CompilerParams(has_side_effects=EFFECT),
    )(*[pltpu.with_memory_space_constraint(s, pltpu.HBM) for s in list(srcs) + list(lands)])
    return outs[0], outs[1], outs[2:2 + ns], outs[2 + ns:2 + ns + n], outs[-1]


def _push_wait(mode, send_sems, recv_sems, srcs, lands, axes, after, name):
    n, ns = len(lands), len(srcs)
    after = list(after) if isinstance(after, (list, tuple)) else [after]

    def body(*refs):
        land_refs = refs[ns:ns + n]
        send_sems, recv_sems = refs[ns + n], refs[ns + n + 1]
        x, y, c = _me()
        for a in range(n):
            seven = _seven_blocks(mode, land_refs[a], axes[a])
            cp = pltpu.make_async_remote_copy(src_ref=seven, dst_ref=seven, send_sem=send_sems.at[a], recv_sem=recv_sems.at[a],
                                              device_id=(x, y, 1 - c), device_id_type=MESH)
            cp.wait_send()
            cp.wait_recv()

    hbm = lambda s: pltpu.HBM(s.shape, s.dtype)
    outs = pl.pallas_call(
        body, name=name, out_shape=tuple(hbm(s) for s in list(srcs) + list(lands)),
        in_specs=[HBM_SPEC] * (ns + n) + [SEM_SPEC, SEM_SPEC] + [ANY] * len(after), out_specs=tuple([HBM_SPEC] * (ns + n)),
        input_output_aliases={i: i for i in range(ns + n)},
        compiler_params=pltpu.CompilerParams(has_side_effects=EFFECT),
    )(*srcs, *lands, send_sems, recv_sems, *after)
    return outs[ns:]


def _sum_parts(parts):
    n, R, C = parts.shape

    def body(p_ref, o_ref):
        g = p_ref[0]
        for k in range(1, n):
            g = g + p_ref[k]
        o_ref[...] = g

    return pl.pallas_call(body, name="sum_parts", out_shape=_S((R, C)))(parts)


SMALL =['conv_b', 'rg_wa', 'rg_ba', 'rg_wx', 'rg_bx', 'rg_lambda', 's5_a_re', 's5_a_im', 's5_b_re', 's5_b_im',
         's5_c_re', 's5_c_im', 's5_d', 's5_log_step', 's5_b_glu', 'ln1_g', 'ln1_b', 'ple_gate_b', 'ln2_g', 'ln2_b']
WEIGHTS = ['w_in', 'conv_w', 'conv_b', 'rg_wa', 'rg_ba', 'rg_wx', 'rg_bx', 'rg_lambda', 's5_a_re', 's5_a_im', 's5_b_re',
           's5_b_im', 's5_c_re', 's5_c_im', 's5_d', 's5_log_step', 's5_w_glu', 's5_b_glu', 'w_out', 'ln1_g', 'ln1_b',
           'ple_w', 'ple_gate_w', 'ple_gate_b', 'ln2_g', 'ln2_b']
PACK_ROWS_MULT = 64


def _pack(tree, scalar):
    flat = jnp.concatenate([tree[k].reshape(-1) for k in SMALL] + [scalar.reshape(1)])
    rows = -(-flat.shape[0] // (LANE * PACK_ROWS_MULT)) * PACK_ROWS_MULT
    return jnp.pad(flat, (0, rows * LANE - flat.shape[0])).reshape(rows, LANE)


def _unpack(packed, like):
    flat, out, o = packed.reshape(-1), {}, 0
    for k in SMALL:
        n = math.prod(like[k].shape)
        out[k] = flat[o:o + n].reshape(like[k].shape)
        o += n
    return out, flat[o]


class _NoHooks:
    token = None
    first_token = None

    def first_weights(self, full, after):
        return full

    def layer_start(self, i, W, after):
        return W

    def late_weights(self, i, W, after):
        return W

    def post_done(self, i, g):
        return None

    def smalls_done(self, grads, loss):
        self.small = _small_grads(grads, self.res)
        return None

    def w_in_done(self, i, g):
        return None

    def layer_done(self, i, g, dx):
        return None


def _local_grads(x, p, target, W, disc, hooks):
    depth = 2
    saved = []
    for i in range(depth):
        if i > 0:
            W = hooks.layer_start(i, W, x)
        w = W[i]
        z = _inproj_fwd(x, w['w_in'], hooks.token if i == 0 else None)
        hs, *gates = _rg_fwd(z, w['conv_w'], w['conv_b'], w['wa_bd'], w['wx_bd'], w['rg_ba'], w['rg_bx'], w['rg_lambda'], i)
        d = disc[i]
        y0, s_re, s_im = _s5_fwd(z, d['bb_re'], d['bb_im'], d['lb_re'], d['lb_im'], d['c_re'], d['c_im'], w['s5_d'], i)
        W = hooks.late_weights(i, W, y0)
        w = W[i]
        x2, *norms = _post_fwd(x, hs, z, y0, p, w['s5_w_glu'], w['s5_b_glu'], w['w_out'], w['ln1_g'], w['ln1_b'],
                               w['ple_w'], w['ple_gate_w'], w['ple_gate_b'], w['ln2_g'], w['ln2_b'], i)
        saved.append((x, z, hs, gates, y0, s_re, s_im, norms))
        x = x2

    grads = [None] * depth
    dx = target
    loss = None
    token = None
    for i in reversed(range(depth)):
        w, d = W[i], disc[i]
        xin, z, hs, gates, y0, s_re, s_im, (xh1, xh2, q, gt, rstd1, rstd2) = saved[i]
        g = {}
        (dt1, g['ple_w'], g['ple_gate_w'], g['ple_gate_b'], g['ln1_g'], g['ln1_b'], g['ln2_g'], g['ln2_b'], lrow) = _post_bwd_a(
            dx, i == depth - 1, xh2, xh1, rstd2, rstd1, q, gt, p, w['ple_gate_w'], w['ln1_g'], w['ln1_b'],
            w['ln2_g'], w['ln2_b'], i, token)
        if i == depth - 1:
            loss = 0.5 / D_MODEL * jnp.sum(lrow)
        dhs, dy0, dzg, g['w_out'], g['s5_w_glu'], g['s5_b_glu'] = _post_bwd_b(dt1, z, hs, y0, w['w_out'], w['s5_w_glu'],
                                                                           w['s5_b_glu'], i)
        (dzu, g['bb_re'], g['bb_im'], g['lb_re'], g['lb_im'], g['c_re'], g['c_im'], g['s5_d']) = _s5_bwd(
            dy0, z, s_re, s_im, d['bb_re'], d['bb_im'], d['lb_re'], d['lb_im'], d['c_re'], d['c_im'], w['s5_d'], i,
            hooks.post_done(i, g))
        (dzx, g['conv_w'], g['conv_b'], g['wa_bd'], g['wx_bd'], g['rg_ba'], g['rg_bx'], g['rg_lambda']) = _rg_bwd(
            dhs, z, hs, gates, w['conv_w'], w['wa_bd'], w['wx_bd'], w['rg_lambda'], i)
        if i == 0:
            g['w_in'] = _inproj_bwd_dw(xin, dzx, dzg, dzu, hooks.smalls_done([g, grads[1]], loss))
            dx = _inproj_bwd_dx(dt1, dzx, dzg, dzu, w['w_in'], hooks.w_in_done(i, g))
        else:
            dx, g['w_in'] = _inproj_bwd(dt1, xin, dzx, dzg, dzu, w['w_in'])
        grads[i] = g
        token = hooks.layer_done(i, g, dx)
    return loss, dx, grads


def _s5_layouts_fwd(s5_a_re, s5_a_im, s5_log_step, s5_b_re, s5_b_im, s5_c_re, s5_c_im, token=None):
    depth = s5_a_re.shape[0]
    ar, ai = s5_a_re.reshape(depth * 24, S5_P), s5_a_im.reshape(depth * 24, S5_P)
    ls = s5_log_step.reshape(depth * 24, 1)
    lr, li, cr, ci = _s5_disc_fwd(ar, ai, ls, token)
    per_group = lambda a: a.reshape(depth * 24, 1, S5_P)
    as_c = lambda b: jnp.swapaxes(b, 2, 3).reshape(depth * 24, S5_H, S5_P)
    res = (ar, ai, ls, per_group(cr), per_group(ci), as_c(s5_b_re), as_c(s5_b_im))
    bbr, bbi = _s5_bscale_fwd(*res[3:])
    tiles = lambda a: a.reshape(depth * N_S5_T, S5_GT, S5_H, S5_P)
    rows = lambda a: a.reshape(depth * N_S5_T, S5_GT, S5_P)
    disc = dict(bb_re=tiles(bbr), bb_im=tiles(bbi), lb_re=rows(lr), lb_im=rows(li), c_re=tiles(s5_c_re), c_im=tiles(s5_c_im))
    return [disc] * depth, res


def _s5_layouts_bwd(grads, res):
    ar, ai, ls, cr, ci, br, bi = res
    depth = len(grads)
    stack = lambda k, shape: jnp.stack([g[k] for g in grads]).reshape(shape)
    groups, shape_c = (depth * 24, S5_H, S5_P), (depth, 24, S5_H, S5_P)
    dbr, dbi, dcr, dci = _s5_bscale_bwd(cr, ci, br, bi, stack('bb_re', groups), stack('bb_im', groups))
    gp = (depth * 24, S5_P)
    dar, dai, dls = _s5_disc_bwd(ar, ai, ls, stack('lb_re', gp), stack('lb_im', gp), dcr.reshape(gp), dci.reshape(gp))
    return dict(
        s5_a_re=dar.reshape(depth, 24, S5_P), s5_a_im=dai.reshape(depth, 24, S5_P), s5_log_step=dls.reshape(depth, 24),
        s5_b_re=jnp.swapaxes(dbr.reshape(shape_c), 2, 3), s5_b_im=jnp.swapaxes(dbi.reshape(shape_c), 2, 3),
        s5_c_re=stack('c_re', shape_c), s5_c_im=stack('c_im', shape_c))


LATE = ('w_out', 'ple_w', 'ple_gate_w', 's5_w_glu')


ROWS = ('conv_b', 'rg_ba', 'rg_bx', 'rg_lambda', 's5_d', 's5_b_glu', 'ln1_g', 'ln1_b', 'ple_gate_b', 'ln2_g', 'ln2_b')


def _shared_weights(full):
    depth = full['conv_b'].shape[0]
    shared = {k: full[k].reshape(depth, 1, -1) for k in ROWS}
    shared.update(conv_w=full['conv_w'], wa_bd=full['rg_wa'], wx_bd=full['rg_wx'])
    return shared


def _layer_weights(full, shared, i):
    return dict(shared, w_in=full['w_in'][i])


class _AllLocal(_NoHooks):
    def __init__(self, full):
        self.full = full

    def late_weights(self, i, W, after):
        W[i].update({k: self.full[k][i] for k in LATE})
        return W


def _full_grads(full, x, p, target, hooks=None):
    hooks = hooks or _AllLocal(full)
    disc, res = _s5_layouts_fwd(full['s5_a_re'], full['s5_a_im'], full['s5_log_step'], full['s5_b_re'], full['s5_b_im'],
                                full['s5_c_re'], full['s5_c_im'], hooks.first_token)
    full = hooks.first_weights(full, disc[-1]['bb_im'])
    shared = _shared_weights(full)
    W = [_layer_weights(full, shared, i) for i in range(2)]
    hooks.res = res
    loss, gx, grads = _local_grads(x, p, target, W, disc, hooks)
    out = dict(hooks.small)
    for k in SHARD_AXIS:
        out[k] = [g[k] for g in grads]
    return loss, gx, out


def _small_grads(grads, res):
    stack = lambda f: jnp.stack([f(g) for g in grads])
    out = _s5_layouts_bwd(grads, res)
    out['conv_w'] = stack(lambda g: g['conv_w'])
    for k in ('conv_b', 'rg_ba', 'rg_bx', 'rg_lambda', 's5_b_glu', 'ln1_g', 'ln1_b', 'ple_gate_b', 'ln2_g', 'ln2_b'):
        out[k] = stack(lambda g: g[k][0])
    out['s5_d'] = stack(lambda g: g['s5_d'][0]).reshape(2, 24, 16)
    out['rg_wa'] = stack(lambda g: g['wa_bd'])
    out['rg_wx'] = stack(lambda g: g['wx_bd'])
    return out


SHARD_AXIS = {'w_in': 2, 'w_out': 1, 'ple_w': 2, 'ple_gate_w': 1, 's5_w_glu': 1}


def kernel(x, p, w_in, conv_w, conv_b, rg_wa, rg_ba, rg_wx, rg_bx, rg_lambda, s5_a_re, s5_a_im, s5_b_re, s5_b_im, s5_c_re, s5_c_im, s5_d, s5_log_step, s5_w_glu, s5_b_glu, w_out, ln1_g, ln1_b, ple_w, ple_gate_w, ple_gate_b, ln2_g, ln2_b, loss_target, m_w_in, m_conv_w, m_conv_b, m_rg_wa, m_rg_ba, m_rg_wx, m_rg_bx, m_rg_lambda, m_s5_a_re, m_s5_a_im, m_s5_b_re, m_s5_b_im, m_s5_c_re, m_s5_c_im, m_s5_d, m_s5_log_step, m_s5_w_glu, m_s5_b_glu, m_w_out, m_ln1_g, m_ln1_b, m_ple_w, m_ple_gate_w, m_ple_gate_b, m_ln2_g, m_ln2_b, v_w_in, v_conv_w, v_conv_b, v_rg_wa, v_rg_ba, v_rg_wx, v_rg_bx, v_rg_lambda, v_s5_a_re, v_s5_a_im, v_s5_b_re, v_s5_b_im, v_s5_c_re, v_s5_c_im, v_s5_d, v_s5_log_step, v_s5_w_glu, v_s5_b_glu, v_w_out, v_ln1_g, v_ln1_b, v_ple_w, v_ple_gate_w, v_ple_gate_b, v_ln2_g, v_ln2_b):
    local = dict(locals())
    w = {k: local[k] for k in WEIGHTS}
    mom = {k: local['m_' + k] for k in WEIGHTS}
    var = {k: local['v_' + k] for k in WEIGHTS}

    big = list(SHARD_AXIS)
    late_axes = [SHARD_AXIS[k] - 1 for k in LATE]
    pushed = {}

    def push_weights(key, names, layers, axes, after):
        shards = [w[k] if layer is not None else w[k][None] for k, layer in zip(names, layers)]
        dtypes = [WIRE if k in big else w[k].dtype for k in names]
        lands = _place_shards(shards, layers, axes, dtypes, "place_weights_" + key, after)
        pushed[key] = _push_start("gather", [], lands, axes, "push_weights_" + key)
        return pushed[key][4]

    def await_weights(key, axes, after):
        s = pushed[key]
        return _push_wait("gather", s[0], s[1], s[2], s[3], axes, after, "await_weights_" + key)

    token = push_weights("first", ['w_in', 'conv_w'], [0, None], [1, 0], None)
    token = push_weights("l0", LATE, [0] * len(LATE), late_axes, token)
    push_weights("l1", ['w_in'] + list(LATE), [1] * (1 + len(LATE)), [1] + late_axes, token)

    def push_grads(key, g, names, axes):
        srcs = [g[k] for k in names]
        pushed[key] = _push_start("scatter", srcs, _place_own("scatter", srcs, axes, "place_grads_" + key), axes,
                                  "push_grads_" + key)
        return pushed[key][4]

    def await_grads(key, axes, after):
        s = pushed[key]
        return _push_wait("scatter", s[0], s[1], s[2], s[3], axes, after, "await_grads_" + key)

    class Overlap(_NoHooks):
        token = pushed["l1"][4]
        first_token = token

        def first_weights(self, full, after):
            w_in0, conv = await_weights("first", [1, 0], after)
            return dict(full, w_in=[w_in0, None], conv_w=jnp.moveaxis(conv, 0, 2).reshape(2, 4, RG_W))

        def late_weights(self, i, W, after):
            if i == 0:
                W[0].update(zip(LATE, await_weights("l0", late_axes, after)))
            return W

        def layer_start(self, i, W, after):
            lands = await_weights("l1", [1] + late_axes, after)
            W[1].update(zip(LATE, lands[1:]), w_in=lands[0])
            return W

        def post_done(self, i, g):
            return push_grads("late0", g, LATE, late_axes) if i == 0 else None

        def smalls_done(self, grads, loss):
            super().smalls_done(grads, loss)
            conv = jnp.moveaxis(self.small['conv_w'].reshape(2, 4, N_DEV, RG_W // N_DEV), 2, 0)
            self.packed = _pack(self.small, loss)
            return push_grads("small", dict(conv_w=conv.reshape(N_DEV, 8, RG_W // N_DEV), small=self.packed),
                              ['conv_w', 'small'], [0, 0])

        def w_in_done(self, i, g):
            return push_grads("w_in0", g, ['w_in'], [0])

        def layer_done(self, i, g, dx):
            return push_grads("all1", g, ['w_in'] + list(LATE), [0] + late_axes) if i == 1 else None

    hooks = Overlap()
    _, grad_x, g = _full_grads(dict(w), x[0], p, loss_target[0], hooks)

    recv1 = dict(zip(['w_in'] + list(LATE), await_grads("all1", [0] + late_axes, grad_x)))
    recv0 = dict(zip(LATE, await_grads("late0", late_axes, grad_x)))
    outs = {}

    def update(k, parts):
        shard = w[k].shape
        c = shard[-1]
        two = lambda a: a.reshape(-1, c)
        res = _adamw([r.reshape(N_DEV, -1, c) for r in parts], two(w[k]), two(mom[k]), two(var[k]))
        outs[k] = [o.reshape(shard) for o in res]

    for k in LATE:
        update(k, [recv0[k], recv1[k]])
    done = [outs[k][1] for k in LATE]
    conv_parts, small_parts = await_grads("small", [0, 0], done)

    rows = hooks.packed.shape[0] // N_DEV
    mine = _sum_parts(small_parts.reshape(N_DEV, rows, LANE))
    gathered = _all_gather([mine], [0], "gather_small_grads")[0]
    w_in0, = await_grads("w_in0", [0], gathered)
    update('w_in', [w_in0, recv1['w_in']])
    update('conv_w', [conv_parts])
    summed, loss = _unpack(gathered, w)
    narrow = ['s5_b_re', 's5_b_im']
    for names, name in ((narrow, "adamw_s5_b"), ([k for k in SMALL if k not in narrow], "adamw_small")):
        delta, new_m, new_v = _adamw_natural(names, summed, w, mom, var, name)
        for k in names:
            outs[k] = [summed[k], delta[k], new_m[k], new_v[k]]

    res = [loss, grad_x[None]]
    for j in range(4):
        res += [outs[k][j] for k in WEIGHTS]
    return tuple(res)
```

```python
import math

import jax
import jax.numpy as jnp
from jax import lax
from jax.experimental import pallas as pl
from jax.experimental.pallas import tpu as pltpu

F32 = jnp.float32
MXU = jnp.bfloat16
WIRE = jnp.bfloat16

N_DEV = 8
D_MODEL = 1024
PLE_D = 256
RG_W = 640
S5_W = 384
S5_P = 64
S5_N = 24 * S5_P
Z_W = 2 * RG_W + 2 * S5_W
C_RGG = RG_W
C_S5U = 2 * RG_W
C_S5G = 2 * RG_W + S5_W
LANE = 128
N_RG_T = RG_W // LANE
N_S5_T = S5_W // LANE
W_BLK = Z_W // N_DEV
ALPHA = (2.0 * 2) ** 0.25
LN_EPS = 1e-5
RG_C = 8.0
LR, B1, B2, EPS, WD, STEP = 0.001, 0.9, 0.999, 1e-08, 0.01, 10
BC1 = 1.0 - B1 ** STEP
BC2 = 1.0 - B2 ** STEP
RC = 512
RC_RG = 1024
TM = 256
TM_MM = 1024
VMEM_LIMIT = 56 * 1024 * 1024

MESH = pl.DeviceIdType.MESH
ANY = pl.BlockSpec(memory_space=pl.ANY)


def _params(n_grid_axes, vmem=VMEM_LIMIT):
    return pltpu.CompilerParams(dimension_semantics=("arbitrary",) * n_grid_axes, vmem_limit_bytes=vmem)


def _S(shape, dtype=F32):
    return jax.ShapeDtypeStruct(tuple(shape), dtype)


def _sigmoid(x):
    return 0.5 * jnp.tanh(0.5 * x) + 0.5


def _silu_and_grad(x):
    s = _sigmoid(x)
    return x * s, s * (1.0 + x * (1.0 - s))


_GELU_C = math.sqrt(2.0 / math.pi)


def _gelu(x):
    return 0.5 * x * (1.0 + jnp.tanh(_GELU_C * (x + 0.044715 * (x * x * x))))


def _gelu_grad(x):
    th = jnp.tanh(_GELU_C * (x + 0.044715 * (x * x * x)))
    return 0.5 * (1.0 + th) + 0.5 * x * (1.0 - th * th) * (_GELU_C * (1.0 + 3.0 * 0.044715 * (x * x)))


def _mm(a, b):
    return jnp.dot(a.astype(MXU), b.astype(MXU), preferred_element_type=F32)


def _mm_nt(a, b):
    return lax.dot_general(a.astype(MXU), b.astype(MXU), (((1,), (1,)), ((), ())), preferred_element_type=F32)


def _mm_tn(a, b):
    return lax.dot_general(a.astype(MXU), b.astype(MXU), (((0,), (0,)), ((), ())), preferred_element_type=F32)


def _ln_fwd(t, g, b):
    mu = jnp.mean(t, axis=-1, keepdims=True)
    tc = t - mu
    var = jnp.mean(tc * tc, axis=-1, keepdims=True)
    rstd = lax.rsqrt(var + LN_EPS)
    xhat = tc * rstd
    return xhat * g + b, xhat, rstd


def _ln_bwd(dy, xhat, rstd, g):
    dxh = dy * g
    m1 = jnp.mean(dxh, axis=-1, keepdims=True)
    m2 = jnp.mean(dxh * xhat, axis=-1, keepdims=True)
    return rstd * (dxh - m1 - xhat * m2)


def _colsum(a):
    return jnp.sum(a, axis=0, keepdims=True)


def _up(x, d, rows, fill):
    n = x.shape[0]
    return jnp.where(rows < n - d, pltpu.roll(x, n - d, 0), fill)


SUB = 8
TILE_STEPS = (1, 2, 4)


def _r8(width):
    return lax.broadcasted_iota(jnp.int32, (SUB, width), 0)


def _scan_real(a, u, carry, reverse=False):
    r8 = _r8(a.shape[1])
    n = a.shape[0] // SUB
    outs = [None] * n
    for k in (reversed(range(n)) if reverse else range(n)):
        A, U = a[SUB * k:SUB * k + SUB], u[SUB * k:SUB * k + SUB]
        for d in TILE_STEPS:
            m = (r8 < SUB - d) if reverse else (r8 >= d)
            sh = SUB - d if reverse else d
            U = A * jnp.where(m, pltpu.roll(U, sh, 0), 0.0) + U
            A = A * jnp.where(m, pltpu.roll(A, sh, 0), 1.0)
        h = A * carry + U
        outs[k] = h
        carry = h[0:1] if reverse else h[SUB - 1:SUB]
    return jnp.concatenate(outs, axis=0), carry


def _tile_powers(lr, li, reverse=False):
    width = lr.shape[1]
    r8 = _r8(width)
    steps = []
    pr, pi = lr, li
    er, ei = jnp.broadcast_to(lr, (SUB, width)), jnp.broadcast_to(li, (SUB, width))
    for d in TILE_STEPS:
        m = (r8 < SUB - d) if reverse else (r8 >= d)
        sh = SUB - d if reverse else d
        steps.append((sh, jnp.where(m, pr, 0.0), jnp.where(m, pi, 0.0)))
        er, ei = _cmul(er, ei, jnp.where(m, pltpu.roll(er, sh, 0), 1.0), jnp.where(m, pltpu.roll(ei, sh, 0), 0.0))
        pr, pi = _cmul(pr, pi, pr, pi)
    return steps, (er, ei)


def _scan_lti(xr, xi, carry, steps, e, reverse=False):
    er, ei = e
    kr, ki = carry
    n = xr.shape[0] // SUB
    outr, outi = [None] * n, [None] * n
    for k in (reversed(range(n)) if reverse else range(n)):
        sr, si = xr[SUB * k:SUB * k + SUB], xi[SUB * k:SUB * k + SUB]
        for sh, pr, pi in steps:
            shr, shi = pltpu.roll(sr, sh, 0), pltpu.roll(si, sh, 0)
            sr, si = sr + (pr * shr - pi * shi), si + (pr * shi + pi * shr)
        sr = sr + (er * kr - ei * ki)
        si = si + (er * ki + ei * kr)
        outr[k], outi[k] = sr, si
        kr, ki = (sr[0:1], si[0:1]) if reverse else (sr[SUB - 1:SUB], si[SUB - 1:SUB])
    return jnp.concatenate(outr, axis=0), jnp.concatenate(outi, axis=0), (kr, ki)


def _halo(ref, c, r0):
    rp = pl.multiple_of(jnp.maximum(r0 - 8, 0), 8)
    return jnp.where(c > 0, ref[pl.ds(rp, 8), :], 0.0)


def _conv_taps(xe):
    return [pltpu.roll(xe, 3, 0)[8:, :], pltpu.roll(xe, 2, 0)[8:, :], pltpu.roll(xe, 1, 0)[8:, :], xe[8:, :]]


def _rg_gates(h, wa, wx, ba, bx, sp):
    r = _sigmoid(_mm(h, wa) + ba)
    i = _sigmoid(_mm(h, wx) + bx)
    log_a = (-RG_C) * r * sp
    a = jnp.exp(log_a)
    mult = jnp.sqrt(-jnp.tanh(log_a) * (a * a + 1.0))
    return r, i, a, mult


def _softplus(y):
    return jnp.maximum(y, 0.0) + jnp.log1p(jnp.exp(-jnp.abs(y)))


def _after(token):
    return ([], []) if token is None else ([token], [ANY])


def _inproj_fwd(x, w_in, token=None):
    L = x.shape[0]

    def body(x_ref, w_ref, *rest):
        rest[-1][...] = _mm(x_ref[...], w_ref[...])

    extra, extra_specs = _after(token)
    tm = min(TM_MM, L)
    return pl.pallas_call(
        body, name="inproj_fwd", grid=(L // tm,),
        in_specs=[pl.BlockSpec((tm, D_MODEL), lambda i: (i, 0)), pl.BlockSpec((D_MODEL, Z_W), lambda i: (0, 0))] + extra_specs,
        out_specs=pl.BlockSpec((tm, Z_W), lambda i: (i, 0)),
        out_shape=_S((L, Z_W)), compiler_params=_params(1))(x, w_in, *extra)


def _inproj_bwd(dt1, x, dzx, dzg, dzu, w_in):
    L = x.shape[0]

    def body(dt1_ref, x_ref, dzx_ref, dzg_ref, dzu_ref, w_ref, dx_ref, dw_ref, acc_ref):
        @pl.when(pl.program_id(0) == 0)
        def _():
            acc_ref[...] = jnp.zeros_like(acc_ref)
        dzg = dzg_ref[...]
        dz = jnp.concatenate([dzx_ref[...], dzg[:, :RG_W], dzu_ref[...], dzg[:, RG_W:]], axis=1).astype(MXU)
        xb = x_ref[...].astype(MXU)
        dx_ref[...] = ALPHA * dt1_ref[...] + _mm_nt(dz, w_ref[...])
        for j in range(N_DEV):
            acc_ref[j] += _mm_tn(xb, dz[:, j * W_BLK:(j + 1) * W_BLK])

        @pl.when(pl.program_id(0) == L // TM - 1)
        def _():
            dw_ref[...] = acc_ref[...].astype(WIRE)

    row = lambda w: pl.BlockSpec((TM, w), lambda i: (i, 0))
    wspec = pl.BlockSpec((N_DEV, D_MODEL, W_BLK), lambda i: (0, 0, 0))
    return pl.pallas_call(
        body, name="inproj_bwd", grid=(L // TM,),
        in_specs=[row(D_MODEL), row(D_MODEL), row(RG_W), row(D_MODEL), row(S5_W),
                  pl.BlockSpec((D_MODEL, Z_W), lambda i: (0, 0))],
        out_specs=[row(D_MODEL), wspec],
        out_shape=[_S((L, D_MODEL)), _S((N_DEV, D_MODEL, W_BLK), WIRE)],
        scratch_shapes=[pltpu.VMEM((N_DEV, D_MODEL, W_BLK), F32)],
        compiler_params=_params(1))(dt1, x, dzx, dzg, dzu, w_in)


TM2 = 512


def _dz_block(dzx_ref, dzg_ref, dzu_ref):
    dzg = dzg_ref[...]
    return jnp.concatenate([dzx_ref[...], dzg[:, :RG_W], dzu_ref[...], dzg[:, RG_W:]], axis=1).astype(MXU)


def _inproj_bwd_dw(x, dzx, dzg, dzu, token=None):
    L = x.shape[0]
    extra, extra_specs = _after(token)

    def body(x_ref, dzx_ref, dzg_ref, dzu_ref, *rest):
        dw_ref, acc_ref = rest[len(extra):]
        @pl.when(pl.program_id(0) == 0)
        def _():
            acc_ref[...] = jnp.zeros_like(acc_ref)
        dz = _dz_block(dzx_ref, dzg_ref, dzu_ref)
        xb = x_ref[...].astype(MXU)
        for j in range(N_DEV):
            acc_ref[j] += _mm_tn(xb, dz[:, j * W_BLK:(j + 1) * W_BLK])

        @pl.when(pl.program_id(0) == L // TM2 - 1)
        def _():
            dw_ref[...] = acc_ref[...].astype(WIRE)

    row = lambda w: pl.BlockSpec((TM2, w), lambda i: (i, 0))
    wspec = pl.BlockSpec((N_DEV, D_MODEL, W_BLK), lambda i: (0, 0, 0))
    return pl.pallas_call(
        body, name="inproj_bwd_dw", grid=(L // TM2,),
        in_specs=[row(D_MODEL), row(RG_W), row(D_MODEL), row(S5_W)] + extra_specs, out_specs=wspec,
        out_shape=_S((N_DEV, D_MODEL, W_BLK), WIRE), scratch_shapes=[pltpu.VMEM((N_DEV, D_MODEL, W_BLK), F32)],
        compiler_params=_params(1))(x, dzx, dzg, dzu, *extra)


def _inproj_bwd_dx(dt1, dzx, dzg, dzu, w_in, token=None):
    L = dt1.shape[0]
    extra, extra_specs = _after(token)

    def body(dt1_ref, dzx_ref, dzg_ref, dzu_ref, w_ref, *rest):
        rest[-1][...] = ALPHA * dt1_ref[...] + _mm_nt(_dz_block(dzx_ref, dzg_ref, dzu_ref), w_ref[...])

    tm = min(TM_MM, L)
    row = lambda w: pl.BlockSpec((tm, w), lambda i: (i, 0))
    return pl.pallas_call(
        body, name="inproj_bwd_dx", grid=(L // tm,),
        in_specs=[row(D_MODEL), row(RG_W), row(D_MODEL), row(S5_W), _full((D_MODEL, Z_W))] + extra_specs,
        out_specs=row(D_MODEL), out_shape=_S((L, D_MODEL)), compiler_params=_params(1))(dt1, dzx, dzg, dzu, w_in, *extra)


def _rg_specs(layer):
    tile = lambda rows: pl.BlockSpec((rows, LANE), lambda c: (0, c))
    ptile = lambda rows: pl.BlockSpec((None, rows, LANE), lambda c: (layer, 0, c))
    pheads = pl.BlockSpec((None, 2, RG_HD, RG_HD), lambda c: (layer, c, 0, 0))
    return tile, ptile, pheads, pl.BlockSpec((2, RG_HD, RG_HD), lambda c: (c, 0, 0))


RG_HD = 64


def _bd2(w):
    z = jnp.zeros((RG_HD, RG_HD), w.dtype)
    return jnp.concatenate([jnp.concatenate([w[0], z], axis=1), jnp.concatenate([z, w[1]], axis=1)], axis=0)


def _bd2_diag(m):
    return jnp.stack([m[:RG_HD, :RG_HD], m[RG_HD:, RG_HD:]])


def _rg_fwd(z, cw, cb, wa_bd, wx_bd, ba, bx, lam, layer):
    L = z.shape[0]
    RC = min(RC_RG, L)

    def body(x_ref, cw_ref, cb_ref, wa_ref, wx_ref, ba_ref, bx_ref, lam_ref, hs_ref, *saved):
        w, b = cw_ref[...], cb_ref[...]
        wa, wx, ba_, bx_ = _bd2(wa_ref[...]).astype(MXU), _bd2(wx_ref[...]).astype(MXU), ba_ref[...], bx_ref[...]
        sp = _softplus(-lam_ref[...])

        def step(c, carry):
            r0 = pl.multiple_of(c * RC, RC)
            xe = jnp.concatenate([_halo(x_ref, c, r0), x_ref[pl.ds(r0, RC), :]], axis=0)
            t = _conv_taps(xe)
            h = t[0] * w[0:1] + t[1] * w[1:2] + t[2] * w[2:3] + t[3] * w[3:4] + b
            r, i, a, mult = _rg_gates(h, wa, wx, ba_, bx_, sp)
            hs, carry = _scan_real(a, mult * (i * h), carry)
            hs_ref[pl.ds(r0, RC), :] = hs
            for ref, val in zip(saved, (h, r, i, a, mult)):
                ref[pl.ds(r0, RC), :] = val
            return carry

        lax.fori_loop(0, L // RC, step, jnp.zeros((1, LANE), F32))

    tile, ptile, pheads, _ = _rg_specs(layer)
    return pl.pallas_call(
        body, name="rg_fwd", grid=(N_RG_T,),
        in_specs=[tile(L), ptile(4), ptile(1), pheads, pheads, ptile(1), ptile(1), ptile(1)],
        out_specs=[tile(L)] * 6, out_shape=[_S((L, RG_W))] * 6, compiler_params=_params(1))(
            z, cw, cb, wa_bd, wx_bd, ba, bx, lam)


def _rg_bwd(dhs, z, hs, gates, cw, wa_bd, wx_bd, lam, layer):
    L = z.shape[0]
    RC = min(RC_RG, L)

    def body(g_ref, x_ref, hs_ref, h_ref, r_ref, i_ref, a_ref, mult_ref, cw_ref, wa_ref, wx_ref, lam_ref,
             dx_ref, dcw_ref, dcb_ref, dwa_out, dwx_out, dba_ref, dbx_ref, dlam_ref, dwa_ref, dwx_ref):
        w = cw_ref[...]
        wa, wx = _bd2(wa_ref[...]).astype(MXU), _bd2(wx_ref[...]).astype(MXU)
        lam = lam_ref[...]
        sp = _softplus(-lam)
        rows = lax.broadcasted_iota(jnp.int32, (RC, LANE), 0)
        for ref in (dcw_ref, dcb_ref, dwa_ref, dwx_ref, dba_ref, dbx_ref, dlam_ref):
            ref[...] = jnp.zeros_like(ref)
        nch = L // RC

        def step(k, carry):
            cin, nxt = carry
            c = nch - 1 - k
            r0 = pl.multiple_of(c * RC, RC)
            xe = jnp.concatenate([_halo(x_ref, c, r0), x_ref[pl.ds(r0, RC), :]], axis=0)
            t = _conv_taps(xe)
            h, r, i, a, mult = (ref[pl.ds(r0, RC), :] for ref in (h_ref, r_ref, i_ref, a_ref, mult_ref))
            hs_e = jnp.concatenate([_halo(hs_ref, c, r0), hs_ref[pl.ds(r0, RC), :]], axis=0)
            hs_prev = pltpu.roll(hs_e, 1, 0)[8:, :]
            g = g_ref[pl.ds(r0, RC), :]
            cc, cin_new = _scan_real(a, a * g, cin, reverse=True)
            dh = g + _up(cc, 1, rows, cin)
            ih = i * h
            dlog_a = dh * hs_prev * a - (dh * ih) * (a * a) / mult
            di = dh * mult * h
            dhin = dh * mult * i
            dr = dlog_a * ((-RG_C) * sp)
            dlam_ref[...] += _colsum(dlog_a * r)
            dra = dr * r * (1.0 - r)
            dia = di * i * (1.0 - i)
            dwa_ref[...] += _mm_tn(h, dra)
            dwx_ref[...] += _mm_tn(h, dia)
            dba_ref[...] += _colsum(dra)
            dbx_ref[...] += _colsum(dia)
            dhin = dhin + _mm_nt(dra, wa) + _mm_nt(dia, wx)
            de = jnp.concatenate([dhin, nxt], axis=0)
            n = RC + 8
            dx = (dhin * w[3:4] + pltpu.roll(de, n - 1, 0)[:RC, :] * w[2:3]
                  + pltpu.roll(de, n - 2, 0)[:RC, :] * w[1:2] + pltpu.roll(de, n - 3, 0)[:RC, :] * w[0:1])
            dx_ref[pl.ds(r0, RC), :] = dx
            for kk in range(4):
                dcw_ref[kk:kk + 1, :] += _colsum(dhin * t[kk])
            dcb_ref[...] += _colsum(dhin)
            return cin_new, dhin[0:8, :]

        lax.fori_loop(0, nch, step, (jnp.zeros((1, LANE), F32), jnp.zeros((8, LANE), F32)))
        dlam_ref[...] = dlam_ref[...] * (RG_C * _sigmoid(-lam))
        dwa_out[...], dwx_out[...] = _bd2_diag(dwa_ref[...]), _bd2_diag(dwx_ref[...])

    tile, ptile, pheads, gheads = _rg_specs(layer)
    heads = _S((2 * N_RG_T, RG_HD, RG_HD))
    return pl.pallas_call(
        body, name="rg_bwd", grid=(N_RG_T,),
        in_specs=[tile(L)] * 8 + [ptile(4), pheads, pheads, ptile(1)],
        out_specs=[tile(L), tile(4), tile(1), gheads, gheads, tile(1), tile(1), tile(1)],
        out_shape=[_S((L, RG_W)), _S((4, RG_W)), _S((1, RG_W)), heads, heads, _S((1, RG_W)), _S((1, RG_W)), _S((1, RG_W))],
        scratch_shapes=[pltpu.VMEM((LANE, LANE), F32), pltpu.VMEM((LANE, LANE), F32)],
        compiler_params=_params(1))(dhs, z, hs, *gates, cw, wa_bd, wx_bd, lam)


def _cmul(ar, ai, br, bi):
    return ar * br - ai * bi, ar * bi + ai * br


S5_TW = S5_N // N_S5_T


S5_H = 16
S5_GT = LANE // S5_H


def _s5_specs(L, layer):
    in_tile = pl.BlockSpec((L, LANE), lambda t: (0, t))
    st = pl.BlockSpec((L, S5_TW), lambda t: (0, t))
    pg = pl.BlockSpec((None, S5_GT, S5_H, S5_P), lambda t: (layer * N_S5_T + t, 0, 0, 0))
    plb = pl.BlockSpec((None, S5_GT, S5_P), lambda t: (layer * N_S5_T + t, 0, 0))
    gg = pl.BlockSpec((None, S5_GT, S5_H, S5_P), lambda t: (t, 0, 0, 0))
    glb = pl.BlockSpec((None, S5_GT, S5_P), lambda t: (t, 0, 0))
    dv = pl.BlockSpec((1, LANE), lambda t: (0, t))
    return in_tile, st, pg, plb, gg, glb, dv


def _bd8(blocks):
    rows = []
    for g in range(S5_GT):
        pieces = [blocks[g]]
        if g:
            pieces.insert(0, jnp.zeros((S5_H, S5_P * g), blocks.dtype))
        if g < S5_GT - 1:
            pieces.append(jnp.zeros((S5_H, S5_P * (S5_GT - 1 - g)), blocks.dtype))
        rows.append(jnp.concatenate(pieces, axis=1))
    return jnp.concatenate(rows, axis=0)


def _bd8_diag(m):
    return jnp.stack([m[S5_H * g:S5_H * (g + 1), S5_P * g:S5_P * (g + 1)] for g in range(S5_GT)])


def _row8(v):
    return jnp.concatenate([v[g:g + 1] for g in range(S5_GT)], axis=1)


def _row8_split(r):
    return jnp.concatenate([r[:, S5_P * g:S5_P * (g + 1)] for g in range(S5_GT)], axis=0)


def _layer_row_tile(layer):
    return pl.BlockSpec((None, 1, LANE), lambda t: (layer, 0, t))


def _s5_fwd(z, bb_re, bb_im, lb_re, lb_im, c_re, c_im, dvec, layer):
    L = z.shape[0]

    def body(u_ref, bbr_ref, bbi_ref, lr_ref, li_ref, cr_ref, ci_ref, d_ref, y_ref, sr_ref, si_ref):
        bbr, bbi = _bd8(bbr_ref[...]).astype(MXU), _bd8(bbi_ref[...]).astype(MXU)
        cr, ci = _bd8(cr_ref[...]).astype(MXU), _bd8(ci_ref[...]).astype(MXU)
        dv = d_ref[...]
        steps, e = _tile_powers(_row8(lr_ref[...]), _row8(li_ref[...]))

        def step(c, carry):
            r0 = pl.multiple_of(c * RC, RC)
            u = u_ref[pl.ds(r0, RC), :]
            ub = u.astype(MXU)
            sr = jnp.dot(ub, bbr, preferred_element_type=F32)
            si = jnp.dot(ub, bbi, preferred_element_type=F32)
            sr, si, carry = _scan_lti(sr, si, carry, steps, e)
            sr_ref[pl.ds(r0, RC), :] = sr
            si_ref[pl.ds(r0, RC), :] = si
            y_ref[pl.ds(r0, RC), :] = dv * u + (_mm_nt(sr, cr) - _mm_nt(si, ci))
            return carry

        zero = jnp.zeros((1, S5_TW), F32)
        lax.fori_loop(0, L // RC, step, (zero, zero))

    in_tile, st, pg, plb, _, _, _ = _s5_specs(L, layer)
    u_tile = pl.BlockSpec((L, LANE), lambda t: (0, C_S5U // LANE + t))
    return pl.pallas_call(
        body, name="s5_fwd", grid=(N_S5_T,),
        in_specs=[u_tile, pg, pg, plb, plb, pg, pg, _layer_row_tile(layer)],
        out_specs=[in_tile, st, st],
        out_shape=[_S((L, S5_W)), _S((L, S5_N)), _S((L, S5_N))],
        compiler_params=_params(1))(z, bb_re, bb_im, lb_re, lb_im, c_re, c_im, dvec)


def _s5_bwd(dy0, z, s_re, s_im, bb_re, bb_im, lb_re, lb_im, c_re, c_im, dvec, layer, token=None):
    L = z.shape[0]
    extra, extra_specs = _after(token)

    def body(dy_ref, u_ref, sr_ref, si_ref, bbr_ref, bbi_ref, lr_ref, li_ref, cr_ref, ci_ref, d_ref, *rest):
        (du_ref, dbbr_out, dbbi_out, dlr_out, dli_out, dcr_out, dci_out, dd_ref,
         dbbr_ref, dbbi_ref, dcr_ref, dci_ref, dlr_ref, dli_ref) = rest[len(extra):]
        bbr, bbi = _bd8(bbr_ref[...]).astype(MXU), _bd8(bbi_ref[...]).astype(MXU)
        cr, ci = _bd8(cr_ref[...]).astype(MXU), _bd8(ci_ref[...]).astype(MXU)
        lr, li = _row8(lr_ref[...]), -_row8(li_ref[...])
        dv = d_ref[...]
        steps, e = _tile_powers(lr, li, reverse=True)
        for ref in (dbbr_ref, dbbi_ref, dlr_ref, dli_ref, dcr_ref, dci_ref, dd_ref):
            ref[...] = jnp.zeros_like(ref)
        nch = L // RC

        def step(k, carry):
            c = nch - 1 - k
            r0 = pl.multiple_of(c * RC, RC)
            dy = dy_ref[pl.ds(r0, RC), :]
            u = u_ref[pl.ds(r0, RC), :]
            dyb, ub = dy.astype(MXU), u.astype(MXU)
            sr, si = sr_ref[pl.ds(r0, RC), :], si_ref[pl.ds(r0, RC), :]
            dcr_ref[...] += _mm_tn(dyb, sr)
            dci_ref[...] -= _mm_tn(dyb, si)
            gr = jnp.dot(dyb, cr, preferred_element_type=F32)
            gi = -jnp.dot(dyb, ci, preferred_element_type=F32)
            gr, gi, carry = _scan_lti(gr, gi, carry, steps, e, reverse=True)
            pr_ = pltpu.roll(jnp.concatenate([_halo(sr_ref, c, r0), sr], axis=0), 1, 0)[8:, :]
            pi_ = pltpu.roll(jnp.concatenate([_halo(si_ref, c, r0), si], axis=0), 1, 0)[8:, :]
            dlr_ref[...] += _colsum(pr_ * gr + pi_ * gi)
            dli_ref[...] += _colsum(pr_ * gi - pi_ * gr)
            grb, gib = gr.astype(MXU), gi.astype(MXU)
            dbbr_ref[...] += _mm_tn(ub, grb)
            dbbi_ref[...] += _mm_tn(ub, gib)
            du_ref[pl.ds(r0, RC), :] = dv * dy + (_mm_nt(grb, bbr) + _mm_nt(gib, bbi))
            dd_ref[...] += _colsum(dy * u)
            return carry

        zero = jnp.zeros((1, S5_TW), F32)
        lax.fori_loop(0, nch, step, (zero, zero))
        dbbr_out[...], dbbi_out[...] = _bd8_diag(dbbr_ref[...]), _bd8_diag(dbbi_ref[...])
        dcr_out[...], dci_out[...] = _bd8_diag(dcr_ref[...]), _bd8_diag(dci_ref[...])
        dlr_out[...], dli_out[...] = _row8_split(dlr_ref[...]), _row8_split(dli_ref[...])

    in_tile, st, pg, plb, gg, glb, dv = _s5_specs(L, layer)
    u_tile = pl.BlockSpec((L, LANE), lambda t: (0, C_S5U // LANE + t))
    groups, rows = _S((N_S5_T, S5_GT, S5_H, S5_P)), _S((N_S5_T, S5_GT, S5_P))
    wide = pltpu.VMEM((LANE, S5_TW), F32)
    return pl.pallas_call(
        body, name="s5_bwd", grid=(N_S5_T,),
        in_specs=[in_tile, u_tile, st, st, pg, pg, plb, plb, pg, pg, _layer_row_tile(layer)] + extra_specs,
        out_specs=[in_tile, gg, gg, glb, glb, gg, gg, dv],
        out_shape=[_S((L, S5_W)), groups, groups, rows, rows, groups, groups, _S((1, S5_W))],
        scratch_shapes=[wide, wide, wide, wide, pltpu.VMEM((1, S5_TW), F32), pltpu.VMEM((1, S5_TW), F32)],
        compiler_params=_params(1))(dy0, z, s_re, s_im, bb_re, bb_im, lb_re, lb_im, c_re, c_im, dvec, *extra)


def _disc(ar, ai, ls):
    dt = jnp.exp(ls)
    mag = jnp.exp(ar * dt)
    lr = mag * jnp.cos(ai * dt)
    li = mag * jnp.sin(ai * dt)
    den = ar * ar + ai * ai
    cr = ((lr - 1.0) * ar + li * ai) / den
    ci = (li * ar - (lr - 1.0) * ai) / den
    return lr, li, cr, ci


def _s5_disc_fwd(ar, ai, ls, token=None):
    extra, extra_specs = _after(token)

    def body(ar_ref, ai_ref, ls_ref, *rest):
        lr_ref, li_ref, cr_ref, ci_ref = rest[len(extra):]
        lr, li, cr, ci = _disc(ar_ref[...], ai_ref[...], ls_ref[...])
        lr_ref[...], li_ref[...], cr_ref[...], ci_ref[...] = lr, li, cr, ci

    sh = _S(ar.shape)
    vm = pl.BlockSpec(memory_space=pltpu.VMEM)
    return pl.pallas_call(body, name="s5_disc_fwd", in_specs=[vm, vm, vm] + extra_specs, out_shape=[sh, sh, sh, sh])(
        ar, ai, ls, *extra)


def _s5_disc_bwd(ar, ai, ls, dlr, dli, dcr, dci):
    def body(ar_ref, ai_ref, ls_ref, dlr_ref, dli_ref, dcr_ref, dci_ref, dar_ref, dai_ref, dls_ref):
        _, vjp = jax.vjp(_disc, ar_ref[...], ai_ref[...], jnp.broadcast_to(ls_ref[...], ar_ref.shape))
        dar, dai, dls = vjp((dlr_ref[...], dli_ref[...], dcr_ref[...], dci_ref[...]))
        dar_ref[...], dai_ref[...] = dar, dai
        dls_ref[...] = jnp.sum(dls, axis=1, keepdims=True)

    return pl.pallas_call(body, name="s5_disc_bwd", out_shape=[_S(ar.shape), _S(ar.shape), _S(ls.shape)])(
        ar, ai, ls, dlr, dli, dcr, dci)


def _s5_bscale_fwd(cr, ci, br, bi):
    def body(cr_ref, ci_ref, br_ref, bi_ref, or_ref, oi_ref):
        or_ref[...], oi_ref[...] = _cmul(cr_ref[...], ci_ref[...], br_ref[...], bi_ref[...])

    return pl.pallas_call(body, name="s5_bscale_fwd", out_shape=[_S(br.shape), _S(br.shape)])(cr, ci, br, bi)


def _s5_bscale_bwd(cr, ci, br, bi, gr, gi):
    def body(cr_ref, ci_ref, br_ref, bi_ref, gr_ref, gi_ref, dbr_ref, dbi_ref, dcr_ref, dci_ref):
        cr_, ci_, br_, bi_, gr_, gi_ = (r[...] for r in (cr_ref, ci_ref, br_ref, bi_ref, gr_ref, gi_ref))
        dbr_ref[...] = cr_ * gr_ + ci_ * gi_
        dbi_ref[...] = cr_ * gi_ - ci_ * gr_
        dcr_ref[...] = jnp.sum(gr_ * br_ + gi_ * bi_, axis=1, keepdims=True)
        dci_ref[...] = jnp.sum(gi_ * br_ - gr_ * bi_, axis=1, keepdims=True)

    return pl.pallas_call(body, name="s5_bscale_bwd",
                          out_shape=[_S(br.shape), _S(br.shape), _S(cr.shape), _S(cr.shape)])(cr, ci, br, bi, gr, gi)


def _row(w):
    return pl.BlockSpec((TM, w), lambda i: (i, 0))


def _full(shape):
    return pl.BlockSpec(tuple(shape), lambda i: (0,) * len(shape))


def _p_rows(layer):
    return pl.BlockSpec((None, None, TM, PLE_D), lambda i: (layer, 0, i, 0))


def _lrow(layer, width):
    return pl.BlockSpec((None, 1, width), lambda i: (layer, 0, 0))


def _post_fwd(x, hs, z, y0, p, w_glu, b_glu, w_out, g1, b1, ple_w, w_pg, b_pg, g2, b2, layer):
    L = x.shape[0]

    def body(x_ref, hs_ref, z_ref, y0_ref, p_ref, wg_ref, bg_ref, wo_ref, g1_ref, b1_ref, pw_ref, wpg_ref, bpg_ref,
             g2_ref, b2_ref, x2_ref, xh1_ref, xh2_ref, q_ref, gt_ref, rstd1_ref, rstd2_ref):
        rg_gate = z_ref[:, C_RGG:C_RGG + RG_W]
        s5_gate = z_ref[:, C_S5G:C_S5G + S5_W]
        rg_y = hs_ref[...] * _silu_and_grad(rg_gate)[0]
        y1 = _gelu(y0_ref[...])
        gl = _sigmoid(_mm(y1, wg_ref[...]) + bg_ref[...])
        s5_y = (y1 * gl) * _silu_and_grad(s5_gate)[0]
        mix = _mm(jnp.concatenate([rg_y.astype(MXU), s5_y.astype(MXU)], axis=1), wo_ref[...])
        t1 = ALPHA * x_ref[...] + mix
        x1, xh1, rstd1 = _ln_fwd(t1, g1_ref[...], b1_ref[...])
        q = _mm(p_ref[...], pw_ref[...])
        gt = _sigmoid(_mm(x1, wpg_ref[...]) + bpg_ref[...])
        t2 = ALPHA * x1 + q * gt
        x2, xh2, rstd2 = _ln_fwd(t2, g2_ref[...], b2_ref[...])
        x2_ref[...], xh1_ref[...], xh2_ref[...], q_ref[...], gt_ref[...] = x2, xh1, xh2, q, gt
        rstd1_ref[...], rstd2_ref[...] = rstd1, rstd2

    vec = _lrow(layer, D_MODEL)
    return pl.pallas_call(
        body, name="post_fwd", grid=(L // TM,),
        in_specs=[_row(D_MODEL), _row(RG_W), _row(Z_W), _row(S5_W), _p_rows(layer), _full((S5_W, S5_W)), _lrow(layer, S5_W),
                  _full((D_MODEL, D_MODEL)), vec, vec, _full((PLE_D, D_MODEL)), _full((D_MODEL, D_MODEL)), vec, vec, vec],
        out_specs=[_row(D_MODEL)] * 5 + [_row(1)] * 2, out_shape=[_S((L, D_MODEL))] * 5 + [_S((L, 1))] * 2,
        compiler_params=_params(1))(x, hs, z, y0, p, w_glu, b_glu, w_out, g1, b1, ple_w, w_pg, b_pg, g2, b2)


def _post_bwd_a(dx2_or_target, is_top, xh2, xh1, rstd2, rstd1, q, gt, p, w_pg, g1, b1, g2, b2, layer, token=None):
    L = xh1.shape[0]
    extra, extra_specs = _after(token)

    def body(d_ref, xh2_ref, xh1_ref, rstd2_ref, rstd1_ref, q_ref, gt_ref, p_ref, wpg_ref, g1_ref, b1_ref, g2_ref,
             b2_ref, *rest):
        (dt1_ref, dpw_out, dwpg_out, dbpg_ref, dg1_ref, db1_ref, dg2_ref, db2_ref, loss_ref, dpw_ref,
         dwpg_ref) = rest[len(extra):]
        @pl.when(pl.program_id(0) == 0)
        def _():
            for ref in (dpw_ref, dwpg_ref, dbpg_ref, dg1_ref, db1_ref, dg2_ref, db2_ref, loss_ref):
                ref[...] = jnp.zeros_like(ref)

        g1, g2 = g1_ref[...], g2_ref[...]
        xh1, xh2, rstd1, rstd2 = xh1_ref[...], xh2_ref[...], rstd1_ref[...], rstd2_ref[...]
        x1 = xh1 * g1 + b1_ref[...]
        if is_top:
            err = (xh2 * g2 + b2_ref[...]) - d_ref[...]
            loss_ref[...] += _colsum(err * err)
            dx2 = err * (1.0 / D_MODEL)
        else:
            dx2 = d_ref[...]
        p = p_ref[...]
        q, gt = q_ref[...], gt_ref[...]
        dg2_ref[...] += _colsum(dx2 * xh2)
        db2_ref[...] += _colsum(dx2)
        dt2 = _ln_bwd(dx2, xh2, rstd2, g2)
        dq = dt2 * gt
        dgpre = (dt2 * q) * gt * (1.0 - gt)
        dpw_ref[...] += _mm_tn(p, dq)
        dwpg_ref[...] += _mm_tn(x1, dgpre)
        dbpg_ref[...] += _colsum(dgpre)
        dx1 = ALPHA * dt2 + _mm_nt(dgpre, wpg_ref[...])
        dg1_ref[...] += _colsum(dx1 * xh1)
        db1_ref[...] += _colsum(dx1)
        dt1_ref[...] = _ln_bwd(dx1, xh1, rstd1, g1)

        @pl.when(pl.program_id(0) == L // TM - 1)
        def _():
            dpw_out[...] = dpw_ref[...].astype(WIRE)
            dwpg_out[...] = dwpg_ref[...].astype(WIRE)

    vec, lvec = _full((1, D_MODEL)), _lrow(layer, D_MODEL)
    return pl.pallas_call(
        body, name="post_bwd_a_top" if is_top else "post_bwd_a", grid=(L // TM,),
        in_specs=[_row(D_MODEL), _row(D_MODEL), _row(D_MODEL), _row(1), _row(1), _row(D_MODEL), _row(D_MODEL), _p_rows(layer),
                  _full((D_MODEL, D_MODEL)), lvec, lvec, lvec, lvec] + extra_specs,
        out_specs=[_row(D_MODEL), _full((PLE_D, D_MODEL)), _full((D_MODEL, D_MODEL)), vec, vec, vec, vec, vec, vec],
        out_shape=[_S((L, D_MODEL)), _S((PLE_D, D_MODEL), WIRE), _S((D_MODEL, D_MODEL), WIRE)] + [_S((1, D_MODEL))] * 6,
        scratch_shapes=[pltpu.VMEM((PLE_D, D_MODEL), F32), pltpu.VMEM((D_MODEL, D_MODEL), F32)],
        compiler_params=_params(1))(dx2_or_target, xh2, xh1, rstd2, rstd1, q, gt, p, w_pg, g1, b1, g2, b2, *extra)


def _post_bwd_b(dt1, z, hs, y0, w_out, w_glu, b_glu, layer):
    L = dt1.shape[0]

    def body(dt1_ref, z_ref, hs_ref, y0_ref, wo_ref, wg_ref, bg_ref,
             dhs_ref, dy0_ref, dzg_ref, dwo_out, dwg_out, dbg_ref, dwo_ref, dwg_ref):
        @pl.when(pl.program_id(0) == 0)
        def _():
            for ref in (dwo_ref, dwg_ref, dbg_ref):
                ref[...] = jnp.zeros_like(ref)

        dt1b = dt1_ref[...].astype(MXU)
        dm = _mm_nt(dt1b, wo_ref[...])
        d_rgy, d_s5y = dm[:, :RG_W], dm[:, RG_W:]
        rg_gate = z_ref[:, C_RGG:C_RGG + RG_W]
        s5_gate = z_ref[:, C_S5G:C_S5G + S5_W]
        hs = hs_ref[...]
        sl, dsl = _silu_and_grad(rg_gate)
        dhs_ref[...] = d_rgy * sl
        dzg_ref[:, :RG_W] = d_rgy * hs * dsl
        y0 = y0_ref[...]
        y1 = _gelu(y0)
        gl = _sigmoid(_mm(y1, wg_ref[...]) + bg_ref[...])
        y2 = y1 * gl
        sl2, dsl = _silu_and_grad(s5_gate)
        m = jnp.concatenate([(hs * sl).astype(MXU), (y2 * sl2).astype(MXU)], axis=1)
        dwo_ref[...] += _mm_tn(m, dt1b)
        dy2 = d_s5y * sl2
        dzg_ref[:, RG_W:] = d_s5y * y2 * dsl
        dglpre = (dy2 * y1) * gl * (1.0 - gl)
        dwg_ref[...] += _mm_tn(y1, dglpre)
        dbg_ref[...] += _colsum(dglpre)
        dy1 = dy2 * gl + _mm_nt(dglpre, wg_ref[...])
        dy0_ref[...] = dy1 * _gelu_grad(y0)

        @pl.when(pl.program_id(0) == L // TM - 1)
        def _():
            dwo_out[...] = dwo_ref[...].astype(WIRE)
            dwg_out[...] = dwg_ref[...].astype(WIRE)

    return pl.pallas_call(
        body, name="post_bwd_b", grid=(L // TM,),
        in_specs=[_row(D_MODEL), _row(Z_W), _row(RG_W), _row(S5_W), _full((D_MODEL, D_MODEL)),
                  _full((S5_W, S5_W)), _lrow(layer, S5_W)],
        out_specs=[_row(RG_W), _row(S5_W), _row(D_MODEL), _full((D_MODEL, D_MODEL)), _full((S5_W, S5_W)), _full((1, S5_W))],
        out_shape=[_S((L, RG_W)), _S((L, S5_W)), _S((L, D_MODEL)), _S((D_MODEL, D_MODEL), WIRE), _S((S5_W, S5_W), WIRE),
                   _S((1, S5_W))],
        scratch_shapes=[pltpu.VMEM((D_MODEL, D_MODEL), F32), pltpu.VMEM((S5_W, S5_W), F32)],
        compiler_params=_params(1))(dt1, z, hs, y0, w_out, w_glu, b_glu)


def _adamw(parts, w, m, v, token=None):
    nl = len(parts)
    extra, extra_specs = _after(token)
    n, R, C = parts[0].shape
    tr = R
    for cand in (512, 256, 128, 64, 32, 16, 8):
        if R % cand == 0 and n * cand * C * 4 <= 4 * 1024 * 1024:
            tr = cand
            break
    nblk = R // tr

    def body(*refs):
        p_refs = refs[:nl]
        w_ref, m_ref, v_ref = refs[nl:nl + 3]
        g_ref, d_ref, nm_ref, nv_ref = refs[nl + 3 + len(extra):]
        layer = pl.program_id(0)
        g = None
        for li, p_ref in enumerate(p_refs):
            s = p_ref[0].astype(F32)
            for k in range(1, n):
                s = s + p_ref[k].astype(F32)
            g = s if g is None else jnp.where(layer == li, s, g)
        nm = B1 * m_ref[...] + (1.0 - B1) * g
        nv = B2 * v_ref[...] + (1.0 - B2) * (g * g)
        d_ref[...] = (-LR) * ((nm / BC1) / (jnp.sqrt(nv / BC2) + EPS) + WD * w_ref[...])
        g_ref[...], nm_ref[...], nv_ref[...] = g, nm, nv

    def part_spec(li):
        return pl.BlockSpec((n, tr, C), lambda l, i: (0, jnp.where(l == li, i, jnp.where(l < li, 0, nblk - 1)), 0))

    blk = pl.BlockSpec((tr, C), lambda l, i: (l * nblk + i, 0))
    return pl.pallas_call(
        body, name="adamw", grid=(nl, nblk),
        in_specs=[part_spec(li) for li in range(nl)] + [blk, blk, blk] + extra_specs,
        out_specs=[blk] * 4, out_shape=[_S((nl * R, C))] * 4, compiler_params=_params(2))(*parts, w, m, v, *extra)


def _adamw_natural(names, g, w, m, v, name):
    n = len(names)

    def body(*refs):
        for j in range(n):
            g_ref, w_ref, m_ref, v_ref, d_ref, nm_ref, nv_ref = (refs[k * n + j] for k in range(7))
            gj = g_ref[...]
            nm = B1 * m_ref[...] + (1.0 - B1) * gj
            nv = B2 * v_ref[...] + (1.0 - B2) * (gj * gj)
            d_ref[...] = (-LR) * ((nm / BC1) / (jnp.sqrt(nv / BC2) + EPS) + WD * w_ref[...])
            nm_ref[...], nv_ref[...] = nm, nv

    ins = [t[k] for t in (g, w, m, v) for k in names]
    outs = pl.pallas_call(body, name=name, out_shape=[_S(w[k].shape) for _ in range(3) for k in names],
                          compiler_params=pltpu.CompilerParams(vmem_limit_bytes=VMEM_LIMIT))(*ins)
    return [{k: outs[t * n + j] for j, k in enumerate(names)} for t in range(3)]


def _me():
    return lax.axis_index("x"), lax.axis_index("y"), lax.axis_index("c")


def _lin(dev):
    return 4 * dev[0] + 2 * dev[1] + dev[2]


def _blk(ref, axis, size, idx):
    nd = len(ref.shape)
    start = idx * size
    if axis == nd - 1 and size % LANE == 0:
        start = pl.multiple_of(start, LANE)
    elif axis == nd - 2 and size % 16 == 0:
        start = pl.multiple_of(start, 16)
    ix = [slice(None)] * nd
    ix[axis] = pl.ds(start, size)
    return ref.at[tuple(ix)]


def _all_gather(shards, axes, name):
    n = len(shards)
    sizes = [s.shape[a] for s, a in zip(shards, axes)]
    out_shapes = [_S(s.shape[:a] + (N_DEV * s.shape[a],) + s.shape[a + 1:], s.dtype) for s, a in zip(shards, axes)]

    def body(*refs):
        ins, outs = refs[:n], refs[n:2 * n]
        send_sems, recv_sems, local_sems = refs[2 * n:]
        x, y, c = _me()
        me, sibling = (x, y, c), (x, y, 1 - c)
        chips = [(1 - x, y), (x, 1 - y), (1 - x, 1 - y)]

        def copy(a, k, block, to, from_input=False):
            dst = _blk(outs[a], axes[a], sizes[a], _lin(block))
            return pltpu.make_async_remote_copy(
                src_ref=ins[a] if from_input else dst, dst_ref=dst, send_sem=send_sems.at[a, k],
                recv_sem=recv_sems.at[a, k], device_id=to, device_id_type=MESH)

        mine = [pltpu.make_async_copy(ins[a], _blk(outs[a], axes[a], sizes[a], _lin(me)), local_sems.at[a]) for a in range(n)]
        for cp in mine:
            cp.start()
        first = []
        for a in range(n):
            first.append(copy(a, 0, me, sibling, True))
            first += [copy(a, 1 + j, me, (*chip, c), True) for j, chip in enumerate(chips)]
        for cp in first:
            cp.start()
        passed = []
        for j, chip in enumerate(chips):
            for a in range(n):
                copy(a, 1 + j, (*chip, c), me).wait_recv()
                cp = copy(a, 4 + j, (*chip, c), sibling)
                cp.start()
                passed.append(cp)
        for a in range(n):
            copy(a, 0, sibling, me).wait_recv()
            for j, chip in enumerate(chips):
                copy(a, 4 + j, (*chip, 1 - c), me).wait_recv()
        for cp in first + passed:
            cp.wait_send()
        for cp in mine:
            cp.wait()

    return pl.pallas_call(
        body, name=name, out_shape=out_shapes, in_specs=[ANY] * n, out_specs=[ANY] * n,
        scratch_shapes=[pltpu.SemaphoreType.DMA((n, 7)), pltpu.SemaphoreType.DMA((n, 7)), pltpu.SemaphoreType.DMA((n,))],
    )(*shards)


HBM_SPEC = pl.BlockSpec(memory_space=pltpu.HBM)
SEM_SPEC = pl.BlockSpec(memory_space=pltpu.SEMAPHORE)
EFFECT = pltpu.SideEffectType.DATAFLOW_SIDE_EFFECTING


def _peers(x, y, c):
    flip = lambda v, f: 1 - v if f else v
    return [(flip(x, k & 4), flip(y, k & 2), flip(c, k & 1)) for k in range(1, N_DEV)]


def _land_shape(mode, s, axis):
    if mode == "gather":
        return s.shape[:axis] + (N_DEV * s.shape[axis],) + s.shape[axis + 1:]
    return (N_DEV,) + s.shape[:axis] + (s.shape[axis] // N_DEV,) + s.shape[axis + 1:]


def _src_view(mode, ref, axis, peer):
    return ref if mode == "gather" else _blk(ref, axis, ref.shape[axis] // N_DEV, peer)


def _dst_view(mode, land, axis, sender):
    return _blk(land, axis, land.shape[axis] // N_DEV, sender) if mode == "gather" else land.at[sender]


def _seven_blocks(mode, land, axis):
    if mode == "gather":
        ix = [slice(None)] * len(land.shape)
        ix[axis] = pl.ds(0, (N_DEV - 1) * (land.shape[axis] // N_DEV))
        return land.at[tuple(ix)]
    return land.at[pl.ds(0, N_DEV - 1)]


def _place_own(mode, srcs, axes, name, after=None):
    n = len(srcs)
    extra, extra_specs = _after(after)

    def body(me_ref, *refs):
        for a in range(n):
            out = refs[n + len(extra) + a]
            out[...] = refs[a][...].reshape(out.shape)

    def at_me(shape, axis):
        return lambda i, me: tuple(me[0] if d == axis else 0 for d in range(len(shape)))

    in_specs, out_specs = [], []
    for s, axis in zip(srcs, axes):
        if mode == "gather":
            in_specs.append(pl.BlockSpec(s.shape, lambda i, me, nd=len(s.shape): (0,) * nd))
            out_specs.append(pl.BlockSpec(s.shape, at_me(s.shape, axis)))
        else:
            blk = s.shape[:axis] + (s.shape[axis] // N_DEV,) + s.shape[axis + 1:]
            in_specs.append(pl.BlockSpec(blk, at_me(blk, axis)))
            out_specs.append(pl.BlockSpec((1,) + blk, at_me((1,) + blk, 0)))
    me = _lin(_me()).astype(jnp.int32).reshape(1)
    return pl.pallas_call(
        body, name=name, out_shape=[_S(_land_shape(mode, s, a), s.dtype) for s, a in zip(srcs, axes)],
        grid_spec=pltpu.PrefetchScalarGridSpec(num_scalar_prefetch=1, grid=(1,), in_specs=in_specs + extra_specs,
                                               out_specs=out_specs),
        compiler_params=_params(1))(me, *srcs, *extra)


def _place_shards(shards, layers, axes, dtypes, name, after=None):
    n = len(shards)
    extra, extra_specs = _after(after)

    def body(me_ref, *refs):
        for a in range(n):
            out = refs[n + len(extra) + a]
            out[...] = refs[a][...].astype(out.dtype)

    in_specs, out_specs, out_shape = [], [], []
    for s, layer, axis, dt in zip(shards, layers, axes, dtypes):
        shape = s.shape if layer is None else s.shape[1:]
        nd = len(shape)
        if layer is None:
            in_specs.append(pl.BlockSpec(shape, lambda i, me, nd=nd: (0,) * nd))
        else:
            in_specs.append(pl.BlockSpec((None,) + shape, lambda i, me, nd=nd, layer=layer: (layer,) + (0,) * nd))
        out_specs.append(pl.BlockSpec(shape, lambda i, me, nd=nd, axis=axis: tuple(me[0] if d == axis else 0 for d in range(nd))))
        out_shape.append(_S(shape[:axis] + (N_DEV * shape[axis],) + shape[axis + 1:], dt))
    me = _lin(_me()).astype(jnp.int32).reshape(1)
    return pl.pallas_call(
        body, name=name, out_shape=out_shape,
        grid_spec=pltpu.PrefetchScalarGridSpec(num_scalar_prefetch=1, grid=(1,), in_specs=in_specs + extra_specs,
                                               out_specs=out_specs),
        compiler_params=_params(1))(me, *shards, *extra)


def _push_start(mode, srcs, lands, axes, name):
    n, ns = len(lands), len(srcs)

    def body(*refs):
        src_refs, land_refs = refs[:ns], refs[ns:ns + n]
        send_sems, recv_sems = refs[ns + n], refs[ns + n + 1]
        token = refs[-1]
        x, y, c = _me()
        me = _lin((x, y, c))
        for a in range(n):
            mine = _dst_view(mode, land_refs[a], axes[a], me)
            for peer in _peers(x, y, c):
                pltpu.make_async_remote_copy(
                    src_ref=_src_view(mode, src_refs[a], axes[a], _lin(peer)) if ns else mine, dst_ref=mine,
                    send_sem=send_sems.at[a], recv_sem=recv_sems.at[a], device_id=peer, device_id_type=MESH).start()
        token[...] = jnp.zeros_like(token)

    hbm = lambda s: pltpu.HBM(s.shape, s.dtype)
    outs = pl.pallas_call(
        body, name=name,
        out_shape=(pltpu.SemaphoreType.DMA((n,)), pltpu.SemaphoreType.DMA((n,)), *[hbm(s) for s in srcs], *[hbm(s) for s in lands],
                   _S((SUB, LANE))),
        in_specs=[HBM_SPEC] * (ns + n),
        out_specs=(SEM_SPEC, SEM_SPEC, *[HBM_SPEC] * (ns + n), pl.BlockSpec(memory_space=pltpu.VMEM)),
        input_output_aliases={i: 2 + i for i in range(ns + n)},
        compiler_params=pltpu.CompilerParams(has_side_effects=EFFECT),
    )(*[pltpu.with_memory_space_constraint(s, pltpu.HBM) for s in list(srcs) + list(lands)])
    return outs[0], outs[1], outs[2:2 + ns], outs[2 + ns:2 + ns + n], outs[-1]


def _push_wait(mode, send_sems, recv_sems, srcs, lands, axes, after, name, first=0):
    n, ns = len(lands), len(srcs)
    after = list(after) if isinstance(after, (list, tuple)) else [after]

    def body(*refs):
        land_refs = refs[ns:ns + n]
        send_sems, recv_sems = refs[ns + n], refs[ns + n + 1]
        x, y, c = _me()
        for a in range(n):
            seven = _seven_blocks(mode, land_refs[a], axes[a])
            cp = pltpu.make_async_remote_copy(src_ref=seven, dst_ref=seven, send_sem=send_sems.at[first + a],
                                              recv_sem=recv_sems.at[first + a],
                                              device_id=(x, y, 1 - c), device_id_type=MESH)
            cp.wait_send()
            cp.wait_recv()

    hbm = lambda s: pltpu.HBM(s.shape, s.dtype)
    outs = pl.pallas_call(
        body, name=name, out_shape=tuple(hbm(s) for s in list(srcs) + list(lands)),
        in_specs=[HBM_SPEC] * (ns + n) + [SEM_SPEC, SEM_SPEC] + [ANY] * len(after), out_specs=tuple([HBM_SPEC] * (ns + n)),
        input_output_aliases={i: i for i in range(ns + n)},
        compiler_params=pltpu.CompilerParams(has_side_effects=EFFECT),
    )(*srcs, *lands, send_sems, recv_sems, *after)
    return outs[ns:]


def _sum_parts(parts):
    n, R, C = parts.shape

    def body(p_ref, o_ref):
        g = p_ref[0]
        for k in range(1, n):
            g = g + p_ref[k]
        o_ref[...] = g

    return pl.pallas_call(body, name="sum_parts", out_shape=_S((R, C)))(parts)


SMALL =['conv_b', 'rg_wa', 'rg_ba', 'rg_wx', 'rg_bx', 'rg_lambda', 's5_a_re', 's5_a_im', 's5_b_re', 's5_b_im',
         's5_c_re', 's5_c_im', 's5_d', 's5_log_step', 's5_b_glu', 'ln1_g', 'ln1_b', 'ple_gate_b', 'ln2_g', 'ln2_b']
WEIGHTS = ['w_in', 'conv_w', 'conv_b', 'rg_wa', 'rg_ba', 'rg_wx', 'rg_bx', 'rg_lambda', 's5_a_re', 's5_a_im', 's5_b_re',
           's5_b_im', 's5_c_re', 's5_c_im', 's5_d', 's5_log_step', 's5_w_glu', 's5_b_glu', 'w_out', 'ln1_g', 'ln1_b',
           'ple_w', 'ple_gate_w', 'ple_gate_b', 'ln2_g', 'ln2_b']
PACK_ROWS_MULT = 64


def _pack(tree, scalar):
    flat = jnp.concatenate([tree[k].reshape(-1) for k in SMALL] + [scalar.reshape(1)])
    rows = -(-flat.shape[0] // (LANE * PACK_ROWS_MULT)) * PACK_ROWS_MULT
    return jnp.pad(flat, (0, rows * LANE - flat.shape[0])).reshape(rows, LANE)


def _unpack(packed, like):
    flat, out, o = packed.reshape(-1), {}, 0
    for k in SMALL:
        n = math.prod(like[k].shape)
        out[k] = flat[o:o + n].reshape(like[k].shape)
        o += n
    return out, flat[o]


class _NoHooks:
    token = None
    first_token = None

    def first_weights(self, full, after):
        return full

    def layer_start(self, i, W, after):
        return W

    def late_weights(self, i, W, after):
        return W

    def post_done(self, i, g):
        return None

    def smalls_done(self, grads, loss):
        self.small = _small_grads(grads, self.res)
        return None

    def w_in_done(self, i, g):
        return None

    def layer_done(self, i, g, dx):
        return None


def _local_grads(x, p, target, W, disc, hooks):
    depth = 2
    saved = []
    for i in range(depth):
        if i > 0:
            W = hooks.layer_start(i, W, x)
        w = W[i]
        z = _inproj_fwd(x, w['w_in'], hooks.token if i == 0 else None)
        hs, *gates = _rg_fwd(z, w['conv_w'], w['conv_b'], w['wa_bd'], w['wx_bd'], w['rg_ba'], w['rg_bx'], w['rg_lambda'], i)
        d = disc[i]
        y0, s_re, s_im = _s5_fwd(z, d['bb_re'], d['bb_im'], d['lb_re'], d['lb_im'], d['c_re'], d['c_im'], w['s5_d'], i)
        W = hooks.late_weights(i, W, y0)
        w = W[i]
        x2, *norms = _post_fwd(x, hs, z, y0, p, w['s5_w_glu'], w['s5_b_glu'], w['w_out'], w['ln1_g'], w['ln1_b'],
                               w['ple_w'], w['ple_gate_w'], w['ple_gate_b'], w['ln2_g'], w['ln2_b'], i)
        saved.append((x, z, hs, gates, y0, s_re, s_im, norms))
        x = x2

    grads = [None] * depth
    dx = target
    loss = None
    token = None
    for i in reversed(range(depth)):
        w, d = W[i], disc[i]
        xin, z, hs, gates, y0, s_re, s_im, (xh1, xh2, q, gt, rstd1, rstd2) = saved[i]
        g = {}
        (dt1, g['ple_w'], g['ple_gate_w'], g['ple_gate_b'], g['ln1_g'], g['ln1_b'], g['ln2_g'], g['ln2_b'], lrow) = _post_bwd_a(
            dx, i == depth - 1, xh2, xh1, rstd2, rstd1, q, gt, p, w['ple_gate_w'], w['ln1_g'], w['ln1_b'],
            w['ln2_g'], w['ln2_b'], i, token)
        if i == depth - 1:
            loss = 0.5 / D_MODEL * jnp.sum(lrow)
        dhs, dy0, dzg, g['w_out'], g['s5_w_glu'], g['s5_b_glu'] = _post_bwd_b(dt1, z, hs, y0, w['w_out'], w['s5_w_glu'],
                                                                           w['s5_b_glu'], i)
        (dzu, g['bb_re'], g['bb_im'], g['lb_re'], g['lb_im'], g['c_re'], g['c_im'], g['s5_d']) = _s5_bwd(
            dy0, z, s_re, s_im, d['bb_re'], d['bb_im'], d['lb_re'], d['lb_im'], d['c_re'], d['c_im'], w['s5_d'], i,
            hooks.post_done(i, g))
        (dzx, g['conv_w'], g['conv_b'], g['wa_bd'], g['wx_bd'], g['rg_ba'], g['rg_bx'], g['rg_lambda']) = _rg_bwd(
            dhs, z, hs, gates, w['conv_w'], w['wa_bd'], w['wx_bd'], w['rg_lambda'], i)
        if i == 0:
            g['w_in'] = _inproj_bwd_dw(xin, dzx, dzg, dzu, hooks.smalls_done([g, grads[1]], loss))
            dx = _inproj_bwd_dx(dt1, dzx, dzg, dzu, w['w_in'], hooks.w_in_done(i, g))
        else:
            dx, g['w_in'] = _inproj_bwd(dt1, xin, dzx, dzg, dzu, w['w_in'])
        grads[i] = g
        token = hooks.layer_done(i, g, dx)
    return loss, dx, grads


def _s5_layouts_fwd(s5_a_re, s5_a_im, s5_log_step, s5_b_re, s5_b_im, s5_c_re, s5_c_im, token=None):
    depth = s5_a_re.shape[0]
    ar, ai = s5_a_re.reshape(depth * 24, S5_P), s5_a_im.reshape(depth * 24, S5_P)
    ls = s5_log_step.reshape(depth * 24, 1)
    lr, li, cr, ci = _s5_disc_fwd(ar, ai, ls, token)
    per_group = lambda a: a.reshape(depth * 24, 1, S5_P)
    as_c = lambda b: jnp.swapaxes(b, 2, 3).reshape(depth * 24, S5_H, S5_P)
    res = (ar, ai, ls, per_group(cr), per_group(ci), as_c(s5_b_re), as_c(s5_b_im))
    bbr, bbi = _s5_bscale_fwd(*res[3:])
    tiles = lambda a: a.reshape(depth * N_S5_T, S5_GT, S5_H, S5_P)
    rows = lambda a: a.reshape(depth * N_S5_T, S5_GT, S5_P)
    disc = dict(bb_re=tiles(bbr), bb_im=tiles(bbi), lb_re=rows(lr), lb_im=rows(li), c_re=tiles(s5_c_re), c_im=tiles(s5_c_im))
    return [disc] * depth, res


def _s5_layouts_bwd(grads, res):
    ar, ai, ls, cr, ci, br, bi = res
    depth = len(grads)
    stack = lambda k, shape: jnp.stack([g[k] for g in grads]).reshape(shape)
    groups, shape_c = (depth * 24, S5_H, S5_P), (depth, 24, S5_H, S5_P)
    dbr, dbi, dcr, dci = _s5_bscale_bwd(cr, ci, br, bi, stack('bb_re', groups), stack('bb_im', groups))
    gp = (depth * 24, S5_P)
    dar, dai, dls = _s5_disc_bwd(ar, ai, ls, stack('lb_re', gp), stack('lb_im', gp), dcr.reshape(gp), dci.reshape(gp))
    return dict(
        s5_a_re=dar.reshape(depth, 24, S5_P), s5_a_im=dai.reshape(depth, 24, S5_P), s5_log_step=dls.reshape(depth, 24),
        s5_b_re=jnp.swapaxes(dbr.reshape(shape_c), 2, 3), s5_b_im=jnp.swapaxes(dbi.reshape(shape_c), 2, 3),
        s5_c_re=stack('c_re', shape_c), s5_c_im=stack('c_im', shape_c))


LATE = ('w_out', 'ple_w', 'ple_gate_w', 's5_w_glu')


ROWS = ('conv_b', 'rg_ba', 'rg_bx', 'rg_lambda', 's5_d', 's5_b_glu', 'ln1_g', 'ln1_b', 'ple_gate_b', 'ln2_g', 'ln2_b')


def _shared_weights(full):
    depth = full['conv_b'].shape[0]
    shared = {k: full[k].reshape(depth, 1, -1) for k in ROWS}
    shared.update(conv_w=full['conv_w'], wa_bd=full['rg_wa'], wx_bd=full['rg_wx'])
    return shared


def _layer_weights(full, shared, i):
    return dict(shared, w_in=full['w_in'][i])


class _AllLocal(_NoHooks):
    def __init__(self, full):
        self.full = full

    def late_weights(self, i, W, after):
        W[i].update({k: self.full[k][i] for k in LATE})
        return W


def _full_grads(full, x, p, target, hooks=None):
    hooks = hooks or _AllLocal(full)
    disc, res = _s5_layouts_fwd(full['s5_a_re'], full['s5_a_im'], full['s5_log_step'], full['s5_b_re'], full['s5_b_im'],
                                full['s5_c_re'], full['s5_c_im'], hooks.first_token)
    full = hooks.first_weights(full, disc[-1]['bb_im'])
    shared = _shared_weights(full)
    W = [_layer_weights(full, shared, i) for i in range(2)]
    hooks.res = res
    loss, gx, grads = _local_grads(x, p, target, W, disc, hooks)
    out = dict(hooks.small)
    for k in SHARD_AXIS:
        out[k] = [g[k] for g in grads]
    return loss, gx, out


def _small_grads(grads, res):
    stack = lambda f: jnp.stack([f(g) for g in grads])
    out = _s5_layouts_bwd(grads, res)
    out['conv_w'] = stack(lambda g: g['conv_w'])
    for k in ('conv_b', 'rg_ba', 'rg_bx', 'rg_lambda', 's5_b_glu', 'ln1_g', 'ln1_b', 'ple_gate_b', 'ln2_g', 'ln2_b'):
        out[k] = stack(lambda g: g[k][0])
    out['s5_d'] = stack(lambda g: g['s5_d'][0]).reshape(2, 24, 16)
    out['rg_wa'] = stack(lambda g: g['wa_bd'])
    out['rg_wx'] = stack(lambda g: g['wx_bd'])
    return out


SHARD_AXIS = {'w_in': 2, 'w_out': 1, 'ple_w': 2, 'ple_gate_w': 1, 's5_w_glu': 1}


def kernel(x, p, w_in, conv_w, conv_b, rg_wa, rg_ba, rg_wx, rg_bx, rg_lambda, s5_a_re, s5_a_im, s5_b_re, s5_b_im, s5_c_re, s5_c_im, s5_d, s5_log_step, s5_w_glu, s5_b_glu, w_out, ln1_g, ln1_b, ple_w, ple_gate_w, ple_gate_b, ln2_g, ln2_b, loss_target, m_w_in, m_conv_w, m_conv_b, m_rg_wa, m_rg_ba, m_rg_wx, m_rg_bx, m_rg_lambda, m_s5_a_re, m_s5_a_im, m_s5_b_re, m_s5_b_im, m_s5_c_re, m_s5_c_im, m_s5_d, m_s5_log_step, m_s5_w_glu, m_s5_b_glu, m_w_out, m_ln1_g, m_ln1_b, m_ple_w, m_ple_gate_w, m_ple_gate_b, m_ln2_g, m_ln2_b, v_w_in, v_conv_w, v_conv_b, v_rg_wa, v_rg_ba, v_rg_wx, v_rg_bx, v_rg_lambda, v_s5_a_re, v_s5_a_im, v_s5_b_re, v_s5_b_im, v_s5_c_re, v_s5_c_im, v_s5_d, v_s5_log_step, v_s5_w_glu, v_s5_b_glu, v_w_out, v_ln1_g, v_ln1_b, v_ple_w, v_ple_gate_w, v_ple_gate_b, v_ln2_g, v_ln2_b):
    local = dict(locals())
    w = {k: local[k] for k in WEIGHTS}
    mom = {k: local['m_' + k] for k in WEIGHTS}
    var = {k: local['v_' + k] for k in WEIGHTS}

    big = list(SHARD_AXIS)
    late_axes = [SHARD_AXIS[k] - 1 for k in LATE]
    pushed = {}

    groups = dict(first=(['w_in', 'conv_w'], [0, None], [1, 0]), l0=(list(LATE), [0] * len(LATE), late_axes),
                  l1=(['w_in'] + list(LATE), [1] * (1 + len(LATE)), [1] + late_axes))
    w_names, w_layers, w_axes = (sum((g[j] for g in groups.values()), []) for j in range(3))
    shards = [w[k] if layer is not None else w[k][None] for k, layer in zip(w_names, w_layers)]
    lands = _place_shards(shards, w_layers, w_axes, [WIRE if k in big else w[k].dtype for k in w_names], "place_weights")
    pushed["weights"] = _push_start("gather", [], lands, w_axes, "push_weights")

    def await_weights(key, axes, after):
        first = sum(len(g[0]) for g in list(groups.values())[:list(groups).index(key)])
        s = pushed["weights"]
        return _push_wait("gather", s[0], s[1], [], s[3][first:first + len(axes)], axes, after, "await_weights_" + key, first)

    def push_grads(key, g, names, axes):
        srcs = [g[k] for k in names]
        pushed[key] = _push_start("scatter", srcs, _place_own("scatter", srcs, axes, "place_grads_" + key), axes,
                                  "push_grads_" + key)
        return pushed[key][4]

    def await_grads(key, axes, after):
        s = pushed[key]
        return _push_wait("scatter", s[0], s[1], s[2], s[3], axes, after, "await_grads_" + key)

    class Overlap(_NoHooks):
        token = pushed["weights"][4]
        first_token = token

        def first_weights(self, full, after):
            w_in0, conv = await_weights("first", [1, 0], after)
            return dict(full, w_in=[w_in0, None], conv_w=jnp.moveaxis(conv, 0, 2).reshape(2, 4, RG_W))

        def late_weights(self, i, W, after):
            if i == 0:
                W[0].update(zip(LATE, await_weights("l0", late_axes, after)))
            return W

        def layer_start(self, i, W, after):
            lands = await_weights("l1", [1] + late_axes, after)
            W[1].update(zip(LATE, lands[1:]), w_in=lands[0])
            return W

        def post_done(self, i, g):
            return push_grads("late0", g, LATE, late_axes) if i == 0 else None

        def smalls_done(self, grads, loss):
            super().smalls_done(grads, loss)
            conv = jnp.moveaxis(self.small['conv_w'].reshape(2, 4, N_DEV, RG_W // N_DEV), 2, 0)
            self.packed = _pack(self.small, loss)
            return push_grads("small", dict(conv_w=conv.reshape(N_DEV, 8, RG_W // N_DEV), small=self.packed),
                              ['conv_w', 'small'], [0, 0])

        def w_in_done(self, i, g):
            return push_grads("w_in0", g, ['w_in'], [0])

        def layer_done(self, i, g, dx):
            return push_grads("all1", g, ['w_in'] + list(LATE), [0] + late_axes) if i == 1 else None

    hooks = Overlap()
    _, grad_x, g = _full_grads(dict(w), x[0], p, loss_target[0], hooks)

    recv1 = dict(zip(['w_in'] + list(LATE), await_grads("all1", [0] + late_axes, grad_x)))
    recv0 = dict(zip(LATE, await_grads("late0", late_axes, grad_x)))
    outs = {}

    def update(k, parts):
        shard = w[k].shape
        c = shard[-1]
        two = lambda a: a.reshape(-1, c)
        res = _adamw([r.reshape(N_DEV, -1, c) for r in parts], two(w[k]), two(mom[k]), two(var[k]))
        outs[k] = [o.reshape(shard) for o in res]

    for k in LATE:
        update(k, [recv0[k], recv1[k]])
    done = [outs[k][1] for k in LATE]
    conv_parts, small_parts = await_grads("small", [0, 0], done)

    rows = hooks.packed.shape[0] // N_DEV
    mine = _sum_parts(small_parts.reshape(N_DEV, rows, LANE))
    gathered = _all_gather([mine], [0], "gather_small_grads")[0]
    w_in0, = await_grads("w_in0", [0], gathered)
    update('w_in', [w_in0, recv1['w_in']])
    update('conv_w', [conv_parts])
    summed, loss = _unpack(gathered, w)
    narrow = ['s5_b_re', 's5_b_im']
    for names, name in ((narrow, "adamw_s5_b"), ([k for k in SMALL if k not in narrow], "adamw_small")):
        delta, new_m, new_v = _adamw_natural(names, summed, w, mom, var, name)
        for k in names:
            outs[k] = [summed[k], delta[k], new_m[k], new_v[k]]

    res = [loss, grad_x[None]]
    for j in range(4):
        res += [outs[k][j] for k in WEIGHTS]
    return tuple(res)
```

```python
import math

import jax
import jax.numpy as jnp
from jax import lax
from jax.experimental import pallas as pl
from jax.experimental.pallas import tpu as pltpu

F32 = jnp.float32
MXU = jnp.bfloat16
WIRE = jnp.bfloat16

N_DEV = 8
D_MODEL = 1024
PLE_D = 256
RG_W = 640
S5_W = 384
S5_P = 64
S5_N = 24 * S5_P
Z_W = 2 * RG_W + 2 * S5_W
C_RGG = RG_W
C_S5U = 2 * RG_W
C_S5G = 2 * RG_W + S5_W
LANE = 128
N_RG_T = RG_W // LANE
N_S5_T = S5_W // LANE
W_BLK = Z_W // N_DEV
ALPHA = (2.0 * 2) ** 0.25
LN_EPS = 1e-5
RG_C = 8.0
LR, B1, B2, EPS, WD, STEP = 0.001, 0.9, 0.999, 1e-08, 0.01, 10
BC1 = 1.0 - B1 ** STEP
BC2 = 1.0 - B2 ** STEP
RC = 512
RC_RG = 1024
TM = 256
TM_MM = 1024
VMEM_LIMIT = 56 * 1024 * 1024

MESH = pl.DeviceIdType.MESH
ANY = pl.BlockSpec(memory_space=pl.ANY)


def _params(n_grid_axes, vmem=VMEM_LIMIT):
    return pltpu.CompilerParams(dimension_semantics=("arbitrary",) * n_grid_axes, vmem_limit_bytes=vmem)


def _S(shape, dtype=F32):
    return jax.ShapeDtypeStruct(tuple(shape), dtype)


def _sigmoid(x):
    return 0.5 * jnp.tanh(0.5 * x) + 0.5


def _silu_and_grad(x):
    s = _sigmoid(x)
    return x * s, s * (1.0 + x * (1.0 - s))


_GELU_C = math.sqrt(2.0 / math.pi)


def _gelu(x):
    return 0.5 * x * (1.0 + jnp.tanh(_GELU_C * (x + 0.044715 * (x * x * x))))


def _gelu_grad(x):
    th = jnp.tanh(_GELU_C * (x + 0.044715 * (x * x * x)))
    return 0.5 * (1.0 + th) + 0.5 * x * (1.0 - th * th) * (_GELU_C * (1.0 + 3.0 * 0.044715 * (x * x)))


def _mm(a, b):
    return jnp.dot(a.astype(MXU), b.astype(MXU), preferred_element_type=F32)


def _mm_nt(a, b):
    return lax.dot_general(a.astype(MXU), b.astype(MXU), (((1,), (1,)), ((), ())), preferred_element_type=F32)


def _mm_tn(a, b):
    return lax.dot_general(a.astype(MXU), b.astype(MXU), (((0,), (0,)), ((), ())), preferred_element_type=F32)


def _ln_fwd(t, g, b):
    mu = jnp.mean(t, axis=-1, keepdims=True)
    tc = t - mu
    var = jnp.mean(tc * tc, axis=-1, keepdims=True)
    rstd = lax.rsqrt(var + LN_EPS)
    xhat = tc * rstd
    return xhat * g + b, xhat, rstd


def _ln_bwd(dy, xhat, rstd, g):
    dxh = dy * g
    m1 = jnp.mean(dxh, axis=-1, keepdims=True)
    m2 = jnp.mean(dxh * xhat, axis=-1, keepdims=True)
    return rstd * (dxh - m1 - xhat * m2)


def _colsum(a):
    return jnp.sum(a, axis=0, keepdims=True)


def _up(x, d, rows, fill):
    n = x.shape[0]
    return jnp.where(rows < n - d, pltpu.roll(x, n - d, 0), fill)


SUB = 8
TILE_STEPS = (1, 2, 4)


def _r8(width):
    return lax.broadcasted_iota(jnp.int32, (SUB, width), 0)


def _scan_real(a, u, carry, reverse=False):
    r8 = _r8(a.shape[1])
    n = a.shape[0] // SUB
    outs = [None] * n
    for k in (reversed(range(n)) if reverse else range(n)):
        A, U = a[SUB * k:SUB * k + SUB], u[SUB * k:SUB * k + SUB]
        for d in TILE_STEPS:
            m = (r8 < SUB - d) if reverse else (r8 >= d)
            sh = SUB - d if reverse else d
            U = A * jnp.where(m, pltpu.roll(U, sh, 0), 0.0) + U
            A = A * jnp.where(m, pltpu.roll(A, sh, 0), 1.0)
        h = A * carry + U
        outs[k] = h
        carry = h[0:1] if reverse else h[SUB - 1:SUB]
    return jnp.concatenate(outs, axis=0), carry


def _tile_powers(lr, li, reverse=False):
    width = lr.shape[1]
    r8 = _r8(width)
    steps = []
    pr, pi = lr, li
    er, ei = jnp.broadcast_to(lr, (SUB, width)), jnp.broadcast_to(li, (SUB, width))
    for d in TILE_STEPS:
        m = (r8 < SUB - d) if reverse else (r8 >= d)
        sh = SUB - d if reverse else d
        steps.append((sh, jnp.where(m, pr, 0.0), jnp.where(m, pi, 0.0)))
        er, ei = _cmul(er, ei, jnp.where(m, pltpu.roll(er, sh, 0), 1.0), jnp.where(m, pltpu.roll(ei, sh, 0), 0.0))
        pr, pi = _cmul(pr, pi, pr, pi)
    return steps, (er, ei)


def _scan_lti(xr, xi, carry, steps, e, reverse=False):
    er, ei = e
    kr, ki = carry
    n = xr.shape[0] // SUB
    outr, outi = [None] * n, [None] * n
    for k in (reversed(range(n)) if reverse else range(n)):
        sr, si = xr[SUB * k:SUB * k + SUB], xi[SUB * k:SUB * k + SUB]
        for sh, pr, pi in steps:
            shr, shi = pltpu.roll(sr, sh, 0), pltpu.roll(si, sh, 0)
            sr, si = sr + (pr * shr - pi * shi), si + (pr * shi + pi * shr)
        sr = sr + (er * kr - ei * ki)
        si = si + (er * ki + ei * kr)
        outr[k], outi[k] = sr, si
        kr, ki = (sr[0:1], si[0:1]) if reverse else (sr[SUB - 1:SUB], si[SUB - 1:SUB])
    return jnp.concatenate(outr, axis=0), jnp.concatenate(outi, axis=0), (kr, ki)


def _halo(ref, c, r0):
    rp = pl.multiple_of(jnp.maximum(r0 - 8, 0), 8)
    return jnp.where(c > 0, ref[pl.ds(rp, 8), :], 0.0)


def _conv_taps(xe):
    return [pltpu.roll(xe, 3, 0)[8:, :], pltpu.roll(xe, 2, 0)[8:, :], pltpu.roll(xe, 1, 0)[8:, :], xe[8:, :]]


def _rg_gates(h, wa, wx, ba, bx, sp):
    r = _sigmoid(_mm(h, wa) + ba)
    i = _sigmoid(_mm(h, wx) + bx)
    log_a = (-RG_C) * r * sp
    a = jnp.exp(log_a)
    mult = jnp.sqrt(-jnp.tanh(log_a) * (a * a + 1.0))
    return r, i, a, mult


def _softplus(y):
    return jnp.maximum(y, 0.0) + jnp.log1p(jnp.exp(-jnp.abs(y)))


def _after(token):
    return ([], []) if token is None else ([token], [ANY])


def _inproj_fwd(x, w_in, token=None):
    L = x.shape[0]

    def body(x_ref, w_ref, *rest):
        rest[-1][...] = _mm(x_ref[...], w_ref[...])

    extra, extra_specs = _after(token)
    tm = min(TM_MM, L)
    return pl.pallas_call(
        body, name="inproj_fwd", grid=(L // tm,),
        in_specs=[pl.BlockSpec((tm, D_MODEL), lambda i: (i, 0)), pl.BlockSpec((D_MODEL, Z_W), lambda i: (0, 0))] + extra_specs,
        out_specs=pl.BlockSpec((tm, Z_W), lambda i: (i, 0)),
        out_shape=_S((L, Z_W)), compiler_params=_params(1))(x, w_in, *extra)


def _inproj_bwd(dt1, x, dzx, dzg, dzu, w_in):
    L = x.shape[0]

    def body(dt1_ref, x_ref, dzx_ref, dzg_ref, dzu_ref, w_ref, dx_ref, dw_ref, acc_ref):
        @pl.when(pl.program_id(0) == 0)
        def _():
            acc_ref[...] = jnp.zeros_like(acc_ref)
        dzg = dzg_ref[...]
        dz = jnp.concatenate([dzx_ref[...], dzg[:, :RG_W], dzu_ref[...], dzg[:, RG_W:]], axis=1).astype(MXU)
        xb = x_ref[...].astype(MXU)
        dx_ref[...] = ALPHA * dt1_ref[...] + _mm_nt(dz, w_ref[...])
        for j in range(N_DEV):
            acc_ref[j] += _mm_tn(xb, dz[:, j * W_BLK:(j + 1) * W_BLK])

        @pl.when(pl.program_id(0) == L // TM - 1)
        def _():
            dw_ref[...] = acc_ref[...].astype(WIRE)

    row = lambda w: pl.BlockSpec((TM, w), lambda i: (i, 0))
    wspec = pl.BlockSpec((N_DEV, D_MODEL, W_BLK), lambda i: (0, 0, 0))
    return pl.pallas_call(
        body, name="inproj_bwd", grid=(L // TM,),
        in_specs=[row(D_MODEL), row(D_MODEL), row(RG_W), row(D_MODEL), row(S5_W),
                  pl.BlockSpec((D_MODEL, Z_W), lambda i: (0, 0))],
        out_specs=[row(D_MODEL), wspec],
        out_shape=[_S((L, D_MODEL)), _S((N_DEV, D_MODEL, W_BLK), WIRE)],
        scratch_shapes=[pltpu.VMEM((N_DEV, D_MODEL, W_BLK), F32)],
        compiler_params=_params(1))(dt1, x, dzx, dzg, dzu, w_in)


TM2 = 512


def _dz_block(dzx_ref, dzg_ref, dzu_ref):
    dzg = dzg_ref[...]
    return jnp.concatenate([dzx_ref[...], dzg[:, :RG_W], dzu_ref[...], dzg[:, RG_W:]], axis=1).astype(MXU)


def _inproj_bwd_dw(x, dzx, dzg, dzu, token=None):
    L = x.shape[0]
    extra, extra_specs = _after(token)

    def body(x_ref, dzx_ref, dzg_ref, dzu_ref, *rest):
        dw_ref, acc_ref = rest[len(extra):]
        @pl.when(pl.program_id(0) == 0)
        def _():
            acc_ref[...] = jnp.zeros_like(acc_ref)
        dz = _dz_block(dzx_ref, dzg_ref, dzu_ref)
        xb = x_ref[...].astype(MXU)
        for j in range(N_DEV):
            acc_ref[j] += _mm_tn(xb, dz[:, j * W_BLK:(j + 1) * W_BLK])

        @pl.when(pl.program_id(0) == L // TM2 - 1)
        def _():
            dw_ref[...] = acc_ref[...].astype(WIRE)

    row = lambda w: pl.BlockSpec((TM2, w), lambda i: (i, 0))
    wspec = pl.BlockSpec((N_DEV, D_MODEL, W_BLK), lambda i: (0, 0, 0))
    return pl.pallas_call(
        body, name="inproj_bwd_dw", grid=(L // TM2,),
        in_specs=[row(D_MODEL), row(RG_W), row(D_MODEL), row(S5_W)] + extra_specs, out_specs=wspec,
        out_shape=_S((N_DEV, D_MODEL, W_BLK), WIRE), scratch_shapes=[pltpu.VMEM((N_DEV, D_MODEL, W_BLK), F32)],
        compiler_params=_params(1))(x, dzx, dzg, dzu, *extra)


def _inproj_bwd_dx(dt1, dzx, dzg, dzu, w_in, token=None):
    L = dt1.shape[0]
    extra, extra_specs = _after(token)

    def body(dt1_ref, dzx_ref, dzg_ref, dzu_ref, w_ref, *rest):
        rest[-1][...] = ALPHA * dt1_ref[...] + _mm_nt(_dz_block(dzx_ref, dzg_ref, dzu_ref), w_ref[...])

    tm = min(TM_MM, L)
    row = lambda w: pl.BlockSpec((tm, w), lambda i: (i, 0))
    return pl.pallas_call(
        body, name="inproj_bwd_dx", grid=(L // tm,),
        in_specs=[row(D_MODEL), row(RG_W), row(D_MODEL), row(S5_W), _full((D_MODEL, Z_W))] + extra_specs,
        out_specs=row(D_MODEL), out_shape=_S((L, D_MODEL)), compiler_params=_params(1))(dt1, dzx, dzg, dzu, w_in, *extra)


def _rg_specs(layer):
    tile = lambda rows: pl.BlockSpec((rows, LANE), lambda c: (0, c))
    ptile = lambda rows: pl.BlockSpec((None, rows, LANE), lambda c: (layer, 0, c))
    pheads = pl.BlockSpec((None, 2, RG_HD, RG_HD), lambda c: (layer, c, 0, 0))
    return tile, ptile, pheads, pl.BlockSpec((2, RG_HD, RG_HD), lambda c: (c, 0, 0))


RG_HD = 64


def _bd2(w):
    z = jnp.zeros((RG_HD, RG_HD), w.dtype)
    return jnp.concatenate([jnp.concatenate([w[0], z], axis=1), jnp.concatenate([z, w[1]], axis=1)], axis=0)


def _bd2_diag(m):
    return jnp.stack([m[:RG_HD, :RG_HD], m[RG_HD:, RG_HD:]])


def _rg_fwd(z, cw, cb, wa_bd, wx_bd, ba, bx, lam, layer):
    L = z.shape[0]
    RC = min(RC_RG, L)

    def body(x_ref, cw_ref, cb_ref, wa_ref, wx_ref, ba_ref, bx_ref, lam_ref, hs_ref, *saved):
        w, b = cw_ref[...], cb_ref[...]
        wa, wx, ba_, bx_ = _bd2(wa_ref[...]).astype(MXU), _bd2(wx_ref[...]).astype(MXU), ba_ref[...], bx_ref[...]
        sp = _softplus(-lam_ref[...])

        def step(c, carry):
            r0 = pl.multiple_of(c * RC, RC)
            xe = jnp.concatenate([_halo(x_ref, c, r0), x_ref[pl.ds(r0, RC), :]], axis=0)
            t = _conv_taps(xe)
            h = t[0] * w[0:1] + t[1] * w[1:2] + t[2] * w[2:3] + t[3] * w[3:4] + b
            r, i, a, mult = _rg_gates(h, wa, wx, ba_, bx_, sp)
            hs, carry = _scan_real(a, mult * (i * h), carry)
            hs_ref[pl.ds(r0, RC), :] = hs
            for ref, val in zip(saved, (h, r, i, a, mult)):
                ref[pl.ds(r0, RC), :] = val
            return carry

        lax.fori_loop(0, L // RC, step, jnp.zeros((1, LANE), F32))

    tile, ptile, pheads, _ = _rg_specs(layer)
    return pl.pallas_call(
        body, name="rg_fwd", grid=(N_RG_T,),
        in_specs=[tile(L), ptile(4), ptile(1), pheads, pheads, ptile(1), ptile(1), ptile(1)],
        out_specs=[tile(L)] * 6, out_shape=[_S((L, RG_W))] * 6, compiler_params=_params(1))(
            z, cw, cb, wa_bd, wx_bd, ba, bx, lam)


def _rg_bwd(dhs, z, hs, gates, cw, wa_bd, wx_bd, lam, layer):
    L = z.shape[0]
    RC = min(RC_RG, L)

    def body(g_ref, x_ref, hs_ref, h_ref, r_ref, i_ref, a_ref, mult_ref, cw_ref, wa_ref, wx_ref, lam_ref,
             dx_ref, dcw_ref, dcb_ref, dwa_out, dwx_out, dba_ref, dbx_ref, dlam_ref, dwa_ref, dwx_ref):
        w = cw_ref[...]
        wa, wx = _bd2(wa_ref[...]).astype(MXU), _bd2(wx_ref[...]).astype(MXU)
        lam = lam_ref[...]
        sp = _softplus(-lam)
        rows = lax.broadcasted_iota(jnp.int32, (RC, LANE), 0)
        for ref in (dcw_ref, dcb_ref, dwa_ref, dwx_ref, dba_ref, dbx_ref, dlam_ref):
            ref[...] = jnp.zeros_like(ref)
        nch = L // RC

        def step(k, carry):
            cin, nxt = carry
            c = nch - 1 - k
            r0 = pl.multiple_of(c * RC, RC)
            xe = jnp.concatenate([_halo(x_ref, c, r0), x_ref[pl.ds(r0, RC), :]], axis=0)
            t = _conv_taps(xe)
            h, r, i, a, mult = (ref[pl.ds(r0, RC), :] for ref in (h_ref, r_ref, i_ref, a_ref, mult_ref))
            hs_e = jnp.concatenate([_halo(hs_ref, c, r0), hs_ref[pl.ds(r0, RC), :]], axis=0)
            hs_prev = pltpu.roll(hs_e, 1, 0)[8:, :]
            g = g_ref[pl.ds(r0, RC), :]
            cc, cin_new = _scan_real(a, a * g, cin, reverse=True)
            dh = g + _up(cc, 1, rows, cin)
            ih = i * h
            dlog_a = dh * hs_prev * a - (dh * ih) * (a * a) / mult
            di = dh * mult * h
            dhin = dh * mult * i
            dr = dlog_a * ((-RG_C) * sp)
            dlam_ref[...] += _colsum(dlog_a * r)
            dra = dr * r * (1.0 - r)
            dia = di * i * (1.0 - i)
            dwa_ref[...] += _mm_tn(h, dra)
            dwx_ref[...] += _mm_tn(h, dia)
            dba_ref[...] += _colsum(dra)
            dbx_ref[...] += _colsum(dia)
            dhin = dhin + _mm_nt(dra, wa) + _mm_nt(dia, wx)
            de = jnp.concatenate([dhin, nxt], axis=0)
            n = RC + 8
            dx = (dhin * w[3:4] + pltpu.roll(de, n - 1, 0)[:RC, :] * w[2:3]
                  + pltpu.roll(de, n - 2, 0)[:RC, :] * w[1:2] + pltpu.roll(de, n - 3, 0)[:RC, :] * w[0:1])
            dx_ref[pl.ds(r0, RC), :] = dx
            for kk in range(4):
                dcw_ref[kk:kk + 1, :] += _colsum(dhin * t[kk])
            dcb_ref[...] += _colsum(dhin)
            return cin_new, dhin[0:8, :]

        lax.fori_loop(0, nch, step, (jnp.zeros((1, LANE), F32), jnp.zeros((8, LANE), F32)))
        dlam_ref[...] = dlam_ref[...] * (RG_C * _sigmoid(-lam))
        dwa_out[...], dwx_out[...] = _bd2_diag(dwa_ref[...]), _bd2_diag(dwx_ref[...])

    tile, ptile, pheads, gheads = _rg_specs(layer)
    heads = _S((2 * N_RG_T, RG_HD, RG_HD))
    return pl.pallas_call(
        body, name="rg_bwd", grid=(N_RG_T,),
        in_specs=[tile(L)] * 8 + [ptile(4), pheads, pheads, ptile(1)],
        out_specs=[tile(L), tile(4), tile(1), gheads, gheads, tile(1), tile(1), tile(1)],
        out_shape=[_S((L, RG_W)), _S((4, RG_W)), _S((1, RG_W)), heads, heads, _S((1, RG_W)), _S((1, RG_W)), _S((1, RG_W))],
        scratch_shapes=[pltpu.VMEM((LANE, LANE), F32), pltpu.VMEM((LANE, LANE), F32)],
        compiler_params=_params(1))(dhs, z, hs, *gates, cw, wa_bd, wx_bd, lam)


def _cmul(ar, ai, br, bi):
    return ar * br - ai * bi, ar * bi + ai * br


S5_TW = S5_N // N_S5_T


S5_H = 16
S5_GT = LANE // S5_H


def _s5_specs(L, layer):
    in_tile = pl.BlockSpec((L, LANE), lambda t: (0, t))
    st = pl.BlockSpec((L, S5_TW), lambda t: (0, t))
    pg = pl.BlockSpec((None, S5_GT, S5_H, S5_P), lambda t: (layer * N_S5_T + t, 0, 0, 0))
    plb = pl.BlockSpec((None, S5_GT, S5_P), lambda t: (layer * N_S5_T + t, 0, 0))
    gg = pl.BlockSpec((None, S5_GT, S5_H, S5_P), lambda t: (t, 0, 0, 0))
    glb = pl.BlockSpec((None, S5_GT, S5_P), lambda t: (t, 0, 0))
    dv = pl.BlockSpec((1, LANE), lambda t: (0, t))
    return in_tile, st, pg, plb, gg, glb, dv


def _bd8(blocks):
    rows = []
    for g in range(S5_GT):
        pieces = [blocks[g]]
        if g:
            pieces.insert(0, jnp.zeros((S5_H, S5_P * g), blocks.dtype))
        if g < S5_GT - 1:
            pieces.append(jnp.zeros((S5_H, S5_P * (S5_GT - 1 - g)), blocks.dtype))
        rows.append(jnp.concatenate(pieces, axis=1))
    return jnp.concatenate(rows, axis=0)


def _bd8_diag(m):
    return jnp.stack([m[S5_H * g:S5_H * (g + 1), S5_P * g:S5_P * (g + 1)] for g in range(S5_GT)])


def _row8(v):
    return jnp.concatenate([v[g:g + 1] for g in range(S5_GT)], axis=1)


def _row8_split(r):
    return jnp.concatenate([r[:, S5_P * g:S5_P * (g + 1)] for g in range(S5_GT)], axis=0)


def _layer_row_tile(layer):
    return pl.BlockSpec((None, 1, LANE), lambda t: (layer, 0, t))


def _s5_fwd(z, bb_re, bb_im, lb_re, lb_im, c_re, c_im, dvec, layer):
    L = z.shape[0]

    def body(u_ref, bbr_ref, bbi_ref, lr_ref, li_ref, cr_ref, ci_ref, d_ref, y_ref, sr_ref, si_ref):
        bbr, bbi = _bd8(bbr_ref[...]).astype(MXU), _bd8(bbi_ref[...]).astype(MXU)
        cr, ci = _bd8(cr_ref[...]).astype(MXU), _bd8(ci_ref[...]).astype(MXU)
        dv = d_ref[...]
        steps, e = _tile_powers(_row8(lr_ref[...]), _row8(li_ref[...]))

        def step(c, carry):
            r0 = pl.multiple_of(c * RC, RC)
            u = u_ref[pl.ds(r0, RC), :]
            ub = u.astype(MXU)
            sr = jnp.dot(ub, bbr, preferred_element_type=F32)
            si = jnp.dot(ub, bbi, preferred_element_type=F32)
            sr, si, carry = _scan_lti(sr, si, carry, steps, e)
            sr_ref[pl.ds(r0, RC), :] = sr
            si_ref[pl.ds(r0, RC), :] = si
            y_ref[pl.ds(r0, RC), :] = dv * u + (_mm_nt(sr, cr) - _mm_nt(si, ci))
            return carry

        zero = jnp.zeros((1, S5_TW), F32)
        lax.fori_loop(0, L // RC, step, (zero, zero))

    in_tile, st, pg, plb, _, _, _ = _s5_specs(L, layer)
    u_tile = pl.BlockSpec((L, LANE), lambda t: (0, C_S5U // LANE + t))
    return pl.pallas_call(
        body, name="s5_fwd", grid=(N_S5_T,),
        in_specs=[u_tile, pg, pg, plb, plb, pg, pg, _layer_row_tile(layer)],
        out_specs=[in_tile, st, st],
        out_shape=[_S((L, S5_W)), _S((L, S5_N)), _S((L, S5_N))],
        compiler_params=_params(1))(z, bb_re, bb_im, lb_re, lb_im, c_re, c_im, dvec)


def _s5_bwd(dy0, z, s_re, s_im, bb_re, bb_im, lb_re, lb_im, c_re, c_im, dvec, layer, token=None):
    L = z.shape[0]
    extra, extra_specs = _after(token)

    def body(dy_ref, u_ref, sr_ref, si_ref, bbr_ref, bbi_ref, lr_ref, li_ref, cr_ref, ci_ref, d_ref, *rest):
        (du_ref, dbbr_out, dbbi_out, dlr_out, dli_out, dcr_out, dci_out, dd_ref,
         dbbr_ref, dbbi_ref, dcr_ref, dci_ref, dlr_ref, dli_ref) = rest[len(extra):]
        bbr, bbi = _bd8(bbr_ref[...]).astype(MXU), _bd8(bbi_ref[...]).astype(MXU)
        cr, ci = _bd8(cr_ref[...]).astype(MXU), _bd8(ci_ref[...]).astype(MXU)
        lr, li = _row8(lr_ref[...]), -_row8(li_ref[...])
        dv = d_ref[...]
        steps, e = _tile_powers(lr, li, reverse=True)
        for ref in (dbbr_ref, dbbi_ref, dlr_ref, dli_ref, dcr_ref, dci_ref, dd_ref):
            ref[...] = jnp.zeros_like(ref)
        nch = L // RC

        def step(k, carry):
            c = nch - 1 - k
            r0 = pl.multiple_of(c * RC, RC)
            dy = dy_ref[pl.ds(r0, RC), :]
            u = u_ref[pl.ds(r0, RC), :]
            dyb, ub = dy.astype(MXU), u.astype(MXU)
            sr, si = sr_ref[pl.ds(r0, RC), :], si_ref[pl.ds(r0, RC), :]
            dcr_ref[...] += _mm_tn(dyb, sr)
            dci_ref[...] -= _mm_tn(dyb, si)
            gr = jnp.dot(dyb, cr, preferred_element_type=F32)
            gi = -jnp.dot(dyb, ci, preferred_element_type=F32)
            gr, gi, carry = _scan_lti(gr, gi, carry, steps, e, reverse=True)
            pr_ = pltpu.roll(jnp.concatenate([_halo(sr_ref, c, r0), sr], axis=0), 1, 0)[8:, :]
            pi_ = pltpu.roll(jnp.concatenate([_halo(si_ref, c, r0), si], axis=0), 1, 0)[8:, :]
            dlr_ref[...] += _colsum(pr_ * gr + pi_ * gi)
            dli_ref[...] += _colsum(pr_ * gi - pi_ * gr)
            grb, gib = gr.astype(MXU), gi.astype(MXU)
            dbbr_ref[...] += _mm_tn(ub, grb)
            dbbi_ref[...] += _mm_tn(ub, gib)
            du_ref[pl.ds(r0, RC), :] = dv * dy + (_mm_nt(grb, bbr) + _mm_nt(gib, bbi))
            dd_ref[...] += _colsum(dy * u)
            return carry

        zero = jnp.zeros((1, S5_TW), F32)
        lax.fori_loop(0, nch, step, (zero, zero))
        dbbr_out[...], dbbi_out[...] = _bd8_diag(dbbr_ref[...]), _bd8_diag(dbbi_ref[...])
        dcr_out[...], dci_out[...] = _bd8_diag(dcr_ref[...]), _bd8_diag(dci_ref[...])
        dlr_out[...], dli_out[...] = _row8_split(dlr_ref[...]), _row8_split(dli_ref[...])

    in_tile, st, pg, plb, gg, glb, dv = _s5_specs(L, layer)
    u_tile = pl.BlockSpec((L, LANE), lambda t: (0, C_S5U // LANE + t))
    groups, rows = _S((N_S5_T, S5_GT, S5_H, S5_P)), _S((N_S5_T, S5_GT, S5_P))
    wide = pltpu.VMEM((LANE, S5_TW), F32)
    return pl.pallas_call(
        body, name="s5_bwd", grid=(N_S5_T,),
        in_specs=[in_tile, u_tile, st, st, pg, pg, plb, plb, pg, pg, _layer_row_tile(layer)] + extra_specs,
        out_specs=[in_tile, gg, gg, glb, glb, gg, gg, dv],
        out_shape=[_S((L, S5_W)), groups, groups, rows, rows, groups, groups, _S((1, S5_W))],
        scratch_shapes=[wide, wide, wide, wide, pltpu.VMEM((1, S5_TW), F32), pltpu.VMEM((1, S5_TW), F32)],
        compiler_params=_params(1))(dy0, z, s_re, s_im, bb_re, bb_im, lb_re, lb_im, c_re, c_im, dvec, *extra)


def _disc(ar, ai, ls):
    dt = jnp.exp(ls)
    mag = jnp.exp(ar * dt)
    lr = mag * jnp.cos(ai * dt)
    li = mag * jnp.sin(ai * dt)
    den = ar * ar + ai * ai
    cr = ((lr - 1.0) * ar + li * ai) / den
    ci = (li * ar - (lr - 1.0) * ai) / den
    return lr, li, cr, ci


def _s5_disc_fwd(ar, ai, ls, token=None):
    extra, extra_specs = _after(token)

    def body(ar_ref, ai_ref, ls_ref, *rest):
        lr_ref, li_ref, cr_ref, ci_ref = rest[len(extra):]
        lr, li, cr, ci = _disc(ar_ref[...], ai_ref[...], ls_ref[...])
        lr_ref[...], li_ref[...], cr_ref[...], ci_ref[...] = lr, li, cr, ci

    sh = _S(ar.shape)
    vm = pl.BlockSpec(memory_space=pltpu.VMEM)
    return pl.pallas_call(body, name="s5_disc_fwd", in_specs=[vm, vm, vm] + extra_specs, out_shape=[sh, sh, sh, sh])(
        ar, ai, ls, *extra)


def _s5_disc_bwd(ar, ai, ls, dlr, dli, dcr, dci):
    def body(ar_ref, ai_ref, ls_ref, dlr_ref, dli_ref, dcr_ref, dci_ref, dar_ref, dai_ref, dls_ref):
        _, vjp = jax.vjp(_disc, ar_ref[...], ai_ref[...], jnp.broadcast_to(ls_ref[...], ar_ref.shape))
        dar, dai, dls = vjp((dlr_ref[...], dli_ref[...], dcr_ref[...], dci_ref[...]))
        dar_ref[...], dai_ref[...] = dar, dai
        dls_ref[...] = jnp.sum(dls, axis=1, keepdims=True)

    return pl.pallas_call(body, name="s5_disc_bwd", out_shape=[_S(ar.shape), _S(ar.shape), _S(ls.shape)])(
        ar, ai, ls, dlr, dli, dcr, dci)


def _s5_bscale_fwd(cr, ci, br, bi):
    def body(cr_ref, ci_ref, br_ref, bi_ref, or_ref, oi_ref):
        or_ref[...], oi_ref[...] = _cmul(cr_ref[...], ci_ref[...], br_ref[...], bi_ref[...])

    return pl.pallas_call(body, name="s5_bscale_fwd", out_shape=[_S(br.shape), _S(br.shape)])(cr, ci, br, bi)


def _s5_bscale_bwd(cr, ci, br, bi, gr, gi):
    def body(cr_ref, ci_ref, br_ref, bi_ref, gr_ref, gi_ref, dbr_ref, dbi_ref, dcr_ref, dci_ref):
        cr_, ci_, br_, bi_, gr_, gi_ = (r[...] for r in (cr_ref, ci_ref, br_ref, bi_ref, gr_ref, gi_ref))
        dbr_ref[...] = cr_ * gr_ + ci_ * gi_
        dbi_ref[...] = cr_ * gi_ - ci_ * gr_
        dcr_ref[...] = jnp.sum(gr_ * br_ + gi_ * bi_, axis=1, keepdims=True)
        dci_ref[...] = jnp.sum(gi_ * br_ - gr_ * bi_, axis=1, keepdims=True)

    return pl.pallas_call(body, name="s5_bscale_bwd",
                          out_shape=[_S(br.shape), _S(br.shape), _S(cr.shape), _S(cr.shape)])(cr, ci, br, bi, gr, gi)


def _row(w):
    return pl.BlockSpec((TM, w), lambda i: (i, 0))


def _full(shape):
    return pl.BlockSpec(tuple(shape), lambda i: (0,) * len(shape))


def _p_rows(layer):
    return pl.BlockSpec((None, None, TM, PLE_D), lambda i: (layer, 0, i, 0))


def _lrow(layer, width):
    return pl.BlockSpec((None, 1, width), lambda i: (layer, 0, 0))


def _post_fwd(x, hs, z, y0, p, w_glu, b_glu, w_out, g1, b1, ple_w, w_pg, b_pg, g2, b2, layer):
    L = x.shape[0]

    def body(x_ref, hs_ref, z_ref, y0_ref, p_ref, wg_ref, bg_ref, wo_ref, g1_ref, b1_ref, pw_ref, wpg_ref, bpg_ref,
             g2_ref, b2_ref, x2_ref, xh1_ref, xh2_ref, q_ref, gt_ref, rstd1_ref, rstd2_ref):
        rg_gate = z_ref[:, C_RGG:C_RGG + RG_W]
        s5_gate = z_ref[:, C_S5G:C_S5G + S5_W]
        rg_y = hs_ref[...] * _silu_and_grad(rg_gate)[0]
        y1 = _gelu(y0_ref[...])
        gl = _sigmoid(_mm(y1, wg_ref[...]) + bg_ref[...])
        s5_y = (y1 * gl) * _silu_and_grad(s5_gate)[0]
        mix = _mm(jnp.concatenate([rg_y.astype(MXU), s5_y.astype(MXU)], axis=1), wo_ref[...])
        t1 = ALPHA * x_ref[...] + mix
        x1, xh1, rstd1 = _ln_fwd(t1, g1_ref[...], b1_ref[...])
        q = _mm(p_ref[...], pw_ref[...])
        gt = _sigmoid(_mm(x1, wpg_ref[...]) + bpg_ref[...])
        t2 = ALPHA * x1 + q * gt
        x2, xh2, rstd2 = _ln_fwd(t2, g2_ref[...], b2_ref[...])
        x2_ref[...], xh1_ref[...], xh2_ref[...], q_ref[...], gt_ref[...] = x2, xh1, xh2, q, gt
        rstd1_ref[...], rstd2_ref[...] = rstd1, rstd2

    vec = _lrow(layer, D_MODEL)
    return pl.pallas_call(
        body, name="post_fwd", grid=(L // TM,),
        in_specs=[_row(D_MODEL), _row(RG_W), _row(Z_W), _row(S5_W), _p_rows(layer), _full((S5_W, S5_W)), _lrow(layer, S5_W),
                  _full((D_MODEL, D_MODEL)), vec, vec, _full((PLE_D, D_MODEL)), _full((D_MODEL, D_MODEL)), vec, vec, vec],
        out_specs=[_row(D_MODEL)] * 5 + [_row(1)] * 2, out_shape=[_S((L, D_MODEL))] * 5 + [_S((L, 1))] * 2,
        compiler_params=_params(1))(x, hs, z, y0, p, w_glu, b_glu, w_out, g1, b1, ple_w, w_pg, b_pg, g2, b2)


def _post_bwd_a(dx2_or_target, is_top, xh2, xh1, rstd2, rstd1, q, gt, p, w_pg, g1, b1, g2, b2, layer, token=None):
    L = xh1.shape[0]
    extra, extra_specs = _after(token)

    def body(d_ref, xh2_ref, xh1_ref, rstd2_ref, rstd1_ref, q_ref, gt_ref, p_ref, wpg_ref, g1_ref, b1_ref, g2_ref,
             b2_ref, *rest):
        (dt1_ref, dpw_out, dwpg_out, dbpg_ref, dg1_ref, db1_ref, dg2_ref, db2_ref, loss_ref, dpw_ref,
         dwpg_ref) = rest[len(extra):]
        @pl.when(pl.program_id(0) == 0)
        def _():
            for ref in (dpw_ref, dwpg_ref, dbpg_ref, dg1_ref, db1_ref, dg2_ref, db2_ref, loss_ref):
                ref[...] = jnp.zeros_like(ref)

        g1, g2 = g1_ref[...], g2_ref[...]
        xh1, xh2, rstd1, rstd2 = xh1_ref[...], xh2_ref[...], rstd1_ref[...], rstd2_ref[...]
        x1 = xh1 * g1 + b1_ref[...]
        if is_top:
            err = (xh2 * g2 + b2_ref[...]) - d_ref[...]
            loss_ref[...] += _colsum(err * err)
            dx2 = err * (1.0 / D_MODEL)
        else:
            dx2 = d_ref[...]
        p = p_ref[...]
        q, gt = q_ref[...], gt_ref[...]
        dg2_ref[...] += _colsum(dx2 * xh2)
        db2_ref[...] += _colsum(dx2)
        dt2 = _ln_bwd(dx2, xh2, rstd2, g2)
        dq = dt2 * gt
        dgpre = (dt2 * q) * gt * (1.0 - gt)
        dpw_ref[...] += _mm_tn(p, dq)
        dwpg_ref[...] += _mm_tn(x1, dgpre)
        dbpg_ref[...] += _colsum(dgpre)
        dx1 = ALPHA * dt2 + _mm_nt(dgpre, wpg_ref[...])
        dg1_ref[...] += _colsum(dx1 * xh1)
        db1_ref[...] += _colsum(dx1)
        dt1_ref[...] = _ln_bwd(dx1, xh1, rstd1, g1)

        @pl.when(pl.program_id(0) == L // TM - 1)
        def _():
            dpw_out[...] = dpw_ref[...].astype(WIRE)
            dwpg_out[...] = dwpg_ref[...].astype(WIRE)

    vec, lvec = _full((1, D_MODEL)), _lrow(layer, D_MODEL)
    return pl.pallas_call(
        body, name="post_bwd_a_top" if is_top else "post_bwd_a", grid=(L // TM,),
        in_specs=[_row(D_MODEL), _row(D_MODEL), _row(D_MODEL), _row(1), _row(1), _row(D_MODEL), _row(D_MODEL), _p_rows(layer),
                  _full((D_MODEL, D_MODEL)), lvec, lvec, lvec, lvec] + extra_specs,
        out_specs=[_row(D_MODEL), _full((PLE_D, D_MODEL)), _full((D_MODEL, D_MODEL)), vec, vec, vec, vec, vec, vec],
        out_shape=[_S((L, D_MODEL)), _S((PLE_D, D_MODEL), WIRE), _S((D_MODEL, D_MODEL), WIRE)] + [_S((1, D_MODEL))] * 6,
        scratch_shapes=[pltpu.VMEM((PLE_D, D_MODEL), F32), pltpu.VMEM((D_MODEL, D_MODEL), F32)],
        compiler_params=_params(1))(dx2_or_target, xh2, xh1, rstd2, rstd1, q, gt, p, w_pg, g1, b1, g2, b2, *extra)


def _post_bwd_b(dt1, z, hs, y0, w_out, w_glu, b_glu, layer):
    L = dt1.shape[0]

    def body(dt1_ref, z_ref, hs_ref, y0_ref, wo_ref, wg_ref, bg_ref,
             dhs_ref, dy0_ref, dzg_ref, dwo_out, dwg_out, dbg_ref, dwo_ref, dwg_ref):
        @pl.when(pl.program_id(0) == 0)
        def _():
            for ref in (dwo_ref, dwg_ref, dbg_ref):
                ref[...] = jnp.zeros_like(ref)

        dt1b = dt1_ref[...].astype(MXU)
        dm = _mm_nt(dt1b, wo_ref[...])
        d_rgy, d_s5y = dm[:, :RG_W], dm[:, RG_W:]
        rg_gate = z_ref[:, C_RGG:C_RGG + RG_W]
        s5_gate = z_ref[:, C_S5G:C_S5G + S5_W]
        hs = hs_ref[...]
        sl, dsl = _silu_and_grad(rg_gate)
        dhs_ref[...] = d_rgy * sl
        dzg_ref[:, :RG_W] = d_rgy * hs * dsl
        y0 = y0_ref[...]
        y1 = _gelu(y0)
        gl = _sigmoid(_mm(y1, wg_ref[...]) + bg_ref[...])
        y2 = y1 * gl
        sl2, dsl = _silu_and_grad(s5_gate)
        m = jnp.concatenate([(hs * sl).astype(MXU), (y2 * sl2).astype(MXU)], axis=1)
        dwo_ref[...] += _mm_tn(m, dt1b)
        dy2 = d_s5y * sl2
        dzg_ref[:, RG_W:] = d_s5y * y2 * dsl
        dglpre = (dy2 * y1) * gl * (1.0 - gl)
        dwg_ref[...] += _mm_tn(y1, dglpre)
        dbg_ref[...] += _colsum(dglpre)
        dy1 = dy2 * gl + _mm_nt(dglpre, wg_ref[...])
        dy0_ref[...] = dy1 * _gelu_grad(y0)

        @pl.when(pl.program_id(0) == L // TM - 1)
        def _():
            dwo_out[...] = dwo_ref[...].astype(WIRE)
            dwg_out[...] = dwg_ref[...].astype(WIRE)

    return pl.pallas_call(
        body, name="post_bwd_b", grid=(L // TM,),
        in_specs=[_row(D_MODEL), _row(Z_W), _row(RG_W), _row(S5_W), _full((D_MODEL, D_MODEL)),
                  _full((S5_W, S5_W)), _lrow(layer, S5_W)],
        out_specs=[_row(RG_W), _row(S5_W), _row(D_MODEL), _full((D_MODEL, D_MODEL)), _full((S5_W, S5_W)), _full((1, S5_W))],
        out_shape=[_S((L, RG_W)), _S((L, S5_W)), _S((L, D_MODEL)), _S((D_MODEL, D_MODEL), WIRE), _S((S5_W, S5_W), WIRE),
                   _S((1, S5_W))],
        scratch_shapes=[pltpu.VMEM((D_MODEL, D_MODEL), F32), pltpu.VMEM((S5_W, S5_W), F32)],
        compiler_params=_params(1))(dt1, z, hs, y0, w_out, w_glu, b_glu)


def _adamw(parts, w, m, v, token=None):
    nl = len(parts)
    extra, extra_specs = _after(token)
    n, R, C = parts[0].shape
    tr = R
    for cand in (512, 256, 128, 64, 32, 16, 8):
        if R % cand == 0 and n * cand * C * 4 <= 4 * 1024 * 1024:
            tr = cand
            break
    nblk = R // tr

    def body(*refs):
        p_refs = refs[:nl]
        w_ref, m_ref, v_ref = refs[nl:nl + 3]
        g_ref, d_ref, nm_ref, nv_ref = refs[nl + 3 + len(extra):]
        layer = pl.program_id(0)
        g = None
        for li, p_ref in enumerate(p_refs):
            s = p_ref[0].astype(F32)
            for k in range(1, n):
                s = s + p_ref[k].astype(F32)
            g = s if g is None else jnp.where(layer == li, s, g)
        nm = B1 * m_ref[...] + (1.0 - B1) * g
        nv = B2 * v_ref[...] + (1.0 - B2) * (g * g)
        d_ref[...] = (-LR) * ((nm / BC1) / (jnp.sqrt(nv / BC2) + EPS) + WD * w_ref[...])
        g_ref[...], nm_ref[...], nv_ref[...] = g, nm, nv

    def part_spec(li):
        return pl.BlockSpec((n, tr, C), lambda l, i: (0, jnp.where(l == li, i, jnp.where(l < li, 0, nblk - 1)), 0))

    blk = pl.BlockSpec((tr, C), lambda l, i: (l * nblk + i, 0))
    return pl.pallas_call(
        body, name="adamw", grid=(nl, nblk),
        in_specs=[part_spec(li) for li in range(nl)] + [blk, blk, blk] + extra_specs,
        out_specs=[blk] * 4, out_shape=[_S((nl * R, C))] * 4, compiler_params=_params(2))(*parts, w, m, v, *extra)


def _adamw_natural(names, g, w, m, v, name):
    n = len(names)

    def body(*refs):
        for j in range(n):
            g_ref, w_ref, m_ref, v_ref, d_ref, nm_ref, nv_ref = (refs[k * n + j] for k in range(7))
            gj = g_ref[...]
            nm = B1 * m_ref[...] + (1.0 - B1) * gj
            nv = B2 * v_ref[...] + (1.0 - B2) * (gj * gj)
            d_ref[...] = (-LR) * ((nm / BC1) / (jnp.sqrt(nv / BC2) + EPS) + WD * w_ref[...])
            nm_ref[...], nv_ref[...] = nm, nv

    ins = [t[k] for t in (g, w, m, v) for k in names]
    outs = pl.pallas_call(body, name=name, out_shape=[_S(w[k].shape) for _ in range(3) for k in names],
                          compiler_params=pltpu.CompilerParams(vmem_limit_bytes=VMEM_LIMIT))(*ins)
    return [{k: outs[t * n + j] for j, k in enumerate(names)} for t in range(3)]


def _me():
    return lax.axis_index("x"), lax.axis_index("y"), lax.axis_index("c")


def _lin(dev):
    return 4 * dev[0] + 2 * dev[1] + dev[2]


def _blk(ref, axis, size, idx):
    nd = len(ref.shape)
    start = idx * size
    if axis == nd - 1 and size % LANE == 0:
        start = pl.multiple_of(start, LANE)
    elif axis == nd - 2 and size % 16 == 0:
        start = pl.multiple_of(start, 16)
    ix = [slice(None)] * nd
    ix[axis] = pl.ds(start, size)
    return ref.at[tuple(ix)]


HBM_SPEC = pl.BlockSpec(memory_space=pltpu.HBM)
SEM_SPEC = pl.BlockSpec(memory_space=pltpu.SEMAPHORE)
EFFECT = pltpu.SideEffectType.DATAFLOW_SIDE_EFFECTING


def _peers(x, y, c):
    flip = lambda v, f: 1 - v if f else v
    return [(flip(x, k & 4), flip(y, k & 2), flip(c, k & 1)) for k in range(1, N_DEV)]


def _land_shape(mode, s, axis):
    if mode == "gather":
        return s.shape[:axis] + (N_DEV * s.shape[axis],) + s.shape[axis + 1:]
    return (N_DEV,) + s.shape[:axis] + (s.shape[axis] // N_DEV,) + s.shape[axis + 1:]


def _src_view(mode, ref, axis, peer):
    return ref if mode == "gather" else _blk(ref, axis, ref.shape[axis] // N_DEV, peer)


def _dst_view(mode, land, axis, sender):
    return _blk(land, axis, land.shape[axis] // N_DEV, sender) if mode == "gather" else land.at[sender]


def _seven_blocks(mode, land, axis):
    if mode == "gather":
        ix = [slice(None)] * len(land.shape)
        ix[axis] = pl.ds(0, (N_DEV - 1) * (land.shape[axis] // N_DEV))
        return land.at[tuple(ix)]
    return land.at[pl.ds(0, N_DEV - 1)]


def _place_own(mode, srcs, axes, name, after=None):
    n = len(srcs)
    extra, extra_specs = _after(after)

    def body(me_ref, *refs):
        for a in range(n):
            out = refs[n + len(extra) + a]
            out[...] = refs[a][...].reshape(out.shape)

    def at_me(shape, axis):
        return lambda i, me: tuple(me[0] if d == axis else 0 for d in range(len(shape)))

    in_specs, out_specs = [], []
    for s, axis in zip(srcs, axes):
        if mode == "gather":
            in_specs.append(pl.BlockSpec(s.shape, lambda i, me, nd=len(s.shape): (0,) * nd))
            out_specs.append(pl.BlockSpec(s.shape, at_me(s.shape, axis)))
        else:
            blk = s.shape[:axis] + (s.shape[axis] // N_DEV,) + s.shape[axis + 1:]
            in_specs.append(pl.BlockSpec(blk, at_me(blk, axis)))
            out_specs.append(pl.BlockSpec((1,) + blk, at_me((1,) + blk, 0)))
    me = _lin(_me()).astype(jnp.int32).reshape(1)
    return pl.pallas_call(
        body, name=name, out_shape=[_S(_land_shape(mode, s, a), s.dtype) for s, a in zip(srcs, axes)],
        grid_spec=pltpu.PrefetchScalarGridSpec(num_scalar_prefetch=1, grid=(1,), in_specs=in_specs + extra_specs,
                                               out_specs=out_specs),
        compiler_params=_params(1))(me, *srcs, *extra)


def _place_shards(shards, layers, axes, dtypes, name, after=None):
    n = len(shards)
    extra, extra_specs = _after(after)

    def body(me_ref, *refs):
        for a in range(n):
            out = refs[n + len(extra) + a]
            out[...] = refs[a][...].astype(out.dtype)

    in_specs, out_specs, out_shape = [], [], []
    for s, layer, axis, dt in zip(shards, layers, axes, dtypes):
        shape = s.shape if layer is None else s.shape[1:]
        nd = len(shape)
        if layer is None:
            in_specs.append(pl.BlockSpec(shape, lambda i, me, nd=nd: (0,) * nd))
        else:
            in_specs.append(pl.BlockSpec((None,) + shape, lambda i, me, nd=nd, layer=layer: (layer,) + (0,) * nd))
        out_specs.append(pl.BlockSpec(shape, lambda i, me, nd=nd, axis=axis: tuple(me[0] if d == axis else 0 for d in range(nd))))
        out_shape.append(_S(shape[:axis] + (N_DEV * shape[axis],) + shape[axis + 1:], dt))
    me = _lin(_me()).astype(jnp.int32).reshape(1)
    return pl.pallas_call(
        body, name=name, out_shape=out_shape,
        grid_spec=pltpu.PrefetchScalarGridSpec(num_scalar_prefetch=1, grid=(1,), in_specs=in_specs + extra_specs,
                                               out_specs=out_specs),
        compiler_params=_params(1))(me, *shards, *extra)


def _push_start(mode, srcs, lands, axes, name):
    n, ns = len(lands), len(srcs)

    def body(*refs):
        src_refs, land_refs = refs[:ns], refs[ns:ns + n]
        send_sems, recv_sems = refs[ns + n], refs[ns + n + 1]
        token = refs[-1]
        x, y, c = _me()
        me = _lin((x, y, c))
        for a in range(n):
            mine = _dst_view(mode, land_refs[a], axes[a], me)
            for peer in _peers(x, y, c):
                pltpu.make_async_remote_copy(
                    src_ref=_src_view(mode, src_refs[a], axes[a], _lin(peer)) if ns else mine, dst_ref=mine,
                    send_sem=send_sems.at[a], recv_sem=recv_sems.at[a], device_id=peer, device_id_type=MESH).start()
        token[...] = jnp.zeros_like(token)

    hbm = lambda s: pltpu.HBM(s.shape, s.dtype)
    outs = pl.pallas_call(
        body, name=name,
        out_shape=(pltpu.SemaphoreType.DMA((n,)), pltpu.SemaphoreType.DMA((n,)), *[hbm(s) for s in srcs], *[hbm(s) for s in lands],
                   _S((SUB, LANE))),
        in_specs=[HBM_SPEC] * (ns + n),
        out_specs=(SEM_SPEC, SEM_SPEC, *[HBM_SPEC] * (ns + n), pl.BlockSpec(memory_space=pltpu.VMEM)),
        input_output_aliases={i: 2 + i for i in range(ns + n)},
        compiler_params=pltpu.CompilerParams(has_side_effects=EFFECT),
    )(*[pltpu.with_memory_space_constraint(s, pltpu.HBM) for s in list(srcs) + list(lands)])
    return outs[0], outs[1], outs[2:2 + ns], outs[2 + ns:2 + ns + n], outs[-1]


def _push_wait(mode, send_sems, recv_sems, srcs, lands, axes, after, name, first=0):
    n, ns = len(lands), len(srcs)
    after = list(after) if isinstance(after, (list, tuple)) else [after]

    def body(*refs):
        land_refs = refs[ns:ns + n]
        send_sems, recv_sems = refs[ns + n], refs[ns + n + 1]
        x, y, c = _me()
        for a in range(n):
            seven = _seven_blocks(mode, land_refs[a], axes[a])
            cp = pltpu.make_async_remote_copy(src_ref=seven, dst_ref=seven, send_sem=send_sems.at[first + a],
                                              recv_sem=recv_sems.at[first + a],
                                              device_id=(x, y, 1 - c), device_id_type=MESH)
            cp.wait_send()
            cp.wait_recv()

    hbm = lambda s: pltpu.HBM(s.shape, s.dtype)
    outs = pl.pallas_call(
        body, name=name, out_shape=tuple(hbm(s) for s in list(srcs) + list(lands)),
        in_specs=[HBM_SPEC] * (ns + n) + [SEM_SPEC, SEM_SPEC] + [ANY] * len(after), out_specs=tuple([HBM_SPEC] * (ns + n)),
        input_output_aliases={i: i for i in range(ns + n)},
        compiler_params=pltpu.CompilerParams(has_side_effects=EFFECT),
    )(*srcs, *lands, send_sems, recv_sems, *after)
    return outs[ns:]


def _sum_parts(parts):
    n, R, C = parts.shape

    def body(p_ref, o_ref):
        g = p_ref[0]
        for k in range(1, n):
            g = g + p_ref[k]
        o_ref[...] = g

    return pl.pallas_call(body, name="sum_parts", out_shape=_S((R, C)))(parts)


SMALL =['conv_b', 'rg_wa', 'rg_ba', 'rg_wx', 'rg_bx', 'rg_lambda', 's5_a_re', 's5_a_im', 's5_b_re', 's5_b_im',
         's5_c_re', 's5_c_im', 's5_d', 's5_log_step', 's5_b_glu', 'ln1_g', 'ln1_b', 'ple_gate_b', 'ln2_g', 'ln2_b']
WEIGHTS = ['w_in', 'conv_w', 'conv_b', 'rg_wa', 'rg_ba', 'rg_wx', 'rg_bx', 'rg_lambda', 's5_a_re', 's5_a_im', 's5_b_re',
           's5_b_im', 's5_c_re', 's5_c_im', 's5_d', 's5_log_step', 's5_w_glu', 's5_b_glu', 'w_out', 'ln1_g', 'ln1_b',
           'ple_w', 'ple_gate_w', 'ple_gate_b', 'ln2_g', 'ln2_b']
PACK_ROWS_MULT = 64


def _pack(tree, scalar):
    flat = jnp.concatenate([tree[k].reshape(-1) for k in SMALL] + [scalar.reshape(1)])
    rows = -(-flat.shape[0] // (LANE * PACK_ROWS_MULT)) * PACK_ROWS_MULT
    return jnp.pad(flat, (0, rows * LANE - flat.shape[0])).reshape(rows, LANE)


def _unpack(packed, like):
    flat, out, o = packed.reshape(-1), {}, 0
    for k in SMALL:
        n = math.prod(like[k].shape)
        out[k] = flat[o:o + n].reshape(like[k].shape)
        o += n
    return out, flat[o]


class _NoHooks:
    token = None
    first_token = None

    def first_weights(self, full, after):
        return full

    def layer_start(self, i, W, after):
        return W

    def late_weights(self, i, W, after):
        return W

    def post_done(self, i, g):
        return None

    def smalls_done(self, grads, loss):
        self.small = _small_grads(grads, self.res)
        return None

    def w_in_done(self, i, g):
        return None

    def layer_done(self, i, g, dx):
        return None


def _local_grads(x, p, target, W, disc, hooks):
    depth = 2
    saved = []
    for i in range(depth):
        if i > 0:
            W = hooks.layer_start(i, W, x)
        w = W[i]
        z = _inproj_fwd(x, w['w_in'], hooks.token if i == 0 else None)
        hs, *gates = _rg_fwd(z, w['conv_w'], w['conv_b'], w['wa_bd'], w['wx_bd'], w['rg_ba'], w['rg_bx'], w['rg_lambda'], i)
        d = disc[i]
        y0, s_re, s_im = _s5_fwd(z, d['bb_re'], d['bb_im'], d['lb_re'], d['lb_im'], d['c_re'], d['c_im'], w['s5_d'], i)
        W = hooks.late_weights(i, W, y0)
        w = W[i]
        x2, *norms = _post_fwd(x, hs, z, y0, p, w['s5_w_glu'], w['s5_b_glu'], w['w_out'], w['ln1_g'], w['ln1_b'],
                               w['ple_w'], w['ple_gate_w'], w['ple_gate_b'], w['ln2_g'], w['ln2_b'], i)
        saved.append((x, z, hs, gates, y0, s_re, s_im, norms))
        x = x2

    grads = [None] * depth
    dx = target
    loss = None
    token = None
    for i in reversed(range(depth)):
        w, d = W[i], disc[i]
        xin, z, hs, gates, y0, s_re, s_im, (xh1, xh2, q, gt, rstd1, rstd2) = saved[i]
        g = {}
        (dt1, g['ple_w'], g['ple_gate_w'], g['ple_gate_b'], g['ln1_g'], g['ln1_b'], g['ln2_g'], g['ln2_b'], lrow) = _post_bwd_a(
            dx, i == depth - 1, xh2, xh1, rstd2, rstd1, q, gt, p, w['ple_gate_w'], w['ln1_g'], w['ln1_b'],
            w['ln2_g'], w['ln2_b'], i, token)
        if i == depth - 1:
            loss = 0.5 / D_MODEL * jnp.sum(lrow)
        dhs, dy0, dzg, g['w_out'], g['s5_w_glu'], g['s5_b_glu'] = _post_bwd_b(dt1, z, hs, y0, w['w_out'], w['s5_w_glu'],
                                                                           w['s5_b_glu'], i)
        (dzu, g['bb_re'], g['bb_im'], g['lb_re'], g['lb_im'], g['c_re'], g['c_im'], g['s5_d']) = _s5_bwd(
            dy0, z, s_re, s_im, d['bb_re'], d['bb_im'], d['lb_re'], d['lb_im'], d['c_re'], d['c_im'], w['s5_d'], i,
            hooks.post_done(i, g))
        (dzx, g['conv_w'], g['conv_b'], g['wa_bd'], g['wx_bd'], g['rg_ba'], g['rg_bx'], g['rg_lambda']) = _rg_bwd(
            dhs, z, hs, gates, w['conv_w'], w['wa_bd'], w['wx_bd'], w['rg_lambda'], i)
        if i == 0:
            g['w_in'] = _inproj_bwd_dw(xin, dzx, dzg, dzu, hooks.smalls_done([g, grads[1]], loss))
            dx = _inproj_bwd_dx(dt1, dzx, dzg, dzu, w['w_in'], hooks.w_in_done(i, g))
        else:
            dx, g['w_in'] = _inproj_bwd(dt1, xin, dzx, dzg, dzu, w['w_in'])
        grads[i] = g
        token = hooks.layer_done(i, g, dx)
    return loss, dx, grads


def _s5_layouts_fwd(s5_a_re, s5_a_im, s5_log_step, s5_b_re, s5_b_im, s5_c_re, s5_c_im, token=None):
    depth = s5_a_re.shape[0]
    ar, ai = s5_a_re.reshape(depth * 24, S5_P), s5_a_im.reshape(depth * 24, S5_P)
    ls = s5_log_step.reshape(depth * 24, 1)
    lr, li, cr, ci = _s5_disc_fwd(ar, ai, ls, token)
    per_group = lambda a: a.reshape(depth * 24, 1, S5_P)
    as_c = lambda b: jnp.swapaxes(b, 2, 3).reshape(depth * 24, S5_H, S5_P)
    res = (ar, ai, ls, per_group(cr), per_group(ci), as_c(s5_b_re), as_c(s5_b_im))
    bbr, bbi = _s5_bscale_fwd(*res[3:])
    tiles = lambda a: a.reshape(depth * N_S5_T, S5_GT, S5_H, S5_P)
    rows = lambda a: a.reshape(depth * N_S5_T, S5_GT, S5_P)
    disc = dict(bb_re=tiles(bbr), bb_im=tiles(bbi), lb_re=rows(lr), lb_im=rows(li), c_re=tiles(s5_c_re), c_im=tiles(s5_c_im))
    return [disc] * depth, res


def _s5_layouts_bwd(grads, res):
    ar, ai, ls, cr, ci, br, bi = res
    depth = len(grads)
    stack = lambda k, shape: jnp.stack([g[k] for g in grads]).reshape(shape)
    groups, shape_c = (depth * 24, S5_H, S5_P), (depth, 24, S5_H, S5_P)
    dbr, dbi, dcr, dci = _s5_bscale_bwd(cr, ci, br, bi, stack('bb_re', groups), stack('bb_im', groups))
    gp = (depth * 24, S5_P)
    dar, dai, dls = _s5_disc_bwd(ar, ai, ls, stack('lb_re', gp), stack('lb_im', gp), dcr.reshape(gp), dci.reshape(gp))
    return dict(
        s5_a_re=dar.reshape(depth, 24, S5_P), s5_a_im=dai.reshape(depth, 24, S5_P), s5_log_step=dls.reshape(depth, 24),
        s5_b_re=jnp.swapaxes(dbr.reshape(shape_c), 2, 3), s5_b_im=jnp.swapaxes(dbi.reshape(shape_c), 2, 3),
        s5_c_re=stack('c_re', shape_c), s5_c_im=stack('c_im', shape_c))


LATE = ('w_out', 'ple_w', 'ple_gate_w', 's5_w_glu')


ROWS = ('conv_b', 'rg_ba', 'rg_bx', 'rg_lambda', 's5_d', 's5_b_glu', 'ln1_g', 'ln1_b', 'ple_gate_b', 'ln2_g', 'ln2_b')


def _shared_weights(full):
    depth = full['conv_b'].shape[0]
    shared = {k: full[k].reshape(depth, 1, -1) for k in ROWS}
    shared.update(conv_w=full['conv_w'], wa_bd=full['rg_wa'], wx_bd=full['rg_wx'])
    return shared


def _layer_weights(full, shared, i):
    return dict(shared, w_in=full['w_in'][i])


class _AllLocal(_NoHooks):
    def __init__(self, full):
        self.full = full

    def late_weights(self, i, W, after):
        W[i].update({k: self.full[k][i] for k in LATE})
        return W


def _full_grads(full, x, p, target, hooks=None):
    hooks = hooks or _AllLocal(full)
    disc, res = _s5_layouts_fwd(full['s5_a_re'], full['s5_a_im'], full['s5_log_step'], full['s5_b_re'], full['s5_b_im'],
                                full['s5_c_re'], full['s5_c_im'], hooks.first_token)
    full = hooks.first_weights(full, disc[-1]['bb_im'])
    shared = _shared_weights(full)
    W = [_layer_weights(full, shared, i) for i in range(2)]
    hooks.res = res
    loss, gx, grads = _local_grads(x, p, target, W, disc, hooks)
    out = dict(hooks.small)
    for k in SHARD_AXIS:
        out[k] = [g[k] for g in grads]
    return loss, gx, out


def _small_grads(grads, res):
    stack = lambda f: jnp.stack([f(g) for g in grads])
    out = _s5_layouts_bwd(grads, res)
    out['conv_w'] = stack(lambda g: g['conv_w'])
    for k in ('conv_b', 'rg_ba', 'rg_bx', 'rg_lambda', 's5_b_glu', 'ln1_g', 'ln1_b', 'ple_gate_b', 'ln2_g', 'ln2_b'):
        out[k] = stack(lambda g: g[k][0])
    out['s5_d'] = stack(lambda g: g['s5_d'][0]).reshape(2, 24, 16)
    out['rg_wa'] = stack(lambda g: g['wa_bd'])
    out['rg_wx'] = stack(lambda g: g['wx_bd'])
    return out


SHARD_AXIS = {'w_in': 2, 'w_out': 1, 'ple_w': 2, 'ple_gate_w': 1, 's5_w_glu': 1}


def kernel(x, p, w_in, conv_w, conv_b, rg_wa, rg_ba, rg_wx, rg_bx, rg_lambda, s5_a_re, s5_a_im, s5_b_re, s5_b_im, s5_c_re, s5_c_im, s5_d, s5_log_step, s5_w_glu, s5_b_glu, w_out, ln1_g, ln1_b, ple_w, ple_gate_w, ple_gate_b, ln2_g, ln2_b, loss_target, m_w_in, m_conv_w, m_conv_b, m_rg_wa, m_rg_ba, m_rg_wx, m_rg_bx, m_rg_lambda, m_s5_a_re, m_s5_a_im, m_s5_b_re, m_s5_b_im, m_s5_c_re, m_s5_c_im, m_s5_d, m_s5_log_step, m_s5_w_glu, m_s5_b_glu, m_w_out, m_ln1_g, m_ln1_b, m_ple_w, m_ple_gate_w, m_ple_gate_b, m_ln2_g, m_ln2_b, v_w_in, v_conv_w, v_conv_b, v_rg_wa, v_rg_ba, v_rg_wx, v_rg_bx, v_rg_lambda, v_s5_a_re, v_s5_a_im, v_s5_b_re, v_s5_b_im, v_s5_c_re, v_s5_c_im, v_s5_d, v_s5_log_step, v_s5_w_glu, v_s5_b_glu, v_w_out, v_ln1_g, v_ln1_b, v_ple_w, v_ple_gate_w, v_ple_gate_b, v_ln2_g, v_ln2_b):
    local = dict(locals())
    w = {k: local[k] for k in WEIGHTS}
    mom = {k: local['m_' + k] for k in WEIGHTS}
    var = {k: local['v_' + k] for k in WEIGHTS}

    big = list(SHARD_AXIS)
    late_axes = [SHARD_AXIS[k] - 1 for k in LATE]
    pushed = {}

    groups = dict(first=(['w_in', 'conv_w'], [0, None], [1, 0]), l0=(list(LATE), [0] * len(LATE), late_axes),
                  l1=(['w_in'] + list(LATE), [1] * (1 + len(LATE)), [1] + late_axes))
    token = None
    for key, members in (("first", ["first"]), ("rest", ["l0", "l1"])):
        names, layers, axes = (sum((groups[m][j] for m in members), []) for j in range(3))
        shards = [w[k] if layer is not None else w[k][None] for k, layer in zip(names, layers)]
        lands = _place_shards(shards, layers, axes, [WIRE if k in big else w[k].dtype for k in names],
                              "place_weights_" + key, token)
        pushed[key] = _push_start("gather", [], lands, axes, "push_weights_" + key)
        token = pushed[key][4]

    def await_weights(key, axes, after):
        s, first = (pushed["first"], 0) if key == "first" else (pushed["rest"], 0 if key == "l0" else len(LATE))
        return _push_wait("gather", s[0], s[1], [], s[3][first:first + len(axes)], axes, after, "await_weights_" + key, first)

    def push_grads(key, g, names, axes):
        srcs = [g[k] for k in names]
        pushed[key] = _push_start("scatter", srcs, _place_own("scatter", srcs, axes, "place_grads_" + key), axes,
                                  "push_grads_" + key)
        return pushed[key][4]

    def await_grads(key, axes, after):
        s = pushed[key]
        return _push_wait("scatter", s[0], s[1], s[2], s[3], axes, after, "await_grads_" + key)

    class Overlap(_NoHooks):
        token = pushed["rest"][4]
        first_token = token

        def first_weights(self, full, after):
            w_in0, conv = await_weights("first", [1, 0], after)
            return dict(full, w_in=[w_in0, None], conv_w=jnp.moveaxis(conv, 0, 2).reshape(2, 4, RG_W))

        def late_weights(self, i, W, after):
            if i == 0:
                W[0].update(zip(LATE, await_weights("l0", late_axes, after)))
            return W

        def layer_start(self, i, W, after):
            lands = await_weights("l1", [1] + late_axes, after)
            W[1].update(zip(LATE, lands[1:]), w_in=lands[0])
            return W

        def post_done(self, i, g):
            return push_grads("late0", g, LATE, late_axes) if i == 0 else None

        def smalls_done(self, grads, loss):
            super().smalls_done(grads, loss)
            conv = jnp.moveaxis(self.small['conv_w'].reshape(2, 4, N_DEV, RG_W // N_DEV), 2, 0)
            self.packed = _pack(self.small, loss)
            return push_grads("small", dict(conv_w=conv.reshape(N_DEV, 8, RG_W // N_DEV), small=self.packed),
                              ['conv_w', 'small'], [0, 0])

        def w_in_done(self, i, g):
            return push_grads("w_in0", g, ['w_in'], [0])

        def layer_done(self, i, g, dx):
            return push_grads("all1", g, ['w_in'] + list(LATE), [0] + late_axes) if i == 1 else None

    hooks = Overlap()
    _, grad_x, g = _full_grads(dict(w), x[0], p, loss_target[0], hooks)

    recv1 = dict(zip(['w_in'] + list(LATE), await_grads("all1", [0] + late_axes, grad_x)))
    recv0 = dict(zip(LATE, await_grads("late0", late_axes, grad_x)))
    outs = {}

    def update(k, parts):
        shard = w[k].shape
        c = shard[-1]
        two = lambda a: a.reshape(-1, c)
        res = _adamw([r.reshape(N_DEV, -1, c) for r in parts], two(w[k]), two(mom[k]), two(var[k]))
        outs[k] = [o.reshape(shard) for o in res]

    for k in LATE:
        update(k, [recv0[k], recv1[k]])
    done = [outs[k][1] for k in LATE]
    conv_parts, small_parts = await_grads("small", [0, 0], done)

    rows = hooks.packed.shape[0] // N_DEV
    mine = _sum_parts(small_parts.reshape(N_DEV, rows, LANE))
    sums = _push_start("gather", [mine], _place_own("gather", [mine], [0], "place_small_sums"), [0], "push_small_sums")
    w_in0, = await_grads("w_in0", [0], sums[4])
    update('w_in', [w_in0, recv1['w_in']])
    update('conv_w', [conv_parts])
    gathered, = _push_wait("gather", sums[0], sums[1], sums[2], sums[3], [0], [outs['w_in'][1], outs['conv_w'][1]],
                           "await_small_sums")
    summed, loss = _unpack(gathered, w)
    narrow = ['s5_b_re', 's5_b_im']
    for names, name in ((narrow, "adamw_s5_b"), ([k for k in SMALL if k not in narrow], "adamw_small")):
        delta, new_m, new_v = _adamw_natural(names, summed, w, mom, var, name)
        for k in names:
            outs[k] = [summed[k], delta[k], new_m[k], new_v[k]]

    res = [loss, grad_x[None]]
    for j in range(4):
        res += [outs[k][j] for k in WEIGHTS]
    return tuple(res)
```

```python
import math

import jax
import jax.numpy as jnp
from jax import lax
from jax.experimental import pallas as pl
from jax.experimental.pallas import tpu as pltpu

F32 = jnp.float32
MXU = jnp.bfloat16
WIRE = jnp.bfloat16

N_DEV = 8
D_MODEL = 1024
PLE_D = 256
RG_W = 640
S5_W = 384
S5_P = 64
S5_N = 24 * S5_P
Z_W = 2 * RG_W + 2 * S5_W
C_RGG = RG_W
C_S5U = 2 * RG_W
C_S5G = 2 * RG_W + S5_W
LANE = 128
N_RG_T = RG_W // LANE
N_S5_T = S5_W // LANE
W_BLK = Z_W // N_DEV
ALPHA = (2.0 * 2) ** 0.25
LN_EPS = 1e-5
RG_C = 8.0
LR, B1, B2, EPS, WD, STEP = 0.001, 0.9, 0.999, 1e-08, 0.01, 10
BC1 = 1.0 - B1 ** STEP
BC2 = 1.0 - B2 ** STEP
RC = 512
RC_RG = 1024
TM = 256
TM_MM = 1024
VMEM_LIMIT = 56 * 1024 * 1024

MESH = pl.DeviceIdType.MESH
ANY = pl.BlockSpec(memory_space=pl.ANY)


def _params(n_grid_axes, vmem=VMEM_LIMIT):
    return pltpu.CompilerParams(dimension_semantics=("arbitrary",) * n_grid_axes, vmem_limit_bytes=vmem)


def _S(shape, dtype=F32):
    return jax.ShapeDtypeStruct(tuple(shape), dtype)


def _sigmoid(x):
    return 0.5 * jnp.tanh(0.5 * x) + 0.5


def _silu_and_grad(x):
    s = _sigmoid(x)
    return x * s, s * (1.0 + x * (1.0 - s))


_GELU_C = math.sqrt(2.0 / math.pi)


def _gelu(x):
    return 0.5 * x * (1.0 + jnp.tanh(_GELU_C * (x + 0.044715 * (x * x * x))))


def _gelu_grad(x):
    th = jnp.tanh(_GELU_C * (x + 0.044715 * (x * x * x)))
    return 0.5 * (1.0 + th) + 0.5 * x * (1.0 - th * th) * (_GELU_C * (1.0 + 3.0 * 0.044715 * (x * x)))


def _mm(a, b):
    return jnp.dot(a.astype(MXU), b.astype(MXU), preferred_element_type=F32)


def _mm_nt(a, b):
    return lax.dot_general(a.astype(MXU), b.astype(MXU), (((1,), (1,)), ((), ())), preferred_element_type=F32)


def _mm_tn(a, b):
    return lax.dot_general(a.astype(MXU), b.astype(MXU), (((0,), (0,)), ((), ())), preferred_element_type=F32)


def _ln_fwd(t, g, b):
    mu = jnp.mean(t, axis=-1, keepdims=True)
    tc = t - mu
    var = jnp.mean(tc * tc, axis=-1, keepdims=True)
    rstd = lax.rsqrt(var + LN_EPS)
    xhat = tc * rstd
    return xhat * g + b, xhat, rstd


def _ln_bwd(dy, xhat, rstd, g):
    dxh = dy * g
    m1 = jnp.mean(dxh, axis=-1, keepdims=True)
    m2 = jnp.mean(dxh * xhat, axis=-1, keepdims=True)
    return rstd * (dxh - m1 - xhat * m2)


def _colsum(a):
    return jnp.sum(a, axis=0, keepdims=True)


def _up(x, d, rows, fill):
    n = x.shape[0]
    return jnp.where(rows < n - d, pltpu.roll(x, n - d, 0), fill)


SUB = 8
TILE_STEPS = (1, 2, 4)


def _r8(width):
    return lax.broadcasted_iota(jnp.int32, (SUB, width), 0)


def _scan_real(a, u, carry, reverse=False):
    r8 = _r8(a.shape[1])
    n = a.shape[0] // SUB
    outs = [None] * n
    for k in (reversed(range(n)) if reverse else range(n)):
        A, U = a[SUB * k:SUB * k + SUB], u[SUB * k:SUB * k + SUB]
        for d in TILE_STEPS:
            m = (r8 < SUB - d) if reverse else (r8 >= d)
            sh = SUB - d if reverse else d
            U = A * jnp.where(m, pltpu.roll(U, sh, 0), 0.0) + U
            A = A * jnp.where(m, pltpu.roll(A, sh, 0), 1.0)
        h = A * carry + U
        outs[k] = h
        carry = h[0:1] if reverse else h[SUB - 1:SUB]
    return jnp.concatenate(outs, axis=0), carry


def _tile_powers(lr, li, reverse=False):
    width = lr.shape[1]
    r8 = _r8(width)
    steps = []
    pr, pi = lr, li
    er, ei = jnp.broadcast_to(lr, (SUB, width)), jnp.broadcast_to(li, (SUB, width))
    for d in TILE_STEPS:
        m = (r8 < SUB - d) if reverse else (r8 >= d)
        sh = SUB - d if reverse else d
        steps.append((sh, jnp.where(m, pr, 0.0), jnp.where(m, pi, 0.0)))
        er, ei = _cmul(er, ei, jnp.where(m, pltpu.roll(er, sh, 0), 1.0), jnp.where(m, pltpu.roll(ei, sh, 0), 0.0))
        pr, pi = _cmul(pr, pi, pr, pi)
    return steps, (er, ei)


def _scan_lti(xr, xi, carry, steps, e, reverse=False):
    er, ei = e
    kr, ki = carry
    n = xr.shape[0] // SUB
    outr, outi = [None] * n, [None] * n
    for k in (reversed(range(n)) if reverse else range(n)):
        sr, si = xr[SUB * k:SUB * k + SUB], xi[SUB * k:SUB * k + SUB]
        for sh, pr, pi in steps:
            shr, shi = pltpu.roll(sr, sh, 0), pltpu.roll(si, sh, 0)
            sr, si = sr + (pr * shr - pi * shi), si + (pr * shi + pi * shr)
        sr = sr + (er * kr - ei * ki)
        si = si + (er * ki + ei * kr)
        outr[k], outi[k] = sr, si
        kr, ki = (sr[0:1], si[0:1]) if reverse else (sr[SUB - 1:SUB], si[SUB - 1:SUB])
    return jnp.concatenate(outr, axis=0), jnp.concatenate(outi, axis=0), (kr, ki)


def _halo(ref, c, r0):
    rp = pl.multiple_of(jnp.maximum(r0 - 8, 0), 8)
    return jnp.where(c > 0, ref[pl.ds(rp, 8), :], 0.0)


def _conv_taps(xe):
    return [pltpu.roll(xe, 3, 0)[8:, :], pltpu.roll(xe, 2, 0)[8:, :], pltpu.roll(xe, 1, 0)[8:, :], xe[8:, :]]


def _rg_gates(h, wa, wx, ba, bx, sp):
    r = _sigmoid(_mm(h, wa) + ba)
    i = _sigmoid(_mm(h, wx) + bx)
    log_a = (-RG_C) * r * sp
    a = jnp.exp(log_a)
    mult = jnp.sqrt(-jnp.tanh(log_a) * (a * a + 1.0))
    return r, i, a, mult


def _softplus(y):
    return jnp.maximum(y, 0.0) + jnp.log1p(jnp.exp(-jnp.abs(y)))


def _after(token):
    return ([], []) if token is None else ([token], [ANY])


def _inproj_fwd(x, w_in, token=None):
    L = x.shape[0]

    def body(x_ref, w_ref, *rest):
        rest[-1][...] = _mm(x_ref[...], w_ref[...])

    extra, extra_specs = _after(token)
    tm = min(TM_MM, L)
    return pl.pallas_call(
        body, name="inproj_fwd", grid=(L // tm,),
        in_specs=[pl.BlockSpec((tm, D_MODEL), lambda i: (i, 0)), pl.BlockSpec((D_MODEL, Z_W), lambda i: (0, 0))] + extra_specs,
        out_specs=pl.BlockSpec((tm, Z_W), lambda i: (i, 0)),
        out_shape=_S((L, Z_W)), compiler_params=_params(1))(x, w_in, *extra)


def _inproj_bwd(dt1, x, dzx, dzg, dzu, w_in):
    L = x.shape[0]

    def body(dt1_ref, x_ref, dzx_ref, dzg_ref, dzu_ref, w_ref, dx_ref, dw_ref, acc_ref):
        @pl.when(pl.program_id(0) == 0)
        def _():
            acc_ref[...] = jnp.zeros_like(acc_ref)
        dzg = dzg_ref[...]
        dz = jnp.concatenate([dzx_ref[...], dzg[:, :RG_W], dzu_ref[...], dzg[:, RG_W:]], axis=1).astype(MXU)
        xb = x_ref[...].astype(MXU)
        dx_ref[...] = ALPHA * dt1_ref[...] + _mm_nt(dz, w_ref[...])
        for j in range(N_DEV):
            acc_ref[j] += _mm_tn(xb, dz[:, j * W_BLK:(j + 1) * W_BLK])

        @pl.when(pl.program_id(0) == L // TM - 1)
        def _():
            dw_ref[...] = acc_ref[...].astype(WIRE)

    row = lambda w: pl.BlockSpec((TM, w), lambda i: (i, 0))
    wspec = pl.BlockSpec((N_DEV, D_MODEL, W_BLK), lambda i: (0, 0, 0))
    return pl.pallas_call(
        body, name="inproj_bwd", grid=(L // TM,),
        in_specs=[row(D_MODEL), row(D_MODEL), row(RG_W), row(D_MODEL), row(S5_W),
                  pl.BlockSpec((D_MODEL, Z_W), lambda i: (0, 0))],
        out_specs=[row(D_MODEL), wspec],
        out_shape=[_S((L, D_MODEL)), _S((N_DEV, D_MODEL, W_BLK), WIRE)],
        scratch_shapes=[pltpu.VMEM((N_DEV, D_MODEL, W_BLK), F32)],
        compiler_params=_params(1))(dt1, x, dzx, dzg, dzu, w_in)


TM2 = 512


def _dz_block(dzx_ref, dzg_ref, dzu_ref):
    dzg = dzg_ref[...]
    return jnp.concatenate([dzx_ref[...], dzg[:, :RG_W], dzu_ref[...], dzg[:, RG_W:]], axis=1).astype(MXU)


def _inproj_bwd_dw(x, dzx, dzg, dzu, token=None):
    L = x.shape[0]
    extra, extra_specs = _after(token)

    def body(x_ref, dzx_ref, dzg_ref, dzu_ref, *rest):
        dw_ref, acc_ref = rest[len(extra):]
        @pl.when(pl.program_id(0) == 0)
        def _():
            acc_ref[...] = jnp.zeros_like(acc_ref)
        dz = _dz_block(dzx_ref, dzg_ref, dzu_ref)
        xb = x_ref[...].astype(MXU)
        for j in range(N_DEV):
            acc_ref[j] += _mm_tn(xb, dz[:, j * W_BLK:(j + 1) * W_BLK])

        @pl.when(pl.program_id(0) == L // TM2 - 1)
        def _():
            dw_ref[...] = acc_ref[...].astype(WIRE)

    row = lambda w: pl.BlockSpec((TM2, w), lambda i: (i, 0))
    wspec = pl.BlockSpec((N_DEV, D_MODEL, W_BLK), lambda i: (0, 0, 0))
    return pl.pallas_call(
        body, name="inproj_bwd_dw", grid=(L // TM2,),
        in_specs=[row(D_MODEL), row(RG_W), row(D_MODEL), row(S5_W)] + extra_specs, out_specs=wspec,
        out_shape=_S((N_DEV, D_MODEL, W_BLK), WIRE), scratch_shapes=[pltpu.VMEM((N_DEV, D_MODEL, W_BLK), F32)],
        compiler_params=_params(1))(x, dzx, dzg, dzu, *extra)


def _inproj_bwd_dx(dt1, dzx, dzg, dzu, w_in, token=None):
    L = dt1.shape[0]
    extra, extra_specs = _after(token)

    def body(dt1_ref, dzx_ref, dzg_ref, dzu_ref, w_ref, *rest):
        rest[-1][...] = ALPHA * dt1_ref[...] + _mm_nt(_dz_block(dzx_ref, dzg_ref, dzu_ref), w_ref[...])

    tm = min(TM_MM, L)
    row = lambda w: pl.BlockSpec((tm, w), lambda i: (i, 0))
    return pl.pallas_call(
        body, name="inproj_bwd_dx", grid=(L // tm,),
        in_specs=[row(D_MODEL), row(RG_W), row(D_MODEL), row(S5_W), _full((D_MODEL, Z_W))] + extra_specs,
        out_specs=row(D_MODEL), out_shape=_S((L, D_MODEL)), compiler_params=_params(1))(dt1, dzx, dzg, dzu, w_in, *extra)


def _rg_specs(layer):
    tile = lambda rows: pl.BlockSpec((rows, LANE), lambda c: (0, c))
    ptile = lambda rows: pl.BlockSpec((None, rows, LANE), lambda c: (layer, 0, c))
    pheads = pl.BlockSpec((None, 2, RG_HD, RG_HD), lambda c: (layer, c, 0, 0))
    return tile, ptile, pheads, pl.BlockSpec((2, RG_HD, RG_HD), lambda c: (c, 0, 0))


RG_HD = 64


def _bd2(w):
    z = jnp.zeros((RG_HD, RG_HD), w.dtype)
    return jnp.concatenate([jnp.concatenate([w[0], z], axis=1), jnp.concatenate([z, w[1]], axis=1)], axis=0)


def _bd2_diag(m):
    return jnp.stack([m[:RG_HD, :RG_HD], m[RG_HD:, RG_HD:]])


def _rg_fwd(z, cw, cb, wa_bd, wx_bd, ba, bx, lam, layer):
    L = z.shape[0]
    RC = min(RC_RG, L)

    def body(x_ref, cw_ref, cb_ref, wa_ref, wx_ref, ba_ref, bx_ref, lam_ref, hs_ref, *saved):
        w, b = cw_ref[...], cb_ref[...]
        wa, wx, ba_, bx_ = _bd2(wa_ref[...]).astype(MXU), _bd2(wx_ref[...]).astype(MXU), ba_ref[...], bx_ref[...]
        sp = _softplus(-lam_ref[...])

        def step(c, carry):
            r0 = pl.multiple_of(c * RC, RC)
            xe = jnp.concatenate([_halo(x_ref, c, r0), x_ref[pl.ds(r0, RC), :]], axis=0)
            t = _conv_taps(xe)
            h = t[0] * w[0:1] + t[1] * w[1:2] + t[2] * w[2:3] + t[3] * w[3:4] + b
            r, i, a, mult = _rg_gates(h, wa, wx, ba_, bx_, sp)
            hs, carry = _scan_real(a, mult * (i * h), carry)
            hs_ref[pl.ds(r0, RC), :] = hs
            for ref, val in zip(saved, (h, r, i, a, mult)):
                ref[pl.ds(r0, RC), :] = val
            return carry

        lax.fori_loop(0, L // RC, step, jnp.zeros((1, LANE), F32))

    tile, ptile, pheads, _ = _rg_specs(layer)
    return pl.pallas_call(
        body, name="rg_fwd", grid=(N_RG_T,),
        in_specs=[tile(L), ptile(4), ptile(1), pheads, pheads, ptile(1), ptile(1), ptile(1)],
        out_specs=[tile(L)] * 6, out_shape=[_S((L, RG_W))] * 6, compiler_params=_params(1))(
            z, cw, cb, wa_bd, wx_bd, ba, bx, lam)


def _rg_bwd(dhs, z, hs, gates, cw, wa_bd, wx_bd, lam, layer):
    L = z.shape[0]
    RC = min(RC_RG, L)

    def body(g_ref, x_ref, hs_ref, h_ref, r_ref, i_ref, a_ref, mult_ref, cw_ref, wa_ref, wx_ref, lam_ref,
             dx_ref, dcw_ref, dcb_ref, dwa_out, dwx_out, dba_ref, dbx_ref, dlam_ref, dwa_ref, dwx_ref):
        w = cw_ref[...]
        wa, wx = _bd2(wa_ref[...]).astype(MXU), _bd2(wx_ref[...]).astype(MXU)
        lam = lam_ref[...]
        sp = _softplus(-lam)
        rows = lax.broadcasted_iota(jnp.int32, (RC, LANE), 0)
        for ref in (dcw_ref, dcb_ref, dwa_ref, dwx_ref, dba_ref, dbx_ref, dlam_ref):
            ref[...] = jnp.zeros_like(ref)
        nch = L // RC

        def step(k, carry):
            cin, nxt = carry
            c = nch - 1 - k
            r0 = pl.multiple_of(c * RC, RC)
            xe = jnp.concatenate([_halo(x_ref, c, r0), x_ref[pl.ds(r0, RC), :]], axis=0)
            t = _conv_taps(xe)
            h, r, i, a, mult = (ref[pl.ds(r0, RC), :] for ref in (h_ref, r_ref, i_ref, a_ref, mult_ref))
            hs_e = jnp.concatenate([_halo(hs_ref, c, r0), hs_ref[pl.ds(r0, RC), :]], axis=0)
            hs_prev = pltpu.roll(hs_e, 1, 0)[8:, :]
            g = g_ref[pl.ds(r0, RC), :]
            cc, cin_new = _scan_real(a, a * g, cin, reverse=True)
            dh = g + _up(cc, 1, rows, cin)
            ih = i * h
            dlog_a = dh * hs_prev * a - (dh * ih) * (a * a) / mult
            di = dh * mult * h
            dhin = dh * mult * i
            dr = dlog_a * ((-RG_C) * sp)
            dlam_ref[...] += _colsum(dlog_a * r)
            dra = dr * r * (1.0 - r)
            dia = di * i * (1.0 - i)
            dwa_ref[...] += _mm_tn(h, dra)
            dwx_ref[...] += _mm_tn(h, dia)
            dba_ref[...] += _colsum(dra)
            dbx_ref[...] += _colsum(dia)
            dhin = dhin + _mm_nt(dra, wa) + _mm_nt(dia, wx)
            de = jnp.concatenate([dhin, nxt], axis=0)
            n = RC + 8
            dx = (dhin * w[3:4] + pltpu.roll(de, n - 1, 0)[:RC, :] * w[2:3]
                  + pltpu.roll(de, n - 2, 0)[:RC, :] * w[1:2] + pltpu.roll(de, n - 3, 0)[:RC, :] * w[0:1])
            dx_ref[pl.ds(r0, RC), :] = dx
            for kk in range(4):
                dcw_ref[kk:kk + 1, :] += _colsum(dhin * t[kk])
            dcb_ref[...] += _colsum(dhin)
            return cin_new, dhin[0:8, :]

        lax.fori_loop(0, nch, step, (jnp.zeros((1, LANE), F32), jnp.zeros((8, LANE), F32)))
        dlam_ref[...] = dlam_ref[...] * (RG_C * _sigmoid(-lam))
        dwa_out[...], dwx_out[...] = _bd2_diag(dwa_ref[...]), _bd2_diag(dwx_ref[...])

    tile, ptile, pheads, gheads = _rg_specs(layer)
    heads = _S((2 * N_RG_T, RG_HD, RG_HD))
    return pl.pallas_call(
        body, name="rg_bwd", grid=(N_RG_T,),
        in_specs=[tile(L)] * 8 + [ptile(4), pheads, pheads, ptile(1)],
        out_specs=[tile(L), tile(4), tile(1), gheads, gheads, tile(1), tile(1), tile(1)],
        out_shape=[_S((L, RG_W)), _S((4, RG_W)), _S((1, RG_W)), heads, heads, _S((1, RG_W)), _S((1, RG_W)), _S((1, RG_W))],
        scratch_shapes=[pltpu.VMEM((LANE, LANE), F32), pltpu.VMEM((LANE, LANE), F32)],
        compiler_params=_params(1))(dhs, z, hs, *gates, cw, wa_bd, wx_bd, lam)


def _cmul(ar, ai, br, bi):
    return ar * br - ai * bi, ar * bi + ai * br


S5_TW = S5_N // N_S5_T


S5_H = 16
S5_GT = LANE // S5_H


def _s5_specs(L, layer):
    in_tile = pl.BlockSpec((L, LANE), lambda t: (0, t))
    st = pl.BlockSpec((L, S5_TW), lambda t: (0, t))
    pg = pl.BlockSpec((None, S5_GT, S5_H, S5_P), lambda t: (layer * N_S5_T + t, 0, 0, 0))
    plb = pl.BlockSpec((None, S5_GT, S5_P), lambda t: (layer * N_S5_T + t, 0, 0))
    gg = pl.BlockSpec((None, S5_GT, S5_H, S5_P), lambda t: (t, 0, 0, 0))
    glb = pl.BlockSpec((None, S5_GT, S5_P), lambda t: (t, 0, 0))
    dv = pl.BlockSpec((1, LANE), lambda t: (0, t))
    return in_tile, st, pg, plb, gg, glb, dv


def _bd8(blocks):
    rows = []
    for g in range(S5_GT):
        pieces = [blocks[g]]
        if g:
            pieces.insert(0, jnp.zeros((S5_H, S5_P * g), blocks.dtype))
        if g < S5_GT - 1:
            pieces.append(jnp.zeros((S5_H, S5_P * (S5_GT - 1 - g)), blocks.dtype))
        rows.append(jnp.concatenate(pieces, axis=1))
    return jnp.concatenate(rows, axis=0)


def _bd8_diag(m):
    return jnp.stack([m[S5_H * g:S5_H * (g + 1), S5_P * g:S5_P * (g + 1)] for g in range(S5_GT)])


def _row8(v):
    return jnp.concatenate([v[g:g + 1] for g in range(S5_GT)], axis=1)


def _row8_split(r):
    return jnp.concatenate([r[:, S5_P * g:S5_P * (g + 1)] for g in range(S5_GT)], axis=0)


def _layer_row_tile(layer):
    return pl.BlockSpec((None, 1, LANE), lambda t: (layer, 0, t))


def _s5_fwd(z, bb_re, bb_im, lb_re, lb_im, c_re, c_im, dvec, layer):
    L = z.shape[0]

    def body(u_ref, bbr_ref, bbi_ref, lr_ref, li_ref, cr_ref, ci_ref, d_ref, y_ref, sr_ref, si_ref):
        bbr, bbi = _bd8(bbr_ref[...]).astype(MXU), _bd8(bbi_ref[...]).astype(MXU)
        cr, ci = _bd8(cr_ref[...]).astype(MXU), _bd8(ci_ref[...]).astype(MXU)
        dv = d_ref[...]
        steps, e = _tile_powers(_row8(lr_ref[...]), _row8(li_ref[...]))

        def step(c, carry):
            r0 = pl.multiple_of(c * RC, RC)
            u = u_ref[pl.ds(r0, RC), :]
            ub = u.astype(MXU)
            sr = jnp.dot(ub, bbr, preferred_element_type=F32)
            si = jnp.dot(ub, bbi, preferred_element_type=F32)
            sr, si, carry = _scan_lti(sr, si, carry, steps, e)
            sr_ref[pl.ds(r0, RC), :] = sr
            si_ref[pl.ds(r0, RC), :] = si
            y_ref[pl.ds(r0, RC), :] = dv * u + (_mm_nt(sr, cr) - _mm_nt(si, ci))
            return carry

        zero = jnp.zeros((1, S5_TW), F32)
        lax.fori_loop(0, L // RC, step, (zero, zero))

    in_tile, st, pg, plb, _, _, _ = _s5_specs(L, layer)
    u_tile = pl.BlockSpec((L, LANE), lambda t: (0, C_S5U // LANE + t))
    return pl.pallas_call(
        body, name="s5_fwd", grid=(N_S5_T,),
        in_specs=[u_tile, pg, pg, plb, plb, pg, pg, _layer_row_tile(layer)],
        out_specs=[in_tile, st, st],
        out_shape=[_S((L, S5_W)), _S((L, S5_N)), _S((L, S5_N))],
        compiler_params=_params(1))(z, bb_re, bb_im, lb_re, lb_im, c_re, c_im, dvec)


def _s5_bwd(dy0, z, s_re, s_im, bb_re, bb_im, lb_re, lb_im, c_re, c_im, dvec, layer, token=None):
    L = z.shape[0]
    extra, extra_specs = _after(token)

    def body(dy_ref, u_ref, sr_ref, si_ref, bbr_ref, bbi_ref, lr_ref, li_ref, cr_ref, ci_ref, d_ref, *rest):
        (du_ref, dbbr_out, dbbi_out, dlr_out, dli_out, dcr_out, dci_out, dd_ref,
         dbbr_ref, dbbi_ref, dcr_ref, dci_ref, dlr_ref, dli_ref) = rest[len(extra):]
        bbr, bbi = _bd8(bbr_ref[...]).astype(MXU), _bd8(bbi_ref[...]).astype(MXU)
        cr, ci = _bd8(cr_ref[...]).astype(MXU), _bd8(ci_ref[...]).astype(MXU)
        lr, li = _row8(lr_ref[...]), -_row8(li_ref[...])
        dv = d_ref[...]
        steps, e = _tile_powers(lr, li, reverse=True)
        for ref in (dbbr_ref, dbbi_ref, dlr_ref, dli_ref, dcr_ref, dci_ref, dd_ref):
            ref[...] = jnp.zeros_like(ref)
        nch = L // RC

        def step(k, carry):
            c = nch - 1 - k
            r0 = pl.multiple_of(c * RC, RC)
            dy = dy_ref[pl.ds(r0, RC), :]
            u = u_ref[pl.ds(r0, RC), :]
            dyb, ub = dy.astype(MXU), u.astype(MXU)
            sr, si = sr_ref[pl.ds(r0, RC), :], si_ref[pl.ds(r0, RC), :]
            dcr_ref[...] += _mm_tn(dyb, sr)
            dci_ref[...] -= _mm_tn(dyb, si)
            gr = jnp.dot(dyb, cr, preferred_element_type=F32)
            gi = -jnp.dot(dyb, ci, preferred_element_type=F32)
            gr, gi, carry = _scan_lti(gr, gi, carry, steps, e, reverse=True)
            pr_ = pltpu.roll(jnp.concatenate([_halo(sr_ref, c, r0), sr], axis=0), 1, 0)[8:, :]
            pi_ = pltpu.roll(jnp.concatenate([_halo(si_ref, c, r0), si], axis=0), 1, 0)[8:, :]
            dlr_ref[...] += _colsum(pr_ * gr + pi_ * gi)
            dli_ref[...] += _colsum(pr_ * gi - pi_ * gr)
            grb, gib = gr.astype(MXU), gi.astype(MXU)
            dbbr_ref[...] += _mm_tn(ub, grb)
            dbbi_ref[...] += _mm_tn(ub, gib)
            du_ref[pl.ds(r0, RC), :] = dv * dy + (_mm_nt(grb, bbr) + _mm_nt(gib, bbi))
            dd_ref[...] += _colsum(dy * u)
            return carry

        zero = jnp.zeros((1, S5_TW), F32)
        lax.fori_loop(0, nch, step, (zero, zero))
        dbbr_out[...], dbbi_out[...] = _bd8_diag(dbbr_ref[...]), _bd8_diag(dbbi_ref[...])
        dcr_out[...], dci_out[...] = _bd8_diag(dcr_ref[...]), _bd8_diag(dci_ref[...])
        dlr_out[...], dli_out[...] = _row8_split(dlr_ref[...]), _row8_split(dli_ref[...])

    in_tile, st, pg, plb, gg, glb, dv = _s5_specs(L, layer)
    u_tile = pl.BlockSpec((L, LANE), lambda t: (0, C_S5U // LANE + t))
    groups, rows = _S((N_S5_T, S5_GT, S5_H, S5_P)), _S((N_S5_T, S5_GT, S5_P))
    wide = pltpu.VMEM((LANE, S5_TW), F32)
    return pl.pallas_call(
        body, name="s5_bwd", grid=(N_S5_T,),
        in_specs=[in_tile, u_tile, st, st, pg, pg, plb, plb, pg, pg, _layer_row_tile(layer)] + extra_specs,
        out_specs=[in_tile, gg, gg, glb, glb, gg, gg, dv],
        out_shape=[_S((L, S5_W)), groups, groups, rows, rows, groups, groups, _S((1, S5_W))],
        scratch_shapes=[wide, wide, wide, wide, pltpu.VMEM((1, S5_TW), F32), pltpu.VMEM((1, S5_TW), F32)],
        compiler_params=_params(1))(dy0, z, s_re, s_im, bb_re, bb_im, lb_re, lb_im, c_re, c_im, dvec, *extra)


def _disc(ar, ai, ls):
    dt = jnp.exp(ls)
    mag = jnp.exp(ar * dt)
    lr = mag * jnp.cos(ai * dt)
    li = mag * jnp.sin(ai * dt)
    den = ar * ar + ai * ai
    cr = ((lr - 1.0) * ar + li * ai) / den
    ci = (li * ar - (lr - 1.0) * ai) / den
    return lr, li, cr, ci


def _s5_disc_fwd(ar, ai, ls, token=None):
    extra, extra_specs = _after(token)

    def body(ar_ref, ai_ref, ls_ref, *rest):
        lr_ref, li_ref, cr_ref, ci_ref = rest[len(extra):]
        lr, li, cr, ci = _disc(ar_ref[...], ai_ref[...], ls_ref[...])
        lr_ref[...], li_ref[...], cr_ref[...], ci_ref[...] = lr, li, cr, ci

    sh = _S(ar.shape)
    vm = pl.BlockSpec(memory_space=pltpu.VMEM)
    return pl.pallas_call(body, name="s5_disc_fwd", in_specs=[vm, vm, vm] + extra_specs, out_shape=[sh, sh, sh, sh])(
        ar, ai, ls, *extra)


def _s5_disc_bwd(ar, ai, ls, dlr, dli, dcr, dci):
    def body(ar_ref, ai_ref, ls_ref, dlr_ref, dli_ref, dcr_ref, dci_ref, dar_ref, dai_ref, dls_ref):
        _, vjp = jax.vjp(_disc, ar_ref[...], ai_ref[...], jnp.broadcast_to(ls_ref[...], ar_ref.shape))
        dar, dai, dls = vjp((dlr_ref[...], dli_ref[...], dcr_ref[...], dci_ref[...]))
        dar_ref[...], dai_ref[...] = dar, dai
        dls_ref[...] = jnp.sum(dls, axis=1, keepdims=True)

    return pl.pallas_call(body, name="s5_disc_bwd", out_shape=[_S(ar.shape), _S(ar.shape), _S(ls.shape)])(
        ar, ai, ls, dlr, dli, dcr, dci)


def _s5_bscale_fwd(cr, ci, br, bi):
    def body(cr_ref, ci_ref, br_ref, bi_ref, or_ref, oi_ref):
        or_ref[...], oi_ref[...] = _cmul(cr_ref[...], ci_ref[...], br_ref[...], bi_ref[...])

    return pl.pallas_call(body, name="s5_bscale_fwd", out_shape=[_S(br.shape), _S(br.shape)])(cr, ci, br, bi)


def _s5_bscale_bwd(cr, ci, br, bi, gr, gi):
    def body(cr_ref, ci_ref, br_ref, bi_ref, gr_ref, gi_ref, dbr_ref, dbi_ref, dcr_ref, dci_ref):
        cr_, ci_, br_, bi_, gr_, gi_ = (r[...] for r in (cr_ref, ci_ref, br_ref, bi_ref, gr_ref, gi_ref))
        dbr_ref[...] = cr_ * gr_ + ci_ * gi_
        dbi_ref[...] = cr_ * gi_ - ci_ * gr_
        dcr_ref[...] = jnp.sum(gr_ * br_ + gi_ * bi_, axis=1, keepdims=True)
        dci_ref[...] = jnp.sum(gi_ * br_ - gr_ * bi_, axis=1, keepdims=True)

    return pl.pallas_call(body, name="s5_bscale_bwd",
                          out_shape=[_S(br.shape), _S(br.shape), _S(cr.shape), _S(cr.shape)])(cr, ci, br, bi, gr, gi)


def _row(w):
    return pl.BlockSpec((TM, w), lambda i: (i, 0))


def _full(shape):
    return pl.BlockSpec(tuple(shape), lambda i: (0,) * len(shape))


def _p_rows(layer):
    return pl.BlockSpec((None, None, TM, PLE_D), lambda i: (layer, 0, i, 0))


def _lrow(layer, width):
    return pl.BlockSpec((None, 1, width), lambda i: (layer, 0, 0))


def _post_fwd(x, hs, z, y0, p, w_glu, b_glu, w_out, g1, b1, ple_w, w_pg, b_pg, g2, b2, layer):
    L = x.shape[0]

    def body(x_ref, hs_ref, z_ref, y0_ref, p_ref, wg_ref, bg_ref, wo_ref, g1_ref, b1_ref, pw_ref, wpg_ref, bpg_ref,
             g2_ref, b2_ref, x2_ref, xh1_ref, xh2_ref, q_ref, gt_ref, rstd1_ref, rstd2_ref):
        rg_gate = z_ref[:, C_RGG:C_RGG + RG_W]
        s5_gate = z_ref[:, C_S5G:C_S5G + S5_W]
        rg_y = hs_ref[...] * _silu_and_grad(rg_gate)[0]
        y1 = _gelu(y0_ref[...])
        gl = _sigmoid(_mm(y1, wg_ref[...]) + bg_ref[...])
        s5_y = (y1 * gl) * _silu_and_grad(s5_gate)[0]
        mix = _mm(jnp.concatenate([rg_y.astype(MXU), s5_y.astype(MXU)], axis=1), wo_ref[...])
        t1 = ALPHA * x_ref[...] + mix
        x1, xh1, rstd1 = _ln_fwd(t1, g1_ref[...], b1_ref[...])
        q = _mm(p_ref[...], pw_ref[...])
        gt = _sigmoid(_mm(x1, wpg_ref[...]) + bpg_ref[...])
        t2 = ALPHA * x1 + q * gt
        x2, xh2, rstd2 = _ln_fwd(t2, g2_ref[...], b2_ref[...])
        x2_ref[...], xh1_ref[...], xh2_ref[...], q_ref[...], gt_ref[...] = x2, xh1, xh2, q, gt
        rstd1_ref[...], rstd2_ref[...] = rstd1, rstd2

    vec = _lrow(layer, D_MODEL)
    return pl.pallas_call(
        body, name="post_fwd", grid=(L // TM,),
        in_specs=[_row(D_MODEL), _row(RG_W), _row(Z_W), _row(S5_W), _p_rows(layer), _full((S5_W, S5_W)), _lrow(layer, S5_W),
                  _full((D_MODEL, D_MODEL)), vec, vec, _full((PLE_D, D_MODEL)), _full((D_MODEL, D_MODEL)), vec, vec, vec],
        out_specs=[_row(D_MODEL)] * 5 + [_row(1)] * 2, out_shape=[_S((L, D_MODEL))] * 5 + [_S((L, 1))] * 2,
        compiler_params=_params(1))(x, hs, z, y0, p, w_glu, b_glu, w_out, g1, b1, ple_w, w_pg, b_pg, g2, b2)


def _post_bwd_a(dx2_or_target, is_top, xh2, xh1, rstd2, rstd1, q, gt, p, w_pg, g1, b1, g2, b2, layer, token=None):
    L = xh1.shape[0]
    extra, extra_specs = _after(token)

    def body(d_ref, xh2_ref, xh1_ref, rstd2_ref, rstd1_ref, q_ref, gt_ref, p_ref, wpg_ref, g1_ref, b1_ref, g2_ref,
             b2_ref, *rest):
        (dt1_ref, dpw_out, dwpg_out, dbpg_ref, dg1_ref, db1_ref, dg2_ref, db2_ref, loss_ref, dpw_ref,
         dwpg_ref) = rest[len(extra):]
        @pl.when(pl.program_id(0) == 0)
        def _():
            for ref in (dpw_ref, dwpg_ref, dbpg_ref, dg1_ref, db1_ref, dg2_ref, db2_ref, loss_ref):
                ref[...] = jnp.zeros_like(ref)

        g1, g2 = g1_ref[...], g2_ref[...]
        xh1, xh2, rstd1, rstd2 = xh1_ref[...], xh2_ref[...], rstd1_ref[...], rstd2_ref[...]
        x1 = xh1 * g1 + b1_ref[...]
        if is_top:
            err = (xh2 * g2 + b2_ref[...]) - d_ref[...]
            loss_ref[...] += _colsum(err * err)
            dx2 = err * (1.0 / D_MODEL)
        else:
            dx2 = d_ref[...]
        p = p_ref[...]
        q, gt = q_ref[...], gt_ref[...]
        dg2_ref[...] += _colsum(dx2 * xh2)
        db2_ref[...] += _colsum(dx2)
        dt2 = _ln_bwd(dx2, xh2, rstd2, g2)
        dq = dt2 * gt
        dgpre = (dt2 * q) * gt * (1.0 - gt)
        dpw_ref[...] += _mm_tn(p, dq)
        dwpg_ref[...] += _mm_tn(x1, dgpre)
        dbpg_ref[...] += _colsum(dgpre)
        dx1 = ALPHA * dt2 + _mm_nt(dgpre, wpg_ref[...])
        dg1_ref[...] += _colsum(dx1 * xh1)
        db1_ref[...] += _colsum(dx1)
        dt1_ref[...] = _ln_bwd(dx1, xh1, rstd1, g1)

        @pl.when(pl.program_id(0) == L // TM - 1)
        def _():
            dpw_out[...] = dpw_ref[...].astype(WIRE)
            dwpg_out[...] = dwpg_ref[...].astype(WIRE)

    vec, lvec = _full((1, D_MODEL)), _lrow(layer, D_MODEL)
    return pl.pallas_call(
        body, name="post_bwd_a_top" if is_top else "post_bwd_a", grid=(L // TM,),
        in_specs=[_row(D_MODEL), _row(D_MODEL), _row(D_MODEL), _row(1), _row(1), _row(D_MODEL), _row(D_MODEL), _p_rows(layer),
                  _full((D_MODEL, D_MODEL)), lvec, lvec, lvec, lvec] + extra_specs,
        out_specs=[_row(D_MODEL), _full((PLE_D, D_MODEL)), _full((D_MODEL, D_MODEL)), vec, vec, vec, vec, vec, vec],
        out_shape=[_S((L, D_MODEL)), _S((PLE_D, D_MODEL), WIRE), _S((D_MODEL, D_MODEL), WIRE)] + [_S((1, D_MODEL))] * 6,
        scratch_shapes=[pltpu.VMEM((PLE_D, D_MODEL), F32), pltpu.VMEM((D_MODEL, D_MODEL), F32)],
        compiler_params=_params(1))(dx2_or_target, xh2, xh1, rstd2, rstd1, q, gt, p, w_pg, g1, b1, g2, b2, *extra)


def _post_bwd_b(dt1, z, hs, y0, w_out, w_glu, b_glu, layer):
    L = dt1.shape[0]

    def body(dt1_ref, z_ref, hs_ref, y0_ref, wo_ref, wg_ref, bg_ref,
             dhs_ref, dy0_ref, dzg_ref, dwo_out, dwg_out, dbg_ref, dwo_ref, dwg_ref):
        @pl.when(pl.program_id(0) == 0)
        def _():
            for ref in (dwo_ref, dwg_ref, dbg_ref):
                ref[...] = jnp.zeros_like(ref)

        dt1b = dt1_ref[...].astype(MXU)
        dm = _mm_nt(dt1b, wo_ref[...])
        d_rgy, d_s5y = dm[:, :RG_W], dm[:, RG_W:]
        rg_gate = z_ref[:, C_RGG:C_RGG + RG_W]
        s5_gate = z_ref[:, C_S5G:C_S5G + S5_W]
        hs = hs_ref[...]
        sl, dsl = _silu_and_grad(rg_gate)
        dhs_ref[...] = d_rgy * sl
        dzg_ref[:, :RG_W] = d_rgy * hs * dsl
        y0 = y0_ref[...]
        y1 = _gelu(y0)
        gl = _sigmoid(_mm(y1, wg_ref[...]) + bg_ref[...])
        y2 = y1 * gl
        sl2, dsl = _silu_and_grad(s5_gate)
        m = jnp.concatenate([(hs * sl).astype(MXU), (y2 * sl2).astype(MXU)], axis=1)
        dwo_ref[...] += _mm_tn(m, dt1b)
        dy2 = d_s5y * sl2
        dzg_ref[:, RG_W:] = d_s5y * y2 * dsl
        dglpre = (dy2 * y1) * gl * (1.0 - gl)
        dwg_ref[...] += _mm_tn(y1, dglpre)
        dbg_ref[...] += _colsum(dglpre)
        dy1 = dy2 * gl + _mm_nt(dglpre, wg_ref[...])
        dy0_ref[...] = dy1 * _gelu_grad(y0)

        @pl.when(pl.program_id(0) == L // TM - 1)
        def _():
            dwo_out[...] = dwo_ref[...].astype(WIRE)
            dwg_out[...] = dwg_ref[...].astype(WIRE)

    return pl.pallas_call(
        body, name="post_bwd_b", grid=(L // TM,),
        in_specs=[_row(D_MODEL), _row(Z_W), _row(RG_W), _row(S5_W), _full((D_MODEL, D_MODEL)),
                  _full((S5_W, S5_W)), _lrow(layer, S5_W)],
        out_specs=[_row(RG_W), _row(S5_W), _row(D_MODEL), _full((D_MODEL, D_MODEL)), _full((S5_W, S5_W)), _full((1, S5_W))],
        out_shape=[_S((L, RG_W)), _S((L, S5_W)), _S((L, D_MODEL)), _S((D_MODEL, D_MODEL), WIRE), _S((S5_W, S5_W), WIRE),
                   _S((1, S5_W))],
        scratch_shapes=[pltpu.VMEM((D_MODEL, D_MODEL), F32), pltpu.VMEM((S5_W, S5_W), F32)],
        compiler_params=_params(1))(dt1, z, hs, y0, w_out, w_glu, b_glu)


def _adamw(parts, w, m, v, token=None):
    nl = len(parts)
    extra, extra_specs = _after(token)
    n, R, C = parts[0].shape
    tr = R
    for cand in (512, 256, 128, 64, 32, 16, 8):
        if R % cand == 0 and n * cand * C * 4 <= 4 * 1024 * 1024:
            tr = cand
            break
    nblk = R // tr

    def body(*refs):
        p_refs = refs[:nl]
        w_ref, m_ref, v_ref = refs[nl:nl + 3]
        g_ref, d_ref, nm_ref, nv_ref = refs[nl + 3 + len(extra):]
        layer = pl.program_id(0)
        g = None
        for li, p_ref in enumerate(p_refs):
            s = p_ref[0].astype(F32)
            for k in range(1, n):
                s = s + p_ref[k].astype(F32)
            g = s if g is None else jnp.where(layer == li, s, g)
        nm = B1 * m_ref[...] + (1.0 - B1) * g
        nv = B2 * v_ref[...] + (1.0 - B2) * (g * g)
        d_ref[...] = (-LR) * ((nm / BC1) / (jnp.sqrt(nv / BC2) + EPS) + WD * w_ref[...])
        g_ref[...], nm_ref[...], nv_ref[...] = g, nm, nv

    def part_spec(li):
        return pl.BlockSpec((n, tr, C), lambda l, i: (0, jnp.where(l == li, i, jnp.where(l < li, 0, nblk - 1)), 0))

    blk = pl.BlockSpec((tr, C), lambda l, i: (l * nblk + i, 0))
    return pl.pallas_call(
        body, name="adamw", grid=(nl, nblk),
        in_specs=[part_spec(li) for li in range(nl)] + [blk, blk, blk] + extra_specs,
        out_specs=[blk] * 4, out_shape=[_S((nl * R, C))] * 4, compiler_params=_params(2))(*parts, w, m, v, *extra)


def _adamw_natural(names, g, w, m, v, name):
    n = len(names)

    def body(*refs):
        for j in range(n):
            g_ref, w_ref, m_ref, v_ref, d_ref, nm_ref, nv_ref = (refs[k * n + j] for k in range(7))
            gj = g_ref[...]
            nm = B1 * m_ref[...] + (1.0 - B1) * gj
            nv = B2 * v_ref[...] + (1.0 - B2) * (gj * gj)
            d_ref[...] = (-LR) * ((nm / BC1) / (jnp.sqrt(nv / BC2) + EPS) + WD * w_ref[...])
            nm_ref[...], nv_ref[...] = nm, nv

    ins = [t[k] for t in (g, w, m, v) for k in names]
    outs = pl.pallas_call(body, name=name, out_shape=[_S(w[k].shape) for _ in range(3) for k in names],
                          compiler_params=pltpu.CompilerParams(vmem_limit_bytes=VMEM_LIMIT))(*ins)
    return [{k: outs[t * n + j] for j, k in enumerate(names)} for t in range(3)]


def _me():
    return lax.axis_index("x"), lax.axis_index("y"), lax.axis_index("c")


def _lin(dev):
    return 4 * dev[0] + 2 * dev[1] + dev[2]


def _blk(ref, axis, size, idx):
    nd = len(ref.shape)
    start = idx * size
    if axis == nd - 1 and size % LANE == 0:
        start = pl.multiple_of(start, LANE)
    elif axis == nd - 2 and size % 16 == 0:
        start = pl.multiple_of(start, 16)
    ix = [slice(None)] * nd
    ix[axis] = pl.ds(start, size)
    return ref.at[tuple(ix)]


HBM_SPEC = pl.BlockSpec(memory_space=pltpu.HBM)
SEM_SPEC = pl.BlockSpec(memory_space=pltpu.SEMAPHORE)
EFFECT = pltpu.SideEffectType.DATAFLOW_SIDE_EFFECTING


def _peers(x, y, c):
    flip = lambda v, f: 1 - v if f else v
    return [(flip(x, k & 4), flip(y, k & 2), flip(c, k & 1)) for k in range(1, N_DEV)]


def _land_shape(mode, s, axis):
    if mode == "gather":
        return s.shape[:axis] + (N_DEV * s.shape[axis],) + s.shape[axis + 1:]
    return (N_DEV,) + s.shape[:axis] + (s.shape[axis] // N_DEV,) + s.shape[axis + 1:]


def _src_view(mode, ref, axis, peer):
    return ref if mode == "gather" else _blk(ref, axis, ref.shape[axis] // N_DEV, peer)


def _dst_view(mode, land, axis, sender):
    return _blk(land, axis, land.shape[axis] // N_DEV, sender) if mode == "gather" else land.at[sender]


def _blocks(mode, land, axis, k):
    if mode == "gather":
        ix = [slice(None)] * len(land.shape)
        ix[axis] = pl.ds(0, k * (land.shape[axis] // N_DEV))
        return land.at[tuple(ix)]
    return land.at[pl.ds(0, k)]


ARRIVALS = {None: N_DEV - 1, "near": 4, "relay": 3}


def _routes(route, x, y, c):
    me, sibling = (x, y, c), (x, y, 1 - c)
    chips = [(1 - x, y), (x, 1 - y), (1 - x, 1 - y)]
    if route == "near":
        return [(me, sibling)] + [(me, (*chip, c)) for chip in chips]
    if route == "relay":
        return [((*chip, c), sibling) for chip in chips]
    return [(me, peer) for peer in _peers(x, y, c)]


def _place_own(mode, srcs, axes, name, after=None):
    n = len(srcs)
    extra, extra_specs = _after(after)

    def body(me_ref, *refs):
        for a in range(n):
            out = refs[n + len(extra) + a]
            out[...] = refs[a][...].reshape(out.shape)

    def at_me(shape, axis):
        return lambda i, me: tuple(me[0] if d == axis else 0 for d in range(len(shape)))

    in_specs, out_specs = [], []
    for s, axis in zip(srcs, axes):
        if mode == "gather":
            in_specs.append(pl.BlockSpec(s.shape, lambda i, me, nd=len(s.shape): (0,) * nd))
            out_specs.append(pl.BlockSpec(s.shape, at_me(s.shape, axis)))
        else:
            blk = s.shape[:axis] + (s.shape[axis] // N_DEV,) + s.shape[axis + 1:]
            in_specs.append(pl.BlockSpec(blk, at_me(blk, axis)))
            out_specs.append(pl.BlockSpec((1,) + blk, at_me((1,) + blk, 0)))
    me = _lin(_me()).astype(jnp.int32).reshape(1)
    return pl.pallas_call(
        body, name=name, out_shape=[_S(_land_shape(mode, s, a), s.dtype) for s, a in zip(srcs, axes)],
        grid_spec=pltpu.PrefetchScalarGridSpec(num_scalar_prefetch=1, grid=(1,), in_specs=in_specs + extra_specs,
                                               out_specs=out_specs),
        compiler_params=_params(1))(me, *srcs, *extra)


def _place_shards(shards, layers, axes, dtypes, name, after=None):
    n = len(shards)
    extra, extra_specs = _after(after)

    def body(me_ref, *refs):
        for a in range(n):
            out = refs[n + len(extra) + a]
            out[...] = refs[a][...].astype(out.dtype)

    in_specs, out_specs, out_shape = [], [], []
    for s, layer, axis, dt in zip(shards, layers, axes, dtypes):
        shape = s.shape if layer is None else s.shape[1:]
        nd = len(shape)
        if layer is None:
            in_specs.append(pl.BlockSpec(shape, lambda i, me, nd=nd: (0,) * nd))
        else:
            in_specs.append(pl.BlockSpec((None,) + shape, lambda i, me, nd=nd, layer=layer: (layer,) + (0,) * nd))
        out_specs.append(pl.BlockSpec(shape, lambda i, me, nd=nd, axis=axis: tuple(me[0] if d == axis else 0 for d in range(nd))))
        out_shape.append(_S(shape[:axis] + (N_DEV * shape[axis],) + shape[axis + 1:], dt))
    me = _lin(_me()).astype(jnp.int32).reshape(1)
    return pl.pallas_call(
        body, name=name, out_shape=out_shape,
        grid_spec=pltpu.PrefetchScalarGridSpec(num_scalar_prefetch=1, grid=(1,), in_specs=in_specs + extra_specs,
                                               out_specs=out_specs),
        compiler_params=_params(1))(me, *shards, *extra)


def _push_start(mode, srcs, lands, axes, name, route=None):
    n, ns = len(lands), len(srcs)

    def body(*refs):
        src_refs, land_refs = refs[:ns], refs[ns:ns + n]
        send_sems, recv_sems = refs[ns + n], refs[ns + n + 1]
        token = refs[-1]
        x, y, c = _me()
        for a in range(n):
            for block, peer in _routes(route, x, y, c):
                there = _dst_view(mode, land_refs[a], axes[a], _lin(block))
                pltpu.make_async_remote_copy(
                    src_ref=_src_view(mode, src_refs[a], axes[a], _lin(peer)) if ns else there, dst_ref=there,
                    send_sem=send_sems.at[a], recv_sem=recv_sems.at[a], device_id=peer, device_id_type=MESH).start()
        token[...] = jnp.zeros_like(token)

    hbm = lambda s: pltpu.HBM(s.shape, s.dtype)
    outs = pl.pallas_call(
        body, name=name,
        out_shape=(pltpu.SemaphoreType.DMA((n,)), pltpu.SemaphoreType.DMA((n,)), *[hbm(s) for s in srcs], *[hbm(s) for s in lands],
                   _S((SUB, LANE))),
        in_specs=[HBM_SPEC] * (ns + n),
        out_specs=(SEM_SPEC, SEM_SPEC, *[HBM_SPEC] * (ns + n), pl.BlockSpec(memory_space=pltpu.VMEM)),
        input_output_aliases={i: 2 + i for i in range(ns + n)},
        compiler_params=pltpu.CompilerParams(has_side_effects=EFFECT),
    )(*[pltpu.with_memory_space_constraint(s, pltpu.HBM) for s in list(srcs) + list(lands)])
    return outs[0], outs[1], outs[2:2 + ns], outs[2 + ns:2 + ns + n], outs[-1]


def _push_wait(mode, send_sems, recv_sems, srcs, lands, axes, after, name, first=0, route=None):
    n, ns = len(lands), len(srcs)
    after = list(after) if isinstance(after, (list, tuple)) else [after]

    def body(*refs):
        land_refs = refs[ns:ns + n]
        send_sems, recv_sems = refs[ns + n], refs[ns + n + 1]
        x, y, c = _me()
        for a in range(n):
            seven = _blocks(mode, land_refs[a], axes[a], ARRIVALS[route])
            cp = pltpu.make_async_remote_copy(src_ref=seven, dst_ref=seven, send_sem=send_sems.at[first + a],
                                              recv_sem=recv_sems.at[first + a],
                                              device_id=(x, y, 1 - c), device_id_type=MESH)
            cp.wait_send()
            cp.wait_recv()

    hbm = lambda s: pltpu.HBM(s.shape, s.dtype)
    outs = pl.pallas_call(
        body, name=name, out_shape=tuple(hbm(s) for s in list(srcs) + list(lands)),
        in_specs=[HBM_SPEC] * (ns + n) + [SEM_SPEC, SEM_SPEC] + [ANY] * len(after), out_specs=tuple([HBM_SPEC] * (ns + n)),
        input_output_aliases={i: i for i in range(ns + n)},
        compiler_params=pltpu.CompilerParams(has_side_effects=EFFECT),
    )(*srcs, *lands, send_sems, recv_sems, *after)
    return outs[ns:]


def _sum_parts(parts):
    n, R, C = parts.shape

    def body(p_ref, o_ref):
        g = p_ref[0]
        for k in range(1, n):
            g = g + p_ref[k]
        o_ref[...] = g

    return pl.pallas_call(body, name="sum_parts", out_shape=_S((R, C)))(parts)


SMALL =['conv_b', 'rg_wa', 'rg_ba', 'rg_wx', 'rg_bx', 'rg_lambda', 's5_a_re', 's5_a_im', 's5_b_re', 's5_b_im',
         's5_c_re', 's5_c_im', 's5_d', 's5_log_step', 's5_b_glu', 'ln1_g', 'ln1_b', 'ple_gate_b', 'ln2_g', 'ln2_b']
WEIGHTS = ['w_in', 'conv_w', 'conv_b', 'rg_wa', 'rg_ba', 'rg_wx', 'rg_bx', 'rg_lambda', 's5_a_re', 's5_a_im', 's5_b_re',
           's5_b_im', 's5_c_re', 's5_c_im', 's5_d', 's5_log_step', 's5_w_glu', 's5_b_glu', 'w_out', 'ln1_g', 'ln1_b',
           'ple_w', 'ple_gate_w', 'ple_gate_b', 'ln2_g', 'ln2_b']
PACK_ROWS_MULT = 64


def _pack(tree, scalar):
    flat = jnp.concatenate([tree[k].reshape(-1) for k in SMALL] + [scalar.reshape(1)])
    rows = -(-flat.shape[0] // (LANE * PACK_ROWS_MULT)) * PACK_ROWS_MULT
    return jnp.pad(flat, (0, rows * LANE - flat.shape[0])).reshape(rows, LANE)


def _unpack(packed, like):
    flat, out, o = packed.reshape(-1), {}, 0
    for k in SMALL:
        n = math.prod(like[k].shape)
        out[k] = flat[o:o + n].reshape(like[k].shape)
        o += n
    return out, flat[o]


class _NoHooks:
    token = None
    first_token = None

    def first_weights(self, full, after):
        return full

    def layer_start(self, i, W, after):
        return W

    def late_weights(self, i, W, after):
        return W

    def post_done(self, i, g):
        return None

    def smalls_done(self, grads, loss):
        self.small = _small_grads(grads, self.res)
        return None

    def w_in_done(self, i, g):
        return None

    def layer_done(self, i, g, dx):
        return None


def _local_grads(x, p, target, W, disc, hooks):
    depth = 2
    saved = []
    for i in range(depth):
        if i > 0:
            W = hooks.layer_start(i, W, x)
        w = W[i]
        z = _inproj_fwd(x, w['w_in'], hooks.token if i == 0 else None)
        hs, *gates = _rg_fwd(z, w['conv_w'], w['conv_b'], w['wa_bd'], w['wx_bd'], w['rg_ba'], w['rg_bx'], w['rg_lambda'], i)
        d = disc[i]
        y0, s_re, s_im = _s5_fwd(z, d['bb_re'], d['bb_im'], d['lb_re'], d['lb_im'], d['c_re'], d['c_im'], w['s5_d'], i)
        W = hooks.late_weights(i, W, y0)
        w = W[i]
        x2, *norms = _post_fwd(x, hs, z, y0, p, w['s5_w_glu'], w['s5_b_glu'], w['w_out'], w['ln1_g'], w['ln1_b'],
                               w['ple_w'], w['ple_gate_w'], w['ple_gate_b'], w['ln2_g'], w['ln2_b'], i)
        saved.append((x, z, hs, gates, y0, s_re, s_im, norms))
        x = x2

    grads = [None] * depth
    dx = target
    loss = None
    token = None
    for i in reversed(range(depth)):
        w, d = W[i], disc[i]
        xin, z, hs, gates, y0, s_re, s_im, (xh1, xh2, q, gt, rstd1, rstd2) = saved[i]
        g = {}
        (dt1, g['ple_w'], g['ple_gate_w'], g['ple_gate_b'], g['ln1_g'], g['ln1_b'], g['ln2_g'], g['ln2_b'], lrow) = _post_bwd_a(
            dx, i == depth - 1, xh2, xh1, rstd2, rstd1, q, gt, p, w['ple_gate_w'], w['ln1_g'], w['ln1_b'],
            w['ln2_g'], w['ln2_b'], i, token)
        if i == depth - 1:
            loss = 0.5 / D_MODEL * jnp.sum(lrow)
        dhs, dy0, dzg, g['w_out'], g['s5_w_glu'], g['s5_b_glu'] = _post_bwd_b(dt1, z, hs, y0, w['w_out'], w['s5_w_glu'],
                                                                           w['s5_b_glu'], i)
        (dzu, g['bb_re'], g['bb_im'], g['lb_re'], g['lb_im'], g['c_re'], g['c_im'], g['s5_d']) = _s5_bwd(
            dy0, z, s_re, s_im, d['bb_re'], d['bb_im'], d['lb_re'], d['lb_im'], d['c_re'], d['c_im'], w['s5_d'], i,
            hooks.post_done(i, g))
        (dzx, g['conv_w'], g['conv_b'], g['wa_bd'], g['wx_bd'], g['rg_ba'], g['rg_bx'], g['rg_lambda']) = _rg_bwd(
            dhs, z, hs, gates, w['conv_w'], w['wa_bd'], w['wx_bd'], w['rg_lambda'], i)
        if i == 0:
            g['w_in'] = _inproj_bwd_dw(xin, dzx, dzg, dzu, hooks.smalls_done([g, grads[1]], loss))
            dx = _inproj_bwd_dx(dt1, dzx, dzg, dzu, w['w_in'], hooks.w_in_done(i, g))
        else:
            dx, g['w_in'] = _inproj_bwd(dt1, xin, dzx, dzg, dzu, w['w_in'])
        grads[i] = g
        token = hooks.layer_done(i, g, dx)
    return loss, dx, grads


def _s5_layouts_fwd(s5_a_re, s5_a_im, s5_log_step, s5_b_re, s5_b_im, s5_c_re, s5_c_im, token=None):
    depth = s5_a_re.shape[0]
    ar, ai = s5_a_re.reshape(depth * 24, S5_P), s5_a_im.reshape(depth * 24, S5_P)
    ls = s5_log_step.reshape(depth * 24, 1)
    lr, li, cr, ci = _s5_disc_fwd(ar, ai, ls, token)
    per_group = lambda a: a.reshape(depth * 24, 1, S5_P)
    as_c = lambda b: jnp.swapaxes(b, 2, 3).reshape(depth * 24, S5_H, S5_P)
    res = (ar, ai, ls, per_group(cr), per_group(ci), as_c(s5_b_re), as_c(s5_b_im))
    bbr, bbi = _s5_bscale_fwd(*res[3:])
    tiles = lambda a: a.reshape(depth * N_S5_T, S5_GT, S5_H, S5_P)
    rows = lambda a: a.reshape(depth * N_S5_T, S5_GT, S5_P)
    disc = dict(bb_re=tiles(bbr), bb_im=tiles(bbi), lb_re=rows(lr), lb_im=rows(li), c_re=tiles(s5_c_re), c_im=tiles(s5_c_im))
    return [disc] * depth, res


def _s5_layouts_bwd(grads, res):
    ar, ai, ls, cr, ci, br, bi = res
    depth = len(grads)
    stack = lambda k, shape: jnp.stack([g[k] for g in grads]).reshape(shape)
    groups, shape_c = (depth * 24, S5_H, S5_P), (depth, 24, S5_H, S5_P)
    dbr, dbi, dcr, dci = _s5_bscale_bwd(cr, ci, br, bi, stack('bb_re', groups), stack('bb_im', groups))
    gp = (depth * 24, S5_P)
    dar, dai, dls = _s5_disc_bwd(ar, ai, ls, stack('lb_re', gp), stack('lb_im', gp), dcr.reshape(gp), dci.reshape(gp))
    return dict(
        s5_a_re=dar.reshape(depth, 24, S5_P), s5_a_im=dai.reshape(depth, 24, S5_P), s5_log_step=dls.reshape(depth, 24),
        s5_b_re=jnp.swapaxes(dbr.reshape(shape_c), 2, 3), s5_b_im=jnp.swapaxes(dbi.reshape(shape_c), 2, 3),
        s5_c_re=stack('c_re', shape_c), s5_c_im=stack('c_im', shape_c))


LATE = ('w_out', 'ple_w', 'ple_gate_w', 's5_w_glu')


ROWS = ('conv_b', 'rg_ba', 'rg_bx', 'rg_lambda', 's5_d', 's5_b_glu', 'ln1_g', 'ln1_b', 'ple_gate_b', 'ln2_g', 'ln2_b')


def _shared_weights(full):
    depth = full['conv_b'].shape[0]
    shared = {k: full[k].reshape(depth, 1, -1) for k in ROWS}
    shared.update(conv_w=full['conv_w'], wa_bd=full['rg_wa'], wx_bd=full['rg_wx'])
    return shared


def _layer_weights(full, shared, i):
    return dict(shared, w_in=full['w_in'][i])


class _AllLocal(_NoHooks):
    def __init__(self, full):
        self.full = full

    def late_weights(self, i, W, after):
        W[i].update({k: self.full[k][i] for k in LATE})
        return W


def _full_grads(full, x, p, target, hooks=None):
    hooks = hooks or _AllLocal(full)
    disc, res = _s5_layouts_fwd(full['s5_a_re'], full['s5_a_im'], full['s5_log_step'], full['s5_b_re'], full['s5_b_im'],
                                full['s5_c_re'], full['s5_c_im'], hooks.first_token)
    full = hooks.first_weights(full, disc[-1]['bb_im'])
    shared = _shared_weights(full)
    W = [_layer_weights(full, shared, i) for i in range(2)]
    hooks.res = res
    loss, gx, grads = _local_grads(x, p, target, W, disc, hooks)
    out = dict(hooks.small)
    for k in SHARD_AXIS:
        out[k] = [g[k] for g in grads]
    return loss, gx, out


def _small_grads(grads, res):
    stack = lambda f: jnp.stack([f(g) for g in grads])
    out = _s5_layouts_bwd(grads, res)
    out['conv_w'] = stack(lambda g: g['conv_w'])
    for k in ('conv_b', 'rg_ba', 'rg_bx', 'rg_lambda', 's5_b_glu', 'ln1_g', 'ln1_b', 'ple_gate_b', 'ln2_g', 'ln2_b'):
        out[k] = stack(lambda g: g[k][0])
    out['s5_d'] = stack(lambda g: g['s5_d'][0]).reshape(2, 24, 16)
    out['rg_wa'] = stack(lambda g: g['wa_bd'])
    out['rg_wx'] = stack(lambda g: g['wx_bd'])
    return out


SHARD_AXIS = {'w_in': 2, 'w_out': 1, 'ple_w': 2, 'ple_gate_w': 1, 's5_w_glu': 1}


def kernel(x, p, w_in, conv_w, conv_b, rg_wa, rg_ba, rg_wx, rg_bx, rg_lambda, s5_a_re, s5_a_im, s5_b_re, s5_b_im, s5_c_re, s5_c_im, s5_d, s5_log_step, s5_w_glu, s5_b_glu, w_out, ln1_g, ln1_b, ple_w, ple_gate_w, ple_gate_b, ln2_g, ln2_b, loss_target, m_w_in, m_conv_w, m_conv_b, m_rg_wa, m_rg_ba, m_rg_wx, m_rg_bx, m_rg_lambda, m_s5_a_re, m_s5_a_im, m_s5_b_re, m_s5_b_im, m_s5_c_re, m_s5_c_im, m_s5_d, m_s5_log_step, m_s5_w_glu, m_s5_b_glu, m_w_out, m_ln1_g, m_ln1_b, m_ple_w, m_ple_gate_w, m_ple_gate_b, m_ln2_g, m_ln2_b, v_w_in, v_conv_w, v_conv_b, v_rg_wa, v_rg_ba, v_rg_wx, v_rg_bx, v_rg_lambda, v_s5_a_re, v_s5_a_im, v_s5_b_re, v_s5_b_im, v_s5_c_re, v_s5_c_im, v_s5_d, v_s5_log_step, v_s5_w_glu, v_s5_b_glu, v_w_out, v_ln1_g, v_ln1_b, v_ple_w, v_ple_gate_w, v_ple_gate_b, v_ln2_g, v_ln2_b):
    local = dict(locals())
    w = {k: local[k] for k in WEIGHTS}
    mom = {k: local['m_' + k] for k in WEIGHTS}
    var = {k: local['v_' + k] for k in WEIGHTS}

    big = list(SHARD_AXIS)
    late_axes = [SHARD_AXIS[k] - 1 for k in LATE]
    pushed = {}

    groups = dict(first=(['w_in', 'conv_w'], [0, None], [1, 0]), l0=(list(LATE), [0] * len(LATE), late_axes),
                  l1=(['w_in'] + list(LATE), [1] * (1 + len(LATE)), [1] + late_axes))
    token = None
    for key, members in (("first", ["first"]), ("rest", ["l0", "l1"])):
        names, layers, axes = (sum((groups[m][j] for m in members), []) for j in range(3))
        shards = [w[k] if layer is not None else w[k][None] for k, layer in zip(names, layers)]
        lands = _place_shards(shards, layers, axes, [WIRE if k in big else w[k].dtype for k in names],
                              "place_weights_" + key, token)
        pushed[key] = _push_start("gather", [], lands, axes, "push_weights_" + key, "near" if key == "first" else None)
        token = pushed[key][4]

    def await_weights(key, axes, after):
        s, first = (pushed["first"], 0) if key == "first" else (pushed["rest"], 0 if key == "l0" else len(LATE))
        return _push_wait("gather", s[0], s[1], [], s[3][first:first + len(axes)], axes, after, "await_weights_" + key, first)

    def push_grads(key, g, names, axes):
        srcs = [g[k] for k in names]
        pushed[key] = _push_start("scatter", srcs, _place_own("scatter", srcs, axes, "place_grads_" + key), axes,
                                  "push_grads_" + key)
        return pushed[key][4]

    def await_grads(key, axes, after):
        s = pushed[key]
        return _push_wait("scatter", s[0], s[1], s[2], s[3], axes, after, "await_grads_" + key)

    class Overlap(_NoHooks):
        token = pushed["rest"][4]
        first_token = token

        def first_weights(self, full, after):
            s, axes = pushed["first"], [1, 0]
            near = _push_wait("gather", s[0], s[1], [], s[3], axes, after, "await_weights_near", route="near")
            s = _push_start("gather", [], near, axes, "relay_weights", "relay")
            w_in0, conv = _push_wait("gather", s[0], s[1], [], s[3], axes, s[4], "await_weights_relay", route="relay")
            return dict(full, w_in=[w_in0, None], conv_w=jnp.moveaxis(conv, 0, 2).reshape(2, 4, RG_W))

        def late_weights(self, i, W, after):
            if i == 0:
                W[0].update(zip(LATE, await_weights("l0", late_axes, after)))
            return W

        def layer_start(self, i, W, after):
            lands = await_weights("l1", [1] + late_axes, after)
            W[1].update(zip(LATE, lands[1:]), w_in=lands[0])
            return W

        def post_done(self, i, g):
            return push_grads("late0", g, LATE, late_axes) if i == 0 else None

        def smalls_done(self, grads, loss):
            super().smalls_done(grads, loss)
            conv = jnp.moveaxis(self.small['conv_w'].reshape(2, 4, N_DEV, RG_W // N_DEV), 2, 0)
            self.packed = _pack(self.small, loss)
            return push_grads("small", dict(conv_w=conv.reshape(N_DEV, 8, RG_W // N_DEV), small=self.packed),
                              ['conv_w', 'small'], [0, 0])

        def w_in_done(self, i, g):
            return push_grads("w_in0", g, ['w_in'], [0])

        def layer_done(self, i, g, dx):
            return push_grads("all1", g, ['w_in'] + list(LATE), [0] + late_axes) if i == 1 else None

    hooks = Overlap()
    _, grad_x, g = _full_grads(dict(w), x[0], p, loss_target[0], hooks)

    recv1 = dict(zip(['w_in'] + list(LATE), await_grads("all1", [0] + late_axes, grad_x)))
    recv0 = dict(zip(LATE, await_grads("late0", late_axes, grad_x)))
    outs = {}

    def update(k, parts):
        shard = w[k].shape
        c = shard[-1]
        two = lambda a: a.reshape(-1, c)
        res = _adamw([r.reshape(N_DEV, -1, c) for r in parts], two(w[k]), two(mom[k]), two(var[k]))
        outs[k] = [o.reshape(shard) for o in res]

    for k in LATE:
        update(k, [recv0[k], recv1[k]])
    done = [outs[k][1] for k in LATE]
    conv_parts, small_parts = await_grads("small", [0, 0], done)

    rows = hooks.packed.shape[0] // N_DEV
    mine = _sum_parts(small_parts.reshape(N_DEV, rows, LANE))
    sums = _push_start("gather", [mine], _place_own("gather", [mine], [0], "place_small_sums"), [0], "push_small_sums")
    w_in0, = await_grads("w_in0", [0], sums[4])
    update('w_in', [w_in0, recv1['w_in']])
    update('conv_w', [conv_parts])
    gathered, = _push_wait("gather", sums[0], sums[1], sums[2], sums[3], [0], [outs['w_in'][1], outs['conv_w'][1]],
                           "await_small_sums")
    summed, loss = _unpack(gathered, w)
    narrow = ['s5_b_re', 's5_b_im']
    for names, name in ((narrow, "adamw_s5_b"), ([k for k in SMALL if k not in narrow], "adamw_small")):
        delta, new_m, new_v = _adamw_natural(names, summed, w, mom, var, name)
        for k in names:
            outs[k] = [summed[k], delta[k], new_m[k], new_v[k]]

    res = [loss, grad_x[None]]
    for j in range(4):
        res += [outs[k][j] for k in WEIGHTS]
    return tuple(res)
```

```python
import math

import jax
import jax.numpy as jnp
from jax import lax
from jax.experimental import pallas as pl
from jax.experimental.pallas import tpu as pltpu

F32 = jnp.float32
MXU = jnp.bfloat16
WIRE = jnp.bfloat16

N_DEV = 8
D_MODEL = 1024
PLE_D = 256
RG_W = 640
S5_W = 384
S5_P = 64
S5_N = 24 * S5_P
Z_W = 2 * RG_W + 2 * S5_W
C_RGG = RG_W
C_S5U = 2 * RG_W
C_S5G = 2 * RG_W + S5_W
LANE = 128
N_RG_T = RG_W // LANE
N_S5_T = S5_W // LANE
W_BLK = Z_W // N_DEV
ALPHA = (2.0 * 2) ** 0.25
LN_EPS = 1e-5
RG_C = 8.0
LR, B1, B2, EPS, WD, STEP = 0.001, 0.9, 0.999, 1e-08, 0.01, 10
BC1 = 1.0 - B1 ** STEP
BC2 = 1.0 - B2 ** STEP
RC = 512
RC_RG = 1024
TM = 256
TM_MM = 1024
VMEM_LIMIT = 56 * 1024 * 1024

MESH = pl.DeviceIdType.MESH
ANY = pl.BlockSpec(memory_space=pl.ANY)


def _params(n_grid_axes, vmem=VMEM_LIMIT):
    return pltpu.CompilerParams(dimension_semantics=("arbitrary",) * n_grid_axes, vmem_limit_bytes=vmem)


def _S(shape, dtype=F32):
    return jax.ShapeDtypeStruct(tuple(shape), dtype)


def _sigmoid(x):
    return 0.5 * jnp.tanh(0.5 * x) + 0.5


def _silu_and_grad(x):
    s = _sigmoid(x)
    return x * s, s * (1.0 + x * (1.0 - s))


_GELU_C = math.sqrt(2.0 / math.pi)


def _gelu(x):
    return 0.5 * x * (1.0 + jnp.tanh(_GELU_C * (x + 0.044715 * (x * x * x))))


def _gelu_grad(x):
    th = jnp.tanh(_GELU_C * (x + 0.044715 * (x * x * x)))
    return 0.5 * (1.0 + th) + 0.5 * x * (1.0 - th * th) * (_GELU_C * (1.0 + 3.0 * 0.044715 * (x * x)))


def _mm(a, b):
    return jnp.dot(a.astype(MXU), b.astype(MXU), preferred_element_type=F32)


def _mm_nt(a, b):
    return lax.dot_general(a.astype(MXU), b.astype(MXU), (((1,), (1,)), ((), ())), preferred_element_type=F32)


def _mm_tn(a, b):
    return lax.dot_general(a.astype(MXU), b.astype(MXU), (((0,), (0,)), ((), ())), preferred_element_type=F32)


def _ln_fwd(t, g, b):
    mu = jnp.mean(t, axis=-1, keepdims=True)
    tc = t - mu
    var = jnp.mean(tc * tc, axis=-1, keepdims=True)
    rstd = lax.rsqrt(var + LN_EPS)
    xhat = tc * rstd
    return xhat * g + b, xhat, rstd


def _ln_bwd(dy, xhat, rstd, g):
    dxh = dy * g
    m1 = jnp.mean(dxh, axis=-1, keepdims=True)
    m2 = jnp.mean(dxh * xhat, axis=-1, keepdims=True)
    return rstd * (dxh - m1 - xhat * m2)


def _colsum(a):
    return jnp.sum(a, axis=0, keepdims=True)


def _up(x, d, rows, fill):
    n = x.shape[0]
    return jnp.where(rows < n - d, pltpu.roll(x, n - d, 0), fill)


SUB = 8
TILE_STEPS = (1, 2, 4)


def _r8(width):
    return lax.broadcasted_iota(jnp.int32, (SUB, width), 0)


def _scan_real(a, u, carry, reverse=False):
    r8 = _r8(a.shape[1])
    n = a.shape[0] // SUB
    outs = [None] * n
    for k in (reversed(range(n)) if reverse else range(n)):
        A, U = a[SUB * k:SUB * k + SUB], u[SUB * k:SUB * k + SUB]
        for d in TILE_STEPS:
            m = (r8 < SUB - d) if reverse else (r8 >= d)
            sh = SUB - d if reverse else d
            U = A * jnp.where(m, pltpu.roll(U, sh, 0), 0.0) + U
            A = A * jnp.where(m, pltpu.roll(A, sh, 0), 1.0)
        h = A * carry + U
        outs[k] = h
        carry = h[0:1] if reverse else h[SUB - 1:SUB]
    return jnp.concatenate(outs, axis=0), carry


def _tile_powers(lr, li, reverse=False):
    width = lr.shape[1]
    r8 = _r8(width)
    steps = []
    pr, pi = lr, li
    er, ei = jnp.broadcast_to(lr, (SUB, width)), jnp.broadcast_to(li, (SUB, width))
    for d in TILE_STEPS:
        m = (r8 < SUB - d) if reverse else (r8 >= d)
        sh = SUB - d if reverse else d
        steps.append((sh, jnp.where(m, pr, 0.0), jnp.where(m, pi, 0.0)))
        er, ei = _cmul(er, ei, jnp.where(m, pltpu.roll(er, sh, 0), 1.0), jnp.where(m, pltpu.roll(ei, sh, 0), 0.0))
        pr, pi = _cmul(pr, pi, pr, pi)
    return steps, (er, ei)


def _scan_lti(xr, xi, carry, steps, e, reverse=False):
    er, ei = e
    kr, ki = carry
    n = xr.shape[0] // SUB
    outr, outi = [None] * n, [None] * n
    for k in (reversed(range(n)) if reverse else range(n)):
        sr, si = xr[SUB * k:SUB * k + SUB], xi[SUB * k:SUB * k + SUB]
        for sh, pr, pi in steps:
            shr, shi = pltpu.roll(sr, sh, 0), pltpu.roll(si, sh, 0)
            sr, si = sr + (pr * shr - pi * shi), si + (pr * shi + pi * shr)
        sr = sr + (er * kr - ei * ki)
        si = si + (er * ki + ei * kr)
        outr[k], outi[k] = sr, si
        kr, ki = (sr[0:1], si[0:1]) if reverse else (sr[SUB - 1:SUB], si[SUB - 1:SUB])
    return jnp.concatenate(outr, axis=0), jnp.concatenate(outi, axis=0), (kr, ki)


def _halo(ref, c, r0):
    rp = pl.multiple_of(jnp.maximum(r0 - 8, 0), 8)
    return jnp.where(c > 0, ref[pl.ds(rp, 8), :], 0.0)


def _conv_taps(xe):
    return [pltpu.roll(xe, 3, 0)[8:, :], pltpu.roll(xe, 2, 0)[8:, :], pltpu.roll(xe, 1, 0)[8:, :], xe[8:, :]]


def _rg_gates(h, wa, wx, ba, bx, sp):
    r = _sigmoid(_mm(h, wa) + ba)
    i = _sigmoid(_mm(h, wx) + bx)
    log_a = (-RG_C) * r * sp
    a = jnp.exp(log_a)
    mult = jnp.sqrt(-jnp.tanh(log_a) * (a * a + 1.0))
    return r, i, a, mult


def _softplus(y):
    return jnp.maximum(y, 0.0) + jnp.log1p(jnp.exp(-jnp.abs(y)))


def _after(token):
    return ([], []) if token is None else ([token], [ANY])


def _inproj_fwd(x, w_in, token=None):
    L = x.shape[0]

    def body(x_ref, w_ref, *rest):
        rest[-1][...] = _mm(x_ref[...], w_ref[...])

    extra, extra_specs = _after(token)
    tm = min(TM_MM, L)
    return pl.pallas_call(
        body, name="inproj_fwd", grid=(L // tm,),
        in_specs=[pl.BlockSpec((tm, D_MODEL), lambda i: (i, 0)), pl.BlockSpec((D_MODEL, Z_W), lambda i: (0, 0))] + extra_specs,
        out_specs=pl.BlockSpec((tm, Z_W), lambda i: (i, 0)),
        out_shape=_S((L, Z_W)), compiler_params=_params(1))(x, w_in, *extra)


def _inproj_bwd(dt1, x, dzx, dzg, dzu, w_in):
    L = x.shape[0]

    def body(dt1_ref, x_ref, dzx_ref, dzg_ref, dzu_ref, w_ref, dx_ref, dw_ref, acc_ref):
        @pl.when(pl.program_id(0) == 0)
        def _():
            acc_ref[...] = jnp.zeros_like(acc_ref)
        dzg = dzg_ref[...]
        dz = jnp.concatenate([dzx_ref[...], dzg[:, :RG_W], dzu_ref[...], dzg[:, RG_W:]], axis=1).astype(MXU)
        xb = x_ref[...].astype(MXU)
        dx_ref[...] = ALPHA * dt1_ref[...] + _mm_nt(dz, w_ref[...])
        for j in range(N_DEV):
            acc_ref[j] += _mm_tn(xb, dz[:, j * W_BLK:(j + 1) * W_BLK])

        @pl.when(pl.program_id(0) == L // TM - 1)
        def _():
            dw_ref[...] = acc_ref[...].astype(WIRE)

    row = lambda w: pl.BlockSpec((TM, w), lambda i: (i, 0))
    wspec = pl.BlockSpec((N_DEV, D_MODEL, W_BLK), lambda i: (0, 0, 0))
    return pl.pallas_call(
        body, name="inproj_bwd", grid=(L // TM,),
        in_specs=[row(D_MODEL), row(D_MODEL), row(RG_W), row(D_MODEL), row(S5_W),
                  pl.BlockSpec((D_MODEL, Z_W), lambda i: (0, 0))],
        out_specs=[row(D_MODEL), wspec],
        out_shape=[_S((L, D_MODEL)), _S((N_DEV, D_MODEL, W_BLK), WIRE)],
        scratch_shapes=[pltpu.VMEM((N_DEV, D_MODEL, W_BLK), F32)],
        compiler_params=_params(1))(dt1, x, dzx, dzg, dzu, w_in)


TM2 = 512


def _dz_block(dzx_ref, dzg_ref, dzu_ref):
    dzg = dzg_ref[...]
    return jnp.concatenate([dzx_ref[...], dzg[:, :RG_W], dzu_ref[...], dzg[:, RG_W:]], axis=1).astype(MXU)


def _inproj_bwd_dw(x, dzx, dzg, dzu, token=None):
    L = x.shape[0]
    extra, extra_specs = _after(token)

    def body(x_ref, dzx_ref, dzg_ref, dzu_ref, *rest):
        dw_ref, acc_ref = rest[len(extra):]
        @pl.when(pl.program_id(0) == 0)
        def _():
            acc_ref[...] = jnp.zeros_like(acc_ref)
        dz = _dz_block(dzx_ref, dzg_ref, dzu_ref)
        xb = x_ref[...].astype(MXU)
        for j in range(N_DEV):
            acc_ref[j] += _mm_tn(xb, dz[:, j * W_BLK:(j + 1) * W_BLK])

        @pl.when(pl.program_id(0) == L // TM2 - 1)
        def _():
            dw_ref[...] = acc_ref[...].astype(WIRE)

    row = lambda w: pl.BlockSpec((TM2, w), lambda i: (i, 0))
    wspec = pl.BlockSpec((N_DEV, D_MODEL, W_BLK), lambda i: (0, 0, 0))
    return pl.pallas_call(
        body, name="inproj_bwd_dw", grid=(L // TM2,),
        in_specs=[row(D_MODEL), row(RG_W), row(D_MODEL), row(S5_W)] + extra_specs, out_specs=wspec,
        out_shape=_S((N_DEV, D_MODEL, W_BLK), WIRE), scratch_shapes=[pltpu.VMEM((N_DEV, D_MODEL, W_BLK), F32)],
        compiler_params=_params(1))(x, dzx, dzg, dzu, *extra)


def _inproj_bwd_dx(dt1, dzx, dzg, dzu, w_in, token=None):
    L = dt1.shape[0]
    extra, extra_specs = _after(token)

    def body(dt1_ref, dzx_ref, dzg_ref, dzu_ref, w_ref, *rest):
        rest[-1][...] = ALPHA * dt1_ref[...] + _mm_nt(_dz_block(dzx_ref, dzg_ref, dzu_ref), w_ref[...])

    tm = min(TM_MM, L)
    row = lambda w: pl.BlockSpec((tm, w), lambda i: (i, 0))
    return pl.pallas_call(
        body, name="inproj_bwd_dx", grid=(L // tm,),
        in_specs=[row(D_MODEL), row(RG_W), row(D_MODEL), row(S5_W), _full((D_MODEL, Z_W))] + extra_specs,
        out_specs=row(D_MODEL), out_shape=_S((L, D_MODEL)), compiler_params=_params(1))(dt1, dzx, dzg, dzu, w_in, *extra)


def _rg_specs(layer):
    tile = lambda rows: pl.BlockSpec((rows, LANE), lambda c: (0, c))
    ptile = lambda rows: pl.BlockSpec((None, rows, LANE), lambda c: (layer, 0, c))
    pheads = pl.BlockSpec((None, 2, RG_HD, RG_HD), lambda c: (layer, c, 0, 0))
    return tile, ptile, pheads, pl.BlockSpec((2, RG_HD, RG_HD), lambda c: (c, 0, 0))


RG_HD = 64


def _bd2(w):
    z = jnp.zeros((RG_HD, RG_HD), w.dtype)
    return jnp.concatenate([jnp.concatenate([w[0], z], axis=1), jnp.concatenate([z, w[1]], axis=1)], axis=0)


def _bd2_diag(m):
    return jnp.stack([m[:RG_HD, :RG_HD], m[RG_HD:, RG_HD:]])


def _rg_fwd(z, cw, cb, wa_bd, wx_bd, ba, bx, lam, layer):
    L = z.shape[0]
    RC = min(RC_RG, L)

    def body(x_ref, cw_ref, cb_ref, wa_ref, wx_ref, ba_ref, bx_ref, lam_ref, hs_ref, *saved):
        w, b = cw_ref[...], cb_ref[...]
        wa, wx, ba_, bx_ = _bd2(wa_ref[...]).astype(MXU), _bd2(wx_ref[...]).astype(MXU), ba_ref[...], bx_ref[...]
        sp = _softplus(-lam_ref[...])

        def step(c, carry):
            r0 = pl.multiple_of(c * RC, RC)
            xe = jnp.concatenate([_halo(x_ref, c, r0), x_ref[pl.ds(r0, RC), :]], axis=0)
            t = _conv_taps(xe)
            h = t[0] * w[0:1] + t[1] * w[1:2] + t[2] * w[2:3] + t[3] * w[3:4] + b
            r, i, a, mult = _rg_gates(h, wa, wx, ba_, bx_, sp)
            hs, carry = _scan_real(a, mult * (i * h), carry)
            hs_ref[pl.ds(r0, RC), :] = hs
            for ref, val in zip(saved, (h, r, i, a, mult)):
                ref[pl.ds(r0, RC), :] = val
            return carry

        lax.fori_loop(0, L // RC, step, jnp.zeros((1, LANE), F32))

    tile, ptile, pheads, _ = _rg_specs(layer)
    return pl.pallas_call(
        body, name="rg_fwd", grid=(N_RG_T,),
        in_specs=[tile(L), ptile(4), ptile(1), pheads, pheads, ptile(1), ptile(1), ptile(1)],
        out_specs=[tile(L)] * 6, out_shape=[_S((L, RG_W))] * 6, compiler_params=_params(1))(
            z, cw, cb, wa_bd, wx_bd, ba, bx, lam)


def _rg_bwd(dhs, z, hs, gates, cw, wa_bd, wx_bd, lam, layer):
    L = z.shape[0]
    RC = min(RC_RG, L)

    def body(g_ref, x_ref, hs_ref, h_ref, r_ref, i_ref, a_ref, mult_ref, cw_ref, wa_ref, wx_ref, lam_ref,
             dx_ref, dcw_ref, dcb_ref, dwa_out, dwx_out, dba_ref, dbx_ref, dlam_ref, dwa_ref, dwx_ref):
        w = cw_ref[...]
        wa, wx = _bd2(wa_ref[...]).astype(MXU), _bd2(wx_ref[...]).astype(MXU)
        lam = lam_ref[...]
        sp = _softplus(-lam)
        rows = lax.broadcasted_iota(jnp.int32, (RC, LANE), 0)
        for ref in (dcw_ref, dcb_ref, dwa_ref, dwx_ref, dba_ref, dbx_ref, dlam_ref):
            ref[...] = jnp.zeros_like(ref)
        nch = L // RC

        def step(k, carry):
            cin, nxt = carry
            c = nch - 1 - k
            r0 = pl.multiple_of(c * RC, RC)
            xe = jnp.concatenate([_halo(x_ref, c, r0), x_ref[pl.ds(r0, RC), :]], axis=0)
            t = _conv_taps(xe)
            h, r, i, a, mult = (ref[pl.ds(r0, RC), :] for ref in (h_ref, r_ref, i_ref, a_ref, mult_ref))
            hs_e = jnp.concatenate([_halo(hs_ref, c, r0), hs_ref[pl.ds(r0, RC), :]], axis=0)
            hs_prev = pltpu.roll(hs_e, 1, 0)[8:, :]
            g = g_ref[pl.ds(r0, RC), :]
            cc, cin_new = _scan_real(a, a * g, cin, reverse=True)
            dh = g + _up(cc, 1, rows, cin)
            ih = i * h
            dlog_a = dh * hs_prev * a - (dh * ih) * (a * a) / mult
            di = dh * mult * h
            dhin = dh * mult * i
            dr = dlog_a * ((-RG_C) * sp)
            dlam_ref[...] += _colsum(dlog_a * r)
            dra = dr * r * (1.0 - r)
            dia = di * i * (1.0 - i)
            dwa_ref[...] += _mm_tn(h, dra)
            dwx_ref[...] += _mm_tn(h, dia)
            dba_ref[...] += _colsum(dra)
            dbx_ref[...] += _colsum(dia)
            dhin = dhin + _mm_nt(dra, wa) + _mm_nt(dia, wx)
            de = jnp.concatenate([dhin, nxt], axis=0)
            n = RC + 8
            dx = (dhin * w[3:4] + pltpu.roll(de, n - 1, 0)[:RC, :] * w[2:3]
                  + pltpu.roll(de, n - 2, 0)[:RC, :] * w[1:2] + pltpu.roll(de, n - 3, 0)[:RC, :] * w[0:1])
            dx_ref[pl.ds(r0, RC), :] = dx
            for kk in range(4):
                dcw_ref[kk:kk + 1, :] += _colsum(dhin * t[kk])
            dcb_ref[...] += _colsum(dhin)
            return cin_new, dhin[0:8, :]

        lax.fori_loop(0, nch, step, (jnp.zeros((1, LANE), F32), jnp.zeros((8, LANE), F32)))
        dlam_ref[...] = dlam_ref[...] * (RG_C * _sigmoid(-lam))
        dwa_out[...], dwx_out[...] = _bd2_diag(dwa_ref[...]), _bd2_diag(dwx_ref[...])

    tile, ptile, pheads, gheads = _rg_specs(layer)
    heads = _S((2 * N_RG_T, RG_HD, RG_HD))
    return pl.pallas_call(
        body, name="rg_bwd", grid=(N_RG_T,),
        in_specs=[tile(L)] * 8 + [ptile(4), pheads, pheads, ptile(1)],
        out_specs=[tile(L), tile(4), tile(1), gheads, gheads, tile(1), tile(1), tile(1)],
        out_shape=[_S((L, RG_W)), _S((4, RG_W)), _S((1, RG_W)), heads, heads, _S((1, RG_W)), _S((1, RG_W)), _S((1, RG_W))],
        scratch_shapes=[pltpu.VMEM((LANE, LANE), F32), pltpu.VMEM((LANE, LANE), F32)],
        compiler_params=_params(1))(dhs, z, hs, *gates, cw, wa_bd, wx_bd, lam)


def _cmul(ar, ai, br, bi):
    return ar * br - ai * bi, ar * bi + ai * br


S5_TW = S5_N // N_S5_T


S5_H = 16
S5_GT = LANE // S5_H


def _s5_specs(L, layer):
    in_tile = pl.BlockSpec((L, LANE), lambda t: (0, t))
    st = pl.BlockSpec((L, S5_TW), lambda t: (0, t))
    pg = pl.BlockSpec((None, S5_GT, S5_H, S5_P), lambda t: (layer * N_S5_T + t, 0, 0, 0))
    plb = pl.BlockSpec((None, S5_GT, S5_P), lambda t: (layer * N_S5_T + t, 0, 0))
    gg = pl.BlockSpec((None, S5_GT, S5_H, S5_P), lambda t: (t, 0, 0, 0))
    glb = pl.BlockSpec((None, S5_GT, S5_P), lambda t: (t, 0, 0))
    dv = pl.BlockSpec((1, LANE), lambda t: (0, t))
    return in_tile, st, pg, plb, gg, glb, dv


def _bd8(blocks):
    rows = []
    for g in range(S5_GT):
        pieces = [blocks[g]]
        if g:
            pieces.insert(0, jnp.zeros((S5_H, S5_P * g), blocks.dtype))
        if g < S5_GT - 1:
            pieces.append(jnp.zeros((S5_H, S5_P * (S5_GT - 1 - g)), blocks.dtype))
        rows.append(jnp.concatenate(pieces, axis=1))
    return jnp.concatenate(rows, axis=0)


def _bd8_diag(m):
    return jnp.stack([m[S5_H * g:S5_H * (g + 1), S5_P * g:S5_P * (g + 1)] for g in range(S5_GT)])


def _row8(v):
    return jnp.concatenate([v[g:g + 1] for g in range(S5_GT)], axis=1)


def _row8_split(r):
    return jnp.concatenate([r[:, S5_P * g:S5_P * (g + 1)] for g in range(S5_GT)], axis=0)


def _layer_row_tile(layer):
    return pl.BlockSpec((None, 1, LANE), lambda t: (layer, 0, t))


def _s5_fwd(z, bb_re, bb_im, lb_re, lb_im, c_re, c_im, dvec, layer):
    L = z.shape[0]

    def body(u_ref, bbr_ref, bbi_ref, lr_ref, li_ref, cr_ref, ci_ref, d_ref, y_ref, sr_ref, si_ref):
        bbr, bbi = _bd8(bbr_ref[...]).astype(MXU), _bd8(bbi_ref[...]).astype(MXU)
        cr, ci = _bd8(cr_ref[...]).astype(MXU), _bd8(ci_ref[...]).astype(MXU)
        dv = d_ref[...]
        steps, e = _tile_powers(_row8(lr_ref[...]), _row8(li_ref[...]))

        def step(c, carry):
            r0 = pl.multiple_of(c * RC, RC)
            u = u_ref[pl.ds(r0, RC), :]
            ub = u.astype(MXU)
            sr = jnp.dot(ub, bbr, preferred_element_type=F32)
            si = jnp.dot(ub, bbi, preferred_element_type=F32)
            sr, si, carry = _scan_lti(sr, si, carry, steps, e)
            sr_ref[pl.ds(r0, RC), :] = sr
            si_ref[pl.ds(r0, RC), :] = si
            y_ref[pl.ds(r0, RC), :] = dv * u + (_mm_nt(sr, cr) - _mm_nt(si, ci))
            return carry

        zero = jnp.zeros((1, S5_TW), F32)
        lax.fori_loop(0, L // RC, step, (zero, zero))

    in_tile, st, pg, plb, _, _, _ = _s5_specs(L, layer)
    u_tile = pl.BlockSpec((L, LANE), lambda t: (0, C_S5U // LANE + t))
    return pl.pallas_call(
        body, name="s5_fwd", grid=(N_S5_T,),
        in_specs=[u_tile, pg, pg, plb, plb, pg, pg, _layer_row_tile(layer)],
        out_specs=[in_tile, st, st],
        out_shape=[_S((L, S5_W)), _S((L, S5_N)), _S((L, S5_N))],
        compiler_params=_params(1))(z, bb_re, bb_im, lb_re, lb_im, c_re, c_im, dvec)


def _s5_bwd(dy0, z, s_re, s_im, bb_re, bb_im, lb_re, lb_im, c_re, c_im, dvec, layer, token=None):
    L = z.shape[0]
    extra, extra_specs = _after(token)

    def body(dy_ref, u_ref, sr_ref, si_ref, bbr_ref, bbi_ref, lr_ref, li_ref, cr_ref, ci_ref, d_ref, *rest):
        (du_ref, dbbr_out, dbbi_out, dlr_out, dli_out, dcr_out, dci_out, dd_ref,
         dbbr_ref, dbbi_ref, dcr_ref, dci_ref, dlr_ref, dli_ref) = rest[len(extra):]
        bbr, bbi = _bd8(bbr_ref[...]).astype(MXU), _bd8(bbi_ref[...]).astype(MXU)
        cr, ci = _bd8(cr_ref[...]).astype(MXU), _bd8(ci_ref[...]).astype(MXU)
        lr, li = _row8(lr_ref[...]), -_row8(li_ref[...])
        dv = d_ref[...]
        steps, e = _tile_powers(lr, li, reverse=True)
        for ref in (dbbr_ref, dbbi_ref, dlr_ref, dli_ref, dcr_ref, dci_ref, dd_ref):
            ref[...] = jnp.zeros_like(ref)
        nch = L // RC

        def step(k, carry):
            c = nch - 1 - k
            r0 = pl.multiple_of(c * RC, RC)
            dy = dy_ref[pl.ds(r0, RC), :]
            u = u_ref[pl.ds(r0, RC), :]
            dyb, ub = dy.astype(MXU), u.astype(MXU)
            sr, si = sr_ref[pl.ds(r0, RC), :], si_ref[pl.ds(r0, RC), :]
            dcr_ref[...] += _mm_tn(dyb, sr)
            dci_ref[...] -= _mm_tn(dyb, si)
            gr = jnp.dot(dyb, cr, preferred_element_type=F32)
            gi = -jnp.dot(dyb, ci, preferred_element_type=F32)
            gr, gi, carry = _scan_lti(gr, gi, carry, steps, e, reverse=True)
            pr_ = pltpu.roll(jnp.concatenate([_halo(sr_ref, c, r0), sr], axis=0), 1, 0)[8:, :]
            pi_ = pltpu.roll(jnp.concatenate([_halo(si_ref, c, r0), si], axis=0), 1, 0)[8:, :]
            dlr_ref[...] += _colsum(pr_ * gr + pi_ * gi)
            dli_ref[...] += _colsum(pr_ * gi - pi_ * gr)
            grb, gib = gr.astype(MXU), gi.astype(MXU)
            dbbr_ref[...] += _mm_tn(ub, grb)
            dbbi_ref[...] += _mm_tn(ub, gib)
            du_ref[pl.ds(r0, RC), :] = dv * dy + (_mm_nt(grb, bbr) + _mm_nt(gib, bbi))
            dd_ref[...] += _colsum(dy * u)
            return carry

        zero = jnp.zeros((1, S5_TW), F32)
        lax.fori_loop(0, nch, step, (zero, zero))
        dbbr_out[...], dbbi_out[...] = _bd8_diag(dbbr_ref[...]), _bd8_diag(dbbi_ref[...])
        dcr_out[...], dci_out[...] = _bd8_diag(dcr_ref[...]), _bd8_diag(dci_ref[...])
        dlr_out[...], dli_out[...] = _row8_split(dlr_ref[...]), _row8_split(dli_ref[...])

    in_tile, st, pg, plb, gg, glb, dv = _s5_specs(L, layer)
    u_tile = pl.BlockSpec((L, LANE), lambda t: (0, C_S5U // LANE + t))
    groups, rows = _S((N_S5_T, S5_GT, S5_H, S5_P)), _S((N_S5_T, S5_GT, S5_P))
    wide = pltpu.VMEM((LANE, S5_TW), F32)
    return pl.pallas_call(
        body, name="s5_bwd", grid=(N_S5_T,),
        in_specs=[in_tile, u_tile, st, st, pg, pg, plb, plb, pg, pg, _layer_row_tile(layer)] + extra_specs,
        out_specs=[in_tile, gg, gg, glb, glb, gg, gg, dv],
        out_shape=[_S((L, S5_W)), groups, groups, rows, rows, groups, groups, _S((1, S5_W))],
        scratch_shapes=[wide, wide, wide, wide, pltpu.VMEM((1, S5_TW), F32), pltpu.VMEM((1, S5_TW), F32)],
        compiler_params=_params(1))(dy0, z, s_re, s_im, bb_re, bb_im, lb_re, lb_im, c_re, c_im, dvec, *extra)


def _disc(ar, ai, ls):
    dt = jnp.exp(ls)
    mag = jnp.exp(ar * dt)
    lr = mag * jnp.cos(ai * dt)
    li = mag * jnp.sin(ai * dt)
    den = ar * ar + ai * ai
    cr = ((lr - 1.0) * ar + li * ai) / den
    ci = (li * ar - (lr - 1.0) * ai) / den
    return lr, li, cr, ci


def _s5_disc_fwd(ar, ai, ls, token=None):
    extra, extra_specs = _after(token)

    def body(ar_ref, ai_ref, ls_ref, *rest):
        lr_ref, li_ref, cr_ref, ci_ref = rest[len(extra):]
        lr, li, cr, ci = _disc(ar_ref[...], ai_ref[...], ls_ref[...])
        lr_ref[...], li_ref[...], cr_ref[...], ci_ref[...] = lr, li, cr, ci

    sh = _S(ar.shape)
    vm = pl.BlockSpec(memory_space=pltpu.VMEM)
    return pl.pallas_call(body, name="s5_disc_fwd", in_specs=[vm, vm, vm] + extra_specs, out_shape=[sh, sh, sh, sh])(
        ar, ai, ls, *extra)


def _s5_disc_bwd(ar, ai, ls, dlr, dli, dcr, dci):
    def body(ar_ref, ai_ref, ls_ref, dlr_ref, dli_ref, dcr_ref, dci_ref, dar_ref, dai_ref, dls_ref):
        _, vjp = jax.vjp(_disc, ar_ref[...], ai_ref[...], jnp.broadcast_to(ls_ref[...], ar_ref.shape))
        dar, dai, dls = vjp((dlr_ref[...], dli_ref[...], dcr_ref[...], dci_ref[...]))
        dar_ref[...], dai_ref[...] = dar, dai
        dls_ref[...] = jnp.sum(dls, axis=1, keepdims=True)

    return pl.pallas_call(body, name="s5_disc_bwd", out_shape=[_S(ar.shape), _S(ar.shape), _S(ls.shape)])(
        ar, ai, ls, dlr, dli, dcr, dci)


def _s5_bscale_fwd(cr, ci, br, bi):
    def body(cr_ref, ci_ref, br_ref, bi_ref, or_ref, oi_ref):
        or_ref[...], oi_ref[...] = _cmul(cr_ref[...], ci_ref[...], br_ref[...], bi_ref[...])

    return pl.pallas_call(body, name="s5_bscale_fwd", out_shape=[_S(br.shape), _S(br.shape)])(cr, ci, br, bi)


def _s5_bscale_bwd(cr, ci, br, bi, gr, gi):
    def body(cr_ref, ci_ref, br_ref, bi_ref, gr_ref, gi_ref, dbr_ref, dbi_ref, dcr_ref, dci_ref):
        cr_, ci_, br_, bi_, gr_, gi_ = (r[...] for r in (cr_ref, ci_ref, br_ref, bi_ref, gr_ref, gi_ref))
        dbr_ref[...] = cr_ * gr_ + ci_ * gi_
        dbi_ref[...] = cr_ * gi_ - ci_ * gr_
        dcr_ref[...] = jnp.sum(gr_ * br_ + gi_ * bi_, axis=1, keepdims=True)
        dci_ref[...] = jnp.sum(gi_ * br_ - gr_ * bi_, axis=1, keepdims=True)

    return pl.pallas_call(body, name="s5_bscale_bwd",
                          out_shape=[_S(br.shape), _S(br.shape), _S(cr.shape), _S(cr.shape)])(cr, ci, br, bi, gr, gi)


def _row(w):
    return pl.BlockSpec((TM, w), lambda i: (i, 0))


def _full(shape):
    return pl.BlockSpec(tuple(shape), lambda i: (0,) * len(shape))


def _p_rows(layer):
    return pl.BlockSpec((None, None, TM, PLE_D), lambda i: (layer, 0, i, 0))


def _lrow(layer, width):
    return pl.BlockSpec((None, 1, width), lambda i: (layer, 0, 0))


def _post_fwd(x, hs, z, y0, p, w_glu, b_glu, w_out, g1, b1, ple_w, w_pg, b_pg, g2, b2, layer):
    L = x.shape[0]

    def body(x_ref, hs_ref, z_ref, y0_ref, p_ref, wg_ref, bg_ref, wo_ref, g1_ref, b1_ref, pw_ref, wpg_ref, bpg_ref,
             g2_ref, b2_ref, x2_ref, xh1_ref, xh2_ref, q_ref, gt_ref, rstd1_ref, rstd2_ref):
        rg_gate = z_ref[:, C_RGG:C_RGG + RG_W]
        s5_gate = z_ref[:, C_S5G:C_S5G + S5_W]
        rg_y = hs_ref[...] * _silu_and_grad(rg_gate)[0]
        y1 = _gelu(y0_ref[...])
        gl = _sigmoid(_mm(y1, wg_ref[...]) + bg_ref[...])
        s5_y = (y1 * gl) * _silu_and_grad(s5_gate)[0]
        mix = _mm(jnp.concatenate([rg_y.astype(MXU), s5_y.astype(MXU)], axis=1), wo_ref[...])
        t1 = ALPHA * x_ref[...] + mix
        x1, xh1, rstd1 = _ln_fwd(t1, g1_ref[...], b1_ref[...])
        q = _mm(p_ref[...], pw_ref[...])
        gt = _sigmoid(_mm(x1, wpg_ref[...]) + bpg_ref[...])
        t2 = ALPHA * x1 + q * gt
        x2, xh2, rstd2 = _ln_fwd(t2, g2_ref[...], b2_ref[...])
        x2_ref[...], xh1_ref[...], xh2_ref[...], q_ref[...], gt_ref[...] = x2, xh1, xh2, q, gt
        rstd1_ref[...], rstd2_ref[...] = rstd1, rstd2

    vec = _lrow(layer, D_MODEL)
    return pl.pallas_call(
        body, name="post_fwd", grid=(L // TM,),
        in_specs=[_row(D_MODEL), _row(RG_W), _row(Z_W), _row(S5_W), _p_rows(layer), _full((S5_W, S5_W)), _lrow(layer, S5_W),
                  _full((D_MODEL, D_MODEL)), vec, vec, _full((PLE_D, D_MODEL)), _full((D_MODEL, D_MODEL)), vec, vec, vec],
        out_specs=[_row(D_MODEL)] * 5 + [_row(1)] * 2, out_shape=[_S((L, D_MODEL))] * 5 + [_S((L, 1))] * 2,
        compiler_params=_params(1))(x, hs, z, y0, p, w_glu, b_glu, w_out, g1, b1, ple_w, w_pg, b_pg, g2, b2)


def _post_bwd_a(dx2_or_target, is_top, xh2, xh1, rstd2, rstd1, q, gt, p, w_pg, g1, b1, g2, b2, layer, token=None):
    L = xh1.shape[0]
    extra, extra_specs = _after(token)

    def body(d_ref, xh2_ref, xh1_ref, rstd2_ref, rstd1_ref, q_ref, gt_ref, p_ref, wpg_ref, g1_ref, b1_ref, g2_ref,
             b2_ref, *rest):
        (dt1_ref, dpw_out, dwpg_out, dbpg_ref, dg1_ref, db1_ref, dg2_ref, db2_ref, loss_ref, dpw_ref,
         dwpg_ref) = rest[len(extra):]
        @pl.when(pl.program_id(0) == 0)
        def _():
            for ref in (dpw_ref, dwpg_ref, dbpg_ref, dg1_ref, db1_ref, dg2_ref, db2_ref, loss_ref):
                ref[...] = jnp.zeros_like(ref)

        g1, g2 = g1_ref[...], g2_ref[...]
        xh1, xh2, rstd1, rstd2 = xh1_ref[...], xh2_ref[...], rstd1_ref[...], rstd2_ref[...]
        x1 = xh1 * g1 + b1_ref[...]
        if is_top:
            err = (xh2 * g2 + b2_ref[...]) - d_ref[...]
            loss_ref[...] += _colsum(err * err)
            dx2 = err * (1.0 / D_MODEL)
        else:
            dx2 = d_ref[...]
        p = p_ref[...]
        q, gt = q_ref[...], gt_ref[...]
        dg2_ref[...] += _colsum(dx2 * xh2)
        db2_ref[...] += _colsum(dx2)
        dt2 = _ln_bwd(dx2, xh2, rstd2, g2)
        dq = dt2 * gt
        dgpre = (dt2 * q) * gt * (1.0 - gt)
        dpw_ref[...] += _mm_tn(p, dq)
        dwpg_ref[...] += _mm_tn(x1, dgpre)
        dbpg_ref[...] += _colsum(dgpre)
        dx1 = ALPHA * dt2 + _mm_nt(dgpre, wpg_ref[...])
        dg1_ref[...] += _colsum(dx1 * xh1)
        db1_ref[...] += _colsum(dx1)
        dt1_ref[...] = _ln_bwd(dx1, xh1, rstd1, g1)

        @pl.when(pl.program_id(0) == L // TM - 1)
        def _():
            dpw_out[...] = dpw_ref[...].astype(WIRE)
            dwpg_out[...] = dwpg_ref[...].astype(WIRE)

    vec, lvec = _full((1, D_MODEL)), _lrow(layer, D_MODEL)
    return pl.pallas_call(
        body, name="post_bwd_a_top" if is_top else "post_bwd_a", grid=(L // TM,),
        in_specs=[_row(D_MODEL), _row(D_MODEL), _row(D_MODEL), _row(1), _row(1), _row(D_MODEL), _row(D_MODEL), _p_rows(layer),
                  _full((D_MODEL, D_MODEL)), lvec, lvec, lvec, lvec] + extra_specs,
        out_specs=[_row(D_MODEL), _full((PLE_D, D_MODEL)), _full((D_MODEL, D_MODEL)), vec, vec, vec, vec, vec, vec],
        out_shape=[_S((L, D_MODEL)), _S((PLE_D, D_MODEL), WIRE), _S((D_MODEL, D_MODEL), WIRE)] + [_S((1, D_MODEL))] * 6,
        scratch_shapes=[pltpu.VMEM((PLE_D, D_MODEL), F32), pltpu.VMEM((D_MODEL, D_MODEL), F32)],
        compiler_params=_params(1))(dx2_or_target, xh2, xh1, rstd2, rstd1, q, gt, p, w_pg, g1, b1, g2, b2, *extra)


def _post_bwd_b(dt1, z, hs, y0, w_out, w_glu, b_glu, layer):
    L = dt1.shape[0]

    def body(dt1_ref, z_ref, hs_ref, y0_ref, wo_ref, wg_ref, bg_ref,
             dhs_ref, dy0_ref, dzg_ref, dwo_out, dwg_out, dbg_ref, dwo_ref, dwg_ref):
        @pl.when(pl.program_id(0) == 0)
        def _():
            for ref in (dwo_ref, dwg_ref, dbg_ref):
                ref[...] = jnp.zeros_like(ref)

        dt1b = dt1_ref[...].astype(MXU)
        dm = _mm_nt(dt1b, wo_ref[...])
        d_rgy, d_s5y = dm[:, :RG_W], dm[:, RG_W:]
        rg_gate = z_ref[:, C_RGG:C_RGG + RG_W]
        s5_gate = z_ref[:, C_S5G:C_S5G + S5_W]
        hs = hs_ref[...]
        sl, dsl = _silu_and_grad(rg_gate)
        dhs_ref[...] = d_rgy * sl
        dzg_ref[:, :RG_W] = d_rgy * hs * dsl
        y0 = y0_ref[...]
        y1 = _gelu(y0)
        gl = _sigmoid(_mm(y1, wg_ref[...]) + bg_ref[...])
        y2 = y1 * gl
        sl2, dsl = _silu_and_grad(s5_gate)
        m = jnp.concatenate([(hs * sl).astype(MXU), (y2 * sl2).astype(MXU)], axis=1)
        dwo_ref[...] += _mm_tn(m, dt1b)
        dy2 = d_s5y * sl2
        dzg_ref[:, RG_W:] = d_s5y * y2 * dsl
        dglpre = (dy2 * y1) * gl * (1.0 - gl)
        dwg_ref[...] += _mm_tn(y1, dglpre)
        dbg_ref[...] += _colsum(dglpre)
        dy1 = dy2 * gl + _mm_nt(dglpre, wg_ref[...])
        dy0_ref[...] = dy1 * _gelu_grad(y0)

        @pl.when(pl.program_id(0) == L // TM - 1)
        def _():
            dwo_out[...] = dwo_ref[...].astype(WIRE)
            dwg_out[...] = dwg_ref[...].astype(WIRE)

    return pl.pallas_call(
        body, name="post_bwd_b", grid=(L // TM,),
        in_specs=[_row(D_MODEL), _row(Z_W), _row(RG_W), _row(S5_W), _full((D_MODEL, D_MODEL)),
                  _full((S5_W, S5_W)), _lrow(layer, S5_W)],
        out_specs=[_row(RG_W), _row(S5_W), _row(D_MODEL), _full((D_MODEL, D_MODEL)), _full((S5_W, S5_W)), _full((1, S5_W))],
        out_shape=[_S((L, RG_W)), _S((L, S5_W)), _S((L, D_MODEL)), _S((D_MODEL, D_MODEL), WIRE), _S((S5_W, S5_W), WIRE),
                   _S((1, S5_W))],
        scratch_shapes=[pltpu.VMEM((D_MODEL, D_MODEL), F32), pltpu.VMEM((S5_W, S5_W), F32)],
        compiler_params=_params(1))(dt1, z, hs, y0, w_out, w_glu, b_glu)


def _adamw(parts, w, m, v, token=None):
    nl = len(parts)
    extra, extra_specs = _after(token)
    n, R, C = parts[0].shape
    tr = R
    for cand in (512, 256, 128, 64, 32, 16, 8):
        if R % cand == 0 and n * cand * C * 4 <= 4 * 1024 * 1024:
            tr = cand
            break
    nblk = R // tr

    def body(*refs):
        p_refs = refs[:nl]
        w_ref, m_ref, v_ref = refs[nl:nl + 3]
        g_ref, d_ref, nm_ref, nv_ref = refs[nl + 3 + len(extra):]
        layer = pl.program_id(0)
        g = None
        for li, p_ref in enumerate(p_refs):
            s = p_ref[0].astype(F32)
            for k in range(1, n):
                s = s + p_ref[k].astype(F32)
            g = s if g is None else jnp.where(layer == li, s, g)
        nm = B1 * m_ref[...] + (1.0 - B1) * g
        nv = B2 * v_ref[...] + (1.0 - B2) * (g * g)
        d_ref[...] = (-LR) * ((nm / BC1) / (jnp.sqrt(nv / BC2) + EPS) + WD * w_ref[...])
        g_ref[...], nm_ref[...], nv_ref[...] = g, nm, nv

    def part_spec(li):
        return pl.BlockSpec((n, tr, C), lambda l, i: (0, jnp.where(l == li, i, jnp.where(l < li, 0, nblk - 1)), 0))

    blk = pl.BlockSpec((tr, C), lambda l, i: (l * nblk + i, 0))
    return pl.pallas_call(
        body, name="adamw", grid=(nl, nblk),
        in_specs=[part_spec(li) for li in range(nl)] + [blk, blk, blk] + extra_specs,
        out_specs=[blk] * 4, out_shape=[_S((nl * R, C))] * 4, compiler_params=_params(2))(*parts, w, m, v, *extra)


def _adamw_natural(names, g, w, m, v, name):
    n = len(names)

    def body(*refs):
        for j in range(n):
            g_ref, w_ref, m_ref, v_ref, d_ref, nm_ref, nv_ref = (refs[k * n + j] for k in range(7))
            gj = g_ref[...]
            nm = B1 * m_ref[...] + (1.0 - B1) * gj
            nv = B2 * v_ref[...] + (1.0 - B2) * (gj * gj)
            d_ref[...] = (-LR) * ((nm / BC1) / (jnp.sqrt(nv / BC2) + EPS) + WD * w_ref[...])
            nm_ref[...], nv_ref[...] = nm, nv

    ins = [t[k] for t in (g, w, m, v) for k in names]
    outs = pl.pallas_call(body, name=name, out_shape=[_S(w[k].shape) for _ in range(3) for k in names],
                          compiler_params=pltpu.CompilerParams(vmem_limit_bytes=VMEM_LIMIT))(*ins)
    return [{k: outs[t * n + j] for j, k in enumerate(names)} for t in range(3)]


def _me():
    return lax.axis_index("x"), lax.axis_index("y"), lax.axis_index("c")


def _lin(dev):
    return 4 * dev[0] + 2 * dev[1] + dev[2]


def _blk(ref, axis, size, idx):
    nd = len(ref.shape)
    start = idx * size
    if axis == nd - 1 and size % LANE == 0:
        start = pl.multiple_of(start, LANE)
    elif axis == nd - 2 and size % 16 == 0:
        start = pl.multiple_of(start, 16)
    ix = [slice(None)] * nd
    ix[axis] = pl.ds(start, size)
    return ref.at[tuple(ix)]


HBM_SPEC = pl.BlockSpec(memory_space=pltpu.HBM)
SEM_SPEC = pl.BlockSpec(memory_space=pltpu.SEMAPHORE)
EFFECT = pltpu.SideEffectType.DATAFLOW_SIDE_EFFECTING


def _peers(x, y, c):
    flip = lambda v, f: 1 - v if f else v
    return [(flip(x, k & 4), flip(y, k & 2), flip(c, k & 1)) for k in range(1, N_DEV)]


def _land_shape(mode, s, axis):
    if mode == "gather":
        return s.shape[:axis] + (N_DEV * s.shape[axis],) + s.shape[axis + 1:]
    return (N_DEV,) + s.shape[:axis] + (s.shape[axis] // N_DEV,) + s.shape[axis + 1:]


def _src_view(mode, ref, axis, peer):
    return ref if mode == "gather" else _blk(ref, axis, ref.shape[axis] // N_DEV, peer)


def _dst_view(mode, land, axis, sender):
    return _blk(land, axis, land.shape[axis] // N_DEV, sender) if mode == "gather" else land.at[sender]


def _blocks(mode, land, axis, k):
    if mode == "gather":
        ix = [slice(None)] * len(land.shape)
        ix[axis] = pl.ds(0, k * (land.shape[axis] // N_DEV))
        return land.at[tuple(ix)]
    return land.at[pl.ds(0, k)]


ARRIVALS = {None: N_DEV - 1, "near": 4, "relay": 3}


def _routes(route, x, y, c):
    me, sibling = (x, y, c), (x, y, 1 - c)
    chips = [(1 - x, y), (x, 1 - y), (1 - x, 1 - y)]
    if route == "near":
        return [(me, sibling)] + [(me, (*chip, c)) for chip in chips]
    if route == "relay":
        return [((*chip, c), sibling) for chip in chips]
    return [(me, peer) for peer in _peers(x, y, c)]


def _place_own(mode, srcs, axes, name, after=None):
    n = len(srcs)
    extra, extra_specs = _after(after)

    def body(me_ref, *refs):
        for a in range(n):
            out = refs[n + len(extra) + a]
            out[...] = refs[a][...].reshape(out.shape)

    def at_me(shape, axis):
        return lambda i, me: tuple(me[0] if d == axis else 0 for d in range(len(shape)))

    in_specs, out_specs = [], []
    for s, axis in zip(srcs, axes):
        if mode == "gather":
            in_specs.append(pl.BlockSpec(s.shape, lambda i, me, nd=len(s.shape): (0,) * nd))
            out_specs.append(pl.BlockSpec(s.shape, at_me(s.shape, axis)))
        else:
            blk = s.shape[:axis] + (s.shape[axis] // N_DEV,) + s.shape[axis + 1:]
            in_specs.append(pl.BlockSpec(blk, at_me(blk, axis)))
            out_specs.append(pl.BlockSpec((1,) + blk, at_me((1,) + blk, 0)))
    me = _lin(_me()).astype(jnp.int32).reshape(1)
    return pl.pallas_call(
        body, name=name, out_shape=[_S(_land_shape(mode, s, a), s.dtype) for s, a in zip(srcs, axes)],
        grid_spec=pltpu.PrefetchScalarGridSpec(num_scalar_prefetch=1, grid=(1,), in_specs=in_specs + extra_specs,
                                               out_specs=out_specs),
        compiler_params=_params(1))(me, *srcs, *extra)


def _place_shards(shards, layers, axes, dtypes, name, after=None):
    n = len(shards)
    extra, extra_specs = _after(after)

    def body(me_ref, *refs):
        for a in range(n):
            out = refs[n + len(extra) + a]
            out[...] = refs[a][...].astype(out.dtype)

    in_specs, out_specs, out_shape = [], [], []
    for s, layer, axis, dt in zip(shards, layers, axes, dtypes):
        shape = s.shape if layer is None else s.shape[1:]
        nd = len(shape)
        if layer is None:
            in_specs.append(pl.BlockSpec(shape, lambda i, me, nd=nd: (0,) * nd))
        else:
            in_specs.append(pl.BlockSpec((None,) + shape, lambda i, me, nd=nd, layer=layer: (layer,) + (0,) * nd))
        out_specs.append(pl.BlockSpec(shape, lambda i, me, nd=nd, axis=axis: tuple(me[0] if d == axis else 0 for d in range(nd))))
        out_shape.append(_S(shape[:axis] + (N_DEV * shape[axis],) + shape[axis + 1:], dt))
    me = _lin(_me()).astype(jnp.int32).reshape(1)
    return pl.pallas_call(
        body, name=name, out_shape=out_shape,
        grid_spec=pltpu.PrefetchScalarGridSpec(num_scalar_prefetch=1, grid=(1,), in_specs=in_specs + extra_specs,
                                               out_specs=out_specs),
        compiler_params=_params(1))(me, *shards, *extra)


def _push_start(mode, srcs, lands, axes, name, route=None):
    n, ns = len(lands), len(srcs)

    def body(*refs):
        src_refs, land_refs = refs[:ns], refs[ns:ns + n]
        send_sems, recv_sems = refs[ns + n], refs[ns + n + 1]
        token = refs[-1]
        x, y, c = _me()
        for a in range(n):
            for block, peer in _routes(route, x, y, c):
                there = _dst_view(mode, land_refs[a], axes[a], _lin(block))
                pltpu.make_async_remote_copy(
                    src_ref=_src_view(mode, src_refs[a], axes[a], _lin(peer)) if ns else there, dst_ref=there,
                    send_sem=send_sems.at[a], recv_sem=recv_sems.at[a], device_id=peer, device_id_type=MESH).start()
        token[...] = jnp.zeros_like(token)

    hbm = lambda s: pltpu.HBM(s.shape, s.dtype)
    outs = pl.pallas_call(
        body, name=name,
        out_shape=(pltpu.SemaphoreType.DMA((n,)), pltpu.SemaphoreType.DMA((n,)), *[hbm(s) for s in srcs], *[hbm(s) for s in lands],
                   _S((SUB, LANE))),
        in_specs=[HBM_SPEC] * (ns + n),
        out_specs=(SEM_SPEC, SEM_SPEC, *[HBM_SPEC] * (ns + n), pl.BlockSpec(memory_space=pltpu.VMEM)),
        input_output_aliases={i: 2 + i for i in range(ns + n)},
        compiler_params=pltpu.CompilerParams(has_side_effects=EFFECT),
    )(*[pltpu.with_memory_space_constraint(s, pltpu.HBM) for s in list(srcs) + list(lands)])
    return outs[0], outs[1], outs[2:2 + ns], outs[2 + ns:2 + ns + n], outs[-1]


def _push_wait(mode, send_sems, recv_sems, srcs, lands, axes, after, name, first=0, route=None):
    n, ns = len(lands), len(srcs)
    after = list(after) if isinstance(after, (list, tuple)) else [after]

    def body(*refs):
        land_refs = refs[ns:ns + n]
        send_sems, recv_sems = refs[ns + n], refs[ns + n + 1]
        x, y, c = _me()
        for a in range(n):
            seven = _blocks(mode, land_refs[a], axes[a], ARRIVALS[route])
            cp = pltpu.make_async_remote_copy(src_ref=seven, dst_ref=seven, send_sem=send_sems.at[first + a],
                                              recv_sem=recv_sems.at[first + a],
                                              device_id=(x, y, 1 - c), device_id_type=MESH)
            cp.wait_send()
            cp.wait_recv()

    hbm = lambda s: pltpu.HBM(s.shape, s.dtype)
    outs = pl.pallas_call(
        body, name=name, out_shape=tuple(hbm(s) for s in list(srcs) + list(lands)),
        in_specs=[HBM_SPEC] * (ns + n) + [SEM_SPEC, SEM_SPEC] + [ANY] * len(after), out_specs=tuple([HBM_SPEC] * (ns + n)),
        input_output_aliases={i: i for i in range(ns + n)},
        compiler_params=pltpu.CompilerParams(has_side_effects=EFFECT),
    )(*srcs, *lands, send_sems, recv_sems, *after)
    return outs[ns:]


def _sum_parts(parts):
    n, R, C = parts.shape

    def body(p_ref, o_ref):
        g = p_ref[0]
        for k in range(1, n):
            g = g + p_ref[k]
        o_ref[...] = g

    return pl.pallas_call(body, name="sum_parts", out_shape=_S((R, C)))(parts)


SMALL =['conv_b', 'rg_wa', 'rg_ba', 'rg_wx', 'rg_bx', 'rg_lambda', 's5_a_re', 's5_a_im', 's5_b_re', 's5_b_im',
         's5_c_re', 's5_c_im', 's5_d', 's5_log_step', 's5_b_glu', 'ln1_g', 'ln1_b', 'ple_gate_b', 'ln2_g', 'ln2_b']
WEIGHTS = ['w_in', 'conv_w', 'conv_b', 'rg_wa', 'rg_ba', 'rg_wx', 'rg_bx', 'rg_lambda', 's5_a_re', 's5_a_im', 's5_b_re',
           's5_b_im', 's5_c_re', 's5_c_im', 's5_d', 's5_log_step', 's5_w_glu', 's5_b_glu', 'w_out', 'ln1_g', 'ln1_b',
           'ple_w', 'ple_gate_w', 'ple_gate_b', 'ln2_g', 'ln2_b']
PACK_ROWS_MULT = 64


STORED = {'s5_b_re': (2, 3), 's5_b_im': (2, 3), 's5_d': (1, 2)}


def _stored(k, a):
    return jnp.swapaxes(a, *STORED[k]) if k in STORED else a


def _pack(tree, scalar):
    flat = jnp.concatenate([tree[k].reshape(-1) for k in SMALL] + [scalar.reshape(1)])
    rows = -(-flat.shape[0] // (LANE * PACK_ROWS_MULT)) * PACK_ROWS_MULT
    return jnp.pad(flat, (0, rows * LANE - flat.shape[0])).reshape(rows, LANE)


def _unpack(packed, like):
    flat, out, o = packed.reshape(-1), {}, 0
    for k in SMALL:
        n = math.prod(like[k].shape)
        out[k] = flat[o:o + n].reshape(like[k].shape)
        o += n
    return out, flat[o]


class _NoHooks:
    token = None
    first_token = None

    def first_weights(self, full, after):
        return full

    def layer_start(self, i, W, after):
        return W

    def late_weights(self, i, W, after):
        return W

    def post_done(self, i, g):
        return None

    def smalls_done(self, grads, loss):
        self.small = _small_grads(grads, self.res)
        return None

    def w_in_done(self, i, g):
        return None

    def layer_done(self, i, g, dx):
        return None


def _local_grads(x, p, target, W, disc, hooks):
    depth = 2
    saved = []
    for i in range(depth):
        if i > 0:
            W = hooks.layer_start(i, W, x)
        w = W[i]
        z = _inproj_fwd(x, w['w_in'], hooks.token if i == 0 else None)
        hs, *gates = _rg_fwd(z, w['conv_w'], w['conv_b'], w['wa_bd'], w['wx_bd'], w['rg_ba'], w['rg_bx'], w['rg_lambda'], i)
        d = disc[i]
        y0, s_re, s_im = _s5_fwd(z, d['bb_re'], d['bb_im'], d['lb_re'], d['lb_im'], d['c_re'], d['c_im'], w['s5_d'], i)
        W = hooks.late_weights(i, W, y0)
        w = W[i]
        x2, *norms = _post_fwd(x, hs, z, y0, p, w['s5_w_glu'], w['s5_b_glu'], w['w_out'], w['ln1_g'], w['ln1_b'],
                               w['ple_w'], w['ple_gate_w'], w['ple_gate_b'], w['ln2_g'], w['ln2_b'], i)
        saved.append((x, z, hs, gates, y0, s_re, s_im, norms))
        x = x2

    grads = [None] * depth
    dx = target
    loss = None
    token = None
    for i in reversed(range(depth)):
        w, d = W[i], disc[i]
        xin, z, hs, gates, y0, s_re, s_im, (xh1, xh2, q, gt, rstd1, rstd2) = saved[i]
        g = {}
        (dt1, g['ple_w'], g['ple_gate_w'], g['ple_gate_b'], g['ln1_g'], g['ln1_b'], g['ln2_g'], g['ln2_b'], lrow) = _post_bwd_a(
            dx, i == depth - 1, xh2, xh1, rstd2, rstd1, q, gt, p, w['ple_gate_w'], w['ln1_g'], w['ln1_b'],
            w['ln2_g'], w['ln2_b'], i, token)
        if i == depth - 1:
            loss = 0.5 / D_MODEL * jnp.sum(lrow)
        dhs, dy0, dzg, g['w_out'], g['s5_w_glu'], g['s5_b_glu'] = _post_bwd_b(dt1, z, hs, y0, w['w_out'], w['s5_w_glu'],
                                                                           w['s5_b_glu'], i)
        (dzu, g['bb_re'], g['bb_im'], g['lb_re'], g['lb_im'], g['c_re'], g['c_im'], g['s5_d']) = _s5_bwd(
            dy0, z, s_re, s_im, d['bb_re'], d['bb_im'], d['lb_re'], d['lb_im'], d['c_re'], d['c_im'], w['s5_d'], i,
            hooks.post_done(i, g))
        (dzx, g['conv_w'], g['conv_b'], g['wa_bd'], g['wx_bd'], g['rg_ba'], g['rg_bx'], g['rg_lambda']) = _rg_bwd(
            dhs, z, hs, gates, w['conv_w'], w['wa_bd'], w['wx_bd'], w['rg_lambda'], i)
        if i == 0:
            g['w_in'] = _inproj_bwd_dw(xin, dzx, dzg, dzu, hooks.smalls_done([g, grads[1]], loss))
            dx = _inproj_bwd_dx(dt1, dzx, dzg, dzu, w['w_in'], hooks.w_in_done(i, g))
        else:
            dx, g['w_in'] = _inproj_bwd(dt1, xin, dzx, dzg, dzu, w['w_in'])
        grads[i] = g
        token = hooks.layer_done(i, g, dx)
    return loss, dx, grads


def _s5_layouts_fwd(s5_a_re, s5_a_im, s5_log_step, s5_b_re, s5_b_im, s5_c_re, s5_c_im, token=None):
    depth = s5_a_re.shape[0]
    ar, ai = s5_a_re.reshape(depth * 24, S5_P), s5_a_im.reshape(depth * 24, S5_P)
    ls = s5_log_step.reshape(depth * 24, 1)
    lr, li, cr, ci = _s5_disc_fwd(ar, ai, ls, token)
    per_group = lambda a: a.reshape(depth * 24, 1, S5_P)
    as_c = lambda b: jnp.swapaxes(b, 2, 3).reshape(depth * 24, S5_H, S5_P)
    res = (ar, ai, ls, per_group(cr), per_group(ci), as_c(s5_b_re), as_c(s5_b_im))
    bbr, bbi = _s5_bscale_fwd(*res[3:])
    tiles = lambda a: a.reshape(depth * N_S5_T, S5_GT, S5_H, S5_P)
    rows = lambda a: a.reshape(depth * N_S5_T, S5_GT, S5_P)
    disc = dict(bb_re=tiles(bbr), bb_im=tiles(bbi), lb_re=rows(lr), lb_im=rows(li), c_re=tiles(s5_c_re), c_im=tiles(s5_c_im))
    return [disc] * depth, res


def _s5_layouts_bwd(grads, res):
    ar, ai, ls, cr, ci, br, bi = res
    depth = len(grads)
    stack = lambda k, shape: jnp.stack([g[k] for g in grads]).reshape(shape)
    groups, shape_c = (depth * 24, S5_H, S5_P), (depth, 24, S5_H, S5_P)
    dbr, dbi, dcr, dci = _s5_bscale_bwd(cr, ci, br, bi, stack('bb_re', groups), stack('bb_im', groups))
    gp = (depth * 24, S5_P)
    dar, dai, dls = _s5_disc_bwd(ar, ai, ls, stack('lb_re', gp), stack('lb_im', gp), dcr.reshape(gp), dci.reshape(gp))
    return dict(
        s5_a_re=dar.reshape(depth, 24, S5_P), s5_a_im=dai.reshape(depth, 24, S5_P), s5_log_step=dls.reshape(depth, 24),
        s5_b_re=dbr.reshape(shape_c), s5_b_im=dbi.reshape(shape_c),
        s5_c_re=stack('c_re', shape_c), s5_c_im=stack('c_im', shape_c))


LATE = ('w_out', 'ple_w', 'ple_gate_w', 's5_w_glu')


ROWS = ('conv_b', 'rg_ba', 'rg_bx', 'rg_lambda', 's5_d', 's5_b_glu', 'ln1_g', 'ln1_b', 'ple_gate_b', 'ln2_g', 'ln2_b')


def _shared_weights(full):
    depth = full['conv_b'].shape[0]
    shared = {k: full[k].reshape(depth, 1, -1) for k in ROWS}
    shared.update(conv_w=full['conv_w'], wa_bd=full['rg_wa'], wx_bd=full['rg_wx'])
    return shared


def _layer_weights(full, shared, i):
    return dict(shared, w_in=full['w_in'][i])


class _AllLocal(_NoHooks):
    def __init__(self, full):
        self.full = full

    def late_weights(self, i, W, after):
        W[i].update({k: self.full[k][i] for k in LATE})
        return W


def _full_grads(full, x, p, target, hooks=None):
    hooks = hooks or _AllLocal(full)
    disc, res = _s5_layouts_fwd(full['s5_a_re'], full['s5_a_im'], full['s5_log_step'], full['s5_b_re'], full['s5_b_im'],
                                full['s5_c_re'], full['s5_c_im'], hooks.first_token)
    full = hooks.first_weights(full, disc[-1]['bb_im'])
    shared = _shared_weights(full)
    W = [_layer_weights(full, shared, i) for i in range(2)]
    hooks.res = res
    loss, gx, grads = _local_grads(x, p, target, W, disc, hooks)
    out = dict(hooks.small)
    for k in SHARD_AXIS:
        out[k] = [g[k] for g in grads]
    return loss, gx, out


def _small_grads(grads, res):
    stack = lambda f: jnp.stack([f(g) for g in grads])
    out = _s5_layouts_bwd(grads, res)
    out['conv_w'] = stack(lambda g: g['conv_w'])
    for k in ('conv_b', 'rg_ba', 'rg_bx', 'rg_lambda', 's5_b_glu', 'ln1_g', 'ln1_b', 'ple_gate_b', 'ln2_g', 'ln2_b'):
        out[k] = stack(lambda g: g[k][0])
    out['s5_d'] = _stored('s5_d', stack(lambda g: g['s5_d'][0]).reshape(2, 24, 16))
    out['rg_wa'] = stack(lambda g: g['wa_bd'])
    out['rg_wx'] = stack(lambda g: g['wx_bd'])
    return out


SHARD_AXIS = {'w_in': 2, 'w_out': 1, 'ple_w': 2, 'ple_gate_w': 1, 's5_w_glu': 1}


def kernel(x, p, w_in, conv_w, conv_b, rg_wa, rg_ba, rg_wx, rg_bx, rg_lambda, s5_a_re, s5_a_im, s5_b_re, s5_b_im, s5_c_re, s5_c_im, s5_d, s5_log_step, s5_w_glu, s5_b_glu, w_out, ln1_g, ln1_b, ple_w, ple_gate_w, ple_gate_b, ln2_g, ln2_b, loss_target, m_w_in, m_conv_w, m_conv_b, m_rg_wa, m_rg_ba, m_rg_wx, m_rg_bx, m_rg_lambda, m_s5_a_re, m_s5_a_im, m_s5_b_re, m_s5_b_im, m_s5_c_re, m_s5_c_im, m_s5_d, m_s5_log_step, m_s5_w_glu, m_s5_b_glu, m_w_out, m_ln1_g, m_ln1_b, m_ple_w, m_ple_gate_w, m_ple_gate_b, m_ln2_g, m_ln2_b, v_w_in, v_conv_w, v_conv_b, v_rg_wa, v_rg_ba, v_rg_wx, v_rg_bx, v_rg_lambda, v_s5_a_re, v_s5_a_im, v_s5_b_re, v_s5_b_im, v_s5_c_re, v_s5_c_im, v_s5_d, v_s5_log_step, v_s5_w_glu, v_s5_b_glu, v_w_out, v_ln1_g, v_ln1_b, v_ple_w, v_ple_gate_w, v_ple_gate_b, v_ln2_g, v_ln2_b):
    local = dict(locals())
    w = {k: local[k] for k in WEIGHTS}
    mom = {k: local['m_' + k] for k in WEIGHTS}
    var = {k: local['v_' + k] for k in WEIGHTS}

    big = list(SHARD_AXIS)
    late_axes = [SHARD_AXIS[k] - 1 for k in LATE]
    pushed = {}

    groups = dict(first=(['w_in', 'conv_w'], [0, None], [1, 0]), l0=(list(LATE), [0] * len(LATE), late_axes),
                  l1=(['w_in'] + list(LATE), [1] * (1 + len(LATE)), [1] + late_axes))
    token = None
    for key, members in (("first", ["first"]), ("rest", ["l0", "l1"])):
        names, layers, axes = (sum((groups[m][j] for m in members), []) for j in range(3))
        shards = [w[k] if layer is not None else w[k][None] for k, layer in zip(names, layers)]
        lands = _place_shards(shards, layers, axes, [WIRE if k in big else w[k].dtype for k in names],
                              "place_weights_" + key, token)
        pushed[key] = _push_start("gather", [], lands, axes, "push_weights_" + key, "near" if key == "first" else None)
        token = pushed[key][4]

    def await_weights(key, axes, after):
        s, first = (pushed["first"], 0) if key == "first" else (pushed["rest"], 0 if key == "l0" else len(LATE))
        return _push_wait("gather", s[0], s[1], [], s[3][first:first + len(axes)], axes, after, "await_weights_" + key, first)

    def push_grads(key, g, names, axes):
        srcs = [g[k] for k in names]
        pushed[key] = _push_start("scatter", srcs, _place_own("scatter", srcs, axes, "place_grads_" + key), axes,
                                  "push_grads_" + key)
        return pushed[key][4]

    def await_grads(key, axes, after):
        s = pushed[key]
        return _push_wait("scatter", s[0], s[1], s[2], s[3], axes, after, "await_grads_" + key)

    class Overlap(_NoHooks):
        token = pushed["rest"][4]
        first_token = token

        def first_weights(self, full, after):
            s, axes = pushed["first"], [1, 0]
            near = _push_wait("gather", s[0], s[1], [], s[3], axes, after, "await_weights_near", route="near")
            s = _push_start("gather", [], near, axes, "relay_weights", "relay")
            w_in0, conv = _push_wait("gather", s[0], s[1], [], s[3], axes, s[4], "await_weights_relay", route="relay")
            return dict(full, w_in=[w_in0, None], conv_w=jnp.moveaxis(conv, 0, 2).reshape(2, 4, RG_W))

        def late_weights(self, i, W, after):
            if i == 0:
                W[0].update(zip(LATE, await_weights("l0", late_axes, after)))
            return W

        def layer_start(self, i, W, after):
            lands = await_weights("l1", [1] + late_axes, after)
            W[1].update(zip(LATE, lands[1:]), w_in=lands[0])
            return W

        def post_done(self, i, g):
            return push_grads("late0", g, LATE, late_axes) if i == 0 else None

        def smalls_done(self, grads, loss):
            super().smalls_done(grads, loss)
            conv = jnp.moveaxis(self.small['conv_w'].reshape(2, 4, N_DEV, RG_W // N_DEV), 2, 0)
            self.packed = _pack(self.small, loss)
            return push_grads("small", dict(conv_w=conv.reshape(N_DEV, 8, RG_W // N_DEV), small=self.packed),
                              ['conv_w', 'small'], [0, 0])

        def w_in_done(self, i, g):
            return push_grads("w_in0", g, ['w_in'], [0])

        def layer_done(self, i, g, dx):
            return push_grads("all1", g, ['w_in'] + list(LATE), [0] + late_axes) if i == 1 else None

    hooks = Overlap()
    _, grad_x, g = _full_grads(dict(w), x[0], p, loss_target[0], hooks)

    recv1 = dict(zip(['w_in'] + list(LATE), await_grads("all1", [0] + late_axes, grad_x)))
    recv0 = dict(zip(LATE, await_grads("late0", late_axes, grad_x)))
    outs = {}

    def update(k, parts):
        shard = w[k].shape
        c = shard[-1]
        two = lambda a: a.reshape(-1, c)
        res = _adamw([r.reshape(N_DEV, -1, c) for r in parts], two(w[k]), two(mom[k]), two(var[k]))
        outs[k] = [o.reshape(shard) for o in res]

    for k in LATE:
        update(k, [recv0[k], recv1[k]])
    done = [outs[k][1] for k in LATE]
    conv_parts, small_parts = await_grads("small", [0, 0], done)

    rows = hooks.packed.shape[0] // N_DEV
    mine = _sum_parts(small_parts.reshape(N_DEV, rows, LANE))
    sums = _push_start("gather", [mine], _place_own("gather", [mine], [0], "place_small_sums"), [0], "push_small_sums")
    w_in0, = await_grads("w_in0", [0], sums[4])
    update('w_in', [w_in0, recv1['w_in']])
    update('conv_w', [conv_parts])
    gathered, = _push_wait("gather", sums[0], sums[1], sums[2], sums[3], [0], [outs['w_in'][1], outs['conv_w'][1]],
                           "await_small_sums")
    stored = [{k: _stored(k, t[k]) for k in SMALL} for t in (w, mom, var)]
    summed, loss = _unpack(gathered, stored[0])
    wide = ['s5_b_re', 's5_b_im']
    for names, name in ((wide, "adamw_s5_b"), ([k for k in SMALL if k not in wide], "adamw_small")):
        delta, new_m, new_v = _adamw_natural(names, summed, *stored, name)
        for k in names:
            outs[k] = [_stored(k, o[k]) for o in (summed, delta, new_m, new_v)]

    res = [loss, grad_x[None]]
    for j in range(4):
        res += [outs[k][j] for k in WEIGHTS]
    return tuple(res)
```

```python
import math

import jax
import jax.numpy as jnp
from jax import lax
from jax.experimental import pallas as pl
from jax.experimental.pallas import tpu as pltpu

F32 = jnp.float32
MXU = jnp.bfloat16
WIRE = jnp.bfloat16

N_DEV = 8
D_MODEL = 1024
PLE_D = 256
RG_W = 640
S5_W = 384
S5_P = 64
S5_N = 24 * S5_P
Z_W = 2 * RG_W + 2 * S5_W
C_RGG = RG_W
C_S5U = 2 * RG_W
C_S5G = 2 * RG_W + S5_W
LANE = 128
N_RG_T = RG_W // LANE
N_S5_T = S5_W // LANE
W_BLK = Z_W // N_DEV
ALPHA = (2.0 * 2) ** 0.25
LN_EPS = 1e-5
RG_C = 8.0
LR, B1, B2, EPS, WD, STEP = 0.001, 0.9, 0.999, 1e-08, 0.01, 10
BC1 = 1.0 - B1 ** STEP
BC2 = 1.0 - B2 ** STEP
RC = 512
RC_RG = 1024
TM = 512
TM_MM = 1024
VMEM_LIMIT = 56 * 1024 * 1024

MESH = pl.DeviceIdType.MESH
ANY = pl.BlockSpec(memory_space=pl.ANY)


def _params(n_grid_axes, vmem=VMEM_LIMIT):
    return pltpu.CompilerParams(dimension_semantics=("arbitrary",) * n_grid_axes, vmem_limit_bytes=vmem)


def _S(shape, dtype=F32):
    return jax.ShapeDtypeStruct(tuple(shape), dtype)


def _sigmoid(x):
    return 0.5 * jnp.tanh(0.5 * x) + 0.5


def _silu_and_grad(x):
    s = _sigmoid(x)
    return x * s, s * (1.0 + x * (1.0 - s))


_GELU_C = math.sqrt(2.0 / math.pi)


def _gelu(x):
    return 0.5 * x * (1.0 + jnp.tanh(_GELU_C * (x + 0.044715 * (x * x * x))))


def _gelu_grad(x):
    th = jnp.tanh(_GELU_C * (x + 0.044715 * (x * x * x)))
    return 0.5 * (1.0 + th) + 0.5 * x * (1.0 - th * th) * (_GELU_C * (1.0 + 3.0 * 0.044715 * (x * x)))


def _mm(a, b):
    return jnp.dot(a.astype(MXU), b.astype(MXU), preferred_element_type=F32)


def _mm_nt(a, b):
    return lax.dot_general(a.astype(MXU), b.astype(MXU), (((1,), (1,)), ((), ())), preferred_element_type=F32)


def _mm_tn(a, b):
    return lax.dot_general(a.astype(MXU), b.astype(MXU), (((0,), (0,)), ((), ())), preferred_element_type=F32)


def _ln_fwd(t, g, b):
    mu = jnp.mean(t, axis=-1, keepdims=True)
    tc = t - mu
    var = jnp.mean(tc * tc, axis=-1, keepdims=True)
    rstd = lax.rsqrt(var + LN_EPS)
    xhat = tc * rstd
    return xhat * g + b, xhat, rstd


def _ln_bwd(dy, xhat, rstd, g):
    dxh = dy * g
    m1 = jnp.mean(dxh, axis=-1, keepdims=True)
    m2 = jnp.mean(dxh * xhat, axis=-1, keepdims=True)
    return rstd * (dxh - m1 - xhat * m2)


def _colsum(a):
    return jnp.sum(a, axis=0, keepdims=True)


def _up(x, d, rows, fill):
    n = x.shape[0]
    return jnp.where(rows < n - d, pltpu.roll(x, n - d, 0), fill)


SUB = 8
TILE_STEPS = (1, 2, 4)


def _r8(width):
    return lax.broadcasted_iota(jnp.int32, (SUB, width), 0)


def _scan_real(a, u, carry, reverse=False):
    r8 = _r8(a.shape[1])
    n = a.shape[0] // SUB
    outs = [None] * n
    for k in (reversed(range(n)) if reverse else range(n)):
        A, U = a[SUB * k:SUB * k + SUB], u[SUB * k:SUB * k + SUB]
        for d in TILE_STEPS:
            m = (r8 < SUB - d) if reverse else (r8 >= d)
            sh = SUB - d if reverse else d
            U = A * jnp.where(m, pltpu.roll(U, sh, 0), 0.0) + U
            A = A * jnp.where(m, pltpu.roll(A, sh, 0), 1.0)
        h = A * carry + U
        outs[k] = h
        carry = h[0:1] if reverse else h[SUB - 1:SUB]
    return jnp.concatenate(outs, axis=0), carry


def _tile_powers(lr, li, reverse=False):
    width = lr.shape[1]
    r8 = _r8(width)
    steps = []
    pr, pi = lr, li
    er, ei = jnp.broadcast_to(lr, (SUB, width)), jnp.broadcast_to(li, (SUB, width))
    for d in TILE_STEPS:
        m = (r8 < SUB - d) if reverse else (r8 >= d)
        sh = SUB - d if reverse else d
        steps.append((sh, jnp.where(m, pr, 0.0), jnp.where(m, pi, 0.0)))
        er, ei = _cmul(er, ei, jnp.where(m, pltpu.roll(er, sh, 0), 1.0), jnp.where(m, pltpu.roll(ei, sh, 0), 0.0))
        pr, pi = _cmul(pr, pi, pr, pi)
    return steps, (er, ei)


def _scan_lti(xr, xi, carry, steps, e, reverse=False):
    er, ei = e
    kr, ki = carry
    n = xr.shape[0] // SUB
    outr, outi = [None] * n, [None] * n
    for k in (reversed(range(n)) if reverse else range(n)):
        sr, si = xr[SUB * k:SUB * k + SUB], xi[SUB * k:SUB * k + SUB]
        for sh, pr, pi in steps:
            shr, shi = pltpu.roll(sr, sh, 0), pltpu.roll(si, sh, 0)
            sr, si = sr + (pr * shr - pi * shi), si + (pr * shi + pi * shr)
        sr = sr + (er * kr - ei * ki)
        si = si + (er * ki + ei * kr)
        outr[k], outi[k] = sr, si
        kr, ki = (sr[0:1], si[0:1]) if reverse else (sr[SUB - 1:SUB], si[SUB - 1:SUB])
    return jnp.concatenate(outr, axis=0), jnp.concatenate(outi, axis=0), (kr, ki)


def _halo(ref, c, r0):
    rp = pl.multiple_of(jnp.maximum(r0 - 8, 0), 8)
    return jnp.where(c > 0, ref[pl.ds(rp, 8), :], 0.0)


def _conv_taps(xe):
    return [pltpu.roll(xe, 3, 0)[8:, :], pltpu.roll(xe, 2, 0)[8:, :], pltpu.roll(xe, 1, 0)[8:, :], xe[8:, :]]


def _rg_gates(h, wa, wx, ba, bx, sp):
    r = _sigmoid(_mm(h, wa) + ba)
    i = _sigmoid(_mm(h, wx) + bx)
    log_a = (-RG_C) * r * sp
    a = jnp.exp(log_a)
    mult = jnp.sqrt(-jnp.tanh(log_a) * (a * a + 1.0))
    return r, i, a, mult


def _softplus(y):
    return jnp.maximum(y, 0.0) + jnp.log1p(jnp.exp(-jnp.abs(y)))


def _after(token):
    return ([], []) if token is None else ([token], [ANY])


def _inproj_fwd(x, w_in, token=None):
    L = x.shape[0]

    def body(x_ref, w_ref, *rest):
        rest[-1][...] = _mm(x_ref[...], w_ref[...])

    extra, extra_specs = _after(token)
    tm = min(TM_MM, L)
    return pl.pallas_call(
        body, name="inproj_fwd", grid=(L // tm,),
        in_specs=[pl.BlockSpec((tm, D_MODEL), lambda i: (i, 0)), pl.BlockSpec((D_MODEL, Z_W), lambda i: (0, 0))] + extra_specs,
        out_specs=pl.BlockSpec((tm, Z_W), lambda i: (i, 0)),
        out_shape=_S((L, Z_W)), compiler_params=_params(1))(x, w_in, *extra)


def _inproj_bwd(dt1, x, dzx, dzg, dzu, w_in):
    L = x.shape[0]

    def body(dt1_ref, x_ref, dzx_ref, dzg_ref, dzu_ref, w_ref, dx_ref, dw_ref, acc_ref):
        @pl.when(pl.program_id(0) == 0)
        def _():
            acc_ref[...] = jnp.zeros_like(acc_ref)
        dzg = dzg_ref[...]
        dz = jnp.concatenate([dzx_ref[...], dzg[:, :RG_W], dzu_ref[...], dzg[:, RG_W:]], axis=1).astype(MXU)
        xb = x_ref[...].astype(MXU)
        dx_ref[...] = ALPHA * dt1_ref[...] + _mm_nt(dz, w_ref[...])
        for j in range(N_DEV):
            acc_ref[j] += _mm_tn(xb, dz[:, j * W_BLK:(j + 1) * W_BLK])

        @pl.when(pl.program_id(0) == L // TM - 1)
        def _():
            dw_ref[...] = acc_ref[...].astype(WIRE)

    row = lambda w: pl.BlockSpec((TM, w), lambda i: (i, 0))
    wspec = pl.BlockSpec((N_DEV, D_MODEL, W_BLK), lambda i: (0, 0, 0))
    return pl.pallas_call(
        body, name="inproj_bwd", grid=(L // TM,),
        in_specs=[row(D_MODEL), row(D_MODEL), row(RG_W), row(D_MODEL), row(S5_W),
                  pl.BlockSpec((D_MODEL, Z_W), lambda i: (0, 0))],
        out_specs=[row(D_MODEL), wspec],
        out_shape=[_S((L, D_MODEL)), _S((N_DEV, D_MODEL, W_BLK), WIRE)],
        scratch_shapes=[pltpu.VMEM((N_DEV, D_MODEL, W_BLK), F32)],
        compiler_params=_params(1))(dt1, x, dzx, dzg, dzu, w_in)


TM2 = 512


def _dz_block(dzx_ref, dzg_ref, dzu_ref):
    dzg = dzg_ref[...]
    return jnp.concatenate([dzx_ref[...], dzg[:, :RG_W], dzu_ref[...], dzg[:, RG_W:]], axis=1).astype(MXU)


def _inproj_bwd_dw(x, dzx, dzg, dzu, token=None):
    L = x.shape[0]
    extra, extra_specs = _after(token)

    def body(x_ref, dzx_ref, dzg_ref, dzu_ref, *rest):
        dw_ref, acc_ref = rest[len(extra):]
        @pl.when(pl.program_id(0) == 0)
        def _():
            acc_ref[...] = jnp.zeros_like(acc_ref)
        dz = _dz_block(dzx_ref, dzg_ref, dzu_ref)
        xb = x_ref[...].astype(MXU)
        for j in range(N_DEV):
            acc_ref[j] += _mm_tn(xb, dz[:, j * W_BLK:(j + 1) * W_BLK])

        @pl.when(pl.program_id(0) == L // TM2 - 1)
        def _():
            dw_ref[...] = acc_ref[...].astype(WIRE)

    row = lambda w: pl.BlockSpec((TM2, w), lambda i: (i, 0))
    wspec = pl.BlockSpec((N_DEV, D_MODEL, W_BLK), lambda i: (0, 0, 0))
    return pl.pallas_call(
        body, name="inproj_bwd_dw", grid=(L // TM2,),
        in_specs=[row(D_MODEL), row(RG_W), row(D_MODEL), row(S5_W)] + extra_specs, out_specs=wspec,
        out_shape=_S((N_DEV, D_MODEL, W_BLK), WIRE), scratch_shapes=[pltpu.VMEM((N_DEV, D_MODEL, W_BLK), F32)],
        compiler_params=_params(1))(x, dzx, dzg, dzu, *extra)


def _inproj_bwd_dx(dt1, dzx, dzg, dzu, w_in, token=None):
    L = dt1.shape[0]
    extra, extra_specs = _after(token)

    def body(dt1_ref, dzx_ref, dzg_ref, dzu_ref, w_ref, *rest):
        rest[-1][...] = ALPHA * dt1_ref[...] + _mm_nt(_dz_block(dzx_ref, dzg_ref, dzu_ref), w_ref[...])

    tm = min(TM_MM, L)
    row = lambda w: pl.BlockSpec((tm, w), lambda i: (i, 0))
    return pl.pallas_call(
        body, name="inproj_bwd_dx", grid=(L // tm,),
        in_specs=[row(D_MODEL), row(RG_W), row(D_MODEL), row(S5_W), _full((D_MODEL, Z_W))] + extra_specs,
        out_specs=row(D_MODEL), out_shape=_S((L, D_MODEL)), compiler_params=_params(1))(dt1, dzx, dzg, dzu, w_in, *extra)


def _rg_specs(layer):
    tile = lambda rows: pl.BlockSpec((rows, LANE), lambda c: (0, c))
    ptile = lambda rows: pl.BlockSpec((None, rows, LANE), lambda c: (layer, 0, c))
    pheads = pl.BlockSpec((None, 2, RG_HD, RG_HD), lambda c: (layer, c, 0, 0))
    return tile, ptile, pheads, pl.BlockSpec((2, RG_HD, RG_HD), lambda c: (c, 0, 0))


RG_HD = 64


def _bd2(w):
    z = jnp.zeros((RG_HD, RG_HD), w.dtype)
    return jnp.concatenate([jnp.concatenate([w[0], z], axis=1), jnp.concatenate([z, w[1]], axis=1)], axis=0)


def _bd2_diag(m):
    return jnp.stack([m[:RG_HD, :RG_HD], m[RG_HD:, RG_HD:]])


def _rg_fwd(z, cw, cb, wa_bd, wx_bd, ba, bx, lam, layer):
    L = z.shape[0]
    RC = min(RC_RG, L)

    def body(x_ref, cw_ref, cb_ref, wa_ref, wx_ref, ba_ref, bx_ref, lam_ref, hs_ref, *saved):
        w, b = cw_ref[...], cb_ref[...]
        wa, wx, ba_, bx_ = _bd2(wa_ref[...]).astype(MXU), _bd2(wx_ref[...]).astype(MXU), ba_ref[...], bx_ref[...]
        sp = _softplus(-lam_ref[...])

        def step(c, carry):
            r0 = pl.multiple_of(c * RC, RC)
            xe = jnp.concatenate([_halo(x_ref, c, r0), x_ref[pl.ds(r0, RC), :]], axis=0)
            t = _conv_taps(xe)
            h = t[0] * w[0:1] + t[1] * w[1:2] + t[2] * w[2:3] + t[3] * w[3:4] + b
            r, i, a, mult = _rg_gates(h, wa, wx, ba_, bx_, sp)
            hs, carry = _scan_real(a, mult * (i * h), carry)
            hs_ref[pl.ds(r0, RC), :] = hs
            for ref, val in zip(saved, (h, r, i, a, mult)):
                ref[pl.ds(r0, RC), :] = val
            return carry

        lax.fori_loop(0, L // RC, step, jnp.zeros((1, LANE), F32))

    tile, ptile, pheads, _ = _rg_specs(layer)
    return pl.pallas_call(
        body, name="rg_fwd", grid=(N_RG_T,),
        in_specs=[tile(L), ptile(4), ptile(1), pheads, pheads, ptile(1), ptile(1), ptile(1)],
        out_specs=[tile(L)] * 6, out_shape=[_S((L, RG_W))] * 6, compiler_params=_params(1))(
            z, cw, cb, wa_bd, wx_bd, ba, bx, lam)


def _rg_bwd(dhs, z, hs, gates, cw, wa_bd, wx_bd, lam, layer):
    L = z.shape[0]
    RC = min(RC_RG, L)

    def body(g_ref, x_ref, hs_ref, h_ref, r_ref, i_ref, a_ref, mult_ref, cw_ref, wa_ref, wx_ref, lam_ref,
             dx_ref, dcw_ref, dcb_ref, dwa_out, dwx_out, dba_ref, dbx_ref, dlam_ref, dwa_ref, dwx_ref):
        w = cw_ref[...]
        wa, wx = _bd2(wa_ref[...]).astype(MXU), _bd2(wx_ref[...]).astype(MXU)
        lam = lam_ref[...]
        sp = _softplus(-lam)
        rows = lax.broadcasted_iota(jnp.int32, (RC, LANE), 0)
        for ref in (dcw_ref, dcb_ref, dwa_ref, dwx_ref, dba_ref, dbx_ref, dlam_ref):
            ref[...] = jnp.zeros_like(ref)
        nch = L // RC

        def step(k, carry):
            cin, nxt = carry
            c = nch - 1 - k
            r0 = pl.multiple_of(c * RC, RC)
            xe = jnp.concatenate([_halo(x_ref, c, r0), x_ref[pl.ds(r0, RC), :]], axis=0)
            t = _conv_taps(xe)
            h, r, i, a, mult = (ref[pl.ds(r0, RC), :] for ref in (h_ref, r_ref, i_ref, a_ref, mult_ref))
            hs_e = jnp.concatenate([_halo(hs_ref, c, r0), hs_ref[pl.ds(r0, RC), :]], axis=0)
            hs_prev = pltpu.roll(hs_e, 1, 0)[8:, :]
            g = g_ref[pl.ds(r0, RC), :]
            cc, cin_new = _scan_real(a, a * g, cin, reverse=True)
            dh = g + _up(cc, 1, rows, cin)
            ih = i * h
            dlog_a = dh * hs_prev * a - (dh * ih) * (a * a) / mult
            di = dh * mult * h
            dhin = dh * mult * i
            dr = dlog_a * ((-RG_C) * sp)
            dlam_ref[...] += _colsum(dlog_a * r)
            dra = dr * r * (1.0 - r)
            dia = di * i * (1.0 - i)
            dwa_ref[...] += _mm_tn(h, dra)
            dwx_ref[...] += _mm_tn(h, dia)
            dba_ref[...] += _colsum(dra)
            dbx_ref[...] += _colsum(dia)
            dhin = dhin + _mm_nt(dra, wa) + _mm_nt(dia, wx)
            de = jnp.concatenate([dhin, nxt], axis=0)
            n = RC + 8
            dx = (dhin * w[3:4] + pltpu.roll(de, n - 1, 0)[:RC, :] * w[2:3]
                  + pltpu.roll(de, n - 2, 0)[:RC, :] * w[1:2] + pltpu.roll(de, n - 3, 0)[:RC, :] * w[0:1])
            dx_ref[pl.ds(r0, RC), :] = dx
            for kk in range(4):
                dcw_ref[kk:kk + 1, :] += _colsum(dhin * t[kk])
            dcb_ref[...] += _colsum(dhin)
            return cin_new, dhin[0:8, :]

        lax.fori_loop(0, nch, step, (jnp.zeros((1, LANE), F32), jnp.zeros((8, LANE), F32)))
        dlam_ref[...] = dlam_ref[...] * (RG_C * _sigmoid(-lam))
        dwa_out[...], dwx_out[...] = _bd2_diag(dwa_ref[...]), _bd2_diag(dwx_ref[...])

    tile, ptile, pheads, gheads = _rg_specs(layer)
    heads = _S((2 * N_RG_T, RG_HD, RG_HD))
    return pl.pallas_call(
        body, name="rg_bwd", grid=(N_RG_T,),
        in_specs=[tile(L)] * 8 + [ptile(4), pheads, pheads, ptile(1)],
        out_specs=[tile(L), tile(4), tile(1), gheads, gheads, tile(1), tile(1), tile(1)],
        out_shape=[_S((L, RG_W)), _S((4, RG_W)), _S((1, RG_W)), heads, heads, _S((1, RG_W)), _S((1, RG_W)), _S((1, RG_W))],
        scratch_shapes=[pltpu.VMEM((LANE, LANE), F32), pltpu.VMEM((LANE, LANE), F32)],
        compiler_params=_params(1))(dhs, z, hs, *gates, cw, wa_bd, wx_bd, lam)


def _cmul(ar, ai, br, bi):
    return ar * br - ai * bi, ar * bi + ai * br


S5_TW = S5_N // N_S5_T


S5_H = 16
S5_GT = LANE // S5_H


def _s5_specs(L, layer):
    in_tile = pl.BlockSpec((L, LANE), lambda t: (0, t))
    st = pl.BlockSpec((L, S5_TW), lambda t: (0, t))
    pg = pl.BlockSpec((None, S5_GT, S5_H, S5_P), lambda t: (layer * N_S5_T + t, 0, 0, 0))
    plb = pl.BlockSpec((None, S5_GT, S5_P), lambda t: (layer * N_S5_T + t, 0, 0))
    gg = pl.BlockSpec((None, S5_GT, S5_H, S5_P), lambda t: (t, 0, 0, 0))
    glb = pl.BlockSpec((None, S5_GT, S5_P), lambda t: (t, 0, 0))
    dv = pl.BlockSpec((1, LANE), lambda t: (0, t))
    return in_tile, st, pg, plb, gg, glb, dv


def _bd8(blocks):
    rows = []
    for g in range(S5_GT):
        pieces = [blocks[g]]
        if g:
            pieces.insert(0, jnp.zeros((S5_H, S5_P * g), blocks.dtype))
        if g < S5_GT - 1:
            pieces.append(jnp.zeros((S5_H, S5_P * (S5_GT - 1 - g)), blocks.dtype))
        rows.append(jnp.concatenate(pieces, axis=1))
    return jnp.concatenate(rows, axis=0)


def _bd8_diag(m):
    return jnp.stack([m[S5_H * g:S5_H * (g + 1), S5_P * g:S5_P * (g + 1)] for g in range(S5_GT)])


def _row8(v):
    return jnp.concatenate([v[g:g + 1] for g in range(S5_GT)], axis=1)


def _row8_split(r):
    return jnp.concatenate([r[:, S5_P * g:S5_P * (g + 1)] for g in range(S5_GT)], axis=0)


def _layer_row_tile(layer):
    return pl.BlockSpec((None, 1, LANE), lambda t: (layer, 0, t))


def _s5_fwd(z, bb_re, bb_im, lb_re, lb_im, c_re, c_im, dvec, layer):
    L = z.shape[0]

    def body(u_ref, bbr_ref, bbi_ref, lr_ref, li_ref, cr_ref, ci_ref, d_ref, y_ref, sr_ref, si_ref):
        bbr, bbi = _bd8(bbr_ref[...]).astype(MXU), _bd8(bbi_ref[...]).astype(MXU)
        cr, ci = _bd8(cr_ref[...]).astype(MXU), _bd8(ci_ref[...]).astype(MXU)
        dv = d_ref[...]
        steps, e = _tile_powers(_row8(lr_ref[...]), _row8(li_ref[...]))

        def step(c, carry):
            r0 = pl.multiple_of(c * RC, RC)
            u = u_ref[pl.ds(r0, RC), :]
            ub = u.astype(MXU)
            sr = jnp.dot(ub, bbr, preferred_element_type=F32)
            si = jnp.dot(ub, bbi, preferred_element_type=F32)
            sr, si, carry = _scan_lti(sr, si, carry, steps, e)
            sr_ref[pl.ds(r0, RC), :] = sr
            si_ref[pl.ds(r0, RC), :] = si
            y_ref[pl.ds(r0, RC), :] = dv * u + (_mm_nt(sr, cr) - _mm_nt(si, ci))
            return carry

        zero = jnp.zeros((1, S5_TW), F32)
        lax.fori_loop(0, L // RC, step, (zero, zero))

    in_tile, st, pg, plb, _, _, _ = _s5_specs(L, layer)
    u_tile = pl.BlockSpec((L, LANE), lambda t: (0, C_S5U // LANE + t))
    return pl.pallas_call(
        body, name="s5_fwd", grid=(N_S5_T,),
        in_specs=[u_tile, pg, pg, plb, plb, pg, pg, _layer_row_tile(layer)],
        out_specs=[in_tile, st, st],
        out_shape=[_S((L, S5_W)), _S((L, S5_N)), _S((L, S5_N))],
        compiler_params=_params(1))(z, bb_re, bb_im, lb_re, lb_im, c_re, c_im, dvec)


def _s5_bwd(dy0, z, s_re, s_im, bb_re, bb_im, lb_re, lb_im, c_re, c_im, dvec, layer, token=None):
    L = z.shape[0]
    extra, extra_specs = _after(token)

    def body(dy_ref, u_ref, sr_ref, si_ref, bbr_ref, bbi_ref, lr_ref, li_ref, cr_ref, ci_ref, d_ref, *rest):
        (du_ref, dbbr_out, dbbi_out, dlr_out, dli_out, dcr_out, dci_out, dd_ref,
         dbbr_ref, dbbi_ref, dcr_ref, dci_ref, dlr_ref, dli_ref) = rest[len(extra):]
        bbr, bbi = _bd8(bbr_ref[...]).astype(MXU), _bd8(bbi_ref[...]).astype(MXU)
        cr, ci = _bd8(cr_ref[...]).astype(MXU), _bd8(ci_ref[...]).astype(MXU)
        lr, li = _row8(lr_ref[...]), -_row8(li_ref[...])
        dv = d_ref[...]
        steps, e = _tile_powers(lr, li, reverse=True)
        for ref in (dbbr_ref, dbbi_ref, dlr_ref, dli_ref, dcr_ref, dci_ref, dd_ref):
            ref[...] = jnp.zeros_like(ref)
        nch = L // RC

        def step(k, carry):
            c = nch - 1 - k
            r0 = pl.multiple_of(c * RC, RC)
            dy = dy_ref[pl.ds(r0, RC), :]
            u = u_ref[pl.ds(r0, RC), :]
            dyb, ub = dy.astype(MXU), u.astype(MXU)
            sr, si = sr_ref[pl.ds(r0, RC), :], si_ref[pl.ds(r0, RC), :]
            dcr_ref[...] += _mm_tn(dyb, sr)
            dci_ref[...] -= _mm_tn(dyb, si)
            gr = jnp.dot(dyb, cr, preferred_element_type=F32)
            gi = -jnp.dot(dyb, ci, preferred_element_type=F32)
            gr, gi, carry = _scan_lti(gr, gi, carry, steps, e, reverse=True)
            pr_ = pltpu.roll(jnp.concatenate([_halo(sr_ref, c, r0), sr], axis=0), 1, 0)[8:, :]
            pi_ = pltpu.roll(jnp.concatenate([_halo(si_ref, c, r0), si], axis=0), 1, 0)[8:, :]
            dlr_ref[...] += _colsum(pr_ * gr + pi_ * gi)
            dli_ref[...] += _colsum(pr_ * gi - pi_ * gr)
            grb, gib = gr.astype(MXU), gi.astype(MXU)
            dbbr_ref[...] += _mm_tn(ub, grb)
            dbbi_ref[...] += _mm_tn(ub, gib)
            du_ref[pl.ds(r0, RC), :] = dv * dy + (_mm_nt(grb, bbr) + _mm_nt(gib, bbi))
            dd_ref[...] += _colsum(dy * u)
            return carry

        zero = jnp.zeros((1, S5_TW), F32)
        lax.fori_loop(0, nch, step, (zero, zero))
        dbbr_out[...], dbbi_out[...] = _bd8_diag(dbbr_ref[...]), _bd8_diag(dbbi_ref[...])
        dcr_out[...], dci_out[...] = _bd8_diag(dcr_ref[...]), _bd8_diag(dci_ref[...])
        dlr_out[...], dli_out[...] = _row8_split(dlr_ref[...]), _row8_split(dli_ref[...])

    in_tile, st, pg, plb, gg, glb, dv = _s5_specs(L, layer)
    u_tile = pl.BlockSpec((L, LANE), lambda t: (0, C_S5U // LANE + t))
    groups, rows = _S((N_S5_T, S5_GT, S5_H, S5_P)), _S((N_S5_T, S5_GT, S5_P))
    wide = pltpu.VMEM((LANE, S5_TW), F32)
    return pl.pallas_call(
        body, name="s5_bwd", grid=(N_S5_T,),
        in_specs=[in_tile, u_tile, st, st, pg, pg, plb, plb, pg, pg, _layer_row_tile(layer)] + extra_specs,
        out_specs=[in_tile, gg, gg, glb, glb, gg, gg, dv],
        out_shape=[_S((L, S5_W)), groups, groups, rows, rows, groups, groups, _S((1, S5_W))],
        scratch_shapes=[wide, wide, wide, wide, pltpu.VMEM((1, S5_TW), F32), pltpu.VMEM((1, S5_TW), F32)],
        compiler_params=_params(1))(dy0, z, s_re, s_im, bb_re, bb_im, lb_re, lb_im, c_re, c_im, dvec, *extra)


def _disc(ar, ai, ls):
    dt = jnp.exp(ls)
    mag = jnp.exp(ar * dt)
    lr = mag * jnp.cos(ai * dt)
    li = mag * jnp.sin(ai * dt)
    den = ar * ar + ai * ai
    cr = ((lr - 1.0) * ar + li * ai) / den
    ci = (li * ar - (lr - 1.0) * ai) / den
    return lr, li, cr, ci


def _s5_disc_fwd(ar, ai, ls, token=None):
    extra, extra_specs = _after(token)

    def body(ar_ref, ai_ref, ls_ref, *rest):
        lr_ref, li_ref, cr_ref, ci_ref = rest[len(extra):]
        lr, li, cr, ci = _disc(ar_ref[...], ai_ref[...], ls_ref[...])
        lr_ref[...], li_ref[...], cr_ref[...], ci_ref[...] = lr, li, cr, ci

    sh = _S(ar.shape)
    vm = pl.BlockSpec(memory_space=pltpu.VMEM)
    return pl.pallas_call(body, name="s5_disc_fwd", in_specs=[vm, vm, vm] + extra_specs, out_shape=[sh, sh, sh, sh])(
        ar, ai, ls, *extra)


def _s5_disc_bwd(ar, ai, ls, dlr, dli, dcr, dci):
    def body(ar_ref, ai_ref, ls_ref, dlr_ref, dli_ref, dcr_ref, dci_ref, dar_ref, dai_ref, dls_ref):
        _, vjp = jax.vjp(_disc, ar_ref[...], ai_ref[...], jnp.broadcast_to(ls_ref[...], ar_ref.shape))
        dar, dai, dls = vjp((dlr_ref[...], dli_ref[...], dcr_ref[...], dci_ref[...]))
        dar_ref[...], dai_ref[...] = dar, dai
        dls_ref[...] = jnp.sum(dls, axis=1, keepdims=True)

    return pl.pallas_call(body, name="s5_disc_bwd", out_shape=[_S(ar.shape), _S(ar.shape), _S(ls.shape)])(
        ar, ai, ls, dlr, dli, dcr, dci)


def _s5_bscale_fwd(cr, ci, br, bi):
    def body(cr_ref, ci_ref, br_ref, bi_ref, or_ref, oi_ref):
        or_ref[...], oi_ref[...] = _cmul(cr_ref[...], ci_ref[...], br_ref[...], bi_ref[...])

    return pl.pallas_call(body, name="s5_bscale_fwd", out_shape=[_S(br.shape), _S(br.shape)])(cr, ci, br, bi)


def _s5_bscale_bwd(cr, ci, br, bi, gr, gi):
    def body(cr_ref, ci_ref, br_ref, bi_ref, gr_ref, gi_ref, dbr_ref, dbi_ref, dcr_ref, dci_ref):
        cr_, ci_, br_, bi_, gr_, gi_ = (r[...] for r in (cr_ref, ci_ref, br_ref, bi_ref, gr_ref, gi_ref))
        dbr_ref[...] = cr_ * gr_ + ci_ * gi_
        dbi_ref[...] = cr_ * gi_ - ci_ * gr_
        dcr_ref[...] = jnp.sum(gr_ * br_ + gi_ * bi_, axis=1, keepdims=True)
        dci_ref[...] = jnp.sum(gi_ * br_ - gr_ * bi_, axis=1, keepdims=True)

    return pl.pallas_call(body, name="s5_bscale_bwd",
                          out_shape=[_S(br.shape), _S(br.shape), _S(cr.shape), _S(cr.shape)])(cr, ci, br, bi, gr, gi)


def _row(w):
    return pl.BlockSpec((TM, w), lambda i: (i, 0))


def _full(shape):
    return pl.BlockSpec(tuple(shape), lambda i: (0,) * len(shape))


def _p_rows(layer):
    return pl.BlockSpec((None, None, TM, PLE_D), lambda i: (layer, 0, i, 0))


def _lrow(layer, width):
    return pl.BlockSpec((None, 1, width), lambda i: (layer, 0, 0))


def _post_fwd(x, hs, z, y0, p, w_glu, b_glu, w_out, g1, b1, ple_w, w_pg, b_pg, g2, b2, layer):
    L = x.shape[0]

    def body(x_ref, hs_ref, z_ref, y0_ref, p_ref, wg_ref, bg_ref, wo_ref, g1_ref, b1_ref, pw_ref, wpg_ref, bpg_ref,
             g2_ref, b2_ref, x2_ref, xh1_ref, xh2_ref, q_ref, gt_ref, rstd1_ref, rstd2_ref):
        rg_gate = z_ref[:, C_RGG:C_RGG + RG_W]
        s5_gate = z_ref[:, C_S5G:C_S5G + S5_W]
        rg_y = hs_ref[...] * _silu_and_grad(rg_gate)[0]
        y1 = _gelu(y0_ref[...])
        gl = _sigmoid(_mm(y1, wg_ref[...]) + bg_ref[...])
        s5_y = (y1 * gl) * _silu_and_grad(s5_gate)[0]
        mix = _mm(jnp.concatenate([rg_y.astype(MXU), s5_y.astype(MXU)], axis=1), wo_ref[...])
        t1 = ALPHA * x_ref[...] + mix
        x1, xh1, rstd1 = _ln_fwd(t1, g1_ref[...], b1_ref[...])
        q = _mm(p_ref[...], pw_ref[...])
        gt = _sigmoid(_mm(x1, wpg_ref[...]) + bpg_ref[...])
        t2 = ALPHA * x1 + q * gt
        x2, xh2, rstd2 = _ln_fwd(t2, g2_ref[...], b2_ref[...])
        x2_ref[...], xh1_ref[...], xh2_ref[...], q_ref[...], gt_ref[...] = x2, xh1, xh2, q, gt
        rstd1_ref[...], rstd2_ref[...] = rstd1, rstd2

    vec = _lrow(layer, D_MODEL)
    return pl.pallas_call(
        body, name="post_fwd", grid=(L // TM,),
        in_specs=[_row(D_MODEL), _row(RG_W), _row(Z_W), _row(S5_W), _p_rows(layer), _full((S5_W, S5_W)), _lrow(layer, S5_W),
                  _full((D_MODEL, D_MODEL)), vec, vec, _full((PLE_D, D_MODEL)), _full((D_MODEL, D_MODEL)), vec, vec, vec],
        out_specs=[_row(D_MODEL)] * 5 + [_row(1)] * 2, out_shape=[_S((L, D_MODEL))] * 5 + [_S((L, 1))] * 2,
        compiler_params=_params(1))(x, hs, z, y0, p, w_glu, b_glu, w_out, g1, b1, ple_w, w_pg, b_pg, g2, b2)


def _post_bwd_a(dx2_or_target, is_top, xh2, xh1, rstd2, rstd1, q, gt, p, w_pg, g1, b1, g2, b2, layer, token=None):
    L = xh1.shape[0]
    extra, extra_specs = _after(token)

    def body(d_ref, xh2_ref, xh1_ref, rstd2_ref, rstd1_ref, q_ref, gt_ref, p_ref, wpg_ref, g1_ref, b1_ref, g2_ref,
             b2_ref, *rest):
        (dt1_ref, dpw_out, dwpg_out, dbpg_ref, dg1_ref, db1_ref, dg2_ref, db2_ref, loss_ref, dpw_ref,
         dwpg_ref) = rest[len(extra):]
        @pl.when(pl.program_id(0) == 0)
        def _():
            for ref in (dpw_ref, dwpg_ref, dbpg_ref, dg1_ref, db1_ref, dg2_ref, db2_ref, loss_ref):
                ref[...] = jnp.zeros_like(ref)

        g1, g2 = g1_ref[...], g2_ref[...]
        xh1, xh2, rstd1, rstd2 = xh1_ref[...], xh2_ref[...], rstd1_ref[...], rstd2_ref[...]
        x1 = xh1 * g1 + b1_ref[...]
        if is_top:
            err = (xh2 * g2 + b2_ref[...]) - d_ref[...]
            loss_ref[...] += _colsum(err * err)
            dx2 = err * (1.0 / D_MODEL)
        else:
            dx2 = d_ref[...]
        p = p_ref[...]
        q, gt = q_ref[...], gt_ref[...]
        dg2_ref[...] += _colsum(dx2 * xh2)
        db2_ref[...] += _colsum(dx2)
        dt2 = _ln_bwd(dx2, xh2, rstd2, g2)
        dq = dt2 * gt
        dgpre = (dt2 * q) * gt * (1.0 - gt)
        dpw_ref[...] += _mm_tn(p, dq)
        dwpg_ref[...] += _mm_tn(x1, dgpre)
        dbpg_ref[...] += _colsum(dgpre)
        dx1 = ALPHA * dt2 + _mm_nt(dgpre, wpg_ref[...])
        dg1_ref[...] += _colsum(dx1 * xh1)
        db1_ref[...] += _colsum(dx1)
        dt1_ref[...] = _ln_bwd(dx1, xh1, rstd1, g1)

        @pl.when(pl.program_id(0) == L // TM - 1)
        def _():
            dpw_out[...] = dpw_ref[...].astype(WIRE)
            dwpg_out[...] = dwpg_ref[...].astype(WIRE)

    vec, lvec = _full((1, D_MODEL)), _lrow(layer, D_MODEL)
    return pl.pallas_call(
        body, name="post_bwd_a_top" if is_top else "post_bwd_a", grid=(L // TM,),
        in_specs=[_row(D_MODEL), _row(D_MODEL), _row(D_MODEL), _row(1), _row(1), _row(D_MODEL), _row(D_MODEL), _p_rows(layer),
                  _full((D_MODEL, D_MODEL)), lvec, lvec, lvec, lvec] + extra_specs,
        out_specs=[_row(D_MODEL), _full((PLE_D, D_MODEL)), _full((D_MODEL, D_MODEL)), vec, vec, vec, vec, vec, vec],
        out_shape=[_S((L, D_MODEL)), _S((PLE_D, D_MODEL), WIRE), _S((D_MODEL, D_MODEL), WIRE)] + [_S((1, D_MODEL))] * 6,
        scratch_shapes=[pltpu.VMEM((PLE_D, D_MODEL), F32), pltpu.VMEM((D_MODEL, D_MODEL), F32)],
        compiler_params=_params(1))(dx2_or_target, xh2, xh1, rstd2, rstd1, q, gt, p, w_pg, g1, b1, g2, b2, *extra)


def _post_bwd_b(dt1, z, hs, y0, w_out, w_glu, b_glu, layer):
    L = dt1.shape[0]

    def body(dt1_ref, z_ref, hs_ref, y0_ref, wo_ref, wg_ref, bg_ref,
             dhs_ref, dy0_ref, dzg_ref, dwo_out, dwg_out, dbg_ref, dwo_ref, dwg_ref):
        @pl.when(pl.program_id(0) == 0)
        def _():
            for ref in (dwo_ref, dwg_ref, dbg_ref):
                ref[...] = jnp.zeros_like(ref)

        dt1b = dt1_ref[...].astype(MXU)
        dm = _mm_nt(dt1b, wo_ref[...])
        d_rgy, d_s5y = dm[:, :RG_W], dm[:, RG_W:]
        rg_gate = z_ref[:, C_RGG:C_RGG + RG_W]
        s5_gate = z_ref[:, C_S5G:C_S5G + S5_W]
        hs = hs_ref[...]
        sl, dsl = _silu_and_grad(rg_gate)
        dhs_ref[...] = d_rgy * sl
        dzg_ref[:, :RG_W] = d_rgy * hs * dsl
        y0 = y0_ref[...]
        y1 = _gelu(y0)
        gl = _sigmoid(_mm(y1, wg_ref[...]) + bg_ref[...])
        y2 = y1 * gl
        sl2, dsl = _silu_and_grad(s5_gate)
        m = jnp.concatenate([(hs * sl).astype(MXU), (y2 * sl2).astype(MXU)], axis=1)
        dwo_ref[...] += _mm_tn(m, dt1b)
        dy2 = d_s5y * sl2
        dzg_ref[:, RG_W:] = d_s5y * y2 * dsl
        dglpre = (dy2 * y1) * gl * (1.0 - gl)
        dwg_ref[...] += _mm_tn(y1, dglpre)
        dbg_ref[...] += _colsum(dglpre)
        dy1 = dy2 * gl + _mm_nt(dglpre, wg_ref[...])
        dy0_ref[...] = dy1 * _gelu_grad(y0)

        @pl.when(pl.program_id(0) == L // TM - 1)
        def _():
            dwo_out[...] = dwo_ref[...].astype(WIRE)
            dwg_out[...] = dwg_ref[...].astype(WIRE)

    return pl.pallas_call(
        body, name="post_bwd_b", grid=(L // TM,),
        in_specs=[_row(D_MODEL), _row(Z_W), _row(RG_W), _row(S5_W), _full((D_MODEL, D_MODEL)),
                  _full((S5_W, S5_W)), _lrow(layer, S5_W)],
        out_specs=[_row(RG_W), _row(S5_W), _row(D_MODEL), _full((D_MODEL, D_MODEL)), _full((S5_W, S5_W)), _full((1, S5_W))],
        out_shape=[_S((L, RG_W)), _S((L, S5_W)), _S((L, D_MODEL)), _S((D_MODEL, D_MODEL), WIRE), _S((S5_W, S5_W), WIRE),
                   _S((1, S5_W))],
        scratch_shapes=[pltpu.VMEM((D_MODEL, D_MODEL), F32), pltpu.VMEM((S5_W, S5_W), F32)],
        compiler_params=_params(1))(dt1, z, hs, y0, w_out, w_glu, b_glu)


def _adamw(parts, w, m, v, token=None):
    nl = len(parts)
    extra, extra_specs = _after(token)
    n, R, C = parts[0].shape
    tr = R
    for cand in (512, 256, 128, 64, 32, 16, 8):
        if R % cand == 0 and n * cand * C * 4 <= 4 * 1024 * 1024:
            tr = cand
            break
    nblk = R // tr

    def body(*refs):
        p_refs = refs[:nl]
        w_ref, m_ref, v_ref = refs[nl:nl + 3]
        g_ref, d_ref, nm_ref, nv_ref = refs[nl + 3 + len(extra):]
        layer = pl.program_id(0)
        g = None
        for li, p_ref in enumerate(p_refs):
            s = p_ref[0].astype(F32)
            for k in range(1, n):
                s = s + p_ref[k].astype(F32)
            g = s if g is None else jnp.where(layer == li, s, g)
        nm = B1 * m_ref[...] + (1.0 - B1) * g
        nv = B2 * v_ref[...] + (1.0 - B2) * (g * g)
        d_ref[...] = (-LR) * ((nm / BC1) / (jnp.sqrt(nv / BC2) + EPS) + WD * w_ref[...])
        g_ref[...], nm_ref[...], nv_ref[...] = g, nm, nv

    def part_spec(li):
        return pl.BlockSpec((n, tr, C), lambda l, i: (0, jnp.where(l == li, i, jnp.where(l < li, 0, nblk - 1)), 0))

    blk = pl.BlockSpec((tr, C), lambda l, i: (l * nblk + i, 0))
    return pl.pallas_call(
        body, name="adamw", grid=(nl, nblk),
        in_specs=[part_spec(li) for li in range(nl)] + [blk, blk, blk] + extra_specs,
        out_specs=[blk] * 4, out_shape=[_S((nl * R, C))] * 4, compiler_params=_params(2))(*parts, w, m, v, *extra)


def _adamw_natural(names, g, w, m, v, name):
    n = len(names)

    def body(*refs):
        for j in range(n):
            g_ref, w_ref, m_ref, v_ref, d_ref, nm_ref, nv_ref = (refs[k * n + j] for k in range(7))
            gj = g_ref[...]
            nm = B1 * m_ref[...] + (1.0 - B1) * gj
            nv = B2 * v_ref[...] + (1.0 - B2) * (gj * gj)
            d_ref[...] = (-LR) * ((nm / BC1) / (jnp.sqrt(nv / BC2) + EPS) + WD * w_ref[...])
            nm_ref[...], nv_ref[...] = nm, nv

    ins = [t[k] for t in (g, w, m, v) for k in names]
    outs = pl.pallas_call(body, name=name, out_shape=[_S(w[k].shape) for _ in range(3) for k in names],
                          compiler_params=pltpu.CompilerParams(vmem_limit_bytes=VMEM_LIMIT))(*ins)
    return [{k: outs[t * n + j] for j, k in enumerate(names)} for t in range(3)]


def _me():
    return lax.axis_index("x"), lax.axis_index("y"), lax.axis_index("c")


def _lin(dev):
    return 4 * dev[0] + 2 * dev[1] + dev[2]


def _blk(ref, axis, size, idx):
    nd = len(ref.shape)
    start = idx * size
    if axis == nd - 1 and size % LANE == 0:
        start = pl.multiple_of(start, LANE)
    elif axis == nd - 2 and size % 16 == 0:
        start = pl.multiple_of(start, 16)
    ix = [slice(None)] * nd
    ix[axis] = pl.ds(start, size)
    return ref.at[tuple(ix)]


HBM_SPEC = pl.BlockSpec(memory_space=pltpu.HBM)
SEM_SPEC = pl.BlockSpec(memory_space=pltpu.SEMAPHORE)
EFFECT = pltpu.SideEffectType.DATAFLOW_SIDE_EFFECTING


def _peers(x, y, c):
    flip = lambda v, f: 1 - v if f else v
    return [(flip(x, k & 4), flip(y, k & 2), flip(c, k & 1)) for k in range(1, N_DEV)]


def _land_shape(mode, s, axis):
    if mode == "gather":
        return s.shape[:axis] + (N_DEV * s.shape[axis],) + s.shape[axis + 1:]
    return (N_DEV,) + s.shape[:axis] + (s.shape[axis] // N_DEV,) + s.shape[axis + 1:]


def _src_view(mode, ref, axis, peer):
    return ref if mode == "gather" else _blk(ref, axis, ref.shape[axis] // N_DEV, peer)


def _dst_view(mode, land, axis, sender):
    return _blk(land, axis, land.shape[axis] // N_DEV, sender) if mode == "gather" else land.at[sender]


def _blocks(mode, land, axis, k):
    if mode == "gather":
        ix = [slice(None)] * len(land.shape)
        ix[axis] = pl.ds(0, k * (land.shape[axis] // N_DEV))
        return land.at[tuple(ix)]
    return land.at[pl.ds(0, k)]


ARRIVALS = {None: N_DEV - 1, "near": 4, "relay": 3}


def _routes(route, x, y, c):
    me, sibling = (x, y, c), (x, y, 1 - c)
    chips = [(1 - x, y), (x, 1 - y), (1 - x, 1 - y)]
    if route == "near":
        return [(me, sibling)] + [(me, (*chip, c)) for chip in chips]
    if route == "relay":
        return [((*chip, c), sibling) for chip in chips]
    return [(me, peer) for peer in _peers(x, y, c)]


def _place_own(mode, srcs, axes, name, after=None):
    n = len(srcs)
    extra, extra_specs = _after(after)

    def body(me_ref, *refs):
        for a in range(n):
            out = refs[n + len(extra) + a]
            out[...] = refs[a][...].reshape(out.shape)

    def at_me(shape, axis):
        return lambda i, me: tuple(me[0] if d == axis else 0 for d in range(len(shape)))

    in_specs, out_specs = [], []
    for s, axis in zip(srcs, axes):
        if mode == "gather":
            in_specs.append(pl.BlockSpec(s.shape, lambda i, me, nd=len(s.shape): (0,) * nd))
            out_specs.append(pl.BlockSpec(s.shape, at_me(s.shape, axis)))
        else:
            blk = s.shape[:axis] + (s.shape[axis] // N_DEV,) + s.shape[axis + 1:]
            in_specs.append(pl.BlockSpec(blk, at_me(blk, axis)))
            out_specs.append(pl.BlockSpec((1,) + blk, at_me((1,) + blk, 0)))
    me = _lin(_me()).astype(jnp.int32).reshape(1)
    return pl.pallas_call(
        body, name=name, out_shape=[_S(_land_shape(mode, s, a), s.dtype) for s, a in zip(srcs, axes)],
        grid_spec=pltpu.PrefetchScalarGridSpec(num_scalar_prefetch=1, grid=(1,), in_specs=in_specs + extra_specs,
                                               out_specs=out_specs),
        compiler_params=_params(1))(me, *srcs, *extra)


def _place_shards(shards, layers, axes, dtypes, name, after=None):
    n = len(shards)
    extra, extra_specs = _after(after)

    def body(me_ref, *refs):
        for a in range(n):
            out = refs[n + len(extra) + a]
            out[...] = refs[a][...].astype(out.dtype)

    in_specs, out_specs, out_shape = [], [], []
    for s, layer, axis, dt in zip(shards, layers, axes, dtypes):
        shape = s.shape if layer is None else s.shape[1:]
        nd = len(shape)
        if layer is None:
            in_specs.append(pl.BlockSpec(shape, lambda i, me, nd=nd: (0,) * nd))
        else:
            in_specs.append(pl.BlockSpec((None,) + shape, lambda i, me, nd=nd, layer=layer: (layer,) + (0,) * nd))
        out_specs.append(pl.BlockSpec(shape, lambda i, me, nd=nd, axis=axis: tuple(me[0] if d == axis else 0 for d in range(nd))))
        out_shape.append(_S(shape[:axis] + (N_DEV * shape[axis],) + shape[axis + 1:], dt))
    me = _lin(_me()).astype(jnp.int32).reshape(1)
    return pl.pallas_call(
        body, name=name, out_shape=out_shape,
        grid_spec=pltpu.PrefetchScalarGridSpec(num_scalar_prefetch=1, grid=(1,), in_specs=in_specs + extra_specs,
                                               out_specs=out_specs),
        compiler_params=_params(1))(me, *shards, *extra)


def _push_start(mode, srcs, lands, axes, name, route=None):
    n, ns = len(lands), len(srcs)

    def body(*refs):
        src_refs, land_refs = refs[:ns], refs[ns:ns + n]
        send_sems, recv_sems = refs[ns + n], refs[ns + n + 1]
        token = refs[-1]
        x, y, c = _me()
        for a in range(n):
            for block, peer in _routes(route, x, y, c):
                there = _dst_view(mode, land_refs[a], axes[a], _lin(block))
                pltpu.make_async_remote_copy(
                    src_ref=_src_view(mode, src_refs[a], axes[a], _lin(peer)) if ns else there, dst_ref=there,
                    send_sem=send_sems.at[a], recv_sem=recv_sems.at[a], device_id=peer, device_id_type=MESH).start()
        token[...] = jnp.zeros_like(token)

    hbm = lambda s: pltpu.HBM(s.shape, s.dtype)
    outs = pl.pallas_call(
        body, name=name,
        out_shape=(pltpu.SemaphoreType.DMA((n,)), pltpu.SemaphoreType.DMA((n,)), *[hbm(s) for s in srcs], *[hbm(s) for s in lands],
                   _S((SUB, LANE))),
        in_specs=[HBM_SPEC] * (ns + n),
        out_specs=(SEM_SPEC, SEM_SPEC, *[HBM_SPEC] * (ns + n), pl.BlockSpec(memory_space=pltpu.VMEM)),
        input_output_aliases={i: 2 + i for i in range(ns + n)},
        compiler_params=pltpu.CompilerParams(has_side_effects=EFFECT),
    )(*[pltpu.with_memory_space_constraint(s, pltpu.HBM) for s in list(srcs) + list(lands)])
    return outs[0], outs[1], outs[2:2 + ns], outs[2 + ns:2 + ns + n], outs[-1]


def _push_wait(mode, send_sems, recv_sems, srcs, lands, axes, after, name, first=0, route=None):
    n, ns = len(lands), len(srcs)
    after = list(after) if isinstance(after, (list, tuple)) else [after]

    def body(*refs):
        land_refs = refs[ns:ns + n]
        send_sems, recv_sems = refs[ns + n], refs[ns + n + 1]
        x, y, c = _me()
        for a in range(n):
            seven = _blocks(mode, land_refs[a], axes[a], ARRIVALS[route])
            cp = pltpu.make_async_remote_copy(src_ref=seven, dst_ref=seven, send_sem=send_sems.at[first + a],
                                              recv_sem=recv_sems.at[first + a],
                                              device_id=(x, y, 1 - c), device_id_type=MESH)
            cp.wait_send()
            cp.wait_recv()

    hbm = lambda s: pltpu.HBM(s.shape, s.dtype)
    outs = pl.pallas_call(
        body, name=name, out_shape=tuple(hbm(s) for s in list(srcs) + list(lands)),
        in_specs=[HBM_SPEC] * (ns + n) + [SEM_SPEC, SEM_SPEC] + [ANY] * len(after), out_specs=tuple([HBM_SPEC] * (ns + n)),
        input_output_aliases={i: i for i in range(ns + n)},
        compiler_params=pltpu.CompilerParams(has_side_effects=EFFECT),
    )(*srcs, *lands, send_sems, recv_sems, *after)
    return outs[ns:]


def _sum_parts(parts):
    n, R, C = parts.shape

    def body(p_ref, o_ref):
        g = p_ref[0]
        for k in range(1, n):
            g = g + p_ref[k]
        o_ref[...] = g

    return pl.pallas_call(body, name="sum_parts", out_shape=_S((R, C)))(parts)


SMALL =['conv_b', 'rg_wa', 'rg_ba', 'rg_wx', 'rg_bx', 'rg_lambda', 's5_a_re', 's5_a_im', 's5_b_re', 's5_b_im',
         's5_c_re', 's5_c_im', 's5_d', 's5_log_step', 's5_b_glu', 'ln1_g', 'ln1_b', 'ple_gate_b', 'ln2_g', 'ln2_b']
WEIGHTS = ['w_in', 'conv_w', 'conv_b', 'rg_wa', 'rg_ba', 'rg_wx', 'rg_bx', 'rg_lambda', 's5_a_re', 's5_a_im', 's5_b_re',
           's5_b_im', 's5_c_re', 's5_c_im', 's5_d', 's5_log_step', 's5_w_glu', 's5_b_glu', 'w_out', 'ln1_g', 'ln1_b',
           'ple_w', 'ple_gate_w', 'ple_gate_b', 'ln2_g', 'ln2_b']
PACK_ROWS_MULT = 64


STORED = {'s5_b_re': (2, 3), 's5_b_im': (2, 3), 's5_d': (1, 2)}


def _stored(k, a):
    return jnp.swapaxes(a, *STORED[k]) if k in STORED else a


def _pack(tree, scalar):
    flat = jnp.concatenate([tree[k].reshape(-1) for k in SMALL] + [scalar.reshape(1)])
    rows = -(-flat.shape[0] // (LANE * PACK_ROWS_MULT)) * PACK_ROWS_MULT
    return jnp.pad(flat, (0, rows * LANE - flat.shape[0])).reshape(rows, LANE)


def _unpack(packed, like):
    flat, out, o = packed.reshape(-1), {}, 0
    for k in SMALL:
        n = math.prod(like[k].shape)
        out[k] = flat[o:o + n].reshape(like[k].shape)
        o += n
    return out, flat[o]


class _NoHooks:
    token = None
    first_token = None

    def first_weights(self, full, after):
        return full

    def layer_start(self, i, W, after):
        return W

    def late_weights(self, i, W, after):
        return W

    def post_done(self, i, g):
        return None

    def smalls_done(self, grads, loss):
        self.small = _small_grads(grads, self.res)
        return None

    def w_in_done(self, i, g):
        return None

    def layer_done(self, i, g, dx):
        return None


def _local_grads(x, p, target, W, disc, hooks):
    depth = 2
    saved = []
    for i in range(depth):
        if i > 0:
            W = hooks.layer_start(i, W, x)
        w = W[i]
        z = _inproj_fwd(x, w['w_in'], hooks.token if i == 0 else None)
        hs, *gates = _rg_fwd(z, w['conv_w'], w['conv_b'], w['wa_bd'], w['wx_bd'], w['rg_ba'], w['rg_bx'], w['rg_lambda'], i)
        d = disc[i]
        y0, s_re, s_im = _s5_fwd(z, d['bb_re'], d['bb_im'], d['lb_re'], d['lb_im'], d['c_re'], d['c_im'], w['s5_d'], i)
        W = hooks.late_weights(i, W, y0)
        w = W[i]
        x2, *norms = _post_fwd(x, hs, z, y0, p, w['s5_w_glu'], w['s5_b_glu'], w['w_out'], w['ln1_g'], w['ln1_b'],
                               w['ple_w'], w['ple_gate_w'], w['ple_gate_b'], w['ln2_g'], w['ln2_b'], i)
        saved.append((x, z, hs, gates, y0, s_re, s_im, norms))
        x = x2

    grads = [None] * depth
    dx = target
    loss = None
    token = None
    for i in reversed(range(depth)):
        w, d = W[i], disc[i]
        xin, z, hs, gates, y0, s_re, s_im, (xh1, xh2, q, gt, rstd1, rstd2) = saved[i]
        g = {}
        (dt1, g['ple_w'], g['ple_gate_w'], g['ple_gate_b'], g['ln1_g'], g['ln1_b'], g['ln2_g'], g['ln2_b'], lrow) = _post_bwd_a(
            dx, i == depth - 1, xh2, xh1, rstd2, rstd1, q, gt, p, w['ple_gate_w'], w['ln1_g'], w['ln1_b'],
            w['ln2_g'], w['ln2_b'], i, token)
        if i == depth - 1:
            loss = 0.5 / D_MODEL * jnp.sum(lrow)
        dhs, dy0, dzg, g['w_out'], g['s5_w_glu'], g['s5_b_glu'] = _post_bwd_b(dt1, z, hs, y0, w['w_out'], w['s5_w_glu'],
                                                                           w['s5_b_glu'], i)
        (dzu, g['bb_re'], g['bb_im'], g['lb_re'], g['lb_im'], g['c_re'], g['c_im'], g['s5_d']) = _s5_bwd(
            dy0, z, s_re, s_im, d['bb_re'], d['bb_im'], d['lb_re'], d['lb_im'], d['c_re'], d['c_im'], w['s5_d'], i,
            hooks.post_done(i, g))
        (dzx, g['conv_w'], g['conv_b'], g['wa_bd'], g['wx_bd'], g['rg_ba'], g['rg_bx'], g['rg_lambda']) = _rg_bwd(
            dhs, z, hs, gates, w['conv_w'], w['wa_bd'], w['wx_bd'], w['rg_lambda'], i)
        if i == 0:
            g['w_in'] = _inproj_bwd_dw(xin, dzx, dzg, dzu, hooks.smalls_done([g, grads[1]], loss))
            dx = _inproj_bwd_dx(dt1, dzx, dzg, dzu, w['w_in'], hooks.w_in_done(i, g))
        else:
            dx, g['w_in'] = _inproj_bwd(dt1, xin, dzx, dzg, dzu, w['w_in'])
        grads[i] = g
        token = hooks.layer_done(i, g, dx)
    return loss, dx, grads


def _s5_layouts_fwd(s5_a_re, s5_a_im, s5_log_step, s5_b_re, s5_b_im, s5_c_re, s5_c_im, token=None):
    depth = s5_a_re.shape[0]
    ar, ai = s5_a_re.reshape(depth * 24, S5_P), s5_a_im.reshape(depth * 24, S5_P)
    ls = s5_log_step.reshape(depth * 24, 1)
    lr, li, cr, ci = _s5_disc_fwd(ar, ai, ls, token)
    per_group = lambda a: a.reshape(depth * 24, 1, S5_P)
    as_c = lambda b: jnp.swapaxes(b, 2, 3).reshape(depth * 24, S5_H, S5_P)
    res = (ar, ai, ls, per_group(cr), per_group(ci), as_c(s5_b_re), as_c(s5_b_im))
    bbr, bbi = _s5_bscale_fwd(*res[3:])
    tiles = lambda a: a.reshape(depth * N_S5_T, S5_GT, S5_H, S5_P)
    rows = lambda a: a.reshape(depth * N_S5_T, S5_GT, S5_P)
    disc = dict(bb_re=tiles(bbr), bb_im=tiles(bbi), lb_re=rows(lr), lb_im=rows(li), c_re=tiles(s5_c_re), c_im=tiles(s5_c_im))
    return [disc] * depth, res


def _s5_layouts_bwd(grads, res):
    ar, ai, ls, cr, ci, br, bi = res
    depth = len(grads)
    stack = lambda k, shape: jnp.stack([g[k] for g in grads]).reshape(shape)
    groups, shape_c = (depth * 24, S5_H, S5_P), (depth, 24, S5_H, S5_P)
    dbr, dbi, dcr, dci = _s5_bscale_bwd(cr, ci, br, bi, stack('bb_re', groups), stack('bb_im', groups))
    gp = (depth * 24, S5_P)
    dar, dai, dls = _s5_disc_bwd(ar, ai, ls, stack('lb_re', gp), stack('lb_im', gp), dcr.reshape(gp), dci.reshape(gp))
    return dict(
        s5_a_re=dar.reshape(depth, 24, S5_P), s5_a_im=dai.reshape(depth, 24, S5_P), s5_log_step=dls.reshape(depth, 24),
        s5_b_re=dbr.reshape(shape_c), s5_b_im=dbi.reshape(shape_c),
        s5_c_re=stack('c_re', shape_c), s5_c_im=stack('c_im', shape_c))


LATE = ('w_out', 'ple_w', 'ple_gate_w', 's5_w_glu')


ROWS = ('conv_b', 'rg_ba', 'rg_bx', 'rg_lambda', 's5_d', 's5_b_glu', 'ln1_g', 'ln1_b', 'ple_gate_b', 'ln2_g', 'ln2_b')


def _shared_weights(full):
    depth = full['conv_b'].shape[0]
    shared = {k: full[k].reshape(depth, 1, -1) for k in ROWS}
    shared.update(conv_w=full['conv_w'], wa_bd=full['rg_wa'], wx_bd=full['rg_wx'])
    return shared


def _layer_weights(full, shared, i):
    return dict(shared, w_in=full['w_in'][i])


class _AllLocal(_NoHooks):
    def __init__(self, full):
        self.full = full

    def late_weights(self, i, W, after):
        W[i].update({k: self.full[k][i] for k in LATE})
        return W


def _full_grads(full, x, p, target, hooks=None):
    hooks = hooks or _AllLocal(full)
    disc, res = _s5_layouts_fwd(full['s5_a_re'], full['s5_a_im'], full['s5_log_step'], full['s5_b_re'], full['s5_b_im'],
                                full['s5_c_re'], full['s5_c_im'], hooks.first_token)
    full = hooks.first_weights(full, disc[-1]['bb_im'])
    shared = _shared_weights(full)
    W = [_layer_weights(full, shared, i) for i in range(2)]
    hooks.res = res
    loss, gx, grads = _local_grads(x, p, target, W, disc, hooks)
    out = dict(hooks.small)
    for k in SHARD_AXIS:
        out[k] = [g[k] for g in grads]
    return loss, gx, out


def _small_grads(grads, res):
    stack = lambda f: jnp.stack([f(g) for g in grads])
    out = _s5_layouts_bwd(grads, res)
    out['conv_w'] = stack(lambda g: g['conv_w'])
    for k in ('conv_b', 'rg_ba', 'rg_bx', 'rg_lambda', 's5_b_glu', 'ln1_g', 'ln1_b', 'ple_gate_b', 'ln2_g', 'ln2_b'):
        out[k] = stack(lambda g: g[k][0])
    out['s5_d'] = _stored('s5_d', stack(lambda g: g['s5_d'][0]).reshape(2, 24, 16))
    out['rg_wa'] = stack(lambda g: g['wa_bd'])
    out['rg_wx'] = stack(lambda g: g['wx_bd'])
    return out


SHARD_AXIS = {'w_in': 2, 'w_out': 1, 'ple_w': 2, 'ple_gate_w': 1, 's5_w_glu': 1}


def kernel(x, p, w_in, conv_w, conv_b, rg_wa, rg_ba, rg_wx, rg_bx, rg_lambda, s5_a_re, s5_a_im, s5_b_re, s5_b_im, s5_c_re, s5_c_im, s5_d, s5_log_step, s5_w_glu, s5_b_glu, w_out, ln1_g, ln1_b, ple_w, ple_gate_w, ple_gate_b, ln2_g, ln2_b, loss_target, m_w_in, m_conv_w, m_conv_b, m_rg_wa, m_rg_ba, m_rg_wx, m_rg_bx, m_rg_lambda, m_s5_a_re, m_s5_a_im, m_s5_b_re, m_s5_b_im, m_s5_c_re, m_s5_c_im, m_s5_d, m_s5_log_step, m_s5_w_glu, m_s5_b_glu, m_w_out, m_ln1_g, m_ln1_b, m_ple_w, m_ple_gate_w, m_ple_gate_b, m_ln2_g, m_ln2_b, v_w_in, v_conv_w, v_conv_b, v_rg_wa, v_rg_ba, v_rg_wx, v_rg_bx, v_rg_lambda, v_s5_a_re, v_s5_a_im, v_s5_b_re, v_s5_b_im, v_s5_c_re, v_s5_c_im, v_s5_d, v_s5_log_step, v_s5_w_glu, v_s5_b_glu, v_w_out, v_ln1_g, v_ln1_b, v_ple_w, v_ple_gate_w, v_ple_gate_b, v_ln2_g, v_ln2_b):
    local = dict(locals())
    w = {k: local[k] for k in WEIGHTS}
    mom = {k: local['m_' + k] for k in WEIGHTS}
    var = {k: local['v_' + k] for k in WEIGHTS}

    big = list(SHARD_AXIS)
    late_axes = [SHARD_AXIS[k] - 1 for k in LATE]
    pushed = {}

    groups = dict(first=(['w_in', 'conv_w'], [0, None], [1, 0]), l0=(list(LATE), [0] * len(LATE), late_axes),
                  l1=(['w_in'] + list(LATE), [1] * (1 + len(LATE)), [1] + late_axes))
    token = None
    for key, members in (("first", ["first"]), ("rest", ["l0", "l1"])):
        names, layers, axes = (sum((groups[m][j] for m in members), []) for j in range(3))
        shards = [w[k] if layer is not None else w[k][None] for k, layer in zip(names, layers)]
        lands = _place_shards(shards, layers, axes, [WIRE if k in big else w[k].dtype for k in names],
                              "place_weights_" + key, token)
        pushed[key] = _push_start("gather", [], lands, axes, "push_weights_" + key, "near" if key == "first" else None)
        token = pushed[key][4]

    def await_weights(key, axes, after):
        s, first = (pushed["first"], 0) if key == "first" else (pushed["rest"], 0 if key == "l0" else len(LATE))
        return _push_wait("gather", s[0], s[1], [], s[3][first:first + len(axes)], axes, after, "await_weights_" + key, first)

    def push_grads(key, g, names, axes):
        srcs = [g[k] for k in names]
        pushed[key] = _push_start("scatter", srcs, _place_own("scatter", srcs, axes, "place_grads_" + key), axes,
                                  "push_grads_" + key)
        return pushed[key][4]

    def await_grads(key, axes, after):
        s = pushed[key]
        return _push_wait("scatter", s[0], s[1], s[2], s[3], axes, after, "await_grads_" + key)

    class Overlap(_NoHooks):
        token = pushed["rest"][4]
        first_token = token

        def first_weights(self, full, after):
            s, axes = pushed["first"], [1, 0]
            near = _push_wait("gather", s[0], s[1], [], s[3], axes, after, "await_weights_near", route="near")
            s = _push_start("gather", [], near, axes, "relay_weights", "relay")
            w_in0, conv = _push_wait("gather", s[0], s[1], [], s[3], axes, s[4], "await_weights_relay", route="relay")
            return dict(full, w_in=[w_in0, None], conv_w=jnp.moveaxis(conv, 0, 2).reshape(2, 4, RG_W))

        def late_weights(self, i, W, after):
            if i == 0:
                W[0].update(zip(LATE, await_weights("l0", late_axes, after)))
            return W

        def layer_start(self, i, W, after):
            lands = await_weights("l1", [1] + late_axes, after)
            W[1].update(zip(LATE, lands[1:]), w_in=lands[0])
            return W

        def post_done(self, i, g):
            return push_grads("late0", g, LATE, late_axes) if i == 0 else None

        def smalls_done(self, grads, loss):
            super().smalls_done(grads, loss)
            conv = jnp.moveaxis(self.small['conv_w'].reshape(2, 4, N_DEV, RG_W // N_DEV), 2, 0)
            self.packed = _pack(self.small, loss)
            return push_grads("small", dict(conv_w=conv.reshape(N_DEV, 8, RG_W // N_DEV), small=self.packed),
                              ['conv_w', 'small'], [0, 0])

        def w_in_done(self, i, g):
            return push_grads("w_in0", g, ['w_in'], [0])

        def layer_done(self, i, g, dx):
            return push_grads("all1", g, ['w_in'] + list(LATE), [0] + late_axes) if i == 1 else None

    hooks = Overlap()
    _, grad_x, g = _full_grads(dict(w), x[0], p, loss_target[0], hooks)

    recv1 = dict(zip(['w_in'] + list(LATE), await_grads("all1", [0] + late_axes, grad_x)))
    recv0 = dict(zip(LATE, await_grads("late0", late_axes, grad_x)))
    outs = {}

    def update(k, parts):
        shard = w[k].shape
        c = shard[-1]
        two = lambda a: a.reshape(-1, c)
        res = _adamw([r.reshape(N_DEV, -1, c) for r in parts], two(w[k]), two(mom[k]), two(var[k]))
        outs[k] = [o.reshape(shard) for o in res]

    for k in LATE:
        update(k, [recv0[k], recv1[k]])
    done = [outs[k][1] for k in LATE]
    conv_parts, small_parts = await_grads("small", [0, 0], done)

    rows = hooks.packed.shape[0] // N_DEV
    mine = _sum_parts(small_parts.reshape(N_DEV, rows, LANE))
    sums = _push_start("gather", [mine], _place_own("gather", [mine], [0], "place_small_sums"), [0], "push_small_sums")
    w_in0, = await_grads("w_in0", [0], sums[4])
    update('w_in', [w_in0, recv1['w_in']])
    update('conv_w', [conv_parts])
    gathered, = _push_wait("gather", sums[0], sums[1], sums[2], sums[3], [0], [outs['w_in'][1], outs['conv_w'][1]],
                           "await_small_sums")
    stored = [{k: _stored(k, t[k]) for k in SMALL} for t in (w, mom, var)]
    summed, loss = _unpack(gathered, stored[0])
    wide = ['s5_b_re', 's5_b_im']
    for names, name in ((wide, "adamw_s5_b"), ([k for k in SMALL if k not in wide], "adamw_small")):
        delta, new_m, new_v = _adamw_natural(names, summed, *stored, name)
        for k in names:
            outs[k] = [_stored(k, o[k]) for o in (summed, delta, new_m, new_v)]

    res = [loss, grad_x[None]]
    for j in range(4):
        res += [outs[k][j] for k in WEIGHTS]
    return tuple(res)
```

```python
import math

import jax
import jax.numpy as jnp
from jax import lax
from jax.experimental import pallas as pl
from jax.experimental.pallas import tpu as pltpu

F32 = jnp.float32
MXU = jnp.bfloat16
WIRE = jnp.bfloat16

N_DEV = 8
D_MODEL = 1024
PLE_D = 256
RG_W = 640
S5_W = 384
S5_P = 64
S5_N = 24 * S5_P
Z_W = 2 * RG_W + 2 * S5_W
C_RGG = RG_W
C_S5U = 2 * RG_W
C_S5G = 2 * RG_W + S5_W
LANE = 128
N_RG_T = RG_W // LANE
N_S5_T = S5_W // LANE
W_BLK = Z_W // N_DEV
ALPHA = (2.0 * 2) ** 0.25
LN_EPS = 1e-5
RG_C = 8.0
LR, B1, B2, EPS, WD, STEP = 0.001, 0.9, 0.999, 1e-08, 0.01, 10
BC1 = 1.0 - B1 ** STEP
BC2 = 1.0 - B2 ** STEP
RC = 512
RC_RG = 1024
TM = 512
TM_MM = 1024
VMEM_LIMIT = 56 * 1024 * 1024

MESH = pl.DeviceIdType.MESH
ANY = pl.BlockSpec(memory_space=pl.ANY)


def _params(n_grid_axes, vmem=VMEM_LIMIT):
    return pltpu.CompilerParams(dimension_semantics=("arbitrary",) * n_grid_axes, vmem_limit_bytes=vmem)


def _S(shape, dtype=F32):
    return jax.ShapeDtypeStruct(tuple(shape), dtype)


def _sigmoid(x):
    return 0.5 * jnp.tanh(0.5 * x) + 0.5


def _silu_and_grad(x):
    s = _sigmoid(x)
    return x * s, s * (1.0 + x * (1.0 - s))


_GELU_C = math.sqrt(2.0 / math.pi)


def _gelu(x):
    return 0.5 * x * (1.0 + jnp.tanh(_GELU_C * (x + 0.044715 * (x * x * x))))


def _gelu_grad(x):
    th = jnp.tanh(_GELU_C * (x + 0.044715 * (x * x * x)))
    return 0.5 * (1.0 + th) + 0.5 * x * (1.0 - th * th) * (_GELU_C * (1.0 + 3.0 * 0.044715 * (x * x)))


def _mm(a, b):
    return jnp.dot(a.astype(MXU), b.astype(MXU), preferred_element_type=F32)


def _mm_nt(a, b):
    return lax.dot_general(a.astype(MXU), b.astype(MXU), (((1,), (1,)), ((), ())), preferred_element_type=F32)


def _mm_tn(a, b):
    return lax.dot_general(a.astype(MXU), b.astype(MXU), (((0,), (0,)), ((), ())), preferred_element_type=F32)


def _ln_fwd(t, g, b):
    mu = jnp.mean(t, axis=-1, keepdims=True)
    tc = t - mu
    var = jnp.mean(tc * tc, axis=-1, keepdims=True)
    rstd = lax.rsqrt(var + LN_EPS)
    xhat = tc * rstd
    return xhat * g + b, xhat, rstd


def _ln_bwd(dy, xhat, rstd, g):
    dxh = dy * g
    m1 = jnp.mean(dxh, axis=-1, keepdims=True)
    m2 = jnp.mean(dxh * xhat, axis=-1, keepdims=True)
    return rstd * (dxh - m1 - xhat * m2)


def _colsum(a):
    return jnp.sum(a, axis=0, keepdims=True)


def _up(x, d, rows, fill):
    n = x.shape[0]
    return jnp.where(rows < n - d, pltpu.roll(x, n - d, 0), fill)


SUB = 8
TILE_STEPS = (1, 2, 4)


def _r8(width):
    return lax.broadcasted_iota(jnp.int32, (SUB, width), 0)


def _scan_real(a, u, carry, reverse=False):
    r8 = _r8(a.shape[1])
    n = a.shape[0] // SUB
    outs = [None] * n
    for k in (reversed(range(n)) if reverse else range(n)):
        A, U = a[SUB * k:SUB * k + SUB], u[SUB * k:SUB * k + SUB]
        for d in TILE_STEPS:
            m = (r8 < SUB - d) if reverse else (r8 >= d)
            sh = SUB - d if reverse else d
            U = A * jnp.where(m, pltpu.roll(U, sh, 0), 0.0) + U
            A = A * jnp.where(m, pltpu.roll(A, sh, 0), 1.0)
        h = A * carry + U
        outs[k] = h
        carry = h[0:1] if reverse else h[SUB - 1:SUB]
    return jnp.concatenate(outs, axis=0), carry


def _tile_powers(lr, li, reverse=False):
    width = lr.shape[1]
    r8 = _r8(width)
    steps = []
    pr, pi = lr, li
    er, ei = jnp.broadcast_to(lr, (SUB, width)), jnp.broadcast_to(li, (SUB, width))
    for d in TILE_STEPS:
        m = (r8 < SUB - d) if reverse else (r8 >= d)
        sh = SUB - d if reverse else d
        steps.append((sh, jnp.where(m, pr, 0.0), jnp.where(m, pi, 0.0)))
        er, ei = _cmul(er, ei, jnp.where(m, pltpu.roll(er, sh, 0), 1.0), jnp.where(m, pltpu.roll(ei, sh, 0), 0.0))
        pr, pi = _cmul(pr, pi, pr, pi)
    return steps, (er, ei)


def _scan_lti(xr, xi, carry, steps, e, reverse=False):
    er, ei = e
    kr, ki = carry
    n = xr.shape[0] // SUB
    outr, outi = [None] * n, [None] * n
    for k in (reversed(range(n)) if reverse else range(n)):
        sr, si = xr[SUB * k:SUB * k + SUB], xi[SUB * k:SUB * k + SUB]
        for sh, pr, pi in steps:
            shr, shi = pltpu.roll(sr, sh, 0), pltpu.roll(si, sh, 0)
            sr, si = sr + (pr * shr - pi * shi), si + (pr * shi + pi * shr)
        sr = sr + (er * kr - ei * ki)
        si = si + (er * ki + ei * kr)
        outr[k], outi[k] = sr, si
        kr, ki = (sr[0:1], si[0:1]) if reverse else (sr[SUB - 1:SUB], si[SUB - 1:SUB])
    return jnp.concatenate(outr, axis=0), jnp.concatenate(outi, axis=0), (kr, ki)


def _halo(ref, c, r0):
    rp = pl.multiple_of(jnp.maximum(r0 - 8, 0), 8)
    return jnp.where(c > 0, ref[pl.ds(rp, 8), :], 0.0)


def _conv_taps(xe):
    return [pltpu.roll(xe, 3, 0)[8:, :], pltpu.roll(xe, 2, 0)[8:, :], pltpu.roll(xe, 1, 0)[8:, :], xe[8:, :]]


def _rg_gates(h, wa, wx, ba, bx, sp):
    r = _sigmoid(_mm(h, wa) + ba)
    i = _sigmoid(_mm(h, wx) + bx)
    log_a = (-RG_C) * r * sp
    a = jnp.exp(log_a)
    mult = jnp.sqrt(-jnp.tanh(log_a) * (a * a + 1.0))
    return r, i, a, mult


def _softplus(y):
    return jnp.maximum(y, 0.0) + jnp.log1p(jnp.exp(-jnp.abs(y)))


def _after(token):
    return ([], []) if token is None else ([token], [ANY])


def _inproj_fwd(x, w_in, token=None):
    L = x.shape[0]

    def body(x_ref, w_ref, *rest):
        rest[-1][...] = _mm(x_ref[...], w_ref[...])

    extra, extra_specs = _after(token)
    tm = min(TM_MM, L)
    return pl.pallas_call(
        body, name="inproj_fwd", grid=(L // tm,),
        in_specs=[pl.BlockSpec((tm, D_MODEL), lambda i: (i, 0)), pl.BlockSpec((D_MODEL, Z_W), lambda i: (0, 0))] + extra_specs,
        out_specs=pl.BlockSpec((tm, Z_W), lambda i: (i, 0)),
        out_shape=_S((L, Z_W)), compiler_params=_params(1))(x, w_in, *extra)


def _inproj_bwd(dt1, x, dzx, dzg, dzu, w_in):
    L = x.shape[0]

    def body(dt1_ref, x_ref, dzx_ref, dzg_ref, dzu_ref, w_ref, dx_ref, dw_ref, acc_ref):
        @pl.when(pl.program_id(0) == 0)
        def _():
            acc_ref[...] = jnp.zeros_like(acc_ref)
        dzg = dzg_ref[...]
        dz = jnp.concatenate([dzx_ref[...], dzg[:, :RG_W], dzu_ref[...], dzg[:, RG_W:]], axis=1).astype(MXU)
        xb = x_ref[...].astype(MXU)
        dx_ref[...] = ALPHA * dt1_ref[...] + _mm_nt(dz, w_ref[...])
        for j in range(N_DEV):
            acc_ref[j] += _mm_tn(xb, dz[:, j * W_BLK:(j + 1) * W_BLK])

        @pl.when(pl.program_id(0) == L // TM - 1)
        def _():
            dw_ref[...] = acc_ref[...].astype(WIRE)

    row = lambda w: pl.BlockSpec((TM, w), lambda i: (i, 0))
    wspec = pl.BlockSpec((N_DEV, D_MODEL, W_BLK), lambda i: (0, 0, 0))
    return pl.pallas_call(
        body, name="inproj_bwd", grid=(L // TM,),
        in_specs=[row(D_MODEL), row(D_MODEL), row(RG_W), row(D_MODEL), row(S5_W),
                  pl.BlockSpec((D_MODEL, Z_W), lambda i: (0, 0))],
        out_specs=[row(D_MODEL), wspec],
        out_shape=[_S((L, D_MODEL)), _S((N_DEV, D_MODEL, W_BLK), WIRE)],
        scratch_shapes=[pltpu.VMEM((N_DEV, D_MODEL, W_BLK), F32)],
        compiler_params=_params(1))(dt1, x, dzx, dzg, dzu, w_in)


TM2 = 512


def _dz_block(dzx_ref, dzg_ref, dzu_ref):
    dzg = dzg_ref[...]
    return jnp.concatenate([dzx_ref[...], dzg[:, :RG_W], dzu_ref[...], dzg[:, RG_W:]], axis=1).astype(MXU)


def _inproj_bwd_dw(x, dzx, dzg, dzu, token=None):
    L = x.shape[0]
    extra, extra_specs = _after(token)

    def body(x_ref, dzx_ref, dzg_ref, dzu_ref, *rest):
        dw_ref, acc_ref = rest[len(extra):]
        @pl.when(pl.program_id(0) == 0)
        def _():
            acc_ref[...] = jnp.zeros_like(acc_ref)
        dz = _dz_block(dzx_ref, dzg_ref, dzu_ref)
        xb = x_ref[...].astype(MXU)
        for j in range(N_DEV):
            acc_ref[j] += _mm_tn(xb, dz[:, j * W_BLK:(j + 1) * W_BLK])

        @pl.when(pl.program_id(0) == L // TM2 - 1)
        def _():
            dw_ref[...] = acc_ref[...].astype(WIRE)

    row = lambda w: pl.BlockSpec((TM2, w), lambda i: (i, 0))
    wspec = pl.BlockSpec((N_DEV, D_MODEL, W_BLK), lambda i: (0, 0, 0))
    return pl.pallas_call(
        body, name="inproj_bwd_dw", grid=(L // TM2,),
        in_specs=[row(D_MODEL), row(RG_W), row(D_MODEL), row(S5_W)] + extra_specs, out_specs=wspec,
        out_shape=_S((N_DEV, D_MODEL, W_BLK), WIRE), scratch_shapes=[pltpu.VMEM((N_DEV, D_MODEL, W_BLK), F32)],
        compiler_params=_params(1))(x, dzx, dzg, dzu, *extra)


def _inproj_bwd_dx(dt1, dzx, dzg, dzu, w_in, token=None):
    L = dt1.shape[0]
    extra, extra_specs = _after(token)

    def body(dt1_ref, dzx_ref, dzg_ref, dzu_ref, w_ref, *rest):
        rest[-1][...] = ALPHA * dt1_ref[...] + _mm_nt(_dz_block(dzx_ref, dzg_ref, dzu_ref), w_ref[...])

    tm = min(TM_MM, L)
    row = lambda w: pl.BlockSpec((tm, w), lambda i: (i, 0))
    return pl.pallas_call(
        body, name="inproj_bwd_dx", grid=(L // tm,),
        in_specs=[row(D_MODEL), row(RG_W), row(D_MODEL), row(S5_W), _full((D_MODEL, Z_W))] + extra_specs,
        out_specs=row(D_MODEL), out_shape=_S((L, D_MODEL)), compiler_params=_params(1))(dt1, dzx, dzg, dzu, w_in, *extra)


def _rg_specs(layer):
    tile = lambda rows: pl.BlockSpec((rows, LANE), lambda c: (0, c))
    ptile = lambda rows: pl.BlockSpec((None, rows, LANE), lambda c: (layer, 0, c))
    pheads = pl.BlockSpec((None, 2, RG_HD, RG_HD), lambda c: (layer, c, 0, 0))
    return tile, ptile, pheads, pl.BlockSpec((2, RG_HD, RG_HD), lambda c: (c, 0, 0))


RG_HD = 64


def _bd2(w):
    z = jnp.zeros((RG_HD, RG_HD), w.dtype)
    return jnp.concatenate([jnp.concatenate([w[0], z], axis=1), jnp.concatenate([z, w[1]], axis=1)], axis=0)


def _bd2_diag(m):
    return jnp.stack([m[:RG_HD, :RG_HD], m[RG_HD:, RG_HD:]])


def _rg_fwd(z, cw, cb, wa_bd, wx_bd, ba, bx, lam, layer):
    L = z.shape[0]
    RC = min(RC_RG, L)

    def body(x_ref, cw_ref, cb_ref, wa_ref, wx_ref, ba_ref, bx_ref, lam_ref, hs_ref, *saved):
        w, b = cw_ref[...], cb_ref[...]
        wa, wx, ba_, bx_ = _bd2(wa_ref[...]).astype(MXU), _bd2(wx_ref[...]).astype(MXU), ba_ref[...], bx_ref[...]
        sp = _softplus(-lam_ref[...])

        def step(c, carry):
            r0 = pl.multiple_of(c * RC, RC)
            xe = jnp.concatenate([_halo(x_ref, c, r0), x_ref[pl.ds(r0, RC), :]], axis=0)
            t = _conv_taps(xe)
            h = t[0] * w[0:1] + t[1] * w[1:2] + t[2] * w[2:3] + t[3] * w[3:4] + b
            r, i, a, mult = _rg_gates(h, wa, wx, ba_, bx_, sp)
            hs, carry = _scan_real(a, mult * (i * h), carry)
            hs_ref[pl.ds(r0, RC), :] = hs
            for ref, val in zip(saved, (h, r, i, a, mult)):
                ref[pl.ds(r0, RC), :] = val
            return carry

        lax.fori_loop(0, L // RC, step, jnp.zeros((1, LANE), F32))

    tile, ptile, pheads, _ = _rg_specs(layer)
    return pl.pallas_call(
        body, name="rg_fwd", grid=(N_RG_T,),
        in_specs=[tile(L), ptile(4), ptile(1), pheads, pheads, ptile(1), ptile(1), ptile(1)],
        out_specs=[tile(L)] * 6, out_shape=[_S((L, RG_W))] * 6, compiler_params=_params(1))(
            z, cw, cb, wa_bd, wx_bd, ba, bx, lam)


def _rg_bwd(dhs, z, hs, gates, cw, wa_bd, wx_bd, lam, layer):
    L = z.shape[0]
    RC = min(RC_RG, L)

    def body(g_ref, x_ref, hs_ref, h_ref, r_ref, i_ref, a_ref, mult_ref, cw_ref, wa_ref, wx_ref, lam_ref,
             dx_ref, dcw_ref, dcb_ref, dwa_out, dwx_out, dba_ref, dbx_ref, dlam_ref, dwa_ref, dwx_ref):
        w = cw_ref[...]
        wa, wx = _bd2(wa_ref[...]).astype(MXU), _bd2(wx_ref[...]).astype(MXU)
        lam = lam_ref[...]
        sp = _softplus(-lam)
        rows = lax.broadcasted_iota(jnp.int32, (RC, LANE), 0)
        for ref in (dcw_ref, dcb_ref, dwa_ref, dwx_ref, dba_ref, dbx_ref, dlam_ref):
            ref[...] = jnp.zeros_like(ref)
        nch = L // RC

        def step(k, carry):
            cin, nxt = carry
            c = nch - 1 - k
            r0 = pl.multiple_of(c * RC, RC)
            xe = jnp.concatenate([_halo(x_ref, c, r0), x_ref[pl.ds(r0, RC), :]], axis=0)
            t = _conv_taps(xe)
            h, r, i, a, mult = (ref[pl.ds(r0, RC), :] for ref in (h_ref, r_ref, i_ref, a_ref, mult_ref))
            hs_e = jnp.concatenate([_halo(hs_ref, c, r0), hs_ref[pl.ds(r0, RC), :]], axis=0)
            hs_prev = pltpu.roll(hs_e, 1, 0)[8:, :]
            g = g_ref[pl.ds(r0, RC), :]
            cc, cin_new = _scan_real(a, a * g, cin, reverse=True)
            dh = g + _up(cc, 1, rows, cin)
            ih = i * h
            dlog_a = dh * hs_prev * a - (dh * ih) * (a * a) / mult
            di = dh * mult * h
            dhin = dh * mult * i
            dr = dlog_a * ((-RG_C) * sp)
            dlam_ref[...] += _colsum(dlog_a * r)
            dra = dr * r * (1.0 - r)
            dia = di * i * (1.0 - i)
            dwa_ref[...] += _mm_tn(h, dra)
            dwx_ref[...] += _mm_tn(h, dia)
            dba_ref[...] += _colsum(dra)
            dbx_ref[...] += _colsum(dia)
            dhin = dhin + _mm_nt(dra, wa) + _mm_nt(dia, wx)
            de = jnp.concatenate([dhin, nxt], axis=0)
            n = RC + 8
            dx = (dhin * w[3:4] + pltpu.roll(de, n - 1, 0)[:RC, :] * w[2:3]
                  + pltpu.roll(de, n - 2, 0)[:RC, :] * w[1:2] + pltpu.roll(de, n - 3, 0)[:RC, :] * w[0:1])
            dx_ref[pl.ds(r0, RC), :] = dx
            for kk in range(4):
                dcw_ref[kk:kk + 1, :] += _colsum(dhin * t[kk])
            dcb_ref[...] += _colsum(dhin)
            return cin_new, dhin[0:8, :]

        lax.fori_loop(0, nch, step, (jnp.zeros((1, LANE), F32), jnp.zeros((8, LANE), F32)))
        dlam_ref[...] = dlam_ref[...] * (RG_C * _sigmoid(-lam))
        dwa_out[...], dwx_out[...] = _bd2_diag(dwa_ref[...]), _bd2_diag(dwx_ref[...])

    tile, ptile, pheads, gheads = _rg_specs(layer)
    heads = _S((2 * N_RG_T, RG_HD, RG_HD))
    return pl.pallas_call(
        body, name="rg_bwd", grid=(N_RG_T,),
        in_specs=[tile(L)] * 8 + [ptile(4), pheads, pheads, ptile(1)],
        out_specs=[tile(L), tile(4), tile(1), gheads, gheads, tile(1), tile(1), tile(1)],
        out_shape=[_S((L, RG_W)), _S((4, RG_W)), _S((1, RG_W)), heads, heads, _S((1, RG_W)), _S((1, RG_W)), _S((1, RG_W))],
        scratch_shapes=[pltpu.VMEM((LANE, LANE), F32), pltpu.VMEM((LANE, LANE), F32)],
        compiler_params=_params(1))(dhs, z, hs, *gates, cw, wa_bd, wx_bd, lam)


def _cmul(ar, ai, br, bi):
    return ar * br - ai * bi, ar * bi + ai * br


S5_TW = S5_N // N_S5_T


S5_H = 16
S5_GT = LANE // S5_H


def _s5_specs(L, layer):
    in_tile = pl.BlockSpec((L, LANE), lambda t: (0, t))
    st = pl.BlockSpec((L, S5_TW), lambda t: (0, t))
    pg = pl.BlockSpec((None, S5_GT, S5_H, S5_P), lambda t: (layer * N_S5_T + t, 0, 0, 0))
    plb = pl.BlockSpec((None, S5_GT, S5_P), lambda t: (layer * N_S5_T + t, 0, 0))
    gg = pl.BlockSpec((None, S5_GT, S5_H, S5_P), lambda t: (t, 0, 0, 0))
    glb = pl.BlockSpec((None, S5_GT, S5_P), lambda t: (t, 0, 0))
    dv = pl.BlockSpec((1, LANE), lambda t: (0, t))
    return in_tile, st, pg, plb, gg, glb, dv


def _bd8(blocks):
    rows = []
    for g in range(S5_GT):
        pieces = [blocks[g]]
        if g:
            pieces.insert(0, jnp.zeros((S5_H, S5_P * g), blocks.dtype))
        if g < S5_GT - 1:
            pieces.append(jnp.zeros((S5_H, S5_P * (S5_GT - 1 - g)), blocks.dtype))
        rows.append(jnp.concatenate(pieces, axis=1))
    return jnp.concatenate(rows, axis=0)


def _bd8_diag(m):
    return jnp.stack([m[S5_H * g:S5_H * (g + 1), S5_P * g:S5_P * (g + 1)] for g in range(S5_GT)])


def _row8(v):
    return jnp.concatenate([v[g:g + 1] for g in range(S5_GT)], axis=1)


def _row8_split(r):
    return jnp.concatenate([r[:, S5_P * g:S5_P * (g + 1)] for g in range(S5_GT)], axis=0)


def _layer_row_tile(layer):
    return pl.BlockSpec((None, 1, LANE), lambda t: (layer, 0, t))


def _s5_fwd(z, bb_re, bb_im, lb_re, lb_im, c_re, c_im, dvec, layer):
    L = z.shape[0]

    def body(u_ref, bbr_ref, bbi_ref, lr_ref, li_ref, cr_ref, ci_ref, d_ref, y_ref, sr_ref, si_ref):
        bbr, bbi = _bd8(bbr_ref[...]).astype(MXU), _bd8(bbi_ref[...]).astype(MXU)
        cr, ci = _bd8(cr_ref[...]).astype(MXU), _bd8(ci_ref[...]).astype(MXU)
        dv = d_ref[...]
        steps, e = _tile_powers(_row8(lr_ref[...]), _row8(li_ref[...]))

        def step(c, carry):
            r0 = pl.multiple_of(c * RC, RC)
            u = u_ref[pl.ds(r0, RC), :]
            ub = u.astype(MXU)
            sr = jnp.dot(ub, bbr, preferred_element_type=F32)
            si = jnp.dot(ub, bbi, preferred_element_type=F32)
            sr, si, carry = _scan_lti(sr, si, carry, steps, e)
            sr_ref[pl.ds(r0, RC), :] = sr
            si_ref[pl.ds(r0, RC), :] = si
            y_ref[pl.ds(r0, RC), :] = dv * u + (_mm_nt(sr, cr) - _mm_nt(si, ci))
            return carry

        zero = jnp.zeros((1, S5_TW), F32)
        lax.fori_loop(0, L // RC, step, (zero, zero))

    in_tile, st, pg, plb, _, _, _ = _s5_specs(L, layer)
    u_tile = pl.BlockSpec((L, LANE), lambda t: (0, C_S5U // LANE + t))
    return pl.pallas_call(
        body, name="s5_fwd", grid=(N_S5_T,),
        in_specs=[u_tile, pg, pg, plb, plb, pg, pg, _layer_row_tile(layer)],
        out_specs=[in_tile, st, st],
        out_shape=[_S((L, S5_W)), _S((L, S5_N)), _S((L, S5_N))],
        compiler_params=_params(1))(z, bb_re, bb_im, lb_re, lb_im, c_re, c_im, dvec)


def _s5_bwd(dy0, z, s_re, s_im, bb_re, bb_im, lb_re, lb_im, c_re, c_im, dvec, layer, token=None):
    L = z.shape[0]
    extra, extra_specs = _after(token)

    def body(dy_ref, u_ref, sr_ref, si_ref, bbr_ref, bbi_ref, lr_ref, li_ref, cr_ref, ci_ref, d_ref, *rest):
        (du_ref, dbbr_out, dbbi_out, dlr_out, dli_out, dcr_out, dci_out, dd_ref,
         dbbr_ref, dbbi_ref, dcr_ref, dci_ref, dlr_ref, dli_ref) = rest[len(extra):]
        bbr, bbi = _bd8(bbr_ref[...]).astype(MXU), _bd8(bbi_ref[...]).astype(MXU)
        cr, ci = _bd8(cr_ref[...]).astype(MXU), _bd8(ci_ref[...]).astype(MXU)
        lr, li = _row8(lr_ref[...]), -_row8(li_ref[...])
        dv = d_ref[...]
        steps, e = _tile_powers(lr, li, reverse=True)
        for ref in (dbbr_ref, dbbi_ref, dlr_ref, dli_ref, dcr_ref, dci_ref, dd_ref):
            ref[...] = jnp.zeros_like(ref)
        nch = L // RC

        def step(k, carry):
            c = nch - 1 - k
            r0 = pl.multiple_of(c * RC, RC)
            dy = dy_ref[pl.ds(r0, RC), :]
            u = u_ref[pl.ds(r0, RC), :]
            dyb, ub = dy.astype(MXU), u.astype(MXU)
            sr, si = sr_ref[pl.ds(r0, RC), :], si_ref[pl.ds(r0, RC), :]
            dcr_ref[...] += _mm_tn(dyb, sr)
            dci_ref[...] -= _mm_tn(dyb, si)
            gr = jnp.dot(dyb, cr, preferred_element_type=F32)
            gi = -jnp.dot(dyb, ci, preferred_element_type=F32)
            gr, gi, carry = _scan_lti(gr, gi, carry, steps, e, reverse=True)
            pr_ = pltpu.roll(jnp.concatenate([_halo(sr_ref, c, r0), sr], axis=0), 1, 0)[8:, :]
            pi_ = pltpu.roll(jnp.concatenate([_halo(si_ref, c, r0), si], axis=0), 1, 0)[8:, :]
            dlr_ref[...] += _colsum(pr_ * gr + pi_ * gi)
            dli_ref[...] += _colsum(pr_ * gi - pi_ * gr)
            grb, gib = gr.astype(MXU), gi.astype(MXU)
            dbbr_ref[...] += _mm_tn(ub, grb)
            dbbi_ref[...] += _mm_tn(ub, gib)
            du_ref[pl.ds(r0, RC), :] = dv * dy + (_mm_nt(grb, bbr) + _mm_nt(gib, bbi))
            dd_ref[...] += _colsum(dy * u)
            return carry

        zero = jnp.zeros((1, S5_TW), F32)
        lax.fori_loop(0, nch, step, (zero, zero))
        dbbr_out[...], dbbi_out[...] = _bd8_diag(dbbr_ref[...]), _bd8_diag(dbbi_ref[...])
        dcr_out[...], dci_out[...] = _bd8_diag(dcr_ref[...]), _bd8_diag(dci_ref[...])
        dlr_out[...], dli_out[...] = _row8_split(dlr_ref[...]), _row8_split(dli_ref[...])

    in_tile, st, pg, plb, gg, glb, dv = _s5_specs(L, layer)
    u_tile = pl.BlockSpec((L, LANE), lambda t: (0, C_S5U // LANE + t))
    groups, rows = _S((N_S5_T, S5_GT, S5_H, S5_P)), _S((N_S5_T, S5_GT, S5_P))
    wide = pltpu.VMEM((LANE, S5_TW), F32)
    return pl.pallas_call(
        body, name="s5_bwd", grid=(N_S5_T,),
        in_specs=[in_tile, u_tile, st, st, pg, pg, plb, plb, pg, pg, _layer_row_tile(layer)] + extra_specs,
        out_specs=[in_tile, gg, gg, glb, glb, gg, gg, dv],
        out_shape=[_S((L, S5_W)), groups, groups, rows, rows, groups, groups, _S((1, S5_W))],
        scratch_shapes=[wide, wide, wide, wide, pltpu.VMEM((1, S5_TW), F32), pltpu.VMEM((1, S5_TW), F32)],
        compiler_params=_params(1))(dy0, z, s_re, s_im, bb_re, bb_im, lb_re, lb_im, c_re, c_im, dvec, *extra)


def _disc(ar, ai, ls):
    dt = jnp.exp(ls)
    mag = jnp.exp(ar * dt)
    lr = mag * jnp.cos(ai * dt)
    li = mag * jnp.sin(ai * dt)
    den = ar * ar + ai * ai
    cr = ((lr - 1.0) * ar + li * ai) / den
    ci = (li * ar - (lr - 1.0) * ai) / den
    return lr, li, cr, ci


def _s5_disc_fwd(ar, ai, ls, token=None):
    extra, extra_specs = _after(token)

    def body(ar_ref, ai_ref, ls_ref, *rest):
        lr_ref, li_ref, cr_ref, ci_ref = rest[len(extra):]
        lr, li, cr, ci = _disc(ar_ref[...], ai_ref[...], ls_ref[...])
        lr_ref[...], li_ref[...], cr_ref[...], ci_ref[...] = lr, li, cr, ci

    sh = _S(ar.shape)
    vm = pl.BlockSpec(memory_space=pltpu.VMEM)
    return pl.pallas_call(body, name="s5_disc_fwd", in_specs=[vm, vm, vm] + extra_specs, out_shape=[sh, sh, sh, sh])(
        ar, ai, ls, *extra)


def _s5_disc_bwd(ar, ai, ls, dlr, dli, dcr, dci):
    def body(ar_ref, ai_ref, ls_ref, dlr_ref, dli_ref, dcr_ref, dci_ref, dar_ref, dai_ref, dls_ref):
        _, vjp = jax.vjp(_disc, ar_ref[...], ai_ref[...], jnp.broadcast_to(ls_ref[...], ar_ref.shape))
        dar, dai, dls = vjp((dlr_ref[...], dli_ref[...], dcr_ref[...], dci_ref[...]))
        dar_ref[...], dai_ref[...] = dar, dai
        dls_ref[...] = jnp.sum(dls, axis=1, keepdims=True)

    return pl.pallas_call(body, name="s5_disc_bwd", out_shape=[_S(ar.shape), _S(ar.shape), _S(ls.shape)])(
        ar, ai, ls, dlr, dli, dcr, dci)


def _s5_bscale_fwd(cr, ci, br, bi):
    def body(cr_ref, ci_ref, br_ref, bi_ref, or_ref, oi_ref):
        or_ref[...], oi_ref[...] = _cmul(cr_ref[...], ci_ref[...], br_ref[...], bi_ref[...])

    return pl.pallas_call(body, name="s5_bscale_fwd", out_shape=[_S(br.shape), _S(br.shape)])(cr, ci, br, bi)


def _s5_bscale_bwd(cr, ci, br, bi, gr, gi):
    def body(cr_ref, ci_ref, br_ref, bi_ref, gr_ref, gi_ref, dbr_ref, dbi_ref, dcr_ref, dci_ref):
        cr_, ci_, br_, bi_, gr_, gi_ = (r[...] for r in (cr_ref, ci_ref, br_ref, bi_ref, gr_ref, gi_ref))
        dbr_ref[...] = cr_ * gr_ + ci_ * gi_
        dbi_ref[...] = cr_ * gi_ - ci_ * gr_
        dcr_ref[...] = jnp.sum(gr_ * br_ + gi_ * bi_, axis=1, keepdims=True)
        dci_ref[...] = jnp.sum(gi_ * br_ - gr_ * bi_, axis=1, keepdims=True)

    return pl.pallas_call(body, name="s5_bscale_bwd",
                          out_shape=[_S(br.shape), _S(br.shape), _S(cr.shape), _S(cr.shape)])(cr, ci, br, bi, gr, gi)


def _row(w):
    return pl.BlockSpec((TM, w), lambda i: (i, 0))


def _full(shape):
    return pl.BlockSpec(tuple(shape), lambda i: (0,) * len(shape))


def _gate_rows():
    return [pl.BlockSpec((TM, RG_W), lambda i: (i, C_RGG // RG_W))] + [
        pl.BlockSpec((TM, LANE), lambda i, k=k: (i, C_S5G // LANE + k)) for k in range(N_S5_T)]


def _p_rows(layer):
    return pl.BlockSpec((None, None, TM, PLE_D), lambda i: (layer, 0, i, 0))


def _lrow(layer, width):
    return pl.BlockSpec((None, 1, width), lambda i: (layer, 0, 0))


def _post_fwd(x, hs, z, y0, p, w_glu, b_glu, w_out, g1, b1, ple_w, w_pg, b_pg, g2, b2, layer):
    L = x.shape[0]

    def body(x_ref, hs_ref, zg_ref, zs0_ref, zs1_ref, zs2_ref, y0_ref, p_ref, wg_ref, bg_ref, wo_ref, g1_ref, b1_ref, pw_ref,
             wpg_ref, bpg_ref, g2_ref, b2_ref, x2_ref, xh1_ref, xh2_ref, q_ref, gt_ref, rstd1_ref, rstd2_ref):
        rg_gate = zg_ref[...]
        s5_gate = jnp.concatenate([zs0_ref[...], zs1_ref[...], zs2_ref[...]], axis=1)
        rg_y = hs_ref[...] * _silu_and_grad(rg_gate)[0]
        y1 = _gelu(y0_ref[...])
        gl = _sigmoid(_mm(y1, wg_ref[...]) + bg_ref[...])
        s5_y = (y1 * gl) * _silu_and_grad(s5_gate)[0]
        mix = _mm(jnp.concatenate([rg_y.astype(MXU), s5_y.astype(MXU)], axis=1), wo_ref[...])
        t1 = ALPHA * x_ref[...] + mix
        x1, xh1, rstd1 = _ln_fwd(t1, g1_ref[...], b1_ref[...])
        q = _mm(p_ref[...], pw_ref[...])
        gt = _sigmoid(_mm(x1, wpg_ref[...]) + bpg_ref[...])
        t2 = ALPHA * x1 + q * gt
        x2, xh2, rstd2 = _ln_fwd(t2, g2_ref[...], b2_ref[...])
        x2_ref[...], xh1_ref[...], xh2_ref[...], q_ref[...], gt_ref[...] = x2, xh1, xh2, q, gt
        rstd1_ref[...], rstd2_ref[...] = rstd1, rstd2

    vec = _lrow(layer, D_MODEL)
    return pl.pallas_call(
        body, name="post_fwd", grid=(L // TM,),
        in_specs=[_row(D_MODEL), _row(RG_W), *_gate_rows(), _row(S5_W), _p_rows(layer), _full((S5_W, S5_W)),
                  _lrow(layer, S5_W), _full((D_MODEL, D_MODEL)), vec, vec, _full((PLE_D, D_MODEL)), _full((D_MODEL, D_MODEL)),
                  vec, vec, vec],
        out_specs=[_row(D_MODEL)] * 5 + [_row(1)] * 2, out_shape=[_S((L, D_MODEL))] * 5 + [_S((L, 1))] * 2,
        compiler_params=_params(1))(x, hs, z, z, z, z, y0, p, w_glu, b_glu, w_out, g1, b1, ple_w, w_pg, b_pg, g2, b2)


def _post_bwd_a(dx2_or_target, is_top, xh2, xh1, rstd2, rstd1, q, gt, p, w_pg, g1, b1, g2, b2, layer, token=None):
    L = xh1.shape[0]
    extra, extra_specs = _after(token)

    def body(d_ref, xh2_ref, xh1_ref, rstd2_ref, rstd1_ref, q_ref, gt_ref, p_ref, wpg_ref, g1_ref, b1_ref, g2_ref,
             b2_ref, *rest):
        (dt1_ref, dpw_out, dwpg_out, dbpg_ref, dg1_ref, db1_ref, dg2_ref, db2_ref, loss_ref, dpw_ref,
         dwpg_ref) = rest[len(extra):]
        @pl.when(pl.program_id(0) == 0)
        def _():
            for ref in (dpw_ref, dwpg_ref, dbpg_ref, dg1_ref, db1_ref, dg2_ref, db2_ref, loss_ref):
                ref[...] = jnp.zeros_like(ref)

        g1, g2 = g1_ref[...], g2_ref[...]
        xh1, xh2, rstd1, rstd2 = xh1_ref[...], xh2_ref[...], rstd1_ref[...], rstd2_ref[...]
        x1 = xh1 * g1 + b1_ref[...]
        if is_top:
            err = (xh2 * g2 + b2_ref[...]) - d_ref[...]
            loss_ref[...] += _colsum(err * err)
            dx2 = err * (1.0 / D_MODEL)
        else:
            dx2 = d_ref[...]
        p = p_ref[...]
        q, gt = q_ref[...], gt_ref[...]
        dg2_ref[...] += _colsum(dx2 * xh2)
        db2_ref[...] += _colsum(dx2)
        dt2 = _ln_bwd(dx2, xh2, rstd2, g2)
        dq = dt2 * gt
        dgpre = (dt2 * q) * gt * (1.0 - gt)
        dpw_ref[...] += _mm_tn(p, dq)
        dwpg_ref[...] += _mm_tn(x1, dgpre)
        dbpg_ref[...] += _colsum(dgpre)
        dx1 = ALPHA * dt2 + _mm_nt(dgpre, wpg_ref[...])
        dg1_ref[...] += _colsum(dx1 * xh1)
        db1_ref[...] += _colsum(dx1)
        dt1_ref[...] = _ln_bwd(dx1, xh1, rstd1, g1)

        @pl.when(pl.program_id(0) == L // TM - 1)
        def _():
            dpw_out[...] = dpw_ref[...].astype(WIRE)
            dwpg_out[...] = dwpg_ref[...].astype(WIRE)

    vec, lvec = _full((1, D_MODEL)), _lrow(layer, D_MODEL)
    return pl.pallas_call(
        body, name="post_bwd_a_top" if is_top else "post_bwd_a", grid=(L // TM,),
        in_specs=[_row(D_MODEL), _row(D_MODEL), _row(D_MODEL), _row(1), _row(1), _row(D_MODEL), _row(D_MODEL), _p_rows(layer),
                  _full((D_MODEL, D_MODEL)), lvec, lvec, lvec, lvec] + extra_specs,
        out_specs=[_row(D_MODEL), _full((PLE_D, D_MODEL)), _full((D_MODEL, D_MODEL)), vec, vec, vec, vec, vec, vec],
        out_shape=[_S((L, D_MODEL)), _S((PLE_D, D_MODEL), WIRE), _S((D_MODEL, D_MODEL), WIRE)] + [_S((1, D_MODEL))] * 6,
        scratch_shapes=[pltpu.VMEM((PLE_D, D_MODEL), F32), pltpu.VMEM((D_MODEL, D_MODEL), F32)],
        compiler_params=_params(1))(dx2_or_target, xh2, xh1, rstd2, rstd1, q, gt, p, w_pg, g1, b1, g2, b2, *extra)


def _post_bwd_b(dt1, z, hs, y0, w_out, w_glu, b_glu, layer):
    L = dt1.shape[0]

    def body(dt1_ref, zg_ref, zs0_ref, zs1_ref, zs2_ref, hs_ref, y0_ref, wo_ref, wg_ref, bg_ref,
             dhs_ref, dy0_ref, dzg_ref, dwo_out, dwg_out, dbg_ref, dwo_ref, dwg_ref):
        @pl.when(pl.program_id(0) == 0)
        def _():
            for ref in (dwo_ref, dwg_ref, dbg_ref):
                ref[...] = jnp.zeros_like(ref)

        dt1b = dt1_ref[...].astype(MXU)
        dm = _mm_nt(dt1b, wo_ref[...])
        d_rgy, d_s5y = dm[:, :RG_W], dm[:, RG_W:]
        rg_gate = zg_ref[...]
        s5_gate = jnp.concatenate([zs0_ref[...], zs1_ref[...], zs2_ref[...]], axis=1)
        hs = hs_ref[...]
        sl, dsl = _silu_and_grad(rg_gate)
        dhs_ref[...] = d_rgy * sl
        dzg_ref[:, :RG_W] = d_rgy * hs * dsl
        y0 = y0_ref[...]
        y1 = _gelu(y0)
        gl = _sigmoid(_mm(y1, wg_ref[...]) + bg_ref[...])
        y2 = y1 * gl
        sl2, dsl = _silu_and_grad(s5_gate)
        m = jnp.concatenate([(hs * sl).astype(MXU), (y2 * sl2).astype(MXU)], axis=1)
        dwo_ref[...] += _mm_tn(m, dt1b)
        dy2 = d_s5y * sl2
        dzg_ref[:, RG_W:] = d_s5y * y2 * dsl
        dglpre = (dy2 * y1) * gl * (1.0 - gl)
        dwg_ref[...] += _mm_tn(y1, dglpre)
        dbg_ref[...] += _colsum(dglpre)
        dy1 = dy2 * gl + _mm_nt(dglpre, wg_ref[...])
        dy0_ref[...] = dy1 * _gelu_grad(y0)

        @pl.when(pl.program_id(0) == L // TM - 1)
        def _():
            dwo_out[...] = dwo_ref[...].astype(WIRE)
            dwg_out[...] = dwg_ref[...].astype(WIRE)

    return pl.pallas_call(
        body, name="post_bwd_b", grid=(L // TM,),
        in_specs=[_row(D_MODEL), *_gate_rows(), _row(RG_W), _row(S5_W), _full((D_MODEL, D_MODEL)),
                  _full((S5_W, S5_W)), _lrow(layer, S5_W)],
        out_specs=[_row(RG_W), _row(S5_W), _row(D_MODEL), _full((D_MODEL, D_MODEL)), _full((S5_W, S5_W)), _full((1, S5_W))],
        out_shape=[_S((L, RG_W)), _S((L, S5_W)), _S((L, D_MODEL)), _S((D_MODEL, D_MODEL), WIRE), _S((S5_W, S5_W), WIRE),
                   _S((1, S5_W))],
        scratch_shapes=[pltpu.VMEM((D_MODEL, D_MODEL), F32), pltpu.VMEM((S5_W, S5_W), F32)],
        compiler_params=_params(1))(dt1, z, z, z, z, hs, y0, w_out, w_glu, b_glu)


def _adamw(parts, w, m, v, token=None):
    nl = len(parts)
    extra, extra_specs = _after(token)
    n, R, C = parts[0].shape
    tr = R
    for cand in (512, 256, 128, 64, 32, 16, 8):
        if R % cand == 0 and n * cand * C * 4 <= 4 * 1024 * 1024:
            tr = cand
            break
    nblk = R // tr

    def body(*refs):
        p_refs = refs[:nl]
        w_ref, m_ref, v_ref = refs[nl:nl + 3]
        g_ref, d_ref, nm_ref, nv_ref = refs[nl + 3 + len(extra):]
        layer = pl.program_id(0)
        g = None
        for li, p_ref in enumerate(p_refs):
            s = p_ref[0].astype(F32)
            for k in range(1, n):
                s = s + p_ref[k].astype(F32)
            g = s if g is None else jnp.where(layer == li, s, g)
        nm = B1 * m_ref[...] + (1.0 - B1) * g
        nv = B2 * v_ref[...] + (1.0 - B2) * (g * g)
        d_ref[...] = (-LR) * ((nm / BC1) / (jnp.sqrt(nv / BC2) + EPS) + WD * w_ref[...])
        g_ref[...], nm_ref[...], nv_ref[...] = g, nm, nv

    def part_spec(li):
        return pl.BlockSpec((n, tr, C), lambda l, i: (0, jnp.where(l == li, i, jnp.where(l < li, 0, nblk - 1)), 0))

    blk = pl.BlockSpec((tr, C), lambda l, i: (l * nblk + i, 0))
    return pl.pallas_call(
        body, name="adamw", grid=(nl, nblk),
        in_specs=[part_spec(li) for li in range(nl)] + [blk, blk, blk] + extra_specs,
        out_specs=[blk] * 4, out_shape=[_S((nl * R, C))] * 4, compiler_params=_params(2))(*parts, w, m, v, *extra)


def _adamw_natural(names, g, w, m, v, name):
    n = len(names)

    def body(*refs):
        for j in range(n):
            g_ref, w_ref, m_ref, v_ref, d_ref, nm_ref, nv_ref = (refs[k * n + j] for k in range(7))
            gj = g_ref[...]
            nm = B1 * m_ref[...] + (1.0 - B1) * gj
            nv = B2 * v_ref[...] + (1.0 - B2) * (gj * gj)
            d_ref[...] = (-LR) * ((nm / BC1) / (jnp.sqrt(nv / BC2) + EPS) + WD * w_ref[...])
            nm_ref[...], nv_ref[...] = nm, nv

    ins = [t[k] for t in (g, w, m, v) for k in names]
    outs = pl.pallas_call(body, name=name, out_shape=[_S(w[k].shape) for _ in range(3) for k in names],
                          compiler_params=pltpu.CompilerParams(vmem_limit_bytes=VMEM_LIMIT))(*ins)
    return [{k: outs[t * n + j] for j, k in enumerate(names)} for t in range(3)]


def _me():
    return lax.axis_index("x"), lax.axis_index("y"), lax.axis_index("c")


def _lin(dev):
    return 4 * dev[0] + 2 * dev[1] + dev[2]


def _blk(ref, axis, size, idx):
    nd = len(ref.shape)
    start = idx * size
    if axis == nd - 1 and size % LANE == 0:
        start = pl.multiple_of(start, LANE)
    elif axis == nd - 2 and size % 16 == 0:
        start = pl.multiple_of(start, 16)
    ix = [slice(None)] * nd
    ix[axis] = pl.ds(start, size)
    return ref.at[tuple(ix)]


HBM_SPEC = pl.BlockSpec(memory_space=pltpu.HBM)
SEM_SPEC = pl.BlockSpec(memory_space=pltpu.SEMAPHORE)
EFFECT = pltpu.SideEffectType.DATAFLOW_SIDE_EFFECTING


def _peers(x, y, c):
    flip = lambda v, f: 1 - v if f else v
    return [(flip(x, k & 4), flip(y, k & 2), flip(c, k & 1)) for k in range(1, N_DEV)]


def _land_shape(mode, s, axis):
    if mode == "gather":
        return s.shape[:axis] + (N_DEV * s.shape[axis],) + s.shape[axis + 1:]
    return (N_DEV,) + s.shape[:axis] + (s.shape[axis] // N_DEV,) + s.shape[axis + 1:]


def _src_view(mode, ref, axis, peer):
    return ref if mode == "gather" else _blk(ref, axis, ref.shape[axis] // N_DEV, peer)


def _dst_view(mode, land, axis, sender):
    return _blk(land, axis, land.shape[axis] // N_DEV, sender) if mode == "gather" else land.at[sender]


def _blocks(mode, land, axis, k):
    if mode == "gather":
        ix = [slice(None)] * len(land.shape)
        ix[axis] = pl.ds(0, k * (land.shape[axis] // N_DEV))
        return land.at[tuple(ix)]
    return land.at[pl.ds(0, k)]


ARRIVALS = {None: N_DEV - 1, "near": 4, "relay": 3}


def _routes(route, x, y, c):
    me, sibling = (x, y, c), (x, y, 1 - c)
    chips = [(1 - x, y), (x, 1 - y), (1 - x, 1 - y)]
    if route == "near":
        return [(me, sibling)] + [(me, (*chip, c)) for chip in chips]
    if route == "relay":
        return [((*chip, c), sibling) for chip in chips]
    return [(me, peer) for peer in _peers(x, y, c)]


def _place_own(mode, srcs, axes, name, after=None):
    n = len(srcs)
    extra, extra_specs = _after(after)

    def body(me_ref, *refs):
        for a in range(n):
            out = refs[n + len(extra) + a]
            out[...] = refs[a][...].reshape(out.shape)

    def at_me(shape, axis):
        return lambda i, me: tuple(me[0] if d == axis else 0 for d in range(len(shape)))

    in_specs, out_specs = [], []
    for s, axis in zip(srcs, axes):
        if mode == "gather":
            in_specs.append(pl.BlockSpec(s.shape, lambda i, me, nd=len(s.shape): (0,) * nd))
            out_specs.append(pl.BlockSpec(s.shape, at_me(s.shape, axis)))
        else:
            blk = s.shape[:axis] + (s.shape[axis] // N_DEV,) + s.shape[axis + 1:]
            in_specs.append(pl.BlockSpec(blk, at_me(blk, axis)))
            out_specs.append(pl.BlockSpec((1,) + blk, at_me((1,) + blk, 0)))
    me = _lin(_me()).astype(jnp.int32).reshape(1)
    return pl.pallas_call(
        body, name=name, out_shape=[_S(_land_shape(mode, s, a), s.dtype) for s, a in zip(srcs, axes)],
        grid_spec=pltpu.PrefetchScalarGridSpec(num_scalar_prefetch=1, grid=(1,), in_specs=in_specs + extra_specs,
                                               out_specs=out_specs),
        compiler_params=_params(1))(me, *srcs, *extra)


def _place_shards(shards, layers, axes, dtypes, name, after=None):
    n = len(shards)
    extra, extra_specs = _after(after)

    def body(me_ref, *refs):
        for a in range(n):
            out = refs[n + len(extra) + a]
            out[...] = refs[a][...].astype(out.dtype)

    in_specs, out_specs, out_shape = [], [], []
    for s, layer, axis, dt in zip(shards, layers, axes, dtypes):
        shape = s.shape if layer is None else s.shape[1:]
        nd = len(shape)
        if layer is None:
            in_specs.append(pl.BlockSpec(shape, lambda i, me, nd=nd: (0,) * nd))
        else:
            in_specs.append(pl.BlockSpec((None,) + shape, lambda i, me, nd=nd, layer=layer: (layer,) + (0,) * nd))
        out_specs.append(pl.BlockSpec(shape, lambda i, me, nd=nd, axis=axis: tuple(me[0] if d == axis else 0 for d in range(nd))))
        out_shape.append(_S(shape[:axis] + (N_DEV * shape[axis],) + shape[axis + 1:], dt))
    me = _lin(_me()).astype(jnp.int32).reshape(1)
    return pl.pallas_call(
        body, name=name, out_shape=out_shape,
        grid_spec=pltpu.PrefetchScalarGridSpec(num_scalar_prefetch=1, grid=(1,), in_specs=in_specs + extra_specs,
                                               out_specs=out_specs),
        compiler_params=_params(1))(me, *shards, *extra)


def _push_start(mode, srcs, lands, axes, name, route=None):
    n, ns = len(lands), len(srcs)

    def body(*refs):
        src_refs, land_refs = refs[:ns], refs[ns:ns + n]
        send_sems, recv_sems = refs[ns + n], refs[ns + n + 1]
        token = refs[-1]
        x, y, c = _me()
        for a in range(n):
            for block, peer in _routes(route, x, y, c):
                there = _dst_view(mode, land_refs[a], axes[a], _lin(block))
                pltpu.make_async_remote_copy(
                    src_ref=_src_view(mode, src_refs[a], axes[a], _lin(peer)) if ns else there, dst_ref=there,
                    send_sem=send_sems.at[a], recv_sem=recv_sems.at[a], device_id=peer, device_id_type=MESH).start()
        token[...] = jnp.zeros_like(token)

    hbm = lambda s: pltpu.HBM(s.shape, s.dtype)
    outs = pl.pallas_call(
        body, name=name,
        out_shape=(pltpu.SemaphoreType.DMA((n,)), pltpu.SemaphoreType.DMA((n,)), *[hbm(s) for s in srcs], *[hbm(s) for s in lands],
                   _S((SUB, LANE))),
        in_specs=[HBM_SPEC] * (ns + n),
        out_specs=(SEM_SPEC, SEM_SPEC, *[HBM_SPEC] * (ns + n), pl.BlockSpec(memory_space=pltpu.VMEM)),
        input_output_aliases={i: 2 + i for i in range(ns + n)},
        compiler_params=pltpu.CompilerParams(has_side_effects=EFFECT),
    )(*[pltpu.with_memory_space_constraint(s, pltpu.HBM) for s in list(srcs) + list(lands)])
    return outs[0], outs[1], outs[2:2 + ns], outs[2 + ns:2 + ns + n], outs[-1]


def _push_wait(mode, send_sems, recv_sems, srcs, lands, axes, after, name, first=0, route=None):
    n, ns = len(lands), len(srcs)
    after = list(after) if isinstance(after, (list, tuple)) else [after]

    def body(*refs):
        land_refs = refs[ns:ns + n]
        send_sems, recv_sems = refs[ns + n], refs[ns + n + 1]
        x, y, c = _me()
        for a in range(n):
            seven = _blocks(mode, land_refs[a], axes[a], ARRIVALS[route])
            cp = pltpu.make_async_remote_copy(src_ref=seven, dst_ref=seven, send_sem=send_sems.at[first + a],
                                              recv_sem=recv_sems.at[first + a],
                                              device_id=(x, y, 1 - c), device_id_type=MESH)
            cp.wait_send()
            cp.wait_recv()

    hbm = lambda s: pltpu.HBM(s.shape, s.dtype)
    outs = pl.pallas_call(
        body, name=name, out_shape=tuple(hbm(s) for s in list(srcs) + list(lands)),
        in_specs=[HBM_SPEC] * (ns + n) + [SEM_SPEC, SEM_SPEC] + [ANY] * len(after), out_specs=tuple([HBM_SPEC] * (ns + n)),
        input_output_aliases={i: i for i in range(ns + n)},
        compiler_params=pltpu.CompilerParams(has_side_effects=EFFECT),
    )(*srcs, *lands, send_sems, recv_sems, *after)
    return outs[ns:]


def _sum_parts(parts):
    n, R, C = parts.shape

    def body(p_ref, o_ref):
        g = p_ref[0]
        for k in range(1, n):
            g = g + p_ref[k]
        o_ref[...] = g

    return pl.pallas_call(body, name="sum_parts", out_shape=_S((R, C)))(parts)


SMALL =['conv_b', 'rg_wa', 'rg_ba', 'rg_wx', 'rg_bx', 'rg_lambda', 's5_a_re', 's5_a_im', 's5_b_re', 's5_b_im',
         's5_c_re', 's5_c_im', 's5_d', 's5_log_step', 's5_b_glu', 'ln1_g', 'ln1_b', 'ple_gate_b', 'ln2_g', 'ln2_b']
WEIGHTS = ['w_in', 'conv_w', 'conv_b', 'rg_wa', 'rg_ba', 'rg_wx', 'rg_bx', 'rg_lambda', 's5_a_re', 's5_a_im', 's5_b_re',
           's5_b_im', 's5_c_re', 's5_c_im', 's5_d', 's5_log_step', 's5_w_glu', 's5_b_glu', 'w_out', 'ln1_g', 'ln1_b',
           'ple_w', 'ple_gate_w', 'ple_gate_b', 'ln2_g', 'ln2_b']
PACK_ROWS_MULT = 64


STORED = {'s5_b_re': (2, 3), 's5_b_im': (2, 3), 's5_d': (1, 2)}


def _stored(k, a):
    return jnp.swapaxes(a, *STORED[k]) if k in STORED else a


def _pack(tree, scalar):
    flat = jnp.concatenate([tree[k].reshape(-1) for k in SMALL] + [scalar.reshape(1)])
    rows = -(-flat.shape[0] // (LANE * PACK_ROWS_MULT)) * PACK_ROWS_MULT
    return jnp.pad(flat, (0, rows * LANE - flat.shape[0])).reshape(rows, LANE)


def _unpack(packed, like):
    flat, out, o = packed.reshape(-1), {}, 0
    for k in SMALL:
        n = math.prod(like[k].shape)
        out[k] = flat[o:o + n].reshape(like[k].shape)
        o += n
    return out, flat[o]


class _NoHooks:
    token = None
    first_token = None

    def first_weights(self, full, after):
        return full

    def layer_start(self, i, W, after):
        return W

    def late_weights(self, i, W, after):
        return W

    def post_done(self, i, g):
        return None

    def smalls_done(self, grads, loss):
        self.small = _small_grads(grads, self.res)
        return None

    def w_in_done(self, i, g):
        return None

    def layer_done(self, i, g, dx):
        return None


def _local_grads(x, p, target, W, disc, hooks):
    depth = 2
    saved = []
    for i in range(depth):
        if i > 0:
            W = hooks.layer_start(i, W, x)
        w = W[i]
        z = _inproj_fwd(x, w['w_in'], hooks.token if i == 0 else None)
        hs, *gates = _rg_fwd(z, w['conv_w'], w['conv_b'], w['wa_bd'], w['wx_bd'], w['rg_ba'], w['rg_bx'], w['rg_lambda'], i)
        d = disc[i]
        y0, s_re, s_im = _s5_fwd(z, d['bb_re'], d['bb_im'], d['lb_re'], d['lb_im'], d['c_re'], d['c_im'], w['s5_d'], i)
        W = hooks.late_weights(i, W, y0)
        w = W[i]
        x2, *norms = _post_fwd(x, hs, z, y0, p, w['s5_w_glu'], w['s5_b_glu'], w['w_out'], w['ln1_g'], w['ln1_b'],
                               w['ple_w'], w['ple_gate_w'], w['ple_gate_b'], w['ln2_g'], w['ln2_b'], i)
        saved.append((x, z, hs, gates, y0, s_re, s_im, norms))
        x = x2

    grads = [None] * depth
    dx = target
    loss = None
    token = None
    for i in reversed(range(depth)):
        w, d = W[i], disc[i]
        xin, z, hs, gates, y0, s_re, s_im, (xh1, xh2, q, gt, rstd1, rstd2) = saved[i]
        g = {}
        (dt1, g['ple_w'], g['ple_gate_w'], g['ple_gate_b'], g['ln1_g'], g['ln1_b'], g['ln2_g'], g['ln2_b'], lrow) = _post_bwd_a(
            dx, i == depth - 1, xh2, xh1, rstd2, rstd1, q, gt, p, w['ple_gate_w'], w['ln1_g'], w['ln1_b'],
            w['ln2_g'], w['ln2_b'], i, token)
        if i == depth - 1:
            loss = 0.5 / D_MODEL * jnp.sum(lrow)
        dhs, dy0, dzg, g['w_out'], g['s5_w_glu'], g['s5_b_glu'] = _post_bwd_b(dt1, z, hs, y0, w['w_out'], w['s5_w_glu'],
                                                                           w['s5_b_glu'], i)
        (dzu, g['bb_re'], g['bb_im'], g['lb_re'], g['lb_im'], g['c_re'], g['c_im'], g['s5_d']) = _s5_bwd(
            dy0, z, s_re, s_im, d['bb_re'], d['bb_im'], d['lb_re'], d['lb_im'], d['c_re'], d['c_im'], w['s5_d'], i,
            hooks.post_done(i, g))
        (dzx, g['conv_w'], g['conv_b'], g['wa_bd'], g['wx_bd'], g['rg_ba'], g['rg_bx'], g['rg_lambda']) = _rg_bwd(
            dhs, z, hs, gates, w['conv_w'], w['wa_bd'], w['wx_bd'], w['rg_lambda'], i)
        if i == 0:
            g['w_in'] = _inproj_bwd_dw(xin, dzx, dzg, dzu, hooks.smalls_done([g, grads[1]], loss))
            dx = _inproj_bwd_dx(dt1, dzx, dzg, dzu, w['w_in'], hooks.w_in_done(i, g))
        else:
            dx, g['w_in'] = _inproj_bwd(dt1, xin, dzx, dzg, dzu, w['w_in'])
        grads[i] = g
        token = hooks.layer_done(i, g, dx)
    return loss, dx, grads


def _s5_layouts_fwd(s5_a_re, s5_a_im, s5_log_step, s5_b_re, s5_b_im, s5_c_re, s5_c_im, token=None):
    depth = s5_a_re.shape[0]
    ar, ai = s5_a_re.reshape(depth * 24, S5_P), s5_a_im.reshape(depth * 24, S5_P)
    ls = s5_log_step.reshape(depth * 24, 1)
    lr, li, cr, ci = _s5_disc_fwd(ar, ai, ls, token)
    per_group = lambda a: a.reshape(depth * 24, 1, S5_P)
    as_c = lambda b: jnp.swapaxes(b, 2, 3).reshape(depth * 24, S5_H, S5_P)
    res = (ar, ai, ls, per_group(cr), per_group(ci), as_c(s5_b_re), as_c(s5_b_im))
    bbr, bbi = _s5_bscale_fwd(*res[3:])
    tiles = lambda a: a.reshape(depth * N_S5_T, S5_GT, S5_H, S5_P)
    rows = lambda a: a.reshape(depth * N_S5_T, S5_GT, S5_P)
    disc = dict(bb_re=tiles(bbr), bb_im=tiles(bbi), lb_re=rows(lr), lb_im=rows(li), c_re=tiles(s5_c_re), c_im=tiles(s5_c_im))
    return [disc] * depth, res


def _s5_layouts_bwd(grads, res):
    ar, ai, ls, cr, ci, br, bi = res
    depth = len(grads)
    stack = lambda k, shape: jnp.stack([g[k] for g in grads]).reshape(shape)
    groups, shape_c = (depth * 24, S5_H, S5_P), (depth, 24, S5_H, S5_P)
    dbr, dbi, dcr, dci = _s5_bscale_bwd(cr, ci, br, bi, stack('bb_re', groups), stack('bb_im', groups))
    gp = (depth * 24, S5_P)
    dar, dai, dls = _s5_disc_bwd(ar, ai, ls, stack('lb_re', gp), stack('lb_im', gp), dcr.reshape(gp), dci.reshape(gp))
    return dict(
        s5_a_re=dar.reshape(depth, 24, S5_P), s5_a_im=dai.reshape(depth, 24, S5_P), s5_log_step=dls.reshape(depth, 24),
        s5_b_re=dbr.reshape(shape_c), s5_b_im=dbi.reshape(shape_c),
        s5_c_re=stack('c_re', shape_c), s5_c_im=stack('c_im', shape_c))


LATE = ('w_out', 'ple_w', 'ple_gate_w', 's5_w_glu')


ROWS = ('conv_b', 'rg_ba', 'rg_bx', 'rg_lambda', 's5_d', 's5_b_glu', 'ln1_g', 'ln1_b', 'ple_gate_b', 'ln2_g', 'ln2_b')


def _shared_weights(full):
    depth = full['conv_b'].shape[0]
    shared = {k: full[k].reshape(depth, 1, -1) for k in ROWS}
    shared.update(conv_w=full['conv_w'], wa_bd=full['rg_wa'], wx_bd=full['rg_wx'])
    return shared


def _layer_weights(full, shared, i):
    return dict(shared, w_in=full['w_in'][i])


class _AllLocal(_NoHooks):
    def __init__(self, full):
        self.full = full

    def late_weights(self, i, W, after):
        W[i].update({k: self.full[k][i] for k in LATE})
        return W


def _full_grads(full, x, p, target, hooks=None):
    hooks = hooks or _AllLocal(full)
    disc, res = _s5_layouts_fwd(full['s5_a_re'], full['s5_a_im'], full['s5_log_step'], full['s5_b_re'], full['s5_b_im'],
                                full['s5_c_re'], full['s5_c_im'], hooks.first_token)
    full = hooks.first_weights(full, disc[-1]['bb_im'])
    shared = _shared_weights(full)
    W = [_layer_weights(full, shared, i) for i in range(2)]
    hooks.res = res
    loss, gx, grads = _local_grads(x, p, target, W, disc, hooks)
    out = dict(hooks.small)
    for k in SHARD_AXIS:
        out[k] = [g[k] for g in grads]
    return loss, gx, out


def _small_grads(grads, res):
    stack = lambda f: jnp.stack([f(g) for g in grads])
    out = _s5_layouts_bwd(grads, res)
    out['conv_w'] = stack(lambda g: g['conv_w'])
    for k in ('conv_b', 'rg_ba', 'rg_bx', 'rg_lambda', 's5_b_glu', 'ln1_g', 'ln1_b', 'ple_gate_b', 'ln2_g', 'ln2_b'):
        out[k] = stack(lambda g: g[k][0])
    out['s5_d'] = _stored('s5_d', stack(lambda g: g['s5_d'][0]).reshape(2, 24, 16))
    out['rg_wa'] = stack(lambda g: g['wa_bd'])
    out['rg_wx'] = stack(lambda g: g['wx_bd'])
    return out


SHARD_AXIS = {'w_in': 2, 'w_out': 1, 'ple_w': 2, 'ple_gate_w': 1, 's5_w_glu': 1}


def kernel(x, p, w_in, conv_w, conv_b, rg_wa, rg_ba, rg_wx, rg_bx, rg_lambda, s5_a_re, s5_a_im, s5_b_re, s5_b_im, s5_c_re, s5_c_im, s5_d, s5_log_step, s5_w_glu, s5_b_glu, w_out, ln1_g, ln1_b, ple_w, ple_gate_w, ple_gate_b, ln2_g, ln2_b, loss_target, m_w_in, m_conv_w, m_conv_b, m_rg_wa, m_rg_ba, m_rg_wx, m_rg_bx, m_rg_lambda, m_s5_a_re, m_s5_a_im, m_s5_b_re, m_s5_b_im, m_s5_c_re, m_s5_c_im, m_s5_d, m_s5_log_step, m_s5_w_glu, m_s5_b_glu, m_w_out, m_ln1_g, m_ln1_b, m_ple_w, m_ple_gate_w, m_ple_gate_b, m_ln2_g, m_ln2_b, v_w_in, v_conv_w, v_conv_b, v_rg_wa, v_rg_ba, v_rg_wx, v_rg_bx, v_rg_lambda, v_s5_a_re, v_s5_a_im, v_s5_b_re, v_s5_b_im, v_s5_c_re, v_s5_c_im, v_s5_d, v_s5_log_step, v_s5_w_glu, v_s5_b_glu, v_w_out, v_ln1_g, v_ln1_b, v_ple_w, v_ple_gate_w, v_ple_gate_b, v_ln2_g, v_ln2_b):
    local = dict(locals())
    w = {k: local[k] for k in WEIGHTS}
    mom = {k: local['m_' + k] for k in WEIGHTS}
    var = {k: local['v_' + k] for k in WEIGHTS}

    big = list(SHARD_AXIS)
    late_axes = [SHARD_AXIS[k] - 1 for k in LATE]
    pushed = {}

    groups = dict(first=(['w_in', 'conv_w'], [0, None], [1, 0]), l0=(list(LATE), [0] * len(LATE), late_axes),
                  l1=(['w_in'] + list(LATE), [1] * (1 + len(LATE)), [1] + late_axes))
    token = None
    for key, members in (("first", ["first"]), ("rest", ["l0", "l1"])):
        names, layers, axes = (sum((groups[m][j] for m in members), []) for j in range(3))
        shards = [w[k] if layer is not None else w[k][None] for k, layer in zip(names, layers)]
        lands = _place_shards(shards, layers, axes, [WIRE if k in big else w[k].dtype for k in names],
                              "place_weights_" + key, token)
        pushed[key] = _push_start("gather", [], lands, axes, "push_weights_" + key, "near" if key == "first" else None)
        token = pushed[key][4]

    def await_weights(key, axes, after):
        s, first = (pushed["first"], 0) if key == "first" else (pushed["rest"], 0 if key == "l0" else len(LATE))
        return _push_wait("gather", s[0], s[1], [], s[3][first:first + len(axes)], axes, after, "await_weights_" + key, first)

    def push_grads(key, g, names, axes):
        srcs = [g[k] for k in names]
        pushed[key] = _push_start("scatter", srcs, _place_own("scatter", srcs, axes, "place_grads_" + key), axes,
                                  "push_grads_" + key)
        return pushed[key][4]

    def await_grads(key, axes, after):
        s = pushed[key]
        return _push_wait("scatter", s[0], s[1], s[2], s[3], axes, after, "await_grads_" + key)

    class Overlap(_NoHooks):
        token = pushed["rest"][4]
        first_token = token

        def first_weights(self, full, after):
            s, axes = pushed["first"], [1, 0]
            near = _push_wait("gather", s[0], s[1], [], s[3], axes, after, "await_weights_near", route="near")
            s = _push_start("gather", [], near, axes, "relay_weights", "relay")
            w_in0, conv = _push_wait("gather", s[0], s[1], [], s[3], axes, s[4], "await_weights_relay", route="relay")
            return dict(full, w_in=[w_in0, None], conv_w=jnp.moveaxis(conv, 0, 2).reshape(2, 4, RG_W))

        def late_weights(self, i, W, after):
            if i == 0:
                W[0].update(zip(LATE, await_weights("l0", late_axes, after)))
            return W

        def layer_start(self, i, W, after):
            lands = await_weights("l1", [1] + late_axes, after)
            W[1].update(zip(LATE, lands[1:]), w_in=lands[0])
            return W

        def post_done(self, i, g):
            return push_grads("late0", g, LATE, late_axes) if i == 0 else None

        def smalls_done(self, grads, loss):
            super().smalls_done(grads, loss)
            conv = jnp.moveaxis(self.small['conv_w'].reshape(2, 4, N_DEV, RG_W // N_DEV), 2, 0)
            self.packed = _pack(self.small, loss)
            return push_grads("small", dict(conv_w=conv.reshape(N_DEV, 8, RG_W // N_DEV), small=self.packed),
                              ['conv_w', 'small'], [0, 0])

        def w_in_done(self, i, g):
            return push_grads("w_in0", g, ['w_in'], [0])

        def layer_done(self, i, g, dx):
            return push_grads("all1", g, ['w_in'] + list(LATE), [0] + late_axes) if i == 1 else None

    hooks = Overlap()
    _, grad_x, g = _full_grads(dict(w), x[0], p, loss_target[0], hooks)

    recv1 = dict(zip(['w_in'] + list(LATE), await_grads("all1", [0] + late_axes, grad_x)))
    recv0 = dict(zip(LATE, await_grads("late0", late_axes, grad_x)))
    outs = {}

    def update(k, parts):
        shard = w[k].shape
        c = shard[-1]
        two = lambda a: a.reshape(-1, c)
        res = _adamw([r.reshape(N_DEV, -1, c) for r in parts], two(w[k]), two(mom[k]), two(var[k]))
        outs[k] = [o.reshape(shard) for o in res]

    for k in LATE:
        update(k, [recv0[k], recv1[k]])
    done = [outs[k][1] for k in LATE]
    conv_parts, small_parts = await_grads("small", [0, 0], done)

    rows = hooks.packed.shape[0] // N_DEV
    mine = _sum_parts(small_parts.reshape(N_DEV, rows, LANE))
    sums = _push_start("gather", [mine], _place_own("gather", [mine], [0], "place_small_sums"), [0], "push_small_sums")
    w_in0, = await_grads("w_in0", [0], sums[4])
    update('w_in', [w_in0, recv1['w_in']])
    update('conv_w', [conv_parts])
    gathered, = _push_wait("gather", sums[0], sums[1], sums[2], sums[3], [0], [outs['w_in'][1], outs['conv_w'][1]],
                           "await_small_sums")
    stored = [{k: _stored(k, t[k]) for k in SMALL} for t in (w, mom, var)]
    summed, loss = _unpack(gathered, stored[0])
    wide = ['s5_b_re', 's5_b_im']
    for names, name in ((wide, "adamw_s5_b"), ([k for k in SMALL if k not in wide], "adamw_small")):
        delta, new_m, new_v = _adamw_natural(names, summed, *stored, name)
        for k in names:
            outs[k] = [_stored(k, o[k]) for o in (summed, delta, new_m, new_v)]

    res = [loss, grad_x[None]]
    for j in range(4):
        res += [outs[k][j] for k in WEIGHTS]
    return tuple(res)
```

```python
import math

import jax
import jax.numpy as jnp
from jax import lax
from jax.experimental import pallas as pl
from jax.experimental.pallas import tpu as pltpu

F32 = jnp.float32
MXU = jnp.bfloat16
WIRE = jnp.bfloat16

N_DEV = 8
D_MODEL = 1024
PLE_D = 256
RG_W = 640
S5_W = 384
S5_P = 64
S5_N = 24 * S5_P
Z_W = 2 * RG_W + 2 * S5_W
C_RGG = RG_W
C_S5U = 2 * RG_W
C_S5G = 2 * RG_W + S5_W
LANE = 128
N_RG_T = RG_W // LANE
N_S5_T = S5_W // LANE
W_BLK = Z_W // N_DEV
ALPHA = (2.0 * 2) ** 0.25
LN_EPS = 1e-5
RG_C = 8.0
LR, B1, B2, EPS, WD, STEP = 0.001, 0.9, 0.999, 1e-08, 0.01, 10
BC1 = 1.0 - B1 ** STEP
BC2 = 1.0 - B2 ** STEP
RC = 512
RC_RG = 1024
TM = 512
TM_MM = 1024
VMEM_LIMIT = 56 * 1024 * 1024

MESH = pl.DeviceIdType.MESH
ANY = pl.BlockSpec(memory_space=pl.ANY)


def _params(n_grid_axes, vmem=VMEM_LIMIT):
    return pltpu.CompilerParams(dimension_semantics=("arbitrary",) * n_grid_axes, vmem_limit_bytes=vmem)


def _S(shape, dtype=F32):
    return jax.ShapeDtypeStruct(tuple(shape), dtype)


def _sigmoid(x):
    return 0.5 * jnp.tanh(0.5 * x) + 0.5


def _silu_and_grad(x):
    s = _sigmoid(x)
    return x * s, s * (1.0 + x * (1.0 - s))


_GELU_C = math.sqrt(2.0 / math.pi)


def _gelu(x):
    return 0.5 * x * (1.0 + jnp.tanh(_GELU_C * (x + 0.044715 * (x * x * x))))


def _gelu_grad(x):
    th = jnp.tanh(_GELU_C * (x + 0.044715 * (x * x * x)))
    return 0.5 * (1.0 + th) + 0.5 * x * (1.0 - th * th) * (_GELU_C * (1.0 + 3.0 * 0.044715 * (x * x)))


def _mm(a, b):
    return jnp.dot(a.astype(MXU), b.astype(MXU), preferred_element_type=F32)


def _mm_nt(a, b):
    return lax.dot_general(a.astype(MXU), b.astype(MXU), (((1,), (1,)), ((), ())), preferred_element_type=F32)


def _mm_tn(a, b):
    return lax.dot_general(a.astype(MXU), b.astype(MXU), (((0,), (0,)), ((), ())), preferred_element_type=F32)


def _ln_fwd(t, g, b):
    mu = jnp.mean(t, axis=-1, keepdims=True)
    tc = t - mu
    var = jnp.mean(tc * tc, axis=-1, keepdims=True)
    rstd = lax.rsqrt(var + LN_EPS)
    xhat = tc * rstd
    return xhat * g + b, xhat, rstd


def _ln_bwd(dy, xhat, rstd, g):
    dxh = dy * g
    m1 = jnp.mean(dxh, axis=-1, keepdims=True)
    m2 = jnp.mean(dxh * xhat, axis=-1, keepdims=True)
    return rstd * (dxh - m1 - xhat * m2)


def _colsum(a):
    return jnp.sum(a, axis=0, keepdims=True)


def _up(x, d, rows, fill):
    n = x.shape[0]
    return jnp.where(rows < n - d, pltpu.roll(x, n - d, 0), fill)


SUB = 8
TILE_STEPS = (1, 2, 4)


def _r8(width):
    return lax.broadcasted_iota(jnp.int32, (SUB, width), 0)


def _scan_real(a, u, carry, reverse=False):
    r8 = _r8(a.shape[1])
    n = a.shape[0] // SUB
    outs = [None] * n
    for k in (reversed(range(n)) if reverse else range(n)):
        A, U = a[SUB * k:SUB * k + SUB], u[SUB * k:SUB * k + SUB]
        for d in TILE_STEPS:
            m = (r8 < SUB - d) if reverse else (r8 >= d)
            sh = SUB - d if reverse else d
            U = A * jnp.where(m, pltpu.roll(U, sh, 0), 0.0) + U
            A = A * jnp.where(m, pltpu.roll(A, sh, 0), 1.0)
        h = A * carry + U
        outs[k] = h
        carry = h[0:1] if reverse else h[SUB - 1:SUB]
    return jnp.concatenate(outs, axis=0), carry


def _tile_powers(lr, li, reverse=False):
    width = lr.shape[1]
    r8 = _r8(width)
    steps = []
    pr, pi = lr, li
    er, ei = jnp.broadcast_to(lr, (SUB, width)), jnp.broadcast_to(li, (SUB, width))
    for d in TILE_STEPS:
        m = (r8 < SUB - d) if reverse else (r8 >= d)
        sh = SUB - d if reverse else d
        steps.append((sh, jnp.where(m, pr, 0.0), jnp.where(m, pi, 0.0)))
        er, ei = _cmul(er, ei, jnp.where(m, pltpu.roll(er, sh, 0), 1.0), jnp.where(m, pltpu.roll(ei, sh, 0), 0.0))
        pr, pi = _cmul(pr, pi, pr, pi)
    return steps, (er, ei)


def _scan_lti(xr, xi, carry, steps, e, reverse=False):
    er, ei = e
    kr, ki = carry
    n = xr.shape[0] // SUB
    outr, outi = [None] * n, [None] * n
    for k in (reversed(range(n)) if reverse else range(n)):
        sr, si = xr[SUB * k:SUB * k + SUB], xi[SUB * k:SUB * k + SUB]
        for sh, pr, pi in steps:
            shr, shi = pltpu.roll(sr, sh, 0), pltpu.roll(si, sh, 0)
            sr, si = sr + (pr * shr - pi * shi), si + (pr * shi + pi * shr)
        sr = sr + (er * kr - ei * ki)
        si = si + (er * ki + ei * kr)
        outr[k], outi[k] = sr, si
        kr, ki = (sr[0:1], si[0:1]) if reverse else (sr[SUB - 1:SUB], si[SUB - 1:SUB])
    return jnp.concatenate(outr, axis=0), jnp.concatenate(outi, axis=0), (kr, ki)


def _halo(ref, c, r0):
    rp = pl.multiple_of(jnp.maximum(r0 - 8, 0), 8)
    return jnp.where(c > 0, ref[pl.ds(rp, 8), :], 0.0)


def _conv_taps(xe):
    return [pltpu.roll(xe, 3, 0)[8:, :], pltpu.roll(xe, 2, 0)[8:, :], pltpu.roll(xe, 1, 0)[8:, :], xe[8:, :]]


def _rg_gates(h, wa, wx, ba, bx, sp):
    r = _sigmoid(_mm(h, wa) + ba)
    i = _sigmoid(_mm(h, wx) + bx)
    log_a = (-RG_C) * r * sp
    a = jnp.exp(log_a)
    mult = jnp.sqrt(-jnp.tanh(log_a) * (a * a + 1.0))
    return r, i, a, mult


def _softplus(y):
    return jnp.maximum(y, 0.0) + jnp.log1p(jnp.exp(-jnp.abs(y)))


def _after(token):
    return ([], []) if token is None else ([token], [ANY])


def _inproj_fwd(x, w_in, token=None):
    L = x.shape[0]

    def body(x_ref, w_ref, *rest):
        rest[-1][...] = _mm(x_ref[...], w_ref[...])

    extra, extra_specs = _after(token)
    tm = min(TM_MM, L)
    return pl.pallas_call(
        body, name="inproj_fwd", grid=(L // tm,),
        in_specs=[pl.BlockSpec((tm, D_MODEL), lambda i: (i, 0)), pl.BlockSpec((D_MODEL, Z_W), lambda i: (0, 0))] + extra_specs,
        out_specs=pl.BlockSpec((tm, Z_W), lambda i: (i, 0)),
        out_shape=_S((L, Z_W)), compiler_params=_params(1))(x, w_in, *extra)


def _inproj_bwd(dt1, x, dzx, dzg, dzu, w_in):
    L = x.shape[0]

    def body(dt1_ref, x_ref, dzx_ref, dzg_ref, dzu_ref, w_ref, dx_ref, dw_ref, acc_ref):
        @pl.when(pl.program_id(0) == 0)
        def _():
            acc_ref[...] = jnp.zeros_like(acc_ref)
        dzg = dzg_ref[...]
        dz = jnp.concatenate([dzx_ref[...], dzg[:, :RG_W], dzu_ref[...], dzg[:, RG_W:]], axis=1).astype(MXU)
        xb = x_ref[...].astype(MXU)
        dx_ref[...] = ALPHA * dt1_ref[...] + _mm_nt(dz, w_ref[...])
        for j in range(N_DEV):
            acc_ref[j] += _mm_tn(xb, dz[:, j * W_BLK:(j + 1) * W_BLK])

        @pl.when(pl.program_id(0) == L // TM - 1)
        def _():
            dw_ref[...] = acc_ref[...].astype(WIRE)

    row = lambda w: pl.BlockSpec((TM, w), lambda i: (i, 0))
    wspec = pl.BlockSpec((N_DEV, D_MODEL, W_BLK), lambda i: (0, 0, 0))
    return pl.pallas_call(
        body, name="inproj_bwd", grid=(L // TM,),
        in_specs=[row(D_MODEL), row(D_MODEL), row(RG_W), row(D_MODEL), row(S5_W),
                  pl.BlockSpec((D_MODEL, Z_W), lambda i: (0, 0))],
        out_specs=[row(D_MODEL), wspec],
        out_shape=[_S((L, D_MODEL)), _S((N_DEV, D_MODEL, W_BLK), WIRE)],
        scratch_shapes=[pltpu.VMEM((N_DEV, D_MODEL, W_BLK), F32)],
        compiler_params=_params(1))(dt1, x, dzx, dzg, dzu, w_in)


TM2 = 512


def _dz_block(dzx_ref, dzg_ref, dzu_ref):
    dzg = dzg_ref[...]
    return jnp.concatenate([dzx_ref[...], dzg[:, :RG_W], dzu_ref[...], dzg[:, RG_W:]], axis=1).astype(MXU)


def _inproj_bwd_dw(x, dzx, dzg, dzu, token=None):
    L = x.shape[0]
    extra, extra_specs = _after(token)

    def body(x_ref, dzx_ref, dzg_ref, dzu_ref, *rest):
        dw_ref, acc_ref = rest[len(extra):]
        @pl.when(pl.program_id(0) == 0)
        def _():
            acc_ref[...] = jnp.zeros_like(acc_ref)
        dz = _dz_block(dzx_ref, dzg_ref, dzu_ref)
        xb = x_ref[...].astype(MXU)
        for j in range(N_DEV):
            acc_ref[j] += _mm_tn(xb, dz[:, j * W_BLK:(j + 1) * W_BLK])

        @pl.when(pl.program_id(0) == L // TM2 - 1)
        def _():
            dw_ref[...] = acc_ref[...].astype(WIRE)

    row = lambda w: pl.BlockSpec((TM2, w), lambda i: (i, 0))
    wspec = pl.BlockSpec((N_DEV, D_MODEL, W_BLK), lambda i: (0, 0, 0))
    return pl.pallas_call(
        body, name="inproj_bwd_dw", grid=(L // TM2,),
        in_specs=[row(D_MODEL), row(RG_W), row(D_MODEL), row(S5_W)] + extra_specs, out_specs=wspec,
        out_shape=_S((N_DEV, D_MODEL, W_BLK), WIRE), scratch_shapes=[pltpu.VMEM((N_DEV, D_MODEL, W_BLK), F32)],
        compiler_params=_params(1))(x, dzx, dzg, dzu, *extra)


def _inproj_bwd_dx(dt1, dzx, dzg, dzu, w_in, token=None):
    L = dt1.shape[0]
    extra, extra_specs = _after(token)

    def body(dt1_ref, dzx_ref, dzg_ref, dzu_ref, w_ref, *rest):
        rest[-1][...] = ALPHA * dt1_ref[...] + _mm_nt(_dz_block(dzx_ref, dzg_ref, dzu_ref), w_ref[...])

    tm = min(TM_MM, L)
    row = lambda w: pl.BlockSpec((tm, w), lambda i: (i, 0))
    return pl.pallas_call(
        body, name="inproj_bwd_dx", grid=(L // tm,),
        in_specs=[row(D_MODEL), row(RG_W), row(D_MODEL), row(S5_W), _full((D_MODEL, Z_W))] + extra_specs,
        out_specs=row(D_MODEL), out_shape=_S((L, D_MODEL)), compiler_params=_params(1))(dt1, dzx, dzg, dzu, w_in, *extra)


def _rg_specs(layer):
    tile = lambda rows: pl.BlockSpec((rows, LANE), lambda c: (0, c))
    ptile = lambda rows: pl.BlockSpec((None, rows, LANE), lambda c: (layer, 0, c))
    pheads = pl.BlockSpec((None, 2, RG_HD, RG_HD), lambda c: (layer, c, 0, 0))
    return tile, ptile, pheads, pl.BlockSpec((2, RG_HD, RG_HD), lambda c: (c, 0, 0))


RG_HD = 64


def _bd2(w):
    z = jnp.zeros((RG_HD, RG_HD), w.dtype)
    return jnp.concatenate([jnp.concatenate([w[0], z], axis=1), jnp.concatenate([z, w[1]], axis=1)], axis=0)


def _bd2_diag(m):
    return jnp.stack([m[:RG_HD, :RG_HD], m[RG_HD:, RG_HD:]])


def _rg_fwd(z, cw, cb, wa_bd, wx_bd, ba, bx, lam, layer):
    L = z.shape[0]
    RC = min(RC_RG, L)

    def body(x_ref, cw_ref, cb_ref, wa_ref, wx_ref, ba_ref, bx_ref, lam_ref, hs_ref, *saved):
        w, b = cw_ref[...], cb_ref[...]
        wa, wx, ba_, bx_ = _bd2(wa_ref[...]).astype(MXU), _bd2(wx_ref[...]).astype(MXU), ba_ref[...], bx_ref[...]
        sp = _softplus(-lam_ref[...])

        def step(c, carry):
            r0 = pl.multiple_of(c * RC, RC)
            xe = jnp.concatenate([_halo(x_ref, c, r0), x_ref[pl.ds(r0, RC), :]], axis=0)
            t = _conv_taps(xe)
            h = t[0] * w[0:1] + t[1] * w[1:2] + t[2] * w[2:3] + t[3] * w[3:4] + b
            r, i, a, mult = _rg_gates(h, wa, wx, ba_, bx_, sp)
            hs, carry = _scan_real(a, mult * (i * h), carry)
            hs_ref[pl.ds(r0, RC), :] = hs
            for ref, val in zip(saved, (h, r, i, a, mult)):
                ref[pl.ds(r0, RC), :] = val
            return carry

        lax.fori_loop(0, L // RC, step, jnp.zeros((1, LANE), F32))

    tile, ptile, pheads, _ = _rg_specs(layer)
    return pl.pallas_call(
        body, name="rg_fwd", grid=(N_RG_T,),
        in_specs=[tile(L), ptile(4), ptile(1), pheads, pheads, ptile(1), ptile(1), ptile(1)],
        out_specs=[tile(L)] * 6, out_shape=[_S((L, RG_W))] * 6, compiler_params=_params(1))(
            z, cw, cb, wa_bd, wx_bd, ba, bx, lam)


def _rg_bwd(dhs, z, hs, gates, cw, wa_bd, wx_bd, lam, layer):
    L = z.shape[0]
    RC = min(RC_RG, L)

    def body(g_ref, x_ref, hs_ref, h_ref, r_ref, i_ref, a_ref, mult_ref, cw_ref, wa_ref, wx_ref, lam_ref,
             dx_ref, dcw_ref, dcb_ref, dwa_out, dwx_out, dba_ref, dbx_ref, dlam_ref, dwa_ref, dwx_ref):
        w = cw_ref[...]
        wa, wx = _bd2(wa_ref[...]).astype(MXU), _bd2(wx_ref[...]).astype(MXU)
        lam = lam_ref[...]
        sp = _softplus(-lam)
        rows = lax.broadcasted_iota(jnp.int32, (RC, LANE), 0)
        for ref in (dcw_ref, dcb_ref, dwa_ref, dwx_ref, dba_ref, dbx_ref, dlam_ref):
            ref[...] = jnp.zeros_like(ref)
        nch = L // RC

        def step(k, carry):
            cin, nxt = carry
            c = nch - 1 - k
            r0 = pl.multiple_of(c * RC, RC)
            xe = jnp.concatenate([_halo(x_ref, c, r0), x_ref[pl.ds(r0, RC), :]], axis=0)
            t = _conv_taps(xe)
            h, r, i, a, mult = (ref[pl.ds(r0, RC), :] for ref in (h_ref, r_ref, i_ref, a_ref, mult_ref))
            hs_e = jnp.concatenate([_halo(hs_ref, c, r0), hs_ref[pl.ds(r0, RC), :]], axis=0)
            hs_prev = pltpu.roll(hs_e, 1, 0)[8:, :]
            g = g_ref[pl.ds(r0, RC), :]
            cc, cin_new = _scan_real(a, a * g, cin, reverse=True)
            dh = g + _up(cc, 1, rows, cin)
            ih = i * h
            dlog_a = dh * hs_prev * a - (dh * ih) * (a * a) / mult
            di = dh * mult * h
            dhin = dh * mult * i
            dr = dlog_a * ((-RG_C) * sp)
            dlam_ref[...] += _colsum(dlog_a * r)
            dra = dr * r * (1.0 - r)
            dia = di * i * (1.0 - i)
            dwa_ref[...] += _mm_tn(h, dra)
            dwx_ref[...] += _mm_tn(h, dia)
            dba_ref[...] += _colsum(dra)
            dbx_ref[...] += _colsum(dia)
            dhin = dhin + _mm_nt(dra, wa) + _mm_nt(dia, wx)
            de = jnp.concatenate([dhin, nxt], axis=0)
            n = RC + 8
            dx = (dhin * w[3:4] + pltpu.roll(de, n - 1, 0)[:RC, :] * w[2:3]
                  + pltpu.roll(de, n - 2, 0)[:RC, :] * w[1:2] + pltpu.roll(de, n - 3, 0)[:RC, :] * w[0:1])
            dx_ref[pl.ds(r0, RC), :] = dx
            for kk in range(4):
                dcw_ref[kk:kk + 1, :] += _colsum(dhin * t[kk])
            dcb_ref[...] += _colsum(dhin)
            return cin_new, dhin[0:8, :]

        lax.fori_loop(0, nch, step, (jnp.zeros((1, LANE), F32), jnp.zeros((8, LANE), F32)))
        dlam_ref[...] = dlam_ref[...] * (RG_C * _sigmoid(-lam))
        dwa_out[...], dwx_out[...] = _bd2_diag(dwa_ref[...]), _bd2_diag(dwx_ref[...])

    tile, ptile, pheads, gheads = _rg_specs(layer)
    heads = _S((2 * N_RG_T, RG_HD, RG_HD))
    return pl.pallas_call(
        body, name="rg_bwd", grid=(N_RG_T,),
        in_specs=[tile(L)] * 8 + [ptile(4), pheads, pheads, ptile(1)],
        out_specs=[tile(L), tile(4), tile(1), gheads, gheads, tile(1), tile(1), tile(1)],
        out_shape=[_S((L, RG_W)), _S((4, RG_W)), _S((1, RG_W)), heads, heads, _S((1, RG_W)), _S((1, RG_W)), _S((1, RG_W))],
        scratch_shapes=[pltpu.VMEM((LANE, LANE), F32), pltpu.VMEM((LANE, LANE), F32)],
        compiler_params=_params(1))(dhs, z, hs, *gates, cw, wa_bd, wx_bd, lam)


def _cmul(ar, ai, br, bi):
    return ar * br - ai * bi, ar * bi + ai * br


S5_TW = S5_N // N_S5_T


S5_H = 16
S5_GT = LANE // S5_H


def _s5_specs(L, layer):
    in_tile = pl.BlockSpec((L, LANE), lambda t: (0, t))
    st = pl.BlockSpec((L, S5_TW), lambda t: (0, t))
    pg = pl.BlockSpec((None, S5_GT, S5_H, S5_P), lambda t: (layer * N_S5_T + t, 0, 0, 0))
    plb = pl.BlockSpec((None, S5_GT, S5_P), lambda t: (layer * N_S5_T + t, 0, 0))
    gg = pl.BlockSpec((None, S5_GT, S5_H, S5_P), lambda t: (t, 0, 0, 0))
    glb = pl.BlockSpec((None, S5_GT, S5_P), lambda t: (t, 0, 0))
    dv = pl.BlockSpec((1, LANE), lambda t: (0, t))
    return in_tile, st, pg, plb, gg, glb, dv


def _bd8(blocks):
    rows = []
    for g in range(S5_GT):
        pieces = [blocks[g]]
        if g:
            pieces.insert(0, jnp.zeros((S5_H, S5_P * g), blocks.dtype))
        if g < S5_GT - 1:
            pieces.append(jnp.zeros((S5_H, S5_P * (S5_GT - 1 - g)), blocks.dtype))
        rows.append(jnp.concatenate(pieces, axis=1))
    return jnp.concatenate(rows, axis=0)


def _bd8_diag(m):
    return jnp.stack([m[S5_H * g:S5_H * (g + 1), S5_P * g:S5_P * (g + 1)] for g in range(S5_GT)])


def _row8(v):
    return jnp.concatenate([v[g:g + 1] for g in range(S5_GT)], axis=1)


def _row8_split(r):
    return jnp.concatenate([r[:, S5_P * g:S5_P * (g + 1)] for g in range(S5_GT)], axis=0)


def _layer_row_tile(layer):
    return pl.BlockSpec((None, 1, LANE), lambda t: (layer, 0, t))


def _s5_fwd(z, bb_re, bb_im, lb_re, lb_im, c_re, c_im, dvec, layer):
    L = z.shape[0]

    def body(u_ref, bbr_ref, bbi_ref, lr_ref, li_ref, cr_ref, ci_ref, d_ref, y_ref, sr_ref, si_ref):
        bbr, bbi = _bd8(bbr_ref[...]).astype(MXU), _bd8(bbi_ref[...]).astype(MXU)
        cr, ci = _bd8(cr_ref[...]).astype(MXU), _bd8(ci_ref[...]).astype(MXU)
        dv = d_ref[...]
        steps, e = _tile_powers(_row8(lr_ref[...]), _row8(li_ref[...]))

        def step(c, carry):
            r0 = pl.multiple_of(c * RC, RC)
            u = u_ref[pl.ds(r0, RC), :]
            ub = u.astype(MXU)
            sr = jnp.dot(ub, bbr, preferred_element_type=F32)
            si = jnp.dot(ub, bbi, preferred_element_type=F32)
            sr, si, carry = _scan_lti(sr, si, carry, steps, e)
            sr_ref[pl.ds(r0, RC), :] = sr
            si_ref[pl.ds(r0, RC), :] = si
            y_ref[pl.ds(r0, RC), :] = dv * u + (_mm_nt(sr, cr) - _mm_nt(si, ci))
            return carry

        zero = jnp.zeros((1, S5_TW), F32)
        lax.fori_loop(0, L // RC, step, (zero, zero))

    in_tile, st, pg, plb, _, _, _ = _s5_specs(L, layer)
    u_tile = pl.BlockSpec((L, LANE), lambda t: (0, C_S5U // LANE + t))
    return pl.pallas_call(
        body, name="s5_fwd", grid=(N_S5_T,),
        in_specs=[u_tile, pg, pg, plb, plb, pg, pg, _layer_row_tile(layer)],
        out_specs=[in_tile, st, st],
        out_shape=[_S((L, S5_W)), _S((L, S5_N)), _S((L, S5_N))],
        compiler_params=_params(1))(z, bb_re, bb_im, lb_re, lb_im, c_re, c_im, dvec)


def _s5_bwd(dy0, z, s_re, s_im, bb_re, bb_im, lb_re, lb_im, c_re, c_im, dvec, layer, token=None):
    L = z.shape[0]
    extra, extra_specs = _after(token)

    def body(dy_ref, u_ref, sr_ref, si_ref, bbr_ref, bbi_ref, lr_ref, li_ref, cr_ref, ci_ref, d_ref, *rest):
        (du_ref, dbbr_out, dbbi_out, dlr_out, dli_out, dcr_out, dci_out, dd_ref,
         dbbr_ref, dbbi_ref, dcr_ref, dci_ref, dlr_ref, dli_ref) = rest[len(extra):]
        bbr, bbi = _bd8(bbr_ref[...]).astype(MXU), _bd8(bbi_ref[...]).astype(MXU)
        cr, ci = _bd8(cr_ref[...]).astype(MXU), _bd8(ci_ref[...]).astype(MXU)
        lr, li = _row8(lr_ref[...]), -_row8(li_ref[...])
        dv = d_ref[...]
        steps, e = _tile_powers(lr, li, reverse=True)
        for ref in (dbbr_ref, dbbi_ref, dlr_ref, dli_ref, dcr_ref, dci_ref, dd_ref):
            ref[...] = jnp.zeros_like(ref)
        nch = L // RC

        def step(k, carry):
            c = nch - 1 - k
            r0 = pl.multiple_of(c * RC, RC)
            dy = dy_ref[pl.ds(r0, RC), :]
            u = u_ref[pl.ds(r0, RC), :]
            dyb, ub = dy.astype(MXU), u.astype(MXU)
            sr, si = sr_ref[pl.ds(r0, RC), :], si_ref[pl.ds(r0, RC), :]
            dcr_ref[...] += _mm_tn(dyb, sr)
            dci_ref[...] -= _mm_tn(dyb, si)
            gr = jnp.dot(dyb, cr, preferred_element_type=F32)
            gi = -jnp.dot(dyb, ci, preferred_element_type=F32)
            gr, gi, carry = _scan_lti(gr, gi, carry, steps, e, reverse=True)
            pr_ = pltpu.roll(jnp.concatenate([_halo(sr_ref, c, r0), sr], axis=0), 1, 0)[8:, :]
            pi_ = pltpu.roll(jnp.concatenate([_halo(si_ref, c, r0), si], axis=0), 1, 0)[8:, :]
            dlr_ref[...] += _colsum(pr_ * gr + pi_ * gi)
            dli_ref[...] += _colsum(pr_ * gi - pi_ * gr)
            grb, gib = gr.astype(MXU), gi.astype(MXU)
            dbbr_ref[...] += _mm_tn(ub, grb)
            dbbi_ref[...] += _mm_tn(ub, gib)
            du_ref[pl.ds(r0, RC), :] = dv * dy + (_mm_nt(grb, bbr) + _mm_nt(gib, bbi))
            dd_ref[...] += _colsum(dy * u)
            return carry

        zero = jnp.zeros((1, S5_TW), F32)
        lax.fori_loop(0, nch, step, (zero, zero))
        dbbr_out[...], dbbi_out[...] = _bd8_diag(dbbr_ref[...]), _bd8_diag(dbbi_ref[...])
        dcr_out[...], dci_out[...] = _bd8_diag(dcr_ref[...]), _bd8_diag(dci_ref[...])
        dlr_out[...], dli_out[...] = _row8_split(dlr_ref[...]), _row8_split(dli_ref[...])

    in_tile, st, pg, plb, gg, glb, dv = _s5_specs(L, layer)
    u_tile = pl.BlockSpec((L, LANE), lambda t: (0, C_S5U // LANE + t))
    groups, rows = _S((N_S5_T, S5_GT, S5_H, S5_P)), _S((N_S5_T, S5_GT, S5_P))
    wide = pltpu.VMEM((LANE, S5_TW), F32)
    return pl.pallas_call(
        body, name="s5_bwd", grid=(N_S5_T,),
        in_specs=[in_tile, u_tile, st, st, pg, pg, plb, plb, pg, pg, _layer_row_tile(layer)] + extra_specs,
        out_specs=[in_tile, gg, gg, glb, glb, gg, gg, dv],
        out_shape=[_S((L, S5_W)), groups, groups, rows, rows, groups, groups, _S((1, S5_W))],
        scratch_shapes=[wide, wide, wide, wide, pltpu.VMEM((1, S5_TW), F32), pltpu.VMEM((1, S5_TW), F32)],
        compiler_params=_params(1))(dy0, z, s_re, s_im, bb_re, bb_im, lb_re, lb_im, c_re, c_im, dvec, *extra)


def _disc(ar, ai, ls):
    dt = jnp.exp(ls)
    mag = jnp.exp(ar * dt)
    lr = mag * jnp.cos(ai * dt)
    li = mag * jnp.sin(ai * dt)
    den = ar * ar + ai * ai
    cr = ((lr - 1.0) * ar + li * ai) / den
    ci = (li * ar - (lr - 1.0) * ai) / den
    return lr, li, cr, ci


def _s5_disc_fwd(ar, ai, ls, token=None):
    extra, extra_specs = _after(token)

    def body(ar_ref, ai_ref, ls_ref, *rest):
        lr_ref, li_ref, cr_ref, ci_ref = rest[len(extra):]
        lr, li, cr, ci = _disc(ar_ref[...], ai_ref[...], ls_ref[...])
        lr_ref[...], li_ref[...], cr_ref[...], ci_ref[...] = lr, li, cr, ci

    sh = _S(ar.shape)
    vm = pl.BlockSpec(memory_space=pltpu.VMEM)
    return pl.pallas_call(body, name="s5_disc_fwd", in_specs=[vm, vm, vm] + extra_specs, out_shape=[sh, sh, sh, sh])(
        ar, ai, ls, *extra)


def _s5_disc_bwd(ar, ai, ls, dlr, dli, dcr, dci):
    def body(ar_ref, ai_ref, ls_ref, dlr_ref, dli_ref, dcr_ref, dci_ref, dar_ref, dai_ref, dls_ref):
        _, vjp = jax.vjp(_disc, ar_ref[...], ai_ref[...], jnp.broadcast_to(ls_ref[...], ar_ref.shape))
        dar, dai, dls = vjp((dlr_ref[...], dli_ref[...], dcr_ref[...], dci_ref[...]))
        dar_ref[...], dai_ref[...] = dar, dai
        dls_ref[...] = jnp.sum(dls, axis=1, keepdims=True)

    return pl.pallas_call(body, name="s5_disc_bwd", out_shape=[_S(ar.shape), _S(ar.shape), _S(ls.shape)])(
        ar, ai, ls, dlr, dli, dcr, dci)


def _s5_bscale_fwd(cr, ci, br, bi):
    def body(cr_ref, ci_ref, br_ref, bi_ref, or_ref, oi_ref):
        or_ref[...], oi_ref[...] = _cmul(cr_ref[...], ci_ref[...], br_ref[...], bi_ref[...])

    return pl.pallas_call(body, name="s5_bscale_fwd", out_shape=[_S(br.shape), _S(br.shape)])(cr, ci, br, bi)


def _s5_bscale_bwd(cr, ci, br, bi, gr, gi):
    def body(cr_ref, ci_ref, br_ref, bi_ref, gr_ref, gi_ref, dbr_ref, dbi_ref, dcr_ref, dci_ref):
        cr_, ci_, br_, bi_, gr_, gi_ = (r[...] for r in (cr_ref, ci_ref, br_ref, bi_ref, gr_ref, gi_ref))
        dbr_ref[...] = cr_ * gr_ + ci_ * gi_
        dbi_ref[...] = cr_ * gi_ - ci_ * gr_
        dcr_ref[...] = jnp.sum(gr_ * br_ + gi_ * bi_, axis=1, keepdims=True)
        dci_ref[...] = jnp.sum(gi_ * br_ - gr_ * bi_, axis=1, keepdims=True)

    return pl.pallas_call(body, name="s5_bscale_bwd",
                          out_shape=[_S(br.shape), _S(br.shape), _S(cr.shape), _S(cr.shape)])(cr, ci, br, bi, gr, gi)


def _row(w):
    return pl.BlockSpec((TM, w), lambda i: (i, 0))


def _full(shape):
    return pl.BlockSpec(tuple(shape), lambda i: (0,) * len(shape))


def _gate_rows():
    return [pl.BlockSpec((TM, RG_W), lambda i: (i, C_RGG // RG_W))] + [
        pl.BlockSpec((TM, LANE), lambda i, k=k: (i, C_S5G // LANE + k)) for k in range(N_S5_T)]


def _p_rows(layer):
    return pl.BlockSpec((None, None, TM, PLE_D), lambda i: (layer, 0, i, 0))


def _lrow(layer, width):
    return pl.BlockSpec((None, 1, width), lambda i: (layer, 0, 0))


def _post_fwd(x, hs, z, y0, p, w_glu, b_glu, w_out, g1, b1, ple_w, w_pg, b_pg, g2, b2, layer):
    L = x.shape[0]

    def body(x_ref, hs_ref, zg_ref, zs0_ref, zs1_ref, zs2_ref, y0_ref, p_ref, wg_ref, bg_ref, wo_ref, g1_ref, b1_ref, pw_ref,
             wpg_ref, bpg_ref, g2_ref, b2_ref, x2_ref, xh1_ref, xh2_ref, q_ref, gt_ref, rstd1_ref, rstd2_ref):
        rg_gate = zg_ref[...]
        s5_gate = jnp.concatenate([zs0_ref[...], zs1_ref[...], zs2_ref[...]], axis=1)
        rg_y = hs_ref[...] * _silu_and_grad(rg_gate)[0]
        y1 = _gelu(y0_ref[...])
        gl = _sigmoid(_mm(y1, wg_ref[...]) + bg_ref[...])
        s5_y = (y1 * gl) * _silu_and_grad(s5_gate)[0]
        mix = _mm(jnp.concatenate([rg_y.astype(MXU), s5_y.astype(MXU)], axis=1), wo_ref[...])
        t1 = ALPHA * x_ref[...] + mix
        x1, xh1, rstd1 = _ln_fwd(t1, g1_ref[...], b1_ref[...])
        q = _mm(p_ref[...], pw_ref[...])
        gt = _sigmoid(_mm(x1, wpg_ref[...]) + bpg_ref[...])
        t2 = ALPHA * x1 + q * gt
        x2, xh2, rstd2 = _ln_fwd(t2, g2_ref[...], b2_ref[...])
        x2_ref[...], xh1_ref[...], xh2_ref[...], q_ref[...], gt_ref[...] = x2, xh1, xh2, q, gt
        rstd1_ref[...], rstd2_ref[...] = rstd1, rstd2

    vec = _lrow(layer, D_MODEL)
    return pl.pallas_call(
        body, name="post_fwd", grid=(L // TM,),
        in_specs=[_row(D_MODEL), _row(RG_W), *_gate_rows(), _row(S5_W), _p_rows(layer), _full((S5_W, S5_W)),
                  _lrow(layer, S5_W), _full((D_MODEL, D_MODEL)), vec, vec, _full((PLE_D, D_MODEL)), _full((D_MODEL, D_MODEL)),
                  vec, vec, vec],
        out_specs=[_row(D_MODEL)] * 5 + [_row(1)] * 2, out_shape=[_S((L, D_MODEL))] * 5 + [_S((L, 1))] * 2,
        compiler_params=_params(1))(x, hs, z, z, z, z, y0, p, w_glu, b_glu, w_out, g1, b1, ple_w, w_pg, b_pg, g2, b2)


def _post_bwd_a(dx2_or_target, is_top, xh2, xh1, rstd2, rstd1, q, gt, p, w_pg, g1, b1, g2, b2, layer, token=None):
    L = xh1.shape[0]
    extra, extra_specs = _after(token)

    def body(d_ref, xh2_ref, xh1_ref, rstd2_ref, rstd1_ref, q_ref, gt_ref, p_ref, wpg_ref, g1_ref, b1_ref, g2_ref,
             b2_ref, *rest):
        (dt1_ref, dpw_out, dwpg_out, dbpg_ref, dg1_ref, db1_ref, dg2_ref, db2_ref, loss_ref, dpw_ref,
         dwpg_ref) = rest[len(extra):]
        @pl.when(pl.program_id(0) == 0)
        def _():
            for ref in (dpw_ref, dwpg_ref, dbpg_ref, dg1_ref, db1_ref, dg2_ref, db2_ref, loss_ref):
                ref[...] = jnp.zeros_like(ref)

        g1, g2 = g1_ref[...], g2_ref[...]
        xh1, xh2, rstd1, rstd2 = xh1_ref[...], xh2_ref[...], rstd1_ref[...], rstd2_ref[...]
        x1 = xh1 * g1 + b1_ref[...]
        if is_top:
            err = (xh2 * g2 + b2_ref[...]) - d_ref[...]
            loss_ref[...] += _colsum(err * err)
            dx2 = err * (1.0 / D_MODEL)
        else:
            dx2 = d_ref[...]
        p = p_ref[...]
        q, gt = q_ref[...], gt_ref[...]
        dg2_ref[...] += _colsum(dx2 * xh2)
        db2_ref[...] += _colsum(dx2)
        dt2 = _ln_bwd(dx2, xh2, rstd2, g2)
        dq = dt2 * gt
        dgpre = (dt2 * q) * gt * (1.0 - gt)
        dpw_ref[...] += _mm_tn(p, dq)
        dwpg_ref[...] += _mm_tn(x1, dgpre)
        dbpg_ref[...] += _colsum(dgpre)
        dx1 = ALPHA * dt2 + _mm_nt(dgpre, wpg_ref[...])
        dg1_ref[...] += _colsum(dx1 * xh1)
        db1_ref[...] += _colsum(dx1)
        dt1_ref[...] = _ln_bwd(dx1, xh1, rstd1, g1)

        @pl.when(pl.program_id(0) == L // TM - 1)
        def _():
            dpw_out[...] = dpw_ref[...].astype(WIRE)
            dwpg_out[...] = dwpg_ref[...].astype(WIRE)

    vec, lvec = _full((1, D_MODEL)), _lrow(layer, D_MODEL)
    return pl.pallas_call(
        body, name="post_bwd_a_top" if is_top else "post_bwd_a", grid=(L // TM,),
        in_specs=[_row(D_MODEL), _row(D_MODEL), _row(D_MODEL), _row(1), _row(1), _row(D_MODEL), _row(D_MODEL), _p_rows(layer),
                  _full((D_MODEL, D_MODEL)), lvec, lvec, lvec, lvec] + extra_specs,
        out_specs=[_row(D_MODEL), _full((PLE_D, D_MODEL)), _full((D_MODEL, D_MODEL)), vec, vec, vec, vec, vec, vec],
        out_shape=[_S((L, D_MODEL)), _S((PLE_D, D_MODEL), WIRE), _S((D_MODEL, D_MODEL), WIRE)] + [_S((1, D_MODEL))] * 6,
        scratch_shapes=[pltpu.VMEM((PLE_D, D_MODEL), F32), pltpu.VMEM((D_MODEL, D_MODEL), F32)],
        compiler_params=_params(1))(dx2_or_target, xh2, xh1, rstd2, rstd1, q, gt, p, w_pg, g1, b1, g2, b2, *extra)


def _post_bwd_b(dt1, z, hs, y0, w_out, w_glu, b_glu, layer):
    L = dt1.shape[0]

    def body(dt1_ref, zg_ref, zs0_ref, zs1_ref, zs2_ref, hs_ref, y0_ref, wo_ref, wg_ref, bg_ref,
             dhs_ref, dy0_ref, dzg_ref, dwo_out, dwg_out, dbg_ref, dwo_ref, dwg_ref):
        @pl.when(pl.program_id(0) == 0)
        def _():
            for ref in (dwo_ref, dwg_ref, dbg_ref):
                ref[...] = jnp.zeros_like(ref)

        dt1b = dt1_ref[...].astype(MXU)
        dm = _mm_nt(dt1b, wo_ref[...])
        d_rgy, d_s5y = dm[:, :RG_W], dm[:, RG_W:]
        rg_gate = zg_ref[...]
        s5_gate = jnp.concatenate([zs0_ref[...], zs1_ref[...], zs2_ref[...]], axis=1)
        hs = hs_ref[...]
        sl, dsl = _silu_and_grad(rg_gate)
        dhs_ref[...] = d_rgy * sl
        dzg_ref[:, :RG_W] = d_rgy * hs * dsl
        y0 = y0_ref[...]
        y1 = _gelu(y0)
        gl = _sigmoid(_mm(y1, wg_ref[...]) + bg_ref[...])
        y2 = y1 * gl
        sl2, dsl = _silu_and_grad(s5_gate)
        m = jnp.concatenate([(hs * sl).astype(MXU), (y2 * sl2).astype(MXU)], axis=1)
        dwo_ref[...] += _mm_tn(m, dt1b)
        dy2 = d_s5y * sl2
        dzg_ref[:, RG_W:] = d_s5y * y2 * dsl
        dglpre = (dy2 * y1) * gl * (1.0 - gl)
        dwg_ref[...] += _mm_tn(y1, dglpre)
        dbg_ref[...] += _colsum(dglpre)
        dy1 = dy2 * gl + _mm_nt(dglpre, wg_ref[...])
        dy0_ref[...] = dy1 * _gelu_grad(y0)

        @pl.when(pl.program_id(0) == L // TM - 1)
        def _():
            dwo_out[...] = dwo_ref[...].astype(WIRE)
            dwg_out[...] = dwg_ref[...].astype(WIRE)

    return pl.pallas_call(
        body, name="post_bwd_b", grid=(L // TM,),
        in_specs=[_row(D_MODEL), *_gate_rows(), _row(RG_W), _row(S5_W), _full((D_MODEL, D_MODEL)),
                  _full((S5_W, S5_W)), _lrow(layer, S5_W)],
        out_specs=[_row(RG_W), _row(S5_W), _row(D_MODEL), _full((D_MODEL, D_MODEL)), _full((S5_W, S5_W)), _full((1, S5_W))],
        out_shape=[_S((L, RG_W)), _S((L, S5_W)), _S((L, D_MODEL)), _S((D_MODEL, D_MODEL), WIRE), _S((S5_W, S5_W), WIRE),
                   _S((1, S5_W))],
        scratch_shapes=[pltpu.VMEM((D_MODEL, D_MODEL), F32), pltpu.VMEM((S5_W, S5_W), F32)],
        compiler_params=_params(1))(dt1, z, z, z, z, hs, y0, w_out, w_glu, b_glu)


def _adamw(parts, w, m, v, token=None):
    nl = len(parts)
    extra, extra_specs = _after(token)
    n, R, C = parts[0].shape
    tr = R
    for cand in (512, 256, 128, 64, 32, 16, 8):
        if R % cand == 0 and n * cand * C * 4 <= 4 * 1024 * 1024:
            tr = cand
            break
    nblk = R // tr

    def body(*refs):
        p_refs = refs[:nl]
        w_ref, m_ref, v_ref = refs[nl:nl + 3]
        g_ref, d_ref, nm_ref, nv_ref = refs[nl + 3 + len(extra):]
        layer = pl.program_id(0)
        g = None
        for li, p_ref in enumerate(p_refs):
            s = p_ref[0].astype(F32)
            for k in range(1, n):
                s = s + p_ref[k].astype(F32)
            g = s if g is None else jnp.where(layer == li, s, g)
        nm = B1 * m_ref[...] + (1.0 - B1) * g
        nv = B2 * v_ref[...] + (1.0 - B2) * (g * g)
        d_ref[...] = (-LR) * ((nm / BC1) / (jnp.sqrt(nv / BC2) + EPS) + WD * w_ref[...])
        g_ref[...], nm_ref[...], nv_ref[...] = g, nm, nv

    def part_spec(li):
        return pl.BlockSpec((n, tr, C), lambda l, i: (0, jnp.where(l == li, i, jnp.where(l < li, 0, nblk - 1)), 0))

    blk = pl.BlockSpec((tr, C), lambda l, i: (l * nblk + i, 0))
    return pl.pallas_call(
        body, name="adamw", grid=(nl, nblk),
        in_specs=[part_spec(li) for li in range(nl)] + [blk, blk, blk] + extra_specs,
        out_specs=[blk] * 4, out_shape=[_S((nl * R, C))] * 4, compiler_params=_params(2))(*parts, w, m, v, *extra)


def _adamw_natural(names, g, w, m, v, name):
    n = len(names)

    def body(*refs):
        for j in range(n):
            g_ref, w_ref, m_ref, v_ref, d_ref, nm_ref, nv_ref = (refs[k * n + j] for k in range(7))
            gj = g_ref[...]
            nm = B1 * m_ref[...] + (1.0 - B1) * gj
            nv = B2 * v_ref[...] + (1.0 - B2) * (gj * gj)
            d_ref[...] = (-LR) * ((nm / BC1) / (jnp.sqrt(nv / BC2) + EPS) + WD * w_ref[...])
            nm_ref[...], nv_ref[...] = nm, nv

    ins = [t[k] for t in (g, w, m, v) for k in names]
    outs = pl.pallas_call(body, name=name, out_shape=[_S(w[k].shape) for _ in range(3) for k in names],
                          compiler_params=pltpu.CompilerParams(vmem_limit_bytes=VMEM_LIMIT))(*ins)
    return [{k: outs[t * n + j] for j, k in enumerate(names)} for t in range(3)]


def _me():
    return lax.axis_index("x"), lax.axis_index("y"), lax.axis_index("c")


def _lin(dev):
    return 4 * dev[0] + 2 * dev[1] + dev[2]


def _blk(ref, axis, size, idx):
    nd = len(ref.shape)
    start = idx * size
    if axis == nd - 1 and size % LANE == 0:
        start = pl.multiple_of(start, LANE)
    elif axis == nd - 2 and size % 16 == 0:
        start = pl.multiple_of(start, 16)
    ix = [slice(None)] * nd
    ix[axis] = pl.ds(start, size)
    return ref.at[tuple(ix)]


HBM_SPEC = pl.BlockSpec(memory_space=pltpu.HBM)
SEM_SPEC = pl.BlockSpec(memory_space=pltpu.SEMAPHORE)
EFFECT = pltpu.SideEffectType.DATAFLOW_SIDE_EFFECTING


def _peers(x, y, c):
    flip = lambda v, f: 1 - v if f else v
    return [(flip(x, k & 4), flip(y, k & 2), flip(c, k & 1)) for k in range(1, N_DEV)]


def _land_shape(mode, s, axis):
    if mode == "gather":
        return s.shape[:axis] + (N_DEV * s.shape[axis],) + s.shape[axis + 1:]
    return (N_DEV,) + s.shape[:axis] + (s.shape[axis] // N_DEV,) + s.shape[axis + 1:]


def _src_view(mode, ref, axis, peer):
    return ref if mode == "gather" else _blk(ref, axis, ref.shape[axis] // N_DEV, peer)


def _dst_view(mode, land, axis, sender):
    return _blk(land, axis, land.shape[axis] // N_DEV, sender) if mode == "gather" else land.at[sender]


def _blocks(mode, land, axis, k):
    if mode == "gather":
        ix = [slice(None)] * len(land.shape)
        ix[axis] = pl.ds(0, k * (land.shape[axis] // N_DEV))
        return land.at[tuple(ix)]
    return land.at[pl.ds(0, k)]


ARRIVALS = {None: N_DEV - 1, "near": 4, "relay": 3}


def _routes(route, x, y, c):
    me, sibling = (x, y, c), (x, y, 1 - c)
    chips = [(1 - x, y), (x, 1 - y), (1 - x, 1 - y)]
    if route == "near":
        return [(me, sibling)] + [(me, (*chip, c)) for chip in chips]
    if route == "relay":
        return [((*chip, c), sibling) for chip in chips]
    return [(me, peer) for peer in _peers(x, y, c)]


def _place_own(mode, srcs, axes, name, after=None):
    n = len(srcs)
    extra, extra_specs = _after(after)

    def body(me_ref, *refs):
        for a in range(n):
            out = refs[n + len(extra) + a]
            out[...] = refs[a][...].reshape(out.shape)

    def at_me(shape, axis):
        return lambda i, me: tuple(me[0] if d == axis else 0 for d in range(len(shape)))

    in_specs, out_specs = [], []
    for s, axis in zip(srcs, axes):
        if mode == "gather":
            in_specs.append(pl.BlockSpec(s.shape, lambda i, me, nd=len(s.shape): (0,) * nd))
            out_specs.append(pl.BlockSpec(s.shape, at_me(s.shape, axis)))
        else:
            blk = s.shape[:axis] + (s.shape[axis] // N_DEV,) + s.shape[axis + 1:]
            in_specs.append(pl.BlockSpec(blk, at_me(blk, axis)))
            out_specs.append(pl.BlockSpec((1,) + blk, at_me((1,) + blk, 0)))
    me = _lin(_me()).astype(jnp.int32).reshape(1)
    return pl.pallas_call(
        body, name=name, out_shape=[_S(_land_shape(mode, s, a), s.dtype) for s, a in zip(srcs, axes)],
        grid_spec=pltpu.PrefetchScalarGridSpec(num_scalar_prefetch=1, grid=(1,), in_specs=in_specs + extra_specs,
                                               out_specs=out_specs),
        compiler_params=_params(1))(me, *srcs, *extra)


def _place_shards(shards, layers, axes, dtypes, name, after=None):
    n = len(shards)
    extra, extra_specs = _after(after)

    def body(me_ref, *refs):
        for a in range(n):
            out = refs[n + len(extra) + a]
            out[...] = refs[a][...].astype(out.dtype)

    in_specs, out_specs, out_shape = [], [], []
    for s, layer, axis, dt in zip(shards, layers, axes, dtypes):
        shape = s.shape if layer is None else s.shape[1:]
        nd = len(shape)
        if layer is None:
            in_specs.append(pl.BlockSpec(shape, lambda i, me, nd=nd: (0,) * nd))
        else:
            in_specs.append(pl.BlockSpec((None,) + shape, lambda i, me, nd=nd, layer=layer: (layer,) + (0,) * nd))
        out_specs.append(pl.BlockSpec(shape, lambda i, me, nd=nd, axis=axis: tuple(me[0] if d == axis else 0 for d in range(nd))))
        out_shape.append(_S(shape[:axis] + (N_DEV * shape[axis],) + shape[axis + 1:], dt))
    me = _lin(_me()).astype(jnp.int32).reshape(1)
    return pl.pallas_call(
        body, name=name, out_shape=out_shape,
        grid_spec=pltpu.PrefetchScalarGridSpec(num_scalar_prefetch=1, grid=(1,), in_specs=in_specs + extra_specs,
                                               out_specs=out_specs),
        compiler_params=_params(1))(me, *shards, *extra)


def _push_start(mode, srcs, lands, axes, name, route=None):
    n, ns = len(lands), len(srcs)

    def body(*refs):
        src_refs, land_refs = refs[:ns], refs[ns:ns + n]
        send_sems, recv_sems = refs[ns + n], refs[ns + n + 1]
        token = refs[-1]
        x, y, c = _me()
        for a in range(n):
            for block, peer in _routes(route, x, y, c):
                there = _dst_view(mode, land_refs[a], axes[a], _lin(block))
                pltpu.make_async_remote_copy(
                    src_ref=_src_view(mode, src_refs[a], axes[a], _lin(peer)) if ns else there, dst_ref=there,
                    send_sem=send_sems.at[a], recv_sem=recv_sems.at[a], device_id=peer, device_id_type=MESH).start()
        token[...] = jnp.zeros_like(token)

    hbm = lambda s: pltpu.HBM(s.shape, s.dtype)
    outs = pl.pallas_call(
        body, name=name,
        out_shape=(pltpu.SemaphoreType.DMA((n,)), pltpu.SemaphoreType.DMA((n,)), *[hbm(s) for s in srcs], *[hbm(s) for s in lands],
                   _S((SUB, LANE))),
        in_specs=[HBM_SPEC] * (ns + n),
        out_specs=(SEM_SPEC, SEM_SPEC, *[HBM_SPEC] * (ns + n), pl.BlockSpec(memory_space=pltpu.VMEM)),
        input_output_aliases={i: 2 + i for i in range(ns + n)},
        compiler_params=pltpu.CompilerParams(has_side_effects=EFFECT),
    )(*[pltpu.with_memory_space_constraint(s, pltpu.HBM) for s in list(srcs) + list(lands)])
    return outs[0], outs[1], outs[2:2 + ns], outs[2 + ns:2 + ns + n], outs[-1]


def _push_wait(mode, send_sems, recv_sems, srcs, lands, axes, after, name, first=0, route=None):
    n, ns = len(lands), len(srcs)
    after = list(after) if isinstance(after, (list, tuple)) else [after]

    def body(*refs):
        land_refs = refs[ns:ns + n]
        send_sems, recv_sems = refs[ns + n], refs[ns + n + 1]
        x, y, c = _me()
        for a in range(n):
            seven = _blocks(mode, land_refs[a], axes[a], ARRIVALS[route])
            cp = pltpu.make_async_remote_copy(src_ref=seven, dst_ref=seven, send_sem=send_sems.at[first + a],
                                              recv_sem=recv_sems.at[first + a],
                                              device_id=(x, y, 1 - c), device_id_type=MESH)
            cp.wait_send()
            cp.wait_recv()

    hbm = lambda s: pltpu.HBM(s.shape, s.dtype)
    outs = pl.pallas_call(
        body, name=name, out_shape=tuple(hbm(s) for s in list(srcs) + list(lands)),
        in_specs=[HBM_SPEC] * (ns + n) + [SEM_SPEC, SEM_SPEC] + [ANY] * len(after), out_specs=tuple([HBM_SPEC] * (ns + n)),
        input_output_aliases={i: i for i in range(ns + n)},
        compiler_params=pltpu.CompilerParams(has_side_effects=EFFECT),
    )(*srcs, *lands, send_sems, recv_sems, *after)
    return outs[ns:]


def _sum_parts(parts):
    n, R, C = parts.shape

    def body(p_ref, o_ref):
        g = p_ref[0]
        for k in range(1, n):
            g = g + p_ref[k]
        o_ref[...] = g

    return pl.pallas_call(body, name="sum_parts", out_shape=_S((R, C)))(parts)


SMALL =['conv_b', 'rg_wa', 'rg_ba', 'rg_wx', 'rg_bx', 'rg_lambda', 's5_a_re', 's5_a_im', 's5_b_re', 's5_b_im',
         's5_c_re', 's5_c_im', 's5_d', 's5_log_step', 's5_b_glu', 'ln1_g', 'ln1_b', 'ple_gate_b', 'ln2_g', 'ln2_b']
WEIGHTS = ['w_in', 'conv_w', 'conv_b', 'rg_wa', 'rg_ba', 'rg_wx', 'rg_bx', 'rg_lambda', 's5_a_re', 's5_a_im', 's5_b_re',
           's5_b_im', 's5_c_re', 's5_c_im', 's5_d', 's5_log_step', 's5_w_glu', 's5_b_glu', 'w_out', 'ln1_g', 'ln1_b',
           'ple_w', 'ple_gate_w', 'ple_gate_b', 'ln2_g', 'ln2_b']
PACK_ROWS_MULT = 64


STORED = {'s5_b_re': (2, 3), 's5_b_im': (2, 3), 's5_d': (1, 2)}


def _stored(k, a):
    return jnp.swapaxes(a, *STORED[k]) if k in STORED else a


def _tile_rows(n):
    return -(-n // (SUB * LANE)) * SUB


def _pack(tree, scalar):
    parts = []
    for a in [tree[k] for k in SMALL] + [scalar.reshape(1)]:
        rows = _tile_rows(a.size)
        parts.append(jnp.pad(a.reshape(-1), (0, rows * LANE - a.size)).reshape(rows, LANE))
    rows = sum(p.shape[0] for p in parts)
    parts.append(jnp.zeros((-rows % PACK_ROWS_MULT, LANE), F32))
    return jnp.concatenate(parts, axis=0)


def _unpack(packed, like):
    out, r = {}, 0
    for k in SMALL:
        n = math.prod(like[k].shape)
        rows = _tile_rows(n)
        part = packed[r:r + rows]
        out[k] = (part if n == rows * LANE else part.reshape(-1)[:n]).reshape(like[k].shape)
        r += rows
    return out, packed[r, 0]


class _NoHooks:
    token = None
    first_token = None

    def first_weights(self, full, after):
        return full

    def layer_start(self, i, W, after):
        return W

    def late_weights(self, i, W, after):
        return W

    def post_done(self, i, g):
        return None

    def smalls_done(self, grads, loss):
        self.small = _small_grads(grads, self.res)
        return None

    def w_in_done(self, i, g):
        return None

    def layer_done(self, i, g, dx):
        return None


def _local_grads(x, p, target, W, disc, hooks):
    depth = 2
    saved = []
    for i in range(depth):
        if i > 0:
            W = hooks.layer_start(i, W, x)
        w = W[i]
        z = _inproj_fwd(x, w['w_in'], hooks.token if i == 0 else None)
        hs, *gates = _rg_fwd(z, w['conv_w'], w['conv_b'], w['wa_bd'], w['wx_bd'], w['rg_ba'], w['rg_bx'], w['rg_lambda'], i)
        d = disc[i]
        y0, s_re, s_im = _s5_fwd(z, d['bb_re'], d['bb_im'], d['lb_re'], d['lb_im'], d['c_re'], d['c_im'], w['s5_d'], i)
        W = hooks.late_weights(i, W, y0)
        w = W[i]
        x2, *norms = _post_fwd(x, hs, z, y0, p, w['s5_w_glu'], w['s5_b_glu'], w['w_out'], w['ln1_g'], w['ln1_b'],
                               w['ple_w'], w['ple_gate_w'], w['ple_gate_b'], w['ln2_g'], w['ln2_b'], i)
        saved.append((x, z, hs, gates, y0, s_re, s_im, norms))
        x = x2

    grads = [None] * depth
    dx = target
    loss = None
    token = None
    for i in reversed(range(depth)):
        w, d = W[i], disc[i]
        xin, z, hs, gates, y0, s_re, s_im, (xh1, xh2, q, gt, rstd1, rstd2) = saved[i]
        g = {}
        (dt1, g['ple_w'], g['ple_gate_w'], g['ple_gate_b'], g['ln1_g'], g['ln1_b'], g['ln2_g'], g['ln2_b'], lrow) = _post_bwd_a(
            dx, i == depth - 1, xh2, xh1, rstd2, rstd1, q, gt, p, w['ple_gate_w'], w['ln1_g'], w['ln1_b'],
            w['ln2_g'], w['ln2_b'], i, token)
        if i == depth - 1:
            loss = 0.5 / D_MODEL * jnp.sum(lrow)
        dhs, dy0, dzg, g['w_out'], g['s5_w_glu'], g['s5_b_glu'] = _post_bwd_b(dt1, z, hs, y0, w['w_out'], w['s5_w_glu'],
                                                                           w['s5_b_glu'], i)
        (dzu, g['bb_re'], g['bb_im'], g['lb_re'], g['lb_im'], g['c_re'], g['c_im'], g['s5_d']) = _s5_bwd(
            dy0, z, s_re, s_im, d['bb_re'], d['bb_im'], d['lb_re'], d['lb_im'], d['c_re'], d['c_im'], w['s5_d'], i,
            hooks.post_done(i, g))
        (dzx, g['conv_w'], g['conv_b'], g['wa_bd'], g['wx_bd'], g['rg_ba'], g['rg_bx'], g['rg_lambda']) = _rg_bwd(
            dhs, z, hs, gates, w['conv_w'], w['wa_bd'], w['wx_bd'], w['rg_lambda'], i)
        if i == 0:
            g['w_in'] = _inproj_bwd_dw(xin, dzx, dzg, dzu, hooks.smalls_done([g, grads[1]], loss))
            dx = _inproj_bwd_dx(dt1, dzx, dzg, dzu, w['w_in'], hooks.w_in_done(i, g))
        else:
            dx, g['w_in'] = _inproj_bwd(dt1, xin, dzx, dzg, dzu, w['w_in'])
        grads[i] = g
        token = hooks.layer_done(i, g, dx)
    return loss, dx, grads


def _s5_layouts_fwd(s5_a_re, s5_a_im, s5_log_step, s5_b_re, s5_b_im, s5_c_re, s5_c_im, token=None):
    depth = s5_a_re.shape[0]
    ar, ai = s5_a_re.reshape(depth * 24, S5_P), s5_a_im.reshape(depth * 24, S5_P)
    ls = s5_log_step.reshape(depth * 24, 1)
    lr, li, cr, ci = _s5_disc_fwd(ar, ai, ls, token)
    per_group = lambda a: a.reshape(depth * 24, 1, S5_P)
    as_c = lambda b: jnp.swapaxes(b, 2, 3).reshape(depth * 24, S5_H, S5_P)
    res = (ar, ai, ls, per_group(cr), per_group(ci), as_c(s5_b_re), as_c(s5_b_im))
    bbr, bbi = _s5_bscale_fwd(*res[3:])
    tiles = lambda a: a.reshape(depth * N_S5_T, S5_GT, S5_H, S5_P)
    rows = lambda a: a.reshape(depth * N_S5_T, S5_GT, S5_P)
    disc = dict(bb_re=tiles(bbr), bb_im=tiles(bbi), lb_re=rows(lr), lb_im=rows(li), c_re=tiles(s5_c_re), c_im=tiles(s5_c_im))
    return [disc] * depth, res


def _s5_layouts_bwd(grads, res):
    ar, ai, ls, cr, ci, br, bi = res
    depth = len(grads)
    stack = lambda k, shape: jnp.stack([g[k] for g in grads]).reshape(shape)
    groups, shape_c = (depth * 24, S5_H, S5_P), (depth, 24, S5_H, S5_P)
    dbr, dbi, dcr, dci = _s5_bscale_bwd(cr, ci, br, bi, stack('bb_re', groups), stack('bb_im', groups))
    gp = (depth * 24, S5_P)
    dar, dai, dls = _s5_disc_bwd(ar, ai, ls, stack('lb_re', gp), stack('lb_im', gp), dcr.reshape(gp), dci.reshape(gp))
    return dict(
        s5_a_re=dar.reshape(depth, 24, S5_P), s5_a_im=dai.reshape(depth, 24, S5_P), s5_log_step=dls.reshape(depth, 24),
        s5_b_re=dbr.reshape(shape_c), s5_b_im=dbi.reshape(shape_c),
        s5_c_re=stack('c_re', shape_c), s5_c_im=stack('c_im', shape_c))


LATE = ('w_out', 'ple_w', 'ple_gate_w', 's5_w_glu')


ROWS = ('conv_b', 'rg_ba', 'rg_bx', 'rg_lambda', 's5_d', 's5_b_glu', 'ln1_g', 'ln1_b', 'ple_gate_b', 'ln2_g', 'ln2_b')


def _shared_weights(full):
    depth = full['conv_b'].shape[0]
    shared = {k: full[k].reshape(depth, 1, -1) for k in ROWS}
    shared.update(conv_w=full['conv_w'], wa_bd=full['rg_wa'], wx_bd=full['rg_wx'])
    return shared


def _layer_weights(full, shared, i):
    return dict(shared, w_in=full['w_in'][i])


class _AllLocal(_NoHooks):
    def __init__(self, full):
        self.full = full

    def late_weights(self, i, W, after):
        W[i].update({k: self.full[k][i] for k in LATE})
        return W


def _full_grads(full, x, p, target, hooks=None):
    hooks = hooks or _AllLocal(full)
    disc, res = _s5_layouts_fwd(full['s5_a_re'], full['s5_a_im'], full['s5_log_step'], full['s5_b_re'], full['s5_b_im'],
                                full['s5_c_re'], full['s5_c_im'], hooks.first_token)
    full = hooks.first_weights(full, disc[-1]['bb_im'])
    shared = _shared_weights(full)
    W = [_layer_weights(full, shared, i) for i in range(2)]
    hooks.res = res
    loss, gx, grads = _local_grads(x, p, target, W, disc, hooks)
    out = dict(hooks.small)
    for k in SHARD_AXIS:
        out[k] = [g[k] for g in grads]
    return loss, gx, out


def _small_grads(grads, res):
    stack = lambda f: jnp.stack([f(g) for g in grads])
    out = _s5_layouts_bwd(grads, res)
    out['conv_w'] = stack(lambda g: g['conv_w'])
    for k in ('conv_b', 'rg_ba', 'rg_bx', 'rg_lambda', 's5_b_glu', 'ln1_g', 'ln1_b', 'ple_gate_b', 'ln2_g', 'ln2_b'):
        out[k] = stack(lambda g: g[k][0])
    out['s5_d'] = _stored('s5_d', stack(lambda g: g['s5_d'][0]).reshape(2, 24, 16))
    out['rg_wa'] = stack(lambda g: g['wa_bd'])
    out['rg_wx'] = stack(lambda g: g['wx_bd'])
    return out


SHARD_AXIS = {'w_in': 2, 'w_out': 1, 'ple_w': 2, 'ple_gate_w': 1, 's5_w_glu': 1}


def kernel(x, p, w_in, conv_w, conv_b, rg_wa, rg_ba, rg_wx, rg_bx, rg_lambda, s5_a_re, s5_a_im, s5_b_re, s5_b_im, s5_c_re, s5_c_im, s5_d, s5_log_step, s5_w_glu, s5_b_glu, w_out, ln1_g, ln1_b, ple_w, ple_gate_w, ple_gate_b, ln2_g, ln2_b, loss_target, m_w_in, m_conv_w, m_conv_b, m_rg_wa, m_rg_ba, m_rg_wx, m_rg_bx, m_rg_lambda, m_s5_a_re, m_s5_a_im, m_s5_b_re, m_s5_b_im, m_s5_c_re, m_s5_c_im, m_s5_d, m_s5_log_step, m_s5_w_glu, m_s5_b_glu, m_w_out, m_ln1_g, m_ln1_b, m_ple_w, m_ple_gate_w, m_ple_gate_b, m_ln2_g, m_ln2_b, v_w_in, v_conv_w, v_conv_b, v_rg_wa, v_rg_ba, v_rg_wx, v_rg_bx, v_rg_lambda, v_s5_a_re, v_s5_a_im, v_s5_b_re, v_s5_b_im, v_s5_c_re, v_s5_c_im, v_s5_d, v_s5_log_step, v_s5_w_glu, v_s5_b_glu, v_w_out, v_ln1_g, v_ln1_b, v_ple_w, v_ple_gate_w, v_ple_gate_b, v_ln2_g, v_ln2_b):
    local = dict(locals())
    w = {k: local[k] for k in WEIGHTS}
    mom = {k: local['m_' + k] for k in WEIGHTS}
    var = {k: local['v_' + k] for k in WEIGHTS}

    big = list(SHARD_AXIS)
    late_axes = [SHARD_AXIS[k] - 1 for k in LATE]
    pushed = {}

    groups = dict(first=(['w_in', 'conv_w'], [0, None], [1, 0]), l0=(list(LATE), [0] * len(LATE), late_axes),
                  l1=(['w_in'] + list(LATE), [1] * (1 + len(LATE)), [1] + late_axes))
    token = None
    for key, members in (("first", ["first"]), ("rest", ["l0", "l1"])):
        names, layers, axes = (sum((groups[m][j] for m in members), []) for j in range(3))
        shards = [w[k] if layer is not None else w[k][None] for k, layer in zip(names, layers)]
        lands = _place_shards(shards, layers, axes, [WIRE if k in big else w[k].dtype for k in names],
                              "place_weights_" + key, token)
        pushed[key] = _push_start("gather", [], lands, axes, "push_weights_" + key, "near" if key == "first" else None)
        token = pushed[key][4]

    def await_weights(key, axes, after):
        s, first = (pushed["first"], 0) if key == "first" else (pushed["rest"], 0 if key == "l0" else len(LATE))
        return _push_wait("gather", s[0], s[1], [], s[3][first:first + len(axes)], axes, after, "await_weights_" + key, first)

    def push_grads(key, g, names, axes):
        srcs = [g[k] for k in names]
        pushed[key] = _push_start("scatter", srcs, _place_own("scatter", srcs, axes, "place_grads_" + key), axes,
                                  "push_grads_" + key)
        return pushed[key][4]

    def await_grads(key, axes, after):
        s = pushed[key]
        return _push_wait("scatter", s[0], s[1], s[2], s[3], axes, after, "await_grads_" + key)

    class Overlap(_NoHooks):
        token = pushed["rest"][4]
        first_token = token

        def first_weights(self, full, after):
            s, axes = pushed["first"], [1, 0]
            near = _push_wait("gather", s[0], s[1], [], s[3], axes, after, "await_weights_near", route="near")
            s = _push_start("gather", [], near, axes, "relay_weights", "relay")
            w_in0, conv = _push_wait("gather", s[0], s[1], [], s[3], axes, s[4], "await_weights_relay", route="relay")
            return dict(full, w_in=[w_in0, None], conv_w=jnp.moveaxis(conv, 0, 2).reshape(2, 4, RG_W))

        def late_weights(self, i, W, after):
            if i == 0:
                W[0].update(zip(LATE, await_weights("l0", late_axes, after)))
            return W

        def layer_start(self, i, W, after):
            lands = await_weights("l1", [1] + late_axes, after)
            W[1].update(zip(LATE, lands[1:]), w_in=lands[0])
            return W

        def post_done(self, i, g):
            return push_grads("late0", g, LATE, late_axes) if i == 0 else None

        def smalls_done(self, grads, loss):
            super().smalls_done(grads, loss)
            conv = jnp.moveaxis(self.small['conv_w'].reshape(2, 4, N_DEV, RG_W // N_DEV), 2, 0)
            self.packed = _pack(self.small, loss)
            return push_grads("small", dict(conv_w=conv.reshape(N_DEV, 8, RG_W // N_DEV), small=self.packed),
                              ['conv_w', 'small'], [0, 0])

        def w_in_done(self, i, g):
            return push_grads("w_in0", g, ['w_in'], [0])

        def layer_done(self, i, g, dx):
            return push_grads("all1", g, ['w_in'] + list(LATE), [0] + late_axes) if i == 1 else None

    hooks = Overlap()
    _, grad_x, g = _full_grads(dict(w), x[0], p, loss_target[0], hooks)

    recv1 = dict(zip(['w_in'] + list(LATE), await_grads("all1", [0] + late_axes, grad_x)))
    recv0 = dict(zip(LATE, await_grads("late0", late_axes, grad_x)))
    outs = {}

    def update(k, parts, token=None):
        shard = w[k].shape
        c = shard[-1]
        two = lambda a: a.reshape(-1, c)
        res = _adamw([r.reshape(N_DEV, -1, c) for r in parts], two(w[k]), two(mom[k]), two(var[k]), token)
        outs[k] = [o.reshape(shard) for o in res]

    conv_parts, small_parts = await_grads("small", [0, 0], grad_x)
    rows = hooks.packed.shape[0] // N_DEV
    mine = _sum_parts(small_parts.reshape(N_DEV, rows, LANE))
    sums = _push_start("gather", [mine], _place_own("gather", [mine], [0], "place_small_sums"), [0], "push_small_sums")
    token = sums[4]
    for k in LATE:
        update(k, [recv0[k], recv1[k]], token)
        token = None
    w_in0, = await_grads("w_in0", [0], [outs[k][1] for k in LATE])
    update('w_in', [w_in0, recv1['w_in']])
    update('conv_w', [conv_parts])
    gathered, = _push_wait("gather", sums[0], sums[1], sums[2], sums[3], [0], [outs['w_in'][1], outs['conv_w'][1]],
                           "await_small_sums")
    stored = [{k: _stored(k, t[k]) for k in SMALL} for t in (w, mom, var)]
    summed, loss = _unpack(gathered, stored[0])
    wide = ['s5_b_re', 's5_b_im']
    for names, name in ((wide, "adamw_s5_b"), ([k for k in SMALL if k not in wide], "adamw_small")):
        delta, new_m, new_v = _adamw_natural(names, summed, *stored, name)
        for k in names:
            outs[k] = [_stored(k, o[k]) for o in (summed, delta, new_m, new_v)]

    res = [loss, grad_x[None]]
    for j in range(4):
        res += [outs[k][j] for k in WEIGHTS]
    return tuple(res)
```

```python
import math

import jax
import jax.numpy as jnp
from jax import lax
from jax.experimental import pallas as pl
from jax.experimental.pallas import tpu as pltpu

F32 = jnp.float32
MXU = jnp.bfloat16
WIRE = jnp.bfloat16

N_DEV = 8
D_MODEL = 1024
PLE_D = 256
RG_W = 640
S5_W = 384
S5_P = 64
S5_N = 24 * S5_P
Z_W = 2 * RG_W + 2 * S5_W
C_RGG = RG_W
C_S5U = 2 * RG_W
C_S5G = 2 * RG_W + S5_W
LANE = 128
N_RG_T = RG_W // LANE
N_S5_T = S5_W // LANE
W_BLK = Z_W // N_DEV
ALPHA = (2.0 * 2) ** 0.25
LN_EPS = 1e-5
RG_C = 8.0
LR, B1, B2, EPS, WD, STEP = 0.001, 0.9, 0.999, 1e-08, 0.01, 10
BC1 = 1.0 - B1 ** STEP
BC2 = 1.0 - B2 ** STEP
RC = 512
RC_RG = 1024
TM = 512
TM_MM = 1024
VMEM_LIMIT = 56 * 1024 * 1024

MESH = pl.DeviceIdType.MESH
ANY = pl.BlockSpec(memory_space=pl.ANY)


def _params(n_grid_axes, vmem=VMEM_LIMIT):
    return pltpu.CompilerParams(dimension_semantics=("arbitrary",) * n_grid_axes, vmem_limit_bytes=vmem)


def _S(shape, dtype=F32):
    return jax.ShapeDtypeStruct(tuple(shape), dtype)


def _sigmoid(x):
    return 0.5 * jnp.tanh(0.5 * x) + 0.5


def _silu_and_grad(x):
    s = _sigmoid(x)
    return x * s, s * (1.0 + x * (1.0 - s))


_GELU_C = math.sqrt(2.0 / math.pi)


def _gelu(x):
    return 0.5 * x * (1.0 + jnp.tanh(_GELU_C * (x + 0.044715 * (x * x * x))))


def _gelu_grad(x):
    th = jnp.tanh(_GELU_C * (x + 0.044715 * (x * x * x)))
    return 0.5 * (1.0 + th) + 0.5 * x * (1.0 - th * th) * (_GELU_C * (1.0 + 3.0 * 0.044715 * (x * x)))


def _mm(a, b):
    return jnp.dot(a.astype(MXU), b.astype(MXU), preferred_element_type=F32)


def _mm_nt(a, b):
    return lax.dot_general(a.astype(MXU), b.astype(MXU), (((1,), (1,)), ((), ())), preferred_element_type=F32)


def _mm_tn(a, b):
    return lax.dot_general(a.astype(MXU), b.astype(MXU), (((0,), (0,)), ((), ())), preferred_element_type=F32)


def _ln_fwd(t, g, b):
    mu = jnp.mean(t, axis=-1, keepdims=True)
    tc = t - mu
    var = jnp.mean(tc * tc, axis=-1, keepdims=True)
    rstd = lax.rsqrt(var + LN_EPS)
    xhat = tc * rstd
    return xhat * g + b, xhat, rstd


def _ln_bwd(dy, xhat, rstd, g):
    dxh = dy * g
    m1 = jnp.mean(dxh, axis=-1, keepdims=True)
    m2 = jnp.mean(dxh * xhat, axis=-1, keepdims=True)
    return rstd * (dxh - m1 - xhat * m2)


def _colsum(a):
    return jnp.sum(a, axis=0, keepdims=True)


def _up(x, d, rows, fill):
    n = x.shape[0]
    return jnp.where(rows < n - d, pltpu.roll(x, n - d, 0), fill)


SUB = 8
TILE_STEPS = (1, 2, 4)


def _r8(width):
    return lax.broadcasted_iota(jnp.int32, (SUB, width), 0)


def _scan_real(a, u, carry, reverse=False):
    r8 = _r8(a.shape[1])
    n = a.shape[0] // SUB
    outs = [None] * n
    for k in (reversed(range(n)) if reverse else range(n)):
        A, U = a[SUB * k:SUB * k + SUB], u[SUB * k:SUB * k + SUB]
        for d in TILE_STEPS:
            m = (r8 < SUB - d) if reverse else (r8 >= d)
            sh = SUB - d if reverse else d
            U = A * jnp.where(m, pltpu.roll(U, sh, 0), 0.0) + U
            A = A * jnp.where(m, pltpu.roll(A, sh, 0), 1.0)
        h = A * carry + U
        outs[k] = h
        carry = h[0:1] if reverse else h[SUB - 1:SUB]
    return jnp.concatenate(outs, axis=0), carry


def _tile_powers(lr, li, reverse=False):
    width = lr.shape[1]
    r8 = _r8(width)
    steps = []
    pr, pi = lr, li
    er, ei = jnp.broadcast_to(lr, (SUB, width)), jnp.broadcast_to(li, (SUB, width))
    for d in TILE_STEPS:
        m = (r8 < SUB - d) if reverse else (r8 >= d)
        sh = SUB - d if reverse else d
        steps.append((sh, jnp.where(m, pr, 0.0), jnp.where(m, pi, 0.0)))
        er, ei = _cmul(er, ei, jnp.where(m, pltpu.roll(er, sh, 0), 1.0), jnp.where(m, pltpu.roll(ei, sh, 0), 0.0))
        pr, pi = _cmul(pr, pi, pr, pi)
    return steps, (er, ei)


def _scan_lti(xr, xi, carry, steps, e, reverse=False):
    er, ei = e
    kr, ki = carry
    n = xr.shape[0] // SUB
    outr, outi = [None] * n, [None] * n
    for k in (reversed(range(n)) if reverse else range(n)):
        sr, si = xr[SUB * k:SUB * k + SUB], xi[SUB * k:SUB * k + SUB]
        for sh, pr, pi in steps:
            shr, shi = pltpu.roll(sr, sh, 0), pltpu.roll(si, sh, 0)
            sr, si = sr + (pr * shr - pi * shi), si + (pr * shi + pi * shr)
        sr = sr + (er * kr - ei * ki)
        si = si + (er * ki + ei * kr)
        outr[k], outi[k] = sr, si
        kr, ki = (sr[0:1], si[0:1]) if reverse else (sr[SUB - 1:SUB], si[SUB - 1:SUB])
    return jnp.concatenate(outr, axis=0), jnp.concatenate(outi, axis=0), (kr, ki)


def _halo(ref, c, r0):
    rp = pl.multiple_of(jnp.maximum(r0 - 8, 0), 8)
    return jnp.where(c > 0, ref[pl.ds(rp, 8), :], 0.0)


def _conv_taps(xe):
    return [pltpu.roll(xe, 3, 0)[8:, :], pltpu.roll(xe, 2, 0)[8:, :], pltpu.roll(xe, 1, 0)[8:, :], xe[8:, :]]


def _rg_gates(h, wa, wx, ba, bx, sp):
    r = _sigmoid(_mm(h, wa) + ba)
    i = _sigmoid(_mm(h, wx) + bx)
    log_a = (-RG_C) * r * sp
    a = jnp.exp(log_a)
    mult = jnp.sqrt(-jnp.tanh(log_a) * (a * a + 1.0))
    return r, i, a, mult


def _softplus(y):
    return jnp.maximum(y, 0.0) + jnp.log1p(jnp.exp(-jnp.abs(y)))


def _after(token):
    return ([], []) if token is None else ([token], [ANY])


def _inproj_fwd(x, w_in, token=None):
    L = x.shape[0]

    def body(x_ref, w_ref, *rest):
        rest[-1][...] = _mm(x_ref[...], w_ref[...])

    extra, extra_specs = _after(token)
    tm = min(TM_MM, L)
    return pl.pallas_call(
        body, name="inproj_fwd", grid=(L // tm,),
        in_specs=[pl.BlockSpec((tm, D_MODEL), lambda i: (i, 0)), pl.BlockSpec((D_MODEL, Z_W), lambda i: (0, 0))] + extra_specs,
        out_specs=pl.BlockSpec((tm, Z_W), lambda i: (i, 0)),
        out_shape=_S((L, Z_W)), compiler_params=_params(1))(x, w_in, *extra)


def _inproj_bwd(dt1, x, dzx, dzg, dzu, w_in):
    L = x.shape[0]

    def body(dt1_ref, x_ref, dzx_ref, dzg_ref, dzu_ref, w_ref, dx_ref, dw_ref, acc_ref):
        @pl.when(pl.program_id(0) == 0)
        def _():
            acc_ref[...] = jnp.zeros_like(acc_ref)
        dzg = dzg_ref[...]
        dz = jnp.concatenate([dzx_ref[...], dzg[:, :RG_W], dzu_ref[...], dzg[:, RG_W:]], axis=1).astype(MXU)
        xb = x_ref[...].astype(MXU)
        dx_ref[...] = ALPHA * dt1_ref[...] + _mm_nt(dz, w_ref[...])
        for j in range(N_DEV):
            acc_ref[j] += _mm_tn(xb, dz[:, j * W_BLK:(j + 1) * W_BLK])

        @pl.when(pl.program_id(0) == L // TM - 1)
        def _():
            dw_ref[...] = acc_ref[...].astype(WIRE)

    row = lambda w: pl.BlockSpec((TM, w), lambda i: (i, 0))
    wspec = pl.BlockSpec((N_DEV, D_MODEL, W_BLK), lambda i: (0, 0, 0))
    return pl.pallas_call(
        body, name="inproj_bwd", grid=(L // TM,),
        in_specs=[row(D_MODEL), row(D_MODEL), row(RG_W), row(D_MODEL), row(S5_W),
                  pl.BlockSpec((D_MODEL, Z_W), lambda i: (0, 0))],
        out_specs=[row(D_MODEL), wspec],
        out_shape=[_S((L, D_MODEL)), _S((N_DEV, D_MODEL, W_BLK), WIRE)],
        scratch_shapes=[pltpu.VMEM((N_DEV, D_MODEL, W_BLK), F32)],
        compiler_params=_params(1))(dt1, x, dzx, dzg, dzu, w_in)


TM2 = 512


def _dz_block(dzx_ref, dzg_ref, dzu_ref):
    dzg = dzg_ref[...]
    return jnp.concatenate([dzx_ref[...], dzg[:, :RG_W], dzu_ref[...], dzg[:, RG_W:]], axis=1).astype(MXU)


def _inproj_bwd_dw(x, dzx, dzg, dzu, token=None):
    L = x.shape[0]
    extra, extra_specs = _after(token)

    def body(x_ref, dzx_ref, dzg_ref, dzu_ref, *rest):
        dw_ref, acc_ref = rest[len(extra):]
        @pl.when(pl.program_id(0) == 0)
        def _():
            acc_ref[...] = jnp.zeros_like(acc_ref)
        dz = _dz_block(dzx_ref, dzg_ref, dzu_ref)
        xb = x_ref[...].astype(MXU)
        for j in range(N_DEV):
            acc_ref[j] += _mm_tn(xb, dz[:, j * W_BLK:(j + 1) * W_BLK])

        @pl.when(pl.program_id(0) == L // TM2 - 1)
        def _():
            dw_ref[...] = acc_ref[...].astype(WIRE)

    row = lambda w: pl.BlockSpec((TM2, w), lambda i: (i, 0))
    wspec = pl.BlockSpec((N_DEV, D_MODEL, W_BLK), lambda i: (0, 0, 0))
    return pl.pallas_call(
        body, name="inproj_bwd_dw", grid=(L // TM2,),
        in_specs=[row(D_MODEL), row(RG_W), row(D_MODEL), row(S5_W)] + extra_specs, out_specs=wspec,
        out_shape=_S((N_DEV, D_MODEL, W_BLK), WIRE), scratch_shapes=[pltpu.VMEM((N_DEV, D_MODEL, W_BLK), F32)],
        compiler_params=_params(1))(x, dzx, dzg, dzu, *extra)


def _inproj_bwd_dx(dt1, dzx, dzg, dzu, w_in, token=None):
    L = dt1.shape[0]
    extra, extra_specs = _after(token)

    def body(dt1_ref, dzx_ref, dzg_ref, dzu_ref, w_ref, *rest):
        rest[-1][...] = ALPHA * dt1_ref[...] + _mm_nt(_dz_block(dzx_ref, dzg_ref, dzu_ref), w_ref[...])

    tm = min(TM_MM, L)
    row = lambda w: pl.BlockSpec((tm, w), lambda i: (i, 0))
    return pl.pallas_call(
        body, name="inproj_bwd_dx", grid=(L // tm,),
        in_specs=[row(D_MODEL), row(RG_W), row(D_MODEL), row(S5_W), _full((D_MODEL, Z_W))] + extra_specs,
        out_specs=row(D_MODEL), out_shape=_S((L, D_MODEL)), compiler_params=_params(1))(dt1, dzx, dzg, dzu, w_in, *extra)


def _rg_specs(layer):
    tile = lambda rows: pl.BlockSpec((rows, LANE), lambda c: (0, c))
    ptile = lambda rows: pl.BlockSpec((None, rows, LANE), lambda c: (layer, 0, c))
    pheads = pl.BlockSpec((None, 2, RG_HD, RG_HD), lambda c: (layer, c, 0, 0))
    return tile, ptile, pheads, pl.BlockSpec((2, RG_HD, RG_HD), lambda c: (c, 0, 0))


RG_HD = 64


def _bd2(w):
    z = jnp.zeros((RG_HD, RG_HD), w.dtype)
    return jnp.concatenate([jnp.concatenate([w[0], z], axis=1), jnp.concatenate([z, w[1]], axis=1)], axis=0)


def _bd2_diag(m):
    return jnp.stack([m[:RG_HD, :RG_HD], m[RG_HD:, RG_HD:]])


def _rg_fwd(z, cw, cb, wa_bd, wx_bd, ba, bx, lam, layer):
    L = z.shape[0]
    RC = min(RC_RG, L)

    def body(x_ref, cw_ref, cb_ref, wa_ref, wx_ref, ba_ref, bx_ref, lam_ref, hs_ref, *saved):
        row = slice(layer, layer + 1)
        w, b = cw_ref[...], cb_ref[row, :]
        wa, wx, ba_, bx_ = _bd2(wa_ref[...]).astype(MXU), _bd2(wx_ref[...]).astype(MXU), ba_ref[row, :], bx_ref[row, :]
        sp = _softplus(-lam_ref[row, :])

        def step(c, carry):
            r0 = pl.multiple_of(c * RC, RC)
            xe = jnp.concatenate([_halo(x_ref, c, r0), x_ref[pl.ds(r0, RC), :]], axis=0)
            t = _conv_taps(xe)
            h = t[0] * w[0:1] + t[1] * w[1:2] + t[2] * w[2:3] + t[3] * w[3:4] + b
            r, i, a, mult = _rg_gates(h, wa, wx, ba_, bx_, sp)
            hs, carry = _scan_real(a, mult * (i * h), carry)
            hs_ref[pl.ds(r0, RC), :] = hs
            for ref, val in zip(saved, (h, r, i, a, mult)):
                ref[pl.ds(r0, RC), :] = val
            return carry

        lax.fori_loop(0, L // RC, step, jnp.zeros((1, LANE), F32))

    tile, ptile, pheads, _ = _rg_specs(layer)
    return pl.pallas_call(
        body, name="rg_fwd", grid=(N_RG_T,),
        in_specs=[tile(L), ptile(4), tile(2), pheads, pheads, tile(2), tile(2), tile(2)],
        out_specs=[tile(L)] * 6, out_shape=[_S((L, RG_W))] * 6, compiler_params=_params(1))(
            z, cw, cb, wa_bd, wx_bd, ba, bx, lam)


def _rg_bwd(dhs, z, hs, gates, cw, wa_bd, wx_bd, lam, layer):
    L = z.shape[0]
    RC = min(RC_RG, L)

    def body(g_ref, x_ref, hs_ref, h_ref, r_ref, i_ref, a_ref, mult_ref, cw_ref, wa_ref, wx_ref, lam_ref,
             dx_ref, dcw_ref, dcb_ref, dwa_out, dwx_out, dba_ref, dbx_ref, dlam_ref, dwa_ref, dwx_ref):
        w = cw_ref[...]
        wa, wx = _bd2(wa_ref[...]).astype(MXU), _bd2(wx_ref[...]).astype(MXU)
        lam = lam_ref[layer:layer + 1, :]
        sp = _softplus(-lam)
        rows = lax.broadcasted_iota(jnp.int32, (RC, LANE), 0)
        for ref in (dcw_ref, dcb_ref, dwa_ref, dwx_ref, dba_ref, dbx_ref, dlam_ref):
            ref[...] = jnp.zeros_like(ref)
        nch = L // RC

        def step(k, carry):
            cin, nxt = carry
            c = nch - 1 - k
            r0 = pl.multiple_of(c * RC, RC)
            xe = jnp.concatenate([_halo(x_ref, c, r0), x_ref[pl.ds(r0, RC), :]], axis=0)
            t = _conv_taps(xe)
            h, r, i, a, mult = (ref[pl.ds(r0, RC), :] for ref in (h_ref, r_ref, i_ref, a_ref, mult_ref))
            hs_e = jnp.concatenate([_halo(hs_ref, c, r0), hs_ref[pl.ds(r0, RC), :]], axis=0)
            hs_prev = pltpu.roll(hs_e, 1, 0)[8:, :]
            g = g_ref[pl.ds(r0, RC), :]
            cc, cin_new = _scan_real(a, a * g, cin, reverse=True)
            dh = g + _up(cc, 1, rows, cin)
            ih = i * h
            dlog_a = dh * hs_prev * a - (dh * ih) * (a * a) / mult
            di = dh * mult * h
            dhin = dh * mult * i
            dr = dlog_a * ((-RG_C) * sp)
            dlam_ref[...] += _colsum(dlog_a * r)
            dra = dr * r * (1.0 - r)
            dia = di * i * (1.0 - i)
            dwa_ref[...] += _mm_tn(h, dra)
            dwx_ref[...] += _mm_tn(h, dia)
            dba_ref[...] += _colsum(dra)
            dbx_ref[...] += _colsum(dia)
            dhin = dhin + _mm_nt(dra, wa) + _mm_nt(dia, wx)
            de = jnp.concatenate([dhin, nxt], axis=0)
            n = RC + 8
            dx = (dhin * w[3:4] + pltpu.roll(de, n - 1, 0)[:RC, :] * w[2:3]
                  + pltpu.roll(de, n - 2, 0)[:RC, :] * w[1:2] + pltpu.roll(de, n - 3, 0)[:RC, :] * w[0:1])
            dx_ref[pl.ds(r0, RC), :] = dx
            for kk in range(4):
                dcw_ref[kk:kk + 1, :] += _colsum(dhin * t[kk])
            dcb_ref[...] += _colsum(dhin)
            return cin_new, dhin[0:8, :]

        lax.fori_loop(0, nch, step, (jnp.zeros((1, LANE), F32), jnp.zeros((8, LANE), F32)))
        dlam_ref[...] = dlam_ref[...] * (RG_C * _sigmoid(-lam))
        dwa_out[...], dwx_out[...] = _bd2_diag(dwa_ref[...]), _bd2_diag(dwx_ref[...])

    tile, ptile, pheads, gheads = _rg_specs(layer)
    heads = _S((2 * N_RG_T, RG_HD, RG_HD))
    return pl.pallas_call(
        body, name="rg_bwd", grid=(N_RG_T,),
        in_specs=[tile(L)] * 8 + [ptile(4), pheads, pheads, tile(2)],
        out_specs=[tile(L), tile(4), tile(1), gheads, gheads, tile(1), tile(1), tile(1)],
        out_shape=[_S((L, RG_W)), _S((4, RG_W)), _S((1, RG_W)), heads, heads, _S((1, RG_W)), _S((1, RG_W)), _S((1, RG_W))],
        scratch_shapes=[pltpu.VMEM((LANE, LANE), F32), pltpu.VMEM((LANE, LANE), F32)],
        compiler_params=_params(1))(dhs, z, hs, *gates, cw, wa_bd, wx_bd, lam)


def _cmul(ar, ai, br, bi):
    return ar * br - ai * bi, ar * bi + ai * br


S5_TW = S5_N // N_S5_T


S5_H = 16
S5_GT = LANE // S5_H


def _s5_specs(L, layer):
    in_tile = pl.BlockSpec((L, LANE), lambda t: (0, t))
    st = pl.BlockSpec((L, S5_TW), lambda t: (0, t))
    pg = pl.BlockSpec((None, S5_GT, S5_H, S5_P), lambda t: (layer * N_S5_T + t, 0, 0, 0))
    plb = pl.BlockSpec((None, S5_GT, S5_P), lambda t: (layer * N_S5_T + t, 0, 0))
    gg = pl.BlockSpec((None, S5_GT, S5_H, S5_P), lambda t: (t, 0, 0, 0))
    glb = pl.BlockSpec((None, S5_GT, S5_P), lambda t: (t, 0, 0))
    dv = pl.BlockSpec((1, LANE), lambda t: (0, t))
    return in_tile, st, pg, plb, gg, glb, dv


def _bd8(blocks):
    rows = []
    for g in range(S5_GT):
        pieces = [blocks[g]]
        if g:
            pieces.insert(0, jnp.zeros((S5_H, S5_P * g), blocks.dtype))
        if g < S5_GT - 1:
            pieces.append(jnp.zeros((S5_H, S5_P * (S5_GT - 1 - g)), blocks.dtype))
        rows.append(jnp.concatenate(pieces, axis=1))
    return jnp.concatenate(rows, axis=0)


def _bd8_diag(m):
    return jnp.stack([m[S5_H * g:S5_H * (g + 1), S5_P * g:S5_P * (g + 1)] for g in range(S5_GT)])


def _row8(v):
    return jnp.concatenate([v[g:g + 1] for g in range(S5_GT)], axis=1)


def _row8_split(r):
    return jnp.concatenate([r[:, S5_P * g:S5_P * (g + 1)] for g in range(S5_GT)], axis=0)


def _layer_row_tile(layer):
    return pl.BlockSpec((None, 1, LANE), lambda t: (layer, 0, t))


def _s5_fwd(z, bb_re, bb_im, lb_re, lb_im, c_re, c_im, dvec, layer):
    L = z.shape[0]

    def body(u_ref, bbr_ref, bbi_ref, lr_ref, li_ref, cr_ref, ci_ref, d_ref, y_ref, sr_ref, si_ref):
        bbr, bbi = _bd8(bbr_ref[...]).astype(MXU), _bd8(bbi_ref[...]).astype(MXU)
        cr, ci = _bd8(cr_ref[...]).astype(MXU), _bd8(ci_ref[...]).astype(MXU)
        dv = d_ref[...]
        steps, e = _tile_powers(_row8(lr_ref[...]), _row8(li_ref[...]))

        def step(c, carry):
            r0 = pl.multiple_of(c * RC, RC)
            u = u_ref[pl.ds(r0, RC), :]
            ub = u.astype(MXU)
            sr = jnp.dot(ub, bbr, preferred_element_type=F32)
            si = jnp.dot(ub, bbi, preferred_element_type=F32)
            sr, si, carry = _scan_lti(sr, si, carry, steps, e)
            sr_ref[pl.ds(r0, RC), :] = sr
            si_ref[pl.ds(r0, RC), :] = si
            y_ref[pl.ds(r0, RC), :] = dv * u + (_mm_nt(sr, cr) - _mm_nt(si, ci))
            return carry

        zero = jnp.zeros((1, S5_TW), F32)
        lax.fori_loop(0, L // RC, step, (zero, zero))

    in_tile, st, pg, plb, _, _, _ = _s5_specs(L, layer)
    u_tile = pl.BlockSpec((L, LANE), lambda t: (0, C_S5U // LANE + t))
    return pl.pallas_call(
        body, name="s5_fwd", grid=(N_S5_T,),
        in_specs=[u_tile, pg, pg, plb, plb, pg, pg, _layer_row_tile(layer)],
        out_specs=[in_tile, st, st],
        out_shape=[_S((L, S5_W)), _S((L, S5_N)), _S((L, S5_N))],
        compiler_params=_params(1))(z, bb_re, bb_im, lb_re, lb_im, c_re, c_im, dvec)


def _s5_bwd(dy0, z, s_re, s_im, bb_re, bb_im, lb_re, lb_im, c_re, c_im, dvec, layer, token=None):
    L = z.shape[0]
    extra, extra_specs = _after(token)

    def body(dy_ref, u_ref, sr_ref, si_ref, bbr_ref, bbi_ref, lr_ref, li_ref, cr_ref, ci_ref, d_ref, *rest):
        (du_ref, dbbr_out, dbbi_out, dlr_out, dli_out, dcr_out, dci_out, dd_ref,
         dbbr_ref, dbbi_ref, dcr_ref, dci_ref, dlr_ref, dli_ref) = rest[len(extra):]
        bbr, bbi = _bd8(bbr_ref[...]).astype(MXU), _bd8(bbi_ref[...]).astype(MXU)
        cr, ci = _bd8(cr_ref[...]).astype(MXU), _bd8(ci_ref[...]).astype(MXU)
        lr, li = _row8(lr_ref[...]), -_row8(li_ref[...])
        dv = d_ref[...]
        steps, e = _tile_powers(lr, li, reverse=True)
        for ref in (dbbr_ref, dbbi_ref, dlr_ref, dli_ref, dcr_ref, dci_ref, dd_ref):
            ref[...] = jnp.zeros_like(ref)
        nch = L // RC

        def step(k, carry):
            c = nch - 1 - k
            r0 = pl.multiple_of(c * RC, RC)
            dy = dy_ref[pl.ds(r0, RC), :]
            u = u_ref[pl.ds(r0, RC), :]
            dyb, ub = dy.astype(MXU), u.astype(MXU)
            sr, si = sr_ref[pl.ds(r0, RC), :], si_ref[pl.ds(r0, RC), :]
            dcr_ref[...] += _mm_tn(dyb, sr)
            dci_ref[...] -= _mm_tn(dyb, si)
            gr = jnp.dot(dyb, cr, preferred_element_type=F32)
            gi = -jnp.dot(dyb, ci, preferred_element_type=F32)
            gr, gi, carry = _scan_lti(gr, gi, carry, steps, e, reverse=True)
            pr_ = pltpu.roll(jnp.concatenate([_halo(sr_ref, c, r0), sr], axis=0), 1, 0)[8:, :]
            pi_ = pltpu.roll(jnp.concatenate([_halo(si_ref, c, r0), si], axis=0), 1, 0)[8:, :]
            dlr_ref[...] += _colsum(pr_ * gr + pi_ * gi)
            dli_ref[...] += _colsum(pr_ * gi - pi_ * gr)
            grb, gib = gr.astype(MXU), gi.astype(MXU)
            dbbr_ref[...] += _mm_tn(ub, grb)
            dbbi_ref[...] += _mm_tn(ub, gib)
            du_ref[pl.ds(r0, RC), :] = dv * dy + (_mm_nt(grb, bbr) + _mm_nt(gib, bbi))
            dd_ref[...] += _colsum(dy * u)
            return carry

        zero = jnp.zeros((1, S5_TW), F32)
        lax.fori_loop(0, nch, step, (zero, zero))
        dbbr_out[...], dbbi_out[...] = _bd8_diag(dbbr_ref[...]), _bd8_diag(dbbi_ref[...])
        dcr_out[...], dci_out[...] = _bd8_diag(dcr_ref[...]), _bd8_diag(dci_ref[...])
        dlr_out[...], dli_out[...] = _row8_split(dlr_ref[...]), _row8_split(dli_ref[...])

    in_tile, st, pg, plb, gg, glb, dv = _s5_specs(L, layer)
    u_tile = pl.BlockSpec((L, LANE), lambda t: (0, C_S5U // LANE + t))
    groups, rows = _S((N_S5_T, S5_GT, S5_H, S5_P)), _S((N_S5_T, S5_GT, S5_P))
    wide = pltpu.VMEM((LANE, S5_TW), F32)
    return pl.pallas_call(
        body, name="s5_bwd", grid=(N_S5_T,),
        in_specs=[in_tile, u_tile, st, st, pg, pg, plb, plb, pg, pg, _layer_row_tile(layer)] + extra_specs,
        out_specs=[in_tile, gg, gg, glb, glb, gg, gg, dv],
        out_shape=[_S((L, S5_W)), groups, groups, rows, rows, groups, groups, _S((1, S5_W))],
        scratch_shapes=[wide, wide, wide, wide, pltpu.VMEM((1, S5_TW), F32), pltpu.VMEM((1, S5_TW), F32)],
        compiler_params=_params(1))(dy0, z, s_re, s_im, bb_re, bb_im, lb_re, lb_im, c_re, c_im, dvec, *extra)


def _disc(ar, ai, ls):
    dt = jnp.exp(ls)
    mag = jnp.exp(ar * dt)
    lr = mag * jnp.cos(ai * dt)
    li = mag * jnp.sin(ai * dt)
    den = ar * ar + ai * ai
    cr = ((lr - 1.0) * ar + li * ai) / den
    ci = (li * ar - (lr - 1.0) * ai) / den
    return lr, li, cr, ci


def _s5_disc_fwd(ar, ai, ls, token=None):
    extra, extra_specs = _after(token)

    def body(ar_ref, ai_ref, ls_ref, *rest):
        lr_ref, li_ref, cr_ref, ci_ref = rest[len(extra):]
        lr, li, cr, ci = _disc(ar_ref[...], ai_ref[...], ls_ref[...])
        lr_ref[...], li_ref[...], cr_ref[...], ci_ref[...] = lr, li, cr, ci

    sh = _S(ar.shape)
    vm = pl.BlockSpec(memory_space=pltpu.VMEM)
    return pl.pallas_call(body, name="s5_disc_fwd", in_specs=[vm, vm, vm] + extra_specs, out_shape=[sh, sh, sh, sh])(
        ar, ai, ls, *extra)


def _s5_disc_bwd(ar, ai, ls, dlr, dli, dcr, dci):
    def body(ar_ref, ai_ref, ls_ref, dlr_ref, dli_ref, dcr_ref, dci_ref, dar_ref, dai_ref, dls_ref):
        _, vjp = jax.vjp(_disc, ar_ref[...], ai_ref[...], jnp.broadcast_to(ls_ref[...], ar_ref.shape))
        dar, dai, dls = vjp((dlr_ref[...], dli_ref[...], dcr_ref[...], dci_ref[...]))
        dar_ref[...], dai_ref[...] = dar, dai
        dls_ref[...] = jnp.sum(dls, axis=1, keepdims=True)

    return pl.pallas_call(body, name="s5_disc_bwd", out_shape=[_S(ar.shape), _S(ar.shape), _S(ls.shape)])(
        ar, ai, ls, dlr, dli, dcr, dci)


def _s5_bscale_fwd(cr, ci, br, bi):
    def body(cr_ref, ci_ref, br_ref, bi_ref, or_ref, oi_ref):
        or_ref[...], oi_ref[...] = _cmul(cr_ref[...], ci_ref[...], br_ref[...], bi_ref[...])

    return pl.pallas_call(body, name="s5_bscale_fwd", out_shape=[_S(br.shape), _S(br.shape)])(cr, ci, br, bi)


def _s5_bscale_bwd(cr, ci, br, bi, gr, gi):
    def body(cr_ref, ci_ref, br_ref, bi_ref, gr_ref, gi_ref, dbr_ref, dbi_ref, dcr_ref, dci_ref):
        cr_, ci_, br_, bi_, gr_, gi_ = (r[...] for r in (cr_ref, ci_ref, br_ref, bi_ref, gr_ref, gi_ref))
        dbr_ref[...] = cr_ * gr_ + ci_ * gi_
        dbi_ref[...] = cr_ * gi_ - ci_ * gr_
        dcr_ref[...] = jnp.sum(gr_ * br_ + gi_ * bi_, axis=1, keepdims=True)
        dci_ref[...] = jnp.sum(gi_ * br_ - gr_ * bi_, axis=1, keepdims=True)

    return pl.pallas_call(body, name="s5_bscale_bwd",
                          out_shape=[_S(br.shape), _S(br.shape), _S(cr.shape), _S(cr.shape)])(cr, ci, br, bi, gr, gi)


def _row(w):
    return pl.BlockSpec((TM, w), lambda i: (i, 0))


def _full(shape):
    return pl.BlockSpec(tuple(shape), lambda i: (0,) * len(shape))


def _gate_rows():
    return [pl.BlockSpec((TM, RG_W), lambda i: (i, C_RGG // RG_W))] + [
        pl.BlockSpec((TM, LANE), lambda i, k=k: (i, C_S5G // LANE + k)) for k in range(N_S5_T)]


def _p_rows(layer):
    return pl.BlockSpec((None, None, TM, PLE_D), lambda i: (layer, 0, i, 0))


DEPTH = 2


def _lrow(layer, width):
    return _full((DEPTH, width))


def _pick(ref, layer):
    return ref[layer:layer + 1, :]


def _post_fwd(x, hs, z, y0, p, w_glu, b_glu, w_out, g1, b1, ple_w, w_pg, b_pg, g2, b2, layer):
    L = x.shape[0]

    def body(x_ref, hs_ref, zg_ref, zs0_ref, zs1_ref, zs2_ref, y0_ref, p_ref, wg_ref, bg_ref, wo_ref, g1_ref, b1_ref, pw_ref,
             wpg_ref, bpg_ref, g2_ref, b2_ref, x2_ref, xh1_ref, xh2_ref, q_ref, gt_ref, rstd1_ref, rstd2_ref):
        rg_gate = zg_ref[...]
        s5_gate = jnp.concatenate([zs0_ref[...], zs1_ref[...], zs2_ref[...]], axis=1)
        rg_y = hs_ref[...] * _silu_and_grad(rg_gate)[0]
        y1 = _gelu(y0_ref[...])
        gl = _sigmoid(_mm(y1, wg_ref[...]) + _pick(bg_ref, layer))
        s5_y = (y1 * gl) * _silu_and_grad(s5_gate)[0]
        mix = _mm(jnp.concatenate([rg_y.astype(MXU), s5_y.astype(MXU)], axis=1), wo_ref[...])
        t1 = ALPHA * x_ref[...] + mix
        x1, xh1, rstd1 = _ln_fwd(t1, _pick(g1_ref, layer), _pick(b1_ref, layer))
        q = _mm(p_ref[...], pw_ref[...])
        gt = _sigmoid(_mm(x1, wpg_ref[...]) + _pick(bpg_ref, layer))
        t2 = ALPHA * x1 + q * gt
        x2, xh2, rstd2 = _ln_fwd(t2, _pick(g2_ref, layer), _pick(b2_ref, layer))
        x2_ref[...], xh1_ref[...], xh2_ref[...], q_ref[...], gt_ref[...] = x2, xh1, xh2, q, gt
        rstd1_ref[...], rstd2_ref[...] = rstd1, rstd2

    vec = _lrow(layer, D_MODEL)
    return pl.pallas_call(
        body, name="post_fwd", grid=(L // TM,),
        in_specs=[_row(D_MODEL), _row(RG_W), *_gate_rows(), _row(S5_W), _p_rows(layer), _full((S5_W, S5_W)),
                  _lrow(layer, S5_W), _full((D_MODEL, D_MODEL)), vec, vec, _full((PLE_D, D_MODEL)), _full((D_MODEL, D_MODEL)),
                  vec, vec, vec],
        out_specs=[_row(D_MODEL)] * 5 + [_row(1)] * 2, out_shape=[_S((L, D_MODEL))] * 5 + [_S((L, 1))] * 2,
        compiler_params=_params(1))(x, hs, z, z, z, z, y0, p, w_glu, b_glu, w_out, g1, b1, ple_w, w_pg, b_pg, g2, b2)


def _post_bwd_a(dx2_or_target, is_top, xh2, xh1, rstd2, rstd1, q, gt, p, w_pg, g1, b1, g2, b2, layer, token=None):
    L = xh1.shape[0]
    extra, extra_specs = _after(token)

    def body(d_ref, xh2_ref, xh1_ref, rstd2_ref, rstd1_ref, q_ref, gt_ref, p_ref, wpg_ref, g1_ref, b1_ref, g2_ref,
             b2_ref, *rest):
        (dt1_ref, dpw_out, dwpg_out, dbpg_ref, dg1_ref, db1_ref, dg2_ref, db2_ref, loss_ref, dpw_ref,
         dwpg_ref) = rest[len(extra):]
        @pl.when(pl.program_id(0) == 0)
        def _():
            for ref in (dpw_ref, dwpg_ref, dbpg_ref, dg1_ref, db1_ref, dg2_ref, db2_ref, loss_ref):
                ref[...] = jnp.zeros_like(ref)

        g1, g2 = _pick(g1_ref, layer), _pick(g2_ref, layer)
        xh1, xh2, rstd1, rstd2 = xh1_ref[...], xh2_ref[...], rstd1_ref[...], rstd2_ref[...]
        x1 = xh1 * g1 + _pick(b1_ref, layer)
        if is_top:
            err = (xh2 * g2 + _pick(b2_ref, layer)) - d_ref[...]
            loss_ref[...] += _colsum(err * err)
            dx2 = err * (1.0 / D_MODEL)
        else:
            dx2 = d_ref[...]
        p = p_ref[...]
        q, gt = q_ref[...], gt_ref[...]
        dg2_ref[...] += _colsum(dx2 * xh2)
        db2_ref[...] += _colsum(dx2)
        dt2 = _ln_bwd(dx2, xh2, rstd2, g2)
        dq = dt2 * gt
        dgpre = (dt2 * q) * gt * (1.0 - gt)
        dpw_ref[...] += _mm_tn(p, dq)
        dwpg_ref[...] += _mm_tn(x1, dgpre)
        dbpg_ref[...] += _colsum(dgpre)
        dx1 = ALPHA * dt2 + _mm_nt(dgpre, wpg_ref[...])
        dg1_ref[...] += _colsum(dx1 * xh1)
        db1_ref[...] += _colsum(dx1)
        dt1_ref[...] = _ln_bwd(dx1, xh1, rstd1, g1)

        @pl.when(pl.program_id(0) == L // TM - 1)
        def _():
            dpw_out[...] = dpw_ref[...].astype(WIRE)
            dwpg_out[...] = dwpg_ref[...].astype(WIRE)

    vec, lvec = _full((1, D_MODEL)), _lrow(layer, D_MODEL)
    return pl.pallas_call(
        body, name="post_bwd_a_top" if is_top else "post_bwd_a", grid=(L // TM,),
        in_specs=[_row(D_MODEL), _row(D_MODEL), _row(D_MODEL), _row(1), _row(1), _row(D_MODEL), _row(D_MODEL), _p_rows(layer),
                  _full((D_MODEL, D_MODEL)), lvec, lvec, lvec, lvec] + extra_specs,
        out_specs=[_row(D_MODEL), _full((PLE_D, D_MODEL)), _full((D_MODEL, D_MODEL)), vec, vec, vec, vec, vec, vec],
        out_shape=[_S((L, D_MODEL)), _S((PLE_D, D_MODEL), WIRE), _S((D_MODEL, D_MODEL), WIRE)] + [_S((1, D_MODEL))] * 6,
        scratch_shapes=[pltpu.VMEM((PLE_D, D_MODEL), F32), pltpu.VMEM((D_MODEL, D_MODEL), F32)],
        compiler_params=_params(1))(dx2_or_target, xh2, xh1, rstd2, rstd1, q, gt, p, w_pg, g1, b1, g2, b2, *extra)


def _post_bwd_b(dt1, z, hs, y0, w_out, w_glu, b_glu, layer):
    L = dt1.shape[0]

    def body(dt1_ref, zg_ref, zs0_ref, zs1_ref, zs2_ref, hs_ref, y0_ref, wo_ref, wg_ref, bg_ref,
             dhs_ref, dy0_ref, dzg_ref, dwo_out, dwg_out, dbg_ref, dwo_ref, dwg_ref):
        @pl.when(pl.program_id(0) == 0)
        def _():
            for ref in (dwo_ref, dwg_ref, dbg_ref):
                ref[...] = jnp.zeros_like(ref)

        dt1b = dt1_ref[...].astype(MXU)
        dm = _mm_nt(dt1b, wo_ref[...])
        d_rgy, d_s5y = dm[:, :RG_W], dm[:, RG_W:]
        rg_gate = zg_ref[...]
        s5_gate = jnp.concatenate([zs0_ref[...], zs1_ref[...], zs2_ref[...]], axis=1)
        hs = hs_ref[...]
        sl, dsl = _silu_and_grad(rg_gate)
        dhs_ref[...] = d_rgy * sl
        dzg_ref[:, :RG_W] = d_rgy * hs * dsl
        y0 = y0_ref[...]
        y1 = _gelu(y0)
        gl = _sigmoid(_mm(y1, wg_ref[...]) + _pick(bg_ref, layer))
        y2 = y1 * gl
        sl2, dsl = _silu_and_grad(s5_gate)
        m = jnp.concatenate([(hs * sl).astype(MXU), (y2 * sl2).astype(MXU)], axis=1)
        dwo_ref[...] += _mm_tn(m, dt1b)
        dy2 = d_s5y * sl2
        dzg_ref[:, RG_W:] = d_s5y * y2 * dsl
        dglpre = (dy2 * y1) * gl * (1.0 - gl)
        dwg_ref[...] += _mm_tn(y1, dglpre)
        dbg_ref[...] += _colsum(dglpre)
        dy1 = dy2 * gl + _mm_nt(dglpre, wg_ref[...])
        dy0_ref[...] = dy1 * _gelu_grad(y0)

        @pl.when(pl.program_id(0) == L // TM - 1)
        def _():
            dwo_out[...] = dwo_ref[...].astype(WIRE)
            dwg_out[...] = dwg_ref[...].astype(WIRE)

    return pl.pallas_call(
        body, name="post_bwd_b", grid=(L // TM,),
        in_specs=[_row(D_MODEL), *_gate_rows(), _row(RG_W), _row(S5_W), _full((D_MODEL, D_MODEL)),
                  _full((S5_W, S5_W)), _lrow(layer, S5_W)],
        out_specs=[_row(RG_W), _row(S5_W), _row(D_MODEL), _full((D_MODEL, D_MODEL)), _full((S5_W, S5_W)), _full((1, S5_W))],
        out_shape=[_S((L, RG_W)), _S((L, S5_W)), _S((L, D_MODEL)), _S((D_MODEL, D_MODEL), WIRE), _S((S5_W, S5_W), WIRE),
                   _S((1, S5_W))],
        scratch_shapes=[pltpu.VMEM((D_MODEL, D_MODEL), F32), pltpu.VMEM((S5_W, S5_W), F32)],
        compiler_params=_params(1))(dt1, z, z, z, z, hs, y0, w_out, w_glu, b_glu)


def _adamw(parts, w, m, v, token=None):
    nl = len(parts)
    extra, extra_specs = _after(token)
    n, R, C = parts[0].shape
    tr = R
    for cand in (512, 256, 128, 64, 32, 16, 8):
        if R % cand == 0 and n * cand * C * 4 <= 4 * 1024 * 1024:
            tr = cand
            break
    nblk = R // tr

    def body(*refs):
        p_refs = refs[:nl]
        w_ref, m_ref, v_ref = refs[nl:nl + 3]
        g_ref, d_ref, nm_ref, nv_ref = refs[nl + 3 + len(extra):]
        layer = pl.program_id(0)
        g = None
        for li, p_ref in enumerate(p_refs):
            s = p_ref[0].astype(F32)
            for k in range(1, n):
                s = s + p_ref[k].astype(F32)
            g = s if g is None else jnp.where(layer == li, s, g)
        nm = B1 * m_ref[...] + (1.0 - B1) * g
        nv = B2 * v_ref[...] + (1.0 - B2) * (g * g)
        d_ref[...] = (-LR) * ((nm / BC1) / (jnp.sqrt(nv / BC2) + EPS) + WD * w_ref[...])
        g_ref[...], nm_ref[...], nv_ref[...] = g, nm, nv

    def part_spec(li):
        return pl.BlockSpec((n, tr, C), lambda l, i: (0, jnp.where(l == li, i, jnp.where(l < li, 0, nblk - 1)), 0))

    blk = pl.BlockSpec((tr, C), lambda l, i: (l * nblk + i, 0))
    return pl.pallas_call(
        body, name="adamw", grid=(nl, nblk),
        in_specs=[part_spec(li) for li in range(nl)] + [blk, blk, blk] + extra_specs,
        out_specs=[blk] * 4, out_shape=[_S((nl * R, C))] * 4, compiler_params=_params(2))(*parts, w, m, v, *extra)


def _adamw_natural(names, g, w, m, v, name):
    n = len(names)

    def body(*refs):
        for j in range(n):
            g_ref, w_ref, m_ref, v_ref, d_ref, nm_ref, nv_ref = (refs[k * n + j] for k in range(7))
            gj = g_ref[...]
            nm = B1 * m_ref[...] + (1.0 - B1) * gj
            nv = B2 * v_ref[...] + (1.0 - B2) * (gj * gj)
            d_ref[...] = (-LR) * ((nm / BC1) / (jnp.sqrt(nv / BC2) + EPS) + WD * w_ref[...])
            nm_ref[...], nv_ref[...] = nm, nv

    ins = [t[k] for t in (g, w, m, v) for k in names]
    outs = pl.pallas_call(body, name=name, out_shape=[_S(w[k].shape) for _ in range(3) for k in names],
                          compiler_params=pltpu.CompilerParams(vmem_limit_bytes=VMEM_LIMIT))(*ins)
    return [{k: outs[t * n + j] for j, k in enumerate(names)} for t in range(3)]


def _me():
    return lax.axis_index("x"), lax.axis_index("y"), lax.axis_index("c")


def _lin(dev):
    return 4 * dev[0] + 2 * dev[1] + dev[2]


def _blk(ref, axis, size, idx):
    nd = len(ref.shape)
    start = idx * size
    if axis == nd - 1 and size % LANE == 0:
        start = pl.multiple_of(start, LANE)
    elif axis == nd - 2 and size % 16 == 0:
        start = pl.multiple_of(start, 16)
    ix = [slice(None)] * nd
    ix[axis] = pl.ds(start, size)
    return ref.at[tuple(ix)]


HBM_SPEC = pl.BlockSpec(memory_space=pltpu.HBM)
SEM_SPEC = pl.BlockSpec(memory_space=pltpu.SEMAPHORE)
EFFECT = pltpu.SideEffectType.DATAFLOW_SIDE_EFFECTING


def _peers(x, y, c):
    flip = lambda v, f: 1 - v if f else v
    return [(flip(x, k & 4), flip(y, k & 2), flip(c, k & 1)) for k in range(1, N_DEV)]


def _land_shape(mode, s, axis):
    if mode == "gather":
        return s.shape[:axis] + (N_DEV * s.shape[axis],) + s.shape[axis + 1:]
    return (N_DEV,) + s.shape[:axis] + (s.shape[axis] // N_DEV,) + s.shape[axis + 1:]


def _src_view(mode, ref, axis, peer):
    return ref if mode == "gather" else _blk(ref, axis, ref.shape[axis] // N_DEV, peer)


def _dst_view(mode, land, axis, sender):
    return _blk(land, axis, land.shape[axis] // N_DEV, sender) if mode == "gather" else land.at[sender]


def _blocks(mode, land, axis, k):
    if mode == "gather":
        ix = [slice(None)] * len(land.shape)
        ix[axis] = pl.ds(0, k * (land.shape[axis] // N_DEV))
        return land.at[tuple(ix)]
    return land.at[pl.ds(0, k)]


ARRIVALS = {None: N_DEV - 1, "near": 4, "relay": 3}


def _routes(route, x, y, c):
    me, sibling = (x, y, c), (x, y, 1 - c)
    chips = [(1 - x, y), (x, 1 - y), (1 - x, 1 - y)]
    if route == "near":
        return [(me, sibling)] + [(me, (*chip, c)) for chip in chips]
    if route == "relay":
        return [((*chip, c), sibling) for chip in chips]
    return [(me, peer) for peer in _peers(x, y, c)]


def _place_own(mode, srcs, axes, name, after=None):
    n = len(srcs)
    extra, extra_specs = _after(after)

    def body(me_ref, *refs):
        for a in range(n):
            out = refs[n + len(extra) + a]
            out[...] = refs[a][...].reshape(out.shape)

    def at_me(shape, axis):
        return lambda i, me: tuple(me[0] if d == axis else 0 for d in range(len(shape)))

    in_specs, out_specs = [], []
    for s, axis in zip(srcs, axes):
        if mode == "gather":
            in_specs.append(pl.BlockSpec(s.shape, lambda i, me, nd=len(s.shape): (0,) * nd))
            out_specs.append(pl.BlockSpec(s.shape, at_me(s.shape, axis)))
        else:
            blk = s.shape[:axis] + (s.shape[axis] // N_DEV,) + s.shape[axis + 1:]
            in_specs.append(pl.BlockSpec(blk, at_me(blk, axis)))
            out_specs.append(pl.BlockSpec((1,) + blk, at_me((1,) + blk, 0)))
    me = _lin(_me()).astype(jnp.int32).reshape(1)
    return pl.pallas_call(
        body, name=name, out_shape=[_S(_land_shape(mode, s, a), s.dtype) for s, a in zip(srcs, axes)],
        grid_spec=pltpu.PrefetchScalarGridSpec(num_scalar_prefetch=1, grid=(1,), in_specs=in_specs + extra_specs,
                                               out_specs=out_specs),
        compiler_params=_params(1))(me, *srcs, *extra)


def _place_shards(shards, layers, axes, dtypes, name, after=None):
    n = len(shards)
    extra, extra_specs = _after(after)

    def body(me_ref, *refs):
        for a in range(n):
            out = refs[n + len(extra) + a]
            out[...] = refs[a][...].astype(out.dtype)

    in_specs, out_specs, out_shape = [], [], []
    for s, layer, axis, dt in zip(shards, layers, axes, dtypes):
        shape = s.shape if layer is None else s.shape[1:]
        nd = len(shape)
        if layer is None:
            in_specs.append(pl.BlockSpec(shape, lambda i, me, nd=nd: (0,) * nd))
        else:
            in_specs.append(pl.BlockSpec((None,) + shape, lambda i, me, nd=nd, layer=layer: (layer,) + (0,) * nd))
        out_specs.append(pl.BlockSpec(shape, lambda i, me, nd=nd, axis=axis: tuple(me[0] if d == axis else 0 for d in range(nd))))
        out_shape.append(_S(shape[:axis] + (N_DEV * shape[axis],) + shape[axis + 1:], dt))
    me = _lin(_me()).astype(jnp.int32).reshape(1)
    return pl.pallas_call(
        body, name=name, out_shape=out_shape,
        grid_spec=pltpu.PrefetchScalarGridSpec(num_scalar_prefetch=1, grid=(1,), in_specs=in_specs + extra_specs,
                                               out_specs=out_specs),
        compiler_params=_params(1))(me, *shards, *extra)


def _push_start(mode, srcs, lands, axes, name, route=None):
    n, ns = len(lands), len(srcs)

    def body(*refs):
        src_refs, land_refs = refs[:ns], refs[ns:ns + n]
        send_sems, recv_sems = refs[ns + n], refs[ns + n + 1]
        token = refs[-1]
        x, y, c = _me()
        for a in range(n):
            for block, peer in _routes(route, x, y, c):
                there = _dst_view(mode, land_refs[a], axes[a], _lin(block))
                pltpu.make_async_remote_copy(
                    src_ref=_src_view(mode, src_refs[a], axes[a], _lin(peer)) if ns else there, dst_ref=there,
                    send_sem=send_sems.at[a], recv_sem=recv_sems.at[a], device_id=peer, device_id_type=MESH).start()
        token[...] = jnp.zeros_like(token)

    hbm = lambda s: pltpu.HBM(s.shape, s.dtype)
    outs = pl.pallas_call(
        body, name=name,
        out_shape=(pltpu.SemaphoreType.DMA((n,)), pltpu.SemaphoreType.DMA((n,)), *[hbm(s) for s in srcs], *[hbm(s) for s in lands],
                   _S((SUB, LANE))),
        in_specs=[HBM_SPEC] * (ns + n),
        out_specs=(SEM_SPEC, SEM_SPEC, *[HBM_SPEC] * (ns + n), pl.BlockSpec(memory_space=pltpu.VMEM)),
        input_output_aliases={i: 2 + i for i in range(ns + n)},
        compiler_params=pltpu.CompilerParams(has_side_effects=EFFECT),
    )(*[pltpu.with_memory_space_constraint(s, pltpu.HBM) for s in list(srcs) + list(lands)])
    return outs[0], outs[1], outs[2:2 + ns], outs[2 + ns:2 + ns + n], outs[-1]


def _push_wait(mode, send_sems, recv_sems, srcs, lands, axes, after, name, first=0, route=None):
    n, ns = len(lands), len(srcs)
    after = list(after) if isinstance(after, (list, tuple)) else [after]

    def body(*refs):
        land_refs = refs[ns:ns + n]
        send_sems, recv_sems = refs[ns + n], refs[ns + n + 1]
        x, y, c = _me()
        for a in range(n):
            seven = _blocks(mode, land_refs[a], axes[a], ARRIVALS[route])
            cp = pltpu.make_async_remote_copy(src_ref=seven, dst_ref=seven, send_sem=send_sems.at[first + a],
                                              recv_sem=recv_sems.at[first + a],
                                              device_id=(x, y, 1 - c), device_id_type=MESH)
            cp.wait_send()
            cp.wait_recv()

    hbm = lambda s: pltpu.HBM(s.shape, s.dtype)
    outs = pl.pallas_call(
        body, name=name, out_shape=tuple(hbm(s) for s in list(srcs) + list(lands)),
        in_specs=[HBM_SPEC] * (ns + n) + [SEM_SPEC, SEM_SPEC] + [ANY] * len(after), out_specs=tuple([HBM_SPEC] * (ns + n)),
        input_output_aliases={i: i for i in range(ns + n)},
        compiler_params=pltpu.CompilerParams(has_side_effects=EFFECT),
    )(*srcs, *lands, send_sems, recv_sems, *after)
    return outs[ns:]


def _sum_parts(parts):
    n, R, C = parts.shape

    def body(p_ref, o_ref):
        g = p_ref[0]
        for k in range(1, n):
            g = g + p_ref[k]
        o_ref[...] = g

    return pl.pallas_call(body, name="sum_parts", out_shape=_S((R, C)))(parts)


SMALL =['conv_b', 'rg_wa', 'rg_ba', 'rg_wx', 'rg_bx', 'rg_lambda', 's5_a_re', 's5_a_im', 's5_b_re', 's5_b_im',
         's5_c_re', 's5_c_im', 's5_d', 's5_log_step', 's5_b_glu', 'ln1_g', 'ln1_b', 'ple_gate_b', 'ln2_g', 'ln2_b']
WEIGHTS = ['w_in', 'conv_w', 'conv_b', 'rg_wa', 'rg_ba', 'rg_wx', 'rg_bx', 'rg_lambda', 's5_a_re', 's5_a_im', 's5_b_re',
           's5_b_im', 's5_c_re', 's5_c_im', 's5_d', 's5_log_step', 's5_w_glu', 's5_b_glu', 'w_out', 'ln1_g', 'ln1_b',
           'ple_w', 'ple_gate_w', 'ple_gate_b', 'ln2_g', 'ln2_b']
PACK_ROWS_MULT = 64


STORED = {'s5_b_re': (2, 3), 's5_b_im': (2, 3), 's5_d': (1, 2)}


def _stored(k, a):
    return jnp.swapaxes(a, *STORED[k]) if k in STORED else a


def _tile_rows(n):
    return -(-n // (SUB * LANE)) * SUB


def _pack(tree, scalar):
    parts = []
    for a in [tree[k] for k in SMALL] + [scalar.reshape(1)]:
        rows = _tile_rows(a.size)
        parts.append(jnp.pad(a.reshape(-1), (0, rows * LANE - a.size)).reshape(rows, LANE))
    rows = sum(p.shape[0] for p in parts)
    parts.append(jnp.zeros((-rows % PACK_ROWS_MULT, LANE), F32))
    return jnp.concatenate(parts, axis=0)


def _unpack(packed, like):
    out, r = {}, 0
    for k in SMALL:
        n = math.prod(like[k].shape)
        rows = _tile_rows(n)
        part = packed[r:r + rows]
        out[k] = (part if n == rows * LANE else part.reshape(-1)[:n]).reshape(like[k].shape)
        r += rows
    return out, packed[r, 0]


class _NoHooks:
    token = None
    first_token = None

    def first_weights(self, full, after):
        return full

    def layer_start(self, i, W, after):
        return W

    def late_weights(self, i, W, after):
        return W

    def post_done(self, i, g):
        return None

    def smalls_done(self, grads, loss):
        self.small = _small_grads(grads, self.res)
        return None

    def w_in_done(self, i, g):
        return None

    def layer_done(self, i, g, dx):
        return None


def _local_grads(x, p, target, W, disc, hooks):
    depth = 2
    saved = []
    for i in range(depth):
        if i > 0:
            W = hooks.layer_start(i, W, x)
        w = W[i]
        z = _inproj_fwd(x, w['w_in'], hooks.token if i == 0 else None)
        hs, *gates = _rg_fwd(z, w['conv_w'], w['conv_b'], w['wa_bd'], w['wx_bd'], w['rg_ba'], w['rg_bx'], w['rg_lambda'], i)
        d = disc[i]
        y0, s_re, s_im = _s5_fwd(z, d['bb_re'], d['bb_im'], d['lb_re'], d['lb_im'], d['c_re'], d['c_im'], w['s5_d'], i)
        W = hooks.late_weights(i, W, y0)
        w = W[i]
        x2, *norms = _post_fwd(x, hs, z, y0, p, w['s5_w_glu'], w['s5_b_glu'], w['w_out'], w['ln1_g'], w['ln1_b'],
                               w['ple_w'], w['ple_gate_w'], w['ple_gate_b'], w['ln2_g'], w['ln2_b'], i)
        saved.append((x, z, hs, gates, y0, s_re, s_im, norms))
        x = x2

    grads = [None] * depth
    dx = target
    loss = None
    token = None
    for i in reversed(range(depth)):
        w, d = W[i], disc[i]
        xin, z, hs, gates, y0, s_re, s_im, (xh1, xh2, q, gt, rstd1, rstd2) = saved[i]
        g = {}
        (dt1, g['ple_w'], g['ple_gate_w'], g['ple_gate_b'], g['ln1_g'], g['ln1_b'], g['ln2_g'], g['ln2_b'], lrow) = _post_bwd_a(
            dx, i == depth - 1, xh2, xh1, rstd2, rstd1, q, gt, p, w['ple_gate_w'], w['ln1_g'], w['ln1_b'],
            w['ln2_g'], w['ln2_b'], i, token)
        if i == depth - 1:
            loss = 0.5 / D_MODEL * jnp.sum(lrow)
        dhs, dy0, dzg, g['w_out'], g['s5_w_glu'], g['s5_b_glu'] = _post_bwd_b(dt1, z, hs, y0, w['w_out'], w['s5_w_glu'],
                                                                           w['s5_b_glu'], i)
        (dzu, g['bb_re'], g['bb_im'], g['lb_re'], g['lb_im'], g['c_re'], g['c_im'], g['s5_d']) = _s5_bwd(
            dy0, z, s_re, s_im, d['bb_re'], d['bb_im'], d['lb_re'], d['lb_im'], d['c_re'], d['c_im'], w['s5_d'], i,
            hooks.post_done(i, g))
        (dzx, g['conv_w'], g['conv_b'], g['wa_bd'], g['wx_bd'], g['rg_ba'], g['rg_bx'], g['rg_lambda']) = _rg_bwd(
            dhs, z, hs, gates, w['conv_w'], w['wa_bd'], w['wx_bd'], w['rg_lambda'], i)
        if i == 0:
            g['w_in'] = _inproj_bwd_dw(xin, dzx, dzg, dzu, hooks.smalls_done([g, grads[1]], loss))
            dx = _inproj_bwd_dx(dt1, dzx, dzg, dzu, w['w_in'], hooks.w_in_done(i, g))
        else:
            dx, g['w_in'] = _inproj_bwd(dt1, xin, dzx, dzg, dzu, w['w_in'])
        grads[i] = g
        token = hooks.layer_done(i, g, dx)
    return loss, dx, grads


def _s5_layouts_fwd(s5_a_re, s5_a_im, s5_log_step, s5_b_re, s5_b_im, s5_c_re, s5_c_im, token=None):
    depth = s5_a_re.shape[0]
    ar, ai = s5_a_re.reshape(depth * 24, S5_P), s5_a_im.reshape(depth * 24, S5_P)
    ls = s5_log_step.reshape(depth * 24, 1)
    lr, li, cr, ci = _s5_disc_fwd(ar, ai, ls, token)
    per_group = lambda a: a.reshape(depth * 24, 1, S5_P)
    as_c = lambda b: jnp.swapaxes(b, 2, 3).reshape(depth * 24, S5_H, S5_P)
    res = (ar, ai, ls, per_group(cr), per_group(ci), as_c(s5_b_re), as_c(s5_b_im))
    bbr, bbi = _s5_bscale_fwd(*res[3:])
    tiles = lambda a: a.reshape(depth * N_S5_T, S5_GT, S5_H, S5_P)
    rows = lambda a: a.reshape(depth * N_S5_T, S5_GT, S5_P)
    disc = dict(bb_re=tiles(bbr), bb_im=tiles(bbi), lb_re=rows(lr), lb_im=rows(li), c_re=tiles(s5_c_re), c_im=tiles(s5_c_im))
    return [disc] * depth, res


def _s5_layouts_bwd(grads, res):
    ar, ai, ls, cr, ci, br, bi = res
    depth = len(grads)
    stack = lambda k, shape: jnp.stack([g[k] for g in grads]).reshape(shape)
    groups, shape_c = (depth * 24, S5_H, S5_P), (depth, 24, S5_H, S5_P)
    dbr, dbi, dcr, dci = _s5_bscale_bwd(cr, ci, br, bi, stack('bb_re', groups), stack('bb_im', groups))
    gp = (depth * 24, S5_P)
    dar, dai, dls = _s5_disc_bwd(ar, ai, ls, stack('lb_re', gp), stack('lb_im', gp), dcr.reshape(gp), dci.reshape(gp))
    return dict(
        s5_a_re=dar.reshape(depth, 24, S5_P), s5_a_im=dai.reshape(depth, 24, S5_P), s5_log_step=dls.reshape(depth, 24),
        s5_b_re=dbr.reshape(shape_c), s5_b_im=dbi.reshape(shape_c),
        s5_c_re=stack('c_re', shape_c), s5_c_im=stack('c_im', shape_c))


LATE = ('w_out', 'ple_w', 'ple_gate_w', 's5_w_glu')


ROWS = ('conv_b', 'rg_ba', 'rg_bx', 'rg_lambda', 's5_d', 's5_b_glu', 'ln1_g', 'ln1_b', 'ple_gate_b', 'ln2_g', 'ln2_b')


def _shared_weights(full):
    shared = {k: full[k] for k in ROWS}
    shared.update(conv_w=full['conv_w'], wa_bd=full['rg_wa'], wx_bd=full['rg_wx'], s5_d=full['s5_d'].reshape(DEPTH, 1, S5_W))
    return shared


def _layer_weights(full, shared, i):
    return dict(shared, w_in=full['w_in'][i])


class _AllLocal(_NoHooks):
    def __init__(self, full):
        self.full = full

    def late_weights(self, i, W, after):
        W[i].update({k: self.full[k][i] for k in LATE})
        return W


def _full_grads(full, x, p, target, hooks=None):
    hooks = hooks or _AllLocal(full)
    disc, res = _s5_layouts_fwd(full['s5_a_re'], full['s5_a_im'], full['s5_log_step'], full['s5_b_re'], full['s5_b_im'],
                                full['s5_c_re'], full['s5_c_im'], hooks.first_token)
    full = hooks.first_weights(full, disc[-1]['bb_im'])
    shared = _shared_weights(full)
    W = [_layer_weights(full, shared, i) for i in range(2)]
    hooks.res = res
    loss, gx, grads = _local_grads(x, p, target, W, disc, hooks)
    out = dict(hooks.small)
    for k in SHARD_AXIS:
        out[k] = [g[k] for g in grads]
    return loss, gx, out


def _small_grads(grads, res):
    stack = lambda f: jnp.stack([f(g) for g in grads])
    out = _s5_layouts_bwd(grads, res)
    out['conv_w'] = stack(lambda g: g['conv_w'])
    for k in ('conv_b', 'rg_ba', 'rg_bx', 'rg_lambda', 's5_b_glu', 'ln1_g', 'ln1_b', 'ple_gate_b', 'ln2_g', 'ln2_b'):
        out[k] = stack(lambda g: g[k][0])
    out['s5_d'] = _stored('s5_d', stack(lambda g: g['s5_d'][0]).reshape(2, 24, 16))
    out['rg_wa'] = stack(lambda g: g['wa_bd'])
    out['rg_wx'] = stack(lambda g: g['wx_bd'])
    return out


SHARD_AXIS = {'w_in': 2, 'w_out': 1, 'ple_w': 2, 'ple_gate_w': 1, 's5_w_glu': 1}


def kernel(x, p, w_in, conv_w, conv_b, rg_wa, rg_ba, rg_wx, rg_bx, rg_lambda, s5_a_re, s5_a_im, s5_b_re, s5_b_im, s5_c_re, s5_c_im, s5_d, s5_log_step, s5_w_glu, s5_b_glu, w_out, ln1_g, ln1_b, ple_w, ple_gate_w, ple_gate_b, ln2_g, ln2_b, loss_target, m_w_in, m_conv_w, m_conv_b, m_rg_wa, m_rg_ba, m_rg_wx, m_rg_bx, m_rg_lambda, m_s5_a_re, m_s5_a_im, m_s5_b_re, m_s5_b_im, m_s5_c_re, m_s5_c_im, m_s5_d, m_s5_log_step, m_s5_w_glu, m_s5_b_glu, m_w_out, m_ln1_g, m_ln1_b, m_ple_w, m_ple_gate_w, m_ple_gate_b, m_ln2_g, m_ln2_b, v_w_in, v_conv_w, v_conv_b, v_rg_wa, v_rg_ba, v_rg_wx, v_rg_bx, v_rg_lambda, v_s5_a_re, v_s5_a_im, v_s5_b_re, v_s5_b_im, v_s5_c_re, v_s5_c_im, v_s5_d, v_s5_log_step, v_s5_w_glu, v_s5_b_glu, v_w_out, v_ln1_g, v_ln1_b, v_ple_w, v_ple_gate_w, v_ple_gate_b, v_ln2_g, v_ln2_b):
    local = dict(locals())
    w = {k: local[k] for k in WEIGHTS}
    mom = {k: local['m_' + k] for k in WEIGHTS}
    var = {k: local['v_' + k] for k in WEIGHTS}

    big = list(SHARD_AXIS)
    late_axes = [SHARD_AXIS[k] - 1 for k in LATE]
    pushed = {}

    groups = dict(first=(['w_in', 'conv_w'], [0, None], [1, 0]), l0=(list(LATE), [0] * len(LATE), late_axes),
                  l1=(['w_in'] + list(LATE), [1] * (1 + len(LATE)), [1] + late_axes))
    token = None
    for key, members in (("first", ["first"]), ("rest", ["l0", "l1"])):
        names, layers, axes = (sum((groups[m][j] for m in members), []) for j in range(3))
        shards = [w[k] if layer is not None else w[k][None] for k, layer in zip(names, layers)]
        lands = _place_shards(shards, layers, axes, [WIRE if k in big else w[k].dtype for k in names],
                              "place_weights_" + key, token)
        pushed[key] = _push_start("gather", [], lands, axes, "push_weights_" + key, "near" if key == "first" else None)
        token = pushed[key][4]

    def await_weights(key, axes, after):
        s, first = (pushed["first"], 0) if key == "first" else (pushed["rest"], 0 if key == "l0" else len(LATE))
        return _push_wait("gather", s[0], s[1], [], s[3][first:first + len(axes)], axes, after, "await_weights_" + key, first)

    def push_grads(key, g, names, axes):
        srcs = [g[k] for k in names]
        pushed[key] = _push_start("scatter", srcs, _place_own("scatter", srcs, axes, "place_grads_" + key), axes,
                                  "push_grads_" + key)
        return pushed[key][4]

    def await_grads(key, axes, after):
        s = pushed[key]
        return _push_wait("scatter", s[0], s[1], s[2], s[3], axes, after, "await_grads_" + key)

    class Overlap(_NoHooks):
        token = pushed["rest"][4]
        first_token = token

        def first_weights(self, full, after):
            s, axes = pushed["first"], [1, 0]
            near = _push_wait("gather", s[0], s[1], [], s[3], axes, after, "await_weights_near", route="near")
            s = _push_start("gather", [], near, axes, "relay_weights", "relay")
            w_in0, conv = _push_wait("gather", s[0], s[1], [], s[3], axes, s[4], "await_weights_relay", route="relay")
            return dict(full, w_in=[w_in0, None], conv_w=jnp.moveaxis(conv, 0, 2).reshape(2, 4, RG_W))

        def late_weights(self, i, W, after):
            if i == 0:
                W[0].update(zip(LATE, await_weights("l0", late_axes, after)))
            return W

        def layer_start(self, i, W, after):
            lands = await_weights("l1", [1] + late_axes, after)
            W[1].update(zip(LATE, lands[1:]), w_in=lands[0])
            return W

        def post_done(self, i, g):
            return push_grads("late0", g, LATE, late_axes) if i == 0 else None

        def smalls_done(self, grads, loss):
            super().smalls_done(grads, loss)
            conv = jnp.moveaxis(self.small['conv_w'].reshape(2, 4, N_DEV, RG_W // N_DEV), 2, 0)
            self.packed = _pack(self.small, loss)
            return push_grads("small", dict(conv_w=conv.reshape(N_DEV, 8, RG_W // N_DEV), small=self.packed),
                              ['conv_w', 'small'], [0, 0])

        def w_in_done(self, i, g):
            return push_grads("w_in0", g, ['w_in'], [0])

        def layer_done(self, i, g, dx):
            return push_grads("all1", g, ['w_in'] + list(LATE), [0] + late_axes) if i == 1 else None

    hooks = Overlap()
    _, grad_x, g = _full_grads(dict(w), x[0], p, loss_target[0], hooks)

    recv1 = dict(zip(['w_in'] + list(LATE), await_grads("all1", [0] + late_axes, grad_x)))
    recv0 = dict(zip(LATE, await_grads("late0", late_axes, grad_x)))
    outs = {}

    def update(k, parts, token=None):
        shard = w[k].shape
        c = shard[-1]
        two = lambda a: a.reshape(-1, c)
        res = _adamw([r.reshape(N_DEV, -1, c) for r in parts], two(w[k]), two(mom[k]), two(var[k]), token)
        outs[k] = [o.reshape(shard) for o in res]

    conv_parts, small_parts = await_grads("small", [0, 0], grad_x)
    rows = hooks.packed.shape[0] // N_DEV
    mine = _sum_parts(small_parts.reshape(N_DEV, rows, LANE))
    sums = _push_start("gather", [mine], _place_own("gather", [mine], [0], "place_small_sums"), [0], "push_small_sums")
    token = sums[4]
    for k in LATE:
        update(k, [recv0[k], recv1[k]], token)
        token = None
    w_in0, = await_grads("w_in0", [0], [outs[k][1] for k in LATE])
    update('w_in', [w_in0, recv1['w_in']])
    update('conv_w', [conv_parts])
    gathered, = _push_wait("gather", sums[0], sums[1], sums[2], sums[3], [0], [outs['w_in'][1], outs['conv_w'][1]],
                           "await_small_sums")
    stored = [{k: _stored(k, t[k]) for k in SMALL} for t in (w, mom, var)]
    summed, loss = _unpack(gathered, stored[0])
    wide = ['s5_b_re', 's5_b_im']
    for names, name in ((wide, "adamw_s5_b"), ([k for k in SMALL if k not in wide], "adamw_small")):
        delta, new_m, new_v = _adamw_natural(names, summed, *stored, name)
        for k in names:
            outs[k] = [_stored(k, o[k]) for o in (summed, delta, new_m, new_v)]

    res = [loss, grad_x[None]]
    for j in range(4):
        res += [outs[k][j] for k in WEIGHTS]
    return tuple(res)
```

```python
import math

import jax
import jax.numpy as jnp
from jax import lax
from jax.experimental import pallas as pl
from jax.experimental.pallas import tpu as pltpu

F32 = jnp.float32
MXU = jnp.bfloat16
WIRE = jnp.bfloat16

N_DEV = 8
D_MODEL = 1024
PLE_D = 256
RG_W = 640
S5_W = 384
S5_P = 64
S5_N = 24 * S5_P
Z_W = 2 * RG_W + 2 * S5_W
C_RGG = RG_W
C_S5U = 2 * RG_W
C_S5G = 2 * RG_W + S5_W
LANE = 128
N_RG_T = RG_W // LANE
N_S5_T = S5_W // LANE
W_BLK = Z_W // N_DEV
ALPHA = (2.0 * 2) ** 0.25
LN_EPS = 1e-5
RG_C = 8.0
LR, B1, B2, EPS, WD, STEP = 0.001, 0.9, 0.999, 1e-08, 0.01, 10
BC1 = 1.0 - B1 ** STEP
BC2 = 1.0 - B2 ** STEP
RC = 512
RC_RG = 1024
TM = 512
TM_MM = 1024
VMEM_LIMIT = 56 * 1024 * 1024

MESH = pl.DeviceIdType.MESH
ANY = pl.BlockSpec(memory_space=pl.ANY)


def _params(n_grid_axes, vmem=VMEM_LIMIT):
    return pltpu.CompilerParams(dimension_semantics=("arbitrary",) * n_grid_axes, vmem_limit_bytes=vmem)


def _S(shape, dtype=F32):
    return jax.ShapeDtypeStruct(tuple(shape), dtype)


def _sigmoid(x):
    return 0.5 * jnp.tanh(0.5 * x) + 0.5


def _silu_and_grad(x):
    s = _sigmoid(x)
    return x * s, s * (1.0 + x * (1.0 - s))


_GELU_C = math.sqrt(2.0 / math.pi)


def _gelu(x):
    return 0.5 * x * (1.0 + jnp.tanh(_GELU_C * (x + 0.044715 * (x * x * x))))


def _gelu_grad(x):
    th = jnp.tanh(_GELU_C * (x + 0.044715 * (x * x * x)))
    return 0.5 * (1.0 + th) + 0.5 * x * (1.0 - th * th) * (_GELU_C * (1.0 + 3.0 * 0.044715 * (x * x)))


def _mm(a, b):
    return jnp.dot(a.astype(MXU), b.astype(MXU), preferred_element_type=F32)


def _mm_nt(a, b):
    return lax.dot_general(a.astype(MXU), b.astype(MXU), (((1,), (1,)), ((), ())), preferred_element_type=F32)


def _mm_tn(a, b):
    return lax.dot_general(a.astype(MXU), b.astype(MXU), (((0,), (0,)), ((), ())), preferred_element_type=F32)


def _ln_fwd(t, g, b):
    mu = jnp.mean(t, axis=-1, keepdims=True)
    tc = t - mu
    var = jnp.mean(tc * tc, axis=-1, keepdims=True)
    rstd = lax.rsqrt(var + LN_EPS)
    xhat = tc * rstd
    return xhat * g + b, xhat, rstd


def _ln_bwd(dy, xhat, rstd, g):
    dxh = dy * g
    m1 = jnp.mean(dxh, axis=-1, keepdims=True)
    m2 = jnp.mean(dxh * xhat, axis=-1, keepdims=True)
    return rstd * (dxh - m1 - xhat * m2)


def _colsum(a):
    return jnp.sum(a, axis=0, keepdims=True)


def _up(x, d, rows, fill):
    n = x.shape[0]
    return jnp.where(rows < n - d, pltpu.roll(x, n - d, 0), fill)


SUB = 8
TILE_STEPS = (1, 2, 4)


def _r8(width):
    return lax.broadcasted_iota(jnp.int32, (SUB, width), 0)


def _scan_real(a, u, carry, reverse=False):
    r8 = _r8(a.shape[1])
    n = a.shape[0] // SUB
    outs = [None] * n
    for k in (reversed(range(n)) if reverse else range(n)):
        A, U = a[SUB * k:SUB * k + SUB], u[SUB * k:SUB * k + SUB]
        for d in TILE_STEPS:
            m = (r8 < SUB - d) if reverse else (r8 >= d)
            sh = SUB - d if reverse else d
            U = A * jnp.where(m, pltpu.roll(U, sh, 0), 0.0) + U
            A = A * jnp.where(m, pltpu.roll(A, sh, 0), 1.0)
        h = A * carry + U
        outs[k] = h
        carry = h[0:1] if reverse else h[SUB - 1:SUB]
    return jnp.concatenate(outs, axis=0), carry


def _tile_powers(lr, li, reverse=False):
    width = lr.shape[1]
    r8 = _r8(width)
    steps = []
    pr, pi = lr, li
    er, ei = jnp.broadcast_to(lr, (SUB, width)), jnp.broadcast_to(li, (SUB, width))
    for d in TILE_STEPS:
        m = (r8 < SUB - d) if reverse else (r8 >= d)
        sh = SUB - d if reverse else d
        steps.append((sh, jnp.where(m, pr, 0.0), jnp.where(m, pi, 0.0)))
        er, ei = _cmul(er, ei, jnp.where(m, pltpu.roll(er, sh, 0), 1.0), jnp.where(m, pltpu.roll(ei, sh, 0), 0.0))
        pr, pi = _cmul(pr, pi, pr, pi)
    return steps, (er, ei)


def _scan_lti(xr, xi, carry, steps, e, reverse=False):
    er, ei = e
    kr, ki = carry
    n = xr.shape[0] // SUB
    outr, outi = [None] * n, [None] * n
    for k in (reversed(range(n)) if reverse else range(n)):
        sr, si = xr[SUB * k:SUB * k + SUB], xi[SUB * k:SUB * k + SUB]
        for sh, pr, pi in steps:
            shr, shi = pltpu.roll(sr, sh, 0), pltpu.roll(si, sh, 0)
            sr, si = sr + (pr * shr - pi * shi), si + (pr * shi + pi * shr)
        sr = sr + (er * kr - ei * ki)
        si = si + (er * ki + ei * kr)
        outr[k], outi[k] = sr, si
        kr, ki = (sr[0:1], si[0:1]) if reverse else (sr[SUB - 1:SUB], si[SUB - 1:SUB])
    return jnp.concatenate(outr, axis=0), jnp.concatenate(outi, axis=0), (kr, ki)


def _halo(ref, c, r0):
    rp = pl.multiple_of(jnp.maximum(r0 - 8, 0), 8)
    return jnp.where(c > 0, ref[pl.ds(rp, 8), :], 0.0)


def _conv_taps(xe):
    return [pltpu.roll(xe, 3, 0)[8:, :], pltpu.roll(xe, 2, 0)[8:, :], pltpu.roll(xe, 1, 0)[8:, :], xe[8:, :]]


def _rg_gates(h, wa, wx, ba, bx, sp):
    r = _sigmoid(_mm(h, wa) + ba)
    i = _sigmoid(_mm(h, wx) + bx)
    log_a = (-RG_C) * r * sp
    a = jnp.exp(log_a)
    mult = jnp.sqrt(-jnp.tanh(log_a) * (a * a + 1.0))
    return r, i, a, mult


def _softplus(y):
    return jnp.maximum(y, 0.0) + jnp.log1p(jnp.exp(-jnp.abs(y)))


def _after(token):
    return ([], []) if token is None else ([token], [ANY])


def _inproj_fwd(x, w_in, token=None):
    L = x.shape[0]

    def body(x_ref, w_ref, *rest):
        rest[-1][...] = _mm(x_ref[...], w_ref[...])

    extra, extra_specs = _after(token)
    tm = min(TM_MM, L)
    return pl.pallas_call(
        body, name="inproj_fwd", grid=(L // tm,),
        in_specs=[pl.BlockSpec((tm, D_MODEL), lambda i: (i, 0)), pl.BlockSpec((D_MODEL, Z_W), lambda i: (0, 0))] + extra_specs,
        out_specs=pl.BlockSpec((tm, Z_W), lambda i: (i, 0)),
        out_shape=_S((L, Z_W)), compiler_params=_params(1))(x, w_in, *extra)


def _inproj_bwd(dt1, x, dzx, dzg, dzu, w_in):
    L = x.shape[0]

    def body(dt1_ref, x_ref, dzx_ref, dzg_ref, dzu_ref, w_ref, dx_ref, dw_ref, acc_ref):
        @pl.when(pl.program_id(0) == 0)
        def _():
            acc_ref[...] = jnp.zeros_like(acc_ref)
        dzg = dzg_ref[...]
        dz = jnp.concatenate([dzx_ref[...], dzg[:, :RG_W], dzu_ref[...], dzg[:, RG_W:]], axis=1).astype(MXU)
        xb = x_ref[...].astype(MXU)
        dx_ref[...] = ALPHA * dt1_ref[...] + _mm_nt(dz, w_ref[...])
        for j in range(N_DEV):
            acc_ref[j] += _mm_tn(xb, dz[:, j * W_BLK:(j + 1) * W_BLK])

        @pl.when(pl.program_id(0) == L // TM - 1)
        def _():
            dw_ref[...] = acc_ref[...].astype(WIRE)

    row = lambda w: pl.BlockSpec((TM, w), lambda i: (i, 0))
    wspec = pl.BlockSpec((N_DEV, D_MODEL, W_BLK), lambda i: (0, 0, 0))
    return pl.pallas_call(
        body, name="inproj_bwd", grid=(L // TM,),
        in_specs=[row(D_MODEL), row(D_MODEL), row(RG_W), row(D_MODEL), row(S5_W),
                  pl.BlockSpec((D_MODEL, Z_W), lambda i: (0, 0))],
        out_specs=[row(D_MODEL), wspec],
        out_shape=[_S((L, D_MODEL)), _S((N_DEV, D_MODEL, W_BLK), WIRE)],
        scratch_shapes=[pltpu.VMEM((N_DEV, D_MODEL, W_BLK), F32)],
        compiler_params=_params(1))(dt1, x, dzx, dzg, dzu, w_in)


TM2 = 512


def _dz_block(dzx_ref, dzg_ref, dzu_ref):
    dzg = dzg_ref[...]
    return jnp.concatenate([dzx_ref[...], dzg[:, :RG_W], dzu_ref[...], dzg[:, RG_W:]], axis=1).astype(MXU)


def _inproj_bwd_dw(x, dzx, dzg, dzu, token=None):
    L = x.shape[0]
    extra, extra_specs = _after(token)

    def body(x_ref, dzx_ref, dzg_ref, dzu_ref, *rest):
        dw_ref, acc_ref = rest[len(extra):]
        @pl.when(pl.program_id(0) == 0)
        def _():
            acc_ref[...] = jnp.zeros_like(acc_ref)
        dz = _dz_block(dzx_ref, dzg_ref, dzu_ref)
        xb = x_ref[...].astype(MXU)
        for j in range(N_DEV):
            acc_ref[j] += _mm_tn(xb, dz[:, j * W_BLK:(j + 1) * W_BLK])

        @pl.when(pl.program_id(0) == L // TM2 - 1)
        def _():
            dw_ref[...] = acc_ref[...].astype(WIRE)

    row = lambda w: pl.BlockSpec((TM2, w), lambda i: (i, 0))
    wspec = pl.BlockSpec((N_DEV, D_MODEL, W_BLK), lambda i: (0, 0, 0))
    return pl.pallas_call(
        body, name="inproj_bwd_dw", grid=(L // TM2,),
        in_specs=[row(D_MODEL), row(RG_W), row(D_MODEL), row(S5_W)] + extra_specs, out_specs=wspec,
        out_shape=_S((N_DEV, D_MODEL, W_BLK), WIRE), scratch_shapes=[pltpu.VMEM((N_DEV, D_MODEL, W_BLK), F32)],
        compiler_params=_params(1))(x, dzx, dzg, dzu, *extra)


def _inproj_bwd_dx(dt1, dzx, dzg, dzu, w_in, token=None):
    L = dt1.shape[0]
    extra, extra_specs = _after(token)

    def body(dt1_ref, dzx_ref, dzg_ref, dzu_ref, w_ref, *rest):
        rest[-1][...] = ALPHA * dt1_ref[...] + _mm_nt(_dz_block(dzx_ref, dzg_ref, dzu_ref), w_ref[...])

    tm = min(TM_MM, L)
    row = lambda w: pl.BlockSpec((tm, w), lambda i: (i, 0))
    return pl.pallas_call(
        body, name="inproj_bwd_dx", grid=(L // tm,),
        in_specs=[row(D_MODEL), row(RG_W), row(D_MODEL), row(S5_W), _full((D_MODEL, Z_W))] + extra_specs,
        out_specs=row(D_MODEL), out_shape=_S((L, D_MODEL)), compiler_params=_params(1))(dt1, dzx, dzg, dzu, w_in, *extra)


def _rg_specs(layer):
    tile = lambda rows: pl.BlockSpec((rows, LANE), lambda c: (0, c))
    ptile = lambda rows: pl.BlockSpec((None, rows, LANE), lambda c: (layer, 0, c))
    pheads = pl.BlockSpec((None, 2, RG_HD, RG_HD), lambda c: (layer, c, 0, 0))
    return tile, ptile, pheads, pl.BlockSpec((2, RG_HD, RG_HD), lambda c: (c, 0, 0))


RG_HD = 64


def _bd2(w):
    z = jnp.zeros((RG_HD, RG_HD), w.dtype)
    return jnp.concatenate([jnp.concatenate([w[0], z], axis=1), jnp.concatenate([z, w[1]], axis=1)], axis=0)


def _bd2_diag(m):
    return jnp.stack([m[:RG_HD, :RG_HD], m[RG_HD:, RG_HD:]])


def _rg_fwd(z, cw, cb, wa_bd, wx_bd, ba, bx, lam, layer):
    L = z.shape[0]
    RC = min(RC_RG, L)

    def body(x_ref, cw_ref, cb_ref, wa_ref, wx_ref, ba_ref, bx_ref, lam_ref, hs_ref, *saved):
        row = slice(layer, layer + 1)
        w, b = cw_ref[...], cb_ref[row, :]
        wa, wx, ba_, bx_ = _bd2(wa_ref[...]).astype(MXU), _bd2(wx_ref[...]).astype(MXU), ba_ref[row, :], bx_ref[row, :]
        sp = _softplus(-lam_ref[row, :])

        def step(c, carry):
            r0 = pl.multiple_of(c * RC, RC)
            xe = jnp.concatenate([_halo(x_ref, c, r0), x_ref[pl.ds(r0, RC), :]], axis=0)
            t = _conv_taps(xe)
            h = t[0] * w[0:1] + t[1] * w[1:2] + t[2] * w[2:3] + t[3] * w[3:4] + b
            r, i, a, mult = _rg_gates(h, wa, wx, ba_, bx_, sp)
            hs, carry = _scan_real(a, mult * (i * h), carry)
            hs_ref[pl.ds(r0, RC), :] = hs
            for ref, val in zip(saved, (h, r, i, a, mult)):
                ref[pl.ds(r0, RC), :] = val
            return carry

        lax.fori_loop(0, L // RC, step, jnp.zeros((1, LANE), F32))

    tile, ptile, pheads, _ = _rg_specs(layer)
    return pl.pallas_call(
        body, name="rg_fwd", grid=(N_RG_T,),
        in_specs=[tile(L), ptile(4), tile(2), pheads, pheads, tile(2), tile(2), tile(2)],
        out_specs=[tile(L)] * 6, out_shape=[_S((L, RG_W))] * 6, compiler_params=_params(1))(
            z, cw, cb, wa_bd, wx_bd, ba, bx, lam)


def _rg_bwd(dhs, z, hs, gates, cw, wa_bd, wx_bd, lam, layer):
    L = z.shape[0]
    RC = min(RC_RG, L)

    def body(g_ref, x_ref, hs_ref, h_ref, r_ref, i_ref, a_ref, mult_ref, cw_ref, wa_ref, wx_ref, lam_ref,
             dx_ref, dcw_ref, dcb_ref, dwa_out, dwx_out, dba_ref, dbx_ref, dlam_ref, dwa_ref, dwx_ref):
        w = cw_ref[...]
        wa, wx = _bd2(wa_ref[...]).astype(MXU), _bd2(wx_ref[...]).astype(MXU)
        lam = lam_ref[layer:layer + 1, :]
        sp = _softplus(-lam)
        rows = lax.broadcasted_iota(jnp.int32, (RC, LANE), 0)
        for ref in (dcw_ref, dcb_ref, dwa_ref, dwx_ref, dba_ref, dbx_ref, dlam_ref):
            ref[...] = jnp.zeros_like(ref)
        nch = L // RC

        def step(k, carry):
            cin, nxt = carry
            c = nch - 1 - k
            r0 = pl.multiple_of(c * RC, RC)
            xe = jnp.concatenate([_halo(x_ref, c, r0), x_ref[pl.ds(r0, RC), :]], axis=0)
            t = _conv_taps(xe)
            h, r, i, a, mult = (ref[pl.ds(r0, RC), :] for ref in (h_ref, r_ref, i_ref, a_ref, mult_ref))
            hs_e = jnp.concatenate([_halo(hs_ref, c, r0), hs_ref[pl.ds(r0, RC), :]], axis=0)
            hs_prev = pltpu.roll(hs_e, 1, 0)[8:, :]
            g = g_ref[pl.ds(r0, RC), :]
            cc, cin_new = _scan_real(a, a * g, cin, reverse=True)
            dh = g + _up(cc, 1, rows, cin)
            ih = i * h
            dlog_a = dh * hs_prev * a - (dh * ih) * (a * a) / mult
            di = dh * mult * h
            dhin = dh * mult * i
            dr = dlog_a * ((-RG_C) * sp)
            dlam_ref[...] += _colsum(dlog_a * r)
            dra = dr * r * (1.0 - r)
            dia = di * i * (1.0 - i)
            dwa_ref[...] += _mm_tn(h, dra)
            dwx_ref[...] += _mm_tn(h, dia)
            dba_ref[...] += _colsum(dra)
            dbx_ref[...] += _colsum(dia)
            dhin = dhin + _mm_nt(dra, wa) + _mm_nt(dia, wx)
            de = jnp.concatenate([dhin, nxt], axis=0)
            n = RC + 8
            dx = (dhin * w[3:4] + pltpu.roll(de, n - 1, 0)[:RC, :] * w[2:3]
                  + pltpu.roll(de, n - 2, 0)[:RC, :] * w[1:2] + pltpu.roll(de, n - 3, 0)[:RC, :] * w[0:1])
            dx_ref[pl.ds(r0, RC), :] = dx
            for kk in range(4):
                dcw_ref[kk:kk + 1, :] += _colsum(dhin * t[kk])
            dcb_ref[...] += _colsum(dhin)
            return cin_new, dhin[0:8, :]

        lax.fori_loop(0, nch, step, (jnp.zeros((1, LANE), F32), jnp.zeros((8, LANE), F32)))
        dlam_ref[...] = dlam_ref[...] * (RG_C * _sigmoid(-lam))
        dwa_out[...], dwx_out[...] = _bd2_diag(dwa_ref[...]), _bd2_diag(dwx_ref[...])

    tile, ptile, pheads, gheads = _rg_specs(layer)
    heads = _S((2 * N_RG_T, RG_HD, RG_HD))
    return pl.pallas_call(
        body, name="rg_bwd", grid=(N_RG_T,),
        in_specs=[tile(L)] * 8 + [ptile(4), pheads, pheads, tile(2)],
        out_specs=[tile(L), tile(4), tile(1), gheads, gheads, tile(1), tile(1), tile(1)],
        out_shape=[_S((L, RG_W)), _S((4, RG_W)), _S((1, RG_W)), heads, heads, _S((1, RG_W)), _S((1, RG_W)), _S((1, RG_W))],
        scratch_shapes=[pltpu.VMEM((LANE, LANE), F32), pltpu.VMEM((LANE, LANE), F32)],
        compiler_params=_params(1))(dhs, z, hs, *gates, cw, wa_bd, wx_bd, lam)


def _cmul(ar, ai, br, bi):
    return ar * br - ai * bi, ar * bi + ai * br


S5_TW = S5_N // N_S5_T


S5_H = 16
S5_GT = LANE // S5_H


def _s5_specs(L, layer):
    in_tile = pl.BlockSpec((L, LANE), lambda t: (0, t))
    st = pl.BlockSpec((L, S5_TW), lambda t: (0, t))
    pg = pl.BlockSpec((None, S5_GT, S5_H, S5_P), lambda t: (layer * N_S5_T + t, 0, 0, 0))
    plb = pl.BlockSpec((None, S5_GT, S5_P), lambda t: (layer * N_S5_T + t, 0, 0))
    gg = pl.BlockSpec((None, S5_GT, S5_H, S5_P), lambda t: (t, 0, 0, 0))
    glb = pl.BlockSpec((None, S5_GT, S5_P), lambda t: (t, 0, 0))
    dv = pl.BlockSpec((1, LANE), lambda t: (0, t))
    return in_tile, st, pg, plb, gg, glb, dv


def _bd8(blocks):
    rows = []
    for g in range(S5_GT):
        pieces = [blocks[g]]
        if g:
            pieces.insert(0, jnp.zeros((S5_H, S5_P * g), blocks.dtype))
        if g < S5_GT - 1:
            pieces.append(jnp.zeros((S5_H, S5_P * (S5_GT - 1 - g)), blocks.dtype))
        rows.append(jnp.concatenate(pieces, axis=1))
    return jnp.concatenate(rows, axis=0)


def _bd8_diag(m):
    return jnp.stack([m[S5_H * g:S5_H * (g + 1), S5_P * g:S5_P * (g + 1)] for g in range(S5_GT)])


def _row8(v):
    return jnp.concatenate([v[g:g + 1] for g in range(S5_GT)], axis=1)


def _row8_split(r):
    return jnp.concatenate([r[:, S5_P * g:S5_P * (g + 1)] for g in range(S5_GT)], axis=0)


def _layer_row_tile(layer):
    return pl.BlockSpec((None, 1, LANE), lambda t: (layer, 0, t))


def _s5_fwd(z, bb_re, bb_im, lb_re, lb_im, c_re, c_im, dvec, layer):
    L = z.shape[0]

    def body(u_ref, bbr_ref, bbi_ref, lr_ref, li_ref, cr_ref, ci_ref, d_ref, y_ref, sr_ref, si_ref):
        bbr, bbi = _bd8(bbr_ref[...]).astype(MXU), _bd8(bbi_ref[...]).astype(MXU)
        cr, ci = _bd8(cr_ref[...]).astype(MXU), _bd8(ci_ref[...]).astype(MXU)
        dv = d_ref[...]
        steps, e = _tile_powers(_row8(lr_ref[...]), _row8(li_ref[...]))

        def step(c, carry):
            r0 = pl.multiple_of(c * RC, RC)
            u = u_ref[pl.ds(r0, RC), :]
            ub = u.astype(MXU)
            sr = jnp.dot(ub, bbr, preferred_element_type=F32)
            si = jnp.dot(ub, bbi, preferred_element_type=F32)
            sr, si, carry = _scan_lti(sr, si, carry, steps, e)
            sr_ref[pl.ds(r0, RC), :] = sr
            si_ref[pl.ds(r0, RC), :] = si
            y_ref[pl.ds(r0, RC), :] = dv * u + (_mm_nt(sr, cr) - _mm_nt(si, ci))
            return carry

        zero = jnp.zeros((1, S5_TW), F32)
        lax.fori_loop(0, L // RC, step, (zero, zero))

    in_tile, st, pg, plb, _, _, _ = _s5_specs(L, layer)
    u_tile = pl.BlockSpec((L, LANE), lambda t: (0, C_S5U // LANE + t))
    return pl.pallas_call(
        body, name="s5_fwd", grid=(N_S5_T,),
        in_specs=[u_tile, pg, pg, plb, plb, pg, pg, _layer_row_tile(layer)],
        out_specs=[in_tile, st, st],
        out_shape=[_S((L, S5_W)), _S((L, S5_N)), _S((L, S5_N))],
        compiler_params=_params(1))(z, bb_re, bb_im, lb_re, lb_im, c_re, c_im, dvec)


def _s5_bwd(dy0, z, s_re, s_im, bb_re, bb_im, lb_re, lb_im, c_re, c_im, dvec, layer, token=None):
    L = z.shape[0]
    extra, extra_specs = _after(token)

    def body(dy_ref, u_ref, sr_ref, si_ref, bbr_ref, bbi_ref, lr_ref, li_ref, cr_ref, ci_ref, d_ref, *rest):
        (du_ref, dbbr_out, dbbi_out, dlr_out, dli_out, dcr_out, dci_out, dd_ref,
         dbbr_ref, dbbi_ref, dcr_ref, dci_ref, dlr_ref, dli_ref) = rest[len(extra):]
        bbr, bbi = _bd8(bbr_ref[...]).astype(MXU), _bd8(bbi_ref[...]).astype(MXU)
        cr, ci = _bd8(cr_ref[...]).astype(MXU), _bd8(ci_ref[...]).astype(MXU)
        lr, li = _row8(lr_ref[...]), -_row8(li_ref[...])
        dv = d_ref[...]
        steps, e = _tile_powers(lr, li, reverse=True)
        for ref in (dbbr_ref, dbbi_ref, dlr_ref, dli_ref, dcr_ref, dci_ref, dd_ref):
            ref[...] = jnp.zeros_like(ref)
        nch = L // RC

        def step(k, carry):
            c = nch - 1 - k
            r0 = pl.multiple_of(c * RC, RC)
            dy = dy_ref[pl.ds(r0, RC), :]
            u = u_ref[pl.ds(r0, RC), :]
            dyb, ub = dy.astype(MXU), u.astype(MXU)
            sr, si = sr_ref[pl.ds(r0, RC), :], si_ref[pl.ds(r0, RC), :]
            dcr_ref[...] += _mm_tn(dyb, sr)
            dci_ref[...] -= _mm_tn(dyb, si)
            gr = jnp.dot(dyb, cr, preferred_element_type=F32)
            gi = -jnp.dot(dyb, ci, preferred_element_type=F32)
            gr, gi, carry = _scan_lti(gr, gi, carry, steps, e, reverse=True)
            pr_ = pltpu.roll(jnp.concatenate([_halo(sr_ref, c, r0), sr], axis=0), 1, 0)[8:, :]
            pi_ = pltpu.roll(jnp.concatenate([_halo(si_ref, c, r0), si], axis=0), 1, 0)[8:, :]
            dlr_ref[...] += _colsum(pr_ * gr + pi_ * gi)
            dli_ref[...] += _colsum(pr_ * gi - pi_ * gr)
            grb, gib = gr.astype(MXU), gi.astype(MXU)
            dbbr_ref[...] += _mm_tn(ub, grb)
            dbbi_ref[...] += _mm_tn(ub, gib)
            du_ref[pl.ds(r0, RC), :] = dv * dy + (_mm_nt(grb, bbr) + _mm_nt(gib, bbi))
            dd_ref[...] += _colsum(dy * u)
            return carry

        zero = jnp.zeros((1, S5_TW), F32)
        lax.fori_loop(0, nch, step, (zero, zero))
        dbbr_out[...], dbbi_out[...] = _bd8_diag(dbbr_ref[...]), _bd8_diag(dbbi_ref[...])
        dcr_out[...], dci_out[...] = _bd8_diag(dcr_ref[...]), _bd8_diag(dci_ref[...])
        dlr_out[...], dli_out[...] = _row8_split(dlr_ref[...]), _row8_split(dli_ref[...])

    in_tile, st, pg, plb, gg, glb, dv = _s5_specs(L, layer)
    u_tile = pl.BlockSpec((L, LANE), lambda t: (0, C_S5U // LANE + t))
    groups, rows = _S((N_S5_T, S5_GT, S5_H, S5_P)), _S((N_S5_T, S5_GT, S5_P))
    wide = pltpu.VMEM((LANE, S5_TW), F32)
    return pl.pallas_call(
        body, name="s5_bwd", grid=(N_S5_T,),
        in_specs=[in_tile, u_tile, st, st, pg, pg, plb, plb, pg, pg, _layer_row_tile(layer)] + extra_specs,
        out_specs=[in_tile, gg, gg, glb, glb, gg, gg, dv],
        out_shape=[_S((L, S5_W)), groups, groups, rows, rows, groups, groups, _S((1, S5_W))],
        scratch_shapes=[wide, wide, wide, wide, pltpu.VMEM((1, S5_TW), F32), pltpu.VMEM((1, S5_TW), F32)],
        compiler_params=_params(1))(dy0, z, s_re, s_im, bb_re, bb_im, lb_re, lb_im, c_re, c_im, dvec, *extra)


def _disc(ar, ai, ls):
    dt = jnp.exp(ls)
    mag = jnp.exp(ar * dt)
    lr = mag * jnp.cos(ai * dt)
    li = mag * jnp.sin(ai * dt)
    den = ar * ar + ai * ai
    cr = ((lr - 1.0) * ar + li * ai) / den
    ci = (li * ar - (lr - 1.0) * ai) / den
    return lr, li, cr, ci


def _s5_disc_fwd(ar, ai, ls, token=None):
    extra, extra_specs = _after(token)

    def body(ar_ref, ai_ref, ls_ref, *rest):
        lr_ref, li_ref, cr_ref, ci_ref = rest[len(extra):]
        lr, li, cr, ci = _disc(ar_ref[...], ai_ref[...], ls_ref[...])
        lr_ref[...], li_ref[...], cr_ref[...], ci_ref[...] = lr, li, cr, ci

    sh = _S(ar.shape)
    vm = pl.BlockSpec(memory_space=pltpu.VMEM)
    return pl.pallas_call(body, name="s5_disc_fwd", in_specs=[vm, vm, vm] + extra_specs, out_shape=[sh, sh, sh, sh])(
        ar, ai, ls, *extra)


def _s5_disc_bwd(ar, ai, ls, dlr, dli, dcr, dci):
    def body(ar_ref, ai_ref, ls_ref, dlr_ref, dli_ref, dcr_ref, dci_ref, dar_ref, dai_ref, dls_ref):
        _, vjp = jax.vjp(_disc, ar_ref[...], ai_ref[...], jnp.broadcast_to(ls_ref[...], ar_ref.shape))
        dar, dai, dls = vjp((dlr_ref[...], dli_ref[...], dcr_ref[...], dci_ref[...]))
        dar_ref[...], dai_ref[...] = dar, dai
        dls_ref[...] = jnp.sum(dls, axis=1, keepdims=True)

    return pl.pallas_call(body, name="s5_disc_bwd", out_shape=[_S(ar.shape), _S(ar.shape), _S(ls.shape)])(
        ar, ai, ls, dlr, dli, dcr, dci)


def _s5_bscale_fwd(cr, ci, br, bi):
    def body(cr_ref, ci_ref, br_ref, bi_ref, or_ref, oi_ref):
        or_ref[...], oi_ref[...] = _cmul(cr_ref[...], ci_ref[...], br_ref[...], bi_ref[...])

    return pl.pallas_call(body, name="s5_bscale_fwd", out_shape=[_S(br.shape), _S(br.shape)])(cr, ci, br, bi)


def _s5_bscale_bwd(cr, ci, br, bi, gr, gi):
    def body(cr_ref, ci_ref, br_ref, bi_ref, gr_ref, gi_ref, dbr_ref, dbi_ref, dcr_ref, dci_ref):
        cr_, ci_, br_, bi_, gr_, gi_ = (r[...] for r in (cr_ref, ci_ref, br_ref, bi_ref, gr_ref, gi_ref))
        dbr_ref[...] = cr_ * gr_ + ci_ * gi_
        dbi_ref[...] = cr_ * gi_ - ci_ * gr_
        dcr_ref[...] = jnp.sum(gr_ * br_ + gi_ * bi_, axis=1, keepdims=True)
        dci_ref[...] = jnp.sum(gi_ * br_ - gr_ * bi_, axis=1, keepdims=True)

    return pl.pallas_call(body, name="s5_bscale_bwd",
                          out_shape=[_S(br.shape), _S(br.shape), _S(cr.shape), _S(cr.shape)])(cr, ci, br, bi, gr, gi)


def _row(w):
    return pl.BlockSpec((TM, w), lambda i: (i, 0))


def _full(shape):
    return pl.BlockSpec(tuple(shape), lambda i: (0,) * len(shape))


def _gate_rows():
    return [pl.BlockSpec((TM, RG_W), lambda i: (i, C_RGG // RG_W))] + [
        pl.BlockSpec((TM, LANE), lambda i, k=k: (i, C_S5G // LANE + k)) for k in range(N_S5_T)]


def _p_rows(layer):
    return pl.BlockSpec((None, None, TM, PLE_D), lambda i: (layer, 0, i, 0))


DEPTH = 2


def _lrow(layer, width):
    return _full((DEPTH, width))


def _pick(ref, layer):
    return ref[layer:layer + 1, :]


def _post_fwd(x, hs, z, y0, p, w_glu, b_glu, w_out, g1, b1, ple_w, w_pg, b_pg, g2, b2, layer):
    L = x.shape[0]

    def body(x_ref, hs_ref, zg_ref, zs0_ref, zs1_ref, zs2_ref, y0_ref, p_ref, wg_ref, bg_ref, wo_ref, g1_ref, b1_ref, pw_ref,
             wpg_ref, bpg_ref, g2_ref, b2_ref, x2_ref, xh1_ref, xh2_ref, q_ref, gt_ref, rstd1_ref, rstd2_ref):
        rg_gate = zg_ref[...]
        s5_gate = jnp.concatenate([zs0_ref[...], zs1_ref[...], zs2_ref[...]], axis=1)
        rg_y = hs_ref[...] * _silu_and_grad(rg_gate)[0]
        y1 = _gelu(y0_ref[...])
        gl = _sigmoid(_mm(y1, wg_ref[...]) + _pick(bg_ref, layer))
        s5_y = (y1 * gl) * _silu_and_grad(s5_gate)[0]
        mix = _mm(jnp.concatenate([rg_y.astype(MXU), s5_y.astype(MXU)], axis=1), wo_ref[...])
        t1 = ALPHA * x_ref[...] + mix
        x1, xh1, rstd1 = _ln_fwd(t1, _pick(g1_ref, layer), _pick(b1_ref, layer))
        q = _mm(p_ref[...], pw_ref[...])
        gt = _sigmoid(_mm(x1, wpg_ref[...]) + _pick(bpg_ref, layer))
        t2 = ALPHA * x1 + q * gt
        x2, xh2, rstd2 = _ln_fwd(t2, _pick(g2_ref, layer), _pick(b2_ref, layer))
        x2_ref[...], xh1_ref[...], xh2_ref[...], q_ref[...], gt_ref[...] = x2, xh1, xh2, q, gt
        rstd1_ref[...], rstd2_ref[...] = rstd1, rstd2

    vec = _lrow(layer, D_MODEL)
    return pl.pallas_call(
        body, name="post_fwd", grid=(L // TM,),
        in_specs=[_row(D_MODEL), _row(RG_W), *_gate_rows(), _row(S5_W), _p_rows(layer), _full((S5_W, S5_W)),
                  _lrow(layer, S5_W), _full((D_MODEL, D_MODEL)), vec, vec, _full((PLE_D, D_MODEL)), _full((D_MODEL, D_MODEL)),
                  vec, vec, vec],
        out_specs=[_row(D_MODEL)] * 5 + [_row(1)] * 2, out_shape=[_S((L, D_MODEL))] * 5 + [_S((L, 1))] * 2,
        compiler_params=_params(1))(x, hs, z, z, z, z, y0, p, w_glu, b_glu, w_out, g1, b1, ple_w, w_pg, b_pg, g2, b2)


def _post_bwd_a(dx2_or_target, is_top, xh2, xh1, rstd2, rstd1, q, gt, p, w_pg, g1, b1, g2, b2, layer, token=None):
    L = xh1.shape[0]
    extra, extra_specs = _after(token)

    def body(d_ref, xh2_ref, xh1_ref, rstd2_ref, rstd1_ref, q_ref, gt_ref, p_ref, wpg_ref, g1_ref, b1_ref, g2_ref,
             b2_ref, *rest):
        (dt1_ref, dpw_out, dwpg_out, dbpg_ref, dg1_ref, db1_ref, dg2_ref, db2_ref, loss_ref, dpw_ref,
         dwpg_ref) = rest[len(extra):]
        @pl.when(pl.program_id(0) == 0)
        def _():
            for ref in (dpw_ref, dwpg_ref, dbpg_ref, dg1_ref, db1_ref, dg2_ref, db2_ref, loss_ref):
                ref[...] = jnp.zeros_like(ref)

        g1, g2 = _pick(g1_ref, layer), _pick(g2_ref, layer)
        xh1, xh2, rstd1, rstd2 = xh1_ref[...], xh2_ref[...], rstd1_ref[...], rstd2_ref[...]
        x1 = xh1 * g1 + _pick(b1_ref, layer)
        if is_top:
            err = (xh2 * g2 + _pick(b2_ref, layer)) - d_ref[...]
            loss_ref[...] += _colsum(err * err)
            dx2 = err * (1.0 / D_MODEL)
        else:
            dx2 = d_ref[...]
        p = p_ref[...]
        q, gt = q_ref[...], gt_ref[...]
        dg2_ref[...] += _colsum(dx2 * xh2)
        db2_ref[...] += _colsum(dx2)
        dt2 = _ln_bwd(dx2, xh2, rstd2, g2)
        dq = dt2 * gt
        dgpre = (dt2 * q) * gt * (1.0 - gt)
        dpw_ref[...] += _mm_tn(p, dq)
        dwpg_ref[...] += _mm_tn(x1, dgpre)
        dbpg_ref[...] += _colsum(dgpre)
        dx1 = ALPHA * dt2 + _mm_nt(dgpre, wpg_ref[...])
        dg1_ref[...] += _colsum(dx1 * xh1)
        db1_ref[...] += _colsum(dx1)
        dt1_ref[...] = _ln_bwd(dx1, xh1, rstd1, g1)

        @pl.when(pl.program_id(0) == L // TM - 1)
        def _():
            dpw_out[...] = dpw_ref[...].astype(WIRE)
            dwpg_out[...] = dwpg_ref[...].astype(WIRE)

    vec, lvec = _full((1, D_MODEL)), _lrow(layer, D_MODEL)
    return pl.pallas_call(
        body, name="post_bwd_a_top" if is_top else "post_bwd_a", grid=(L // TM,),
        in_specs=[_row(D_MODEL), _row(D_MODEL), _row(D_MODEL), _row(1), _row(1), _row(D_MODEL), _row(D_MODEL), _p_rows(layer),
                  _full((D_MODEL, D_MODEL)), lvec, lvec, lvec, lvec] + extra_specs,
        out_specs=[_row(D_MODEL), _full((PLE_D, D_MODEL)), _full((D_MODEL, D_MODEL)), vec, vec, vec, vec, vec, vec],
        out_shape=[_S((L, D_MODEL)), _S((PLE_D, D_MODEL), WIRE), _S((D_MODEL, D_MODEL), WIRE)] + [_S((1, D_MODEL))] * 6,
        scratch_shapes=[pltpu.VMEM((PLE_D, D_MODEL), F32), pltpu.VMEM((D_MODEL, D_MODEL), F32)],
        compiler_params=_params(1))(dx2_or_target, xh2, xh1, rstd2, rstd1, q, gt, p, w_pg, g1, b1, g2, b2, *extra)


def _post_bwd_b(dt1, z, hs, y0, w_out, w_glu, b_glu, layer):
    L = dt1.shape[0]

    def body(dt1_ref, zg_ref, zs0_ref, zs1_ref, zs2_ref, hs_ref, y0_ref, wo_ref, wg_ref, bg_ref,
             dhs_ref, dy0_ref, dzg_ref, dwo_out, dwg_out, dbg_ref, dwo_ref, dwg_ref):
        @pl.when(pl.program_id(0) == 0)
        def _():
            for ref in (dwo_ref, dwg_ref, dbg_ref):
                ref[...] = jnp.zeros_like(ref)

        dt1b = dt1_ref[...].astype(MXU)
        dm = _mm_nt(dt1b, wo_ref[...])
        d_rgy, d_s5y = dm[:, :RG_W], dm[:, RG_W:]
        rg_gate = zg_ref[...]
        s5_gate = jnp.concatenate([zs0_ref[...], zs1_ref[...], zs2_ref[...]], axis=1)
        hs = hs_ref[...]
        sl, dsl = _silu_and_grad(rg_gate)
        dhs_ref[...] = d_rgy * sl
        dzg_ref[:, :RG_W] = d_rgy * hs * dsl
        y0 = y0_ref[...]
        y1 = _gelu(y0)
        gl = _sigmoid(_mm(y1, wg_ref[...]) + _pick(bg_ref, layer))
        y2 = y1 * gl
        sl2, dsl = _silu_and_grad(s5_gate)
        m = jnp.concatenate([(hs * sl).astype(MXU), (y2 * sl2).astype(MXU)], axis=1)
        dwo_ref[...] += _mm_tn(m, dt1b)
        dy2 = d_s5y * sl2
        dzg_ref[:, RG_W:] = d_s5y * y2 * dsl
        dglpre = (dy2 * y1) * gl * (1.0 - gl)
        dwg_ref[...] += _mm_tn(y1, dglpre)
        dbg_ref[...] += _colsum(dglpre)
        dy1 = dy2 * gl + _mm_nt(dglpre, wg_ref[...])
        dy0_ref[...] = dy1 * _gelu_grad(y0)

        @pl.when(pl.program_id(0) == L // TM - 1)
        def _():
            dwo_out[...] = dwo_ref[...].astype(WIRE)
            dwg_out[...] = dwg_ref[...].astype(WIRE)

    return pl.pallas_call(
        body, name="post_bwd_b", grid=(L // TM,),
        in_specs=[_row(D_MODEL), *_gate_rows(), _row(RG_W), _row(S5_W), _full((D_MODEL, D_MODEL)),
                  _full((S5_W, S5_W)), _lrow(layer, S5_W)],
        out_specs=[_row(RG_W), _row(S5_W), _row(D_MODEL), _full((D_MODEL, D_MODEL)), _full((S5_W, S5_W)), _full((1, S5_W))],
        out_shape=[_S((L, RG_W)), _S((L, S5_W)), _S((L, D_MODEL)), _S((D_MODEL, D_MODEL), WIRE), _S((S5_W, S5_W), WIRE),
                   _S((1, S5_W))],
        scratch_shapes=[pltpu.VMEM((D_MODEL, D_MODEL), F32), pltpu.VMEM((S5_W, S5_W), F32)],
        compiler_params=_params(1))(dt1, z, z, z, z, hs, y0, w_out, w_glu, b_glu)


def _adamw(parts, w, m, v, token=None):
    nl = len(parts)
    extra, extra_specs = _after(token)
    n, R, C = parts[0].shape
    tr = R
    for cand in (512, 256, 128, 64, 32, 16, 8):
        if R % cand == 0 and n * cand * C * 4 <= 4 * 1024 * 1024:
            tr = cand
            break
    nblk = R // tr

    def body(*refs):
        p_refs = refs[:nl]
        w_ref, m_ref, v_ref = refs[nl:nl + 3]
        g_ref, d_ref, nm_ref, nv_ref = refs[nl + 3 + len(extra):]
        layer = pl.program_id(0)
        g = None
        for li, p_ref in enumerate(p_refs):
            s = p_ref[0].astype(F32)
            for k in range(1, n):
                s = s + p_ref[k].astype(F32)
            g = s if g is None else jnp.where(layer == li, s, g)
        nm = B1 * m_ref[...] + (1.0 - B1) * g
        nv = B2 * v_ref[...] + (1.0 - B2) * (g * g)
        d_ref[...] = (-LR) * ((nm / BC1) / (jnp.sqrt(nv / BC2) + EPS) + WD * w_ref[...])
        g_ref[...], nm_ref[...], nv_ref[...] = g, nm, nv

    def part_spec(li):
        return pl.BlockSpec((n, tr, C), lambda l, i: (0, jnp.where(l == li, i, jnp.where(l < li, 0, nblk - 1)), 0))

    blk = pl.BlockSpec((tr, C), lambda l, i: (l * nblk + i, 0))
    return pl.pallas_call(
        body, name="adamw", grid=(nl, nblk),
        in_specs=[part_spec(li) for li in range(nl)] + [blk, blk, blk] + extra_specs,
        out_specs=[blk] * 4, out_shape=[_S((nl * R, C))] * 4, compiler_params=_params(2))(*parts, w, m, v, *extra)


def _adamw_sharded(names, recv, w, m, v, name, token=None):
    n, nl = len(names), len(recv)
    extra, extra_specs = _after(token)
    n_in = n * (nl + 3)

    def body(*refs):
        outs = refs[n_in + len(extra):]
        for j in range(n):
            w_ref, m_ref, v_ref = (refs[(nl + t) * n + j] for t in range(3))
            g_ref, d_ref, nm_ref, nv_ref = (outs[t * n + j] for t in range(4))
            for l in range(nl):
                p_ref = refs[l * n + j]
                g = p_ref[0].astype(F32)
                for q in range(1, N_DEV):
                    g = g + p_ref[q].astype(F32)
                nm = B1 * m_ref[l] + (1.0 - B1) * g
                nv = B2 * v_ref[l] + (1.0 - B2) * (g * g)
                d_ref[l] = (-LR) * ((nm / BC1) / (jnp.sqrt(nv / BC2) + EPS) + WD * w_ref[l])
                g_ref[l], nm_ref[l], nv_ref[l] = g, nm, nv

    ins = [r[k] for r in recv for k in names] + [t[k] for t in (w, m, v) for k in names]
    vm = pl.BlockSpec(memory_space=pltpu.VMEM)
    outs = pl.pallas_call(body, name=name, in_specs=[vm] * n_in + extra_specs,
                          out_shape=[_S(w[k].shape) for _ in range(4) for k in names],
                          compiler_params=pltpu.CompilerParams(vmem_limit_bytes=VMEM_LIMIT))(*ins, *extra)
    return [{k: outs[t * n + j] for j, k in enumerate(names)} for t in range(4)]


def _adamw_natural(names, g, w, m, v, name):
    n = len(names)

    def body(*refs):
        for j in range(n):
            g_ref, w_ref, m_ref, v_ref, d_ref, nm_ref, nv_ref = (refs[k * n + j] for k in range(7))
            gj = g_ref[...]
            nm = B1 * m_ref[...] + (1.0 - B1) * gj
            nv = B2 * v_ref[...] + (1.0 - B2) * (gj * gj)
            d_ref[...] = (-LR) * ((nm / BC1) / (jnp.sqrt(nv / BC2) + EPS) + WD * w_ref[...])
            nm_ref[...], nv_ref[...] = nm, nv

    ins = [t[k] for t in (g, w, m, v) for k in names]
    outs = pl.pallas_call(body, name=name, out_shape=[_S(w[k].shape) for _ in range(3) for k in names],
                          compiler_params=pltpu.CompilerParams(vmem_limit_bytes=VMEM_LIMIT))(*ins)
    return [{k: outs[t * n + j] for j, k in enumerate(names)} for t in range(3)]


def _me():
    return lax.axis_index("x"), lax.axis_index("y"), lax.axis_index("c")


def _lin(dev):
    return 4 * dev[0] + 2 * dev[1] + dev[2]


def _blk(ref, axis, size, idx):
    nd = len(ref.shape)
    start = idx * size
    if axis == nd - 1 and size % LANE == 0:
        start = pl.multiple_of(start, LANE)
    elif axis == nd - 2 and size % 16 == 0:
        start = pl.multiple_of(start, 16)
    ix = [slice(None)] * nd
    ix[axis] = pl.ds(start, size)
    return ref.at[tuple(ix)]


HBM_SPEC = pl.BlockSpec(memory_space=pltpu.HBM)
SEM_SPEC = pl.BlockSpec(memory_space=pltpu.SEMAPHORE)
EFFECT = pltpu.SideEffectType.DATAFLOW_SIDE_EFFECTING


def _peers(x, y, c):
    flip = lambda v, f: 1 - v if f else v
    return [(flip(x, k & 4), flip(y, k & 2), flip(c, k & 1)) for k in range(1, N_DEV)]


def _land_shape(mode, s, axis):
    if mode == "gather":
        return s.shape[:axis] + (N_DEV * s.shape[axis],) + s.shape[axis + 1:]
    return (N_DEV,) + s.shape[:axis] + (s.shape[axis] // N_DEV,) + s.shape[axis + 1:]


def _src_view(mode, ref, axis, peer):
    return ref if mode == "gather" else _blk(ref, axis, ref.shape[axis] // N_DEV, peer)


def _dst_view(mode, land, axis, sender):
    return _blk(land, axis, land.shape[axis] // N_DEV, sender) if mode == "gather" else land.at[sender]


def _blocks(mode, land, axis, k):
    if mode == "gather":
        ix = [slice(None)] * len(land.shape)
        ix[axis] = pl.ds(0, k * (land.shape[axis] // N_DEV))
        return land.at[tuple(ix)]
    return land.at[pl.ds(0, k)]


ARRIVALS = {None: N_DEV - 1, "near": 4, "relay": 3}


def _routes(route, x, y, c):
    me, sibling = (x, y, c), (x, y, 1 - c)
    chips = [(1 - x, y), (x, 1 - y), (1 - x, 1 - y)]
    if route == "near":
        return [(me, sibling)] + [(me, (*chip, c)) for chip in chips]
    if route == "relay":
        return [((*chip, c), sibling) for chip in chips]
    return [(me, peer) for peer in _peers(x, y, c)]


def _place_own(mode, srcs, axes, name, after=None):
    n = len(srcs)
    extra, extra_specs = _after(after)

    def body(me_ref, *refs):
        for a in range(n):
            out = refs[n + len(extra) + a]
            out[...] = refs[a][...].reshape(out.shape)

    def at_me(shape, axis):
        return lambda i, me: tuple(me[0] if d == axis else 0 for d in range(len(shape)))

    in_specs, out_specs = [], []
    for s, axis in zip(srcs, axes):
        if mode == "gather":
            in_specs.append(pl.BlockSpec(s.shape, lambda i, me, nd=len(s.shape): (0,) * nd))
            out_specs.append(pl.BlockSpec(s.shape, at_me(s.shape, axis)))
        else:
            blk = s.shape[:axis] + (s.shape[axis] // N_DEV,) + s.shape[axis + 1:]
            in_specs.append(pl.BlockSpec(blk, at_me(blk, axis)))
            out_specs.append(pl.BlockSpec((1,) + blk, at_me((1,) + blk, 0)))
    me = _lin(_me()).astype(jnp.int32).reshape(1)
    return pl.pallas_call(
        body, name=name, out_shape=[_S(_land_shape(mode, s, a), s.dtype) for s, a in zip(srcs, axes)],
        grid_spec=pltpu.PrefetchScalarGridSpec(num_scalar_prefetch=1, grid=(1,), in_specs=in_specs + extra_specs,
                                               out_specs=out_specs),
        compiler_params=_params(1))(me, *srcs, *extra)


def _place_shards(shards, layers, axes, dtypes, name, after=None):
    n = len(shards)
    extra, extra_specs = _after(after)

    def body(me_ref, *refs):
        for a in range(n):
            out = refs[n + len(extra) + a]
            out[...] = refs[a][...].astype(out.dtype)

    in_specs, out_specs, out_shape = [], [], []
    for s, layer, axis, dt in zip(shards, layers, axes, dtypes):
        shape = s.shape if layer is None else s.shape[1:]
        nd = len(shape)
        if layer is None:
            in_specs.append(pl.BlockSpec(shape, lambda i, me, nd=nd: (0,) * nd))
        else:
            in_specs.append(pl.BlockSpec((None,) + shape, lambda i, me, nd=nd, layer=layer: (layer,) + (0,) * nd))
        out_specs.append(pl.BlockSpec(shape, lambda i, me, nd=nd, axis=axis: tuple(me[0] if d == axis else 0 for d in range(nd))))
        out_shape.append(_S(shape[:axis] + (N_DEV * shape[axis],) + shape[axis + 1:], dt))
    me = _lin(_me()).astype(jnp.int32).reshape(1)
    return pl.pallas_call(
        body, name=name, out_shape=out_shape,
        grid_spec=pltpu.PrefetchScalarGridSpec(num_scalar_prefetch=1, grid=(1,), in_specs=in_specs + extra_specs,
                                               out_specs=out_specs),
        compiler_params=_params(1))(me, *shards, *extra)


def _push_start(mode, srcs, lands, axes, name, route=None):
    n, ns = len(lands), len(srcs)

    def body(*refs):
        src_refs, land_refs = refs[:ns], refs[ns:ns + n]
        send_sems, recv_sems = refs[ns + n], refs[ns + n + 1]
        token = refs[-1]
        x, y, c = _me()
        for a in range(n):
            for block, peer in _routes(route, x, y, c):
                there = _dst_view(mode, land_refs[a], axes[a], _lin(block))
                pltpu.make_async_remote_copy(
                    src_ref=_src_view(mode, src_refs[a], axes[a], _lin(peer)) if ns else there, dst_ref=there,
                    send_sem=send_sems.at[a], recv_sem=recv_sems.at[a], device_id=peer, device_id_type=MESH).start()
        token[...] = jnp.zeros_like(token)

    hbm = lambda s: pltpu.HBM(s.shape, s.dtype)
    outs = pl.pallas_call(
        body, name=name,
        out_shape=(pltpu.SemaphoreType.DMA((n,)), pltpu.SemaphoreType.DMA((n,)), *[hbm(s) for s in srcs], *[hbm(s) for s in lands],
                   _S((SUB, LANE))),
        in_specs=[HBM_SPEC] * (ns + n),
        out_specs=(SEM_SPEC, SEM_SPEC, *[HBM_SPEC] * (ns + n), pl.BlockSpec(memory_space=pltpu.VMEM)),
        input_output_aliases={i: 2 + i for i in range(ns + n)},
        compiler_params=pltpu.CompilerParams(has_side_effects=EFFECT),
    )(*[pltpu.with_memory_space_constraint(s, pltpu.HBM) for s in list(srcs) + list(lands)])
    return outs[0], outs[1], outs[2:2 + ns], outs[2 + ns:2 + ns + n], outs[-1]


def _push_wait(mode, send_sems, recv_sems, srcs, lands, axes, after, name, first=0, route=None):
    n, ns = len(lands), len(srcs)
    after = list(after) if isinstance(after, (list, tuple)) else [after]

    def body(*refs):
        land_refs = refs[ns:ns + n]
        send_sems, recv_sems = refs[ns + n], refs[ns + n + 1]
        x, y, c = _me()
        for a in range(n):
            seven = _blocks(mode, land_refs[a], axes[a], ARRIVALS[route])
            cp = pltpu.make_async_remote_copy(src_ref=seven, dst_ref=seven, send_sem=send_sems.at[first + a],
                                              recv_sem=recv_sems.at[first + a],
                                              device_id=(x, y, 1 - c), device_id_type=MESH)
            cp.wait_send()
            cp.wait_recv()

    hbm = lambda s: pltpu.HBM(s.shape, s.dtype)
    outs = pl.pallas_call(
        body, name=name, out_shape=tuple(hbm(s) for s in list(srcs) + list(lands)),
        in_specs=[HBM_SPEC] * (ns + n) + [SEM_SPEC, SEM_SPEC] + [ANY] * len(after), out_specs=tuple([HBM_SPEC] * (ns + n)),
        input_output_aliases={i: i for i in range(ns + n)},
        compiler_params=pltpu.CompilerParams(has_side_effects=EFFECT),
    )(*srcs, *lands, send_sems, recv_sems, *after)
    return outs[ns:]


def _sum_parts(parts):
    n, R, C = parts.shape

    def body(p_ref, o_ref):
        g = p_ref[0]
        for k in range(1, n):
            g = g + p_ref[k]
        o_ref[...] = g

    return pl.pallas_call(body, name="sum_parts", out_shape=_S((R, C)))(parts)


SMALL =['conv_b', 'rg_wa', 'rg_ba', 'rg_wx', 'rg_bx', 'rg_lambda', 's5_a_re', 's5_a_im', 's5_b_re', 's5_b_im',
         's5_c_re', 's5_c_im', 's5_d', 's5_log_step', 's5_b_glu', 'ln1_g', 'ln1_b', 'ple_gate_b', 'ln2_g', 'ln2_b']
WEIGHTS = ['w_in', 'conv_w', 'conv_b', 'rg_wa', 'rg_ba', 'rg_wx', 'rg_bx', 'rg_lambda', 's5_a_re', 's5_a_im', 's5_b_re',
           's5_b_im', 's5_c_re', 's5_c_im', 's5_d', 's5_log_step', 's5_w_glu', 's5_b_glu', 'w_out', 'ln1_g', 'ln1_b',
           'ple_w', 'ple_gate_w', 'ple_gate_b', 'ln2_g', 'ln2_b']
PACK_ROWS_MULT = 64


STORED = {'s5_b_re': (2, 3), 's5_b_im': (2, 3), 's5_d': (1, 2)}


def _stored(k, a):
    return jnp.swapaxes(a, *STORED[k]) if k in STORED else a


def _tile_rows(n):
    return -(-n // (SUB * LANE)) * SUB


def _pack(tree, scalar):
    parts = []
    for a in [tree[k] for k in SMALL] + [scalar.reshape(1)]:
        rows = _tile_rows(a.size)
        parts.append(jnp.pad(a.reshape(-1), (0, rows * LANE - a.size)).reshape(rows, LANE))
    rows = sum(p.shape[0] for p in parts)
    parts.append(jnp.zeros((-rows % PACK_ROWS_MULT, LANE), F32))
    return jnp.concatenate(parts, axis=0)


def _unpack(packed, like):
    out, r = {}, 0
    for k in SMALL:
        n = math.prod(like[k].shape)
        rows = _tile_rows(n)
        part = packed[r:r + rows]
        out[k] = (part if n == rows * LANE else part.reshape(-1)[:n]).reshape(like[k].shape)
        r += rows
    return out, packed[r, 0]


class _NoHooks:
    token = None
    first_token = None

    def first_weights(self, full, after):
        return full

    def layer_start(self, i, W, after):
        return W

    def late_weights(self, i, W, after):
        return W

    def post_done(self, i, g):
        return None

    def smalls_done(self, grads, loss):
        self.small = _small_grads(grads, self.res)
        return None

    def w_in_done(self, i, g):
        return None

    def layer_done(self, i, g, dx):
        return None


def _local_grads(x, p, target, W, disc, hooks):
    depth = 2
    saved = []
    for i in range(depth):
        if i > 0:
            W = hooks.layer_start(i, W, x)
        w = W[i]
        z = _inproj_fwd(x, w['w_in'], hooks.token if i == 0 else None)
        hs, *gates = _rg_fwd(z, w['conv_w'], w['conv_b'], w['wa_bd'], w['wx_bd'], w['rg_ba'], w['rg_bx'], w['rg_lambda'], i)
        d = disc[i]
        y0, s_re, s_im = _s5_fwd(z, d['bb_re'], d['bb_im'], d['lb_re'], d['lb_im'], d['c_re'], d['c_im'], w['s5_d'], i)
        W = hooks.late_weights(i, W, y0)
        w = W[i]
        x2, *norms = _post_fwd(x, hs, z, y0, p, w['s5_w_glu'], w['s5_b_glu'], w['w_out'], w['ln1_g'], w['ln1_b'],
                               w['ple_w'], w['ple_gate_w'], w['ple_gate_b'], w['ln2_g'], w['ln2_b'], i)
        saved.append((x, z, hs, gates, y0, s_re, s_im, norms))
        x = x2

    grads = [None] * depth
    dx = target
    loss = None
    token = None
    for i in reversed(range(depth)):
        w, d = W[i], disc[i]
        xin, z, hs, gates, y0, s_re, s_im, (xh1, xh2, q, gt, rstd1, rstd2) = saved[i]
        g = {}
        (dt1, g['ple_w'], g['ple_gate_w'], g['ple_gate_b'], g['ln1_g'], g['ln1_b'], g['ln2_g'], g['ln2_b'], lrow) = _post_bwd_a(
            dx, i == depth - 1, xh2, xh1, rstd2, rstd1, q, gt, p, w['ple_gate_w'], w['ln1_g'], w['ln1_b'],
            w['ln2_g'], w['ln2_b'], i, token)
        if i == depth - 1:
            loss = 0.5 / D_MODEL * jnp.sum(lrow)
        dhs, dy0, dzg, g['w_out'], g['s5_w_glu'], g['s5_b_glu'] = _post_bwd_b(dt1, z, hs, y0, w['w_out'], w['s5_w_glu'],
                                                                           w['s5_b_glu'], i)
        (dzu, g['bb_re'], g['bb_im'], g['lb_re'], g['lb_im'], g['c_re'], g['c_im'], g['s5_d']) = _s5_bwd(
            dy0, z, s_re, s_im, d['bb_re'], d['bb_im'], d['lb_re'], d['lb_im'], d['c_re'], d['c_im'], w['s5_d'], i,
            hooks.post_done(i, g))
        (dzx, g['conv_w'], g['conv_b'], g['wa_bd'], g['wx_bd'], g['rg_ba'], g['rg_bx'], g['rg_lambda']) = _rg_bwd(
            dhs, z, hs, gates, w['conv_w'], w['wa_bd'], w['wx_bd'], w['rg_lambda'], i)
        if i == 0:
            g['w_in'] = _inproj_bwd_dw(xin, dzx, dzg, dzu, hooks.smalls_done([g, grads[1]], loss))
            dx = _inproj_bwd_dx(dt1, dzx, dzg, dzu, w['w_in'], hooks.w_in_done(i, g))
        else:
            dx, g['w_in'] = _inproj_bwd(dt1, xin, dzx, dzg, dzu, w['w_in'])
        grads[i] = g
        token = hooks.layer_done(i, g, dx)
    return loss, dx, grads


def _s5_layouts_fwd(s5_a_re, s5_a_im, s5_log_step, s5_b_re, s5_b_im, s5_c_re, s5_c_im, token=None):
    depth = s5_a_re.shape[0]
    ar, ai = s5_a_re.reshape(depth * 24, S5_P), s5_a_im.reshape(depth * 24, S5_P)
    ls = s5_log_step.reshape(depth * 24, 1)
    lr, li, cr, ci = _s5_disc_fwd(ar, ai, ls, token)
    per_group = lambda a: a.reshape(depth * 24, 1, S5_P)
    as_c = lambda b: jnp.swapaxes(b, 2, 3).reshape(depth * 24, S5_H, S5_P)
    res = (ar, ai, ls, per_group(cr), per_group(ci), as_c(s5_b_re), as_c(s5_b_im))
    bbr, bbi = _s5_bscale_fwd(*res[3:])
    tiles = lambda a: a.reshape(depth * N_S5_T, S5_GT, S5_H, S5_P)
    rows = lambda a: a.reshape(depth * N_S5_T, S5_GT, S5_P)
    disc = dict(bb_re=tiles(bbr), bb_im=tiles(bbi), lb_re=rows(lr), lb_im=rows(li), c_re=tiles(s5_c_re), c_im=tiles(s5_c_im))
    return [disc] * depth, res


def _s5_layouts_bwd(grads, res):
    ar, ai, ls, cr, ci, br, bi = res
    depth = len(grads)
    stack = lambda k, shape: jnp.stack([g[k] for g in grads]).reshape(shape)
    groups, shape_c = (depth * 24, S5_H, S5_P), (depth, 24, S5_H, S5_P)
    dbr, dbi, dcr, dci = _s5_bscale_bwd(cr, ci, br, bi, stack('bb_re', groups), stack('bb_im', groups))
    gp = (depth * 24, S5_P)
    dar, dai, dls = _s5_disc_bwd(ar, ai, ls, stack('lb_re', gp), stack('lb_im', gp), dcr.reshape(gp), dci.reshape(gp))
    return dict(
        s5_a_re=dar.reshape(depth, 24, S5_P), s5_a_im=dai.reshape(depth, 24, S5_P), s5_log_step=dls.reshape(depth, 24),
        s5_b_re=dbr.reshape(shape_c), s5_b_im=dbi.reshape(shape_c),
        s5_c_re=stack('c_re', shape_c), s5_c_im=stack('c_im', shape_c))


LATE = ('w_out', 'ple_w', 'ple_gate_w', 's5_w_glu')


ROWS = ('conv_b', 'rg_ba', 'rg_bx', 'rg_lambda', 's5_d', 's5_b_glu', 'ln1_g', 'ln1_b', 'ple_gate_b', 'ln2_g', 'ln2_b')


def _shared_weights(full):
    shared = {k: full[k] for k in ROWS}
    shared.update(conv_w=full['conv_w'], wa_bd=full['rg_wa'], wx_bd=full['rg_wx'], s5_d=full['s5_d'].reshape(DEPTH, 1, S5_W))
    return shared


def _layer_weights(full, shared, i):
    return dict(shared, w_in=full['w_in'][i])


class _AllLocal(_NoHooks):
    def __init__(self, full):
        self.full = full

    def late_weights(self, i, W, after):
        W[i].update({k: self.full[k][i] for k in LATE})
        return W


def _full_grads(full, x, p, target, hooks=None):
    hooks = hooks or _AllLocal(full)
    disc, res = _s5_layouts_fwd(full['s5_a_re'], full['s5_a_im'], full['s5_log_step'], full['s5_b_re'], full['s5_b_im'],
                                full['s5_c_re'], full['s5_c_im'], hooks.first_token)
    full = hooks.first_weights(full, disc[-1]['bb_im'])
    shared = _shared_weights(full)
    W = [_layer_weights(full, shared, i) for i in range(2)]
    hooks.res = res
    loss, gx, grads = _local_grads(x, p, target, W, disc, hooks)
    out = dict(hooks.small)
    for k in SHARD_AXIS:
        out[k] = [g[k] for g in grads]
    return loss, gx, out


def _small_grads(grads, res):
    stack = lambda f: jnp.stack([f(g) for g in grads])
    out = _s5_layouts_bwd(grads, res)
    out['conv_w'] = stack(lambda g: g['conv_w'])
    for k in ('conv_b', 'rg_ba', 'rg_bx', 'rg_lambda', 's5_b_glu', 'ln1_g', 'ln1_b', 'ple_gate_b', 'ln2_g', 'ln2_b'):
        out[k] = stack(lambda g: g[k][0])
    out['s5_d'] = _stored('s5_d', stack(lambda g: g['s5_d'][0]).reshape(2, 24, 16))
    out['rg_wa'] = stack(lambda g: g['wa_bd'])
    out['rg_wx'] = stack(lambda g: g['wx_bd'])
    return out


SHARD_AXIS = {'w_in': 2, 'w_out': 1, 'ple_w': 2, 'ple_gate_w': 1, 's5_w_glu': 1}


def kernel(x, p, w_in, conv_w, conv_b, rg_wa, rg_ba, rg_wx, rg_bx, rg_lambda, s5_a_re, s5_a_im, s5_b_re, s5_b_im, s5_c_re, s5_c_im, s5_d, s5_log_step, s5_w_glu, s5_b_glu, w_out, ln1_g, ln1_b, ple_w, ple_gate_w, ple_gate_b, ln2_g, ln2_b, loss_target, m_w_in, m_conv_w, m_conv_b, m_rg_wa, m_rg_ba, m_rg_wx, m_rg_bx, m_rg_lambda, m_s5_a_re, m_s5_a_im, m_s5_b_re, m_s5_b_im, m_s5_c_re, m_s5_c_im, m_s5_d, m_s5_log_step, m_s5_w_glu, m_s5_b_glu, m_w_out, m_ln1_g, m_ln1_b, m_ple_w, m_ple_gate_w, m_ple_gate_b, m_ln2_g, m_ln2_b, v_w_in, v_conv_w, v_conv_b, v_rg_wa, v_rg_ba, v_rg_wx, v_rg_bx, v_rg_lambda, v_s5_a_re, v_s5_a_im, v_s5_b_re, v_s5_b_im, v_s5_c_re, v_s5_c_im, v_s5_d, v_s5_log_step, v_s5_w_glu, v_s5_b_glu, v_w_out, v_ln1_g, v_ln1_b, v_ple_w, v_ple_gate_w, v_ple_gate_b, v_ln2_g, v_ln2_b):
    local = dict(locals())
    w = {k: local[k] for k in WEIGHTS}
    mom = {k: local['m_' + k] for k in WEIGHTS}
    var = {k: local['v_' + k] for k in WEIGHTS}

    big = list(SHARD_AXIS)
    late_axes = [SHARD_AXIS[k] - 1 for k in LATE]
    pushed = {}

    groups = dict(first=(['w_in', 'conv_w'], [0, None], [1, 0]), l0=(list(LATE), [0] * len(LATE), late_axes),
                  l1=(['w_in'] + list(LATE), [1] * (1 + len(LATE)), [1] + late_axes))
    token = None
    for key, members in (("first", ["first"]), ("rest", ["l0", "l1"])):
        names, layers, axes = (sum((groups[m][j] for m in members), []) for j in range(3))
        shards = [w[k] if layer is not None else w[k][None] for k, layer in zip(names, layers)]
        lands = _place_shards(shards, layers, axes, [WIRE if k in big else w[k].dtype for k in names],
                              "place_weights_" + key, token)
        pushed[key] = _push_start("gather", [], lands, axes, "push_weights_" + key, "near" if key == "first" else None)
        token = pushed[key][4]

    def await_weights(key, axes, after):
        s, first = (pushed["first"], 0) if key == "first" else (pushed["rest"], 0 if key == "l0" else len(LATE))
        return _push_wait("gather", s[0], s[1], [], s[3][first:first + len(axes)], axes, after, "await_weights_" + key, first)

    def push_grads(key, g, names, axes):
        srcs = [g[k] for k in names]
        pushed[key] = _push_start("scatter", srcs, _place_own("scatter", srcs, axes, "place_grads_" + key), axes,
                                  "push_grads_" + key)
        return pushed[key][4]

    def await_grads(key, axes, after):
        s = pushed[key]
        return _push_wait("scatter", s[0], s[1], s[2], s[3], axes, after, "await_grads_" + key)

    class Overlap(_NoHooks):
        token = pushed["rest"][4]
        first_token = token

        def first_weights(self, full, after):
            s, axes = pushed["first"], [1, 0]
            near = _push_wait("gather", s[0], s[1], [], s[3], axes, after, "await_weights_near", route="near")
            s = _push_start("gather", [], near, axes, "relay_weights", "relay")
            w_in0, conv = _push_wait("gather", s[0], s[1], [], s[3], axes, s[4], "await_weights_relay", route="relay")
            return dict(full, w_in=[w_in0, None], conv_w=jnp.moveaxis(conv, 0, 2).reshape(2, 4, RG_W))

        def late_weights(self, i, W, after):
            if i == 0:
                W[0].update(zip(LATE, await_weights("l0", late_axes, after)))
            return W

        def layer_start(self, i, W, after):
            lands = await_weights("l1", [1] + late_axes, after)
            W[1].update(zip(LATE, lands[1:]), w_in=lands[0])
            return W

        def post_done(self, i, g):
            return push_grads("late0", g, LATE, late_axes) if i == 0 else None

        def smalls_done(self, grads, loss):
            super().smalls_done(grads, loss)
            conv = jnp.moveaxis(self.small['conv_w'].reshape(2, 4, N_DEV, RG_W // N_DEV), 2, 0)
            self.packed = _pack(self.small, loss)
            return push_grads("small", dict(conv_w=conv.reshape(N_DEV, 8, RG_W // N_DEV), small=self.packed),
                              ['conv_w', 'small'], [0, 0])

        def w_in_done(self, i, g):
            return push_grads("w_in0", g, ['w_in'], [0])

        def layer_done(self, i, g, dx):
            return push_grads("all1", g, ['w_in'] + list(LATE), [0] + late_axes) if i == 1 else None

    hooks = Overlap()
    _, grad_x, g = _full_grads(dict(w), x[0], p, loss_target[0], hooks)

    recv1 = dict(zip(['w_in'] + list(LATE), await_grads("all1", [0] + late_axes, grad_x)))
    recv0 = dict(zip(LATE, await_grads("late0", late_axes, grad_x)))
    outs = {}

    def update(k, parts, token=None):
        shard = w[k].shape
        c = shard[-1]
        two = lambda a: a.reshape(-1, c)
        res = _adamw([r.reshape(N_DEV, -1, c) for r in parts], two(w[k]), two(mom[k]), two(var[k]), token)
        outs[k] = [o.reshape(shard) for o in res]

    conv_parts, small_parts = await_grads("small", [0, 0], grad_x)
    rows = hooks.packed.shape[0] // N_DEV
    mine = _sum_parts(small_parts.reshape(N_DEV, rows, LANE))
    sums = _push_start("gather", [mine], _place_own("gather", [mine], [0], "place_small_sums"), [0], "push_small_sums")
    late = _adamw_sharded(list(LATE), [recv0, recv1], w, mom, var, "adamw_late", sums[4])
    for k in LATE:
        outs[k] = [t[k] for t in late]
    w_in0, = await_grads("w_in0", [0], [outs[k][1] for k in LATE])
    update('w_in', [w_in0, recv1['w_in']])
    update('conv_w', [conv_parts])
    gathered, = _push_wait("gather", sums[0], sums[1], sums[2], sums[3], [0], [outs['w_in'][1], outs['conv_w'][1]],
                           "await_small_sums")
    stored = [{k: _stored(k, t[k]) for k in SMALL} for t in (w, mom, var)]
    summed, loss = _unpack(gathered, stored[0])
    wide = ['s5_b_re', 's5_b_im']
    for names, name in ((wide, "adamw_s5_b"), ([k for k in SMALL if k not in wide], "adamw_small")):
        delta, new_m, new_v = _adamw_natural(names, summed, *stored, name)
        for k in names:
            outs[k] = [_stored(k, o[k]) for o in (summed, delta, new_m, new_v)]

    res = [loss, grad_x[None]]
    for j in range(4):
        res += [outs[k][j] for k in WEIGHTS]
    return tuple(res)
```

```python
import math

import jax
import jax.numpy as jnp
from jax import lax
from jax.experimental import pallas as pl
from jax.experimental.pallas import tpu as pltpu

F32 = jnp.float32
MXU = jnp.bfloat16
WIRE = jnp.bfloat16

N_DEV = 8
D_MODEL = 1024
PLE_D = 256
RG_W = 640
S5_W = 384
S5_P = 64
S5_N = 24 * S5_P
Z_W = 2 * RG_W + 2 * S5_W
C_RGG = RG_W
C_S5U = 2 * RG_W
C_S5G = 2 * RG_W + S5_W
LANE = 128
N_RG_T = RG_W // LANE
N_S5_T = S5_W // LANE
W_BLK = Z_W // N_DEV
ALPHA = (2.0 * 2) ** 0.25
LN_EPS = 1e-5
RG_C = 8.0
LR, B1, B2, EPS, WD, STEP = 0.001, 0.9, 0.999, 1e-08, 0.01, 10
BC1 = 1.0 - B1 ** STEP
BC2 = 1.0 - B2 ** STEP
RC = 512
RC_RG = 1024
TM = 512
TM_MM = 1024
VMEM_LIMIT = 56 * 1024 * 1024

MESH = pl.DeviceIdType.MESH
ANY = pl.BlockSpec(memory_space=pl.ANY)


def _params(n_grid_axes, vmem=VMEM_LIMIT):
    return pltpu.CompilerParams(dimension_semantics=("arbitrary",) * n_grid_axes, vmem_limit_bytes=vmem)


def _S(shape, dtype=F32):
    return jax.ShapeDtypeStruct(tuple(shape), dtype)


def _sigmoid(x):
    return 0.5 * jnp.tanh(0.5 * x) + 0.5


def _silu_and_grad(x):
    s = _sigmoid(x)
    return x * s, s * (1.0 + x * (1.0 - s))


_GELU_C = math.sqrt(2.0 / math.pi)


def _gelu(x):
    return 0.5 * x * (1.0 + jnp.tanh(_GELU_C * (x + 0.044715 * (x * x * x))))


def _gelu_grad(x):
    th = jnp.tanh(_GELU_C * (x + 0.044715 * (x * x * x)))
    return 0.5 * (1.0 + th) + 0.5 * x * (1.0 - th * th) * (_GELU_C * (1.0 + 3.0 * 0.044715 * (x * x)))


def _mm(a, b):
    return jnp.dot(a.astype(MXU), b.astype(MXU), preferred_element_type=F32)


def _mm_nt(a, b):
    return lax.dot_general(a.astype(MXU), b.astype(MXU), (((1,), (1,)), ((), ())), preferred_element_type=F32)


def _mm_tn(a, b):
    return lax.dot_general(a.astype(MXU), b.astype(MXU), (((0,), (0,)), ((), ())), preferred_element_type=F32)


def _ln_fwd(t, g, b):
    mu = jnp.mean(t, axis=-1, keepdims=True)
    tc = t - mu
    var = jnp.mean(tc * tc, axis=-1, keepdims=True)
    rstd = lax.rsqrt(var + LN_EPS)
    xhat = tc * rstd
    return xhat * g + b, xhat, rstd


def _ln_bwd(dy, xhat, rstd, g):
    dxh = dy * g
    m1 = jnp.mean(dxh, axis=-1, keepdims=True)
    m2 = jnp.mean(dxh * xhat, axis=-1, keepdims=True)
    return rstd * (dxh - m1 - xhat * m2)


def _colsum(a):
    return jnp.sum(a, axis=0, keepdims=True)


def _up(x, d, rows, fill):
    n = x.shape[0]
    return jnp.where(rows < n - d, pltpu.roll(x, n - d, 0), fill)


SUB = 8
TILE_STEPS = (1, 2, 4)


def _r8(width):
    return lax.broadcasted_iota(jnp.int32, (SUB, width), 0)


def _scan_real(a, u, carry, reverse=False):
    r8 = _r8(a.shape[1])
    n = a.shape[0] // SUB
    outs = [None] * n
    for k in (reversed(range(n)) if reverse else range(n)):
        A, U = a[SUB * k:SUB * k + SUB], u[SUB * k:SUB * k + SUB]
        for d in TILE_STEPS:
            m = (r8 < SUB - d) if reverse else (r8 >= d)
            sh = SUB - d if reverse else d
            U = A * jnp.where(m, pltpu.roll(U, sh, 0), 0.0) + U
            A = A * jnp.where(m, pltpu.roll(A, sh, 0), 1.0)
        h = A * carry + U
        outs[k] = h
        carry = h[0:1] if reverse else h[SUB - 1:SUB]
    return jnp.concatenate(outs, axis=0), carry


def _tile_powers(lr, li, reverse=False):
    width = lr.shape[1]
    r8 = _r8(width)
    steps = []
    pr, pi = lr, li
    er, ei = jnp.broadcast_to(lr, (SUB, width)), jnp.broadcast_to(li, (SUB, width))
    for d in TILE_STEPS:
        m = (r8 < SUB - d) if reverse else (r8 >= d)
        sh = SUB - d if reverse else d
        steps.append((sh, jnp.where(m, pr, 0.0), jnp.where(m, pi, 0.0)))
        er, ei = _cmul(er, ei, jnp.where(m, pltpu.roll(er, sh, 0), 1.0), jnp.where(m, pltpu.roll(ei, sh, 0), 0.0))
        pr, pi = _cmul(pr, pi, pr, pi)
    return steps, (er, ei)


def _scan_lti(xr, xi, carry, steps, e, reverse=False):
    er, ei = e
    kr, ki = carry
    n = xr.shape[0] // SUB
    outr, outi = [None] * n, [None] * n
    for k in (reversed(range(n)) if reverse else range(n)):
        sr, si = xr[SUB * k:SUB * k + SUB], xi[SUB * k:SUB * k + SUB]
        for sh, pr, pi in steps:
            shr, shi = pltpu.roll(sr, sh, 0), pltpu.roll(si, sh, 0)
            sr, si = sr + (pr * shr - pi * shi), si + (pr * shi + pi * shr)
        sr = sr + (er * kr - ei * ki)
        si = si + (er * ki + ei * kr)
        outr[k], outi[k] = sr, si
        kr, ki = (sr[0:1], si[0:1]) if reverse else (sr[SUB - 1:SUB], si[SUB - 1:SUB])
    return jnp.concatenate(outr, axis=0), jnp.concatenate(outi, axis=0), (kr, ki)


def _halo(ref, c, r0):
    rp = pl.multiple_of(jnp.maximum(r0 - 8, 0), 8)
    return jnp.where(c > 0, ref[pl.ds(rp, 8), :], 0.0)


def _conv_taps(xe):
    return [pltpu.roll(xe, 3, 0)[8:, :], pltpu.roll(xe, 2, 0)[8:, :], pltpu.roll(xe, 1, 0)[8:, :], xe[8:, :]]


def _rg_gates(h, wa, wx, ba, bx, sp):
    r = _sigmoid(_mm(h, wa) + ba)
    i = _sigmoid(_mm(h, wx) + bx)
    log_a = (-RG_C) * r * sp
    a = jnp.exp(log_a)
    mult = jnp.sqrt(-jnp.tanh(log_a) * (a * a + 1.0))
    return r, i, a, mult


def _softplus(y):
    return jnp.maximum(y, 0.0) + jnp.log1p(jnp.exp(-jnp.abs(y)))


def _after(token):
    return ([], []) if token is None else ([token], [ANY])


def _inproj_fwd(x, w_in, token=None):
    L = x.shape[0]

    def body(x_ref, w_ref, *rest):
        rest[-1][...] = _mm(x_ref[...], w_ref[...])

    extra, extra_specs = _after(token)
    tm = min(TM_MM, L)
    return pl.pallas_call(
        body, name="inproj_fwd", grid=(L // tm,),
        in_specs=[pl.BlockSpec((tm, D_MODEL), lambda i: (i, 0)), pl.BlockSpec((D_MODEL, Z_W), lambda i: (0, 0))] + extra_specs,
        out_specs=pl.BlockSpec((tm, Z_W), lambda i: (i, 0)),
        out_shape=_S((L, Z_W)), compiler_params=_params(1))(x, w_in, *extra)


def _inproj_bwd(dt1, x, dzx, dzg, dzu, w_in):
    L = x.shape[0]

    def body(dt1_ref, x_ref, dzx_ref, dzg_ref, dzu_ref, w_ref, dx_ref, dw_ref, acc_ref):
        @pl.when(pl.program_id(0) == 0)
        def _():
            acc_ref[...] = jnp.zeros_like(acc_ref)
        dzg = dzg_ref[...]
        dz = jnp.concatenate([dzx_ref[...], dzg[:, :RG_W], dzu_ref[...], dzg[:, RG_W:]], axis=1).astype(MXU)
        xb = x_ref[...].astype(MXU)
        dx_ref[...] = ALPHA * dt1_ref[...] + _mm_nt(dz, w_ref[...])
        for j in range(N_DEV):
            acc_ref[j] += _mm_tn(xb, dz[:, j * W_BLK:(j + 1) * W_BLK])

        @pl.when(pl.program_id(0) == L // TM - 1)
        def _():
            dw_ref[...] = acc_ref[...].astype(WIRE)

    row = lambda w: pl.BlockSpec((TM, w), lambda i: (i, 0))
    wspec = pl.BlockSpec((N_DEV, D_MODEL, W_BLK), lambda i: (0, 0, 0))
    return pl.pallas_call(
        body, name="inproj_bwd", grid=(L // TM,),
        in_specs=[row(D_MODEL), row(D_MODEL), row(RG_W), row(D_MODEL), row(S5_W),
                  pl.BlockSpec((D_MODEL, Z_W), lambda i: (0, 0))],
        out_specs=[row(D_MODEL), wspec],
        out_shape=[_S((L, D_MODEL)), _S((N_DEV, D_MODEL, W_BLK), WIRE)],
        scratch_shapes=[pltpu.VMEM((N_DEV, D_MODEL, W_BLK), F32)],
        compiler_params=_params(1))(dt1, x, dzx, dzg, dzu, w_in)


TM2 = 512


def _dz_block(dzx_ref, dzg_ref, dzu_ref):
    dzg = dzg_ref[...]
    return jnp.concatenate([dzx_ref[...], dzg[:, :RG_W], dzu_ref[...], dzg[:, RG_W:]], axis=1).astype(MXU)


def _inproj_bwd_dw(x, dzx, dzg, dzu, token=None):
    L = x.shape[0]
    extra, extra_specs = _after(token)

    def body(x_ref, dzx_ref, dzg_ref, dzu_ref, *rest):
        dw_ref, acc_ref = rest[len(extra):]
        @pl.when(pl.program_id(0) == 0)
        def _():
            acc_ref[...] = jnp.zeros_like(acc_ref)
        dz = _dz_block(dzx_ref, dzg_ref, dzu_ref)
        xb = x_ref[...].astype(MXU)
        for j in range(N_DEV):
            acc_ref[j] += _mm_tn(xb, dz[:, j * W_BLK:(j + 1) * W_BLK])

        @pl.when(pl.program_id(0) == L // TM2 - 1)
        def _():
            dw_ref[...] = acc_ref[...].astype(WIRE)

    row = lambda w: pl.BlockSpec((TM2, w), lambda i: (i, 0))
    wspec = pl.BlockSpec((N_DEV, D_MODEL, W_BLK), lambda i: (0, 0, 0))
    return pl.pallas_call(
        body, name="inproj_bwd_dw", grid=(L // TM2,),
        in_specs=[row(D_MODEL), row(RG_W), row(D_MODEL), row(S5_W)] + extra_specs, out_specs=wspec,
        out_shape=_S((N_DEV, D_MODEL, W_BLK), WIRE), scratch_shapes=[pltpu.VMEM((N_DEV, D_MODEL, W_BLK), F32)],
        compiler_params=_params(1))(x, dzx, dzg, dzu, *extra)


def _inproj_bwd_dx(dt1, dzx, dzg, dzu, w_in, token=None):
    L = dt1.shape[0]
    extra, extra_specs = _after(token)

    def body(dt1_ref, dzx_ref, dzg_ref, dzu_ref, w_ref, *rest):
        rest[-1][...] = ALPHA * dt1_ref[...] + _mm_nt(_dz_block(dzx_ref, dzg_ref, dzu_ref), w_ref[...])

    tm = min(TM_MM, L)
    row = lambda w: pl.BlockSpec((tm, w), lambda i: (i, 0))
    return pl.pallas_call(
        body, name="inproj_bwd_dx", grid=(L // tm,),
        in_specs=[row(D_MODEL), row(RG_W), row(D_MODEL), row(S5_W), _full((D_MODEL, Z_W))] + extra_specs,
        out_specs=row(D_MODEL), out_shape=_S((L, D_MODEL)), compiler_params=_params(1))(dt1, dzx, dzg, dzu, w_in, *extra)


def _rg_specs(layer):
    tile = lambda rows: pl.BlockSpec((rows, LANE), lambda c: (0, c))
    ptile = lambda rows: pl.BlockSpec((None, rows, LANE), lambda c: (layer, 0, c))
    pheads = pl.BlockSpec((None, 2, RG_HD, RG_HD), lambda c: (layer, c, 0, 0))
    return tile, ptile, pheads, pl.BlockSpec((2, RG_HD, RG_HD), lambda c: (c, 0, 0))


RG_HD = 64


def _bd2(w):
    z = jnp.zeros((RG_HD, RG_HD), w.dtype)
    return jnp.concatenate([jnp.concatenate([w[0], z], axis=1), jnp.concatenate([z, w[1]], axis=1)], axis=0)


def _bd2_diag(m):
    return jnp.stack([m[:RG_HD, :RG_HD], m[RG_HD:, RG_HD:]])


def _rg_fwd(z, cw, cb, wa_bd, wx_bd, ba, bx, lam, layer):
    L = z.shape[0]
    RC = min(RC_RG, L)

    def body(x_ref, cw_ref, cb_ref, wa_ref, wx_ref, ba_ref, bx_ref, lam_ref, hs_ref, *saved):
        row = slice(layer, layer + 1)
        w, b = cw_ref[...], cb_ref[row, :]
        wa, wx, ba_, bx_ = _bd2(wa_ref[...]).astype(MXU), _bd2(wx_ref[...]).astype(MXU), ba_ref[row, :], bx_ref[row, :]
        sp = _softplus(-lam_ref[row, :])

        def step(c, carry):
            r0 = pl.multiple_of(c * RC, RC)
            xe = jnp.concatenate([_halo(x_ref, c, r0), x_ref[pl.ds(r0, RC), :]], axis=0)
            t = _conv_taps(xe)
            h = t[0] * w[0:1] + t[1] * w[1:2] + t[2] * w[2:3] + t[3] * w[3:4] + b
            r, i, a, mult = _rg_gates(h, wa, wx, ba_, bx_, sp)
            hs, carry = _scan_real(a, mult * (i * h), carry)
            hs_ref[pl.ds(r0, RC), :] = hs
            for ref, val in zip(saved, (h, r, i, a, mult)):
                ref[pl.ds(r0, RC), :] = val
            return carry

        lax.fori_loop(0, L // RC, step, jnp.zeros((1, LANE), F32))

    tile, ptile, pheads, _ = _rg_specs(layer)
    return pl.pallas_call(
        body, name="rg_fwd", grid=(N_RG_T,),
        in_specs=[tile(L), ptile(4), tile(2), pheads, pheads, tile(2), tile(2), tile(2)],
        out_specs=[tile(L)] * 6, out_shape=[_S((L, RG_W))] * 6, compiler_params=_params(1))(
            z, cw, cb, wa_bd, wx_bd, ba, bx, lam)


def _rg_bwd(dhs, z, hs, gates, cw, wa_bd, wx_bd, lam, layer):
    L = z.shape[0]
    RC = min(RC_RG, L)

    def body(g_ref, x_ref, hs_ref, h_ref, r_ref, i_ref, a_ref, mult_ref, cw_ref, wa_ref, wx_ref, lam_ref,
             dx_ref, dcw_ref, dcb_ref, dwa_out, dwx_out, dba_ref, dbx_ref, dlam_ref, dwa_ref, dwx_ref):
        w = cw_ref[...]
        wa, wx = _bd2(wa_ref[...]).astype(MXU), _bd2(wx_ref[...]).astype(MXU)
        lam = lam_ref[layer:layer + 1, :]
        sp = _softplus(-lam)
        rows = lax.broadcasted_iota(jnp.int32, (RC, LANE), 0)
        for ref in (dcw_ref, dcb_ref, dwa_ref, dwx_ref, dba_ref, dbx_ref, dlam_ref):
            ref[...] = jnp.zeros_like(ref)
        nch = L // RC

        def step(k, carry):
            cin, nxt = carry
            c = nch - 1 - k
            r0 = pl.multiple_of(c * RC, RC)
            xe = jnp.concatenate([_halo(x_ref, c, r0), x_ref[pl.ds(r0, RC), :]], axis=0)
            t = _conv_taps(xe)
            h, r, i, a, mult = (ref[pl.ds(r0, RC), :] for ref in (h_ref, r_ref, i_ref, a_ref, mult_ref))
            hs_e = jnp.concatenate([_halo(hs_ref, c, r0), hs_ref[pl.ds(r0, RC), :]], axis=0)
            hs_prev = pltpu.roll(hs_e, 1, 0)[8:, :]
            g = g_ref[pl.ds(r0, RC), :]
            cc, cin_new = _scan_real(a, a * g, cin, reverse=True)
            dh = g + _up(cc, 1, rows, cin)
            ih = i * h
            dlog_a = dh * hs_prev * a - (dh * ih) * (a * a) / mult
            di = dh * mult * h
            dhin = dh * mult * i
            dr = dlog_a * ((-RG_C) * sp)
            dlam_ref[...] += _colsum(dlog_a * r)
            dra = dr * r * (1.0 - r)
            dia = di * i * (1.0 - i)
            dwa_ref[...] += _mm_tn(h, dra)
            dwx_ref[...] += _mm_tn(h, dia)
            dba_ref[...] += _colsum(dra)
            dbx_ref[...] += _colsum(dia)
            dhin = dhin + _mm_nt(dra, wa) + _mm_nt(dia, wx)
            de = jnp.concatenate([dhin, nxt], axis=0)
            n = RC + 8
            dx = (dhin * w[3:4] + pltpu.roll(de, n - 1, 0)[:RC, :] * w[2:3]
                  + pltpu.roll(de, n - 2, 0)[:RC, :] * w[1:2] + pltpu.roll(de, n - 3, 0)[:RC, :] * w[0:1])
            dx_ref[pl.ds(r0, RC), :] = dx
            for kk in range(4):
                dcw_ref[kk:kk + 1, :] += _colsum(dhin * t[kk])
            dcb_ref[...] += _colsum(dhin)
            return cin_new, dhin[0:8, :]

        lax.fori_loop(0, nch, step, (jnp.zeros((1, LANE), F32), jnp.zeros((8, LANE), F32)))
        dlam_ref[...] = dlam_ref[...] * (RG_C * _sigmoid(-lam))
        dwa_out[...], dwx_out[...] = _bd2_diag(dwa_ref[...]), _bd2_diag(dwx_ref[...])

    tile, ptile, pheads, gheads = _rg_specs(layer)
    heads = _S((2 * N_RG_T, RG_HD, RG_HD))
    return pl.pallas_call(
        body, name="rg_bwd", grid=(N_RG_T,),
        in_specs=[tile(L)] * 8 + [ptile(4), pheads, pheads, tile(2)],
        out_specs=[tile(L), tile(4), tile(1), gheads, gheads, tile(1), tile(1), tile(1)],
        out_shape=[_S((L, RG_W)), _S((4, RG_W)), _S((1, RG_W)), heads, heads, _S((1, RG_W)), _S((1, RG_W)), _S((1, RG_W))],
        scratch_shapes=[pltpu.VMEM((LANE, LANE), F32), pltpu.VMEM((LANE, LANE), F32)],
        compiler_params=_params(1))(dhs, z, hs, *gates, cw, wa_bd, wx_bd, lam)


def _cmul(ar, ai, br, bi):
    return ar * br - ai * bi, ar * bi + ai * br


S5_TW = S5_N // N_S5_T


S5_H = 16
S5_GT = LANE // S5_H


def _s5_specs(L, layer):
    in_tile = pl.BlockSpec((L, LANE), lambda t: (0, t))
    st = pl.BlockSpec((L, S5_TW), lambda t: (0, t))
    pg = pl.BlockSpec((None, S5_GT, S5_H, S5_P), lambda t: (layer * N_S5_T + t, 0, 0, 0))
    plb = pl.BlockSpec((None, S5_GT, S5_P), lambda t: (layer * N_S5_T + t, 0, 0))
    gg = pl.BlockSpec((None, S5_GT, S5_H, S5_P), lambda t: (t, 0, 0, 0))
    glb = pl.BlockSpec((None, S5_GT, S5_P), lambda t: (t, 0, 0))
    dv = pl.BlockSpec((1, LANE), lambda t: (0, t))
    return in_tile, st, pg, plb, gg, glb, dv


def _bd8(blocks):
    rows = []
    for g in range(S5_GT):
        pieces = [blocks[g]]
        if g:
            pieces.insert(0, jnp.zeros((S5_H, S5_P * g), blocks.dtype))
        if g < S5_GT - 1:
            pieces.append(jnp.zeros((S5_H, S5_P * (S5_GT - 1 - g)), blocks.dtype))
        rows.append(jnp.concatenate(pieces, axis=1))
    return jnp.concatenate(rows, axis=0)


def _bd8_diag(m):
    return jnp.stack([m[S5_H * g:S5_H * (g + 1), S5_P * g:S5_P * (g + 1)] for g in range(S5_GT)])


def _row8(v):
    return jnp.concatenate([v[g:g + 1] for g in range(S5_GT)], axis=1)


def _row8_split(r):
    return jnp.concatenate([r[:, S5_P * g:S5_P * (g + 1)] for g in range(S5_GT)], axis=0)


def _layer_row_tile(layer):
    return pl.BlockSpec((None, 1, LANE), lambda t: (layer, 0, t))


def _s5_fwd(z, bb_re, bb_im, lb_re, lb_im, c_re, c_im, dvec, layer):
    L = z.shape[0]

    def body(u_ref, bbr_ref, bbi_ref, lr_ref, li_ref, cr_ref, ci_ref, d_ref, y_ref, sr_ref, si_ref):
        bbr, bbi = _bd8(bbr_ref[...]).astype(MXU), _bd8(bbi_ref[...]).astype(MXU)
        cr, ci = _bd8(cr_ref[...]).astype(MXU), _bd8(ci_ref[...]).astype(MXU)
        dv = d_ref[...]
        steps, e = _tile_powers(_row8(lr_ref[...]), _row8(li_ref[...]))

        def step(c, carry):
            r0 = pl.multiple_of(c * RC, RC)
            u = u_ref[pl.ds(r0, RC), :]
            ub = u.astype(MXU)
            sr = jnp.dot(ub, bbr, preferred_element_type=F32)
            si = jnp.dot(ub, bbi, preferred_element_type=F32)
            sr, si, carry = _scan_lti(sr, si, carry, steps, e)
            sr_ref[pl.ds(r0, RC), :] = sr
            si_ref[pl.ds(r0, RC), :] = si
            y_ref[pl.ds(r0, RC), :] = dv * u + (_mm_nt(sr, cr) - _mm_nt(si, ci))
            return carry

        zero = jnp.zeros((1, S5_TW), F32)
        lax.fori_loop(0, L // RC, step, (zero, zero))

    in_tile, st, pg, plb, _, _, _ = _s5_specs(L, layer)
    u_tile = pl.BlockSpec((L, LANE), lambda t: (0, C_S5U // LANE + t))
    return pl.pallas_call(
        body, name="s5_fwd", grid=(N_S5_T,),
        in_specs=[u_tile, pg, pg, plb, plb, pg, pg, _layer_row_tile(layer)],
        out_specs=[in_tile, st, st],
        out_shape=[_S((L, S5_W)), _S((L, S5_N)), _S((L, S5_N))],
        compiler_params=_params(1))(z, bb_re, bb_im, lb_re, lb_im, c_re, c_im, dvec)


def _s5_bwd(dy0, z, s_re, s_im, bb_re, bb_im, lb_re, lb_im, c_re, c_im, dvec, layer, token=None):
    L = z.shape[0]
    extra, extra_specs = _after(token)

    def body(dy_ref, u_ref, sr_ref, si_ref, bbr_ref, bbi_ref, lr_ref, li_ref, cr_ref, ci_ref, d_ref, *rest):
        (du_ref, dbbr_out, dbbi_out, dlr_out, dli_out, dcr_out, dci_out, dd_ref,
         dbbr_ref, dbbi_ref, dcr_ref, dci_ref, dlr_ref, dli_ref) = rest[len(extra):]
        bbr, bbi = _bd8(bbr_ref[...]).astype(MXU), _bd8(bbi_ref[...]).astype(MXU)
        cr, ci = _bd8(cr_ref[...]).astype(MXU), _bd8(ci_ref[...]).astype(MXU)
        lr, li = _row8(lr_ref[...]), -_row8(li_ref[...])
        dv = d_ref[...]
        steps, e = _tile_powers(lr, li, reverse=True)
        for ref in (dbbr_ref, dbbi_ref, dlr_ref, dli_ref, dcr_ref, dci_ref, dd_ref):
            ref[...] = jnp.zeros_like(ref)
        nch = L // RC

        def step(k, carry):
            c = nch - 1 - k
            r0 = pl.multiple_of(c * RC, RC)
            dy = dy_ref[pl.ds(r0, RC), :]
            u = u_ref[pl.ds(r0, RC), :]
            dyb, ub = dy.astype(MXU), u.astype(MXU)
            sr, si = sr_ref[pl.ds(r0, RC), :], si_ref[pl.ds(r0, RC), :]
            dcr_ref[...] += _mm_tn(dyb, sr)
            dci_ref[...] -= _mm_tn(dyb, si)
            gr = jnp.dot(dyb, cr, preferred_element_type=F32)
            gi = -jnp.dot(dyb, ci, preferred_element_type=F32)
            gr, gi, carry = _scan_lti(gr, gi, carry, steps, e, reverse=True)
            pr_ = pltpu.roll(jnp.concatenate([_halo(sr_ref, c, r0), sr], axis=0), 1, 0)[8:, :]
            pi_ = pltpu.roll(jnp.concatenate([_halo(si_ref, c, r0), si], axis=0), 1, 0)[8:, :]
            dlr_ref[...] += _colsum(pr_ * gr + pi_ * gi)
            dli_ref[...] += _colsum(pr_ * gi - pi_ * gr)
            grb, gib = gr.astype(MXU), gi.astype(MXU)
            dbbr_ref[...] += _mm_tn(ub, grb)
            dbbi_ref[...] += _mm_tn(ub, gib)
            du_ref[pl.ds(r0, RC), :] = dv * dy + (_mm_nt(grb, bbr) + _mm_nt(gib, bbi))
            dd_ref[...] += _colsum(dy * u)
            return carry

        zero = jnp.zeros((1, S5_TW), F32)
        lax.fori_loop(0, nch, step, (zero, zero))
        dbbr_out[...], dbbi_out[...] = _bd8_diag(dbbr_ref[...]), _bd8_diag(dbbi_ref[...])
        dcr_out[...], dci_out[...] = _bd8_diag(dcr_ref[...]), _bd8_diag(dci_ref[...])
        dlr_out[...], dli_out[...] = _row8_split(dlr_ref[...]), _row8_split(dli_ref[...])

    in_tile, st, pg, plb, gg, glb, dv = _s5_specs(L, layer)
    u_tile = pl.BlockSpec((L, LANE), lambda t: (0, C_S5U // LANE + t))
    groups, rows = _S((N_S5_T, S5_GT, S5_H, S5_P)), _S((N_S5_T, S5_GT, S5_P))
    wide = pltpu.VMEM((LANE, S5_TW), F32)
    return pl.pallas_call(
        body, name="s5_bwd", grid=(N_S5_T,),
        in_specs=[in_tile, u_tile, st, st, pg, pg, plb, plb, pg, pg, _layer_row_tile(layer)] + extra_specs,
        out_specs=[in_tile, gg, gg, glb, glb, gg, gg, dv],
        out_shape=[_S((L, S5_W)), groups, groups, rows, rows, groups, groups, _S((1, S5_W))],
        scratch_shapes=[wide, wide, wide, wide, pltpu.VMEM((1, S5_TW), F32), pltpu.VMEM((1, S5_TW), F32)],
        compiler_params=_params(1))(dy0, z, s_re, s_im, bb_re, bb_im, lb_re, lb_im, c_re, c_im, dvec, *extra)


def _disc(ar, ai, ls):
    dt = jnp.exp(ls)
    mag = jnp.exp(ar * dt)
    lr = mag * jnp.cos(ai * dt)
    li = mag * jnp.sin(ai * dt)
    den = ar * ar + ai * ai
    cr = ((lr - 1.0) * ar + li * ai) / den
    ci = (li * ar - (lr - 1.0) * ai) / den
    return lr, li, cr, ci


def _s5_disc_fwd(ar, ai, ls, token=None):
    extra, extra_specs = _after(token)

    def body(ar_ref, ai_ref, ls_ref, *rest):
        lr_ref, li_ref, cr_ref, ci_ref = rest[len(extra):]
        lr, li, cr, ci = _disc(ar_ref[...], ai_ref[...], ls_ref[...])
        lr_ref[...], li_ref[...], cr_ref[...], ci_ref[...] = lr, li, cr, ci

    sh = _S(ar.shape)
    vm = pl.BlockSpec(memory_space=pltpu.VMEM)
    return pl.pallas_call(body, name="s5_disc_fwd", in_specs=[vm, vm, vm] + extra_specs, out_shape=[sh, sh, sh, sh])(
        ar, ai, ls, *extra)


def _s5_disc_bwd(ar, ai, ls, dlr, dli, dcr, dci):
    def body(ar_ref, ai_ref, ls_ref, dlr_ref, dli_ref, dcr_ref, dci_ref, dar_ref, dai_ref, dls_ref):
        _, vjp = jax.vjp(_disc, ar_ref[...], ai_ref[...], jnp.broadcast_to(ls_ref[...], ar_ref.shape))
        dar, dai, dls = vjp((dlr_ref[...], dli_ref[...], dcr_ref[...], dci_ref[...]))
        dar_ref[...], dai_ref[...] = dar, dai
        dls_ref[...] = jnp.sum(dls, axis=1, keepdims=True)

    return pl.pallas_call(body, name="s5_disc_bwd", out_shape=[_S(ar.shape), _S(ar.shape), _S(ls.shape)])(
        ar, ai, ls, dlr, dli, dcr, dci)


def _s5_bscale_fwd(cr, ci, br, bi):
    def body(cr_ref, ci_ref, br_ref, bi_ref, or_ref, oi_ref):
        or_ref[...], oi_ref[...] = _cmul(cr_ref[...], ci_ref[...], br_ref[...], bi_ref[...])

    return pl.pallas_call(body, name="s5_bscale_fwd", out_shape=[_S(br.shape), _S(br.shape)])(cr, ci, br, bi)


def _s5_bscale_bwd(cr, ci, br, bi, gr, gi):
    def body(cr_ref, ci_ref, br_ref, bi_ref, gr_ref, gi_ref, dbr_ref, dbi_ref, dcr_ref, dci_ref):
        cr_, ci_, br_, bi_, gr_, gi_ = (r[...] for r in (cr_ref, ci_ref, br_ref, bi_ref, gr_ref, gi_ref))
        dbr_ref[...] = cr_ * gr_ + ci_ * gi_
        dbi_ref[...] = cr_ * gi_ - ci_ * gr_
        dcr_ref[...] = jnp.sum(gr_ * br_ + gi_ * bi_, axis=1, keepdims=True)
        dci_ref[...] = jnp.sum(gi_ * br_ - gr_ * bi_, axis=1, keepdims=True)

    return pl.pallas_call(body, name="s5_bscale_bwd",
                          out_shape=[_S(br.shape), _S(br.shape), _S(cr.shape), _S(cr.shape)])(cr, ci, br, bi, gr, gi)


def _row(w):
    return pl.BlockSpec((TM, w), lambda i: (i, 0))


def _full(shape):
    return pl.BlockSpec(tuple(shape), lambda i: (0,) * len(shape))


def _gate_rows():
    return [pl.BlockSpec((TM, RG_W), lambda i: (i, C_RGG // RG_W))] + [
        pl.BlockSpec((TM, LANE), lambda i, k=k: (i, C_S5G // LANE + k)) for k in range(N_S5_T)]


def _p_rows(layer):
    return pl.BlockSpec((None, None, TM, PLE_D), lambda i: (layer, 0, i, 0))


DEPTH = 2


def _lrow(layer, width):
    return _full((DEPTH, width))


def _pick(ref, layer):
    return ref[layer:layer + 1, :]


def _post_fwd(x, hs, z, y0, p, w_glu, b_glu, w_out, g1, b1, ple_w, w_pg, b_pg, g2, b2, layer):
    L = x.shape[0]

    def body(x_ref, hs_ref, zg_ref, zs0_ref, zs1_ref, zs2_ref, y0_ref, p_ref, wg_ref, bg_ref, wo_ref, g1_ref, b1_ref, pw_ref,
             wpg_ref, bpg_ref, g2_ref, b2_ref, x2_ref, xh1_ref, xh2_ref, gt_ref, rstd1_ref, rstd2_ref):
        rg_gate = zg_ref[...]
        s5_gate = jnp.concatenate([zs0_ref[...], zs1_ref[...], zs2_ref[...]], axis=1)
        rg_y = hs_ref[...] * _silu_and_grad(rg_gate)[0]
        y1 = _gelu(y0_ref[...])
        gl = _sigmoid(_mm(y1, wg_ref[...]) + _pick(bg_ref, layer))
        s5_y = (y1 * gl) * _silu_and_grad(s5_gate)[0]
        mix = _mm(jnp.concatenate([rg_y.astype(MXU), s5_y.astype(MXU)], axis=1), wo_ref[...])
        t1 = ALPHA * x_ref[...] + mix
        x1, xh1, rstd1 = _ln_fwd(t1, _pick(g1_ref, layer), _pick(b1_ref, layer))
        q = _mm(p_ref[...], pw_ref[...])
        gt = _sigmoid(_mm(x1, wpg_ref[...]) + _pick(bpg_ref, layer))
        t2 = ALPHA * x1 + q * gt
        x2, xh2, rstd2 = _ln_fwd(t2, _pick(g2_ref, layer), _pick(b2_ref, layer))
        x2_ref[...], xh1_ref[...], xh2_ref[...], gt_ref[...] = x2, xh1, xh2, gt
        rstd1_ref[...], rstd2_ref[...] = rstd1, rstd2

    vec = _lrow(layer, D_MODEL)
    return pl.pallas_call(
        body, name="post_fwd", grid=(L // TM,),
        in_specs=[_row(D_MODEL), _row(RG_W), *_gate_rows(), _row(S5_W), _p_rows(layer), _full((S5_W, S5_W)),
                  _lrow(layer, S5_W), _full((D_MODEL, D_MODEL)), vec, vec, _full((PLE_D, D_MODEL)), _full((D_MODEL, D_MODEL)),
                  vec, vec, vec],
        out_specs=[_row(D_MODEL)] * 4 + [_row(1)] * 2, out_shape=[_S((L, D_MODEL))] * 4 + [_S((L, 1))] * 2,
        compiler_params=_params(1))(x, hs, z, z, z, z, y0, p, w_glu, b_glu, w_out, g1, b1, ple_w, w_pg, b_pg, g2, b2)


def _post_bwd_a(dx2_or_target, is_top, xh2, xh1, rstd2, rstd1, gt, p, ple_w, w_pg, g1, b1, g2, b2, layer, token=None):
    L = xh1.shape[0]
    extra, extra_specs = _after(token)

    def body(d_ref, xh2_ref, xh1_ref, rstd2_ref, rstd1_ref, gt_ref, p_ref, pw_ref, wpg_ref, g1_ref, b1_ref, g2_ref,
             b2_ref, *rest):
        (dt1_ref, dpw_out, dwpg_out, dbpg_ref, dg1_ref, db1_ref, dg2_ref, db2_ref, loss_ref, dpw_ref,
         dwpg_ref) = rest[len(extra):]
        @pl.when(pl.program_id(0) == 0)
        def _():
            for ref in (dpw_ref, dwpg_ref, dbpg_ref, dg1_ref, db1_ref, dg2_ref, db2_ref, loss_ref):
                ref[...] = jnp.zeros_like(ref)

        g1, g2 = _pick(g1_ref, layer), _pick(g2_ref, layer)
        xh1, xh2, rstd1, rstd2 = xh1_ref[...], xh2_ref[...], rstd1_ref[...], rstd2_ref[...]
        x1 = xh1 * g1 + _pick(b1_ref, layer)
        if is_top:
            err = (xh2 * g2 + _pick(b2_ref, layer)) - d_ref[...]
            loss_ref[...] += _colsum(err * err)
            dx2 = err * (1.0 / D_MODEL)
        else:
            dx2 = d_ref[...]
        p = p_ref[...]
        q, gt = _mm(p, pw_ref[...]), gt_ref[...]
        dg2_ref[...] += _colsum(dx2 * xh2)
        db2_ref[...] += _colsum(dx2)
        dt2 = _ln_bwd(dx2, xh2, rstd2, g2)
        dq = dt2 * gt
        dgpre = (dt2 * q) * gt * (1.0 - gt)
        dpw_ref[...] += _mm_tn(p, dq)
        dwpg_ref[...] += _mm_tn(x1, dgpre)
        dbpg_ref[...] += _colsum(dgpre)
        dx1 = ALPHA * dt2 + _mm_nt(dgpre, wpg_ref[...])
        dg1_ref[...] += _colsum(dx1 * xh1)
        db1_ref[...] += _colsum(dx1)
        dt1_ref[...] = _ln_bwd(dx1, xh1, rstd1, g1)

        @pl.when(pl.program_id(0) == L // TM - 1)
        def _():
            dpw_out[...] = dpw_ref[...].astype(WIRE)
            dwpg_out[...] = dwpg_ref[...].astype(WIRE)

    vec, lvec = _full((1, D_MODEL)), _lrow(layer, D_MODEL)
    return pl.pallas_call(
        body, name="post_bwd_a_top" if is_top else "post_bwd_a", grid=(L // TM,),
        in_specs=[_row(D_MODEL), _row(D_MODEL), _row(D_MODEL), _row(1), _row(1), _row(D_MODEL), _p_rows(layer),
                  _full((PLE_D, D_MODEL)), _full((D_MODEL, D_MODEL)), lvec, lvec, lvec, lvec] + extra_specs,
        out_specs=[_row(D_MODEL), _full((PLE_D, D_MODEL)), _full((D_MODEL, D_MODEL)), vec, vec, vec, vec, vec, vec],
        out_shape=[_S((L, D_MODEL)), _S((PLE_D, D_MODEL), WIRE), _S((D_MODEL, D_MODEL), WIRE)] + [_S((1, D_MODEL))] * 6,
        scratch_shapes=[pltpu.VMEM((PLE_D, D_MODEL), F32), pltpu.VMEM((D_MODEL, D_MODEL), F32)],
        compiler_params=_params(1))(dx2_or_target, xh2, xh1, rstd2, rstd1, gt, p, ple_w, w_pg, g1, b1, g2, b2, *extra)


def _post_bwd_b(dt1, z, hs, y0, w_out, w_glu, b_glu, layer):
    L = dt1.shape[0]

    def body(dt1_ref, zg_ref, zs0_ref, zs1_ref, zs2_ref, hs_ref, y0_ref, wo_ref, wg_ref, bg_ref,
             dhs_ref, dy0_ref, dzg_ref, dwo_out, dwg_out, dbg_ref, dwo_ref, dwg_ref):
        @pl.when(pl.program_id(0) == 0)
        def _():
            for ref in (dwo_ref, dwg_ref, dbg_ref):
                ref[...] = jnp.zeros_like(ref)

        dt1b = dt1_ref[...].astype(MXU)
        dm = _mm_nt(dt1b, wo_ref[...])
        d_rgy, d_s5y = dm[:, :RG_W], dm[:, RG_W:]
        rg_gate = zg_ref[...]
        s5_gate = jnp.concatenate([zs0_ref[...], zs1_ref[...], zs2_ref[...]], axis=1)
        hs = hs_ref[...]
        sl, dsl = _silu_and_grad(rg_gate)
        dhs_ref[...] = d_rgy * sl
        dzg_ref[:, :RG_W] = d_rgy * hs * dsl
        y0 = y0_ref[...]
        y1 = _gelu(y0)
        gl = _sigmoid(_mm(y1, wg_ref[...]) + _pick(bg_ref, layer))
        y2 = y1 * gl
        sl2, dsl = _silu_and_grad(s5_gate)
        m = jnp.concatenate([(hs * sl).astype(MXU), (y2 * sl2).astype(MXU)], axis=1)
        dwo_ref[...] += _mm_tn(m, dt1b)
        dy2 = d_s5y * sl2
        dzg_ref[:, RG_W:] = d_s5y * y2 * dsl
        dglpre = (dy2 * y1) * gl * (1.0 - gl)
        dwg_ref[...] += _mm_tn(y1, dglpre)
        dbg_ref[...] += _colsum(dglpre)
        dy1 = dy2 * gl + _mm_nt(dglpre, wg_ref[...])
        dy0_ref[...] = dy1 * _gelu_grad(y0)

        @pl.when(pl.program_id(0) == L // TM - 1)
        def _():
            dwo_out[...] = dwo_ref[...].astype(WIRE)
            dwg_out[...] = dwg_ref[...].astype(WIRE)

    return pl.pallas_call(
        body, name="post_bwd_b", grid=(L // TM,),
        in_specs=[_row(D_MODEL), *_gate_rows(), _row(RG_W), _row(S5_W), _full((D_MODEL, D_MODEL)),
                  _full((S5_W, S5_W)), _lrow(layer, S5_W)],
        out_specs=[_row(RG_W), _row(S5_W), _row(D_MODEL), _full((D_MODEL, D_MODEL)), _full((S5_W, S5_W)), _full((1, S5_W))],
        out_shape=[_S((L, RG_W)), _S((L, S5_W)), _S((L, D_MODEL)), _S((D_MODEL, D_MODEL), WIRE), _S((S5_W, S5_W), WIRE),
                   _S((1, S5_W))],
        scratch_shapes=[pltpu.VMEM((D_MODEL, D_MODEL), F32), pltpu.VMEM((S5_W, S5_W), F32)],
        compiler_params=_params(1))(dt1, z, z, z, z, hs, y0, w_out, w_glu, b_glu)


def _adamw(parts, w, m, v, token=None):
    nl = len(parts)
    extra, extra_specs = _after(token)
    n, R, C = parts[0].shape
    tr = R
    for cand in (512, 256, 128, 64, 32, 16, 8):
        if R % cand == 0 and n * cand * C * 4 <= 4 * 1024 * 1024:
            tr = cand
            break
    nblk = R // tr

    def body(*refs):
        p_refs = refs[:nl]
        w_ref, m_ref, v_ref = refs[nl:nl + 3]
        g_ref, d_ref, nm_ref, nv_ref = refs[nl + 3 + len(extra):]
        layer = pl.program_id(0)
        g = None
        for li, p_ref in enumerate(p_refs):
            s = p_ref[0].astype(F32)
            for k in range(1, n):
                s = s + p_ref[k].astype(F32)
            g = s if g is None else jnp.where(layer == li, s, g)
        nm = B1 * m_ref[...] + (1.0 - B1) * g
        nv = B2 * v_ref[...] + (1.0 - B2) * (g * g)
        d_ref[...] = (-LR) * ((nm / BC1) / (jnp.sqrt(nv / BC2) + EPS) + WD * w_ref[...])
        g_ref[...], nm_ref[...], nv_ref[...] = g, nm, nv

    def part_spec(li):
        return pl.BlockSpec((n, tr, C), lambda l, i: (0, jnp.where(l == li, i, jnp.where(l < li, 0, nblk - 1)), 0))

    blk = pl.BlockSpec((tr, C), lambda l, i: (l * nblk + i, 0))
    return pl.pallas_call(
        body, name="adamw", grid=(nl, nblk),
        in_specs=[part_spec(li) for li in range(nl)] + [blk, blk, blk] + extra_specs,
        out_specs=[blk] * 4, out_shape=[_S((nl * R, C))] * 4, compiler_params=_params(2))(*parts, w, m, v, *extra)


def _adamw_sharded(names, recv, w, m, v, name, token=None):
    n, nl = len(names), len(recv)
    extra, extra_specs = _after(token)
    n_in = n * (nl + 3)

    def body(*refs):
        outs = refs[n_in + len(extra):]
        for j in range(n):
            w_ref, m_ref, v_ref = (refs[(nl + t) * n + j] for t in range(3))
            g_ref, d_ref, nm_ref, nv_ref = (outs[t * n + j] for t in range(4))
            for l in range(nl):
                p_ref = refs[l * n + j]
                g = p_ref[0].astype(F32)
                for q in range(1, N_DEV):
                    g = g + p_ref[q].astype(F32)
                nm = B1 * m_ref[l] + (1.0 - B1) * g
                nv = B2 * v_ref[l] + (1.0 - B2) * (g * g)
                d_ref[l] = (-LR) * ((nm / BC1) / (jnp.sqrt(nv / BC2) + EPS) + WD * w_ref[l])
                g_ref[l], nm_ref[l], nv_ref[l] = g, nm, nv

    ins = [r[k] for r in recv for k in names] + [t[k] for t in (w, m, v) for k in names]
    vm = pl.BlockSpec(memory_space=pltpu.VMEM)
    outs = pl.pallas_call(body, name=name, in_specs=[vm] * n_in + extra_specs,
                          out_shape=[_S(w[k].shape) for _ in range(4) for k in names],
                          compiler_params=pltpu.CompilerParams(vmem_limit_bytes=VMEM_LIMIT))(*ins, *extra)
    return [{k: outs[t * n + j] for j, k in enumerate(names)} for t in range(4)]


def _adamw_natural(names, g, w, m, v, name):
    n = len(names)

    def body(*refs):
        for j in range(n):
            g_ref, w_ref, m_ref, v_ref, d_ref, nm_ref, nv_ref = (refs[k * n + j] for k in range(7))
            gj = g_ref[...]
            nm = B1 * m_ref[...] + (1.0 - B1) * gj
            nv = B2 * v_ref[...] + (1.0 - B2) * (gj * gj)
            d_ref[...] = (-LR) * ((nm / BC1) / (jnp.sqrt(nv / BC2) + EPS) + WD * w_ref[...])
            nm_ref[...], nv_ref[...] = nm, nv

    ins = [t[k] for t in (g, w, m, v) for k in names]
    outs = pl.pallas_call(body, name=name, out_shape=[_S(w[k].shape) for _ in range(3) for k in names],
                          compiler_params=pltpu.CompilerParams(vmem_limit_bytes=VMEM_LIMIT))(*ins)
    return [{k: outs[t * n + j] for j, k in enumerate(names)} for t in range(3)]


def _me():
    return lax.axis_index("x"), lax.axis_index("y"), lax.axis_index("c")


def _lin(dev):
    return 4 * dev[0] + 2 * dev[1] + dev[2]


def _blk(ref, axis, size, idx):
    nd = len(ref.shape)
    start = idx * size
    if axis == nd - 1 and size % LANE == 0:
        start = pl.multiple_of(start, LANE)
    elif axis == nd - 2 and size % 16 == 0:
        start = pl.multiple_of(start, 16)
    ix = [slice(None)] * nd
    ix[axis] = pl.ds(start, size)
    return ref.at[tuple(ix)]


HBM_SPEC = pl.BlockSpec(memory_space=pltpu.HBM)
SEM_SPEC = pl.BlockSpec(memory_space=pltpu.SEMAPHORE)
EFFECT = pltpu.SideEffectType.DATAFLOW_SIDE_EFFECTING


def _peers(x, y, c):
    flip = lambda v, f: 1 - v if f else v
    return [(flip(x, k & 4), flip(y, k & 2), flip(c, k & 1)) for k in range(1, N_DEV)]


def _land_shape(mode, s, axis):
    if mode == "gather":
        return s.shape[:axis] + (N_DEV * s.shape[axis],) + s.shape[axis + 1:]
    return (N_DEV,) + s.shape[:axis] + (s.shape[axis] // N_DEV,) + s.shape[axis + 1:]


def _src_view(mode, ref, axis, peer):
    return ref if mode == "gather" else _blk(ref, axis, ref.shape[axis] // N_DEV, peer)


def _dst_view(mode, land, axis, sender):
    return _blk(land, axis, land.shape[axis] // N_DEV, sender) if mode == "gather" else land.at[sender]


def _blocks(mode, land, axis, k):
    if mode == "gather":
        ix = [slice(None)] * len(land.shape)
        ix[axis] = pl.ds(0, k * (land.shape[axis] // N_DEV))
        return land.at[tuple(ix)]
    return land.at[pl.ds(0, k)]


ARRIVALS = {None: N_DEV - 1, "near": 4, "relay": 3}


def _routes(route, x, y, c):
    me, sibling = (x, y, c), (x, y, 1 - c)
    chips = [(1 - x, y), (x, 1 - y), (1 - x, 1 - y)]
    if route == "near":
        return [(me, sibling)] + [(me, (*chip, c)) for chip in chips]
    if route == "relay":
        return [((*chip, c), sibling) for chip in chips]
    return [(me, peer) for peer in _peers(x, y, c)]


def _place_own(mode, srcs, axes, name, after=None):
    n = len(srcs)
    extra, extra_specs = _after(after)

    def body(me_ref, *refs):
        for a in range(n):
            out = refs[n + len(extra) + a]
            out[...] = refs[a][...].reshape(out.shape)

    def at_me(shape, axis):
        return lambda i, me: tuple(me[0] if d == axis else 0 for d in range(len(shape)))

    in_specs, out_specs = [], []
    for s, axis in zip(srcs, axes):
        if mode == "gather":
            in_specs.append(pl.BlockSpec(s.shape, lambda i, me, nd=len(s.shape): (0,) * nd))
            out_specs.append(pl.BlockSpec(s.shape, at_me(s.shape, axis)))
        else:
            blk = s.shape[:axis] + (s.shape[axis] // N_DEV,) + s.shape[axis + 1:]
            in_specs.append(pl.BlockSpec(blk, at_me(blk, axis)))
            out_specs.append(pl.BlockSpec((1,) + blk, at_me((1,) + blk, 0)))
    me = _lin(_me()).astype(jnp.int32).reshape(1)
    return pl.pallas_call(
        body, name=name, out_shape=[_S(_land_shape(mode, s, a), s.dtype) for s, a in zip(srcs, axes)],
        grid_spec=pltpu.PrefetchScalarGridSpec(num_scalar_prefetch=1, grid=(1,), in_specs=in_specs + extra_specs,
                                               out_specs=out_specs),
        compiler_params=_params(1))(me, *srcs, *extra)


def _place_shards(shards, layers, axes, dtypes, name, after=None):
    n = len(shards)
    extra, extra_specs = _after(after)

    def body(me_ref, *refs):
        for a in range(n):
            out = refs[n + len(extra) + a]
            out[...] = refs[a][...].astype(out.dtype)

    in_specs, out_specs, out_shape = [], [], []
    for s, layer, axis, dt in zip(shards, layers, axes, dtypes):
        shape = s.shape if layer is None else s.shape[1:]
        nd = len(shape)
        if layer is None:
            in_specs.append(pl.BlockSpec(shape, lambda i, me, nd=nd: (0,) * nd))
        else:
            in_specs.append(pl.BlockSpec((None,) + shape, lambda i, me, nd=nd, layer=layer: (layer,) + (0,) * nd))
        out_specs.append(pl.BlockSpec(shape, lambda i, me, nd=nd, axis=axis: tuple(me[0] if d == axis else 0 for d in range(nd))))
        out_shape.append(_S(shape[:axis] + (N_DEV * shape[axis],) + shape[axis + 1:], dt))
    me = _lin(_me()).astype(jnp.int32).reshape(1)
    return pl.pallas_call(
        body, name=name, out_shape=out_shape,
        grid_spec=pltpu.PrefetchScalarGridSpec(num_scalar_prefetch=1, grid=(1,), in_specs=in_specs + extra_specs,
                                               out_specs=out_specs),
        compiler_params=_params(1))(me, *shards, *extra)


def _push_start(mode, srcs, lands, axes, name, route=None):
    n, ns = len(lands), len(srcs)

    def body(*refs):
        src_refs, land_refs = refs[:ns], refs[ns:ns + n]
        send_sems, recv_sems = refs[ns + n], refs[ns + n + 1]
        token = refs[-1]
        x, y, c = _me()
        for a in range(n):
            for block, peer in _routes(route, x, y, c):
                there = _dst_view(mode, land_refs[a], axes[a], _lin(block))
                pltpu.make_async_remote_copy(
                    src_ref=_src_view(mode, src_refs[a], axes[a], _lin(peer)) if ns else there, dst_ref=there,
                    send_sem=send_sems.at[a], recv_sem=recv_sems.at[a], device_id=peer, device_id_type=MESH).start()
        token[...] = jnp.zeros_like(token)

    hbm = lambda s: pltpu.HBM(s.shape, s.dtype)
    outs = pl.pallas_call(
        body, name=name,
        out_shape=(pltpu.SemaphoreType.DMA((n,)), pltpu.SemaphoreType.DMA((n,)), *[hbm(s) for s in srcs], *[hbm(s) for s in lands],
                   _S((SUB, LANE))),
        in_specs=[HBM_SPEC] * (ns + n),
        out_specs=(SEM_SPEC, SEM_SPEC, *[HBM_SPEC] * (ns + n), pl.BlockSpec(memory_space=pltpu.VMEM)),
        input_output_aliases={i: 2 + i for i in range(ns + n)},
        compiler_params=pltpu.CompilerParams(has_side_effects=EFFECT),
    )(*[pltpu.with_memory_space_constraint(s, pltpu.HBM) for s in list(srcs) + list(lands)])
    return outs[0], outs[1], outs[2:2 + ns], outs[2 + ns:2 + ns + n], outs[-1]


def _push_wait(mode, send_sems, recv_sems, srcs, lands, axes, after, name, first=0, route=None):
    n, ns = len(lands), len(srcs)
    after = list(after) if isinstance(after, (list, tuple)) else [after]

    def body(*refs):
        land_refs = refs[ns:ns + n]
        send_sems, recv_sems = refs[ns + n], refs[ns + n + 1]
        x, y, c = _me()
        for a in range(n):
            seven = _blocks(mode, land_refs[a], axes[a], ARRIVALS[route])
            cp = pltpu.make_async_remote_copy(src_ref=seven, dst_ref=seven, send_sem=send_sems.at[first + a],
                                              recv_sem=recv_sems.at[first + a],
                                              device_id=(x, y, 1 - c), device_id_type=MESH)
            cp.wait_send()
            cp.wait_recv()

    hbm = lambda s: pltpu.HBM(s.shape, s.dtype)
    outs = pl.pallas_call(
        body, name=name, out_shape=tuple(hbm(s) for s in list(srcs) + list(lands)),
        in_specs=[HBM_SPEC] * (ns + n) + [SEM_SPEC, SEM_SPEC] + [ANY] * len(after), out_specs=tuple([HBM_SPEC] * (ns + n)),
        input_output_aliases={i: i for i in range(ns + n)},
        compiler_params=pltpu.CompilerParams(has_side_effects=EFFECT),
    )(*srcs, *lands, send_sems, recv_sems, *after)
    return outs[ns:]


def _sum_parts(parts):
    n, R, C = parts.shape

    def body(p_ref, o_ref):
        g = p_ref[0]
        for k in range(1, n):
            g = g + p_ref[k]
        o_ref[...] = g

    return pl.pallas_call(body, name="sum_parts", out_shape=_S((R, C)))(parts)


SMALL =['conv_b', 'rg_wa', 'rg_ba', 'rg_wx', 'rg_bx', 'rg_lambda', 's5_a_re', 's5_a_im', 's5_b_re', 's5_b_im',
         's5_c_re', 's5_c_im', 's5_d', 's5_log_step', 's5_b_glu', 'ln1_g', 'ln1_b', 'ple_gate_b', 'ln2_g', 'ln2_b']
WEIGHTS = ['w_in', 'conv_w', 'conv_b', 'rg_wa', 'rg_ba', 'rg_wx', 'rg_bx', 'rg_lambda', 's5_a_re', 's5_a_im', 's5_b_re',
           's5_b_im', 's5_c_re', 's5_c_im', 's5_d', 's5_log_step', 's5_w_glu', 's5_b_glu', 'w_out', 'ln1_g', 'ln1_b',
           'ple_w', 'ple_gate_w', 'ple_gate_b', 'ln2_g', 'ln2_b']
PACK_ROWS_MULT = 64


STORED = {'s5_b_re': (2, 3), 's5_b_im': (2, 3), 's5_d': (1, 2)}


def _stored(k, a):
    return jnp.swapaxes(a, *STORED[k]) if k in STORED else a


def _tile_rows(n):
    return -(-n // (SUB * LANE)) * SUB


def _pack(tree, scalar):
    parts = []
    for a in [tree[k] for k in SMALL] + [scalar.reshape(1)]:
        rows = _tile_rows(a.size)
        parts.append(jnp.pad(a.reshape(-1), (0, rows * LANE - a.size)).reshape(rows, LANE))
    rows = sum(p.shape[0] for p in parts)
    parts.append(jnp.zeros((-rows % PACK_ROWS_MULT, LANE), F32))
    return jnp.concatenate(parts, axis=0)


def _unpack(packed, like):
    out, r = {}, 0
    for k in SMALL:
        n = math.prod(like[k].shape)
        rows = _tile_rows(n)
        part = packed[r:r + rows]
        out[k] = (part if n == rows * LANE else part.reshape(-1)[:n]).reshape(like[k].shape)
        r += rows
    return out, packed[r, 0]


class _NoHooks:
    token = None
    first_token = None

    def first_weights(self, full, after):
        return full

    def layer_start(self, i, W, after):
        return W

    def late_weights(self, i, W, after):
        return W

    def post_done(self, i, g):
        return None

    def smalls_done(self, grads, loss):
        self.small = _small_grads(grads, self.res)
        return None

    def w_in_done(self, i, g):
        return None

    def layer_done(self, i, g, dx):
        return None


def _local_grads(x, p, target, W, disc, hooks):
    depth = 2
    saved = []
    for i in range(depth):
        if i > 0:
            W = hooks.layer_start(i, W, x)
        w = W[i]
        z = _inproj_fwd(x, w['w_in'], hooks.token if i == 0 else None)
        hs, *gates = _rg_fwd(z, w['conv_w'], w['conv_b'], w['wa_bd'], w['wx_bd'], w['rg_ba'], w['rg_bx'], w['rg_lambda'], i)
        d = disc[i]
        y0, s_re, s_im = _s5_fwd(z, d['bb_re'], d['bb_im'], d['lb_re'], d['lb_im'], d['c_re'], d['c_im'], w['s5_d'], i)
        W = hooks.late_weights(i, W, y0)
        w = W[i]
        x2, *norms = _post_fwd(x, hs, z, y0, p, w['s5_w_glu'], w['s5_b_glu'], w['w_out'], w['ln1_g'], w['ln1_b'],
                               w['ple_w'], w['ple_gate_w'], w['ple_gate_b'], w['ln2_g'], w['ln2_b'], i)
        saved.append((x, z, hs, gates, y0, s_re, s_im, norms))
        x = x2

    grads = [None] * depth
    dx = target
    loss = None
    token = None
    for i in reversed(range(depth)):
        w, d = W[i], disc[i]
        xin, z, hs, gates, y0, s_re, s_im, (xh1, xh2, gt, rstd1, rstd2) = saved[i]
        g = {}
        (dt1, g['ple_w'], g['ple_gate_w'], g['ple_gate_b'], g['ln1_g'], g['ln1_b'], g['ln2_g'], g['ln2_b'], lrow) = _post_bwd_a(
            dx, i == depth - 1, xh2, xh1, rstd2, rstd1, gt, p, w['ple_w'], w['ple_gate_w'], w['ln1_g'], w['ln1_b'],
            w['ln2_g'], w['ln2_b'], i, token)
        if i == depth - 1:
            loss = 0.5 / D_MODEL * jnp.sum(lrow)
        dhs, dy0, dzg, g['w_out'], g['s5_w_glu'], g['s5_b_glu'] = _post_bwd_b(dt1, z, hs, y0, w['w_out'], w['s5_w_glu'],
                                                                           w['s5_b_glu'], i)
        (dzu, g['bb_re'], g['bb_im'], g['lb_re'], g['lb_im'], g['c_re'], g['c_im'], g['s5_d']) = _s5_bwd(
            dy0, z, s_re, s_im, d['bb_re'], d['bb_im'], d['lb_re'], d['lb_im'], d['c_re'], d['c_im'], w['s5_d'], i,
            hooks.post_done(i, g))
        (dzx, g['conv_w'], g['conv_b'], g['wa_bd'], g['wx_bd'], g['rg_ba'], g['rg_bx'], g['rg_lambda']) = _rg_bwd(
            dhs, z, hs, gates, w['conv_w'], w['wa_bd'], w['wx_bd'], w['rg_lambda'], i)
        if i == 0:
            g['w_in'] = _inproj_bwd_dw(xin, dzx, dzg, dzu, hooks.smalls_done([g, grads[1]], loss))
            dx = _inproj_bwd_dx(dt1, dzx, dzg, dzu, w['w_in'], hooks.w_in_done(i, g))
        else:
            dx, g['w_in'] = _inproj_bwd(dt1, xin, dzx, dzg, dzu, w['w_in'])
        grads[i] = g
        token = hooks.layer_done(i, g, dx)
    return loss, dx, grads


def _s5_layouts_fwd(s5_a_re, s5_a_im, s5_log_step, s5_b_re, s5_b_im, s5_c_re, s5_c_im, token=None):
    depth = s5_a_re.shape[0]
    ar, ai = s5_a_re.reshape(depth * 24, S5_P), s5_a_im.reshape(depth * 24, S5_P)
    ls = s5_log_step.reshape(depth * 24, 1)
    lr, li, cr, ci = _s5_disc_fwd(ar, ai, ls, token)
    per_group = lambda a: a.reshape(depth * 24, 1, S5_P)
    as_c = lambda b: jnp.swapaxes(b, 2, 3).reshape(depth * 24, S5_H, S5_P)
    res = (ar, ai, ls, per_group(cr), per_group(ci), as_c(s5_b_re), as_c(s5_b_im))
    bbr, bbi = _s5_bscale_fwd(*res[3:])
    tiles = lambda a: a.reshape(depth * N_S5_T, S5_GT, S5_H, S5_P)
    rows = lambda a: a.reshape(depth * N_S5_T, S5_GT, S5_P)
    disc = dict(bb_re=tiles(bbr), bb_im=tiles(bbi), lb_re=rows(lr), lb_im=rows(li), c_re=tiles(s5_c_re), c_im=tiles(s5_c_im))
    return [disc] * depth, res


def _s5_layouts_bwd(grads, res):
    ar, ai, ls, cr, ci, br, bi = res
    depth = len(grads)
    stack = lambda k, shape: jnp.stack([g[k] for g in grads]).reshape(shape)
    groups, shape_c = (depth * 24, S5_H, S5_P), (depth, 24, S5_H, S5_P)
    dbr, dbi, dcr, dci = _s5_bscale_bwd(cr, ci, br, bi, stack('bb_re', groups), stack('bb_im', groups))
    gp = (depth * 24, S5_P)
    dar, dai, dls = _s5_disc_bwd(ar, ai, ls, stack('lb_re', gp), stack('lb_im', gp), dcr.reshape(gp), dci.reshape(gp))
    return dict(
        s5_a_re=dar.reshape(depth, 24, S5_P), s5_a_im=dai.reshape(depth, 24, S5_P), s5_log_step=dls.reshape(depth, 24),
        s5_b_re=dbr.reshape(shape_c), s5_b_im=dbi.reshape(shape_c),
        s5_c_re=stack('c_re', shape_c), s5_c_im=stack('c_im', shape_c))


LATE = ('w_out', 'ple_w', 'ple_gate_w', 's5_w_glu')


ROWS = ('conv_b', 'rg_ba', 'rg_bx', 'rg_lambda', 's5_d', 's5_b_glu', 'ln1_g', 'ln1_b', 'ple_gate_b', 'ln2_g', 'ln2_b')


def _shared_weights(full):
    shared = {k: full[k] for k in ROWS}
    shared.update(conv_w=full['conv_w'], wa_bd=full['rg_wa'], wx_bd=full['rg_wx'], s5_d=full['s5_d'].reshape(DEPTH, 1, S5_W))
    return shared


def _layer_weights(full, shared, i):
    return dict(shared, w_in=full['w_in'][i])


class _AllLocal(_NoHooks):
    def __init__(self, full):
        self.full = full

    def late_weights(self, i, W, after):
        W[i].update({k: self.full[k][i] for k in LATE})
        return W


def _full_grads(full, x, p, target, hooks=None):
    hooks = hooks or _AllLocal(full)
    disc, res = _s5_layouts_fwd(full['s5_a_re'], full['s5_a_im'], full['s5_log_step'], full['s5_b_re'], full['s5_b_im'],
                                full['s5_c_re'], full['s5_c_im'], hooks.first_token)
    full = hooks.first_weights(full, disc[-1]['bb_im'])
    shared = _shared_weights(full)
    W = [_layer_weights(full, shared, i) for i in range(2)]
    hooks.res = res
    loss, gx, grads = _local_grads(x, p, target, W, disc, hooks)
    out = dict(hooks.small)
    for k in SHARD_AXIS:
        out[k] = [g[k] for g in grads]
    return loss, gx, out


def _small_grads(grads, res):
    stack = lambda f: jnp.stack([f(g) for g in grads])
    out = _s5_layouts_bwd(grads, res)
    out['conv_w'] = stack(lambda g: g['conv_w'])
    for k in ('conv_b', 'rg_ba', 'rg_bx', 'rg_lambda', 's5_b_glu', 'ln1_g', 'ln1_b', 'ple_gate_b', 'ln2_g', 'ln2_b'):
        out[k] = stack(lambda g: g[k][0])
    out['s5_d'] = _stored('s5_d', stack(lambda g: g['s5_d'][0]).reshape(2, 24, 16))
    out['rg_wa'] = stack(lambda g: g['wa_bd'])
    out['rg_wx'] = stack(lambda g: g['wx_bd'])
    return out


SHARD_AXIS = {'w_in': 2, 'w_out': 1, 'ple_w': 2, 'ple_gate_w': 1, 's5_w_glu': 1}


def kernel(x, p, w_in, conv_w, conv_b, rg_wa, rg_ba, rg_wx, rg_bx, rg_lambda, s5_a_re, s5_a_im, s5_b_re, s5_b_im, s5_c_re, s5_c_im, s5_d, s5_log_step, s5_w_glu, s5_b_glu, w_out, ln1_g, ln1_b, ple_w, ple_gate_w, ple_gate_b, ln2_g, ln2_b, loss_target, m_w_in, m_conv_w, m_conv_b, m_rg_wa, m_rg_ba, m_rg_wx, m_rg_bx, m_rg_lambda, m_s5_a_re, m_s5_a_im, m_s5_b_re, m_s5_b_im, m_s5_c_re, m_s5_c_im, m_s5_d, m_s5_log_step, m_s5_w_glu, m_s5_b_glu, m_w_out, m_ln1_g, m_ln1_b, m_ple_w, m_ple_gate_w, m_ple_gate_b, m_ln2_g, m_ln2_b, v_w_in, v_conv_w, v_conv_b, v_rg_wa, v_rg_ba, v_rg_wx, v_rg_bx, v_rg_lambda, v_s5_a_re, v_s5_a_im, v_s5_b_re, v_s5_b_im, v_s5_c_re, v_s5_c_im, v_s5_d, v_s5_log_step, v_s5_w_glu, v_s5_b_glu, v_w_out, v_ln1_g, v_ln1_b, v_ple_w, v_ple_gate_w, v_ple_gate_b, v_ln2_g, v_ln2_b):
    local = dict(locals())
    w = {k: local[k] for k in WEIGHTS}
    mom = {k: local['m_' + k] for k in WEIGHTS}
    var = {k: local['v_' + k] for k in WEIGHTS}

    big = list(SHARD_AXIS)
    late_axes = [SHARD_AXIS[k] - 1 for k in LATE]
    pushed = {}

    groups = dict(first=(['w_in', 'conv_w'], [0, None], [1, 0]), l0=(list(LATE), [0] * len(LATE), late_axes),
                  l1=(['w_in'] + list(LATE), [1] * (1 + len(LATE)), [1] + late_axes))
    token = None
    for key, members in (("first", ["first"]), ("rest", ["l0", "l1"])):
        names, layers, axes = (sum((groups[m][j] for m in members), []) for j in range(3))
        shards = [w[k] if layer is not None else w[k][None] for k, layer in zip(names, layers)]
        lands = _place_shards(shards, layers, axes, [WIRE if k in big else w[k].dtype for k in names],
                              "place_weights_" + key, token)
        pushed[key] = _push_start("gather", [], lands, axes, "push_weights_" + key, "near" if key == "first" else None)
        token = pushed[key][4]

    def await_weights(key, axes, after):
        s, first = (pushed["first"], 0) if key == "first" else (pushed["rest"], 0 if key == "l0" else len(LATE))
        return _push_wait("gather", s[0], s[1], [], s[3][first:first + len(axes)], axes, after, "await_weights_" + key, first)

    def push_grads(key, g, names, axes):
        srcs = [g[k] for k in names]
        pushed[key] = _push_start("scatter", srcs, _place_own("scatter", srcs, axes, "place_grads_" + key), axes,
                                  "push_grads_" + key)
        return pushed[key][4]

    def await_grads(key, axes, after):
        s = pushed[key]
        return _push_wait("scatter", s[0], s[1], s[2], s[3], axes, after, "await_grads_" + key)

    class Overlap(_NoHooks):
        token = pushed["rest"][4]
        first_token = token

        def first_weights(self, full, after):
            s, axes = pushed["first"], [1, 0]
            near = _push_wait("gather", s[0], s[1], [], s[3], axes, after, "await_weights_near", route="near")
            s = _push_start("gather", [], near, axes, "relay_weights", "relay")
            w_in0, conv = _push_wait("gather", s[0], s[1], [], s[3], axes, s[4], "await_weights_relay", route="relay")
            return dict(full, w_in=[w_in0, None], conv_w=jnp.moveaxis(conv, 0, 2).reshape(2, 4, RG_W))

        def late_weights(self, i, W, after):
            if i == 0:
                W[0].update(zip(LATE, await_weights("l0", late_axes, after)))
            return W

        def layer_start(self, i, W, after):
            lands = await_weights("l1", [1] + late_axes, after)
            W[1].update(zip(LATE, lands[1:]), w_in=lands[0])
            return W

        def post_done(self, i, g):
            return push_grads("late0", g, LATE, late_axes) if i == 0 else None

        def smalls_done(self, grads, loss):
            super().smalls_done(grads, loss)
            conv = jnp.moveaxis(self.small['conv_w'].reshape(2, 4, N_DEV, RG_W // N_DEV), 2, 0)
            self.packed = _pack(self.small, loss)
            return push_grads("small", dict(conv_w=conv.reshape(N_DEV, 8, RG_W // N_DEV), small=self.packed),
                              ['conv_w', 'small'], [0, 0])

        def w_in_done(self, i, g):
            return push_grads("w_in0", g, ['w_in'], [0])

        def layer_done(self, i, g, dx):
            return push_grads("all1", g, ['w_in'] + list(LATE), [0] + late_axes) if i == 1 else None

    hooks = Overlap()
    _, grad_x, g = _full_grads(dict(w), x[0], p, loss_target[0], hooks)

    recv1 = dict(zip(['w_in'] + list(LATE), await_grads("all1", [0] + late_axes, grad_x)))
    recv0 = dict(zip(LATE, await_grads("late0", late_axes, grad_x)))
    outs = {}

    def update(k, parts, token=None):
        shard = w[k].shape
        c = shard[-1]
        two = lambda a: a.reshape(-1, c)
        res = _adamw([r.reshape(N_DEV, -1, c) for r in parts], two(w[k]), two(mom[k]), two(var[k]), token)
        outs[k] = [o.reshape(shard) for o in res]

    conv_parts, small_parts = await_grads("small", [0, 0], grad_x)
    rows = hooks.packed.shape[0] // N_DEV
    mine = _sum_parts(small_parts.reshape(N_DEV, rows, LANE))
    sums = _push_start("gather", [mine], _place_own("gather", [mine], [0], "place_small_sums"), [0], "push_small_sums")
    late = _adamw_sharded(list(LATE), [recv0, recv1], w, mom, var, "adamw_late", sums[4])
    for k in LATE:
        outs[k] = [t[k] for t in late]
    w_in0, = await_grads("w_in0", [0], [outs[k][1] for k in LATE])
    update('w_in', [w_in0, recv1['w_in']])
    update('conv_w', [conv_parts])
    gathered, = _push_wait("gather", sums[0], sums[1], sums[2], sums[3], [0], [outs['w_in'][1], outs['conv_w'][1]],
                           "await_small_sums")
    stored = [{k: _stored(k, t[k]) for k in SMALL} for t in (w, mom, var)]
    summed, loss = _unpack(gathered, stored[0])
    delta, new_m, new_v = _adamw_natural(SMALL, summed, *stored, "adamw_small")
    for k in SMALL:
        outs[k] = [_stored(k, o[k]) for o in (summed, delta, new_m, new_v)]

    res = [loss, grad_x[None]]
    for j in range(4):
        res += [outs[k][j] for k in WEIGHTS]
    return tuple(res)
```

```python
import math

import jax
import jax.numpy as jnp
from jax import lax
from jax.experimental import pallas as pl
from jax.experimental.pallas import tpu as pltpu

F32 = jnp.float32
MXU = jnp.bfloat16
WIRE = jnp.bfloat16

N_DEV = 8
D_MODEL = 1024
PLE_D = 256
RG_W = 640
S5_W = 384
S5_P = 64
S5_N = 24 * S5_P
Z_W = 2 * RG_W + 2 * S5_W
C_RGG = RG_W
C_S5U = 2 * RG_W
C_S5G = 2 * RG_W + S5_W
LANE = 128
N_RG_T = RG_W // LANE
N_S5_T = S5_W // LANE
W_BLK = Z_W // N_DEV
ALPHA = (2.0 * 2) ** 0.25
LN_EPS = 1e-5
RG_C = 8.0
LR, B1, B2, EPS, WD, STEP = 0.001, 0.9, 0.999, 1e-08, 0.01, 10
BC1 = 1.0 - B1 ** STEP
BC2 = 1.0 - B2 ** STEP
RC = 512
RC_RG = 1024
TM = 512
TM_MM = 1024
VMEM_LIMIT = 56 * 1024 * 1024

MESH = pl.DeviceIdType.MESH
ANY = pl.BlockSpec(memory_space=pl.ANY)


def _params(n_grid_axes, vmem=VMEM_LIMIT):
    return pltpu.CompilerParams(dimension_semantics=("arbitrary",) * n_grid_axes, vmem_limit_bytes=vmem)


def _S(shape, dtype=F32):
    return jax.ShapeDtypeStruct(tuple(shape), dtype)


def _sigmoid(x):
    return 0.5 * jnp.tanh(0.5 * x) + 0.5


def _silu_and_grad(x):
    s = _sigmoid(x)
    return x * s, s * (1.0 + x * (1.0 - s))


_GELU_C = math.sqrt(2.0 / math.pi)


def _gelu(x):
    return 0.5 * x * (1.0 + jnp.tanh(_GELU_C * (x + 0.044715 * (x * x * x))))


def _gelu_grad(x):
    th = jnp.tanh(_GELU_C * (x + 0.044715 * (x * x * x)))
    return 0.5 * (1.0 + th) + 0.5 * x * (1.0 - th * th) * (_GELU_C * (1.0 + 3.0 * 0.044715 * (x * x)))


def _mm(a, b):
    return jnp.dot(a.astype(MXU), b.astype(MXU), preferred_element_type=F32)


def _mm_nt(a, b):
    return lax.dot_general(a.astype(MXU), b.astype(MXU), (((1,), (1,)), ((), ())), preferred_element_type=F32)


def _mm_tn(a, b):
    return lax.dot_general(a.astype(MXU), b.astype(MXU), (((0,), (0,)), ((), ())), preferred_element_type=F32)


def _ln_fwd(t, g, b):
    mu = jnp.mean(t, axis=-1, keepdims=True)
    tc = t - mu
    var = jnp.mean(tc * tc, axis=-1, keepdims=True)
    rstd = lax.rsqrt(var + LN_EPS)
    xhat = tc * rstd
    return xhat * g + b, xhat, rstd


def _ln_bwd(dy, xhat, rstd, g):
    dxh = dy * g
    m1 = jnp.mean(dxh, axis=-1, keepdims=True)
    m2 = jnp.mean(dxh * xhat, axis=-1, keepdims=True)
    return rstd * (dxh - m1 - xhat * m2)


def _colsum(a):
    return jnp.sum(a, axis=0, keepdims=True)


def _up(x, d, rows, fill):
    n = x.shape[0]
    return jnp.where(rows < n - d, pltpu.roll(x, n - d, 0), fill)


SUB = 8
TILE_STEPS = (1, 2, 4)


def _r8(width):
    return lax.broadcasted_iota(jnp.int32, (SUB, width), 0)


def _scan_real(a, u, carry, reverse=False):
    r8 = _r8(a.shape[1])
    n = a.shape[0] // SUB
    outs = [None] * n
    for k in (reversed(range(n)) if reverse else range(n)):
        A, U = a[SUB * k:SUB * k + SUB], u[SUB * k:SUB * k + SUB]
        for d in TILE_STEPS:
            m = (r8 < SUB - d) if reverse else (r8 >= d)
            sh = SUB - d if reverse else d
            U = A * jnp.where(m, pltpu.roll(U, sh, 0), 0.0) + U
            A = A * jnp.where(m, pltpu.roll(A, sh, 0), 1.0)
        h = A * carry + U
        outs[k] = h
        carry = h[0:1] if reverse else h[SUB - 1:SUB]
    return jnp.concatenate(outs, axis=0), carry


def _tile_powers(lr, li, reverse=False):
    width = lr.shape[1]
    r8 = _r8(width)
    steps = []
    pr, pi = lr, li
    er, ei = jnp.broadcast_to(lr, (SUB, width)), jnp.broadcast_to(li, (SUB, width))
    for d in TILE_STEPS:
        m = (r8 < SUB - d) if reverse else (r8 >= d)
        sh = SUB - d if reverse else d
        steps.append((sh, jnp.where(m, pr, 0.0), jnp.where(m, pi, 0.0)))
        er, ei = _cmul(er, ei, jnp.where(m, pltpu.roll(er, sh, 0), 1.0), jnp.where(m, pltpu.roll(ei, sh, 0), 0.0))
        pr, pi = _cmul(pr, pi, pr, pi)
    return steps, (er, ei)


def _scan_lti(xr, xi, carry, steps, e, reverse=False):
    er, ei = e
    kr, ki = carry
    n = xr.shape[0] // SUB
    outr, outi = [None] * n, [None] * n
    for k in (reversed(range(n)) if reverse else range(n)):
        sr, si = xr[SUB * k:SUB * k + SUB], xi[SUB * k:SUB * k + SUB]
        for sh, pr, pi in steps:
            shr, shi = pltpu.roll(sr, sh, 0), pltpu.roll(si, sh, 0)
            sr, si = sr + (pr * shr - pi * shi), si + (pr * shi + pi * shr)
        sr = sr + (er * kr - ei * ki)
        si = si + (er * ki + ei * kr)
        outr[k], outi[k] = sr, si
        kr, ki = (sr[0:1], si[0:1]) if reverse else (sr[SUB - 1:SUB], si[SUB - 1:SUB])
    return jnp.concatenate(outr, axis=0), jnp.concatenate(outi, axis=0), (kr, ki)


def _halo(ref, c, r0):
    rp = pl.multiple_of(jnp.maximum(r0 - 8, 0), 8)
    return jnp.where(c > 0, ref[pl.ds(rp, 8), :], 0.0)


def _conv_taps(xe):
    return [pltpu.roll(xe, 3, 0)[8:, :], pltpu.roll(xe, 2, 0)[8:, :], pltpu.roll(xe, 1, 0)[8:, :], xe[8:, :]]


def _rg_gates(h, wa, wx, ba, bx, sp):
    r = _sigmoid(_mm(h, wa) + ba)
    i = _sigmoid(_mm(h, wx) + bx)
    log_a = (-RG_C) * r * sp
    a = jnp.exp(log_a)
    mult = jnp.sqrt(-jnp.tanh(log_a) * (a * a + 1.0))
    return r, i, a, mult


def _softplus(y):
    return jnp.maximum(y, 0.0) + jnp.log1p(jnp.exp(-jnp.abs(y)))


def _after(token):
    return ([], []) if token is None else ([token], [ANY])


def _inproj_fwd(x, w_in, token=None):
    L = x.shape[0]

    def body(x_ref, w_ref, *rest):
        rest[-1][...] = _mm(x_ref[...], w_ref[...])

    extra, extra_specs = _after(token)
    tm = min(TM_MM, L)
    return pl.pallas_call(
        body, name="inproj_fwd", grid=(L // tm,),
        in_specs=[pl.BlockSpec((tm, D_MODEL), lambda i: (i, 0)), pl.BlockSpec((D_MODEL, Z_W), lambda i: (0, 0))] + extra_specs,
        out_specs=pl.BlockSpec((tm, Z_W), lambda i: (i, 0)),
        out_shape=_S((L, Z_W)), compiler_params=_params(1))(x, w_in, *extra)


def _inproj_bwd(dt1, x, dzx, dzg, dzu, w_in):
    L = x.shape[0]

    def body(dt1_ref, x_ref, dzx_ref, dzg_ref, dzu_ref, w_ref, dx_ref, dw_ref, acc_ref):
        @pl.when(pl.program_id(0) == 0)
        def _():
            acc_ref[...] = jnp.zeros_like(acc_ref)
        dzg = dzg_ref[...]
        dz = jnp.concatenate([dzx_ref[...], dzg[:, :RG_W], dzu_ref[...], dzg[:, RG_W:]], axis=1).astype(MXU)
        xb = x_ref[...].astype(MXU)
        dx_ref[...] = ALPHA * dt1_ref[...] + _mm_nt(dz, w_ref[...])
        for j in range(N_DEV):
            acc_ref[j] += _mm_tn(xb, dz[:, j * W_BLK:(j + 1) * W_BLK])

        @pl.when(pl.program_id(0) == L // TM - 1)
        def _():
            dw_ref[...] = acc_ref[...].astype(WIRE)

    row = lambda w: pl.BlockSpec((TM, w), lambda i: (i, 0))
    wspec = pl.BlockSpec((N_DEV, D_MODEL, W_BLK), lambda i: (0, 0, 0))
    return pl.pallas_call(
        body, name="inproj_bwd", grid=(L // TM,),
        in_specs=[row(D_MODEL), row(D_MODEL), row(RG_W), row(D_MODEL), row(S5_W),
                  pl.BlockSpec((D_MODEL, Z_W), lambda i: (0, 0))],
        out_specs=[row(D_MODEL), wspec],
        out_shape=[_S((L, D_MODEL)), _S((N_DEV, D_MODEL, W_BLK), WIRE)],
        scratch_shapes=[pltpu.VMEM((N_DEV, D_MODEL, W_BLK), F32)],
        compiler_params=_params(1))(dt1, x, dzx, dzg, dzu, w_in)


TM2 = 1024


def _dz_block(dzx_ref, dzg_ref, dzu_ref):
    dzg = dzg_ref[...]
    return jnp.concatenate([dzx_ref[...], dzg[:, :RG_W], dzu_ref[...], dzg[:, RG_W:]], axis=1).astype(MXU)


def _inproj_bwd_dw(x, dzx, dzg, dzu, token=None):
    L = x.shape[0]
    extra, extra_specs = _after(token)

    def body(x_ref, dzx_ref, dzg_ref, dzu_ref, *rest):
        dw_ref, acc_ref = rest[len(extra):]
        @pl.when(pl.program_id(0) == 0)
        def _():
            acc_ref[...] = jnp.zeros_like(acc_ref)
        dz = _dz_block(dzx_ref, dzg_ref, dzu_ref)
        xb = x_ref[...].astype(MXU)
        for j in range(N_DEV):
            acc_ref[j] += _mm_tn(xb, dz[:, j * W_BLK:(j + 1) * W_BLK])

        @pl.when(pl.program_id(0) == L // TM2 - 1)
        def _():
            dw_ref[...] = acc_ref[...].astype(WIRE)

    row = lambda w: pl.BlockSpec((TM2, w), lambda i: (i, 0))
    wspec = pl.BlockSpec((N_DEV, D_MODEL, W_BLK), lambda i: (0, 0, 0))
    return pl.pallas_call(
        body, name="inproj_bwd_dw", grid=(L // TM2,),
        in_specs=[row(D_MODEL), row(RG_W), row(D_MODEL), row(S5_W)] + extra_specs, out_specs=wspec,
        out_shape=_S((N_DEV, D_MODEL, W_BLK), WIRE), scratch_shapes=[pltpu.VMEM((N_DEV, D_MODEL, W_BLK), F32)],
        compiler_params=_params(1))(x, dzx, dzg, dzu, *extra)


def _inproj_bwd_dx(dt1, dzx, dzg, dzu, w_in, token=None):
    L = dt1.shape[0]
    extra, extra_specs = _after(token)

    def body(dt1_ref, dzx_ref, dzg_ref, dzu_ref, w_ref, *rest):
        rest[-1][...] = ALPHA * dt1_ref[...] + _mm_nt(_dz_block(dzx_ref, dzg_ref, dzu_ref), w_ref[...])

    tm = min(TM_MM, L)
    row = lambda w: pl.BlockSpec((tm, w), lambda i: (i, 0))
    return pl.pallas_call(
        body, name="inproj_bwd_dx", grid=(L // tm,),
        in_specs=[row(D_MODEL), row(RG_W), row(D_MODEL), row(S5_W), _full((D_MODEL, Z_W))] + extra_specs,
        out_specs=row(D_MODEL), out_shape=_S((L, D_MODEL)), compiler_params=_params(1))(dt1, dzx, dzg, dzu, w_in, *extra)


def _rg_specs(layer):
    tile = lambda rows: pl.BlockSpec((rows, LANE), lambda c: (0, c))
    ptile = lambda rows: pl.BlockSpec((None, rows, LANE), lambda c: (layer, 0, c))
    pheads = pl.BlockSpec((None, 2, RG_HD, RG_HD), lambda c: (layer, c, 0, 0))
    return tile, ptile, pheads, pl.BlockSpec((2, RG_HD, RG_HD), lambda c: (c, 0, 0))


RG_HD = 64


def _bd2(w):
    z = jnp.zeros((RG_HD, RG_HD), w.dtype)
    return jnp.concatenate([jnp.concatenate([w[0], z], axis=1), jnp.concatenate([z, w[1]], axis=1)], axis=0)


def _bd2_diag(m):
    return jnp.stack([m[:RG_HD, :RG_HD], m[RG_HD:, RG_HD:]])


def _rg_fwd(z, cw, cb, wa_bd, wx_bd, ba, bx, lam, layer):
    L = z.shape[0]
    RC = min(RC_RG, L)

    def body(x_ref, cw_ref, cb_ref, wa_ref, wx_ref, ba_ref, bx_ref, lam_ref, hs_ref, *saved):
        row = slice(layer, layer + 1)
        w, b = cw_ref[...], cb_ref[row, :]
        wa, wx, ba_, bx_ = _bd2(wa_ref[...]).astype(MXU), _bd2(wx_ref[...]).astype(MXU), ba_ref[row, :], bx_ref[row, :]
        sp = _softplus(-lam_ref[row, :])

        def step(c, carry):
            r0 = pl.multiple_of(c * RC, RC)
            xe = jnp.concatenate([_halo(x_ref, c, r0), x_ref[pl.ds(r0, RC), :]], axis=0)
            t = _conv_taps(xe)
            h = t[0] * w[0:1] + t[1] * w[1:2] + t[2] * w[2:3] + t[3] * w[3:4] + b
            r, i, a, mult = _rg_gates(h, wa, wx, ba_, bx_, sp)
            hs, carry = _scan_real(a, mult * (i * h), carry)
            hs_ref[pl.ds(r0, RC), :] = hs
            for ref, val in zip(saved, (h, r, i, a, mult)):
                ref[pl.ds(r0, RC), :] = val
            return carry

        lax.fori_loop(0, L // RC, step, jnp.zeros((1, LANE), F32))

    tile, ptile, pheads, _ = _rg_specs(layer)
    return pl.pallas_call(
        body, name="rg_fwd", grid=(N_RG_T,),
        in_specs=[tile(L), ptile(4), tile(2), pheads, pheads, tile(2), tile(2), tile(2)],
        out_specs=[tile(L)] * 6, out_shape=[_S((L, RG_W))] * 6, compiler_params=_params(1))(
            z, cw, cb, wa_bd, wx_bd, ba, bx, lam)


def _rg_bwd(dhs, z, hs, gates, cw, wa_bd, wx_bd, lam, layer):
    L = z.shape[0]
    RC = min(RC_RG, L)

    def body(g_ref, x_ref, hs_ref, h_ref, r_ref, i_ref, a_ref, mult_ref, cw_ref, wa_ref, wx_ref, lam_ref,
             dx_ref, dcw_ref, dcb_ref, dwa_out, dwx_out, dba_ref, dbx_ref, dlam_ref, dwa_ref, dwx_ref):
        w = cw_ref[...]
        wa, wx = _bd2(wa_ref[...]).astype(MXU), _bd2(wx_ref[...]).astype(MXU)
        lam = lam_ref[layer:layer + 1, :]
        sp = _softplus(-lam)
        rows = lax.broadcasted_iota(jnp.int32, (RC, LANE), 0)
        for ref in (dcw_ref, dcb_ref, dwa_ref, dwx_ref, dba_ref, dbx_ref, dlam_ref):
            ref[...] = jnp.zeros_like(ref)
        nch = L // RC

        def step(k, carry):
            cin, nxt = carry
            c = nch - 1 - k
            r0 = pl.multiple_of(c * RC, RC)
            xe = jnp.concatenate([_halo(x_ref, c, r0), x_ref[pl.ds(r0, RC), :]], axis=0)
            t = _conv_taps(xe)
            h, r, i, a, mult = (ref[pl.ds(r0, RC), :] for ref in (h_ref, r_ref, i_ref, a_ref, mult_ref))
            hs_e = jnp.concatenate([_halo(hs_ref, c, r0), hs_ref[pl.ds(r0, RC), :]], axis=0)
            hs_prev = pltpu.roll(hs_e, 1, 0)[8:, :]
            g = g_ref[pl.ds(r0, RC), :]
            cc, cin_new = _scan_real(a, a * g, cin, reverse=True)
            dh = g + _up(cc, 1, rows, cin)
            ih = i * h
            dlog_a = dh * hs_prev * a - (dh * ih) * (a * a) / mult
            di = dh * mult * h
            dhin = dh * mult * i
            dr = dlog_a * ((-RG_C) * sp)
            dlam_ref[...] += _colsum(dlog_a * r)
            dra = dr * r * (1.0 - r)
            dia = di * i * (1.0 - i)
            dwa_ref[...] += _mm_tn(h, dra)
            dwx_ref[...] += _mm_tn(h, dia)
            dba_ref[...] += _colsum(dra)
            dbx_ref[...] += _colsum(dia)
            dhin = dhin + _mm_nt(dra, wa) + _mm_nt(dia, wx)
            de = jnp.concatenate([dhin, nxt], axis=0)
            n = RC + 8
            dx = (dhin * w[3:4] + pltpu.roll(de, n - 1, 0)[:RC, :] * w[2:3]
                  + pltpu.roll(de, n - 2, 0)[:RC, :] * w[1:2] + pltpu.roll(de, n - 3, 0)[:RC, :] * w[0:1])
            dx_ref[pl.ds(r0, RC), :] = dx
            for kk in range(4):
                dcw_ref[kk:kk + 1, :] += _colsum(dhin * t[kk])
            dcb_ref[...] += _colsum(dhin)
            return cin_new, dhin[0:8, :]

        lax.fori_loop(0, nch, step, (jnp.zeros((1, LANE), F32), jnp.zeros((8, LANE), F32)))
        dlam_ref[...] = dlam_ref[...] * (RG_C * _sigmoid(-lam))
        dwa_out[...], dwx_out[...] = _bd2_diag(dwa_ref[...]), _bd2_diag(dwx_ref[...])

    tile, ptile, pheads, gheads = _rg_specs(layer)
    heads = _S((2 * N_RG_T, RG_HD, RG_HD))
    return pl.pallas_call(
        body, name="rg_bwd", grid=(N_RG_T,),
        in_specs=[tile(L)] * 8 + [ptile(4), pheads, pheads, tile(2)],
        out_specs=[tile(L), tile(4), tile(1), gheads, gheads, tile(1), tile(1), tile(1)],
        out_shape=[_S((L, RG_W)), _S((4, RG_W)), _S((1, RG_W)), heads, heads, _S((1, RG_W)), _S((1, RG_W)), _S((1, RG_W))],
        scratch_shapes=[pltpu.VMEM((LANE, LANE), F32), pltpu.VMEM((LANE, LANE), F32)],
        compiler_params=_params(1))(dhs, z, hs, *gates, cw, wa_bd, wx_bd, lam)


def _cmul(ar, ai, br, bi):
    return ar * br - ai * bi, ar * bi + ai * br


S5_TW = S5_N // N_S5_T


S5_H = 16
S5_GT = LANE // S5_H


def _s5_specs(L, layer):
    in_tile = pl.BlockSpec((L, LANE), lambda t: (0, t))
    st = pl.BlockSpec((L, S5_TW), lambda t: (0, t))
    pg = pl.BlockSpec((None, S5_GT, S5_H, S5_P), lambda t: (layer * N_S5_T + t, 0, 0, 0))
    plb = pl.BlockSpec((None, S5_GT, S5_P), lambda t: (layer * N_S5_T + t, 0, 0))
    gg = pl.BlockSpec((None, S5_GT, S5_H, S5_P), lambda t: (t, 0, 0, 0))
    glb = pl.BlockSpec((None, S5_GT, S5_P), lambda t: (t, 0, 0))
    dv = pl.BlockSpec((1, LANE), lambda t: (0, t))
    return in_tile, st, pg, plb, gg, glb, dv


def _bd8(blocks):
    rows = []
    for g in range(S5_GT):
        pieces = [blocks[g]]
        if g:
            pieces.insert(0, jnp.zeros((S5_H, S5_P * g), blocks.dtype))
        if g < S5_GT - 1:
            pieces.append(jnp.zeros((S5_H, S5_P * (S5_GT - 1 - g)), blocks.dtype))
        rows.append(jnp.concatenate(pieces, axis=1))
    return jnp.concatenate(rows, axis=0)


def _bd8_diag(m):
    return jnp.stack([m[S5_H * g:S5_H * (g + 1), S5_P * g:S5_P * (g + 1)] for g in range(S5_GT)])


def _row8(v):
    return jnp.concatenate([v[g:g + 1] for g in range(S5_GT)], axis=1)


def _row8_split(r):
    return jnp.concatenate([r[:, S5_P * g:S5_P * (g + 1)] for g in range(S5_GT)], axis=0)


def _layer_row_tile(layer):
    return pl.BlockSpec((None, 1, LANE), lambda t: (layer, 0, t))


def _s5_fwd(z, bb_re, bb_im, lb_re, lb_im, c_re, c_im, dvec, layer):
    L = z.shape[0]

    def body(u_ref, bbr_ref, bbi_ref, lr_ref, li_ref, cr_ref, ci_ref, d_ref, y_ref, sr_ref, si_ref):
        bbr, bbi = _bd8(bbr_ref[...]).astype(MXU), _bd8(bbi_ref[...]).astype(MXU)
        cr, ci = _bd8(cr_ref[...]).astype(MXU), _bd8(ci_ref[...]).astype(MXU)
        dv = d_ref[...]
        steps, e = _tile_powers(_row8(lr_ref[...]), _row8(li_ref[...]))

        def step(c, carry):
            r0 = pl.multiple_of(c * RC, RC)
            u = u_ref[pl.ds(r0, RC), :]
            ub = u.astype(MXU)
            sr = jnp.dot(ub, bbr, preferred_element_type=F32)
            si = jnp.dot(ub, bbi, preferred_element_type=F32)
            sr, si, carry = _scan_lti(sr, si, carry, steps, e)
            sr_ref[pl.ds(r0, RC), :] = sr
            si_ref[pl.ds(r0, RC), :] = si
            y_ref[pl.ds(r0, RC), :] = dv * u + (_mm_nt(sr, cr) - _mm_nt(si, ci))
            return carry

        zero = jnp.zeros((1, S5_TW), F32)
        lax.fori_loop(0, L // RC, step, (zero, zero))

    in_tile, st, pg, plb, _, _, _ = _s5_specs(L, layer)
    u_tile = pl.BlockSpec((L, LANE), lambda t: (0, C_S5U // LANE + t))
    return pl.pallas_call(
        body, name="s5_fwd", grid=(N_S5_T,),
        in_specs=[u_tile, pg, pg, plb, plb, pg, pg, _layer_row_tile(layer)],
        out_specs=[in_tile, st, st],
        out_shape=[_S((L, S5_W)), _S((L, S5_N)), _S((L, S5_N))],
        compiler_params=_params(1))(z, bb_re, bb_im, lb_re, lb_im, c_re, c_im, dvec)


def _s5_bwd(dy0, z, s_re, s_im, bb_re, bb_im, lb_re, lb_im, c_re, c_im, dvec, layer, token=None):
    L = z.shape[0]
    extra, extra_specs = _after(token)

    def body(dy_ref, u_ref, sr_ref, si_ref, bbr_ref, bbi_ref, lr_ref, li_ref, cr_ref, ci_ref, d_ref, *rest):
        (du_ref, dbbr_out, dbbi_out, dlr_out, dli_out, dcr_out, dci_out, dd_ref,
         dbbr_ref, dbbi_ref, dcr_ref, dci_ref, dlr_ref, dli_ref) = rest[len(extra):]
        bbr, bbi = _bd8(bbr_ref[...]).astype(MXU), _bd8(bbi_ref[...]).astype(MXU)
        cr, ci = _bd8(cr_ref[...]).astype(MXU), _bd8(ci_ref[...]).astype(MXU)
        lr, li = _row8(lr_ref[...]), -_row8(li_ref[...])
        dv = d_ref[...]
        steps, e = _tile_powers(lr, li, reverse=True)
        for ref in (dbbr_ref, dbbi_ref, dlr_ref, dli_ref, dcr_ref, dci_ref, dd_ref):
            ref[...] = jnp.zeros_like(ref)
        nch = L // RC

        def step(k, carry):
            c = nch - 1 - k
            r0 = pl.multiple_of(c * RC, RC)
            dy = dy_ref[pl.ds(r0, RC), :]
            u = u_ref[pl.ds(r0, RC), :]
            dyb, ub = dy.astype(MXU), u.astype(MXU)
            sr, si = sr_ref[pl.ds(r0, RC), :], si_ref[pl.ds(r0, RC), :]
            dcr_ref[...] += _mm_tn(dyb, sr)
            dci_ref[...] -= _mm_tn(dyb, si)
            gr = jnp.dot(dyb, cr, preferred_element_type=F32)
            gi = -jnp.dot(dyb, ci, preferred_element_type=F32)
            gr, gi, carry = _scan_lti(gr, gi, carry, steps, e, reverse=True)
            pr_ = pltpu.roll(jnp.concatenate([_halo(sr_ref, c, r0), sr], axis=0), 1, 0)[8:, :]
            pi_ = pltpu.roll(jnp.concatenate([_halo(si_ref, c, r0), si], axis=0), 1, 0)[8:, :]
            dlr_ref[...] += _colsum(pr_ * gr + pi_ * gi)
            dli_ref[...] += _colsum(pr_ * gi - pi_ * gr)
            grb, gib = gr.astype(MXU), gi.astype(MXU)
            dbbr_ref[...] += _mm_tn(ub, grb)
            dbbi_ref[...] += _mm_tn(ub, gib)
            du_ref[pl.ds(r0, RC), :] = dv * dy + (_mm_nt(grb, bbr) + _mm_nt(gib, bbi))
            dd_ref[...] += _colsum(dy * u)
            return carry

        zero = jnp.zeros((1, S5_TW), F32)
        lax.fori_loop(0, nch, step, (zero, zero))
        dbbr_out[...], dbbi_out[...] = _bd8_diag(dbbr_ref[...]), _bd8_diag(dbbi_ref[...])
        dcr_out[...], dci_out[...] = _bd8_diag(dcr_ref[...]), _bd8_diag(dci_ref[...])
        dlr_out[...], dli_out[...] = _row8_split(dlr_ref[...]), _row8_split(dli_ref[...])

    in_tile, st, pg, plb, gg, glb, dv = _s5_specs(L, layer)
    u_tile = pl.BlockSpec((L, LANE), lambda t: (0, C_S5U // LANE + t))
    groups, rows = _S((N_S5_T, S5_GT, S5_H, S5_P)), _S((N_S5_T, S5_GT, S5_P))
    wide = pltpu.VMEM((LANE, S5_TW), F32)
    return pl.pallas_call(
        body, name="s5_bwd", grid=(N_S5_T,),
        in_specs=[in_tile, u_tile, st, st, pg, pg, plb, plb, pg, pg, _layer_row_tile(layer)] + extra_specs,
        out_specs=[in_tile, gg, gg, glb, glb, gg, gg, dv],
        out_shape=[_S((L, S5_W)), groups, groups, rows, rows, groups, groups, _S((1, S5_W))],
        scratch_shapes=[wide, wide, wide, wide, pltpu.VMEM((1, S5_TW), F32), pltpu.VMEM((1, S5_TW), F32)],
        compiler_params=_params(1))(dy0, z, s_re, s_im, bb_re, bb_im, lb_re, lb_im, c_re, c_im, dvec, *extra)


def _disc(ar, ai, ls):
    dt = jnp.exp(ls)
    mag = jnp.exp(ar * dt)
    lr = mag * jnp.cos(ai * dt)
    li = mag * jnp.sin(ai * dt)
    den = ar * ar + ai * ai
    cr = ((lr - 1.0) * ar + li * ai) / den
    ci = (li * ar - (lr - 1.0) * ai) / den
    return lr, li, cr, ci


def _s5_disc_fwd(ar, ai, ls, token=None):
    extra, extra_specs = _after(token)

    def body(ar_ref, ai_ref, ls_ref, *rest):
        lr_ref, li_ref, cr_ref, ci_ref = rest[len(extra):]
        lr, li, cr, ci = _disc(ar_ref[...], ai_ref[...], ls_ref[...])
        lr_ref[...], li_ref[...], cr_ref[...], ci_ref[...] = lr, li, cr, ci

    sh = _S(ar.shape)
    vm = pl.BlockSpec(memory_space=pltpu.VMEM)
    return pl.pallas_call(body, name="s5_disc_fwd", in_specs=[vm, vm, vm] + extra_specs, out_shape=[sh, sh, sh, sh])(
        ar, ai, ls, *extra)


def _s5_disc_bwd(ar, ai, ls, dlr, dli, dcr, dci):
    def body(ar_ref, ai_ref, ls_ref, dlr_ref, dli_ref, dcr_ref, dci_ref, dar_ref, dai_ref, dls_ref):
        _, vjp = jax.vjp(_disc, ar_ref[...], ai_ref[...], jnp.broadcast_to(ls_ref[...], ar_ref.shape))
        dar, dai, dls = vjp((dlr_ref[...], dli_ref[...], dcr_ref[...], dci_ref[...]))
        dar_ref[...], dai_ref[...] = dar, dai
        dls_ref[...] = jnp.sum(dls, axis=1, keepdims=True)

    return pl.pallas_call(body, name="s5_disc_bwd", out_shape=[_S(ar.shape), _S(ar.shape), _S(ls.shape)])(
        ar, ai, ls, dlr, dli, dcr, dci)


def _s5_bscale_fwd(cr, ci, br, bi):
    def body(cr_ref, ci_ref, br_ref, bi_ref, or_ref, oi_ref):
        or_ref[...], oi_ref[...] = _cmul(cr_ref[...], ci_ref[...], br_ref[...], bi_ref[...])

    return pl.pallas_call(body, name="s5_bscale_fwd", out_shape=[_S(br.shape), _S(br.shape)])(cr, ci, br, bi)


def _s5_bscale_bwd(cr, ci, br, bi, gr, gi):
    def body(cr_ref, ci_ref, br_ref, bi_ref, gr_ref, gi_ref, dbr_ref, dbi_ref, dcr_ref, dci_ref):
        cr_, ci_, br_, bi_, gr_, gi_ = (r[...] for r in (cr_ref, ci_ref, br_ref, bi_ref, gr_ref, gi_ref))
        dbr_ref[...] = cr_ * gr_ + ci_ * gi_
        dbi_ref[...] = cr_ * gi_ - ci_ * gr_
        dcr_ref[...] = jnp.sum(gr_ * br_ + gi_ * bi_, axis=1, keepdims=True)
        dci_ref[...] = jnp.sum(gi_ * br_ - gr_ * bi_, axis=1, keepdims=True)

    return pl.pallas_call(body, name="s5_bscale_bwd",
                          out_shape=[_S(br.shape), _S(br.shape), _S(cr.shape), _S(cr.shape)])(cr, ci, br, bi, gr, gi)


def _row(w):
    return pl.BlockSpec((TM, w), lambda i: (i, 0))


def _full(shape):
    return pl.BlockSpec(tuple(shape), lambda i: (0,) * len(shape))


def _gate_rows():
    return [pl.BlockSpec((TM, RG_W), lambda i: (i, C_RGG // RG_W))] + [
        pl.BlockSpec((TM, LANE), lambda i, k=k: (i, C_S5G // LANE + k)) for k in range(N_S5_T)]


def _p_rows(layer):
    return pl.BlockSpec((None, None, TM, PLE_D), lambda i: (layer, 0, i, 0))


DEPTH = 2


def _lrow(layer, width):
    return _full((DEPTH, width))


def _pick(ref, layer):
    return ref[layer:layer + 1, :]


def _post_fwd(x, hs, z, y0, p, w_glu, b_glu, w_out, g1, b1, ple_w, w_pg, b_pg, g2, b2, layer):
    L = x.shape[0]

    def body(x_ref, hs_ref, zg_ref, zs0_ref, zs1_ref, zs2_ref, y0_ref, p_ref, wg_ref, bg_ref, wo_ref, g1_ref, b1_ref, pw_ref,
             wpg_ref, bpg_ref, g2_ref, b2_ref, x2_ref, xh1_ref, xh2_ref, gt_ref, rstd1_ref, rstd2_ref):
        rg_gate = zg_ref[...]
        s5_gate = jnp.concatenate([zs0_ref[...], zs1_ref[...], zs2_ref[...]], axis=1)
        rg_y = hs_ref[...] * _silu_and_grad(rg_gate)[0]
        y1 = _gelu(y0_ref[...])
        gl = _sigmoid(_mm(y1, wg_ref[...]) + _pick(bg_ref, layer))
        s5_y = (y1 * gl) * _silu_and_grad(s5_gate)[0]
        mix = _mm(jnp.concatenate([rg_y.astype(MXU), s5_y.astype(MXU)], axis=1), wo_ref[...])
        t1 = ALPHA * x_ref[...] + mix
        x1, xh1, rstd1 = _ln_fwd(t1, _pick(g1_ref, layer), _pick(b1_ref, layer))
        q = _mm(p_ref[...], pw_ref[...])
        gt = _sigmoid(_mm(x1, wpg_ref[...]) + _pick(bpg_ref, layer))
        t2 = ALPHA * x1 + q * gt
        x2, xh2, rstd2 = _ln_fwd(t2, _pick(g2_ref, layer), _pick(b2_ref, layer))
        x2_ref[...], xh1_ref[...], xh2_ref[...], gt_ref[...] = x2, xh1, xh2, gt
        rstd1_ref[...], rstd2_ref[...] = rstd1, rstd2

    vec = _lrow(layer, D_MODEL)
    return pl.pallas_call(
        body, name="post_fwd", grid=(L // TM,),
        in_specs=[_row(D_MODEL), _row(RG_W), *_gate_rows(), _row(S5_W), _p_rows(layer), _full((S5_W, S5_W)),
                  _lrow(layer, S5_W), _full((D_MODEL, D_MODEL)), vec, vec, _full((PLE_D, D_MODEL)), _full((D_MODEL, D_MODEL)),
                  vec, vec, vec],
        out_specs=[_row(D_MODEL)] * 4 + [_row(1)] * 2, out_shape=[_S((L, D_MODEL))] * 4 + [_S((L, 1))] * 2,
        compiler_params=_params(1))(x, hs, z, z, z, z, y0, p, w_glu, b_glu, w_out, g1, b1, ple_w, w_pg, b_pg, g2, b2)


def _post_bwd_a(dx2_or_target, is_top, xh2, xh1, rstd2, rstd1, gt, p, ple_w, w_pg, g1, b1, g2, b2, layer, token=None):
    L = xh1.shape[0]
    extra, extra_specs = _after(token)

    def body(d_ref, xh2_ref, xh1_ref, rstd2_ref, rstd1_ref, gt_ref, p_ref, pw_ref, wpg_ref, g1_ref, b1_ref, g2_ref,
             b2_ref, *rest):
        (dt1_ref, dpw_out, dwpg_out, dbpg_ref, dg1_ref, db1_ref, dg2_ref, db2_ref, loss_ref, dpw_ref,
         dwpg_ref) = rest[len(extra):]
        @pl.when(pl.program_id(0) == 0)
        def _():
            for ref in (dpw_ref, dwpg_ref, dbpg_ref, dg1_ref, db1_ref, dg2_ref, db2_ref, loss_ref):
                ref[...] = jnp.zeros_like(ref)

        g1, g2 = _pick(g1_ref, layer), _pick(g2_ref, layer)
        xh1, xh2, rstd1, rstd2 = xh1_ref[...], xh2_ref[...], rstd1_ref[...], rstd2_ref[...]
        x1 = xh1 * g1 + _pick(b1_ref, layer)
        if is_top:
            err = (xh2 * g2 + _pick(b2_ref, layer)) - d_ref[...]
            loss_ref[...] += _colsum(err * err)
            dx2 = err * (1.0 / D_MODEL)
        else:
            dx2 = d_ref[...]
        p = p_ref[...]
        q, gt = _mm(p, pw_ref[...]), gt_ref[...]
        dg2_ref[...] += _colsum(dx2 * xh2)
        db2_ref[...] += _colsum(dx2)
        dt2 = _ln_bwd(dx2, xh2, rstd2, g2)
        dq = dt2 * gt
        dgpre = (dt2 * q) * gt * (1.0 - gt)
        dpw_ref[...] += _mm_tn(p, dq)
        dwpg_ref[...] += _mm_tn(x1, dgpre)
        dbpg_ref[...] += _colsum(dgpre)
        dx1 = ALPHA * dt2 + _mm_nt(dgpre, wpg_ref[...])
        dg1_ref[...] += _colsum(dx1 * xh1)
        db1_ref[...] += _colsum(dx1)
        dt1_ref[...] = _ln_bwd(dx1, xh1, rstd1, g1)

        @pl.when(pl.program_id(0) == L // TM - 1)
        def _():
            dpw_out[...] = dpw_ref[...].astype(WIRE)
            dwpg_out[...] = dwpg_ref[...].astype(WIRE)

    vec, lvec = _full((1, D_MODEL)), _lrow(layer, D_MODEL)
    return pl.pallas_call(
        body, name="post_bwd_a_top" if is_top else "post_bwd_a", grid=(L // TM,),
        in_specs=[_row(D_MODEL), _row(D_MODEL), _row(D_MODEL), _row(1), _row(1), _row(D_MODEL), _p_rows(layer),
                  _full((PLE_D, D_MODEL)), _full((D_MODEL, D_MODEL)), lvec, lvec, lvec, lvec] + extra_specs,
        out_specs=[_row(D_MODEL), _full((PLE_D, D_MODEL)), _full((D_MODEL, D_MODEL)), vec, vec, vec, vec, vec, vec],
        out_shape=[_S((L, D_MODEL)), _S((PLE_D, D_MODEL), WIRE), _S((D_MODEL, D_MODEL), WIRE)] + [_S((1, D_MODEL))] * 6,
        scratch_shapes=[pltpu.VMEM((PLE_D, D_MODEL), F32), pltpu.VMEM((D_MODEL, D_MODEL), F32)],
        compiler_params=_params(1))(dx2_or_target, xh2, xh1, rstd2, rstd1, gt, p, ple_w, w_pg, g1, b1, g2, b2, *extra)


def _post_bwd_b(dt1, z, hs, y0, w_out, w_glu, b_glu, layer):
    L = dt1.shape[0]

    def body(dt1_ref, zg_ref, zs0_ref, zs1_ref, zs2_ref, hs_ref, y0_ref, wo_ref, wg_ref, bg_ref,
             dhs_ref, dy0_ref, dzg_ref, dwo_out, dwg_out, dbg_ref, dwo_ref, dwg_ref):
        @pl.when(pl.program_id(0) == 0)
        def _():
            for ref in (dwo_ref, dwg_ref, dbg_ref):
                ref[...] = jnp.zeros_like(ref)

        dt1b = dt1_ref[...].astype(MXU)
        dm = _mm_nt(dt1b, wo_ref[...])
        d_rgy, d_s5y = dm[:, :RG_W], dm[:, RG_W:]
        rg_gate = zg_ref[...]
        s5_gate = jnp.concatenate([zs0_ref[...], zs1_ref[...], zs2_ref[...]], axis=1)
        hs = hs_ref[...]
        sl, dsl = _silu_and_grad(rg_gate)
        dhs_ref[...] = d_rgy * sl
        dzg_ref[:, :RG_W] = d_rgy * hs * dsl
        y0 = y0_ref[...]
        y1 = _gelu(y0)
        gl = _sigmoid(_mm(y1, wg_ref[...]) + _pick(bg_ref, layer))
        y2 = y1 * gl
        sl2, dsl = _silu_and_grad(s5_gate)
        m = jnp.concatenate([(hs * sl).astype(MXU), (y2 * sl2).astype(MXU)], axis=1)
        dwo_ref[...] += _mm_tn(m, dt1b)
        dy2 = d_s5y * sl2
        dzg_ref[:, RG_W:] = d_s5y * y2 * dsl
        dglpre = (dy2 * y1) * gl * (1.0 - gl)
        dwg_ref[...] += _mm_tn(y1, dglpre)
        dbg_ref[...] += _colsum(dglpre)
        dy1 = dy2 * gl + _mm_nt(dglpre, wg_ref[...])
        dy0_ref[...] = dy1 * _gelu_grad(y0)

        @pl.when(pl.program_id(0) == L // TM - 1)
        def _():
            dwo_out[...] = dwo_ref[...].astype(WIRE)
            dwg_out[...] = dwg_ref[...].astype(WIRE)

    return pl.pallas_call(
        body, name="post_bwd_b", grid=(L // TM,),
        in_specs=[_row(D_MODEL), *_gate_rows(), _row(RG_W), _row(S5_W), _full((D_MODEL, D_MODEL)),
                  _full((S5_W, S5_W)), _lrow(layer, S5_W)],
        out_specs=[_row(RG_W), _row(S5_W), _row(D_MODEL), _full((D_MODEL, D_MODEL)), _full((S5_W, S5_W)), _full((1, S5_W))],
        out_shape=[_S((L, RG_W)), _S((L, S5_W)), _S((L, D_MODEL)), _S((D_MODEL, D_MODEL), WIRE), _S((S5_W, S5_W), WIRE),
                   _S((1, S5_W))],
        scratch_shapes=[pltpu.VMEM((D_MODEL, D_MODEL), F32), pltpu.VMEM((S5_W, S5_W), F32)],
        compiler_params=_params(1))(dt1, z, z, z, z, hs, y0, w_out, w_glu, b_glu)


def _adamw(parts, w, m, v, token=None):
    nl = len(parts)
    extra, extra_specs = _after(token)
    n, R, C = parts[0].shape
    tr = R
    for cand in (512, 256, 128, 64, 32, 16, 8):
        if R % cand == 0 and n * cand * C * 4 <= 4 * 1024 * 1024:
            tr = cand
            break
    nblk = R // tr

    def body(*refs):
        p_refs = refs[:nl]
        w_ref, m_ref, v_ref = refs[nl:nl + 3]
        g_ref, d_ref, nm_ref, nv_ref = refs[nl + 3 + len(extra):]
        layer = pl.program_id(0)
        g = None
        for li, p_ref in enumerate(p_refs):
            s = p_ref[0].astype(F32)
            for k in range(1, n):
                s = s + p_ref[k].astype(F32)
            g = s if g is None else jnp.where(layer == li, s, g)
        nm = B1 * m_ref[...] + (1.0 - B1) * g
        nv = B2 * v_ref[...] + (1.0 - B2) * (g * g)
        d_ref[...] = (-LR) * ((nm / BC1) / (jnp.sqrt(nv / BC2) + EPS) + WD * w_ref[...])
        g_ref[...], nm_ref[...], nv_ref[...] = g, nm, nv

    def part_spec(li):
        return pl.BlockSpec((n, tr, C), lambda l, i: (0, jnp.where(l == li, i, jnp.where(l < li, 0, nblk - 1)), 0))

    blk = pl.BlockSpec((tr, C), lambda l, i: (l * nblk + i, 0))
    return pl.pallas_call(
        body, name="adamw", grid=(nl, nblk),
        in_specs=[part_spec(li) for li in range(nl)] + [blk, blk, blk] + extra_specs,
        out_specs=[blk] * 4, out_shape=[_S((nl * R, C))] * 4, compiler_params=_params(2))(*parts, w, m, v, *extra)


def _adamw_sharded(names, recv, w, m, v, name, token=None):
    n, nl = len(names), len(recv)
    extra, extra_specs = _after(token)
    n_in = n * (nl + 3)

    def body(*refs):
        outs = refs[n_in + len(extra):]
        for j in range(n):
            w_ref, m_ref, v_ref = (refs[(nl + t) * n + j] for t in range(3))
            g_ref, d_ref, nm_ref, nv_ref = (outs[t * n + j] for t in range(4))
            for l in range(nl):
                p_ref = refs[l * n + j]
                g = p_ref[0].astype(F32)
                for q in range(1, N_DEV):
                    g = g + p_ref[q].astype(F32)
                nm = B1 * m_ref[l] + (1.0 - B1) * g
                nv = B2 * v_ref[l] + (1.0 - B2) * (g * g)
                d_ref[l] = (-LR) * ((nm / BC1) / (jnp.sqrt(nv / BC2) + EPS) + WD * w_ref[l])
                g_ref[l], nm_ref[l], nv_ref[l] = g, nm, nv

    ins = [r[k] for r in recv for k in names] + [t[k] for t in (w, m, v) for k in names]
    vm = pl.BlockSpec(memory_space=pltpu.VMEM)
    outs = pl.pallas_call(body, name=name, in_specs=[vm] * n_in + extra_specs,
                          out_shape=[_S(w[k].shape) for _ in range(4) for k in names],
                          compiler_params=pltpu.CompilerParams(vmem_limit_bytes=VMEM_LIMIT))(*ins, *extra)
    return [{k: outs[t * n + j] for j, k in enumerate(names)} for t in range(4)]


def _adamw_natural(names, g, w, m, v, name):
    n = len(names)

    def body(*refs):
        for j in range(n):
            g_ref, w_ref, m_ref, v_ref, d_ref, nm_ref, nv_ref = (refs[k * n + j] for k in range(7))
            gj = g_ref[...]
            nm = B1 * m_ref[...] + (1.0 - B1) * gj
            nv = B2 * v_ref[...] + (1.0 - B2) * (gj * gj)
            d_ref[...] = (-LR) * ((nm / BC1) / (jnp.sqrt(nv / BC2) + EPS) + WD * w_ref[...])
            nm_ref[...], nv_ref[...] = nm, nv

    ins = [t[k] for t in (g, w, m, v) for k in names]
    outs = pl.pallas_call(body, name=name, out_shape=[_S(w[k].shape) for _ in range(3) for k in names],
                          compiler_params=pltpu.CompilerParams(vmem_limit_bytes=VMEM_LIMIT))(*ins)
    return [{k: outs[t * n + j] for j, k in enumerate(names)} for t in range(3)]


def _me():
    return lax.axis_index("x"), lax.axis_index("y"), lax.axis_index("c")


def _lin(dev):
    return 4 * dev[0] + 2 * dev[1] + dev[2]


def _blk(ref, axis, size, idx):
    nd = len(ref.shape)
    start = idx * size
    if axis == nd - 1 and size % LANE == 0:
        start = pl.multiple_of(start, LANE)
    elif axis == nd - 2 and size % 16 == 0:
        start = pl.multiple_of(start, 16)
    ix = [slice(None)] * nd
    ix[axis] = pl.ds(start, size)
    return ref.at[tuple(ix)]


HBM_SPEC = pl.BlockSpec(memory_space=pltpu.HBM)
SEM_SPEC = pl.BlockSpec(memory_space=pltpu.SEMAPHORE)
EFFECT = pltpu.SideEffectType.DATAFLOW_SIDE_EFFECTING


def _peers(x, y, c):
    flip = lambda v, f: 1 - v if f else v
    return [(flip(x, k & 4), flip(y, k & 2), flip(c, k & 1)) for k in range(1, N_DEV)]


def _land_shape(mode, s, axis):
    if mode == "gather":
        return s.shape[:axis] + (N_DEV * s.shape[axis],) + s.shape[axis + 1:]
    return (N_DEV,) + s.shape[:axis] + (s.shape[axis] // N_DEV,) + s.shape[axis + 1:]


def _src_view(mode, ref, axis, peer):
    return ref if mode == "gather" else _blk(ref, axis, ref.shape[axis] // N_DEV, peer)


def _dst_view(mode, land, axis, sender):
    return _blk(land, axis, land.shape[axis] // N_DEV, sender) if mode == "gather" else land.at[sender]


def _blocks(mode, land, axis, k):
    if mode == "gather":
        ix = [slice(None)] * len(land.shape)
        ix[axis] = pl.ds(0, k * (land.shape[axis] // N_DEV))
        return land.at[tuple(ix)]
    return land.at[pl.ds(0, k)]


ARRIVALS = {None: N_DEV - 1, "near": 4, "relay": 3}


def _routes(route, x, y, c):
    me, sibling = (x, y, c), (x, y, 1 - c)
    chips = [(1 - x, y), (x, 1 - y), (1 - x, 1 - y)]
    if route == "near":
        return [(me, sibling)] + [(me, (*chip, c)) for chip in chips]
    if route == "relay":
        return [((*chip, c), sibling) for chip in chips]
    return [(me, peer) for peer in _peers(x, y, c)]


def _place_own(mode, srcs, axes, name, after=None):
    n = len(srcs)
    extra, extra_specs = _after(after)

    def body(me_ref, *refs):
        for a in range(n):
            out = refs[n + len(extra) + a]
            out[...] = refs[a][...].reshape(out.shape)

    def at_me(shape, axis):
        return lambda i, me: tuple(me[0] if d == axis else 0 for d in range(len(shape)))

    in_specs, out_specs = [], []
    for s, axis in zip(srcs, axes):
        if mode == "gather":
            in_specs.append(pl.BlockSpec(s.shape, lambda i, me, nd=len(s.shape): (0,) * nd))
            out_specs.append(pl.BlockSpec(s.shape, at_me(s.shape, axis)))
        else:
            blk = s.shape[:axis] + (s.shape[axis] // N_DEV,) + s.shape[axis + 1:]
            in_specs.append(pl.BlockSpec(blk, at_me(blk, axis)))
            out_specs.append(pl.BlockSpec((1,) + blk, at_me((1,) + blk, 0)))
    me = _lin(_me()).astype(jnp.int32).reshape(1)
    return pl.pallas_call(
        body, name=name, out_shape=[_S(_land_shape(mode, s, a), s.dtype) for s, a in zip(srcs, axes)],
        grid_spec=pltpu.PrefetchScalarGridSpec(num_scalar_prefetch=1, grid=(1,), in_specs=in_specs + extra_specs,
                                               out_specs=out_specs),
        compiler_params=_params(1))(me, *srcs, *extra)


def _place_shards(shards, layers, axes, dtypes, name, after=None):
    n = len(shards)
    extra, extra_specs = _after(after)

    def body(me_ref, *refs):
        for a in range(n):
            out = refs[n + len(extra) + a]
            out[...] = refs[a][...].astype(out.dtype)

    in_specs, out_specs, out_shape = [], [], []
    for s, layer, axis, dt in zip(shards, layers, axes, dtypes):
        shape = s.shape if layer is None else s.shape[1:]
        nd = len(shape)
        if layer is None:
            in_specs.append(pl.BlockSpec(shape, lambda i, me, nd=nd: (0,) * nd))
        else:
            in_specs.append(pl.BlockSpec((None,) + shape, lambda i, me, nd=nd, layer=layer: (layer,) + (0,) * nd))
        out_specs.append(pl.BlockSpec(shape, lambda i, me, nd=nd, axis=axis: tuple(me[0] if d == axis else 0 for d in range(nd))))
        out_shape.append(_S(shape[:axis] + (N_DEV * shape[axis],) + shape[axis + 1:], dt))
    me = _lin(_me()).astype(jnp.int32).reshape(1)
    return pl.pallas_call(
        body, name=name, out_shape=out_shape,
        grid_spec=pltpu.PrefetchScalarGridSpec(num_scalar_prefetch=1, grid=(1,), in_specs=in_specs + extra_specs,
                                               out_specs=out_specs),
        compiler_params=_params(1))(me, *shards, *extra)


def _push_start(mode, srcs, lands, axes, name, route=None):
    n, ns = len(lands), len(srcs)

    def body(*refs):
        src_refs, land_refs = refs[:ns], refs[ns:ns + n]
        send_sems, recv_sems = refs[ns + n], refs[ns + n + 1]
        token = refs[-1]
        x, y, c = _me()
        for a in range(n):
            for block, peer in _routes(route, x, y, c):
                there = _dst_view(mode, land_refs[a], axes[a], _lin(block))
                pltpu.make_async_remote_copy(
                    src_ref=_src_view(mode, src_refs[a], axes[a], _lin(peer)) if ns else there, dst_ref=there,
                    send_sem=send_sems.at[a], recv_sem=recv_sems.at[a], device_id=peer, device_id_type=MESH).start()
        token[...] = jnp.zeros_like(token)

    hbm = lambda s: pltpu.HBM(s.shape, s.dtype)
    outs = pl.pallas_call(
        body, name=name,
        out_shape=(pltpu.SemaphoreType.DMA((n,)), pltpu.SemaphoreType.DMA((n,)), *[hbm(s) for s in srcs], *[hbm(s) for s in lands],
                   _S((SUB, LANE))),
        in_specs=[HBM_SPEC] * (ns + n),
        out_specs=(SEM_SPEC, SEM_SPEC, *[HBM_SPEC] * (ns + n), pl.BlockSpec(memory_space=pltpu.VMEM)),
        input_output_aliases={i: 2 + i for i in range(ns + n)},
        compiler_params=pltpu.CompilerParams(has_side_effects=EFFECT),
    )(*[pltpu.with_memory_space_constraint(s, pltpu.HBM) for s in list(srcs) + list(lands)])
    return outs[0], outs[1], outs[2:2 + ns], outs[2 + ns:2 + ns + n], outs[-1]


def _push_wait(mode, send_sems, recv_sems, srcs, lands, axes, after, name, first=0, route=None):
    n, ns = len(lands), len(srcs)
    after = list(after) if isinstance(after, (list, tuple)) else [after]

    def body(*refs):
        land_refs = refs[ns:ns + n]
        send_sems, recv_sems = refs[ns + n], refs[ns + n + 1]
        x, y, c = _me()
        for a in range(n):
            seven = _blocks(mode, land_refs[a], axes[a], ARRIVALS[route])
            cp = pltpu.make_async_remote_copy(src_ref=seven, dst_ref=seven, send_sem=send_sems.at[first + a],
                                              recv_sem=recv_sems.at[first + a],
                                              device_id=(x, y, 1 - c), device_id_type=MESH)
            cp.wait_send()
            cp.wait_recv()

    hbm = lambda s: pltpu.HBM(s.shape, s.dtype)
    outs = pl.pallas_call(
        body, name=name, out_shape=tuple(hbm(s) for s in list(srcs) + list(lands)),
        in_specs=[HBM_SPEC] * (ns + n) + [SEM_SPEC, SEM_SPEC] + [ANY] * len(after), out_specs=tuple([HBM_SPEC] * (ns + n)),
        input_output_aliases={i: i for i in range(ns + n)},
        compiler_params=pltpu.CompilerParams(has_side_effects=EFFECT),
    )(*srcs, *lands, send_sems, recv_sems, *after)
    return outs[ns:]


def _sum_parts(parts):
    n, R, C = parts.shape

    def body(p_ref, o_ref):
        g = p_ref[0]
        for k in range(1, n):
            g = g + p_ref[k]
        o_ref[...] = g

    return pl.pallas_call(body, name="sum_parts", out_shape=_S((R, C)))(parts)


SMALL =['conv_b', 'rg_wa', 'rg_ba', 'rg_wx', 'rg_bx', 'rg_lambda', 's5_a_re', 's5_a_im', 's5_b_re', 's5_b_im',
         's5_c_re', 's5_c_im', 's5_d', 's5_log_step', 's5_b_glu', 'ln1_g', 'ln1_b', 'ple_gate_b', 'ln2_g', 'ln2_b']
WEIGHTS = ['w_in', 'conv_w', 'conv_b', 'rg_wa', 'rg_ba', 'rg_wx', 'rg_bx', 'rg_lambda', 's5_a_re', 's5_a_im', 's5_b_re',
           's5_b_im', 's5_c_re', 's5_c_im', 's5_d', 's5_log_step', 's5_w_glu', 's5_b_glu', 'w_out', 'ln1_g', 'ln1_b',
           'ple_w', 'ple_gate_w', 'ple_gate_b', 'ln2_g', 'ln2_b']
PACK_ROWS_MULT = 64


STORED = {'s5_b_re': (2, 3), 's5_b_im': (2, 3), 's5_d': (1, 2)}


def _stored(k, a):
    return jnp.swapaxes(a, *STORED[k]) if k in STORED else a


def _tile_rows(n):
    return -(-n // (SUB * LANE)) * SUB


def _pack(tree, scalar):
    parts = []
    for a in [tree[k] for k in SMALL] + [scalar.reshape(1)]:
        rows = _tile_rows(a.size)
        parts.append(jnp.pad(a.reshape(-1), (0, rows * LANE - a.size)).reshape(rows, LANE))
    rows = sum(p.shape[0] for p in parts)
    parts.append(jnp.zeros((-rows % PACK_ROWS_MULT, LANE), F32))
    return jnp.concatenate(parts, axis=0)


def _unpack(packed, like):
    out, r = {}, 0
    for k in SMALL:
        n = math.prod(like[k].shape)
        rows = _tile_rows(n)
        part = packed[r:r + rows]
        out[k] = (part if n == rows * LANE else part.reshape(-1)[:n]).reshape(like[k].shape)
        r += rows
    return out, packed[r, 0]


class _NoHooks:
    token = None
    first_token = None

    def first_weights(self, full, after):
        return full

    def layer_start(self, i, W, after):
        return W

    def late_weights(self, i, W, after):
        return W

    def post_done(self, i, g):
        return None

    def smalls_done(self, grads, loss):
        self.small = _small_grads(grads, self.res)
        return None

    def w_in_done(self, i, g):
        return None

    def layer_done(self, i, g, dx):
        return None


def _local_grads(x, p, target, W, disc, hooks):
    depth = 2
    saved = []
    for i in range(depth):
        if i > 0:
            W = hooks.layer_start(i, W, x)
        w = W[i]
        z = _inproj_fwd(x, w['w_in'], hooks.token if i == 0 else None)
        hs, *gates = _rg_fwd(z, w['conv_w'], w['conv_b'], w['wa_bd'], w['wx_bd'], w['rg_ba'], w['rg_bx'], w['rg_lambda'], i)
        d = disc[i]
        y0, s_re, s_im = _s5_fwd(z, d['bb_re'], d['bb_im'], d['lb_re'], d['lb_im'], d['c_re'], d['c_im'], w['s5_d'], i)
        W = hooks.late_weights(i, W, y0)
        w = W[i]
        x2, *norms = _post_fwd(x, hs, z, y0, p, w['s5_w_glu'], w['s5_b_glu'], w['w_out'], w['ln1_g'], w['ln1_b'],
                               w['ple_w'], w['ple_gate_w'], w['ple_gate_b'], w['ln2_g'], w['ln2_b'], i)
        saved.append((x, z, hs, gates, y0, s_re, s_im, norms))
        x = x2

    grads = [None] * depth
    dx = target
    loss = None
    token = None
    for i in reversed(range(depth)):
        w, d = W[i], disc[i]
        xin, z, hs, gates, y0, s_re, s_im, (xh1, xh2, gt, rstd1, rstd2) = saved[i]
        g = {}
        (dt1, g['ple_w'], g['ple_gate_w'], g['ple_gate_b'], g['ln1_g'], g['ln1_b'], g['ln2_g'], g['ln2_b'], lrow) = _post_bwd_a(
            dx, i == depth - 1, xh2, xh1, rstd2, rstd1, gt, p, w['ple_w'], w['ple_gate_w'], w['ln1_g'], w['ln1_b'],
            w['ln2_g'], w['ln2_b'], i, token)
        if i == depth - 1:
            loss = 0.5 / D_MODEL * jnp.sum(lrow)
        dhs, dy0, dzg, g['w_out'], g['s5_w_glu'], g['s5_b_glu'] = _post_bwd_b(dt1, z, hs, y0, w['w_out'], w['s5_w_glu'],
                                                                           w['s5_b_glu'], i)
        (dzu, g['bb_re'], g['bb_im'], g['lb_re'], g['lb_im'], g['c_re'], g['c_im'], g['s5_d']) = _s5_bwd(
            dy0, z, s_re, s_im, d['bb_re'], d['bb_im'], d['lb_re'], d['lb_im'], d['c_re'], d['c_im'], w['s5_d'], i,
            hooks.post_done(i, g))
        (dzx, g['conv_w'], g['conv_b'], g['wa_bd'], g['wx_bd'], g['rg_ba'], g['rg_bx'], g['rg_lambda']) = _rg_bwd(
            dhs, z, hs, gates, w['conv_w'], w['wa_bd'], w['wx_bd'], w['rg_lambda'], i)
        if i == 0:
            g['w_in'] = _inproj_bwd_dw(xin, dzx, dzg, dzu, hooks.smalls_done([g, grads[1]], loss))
            dx = _inproj_bwd_dx(dt1, dzx, dzg, dzu, w['w_in'], hooks.w_in_done(i, g))
        else:
            dx, g['w_in'] = _inproj_bwd(dt1, xin, dzx, dzg, dzu, w['w_in'])
        grads[i] = g
        token = hooks.layer_done(i, g, dx)
    return loss, dx, grads


def _s5_layouts_fwd(s5_a_re, s5_a_im, s5_log_step, s5_b_re, s5_b_im, s5_c_re, s5_c_im, token=None):
    depth = s5_a_re.shape[0]
    ar, ai = s5_a_re.reshape(depth * 24, S5_P), s5_a_im.reshape(depth * 24, S5_P)
    ls = s5_log_step.reshape(depth * 24, 1)
    lr, li, cr, ci = _s5_disc_fwd(ar, ai, ls, token)
    per_group = lambda a: a.reshape(depth * 24, 1, S5_P)
    as_c = lambda b: jnp.swapaxes(b, 2, 3).reshape(depth * 24, S5_H, S5_P)
    res = (ar, ai, ls, per_group(cr), per_group(ci), as_c(s5_b_re), as_c(s5_b_im))
    bbr, bbi = _s5_bscale_fwd(*res[3:])
    tiles = lambda a: a.reshape(depth * N_S5_T, S5_GT, S5_H, S5_P)
    rows = lambda a: a.reshape(depth * N_S5_T, S5_GT, S5_P)
    disc = dict(bb_re=tiles(bbr), bb_im=tiles(bbi), lb_re=rows(lr), lb_im=rows(li), c_re=tiles(s5_c_re), c_im=tiles(s5_c_im))
    return [disc] * depth, res


def _s5_layouts_bwd(grads, res):
    ar, ai, ls, cr, ci, br, bi = res
    depth = len(grads)
    stack = lambda k, shape: jnp.stack([g[k] for g in grads]).reshape(shape)
    groups, shape_c = (depth * 24, S5_H, S5_P), (depth, 24, S5_H, S5_P)
    dbr, dbi, dcr, dci = _s5_bscale_bwd(cr, ci, br, bi, stack('bb_re', groups), stack('bb_im', groups))
    gp = (depth * 24, S5_P)
    dar, dai, dls = _s5_disc_bwd(ar, ai, ls, stack('lb_re', gp), stack('lb_im', gp), dcr.reshape(gp), dci.reshape(gp))
    return dict(
        s5_a_re=dar.reshape(depth, 24, S5_P), s5_a_im=dai.reshape(depth, 24, S5_P), s5_log_step=dls.reshape(depth, 24),
        s5_b_re=dbr.reshape(shape_c), s5_b_im=dbi.reshape(shape_c),
        s5_c_re=stack('c_re', shape_c), s5_c_im=stack('c_im', shape_c))


LATE = ('w_out', 'ple_w', 'ple_gate_w', 's5_w_glu')


ROWS = ('conv_b', 'rg_ba', 'rg_bx', 'rg_lambda', 's5_d', 's5_b_glu', 'ln1_g', 'ln1_b', 'ple_gate_b', 'ln2_g', 'ln2_b')


def _shared_weights(full):
    shared = {k: full[k] for k in ROWS}
    shared.update(conv_w=full['conv_w'], wa_bd=full['rg_wa'], wx_bd=full['rg_wx'], s5_d=full['s5_d'].reshape(DEPTH, 1, S5_W))
    return shared


def _layer_weights(full, shared, i):
    return dict(shared, w_in=full['w_in'][i])


class _AllLocal(_NoHooks):
    def __init__(self, full):
        self.full = full

    def late_weights(self, i, W, after):
        W[i].update({k: self.full[k][i] for k in LATE})
        return W


def _full_grads(full, x, p, target, hooks=None):
    hooks = hooks or _AllLocal(full)
    disc, res = _s5_layouts_fwd(full['s5_a_re'], full['s5_a_im'], full['s5_log_step'], full['s5_b_re'], full['s5_b_im'],
                                full['s5_c_re'], full['s5_c_im'], hooks.first_token)
    full = hooks.first_weights(full, disc[-1]['bb_im'])
    shared = _shared_weights(full)
    W = [_layer_weights(full, shared, i) for i in range(2)]
    hooks.res = res
    loss, gx, grads = _local_grads(x, p, target, W, disc, hooks)
    out = dict(hooks.small)
    for k in SHARD_AXIS:
        out[k] = [g[k] for g in grads]
    return loss, gx, out


def _small_grads(grads, res):
    stack = lambda f: jnp.stack([f(g) for g in grads])
    out = _s5_layouts_bwd(grads, res)
    out['conv_w'] = stack(lambda g: g['conv_w'])
    for k in ('conv_b', 'rg_ba', 'rg_bx', 'rg_lambda', 's5_b_glu', 'ln1_g', 'ln1_b', 'ple_gate_b', 'ln2_g', 'ln2_b'):
        out[k] = stack(lambda g: g[k][0])
    out['s5_d'] = _stored('s5_d', stack(lambda g: g['s5_d'][0]).reshape(2, 24, 16))
    out['rg_wa'] = stack(lambda g: g['wa_bd'])
    out['rg_wx'] = stack(lambda g: g['wx_bd'])
    return out


SHARD_AXIS = {'w_in': 2, 'w_out': 1, 'ple_w': 2, 'ple_gate_w': 1, 's5_w_glu': 1}


def kernel(x, p, w_in, conv_w, conv_b, rg_wa, rg_ba, rg_wx, rg_bx, rg_lambda, s5_a_re, s5_a_im, s5_b_re, s5_b_im, s5_c_re, s5_c_im, s5_d, s5_log_step, s5_w_glu, s5_b_glu, w_out, ln1_g, ln1_b, ple_w, ple_gate_w, ple_gate_b, ln2_g, ln2_b, loss_target, m_w_in, m_conv_w, m_conv_b, m_rg_wa, m_rg_ba, m_rg_wx, m_rg_bx, m_rg_lambda, m_s5_a_re, m_s5_a_im, m_s5_b_re, m_s5_b_im, m_s5_c_re, m_s5_c_im, m_s5_d, m_s5_log_step, m_s5_w_glu, m_s5_b_glu, m_w_out, m_ln1_g, m_ln1_b, m_ple_w, m_ple_gate_w, m_ple_gate_b, m_ln2_g, m_ln2_b, v_w_in, v_conv_w, v_conv_b, v_rg_wa, v_rg_ba, v_rg_wx, v_rg_bx, v_rg_lambda, v_s5_a_re, v_s5_a_im, v_s5_b_re, v_s5_b_im, v_s5_c_re, v_s5_c_im, v_s5_d, v_s5_log_step, v_s5_w_glu, v_s5_b_glu, v_w_out, v_ln1_g, v_ln1_b, v_ple_w, v_ple_gate_w, v_ple_gate_b, v_ln2_g, v_ln2_b):
    local = dict(locals())
    w = {k: local[k] for k in WEIGHTS}
    mom = {k: local['m_' + k] for k in WEIGHTS}
    var = {k: local['v_' + k] for k in WEIGHTS}

    big = list(SHARD_AXIS)
    late_axes = [SHARD_AXIS[k] - 1 for k in LATE]
    pushed = {}

    groups = dict(first=(['w_in', 'conv_w'], [0, None], [1, 0]), l0=(list(LATE), [0] * len(LATE), late_axes),
                  l1=(['w_in'] + list(LATE), [1] * (1 + len(LATE)), [1] + late_axes))
    token = None
    for key, members in (("first", ["first"]), ("rest", ["l0", "l1"])):
        names, layers, axes = (sum((groups[m][j] for m in members), []) for j in range(3))
        shards = [w[k] if layer is not None else w[k][None] for k, layer in zip(names, layers)]
        lands = _place_shards(shards, layers, axes, [WIRE if k in big else w[k].dtype for k in names],
                              "place_weights_" + key, token)
        pushed[key] = _push_start("gather", [], lands, axes, "push_weights_" + key, "near" if key == "first" else None)
        token = pushed[key][4]

    def await_weights(key, axes, after):
        s, first = pushed["rest"], 0 if key == "l0" else len(LATE)
        return _push_wait("gather", s[0], s[1], [], s[3][first:first + len(axes)], axes, after, "await_weights_" + key, first)

    def push_grads(key, g, names, axes):
        srcs = [g[k] for k in names]
        pushed[key] = _push_start("scatter", srcs, _place_own("scatter", srcs, axes, "place_grads_" + key), axes,
                                  "push_grads_" + key)
        return pushed[key][4]

    def await_grads(key, axes, after):
        s = pushed[key]
        return _push_wait("scatter", s[0], s[1], s[2], s[3], axes, after, "await_grads_" + key)

    class Overlap(_NoHooks):
        token = pushed["rest"][4]
        first_token = token

        def first_weights(self, full, after):
            s, axes = pushed["first"], [1, 0]
            near = _push_wait("gather", s[0], s[1], [], s[3], axes, after, "await_weights_near", route="near")
            s = _push_start("gather", [], near, axes, "relay_weights", "relay")
            w_in0, conv = _push_wait("gather", s[0], s[1], [], s[3], axes, s[4], "await_weights_relay", route="relay")
            return dict(full, w_in=[w_in0, None], conv_w=jnp.moveaxis(conv, 0, 2).reshape(2, 4, RG_W))

        def late_weights(self, i, W, after):
            if i == 0:
                W[0].update(zip(LATE, await_weights("l0", late_axes, after)))
            return W

        def layer_start(self, i, W, after):
            lands = await_weights("l1", [1] + late_axes, after)
            W[1].update(zip(LATE, lands[1:]), w_in=lands[0])
            return W

        def post_done(self, i, g):
            return push_grads("late0", g, LATE, late_axes) if i == 0 else None

        def smalls_done(self, grads, loss):
            super().smalls_done(grads, loss)
            conv = jnp.moveaxis(self.small['conv_w'].reshape(2, 4, N_DEV, RG_W // N_DEV), 2, 0)
            self.packed = _pack(self.small, loss)
            return push_grads("small", dict(conv_w=conv.reshape(N_DEV, 8, RG_W // N_DEV), small=self.packed),
                              ['conv_w', 'small'], [0, 0])

        def w_in_done(self, i, g):
            return push_grads("w_in0", g, ['w_in'], [0])

        def layer_done(self, i, g, dx):
            return push_grads("all1", g, ['w_in'] + list(LATE), [0] + late_axes) if i == 1 else None

    hooks = Overlap()
    _, grad_x, g = _full_grads(dict(w), x[0], p, loss_target[0], hooks)

    recv1 = dict(zip(['w_in'] + list(LATE), await_grads("all1", [0] + late_axes, grad_x)))
    recv0 = dict(zip(LATE, await_grads("late0", late_axes, grad_x)))
    outs = {}

    def update(k, parts, token=None):
        shard = w[k].shape
        c = shard[-1]
        two = lambda a: a.reshape(-1, c)
        res = _adamw([r.reshape(N_DEV, -1, c) for r in parts], two(w[k]), two(mom[k]), two(var[k]), token)
        outs[k] = [o.reshape(shard) for o in res]

    conv_parts, small_parts = await_grads("small", [0, 0], grad_x)
    rows = hooks.packed.shape[0] // N_DEV
    mine = _sum_parts(small_parts.reshape(N_DEV, rows, LANE))
    sums = _push_start("gather", [mine], _place_own("gather", [mine], [0], "place_small_sums"), [0], "push_small_sums")
    late = _adamw_sharded(list(LATE), [recv0, recv1], w, mom, var, "adamw_late", sums[4])
    for k in LATE:
        outs[k] = [t[k] for t in late]
    w_in0, = await_grads("w_in0", [0], [outs[k][1] for k in LATE])
    update('w_in', [w_in0, recv1['w_in']])
    update('conv_w', [conv_parts])
    gathered, = _push_wait("gather", sums[0], sums[1], sums[2], sums[3], [0], [outs['w_in'][1], outs['conv_w'][1]],
                           "await_small_sums")
    stored = [{k: _stored(k, t[k]) for k in SMALL} for t in (w, mom, var)]
    summed, loss = _unpack(gathered, stored[0])
    delta, new_m, new_v = _adamw_natural(SMALL, summed, *stored, "adamw_small")
    for k in SMALL:
        outs[k] = [_stored(k, o[k]) for o in (summed, delta, new_m, new_v)]

    res = [loss, grad_x[None]]
    for j in range(4):
        res += [outs[k][j] for k in WEIGHTS]
    return tuple(res)
```

```python
import math

import jax
import jax.numpy as jnp
from jax import lax
from jax.experimental import pallas as pl
from jax.experimental.pallas import tpu as pltpu

F32 = jnp.float32
MXU = jnp.bfloat16
WIRE = jnp.bfloat16

N_DEV = 8
D_MODEL = 1024
PLE_D = 256
RG_W = 640
S5_W = 384
S5_P = 64
S5_N = 24 * S5_P
Z_W = 2 * RG_W + 2 * S5_W
C_RGG = RG_W
C_S5U = 2 * RG_W
C_S5G = 2 * RG_W + S5_W
LANE = 128
N_RG_T = RG_W // LANE
N_S5_T = S5_W // LANE
W_BLK = Z_W // N_DEV
ALPHA = (2.0 * 2) ** 0.25
LN_EPS = 1e-5
RG_C = 8.0
LR, B1, B2, EPS, WD, STEP = 0.001, 0.9, 0.999, 1e-08, 0.01, 10
BC1 = 1.0 - B1 ** STEP
BC2 = 1.0 - B2 ** STEP
RC = 512
RC_RG = 1024
TM = 512
TM_MM = 1024
VMEM_LIMIT = 56 * 1024 * 1024

MESH = pl.DeviceIdType.MESH
ANY = pl.BlockSpec(memory_space=pl.ANY)


def _params(n_grid_axes, vmem=VMEM_LIMIT):
    return pltpu.CompilerParams(dimension_semantics=("arbitrary",) * n_grid_axes, vmem_limit_bytes=vmem)


def _S(shape, dtype=F32):
    return jax.ShapeDtypeStruct(tuple(shape), dtype)


def _sigmoid(x):
    return 0.5 * jnp.tanh(0.5 * x) + 0.5


def _silu_and_grad(x):
    s = _sigmoid(x)
    return x * s, s * (1.0 + x * (1.0 - s))


_GELU_C = math.sqrt(2.0 / math.pi)


def _gelu(x):
    return 0.5 * x * (1.0 + jnp.tanh(_GELU_C * (x + 0.044715 * (x * x * x))))


def _gelu_grad(x):
    th = jnp.tanh(_GELU_C * (x + 0.044715 * (x * x * x)))
    return 0.5 * (1.0 + th) + 0.5 * x * (1.0 - th * th) * (_GELU_C * (1.0 + 3.0 * 0.044715 * (x * x)))


def _mm(a, b):
    return jnp.dot(a.astype(MXU), b.astype(MXU), preferred_element_type=F32)


def _mm_nt(a, b):
    return lax.dot_general(a.astype(MXU), b.astype(MXU), (((1,), (1,)), ((), ())), preferred_element_type=F32)


def _mm_tn(a, b):
    return lax.dot_general(a.astype(MXU), b.astype(MXU), (((0,), (0,)), ((), ())), preferred_element_type=F32)


def _ln_fwd(t, g, b):
    mu = jnp.mean(t, axis=-1, keepdims=True)
    tc = t - mu
    var = jnp.mean(tc * tc, axis=-1, keepdims=True)
    rstd = lax.rsqrt(var + LN_EPS)
    xhat = tc * rstd
    return xhat * g + b, xhat, rstd


def _ln_bwd(dy, xhat, rstd, g):
    dxh = dy * g
    m1 = jnp.mean(dxh, axis=-1, keepdims=True)
    m2 = jnp.mean(dxh * xhat, axis=-1, keepdims=True)
    return rstd * (dxh - m1 - xhat * m2)


def _colsum(a):
    return jnp.sum(a, axis=0, keepdims=True)


def _up(x, d, rows, fill):
    n = x.shape[0]
    return jnp.where(rows < n - d, pltpu.roll(x, n - d, 0), fill)


SUB = 8
TILE_STEPS = (1, 2, 4)


def _r8(width):
    return lax.broadcasted_iota(jnp.int32, (SUB, width), 0)


def _scan_real(a, u, carry, reverse=False):
    r8 = _r8(a.shape[1])
    n = a.shape[0] // SUB
    outs = [None] * n
    for k in (reversed(range(n)) if reverse else range(n)):
        A, U = a[SUB * k:SUB * k + SUB], u[SUB * k:SUB * k + SUB]
        for d in TILE_STEPS:
            m = (r8 < SUB - d) if reverse else (r8 >= d)
            sh = SUB - d if reverse else d
            U = A * jnp.where(m, pltpu.roll(U, sh, 0), 0.0) + U
            A = A * jnp.where(m, pltpu.roll(A, sh, 0), 1.0)
        h = A * carry + U
        outs[k] = h
        carry = h[0:1] if reverse else h[SUB - 1:SUB]
    return jnp.concatenate(outs, axis=0), carry


def _tile_powers(lr, li, reverse=False):
    width = lr.shape[1]
    r8 = _r8(width)
    steps = []
    pr, pi = lr, li
    er, ei = jnp.broadcast_to(lr, (SUB, width)), jnp.broadcast_to(li, (SUB, width))
    for d in TILE_STEPS:
        m = (r8 < SUB - d) if reverse else (r8 >= d)
        sh = SUB - d if reverse else d
        steps.append((sh, jnp.where(m, pr, 0.0), jnp.where(m, pi, 0.0)))
        er, ei = _cmul(er, ei, jnp.where(m, pltpu.roll(er, sh, 0), 1.0), jnp.where(m, pltpu.roll(ei, sh, 0), 0.0))
        pr, pi = _cmul(pr, pi, pr, pi)
    return steps, (er, ei)


def _scan_lti(xr, xi, carry, steps, e, reverse=False):
    er, ei = e
    kr, ki = carry
    n = xr.shape[0] // SUB
    outr, outi = [None] * n, [None] * n
    for k in (reversed(range(n)) if reverse else range(n)):
        sr, si = xr[SUB * k:SUB * k + SUB], xi[SUB * k:SUB * k + SUB]
        for sh, pr, pi in steps:
            shr, shi = pltpu.roll(sr, sh, 0), pltpu.roll(si, sh, 0)
            sr, si = sr + (pr * shr - pi * shi), si + (pr * shi + pi * shr)
        sr = sr + (er * kr - ei * ki)
        si = si + (er * ki + ei * kr)
        outr[k], outi[k] = sr, si
        kr, ki = (sr[0:1], si[0:1]) if reverse else (sr[SUB - 1:SUB], si[SUB - 1:SUB])
    return jnp.concatenate(outr, axis=0), jnp.concatenate(outi, axis=0), (kr, ki)


def _halo(ref, c, r0):
    rp = pl.multiple_of(jnp.maximum(r0 - 8, 0), 8)
    return jnp.where(c > 0, ref[pl.ds(rp, 8), :], 0.0)


def _conv_taps(xe):
    return [pltpu.roll(xe, 3, 0)[8:, :], pltpu.roll(xe, 2, 0)[8:, :], pltpu.roll(xe, 1, 0)[8:, :], xe[8:, :]]


def _rg_gates(h, wa, wx, ba, bx, sp):
    r = _sigmoid(_mm(h, wa) + ba)
    i = _sigmoid(_mm(h, wx) + bx)
    log_a = (-RG_C) * r * sp
    a = jnp.exp(log_a)
    mult = jnp.sqrt(-jnp.tanh(log_a) * (a * a + 1.0))
    return r, i, a, mult


def _softplus(y):
    return jnp.maximum(y, 0.0) + jnp.log1p(jnp.exp(-jnp.abs(y)))


def _after(token):
    return ([], []) if token is None else ([token], [ANY])


def _inproj_fwd(x, w_in, token=None):
    L = x.shape[0]

    def body(x_ref, w_ref, *rest):
        rest[-1][...] = _mm(x_ref[...], w_ref[...])

    extra, extra_specs = _after(token)
    tm = min(TM_MM, L)
    return pl.pallas_call(
        body, name="inproj_fwd", grid=(L // tm,),
        in_specs=[pl.BlockSpec((tm, D_MODEL), lambda i: (i, 0)), pl.BlockSpec((D_MODEL, Z_W), lambda i: (0, 0))] + extra_specs,
        out_specs=pl.BlockSpec((tm, Z_W), lambda i: (i, 0)),
        out_shape=_S((L, Z_W)), compiler_params=_params(1))(x, w_in, *extra)


def _inproj_bwd(dt1, x, dzx, dzg, dzu, w_in):
    L = x.shape[0]

    def body(dt1_ref, x_ref, dzx_ref, dzg_ref, dzu_ref, w_ref, dx_ref, dw_ref, acc_ref):
        @pl.when(pl.program_id(0) == 0)
        def _():
            acc_ref[...] = jnp.zeros_like(acc_ref)
        dzg = dzg_ref[...]
        dz = jnp.concatenate([dzx_ref[...], dzg[:, :RG_W], dzu_ref[...], dzg[:, RG_W:]], axis=1).astype(MXU)
        xb = x_ref[...].astype(MXU)
        dx_ref[...] = ALPHA * dt1_ref[...] + _mm_nt(dz, w_ref[...])
        for j in range(N_DEV):
            acc_ref[j] += _mm_tn(xb, dz[:, j * W_BLK:(j + 1) * W_BLK])

        @pl.when(pl.program_id(0) == L // TM - 1)
        def _():
            dw_ref[...] = acc_ref[...].astype(WIRE)

    row = lambda w: pl.BlockSpec((TM, w), lambda i: (i, 0))
    wspec = pl.BlockSpec((N_DEV, D_MODEL, W_BLK), lambda i: (0, 0, 0))
    return pl.pallas_call(
        body, name="inproj_bwd", grid=(L // TM,),
        in_specs=[row(D_MODEL), row(D_MODEL), row(RG_W), row(D_MODEL), row(S5_W),
                  pl.BlockSpec((D_MODEL, Z_W), lambda i: (0, 0))],
        out_specs=[row(D_MODEL), wspec],
        out_shape=[_S((L, D_MODEL)), _S((N_DEV, D_MODEL, W_BLK), WIRE)],
        scratch_shapes=[pltpu.VMEM((N_DEV, D_MODEL, W_BLK), F32)],
        compiler_params=_params(1))(dt1, x, dzx, dzg, dzu, w_in)


TM2 = 1024


def _dz_block(dzx_ref, dzg_ref, dzu_ref):
    dzg = dzg_ref[...]
    return jnp.concatenate([dzx_ref[...], dzg[:, :RG_W], dzu_ref[...], dzg[:, RG_W:]], axis=1).astype(MXU)


def _inproj_bwd_dw(x, dzx, dzg, dzu, token=None):
    L = x.shape[0]
    extra, extra_specs = _after(token)

    def body(x_ref, dzx_ref, dzg_ref, dzu_ref, *rest):
        dw_ref, acc_ref = rest[len(extra):]
        @pl.when(pl.program_id(0) == 0)
        def _():
            acc_ref[...] = jnp.zeros_like(acc_ref)
        dz = _dz_block(dzx_ref, dzg_ref, dzu_ref)
        xb = x_ref[...].astype(MXU)
        for j in range(N_DEV):
            acc_ref[j] += _mm_tn(xb, dz[:, j * W_BLK:(j + 1) * W_BLK])

        @pl.when(pl.program_id(0) == L // TM2 - 1)
        def _():
            dw_ref[...] = acc_ref[...].astype(WIRE)

    row = lambda w: pl.BlockSpec((TM2, w), lambda i: (i, 0))
    wspec = pl.BlockSpec((N_DEV, D_MODEL, W_BLK), lambda i: (0, 0, 0))
    return pl.pallas_call(
        body, name="inproj_bwd_dw", grid=(L // TM2,),
        in_specs=[row(D_MODEL), row(RG_W), row(D_MODEL), row(S5_W)] + extra_specs, out_specs=wspec,
        out_shape=_S((N_DEV, D_MODEL, W_BLK), WIRE), scratch_shapes=[pltpu.VMEM((N_DEV, D_MODEL, W_BLK), F32)],
        compiler_params=_params(1))(x, dzx, dzg, dzu, *extra)


def _inproj_bwd_dx(dt1, dzx, dzg, dzu, w_in, token=None):
    L = dt1.shape[0]
    extra, extra_specs = _after(token)

    def body(dt1_ref, dzx_ref, dzg_ref, dzu_ref, w_ref, *rest):
        rest[-1][...] = ALPHA * dt1_ref[...] + _mm_nt(_dz_block(dzx_ref, dzg_ref, dzu_ref), w_ref[...])

    tm = min(TM_MM, L)
    row = lambda w: pl.BlockSpec((tm, w), lambda i: (i, 0))
    return pl.pallas_call(
        body, name="inproj_bwd_dx", grid=(L // tm,),
        in_specs=[row(D_MODEL), row(RG_W), row(D_MODEL), row(S5_W), _full((D_MODEL, Z_W))] + extra_specs,
        out_specs=row(D_MODEL), out_shape=_S((L, D_MODEL)), compiler_params=_params(1))(dt1, dzx, dzg, dzu, w_in, *extra)


def _rg_specs(layer):
    tile = lambda rows: pl.BlockSpec((rows, LANE), lambda c: (0, c))
    ptile = lambda rows: pl.BlockSpec((None, rows, LANE), lambda c: (layer, 0, c))
    pheads = pl.BlockSpec((None, 2, RG_HD, RG_HD), lambda c: (layer, c, 0, 0))
    return tile, ptile, pheads, pl.BlockSpec((2, RG_HD, RG_HD), lambda c: (c, 0, 0))


RG_HD = 64


def _bd2(w):
    z = jnp.zeros((RG_HD, RG_HD), w.dtype)
    return jnp.concatenate([jnp.concatenate([w[0], z], axis=1), jnp.concatenate([z, w[1]], axis=1)], axis=0)


def _bd2_diag(m):
    return jnp.stack([m[:RG_HD, :RG_HD], m[RG_HD:, RG_HD:]])


def _rg_fwd(z, cw, cb, wa_bd, wx_bd, ba, bx, lam, layer):
    L = z.shape[0]
    RC = min(RC_RG, L)

    def body(x_ref, cw_ref, cb_ref, wa_ref, wx_ref, ba_ref, bx_ref, lam_ref, hs_ref, *saved):
        row = slice(layer, layer + 1)
        w, b = cw_ref[...], cb_ref[row, :]
        wa, wx, ba_, bx_ = _bd2(wa_ref[...]).astype(MXU), _bd2(wx_ref[...]).astype(MXU), ba_ref[row, :], bx_ref[row, :]
        sp = _softplus(-lam_ref[row, :])

        def step(c, carry):
            r0 = pl.multiple_of(c * RC, RC)
            xe = jnp.concatenate([_halo(x_ref, c, r0), x_ref[pl.ds(r0, RC), :]], axis=0)
            t = _conv_taps(xe)
            h = t[0] * w[0:1] + t[1] * w[1:2] + t[2] * w[2:3] + t[3] * w[3:4] + b
            r, i, a, mult = _rg_gates(h, wa, wx, ba_, bx_, sp)
            hs, carry = _scan_real(a, mult * (i * h), carry)
            hs_ref[pl.ds(r0, RC), :] = hs
            for ref, val in zip(saved, (h, r, i, a, mult)):
                ref[pl.ds(r0, RC), :] = val
            return carry

        lax.fori_loop(0, L // RC, step, jnp.zeros((1, LANE), F32))

    tile, ptile, pheads, _ = _rg_specs(layer)
    return pl.pallas_call(
        body, name="rg_fwd", grid=(N_RG_T,),
        in_specs=[tile(L), ptile(4), tile(2), pheads, pheads, tile(2), tile(2), tile(2)],
        out_specs=[tile(L)] * 6, out_shape=[_S((L, RG_W))] * 6, compiler_params=_params(1))(
            z, cw, cb, wa_bd, wx_bd, ba, bx, lam)


def _rg_bwd(dhs, z, hs, gates, cw, wa_bd, wx_bd, lam, layer):
    L = z.shape[0]
    RC = min(RC_RG, L)

    def body(g_ref, x_ref, hs_ref, h_ref, r_ref, i_ref, a_ref, mult_ref, cw_ref, wa_ref, wx_ref, lam_ref,
             dx_ref, dcw_ref, dcb_ref, dwa_out, dwx_out, dba_ref, dbx_ref, dlam_ref, dwa_ref, dwx_ref):
        w = cw_ref[...]
        wa, wx = _bd2(wa_ref[...]).astype(MXU), _bd2(wx_ref[...]).astype(MXU)
        lam = lam_ref[layer:layer + 1, :]
        sp = _softplus(-lam)
        rows = lax.broadcasted_iota(jnp.int32, (RC, LANE), 0)
        for ref in (dcw_ref, dcb_ref, dwa_ref, dwx_ref, dba_ref, dbx_ref, dlam_ref):
            ref[...] = jnp.zeros_like(ref)
        nch = L // RC

        def step(k, carry):
            cin, nxt = carry
            c = nch - 1 - k
            r0 = pl.multiple_of(c * RC, RC)
            xe = jnp.concatenate([_halo(x_ref, c, r0), x_ref[pl.ds(r0, RC), :]], axis=0)
            t = _conv_taps(xe)
            h, r, i, a, mult = (ref[pl.ds(r0, RC), :] for ref in (h_ref, r_ref, i_ref, a_ref, mult_ref))
            hs_e = jnp.concatenate([_halo(hs_ref, c, r0), hs_ref[pl.ds(r0, RC), :]], axis=0)
            hs_prev = pltpu.roll(hs_e, 1, 0)[8:, :]
            g = g_ref[pl.ds(r0, RC), :]
            cc, cin_new = _scan_real(a, a * g, cin, reverse=True)
            dh = g + _up(cc, 1, rows, cin)
            ih = i * h
            dlog_a = dh * hs_prev * a - (dh * ih) * (a * a) / mult
            di = dh * mult * h
            dhin = dh * mult * i
            dr = dlog_a * ((-RG_C) * sp)
            dlam_ref[...] += _colsum(dlog_a * r)
            dra = dr * r * (1.0 - r)
            dia = di * i * (1.0 - i)
            dwa_ref[...] += _mm_tn(h, dra)
            dwx_ref[...] += _mm_tn(h, dia)
            dba_ref[...] += _colsum(dra)
            dbx_ref[...] += _colsum(dia)
            dhin = dhin + _mm_nt(dra, wa) + _mm_nt(dia, wx)
            de = jnp.concatenate([dhin, nxt], axis=0)
            n = RC + 8
            dx = (dhin * w[3:4] + pltpu.roll(de, n - 1, 0)[:RC, :] * w[2:3]
                  + pltpu.roll(de, n - 2, 0)[:RC, :] * w[1:2] + pltpu.roll(de, n - 3, 0)[:RC, :] * w[0:1])
            dx_ref[pl.ds(r0, RC), :] = dx
            for kk in range(4):
                dcw_ref[kk:kk + 1, :] += _colsum(dhin * t[kk])
            dcb_ref[...] += _colsum(dhin)
            return cin_new, dhin[0:8, :]

        lax.fori_loop(0, nch, step, (jnp.zeros((1, LANE), F32), jnp.zeros((8, LANE), F32)))
        dlam_ref[...] = dlam_ref[...] * (RG_C * _sigmoid(-lam))
        dwa_out[...], dwx_out[...] = _bd2_diag(dwa_ref[...]), _bd2_diag(dwx_ref[...])

    tile, ptile, pheads, gheads = _rg_specs(layer)
    heads = _S((2 * N_RG_T, RG_HD, RG_HD))
    return pl.pallas_call(
        body, name="rg_bwd", grid=(N_RG_T,),
        in_specs=[tile(L)] * 8 + [ptile(4), pheads, pheads, tile(2)],
        out_specs=[tile(L), tile(4), tile(1), gheads, gheads, tile(1), tile(1), tile(1)],
        out_shape=[_S((L, RG_W)), _S((4, RG_W)), _S((1, RG_W)), heads, heads, _S((1, RG_W)), _S((1, RG_W)), _S((1, RG_W))],
        scratch_shapes=[pltpu.VMEM((LANE, LANE), F32), pltpu.VMEM((LANE, LANE), F32)],
        compiler_params=_params(1))(dhs, z, hs, *gates, cw, wa_bd, wx_bd, lam)


def _cmul(ar, ai, br, bi):
    return ar * br - ai * bi, ar * bi + ai * br


S5_TW = S5_N // N_S5_T


S5_H = 16
S5_GT = LANE // S5_H


def _s5_specs(L, layer):
    in_tile = pl.BlockSpec((L, LANE), lambda t: (0, t))
    st = pl.BlockSpec((L, S5_TW), lambda t: (0, t))
    pg = pl.BlockSpec((None, S5_GT, S5_H, S5_P), lambda t: (layer * N_S5_T + t, 0, 0, 0))
    plb = pl.BlockSpec((None, S5_GT, S5_P), lambda t: (layer * N_S5_T + t, 0, 0))
    gg = pl.BlockSpec((None, S5_GT, S5_H, S5_P), lambda t: (t, 0, 0, 0))
    glb = pl.BlockSpec((None, S5_GT, S5_P), lambda t: (t, 0, 0))
    dv = pl.BlockSpec((1, LANE), lambda t: (0, t))
    return in_tile, st, pg, plb, gg, glb, dv


def _bd8(blocks):
    rows = []
    for g in range(S5_GT):
        pieces = [blocks[g]]
        if g:
            pieces.insert(0, jnp.zeros((S5_H, S5_P * g), blocks.dtype))
        if g < S5_GT - 1:
            pieces.append(jnp.zeros((S5_H, S5_P * (S5_GT - 1 - g)), blocks.dtype))
        rows.append(jnp.concatenate(pieces, axis=1))
    return jnp.concatenate(rows, axis=0)


def _bd8_diag(m):
    return jnp.stack([m[S5_H * g:S5_H * (g + 1), S5_P * g:S5_P * (g + 1)] for g in range(S5_GT)])


def _row8(v):
    return jnp.concatenate([v[g:g + 1] for g in range(S5_GT)], axis=1)


def _row8_split(r):
    return jnp.concatenate([r[:, S5_P * g:S5_P * (g + 1)] for g in range(S5_GT)], axis=0)


def _layer_row_tile(layer):
    return pl.BlockSpec((None, 1, LANE), lambda t: (layer, 0, t))


def _s5_fwd(z, bb_re, bb_im, lb_re, lb_im, c_re, c_im, dvec, layer):
    L = z.shape[0]

    def body(u_ref, bbr_ref, bbi_ref, lr_ref, li_ref, cr_ref, ci_ref, d_ref, y_ref, sr_ref, si_ref):
        bbr, bbi = _bd8(bbr_ref[...]).astype(MXU), _bd8(bbi_ref[...]).astype(MXU)
        cr, ci = _bd8(cr_ref[...]).astype(MXU), _bd8(ci_ref[...]).astype(MXU)
        dv = d_ref[...]
        steps, e = _tile_powers(_row8(lr_ref[...]), _row8(li_ref[...]))

        def step(c, carry):
            r0 = pl.multiple_of(c * RC, RC)
            u = u_ref[pl.ds(r0, RC), :]
            ub = u.astype(MXU)
            sr = jnp.dot(ub, bbr, preferred_element_type=F32)
            si = jnp.dot(ub, bbi, preferred_element_type=F32)
            sr, si, carry = _scan_lti(sr, si, carry, steps, e)
            sr_ref[pl.ds(r0, RC), :] = sr
            si_ref[pl.ds(r0, RC), :] = si
            y_ref[pl.ds(r0, RC), :] = dv * u + (_mm_nt(sr, cr) - _mm_nt(si, ci))
            return carry

        zero = jnp.zeros((1, S5_TW), F32)
        lax.fori_loop(0, L // RC, step, (zero, zero))

    in_tile, st, pg, plb, _, _, _ = _s5_specs(L, layer)
    u_tile = pl.BlockSpec((L, LANE), lambda t: (0, C_S5U // LANE + t))
    return pl.pallas_call(
        body, name="s5_fwd", grid=(N_S5_T,),
        in_specs=[u_tile, pg, pg, plb, plb, pg, pg, _layer_row_tile(layer)],
        out_specs=[in_tile, st, st],
        out_shape=[_S((L, S5_W)), _S((L, S5_N)), _S((L, S5_N))],
        compiler_params=_params(1))(z, bb_re, bb_im, lb_re, lb_im, c_re, c_im, dvec)


def _s5_bwd(dy0, z, s_re, s_im, bb_re, bb_im, lb_re, lb_im, c_re, c_im, dvec, layer, token=None):
    L = z.shape[0]
    extra, extra_specs = _after(token)

    def body(dy_ref, u_ref, sr_ref, si_ref, bbr_ref, bbi_ref, lr_ref, li_ref, cr_ref, ci_ref, d_ref, *rest):
        (du_ref, dbbr_out, dbbi_out, dlr_out, dli_out, dcr_out, dci_out, dd_ref,
         dbbr_ref, dbbi_ref, dcr_ref, dci_ref, dlr_ref, dli_ref) = rest[len(extra):]
        bbr, bbi = _bd8(bbr_ref[...]).astype(MXU), _bd8(bbi_ref[...]).astype(MXU)
        cr, ci = _bd8(cr_ref[...]).astype(MXU), _bd8(ci_ref[...]).astype(MXU)
        lr, li = _row8(lr_ref[...]), -_row8(li_ref[...])
        dv = d_ref[...]
        steps, e = _tile_powers(lr, li, reverse=True)
        for ref in (dbbr_ref, dbbi_ref, dlr_ref, dli_ref, dcr_ref, dci_ref, dd_ref):
            ref[...] = jnp.zeros_like(ref)
        nch = L // RC

        def step(k, carry):
            c = nch - 1 - k
            r0 = pl.multiple_of(c * RC, RC)
            dy = dy_ref[pl.ds(r0, RC), :]
            u = u_ref[pl.ds(r0, RC), :]
            dyb, ub = dy.astype(MXU), u.astype(MXU)
            sr, si = sr_ref[pl.ds(r0, RC), :], si_ref[pl.ds(r0, RC), :]
            dcr_ref[...] += _mm_tn(dyb, sr)
            dci_ref[...] -= _mm_tn(dyb, si)
            gr = jnp.dot(dyb, cr, preferred_element_type=F32)
            gi = -jnp.dot(dyb, ci, preferred_element_type=F32)
            gr, gi, carry = _scan_lti(gr, gi, carry, steps, e, reverse=True)
            pr_ = pltpu.roll(jnp.concatenate([_halo(sr_ref, c, r0), sr], axis=0), 1, 0)[8:, :]
            pi_ = pltpu.roll(jnp.concatenate([_halo(si_ref, c, r0), si], axis=0), 1, 0)[8:, :]
            dlr_ref[...] += _colsum(pr_ * gr + pi_ * gi)
            dli_ref[...] += _colsum(pr_ * gi - pi_ * gr)
            grb, gib = gr.astype(MXU), gi.astype(MXU)
            dbbr_ref[...] += _mm_tn(ub, grb)
            dbbi_ref[...] += _mm_tn(ub, gib)
            du_ref[pl.ds(r0, RC), :] = dv * dy + (_mm_nt(grb, bbr) + _mm_nt(gib, bbi))
            dd_ref[...] += _colsum(dy * u)
            return carry

        zero = jnp.zeros((1, S5_TW), F32)
        lax.fori_loop(0, nch, step, (zero, zero))
        dbbr_out[...], dbbi_out[...] = _bd8_diag(dbbr_ref[...]), _bd8_diag(dbbi_ref[...])
        dcr_out[...], dci_out[...] = _bd8_diag(dcr_ref[...]), _bd8_diag(dci_ref[...])
        dlr_out[...], dli_out[...] = _row8_split(dlr_ref[...]), _row8_split(dli_ref[...])

    in_tile, st, pg, plb, gg, glb, dv = _s5_specs(L, layer)
    u_tile = pl.BlockSpec((L, LANE), lambda t: (0, C_S5U // LANE + t))
    groups, rows = _S((N_S5_T, S5_GT, S5_H, S5_P)), _S((N_S5_T, S5_GT, S5_P))
    wide = pltpu.VMEM((LANE, S5_TW), F32)
    return pl.pallas_call(
        body, name="s5_bwd", grid=(N_S5_T,),
        in_specs=[in_tile, u_tile, st, st, pg, pg, plb, plb, pg, pg, _layer_row_tile(layer)] + extra_specs,
        out_specs=[in_tile, gg, gg, glb, glb, gg, gg, dv],
        out_shape=[_S((L, S5_W)), groups, groups, rows, rows, groups, groups, _S((1, S5_W))],
        scratch_shapes=[wide, wide, wide, wide, pltpu.VMEM((1, S5_TW), F32), pltpu.VMEM((1, S5_TW), F32)],
        compiler_params=_params(1))(dy0, z, s_re, s_im, bb_re, bb_im, lb_re, lb_im, c_re, c_im, dvec, *extra)


def _disc(ar, ai, ls):
    dt = jnp.exp(ls)
    mag = jnp.exp(ar * dt)
    lr = mag * jnp.cos(ai * dt)
    li = mag * jnp.sin(ai * dt)
    den = ar * ar + ai * ai
    cr = ((lr - 1.0) * ar + li * ai) / den
    ci = (li * ar - (lr - 1.0) * ai) / den
    return lr, li, cr, ci


def _s5_disc_fwd(ar, ai, ls, token=None):
    extra, extra_specs = _after(token)

    def body(ar_ref, ai_ref, ls_ref, *rest):
        lr_ref, li_ref, cr_ref, ci_ref = rest[len(extra):]
        lr, li, cr, ci = _disc(ar_ref[...], ai_ref[...], ls_ref[...])
        lr_ref[...], li_ref[...], cr_ref[...], ci_ref[...] = lr, li, cr, ci

    sh = _S(ar.shape)
    vm = pl.BlockSpec(memory_space=pltpu.VMEM)
    return pl.pallas_call(body, name="s5_disc_fwd", in_specs=[vm, vm, vm] + extra_specs, out_shape=[sh, sh, sh, sh])(
        ar, ai, ls, *extra)


def _s5_disc_bwd(ar, ai, ls, dlr, dli, dcr, dci):
    def body(ar_ref, ai_ref, ls_ref, dlr_ref, dli_ref, dcr_ref, dci_ref, dar_ref, dai_ref, dls_ref):
        _, vjp = jax.vjp(_disc, ar_ref[...], ai_ref[...], jnp.broadcast_to(ls_ref[...], ar_ref.shape))
        dar, dai, dls = vjp((dlr_ref[...], dli_ref[...], dcr_ref[...], dci_ref[...]))
        dar_ref[...], dai_ref[...] = dar, dai
        dls_ref[...] = jnp.sum(dls, axis=1, keepdims=True)

    return pl.pallas_call(body, name="s5_disc_bwd", out_shape=[_S(ar.shape), _S(ar.shape), _S(ls.shape)])(
        ar, ai, ls, dlr, dli, dcr, dci)


def _s5_bscale_fwd(cr, ci, br, bi):
    def body(cr_ref, ci_ref, br_ref, bi_ref, or_ref, oi_ref):
        or_ref[...], oi_ref[...] = _cmul(cr_ref[...], ci_ref[...], br_ref[...], bi_ref[...])

    return pl.pallas_call(body, name="s5_bscale_fwd", out_shape=[_S(br.shape), _S(br.shape)])(cr, ci, br, bi)


def _s5_bscale_bwd(cr, ci, br, bi, gr, gi):
    def body(cr_ref, ci_ref, br_ref, bi_ref, gr_ref, gi_ref, dbr_ref, dbi_ref, dcr_ref, dci_ref):
        cr_, ci_, br_, bi_, gr_, gi_ = (r[...] for r in (cr_ref, ci_ref, br_ref, bi_ref, gr_ref, gi_ref))
        dbr_ref[...] = cr_ * gr_ + ci_ * gi_
        dbi_ref[...] = cr_ * gi_ - ci_ * gr_
        dcr_ref[...] = jnp.sum(gr_ * br_ + gi_ * bi_, axis=1, keepdims=True)
        dci_ref[...] = jnp.sum(gi_ * br_ - gr_ * bi_, axis=1, keepdims=True)

    return pl.pallas_call(body, name="s5_bscale_bwd",
                          out_shape=[_S(br.shape), _S(br.shape), _S(cr.shape), _S(cr.shape)])(cr, ci, br, bi, gr, gi)


def _row(w):
    return pl.BlockSpec((TM, w), lambda i: (i, 0))


def _full(shape):
    return pl.BlockSpec(tuple(shape), lambda i: (0,) * len(shape))


def _gate_rows():
    return [pl.BlockSpec((TM, RG_W), lambda i: (i, C_RGG // RG_W))] + [
        pl.BlockSpec((TM, LANE), lambda i, k=k: (i, C_S5G // LANE + k)) for k in range(N_S5_T)]


def _p_rows(layer):
    return pl.BlockSpec((None, None, TM, PLE_D), lambda i: (layer, 0, i, 0))


DEPTH = 2


def _lrow(layer, width):
    return _full((DEPTH, width))


def _pick(ref, layer):
    return ref[layer:layer + 1, :]


def _post_fwd(x, hs, z, y0, p, w_glu, b_glu, w_out, g1, b1, ple_w, w_pg, b_pg, g2, b2, layer):
    L = x.shape[0]

    def body(x_ref, hs_ref, zg_ref, zs0_ref, zs1_ref, zs2_ref, y0_ref, p_ref, wg_ref, bg_ref, wo_ref, g1_ref, b1_ref, pw_ref,
             wpg_ref, bpg_ref, g2_ref, b2_ref, x2_ref, xh1_ref, xh2_ref, gt_ref, rstd1_ref, rstd2_ref):
        rg_gate = zg_ref[...]
        s5_gate = jnp.concatenate([zs0_ref[...], zs1_ref[...], zs2_ref[...]], axis=1)
        rg_y = hs_ref[...] * _silu_and_grad(rg_gate)[0]
        y1 = _gelu(y0_ref[...])
        gl = _sigmoid(_mm(y1, wg_ref[...]) + _pick(bg_ref, layer))
        s5_y = (y1 * gl) * _silu_and_grad(s5_gate)[0]
        mix = _mm(jnp.concatenate([rg_y.astype(MXU), s5_y.astype(MXU)], axis=1), wo_ref[...])
        t1 = ALPHA * x_ref[...] + mix
        x1, xh1, rstd1 = _ln_fwd(t1, _pick(g1_ref, layer), _pick(b1_ref, layer))
        q = _mm(p_ref[...], pw_ref[...])
        gt = _sigmoid(_mm(x1, wpg_ref[...]) + _pick(bpg_ref, layer))
        t2 = ALPHA * x1 + q * gt
        x2, xh2, rstd2 = _ln_fwd(t2, _pick(g2_ref, layer), _pick(b2_ref, layer))
        x2_ref[...], xh1_ref[...], xh2_ref[...], gt_ref[...] = x2, xh1, xh2, gt
        rstd1_ref[...], rstd2_ref[...] = rstd1, rstd2

    vec = _lrow(layer, D_MODEL)
    return pl.pallas_call(
        body, name="post_fwd", grid=(L // TM,),
        in_specs=[_row(D_MODEL), _row(RG_W), *_gate_rows(), _row(S5_W), _p_rows(layer), _full((S5_W, S5_W)),
                  _lrow(layer, S5_W), _full((D_MODEL, D_MODEL)), vec, vec, _full((PLE_D, D_MODEL)), _full((D_MODEL, D_MODEL)),
                  vec, vec, vec],
        out_specs=[_row(D_MODEL)] * 4 + [_row(1)] * 2, out_shape=[_S((L, D_MODEL))] * 4 + [_S((L, 1))] * 2,
        compiler_params=_params(1))(x, hs, z, z, z, z, y0, p, w_glu, b_glu, w_out, g1, b1, ple_w, w_pg, b_pg, g2, b2)


def _post_bwd_a(dx2_or_target, is_top, xh2, xh1, rstd2, rstd1, gt, p, ple_w, w_pg, g1, b1, g2, b2, layer, token=None):
    L = xh1.shape[0]
    extra, extra_specs = _after(token)

    def body(d_ref, xh2_ref, xh1_ref, rstd2_ref, rstd1_ref, gt_ref, p_ref, pw_ref, wpg_ref, g1_ref, b1_ref, g2_ref,
             b2_ref, *rest):
        (dt1_ref, dpw_out, dwpg_out, dbpg_ref, dg1_ref, db1_ref, dg2_ref, db2_ref, loss_ref, dpw_ref,
         dwpg_ref) = rest[len(extra):]
        @pl.when(pl.program_id(0) == 0)
        def _():
            for ref in (dpw_ref, dwpg_ref, dbpg_ref, dg1_ref, db1_ref, dg2_ref, db2_ref, loss_ref):
                ref[...] = jnp.zeros_like(ref)

        g1, g2 = _pick(g1_ref, layer), _pick(g2_ref, layer)
        xh1, xh2, rstd1, rstd2 = xh1_ref[...], xh2_ref[...], rstd1_ref[...], rstd2_ref[...]
        x1 = xh1 * g1 + _pick(b1_ref, layer)
        if is_top:
            err = (xh2 * g2 + _pick(b2_ref, layer)) - d_ref[...]
            loss_ref[...] += _colsum(err * err)
            dx2 = err * (1.0 / D_MODEL)
        else:
            dx2 = d_ref[...]
        p = p_ref[...]
        q, gt = _mm(p, pw_ref[...]), gt_ref[...]
        dg2_ref[...] += _colsum(dx2 * xh2)
        db2_ref[...] += _colsum(dx2)
        dt2 = _ln_bwd(dx2, xh2, rstd2, g2)
        dq = dt2 * gt
        dgpre = (dt2 * q) * gt * (1.0 - gt)
        dpw_ref[...] += _mm_tn(p, dq)
        dwpg_ref[...] += _mm_tn(x1, dgpre)
        dbpg_ref[...] += _colsum(dgpre)
        dx1 = ALPHA * dt2 + _mm_nt(dgpre, wpg_ref[...])
        dg1_ref[...] += _colsum(dx1 * xh1)
        db1_ref[...] += _colsum(dx1)
        dt1_ref[...] = _ln_bwd(dx1, xh1, rstd1, g1)

        @pl.when(pl.program_id(0) == L // TM - 1)
        def _():
            dpw_out[...] = dpw_ref[...].astype(WIRE)
            dwpg_out[...] = dwpg_ref[...].astype(WIRE)

    vec, lvec = _full((1, D_MODEL)), _lrow(layer, D_MODEL)
    return pl.pallas_call(
        body, name="post_bwd_a_top" if is_top else "post_bwd_a", grid=(L // TM,),
        in_specs=[_row(D_MODEL), _row(D_MODEL), _row(D_MODEL), _row(1), _row(1), _row(D_MODEL), _p_rows(layer),
                  _full((PLE_D, D_MODEL)), _full((D_MODEL, D_MODEL)), lvec, lvec, lvec, lvec] + extra_specs,
        out_specs=[_row(D_MODEL), _full((PLE_D, D_MODEL)), _full((D_MODEL, D_MODEL)), vec, vec, vec, vec, vec, vec],
        out_shape=[_S((L, D_MODEL)), _S((PLE_D, D_MODEL), WIRE), _S((D_MODEL, D_MODEL), WIRE)] + [_S((1, D_MODEL))] * 6,
        scratch_shapes=[pltpu.VMEM((PLE_D, D_MODEL), F32), pltpu.VMEM((D_MODEL, D_MODEL), F32)],
        compiler_params=_params(1))(dx2_or_target, xh2, xh1, rstd2, rstd1, gt, p, ple_w, w_pg, g1, b1, g2, b2, *extra)


def _post_bwd_b(dt1, z, hs, y0, w_out, w_glu, b_glu, layer):
    L = dt1.shape[0]

    def body(dt1_ref, zg_ref, zs0_ref, zs1_ref, zs2_ref, hs_ref, y0_ref, wo_ref, wg_ref, bg_ref,
             dhs_ref, dy0_ref, dzg_ref, dwo_out, dwg_out, dbg_ref, dwo_ref, dwg_ref):
        @pl.when(pl.program_id(0) == 0)
        def _():
            for ref in (dwo_ref, dwg_ref, dbg_ref):
                ref[...] = jnp.zeros_like(ref)

        dt1b = dt1_ref[...].astype(MXU)
        dm = _mm_nt(dt1b, wo_ref[...])
        d_rgy, d_s5y = dm[:, :RG_W], dm[:, RG_W:]
        rg_gate = zg_ref[...]
        s5_gate = jnp.concatenate([zs0_ref[...], zs1_ref[...], zs2_ref[...]], axis=1)
        hs = hs_ref[...]
        sl, dsl = _silu_and_grad(rg_gate)
        dhs_ref[...] = d_rgy * sl
        dzg_ref[:, :RG_W] = d_rgy * hs * dsl
        y0 = y0_ref[...]
        y1 = _gelu(y0)
        gl = _sigmoid(_mm(y1, wg_ref[...]) + _pick(bg_ref, layer))
        y2 = y1 * gl
        sl2, dsl = _silu_and_grad(s5_gate)
        m = jnp.concatenate([(hs * sl).astype(MXU), (y2 * sl2).astype(MXU)], axis=1)
        dwo_ref[...] += _mm_tn(m, dt1b)
        dy2 = d_s5y * sl2
        dzg_ref[:, RG_W:] = d_s5y * y2 * dsl
        dglpre = (dy2 * y1) * gl * (1.0 - gl)
        dwg_ref[...] += _mm_tn(y1, dglpre)
        dbg_ref[...] += _colsum(dglpre)
        dy1 = dy2 * gl + _mm_nt(dglpre, wg_ref[...])
        dy0_ref[...] = dy1 * _gelu_grad(y0)

        @pl.when(pl.program_id(0) == L // TM - 1)
        def _():
            dwo_out[...] = dwo_ref[...].astype(WIRE)
            dwg_out[...] = dwg_ref[...].astype(WIRE)

    return pl.pallas_call(
        body, name="post_bwd_b", grid=(L // TM,),
        in_specs=[_row(D_MODEL), *_gate_rows(), _row(RG_W), _row(S5_W), _full((D_MODEL, D_MODEL)),
                  _full((S5_W, S5_W)), _lrow(layer, S5_W)],
        out_specs=[_row(RG_W), _row(S5_W), _row(D_MODEL), _full((D_MODEL, D_MODEL)), _full((S5_W, S5_W)), _full((1, S5_W))],
        out_shape=[_S((L, RG_W)), _S((L, S5_W)), _S((L, D_MODEL)), _S((D_MODEL, D_MODEL), WIRE), _S((S5_W, S5_W), WIRE),
                   _S((1, S5_W))],
        scratch_shapes=[pltpu.VMEM((D_MODEL, D_MODEL), F32), pltpu.VMEM((S5_W, S5_W), F32)],
        compiler_params=_params(1))(dt1, z, z, z, z, hs, y0, w_out, w_glu, b_glu)


def _adamw(parts, w, m, v, token=None):
    nl = len(parts)
    extra, extra_specs = _after(token)
    n, R, C = parts[0].shape
    tr = R
    for cand in (512, 256, 128, 64, 32, 16, 8):
        if R % cand == 0 and n * cand * C * 4 <= 4 * 1024 * 1024:
            tr = cand
            break
    nblk = R // tr

    def body(*refs):
        p_refs = refs[:nl]
        w_ref, m_ref, v_ref = refs[nl:nl + 3]
        g_ref, d_ref, nm_ref, nv_ref = refs[nl + 3 + len(extra):]
        layer = pl.program_id(0)
        g = None
        for li, p_ref in enumerate(p_refs):
            s = p_ref[0].astype(F32)
            for k in range(1, n):
                s = s + p_ref[k].astype(F32)
            g = s if g is None else jnp.where(layer == li, s, g)
        nm = B1 * m_ref[...] + (1.0 - B1) * g
        nv = B2 * v_ref[...] + (1.0 - B2) * (g * g)
        d_ref[...] = (-LR) * ((nm / BC1) / (jnp.sqrt(nv / BC2) + EPS) + WD * w_ref[...])
        g_ref[...], nm_ref[...], nv_ref[...] = g, nm, nv

    def part_spec(li):
        return pl.BlockSpec((n, tr, C), lambda l, i: (0, jnp.where(l == li, i, jnp.where(l < li, 0, nblk - 1)), 0))

    blk = pl.BlockSpec((tr, C), lambda l, i: (l * nblk + i, 0))
    return pl.pallas_call(
        body, name="adamw", grid=(nl, nblk),
        in_specs=[part_spec(li) for li in range(nl)] + [blk, blk, blk] + extra_specs,
        out_specs=[blk] * 4, out_shape=[_S((nl * R, C))] * 4, compiler_params=_params(2))(*parts, w, m, v, *extra)


def _adamw_sharded(names, recv, w, m, v, name, token=None):
    n, nl = len(names), len(recv)
    extra, extra_specs = _after(token)
    n_in = n * (nl + 3)

    def body(*refs):
        outs = refs[n_in + len(extra):]
        for j in range(n):
            w_ref, m_ref, v_ref = (refs[(nl + t) * n + j] for t in range(3))
            g_ref, d_ref, nm_ref, nv_ref = (outs[t * n + j] for t in range(4))
            for l in range(nl):
                p_ref = refs[l * n + j]
                g = p_ref[0].astype(F32)
                for q in range(1, N_DEV):
                    g = g + p_ref[q].astype(F32)
                nm = B1 * m_ref[l] + (1.0 - B1) * g
                nv = B2 * v_ref[l] + (1.0 - B2) * (g * g)
                d_ref[l] = (-LR) * ((nm / BC1) / (jnp.sqrt(nv / BC2) + EPS) + WD * w_ref[l])
                g_ref[l], nm_ref[l], nv_ref[l] = g, nm, nv

    ins = [r[k] for r in recv for k in names] + [t[k] for t in (w, m, v) for k in names]
    vm = pl.BlockSpec(memory_space=pltpu.VMEM)
    outs = pl.pallas_call(body, name=name, in_specs=[vm] * n_in + extra_specs,
                          out_shape=[_S(w[k].shape) for _ in range(4) for k in names],
                          compiler_params=pltpu.CompilerParams(vmem_limit_bytes=VMEM_LIMIT))(*ins, *extra)
    return [{k: outs[t * n + j] for j, k in enumerate(names)} for t in range(4)]


def _adamw_packed(names, packed, w, m, v, name):
    n = len(names)
    starts, _ = _packed_starts([w[k].shape for k in names])

    def body(p_ref, *refs):
        for j in range(n):
            w_ref, m_ref, v_ref, g_ref, d_ref, nm_ref, nv_ref = (refs[k * n + j] for k in range(7))
            gj = _packed_get(p_ref, starts[j], w_ref.shape)
            nm = B1 * m_ref[...] + (1.0 - B1) * gj
            nv = B2 * v_ref[...] + (1.0 - B2) * (gj * gj)
            d_ref[...] = (-LR) * ((nm / BC1) / (jnp.sqrt(nv / BC2) + EPS) + WD * w_ref[...])
            g_ref[...], nm_ref[...], nv_ref[...] = gj, nm, nv

    ins = [t[k] for t in (w, m, v) for k in names]
    outs = pl.pallas_call(body, name=name, out_shape=[_S(w[k].shape) for _ in range(4) for k in names],
                          compiler_params=pltpu.CompilerParams(vmem_limit_bytes=VMEM_LIMIT))(packed, *ins)
    return [{k: outs[t * n + j] for j, k in enumerate(names)} for t in range(4)]


def _me():
    return lax.axis_index("x"), lax.axis_index("y"), lax.axis_index("c")


def _lin(dev):
    return 4 * dev[0] + 2 * dev[1] + dev[2]


def _blk(ref, axis, size, idx):
    nd = len(ref.shape)
    start = idx * size
    if axis == nd - 1 and size % LANE == 0:
        start = pl.multiple_of(start, LANE)
    elif axis == nd - 2 and size % 16 == 0:
        start = pl.multiple_of(start, 16)
    ix = [slice(None)] * nd
    ix[axis] = pl.ds(start, size)
    return ref.at[tuple(ix)]


HBM_SPEC = pl.BlockSpec(memory_space=pltpu.HBM)
SEM_SPEC = pl.BlockSpec(memory_space=pltpu.SEMAPHORE)
EFFECT = pltpu.SideEffectType.DATAFLOW_SIDE_EFFECTING


def _peers(x, y, c):
    flip = lambda v, f: 1 - v if f else v
    return [(flip(x, k & 4), flip(y, k & 2), flip(c, k & 1)) for k in range(1, N_DEV)]


def _land_shape(mode, s, axis):
    if mode == "gather":
        return s.shape[:axis] + (N_DEV * s.shape[axis],) + s.shape[axis + 1:]
    return (N_DEV,) + s.shape[:axis] + (s.shape[axis] // N_DEV,) + s.shape[axis + 1:]


def _src_view(mode, ref, axis, peer):
    return ref if mode == "gather" else _blk(ref, axis, ref.shape[axis] // N_DEV, peer)


def _dst_view(mode, land, axis, sender):
    return _blk(land, axis, land.shape[axis] // N_DEV, sender) if mode == "gather" else land.at[sender]


def _blocks(mode, land, axis, k):
    if mode == "gather":
        ix = [slice(None)] * len(land.shape)
        ix[axis] = pl.ds(0, k * (land.shape[axis] // N_DEV))
        return land.at[tuple(ix)]
    return land.at[pl.ds(0, k)]


ARRIVALS = {None: N_DEV - 1, "near": 4, "relay": 3}


def _routes(route, x, y, c):
    me, sibling = (x, y, c), (x, y, 1 - c)
    chips = [(1 - x, y), (x, 1 - y), (1 - x, 1 - y)]
    if route == "near":
        return [(me, sibling)] + [(me, (*chip, c)) for chip in chips]
    if route == "relay":
        return [((*chip, c), sibling) for chip in chips]
    return [(me, peer) for peer in _peers(x, y, c)]


def _place_own(mode, srcs, axes, name, after=None):
    n = len(srcs)
    extra, extra_specs = _after(after)

    def body(me_ref, *refs):
        for a in range(n):
            out = refs[n + len(extra) + a]
            out[...] = refs[a][...].reshape(out.shape)

    def at_me(shape, axis):
        return lambda i, me: tuple(me[0] if d == axis else 0 for d in range(len(shape)))

    in_specs, out_specs = [], []
    for s, axis in zip(srcs, axes):
        if mode == "gather":
            in_specs.append(pl.BlockSpec(s.shape, lambda i, me, nd=len(s.shape): (0,) * nd))
            out_specs.append(pl.BlockSpec(s.shape, at_me(s.shape, axis)))
        else:
            blk = s.shape[:axis] + (s.shape[axis] // N_DEV,) + s.shape[axis + 1:]
            in_specs.append(pl.BlockSpec(blk, at_me(blk, axis)))
            out_specs.append(pl.BlockSpec((1,) + blk, at_me((1,) + blk, 0)))
    me = _lin(_me()).astype(jnp.int32).reshape(1)
    return pl.pallas_call(
        body, name=name, out_shape=[_S(_land_shape(mode, s, a), s.dtype) for s, a in zip(srcs, axes)],
        grid_spec=pltpu.PrefetchScalarGridSpec(num_scalar_prefetch=1, grid=(1,), in_specs=in_specs + extra_specs,
                                               out_specs=out_specs),
        compiler_params=_params(1))(me, *srcs, *extra)


def _place_shards(shards, layers, axes, dtypes, name, after=None):
    n = len(shards)
    extra, extra_specs = _after(after)

    def body(me_ref, *refs):
        for a in range(n):
            out = refs[n + len(extra) + a]
            out[...] = refs[a][...].astype(out.dtype)

    in_specs, out_specs, out_shape = [], [], []
    for s, layer, axis, dt in zip(shards, layers, axes, dtypes):
        shape = s.shape if layer is None else s.shape[1:]
        nd = len(shape)
        if layer is None:
            in_specs.append(pl.BlockSpec(shape, lambda i, me, nd=nd: (0,) * nd))
        else:
            in_specs.append(pl.BlockSpec((None,) + shape, lambda i, me, nd=nd, layer=layer: (layer,) + (0,) * nd))
        out_specs.append(pl.BlockSpec(shape, lambda i, me, nd=nd, axis=axis: tuple(me[0] if d == axis else 0 for d in range(nd))))
        out_shape.append(_S(shape[:axis] + (N_DEV * shape[axis],) + shape[axis + 1:], dt))
    me = _lin(_me()).astype(jnp.int32).reshape(1)
    return pl.pallas_call(
        body, name=name, out_shape=out_shape,
        grid_spec=pltpu.PrefetchScalarGridSpec(num_scalar_prefetch=1, grid=(1,), in_specs=in_specs + extra_specs,
                                               out_specs=out_specs),
        compiler_params=_params(1))(me, *shards, *extra)


def _push_start(mode, srcs, lands, axes, name, route=None):
    n, ns = len(lands), len(srcs)

    def body(*refs):
        src_refs, land_refs = refs[:ns], refs[ns:ns + n]
        send_sems, recv_sems = refs[ns + n], refs[ns + n + 1]
        token = refs[-1]
        x, y, c = _me()
        for a in range(n):
            for block, peer in _routes(route, x, y, c):
                there = _dst_view(mode, land_refs[a], axes[a], _lin(block))
                pltpu.make_async_remote_copy(
                    src_ref=_src_view(mode, src_refs[a], axes[a], _lin(peer)) if ns else there, dst_ref=there,
                    send_sem=send_sems.at[a], recv_sem=recv_sems.at[a], device_id=peer, device_id_type=MESH).start()
        token[...] = jnp.zeros_like(token)

    hbm = lambda s: pltpu.HBM(s.shape, s.dtype)
    outs = pl.pallas_call(
        body, name=name,
        out_shape=(pltpu.SemaphoreType.DMA((n,)), pltpu.SemaphoreType.DMA((n,)), *[hbm(s) for s in srcs], *[hbm(s) for s in lands],
                   _S((SUB, LANE))),
        in_specs=[HBM_SPEC] * (ns + n),
        out_specs=(SEM_SPEC, SEM_SPEC, *[HBM_SPEC] * (ns + n), pl.BlockSpec(memory_space=pltpu.VMEM)),
        input_output_aliases={i: 2 + i for i in range(ns + n)},
        compiler_params=pltpu.CompilerParams(has_side_effects=EFFECT),
    )(*[pltpu.with_memory_space_constraint(s, pltpu.HBM) for s in list(srcs) + list(lands)])
    return outs[0], outs[1], outs[2:2 + ns], outs[2 + ns:2 + ns + n], outs[-1]


def _push_wait(mode, send_sems, recv_sems, srcs, lands, axes, after, name, first=0, route=None):
    n, ns = len(lands), len(srcs)
    after = list(after) if isinstance(after, (list, tuple)) else [after]

    def body(*refs):
        land_refs = refs[ns:ns + n]
        send_sems, recv_sems = refs[ns + n], refs[ns + n + 1]
        x, y, c = _me()
        for a in range(n):
            seven = _blocks(mode, land_refs[a], axes[a], ARRIVALS[route])
            cp = pltpu.make_async_remote_copy(src_ref=seven, dst_ref=seven, send_sem=send_sems.at[first + a],
                                              recv_sem=recv_sems.at[first + a],
                                              device_id=(x, y, 1 - c), device_id_type=MESH)
            cp.wait_send()
            cp.wait_recv()

    hbm = lambda s: pltpu.HBM(s.shape, s.dtype)
    outs = pl.pallas_call(
        body, name=name, out_shape=tuple(hbm(s) for s in list(srcs) + list(lands)),
        in_specs=[HBM_SPEC] * (ns + n) + [SEM_SPEC, SEM_SPEC] + [ANY] * len(after), out_specs=tuple([HBM_SPEC] * (ns + n)),
        input_output_aliases={i: i for i in range(ns + n)},
        compiler_params=pltpu.CompilerParams(has_side_effects=EFFECT),
    )(*srcs, *lands, send_sems, recv_sems, *after)
    return outs[ns:]


def _sum_parts(parts):
    n, R, C = parts.shape

    def body(p_ref, o_ref):
        g = p_ref[0]
        for k in range(1, n):
            g = g + p_ref[k]
        o_ref[...] = g

    return pl.pallas_call(body, name="sum_parts", out_shape=_S((R, C)))(parts)


SMALL =['conv_b', 'rg_wa', 'rg_ba', 'rg_wx', 'rg_bx', 'rg_lambda', 's5_a_re', 's5_a_im', 's5_b_re', 's5_b_im',
         's5_c_re', 's5_c_im', 's5_d', 's5_log_step', 's5_b_glu', 'ln1_g', 'ln1_b', 'ple_gate_b', 'ln2_g', 'ln2_b']
WEIGHTS = ['w_in', 'conv_w', 'conv_b', 'rg_wa', 'rg_ba', 'rg_wx', 'rg_bx', 'rg_lambda', 's5_a_re', 's5_a_im', 's5_b_re',
           's5_b_im', 's5_c_re', 's5_c_im', 's5_d', 's5_log_step', 's5_w_glu', 's5_b_glu', 'w_out', 'ln1_g', 'ln1_b',
           'ple_w', 'ple_gate_w', 'ple_gate_b', 'ln2_g', 'ln2_b']
PACK_ROWS_MULT = 64


STORED = {'s5_b_re': (2, 3), 's5_b_im': (2, 3), 's5_d': (1, 2)}


def _stored(k, a):
    return jnp.swapaxes(a, *STORED[k]) if k in STORED else a


def _two_d(a):
    return a.reshape(-1, a.shape[-1])


def _up8(n):
    return -(-n // SUB) * SUB


def _halves_fit(shape):
    return 2 * shape[1] == LANE and shape[0] % (2 * SUB) == 0


def _packed_rows(shape):
    R, C = shape
    if C % LANE == 0:
        return (C // LANE) * _up8(R)
    return R // 2 if _halves_fit(shape) else _up8(R)


def _packed_put(out_ref, r0, a):
    R, C = a.shape
    if C % LANE == 0:
        for j in range(C // LANE):
            out_ref[r0 + j * _up8(R):r0 + j * _up8(R) + R, :] = a[:, j * LANE:(j + 1) * LANE]
    elif _halves_fit(a.shape):
        out_ref[r0:r0 + R // 2, :] = jnp.concatenate([a[:R // 2], a[R // 2:]], axis=1)
    else:
        out_ref[r0:r0 + R, :C] = a


def _packed_get(ref, r0, shape):
    R, C = shape
    if C % LANE == 0:
        return jnp.concatenate([ref[r0 + j * _up8(R):r0 + j * _up8(R) + R, :] for j in range(C // LANE)], axis=1)
    if _halves_fit(shape):
        both = ref[r0:r0 + R // 2, :]
        return jnp.concatenate([both[:, :C], both[:, C:]], axis=0)
    return ref[r0:r0 + R, :C]


def _packed_starts(shapes):
    starts = [0]
    for s in shapes:
        starts.append(starts[-1] + _packed_rows(s))
    return starts[:-1], starts[-1] + (-starts[-1] % PACK_ROWS_MULT)


def _pack(tree, scalar):
    arrays = [_two_d(tree[k]) for k in SMALL] + [scalar.reshape(1, 1)]
    starts, rows = _packed_starts([a.shape for a in arrays])

    def body(*refs):
        out_ref = refs[-1]
        out_ref[...] = jnp.zeros_like(out_ref)
        for r0, a_ref in zip(starts, refs[:-1]):
            _packed_put(out_ref, r0, a_ref[...])

    return pl.pallas_call(body, name="pack_small", out_shape=_S((rows, LANE)))(*arrays)


class _NoHooks:
    token = None
    first_token = None

    def first_weights(self, full, after):
        return full

    def layer_start(self, i, W, after):
        return W

    def late_weights(self, i, W, after):
        return W

    def post_done(self, i, g):
        return None

    def smalls_done(self, grads, loss):
        self.small = _small_grads(grads, self.res)
        return None

    def w_in_done(self, i, g):
        return None

    def layer_done(self, i, g, dx):
        return None


def _local_grads(x, p, target, W, disc, hooks):
    depth = 2
    saved = []
    for i in range(depth):
        if i > 0:
            W = hooks.layer_start(i, W, x)
        w = W[i]
        z = _inproj_fwd(x, w['w_in'], hooks.token if i == 0 else None)
        hs, *gates = _rg_fwd(z, w['conv_w'], w['conv_b'], w['wa_bd'], w['wx_bd'], w['rg_ba'], w['rg_bx'], w['rg_lambda'], i)
        d = disc[i]
        y0, s_re, s_im = _s5_fwd(z, d['bb_re'], d['bb_im'], d['lb_re'], d['lb_im'], d['c_re'], d['c_im'], w['s5_d'], i)
        W = hooks.late_weights(i, W, y0)
        w = W[i]
        x2, *norms = _post_fwd(x, hs, z, y0, p, w['s5_w_glu'], w['s5_b_glu'], w['w_out'], w['ln1_g'], w['ln1_b'],
                               w['ple_w'], w['ple_gate_w'], w['ple_gate_b'], w['ln2_g'], w['ln2_b'], i)
        saved.append((x, z, hs, gates, y0, s_re, s_im, norms))
        x = x2

    grads = [None] * depth
    dx = target
    loss = None
    token = None
    for i in reversed(range(depth)):
        w, d = W[i], disc[i]
        xin, z, hs, gates, y0, s_re, s_im, (xh1, xh2, gt, rstd1, rstd2) = saved[i]
        g = {}
        (dt1, g['ple_w'], g['ple_gate_w'], g['ple_gate_b'], g['ln1_g'], g['ln1_b'], g['ln2_g'], g['ln2_b'], lrow) = _post_bwd_a(
            dx, i == depth - 1, xh2, xh1, rstd2, rstd1, gt, p, w['ple_w'], w['ple_gate_w'], w['ln1_g'], w['ln1_b'],
            w['ln2_g'], w['ln2_b'], i, token)
        if i == depth - 1:
            loss = 0.5 / D_MODEL * jnp.sum(lrow)
        dhs, dy0, dzg, g['w_out'], g['s5_w_glu'], g['s5_b_glu'] = _post_bwd_b(dt1, z, hs, y0, w['w_out'], w['s5_w_glu'],
                                                                           w['s5_b_glu'], i)
        (dzu, g['bb_re'], g['bb_im'], g['lb_re'], g['lb_im'], g['c_re'], g['c_im'], g['s5_d']) = _s5_bwd(
            dy0, z, s_re, s_im, d['bb_re'], d['bb_im'], d['lb_re'], d['lb_im'], d['c_re'], d['c_im'], w['s5_d'], i,
            hooks.post_done(i, g))
        (dzx, g['conv_w'], g['conv_b'], g['wa_bd'], g['wx_bd'], g['rg_ba'], g['rg_bx'], g['rg_lambda']) = _rg_bwd(
            dhs, z, hs, gates, w['conv_w'], w['wa_bd'], w['wx_bd'], w['rg_lambda'], i)
        if i == 0:
            g['w_in'] = _inproj_bwd_dw(xin, dzx, dzg, dzu, hooks.smalls_done([g, grads[1]], loss))
            dx = _inproj_bwd_dx(dt1, dzx, dzg, dzu, w['w_in'], hooks.w_in_done(i, g))
        else:
            dx, g['w_in'] = _inproj_bwd(dt1, xin, dzx, dzg, dzu, w['w_in'])
        grads[i] = g
        token = hooks.layer_done(i, g, dx)
    return loss, dx, grads


def _s5_layouts_fwd(s5_a_re, s5_a_im, s5_log_step, s5_b_re, s5_b_im, s5_c_re, s5_c_im, token=None):
    depth = s5_a_re.shape[0]
    ar, ai = s5_a_re.reshape(depth * 24, S5_P), s5_a_im.reshape(depth * 24, S5_P)
    ls = s5_log_step.reshape(depth * 24, 1)
    lr, li, cr, ci = _s5_disc_fwd(ar, ai, ls, token)
    per_group = lambda a: a.reshape(depth * 24, 1, S5_P)
    as_c = lambda b: jnp.swapaxes(b, 2, 3).reshape(depth * 24, S5_H, S5_P)
    res = (ar, ai, ls, per_group(cr), per_group(ci), as_c(s5_b_re), as_c(s5_b_im))
    bbr, bbi = _s5_bscale_fwd(*res[3:])
    tiles = lambda a: a.reshape(depth * N_S5_T, S5_GT, S5_H, S5_P)
    rows = lambda a: a.reshape(depth * N_S5_T, S5_GT, S5_P)
    disc = dict(bb_re=tiles(bbr), bb_im=tiles(bbi), lb_re=rows(lr), lb_im=rows(li), c_re=tiles(s5_c_re), c_im=tiles(s5_c_im))
    return [disc] * depth, res


def _s5_layouts_bwd(grads, res):
    ar, ai, ls, cr, ci, br, bi = res
    depth = len(grads)
    stack = lambda k, shape: jnp.stack([g[k] for g in grads]).reshape(shape)
    groups, shape_c = (depth * 24, S5_H, S5_P), (depth, 24, S5_H, S5_P)
    dbr, dbi, dcr, dci = _s5_bscale_bwd(cr, ci, br, bi, stack('bb_re', groups), stack('bb_im', groups))
    gp = (depth * 24, S5_P)
    dar, dai, dls = _s5_disc_bwd(ar, ai, ls, stack('lb_re', gp), stack('lb_im', gp), dcr.reshape(gp), dci.reshape(gp))
    return dict(
        s5_a_re=dar.reshape(depth, 24, S5_P), s5_a_im=dai.reshape(depth, 24, S5_P), s5_log_step=dls.reshape(depth, 24),
        s5_b_re=dbr.reshape(shape_c), s5_b_im=dbi.reshape(shape_c),
        s5_c_re=stack('c_re', shape_c), s5_c_im=stack('c_im', shape_c))


LATE = ('w_out', 'ple_w', 'ple_gate_w', 's5_w_glu')


ROWS = ('conv_b', 'rg_ba', 'rg_bx', 'rg_lambda', 's5_d', 's5_b_glu', 'ln1_g', 'ln1_b', 'ple_gate_b', 'ln2_g', 'ln2_b')


def _shared_weights(full):
    shared = {k: full[k] for k in ROWS}
    shared.update(conv_w=full['conv_w'], wa_bd=full['rg_wa'], wx_bd=full['rg_wx'], s5_d=full['s5_d'].reshape(DEPTH, 1, S5_W))
    return shared


def _layer_weights(full, shared, i):
    return dict(shared, w_in=full['w_in'][i])


class _AllLocal(_NoHooks):
    def __init__(self, full):
        self.full = full

    def late_weights(self, i, W, after):
        W[i].update({k: self.full[k][i] for k in LATE})
        return W


def _full_grads(full, x, p, target, hooks=None):
    hooks = hooks or _AllLocal(full)
    disc, res = _s5_layouts_fwd(full['s5_a_re'], full['s5_a_im'], full['s5_log_step'], full['s5_b_re'], full['s5_b_im'],
                                full['s5_c_re'], full['s5_c_im'], hooks.first_token)
    full = hooks.first_weights(full, disc[-1]['bb_im'])
    shared = _shared_weights(full)
    W = [_layer_weights(full, shared, i) for i in range(2)]
    hooks.res = res
    loss, gx, grads = _local_grads(x, p, target, W, disc, hooks)
    out = dict(hooks.small)
    for k in SHARD_AXIS:
        out[k] = [g[k] for g in grads]
    return loss, gx, out


def _small_grads(grads, res):
    stack = lambda f: jnp.stack([f(g) for g in grads])
    out = _s5_layouts_bwd(grads, res)
    out['conv_w'] = stack(lambda g: g['conv_w'])
    for k in ('conv_b', 'rg_ba', 'rg_bx', 'rg_lambda', 's5_b_glu', 'ln1_g', 'ln1_b', 'ple_gate_b', 'ln2_g', 'ln2_b'):
        out[k] = stack(lambda g: g[k][0])
    out['s5_d'] = _stored('s5_d', stack(lambda g: g['s5_d'][0]).reshape(2, 24, 16))
    out['rg_wa'] = stack(lambda g: g['wa_bd'])
    out['rg_wx'] = stack(lambda g: g['wx_bd'])
    return out


SHARD_AXIS = {'w_in': 2, 'w_out': 1, 'ple_w': 2, 'ple_gate_w': 1, 's5_w_glu': 1}


def kernel(x, p, w_in, conv_w, conv_b, rg_wa, rg_ba, rg_wx, rg_bx, rg_lambda, s5_a_re, s5_a_im, s5_b_re, s5_b_im, s5_c_re, s5_c_im, s5_d, s5_log_step, s5_w_glu, s5_b_glu, w_out, ln1_g, ln1_b, ple_w, ple_gate_w, ple_gate_b, ln2_g, ln2_b, loss_target, m_w_in, m_conv_w, m_conv_b, m_rg_wa, m_rg_ba, m_rg_wx, m_rg_bx, m_rg_lambda, m_s5_a_re, m_s5_a_im, m_s5_b_re, m_s5_b_im, m_s5_c_re, m_s5_c_im, m_s5_d, m_s5_log_step, m_s5_w_glu, m_s5_b_glu, m_w_out, m_ln1_g, m_ln1_b, m_ple_w, m_ple_gate_w, m_ple_gate_b, m_ln2_g, m_ln2_b, v_w_in, v_conv_w, v_conv_b, v_rg_wa, v_rg_ba, v_rg_wx, v_rg_bx, v_rg_lambda, v_s5_a_re, v_s5_a_im, v_s5_b_re, v_s5_b_im, v_s5_c_re, v_s5_c_im, v_s5_d, v_s5_log_step, v_s5_w_glu, v_s5_b_glu, v_w_out, v_ln1_g, v_ln1_b, v_ple_w, v_ple_gate_w, v_ple_gate_b, v_ln2_g, v_ln2_b):
    local = dict(locals())
    w = {k: local[k] for k in WEIGHTS}
    mom = {k: local['m_' + k] for k in WEIGHTS}
    var = {k: local['v_' + k] for k in WEIGHTS}

    big = list(SHARD_AXIS)
    late_axes = [SHARD_AXIS[k] - 1 for k in LATE]
    pushed = {}

    groups = dict(first=(['w_in', 'conv_w'], [0, None], [1, 0]), l0=(list(LATE), [0] * len(LATE), late_axes),
                  l1=(['w_in'] + list(LATE), [1] * (1 + len(LATE)), [1] + late_axes))
    token = None
    for key, members in (("first", ["first"]), ("rest", ["l0", "l1"])):
        names, layers, axes = (sum((groups[m][j] for m in members), []) for j in range(3))
        shards = [w[k] if layer is not None else w[k][None] for k, layer in zip(names, layers)]
        lands = _place_shards(shards, layers, axes, [WIRE if k in big else w[k].dtype for k in names],
                              "place_weights_" + key, token)
        pushed[key] = _push_start("gather", [], lands, axes, "push_weights_" + key, "near" if key == "first" else None)
        token = pushed[key][4]

    def await_weights(key, axes, after):
        s, first = pushed["rest"], 0 if key == "l0" else len(LATE)
        return _push_wait("gather", s[0], s[1], [], s[3][first:first + len(axes)], axes, after, "await_weights_" + key, first)

    def push_grads(key, g, names, axes):
        srcs = [g[k] for k in names]
        pushed[key] = _push_start("scatter", srcs, _place_own("scatter", srcs, axes, "place_grads_" + key), axes,
                                  "push_grads_" + key)
        return pushed[key][4]

    def await_grads(key, axes, after):
        s = pushed[key]
        return _push_wait("scatter", s[0], s[1], s[2], s[3], axes, after, "await_grads_" + key)

    class Overlap(_NoHooks):
        token = pushed["rest"][4]
        first_token = token

        def first_weights(self, full, after):
            s, axes = pushed["first"], [1, 0]
            near = _push_wait("gather", s[0], s[1], [], s[3], axes, after, "await_weights_near", route="near")
            s = _push_start("gather", [], near, axes, "relay_weights", "relay")
            w_in0, conv = _push_wait("gather", s[0], s[1], [], s[3], axes, s[4], "await_weights_relay", route="relay")
            return dict(full, w_in=[w_in0, None], conv_w=jnp.moveaxis(conv, 0, 2).reshape(2, 4, RG_W))

        def late_weights(self, i, W, after):
            if i == 0:
                W[0].update(zip(LATE, await_weights("l0", late_axes, after)))
            return W

        def layer_start(self, i, W, after):
            lands = await_weights("l1", [1] + late_axes, after)
            W[1].update(zip(LATE, lands[1:]), w_in=lands[0])
            return W

        def post_done(self, i, g):
            return push_grads("late0", g, LATE, late_axes) if i == 0 else None

        def smalls_done(self, grads, loss):
            super().smalls_done(grads, loss)
            conv = jnp.moveaxis(self.small['conv_w'].reshape(2, 4, N_DEV, RG_W // N_DEV), 2, 0)
            self.packed = _pack(self.small, loss)
            return push_grads("small", dict(conv_w=conv.reshape(N_DEV, 8, RG_W // N_DEV), small=self.packed),
                              ['conv_w', 'small'], [0, 0])

        def w_in_done(self, i, g):
            return push_grads("w_in0", g, ['w_in'], [0])

        def layer_done(self, i, g, dx):
            return push_grads("all1", g, ['w_in'] + list(LATE), [0] + late_axes) if i == 1 else None

    hooks = Overlap()
    _, grad_x, g = _full_grads(dict(w), x[0], p, loss_target[0], hooks)

    recv1 = dict(zip(['w_in'] + list(LATE), await_grads("all1", [0] + late_axes, grad_x)))
    recv0 = dict(zip(LATE, await_grads("late0", late_axes, grad_x)))
    outs = {}

    def update(k, parts, token=None):
        shard = w[k].shape
        c = shard[-1]
        two = lambda a: a.reshape(-1, c)
        res = _adamw([r.reshape(N_DEV, -1, c) for r in parts], two(w[k]), two(mom[k]), two(var[k]), token)
        outs[k] = [o.reshape(shard) for o in res]

    conv_parts, small_parts = await_grads("small", [0, 0], grad_x)
    rows = hooks.packed.shape[0] // N_DEV
    mine = _sum_parts(small_parts.reshape(N_DEV, rows, LANE))
    sums = _push_start("gather", [mine], _place_own("gather", [mine], [0], "place_small_sums"), [0], "push_small_sums")
    late = _adamw_sharded(list(LATE), [recv0, recv1], w, mom, var, "adamw_late", sums[4])
    for k in LATE:
        outs[k] = [t[k] for t in late]
    w_in0, = await_grads("w_in0", [0], [outs[k][1] for k in LATE])
    update('w_in', [w_in0, recv1['w_in']])
    update('conv_w', [conv_parts])
    gathered, = _push_wait("gather", sums[0], sums[1], sums[2], sums[3], [0], [outs['w_in'][1], outs['conv_w'][1]],
                           "await_small_sums")
    stored = [{k: _two_d(_stored(k, t[k])) for k in SMALL} for t in (w, mom, var)]
    loss = gathered[_packed_starts([stored[0][k].shape for k in SMALL] + [(1, 1)])[0][-1], 0]
    updated = _adamw_packed(SMALL, gathered, *stored, "adamw_small")
    for k in SMALL:
        shape = _stored(k, w[k]).shape
        outs[k] = [_stored(k, o[k].reshape(shape)) for o in updated]

    res = [loss, grad_x[None]]
    for j in range(4):
        res += [outs[k][j] for k in WEIGHTS]
    return tuple(res)
```

```python
import math

import jax
import jax.numpy as jnp
from jax import lax
from jax.experimental import pallas as pl
from jax.experimental.pallas import tpu as pltpu

F32 = jnp.float32
MXU = jnp.bfloat16
WIRE = jnp.bfloat16

N_DEV = 8
D_MODEL = 1024
PLE_D = 256
RG_W = 640
S5_W = 384
S5_P = 64
S5_N = 24 * S5_P
Z_W = 2 * RG_W + 2 * S5_W
C_RGG = RG_W
C_S5U = 2 * RG_W
C_S5G = 2 * RG_W + S5_W
LANE = 128
N_RG_T = RG_W // LANE
N_S5_T = S5_W // LANE
W_BLK = Z_W // N_DEV
ALPHA = (2.0 * 2) ** 0.25
LN_EPS = 1e-5
RG_C = 8.0
LR, B1, B2, EPS, WD, STEP = 0.001, 0.9, 0.999, 1e-08, 0.01, 10
BC1 = 1.0 - B1 ** STEP
BC2 = 1.0 - B2 ** STEP
RC = 512
RC_RG = 1024
TM = 512
TM_MM = 1024
VMEM_LIMIT = 56 * 1024 * 1024

MESH = pl.DeviceIdType.MESH
ANY = pl.BlockSpec(memory_space=pl.ANY)


def _params(n_grid_axes, vmem=VMEM_LIMIT):
    return pltpu.CompilerParams(dimension_semantics=("arbitrary",) * n_grid_axes, vmem_limit_bytes=vmem)


def _S(shape, dtype=F32):
    return jax.ShapeDtypeStruct(tuple(shape), dtype)


def _sigmoid(x):
    return 0.5 * jnp.tanh(0.5 * x) + 0.5


def _silu_and_grad(x):
    s = _sigmoid(x)
    return x * s, s * (1.0 + x * (1.0 - s))


_GELU_C = math.sqrt(2.0 / math.pi)


def _gelu(x):
    return 0.5 * x * (1.0 + jnp.tanh(_GELU_C * (x + 0.044715 * (x * x * x))))


def _gelu_grad(x):
    th = jnp.tanh(_GELU_C * (x + 0.044715 * (x * x * x)))
    return 0.5 * (1.0 + th) + 0.5 * x * (1.0 - th * th) * (_GELU_C * (1.0 + 3.0 * 0.044715 * (x * x)))


def _mm(a, b):
    return jnp.dot(a.astype(MXU), b.astype(MXU), preferred_element_type=F32)


def _mm_nt(a, b):
    return lax.dot_general(a.astype(MXU), b.astype(MXU), (((1,), (1,)), ((), ())), preferred_element_type=F32)


def _mm_tn(a, b):
    return lax.dot_general(a.astype(MXU), b.astype(MXU), (((0,), (0,)), ((), ())), preferred_element_type=F32)


def _ln_fwd(t, g, b):
    mu = jnp.mean(t, axis=-1, keepdims=True)
    tc = t - mu
    var = jnp.mean(tc * tc, axis=-1, keepdims=True)
    rstd = lax.rsqrt(var + LN_EPS)
    xhat = tc * rstd
    return xhat * g + b, xhat, rstd


def _ln_bwd(dy, xhat, rstd, g):
    dxh = dy * g
    m1 = jnp.mean(dxh, axis=-1, keepdims=True)
    m2 = jnp.mean(dxh * xhat, axis=-1, keepdims=True)
    return rstd * (dxh - m1 - xhat * m2)


def _colsum(a):
    return jnp.sum(a, axis=0, keepdims=True)


def _up(x, d, rows, fill):
    n = x.shape[0]
    return jnp.where(rows < n - d, pltpu.roll(x, n - d, 0), fill)


SUB = 8
TILE_STEPS = (1, 2, 4)


def _r8(width):
    return lax.broadcasted_iota(jnp.int32, (SUB, width), 0)


def _scan_real(a, u, carry, reverse=False):
    r8 = _r8(a.shape[1])
    n = a.shape[0] // SUB
    outs = [None] * n
    for k in (reversed(range(n)) if reverse else range(n)):
        A, U = a[SUB * k:SUB * k + SUB], u[SUB * k:SUB * k + SUB]
        for d in TILE_STEPS:
            m = (r8 < SUB - d) if reverse else (r8 >= d)
            sh = SUB - d if reverse else d
            U = A * jnp.where(m, pltpu.roll(U, sh, 0), 0.0) + U
            A = A * jnp.where(m, pltpu.roll(A, sh, 0), 1.0)
        h = A * carry + U
        outs[k] = h
        carry = h[0:1] if reverse else h[SUB - 1:SUB]
    return jnp.concatenate(outs, axis=0), carry


def _tile_powers(lr, li, reverse=False):
    width = lr.shape[1]
    r8 = _r8(width)
    steps = []
    pr, pi = lr, li
    er, ei = jnp.broadcast_to(lr, (SUB, width)), jnp.broadcast_to(li, (SUB, width))
    for d in TILE_STEPS:
        m = (r8 < SUB - d) if reverse else (r8 >= d)
        sh = SUB - d if reverse else d
        steps.append((sh, jnp.where(m, pr, 0.0), jnp.where(m, pi, 0.0)))
        er, ei = _cmul(er, ei, jnp.where(m, pltpu.roll(er, sh, 0), 1.0), jnp.where(m, pltpu.roll(ei, sh, 0), 0.0))
        pr, pi = _cmul(pr, pi, pr, pi)
    return steps, (er, ei)


def _scan_lti(xr, xi, carry, steps, e, reverse=False):
    er, ei = e
    kr, ki = carry
    n = xr.shape[0] // SUB
    outr, outi = [None] * n, [None] * n
    for k in (reversed(range(n)) if reverse else range(n)):
        sr, si = xr[SUB * k:SUB * k + SUB], xi[SUB * k:SUB * k + SUB]
        for sh, pr, pi in steps:
            shr, shi = pltpu.roll(sr, sh, 0), pltpu.roll(si, sh, 0)
            sr, si = sr + (pr * shr - pi * shi), si + (pr * shi + pi * shr)
        sr = sr + (er * kr - ei * ki)
        si = si + (er * ki + ei * kr)
        outr[k], outi[k] = sr, si
        kr, ki = (sr[0:1], si[0:1]) if reverse else (sr[SUB - 1:SUB], si[SUB - 1:SUB])
    return jnp.concatenate(outr, axis=0), jnp.concatenate(outi, axis=0), (kr, ki)


def _halo(ref, c, r0):
    rp = pl.multiple_of(jnp.maximum(r0 - 8, 0), 8)
    return jnp.where(c > 0, ref[pl.ds(rp, 8), :], 0.0)


def _conv_taps(xe):
    return [pltpu.roll(xe, 3, 0)[8:, :], pltpu.roll(xe, 2, 0)[8:, :], pltpu.roll(xe, 1, 0)[8:, :], xe[8:, :]]


def _rg_gates(h, wa, wx, ba, bx, sp):
    r = _sigmoid(_mm(h, wa) + ba)
    i = _sigmoid(_mm(h, wx) + bx)
    log_a = (-RG_C) * r * sp
    a = jnp.exp(log_a)
    mult = jnp.sqrt(-jnp.tanh(log_a) * (a * a + 1.0))
    return r, i, a, mult


def _softplus(y):
    return jnp.maximum(y, 0.0) + jnp.log1p(jnp.exp(-jnp.abs(y)))


def _after(token):
    return ([], []) if token is None else ([token], [ANY])


def _inproj_fwd(x, w_in, token=None):
    L = x.shape[0]

    def body(x_ref, w_ref, *rest):
        rest[-1][...] = _mm(x_ref[...], w_ref[...])

    extra, extra_specs = _after(token)
    tm = min(TM_MM, L)
    return pl.pallas_call(
        body, name="inproj_fwd", grid=(L // tm,),
        in_specs=[pl.BlockSpec((tm, D_MODEL), lambda i: (i, 0)), pl.BlockSpec((D_MODEL, Z_W), lambda i: (0, 0))] + extra_specs,
        out_specs=pl.BlockSpec((tm, Z_W), lambda i: (i, 0)),
        out_shape=_S((L, Z_W)), compiler_params=_params(1))(x, w_in, *extra)


def _inproj_bwd(dt1, x, dzx, dzg, dzu, w_in):
    L = x.shape[0]

    def body(dt1_ref, x_ref, dzx_ref, dzg_ref, dzu_ref, w_ref, dx_ref, dw_ref, acc_ref):
        @pl.when(pl.program_id(0) == 0)
        def _():
            acc_ref[...] = jnp.zeros_like(acc_ref)
        dzg = dzg_ref[...]
        dz = jnp.concatenate([dzx_ref[...], dzg[:, :RG_W], dzu_ref[...], dzg[:, RG_W:]], axis=1).astype(MXU)
        xb = x_ref[...].astype(MXU)
        dx_ref[...] = ALPHA * dt1_ref[...] + _mm_nt(dz, w_ref[...])
        for j in range(N_DEV):
            acc_ref[j] += _mm_tn(xb, dz[:, j * W_BLK:(j + 1) * W_BLK])

        @pl.when(pl.program_id(0) == L // TM - 1)
        def _():
            dw_ref[...] = acc_ref[...].astype(WIRE)

    row = lambda w: pl.BlockSpec((TM, w), lambda i: (i, 0))
    wspec = pl.BlockSpec((N_DEV, D_MODEL, W_BLK), lambda i: (0, 0, 0))
    return pl.pallas_call(
        body, name="inproj_bwd", grid=(L // TM,),
        in_specs=[row(D_MODEL), row(D_MODEL), row(RG_W), row(D_MODEL), row(S5_W),
                  pl.BlockSpec((D_MODEL, Z_W), lambda i: (0, 0))],
        out_specs=[row(D_MODEL), wspec],
        out_shape=[_S((L, D_MODEL)), _S((N_DEV, D_MODEL, W_BLK), WIRE)],
        scratch_shapes=[pltpu.VMEM((N_DEV, D_MODEL, W_BLK), F32)],
        compiler_params=_params(1))(dt1, x, dzx, dzg, dzu, w_in)


TM2 = 1024


def _dz_block(dzx_ref, dzg_ref, dzu_ref):
    dzg = dzg_ref[...]
    return jnp.concatenate([dzx_ref[...], dzg[:, :RG_W], dzu_ref[...], dzg[:, RG_W:]], axis=1).astype(MXU)


def _inproj_bwd_dw(x, dzx, dzg, dzu, token=None):
    L = x.shape[0]
    extra, extra_specs = _after(token)

    def body(x_ref, dzx_ref, dzg_ref, dzu_ref, *rest):
        dw_ref, acc_ref = rest[len(extra):]
        @pl.when(pl.program_id(0) == 0)
        def _():
            acc_ref[...] = jnp.zeros_like(acc_ref)
        dz = _dz_block(dzx_ref, dzg_ref, dzu_ref)
        xb = x_ref[...].astype(MXU)
        for j in range(N_DEV):
            acc_ref[j] += _mm_tn(xb, dz[:, j * W_BLK:(j + 1) * W_BLK])

        @pl.when(pl.program_id(0) == L // TM2 - 1)
        def _():
            dw_ref[...] = acc_ref[...].astype(WIRE)

    row = lambda w: pl.BlockSpec((TM2, w), lambda i: (i, 0))
    wspec = pl.BlockSpec((N_DEV, D_MODEL, W_BLK), lambda i: (0, 0, 0))
    return pl.pallas_call(
        body, name="inproj_bwd_dw", grid=(L // TM2,),
        in_specs=[row(D_MODEL), row(RG_W), row(D_MODEL), row(S5_W)] + extra_specs, out_specs=wspec,
        out_shape=_S((N_DEV, D_MODEL, W_BLK), WIRE), scratch_shapes=[pltpu.VMEM((N_DEV, D_MODEL, W_BLK), F32)],
        compiler_params=_params(1))(x, dzx, dzg, dzu, *extra)


def _inproj_bwd_dx(dt1, dzx, dzg, dzu, w_in, token=None):
    L = dt1.shape[0]
    extra, extra_specs = _after(token)

    def body(dt1_ref, dzx_ref, dzg_ref, dzu_ref, w_ref, *rest):
        rest[-1][...] = ALPHA * dt1_ref[...] + _mm_nt(_dz_block(dzx_ref, dzg_ref, dzu_ref), w_ref[...])

    tm = min(TM_MM, L)
    row = lambda w: pl.BlockSpec((tm, w), lambda i: (i, 0))
    return pl.pallas_call(
        body, name="inproj_bwd_dx", grid=(L // tm,),
        in_specs=[row(D_MODEL), row(RG_W), row(D_MODEL), row(S5_W), _full((D_MODEL, Z_W))] + extra_specs,
        out_specs=row(D_MODEL), out_shape=_S((L, D_MODEL)), compiler_params=_params(1))(dt1, dzx, dzg, dzu, w_in, *extra)


def _rg_specs(layer):
    tile = lambda rows: pl.BlockSpec((rows, LANE), lambda c: (0, c))
    ptile = lambda rows: pl.BlockSpec((None, rows, LANE), lambda c: (layer, 0, c))
    pheads = pl.BlockSpec((None, 2, RG_HD, RG_HD), lambda c: (layer, c, 0, 0))
    return tile, ptile, pheads, pl.BlockSpec((2, RG_HD, RG_HD), lambda c: (c, 0, 0))


RG_HD = 64


def _bd2(w):
    z = jnp.zeros((RG_HD, RG_HD), w.dtype)
    return jnp.concatenate([jnp.concatenate([w[0], z], axis=1), jnp.concatenate([z, w[1]], axis=1)], axis=0)


def _bd2_diag(m):
    return jnp.stack([m[:RG_HD, :RG_HD], m[RG_HD:, RG_HD:]])


def _rg_fwd(z, cw, cb, wa_bd, wx_bd, ba, bx, lam, layer):
    L = z.shape[0]
    RC = min(RC_RG, L)

    def body(x_ref, cw_ref, cb_ref, wa_ref, wx_ref, ba_ref, bx_ref, lam_ref, hs_ref, *saved):
        row = slice(layer, layer + 1)
        w, b = cw_ref[...], cb_ref[row, :]
        wa, wx, ba_, bx_ = _bd2(wa_ref[...]).astype(MXU), _bd2(wx_ref[...]).astype(MXU), ba_ref[row, :], bx_ref[row, :]
        sp = _softplus(-lam_ref[row, :])

        def step(c, carry):
            r0 = pl.multiple_of(c * RC, RC)
            xe = jnp.concatenate([_halo(x_ref, c, r0), x_ref[pl.ds(r0, RC), :]], axis=0)
            t = _conv_taps(xe)
            h = t[0] * w[0:1] + t[1] * w[1:2] + t[2] * w[2:3] + t[3] * w[3:4] + b
            r, i, a, mult = _rg_gates(h, wa, wx, ba_, bx_, sp)
            hs, carry = _scan_real(a, mult * (i * h), carry)
            hs_ref[pl.ds(r0, RC), :] = hs
            for ref, val in zip(saved, (h, r, i, a, mult)):
                ref[pl.ds(r0, RC), :] = val
            return carry

        lax.fori_loop(0, L // RC, step, jnp.zeros((1, LANE), F32))

    tile, ptile, pheads, _ = _rg_specs(layer)
    return pl.pallas_call(
        body, name="rg_fwd", grid=(N_RG_T,),
        in_specs=[tile(L), ptile(4), tile(2), pheads, pheads, tile(2), tile(2), tile(2)],
        out_specs=[tile(L)] * 6, out_shape=[_S((L, RG_W))] * 6, compiler_params=_params(1))(
            z, cw, cb, wa_bd, wx_bd, ba, bx, lam)


def _rg_bwd(dhs, z, hs, gates, cw, wa_bd, wx_bd, lam, layer):
    L = z.shape[0]
    RC = min(RC_RG, L)

    def body(g_ref, x_ref, hs_ref, h_ref, r_ref, i_ref, a_ref, mult_ref, cw_ref, wa_ref, wx_ref, lam_ref,
             dx_ref, dcw_ref, dcb_ref, dwa_out, dwx_out, dba_ref, dbx_ref, dlam_ref, dwa_ref, dwx_ref):
        w = cw_ref[...]
        wa, wx = _bd2(wa_ref[...]).astype(MXU), _bd2(wx_ref[...]).astype(MXU)
        lam = lam_ref[layer:layer + 1, :]
        sp = _softplus(-lam)
        rows = lax.broadcasted_iota(jnp.int32, (RC, LANE), 0)
        for ref in (dcw_ref, dcb_ref, dwa_ref, dwx_ref, dba_ref, dbx_ref, dlam_ref):
            ref[...] = jnp.zeros_like(ref)
        nch = L // RC

        def step(k, carry):
            cin, nxt = carry
            c = nch - 1 - k
            r0 = pl.multiple_of(c * RC, RC)
            xe = jnp.concatenate([_halo(x_ref, c, r0), x_ref[pl.ds(r0, RC), :]], axis=0)
            t = _conv_taps(xe)
            h, r, i, a, mult = (ref[pl.ds(r0, RC), :] for ref in (h_ref, r_ref, i_ref, a_ref, mult_ref))
            hs_e = jnp.concatenate([_halo(hs_ref, c, r0), hs_ref[pl.ds(r0, RC), :]], axis=0)
            hs_prev = pltpu.roll(hs_e, 1, 0)[8:, :]
            g = g_ref[pl.ds(r0, RC), :]
            cc, cin_new = _scan_real(a, a * g, cin, reverse=True)
            dh = g + _up(cc, 1, rows, cin)
            ih = i * h
            dlog_a = dh * hs_prev * a - (dh * ih) * (a * a) / mult
            di = dh * mult * h
            dhin = dh * mult * i
            dr = dlog_a * ((-RG_C) * sp)
            dlam_ref[...] += _colsum(dlog_a * r)
            dra = dr * r * (1.0 - r)
            dia = di * i * (1.0 - i)
            dwa_ref[...] += _mm_tn(h, dra)
            dwx_ref[...] += _mm_tn(h, dia)
            dba_ref[...] += _colsum(dra)
            dbx_ref[...] += _colsum(dia)
            dhin = dhin + _mm_nt(dra, wa) + _mm_nt(dia, wx)
            de = jnp.concatenate([dhin, nxt], axis=0)
            n = RC + 8
            dx = (dhin * w[3:4] + pltpu.roll(de, n - 1, 0)[:RC, :] * w[2:3]
                  + pltpu.roll(de, n - 2, 0)[:RC, :] * w[1:2] + pltpu.roll(de, n - 3, 0)[:RC, :] * w[0:1])
            dx_ref[pl.ds(r0, RC), :] = dx
            for kk in range(4):
                dcw_ref[kk:kk + 1, :] += _colsum(dhin * t[kk])
            dcb_ref[...] += _colsum(dhin)
            return cin_new, dhin[0:8, :]

        lax.fori_loop(0, nch, step, (jnp.zeros((1, LANE), F32), jnp.zeros((8, LANE), F32)))
        dlam_ref[...] = dlam_ref[...] * (RG_C * _sigmoid(-lam))
        dwa_out[...], dwx_out[...] = _bd2_diag(dwa_ref[...]), _bd2_diag(dwx_ref[...])

    tile, ptile, pheads, gheads = _rg_specs(layer)
    heads = _S((2 * N_RG_T, RG_HD, RG_HD))
    return pl.pallas_call(
        body, name="rg_bwd", grid=(N_RG_T,),
        in_specs=[tile(L)] * 8 + [ptile(4), pheads, pheads, tile(2)],
        out_specs=[tile(L), tile(4), tile(1), gheads, gheads, tile(1), tile(1), tile(1)],
        out_shape=[_S((L, RG_W)), _S((4, RG_W)), _S((1, RG_W)), heads, heads, _S((1, RG_W)), _S((1, RG_W)), _S((1, RG_W))],
        scratch_shapes=[pltpu.VMEM((LANE, LANE), F32), pltpu.VMEM((LANE, LANE), F32)],
        compiler_params=_params(1))(dhs, z, hs, *gates, cw, wa_bd, wx_bd, lam)


def _cmul(ar, ai, br, bi):
    return ar * br - ai * bi, ar * bi + ai * br


S5_TW = S5_N // N_S5_T


S5_H = 16
S5_GT = LANE // S5_H


def _s5_specs(L, layer):
    in_tile = pl.BlockSpec((L, LANE), lambda t: (0, t))
    st = pl.BlockSpec((L, S5_TW), lambda t: (0, t))
    pg = pl.BlockSpec((None, S5_GT, S5_H, S5_P), lambda t: (layer * N_S5_T + t, 0, 0, 0))
    plb = pl.BlockSpec((None, S5_GT, S5_P), lambda t: (layer * N_S5_T + t, 0, 0))
    gg = pl.BlockSpec((None, S5_GT, S5_H, S5_P), lambda t: (t, 0, 0, 0))
    glb = pl.BlockSpec((None, S5_GT, S5_P), lambda t: (t, 0, 0))
    dv = pl.BlockSpec((1, LANE), lambda t: (0, t))
    return in_tile, st, pg, plb, gg, glb, dv


def _bd8(blocks):
    rows = []
    for g in range(S5_GT):
        pieces = [blocks[g]]
        if g:
            pieces.insert(0, jnp.zeros((S5_H, S5_P * g), blocks.dtype))
        if g < S5_GT - 1:
            pieces.append(jnp.zeros((S5_H, S5_P * (S5_GT - 1 - g)), blocks.dtype))
        rows.append(jnp.concatenate(pieces, axis=1))
    return jnp.concatenate(rows, axis=0)


def _bd8_diag(m):
    return jnp.stack([m[S5_H * g:S5_H * (g + 1), S5_P * g:S5_P * (g + 1)] for g in range(S5_GT)])


def _row8(v):
    return jnp.concatenate([v[g:g + 1] for g in range(S5_GT)], axis=1)


def _row8_split(r):
    return jnp.concatenate([r[:, S5_P * g:S5_P * (g + 1)] for g in range(S5_GT)], axis=0)


def _layer_row_tile(layer):
    return pl.BlockSpec((None, 1, LANE), lambda t: (layer, 0, t))


def _s5_fwd(z, bb_re, bb_im, lb_re, lb_im, c_re, c_im, dvec, layer):
    L = z.shape[0]

    def body(u_ref, bbr_ref, bbi_ref, lr_ref, li_ref, cr_ref, ci_ref, d_ref, y_ref, sr_ref, si_ref):
        bbr, bbi = _bd8(bbr_ref[...]).astype(MXU), _bd8(bbi_ref[...]).astype(MXU)
        cr, ci = _bd8(cr_ref[...]).astype(MXU), _bd8(ci_ref[...]).astype(MXU)
        dv = d_ref[...]
        steps, e = _tile_powers(_row8(lr_ref[...]), _row8(li_ref[...]))

        def step(c, carry):
            r0 = pl.multiple_of(c * RC, RC)
            u = u_ref[pl.ds(r0, RC), :]
            ub = u.astype(MXU)
            sr = jnp.dot(ub, bbr, preferred_element_type=F32)
            si = jnp.dot(ub, bbi, preferred_element_type=F32)
            sr, si, carry = _scan_lti(sr, si, carry, steps, e)
            sr_ref[pl.ds(r0, RC), :] = sr
            si_ref[pl.ds(r0, RC), :] = si
            y_ref[pl.ds(r0, RC), :] = dv * u + (_mm_nt(sr, cr) - _mm_nt(si, ci))
            return carry

        zero = jnp.zeros((1, S5_TW), F32)
        lax.fori_loop(0, L // RC, step, (zero, zero))

    in_tile, st, pg, plb, _, _, _ = _s5_specs(L, layer)
    u_tile = pl.BlockSpec((L, LANE), lambda t: (0, C_S5U // LANE + t))
    return pl.pallas_call(
        body, name="s5_fwd", grid=(N_S5_T,),
        in_specs=[u_tile, pg, pg, plb, plb, pg, pg, _layer_row_tile(layer)],
        out_specs=[in_tile, st, st],
        out_shape=[_S((L, S5_W)), _S((L, S5_N)), _S((L, S5_N))],
        compiler_params=_params(1))(z, bb_re, bb_im, lb_re, lb_im, c_re, c_im, dvec)


def _s5_bwd(dy0, z, s_re, s_im, bb_re, bb_im, lb_re, lb_im, c_re, c_im, dvec, layer, token=None):
    L = z.shape[0]
    extra, extra_specs = _after(token)

    def body(dy_ref, u_ref, sr_ref, si_ref, bbr_ref, bbi_ref, lr_ref, li_ref, cr_ref, ci_ref, d_ref, *rest):
        (du_ref, dbbr_out, dbbi_out, dlr_out, dli_out, dcr_out, dci_out, dd_ref,
         dbbr_ref, dbbi_ref, dcr_ref, dci_ref, dlr_ref, dli_ref) = rest[len(extra):]
        bbr, bbi = _bd8(bbr_ref[...]).astype(MXU), _bd8(bbi_ref[...]).astype(MXU)
        cr, ci = _bd8(cr_ref[...]).astype(MXU), _bd8(ci_ref[...]).astype(MXU)
        lr, li = _row8(lr_ref[...]), -_row8(li_ref[...])
        dv = d_ref[...]
        steps, e = _tile_powers(lr, li, reverse=True)
        for ref in (dbbr_ref, dbbi_ref, dlr_ref, dli_ref, dcr_ref, dci_ref, dd_ref):
            ref[...] = jnp.zeros_like(ref)
        nch = L // RC

        def step(k, carry):
            c = nch - 1 - k
            r0 = pl.multiple_of(c * RC, RC)
            dy = dy_ref[pl.ds(r0, RC), :]
            u = u_ref[pl.ds(r0, RC), :]
            dyb, ub = dy.astype(MXU), u.astype(MXU)
            sr, si = sr_ref[pl.ds(r0, RC), :], si_ref[pl.ds(r0, RC), :]
            dcr_ref[...] += _mm_tn(dyb, sr)
            dci_ref[...] -= _mm_tn(dyb, si)
            gr = jnp.dot(dyb, cr, preferred_element_type=F32)
            gi = -jnp.dot(dyb, ci, preferred_element_type=F32)
            gr, gi, carry = _scan_lti(gr, gi, carry, steps, e, reverse=True)
            pr_ = pltpu.roll(jnp.concatenate([_halo(sr_ref, c, r0), sr], axis=0), 1, 0)[8:, :]
            pi_ = pltpu.roll(jnp.concatenate([_halo(si_ref, c, r0), si], axis=0), 1, 0)[8:, :]
            dlr_ref[...] += _colsum(pr_ * gr + pi_ * gi)
            dli_ref[...] += _colsum(pr_ * gi - pi_ * gr)
            grb, gib = gr.astype(MXU), gi.astype(MXU)
            dbbr_ref[...] += _mm_tn(ub, grb)
            dbbi_ref[...] += _mm_tn(ub, gib)
            du_ref[pl.ds(r0, RC), :] = dv * dy + (_mm_nt(grb, bbr) + _mm_nt(gib, bbi))
            dd_ref[...] += _colsum(dy * u)
            return carry

        zero = jnp.zeros((1, S5_TW), F32)
        lax.fori_loop(0, nch, step, (zero, zero))
        dbbr_out[...], dbbi_out[...] = _bd8_diag(dbbr_ref[...]), _bd8_diag(dbbi_ref[...])
        dcr_out[...], dci_out[...] = _bd8_diag(dcr_ref[...]), _bd8_diag(dci_ref[...])
        dlr_out[...], dli_out[...] = _row8_split(dlr_ref[...]), _row8_split(dli_ref[...])

    in_tile, st, pg, plb, gg, glb, dv = _s5_specs(L, layer)
    u_tile = pl.BlockSpec((L, LANE), lambda t: (0, C_S5U // LANE + t))
    groups, rows = _S((N_S5_T, S5_GT, S5_H, S5_P)), _S((N_S5_T, S5_GT, S5_P))
    wide = pltpu.VMEM((LANE, S5_TW), F32)
    return pl.pallas_call(
        body, name="s5_bwd", grid=(N_S5_T,),
        in_specs=[in_tile, u_tile, st, st, pg, pg, plb, plb, pg, pg, _layer_row_tile(layer)] + extra_specs,
        out_specs=[in_tile, gg, gg, glb, glb, gg, gg, dv],
        out_shape=[_S((L, S5_W)), groups, groups, rows, rows, groups, groups, _S((1, S5_W))],
        scratch_shapes=[wide, wide, wide, wide, pltpu.VMEM((1, S5_TW), F32), pltpu.VMEM((1, S5_TW), F32)],
        compiler_params=_params(1))(dy0, z, s_re, s_im, bb_re, bb_im, lb_re, lb_im, c_re, c_im, dvec, *extra)


def _disc(ar, ai, ls):
    dt = jnp.exp(ls)
    mag = jnp.exp(ar * dt)
    lr = mag * jnp.cos(ai * dt)
    li = mag * jnp.sin(ai * dt)
    den = ar * ar + ai * ai
    cr = ((lr - 1.0) * ar + li * ai) / den
    ci = (li * ar - (lr - 1.0) * ai) / den
    return lr, li, cr, ci


def _s5_disc_fwd(ar, ai, ls, token=None):
    extra, extra_specs = _after(token)

    def body(ar_ref, ai_ref, ls_ref, *rest):
        lr_ref, li_ref, cr_ref, ci_ref = rest[len(extra):]
        lr, li, cr, ci = _disc(ar_ref[...], ai_ref[...], ls_ref[...])
        lr_ref[...], li_ref[...], cr_ref[...], ci_ref[...] = lr, li, cr, ci

    sh = _S(ar.shape)
    vm = pl.BlockSpec(memory_space=pltpu.VMEM)
    return pl.pallas_call(body, name="s5_disc_fwd", in_specs=[vm, vm, vm] + extra_specs, out_shape=[sh, sh, sh, sh])(
        ar, ai, ls, *extra)


def _s5_disc_bwd(ar, ai, ls, dlr, dli, dcr, dci):
    def body(ar_ref, ai_ref, ls_ref, dlr_ref, dli_ref, dcr_ref, dci_ref, dar_ref, dai_ref, dls_ref):
        _, vjp = jax.vjp(_disc, ar_ref[...], ai_ref[...], jnp.broadcast_to(ls_ref[...], ar_ref.shape))
        dar, dai, dls = vjp((dlr_ref[...], dli_ref[...], dcr_ref[...], dci_ref[...]))
        dar_ref[...], dai_ref[...] = dar, dai
        dls_ref[...] = jnp.sum(dls, axis=1, keepdims=True)

    return pl.pallas_call(body, name="s5_disc_bwd", out_shape=[_S(ar.shape), _S(ar.shape), _S(ls.shape)])(
        ar, ai, ls, dlr, dli, dcr, dci)


def _s5_bscale_fwd(cr, ci, br, bi):
    def body(cr_ref, ci_ref, br_ref, bi_ref, or_ref, oi_ref):
        or_ref[...], oi_ref[...] = _cmul(cr_ref[...], ci_ref[...], br_ref[...], bi_ref[...])

    return pl.pallas_call(body, name="s5_bscale_fwd", out_shape=[_S(br.shape), _S(br.shape)])(cr, ci, br, bi)


def _s5_bscale_bwd(cr, ci, br, bi, gr, gi):
    def body(cr_ref, ci_ref, br_ref, bi_ref, gr_ref, gi_ref, dbr_ref, dbi_ref, dcr_ref, dci_ref):
        cr_, ci_, br_, bi_, gr_, gi_ = (r[...] for r in (cr_ref, ci_ref, br_ref, bi_ref, gr_ref, gi_ref))
        dbr_ref[...] = cr_ * gr_ + ci_ * gi_
        dbi_ref[...] = cr_ * gi_ - ci_ * gr_
        dcr_ref[...] = jnp.sum(gr_ * br_ + gi_ * bi_, axis=1, keepdims=True)
        dci_ref[...] = jnp.sum(gi_ * br_ - gr_ * bi_, axis=1, keepdims=True)

    return pl.pallas_call(body, name="s5_bscale_bwd",
                          out_shape=[_S(br.shape), _S(br.shape), _S(cr.shape), _S(cr.shape)])(cr, ci, br, bi, gr, gi)


def _row(w):
    return pl.BlockSpec((TM, w), lambda i: (i, 0))


def _full(shape):
    return pl.BlockSpec(tuple(shape), lambda i: (0,) * len(shape))


def _gate_rows():
    return [pl.BlockSpec((TM, RG_W), lambda i: (i, C_RGG // RG_W))] + [
        pl.BlockSpec((TM, LANE), lambda i, k=k: (i, C_S5G // LANE + k)) for k in range(N_S5_T)]


def _p_rows(layer):
    return pl.BlockSpec((None, None, TM, PLE_D), lambda i: (layer, 0, i, 0))


DEPTH = 2


def _lrow(layer, width):
    return _full((DEPTH, width))


def _pick(ref, layer):
    return ref[layer:layer + 1, :]


def _post_fwd(x, hs, z, y0, p, w_glu, b_glu, w_out, g1, b1, ple_w, w_pg, b_pg, g2, b2, layer):
    L = x.shape[0]

    def body(x_ref, hs_ref, zg_ref, zs0_ref, zs1_ref, zs2_ref, y0_ref, p_ref, wg_ref, bg_ref, wo_ref, g1_ref, b1_ref, pw_ref,
             wpg_ref, bpg_ref, g2_ref, b2_ref, x2_ref, xh1_ref, xh2_ref, gt_ref, rstd1_ref, rstd2_ref):
        rg_gate = zg_ref[...]
        s5_gate = jnp.concatenate([zs0_ref[...], zs1_ref[...], zs2_ref[...]], axis=1)
        rg_y = hs_ref[...] * _silu_and_grad(rg_gate)[0]
        y1 = _gelu(y0_ref[...])
        gl = _sigmoid(_mm(y1, wg_ref[...]) + _pick(bg_ref, layer))
        s5_y = (y1 * gl) * _silu_and_grad(s5_gate)[0]
        mix = _mm(jnp.concatenate([rg_y.astype(MXU), s5_y.astype(MXU)], axis=1), wo_ref[...])
        t1 = ALPHA * x_ref[...] + mix
        x1, xh1, rstd1 = _ln_fwd(t1, _pick(g1_ref, layer), _pick(b1_ref, layer))
        q = _mm(p_ref[...], pw_ref[...])
        gt = _sigmoid(_mm(x1, wpg_ref[...]) + _pick(bpg_ref, layer))
        t2 = ALPHA * x1 + q * gt
        x2, xh2, rstd2 = _ln_fwd(t2, _pick(g2_ref, layer), _pick(b2_ref, layer))
        x2_ref[...], xh1_ref[...], xh2_ref[...], gt_ref[...] = x2, xh1, xh2, gt
        rstd1_ref[...], rstd2_ref[...] = rstd1, rstd2

    vec = _lrow(layer, D_MODEL)
    return pl.pallas_call(
        body, name="post_fwd", grid=(L // TM,),
        in_specs=[_row(D_MODEL), _row(RG_W), *_gate_rows(), _row(S5_W), _p_rows(layer), _full((S5_W, S5_W)),
                  _lrow(layer, S5_W), _full((D_MODEL, D_MODEL)), vec, vec, _full((PLE_D, D_MODEL)), _full((D_MODEL, D_MODEL)),
                  vec, vec, vec],
        out_specs=[_row(D_MODEL)] * 4 + [_row(1)] * 2, out_shape=[_S((L, D_MODEL))] * 4 + [_S((L, 1))] * 2,
        compiler_params=_params(1))(x, hs, z, z, z, z, y0, p, w_glu, b_glu, w_out, g1, b1, ple_w, w_pg, b_pg, g2, b2)


def _post_bwd_a(dx2_or_target, is_top, xh2, xh1, rstd2, rstd1, gt, p, ple_w, w_pg, g1, b1, g2, b2, layer, token=None):
    L = xh1.shape[0]
    extra, extra_specs = _after(token)

    def body(d_ref, xh2_ref, xh1_ref, rstd2_ref, rstd1_ref, gt_ref, p_ref, pw_ref, wpg_ref, g1_ref, b1_ref, g2_ref,
             b2_ref, *rest):
        (dt1_ref, dpw_out, dwpg_out, dbpg_ref, dg1_ref, db1_ref, dg2_ref, db2_ref, loss_ref, dpw_ref,
         dwpg_ref) = rest[len(extra):]
        @pl.when(pl.program_id(0) == 0)
        def _():
            for ref in (dpw_ref, dwpg_ref, dbpg_ref, dg1_ref, db1_ref, dg2_ref, db2_ref, loss_ref):
                ref[...] = jnp.zeros_like(ref)

        g1, g2 = _pick(g1_ref, layer), _pick(g2_ref, layer)
        xh1, xh2, rstd1, rstd2 = xh1_ref[...], xh2_ref[...], rstd1_ref[...], rstd2_ref[...]
        x1 = xh1 * g1 + _pick(b1_ref, layer)
        if is_top:
            err = (xh2 * g2 + _pick(b2_ref, layer)) - d_ref[...]
            loss_ref[...] += _colsum(err * err)
            dx2 = err * (1.0 / D_MODEL)
        else:
            dx2 = d_ref[...]
        p = p_ref[...]
        q, gt = _mm(p, pw_ref[...]), gt_ref[...]
        dg2_ref[...] += _colsum(dx2 * xh2)
        db2_ref[...] += _colsum(dx2)
        dt2 = _ln_bwd(dx2, xh2, rstd2, g2)
        dq = dt2 * gt
        dgpre = (dt2 * q) * gt * (1.0 - gt)
        dpw_ref[...] += _mm_tn(p, dq)
        dwpg_ref[...] += _mm_tn(x1, dgpre)
        dbpg_ref[...] += _colsum(dgpre)
        dx1 = ALPHA * dt2 + _mm_nt(dgpre, wpg_ref[...])
        dg1_ref[...] += _colsum(dx1 * xh1)
        db1_ref[...] += _colsum(dx1)
        dt1_ref[...] = _ln_bwd(dx1, xh1, rstd1, g1)

        @pl.when(pl.program_id(0) == L // TM - 1)
        def _():
            dpw_out[...] = dpw_ref[...].astype(WIRE)
            dwpg_out[...] = dwpg_ref[...].astype(WIRE)

    vec, lvec = _full((1, D_MODEL)), _lrow(layer, D_MODEL)
    return pl.pallas_call(
        body, name="post_bwd_a_top" if is_top else "post_bwd_a", grid=(L // TM,),
        in_specs=[_row(D_MODEL), _row(D_MODEL), _row(D_MODEL), _row(1), _row(1), _row(D_MODEL), _p_rows(layer),
                  _full((PLE_D, D_MODEL)), _full((D_MODEL, D_MODEL)), lvec, lvec, lvec, lvec] + extra_specs,
        out_specs=[_row(D_MODEL), _full((PLE_D, D_MODEL)), _full((D_MODEL, D_MODEL)), vec, vec, vec, vec, vec, vec],
        out_shape=[_S((L, D_MODEL)), _S((PLE_D, D_MODEL), WIRE), _S((D_MODEL, D_MODEL), WIRE)] + [_S((1, D_MODEL))] * 6,
        scratch_shapes=[pltpu.VMEM((PLE_D, D_MODEL), F32), pltpu.VMEM((D_MODEL, D_MODEL), F32)],
        compiler_params=_params(1))(dx2_or_target, xh2, xh1, rstd2, rstd1, gt, p, ple_w, w_pg, g1, b1, g2, b2, *extra)


def _post_bwd_b(dt1, z, hs, y0, w_out, w_glu, b_glu, layer):
    L = dt1.shape[0]

    def body(dt1_ref, zg_ref, zs0_ref, zs1_ref, zs2_ref, hs_ref, y0_ref, wo_ref, wg_ref, bg_ref,
             dhs_ref, dy0_ref, dzg_ref, dwo_out, dwg_out, dbg_ref, dwo_ref, dwg_ref):
        @pl.when(pl.program_id(0) == 0)
        def _():
            for ref in (dwo_ref, dwg_ref, dbg_ref):
                ref[...] = jnp.zeros_like(ref)

        dt1b = dt1_ref[...].astype(MXU)
        dm = _mm_nt(dt1b, wo_ref[...])
        d_rgy, d_s5y = dm[:, :RG_W], dm[:, RG_W:]
        rg_gate = zg_ref[...]
        s5_gate = jnp.concatenate([zs0_ref[...], zs1_ref[...], zs2_ref[...]], axis=1)
        hs = hs_ref[...]
        sl, dsl = _silu_and_grad(rg_gate)
        dhs_ref[...] = d_rgy * sl
        dzg_ref[:, :RG_W] = d_rgy * hs * dsl
        y0 = y0_ref[...]
        y1 = _gelu(y0)
        gl = _sigmoid(_mm(y1, wg_ref[...]) + _pick(bg_ref, layer))
        y2 = y1 * gl
        sl2, dsl = _silu_and_grad(s5_gate)
        m = jnp.concatenate([(hs * sl).astype(MXU), (y2 * sl2).astype(MXU)], axis=1)
        dwo_ref[...] += _mm_tn(m, dt1b)
        dy2 = d_s5y * sl2
        dzg_ref[:, RG_W:] = d_s5y * y2 * dsl
        dglpre = (dy2 * y1) * gl * (1.0 - gl)
        dwg_ref[...] += _mm_tn(y1, dglpre)
        dbg_ref[...] += _colsum(dglpre)
        dy1 = dy2 * gl + _mm_nt(dglpre, wg_ref[...])
        dy0_ref[...] = dy1 * _gelu_grad(y0)

        @pl.when(pl.program_id(0) == L // TM - 1)
        def _():
            dwo_out[...] = dwo_ref[...].astype(WIRE)
            dwg_out[...] = dwg_ref[...].astype(WIRE)

    return pl.pallas_call(
        body, name="post_bwd_b", grid=(L // TM,),
        in_specs=[_row(D_MODEL), *_gate_rows(), _row(RG_W), _row(S5_W), _full((D_MODEL, D_MODEL)),
                  _full((S5_W, S5_W)), _lrow(layer, S5_W)],
        out_specs=[_row(RG_W), _row(S5_W), _row(D_MODEL), _full((D_MODEL, D_MODEL)), _full((S5_W, S5_W)), _full((1, S5_W))],
        out_shape=[_S((L, RG_W)), _S((L, S5_W)), _S((L, D_MODEL)), _S((D_MODEL, D_MODEL), WIRE), _S((S5_W, S5_W), WIRE),
                   _S((1, S5_W))],
        scratch_shapes=[pltpu.VMEM((D_MODEL, D_MODEL), F32), pltpu.VMEM((S5_W, S5_W), F32)],
        compiler_params=_params(1))(dt1, z, z, z, z, hs, y0, w_out, w_glu, b_glu)


def _adamw(parts, w, m, v, token=None):
    nl = len(parts)
    extra, extra_specs = _after(token)
    n, R, C = parts[0].shape
    tr = R
    for cand in (512, 256, 128, 64, 32, 16, 8):
        if R % cand == 0 and n * cand * C * 4 <= 4 * 1024 * 1024:
            tr = cand
            break
    nblk = R // tr

    def body(*refs):
        p_refs = refs[:nl]
        w_ref, m_ref, v_ref = refs[nl:nl + 3]
        g_ref, d_ref, nm_ref, nv_ref = refs[nl + 3 + len(extra):]
        layer = pl.program_id(0)
        g = None
        for li, p_ref in enumerate(p_refs):
            s = p_ref[0].astype(F32)
            for k in range(1, n):
                s = s + p_ref[k].astype(F32)
            g = s if g is None else jnp.where(layer == li, s, g)
        nm = B1 * m_ref[...] + (1.0 - B1) * g
        nv = B2 * v_ref[...] + (1.0 - B2) * (g * g)
        d_ref[...] = (-LR) * ((nm / BC1) / (jnp.sqrt(nv / BC2) + EPS) + WD * w_ref[...])
        g_ref[...], nm_ref[...], nv_ref[...] = g, nm, nv

    def part_spec(li):
        return pl.BlockSpec((n, tr, C), lambda l, i: (0, jnp.where(l == li, i, jnp.where(l < li, 0, nblk - 1)), 0))

    blk = pl.BlockSpec((tr, C), lambda l, i: (l * nblk + i, 0))
    return pl.pallas_call(
        body, name="adamw", grid=(nl, nblk),
        in_specs=[part_spec(li) for li in range(nl)] + [blk, blk, blk] + extra_specs,
        out_specs=[blk] * 4, out_shape=[_S((nl * R, C))] * 4, compiler_params=_params(2))(*parts, w, m, v, *extra)


def _adamw_sharded(names, recv, w, m, v, name, token=None):
    n, nl = len(names), len(recv)
    extra, extra_specs = _after(token)
    n_in = n * (nl + 3)

    def body(*refs):
        outs = refs[n_in + len(extra):]
        for j in range(n):
            w_ref, m_ref, v_ref = (refs[(nl + t) * n + j] for t in range(3))
            g_ref, d_ref, nm_ref, nv_ref = (outs[t * n + j] for t in range(4))
            for l in range(nl):
                p_ref = refs[l * n + j]
                g = p_ref[0].astype(F32)
                for q in range(1, N_DEV):
                    g = g + p_ref[q].astype(F32)
                nm = B1 * m_ref[l] + (1.0 - B1) * g
                nv = B2 * v_ref[l] + (1.0 - B2) * (g * g)
                d_ref[l] = (-LR) * ((nm / BC1) / (jnp.sqrt(nv / BC2) + EPS) + WD * w_ref[l])
                g_ref[l], nm_ref[l], nv_ref[l] = g, nm, nv

    ins = [r[k] for r in recv for k in names] + [t[k] for t in (w, m, v) for k in names]
    vm = pl.BlockSpec(memory_space=pltpu.VMEM)
    outs = pl.pallas_call(body, name=name, in_specs=[vm] * n_in + extra_specs,
                          out_shape=[_S(w[k].shape) for _ in range(4) for k in names],
                          compiler_params=pltpu.CompilerParams(vmem_limit_bytes=VMEM_LIMIT))(*ins, *extra)
    return [{k: outs[t * n + j] for j, k in enumerate(names)} for t in range(4)]


def _adamw_packed(names, packed, w, m, v, name):
    n = len(names)
    starts, _ = _packed_starts([w[k].shape for k in names])

    def body(p_ref, *refs):
        for j in range(n):
            w_ref, m_ref, v_ref, g_ref, d_ref, nm_ref, nv_ref = (refs[k * n + j] for k in range(7))
            gj = _packed_get(p_ref, starts[j], w_ref.shape)
            nm = B1 * m_ref[...] + (1.0 - B1) * gj
            nv = B2 * v_ref[...] + (1.0 - B2) * (gj * gj)
            d_ref[...] = (-LR) * ((nm / BC1) / (jnp.sqrt(nv / BC2) + EPS) + WD * w_ref[...])
            g_ref[...], nm_ref[...], nv_ref[...] = gj, nm, nv

    ins = [t[k] for t in (w, m, v) for k in names]
    outs = pl.pallas_call(body, name=name, out_shape=[_S(w[k].shape) for _ in range(4) for k in names],
                          compiler_params=pltpu.CompilerParams(vmem_limit_bytes=VMEM_LIMIT))(packed, *ins)
    return [{k: outs[t * n + j] for j, k in enumerate(names)} for t in range(4)]


def _me():
    return lax.axis_index("x"), lax.axis_index("y"), lax.axis_index("c")


def _lin(dev):
    return 4 * dev[0] + 2 * dev[1] + dev[2]


def _blk(ref, axis, size, idx):
    nd = len(ref.shape)
    start = idx * size
    if axis == nd - 1 and size % LANE == 0:
        start = pl.multiple_of(start, LANE)
    elif axis == nd - 2 and size % 16 == 0:
        start = pl.multiple_of(start, 16)
    ix = [slice(None)] * nd
    ix[axis] = pl.ds(start, size)
    return ref.at[tuple(ix)]


HBM_SPEC = pl.BlockSpec(memory_space=pltpu.HBM)
SEM_SPEC = pl.BlockSpec(memory_space=pltpu.SEMAPHORE)
EFFECT = pltpu.SideEffectType.DATAFLOW_SIDE_EFFECTING


def _peers(x, y, c):
    flip = lambda v, f: 1 - v if f else v
    return [(flip(x, k & 4), flip(y, k & 2), flip(c, k & 1)) for k in range(1, N_DEV)]


def _land_shape(mode, s, axis):
    if mode == "gather":
        return s.shape[:axis] + (N_DEV * s.shape[axis],) + s.shape[axis + 1:]
    return (N_DEV,) + s.shape[:axis] + (s.shape[axis] // N_DEV,) + s.shape[axis + 1:]


def _src_view(mode, ref, axis, peer):
    return ref if mode == "gather" else _blk(ref, axis, ref.shape[axis] // N_DEV, peer)


def _dst_view(mode, land, axis, sender):
    return _blk(land, axis, land.shape[axis] // N_DEV, sender) if mode == "gather" else land.at[sender]


def _blocks(mode, land, axis, k):
    if mode == "gather":
        ix = [slice(None)] * len(land.shape)
        ix[axis] = pl.ds(0, k * (land.shape[axis] // N_DEV))
        return land.at[tuple(ix)]
    return land.at[pl.ds(0, k)]


ARRIVALS = {None: N_DEV - 1, "near": 4, "relay": 3}


def _routes(route, x, y, c):
    me, sibling = (x, y, c), (x, y, 1 - c)
    chips = [(1 - x, y), (x, 1 - y), (1 - x, 1 - y)]
    if route == "near":
        return [(me, sibling)] + [(me, (*chip, c)) for chip in chips]
    if route == "relay":
        return [((*chip, c), sibling) for chip in chips]
    return [(me, peer) for peer in _peers(x, y, c)]


def _place_own(mode, srcs, axes, name, after=None):
    n = len(srcs)
    extra, extra_specs = _after(after)

    def body(me_ref, *refs):
        for a in range(n):
            out = refs[n + len(extra) + a]
            out[...] = refs[a][...].reshape(out.shape)

    def at_me(shape, axis):
        return lambda i, me: tuple(me[0] if d == axis else 0 for d in range(len(shape)))

    in_specs, out_specs = [], []
    for s, axis in zip(srcs, axes):
        if mode == "gather":
            in_specs.append(pl.BlockSpec(s.shape, lambda i, me, nd=len(s.shape): (0,) * nd))
            out_specs.append(pl.BlockSpec(s.shape, at_me(s.shape, axis)))
        else:
            blk = s.shape[:axis] + (s.shape[axis] // N_DEV,) + s.shape[axis + 1:]
            in_specs.append(pl.BlockSpec(blk, at_me(blk, axis)))
            out_specs.append(pl.BlockSpec((1,) + blk, at_me((1,) + blk, 0)))
    me = _lin(_me()).astype(jnp.int32).reshape(1)
    return pl.pallas_call(
        body, name=name, out_shape=[_S(_land_shape(mode, s, a), s.dtype) for s, a in zip(srcs, axes)],
        grid_spec=pltpu.PrefetchScalarGridSpec(num_scalar_prefetch=1, grid=(1,), in_specs=in_specs + extra_specs,
                                               out_specs=out_specs),
        compiler_params=_params(1))(me, *srcs, *extra)


def _place_shards(shards, layers, axes, dtypes, name, after=None):
    n = len(shards)
    extra, extra_specs = _after(after)

    def body(me_ref, *refs):
        for a in range(n):
            out = refs[n + len(extra) + a]
            out[...] = refs[a][...].astype(out.dtype)

    in_specs, out_specs, out_shape = [], [], []
    for s, layer, axis, dt in zip(shards, layers, axes, dtypes):
        shape = s.shape if layer is None else s.shape[1:]
        nd = len(shape)
        if layer is None:
            in_specs.append(pl.BlockSpec(shape, lambda i, me, nd=nd: (0,) * nd))
        else:
            in_specs.append(pl.BlockSpec((None,) + shape, lambda i, me, nd=nd, layer=layer: (layer,) + (0,) * nd))
        out_specs.append(pl.BlockSpec(shape, lambda i, me, nd=nd, axis=axis: tuple(me[0] if d == axis else 0 for d in range(nd))))
        out_shape.append(_S(shape[:axis] + (N_DEV * shape[axis],) + shape[axis + 1:], dt))
    me = _lin(_me()).astype(jnp.int32).reshape(1)
    return pl.pallas_call(
        body, name=name, out_shape=out_shape,
        grid_spec=pltpu.PrefetchScalarGridSpec(num_scalar_prefetch=1, grid=(1,), in_specs=in_specs + extra_specs,
                                               out_specs=out_specs),
        compiler_params=_params(1))(me, *shards, *extra)


def _push_start(mode, srcs, lands, axes, name, route=None):
    n, ns = len(lands), len(srcs)

    def body(*refs):
        src_refs, land_refs = refs[:ns], refs[ns:ns + n]
        send_sems, recv_sems = refs[ns + n], refs[ns + n + 1]
        token = refs[-1]
        x, y, c = _me()
        for a in range(n):
            for block, peer in _routes(route, x, y, c):
                there = _dst_view(mode, land_refs[a], axes[a], _lin(block))
                pltpu.make_async_remote_copy(
                    src_ref=_src_view(mode, src_refs[a], axes[a], _lin(peer)) if ns else there, dst_ref=there,
                    send_sem=send_sems.at[a], recv_sem=recv_sems.at[a], device_id=peer, device_id_type=MESH).start()
        token[...] = jnp.zeros_like(token)

    hbm = lambda s: pltpu.HBM(s.shape, s.dtype)
    outs = pl.pallas_call(
        body, name=name,
        out_shape=(pltpu.SemaphoreType.DMA((n,)), pltpu.SemaphoreType.DMA((n,)), *[hbm(s) for s in srcs], *[hbm(s) for s in lands],
                   _S((SUB, LANE))),
        in_specs=[HBM_SPEC] * (ns + n),
        out_specs=(SEM_SPEC, SEM_SPEC, *[HBM_SPEC] * (ns + n), pl.BlockSpec(memory_space=pltpu.VMEM)),
        input_output_aliases={i: 2 + i for i in range(ns + n)},
        compiler_params=pltpu.CompilerParams(has_side_effects=EFFECT),
    )(*[pltpu.with_memory_space_constraint(s, pltpu.HBM) for s in list(srcs) + list(lands)])
    return outs[0], outs[1], outs[2:2 + ns], outs[2 + ns:2 + ns + n], outs[-1]


def _push_wait(mode, send_sems, recv_sems, srcs, lands, axes, after, name, first=0, route=None):
    n, ns = len(lands), len(srcs)
    after = list(after) if isinstance(after, (list, tuple)) else [after]

    def body(*refs):
        land_refs = refs[ns:ns + n]
        send_sems, recv_sems = refs[ns + n], refs[ns + n + 1]
        x, y, c = _me()
        for a in range(n):
            seven = _blocks(mode, land_refs[a], axes[a], ARRIVALS[route])
            cp = pltpu.make_async_remote_copy(src_ref=seven, dst_ref=seven, send_sem=send_sems.at[first + a],
                                              recv_sem=recv_sems.at[first + a],
                                              device_id=(x, y, 1 - c), device_id_type=MESH)
            cp.wait_send()
            cp.wait_recv()

    hbm = lambda s: pltpu.HBM(s.shape, s.dtype)
    outs = pl.pallas_call(
        body, name=name, out_shape=tuple(hbm(s) for s in list(srcs) + list(lands)),
        in_specs=[HBM_SPEC] * (ns + n) + [SEM_SPEC, SEM_SPEC] + [ANY] * len(after), out_specs=tuple([HBM_SPEC] * (ns + n)),
        input_output_aliases={i: i for i in range(ns + n)},
        compiler_params=pltpu.CompilerParams(has_side_effects=EFFECT),
    )(*srcs, *lands, send_sems, recv_sems, *after)
    return outs[ns:]


def _sum_parts(parts):
    n, R, C = parts.shape

    def body(p_ref, o_ref):
        g = p_ref[0]
        for k in range(1, n):
            g = g + p_ref[k]
        o_ref[...] = g

    return pl.pallas_call(body, name="sum_parts", out_shape=_S((R, C)))(parts)


SMALL =['conv_b', 'rg_wa', 'rg_ba', 'rg_wx', 'rg_bx', 'rg_lambda', 's5_a_re', 's5_a_im', 's5_b_re', 's5_b_im',
         's5_c_re', 's5_c_im', 's5_d', 's5_log_step', 's5_b_glu', 'ln1_g', 'ln1_b', 'ple_gate_b', 'ln2_g', 'ln2_b']
WEIGHTS = ['w_in', 'conv_w', 'conv_b', 'rg_wa', 'rg_ba', 'rg_wx', 'rg_bx', 'rg_lambda', 's5_a_re', 's5_a_im', 's5_b_re',
           's5_b_im', 's5_c_re', 's5_c_im', 's5_d', 's5_log_step', 's5_w_glu', 's5_b_glu', 'w_out', 'ln1_g', 'ln1_b',
           'ple_w', 'ple_gate_w', 'ple_gate_b', 'ln2_g', 'ln2_b']
PACK_ROWS_MULT = 64


STORED = {'s5_b_re': (2, 3), 's5_b_im': (2, 3), 's5_d': (1, 2)}


def _stored(k, a):
    return jnp.swapaxes(a, *STORED[k]) if k in STORED else a


def _two_d(a):
    return a.reshape(-1, a.shape[-1])


def _up8(n):
    return -(-n // SUB) * SUB


def _halves_fit(shape):
    return 2 * shape[1] == LANE and shape[0] % (2 * SUB) == 0


def _packed_rows(shape):
    R, C = shape
    if C % LANE == 0:
        return (C // LANE) * _up8(R)
    return R // 2 if _halves_fit(shape) else _up8(R)


def _packed_put(out_ref, r0, pieces):
    R, C = sum(a.shape[0] for a in pieces), pieces[0].shape[1]
    if _halves_fit((R, C)):
        lo, hi = pieces if len(pieces) == 2 else (pieces[0][:R // 2], pieces[0][R // 2:])
        out_ref[r0:r0 + R // 2, :] = jnp.concatenate([lo, hi], axis=1)
        return
    for a in pieces:
        rows = a.shape[0]
        if C % LANE == 0:
            for j in range(C // LANE):
                out_ref[r0 + j * _up8(R):r0 + j * _up8(R) + rows, :] = a[:, j * LANE:(j + 1) * LANE]
        else:
            out_ref[r0:r0 + rows, :C] = a
        r0 += rows


def _packed_get(ref, r0, shape):
    R, C = shape
    if C % LANE == 0:
        return jnp.concatenate([ref[r0 + j * _up8(R):r0 + j * _up8(R) + R, :] for j in range(C // LANE)], axis=1)
    if _halves_fit(shape):
        both = ref[r0:r0 + R // 2, :]
        return jnp.concatenate([both[:, :C], both[:, C:]], axis=0)
    return ref[r0:r0 + R, :C]


def _packed_starts(shapes):
    starts = [0]
    for s in shapes:
        starts.append(starts[-1] + _packed_rows(s))
    return starts[:-1], starts[-1] + (-starts[-1] % PACK_ROWS_MULT)


def _pack(tree, scalar):
    groups = [[_two_d(a) for a in (tree[k] if isinstance(tree[k], list) else [tree[k]])] for k in SMALL]
    groups.append([scalar.reshape(1, 1)])
    starts, rows = _packed_starts([(sum(a.shape[0] for a in g), g[0].shape[1]) for g in groups])

    def body(*refs):
        out_ref, refs = refs[-1], list(refs[:-1])
        out_ref[...] = jnp.zeros_like(out_ref)
        for r0, g in zip(starts, groups):
            _packed_put(out_ref, r0, [refs.pop(0)[...] for _ in g])

    return pl.pallas_call(body, name="pack_small", out_shape=_S((rows, LANE)))(*sum(groups, []))


class _NoHooks:
    token = None
    first_token = None

    def first_weights(self, full, after):
        return full

    def layer_start(self, i, W, after):
        return W

    def late_weights(self, i, W, after):
        return W

    def post_done(self, i, g):
        return None

    def smalls_done(self, grads, loss):
        self.small = _small_grads(grads, self.res)
        return None

    def w_in_done(self, i, g):
        return None

    def layer_done(self, i, g, dx):
        return None


def _local_grads(x, p, target, W, disc, hooks):
    depth = 2
    saved = []
    for i in range(depth):
        if i > 0:
            W = hooks.layer_start(i, W, x)
        w = W[i]
        z = _inproj_fwd(x, w['w_in'], hooks.token if i == 0 else None)
        hs, *gates = _rg_fwd(z, w['conv_w'], w['conv_b'], w['wa_bd'], w['wx_bd'], w['rg_ba'], w['rg_bx'], w['rg_lambda'], i)
        d = disc[i]
        y0, s_re, s_im = _s5_fwd(z, d['bb_re'], d['bb_im'], d['lb_re'], d['lb_im'], d['c_re'], d['c_im'], w['s5_d'], i)
        W = hooks.late_weights(i, W, y0)
        w = W[i]
        x2, *norms = _post_fwd(x, hs, z, y0, p, w['s5_w_glu'], w['s5_b_glu'], w['w_out'], w['ln1_g'], w['ln1_b'],
                               w['ple_w'], w['ple_gate_w'], w['ple_gate_b'], w['ln2_g'], w['ln2_b'], i)
        saved.append((x, z, hs, gates, y0, s_re, s_im, norms))
        x = x2

    grads = [None] * depth
    dx = target
    loss = None
    token = None
    for i in reversed(range(depth)):
        w, d = W[i], disc[i]
        xin, z, hs, gates, y0, s_re, s_im, (xh1, xh2, gt, rstd1, rstd2) = saved[i]
        g = {}
        (dt1, g['ple_w'], g['ple_gate_w'], g['ple_gate_b'], g['ln1_g'], g['ln1_b'], g['ln2_g'], g['ln2_b'], lrow) = _post_bwd_a(
            dx, i == depth - 1, xh2, xh1, rstd2, rstd1, gt, p, w['ple_w'], w['ple_gate_w'], w['ln1_g'], w['ln1_b'],
            w['ln2_g'], w['ln2_b'], i, token)
        if i == depth - 1:
            loss = 0.5 / D_MODEL * jnp.sum(lrow)
        dhs, dy0, dzg, g['w_out'], g['s5_w_glu'], g['s5_b_glu'] = _post_bwd_b(dt1, z, hs, y0, w['w_out'], w['s5_w_glu'],
                                                                           w['s5_b_glu'], i)
        (dzu, g['bb_re'], g['bb_im'], g['lb_re'], g['lb_im'], g['c_re'], g['c_im'], g['s5_d']) = _s5_bwd(
            dy0, z, s_re, s_im, d['bb_re'], d['bb_im'], d['lb_re'], d['lb_im'], d['c_re'], d['c_im'], w['s5_d'], i,
            hooks.post_done(i, g))
        (dzx, g['conv_w'], g['conv_b'], g['wa_bd'], g['wx_bd'], g['rg_ba'], g['rg_bx'], g['rg_lambda']) = _rg_bwd(
            dhs, z, hs, gates, w['conv_w'], w['wa_bd'], w['wx_bd'], w['rg_lambda'], i)
        if i == 0:
            g['w_in'] = _inproj_bwd_dw(xin, dzx, dzg, dzu, hooks.smalls_done([g, grads[1]], loss))
            dx = _inproj_bwd_dx(dt1, dzx, dzg, dzu, w['w_in'], hooks.w_in_done(i, g))
        else:
            dx, g['w_in'] = _inproj_bwd(dt1, xin, dzx, dzg, dzu, w['w_in'])
        grads[i] = g
        token = hooks.layer_done(i, g, dx)
    return loss, dx, grads


def _s5_layouts_fwd(s5_a_re, s5_a_im, s5_log_step, s5_b_re, s5_b_im, s5_c_re, s5_c_im, token=None):
    depth = s5_a_re.shape[0]
    ar, ai = s5_a_re.reshape(depth * 24, S5_P), s5_a_im.reshape(depth * 24, S5_P)
    ls = s5_log_step.reshape(depth * 24, 1)
    lr, li, cr, ci = _s5_disc_fwd(ar, ai, ls, token)
    per_group = lambda a: a.reshape(depth * 24, 1, S5_P)
    as_c = lambda b: jnp.swapaxes(b, 2, 3).reshape(depth * 24, S5_H, S5_P)
    res = (ar, ai, ls, per_group(cr), per_group(ci), as_c(s5_b_re), as_c(s5_b_im))
    bbr, bbi = _s5_bscale_fwd(*res[3:])
    tiles = lambda a: a.reshape(depth * N_S5_T, S5_GT, S5_H, S5_P)
    rows = lambda a: a.reshape(depth * N_S5_T, S5_GT, S5_P)
    disc = dict(bb_re=tiles(bbr), bb_im=tiles(bbi), lb_re=rows(lr), lb_im=rows(li), c_re=tiles(s5_c_re), c_im=tiles(s5_c_im))
    return [disc] * depth, res


def _s5_layouts_bwd(grads, res):
    ar, ai, ls, cr, ci, br, bi = res
    depth = len(grads)
    stack = lambda k, shape: jnp.stack([g[k] for g in grads]).reshape(shape)
    groups, shape_c = (depth * 24, S5_H, S5_P), (depth, 24, S5_H, S5_P)
    dbr, dbi, dcr, dci = _s5_bscale_bwd(cr, ci, br, bi, stack('bb_re', groups), stack('bb_im', groups))
    gp = (depth * 24, S5_P)
    dar, dai, dls = _s5_disc_bwd(ar, ai, ls, stack('lb_re', gp), stack('lb_im', gp), dcr.reshape(gp), dci.reshape(gp))
    return dict(
        s5_a_re=dar.reshape(depth, 24, S5_P), s5_a_im=dai.reshape(depth, 24, S5_P), s5_log_step=dls.reshape(depth, 24),
        s5_b_re=dbr.reshape(shape_c), s5_b_im=dbi.reshape(shape_c),
        s5_c_re=stack('c_re', shape_c), s5_c_im=stack('c_im', shape_c))


LATE = ('w_out', 'ple_w', 'ple_gate_w', 's5_w_glu')


ROWS = ('conv_b', 'rg_ba', 'rg_bx', 'rg_lambda', 's5_d', 's5_b_glu', 'ln1_g', 'ln1_b', 'ple_gate_b', 'ln2_g', 'ln2_b')


def _shared_weights(full):
    shared = {k: full[k] for k in ROWS}
    shared.update(conv_w=full['conv_w'], wa_bd=full['rg_wa'], wx_bd=full['rg_wx'], s5_d=full['s5_d'].reshape(DEPTH, 1, S5_W))
    return shared


def _layer_weights(full, shared, i):
    return dict(shared, w_in=full['w_in'][i])


class _AllLocal(_NoHooks):
    def __init__(self, full):
        self.full = full

    def late_weights(self, i, W, after):
        W[i].update({k: self.full[k][i] for k in LATE})
        return W


def _full_grads(full, x, p, target, hooks=None):
    hooks = hooks or _AllLocal(full)
    disc, res = _s5_layouts_fwd(full['s5_a_re'], full['s5_a_im'], full['s5_log_step'], full['s5_b_re'], full['s5_b_im'],
                                full['s5_c_re'], full['s5_c_im'], hooks.first_token)
    full = hooks.first_weights(full, disc[-1]['bb_im'])
    shared = _shared_weights(full)
    W = [_layer_weights(full, shared, i) for i in range(2)]
    hooks.res = res
    loss, gx, grads = _local_grads(x, p, target, W, disc, hooks)
    out = dict(hooks.small)
    for k in SHARD_AXIS:
        out[k] = [g[k] for g in grads]
    return loss, gx, out


def _small_grads(grads, res):
    stack = lambda f: jnp.stack([f(g) for g in grads])
    out = _s5_layouts_bwd(grads, res)
    out['conv_w'] = stack(lambda g: g['conv_w'])
    for k in ('conv_b', 'rg_ba', 'rg_bx', 'rg_lambda', 's5_b_glu', 'ln1_g', 'ln1_b', 'ple_gate_b', 'ln2_g', 'ln2_b'):
        out[k] = [g[k] for g in grads]
    out['s5_d'] = _stored('s5_d', stack(lambda g: g['s5_d'][0]).reshape(2, 24, 16))
    out['rg_wa'] = [g['wa_bd'] for g in grads]
    out['rg_wx'] = [g['wx_bd'] for g in grads]
    return out


SHARD_AXIS = {'w_in': 2, 'w_out': 1, 'ple_w': 2, 'ple_gate_w': 1, 's5_w_glu': 1}


def kernel(x, p, w_in, conv_w, conv_b, rg_wa, rg_ba, rg_wx, rg_bx, rg_lambda, s5_a_re, s5_a_im, s5_b_re, s5_b_im, s5_c_re, s5_c_im, s5_d, s5_log_step, s5_w_glu, s5_b_glu, w_out, ln1_g, ln1_b, ple_w, ple_gate_w, ple_gate_b, ln2_g, ln2_b, loss_target, m_w_in, m_conv_w, m_conv_b, m_rg_wa, m_rg_ba, m_rg_wx, m_rg_bx, m_rg_lambda, m_s5_a_re, m_s5_a_im, m_s5_b_re, m_s5_b_im, m_s5_c_re, m_s5_c_im, m_s5_d, m_s5_log_step, m_s5_w_glu, m_s5_b_glu, m_w_out, m_ln1_g, m_ln1_b, m_ple_w, m_ple_gate_w, m_ple_gate_b, m_ln2_g, m_ln2_b, v_w_in, v_conv_w, v_conv_b, v_rg_wa, v_rg_ba, v_rg_wx, v_rg_bx, v_rg_lambda, v_s5_a_re, v_s5_a_im, v_s5_b_re, v_s5_b_im, v_s5_c_re, v_s5_c_im, v_s5_d, v_s5_log_step, v_s5_w_glu, v_s5_b_glu, v_w_out, v_ln1_g, v_ln1_b, v_ple_w, v_ple_gate_w, v_ple_gate_b, v_ln2_g, v_ln2_b):
    local = dict(locals())
    w = {k: local[k] for k in WEIGHTS}
    mom = {k: local['m_' + k] for k in WEIGHTS}
    var = {k: local['v_' + k] for k in WEIGHTS}

    big = list(SHARD_AXIS)
    late_axes = [SHARD_AXIS[k] - 1 for k in LATE]
    pushed = {}

    groups = dict(first=(['w_in', 'conv_w'], [0, None], [1, 0]), l0=(list(LATE), [0] * len(LATE), late_axes),
                  l1=(['w_in'] + list(LATE), [1] * (1 + len(LATE)), [1] + late_axes))
    token = None
    for key, members in (("first", ["first"]), ("rest", ["l0", "l1"])):
        names, layers, axes = (sum((groups[m][j] for m in members), []) for j in range(3))
        shards = [w[k] if layer is not None else w[k][None] for k, layer in zip(names, layers)]
        lands = _place_shards(shards, layers, axes, [WIRE if k in big else w[k].dtype for k in names],
                              "place_weights_" + key, token)
        pushed[key] = _push_start("gather", [], lands, axes, "push_weights_" + key, "near" if key == "first" else None)
        token = pushed[key][4]

    def await_weights(key, axes, after):
        s, first = pushed["rest"], 0 if key == "l0" else len(LATE)
        return _push_wait("gather", s[0], s[1], [], s[3][first:first + len(axes)], axes, after, "await_weights_" + key, first)

    def push_grads(key, g, names, axes):
        srcs = [g[k] for k in names]
        pushed[key] = _push_start("scatter", srcs, _place_own("scatter", srcs, axes, "place_grads_" + key), axes,
                                  "push_grads_" + key)
        return pushed[key][4]

    def await_grads(key, axes, after):
        s = pushed[key]
        return _push_wait("scatter", s[0], s[1], s[2], s[3], axes, after, "await_grads_" + key)

    class Overlap(_NoHooks):
        token = pushed["rest"][4]
        first_token = token

        def first_weights(self, full, after):
            s, axes = pushed["first"], [1, 0]
            near = _push_wait("gather", s[0], s[1], [], s[3], axes, after, "await_weights_near", route="near")
            s = _push_start("gather", [], near, axes, "relay_weights", "relay")
            w_in0, conv = _push_wait("gather", s[0], s[1], [], s[3], axes, s[4], "await_weights_relay", route="relay")
            return dict(full, w_in=[w_in0, None], conv_w=jnp.moveaxis(conv, 0, 2).reshape(2, 4, RG_W))

        def late_weights(self, i, W, after):
            if i == 0:
                W[0].update(zip(LATE, await_weights("l0", late_axes, after)))
            return W

        def layer_start(self, i, W, after):
            lands = await_weights("l1", [1] + late_axes, after)
            W[1].update(zip(LATE, lands[1:]), w_in=lands[0])
            return W

        def post_done(self, i, g):
            return push_grads("late0", g, LATE, late_axes) if i == 0 else None

        def smalls_done(self, grads, loss):
            super().smalls_done(grads, loss)
            conv = jnp.moveaxis(self.small['conv_w'].reshape(2, 4, N_DEV, RG_W // N_DEV), 2, 0)
            self.packed = _pack(self.small, loss)
            return push_grads("small", dict(conv_w=conv.reshape(N_DEV, 8, RG_W // N_DEV), small=self.packed),
                              ['conv_w', 'small'], [0, 0])

        def w_in_done(self, i, g):
            return push_grads("w_in0", g, ['w_in'], [0])

        def layer_done(self, i, g, dx):
            return push_grads("all1", g, ['w_in'] + list(LATE), [0] + late_axes) if i == 1 else None

    hooks = Overlap()
    _, grad_x, g = _full_grads(dict(w), x[0], p, loss_target[0], hooks)

    recv1 = dict(zip(['w_in'] + list(LATE), await_grads("all1", [0] + late_axes, grad_x)))
    recv0 = dict(zip(LATE, await_grads("late0", late_axes, grad_x)))
    outs = {}

    def update(k, parts, token=None):
        shard = w[k].shape
        c = shard[-1]
        two = lambda a: a.reshape(-1, c)
        res = _adamw([r.reshape(N_DEV, -1, c) for r in parts], two(w[k]), two(mom[k]), two(var[k]), token)
        outs[k] = [o.reshape(shard) for o in res]

    conv_parts, small_parts = await_grads("small", [0, 0], grad_x)
    rows = hooks.packed.shape[0] // N_DEV
    mine = _sum_parts(small_parts.reshape(N_DEV, rows, LANE))
    sums = _push_start("gather", [mine], _place_own("gather", [mine], [0], "place_small_sums"), [0], "push_small_sums")
    late = _adamw_sharded(list(LATE), [recv0, recv1], w, mom, var, "adamw_late", sums[4])
    for k in LATE:
        outs[k] = [t[k] for t in late]
    w_in0, = await_grads("w_in0", [0], [outs[k][1] for k in LATE])
    update('w_in', [w_in0, recv1['w_in']])
    update('conv_w', [conv_parts])
    gathered, = _push_wait("gather", sums[0], sums[1], sums[2], sums[3], [0], [outs['w_in'][1], outs['conv_w'][1]],
                           "await_small_sums")
    stored = [{k: _two_d(_stored(k, t[k])) for k in SMALL} for t in (w, mom, var)]
    loss = gathered[_packed_starts([stored[0][k].shape for k in SMALL] + [(1, 1)])[0][-1], 0]
    updated = _adamw_packed(SMALL, gathered, *stored, "adamw_small")
    for k in SMALL:
        shape = _stored(k, w[k]).shape
        outs[k] = [_stored(k, o[k].reshape(shape)) for o in updated]

    res = [loss, grad_x[None]]
    for j in range(4):
        res += [outs[k][j] for k in WEIGHTS]
    return tuple(res)
```

```python
import math

import jax
import jax.numpy as jnp
from jax import lax
from jax.experimental import pallas as pl
from jax.experimental.pallas import tpu as pltpu

F32 = jnp.float32
MXU = jnp.bfloat16
WIRE = jnp.bfloat16

N_DEV = 8
D_MODEL = 1024
PLE_D = 256
RG_W = 640
S5_W = 384
S5_P = 64
S5_N = 24 * S5_P
Z_W = 2 * RG_W + 2 * S5_W
C_RGG = RG_W
C_S5U = 2 * RG_W
C_S5G = 2 * RG_W + S5_W
LANE = 128
N_RG_T = RG_W // LANE
N_S5_T = S5_W // LANE
W_BLK = Z_W // N_DEV
ALPHA = (2.0 * 2) ** 0.25
LN_EPS = 1e-5
RG_C = 8.0
LR, B1, B2, EPS, WD, STEP = 0.001, 0.9, 0.999, 1e-08, 0.01, 10
BC1 = 1.0 - B1 ** STEP
BC2 = 1.0 - B2 ** STEP
RC = 512
RC_RG = 1024
TM = 512
TM_MM = 1024
VMEM_LIMIT = 56 * 1024 * 1024

MESH = pl.DeviceIdType.MESH
ANY = pl.BlockSpec(memory_space=pl.ANY)


def _params(n_grid_axes, vmem=VMEM_LIMIT):
    return pltpu.CompilerParams(dimension_semantics=("arbitrary",) * n_grid_axes, vmem_limit_bytes=vmem)


def _S(shape, dtype=F32):
    return jax.ShapeDtypeStruct(tuple(shape), dtype)


def _sigmoid(x):
    return 0.5 * jnp.tanh(0.5 * x) + 0.5


def _silu_and_grad(x):
    s = _sigmoid(x)
    return x * s, s * (1.0 + x * (1.0 - s))


_GELU_C = math.sqrt(2.0 / math.pi)


def _gelu(x):
    return 0.5 * x * (1.0 + jnp.tanh(_GELU_C * (x + 0.044715 * (x * x * x))))


def _gelu_grad(x):
    th = jnp.tanh(_GELU_C * (x + 0.044715 * (x * x * x)))
    return 0.5 * (1.0 + th) + 0.5 * x * (1.0 - th * th) * (_GELU_C * (1.0 + 3.0 * 0.044715 * (x * x)))


def _mm(a, b):
    return jnp.dot(a.astype(MXU), b.astype(MXU), preferred_element_type=F32)


def _mm_nt(a, b):
    return lax.dot_general(a.astype(MXU), b.astype(MXU), (((1,), (1,)), ((), ())), preferred_element_type=F32)


def _mm_tn(a, b):
    return lax.dot_general(a.astype(MXU), b.astype(MXU), (((0,), (0,)), ((), ())), preferred_element_type=F32)


def _ln_fwd(t, g, b):
    mu = jnp.mean(t, axis=-1, keepdims=True)
    tc = t - mu
    var = jnp.mean(tc * tc, axis=-1, keepdims=True)
    rstd = lax.rsqrt(var + LN_EPS)
    xhat = tc * rstd
    return xhat * g + b, xhat, rstd


def _ln_bwd(dy, xhat, rstd, g):
    dxh = dy * g
    m1 = jnp.mean(dxh, axis=-1, keepdims=True)
    m2 = jnp.mean(dxh * xhat, axis=-1, keepdims=True)
    return rstd * (dxh - m1 - xhat * m2)


def _colsum(a):
    return jnp.sum(a, axis=0, keepdims=True)


def _up(x, d, rows, fill):
    n = x.shape[0]
    return jnp.where(rows < n - d, pltpu.roll(x, n - d, 0), fill)


SUB = 8
TILE_STEPS = (1, 2, 4)


def _r8(width):
    return lax.broadcasted_iota(jnp.int32, (SUB, width), 0)


def _scan_real(a, u, carry, reverse=False):
    r8 = _r8(a.shape[1])
    n = a.shape[0] // SUB
    outs = [None] * n
    for k in (reversed(range(n)) if reverse else range(n)):
        A, U = a[SUB * k:SUB * k + SUB], u[SUB * k:SUB * k + SUB]
        for d in TILE_STEPS:
            m = (r8 < SUB - d) if reverse else (r8 >= d)
            sh = SUB - d if reverse else d
            U = A * jnp.where(m, pltpu.roll(U, sh, 0), 0.0) + U
            A = A * jnp.where(m, pltpu.roll(A, sh, 0), 1.0)
        h = A * carry + U
        outs[k] = h
        carry = h[0:1] if reverse else h[SUB - 1:SUB]
    return jnp.concatenate(outs, axis=0), carry


def _tile_powers(lr, li, reverse=False):
    width = lr.shape[1]
    r8 = _r8(width)
    steps = []
    pr, pi = lr, li
    er, ei = jnp.broadcast_to(lr, (SUB, width)), jnp.broadcast_to(li, (SUB, width))
    for d in TILE_STEPS:
        m = (r8 < SUB - d) if reverse else (r8 >= d)
        sh = SUB - d if reverse else d
        steps.append((sh, jnp.where(m, pr, 0.0), jnp.where(m, pi, 0.0)))
        er, ei = _cmul(er, ei, jnp.where(m, pltpu.roll(er, sh, 0), 1.0), jnp.where(m, pltpu.roll(ei, sh, 0), 0.0))
        pr, pi = _cmul(pr, pi, pr, pi)
    return steps, (er, ei)


def _scan_lti(xr, xi, carry, steps, e, reverse=False):
    er, ei = e
    kr, ki = carry
    n = xr.shape[0] // SUB
    outr, outi = [None] * n, [None] * n
    for k in (reversed(range(n)) if reverse else range(n)):
        sr, si = xr[SUB * k:SUB * k + SUB], xi[SUB * k:SUB * k + SUB]
        for sh, pr, pi in steps:
            shr, shi = pltpu.roll(sr, sh, 0), pltpu.roll(si, sh, 0)
            sr, si = sr + (pr * shr - pi * shi), si + (pr * shi + pi * shr)
        sr = sr + (er * kr - ei * ki)
        si = si + (er * ki + ei * kr)
        outr[k], outi[k] = sr, si
        kr, ki = (sr[0:1], si[0:1]) if reverse else (sr[SUB - 1:SUB], si[SUB - 1:SUB])
    return jnp.concatenate(outr, axis=0), jnp.concatenate(outi, axis=0), (kr, ki)


def _halo(ref, c, r0):
    rp = pl.multiple_of(jnp.maximum(r0 - 8, 0), 8)
    return jnp.where(c > 0, ref[pl.ds(rp, 8), :], 0.0)


def _conv_taps(xe):
    return [pltpu.roll(xe, 3, 0)[8:, :], pltpu.roll(xe, 2, 0)[8:, :], pltpu.roll(xe, 1, 0)[8:, :], xe[8:, :]]


def _rg_gates(h, wa, wx, ba, bx, sp):
    r = _sigmoid(_mm(h, wa) + ba)
    i = _sigmoid(_mm(h, wx) + bx)
    log_a = (-RG_C) * r * sp
    a = jnp.exp(log_a)
    mult = jnp.sqrt(-jnp.tanh(log_a) * (a * a + 1.0))
    return r, i, a, mult


def _softplus(y):
    return jnp.maximum(y, 0.0) + jnp.log1p(jnp.exp(-jnp.abs(y)))


def _after(token):
    return ([], []) if token is None else ([token], [ANY])


def _inproj_fwd(x, w_in, token=None):
    L = x.shape[0]

    def body(x_ref, w_ref, *rest):
        rest[-1][...] = _mm(x_ref[...], w_ref[...])

    extra, extra_specs = _after(token)
    tm = min(TM_MM, L)
    return pl.pallas_call(
        body, name="inproj_fwd", grid=(L // tm,),
        in_specs=[pl.BlockSpec((tm, D_MODEL), lambda i: (i, 0)), pl.BlockSpec((D_MODEL, Z_W), lambda i: (0, 0))] + extra_specs,
        out_specs=pl.BlockSpec((tm, Z_W), lambda i: (i, 0)),
        out_shape=_S((L, Z_W)), compiler_params=_params(1))(x, w_in, *extra)


def _inproj_bwd(dt1, x, dzx, dzg, dzu, w_in):
    L = x.shape[0]

    def body(dt1_ref, x_ref, dzx_ref, dzg_ref, dzu_ref, w_ref, dx_ref, dw_ref, acc_ref):
        @pl.when(pl.program_id(0) == 0)
        def _():
            acc_ref[...] = jnp.zeros_like(acc_ref)
        dzg = dzg_ref[...]
        dz = jnp.concatenate([dzx_ref[...], dzg[:, :RG_W], dzu_ref[...], dzg[:, RG_W:]], axis=1).astype(MXU)
        xb = x_ref[...].astype(MXU)
        dx_ref[...] = ALPHA * dt1_ref[...] + _mm_nt(dz, w_ref[...])
        for j in range(N_DEV):
            acc_ref[j] += _mm_tn(xb, dz[:, j * W_BLK:(j + 1) * W_BLK])

        @pl.when(pl.program_id(0) == L // TM - 1)
        def _():
            dw_ref[...] = acc_ref[...].astype(WIRE)

    row = lambda w: pl.BlockSpec((TM, w), lambda i: (i, 0))
    wspec = pl.BlockSpec((N_DEV, D_MODEL, W_BLK), lambda i: (0, 0, 0))
    return pl.pallas_call(
        body, name="inproj_bwd", grid=(L // TM,),
        in_specs=[row(D_MODEL), row(D_MODEL), row(RG_W), row(D_MODEL), row(S5_W),
                  pl.BlockSpec((D_MODEL, Z_W), lambda i: (0, 0))],
        out_specs=[row(D_MODEL), wspec],
        out_shape=[_S((L, D_MODEL)), _S((N_DEV, D_MODEL, W_BLK), WIRE)],
        scratch_shapes=[pltpu.VMEM((N_DEV, D_MODEL, W_BLK), F32)],
        compiler_params=_params(1))(dt1, x, dzx, dzg, dzu, w_in)


TM2 = 1024


def _dz_block(dzx_ref, dzg_ref, dzu_ref):
    dzg = dzg_ref[...]
    return jnp.concatenate([dzx_ref[...], dzg[:, :RG_W], dzu_ref[...], dzg[:, RG_W:]], axis=1).astype(MXU)


def _inproj_bwd_dw(x, dzx, dzg, dzu, token=None):
    L = x.shape[0]
    extra, extra_specs = _after(token)

    def body(x_ref, dzx_ref, dzg_ref, dzu_ref, *rest):
        dw_ref, acc_ref = rest[len(extra):]
        @pl.when(pl.program_id(0) == 0)
        def _():
            acc_ref[...] = jnp.zeros_like(acc_ref)
        dz = _dz_block(dzx_ref, dzg_ref, dzu_ref)
        xb = x_ref[...].astype(MXU)
        for j in range(N_DEV):
            acc_ref[j] += _mm_tn(xb, dz[:, j * W_BLK:(j + 1) * W_BLK])

        @pl.when(pl.program_id(0) == L // TM2 - 1)
        def _():
            dw_ref[...] = acc_ref[...].astype(WIRE)

    row = lambda w: pl.BlockSpec((TM2, w), lambda i: (i, 0))
    wspec = pl.BlockSpec((N_DEV, D_MODEL, W_BLK), lambda i: (0, 0, 0))
    return pl.pallas_call(
        body, name="inproj_bwd_dw", grid=(L // TM2,),
        in_specs=[row(D_MODEL), row(RG_W), row(D_MODEL), row(S5_W)] + extra_specs, out_specs=wspec,
        out_shape=_S((N_DEV, D_MODEL, W_BLK), WIRE), scratch_shapes=[pltpu.VMEM((N_DEV, D_MODEL, W_BLK), F32)],
        compiler_params=_params(1))(x, dzx, dzg, dzu, *extra)


def _inproj_bwd_dx(dt1, dzx, dzg, dzu, w_in, token=None):
    L = dt1.shape[0]
    extra, extra_specs = _after(token)

    def body(dt1_ref, dzx_ref, dzg_ref, dzu_ref, w_ref, *rest):
        rest[-1][...] = ALPHA * dt1_ref[...] + _mm_nt(_dz_block(dzx_ref, dzg_ref, dzu_ref), w_ref[...])

    tm = min(TM_MM, L)
    row = lambda w: pl.BlockSpec((tm, w), lambda i: (i, 0))
    return pl.pallas_call(
        body, name="inproj_bwd_dx", grid=(L // tm,),
        in_specs=[row(D_MODEL), row(RG_W), row(D_MODEL), row(S5_W), _full((D_MODEL, Z_W))] + extra_specs,
        out_specs=row(D_MODEL), out_shape=_S((L, D_MODEL)), compiler_params=_params(1))(dt1, dzx, dzg, dzu, w_in, *extra)


def _rg_specs(layer):
    tile = lambda rows: pl.BlockSpec((rows, LANE), lambda c: (0, c))
    ptile = lambda rows: pl.BlockSpec((None, rows, LANE), lambda c: (layer, 0, c))
    pheads = pl.BlockSpec((None, 2, RG_HD, RG_HD), lambda c: (layer, c, 0, 0))
    return tile, ptile, pheads, pl.BlockSpec((2, RG_HD, RG_HD), lambda c: (c, 0, 0))


RG_HD = 64


def _bd2(w):
    z = jnp.zeros((RG_HD, RG_HD), w.dtype)
    return jnp.concatenate([jnp.concatenate([w[0], z], axis=1), jnp.concatenate([z, w[1]], axis=1)], axis=0)


def _bd2_diag(m):
    return jnp.stack([m[:RG_HD, :RG_HD], m[RG_HD:, RG_HD:]])


def _rg_fwd(z, cw, cb, wa_bd, wx_bd, ba, bx, lam, layer):
    L = z.shape[0]
    RC = min(RC_RG, L)

    def body(x_ref, cw_ref, cb_ref, wa_ref, wx_ref, ba_ref, bx_ref, lam_ref, hs_ref, *saved):
        row = slice(layer, layer + 1)
        w, b = cw_ref[...], cb_ref[row, :]
        wa, wx, ba_, bx_ = _bd2(wa_ref[...]).astype(MXU), _bd2(wx_ref[...]).astype(MXU), ba_ref[row, :], bx_ref[row, :]
        sp = _softplus(-lam_ref[row, :])

        def step(c, carry):
            r0 = pl.multiple_of(c * RC, RC)
            xe = jnp.concatenate([_halo(x_ref, c, r0), x_ref[pl.ds(r0, RC), :]], axis=0)
            t = _conv_taps(xe)
            h = t[0] * w[0:1] + t[1] * w[1:2] + t[2] * w[2:3] + t[3] * w[3:4] + b
            r, i, a, mult = _rg_gates(h, wa, wx, ba_, bx_, sp)
            hs, carry = _scan_real(a, mult * (i * h), carry)
            hs_ref[pl.ds(r0, RC), :] = hs
            for ref, val in zip(saved, (h, r, i, a, mult)):
                ref[pl.ds(r0, RC), :] = val
            return carry

        lax.fori_loop(0, L // RC, step, jnp.zeros((1, LANE), F32))

    tile, ptile, pheads, _ = _rg_specs(layer)
    return pl.pallas_call(
        body, name="rg_fwd", grid=(N_RG_T,),
        in_specs=[tile(L), ptile(4), tile(2), pheads, pheads, tile(2), tile(2), tile(2)],
        out_specs=[tile(L)] * 6, out_shape=[_S((L, RG_W))] * 6, compiler_params=_params(1))(
            z, cw, cb, wa_bd, wx_bd, ba, bx, lam)


def _rg_bwd(dhs, z, hs, gates, cw, wa_bd, wx_bd, lam, layer):
    L = z.shape[0]
    RC = min(RC_RG, L)

    def body(g_ref, x_ref, hs_ref, h_ref, r_ref, i_ref, a_ref, mult_ref, cw_ref, wa_ref, wx_ref, lam_ref,
             dx_ref, dcw_ref, dcb_ref, dwa_out, dwx_out, dba_ref, dbx_ref, dlam_ref, dwa_ref, dwx_ref):
        w = cw_ref[...]
        wa, wx = _bd2(wa_ref[...]).astype(MXU), _bd2(wx_ref[...]).astype(MXU)
        lam = lam_ref[layer:layer + 1, :]
        sp = _softplus(-lam)
        rows = lax.broadcasted_iota(jnp.int32, (RC, LANE), 0)
        for ref in (dcw_ref, dcb_ref, dwa_ref, dwx_ref, dba_ref, dbx_ref, dlam_ref):
            ref[...] = jnp.zeros_like(ref)
        nch = L // RC

        def step(k, carry):
            cin, nxt = carry
            c = nch - 1 - k
            r0 = pl.multiple_of(c * RC, RC)
            xe = jnp.concatenate([_halo(x_ref, c, r0), x_ref[pl.ds(r0, RC), :]], axis=0)
            t = _conv_taps(xe)
            h, r, i, a, mult = (ref[pl.ds(r0, RC), :] for ref in (h_ref, r_ref, i_ref, a_ref, mult_ref))
            hs_e = jnp.concatenate([_halo(hs_ref, c, r0), hs_ref[pl.ds(r0, RC), :]], axis=0)
            hs_prev = pltpu.roll(hs_e, 1, 0)[8:, :]
            g = g_ref[pl.ds(r0, RC), :]
            cc, cin_new = _scan_real(a, a * g, cin, reverse=True)
            dh = g + _up(cc, 1, rows, cin)
            ih = i * h
            dlog_a = dh * hs_prev * a - (dh * ih) * (a * a) / mult
            di = dh * mult * h
            dhin = dh * mult * i
            dr = dlog_a * ((-RG_C) * sp)
            dlam_ref[...] += _colsum(dlog_a * r)
            dra = dr * r * (1.0 - r)
            dia = di * i * (1.0 - i)
            dwa_ref[...] += _mm_tn(h, dra)
            dwx_ref[...] += _mm_tn(h, dia)
            dba_ref[...] += _colsum(dra)
            dbx_ref[...] += _colsum(dia)
            dhin = dhin + _mm_nt(dra, wa) + _mm_nt(dia, wx)
            de = jnp.concatenate([dhin, nxt], axis=0)
            n = RC + 8
            dx = (dhin * w[3:4] + pltpu.roll(de, n - 1, 0)[:RC, :] * w[2:3]
                  + pltpu.roll(de, n - 2, 0)[:RC, :] * w[1:2] + pltpu.roll(de, n - 3, 0)[:RC, :] * w[0:1])
            dx_ref[pl.ds(r0, RC), :] = dx
            for kk in range(4):
                dcw_ref[kk:kk + 1, :] += _colsum(dhin * t[kk])
            dcb_ref[...] += _colsum(dhin)
            return cin_new, dhin[0:8, :]

        lax.fori_loop(0, nch, step, (jnp.zeros((1, LANE), F32), jnp.zeros((8, LANE), F32)))
        dlam_ref[...] = dlam_ref[...] * (RG_C * _sigmoid(-lam))
        dwa_out[...], dwx_out[...] = _bd2_diag(dwa_ref[...]), _bd2_diag(dwx_ref[...])

    tile, ptile, pheads, gheads = _rg_specs(layer)
    heads = _S((2 * N_RG_T, RG_HD, RG_HD))
    return pl.pallas_call(
        body, name="rg_bwd", grid=(N_RG_T,),
        in_specs=[tile(L)] * 8 + [ptile(4), pheads, pheads, tile(2)],
        out_specs=[tile(L), tile(4), tile(1), gheads, gheads, tile(1), tile(1), tile(1)],
        out_shape=[_S((L, RG_W)), _S((4, RG_W)), _S((1, RG_W)), heads, heads, _S((1, RG_W)), _S((1, RG_W)), _S((1, RG_W))],
        scratch_shapes=[pltpu.VMEM((LANE, LANE), F32), pltpu.VMEM((LANE, LANE), F32)],
        compiler_params=_params(1))(dhs, z, hs, *gates, cw, wa_bd, wx_bd, lam)


def _cmul(ar, ai, br, bi):
    return ar * br - ai * bi, ar * bi + ai * br


S5_TW = S5_N // N_S5_T


S5_H = 16
S5_GT = LANE // S5_H


def _s5_specs(L, layer):
    in_tile = pl.BlockSpec((L, LANE), lambda t: (0, t))
    st = pl.BlockSpec((L, S5_TW), lambda t: (0, t))
    pg = pl.BlockSpec((None, S5_GT, S5_H, S5_P), lambda t: (layer * N_S5_T + t, 0, 0, 0))
    plb = pl.BlockSpec((None, S5_GT, S5_P), lambda t: (layer * N_S5_T + t, 0, 0))
    gg = pl.BlockSpec((None, S5_GT, S5_H, S5_P), lambda t: (t, 0, 0, 0))
    glb = pl.BlockSpec((None, S5_GT, S5_P), lambda t: (t, 0, 0))
    dv = pl.BlockSpec((1, LANE), lambda t: (0, t))
    return in_tile, st, pg, plb, gg, glb, dv


def _bd8(blocks):
    rows = []
    for g in range(S5_GT):
        pieces = [blocks[g]]
        if g:
            pieces.insert(0, jnp.zeros((S5_H, S5_P * g), blocks.dtype))
        if g < S5_GT - 1:
            pieces.append(jnp.zeros((S5_H, S5_P * (S5_GT - 1 - g)), blocks.dtype))
        rows.append(jnp.concatenate(pieces, axis=1))
    return jnp.concatenate(rows, axis=0)


def _bd8_diag(m):
    return jnp.stack([m[S5_H * g:S5_H * (g + 1), S5_P * g:S5_P * (g + 1)] for g in range(S5_GT)])


def _row8(v):
    return jnp.concatenate([v[g:g + 1] for g in range(S5_GT)], axis=1)


def _row8_split(r):
    return jnp.concatenate([r[:, S5_P * g:S5_P * (g + 1)] for g in range(S5_GT)], axis=0)


def _layer_row_tile(layer):
    return pl.BlockSpec((None, 1, LANE), lambda t: (layer, 0, t))


def _s5_fwd(z, bb_re, bb_im, lb_re, lb_im, c_re, c_im, dvec, layer):
    L = z.shape[0]

    def body(u_ref, bbr_ref, bbi_ref, lr_ref, li_ref, cr_ref, ci_ref, d_ref, y_ref, sr_ref, si_ref):
        bbr, bbi = _bd8(bbr_ref[...]).astype(MXU), _bd8(bbi_ref[...]).astype(MXU)
        cr, ci = _bd8(cr_ref[...]).astype(MXU), _bd8(ci_ref[...]).astype(MXU)
        dv = d_ref[...]
        steps, e = _tile_powers(_row8(lr_ref[...]), _row8(li_ref[...]))

        def step(c, carry):
            r0 = pl.multiple_of(c * RC, RC)
            u = u_ref[pl.ds(r0, RC), :]
            ub = u.astype(MXU)
            sr = jnp.dot(ub, bbr, preferred_element_type=F32)
            si = jnp.dot(ub, bbi, preferred_element_type=F32)
            sr, si, carry = _scan_lti(sr, si, carry, steps, e)
            sr_ref[pl.ds(r0, RC), :] = sr
            si_ref[pl.ds(r0, RC), :] = si
            y_ref[pl.ds(r0, RC), :] = dv * u + (_mm_nt(sr, cr) - _mm_nt(si, ci))
            return carry

        zero = jnp.zeros((1, S5_TW), F32)
        lax.fori_loop(0, L // RC, step, (zero, zero))

    in_tile, st, pg, plb, _, _, _ = _s5_specs(L, layer)
    u_tile = pl.BlockSpec((L, LANE), lambda t: (0, C_S5U // LANE + t))
    return pl.pallas_call(
        body, name="s5_fwd", grid=(N_S5_T,),
        in_specs=[u_tile, pg, pg, plb, plb, pg, pg, _layer_row_tile(layer)],
        out_specs=[in_tile, st, st],
        out_shape=[_S((L, S5_W)), _S((L, S5_N)), _S((L, S5_N))],
        compiler_params=_params(1))(z, bb_re, bb_im, lb_re, lb_im, c_re, c_im, dvec)


def _s5_bwd(dy0, z, s_re, s_im, bb_re, bb_im, lb_re, lb_im, c_re, c_im, dvec, layer, token=None):
    L = z.shape[0]
    extra, extra_specs = _after(token)

    def body(dy_ref, u_ref, sr_ref, si_ref, bbr_ref, bbi_ref, lr_ref, li_ref, cr_ref, ci_ref, d_ref, *rest):
        (du_ref, dbbr_out, dbbi_out, dlr_out, dli_out, dcr_out, dci_out, dd_ref,
         dbbr_ref, dbbi_ref, dcr_ref, dci_ref, dlr_ref, dli_ref) = rest[len(extra):]
        bbr, bbi = _bd8(bbr_ref[...]).astype(MXU), _bd8(bbi_ref[...]).astype(MXU)
        cr, ci = _bd8(cr_ref[...]).astype(MXU), _bd8(ci_ref[...]).astype(MXU)
        lr, li = _row8(lr_ref[...]), -_row8(li_ref[...])
        dv = d_ref[...]
        steps, e = _tile_powers(lr, li, reverse=True)
        for ref in (dbbr_ref, dbbi_ref, dlr_ref, dli_ref, dcr_ref, dci_ref, dd_ref):
            ref[...] = jnp.zeros_like(ref)
        nch = L // RC

        def step(k, carry):
            c = nch - 1 - k
            r0 = pl.multiple_of(c * RC, RC)
            dy = dy_ref[pl.ds(r0, RC), :]
            u = u_ref[pl.ds(r0, RC), :]
            dyb, ub = dy.astype(MXU), u.astype(MXU)
            sr, si = sr_ref[pl.ds(r0, RC), :], si_ref[pl.ds(r0, RC), :]
            dcr_ref[...] += _mm_tn(dyb, sr)
            dci_ref[...] -= _mm_tn(dyb, si)
            gr = jnp.dot(dyb, cr, preferred_element_type=F32)
            gi = -jnp.dot(dyb, ci, preferred_element_type=F32)
            gr, gi, carry = _scan_lti(gr, gi, carry, steps, e, reverse=True)
            pr_ = pltpu.roll(jnp.concatenate([_halo(sr_ref, c, r0), sr], axis=0), 1, 0)[8:, :]
            pi_ = pltpu.roll(jnp.concatenate([_halo(si_ref, c, r0), si], axis=0), 1, 0)[8:, :]
            dlr_ref[...] += _colsum(pr_ * gr + pi_ * gi)
            dli_ref[...] += _colsum(pr_ * gi - pi_ * gr)
            grb, gib = gr.astype(MXU), gi.astype(MXU)
            dbbr_ref[...] += _mm_tn(ub, grb)
            dbbi_ref[...] += _mm_tn(ub, gib)
            du_ref[pl.ds(r0, RC), :] = dv * dy + (_mm_nt(grb, bbr) + _mm_nt(gib, bbi))
            dd_ref[...] += _colsum(dy * u)
            return carry

        zero = jnp.zeros((1, S5_TW), F32)
        lax.fori_loop(0, nch, step, (zero, zero))
        dbbr_out[...], dbbi_out[...] = _bd8_diag(dbbr_ref[...]), _bd8_diag(dbbi_ref[...])
        dcr_out[...], dci_out[...] = _bd8_diag(dcr_ref[...]), _bd8_diag(dci_ref[...])
        dlr_out[...], dli_out[...] = _row8_split(dlr_ref[...]), _row8_split(dli_ref[...])

    in_tile, st, pg, plb, gg, glb, dv = _s5_specs(L, layer)
    u_tile = pl.BlockSpec((L, LANE), lambda t: (0, C_S5U // LANE + t))
    groups, rows = _S((N_S5_T, S5_GT, S5_H, S5_P)), _S((N_S5_T, S5_GT, S5_P))
    wide = pltpu.VMEM((LANE, S5_TW), F32)
    return pl.pallas_call(
        body, name="s5_bwd", grid=(N_S5_T,),
        in_specs=[in_tile, u_tile, st, st, pg, pg, plb, plb, pg, pg, _layer_row_tile(layer)] + extra_specs,
        out_specs=[in_tile, gg, gg, glb, glb, gg, gg, dv],
        out_shape=[_S((L, S5_W)), groups, groups, rows, rows, groups, groups, _S((1, S5_W))],
        scratch_shapes=[wide, wide, wide, wide, pltpu.VMEM((1, S5_TW), F32), pltpu.VMEM((1, S5_TW), F32)],
        compiler_params=_params(1))(dy0, z, s_re, s_im, bb_re, bb_im, lb_re, lb_im, c_re, c_im, dvec, *extra)


def _disc(ar, ai, ls):
    dt = jnp.exp(ls)
    mag = jnp.exp(ar * dt)
    lr = mag * jnp.cos(ai * dt)
    li = mag * jnp.sin(ai * dt)
    den = ar * ar + ai * ai
    cr = ((lr - 1.0) * ar + li * ai) / den
    ci = (li * ar - (lr - 1.0) * ai) / den
    return lr, li, cr, ci


def _s5_disc_fwd(ar, ai, ls, token=None):
    extra, extra_specs = _after(token)

    def body(ar_ref, ai_ref, ls_ref, *rest):
        lr_ref, li_ref, cr_ref, ci_ref = rest[len(extra):]
        lr, li, cr, ci = _disc(ar_ref[...], ai_ref[...], ls_ref[...])
        lr_ref[...], li_ref[...], cr_ref[...], ci_ref[...] = lr, li, cr, ci

    sh = _S(ar.shape)
    vm = pl.BlockSpec(memory_space=pltpu.VMEM)
    return pl.pallas_call(body, name="s5_disc_fwd", in_specs=[vm, vm, vm] + extra_specs, out_shape=[sh, sh, sh, sh])(
        ar, ai, ls, *extra)


def _layers(refs):
    return jnp.concatenate([r[...] for r in refs], axis=0)


def _s5_disc_bwd(ar, ai, ls, dlr, dli, dcr, dci):
    n = len(dlr)

    def body(ar_ref, ai_ref, ls_ref, *rest):
        dcr_ref, dci_ref, dar_ref, dai_ref, dls_ref = rest[2 * n:]
        _, vjp = jax.vjp(_disc, ar_ref[...], ai_ref[...], jnp.broadcast_to(ls_ref[...], ar_ref.shape))
        dar, dai, dls = vjp((_layers(rest[:n]), _layers(rest[n:2 * n]), dcr_ref[...], dci_ref[...]))
        dar_ref[...], dai_ref[...] = dar, dai
        dls_ref[...] = jnp.sum(dls, axis=1, keepdims=True)

    return pl.pallas_call(body, name="s5_disc_bwd", out_shape=[_S(ar.shape), _S(ar.shape), _S(ls.shape)])(
        ar, ai, ls, *dlr, *dli, dcr, dci)


def _s5_bscale_fwd(cr, ci, br, bi):
    def body(cr_ref, ci_ref, br_ref, bi_ref, or_ref, oi_ref):
        or_ref[...], oi_ref[...] = _cmul(cr_ref[...], ci_ref[...], br_ref[...], bi_ref[...])

    return pl.pallas_call(body, name="s5_bscale_fwd", out_shape=[_S(br.shape), _S(br.shape)])(cr, ci, br, bi)


def _s5_bscale_bwd(cr, ci, br, bi, gr, gi):
    n = len(gr)

    def body(cr_ref, ci_ref, br_ref, bi_ref, *rest):
        dbr_ref, dbi_ref, dcr_ref, dci_ref = rest[2 * n:]
        cr_, ci_, br_, bi_ = (r[...] for r in (cr_ref, ci_ref, br_ref, bi_ref))
        gr_, gi_ = _layers(rest[:n]), _layers(rest[n:2 * n])
        dbr_ref[...] = cr_ * gr_ + ci_ * gi_
        dbi_ref[...] = cr_ * gi_ - ci_ * gr_
        dcr_ref[...] = jnp.sum(gr_ * br_ + gi_ * bi_, axis=1, keepdims=True)
        dci_ref[...] = jnp.sum(gi_ * br_ - gr_ * bi_, axis=1, keepdims=True)

    return pl.pallas_call(body, name="s5_bscale_bwd",
                          out_shape=[_S(br.shape), _S(br.shape), _S(cr.shape), _S(cr.shape)])(cr, ci, br, bi, *gr, *gi)


def _row(w):
    return pl.BlockSpec((TM, w), lambda i: (i, 0))


def _full(shape):
    return pl.BlockSpec(tuple(shape), lambda i: (0,) * len(shape))


def _gate_rows():
    return [pl.BlockSpec((TM, RG_W), lambda i: (i, C_RGG // RG_W))] + [
        pl.BlockSpec((TM, LANE), lambda i, k=k: (i, C_S5G // LANE + k)) for k in range(N_S5_T)]


def _p_rows(layer):
    return pl.BlockSpec((None, None, TM, PLE_D), lambda i: (layer, 0, i, 0))


DEPTH = 2


def _lrow(layer, width):
    return _full((DEPTH, width))


def _pick(ref, layer):
    return ref[layer:layer + 1, :]


def _post_fwd(x, hs, z, y0, p, w_glu, b_glu, w_out, g1, b1, ple_w, w_pg, b_pg, g2, b2, layer):
    L = x.shape[0]

    def body(x_ref, hs_ref, zg_ref, zs0_ref, zs1_ref, zs2_ref, y0_ref, p_ref, wg_ref, bg_ref, wo_ref, g1_ref, b1_ref, pw_ref,
             wpg_ref, bpg_ref, g2_ref, b2_ref, x2_ref, xh1_ref, xh2_ref, gt_ref, rstd1_ref, rstd2_ref):
        rg_gate = zg_ref[...]
        s5_gate = jnp.concatenate([zs0_ref[...], zs1_ref[...], zs2_ref[...]], axis=1)
        rg_y = hs_ref[...] * _silu_and_grad(rg_gate)[0]
        y1 = _gelu(y0_ref[...])
        gl = _sigmoid(_mm(y1, wg_ref[...]) + _pick(bg_ref, layer))
        s5_y = (y1 * gl) * _silu_and_grad(s5_gate)[0]
        mix = _mm(jnp.concatenate([rg_y.astype(MXU), s5_y.astype(MXU)], axis=1), wo_ref[...])
        t1 = ALPHA * x_ref[...] + mix
        x1, xh1, rstd1 = _ln_fwd(t1, _pick(g1_ref, layer), _pick(b1_ref, layer))
        q = _mm(p_ref[...], pw_ref[...])
        gt = _sigmoid(_mm(x1, wpg_ref[...]) + _pick(bpg_ref, layer))
        t2 = ALPHA * x1 + q * gt
        x2, xh2, rstd2 = _ln_fwd(t2, _pick(g2_ref, layer), _pick(b2_ref, layer))
        x2_ref[...], xh1_ref[...], xh2_ref[...], gt_ref[...] = x2, xh1, xh2, gt
        rstd1_ref[...], rstd2_ref[...] = rstd1, rstd2

    vec = _lrow(layer, D_MODEL)
    return pl.pallas_call(
        body, name="post_fwd", grid=(L // TM,),
        in_specs=[_row(D_MODEL), _row(RG_W), *_gate_rows(), _row(S5_W), _p_rows(layer), _full((S5_W, S5_W)),
                  _lrow(layer, S5_W), _full((D_MODEL, D_MODEL)), vec, vec, _full((PLE_D, D_MODEL)), _full((D_MODEL, D_MODEL)),
                  vec, vec, vec],
        out_specs=[_row(D_MODEL)] * 4 + [_row(1)] * 2, out_shape=[_S((L, D_MODEL))] * 4 + [_S((L, 1))] * 2,
        compiler_params=_params(1))(x, hs, z, z, z, z, y0, p, w_glu, b_glu, w_out, g1, b1, ple_w, w_pg, b_pg, g2, b2)


def _post_bwd_a(dx2_or_target, is_top, xh2, xh1, rstd2, rstd1, gt, p, ple_w, w_pg, g1, b1, g2, b2, layer, token=None):
    L = xh1.shape[0]
    extra, extra_specs = _after(token)

    def body(d_ref, xh2_ref, xh1_ref, rstd2_ref, rstd1_ref, gt_ref, p_ref, pw_ref, wpg_ref, g1_ref, b1_ref, g2_ref,
             b2_ref, *rest):
        (dt1_ref, dpw_out, dwpg_out, dbpg_ref, dg1_ref, db1_ref, dg2_ref, db2_ref, loss_ref, dpw_ref,
         dwpg_ref) = rest[len(extra):]
        @pl.when(pl.program_id(0) == 0)
        def _():
            for ref in (dpw_ref, dwpg_ref, dbpg_ref, dg1_ref, db1_ref, dg2_ref, db2_ref, loss_ref):
                ref[...] = jnp.zeros_like(ref)

        g1, g2 = _pick(g1_ref, layer), _pick(g2_ref, layer)
        xh1, xh2, rstd1, rstd2 = xh1_ref[...], xh2_ref[...], rstd1_ref[...], rstd2_ref[...]
        x1 = xh1 * g1 + _pick(b1_ref, layer)
        if is_top:
            err = (xh2 * g2 + _pick(b2_ref, layer)) - d_ref[...]
            loss_ref[...] += _colsum(err * err)
            dx2 = err * (1.0 / D_MODEL)
        else:
            dx2 = d_ref[...]
        p = p_ref[...]
        q, gt = _mm(p, pw_ref[...]), gt_ref[...]
        dg2_ref[...] += _colsum(dx2 * xh2)
        db2_ref[...] += _colsum(dx2)
        dt2 = _ln_bwd(dx2, xh2, rstd2, g2)
        dq = dt2 * gt
        dgpre = (dt2 * q) * gt * (1.0 - gt)
        dpw_ref[...] += _mm_tn(p, dq)
        dwpg_ref[...] += _mm_tn(x1, dgpre)
        dbpg_ref[...] += _colsum(dgpre)
        dx1 = ALPHA * dt2 + _mm_nt(dgpre, wpg_ref[...])
        dg1_ref[...] += _colsum(dx1 * xh1)
        db1_ref[...] += _colsum(dx1)
        dt1_ref[...] = _ln_bwd(dx1, xh1, rstd1, g1)

        @pl.when(pl.program_id(0) == L // TM - 1)
        def _():
            dpw_out[...] = dpw_ref[...].astype(WIRE)
            dwpg_out[...] = dwpg_ref[...].astype(WIRE)

    vec, lvec = _full((1, D_MODEL)), _lrow(layer, D_MODEL)
    return pl.pallas_call(
        body, name="post_bwd_a_top" if is_top else "post_bwd_a", grid=(L // TM,),
        in_specs=[_row(D_MODEL), _row(D_MODEL), _row(D_MODEL), _row(1), _row(1), _row(D_MODEL), _p_rows(layer),
                  _full((PLE_D, D_MODEL)), _full((D_MODEL, D_MODEL)), lvec, lvec, lvec, lvec] + extra_specs,
        out_specs=[_row(D_MODEL), _full((PLE_D, D_MODEL)), _full((D_MODEL, D_MODEL)), vec, vec, vec, vec, vec, vec],
        out_shape=[_S((L, D_MODEL)), _S((PLE_D, D_MODEL), WIRE), _S((D_MODEL, D_MODEL), WIRE)] + [_S((1, D_MODEL))] * 6,
        scratch_shapes=[pltpu.VMEM((PLE_D, D_MODEL), F32), pltpu.VMEM((D_MODEL, D_MODEL), F32)],
        compiler_params=_params(1))(dx2_or_target, xh2, xh1, rstd2, rstd1, gt, p, ple_w, w_pg, g1, b1, g2, b2, *extra)


def _post_bwd_b(dt1, z, hs, y0, w_out, w_glu, b_glu, layer):
    L = dt1.shape[0]

    def body(dt1_ref, zg_ref, zs0_ref, zs1_ref, zs2_ref, hs_ref, y0_ref, wo_ref, wg_ref, bg_ref,
             dhs_ref, dy0_ref, dzg_ref, dwo_out, dwg_out, dbg_ref, dwo_ref, dwg_ref):
        @pl.when(pl.program_id(0) == 0)
        def _():
            for ref in (dwo_ref, dwg_ref, dbg_ref):
                ref[...] = jnp.zeros_like(ref)

        dt1b = dt1_ref[...].astype(MXU)
        dm = _mm_nt(dt1b, wo_ref[...])
        d_rgy, d_s5y = dm[:, :RG_W], dm[:, RG_W:]
        rg_gate = zg_ref[...]
        s5_gate = jnp.concatenate([zs0_ref[...], zs1_ref[...], zs2_ref[...]], axis=1)
        hs = hs_ref[...]
        sl, dsl = _silu_and_grad(rg_gate)
        dhs_ref[...] = d_rgy * sl
        dzg_ref[:, :RG_W] = d_rgy * hs * dsl
        y0 = y0_ref[...]
        y1 = _gelu(y0)
        gl = _sigmoid(_mm(y1, wg_ref[...]) + _pick(bg_ref, layer))
        y2 = y1 * gl
        sl2, dsl = _silu_and_grad(s5_gate)
        m = jnp.concatenate([(hs * sl).astype(MXU), (y2 * sl2).astype(MXU)], axis=1)
        dwo_ref[...] += _mm_tn(m, dt1b)
        dy2 = d_s5y * sl2
        dzg_ref[:, RG_W:] = d_s5y * y2 * dsl
        dglpre = (dy2 * y1) * gl * (1.0 - gl)
        dwg_ref[...] += _mm_tn(y1, dglpre)
        dbg_ref[...] += _colsum(dglpre)
        dy1 = dy2 * gl + _mm_nt(dglpre, wg_ref[...])
        dy0_ref[...] = dy1 * _gelu_grad(y0)

        @pl.when(pl.program_id(0) == L // TM - 1)
        def _():
            dwo_out[...] = dwo_ref[...].astype(WIRE)
            dwg_out[...] = dwg_ref[...].astype(WIRE)

    return pl.pallas_call(
        body, name="post_bwd_b", grid=(L // TM,),
        in_specs=[_row(D_MODEL), *_gate_rows(), _row(RG_W), _row(S5_W), _full((D_MODEL, D_MODEL)),
                  _full((S5_W, S5_W)), _lrow(layer, S5_W)],
        out_specs=[_row(RG_W), _row(S5_W), _row(D_MODEL), _full((D_MODEL, D_MODEL)), _full((S5_W, S5_W)), _full((1, S5_W))],
        out_shape=[_S((L, RG_W)), _S((L, S5_W)), _S((L, D_MODEL)), _S((D_MODEL, D_MODEL), WIRE), _S((S5_W, S5_W), WIRE),
                   _S((1, S5_W))],
        scratch_shapes=[pltpu.VMEM((D_MODEL, D_MODEL), F32), pltpu.VMEM((S5_W, S5_W), F32)],
        compiler_params=_params(1))(dt1, z, z, z, z, hs, y0, w_out, w_glu, b_glu)


def _adamw(parts, w, m, v, token=None):
    nl = len(parts)
    extra, extra_specs = _after(token)
    n, R, C = parts[0].shape
    tr = R
    for cand in (512, 256, 128, 64, 32, 16, 8):
        if R % cand == 0 and n * cand * C * 4 <= 4 * 1024 * 1024:
            tr = cand
            break
    nblk = R // tr

    def body(*refs):
        p_refs = refs[:nl]
        w_ref, m_ref, v_ref = refs[nl:nl + 3]
        g_ref, d_ref, nm_ref, nv_ref = refs[nl + 3 + len(extra):]
        layer = pl.program_id(0)
        g = None
        for li, p_ref in enumerate(p_refs):
            s = p_ref[0].astype(F32)
            for k in range(1, n):
                s = s + p_ref[k].astype(F32)
            g = s if g is None else jnp.where(layer == li, s, g)
        nm = B1 * m_ref[...] + (1.0 - B1) * g
        nv = B2 * v_ref[...] + (1.0 - B2) * (g * g)
        d_ref[...] = (-LR) * ((nm / BC1) / (jnp.sqrt(nv / BC2) + EPS) + WD * w_ref[...])
        g_ref[...], nm_ref[...], nv_ref[...] = g, nm, nv

    def part_spec(li):
        return pl.BlockSpec((n, tr, C), lambda l, i: (0, jnp.where(l == li, i, jnp.where(l < li, 0, nblk - 1)), 0))

    blk = pl.BlockSpec((tr, C), lambda l, i: (l * nblk + i, 0))
    return pl.pallas_call(
        body, name="adamw", grid=(nl, nblk),
        in_specs=[part_spec(li) for li in range(nl)] + [blk, blk, blk] + extra_specs,
        out_specs=[blk] * 4, out_shape=[_S((nl * R, C))] * 4, compiler_params=_params(2))(*parts, w, m, v, *extra)


def _adamw_sharded(names, recv, w, m, v, name, token=None):
    n, nl = len(names), len(recv)
    extra, extra_specs = _after(token)
    n_in = n * (nl + 3)

    def body(*refs):
        outs = refs[n_in + len(extra):]
        for j in range(n):
            w_ref, m_ref, v_ref = (refs[(nl + t) * n + j] for t in range(3))
            g_ref, d_ref, nm_ref, nv_ref = (outs[t * n + j] for t in range(4))
            for l in range(nl):
                p_ref = refs[l * n + j]
                g = p_ref[0].astype(F32)
                for q in range(1, N_DEV):
                    g = g + p_ref[q].astype(F32)
                nm = B1 * m_ref[l] + (1.0 - B1) * g
                nv = B2 * v_ref[l] + (1.0 - B2) * (g * g)
                d_ref[l] = (-LR) * ((nm / BC1) / (jnp.sqrt(nv / BC2) + EPS) + WD * w_ref[l])
                g_ref[l], nm_ref[l], nv_ref[l] = g, nm, nv

    ins = [r[k] for r in recv for k in names] + [t[k] for t in (w, m, v) for k in names]
    vm = pl.BlockSpec(memory_space=pltpu.VMEM)
    outs = pl.pallas_call(body, name=name, in_specs=[vm] * n_in + extra_specs,
                          out_shape=[_S(w[k].shape) for _ in range(4) for k in names],
                          compiler_params=pltpu.CompilerParams(vmem_limit_bytes=VMEM_LIMIT))(*ins, *extra)
    return [{k: outs[t * n + j] for j, k in enumerate(names)} for t in range(4)]


def _adamw_packed(names, packed, w, m, v, name):
    n = len(names)
    starts, _ = _packed_starts([w[k].shape for k in names])

    def body(p_ref, *refs):
        for j in range(n):
            w_ref, m_ref, v_ref, g_ref, d_ref, nm_ref, nv_ref = (refs[k * n + j] for k in range(7))
            gj = _packed_get(p_ref, starts[j], w_ref.shape)
            nm = B1 * m_ref[...] + (1.0 - B1) * gj
            nv = B2 * v_ref[...] + (1.0 - B2) * (gj * gj)
            d_ref[...] = (-LR) * ((nm / BC1) / (jnp.sqrt(nv / BC2) + EPS) + WD * w_ref[...])
            g_ref[...], nm_ref[...], nv_ref[...] = gj, nm, nv

    ins = [t[k] for t in (w, m, v) for k in names]
    outs = pl.pallas_call(body, name=name, out_shape=[_S(w[k].shape) for _ in range(4) for k in names],
                          compiler_params=pltpu.CompilerParams(vmem_limit_bytes=VMEM_LIMIT))(packed, *ins)
    return [{k: outs[t * n + j] for j, k in enumerate(names)} for t in range(4)]


def _me():
    return lax.axis_index("x"), lax.axis_index("y"), lax.axis_index("c")


def _lin(dev):
    return 4 * dev[0] + 2 * dev[1] + dev[2]


def _blk(ref, axis, size, idx):
    nd = len(ref.shape)
    start = idx * size
    if axis == nd - 1 and size % LANE == 0:
        start = pl.multiple_of(start, LANE)
    elif axis == nd - 2 and size % 16 == 0:
        start = pl.multiple_of(start, 16)
    ix = [slice(None)] * nd
    ix[axis] = pl.ds(start, size)
    return ref.at[tuple(ix)]


HBM_SPEC = pl.BlockSpec(memory_space=pltpu.HBM)
SEM_SPEC = pl.BlockSpec(memory_space=pltpu.SEMAPHORE)
EFFECT = pltpu.SideEffectType.DATAFLOW_SIDE_EFFECTING


def _peers(x, y, c):
    flip = lambda v, f: 1 - v if f else v
    return [(flip(x, k & 4), flip(y, k & 2), flip(c, k & 1)) for k in range(1, N_DEV)]


def _land_shape(mode, s, axis):
    if mode == "gather":
        return s.shape[:axis] + (N_DEV * s.shape[axis],) + s.shape[axis + 1:]
    return (N_DEV,) + s.shape[:axis] + (s.shape[axis] // N_DEV,) + s.shape[axis + 1:]


def _src_view(mode, ref, axis, peer):
    return ref if mode == "gather" else _blk(ref, axis, ref.shape[axis] // N_DEV, peer)


def _dst_view(mode, land, axis, sender):
    return _blk(land, axis, land.shape[axis] // N_DEV, sender) if mode == "gather" else land.at[sender]


def _blocks(mode, land, axis, k):
    if mode == "gather":
        ix = [slice(None)] * len(land.shape)
        ix[axis] = pl.ds(0, k * (land.shape[axis] // N_DEV))
        return land.at[tuple(ix)]
    return land.at[pl.ds(0, k)]


ARRIVALS = {None: N_DEV - 1, "near": 4, "relay": 3}


def _routes(route, x, y, c):
    me, sibling = (x, y, c), (x, y, 1 - c)
    chips = [(1 - x, y), (x, 1 - y), (1 - x, 1 - y)]
    if route == "near":
        return [(me, sibling)] + [(me, (*chip, c)) for chip in chips]
    if route == "relay":
        return [((*chip, c), sibling) for chip in chips]
    return [(me, peer) for peer in _peers(x, y, c)]


def _place_own(mode, srcs, axes, name, after=None):
    n = len(srcs)
    extra, extra_specs = _after(after)

    def body(me_ref, *refs):
        for a in range(n):
            out = refs[n + len(extra) + a]
            out[...] = refs[a][...].reshape(out.shape)

    def at_me(shape, axis):
        return lambda i, me: tuple(me[0] if d == axis else 0 for d in range(len(shape)))

    in_specs, out_specs = [], []
    for s, axis in zip(srcs, axes):
        if mode == "gather":
            in_specs.append(pl.BlockSpec(s.shape, lambda i, me, nd=len(s.shape): (0,) * nd))
            out_specs.append(pl.BlockSpec(s.shape, at_me(s.shape, axis)))
        else:
            blk = s.shape[:axis] + (s.shape[axis] // N_DEV,) + s.shape[axis + 1:]
            in_specs.append(pl.BlockSpec(blk, at_me(blk, axis)))
            out_specs.append(pl.BlockSpec((1,) + blk, at_me((1,) + blk, 0)))
    me = _lin(_me()).astype(jnp.int32).reshape(1)
    return pl.pallas_call(
        body, name=name, out_shape=[_S(_land_shape(mode, s, a), s.dtype) for s, a in zip(srcs, axes)],
        grid_spec=pltpu.PrefetchScalarGridSpec(num_scalar_prefetch=1, grid=(1,), in_specs=in_specs + extra_specs,
                                               out_specs=out_specs),
        compiler_params=_params(1))(me, *srcs, *extra)


def _place_shards(shards, layers, axes, dtypes, name, after=None):
    n = len(shards)
    extra, extra_specs = _after(after)

    def body(me_ref, *refs):
        for a in range(n):
            out = refs[n + len(extra) + a]
            out[...] = refs[a][...].astype(out.dtype)

    in_specs, out_specs, out_shape = [], [], []
    for s, layer, axis, dt in zip(shards, layers, axes, dtypes):
        shape = s.shape if layer is None else s.shape[1:]
        nd = len(shape)
        if layer is None:
            in_specs.append(pl.BlockSpec(shape, lambda i, me, nd=nd: (0,) * nd))
        else:
            in_specs.append(pl.BlockSpec((None,) + shape, lambda i, me, nd=nd, layer=layer: (layer,) + (0,) * nd))
        out_specs.append(pl.BlockSpec(shape, lambda i, me, nd=nd, axis=axis: tuple(me[0] if d == axis else 0 for d in range(nd))))
        out_shape.append(_S(shape[:axis] + (N_DEV * shape[axis],) + shape[axis + 1:], dt))
    me = _lin(_me()).astype(jnp.int32).reshape(1)
    return pl.pallas_call(
        body, name=name, out_shape=out_shape,
        grid_spec=pltpu.PrefetchScalarGridSpec(num_scalar_prefetch=1, grid=(1,), in_specs=in_specs + extra_specs,
                                               out_specs=out_specs),
        compiler_params=_params(1))(me, *shards, *extra)


def _push_start(mode, srcs, lands, axes, name, route=None):
    n, ns = len(lands), len(srcs)

    def body(*refs):
        src_refs, land_refs = refs[:ns], refs[ns:ns + n]
        send_sems, recv_sems = refs[ns + n], refs[ns + n + 1]
        token = refs[-1]
        x, y, c = _me()
        for a in range(n):
            for block, peer in _routes(route, x, y, c):
                there = _dst_view(mode, land_refs[a], axes[a], _lin(block))
                pltpu.make_async_remote_copy(
                    src_ref=_src_view(mode, src_refs[a], axes[a], _lin(peer)) if ns else there, dst_ref=there,
                    send_sem=send_sems.at[a], recv_sem=recv_sems.at[a], device_id=peer, device_id_type=MESH).start()
        token[...] = jnp.zeros_like(token)

    hbm = lambda s: pltpu.HBM(s.shape, s.dtype)
    outs = pl.pallas_call(
        body, name=name,
        out_shape=(pltpu.SemaphoreType.DMA((n,)), pltpu.SemaphoreType.DMA((n,)), *[hbm(s) for s in srcs], *[hbm(s) for s in lands],
                   _S((SUB, LANE))),
        in_specs=[HBM_SPEC] * (ns + n),
        out_specs=(SEM_SPEC, SEM_SPEC, *[HBM_SPEC] * (ns + n), pl.BlockSpec(memory_space=pltpu.VMEM)),
        input_output_aliases={i: 2 + i for i in range(ns + n)},
        compiler_params=pltpu.CompilerParams(has_side_effects=EFFECT),
    )(*[pltpu.with_memory_space_constraint(s, pltpu.HBM) for s in list(srcs) + list(lands)])
    return outs[0], outs[1], outs[2:2 + ns], outs[2 + ns:2 + ns + n], outs[-1]


def _push_wait(mode, send_sems, recv_sems, srcs, lands, axes, after, name, first=0, route=None):
    n, ns = len(lands), len(srcs)
    after = list(after) if isinstance(after, (list, tuple)) else [after]

    def body(*refs):
        land_refs = refs[ns:ns + n]
        send_sems, recv_sems = refs[ns + n], refs[ns + n + 1]
        x, y, c = _me()
        for a in range(n):
            seven = _blocks(mode, land_refs[a], axes[a], ARRIVALS[route])
            cp = pltpu.make_async_remote_copy(src_ref=seven, dst_ref=seven, send_sem=send_sems.at[first + a],
                                              recv_sem=recv_sems.at[first + a],
                                              device_id=(x, y, 1 - c), device_id_type=MESH)
            cp.wait_send()
            cp.wait_recv()

    hbm = lambda s: pltpu.HBM(s.shape, s.dtype)
    outs = pl.pallas_call(
        body, name=name, out_shape=tuple(hbm(s) for s in list(srcs) + list(lands)),
        in_specs=[HBM_SPEC] * (ns + n) + [SEM_SPEC, SEM_SPEC] + [ANY] * len(after), out_specs=tuple([HBM_SPEC] * (ns + n)),
        input_output_aliases={i: i for i in range(ns + n)},
        compiler_params=pltpu.CompilerParams(has_side_effects=EFFECT),
    )(*srcs, *lands, send_sems, recv_sems, *after)
    return outs[ns:]


def _sum_parts(parts):
    n, R, C = parts.shape

    def body(p_ref, o_ref):
        g = p_ref[0]
        for k in range(1, n):
            g = g + p_ref[k]
        o_ref[...] = g

    return pl.pallas_call(body, name="sum_parts", out_shape=_S((R, C)))(parts)


SMALL =['conv_b', 'rg_wa', 'rg_ba', 'rg_wx', 'rg_bx', 'rg_lambda', 's5_a_re', 's5_a_im', 's5_b_re', 's5_b_im',
         's5_c_re', 's5_c_im', 's5_d', 's5_log_step', 's5_b_glu', 'ln1_g', 'ln1_b', 'ple_gate_b', 'ln2_g', 'ln2_b']
WEIGHTS = ['w_in', 'conv_w', 'conv_b', 'rg_wa', 'rg_ba', 'rg_wx', 'rg_bx', 'rg_lambda', 's5_a_re', 's5_a_im', 's5_b_re',
           's5_b_im', 's5_c_re', 's5_c_im', 's5_d', 's5_log_step', 's5_w_glu', 's5_b_glu', 'w_out', 'ln1_g', 'ln1_b',
           'ple_w', 'ple_gate_w', 'ple_gate_b', 'ln2_g', 'ln2_b']
PACK_ROWS_MULT = 64


STORED = {'s5_b_re': (2, 3), 's5_b_im': (2, 3), 's5_d': (1, 2)}


def _stored(k, a):
    return jnp.swapaxes(a, *STORED[k]) if k in STORED else a


def _two_d(a):
    return a.reshape(-1, a.shape[-1])


def _up8(n):
    return -(-n // SUB) * SUB


def _halves_fit(shape):
    return 2 * shape[1] == LANE and shape[0] % (2 * SUB) == 0


def _packed_rows(shape):
    R, C = shape
    if C % LANE == 0:
        return (C // LANE) * _up8(R)
    return R // 2 if _halves_fit(shape) else _up8(R)


def _packed_put(out_ref, r0, pieces):
    R, C = sum(a.shape[0] for a in pieces), pieces[0].shape[1]
    if _halves_fit((R, C)):
        lo, hi = pieces if len(pieces) == 2 else (pieces[0][:R // 2], pieces[0][R // 2:])
        out_ref[r0:r0 + R // 2, :] = jnp.concatenate([lo, hi], axis=1)
        return
    for a in pieces:
        rows = a.shape[0]
        if C % LANE == 0:
            for j in range(C // LANE):
                out_ref[r0 + j * _up8(R):r0 + j * _up8(R) + rows, :] = a[:, j * LANE:(j + 1) * LANE]
        else:
            out_ref[r0:r0 + rows, :C] = a
        r0 += rows


def _packed_get(ref, r0, shape):
    R, C = shape
    if C % LANE == 0:
        return jnp.concatenate([ref[r0 + j * _up8(R):r0 + j * _up8(R) + R, :] for j in range(C // LANE)], axis=1)
    if _halves_fit(shape):
        both = ref[r0:r0 + R // 2, :]
        return jnp.concatenate([both[:, :C], both[:, C:]], axis=0)
    return ref[r0:r0 + R, :C]


def _packed_starts(shapes):
    starts = [0]
    for s in shapes:
        starts.append(starts[-1] + _packed_rows(s))
    return starts[:-1], starts[-1] + (-starts[-1] % PACK_ROWS_MULT)


def _pack(tree, scalar):
    groups = [[_two_d(a) for a in (tree[k] if isinstance(tree[k], list) else [tree[k]])] for k in SMALL]
    groups.append([scalar.reshape(1, 1)])
    starts, rows = _packed_starts([(sum(a.shape[0] for a in g), g[0].shape[1]) for g in groups])

    def body(*refs):
        out_ref, refs = refs[-1], list(refs[:-1])
        out_ref[...] = jnp.zeros_like(out_ref)
        for r0, g in zip(starts, groups):
            _packed_put(out_ref, r0, [refs.pop(0)[...] for _ in g])

    return pl.pallas_call(body, name="pack_small", out_shape=_S((rows, LANE)))(*sum(groups, []))


class _NoHooks:
    token = None
    first_token = None

    def first_weights(self, full, after):
        return full

    def layer_start(self, i, W, after):
        return W

    def late_weights(self, i, W, after):
        return W

    def post_done(self, i, g):
        return None

    def smalls_done(self, grads, loss):
        self.small = _small_grads(grads, self.res)
        return None

    def w_in_done(self, i, g):
        return None

    def layer_done(self, i, g, dx):
        return None


def _local_grads(x, p, target, W, disc, hooks):
    depth = 2
    saved = []
    for i in range(depth):
        if i > 0:
            W = hooks.layer_start(i, W, x)
        w = W[i]
        z = _inproj_fwd(x, w['w_in'], hooks.token if i == 0 else None)
        hs, *gates = _rg_fwd(z, w['conv_w'], w['conv_b'], w['wa_bd'], w['wx_bd'], w['rg_ba'], w['rg_bx'], w['rg_lambda'], i)
        d = disc[i]
        y0, s_re, s_im = _s5_fwd(z, d['bb_re'], d['bb_im'], d['lb_re'], d['lb_im'], d['c_re'], d['c_im'], w['s5_d'], i)
        W = hooks.late_weights(i, W, y0)
        w = W[i]
        x2, *norms = _post_fwd(x, hs, z, y0, p, w['s5_w_glu'], w['s5_b_glu'], w['w_out'], w['ln1_g'], w['ln1_b'],
                               w['ple_w'], w['ple_gate_w'], w['ple_gate_b'], w['ln2_g'], w['ln2_b'], i)
        saved.append((x, z, hs, gates, y0, s_re, s_im, norms))
        x = x2

    grads = [None] * depth
    dx = target
    loss = None
    token = None
    for i in reversed(range(depth)):
        w, d = W[i], disc[i]
        xin, z, hs, gates, y0, s_re, s_im, (xh1, xh2, gt, rstd1, rstd2) = saved[i]
        g = {}
        (dt1, g['ple_w'], g['ple_gate_w'], g['ple_gate_b'], g['ln1_g'], g['ln1_b'], g['ln2_g'], g['ln2_b'], lrow) = _post_bwd_a(
            dx, i == depth - 1, xh2, xh1, rstd2, rstd1, gt, p, w['ple_w'], w['ple_gate_w'], w['ln1_g'], w['ln1_b'],
            w['ln2_g'], w['ln2_b'], i, token)
        if i == depth - 1:
            loss = 0.5 / D_MODEL * jnp.sum(lrow)
        dhs, dy0, dzg, g['w_out'], g['s5_w_glu'], g['s5_b_glu'] = _post_bwd_b(dt1, z, hs, y0, w['w_out'], w['s5_w_glu'],
                                                                           w['s5_b_glu'], i)
        (dzu, g['bb_re'], g['bb_im'], g['lb_re'], g['lb_im'], g['c_re'], g['c_im'], g['s5_d']) = _s5_bwd(
            dy0, z, s_re, s_im, d['bb_re'], d['bb_im'], d['lb_re'], d['lb_im'], d['c_re'], d['c_im'], w['s5_d'], i,
            hooks.post_done(i, g))
        (dzx, g['conv_w'], g['conv_b'], g['wa_bd'], g['wx_bd'], g['rg_ba'], g['rg_bx'], g['rg_lambda']) = _rg_bwd(
            dhs, z, hs, gates, w['conv_w'], w['wa_bd'], w['wx_bd'], w['rg_lambda'], i)
        if i == 0:
            g['w_in'] = _inproj_bwd_dw(xin, dzx, dzg, dzu, hooks.smalls_done([g, grads[1]], loss))
            dx = _inproj_bwd_dx(dt1, dzx, dzg, dzu, w['w_in'], hooks.w_in_done(i, g))
        else:
            dx, g['w_in'] = _inproj_bwd(dt1, xin, dzx, dzg, dzu, w['w_in'])
        grads[i] = g
        token = hooks.layer_done(i, g, dx)
    return loss, dx, grads


def _s5_layouts_fwd(s5_a_re, s5_a_im, s5_log_step, s5_b_re, s5_b_im, s5_c_re, s5_c_im, token=None):
    depth = s5_a_re.shape[0]
    ar, ai = s5_a_re.reshape(depth * 24, S5_P), s5_a_im.reshape(depth * 24, S5_P)
    ls = s5_log_step.reshape(depth * 24, 1)
    lr, li, cr, ci = _s5_disc_fwd(ar, ai, ls, token)
    per_group = lambda a: a.reshape(depth * 24, 1, S5_P)
    as_c = lambda b: jnp.swapaxes(b, 2, 3).reshape(depth * 24, S5_H, S5_P)
    res = (ar, ai, ls, per_group(cr), per_group(ci), as_c(s5_b_re), as_c(s5_b_im))
    bbr, bbi = _s5_bscale_fwd(*res[3:])
    tiles = lambda a: a.reshape(depth * N_S5_T, S5_GT, S5_H, S5_P)
    rows = lambda a: a.reshape(depth * N_S5_T, S5_GT, S5_P)
    disc = dict(bb_re=tiles(bbr), bb_im=tiles(bbi), lb_re=rows(lr), lb_im=rows(li), c_re=tiles(s5_c_re), c_im=tiles(s5_c_im))
    return [disc] * depth, res


def _s5_layouts_bwd(grads, res):
    ar, ai, ls, cr, ci, br, bi = res
    depth = len(grads)
    layers = lambda k, *shape: [g[k].reshape(shape) for g in grads]
    shape_c = (depth, 24, S5_H, S5_P)
    dbr, dbi, dcr, dci = _s5_bscale_bwd(cr, ci, br, bi, layers('bb_re', 24, S5_H, S5_P), layers('bb_im', 24, S5_H, S5_P))
    gp = (depth * 24, S5_P)
    dar, dai, dls = _s5_disc_bwd(ar, ai, ls, layers('lb_re', 24, S5_P), layers('lb_im', 24, S5_P), dcr.reshape(gp),
                                 dci.reshape(gp))
    return dict(
        s5_a_re=dar.reshape(depth, 24, S5_P), s5_a_im=dai.reshape(depth, 24, S5_P), s5_log_step=dls.reshape(depth, 24),
        s5_b_re=dbr.reshape(shape_c), s5_b_im=dbi.reshape(shape_c),
        s5_c_re=layers('c_re', 24, S5_H, S5_P), s5_c_im=layers('c_im', 24, S5_H, S5_P))


LATE = ('w_out', 'ple_w', 'ple_gate_w', 's5_w_glu')


ROWS = ('conv_b', 'rg_ba', 'rg_bx', 'rg_lambda', 's5_d', 's5_b_glu', 'ln1_g', 'ln1_b', 'ple_gate_b', 'ln2_g', 'ln2_b')


def _shared_weights(full):
    shared = {k: full[k] for k in ROWS}
    shared.update(conv_w=full['conv_w'], wa_bd=full['rg_wa'], wx_bd=full['rg_wx'], s5_d=full['s5_d'].reshape(DEPTH, 1, S5_W))
    return shared


def _layer_weights(full, shared, i):
    return dict(shared, w_in=full['w_in'][i])


class _AllLocal(_NoHooks):
    def __init__(self, full):
        self.full = full

    def late_weights(self, i, W, after):
        W[i].update({k: self.full[k][i] for k in LATE})
        return W


def _full_grads(full, x, p, target, hooks=None):
    hooks = hooks or _AllLocal(full)
    disc, res = _s5_layouts_fwd(full['s5_a_re'], full['s5_a_im'], full['s5_log_step'], full['s5_b_re'], full['s5_b_im'],
                                full['s5_c_re'], full['s5_c_im'], hooks.first_token)
    full = hooks.first_weights(full, disc[-1]['bb_im'])
    shared = _shared_weights(full)
    W = [_layer_weights(full, shared, i) for i in range(2)]
    hooks.res = res
    loss, gx, grads = _local_grads(x, p, target, W, disc, hooks)
    out = dict(hooks.small)
    for k in SHARD_AXIS:
        out[k] = [g[k] for g in grads]
    return loss, gx, out


def _small_grads(grads, res):
    stack = lambda f: jnp.stack([f(g) for g in grads])
    out = _s5_layouts_bwd(grads, res)
    out['conv_w'] = stack(lambda g: g['conv_w'])
    for k in ('conv_b', 'rg_ba', 'rg_bx', 'rg_lambda', 's5_b_glu', 'ln1_g', 'ln1_b', 'ple_gate_b', 'ln2_g', 'ln2_b'):
        out[k] = [g[k] for g in grads]
    out['s5_d'] = _stored('s5_d', stack(lambda g: g['s5_d'][0]).reshape(2, 24, 16))
    out['rg_wa'] = [g['wa_bd'] for g in grads]
    out['rg_wx'] = [g['wx_bd'] for g in grads]
    return out


SHARD_AXIS = {'w_in': 2, 'w_out': 1, 'ple_w': 2, 'ple_gate_w': 1, 's5_w_glu': 1}


def kernel(x, p, w_in, conv_w, conv_b, rg_wa, rg_ba, rg_wx, rg_bx, rg_lambda, s5_a_re, s5_a_im, s5_b_re, s5_b_im, s5_c_re, s5_c_im, s5_d, s5_log_step, s5_w_glu, s5_b_glu, w_out, ln1_g, ln1_b, ple_w, ple_gate_w, ple_gate_b, ln2_g, ln2_b, loss_target, m_w_in, m_conv_w, m_conv_b, m_rg_wa, m_rg_ba, m_rg_wx, m_rg_bx, m_rg_lambda, m_s5_a_re, m_s5_a_im, m_s5_b_re, m_s5_b_im, m_s5_c_re, m_s5_c_im, m_s5_d, m_s5_log_step, m_s5_w_glu, m_s5_b_glu, m_w_out, m_ln1_g, m_ln1_b, m_ple_w, m_ple_gate_w, m_ple_gate_b, m_ln2_g, m_ln2_b, v_w_in, v_conv_w, v_conv_b, v_rg_wa, v_rg_ba, v_rg_wx, v_rg_bx, v_rg_lambda, v_s5_a_re, v_s5_a_im, v_s5_b_re, v_s5_b_im, v_s5_c_re, v_s5_c_im, v_s5_d, v_s5_log_step, v_s5_w_glu, v_s5_b_glu, v_w_out, v_ln1_g, v_ln1_b, v_ple_w, v_ple_gate_w, v_ple_gate_b, v_ln2_g, v_ln2_b):
    local = dict(locals())
    w = {k: local[k] for k in WEIGHTS}
    mom = {k: local['m_' + k] for k in WEIGHTS}
    var = {k: local['v_' + k] for k in WEIGHTS}

    big = list(SHARD_AXIS)
    late_axes = [SHARD_AXIS[k] - 1 for k in LATE]
    pushed = {}

    groups = dict(first=(['w_in', 'conv_w'], [0, None], [1, 0]), l0=(list(LATE), [0] * len(LATE), late_axes),
                  l1=(['w_in'] + list(LATE), [1] * (1 + len(LATE)), [1] + late_axes))
    token = None
    for key, members in (("first", ["first"]), ("rest", ["l0", "l1"])):
        names, layers, axes = (sum((groups[m][j] for m in members), []) for j in range(3))
        shards = [w[k] if layer is not None else w[k][None] for k, layer in zip(names, layers)]
        lands = _place_shards(shards, layers, axes, [WIRE if k in big else w[k].dtype for k in names],
                              "place_weights_" + key, token)
        pushed[key] = _push_start("gather", [], lands, axes, "push_weights_" + key, "near" if key == "first" else None)
        token = pushed[key][4]

    def await_weights(key, axes, after):
        s, first = pushed["rest"], 0 if key == "l0" else len(LATE)
        return _push_wait("gather", s[0], s[1], [], s[3][first:first + len(axes)], axes, after, "await_weights_" + key, first)

    def push_grads(key, g, names, axes):
        srcs = [g[k] for k in names]
        pushed[key] = _push_start("scatter", srcs, _place_own("scatter", srcs, axes, "place_grads_" + key), axes,
                                  "push_grads_" + key)
        return pushed[key][4]

    def await_grads(key, axes, after):
        s = pushed[key]
        return _push_wait("scatter", s[0], s[1], s[2], s[3], axes, after, "await_grads_" + key)

    class Overlap(_NoHooks):
        token = pushed["rest"][4]
        first_token = token

        def first_weights(self, full, after):
            s, axes = pushed["first"], [1, 0]
            near = _push_wait("gather", s[0], s[1], [], s[3], axes, after, "await_weights_near", route="near")
            s = _push_start("gather", [], near, axes, "relay_weights", "relay")
            w_in0, conv = _push_wait("gather", s[0], s[1], [], s[3], axes, s[4], "await_weights_relay", route="relay")
            return dict(full, w_in=[w_in0, None], conv_w=jnp.moveaxis(conv, 0, 2).reshape(2, 4, RG_W))

        def late_weights(self, i, W, after):
            if i == 0:
                W[0].update(zip(LATE, await_weights("l0", late_axes, after)))
            return W

        def layer_start(self, i, W, after):
            lands = await_weights("l1", [1] + late_axes, after)
            W[1].update(zip(LATE, lands[1:]), w_in=lands[0])
            return W

        def post_done(self, i, g):
            return push_grads("late0", g, LATE, late_axes) if i == 0 else None

        def smalls_done(self, grads, loss):
            super().smalls_done(grads, loss)
            conv = jnp.moveaxis(self.small['conv_w'].reshape(2, 4, N_DEV, RG_W // N_DEV), 2, 0)
            self.packed = _pack(self.small, loss)
            return push_grads("small", dict(conv_w=conv.reshape(N_DEV, 8, RG_W // N_DEV), small=self.packed),
                              ['conv_w', 'small'], [0, 0])

        def w_in_done(self, i, g):
            return push_grads("w_in0", g, ['w_in'], [0])

        def layer_done(self, i, g, dx):
            return push_grads("all1", g, ['w_in'] + list(LATE), [0] + late_axes) if i == 1 else None

    hooks = Overlap()
    _, grad_x, g = _full_grads(dict(w), x[0], p, loss_target[0], hooks)

    recv1 = dict(zip(['w_in'] + list(LATE), await_grads("all1", [0] + late_axes, grad_x)))
    recv0 = dict(zip(LATE, await_grads("late0", late_axes, grad_x)))
    outs = {}

    def update(k, parts, token=None):
        shard = w[k].shape
        c = shard[-1]
        two = lambda a: a.reshape(-1, c)
        res = _adamw([r.reshape(N_DEV, -1, c) for r in parts], two(w[k]), two(mom[k]), two(var[k]), token)
        outs[k] = [o.reshape(shard) for o in res]

    conv_parts, small_parts = await_grads("small", [0, 0], grad_x)
    rows = hooks.packed.shape[0] // N_DEV
    mine = _sum_parts(small_parts.reshape(N_DEV, rows, LANE))
    sums = _push_start("gather", [mine], _place_own("gather", [mine], [0], "place_small_sums"), [0], "push_small_sums")
    late = _adamw_sharded(list(LATE), [recv0, recv1], w, mom, var, "adamw_late", sums[4])
    for k in LATE:
        outs[k] = [t[k] for t in late]
    w_in0, = await_grads("w_in0", [0], [outs[k][1] for k in LATE])
    update('w_in', [w_in0, recv1['w_in']])
    update('conv_w', [conv_parts])
    gathered, = _push_wait("gather", sums[0], sums[1], sums[2], sums[3], [0], [outs['w_in'][1], outs['conv_w'][1]],
                           "await_small_sums")
    stored = [{k: _two_d(_stored(k, t[k])) for k in SMALL} for t in (w, mom, var)]
    loss = gathered[_packed_starts([stored[0][k].shape for k in SMALL] + [(1, 1)])[0][-1], 0]
    updated = _adamw_packed(SMALL, gathered, *stored, "adamw_small")
    for k in SMALL:
        shape = _stored(k, w[k]).shape
        outs[k] = [_stored(k, o[k].reshape(shape)) for o in updated]

    res = [loss, grad_x[None]]
    for j in range(4):
        res += [outs[k][j] for k in WEIGHTS]
    return tuple(res)
```

```python
import math

import jax
import jax.numpy as jnp
from jax import lax
from jax.experimental import pallas as pl
from jax.experimental.pallas import tpu as pltpu

F32 = jnp.float32
MXU = jnp.bfloat16
WIRE = jnp.bfloat16

N_DEV = 8
D_MODEL = 1024
PLE_D = 256
RG_W = 640
S5_W = 384
S5_P = 64
S5_N = 24 * S5_P
Z_W = 2 * RG_W + 2 * S5_W
C_RGG = RG_W
C_S5U = 2 * RG_W
C_S5G = 2 * RG_W + S5_W
LANE = 128
N_RG_T = RG_W // LANE
N_S5_T = S5_W // LANE
W_BLK = Z_W // N_DEV
ALPHA = (2.0 * 2) ** 0.25
LN_EPS = 1e-5
RG_C = 8.0
LR, B1, B2, EPS, WD, STEP = 0.001, 0.9, 0.999, 1e-08, 0.01, 10
BC1 = 1.0 - B1 ** STEP
BC2 = 1.0 - B2 ** STEP
RC = 512
RC_RG = 1024
TM = 512
TM_MM = 1024
VMEM_LIMIT = 56 * 1024 * 1024

MESH = pl.DeviceIdType.MESH
ANY = pl.BlockSpec(memory_space=pl.ANY)


def _params(n_grid_axes, vmem=VMEM_LIMIT):
    return pltpu.CompilerParams(dimension_semantics=("arbitrary",) * n_grid_axes, vmem_limit_bytes=vmem)


def _S(shape, dtype=F32):
    return jax.ShapeDtypeStruct(tuple(shape), dtype)


def _sigmoid(x):
    return 0.5 * jnp.tanh(0.5 * x) + 0.5


def _silu_and_grad(x):
    s = _sigmoid(x)
    return x * s, s * (1.0 + x * (1.0 - s))


_GELU_C = math.sqrt(2.0 / math.pi)


def _gelu(x):
    return 0.5 * x * (1.0 + jnp.tanh(_GELU_C * (x + 0.044715 * (x * x * x))))


def _gelu_grad(x):
    th = jnp.tanh(_GELU_C * (x + 0.044715 * (x * x * x)))
    return 0.5 * (1.0 + th) + 0.5 * x * (1.0 - th * th) * (_GELU_C * (1.0 + 3.0 * 0.044715 * (x * x)))


def _mm(a, b):
    return jnp.dot(a.astype(MXU), b.astype(MXU), preferred_element_type=F32)


def _mm_nt(a, b):
    return lax.dot_general(a.astype(MXU), b.astype(MXU), (((1,), (1,)), ((), ())), preferred_element_type=F32)


def _mm_tn(a, b):
    return lax.dot_general(a.astype(MXU), b.astype(MXU), (((0,), (0,)), ((), ())), preferred_element_type=F32)


def _ln_fwd(t, g, b):
    mu = jnp.mean(t, axis=-1, keepdims=True)
    tc = t - mu
    var = jnp.mean(tc * tc, axis=-1, keepdims=True)
    rstd = lax.rsqrt(var + LN_EPS)
    xhat = tc * rstd
    return xhat * g + b, xhat, rstd


def _ln_bwd(dy, xhat, rstd, g):
    dxh = dy * g
    m1 = jnp.mean(dxh, axis=-1, keepdims=True)
    m2 = jnp.mean(dxh * xhat, axis=-1, keepdims=True)
    return rstd * (dxh - m1 - xhat * m2)


def _colsum(a):
    return jnp.sum(a, axis=0, keepdims=True)


def _up(x, d, rows, fill):
    n = x.shape[0]
    return jnp.where(rows < n - d, pltpu.roll(x, n - d, 0), fill)


SUB = 8
TILE_STEPS = (1, 2, 4)


def _r8(width):
    return lax.broadcasted_iota(jnp.int32, (SUB, width), 0)


def _scan_real(a, u, carry, reverse=False):
    r8 = _r8(a.shape[1])
    n = a.shape[0] // SUB
    outs = [None] * n
    for k in (reversed(range(n)) if reverse else range(n)):
        A, U = a[SUB * k:SUB * k + SUB], u[SUB * k:SUB * k + SUB]
        for d in TILE_STEPS:
            m = (r8 < SUB - d) if reverse else (r8 >= d)
            sh = SUB - d if reverse else d
            U = A * jnp.where(m, pltpu.roll(U, sh, 0), 0.0) + U
            A = A * jnp.where(m, pltpu.roll(A, sh, 0), 1.0)
        h = A * carry + U
        outs[k] = h
        carry = h[0:1] if reverse else h[SUB - 1:SUB]
    return jnp.concatenate(outs, axis=0), carry


def _tile_powers(lr, li, reverse=False):
    width = lr.shape[1]
    r8 = _r8(width)
    steps = []
    pr, pi = lr, li
    er, ei = jnp.broadcast_to(lr, (SUB, width)), jnp.broadcast_to(li, (SUB, width))
    for d in TILE_STEPS:
        m = (r8 < SUB - d) if reverse else (r8 >= d)
        sh = SUB - d if reverse else d
        steps.append((sh, jnp.where(m, pr, 0.0), jnp.where(m, pi, 0.0)))
        er, ei = _cmul(er, ei, jnp.where(m, pltpu.roll(er, sh, 0), 1.0), jnp.where(m, pltpu.roll(ei, sh, 0), 0.0))
        pr, pi = _cmul(pr, pi, pr, pi)
    return steps, (er, ei)


def _scan_lti(xr, xi, carry, steps, e, reverse=False):
    er, ei = e
    kr, ki = carry
    n = xr.shape[0] // SUB
    outr, outi = [None] * n, [None] * n
    for k in (reversed(range(n)) if reverse else range(n)):
        sr, si = xr[SUB * k:SUB * k + SUB], xi[SUB * k:SUB * k + SUB]
        for sh, pr, pi in steps:
            shr, shi = pltpu.roll(sr, sh, 0), pltpu.roll(si, sh, 0)
            sr, si = sr + (pr * shr - pi * shi), si + (pr * shi + pi * shr)
        sr = sr + (er * kr - ei * ki)
        si = si + (er * ki + ei * kr)
        outr[k], outi[k] = sr, si
        kr, ki = (sr[0:1], si[0:1]) if reverse else (sr[SUB - 1:SUB], si[SUB - 1:SUB])
    return jnp.concatenate(outr, axis=0), jnp.concatenate(outi, axis=0), (kr, ki)


def _halo(ref, c, r0):
    rp = pl.multiple_of(jnp.maximum(r0 - 8, 0), 8)
    return jnp.where(c > 0, ref[pl.ds(rp, 8), :], 0.0)


def _conv_taps(xe):
    return [pltpu.roll(xe, 3, 0)[8:, :], pltpu.roll(xe, 2, 0)[8:, :], pltpu.roll(xe, 1, 0)[8:, :], xe[8:, :]]


def _rg_gates(h, wa, wx, ba, bx, sp):
    r = _sigmoid(_mm(h, wa) + ba)
    i = _sigmoid(_mm(h, wx) + bx)
    log_a = (-RG_C) * r * sp
    a = jnp.exp(log_a)
    mult = jnp.sqrt(-jnp.tanh(log_a) * (a * a + 1.0))
    return r, i, a, mult


def _softplus(y):
    return jnp.maximum(y, 0.0) + jnp.log1p(jnp.exp(-jnp.abs(y)))


def _after(token):
    return ([], []) if token is None else ([token], [ANY])


def _inproj_fwd(x, w_in, token=None):
    L = x.shape[0]

    def body(x_ref, w_ref, *rest):
        rest[-1][...] = _mm(x_ref[...], w_ref[...])

    extra, extra_specs = _after(token)
    tm = min(TM_MM, L)
    return pl.pallas_call(
        body, name="inproj_fwd", grid=(L // tm,),
        in_specs=[pl.BlockSpec((tm, D_MODEL), lambda i: (i, 0)), pl.BlockSpec((D_MODEL, Z_W), lambda i: (0, 0))] + extra_specs,
        out_specs=pl.BlockSpec((tm, Z_W), lambda i: (i, 0)),
        out_shape=_S((L, Z_W)), compiler_params=_params(1))(x, w_in, *extra)


def _inproj_bwd(dt1, x, dzx, dzg, dzu, w_in):
    L = x.shape[0]

    def body(dt1_ref, x_ref, dzx_ref, dzg_ref, dzu_ref, w_ref, dx_ref, dw_ref, acc_ref):
        @pl.when(pl.program_id(0) == 0)
        def _():
            acc_ref[...] = jnp.zeros_like(acc_ref)
        dzg = dzg_ref[...]
        dz = jnp.concatenate([dzx_ref[...], dzg[:, :RG_W], dzu_ref[...], dzg[:, RG_W:]], axis=1).astype(MXU)
        xb = x_ref[...].astype(MXU)
        dx_ref[...] = ALPHA * dt1_ref[...] + _mm_nt(dz, w_ref[...])
        for j in range(N_DEV):
            acc_ref[j] += _mm_tn(xb, dz[:, j * W_BLK:(j + 1) * W_BLK])

        @pl.when(pl.program_id(0) == L // TM - 1)
        def _():
            dw_ref[...] = acc_ref[...].astype(WIRE)

    row = lambda w: pl.BlockSpec((TM, w), lambda i: (i, 0))
    wspec = pl.BlockSpec((N_DEV, D_MODEL, W_BLK), lambda i: (0, 0, 0))
    return pl.pallas_call(
        body, name="inproj_bwd", grid=(L // TM,),
        in_specs=[row(D_MODEL), row(D_MODEL), row(RG_W), row(D_MODEL), row(S5_W),
                  pl.BlockSpec((D_MODEL, Z_W), lambda i: (0, 0))],
        out_specs=[row(D_MODEL), wspec],
        out_shape=[_S((L, D_MODEL)), _S((N_DEV, D_MODEL, W_BLK), WIRE)],
        scratch_shapes=[pltpu.VMEM((N_DEV, D_MODEL, W_BLK), F32)],
        compiler_params=_params(1))(dt1, x, dzx, dzg, dzu, w_in)


TM2 = 1024


def _dz_block(dzx_ref, dzg_ref, dzu_ref):
    dzg = dzg_ref[...]
    return jnp.concatenate([dzx_ref[...], dzg[:, :RG_W], dzu_ref[...], dzg[:, RG_W:]], axis=1).astype(MXU)


def _inproj_bwd_dw(x, dzx, dzg, dzu, token=None):
    L = x.shape[0]
    extra, extra_specs = _after(token)

    def body(x_ref, dzx_ref, dzg_ref, dzu_ref, *rest):
        dw_ref, acc_ref = rest[len(extra):]
        @pl.when(pl.program_id(0) == 0)
        def _():
            acc_ref[...] = jnp.zeros_like(acc_ref)
        dz = _dz_block(dzx_ref, dzg_ref, dzu_ref)
        xb = x_ref[...].astype(MXU)
        for j in range(N_DEV):
            acc_ref[j] += _mm_tn(xb, dz[:, j * W_BLK:(j + 1) * W_BLK])

        @pl.when(pl.program_id(0) == L // TM2 - 1)
        def _():
            dw_ref[...] = acc_ref[...].astype(WIRE)

    row = lambda w: pl.BlockSpec((TM2, w), lambda i: (i, 0))
    wspec = pl.BlockSpec((N_DEV, D_MODEL, W_BLK), lambda i: (0, 0, 0))
    return pl.pallas_call(
        body, name="inproj_bwd_dw", grid=(L // TM2,),
        in_specs=[row(D_MODEL), row(RG_W), row(D_MODEL), row(S5_W)] + extra_specs, out_specs=wspec,
        out_shape=_S((N_DEV, D_MODEL, W_BLK), WIRE), scratch_shapes=[pltpu.VMEM((N_DEV, D_MODEL, W_BLK), F32)],
        compiler_params=_params(1))(x, dzx, dzg, dzu, *extra)


def _inproj_bwd_dx(dt1, dzx, dzg, dzu, w_in, token=None):
    L = dt1.shape[0]
    extra, extra_specs = _after(token)

    def body(dt1_ref, dzx_ref, dzg_ref, dzu_ref, w_ref, *rest):
        rest[-1][...] = ALPHA * dt1_ref[...] + _mm_nt(_dz_block(dzx_ref, dzg_ref, dzu_ref), w_ref[...])

    tm = min(TM_MM, L)
    row = lambda w: pl.BlockSpec((tm, w), lambda i: (i, 0))
    return pl.pallas_call(
        body, name="inproj_bwd_dx", grid=(L // tm,),
        in_specs=[row(D_MODEL), row(RG_W), row(D_MODEL), row(S5_W), _full((D_MODEL, Z_W))] + extra_specs,
        out_specs=row(D_MODEL), out_shape=_S((L, D_MODEL)), compiler_params=_params(1))(dt1, dzx, dzg, dzu, w_in, *extra)


def _rg_specs(layer):
    tile = lambda rows: pl.BlockSpec((rows, LANE), lambda c: (0, c))
    ptile = lambda rows: pl.BlockSpec((None, rows, LANE), lambda c: (layer, 0, c))
    pheads = pl.BlockSpec((None, 2, RG_HD, RG_HD), lambda c: (layer, c, 0, 0))
    return tile, ptile, pheads, pl.BlockSpec((2, RG_HD, RG_HD), lambda c: (c, 0, 0))


RG_HD = 64


def _bd2(w):
    z = jnp.zeros((RG_HD, RG_HD), w.dtype)
    return jnp.concatenate([jnp.concatenate([w[0], z], axis=1), jnp.concatenate([z, w[1]], axis=1)], axis=0)


def _bd2_diag(m):
    return jnp.stack([m[:RG_HD, :RG_HD], m[RG_HD:, RG_HD:]])


def _rg_fwd(z, cw, cb, wa_bd, wx_bd, ba, bx, lam, layer):
    L = z.shape[0]
    RC = min(RC_RG, L)

    def body(x_ref, cw_ref, cb_ref, wa_ref, wx_ref, ba_ref, bx_ref, lam_ref, hs_ref, *saved):
        row = slice(layer, layer + 1)
        w, b = cw_ref[...], cb_ref[row, :]
        wa, wx, ba_, bx_ = _bd2(wa_ref[...]).astype(MXU), _bd2(wx_ref[...]).astype(MXU), ba_ref[row, :], bx_ref[row, :]
        sp = _softplus(-lam_ref[row, :])

        def step(c, carry):
            r0 = pl.multiple_of(c * RC, RC)
            xe = jnp.concatenate([_halo(x_ref, c, r0), x_ref[pl.ds(r0, RC), :]], axis=0)
            t = _conv_taps(xe)
            h = t[0] * w[0:1] + t[1] * w[1:2] + t[2] * w[2:3] + t[3] * w[3:4] + b
            r, i, a, mult = _rg_gates(h, wa, wx, ba_, bx_, sp)
            hs, carry = _scan_real(a, mult * (i * h), carry)
            hs_ref[pl.ds(r0, RC), :] = hs
            for ref, val in zip(saved, (h, r, i, a, mult)):
                ref[pl.ds(r0, RC), :] = val
            return carry

        lax.fori_loop(0, L // RC, step, jnp.zeros((1, LANE), F32))

    tile, ptile, pheads, _ = _rg_specs(layer)
    return pl.pallas_call(
        body, name="rg_fwd", grid=(N_RG_T,),
        in_specs=[tile(L), ptile(4), tile(2), pheads, pheads, tile(2), tile(2), tile(2)],
        out_specs=[tile(L)] * 6, out_shape=[_S((L, RG_W))] * 6, compiler_params=_params(1))(
            z, cw, cb, wa_bd, wx_bd, ba, bx, lam)


def _rg_bwd(dhs, z, hs, gates, cw, wa_bd, wx_bd, lam, layer):
    L = z.shape[0]
    RC = min(RC_RG, L)

    def body(g_ref, x_ref, hs_ref, h_ref, r_ref, i_ref, a_ref, mult_ref, cw_ref, wa_ref, wx_ref, lam_ref,
             dx_ref, dcw_ref, dcb_ref, dwa_out, dwx_out, dba_ref, dbx_ref, dlam_ref, dwa_ref, dwx_ref):
        w = cw_ref[...]
        wa, wx = _bd2(wa_ref[...]).astype(MXU), _bd2(wx_ref[...]).astype(MXU)
        lam = lam_ref[layer:layer + 1, :]
        sp = _softplus(-lam)
        rows = lax.broadcasted_iota(jnp.int32, (RC, LANE), 0)
        for ref in (dcw_ref, dcb_ref, dwa_ref, dwx_ref, dba_ref, dbx_ref, dlam_ref):
            ref[...] = jnp.zeros_like(ref)
        nch = L // RC

        def step(k, carry):
            cin, nxt = carry
            c = nch - 1 - k
            r0 = pl.multiple_of(c * RC, RC)
            xe = jnp.concatenate([_halo(x_ref, c, r0), x_ref[pl.ds(r0, RC), :]], axis=0)
            t = _conv_taps(xe)
            h, r, i, a, mult = (ref[pl.ds(r0, RC), :] for ref in (h_ref, r_ref, i_ref, a_ref, mult_ref))
            hs_e = jnp.concatenate([_halo(hs_ref, c, r0), hs_ref[pl.ds(r0, RC), :]], axis=0)
            hs_prev = pltpu.roll(hs_e, 1, 0)[8:, :]
            g = g_ref[pl.ds(r0, RC), :]
            cc, cin_new = _scan_real(a, a * g, cin, reverse=True)
            dh = g + _up(cc, 1, rows, cin)
            ih = i * h
            dlog_a = dh * hs_prev * a - (dh * ih) * (a * a) / mult
            di = dh * mult * h
            dhin = dh * mult * i
            dr = dlog_a * ((-RG_C) * sp)
            dlam_ref[...] += _colsum(dlog_a * r)
            dra = dr * r * (1.0 - r)
            dia = di * i * (1.0 - i)
            dwa_ref[...] += _mm_tn(h, dra)
            dwx_ref[...] += _mm_tn(h, dia)
            dba_ref[...] += _colsum(dra)
            dbx_ref[...] += _colsum(dia)
            dhin = dhin + _mm_nt(dra, wa) + _mm_nt(dia, wx)
            de = jnp.concatenate([dhin, nxt], axis=0)
            n = RC + 8
            dx = (dhin * w[3:4] + pltpu.roll(de, n - 1, 0)[:RC, :] * w[2:3]
                  + pltpu.roll(de, n - 2, 0)[:RC, :] * w[1:2] + pltpu.roll(de, n - 3, 0)[:RC, :] * w[0:1])
            dx_ref[pl.ds(r0, RC), :] = dx
            for kk in range(4):
                dcw_ref[kk:kk + 1, :] += _colsum(dhin * t[kk])
            dcb_ref[...] += _colsum(dhin)
            return cin_new, dhin[0:8, :]

        lax.fori_loop(0, nch, step, (jnp.zeros((1, LANE), F32), jnp.zeros((8, LANE), F32)))
        dlam_ref[...] = dlam_ref[...] * (RG_C * _sigmoid(-lam))
        dwa_out[...], dwx_out[...] = _bd2_diag(dwa_ref[...]), _bd2_diag(dwx_ref[...])

    tile, ptile, pheads, gheads = _rg_specs(layer)
    heads = _S((2 * N_RG_T, RG_HD, RG_HD))
    return pl.pallas_call(
        body, name="rg_bwd", grid=(N_RG_T,),
        in_specs=[tile(L)] * 8 + [ptile(4), pheads, pheads, tile(2)],
        out_specs=[tile(L), tile(4), tile(1), gheads, gheads, tile(1), tile(1), tile(1)],
        out_shape=[_S((L, RG_W)), _S((4, RG_W)), _S((1, RG_W)), heads, heads, _S((1, RG_W)), _S((1, RG_W)), _S((1, RG_W))],
        scratch_shapes=[pltpu.VMEM((LANE, LANE), F32), pltpu.VMEM((LANE, LANE), F32)],
        compiler_params=_params(1))(dhs, z, hs, *gates, cw, wa_bd, wx_bd, lam)


def _cmul(ar, ai, br, bi):
    return ar * br - ai * bi, ar * bi + ai * br


S5_TW = S5_N // N_S5_T


S5_H = 16
S5_GT = LANE // S5_H


def _s5_specs(L, layer):
    in_tile = pl.BlockSpec((L, LANE), lambda t: (0, t))
    st = pl.BlockSpec((L, S5_TW), lambda t: (0, t))
    pg = pl.BlockSpec((None, S5_GT, S5_H, S5_P), lambda t: (layer * N_S5_T + t, 0, 0, 0))
    plb = pl.BlockSpec((None, S5_GT, S5_P), lambda t: (layer * N_S5_T + t, 0, 0))
    gg = pl.BlockSpec((None, S5_GT, S5_H, S5_P), lambda t: (t, 0, 0, 0))
    glb = pl.BlockSpec((None, S5_GT, S5_P), lambda t: (t, 0, 0))
    dv = pl.BlockSpec((1, LANE), lambda t: (0, t))
    return in_tile, st, pg, plb, gg, glb, dv


def _bd8(blocks):
    rows = []
    for g in range(S5_GT):
        pieces = [blocks[g]]
        if g:
            pieces.insert(0, jnp.zeros((S5_H, S5_P * g), blocks.dtype))
        if g < S5_GT - 1:
            pieces.append(jnp.zeros((S5_H, S5_P * (S5_GT - 1 - g)), blocks.dtype))
        rows.append(jnp.concatenate(pieces, axis=1))
    return jnp.concatenate(rows, axis=0)


def _bd8_diag(m):
    return jnp.stack([m[S5_H * g:S5_H * (g + 1), S5_P * g:S5_P * (g + 1)] for g in range(S5_GT)])


def _row8(v):
    return jnp.concatenate([v[g:g + 1] for g in range(S5_GT)], axis=1)


def _row8_split(r):
    return jnp.concatenate([r[:, S5_P * g:S5_P * (g + 1)] for g in range(S5_GT)], axis=0)


def _layer_row_tile(layer):
    return pl.BlockSpec((None, 1, LANE), lambda t: (layer, 0, t))


def _s5_fwd(z, bb_re, bb_im, lb_re, lb_im, c_re, c_im, dvec, layer):
    L = z.shape[0]

    def body(u_ref, bbr_ref, bbi_ref, lr_ref, li_ref, cr_ref, ci_ref, d_ref, y_ref, sr_ref, si_ref):
        bbr, bbi = _bd8(bbr_ref[...]).astype(MXU), _bd8(bbi_ref[...]).astype(MXU)
        cr, ci = _bd8(cr_ref[...]).astype(MXU), _bd8(ci_ref[...]).astype(MXU)
        dv = d_ref[...]
        steps, e = _tile_powers(_row8(lr_ref[...]), _row8(li_ref[...]))

        def step(c, carry):
            r0 = pl.multiple_of(c * RC, RC)
            u = u_ref[pl.ds(r0, RC), :]
            ub = u.astype(MXU)
            sr = jnp.dot(ub, bbr, preferred_element_type=F32)
            si = jnp.dot(ub, bbi, preferred_element_type=F32)
            sr, si, carry = _scan_lti(sr, si, carry, steps, e)
            sr_ref[pl.ds(r0, RC), :] = sr
            si_ref[pl.ds(r0, RC), :] = si
            y_ref[pl.ds(r0, RC), :] = dv * u + (_mm_nt(sr, cr) - _mm_nt(si, ci))
            return carry

        zero = jnp.zeros((1, S5_TW), F32)
        lax.fori_loop(0, L // RC, step, (zero, zero))

    in_tile, st, pg, plb, _, _, _ = _s5_specs(L, layer)
    u_tile = pl.BlockSpec((L, LANE), lambda t: (0, C_S5U // LANE + t))
    return pl.pallas_call(
        body, name="s5_fwd", grid=(N_S5_T,),
        in_specs=[u_tile, pg, pg, plb, plb, pg, pg, _layer_row_tile(layer)],
        out_specs=[in_tile, st, st],
        out_shape=[_S((L, S5_W)), _S((L, S5_N)), _S((L, S5_N))],
        compiler_params=_params(1))(z, bb_re, bb_im, lb_re, lb_im, c_re, c_im, dvec)


def _s5_bwd(dy0, z, s_re, s_im, bb_re, bb_im, lb_re, lb_im, c_re, c_im, dvec, layer, token=None):
    L = z.shape[0]
    extra, extra_specs = _after(token)

    def body(dy_ref, u_ref, sr_ref, si_ref, bbr_ref, bbi_ref, lr_ref, li_ref, cr_ref, ci_ref, d_ref, *rest):
        (du_ref, dbbr_out, dbbi_out, dlr_out, dli_out, dcr_out, dci_out, dd_ref,
         dbbr_ref, dbbi_ref, dcr_ref, dci_ref, dlr_ref, dli_ref) = rest[len(extra):]
        bbr, bbi = _bd8(bbr_ref[...]).astype(MXU), _bd8(bbi_ref[...]).astype(MXU)
        cr, ci = _bd8(cr_ref[...]).astype(MXU), _bd8(ci_ref[...]).astype(MXU)
        lr, li = _row8(lr_ref[...]), -_row8(li_ref[...])
        dv = d_ref[...]
        steps, e = _tile_powers(lr, li, reverse=True)
        for ref in (dbbr_ref, dbbi_ref, dlr_ref, dli_ref, dcr_ref, dci_ref, dd_ref):
            ref[...] = jnp.zeros_like(ref)
        nch = L // RC

        def step(k, carry):
            c = nch - 1 - k
            r0 = pl.multiple_of(c * RC, RC)
            dy = dy_ref[pl.ds(r0, RC), :]
            u = u_ref[pl.ds(r0, RC), :]
            dyb, ub = dy.astype(MXU), u.astype(MXU)
            sr, si = sr_ref[pl.ds(r0, RC), :], si_ref[pl.ds(r0, RC), :]
            dcr_ref[...] += _mm_tn(dyb, sr)
            dci_ref[...] -= _mm_tn(dyb, si)
            gr = jnp.dot(dyb, cr, preferred_element_type=F32)
            gi = -jnp.dot(dyb, ci, preferred_element_type=F32)
            gr, gi, carry = _scan_lti(gr, gi, carry, steps, e, reverse=True)
            pr_ = pltpu.roll(jnp.concatenate([_halo(sr_ref, c, r0), sr], axis=0), 1, 0)[8:, :]
            pi_ = pltpu.roll(jnp.concatenate([_halo(si_ref, c, r0), si], axis=0), 1, 0)[8:, :]
            dlr_ref[...] += _colsum(pr_ * gr + pi_ * gi)
            dli_ref[...] += _colsum(pr_ * gi - pi_ * gr)
            grb, gib = gr.astype(MXU), gi.astype(MXU)
            dbbr_ref[...] += _mm_tn(ub, grb)
            dbbi_ref[...] += _mm_tn(ub, gib)
            du_ref[pl.ds(r0, RC), :] = dv * dy + (_mm_nt(grb, bbr) + _mm_nt(gib, bbi))
            dd_ref[...] += _colsum(dy * u)
            return carry

        zero = jnp.zeros((1, S5_TW), F32)
        lax.fori_loop(0, nch, step, (zero, zero))
        dbbr_out[...], dbbi_out[...] = _bd8_diag(dbbr_ref[...]), _bd8_diag(dbbi_ref[...])
        dcr_out[...], dci_out[...] = _bd8_diag(dcr_ref[...]), _bd8_diag(dci_ref[...])
        dlr_out[...], dli_out[...] = _row8_split(dlr_ref[...]), _row8_split(dli_ref[...])

    in_tile, st, pg, plb, gg, glb, dv = _s5_specs(L, layer)
    u_tile = pl.BlockSpec((L, LANE), lambda t: (0, C_S5U // LANE + t))
    groups, rows = _S((N_S5_T, S5_GT, S5_H, S5_P)), _S((N_S5_T, S5_GT, S5_P))
    wide = pltpu.VMEM((LANE, S5_TW), F32)
    return pl.pallas_call(
        body, name="s5_bwd", grid=(N_S5_T,),
        in_specs=[in_tile, u_tile, st, st, pg, pg, plb, plb, pg, pg, _layer_row_tile(layer)] + extra_specs,
        out_specs=[in_tile, gg, gg, glb, glb, gg, gg, dv],
        out_shape=[_S((L, S5_W)), groups, groups, rows, rows, groups, groups, _S((1, S5_W))],
        scratch_shapes=[wide, wide, wide, wide, pltpu.VMEM((1, S5_TW), F32), pltpu.VMEM((1, S5_TW), F32)],
        compiler_params=_params(1))(dy0, z, s_re, s_im, bb_re, bb_im, lb_re, lb_im, c_re, c_im, dvec, *extra)


def _disc(ar, ai, ls):
    dt = jnp.exp(ls)
    mag = jnp.exp(ar * dt)
    lr = mag * jnp.cos(ai * dt)
    li = mag * jnp.sin(ai * dt)
    den = ar * ar + ai * ai
    cr = ((lr - 1.0) * ar + li * ai) / den
    ci = (li * ar - (lr - 1.0) * ai) / den
    return lr, li, cr, ci


def _s5_disc_fwd(ar, ai, ls, token=None):
    extra, extra_specs = _after(token)

    def body(ar_ref, ai_ref, ls_ref, *rest):
        lr_ref, li_ref, cr_ref, ci_ref = rest[len(extra):]
        lr, li, cr, ci = _disc(ar_ref[...], ai_ref[...], ls_ref[...])
        lr_ref[...], li_ref[...], cr_ref[...], ci_ref[...] = lr, li, cr, ci

    sh = _S(ar.shape)
    vm = pl.BlockSpec(memory_space=pltpu.VMEM)
    return pl.pallas_call(body, name="s5_disc_fwd", in_specs=[vm, vm, vm] + extra_specs, out_shape=[sh, sh, sh, sh])(
        ar, ai, ls, *extra)


def _layers(refs):
    return jnp.concatenate([r[...] for r in refs], axis=0)


def _s5_disc_bwd(ar, ai, ls, dlr, dli, dcr, dci):
    n = len(dlr)

    def body(ar_ref, ai_ref, ls_ref, *rest):
        dcr_ref, dci_ref, dar_ref, dai_ref, dls_ref = rest[2 * n:]
        _, vjp = jax.vjp(_disc, ar_ref[...], ai_ref[...], jnp.broadcast_to(ls_ref[...], ar_ref.shape))
        dar, dai, dls = vjp((_layers(rest[:n]), _layers(rest[n:2 * n]), dcr_ref[...], dci_ref[...]))
        dar_ref[...], dai_ref[...] = dar, dai
        dls_ref[...] = jnp.sum(dls, axis=1, keepdims=True)

    return pl.pallas_call(body, name="s5_disc_bwd", out_shape=[_S(ar.shape), _S(ar.shape), _S(ls.shape)])(
        ar, ai, ls, *dlr, *dli, dcr, dci)


def _s5_bscale_fwd(cr, ci, br, bi):
    def body(cr_ref, ci_ref, br_ref, bi_ref, or_ref, oi_ref):
        or_ref[...], oi_ref[...] = _cmul(cr_ref[...], ci_ref[...], br_ref[...], bi_ref[...])

    return pl.pallas_call(body, name="s5_bscale_fwd", out_shape=[_S(br.shape), _S(br.shape)])(cr, ci, br, bi)


def _s5_bscale_bwd(cr, ci, br, bi, gr, gi):
    n = len(gr)

    def body(cr_ref, ci_ref, br_ref, bi_ref, *rest):
        dbr_ref, dbi_ref, dcr_ref, dci_ref = rest[2 * n:]
        cr_, ci_, br_, bi_ = (r[...] for r in (cr_ref, ci_ref, br_ref, bi_ref))
        gr_, gi_ = _layers(rest[:n]), _layers(rest[n:2 * n])
        dbr_ref[...] = cr_ * gr_ + ci_ * gi_
        dbi_ref[...] = cr_ * gi_ - ci_ * gr_
        dcr_ref[...] = jnp.sum(gr_ * br_ + gi_ * bi_, axis=1, keepdims=True)
        dci_ref[...] = jnp.sum(gi_ * br_ - gr_ * bi_, axis=1, keepdims=True)

    return pl.pallas_call(body, name="s5_bscale_bwd",
                          out_shape=[_S(br.shape), _S(br.shape), _S(cr.shape), _S(cr.shape)])(cr, ci, br, bi, *gr, *gi)


def _row(w):
    return pl.BlockSpec((TM, w), lambda i: (i, 0))


def _full(shape):
    return pl.BlockSpec(tuple(shape), lambda i: (0,) * len(shape))


def _gate_rows():
    return [pl.BlockSpec((TM, RG_W), lambda i: (i, C_RGG // RG_W))] + [
        pl.BlockSpec((TM, LANE), lambda i, k=k: (i, C_S5G // LANE + k)) for k in range(N_S5_T)]


def _p_rows(layer):
    return pl.BlockSpec((None, None, TM, PLE_D), lambda i: (layer, 0, i, 0))


DEPTH = 2


def _lrow(layer, width):
    return _full((DEPTH, width))


def _pick(ref, layer):
    return ref[layer:layer + 1, :]


def _post_fwd(x, hs, z, y0, p, w_glu, b_glu, w_out, g1, b1, ple_w, w_pg, b_pg, g2, b2, layer):
    L = x.shape[0]

    def body(x_ref, hs_ref, zg_ref, zs0_ref, zs1_ref, zs2_ref, y0_ref, p_ref, wg_ref, bg_ref, wo_ref, g1_ref, b1_ref, pw_ref,
             wpg_ref, bpg_ref, g2_ref, b2_ref, x2_ref, xh1_ref, xh2_ref, gt_ref, rstd1_ref, rstd2_ref):
        rg_gate = zg_ref[...]
        s5_gate = jnp.concatenate([zs0_ref[...], zs1_ref[...], zs2_ref[...]], axis=1)
        rg_y = hs_ref[...] * _silu_and_grad(rg_gate)[0]
        y1 = _gelu(y0_ref[...])
        gl = _sigmoid(_mm(y1, wg_ref[...]) + _pick(bg_ref, layer))
        s5_y = (y1 * gl) * _silu_and_grad(s5_gate)[0]
        mix = _mm(jnp.concatenate([rg_y.astype(MXU), s5_y.astype(MXU)], axis=1), wo_ref[...])
        t1 = ALPHA * x_ref[...] + mix
        x1, xh1, rstd1 = _ln_fwd(t1, _pick(g1_ref, layer), _pick(b1_ref, layer))
        q = _mm(p_ref[...], pw_ref[...])
        gt = _sigmoid(_mm(x1, wpg_ref[...]) + _pick(bpg_ref, layer))
        t2 = ALPHA * x1 + q * gt
        x2, xh2, rstd2 = _ln_fwd(t2, _pick(g2_ref, layer), _pick(b2_ref, layer))
        x2_ref[...], xh1_ref[...], xh2_ref[...], gt_ref[...] = x2, xh1, xh2, gt
        rstd1_ref[...], rstd2_ref[...] = rstd1, rstd2

    vec = _lrow(layer, D_MODEL)
    return pl.pallas_call(
        body, name="post_fwd", grid=(L // TM,),
        in_specs=[_row(D_MODEL), _row(RG_W), *_gate_rows(), _row(S5_W), _p_rows(layer), _full((S5_W, S5_W)),
                  _lrow(layer, S5_W), _full((D_MODEL, D_MODEL)), vec, vec, _full((PLE_D, D_MODEL)), _full((D_MODEL, D_MODEL)),
                  vec, vec, vec],
        out_specs=[_row(D_MODEL)] * 4 + [_row(1)] * 2, out_shape=[_S((L, D_MODEL))] * 4 + [_S((L, 1))] * 2,
        compiler_params=_params(1))(x, hs, z, z, z, z, y0, p, w_glu, b_glu, w_out, g1, b1, ple_w, w_pg, b_pg, g2, b2)


def _post_bwd_a(dx2_or_target, is_top, xh2, xh1, rstd2, rstd1, gt, p, ple_w, w_pg, g1, b1, g2, b2, layer, token=None):
    L = xh1.shape[0]
    extra, extra_specs = _after(token)

    def body(d_ref, xh2_ref, xh1_ref, rstd2_ref, rstd1_ref, gt_ref, p_ref, pw_ref, wpg_ref, g1_ref, b1_ref, g2_ref,
             b2_ref, *rest):
        (dt1_ref, dpw_out, dwpg_out, dbpg_ref, dg1_ref, db1_ref, dg2_ref, db2_ref, loss_ref, dpw_ref,
         dwpg_ref) = rest[len(extra):]
        @pl.when(pl.program_id(0) == 0)
        def _():
            for ref in (dpw_ref, dwpg_ref, dbpg_ref, dg1_ref, db1_ref, dg2_ref, db2_ref, loss_ref):
                ref[...] = jnp.zeros_like(ref)

        g1, g2 = _pick(g1_ref, layer), _pick(g2_ref, layer)
        xh1, xh2, rstd1, rstd2 = xh1_ref[...], xh2_ref[...], rstd1_ref[...], rstd2_ref[...]
        x1 = xh1 * g1 + _pick(b1_ref, layer)
        if is_top:
            err = (xh2 * g2 + _pick(b2_ref, layer)) - d_ref[...]
            loss_ref[...] += _colsum(err * err)
            dx2 = err * (1.0 / D_MODEL)
        else:
            dx2 = d_ref[...]
        p = p_ref[...]
        q, gt = _mm(p, pw_ref[...]), gt_ref[...]
        dg2_ref[...] += _colsum(dx2 * xh2)
        db2_ref[...] += _colsum(dx2)
        dt2 = _ln_bwd(dx2, xh2, rstd2, g2)
        dq = dt2 * gt
        dgpre = (dt2 * q) * gt * (1.0 - gt)
        dpw_ref[...] += _mm_tn(p, dq)
        dwpg_ref[...] += _mm_tn(x1, dgpre)
        dbpg_ref[...] += _colsum(dgpre)
        dx1 = ALPHA * dt2 + _mm_nt(dgpre, wpg_ref[...])
        dg1_ref[...] += _colsum(dx1 * xh1)
        db1_ref[...] += _colsum(dx1)
        dt1_ref[...] = _ln_bwd(dx1, xh1, rstd1, g1)

        @pl.when(pl.program_id(0) == L // TM - 1)
        def _():
            dpw_out[...] = dpw_ref[...].astype(WIRE)
            dwpg_out[...] = dwpg_ref[...].astype(WIRE)

    vec, lvec = _full((1, D_MODEL)), _lrow(layer, D_MODEL)
    return pl.pallas_call(
        body, name="post_bwd_a_top" if is_top else "post_bwd_a", grid=(L // TM,),
        in_specs=[_row(D_MODEL), _row(D_MODEL), _row(D_MODEL), _row(1), _row(1), _row(D_MODEL), _p_rows(layer),
                  _full((PLE_D, D_MODEL)), _full((D_MODEL, D_MODEL)), lvec, lvec, lvec, lvec] + extra_specs,
        out_specs=[_row(D_MODEL), _full((PLE_D, D_MODEL)), _full((D_MODEL, D_MODEL)), vec, vec, vec, vec, vec, vec],
        out_shape=[_S((L, D_MODEL)), _S((PLE_D, D_MODEL), WIRE), _S((D_MODEL, D_MODEL), WIRE)] + [_S((1, D_MODEL))] * 6,
        scratch_shapes=[pltpu.VMEM((PLE_D, D_MODEL), F32), pltpu.VMEM((D_MODEL, D_MODEL), F32)],
        compiler_params=_params(1))(dx2_or_target, xh2, xh1, rstd2, rstd1, gt, p, ple_w, w_pg, g1, b1, g2, b2, *extra)


def _post_bwd_b(dt1, z, hs, y0, w_out, w_glu, b_glu, layer):
    L = dt1.shape[0]

    def body(dt1_ref, zg_ref, zs0_ref, zs1_ref, zs2_ref, hs_ref, y0_ref, wo_ref, wg_ref, bg_ref,
             dhs_ref, dy0_ref, dzg_ref, dwo_out, dwg_out, dbg_ref, dwo_ref, dwg_ref):
        @pl.when(pl.program_id(0) == 0)
        def _():
            for ref in (dwo_ref, dwg_ref, dbg_ref):
                ref[...] = jnp.zeros_like(ref)

        dt1b = dt1_ref[...].astype(MXU)
        dm = _mm_nt(dt1b, wo_ref[...])
        d_rgy, d_s5y = dm[:, :RG_W], dm[:, RG_W:]
        rg_gate = zg_ref[...]
        s5_gate = jnp.concatenate([zs0_ref[...], zs1_ref[...], zs2_ref[...]], axis=1)
        hs = hs_ref[...]
        sl, dsl = _silu_and_grad(rg_gate)
        dhs_ref[...] = d_rgy * sl
        dzg_ref[:, :RG_W] = d_rgy * hs * dsl
        y0 = y0_ref[...]
        y1 = _gelu(y0)
        gl = _sigmoid(_mm(y1, wg_ref[...]) + _pick(bg_ref, layer))
        y2 = y1 * gl
        sl2, dsl = _silu_and_grad(s5_gate)
        m = jnp.concatenate([(hs * sl).astype(MXU), (y2 * sl2).astype(MXU)], axis=1)
        dwo_ref[...] += _mm_tn(m, dt1b)
        dy2 = d_s5y * sl2
        dzg_ref[:, RG_W:] = d_s5y * y2 * dsl
        dglpre = (dy2 * y1) * gl * (1.0 - gl)
        dwg_ref[...] += _mm_tn(y1, dglpre)
        dbg_ref[...] += _colsum(dglpre)
        dy1 = dy2 * gl + _mm_nt(dglpre, wg_ref[...])
        dy0_ref[...] = dy1 * _gelu_grad(y0)

        @pl.when(pl.program_id(0) == L // TM - 1)
        def _():
            dwo_out[...] = dwo_ref[...].astype(WIRE)
            dwg_out[...] = dwg_ref[...].astype(WIRE)

    return pl.pallas_call(
        body, name="post_bwd_b", grid=(L // TM,),
        in_specs=[_row(D_MODEL), *_gate_rows(), _row(RG_W), _row(S5_W), _full((D_MODEL, D_MODEL)),
                  _full((S5_W, S5_W)), _lrow(layer, S5_W)],
        out_specs=[_row(RG_W), _row(S5_W), _row(D_MODEL), _full((D_MODEL, D_MODEL)), _full((S5_W, S5_W)), _full((1, S5_W))],
        out_shape=[_S((L, RG_W)), _S((L, S5_W)), _S((L, D_MODEL)), _S((D_MODEL, D_MODEL), WIRE), _S((S5_W, S5_W), WIRE),
                   _S((1, S5_W))],
        scratch_shapes=[pltpu.VMEM((D_MODEL, D_MODEL), F32), pltpu.VMEM((S5_W, S5_W), F32)],
        compiler_params=_params(1))(dt1, z, z, z, z, hs, y0, w_out, w_glu, b_glu)


def _adamw(parts, w, m, v, token=None):
    nl = len(parts)
    extra, extra_specs = _after(token)
    n, R, C = parts[0].shape
    tr = R
    for cand in (512, 256, 128, 64, 32, 16, 8):
        if R % cand == 0 and n * cand * C * 4 <= 4 * 1024 * 1024:
            tr = cand
            break
    nblk = R // tr

    def body(*refs):
        p_refs = refs[:nl]
        w_ref, m_ref, v_ref = refs[nl:nl + 3]
        g_ref, d_ref, nm_ref, nv_ref = refs[nl + 3 + len(extra):]
        layer = pl.program_id(0)
        g = None
        for li, p_ref in enumerate(p_refs):
            s = p_ref[0].astype(F32)
            for k in range(1, n):
                s = s + p_ref[k].astype(F32)
            g = s if g is None else jnp.where(layer == li, s, g)
        nm = B1 * m_ref[...] + (1.0 - B1) * g
        nv = B2 * v_ref[...] + (1.0 - B2) * (g * g)
        d_ref[...] = (-LR) * ((nm / BC1) / (jnp.sqrt(nv / BC2) + EPS) + WD * w_ref[...])
        g_ref[...], nm_ref[...], nv_ref[...] = g, nm, nv

    def part_spec(li):
        return pl.BlockSpec((n, tr, C), lambda l, i: (0, jnp.where(l == li, i, jnp.where(l < li, 0, nblk - 1)), 0))

    blk = pl.BlockSpec((tr, C), lambda l, i: (l * nblk + i, 0))
    return pl.pallas_call(
        body, name="adamw", grid=(nl, nblk),
        in_specs=[part_spec(li) for li in range(nl)] + [blk, blk, blk] + extra_specs,
        out_specs=[blk] * 4, out_shape=[_S((nl * R, C))] * 4, compiler_params=_params(2))(*parts, w, m, v, *extra)


def _adamw_sharded(names, recv, w, m, v, name, token=None):
    n, nl = len(names), len(recv)
    extra, extra_specs = _after(token)
    n_in = n * (nl + 3)

    def body(*refs):
        outs = refs[n_in + len(extra):]
        for j in range(n):
            w_ref, m_ref, v_ref = (refs[(nl + t) * n + j] for t in range(3))
            g_ref, d_ref, nm_ref, nv_ref = (outs[t * n + j] for t in range(4))
            for l in range(nl):
                p_ref = refs[l * n + j]
                g = p_ref[0].astype(F32)
                for q in range(1, N_DEV):
                    g = g + p_ref[q].astype(F32)
                nm = B1 * m_ref[l] + (1.0 - B1) * g
                nv = B2 * v_ref[l] + (1.0 - B2) * (g * g)
                d_ref[l] = (-LR) * ((nm / BC1) / (jnp.sqrt(nv / BC2) + EPS) + WD * w_ref[l])
                g_ref[l], nm_ref[l], nv_ref[l] = g, nm, nv

    ins = [r[k] for r in recv for k in names] + [t[k] for t in (w, m, v) for k in names]
    vm = pl.BlockSpec(memory_space=pltpu.VMEM)
    outs = pl.pallas_call(body, name=name, in_specs=[vm] * n_in + extra_specs,
                          out_shape=[_S(w[k].shape) for _ in range(4) for k in names],
                          compiler_params=pltpu.CompilerParams(vmem_limit_bytes=VMEM_LIMIT))(*ins, *extra)
    return [{k: outs[t * n + j] for j, k in enumerate(names)} for t in range(4)]


def _adamw_packed(names, packed, w, m, v, name):
    n = len(names)
    starts, _ = _packed_starts([w[k].shape for k in names] + [(1, 1)])

    def body(p_ref, *refs):
        refs[-1][...] = p_ref[starts[-1]:starts[-1] + 1, :1]
        for j in range(n):
            w_ref, m_ref, v_ref, g_ref, d_ref, nm_ref, nv_ref = (refs[k * n + j] for k in range(7))
            gj = _packed_get(p_ref, starts[j], w_ref.shape)
            nm = B1 * m_ref[...] + (1.0 - B1) * gj
            nv = B2 * v_ref[...] + (1.0 - B2) * (gj * gj)
            d_ref[...] = (-LR) * ((nm / BC1) / (jnp.sqrt(nv / BC2) + EPS) + WD * w_ref[...])
            g_ref[...], nm_ref[...], nv_ref[...] = gj, nm, nv

    ins = [t[k] for t in (w, m, v) for k in names]
    outs = pl.pallas_call(body, name=name, out_shape=[_S(w[k].shape) for _ in range(4) for k in names] + [_S((1, 1))],
                          compiler_params=pltpu.CompilerParams(vmem_limit_bytes=VMEM_LIMIT))(packed, *ins)
    return [{k: outs[t * n + j] for j, k in enumerate(names)} for t in range(4)] + [outs[-1].reshape(())]


def _me():
    return lax.axis_index("x"), lax.axis_index("y"), lax.axis_index("c")


def _lin(dev):
    return 4 * dev[0] + 2 * dev[1] + dev[2]


def _blk(ref, axis, size, idx):
    nd = len(ref.shape)
    start = idx * size
    if axis == nd - 1 and size % LANE == 0:
        start = pl.multiple_of(start, LANE)
    elif axis == nd - 2 and size % 16 == 0:
        start = pl.multiple_of(start, 16)
    ix = [slice(None)] * nd
    ix[axis] = pl.ds(start, size)
    return ref.at[tuple(ix)]


HBM_SPEC = pl.BlockSpec(memory_space=pltpu.HBM)
SEM_SPEC = pl.BlockSpec(memory_space=pltpu.SEMAPHORE)
EFFECT = pltpu.SideEffectType.DATAFLOW_SIDE_EFFECTING


def _peers(x, y, c):
    flip = lambda v, f: 1 - v if f else v
    return [(flip(x, k & 4), flip(y, k & 2), flip(c, k & 1)) for k in range(1, N_DEV)]


def _land_shape(mode, s, axis):
    if mode == "gather":
        return s.shape[:axis] + (N_DEV * s.shape[axis],) + s.shape[axis + 1:]
    return (N_DEV,) + s.shape[:axis] + (s.shape[axis] // N_DEV,) + s.shape[axis + 1:]


def _src_view(mode, ref, axis, peer):
    return ref if mode == "gather" else _blk(ref, axis, ref.shape[axis] // N_DEV, peer)


def _dst_view(mode, land, axis, sender):
    return _blk(land, axis, land.shape[axis] // N_DEV, sender) if mode == "gather" else land.at[sender]


def _blocks(mode, land, axis, k):
    if mode == "gather":
        ix = [slice(None)] * len(land.shape)
        ix[axis] = pl.ds(0, k * (land.shape[axis] // N_DEV))
        return land.at[tuple(ix)]
    return land.at[pl.ds(0, k)]


ARRIVALS = {None: N_DEV - 1, "near": 4, "relay": 3}


def _routes(route, x, y, c):
    me, sibling = (x, y, c), (x, y, 1 - c)
    chips = [(1 - x, y), (x, 1 - y), (1 - x, 1 - y)]
    if route == "near":
        return [(me, sibling)] + [(me, (*chip, c)) for chip in chips]
    if route == "relay":
        return [((*chip, c), sibling) for chip in chips]
    return [(me, peer) for peer in _peers(x, y, c)]


def _place_own(mode, srcs, axes, name, after=None):
    n = len(srcs)
    extra, extra_specs = _after(after)

    def body(me_ref, *refs):
        for a in range(n):
            out = refs[n + len(extra) + a]
            out[...] = refs[a][...].reshape(out.shape)

    def at_me(shape, axis):
        return lambda i, me: tuple(me[0] if d == axis else 0 for d in range(len(shape)))

    in_specs, out_specs = [], []
    for s, axis in zip(srcs, axes):
        if mode == "gather":
            in_specs.append(pl.BlockSpec(s.shape, lambda i, me, nd=len(s.shape): (0,) * nd))
            out_specs.append(pl.BlockSpec(s.shape, at_me(s.shape, axis)))
        else:
            blk = s.shape[:axis] + (s.shape[axis] // N_DEV,) + s.shape[axis + 1:]
            in_specs.append(pl.BlockSpec(blk, at_me(blk, axis)))
            out_specs.append(pl.BlockSpec((1,) + blk, at_me((1,) + blk, 0)))
    me = _lin(_me()).astype(jnp.int32).reshape(1)
    return pl.pallas_call(
        body, name=name, out_shape=[_S(_land_shape(mode, s, a), s.dtype) for s, a in zip(srcs, axes)],
        grid_spec=pltpu.PrefetchScalarGridSpec(num_scalar_prefetch=1, grid=(1,), in_specs=in_specs + extra_specs,
                                               out_specs=out_specs),
        compiler_params=_params(1))(me, *srcs, *extra)


def _place_shards(shards, layers, axes, dtypes, name, after=None):
    n = len(shards)
    extra, extra_specs = _after(after)

    def body(me_ref, *refs):
        for a in range(n):
            out = refs[n + len(extra) + a]
            out[...] = refs[a][...].astype(out.dtype)

    in_specs, out_specs, out_shape = [], [], []
    for s, layer, axis, dt in zip(shards, layers, axes, dtypes):
        shape = s.shape if layer is None else s.shape[1:]
        nd = len(shape)
        if layer is None:
            in_specs.append(pl.BlockSpec(shape, lambda i, me, nd=nd: (0,) * nd))
        else:
            in_specs.append(pl.BlockSpec((None,) + shape, lambda i, me, nd=nd, layer=layer: (layer,) + (0,) * nd))
        out_specs.append(pl.BlockSpec(shape, lambda i, me, nd=nd, axis=axis: tuple(me[0] if d == axis else 0 for d in range(nd))))
        out_shape.append(_S(shape[:axis] + (N_DEV * shape[axis],) + shape[axis + 1:], dt))
    me = _lin(_me()).astype(jnp.int32).reshape(1)
    return pl.pallas_call(
        body, name=name, out_shape=out_shape,
        grid_spec=pltpu.PrefetchScalarGridSpec(num_scalar_prefetch=1, grid=(1,), in_specs=in_specs + extra_specs,
                                               out_specs=out_specs),
        compiler_params=_params(1))(me, *shards, *extra)


def _push_start(mode, srcs, lands, axes, name, route=None):
    n, ns = len(lands), len(srcs)

    def body(*refs):
        src_refs, land_refs = refs[:ns], refs[ns:ns + n]
        send_sems, recv_sems = refs[ns + n], refs[ns + n + 1]
        token = refs[-1]
        x, y, c = _me()
        for a in range(n):
            for block, peer in _routes(route, x, y, c):
                there = _dst_view(mode, land_refs[a], axes[a], _lin(block))
                pltpu.make_async_remote_copy(
                    src_ref=_src_view(mode, src_refs[a], axes[a], _lin(peer)) if ns else there, dst_ref=there,
                    send_sem=send_sems.at[a], recv_sem=recv_sems.at[a], device_id=peer, device_id_type=MESH).start()
        token[...] = jnp.zeros_like(token)

    hbm = lambda s: pltpu.HBM(s.shape, s.dtype)
    outs = pl.pallas_call(
        body, name=name,
        out_shape=(pltpu.SemaphoreType.DMA((n,)), pltpu.SemaphoreType.DMA((n,)), *[hbm(s) for s in srcs], *[hbm(s) for s in lands],
                   _S((SUB, LANE))),
        in_specs=[HBM_SPEC] * (ns + n),
        out_specs=(SEM_SPEC, SEM_SPEC, *[HBM_SPEC] * (ns + n), pl.BlockSpec(memory_space=pltpu.VMEM)),
        input_output_aliases={i: 2 + i for i in range(ns + n)},
        compiler_params=pltpu.CompilerParams(has_side_effects=EFFECT),
    )(*[pltpu.with_memory_space_constraint(s, pltpu.HBM) for s in list(srcs) + list(lands)])
    return outs[0], outs[1], outs[2:2 + ns], outs[2 + ns:2 + ns + n], outs[-1]


def _push_wait(mode, send_sems, recv_sems, srcs, lands, axes, after, name, first=0, route=None):
    n, ns = len(lands), len(srcs)
    after = list(after) if isinstance(after, (list, tuple)) else [after]

    def body(*refs):
        land_refs = refs[ns:ns + n]
        send_sems, recv_sems = refs[ns + n], refs[ns + n + 1]
        x, y, c = _me()
        for a in range(n):
            seven = _blocks(mode, land_refs[a], axes[a], ARRIVALS[route])
            cp = pltpu.make_async_remote_copy(src_ref=seven, dst_ref=seven, send_sem=send_sems.at[first + a],
                                              recv_sem=recv_sems.at[first + a],
                                              device_id=(x, y, 1 - c), device_id_type=MESH)
            cp.wait_send()
            cp.wait_recv()

    hbm = lambda s: pltpu.HBM(s.shape, s.dtype)
    outs = pl.pallas_call(
        body, name=name, out_shape=tuple(hbm(s) for s in list(srcs) + list(lands)),
        in_specs=[HBM_SPEC] * (ns + n) + [SEM_SPEC, SEM_SPEC] + [ANY] * len(after), out_specs=tuple([HBM_SPEC] * (ns + n)),
        input_output_aliases={i: i for i in range(ns + n)},
        compiler_params=pltpu.CompilerParams(has_side_effects=EFFECT),
    )(*srcs, *lands, send_sems, recv_sems, *after)
    return outs[ns:]


def _sum_parts(parts):
    n, R, C = parts.shape

    def body(p_ref, o_ref):
        g = p_ref[0]
        for k in range(1, n):
            g = g + p_ref[k]
        o_ref[...] = g

    return pl.pallas_call(body, name="sum_parts", out_shape=_S((R, C)))(parts)


SMALL =['conv_b', 'rg_wa', 'rg_ba', 'rg_wx', 'rg_bx', 'rg_lambda', 's5_a_re', 's5_a_im', 's5_b_re', 's5_b_im',
         's5_c_re', 's5_c_im', 's5_d', 's5_log_step', 's5_b_glu', 'ln1_g', 'ln1_b', 'ple_gate_b', 'ln2_g', 'ln2_b']
WEIGHTS = ['w_in', 'conv_w', 'conv_b', 'rg_wa', 'rg_ba', 'rg_wx', 'rg_bx', 'rg_lambda', 's5_a_re', 's5_a_im', 's5_b_re',
           's5_b_im', 's5_c_re', 's5_c_im', 's5_d', 's5_log_step', 's5_w_glu', 's5_b_glu', 'w_out', 'ln1_g', 'ln1_b',
           'ple_w', 'ple_gate_w', 'ple_gate_b', 'ln2_g', 'ln2_b']
PACK_ROWS_MULT = 64


STORED = {'s5_b_re': (2, 3), 's5_b_im': (2, 3), 's5_d': (1, 2)}


def _stored(k, a):
    return jnp.swapaxes(a, *STORED[k]) if k in STORED else a


def _two_d(a):
    return a.reshape(-1, a.shape[-1])


def _up8(n):
    return -(-n // SUB) * SUB


def _halves_fit(shape):
    return 2 * shape[1] == LANE and shape[0] % (2 * SUB) == 0


def _packed_rows(shape):
    R, C = shape
    if C % LANE == 0:
        return (C // LANE) * _up8(R)
    return R // 2 if _halves_fit(shape) else _up8(R)


def _packed_put(out_ref, r0, pieces):
    R, C = sum(a.shape[0] for a in pieces), pieces[0].shape[1]
    if _halves_fit((R, C)):
        lo, hi = pieces if len(pieces) == 2 else (pieces[0][:R // 2], pieces[0][R // 2:])
        out_ref[r0:r0 + R // 2, :] = jnp.concatenate([lo, hi], axis=1)
        return
    for a in pieces:
        rows = a.shape[0]
        if C % LANE == 0:
            for j in range(C // LANE):
                out_ref[r0 + j * _up8(R):r0 + j * _up8(R) + rows, :] = a[:, j * LANE:(j + 1) * LANE]
        else:
            out_ref[r0:r0 + rows, :C] = a
        r0 += rows


def _packed_get(ref, r0, shape):
    R, C = shape
    if C % LANE == 0:
        return jnp.concatenate([ref[r0 + j * _up8(R):r0 + j * _up8(R) + R, :] for j in range(C // LANE)], axis=1)
    if _halves_fit(shape):
        both = ref[r0:r0 + R // 2, :]
        return jnp.concatenate([both[:, :C], both[:, C:]], axis=0)
    return ref[r0:r0 + R, :C]


def _packed_starts(shapes):
    starts = [0]
    for s in shapes:
        starts.append(starts[-1] + _packed_rows(s))
    return starts[:-1], starts[-1] + (-starts[-1] % PACK_ROWS_MULT)


def _pack(tree, row, scale):
    groups = [[_two_d(a) for a in (tree[k] if isinstance(tree[k], list) else [tree[k]])] for k in SMALL]
    starts, rows = _packed_starts([(sum(a.shape[0] for a in g), g[0].shape[1]) for g in groups] + [(1, 1)])

    def body(*refs):
        row_ref, out_ref, refs = refs[-2], refs[-1], list(refs[:-2])
        out_ref[...] = jnp.zeros_like(out_ref)
        for r0, g in zip(starts, groups):
            _packed_put(out_ref, r0, [refs.pop(0)[...] for _ in g])
        out_ref[starts[-1]:starts[-1] + 1, :1] = scale * jnp.sum(row_ref[...], axis=1, keepdims=True)

    return pl.pallas_call(body, name="pack_small", out_shape=_S((rows, LANE)))(*sum(groups, []), row)


class _NoHooks:
    token = None
    first_token = None

    def first_weights(self, full, after):
        return full

    def layer_start(self, i, W, after):
        return W

    def late_weights(self, i, W, after):
        return W

    def post_done(self, i, g):
        return None

    def smalls_done(self, grads, loss_row):
        self.small = _small_grads(grads, self.res)
        return None

    def w_in_done(self, i, g):
        return None

    def layer_done(self, i, g, dx):
        return None


def _local_grads(x, p, target, W, disc, hooks):
    depth = 2
    saved = []
    for i in range(depth):
        if i > 0:
            W = hooks.layer_start(i, W, x)
        w = W[i]
        z = _inproj_fwd(x, w['w_in'], hooks.token if i == 0 else None)
        hs, *gates = _rg_fwd(z, w['conv_w'], w['conv_b'], w['wa_bd'], w['wx_bd'], w['rg_ba'], w['rg_bx'], w['rg_lambda'], i)
        d = disc[i]
        y0, s_re, s_im = _s5_fwd(z, d['bb_re'], d['bb_im'], d['lb_re'], d['lb_im'], d['c_re'], d['c_im'], w['s5_d'], i)
        W = hooks.late_weights(i, W, y0)
        w = W[i]
        x2, *norms = _post_fwd(x, hs, z, y0, p, w['s5_w_glu'], w['s5_b_glu'], w['w_out'], w['ln1_g'], w['ln1_b'],
                               w['ple_w'], w['ple_gate_w'], w['ple_gate_b'], w['ln2_g'], w['ln2_b'], i)
        saved.append((x, z, hs, gates, y0, s_re, s_im, norms))
        x = x2

    grads = [None] * depth
    dx = target
    loss = None
    token = None
    for i in reversed(range(depth)):
        w, d = W[i], disc[i]
        xin, z, hs, gates, y0, s_re, s_im, (xh1, xh2, gt, rstd1, rstd2) = saved[i]
        g = {}
        (dt1, g['ple_w'], g['ple_gate_w'], g['ple_gate_b'], g['ln1_g'], g['ln1_b'], g['ln2_g'], g['ln2_b'], lrow) = _post_bwd_a(
            dx, i == depth - 1, xh2, xh1, rstd2, rstd1, gt, p, w['ple_w'], w['ple_gate_w'], w['ln1_g'], w['ln1_b'],
            w['ln2_g'], w['ln2_b'], i, token)
        if i == depth - 1:
            loss_row, loss = lrow, 0.5 / D_MODEL * jnp.sum(lrow)
        dhs, dy0, dzg, g['w_out'], g['s5_w_glu'], g['s5_b_glu'] = _post_bwd_b(dt1, z, hs, y0, w['w_out'], w['s5_w_glu'],
                                                                           w['s5_b_glu'], i)
        (dzu, g['bb_re'], g['bb_im'], g['lb_re'], g['lb_im'], g['c_re'], g['c_im'], g['s5_d']) = _s5_bwd(
            dy0, z, s_re, s_im, d['bb_re'], d['bb_im'], d['lb_re'], d['lb_im'], d['c_re'], d['c_im'], w['s5_d'], i,
            hooks.post_done(i, g))
        (dzx, g['conv_w'], g['conv_b'], g['wa_bd'], g['wx_bd'], g['rg_ba'], g['rg_bx'], g['rg_lambda']) = _rg_bwd(
            dhs, z, hs, gates, w['conv_w'], w['wa_bd'], w['wx_bd'], w['rg_lambda'], i)
        if i == 0:
            g['w_in'] = _inproj_bwd_dw(xin, dzx, dzg, dzu, hooks.smalls_done([g, grads[1]], loss_row))
            dx = _inproj_bwd_dx(dt1, dzx, dzg, dzu, w['w_in'], hooks.w_in_done(i, g))
        else:
            dx, g['w_in'] = _inproj_bwd(dt1, xin, dzx, dzg, dzu, w['w_in'])
        grads[i] = g
        token = hooks.layer_done(i, g, dx)
    return loss, dx, grads


def _s5_layouts_fwd(s5_a_re, s5_a_im, s5_log_step, s5_b_re, s5_b_im, s5_c_re, s5_c_im, token=None):
    depth = s5_a_re.shape[0]
    ar, ai = s5_a_re.reshape(depth * 24, S5_P), s5_a_im.reshape(depth * 24, S5_P)
    ls = s5_log_step.reshape(depth * 24, 1)
    lr, li, cr, ci = _s5_disc_fwd(ar, ai, ls, token)
    per_group = lambda a: a.reshape(depth * 24, 1, S5_P)
    as_c = lambda b: jnp.swapaxes(b, 2, 3).reshape(depth * 24, S5_H, S5_P)
    res = (ar, ai, ls, per_group(cr), per_group(ci), as_c(s5_b_re), as_c(s5_b_im))
    bbr, bbi = _s5_bscale_fwd(*res[3:])
    tiles = lambda a: a.reshape(depth * N_S5_T, S5_GT, S5_H, S5_P)
    rows = lambda a: a.reshape(depth * N_S5_T, S5_GT, S5_P)
    disc = dict(bb_re=tiles(bbr), bb_im=tiles(bbi), lb_re=rows(lr), lb_im=rows(li), c_re=tiles(s5_c_re), c_im=tiles(s5_c_im))
    return [disc] * depth, res


def _s5_layouts_bwd(grads, res):
    ar, ai, ls, cr, ci, br, bi = res
    depth = len(grads)
    layers = lambda k, *shape: [g[k].reshape(shape) for g in grads]
    shape_c = (depth, 24, S5_H, S5_P)
    dbr, dbi, dcr, dci = _s5_bscale_bwd(cr, ci, br, bi, layers('bb_re', 24, S5_H, S5_P), layers('bb_im', 24, S5_H, S5_P))
    gp = (depth * 24, S5_P)
    dar, dai, dls = _s5_disc_bwd(ar, ai, ls, layers('lb_re', 24, S5_P), layers('lb_im', 24, S5_P), dcr.reshape(gp),
                                 dci.reshape(gp))
    return dict(
        s5_a_re=dar.reshape(depth, 24, S5_P), s5_a_im=dai.reshape(depth, 24, S5_P), s5_log_step=dls.reshape(depth, 24),
        s5_b_re=dbr.reshape(shape_c), s5_b_im=dbi.reshape(shape_c),
        s5_c_re=layers('c_re', 24, S5_H, S5_P), s5_c_im=layers('c_im', 24, S5_H, S5_P))


LATE = ('w_out', 'ple_w', 'ple_gate_w', 's5_w_glu')


ROWS = ('conv_b', 'rg_ba', 'rg_bx', 'rg_lambda', 's5_d', 's5_b_glu', 'ln1_g', 'ln1_b', 'ple_gate_b', 'ln2_g', 'ln2_b')


def _shared_weights(full):
    shared = {k: full[k] for k in ROWS}
    shared.update(conv_w=full['conv_w'], wa_bd=full['rg_wa'], wx_bd=full['rg_wx'], s5_d=full['s5_d'].reshape(DEPTH, 1, S5_W))
    return shared


def _layer_weights(full, shared, i):
    return dict(shared, w_in=full['w_in'][i])


class _AllLocal(_NoHooks):
    def __init__(self, full):
        self.full = full

    def late_weights(self, i, W, after):
        W[i].update({k: self.full[k][i] for k in LATE})
        return W


def _full_grads(full, x, p, target, hooks=None):
    hooks = hooks or _AllLocal(full)
    disc, res = _s5_layouts_fwd(full['s5_a_re'], full['s5_a_im'], full['s5_log_step'], full['s5_b_re'], full['s5_b_im'],
                                full['s5_c_re'], full['s5_c_im'], hooks.first_token)
    full = hooks.first_weights(full, disc[-1]['bb_im'])
    shared = _shared_weights(full)
    W = [_layer_weights(full, shared, i) for i in range(2)]
    hooks.res = res
    loss, gx, grads = _local_grads(x, p, target, W, disc, hooks)
    out = dict(hooks.small)
    for k in SHARD_AXIS:
        out[k] = [g[k] for g in grads]
    return loss, gx, out


def _small_grads(grads, res):
    stack = lambda f: jnp.stack([f(g) for g in grads])
    out = _s5_layouts_bwd(grads, res)
    out['conv_w'] = stack(lambda g: g['conv_w'])
    for k in ('conv_b', 'rg_ba', 'rg_bx', 'rg_lambda', 's5_b_glu', 'ln1_g', 'ln1_b', 'ple_gate_b', 'ln2_g', 'ln2_b'):
        out[k] = [g[k] for g in grads]
    out['s5_d'] = _stored('s5_d', stack(lambda g: g['s5_d'][0]).reshape(2, 24, 16))
    out['rg_wa'] = [g['wa_bd'] for g in grads]
    out['rg_wx'] = [g['wx_bd'] for g in grads]
    return out


SHARD_AXIS = {'w_in': 2, 'w_out': 1, 'ple_w': 2, 'ple_gate_w': 1, 's5_w_glu': 1}


def kernel(x, p, w_in, conv_w, conv_b, rg_wa, rg_ba, rg_wx, rg_bx, rg_lambda, s5_a_re, s5_a_im, s5_b_re, s5_b_im, s5_c_re, s5_c_im, s5_d, s5_log_step, s5_w_glu, s5_b_glu, w_out, ln1_g, ln1_b, ple_w, ple_gate_w, ple_gate_b, ln2_g, ln2_b, loss_target, m_w_in, m_conv_w, m_conv_b, m_rg_wa, m_rg_ba, m_rg_wx, m_rg_bx, m_rg_lambda, m_s5_a_re, m_s5_a_im, m_s5_b_re, m_s5_b_im, m_s5_c_re, m_s5_c_im, m_s5_d, m_s5_log_step, m_s5_w_glu, m_s5_b_glu, m_w_out, m_ln1_g, m_ln1_b, m_ple_w, m_ple_gate_w, m_ple_gate_b, m_ln2_g, m_ln2_b, v_w_in, v_conv_w, v_conv_b, v_rg_wa, v_rg_ba, v_rg_wx, v_rg_bx, v_rg_lambda, v_s5_a_re, v_s5_a_im, v_s5_b_re, v_s5_b_im, v_s5_c_re, v_s5_c_im, v_s5_d, v_s5_log_step, v_s5_w_glu, v_s5_b_glu, v_w_out, v_ln1_g, v_ln1_b, v_ple_w, v_ple_gate_w, v_ple_gate_b, v_ln2_g, v_ln2_b):
    local = dict(locals())
    w = {k: local[k] for k in WEIGHTS}
    mom = {k: local['m_' + k] for k in WEIGHTS}
    var = {k: local['v_' + k] for k in WEIGHTS}

    big = list(SHARD_AXIS)
    late_axes = [SHARD_AXIS[k] - 1 for k in LATE]
    pushed = {}

    groups = dict(first=(['w_in', 'conv_w'], [0, None], [1, 0]), l0=(list(LATE), [0] * len(LATE), late_axes),
                  l1=(['w_in'] + list(LATE), [1] * (1 + len(LATE)), [1] + late_axes))
    token = None
    for key, members in (("first", ["first"]), ("rest", ["l0", "l1"])):
        names, layers, axes = (sum((groups[m][j] for m in members), []) for j in range(3))
        shards = [w[k] if layer is not None else w[k][None] for k, layer in zip(names, layers)]
        lands = _place_shards(shards, layers, axes, [WIRE if k in big else w[k].dtype for k in names],
                              "place_weights_" + key, token)
        pushed[key] = _push_start("gather", [], lands, axes, "push_weights_" + key, "near" if key == "first" else None)
        token = pushed[key][4]

    def await_weights(key, axes, after):
        s, first = pushed["rest"], 0 if key == "l0" else len(LATE)
        return _push_wait("gather", s[0], s[1], [], s[3][first:first + len(axes)], axes, after, "await_weights_" + key, first)

    def push_grads(key, g, names, axes):
        srcs = [g[k] for k in names]
        pushed[key] = _push_start("scatter", srcs, _place_own("scatter", srcs, axes, "place_grads_" + key), axes,
                                  "push_grads_" + key)
        return pushed[key][4]

    def await_grads(key, axes, after):
        s = pushed[key]
        return _push_wait("scatter", s[0], s[1], s[2], s[3], axes, after, "await_grads_" + key)

    class Overlap(_NoHooks):
        token = pushed["rest"][4]
        first_token = token

        def first_weights(self, full, after):
            s, axes = pushed["first"], [1, 0]
            near = _push_wait("gather", s[0], s[1], [], s[3], axes, after, "await_weights_near", route="near")
            s = _push_start("gather", [], near, axes, "relay_weights", "relay")
            w_in0, conv = _push_wait("gather", s[0], s[1], [], s[3], axes, s[4], "await_weights_relay", route="relay")
            return dict(full, w_in=[w_in0, None], conv_w=jnp.moveaxis(conv, 0, 2).reshape(2, 4, RG_W))

        def late_weights(self, i, W, after):
            if i == 0:
                W[0].update(zip(LATE, await_weights("l0", late_axes, after)))
            return W

        def layer_start(self, i, W, after):
            lands = await_weights("l1", [1] + late_axes, after)
            W[1].update(zip(LATE, lands[1:]), w_in=lands[0])
            return W

        def post_done(self, i, g):
            return push_grads("late0", g, LATE, late_axes) if i == 0 else None

        def smalls_done(self, grads, loss_row):
            super().smalls_done(grads, loss_row)
            conv = jnp.moveaxis(self.small['conv_w'].reshape(2, 4, N_DEV, RG_W // N_DEV), 2, 0)
            self.packed = _pack(self.small, loss_row, 0.5 / D_MODEL)
            return push_grads("small", dict(conv_w=conv.reshape(N_DEV, 8, RG_W // N_DEV), small=self.packed),
                              ['conv_w', 'small'], [0, 0])

        def w_in_done(self, i, g):
            return push_grads("w_in0", g, ['w_in'], [0])

        def layer_done(self, i, g, dx):
            return push_grads("all1", g, ['w_in'] + list(LATE), [0] + late_axes) if i == 1 else None

    hooks = Overlap()
    _, grad_x, g = _full_grads(dict(w), x[0], p, loss_target[0], hooks)

    recv1 = dict(zip(['w_in'] + list(LATE), await_grads("all1", [0] + late_axes, grad_x)))
    recv0 = dict(zip(LATE, await_grads("late0", late_axes, grad_x)))
    outs = {}

    def update(k, parts, token=None):
        shard = w[k].shape
        c = shard[-1]
        two = lambda a: a.reshape(-1, c)
        res = _adamw([r.reshape(N_DEV, -1, c) for r in parts], two(w[k]), two(mom[k]), two(var[k]), token)
        outs[k] = [o.reshape(shard) for o in res]

    conv_parts, small_parts = await_grads("small", [0, 0], grad_x)
    rows = hooks.packed.shape[0] // N_DEV
    mine = _sum_parts(small_parts.reshape(N_DEV, rows, LANE))
    sums = _push_start("gather", [mine], _place_own("gather", [mine], [0], "place_small_sums"), [0], "push_small_sums")
    late = _adamw_sharded(list(LATE), [recv0, recv1], w, mom, var, "adamw_late", sums[4])
    for k in LATE:
        outs[k] = [t[k] for t in late]
    w_in0, = await_grads("w_in0", [0], [outs[k][1] for k in LATE])
    update('w_in', [w_in0, recv1['w_in']])
    update('conv_w', [conv_parts])
    gathered, = _push_wait("gather", sums[0], sums[1], sums[2], sums[3], [0], [outs['w_in'][1], outs['conv_w'][1]],
                           "await_small_sums")
    stored = [{k: _two_d(_stored(k, t[k])) for k in SMALL} for t in (w, mom, var)]
    *updated, loss = _adamw_packed(SMALL, gathered, *stored, "adamw_small")
    for k in SMALL:
        shape = _stored(k, w[k]).shape
        outs[k] = [_stored(k, o[k].reshape(shape)) for o in updated]

    res = [loss, grad_x[None]]
    for j in range(4):
        res += [outs[k][j] for k in WEIGHTS]
    return tuple(res)
```

```python
import math

import jax
import jax.numpy as jnp
from jax import lax
from jax.experimental import pallas as pl
from jax.experimental.pallas import tpu as pltpu

F32 = jnp.float32
MXU = jnp.bfloat16
WIRE = jnp.bfloat16

N_DEV = 8
D_MODEL = 1024
PLE_D = 256
RG_W = 640
S5_W = 384
S5_P = 64
S5_N = 24 * S5_P
Z_W = 2 * RG_W + 2 * S5_W
C_RGG = RG_W
C_S5U = 2 * RG_W
C_S5G = 2 * RG_W + S5_W
LANE = 128
N_RG_T = RG_W // LANE
N_S5_T = S5_W // LANE
W_BLK = Z_W // N_DEV
ALPHA = (2.0 * 2) ** 0.25
LN_EPS = 1e-5
RG_C = 8.0
LR, B1, B2, EPS, WD, STEP = 0.001, 0.9, 0.999, 1e-08, 0.01, 10
BC1 = 1.0 - B1 ** STEP
BC2 = 1.0 - B2 ** STEP
RC = 512
RC_RG = 1024
TM = 512
TM_MM = 1024
VMEM_LIMIT = 56 * 1024 * 1024

MESH = pl.DeviceIdType.MESH
ANY = pl.BlockSpec(memory_space=pl.ANY)


def _params(n_grid_axes, vmem=VMEM_LIMIT):
    return pltpu.CompilerParams(dimension_semantics=("arbitrary",) * n_grid_axes, vmem_limit_bytes=vmem)


def _S(shape, dtype=F32):
    return jax.ShapeDtypeStruct(tuple(shape), dtype)


def _sigmoid(x):
    return 0.5 * jnp.tanh(0.5 * x) + 0.5


def _silu_and_grad(x):
    s = _sigmoid(x)
    return x * s, s * (1.0 + x * (1.0 - s))


_GELU_C = math.sqrt(2.0 / math.pi)


def _gelu(x):
    return 0.5 * x * (1.0 + jnp.tanh(_GELU_C * (x + 0.044715 * (x * x * x))))


def _gelu_grad(x):
    th = jnp.tanh(_GELU_C * (x + 0.044715 * (x * x * x)))
    return 0.5 * (1.0 + th) + 0.5 * x * (1.0 - th * th) * (_GELU_C * (1.0 + 3.0 * 0.044715 * (x * x)))


def _mm(a, b):
    return jnp.dot(a.astype(MXU), b.astype(MXU), preferred_element_type=F32)


def _mm_nt(a, b):
    return lax.dot_general(a.astype(MXU), b.astype(MXU), (((1,), (1,)), ((), ())), preferred_element_type=F32)


def _mm_tn(a, b):
    return lax.dot_general(a.astype(MXU), b.astype(MXU), (((0,), (0,)), ((), ())), preferred_element_type=F32)


def _ln_fwd(t, g, b):
    mu = jnp.mean(t, axis=-1, keepdims=True)
    tc = t - mu
    var = jnp.mean(tc * tc, axis=-1, keepdims=True)
    rstd = lax.rsqrt(var + LN_EPS)
    xhat = tc * rstd
    return xhat * g + b, xhat, rstd


def _ln_bwd(dy, xhat, rstd, g):
    dxh = dy * g
    m1 = jnp.mean(dxh, axis=-1, keepdims=True)
    m2 = jnp.mean(dxh * xhat, axis=-1, keepdims=True)
    return rstd * (dxh - m1 - xhat * m2)


def _colsum(a):
    return jnp.sum(a, axis=0, keepdims=True)


def _up(x, d, rows, fill):
    n = x.shape[0]
    return jnp.where(rows < n - d, pltpu.roll(x, n - d, 0), fill)


SUB = 8
TILE_STEPS = (1, 2, 4)


def _r8(width):
    return lax.broadcasted_iota(jnp.int32, (SUB, width), 0)


def _scan_real(a, u, carry, reverse=False):
    r8 = _r8(a.shape[1])
    n = a.shape[0] // SUB
    outs = [None] * n
    for k in (reversed(range(n)) if reverse else range(n)):
        A, U = a[SUB * k:SUB * k + SUB], u[SUB * k:SUB * k + SUB]
        for d in TILE_STEPS:
            m = (r8 < SUB - d) if reverse else (r8 >= d)
            sh = SUB - d if reverse else d
            U = A * jnp.where(m, pltpu.roll(U, sh, 0), 0.0) + U
            A = A * jnp.where(m, pltpu.roll(A, sh, 0), 1.0)
        h = A * carry + U
        outs[k] = h
        carry = h[0:1] if reverse else h[SUB - 1:SUB]
    return jnp.concatenate(outs, axis=0), carry


def _tile_powers(lr, li, reverse=False):
    width = lr.shape[1]
    r8 = _r8(width)
    steps = []
    pr, pi = lr, li
    er, ei = jnp.broadcast_to(lr, (SUB, width)), jnp.broadcast_to(li, (SUB, width))
    for d in TILE_STEPS:
        m = (r8 < SUB - d) if reverse else (r8 >= d)
        sh = SUB - d if reverse else d
        steps.append((sh, jnp.where(m, pr, 0.0), jnp.where(m, pi, 0.0)))
        er, ei = _cmul(er, ei, jnp.where(m, pltpu.roll(er, sh, 0), 1.0), jnp.where(m, pltpu.roll(ei, sh, 0), 0.0))
        pr, pi = _cmul(pr, pi, pr, pi)
    return steps, (er, ei)


def _scan_lti(xr, xi, carry, steps, e, reverse=False):
    er, ei = e
    kr, ki = carry
    n = xr.shape[0] // SUB
    outr, outi = [None] * n, [None] * n
    for k in (reversed(range(n)) if reverse else range(n)):
        sr, si = xr[SUB * k:SUB * k + SUB], xi[SUB * k:SUB * k + SUB]
        for sh, pr, pi in steps:
            shr, shi = pltpu.roll(sr, sh, 0), pltpu.roll(si, sh, 0)
            sr, si = sr + (pr * shr - pi * shi), si + (pr * shi + pi * shr)
        sr = sr + (er * kr - ei * ki)
        si = si + (er * ki + ei * kr)
        outr[k], outi[k] = sr, si
        kr, ki = (sr[0:1], si[0:1]) if reverse else (sr[SUB - 1:SUB], si[SUB - 1:SUB])
    return jnp.concatenate(outr, axis=0), jnp.concatenate(outi, axis=0), (kr, ki)


def _halo(ref, c, r0):
    rp = pl.multiple_of(jnp.maximum(r0 - 8, 0), 8)
    return jnp.where(c > 0, ref[pl.ds(rp, 8), :], 0.0)


def _conv_taps(xe):
    return [pltpu.roll(xe, 3, 0)[8:, :], pltpu.roll(xe, 2, 0)[8:, :], pltpu.roll(xe, 1, 0)[8:, :], xe[8:, :]]


def _rg_gates(h, wa, wx, ba, bx, sp):
    r = _sigmoid(_mm(h, wa) + ba)
    i = _sigmoid(_mm(h, wx) + bx)
    log_a = (-RG_C) * r * sp
    a = jnp.exp(log_a)
    mult = jnp.sqrt(-jnp.tanh(log_a) * (a * a + 1.0))
    return r, i, a, mult


def _softplus(y):
    return jnp.maximum(y, 0.0) + jnp.log1p(jnp.exp(-jnp.abs(y)))


def _after(token):
    return ([], []) if token is None else ([token], [ANY])


def _inproj_fwd(x, w_in, token=None):
    L = x.shape[0]

    def body(x_ref, w_ref, *rest):
        rest[-1][...] = _mm(x_ref[...], w_ref[...])

    extra, extra_specs = _after(token)
    tm = min(TM_MM, L)
    return pl.pallas_call(
        body, name="inproj_fwd", grid=(L // tm,),
        in_specs=[pl.BlockSpec((tm, D_MODEL), lambda i: (i, 0)), pl.BlockSpec((D_MODEL, Z_W), lambda i: (0, 0))] + extra_specs,
        out_specs=pl.BlockSpec((tm, Z_W), lambda i: (i, 0)),
        out_shape=_S((L, Z_W)), compiler_params=_params(1))(x, w_in, *extra)


def _inproj_bwd(dt1, x, dzx, dzg, dzu, w_in):
    L = x.shape[0]

    def body(dt1_ref, x_ref, dzx_ref, dzg_ref, dzu_ref, w_ref, dx_ref, dw_ref, acc_ref):
        @pl.when(pl.program_id(0) == 0)
        def _():
            acc_ref[...] = jnp.zeros_like(acc_ref)
        dzg = dzg_ref[...]
        dz = jnp.concatenate([dzx_ref[...], dzg[:, :RG_W], dzu_ref[...], dzg[:, RG_W:]], axis=1).astype(MXU)
        xb = x_ref[...].astype(MXU)
        dx_ref[...] = ALPHA * dt1_ref[...] + _mm_nt(dz, w_ref[...])
        for j in range(N_DEV):
            acc_ref[j] += _mm_tn(xb, dz[:, j * W_BLK:(j + 1) * W_BLK])

        @pl.when(pl.program_id(0) == L // TM - 1)
        def _():
            dw_ref[...] = acc_ref[...].astype(WIRE)

    row = lambda w: pl.BlockSpec((TM, w), lambda i: (i, 0))
    wspec = pl.BlockSpec((N_DEV, D_MODEL, W_BLK), lambda i: (0, 0, 0))
    return pl.pallas_call(
        body, name="inproj_bwd", grid=(L // TM,),
        in_specs=[row(D_MODEL), row(D_MODEL), row(RG_W), row(D_MODEL), row(S5_W),
                  pl.BlockSpec((D_MODEL, Z_W), lambda i: (0, 0))],
        out_specs=[row(D_MODEL), wspec],
        out_shape=[_S((L, D_MODEL)), _S((N_DEV, D_MODEL, W_BLK), WIRE)],
        scratch_shapes=[pltpu.VMEM((N_DEV, D_MODEL, W_BLK), F32)],
        compiler_params=_params(1))(dt1, x, dzx, dzg, dzu, w_in)


TM2 = 1024


def _dz_block(dzx_ref, dzg_ref, dzu_ref):
    dzg = dzg_ref[...]
    return jnp.concatenate([dzx_ref[...], dzg[:, :RG_W], dzu_ref[...], dzg[:, RG_W:]], axis=1).astype(MXU)


def _inproj_bwd_dw(x, dzx, dzg, dzu, token=None):
    L = x.shape[0]
    extra, extra_specs = _after(token)

    def body(x_ref, dzx_ref, dzg_ref, dzu_ref, *rest):
        dw_ref, acc_ref = rest[len(extra):]
        @pl.when(pl.program_id(0) == 0)
        def _():
            acc_ref[...] = jnp.zeros_like(acc_ref)
        dz = _dz_block(dzx_ref, dzg_ref, dzu_ref)
        xb = x_ref[...].astype(MXU)
        for j in range(N_DEV):
            acc_ref[j] += _mm_tn(xb, dz[:, j * W_BLK:(j + 1) * W_BLK])

        @pl.when(pl.program_id(0) == L // TM2 - 1)
        def _():
            dw_ref[...] = acc_ref[...].astype(WIRE)

    row = lambda w: pl.BlockSpec((TM2, w), lambda i: (i, 0))
    wspec = pl.BlockSpec((N_DEV, D_MODEL, W_BLK), lambda i: (0, 0, 0))
    return pl.pallas_call(
        body, name="inproj_bwd_dw", grid=(L // TM2,),
        in_specs=[row(D_MODEL), row(RG_W), row(D_MODEL), row(S5_W)] + extra_specs, out_specs=wspec,
        out_shape=_S((N_DEV, D_MODEL, W_BLK), WIRE), scratch_shapes=[pltpu.VMEM((N_DEV, D_MODEL, W_BLK), F32)],
        compiler_params=_params(1))(x, dzx, dzg, dzu, *extra)


def _inproj_bwd_dx(dt1, dzx, dzg, dzu, w_in, token=None):
    L = dt1.shape[0]
    extra, extra_specs = _after(token)

    def body(dt1_ref, dzx_ref, dzg_ref, dzu_ref, w_ref, *rest):
        rest[-1][...] = ALPHA * dt1_ref[...] + _mm_nt(_dz_block(dzx_ref, dzg_ref, dzu_ref), w_ref[...])

    tm = min(TM_MM, L)
    row = lambda w: pl.BlockSpec((tm, w), lambda i: (i, 0))
    return pl.pallas_call(
        body, name="inproj_bwd_dx", grid=(L // tm,),
        in_specs=[row(D_MODEL), row(RG_W), row(D_MODEL), row(S5_W), _full((D_MODEL, Z_W))] + extra_specs,
        out_specs=row(D_MODEL), out_shape=_S((L, D_MODEL)), compiler_params=_params(1))(dt1, dzx, dzg, dzu, w_in, *extra)


def _rg_specs(layer):
    tile = lambda rows: pl.BlockSpec((rows, LANE), lambda c: (0, c))
    ptile = lambda rows: pl.BlockSpec((None, rows, LANE), lambda c: (layer, 0, c))
    pheads = pl.BlockSpec((None, 2, RG_HD, RG_HD), lambda c: (layer, c, 0, 0))
    return tile, ptile, pheads, pl.BlockSpec((2, RG_HD, RG_HD), lambda c: (c, 0, 0))


RG_HD = 64


def _bd2(w):
    z = jnp.zeros((RG_HD, RG_HD), w.dtype)
    return jnp.concatenate([jnp.concatenate([w[0], z], axis=1), jnp.concatenate([z, w[1]], axis=1)], axis=0)


def _bd2_diag(m):
    return jnp.stack([m[:RG_HD, :RG_HD], m[RG_HD:, RG_HD:]])


def _rg_fwd(z, cw, cb, wa_bd, wx_bd, ba, bx, lam, layer):
    L = z.shape[0]
    RC = min(RC_RG, L)

    def body(x_ref, cw_ref, cb_ref, wa_ref, wx_ref, ba_ref, bx_ref, lam_ref, hs_ref, *saved):
        row = slice(layer, layer + 1)
        w, b = cw_ref[...], cb_ref[row, :]
        wa, wx, ba_, bx_ = _bd2(wa_ref[...]).astype(MXU), _bd2(wx_ref[...]).astype(MXU), ba_ref[row, :], bx_ref[row, :]
        sp = _softplus(-lam_ref[row, :])

        def step(c, carry):
            r0 = pl.multiple_of(c * RC, RC)
            xe = jnp.concatenate([_halo(x_ref, c, r0), x_ref[pl.ds(r0, RC), :]], axis=0)
            t = _conv_taps(xe)
            h = t[0] * w[0:1] + t[1] * w[1:2] + t[2] * w[2:3] + t[3] * w[3:4] + b
            r, i, a, mult = _rg_gates(h, wa, wx, ba_, bx_, sp)
            hs, carry = _scan_real(a, mult * (i * h), carry)
            hs_ref[pl.ds(r0, RC), :] = hs
            for ref, val in zip(saved, (h, r, i, a, mult)):
                ref[pl.ds(r0, RC), :] = val
            return carry

        lax.fori_loop(0, L // RC, step, jnp.zeros((1, LANE), F32))

    tile, ptile, pheads, _ = _rg_specs(layer)
    return pl.pallas_call(
        body, name="rg_fwd", grid=(N_RG_T,),
        in_specs=[tile(L), ptile(4), tile(2), pheads, pheads, tile(2), tile(2), tile(2)],
        out_specs=[tile(L)] * 6, out_shape=[_S((L, RG_W))] * 6, compiler_params=_params(1))(
            z, cw, cb, wa_bd, wx_bd, ba, bx, lam)


def _rg_bwd(dhs, z, hs, gates, cw, wa_bd, wx_bd, lam, layer):
    L = z.shape[0]
    RC = min(RC_RG, L)

    def body(g_ref, x_ref, hs_ref, h_ref, r_ref, i_ref, a_ref, mult_ref, cw_ref, wa_ref, wx_ref, lam_ref,
             dx_ref, dcw_ref, dcb_ref, dwa_out, dwx_out, dba_ref, dbx_ref, dlam_ref, dwa_ref, dwx_ref):
        w = cw_ref[...]
        wa, wx = _bd2(wa_ref[...]).astype(MXU), _bd2(wx_ref[...]).astype(MXU)
        lam = lam_ref[layer:layer + 1, :]
        sp = _softplus(-lam)
        rows = lax.broadcasted_iota(jnp.int32, (RC, LANE), 0)
        for ref in (dcw_ref, dcb_ref, dwa_ref, dwx_ref, dba_ref, dbx_ref, dlam_ref):
            ref[...] = jnp.zeros_like(ref)
        nch = L // RC

        def step(k, carry):
            cin, nxt = carry
            c = nch - 1 - k
            r0 = pl.multiple_of(c * RC, RC)
            xe = jnp.concatenate([_halo(x_ref, c, r0), x_ref[pl.ds(r0, RC), :]], axis=0)
            t = _conv_taps(xe)
            h, r, i, a, mult = (ref[pl.ds(r0, RC), :] for ref in (h_ref, r_ref, i_ref, a_ref, mult_ref))
            hs_e = jnp.concatenate([_halo(hs_ref, c, r0), hs_ref[pl.ds(r0, RC), :]], axis=0)
            hs_prev = pltpu.roll(hs_e, 1, 0)[8:, :]
            g = g_ref[pl.ds(r0, RC), :]
            cc, cin_new = _scan_real(a, a * g, cin, reverse=True)
            dh = g + _up(cc, 1, rows, cin)
            ih = i * h
            dlog_a = dh * hs_prev * a - (dh * ih) * (a * a) / mult
            di = dh * mult * h
            dhin = dh * mult * i
            dr = dlog_a * ((-RG_C) * sp)
            dlam_ref[...] += _colsum(dlog_a * r)
            dra = dr * r * (1.0 - r)
            dia = di * i * (1.0 - i)
            dwa_ref[...] += _mm_tn(h, dra)
            dwx_ref[...] += _mm_tn(h, dia)
            dba_ref[...] += _colsum(dra)
            dbx_ref[...] += _colsum(dia)
            dhin = dhin + _mm_nt(dra, wa) + _mm_nt(dia, wx)
            de = jnp.concatenate([dhin, nxt], axis=0)
            n = RC + 8
            dx = (dhin * w[3:4] + pltpu.roll(de, n - 1, 0)[:RC, :] * w[2:3]
                  + pltpu.roll(de, n - 2, 0)[:RC, :] * w[1:2] + pltpu.roll(de, n - 3, 0)[:RC, :] * w[0:1])
            dx_ref[pl.ds(r0, RC), :] = dx
            for kk in range(4):
                dcw_ref[kk:kk + 1, :] += _colsum(dhin * t[kk])
            dcb_ref[...] += _colsum(dhin)
            return cin_new, dhin[0:8, :]

        lax.fori_loop(0, nch, step, (jnp.zeros((1, LANE), F32), jnp.zeros((8, LANE), F32)))
        dlam_ref[...] = dlam_ref[...] * (RG_C * _sigmoid(-lam))
        dwa_out[...], dwx_out[...] = _bd2_diag(dwa_ref[...]), _bd2_diag(dwx_ref[...])

    tile, ptile, pheads, gheads = _rg_specs(layer)
    heads = _S((2 * N_RG_T, RG_HD, RG_HD))
    return pl.pallas_call(
        body, name="rg_bwd", grid=(N_RG_T,),
        in_specs=[tile(L)] * 8 + [ptile(4), pheads, pheads, tile(2)],
        out_specs=[tile(L), tile(4), tile(1), gheads, gheads, tile(1), tile(1), tile(1)],
        out_shape=[_S((L, RG_W)), _S((4, RG_W)), _S((1, RG_W)), heads, heads, _S((1, RG_W)), _S((1, RG_W)), _S((1, RG_W))],
        scratch_shapes=[pltpu.VMEM((LANE, LANE), F32), pltpu.VMEM((LANE, LANE), F32)],
        compiler_params=_params(1))(dhs, z, hs, *gates, cw, wa_bd, wx_bd, lam)


def _cmul(ar, ai, br, bi):
    return ar * br - ai * bi, ar * bi + ai * br


S5_TW = S5_N // N_S5_T


S5_H = 16
S5_GT = LANE // S5_H


def _s5_specs(L, layer):
    in_tile = pl.BlockSpec((L, LANE), lambda t: (0, t))
    st = pl.BlockSpec((L, S5_TW), lambda t: (0, t))
    pg = pl.BlockSpec((None, S5_GT, S5_H, S5_P), lambda t: (layer * N_S5_T + t, 0, 0, 0))
    plb = pl.BlockSpec((None, S5_GT, S5_P), lambda t: (layer * N_S5_T + t, 0, 0))
    gg = pl.BlockSpec((None, S5_GT, S5_H, S5_P), lambda t: (t, 0, 0, 0))
    glb = pl.BlockSpec((None, S5_GT, S5_P), lambda t: (t, 0, 0))
    dv = pl.BlockSpec((1, LANE), lambda t: (0, t))
    return in_tile, st, pg, plb, gg, glb, dv


def _bd8(blocks):
    rows = []
    for g in range(S5_GT):
        pieces = [blocks[g]]
        if g:
            pieces.insert(0, jnp.zeros((S5_H, S5_P * g), blocks.dtype))
        if g < S5_GT - 1:
            pieces.append(jnp.zeros((S5_H, S5_P * (S5_GT - 1 - g)), blocks.dtype))
        rows.append(jnp.concatenate(pieces, axis=1))
    return jnp.concatenate(rows, axis=0)


def _bd8_diag(m):
    return jnp.stack([m[S5_H * g:S5_H * (g + 1), S5_P * g:S5_P * (g + 1)] for g in range(S5_GT)])


def _row8(v):
    return jnp.concatenate([v[g:g + 1] for g in range(S5_GT)], axis=1)


def _row8_split(r):
    return jnp.concatenate([r[:, S5_P * g:S5_P * (g + 1)] for g in range(S5_GT)], axis=0)


def _layer_row_tile(layer):
    return pl.BlockSpec((None, 1, LANE), lambda t: (layer, 0, t))


def _s5_fwd(z, bb_re, bb_im, lb_re, lb_im, c_re, c_im, dvec, layer):
    L = z.shape[0]

    span = RC // 2
    seg = span // SUB

    def body(u_ref, bbr_ref, bbi_ref, lr_ref, li_ref, cr_ref, ci_ref, d_ref, y_ref, sr_ref, si_ref, pr_ref, pi_ref,
             tr_ref, ti_ref):
        bbr, bbi = _bd8(bbr_ref[...]).astype(MXU), _bd8(bbi_ref[...]).astype(MXU)
        cr, ci = _bd8(cr_ref[...]).astype(MXU), _bd8(ci_ref[...]).astype(MXU)
        dv = d_ref[...]
        lr = jnp.broadcast_to(_row8(lr_ref[...]), (SUB, S5_TW))
        li = jnp.broadcast_to(_row8(li_ref[...]), (SUB, S5_TW))
        _, (qr, qi) = _tile_powers(lr[0:1], li[0:1])
        e8r, e8i = qr[SUB - 1:SUB], qi[SUB - 1:SUB]
        for i0 in range(seg // SUB):
            pr_ref[SUB * i0:SUB * i0 + SUB, :], pi_ref[SUB * i0:SUB * i0 + SUB, :] = qr, qi
            qr, qi = _cmul(qr, qi, e8r, e8i)

        at = lambda r0, g, i: pl.ds(r0 + g * span + i, SUB, stride=seg)
        tiles = [(g, i) for g in range(RC // span) for i in range(seg)]
        row = lambda g, i: g * span + SUB * i

        def step(c, carry):
            r0 = pl.multiple_of(c * RC, RC)
            u = jnp.concatenate([u_ref[at(r0, g, i), :] for g, i in tiles], axis=0)
            ub = u.astype(MXU)
            xr = jnp.dot(ub, bbr, preferred_element_type=F32)
            xi = jnp.dot(ub, bbi, preferred_element_type=F32)
            zero8 = jnp.zeros((SUB, S5_TW), F32)
            last = {g: (zero8, zero8) for g in range(RC // span)}
            for i in range(seg):
                for g in last:
                    sr, si = last[g]
                    x_r, x_i = xr[row(g, i):row(g, i) + SUB], xi[row(g, i):row(g, i) + SUB]
                    sr, si = last[g] = (lr * sr - li * si) + x_r, (lr * si + li * sr) + x_i
                    for k in range(S5_TW // LANE):
                        tr_ref[k, row(g, i):row(g, i) + SUB, :] = sr[:, LANE * k:LANE * (k + 1)]
                        ti_ref[k, row(g, i):row(g, i) + SUB, :] = si[:, LANE * k:LANE * (k + 1)]
            er, ei = pr_ref[seg - 1:seg, :], pi_ref[seg - 1:seg, :]
            kr, ki = carry
            outr, outi = [], []
            for g in last:
                sr, si = last[g]
                for j in range(SUB):
                    for i0 in range(seg // SUB):
                        rows = pl.ds(g * span + SUB * SUB * i0 + j, SUB, stride=SUB)
                        ar = jnp.concatenate([tr_ref[k, rows, :] for k in range(S5_TW // LANE)], axis=1)
                        ai = jnp.concatenate([ti_ref[k, rows, :] for k in range(S5_TW // LANE)], axis=1)
                        pr, pi = pr_ref[SUB * i0:SUB * i0 + SUB, :], pi_ref[SUB * i0:SUB * i0 + SUB, :]
                        outr.append(ar + (pr * kr - pi * ki))
                        outi.append(ai + (pr * ki + pi * kr))
                    kr, ki = sr[j:j + 1] + (er * kr - ei * ki), si[j:j + 1] + (er * ki + ei * kr)
            s_r, s_i = jnp.concatenate(outr, axis=0), jnp.concatenate(outi, axis=0)
            sr_ref[pl.ds(r0, RC), :] = s_r
            si_ref[pl.ds(r0, RC), :] = s_i
            y_ref[pl.ds(r0, RC), :] = dv * u_ref[pl.ds(r0, RC), :] + (_mm_nt(s_r, cr) - _mm_nt(s_i, ci))
            return kr, ki

        zero = jnp.zeros((1, S5_TW), F32)
        lax.fori_loop(0, L // RC, step, (zero, zero))

    in_tile, st, pg, plb, _, _, _ = _s5_specs(L, layer)
    u_tile = pl.BlockSpec((L, LANE), lambda t: (0, C_S5U // LANE + t))
    return pl.pallas_call(
        body, name="s5_fwd", grid=(N_S5_T,),
        in_specs=[u_tile, pg, pg, plb, plb, pg, pg, _layer_row_tile(layer)],
        out_specs=[in_tile, st, st],
        out_shape=[_S((L, S5_W)), _S((L, S5_N)), _S((L, S5_N))],
        scratch_shapes=[pltpu.VMEM((RC, S5_TW), F32), pltpu.VMEM((RC, S5_TW), F32),
                        pltpu.VMEM((S5_TW // LANE, RC, LANE), F32), pltpu.VMEM((S5_TW // LANE, RC, LANE), F32)],
        compiler_params=_params(1))(z, bb_re, bb_im, lb_re, lb_im, c_re, c_im, dvec)


def _s5_bwd(dy0, z, s_re, s_im, bb_re, bb_im, lb_re, lb_im, c_re, c_im, dvec, layer, token=None):
    L = z.shape[0]
    extra, extra_specs = _after(token)

    def body(dy_ref, u_ref, sr_ref, si_ref, bbr_ref, bbi_ref, lr_ref, li_ref, cr_ref, ci_ref, d_ref, *rest):
        (du_ref, dbbr_out, dbbi_out, dlr_out, dli_out, dcr_out, dci_out, dd_ref,
         dbbr_ref, dbbi_ref, dcr_ref, dci_ref, dlr_ref, dli_ref) = rest[len(extra):]
        bbr, bbi = _bd8(bbr_ref[...]).astype(MXU), _bd8(bbi_ref[...]).astype(MXU)
        cr, ci = _bd8(cr_ref[...]).astype(MXU), _bd8(ci_ref[...]).astype(MXU)
        lr, li = _row8(lr_ref[...]), -_row8(li_ref[...])
        dv = d_ref[...]
        steps, e = _tile_powers(lr, li, reverse=True)
        for ref in (dbbr_ref, dbbi_ref, dlr_ref, dli_ref, dcr_ref, dci_ref, dd_ref):
            ref[...] = jnp.zeros_like(ref)
        nch = L // RC

        def step(k, carry):
            c = nch - 1 - k
            r0 = pl.multiple_of(c * RC, RC)
            dy = dy_ref[pl.ds(r0, RC), :]
            u = u_ref[pl.ds(r0, RC), :]
            dyb, ub = dy.astype(MXU), u.astype(MXU)
            sr, si = sr_ref[pl.ds(r0, RC), :], si_ref[pl.ds(r0, RC), :]
            dcr_ref[...] += _mm_tn(dyb, sr)
            dci_ref[...] -= _mm_tn(dyb, si)
            gr = jnp.dot(dyb, cr, preferred_element_type=F32)
            gi = -jnp.dot(dyb, ci, preferred_element_type=F32)
            gr, gi, carry = _scan_lti(gr, gi, carry, steps, e, reverse=True)
            pr_ = pltpu.roll(jnp.concatenate([_halo(sr_ref, c, r0), sr], axis=0), 1, 0)[8:, :]
            pi_ = pltpu.roll(jnp.concatenate([_halo(si_ref, c, r0), si], axis=0), 1, 0)[8:, :]
            dlr_ref[...] += _colsum(pr_ * gr + pi_ * gi)
            dli_ref[...] += _colsum(pr_ * gi - pi_ * gr)
            grb, gib = gr.astype(MXU), gi.astype(MXU)
            dbbr_ref[...] += _mm_tn(ub, grb)
            dbbi_ref[...] += _mm_tn(ub, gib)
            du_ref[pl.ds(r0, RC), :] = dv * dy + (_mm_nt(grb, bbr) + _mm_nt(gib, bbi))
            dd_ref[...] += _colsum(dy * u)
            return carry

        zero = jnp.zeros((1, S5_TW), F32)
        lax.fori_loop(0, nch, step, (zero, zero))
        dbbr_out[...], dbbi_out[...] = _bd8_diag(dbbr_ref[...]), _bd8_diag(dbbi_ref[...])
        dcr_out[...], dci_out[...] = _bd8_diag(dcr_ref[...]), _bd8_diag(dci_ref[...])
        dlr_out[...], dli_out[...] = _row8_split(dlr_ref[...]), _row8_split(dli_ref[...])

    in_tile, st, pg, plb, gg, glb, dv = _s5_specs(L, layer)
    u_tile = pl.BlockSpec((L, LANE), lambda t: (0, C_S5U // LANE + t))
    groups, rows = _S((N_S5_T, S5_GT, S5_H, S5_P)), _S((N_S5_T, S5_GT, S5_P))
    wide = pltpu.VMEM((LANE, S5_TW), F32)
    return pl.pallas_call(
        body, name="s5_bwd", grid=(N_S5_T,),
        in_specs=[in_tile, u_tile, st, st, pg, pg, plb, plb, pg, pg, _layer_row_tile(layer)] + extra_specs,
        out_specs=[in_tile, gg, gg, glb, glb, gg, gg, dv],
        out_shape=[_S((L, S5_W)), groups, groups, rows, rows, groups, groups, _S((1, S5_W))],
        scratch_shapes=[wide, wide, wide, wide, pltpu.VMEM((1, S5_TW), F32), pltpu.VMEM((1, S5_TW), F32)],
        compiler_params=_params(1))(dy0, z, s_re, s_im, bb_re, bb_im, lb_re, lb_im, c_re, c_im, dvec, *extra)


def _disc(ar, ai, ls):
    dt = jnp.exp(ls)
    mag = jnp.exp(ar * dt)
    lr = mag * jnp.cos(ai * dt)
    li = mag * jnp.sin(ai * dt)
    den = ar * ar + ai * ai
    cr = ((lr - 1.0) * ar + li * ai) / den
    ci = (li * ar - (lr - 1.0) * ai) / den
    return lr, li, cr, ci


def _s5_disc_fwd(ar, ai, ls, token=None):
    extra, extra_specs = _after(token)

    def body(ar_ref, ai_ref, ls_ref, *rest):
        lr_ref, li_ref, cr_ref, ci_ref = rest[len(extra):]
        lr, li, cr, ci = _disc(ar_ref[...], ai_ref[...], ls_ref[...])
        lr_ref[...], li_ref[...], cr_ref[...], ci_ref[...] = lr, li, cr, ci

    sh = _S(ar.shape)
    vm = pl.BlockSpec(memory_space=pltpu.VMEM)
    return pl.pallas_call(body, name="s5_disc_fwd", in_specs=[vm, vm, vm] + extra_specs, out_shape=[sh, sh, sh, sh])(
        ar, ai, ls, *extra)


def _layers(refs):
    return jnp.concatenate([r[...] for r in refs], axis=0)


def _s5_disc_bwd(ar, ai, ls, dlr, dli, dcr, dci):
    n = len(dlr)

    def body(ar_ref, ai_ref, ls_ref, *rest):
        dcr_ref, dci_ref, dar_ref, dai_ref, dls_ref = rest[2 * n:]
        _, vjp = jax.vjp(_disc, ar_ref[...], ai_ref[...], jnp.broadcast_to(ls_ref[...], ar_ref.shape))
        dar, dai, dls = vjp((_layers(rest[:n]), _layers(rest[n:2 * n]), dcr_ref[...], dci_ref[...]))
        dar_ref[...], dai_ref[...] = dar, dai
        dls_ref[...] = jnp.sum(dls, axis=1, keepdims=True)

    return pl.pallas_call(body, name="s5_disc_bwd", out_shape=[_S(ar.shape), _S(ar.shape), _S(ls.shape)])(
        ar, ai, ls, *dlr, *dli, dcr, dci)


def _s5_bscale_fwd(cr, ci, br, bi):
    def body(cr_ref, ci_ref, br_ref, bi_ref, or_ref, oi_ref):
        or_ref[...], oi_ref[...] = _cmul(cr_ref[...], ci_ref[...], br_ref[...], bi_ref[...])

    return pl.pallas_call(body, name="s5_bscale_fwd", out_shape=[_S(br.shape), _S(br.shape)])(cr, ci, br, bi)


def _s5_bscale_bwd(cr, ci, br, bi, gr, gi):
    n = len(gr)

    def body(cr_ref, ci_ref, br_ref, bi_ref, *rest):
        dbr_ref, dbi_ref, dcr_ref, dci_ref = rest[2 * n:]
        cr_, ci_, br_, bi_ = (r[...] for r in (cr_ref, ci_ref, br_ref, bi_ref))
        gr_, gi_ = _layers(rest[:n]), _layers(rest[n:2 * n])
        dbr_ref[...] = cr_ * gr_ + ci_ * gi_
        dbi_ref[...] = cr_ * gi_ - ci_ * gr_
        dcr_ref[...] = jnp.sum(gr_ * br_ + gi_ * bi_, axis=1, keepdims=True)
        dci_ref[...] = jnp.sum(gi_ * br_ - gr_ * bi_, axis=1, keepdims=True)

    return pl.pallas_call(body, name="s5_bscale_bwd",
                          out_shape=[_S(br.shape), _S(br.shape), _S(cr.shape), _S(cr.shape)])(cr, ci, br, bi, *gr, *gi)


def _row(w):
    return pl.BlockSpec((TM, w), lambda i: (i, 0))


def _full(shape):
    return pl.BlockSpec(tuple(shape), lambda i: (0,) * len(shape))


def _gate_rows():
    return [pl.BlockSpec((TM, RG_W), lambda i: (i, C_RGG // RG_W))] + [
        pl.BlockSpec((TM, LANE), lambda i, k=k: (i, C_S5G // LANE + k)) for k in range(N_S5_T)]


def _p_rows(layer):
    return pl.BlockSpec((None, None, TM, PLE_D), lambda i: (layer, 0, i, 0))


DEPTH = 2


def _lrow(layer, width):
    return _full((DEPTH, width))


def _pick(ref, layer):
    return ref[layer:layer + 1, :]


def _post_fwd(x, hs, z, y0, p, w_glu, b_glu, w_out, g1, b1, ple_w, w_pg, b_pg, g2, b2, layer):
    L = x.shape[0]

    def body(x_ref, hs_ref, zg_ref, zs0_ref, zs1_ref, zs2_ref, y0_ref, p_ref, wg_ref, bg_ref, wo_ref, g1_ref, b1_ref, pw_ref,
             wpg_ref, bpg_ref, g2_ref, b2_ref, x2_ref, xh1_ref, xh2_ref, gt_ref, rstd1_ref, rstd2_ref):
        rg_gate = zg_ref[...]
        s5_gate = jnp.concatenate([zs0_ref[...], zs1_ref[...], zs2_ref[...]], axis=1)
        rg_y = hs_ref[...] * _silu_and_grad(rg_gate)[0]
        y1 = _gelu(y0_ref[...])
        gl = _sigmoid(_mm(y1, wg_ref[...]) + _pick(bg_ref, layer))
        s5_y = (y1 * gl) * _silu_and_grad(s5_gate)[0]
        mix = _mm(jnp.concatenate([rg_y.astype(MXU), s5_y.astype(MXU)], axis=1), wo_ref[...])
        t1 = ALPHA * x_ref[...] + mix
        x1, xh1, rstd1 = _ln_fwd(t1, _pick(g1_ref, layer), _pick(b1_ref, layer))
        q = _mm(p_ref[...], pw_ref[...])
        gt = _sigmoid(_mm(x1, wpg_ref[...]) + _pick(bpg_ref, layer))
        t2 = ALPHA * x1 + q * gt
        x2, xh2, rstd2 = _ln_fwd(t2, _pick(g2_ref, layer), _pick(b2_ref, layer))
        x2_ref[...], xh1_ref[...], xh2_ref[...], gt_ref[...] = x2, xh1, xh2, gt
        rstd1_ref[...], rstd2_ref[...] = rstd1, rstd2

    vec = _lrow(layer, D_MODEL)
    return pl.pallas_call(
        body, name="post_fwd", grid=(L // TM,),
        in_specs=[_row(D_MODEL), _row(RG_W), *_gate_rows(), _row(S5_W), _p_rows(layer), _full((S5_W, S5_W)),
                  _lrow(layer, S5_W), _full((D_MODEL, D_MODEL)), vec, vec, _full((PLE_D, D_MODEL)), _full((D_MODEL, D_MODEL)),
                  vec, vec, vec],
        out_specs=[_row(D_MODEL)] * 4 + [_row(1)] * 2, out_shape=[_S((L, D_MODEL))] * 4 + [_S((L, 1))] * 2,
        compiler_params=_params(1))(x, hs, z, z, z, z, y0, p, w_glu, b_glu, w_out, g1, b1, ple_w, w_pg, b_pg, g2, b2)


def _post_bwd_a(dx2_or_target, is_top, xh2, xh1, rstd2, rstd1, gt, p, ple_w, w_pg, g1, b1, g2, b2, layer, token=None):
    L = xh1.shape[0]
    extra, extra_specs = _after(token)

    def body(d_ref, xh2_ref, xh1_ref, rstd2_ref, rstd1_ref, gt_ref, p_ref, pw_ref, wpg_ref, g1_ref, b1_ref, g2_ref,
             b2_ref, *rest):
        (dt1_ref, dpw_out, dwpg_out, dbpg_ref, dg1_ref, db1_ref, dg2_ref, db2_ref, loss_ref, dpw_ref,
         dwpg_ref) = rest[len(extra):]
        @pl.when(pl.program_id(0) == 0)
        def _():
            for ref in (dpw_ref, dwpg_ref, dbpg_ref, dg1_ref, db1_ref, dg2_ref, db2_ref, loss_ref):
                ref[...] = jnp.zeros_like(ref)

        g1, g2 = _pick(g1_ref, layer), _pick(g2_ref, layer)
        xh1, xh2, rstd1, rstd2 = xh1_ref[...], xh2_ref[...], rstd1_ref[...], rstd2_ref[...]
        x1 = xh1 * g1 + _pick(b1_ref, layer)
        if is_top:
            err = (xh2 * g2 + _pick(b2_ref, layer)) - d_ref[...]
            loss_ref[...] += _colsum(err * err)
            dx2 = err * (1.0 / D_MODEL)
        else:
            dx2 = d_ref[...]
        p = p_ref[...]
        q, gt = _mm(p, pw_ref[...]), gt_ref[...]
        dg2_ref[...] += _colsum(dx2 * xh2)
        db2_ref[...] += _colsum(dx2)
        dt2 = _ln_bwd(dx2, xh2, rstd2, g2)
        dq = dt2 * gt
        dgpre = (dt2 * q) * gt * (1.0 - gt)
        dpw_ref[...] += _mm_tn(p, dq)
        dwpg_ref[...] += _mm_tn(x1, dgpre)
        dbpg_ref[...] += _colsum(dgpre)
        dx1 = ALPHA * dt2 + _mm_nt(dgpre, wpg_ref[...])
        dg1_ref[...] += _colsum(dx1 * xh1)
        db1_ref[...] += _colsum(dx1)
        dt1_ref[...] = _ln_bwd(dx1, xh1, rstd1, g1)

        @pl.when(pl.program_id(0) == L // TM - 1)
        def _():
            dpw_out[...] = dpw_ref[...].astype(WIRE)
            dwpg_out[...] = dwpg_ref[...].astype(WIRE)

    vec, lvec = _full((1, D_MODEL)), _lrow(layer, D_MODEL)
    return pl.pallas_call(
        body, name="post_bwd_a_top" if is_top else "post_bwd_a", grid=(L // TM,),
        in_specs=[_row(D_MODEL), _row(D_MODEL), _row(D_MODEL), _row(1), _row(1), _row(D_MODEL), _p_rows(layer),
                  _full((PLE_D, D_MODEL)), _full((D_MODEL, D_MODEL)), lvec, lvec, lvec, lvec] + extra_specs,
        out_specs=[_row(D_MODEL), _full((PLE_D, D_MODEL)), _full((D_MODEL, D_MODEL)), vec, vec, vec, vec, vec, vec],
        out_shape=[_S((L, D_MODEL)), _S((PLE_D, D_MODEL), WIRE), _S((D_MODEL, D_MODEL), WIRE)] + [_S((1, D_MODEL))] * 6,
        scratch_shapes=[pltpu.VMEM((PLE_D, D_MODEL), F32), pltpu.VMEM((D_MODEL, D_MODEL), F32)],
        compiler_params=_params(1))(dx2_or_target, xh2, xh1, rstd2, rstd1, gt, p, ple_w, w_pg, g1, b1, g2, b2, *extra)


def _post_bwd_b(dt1, z, hs, y0, w_out, w_glu, b_glu, layer):
    L = dt1.shape[0]

    def body(dt1_ref, zg_ref, zs0_ref, zs1_ref, zs2_ref, hs_ref, y0_ref, wo_ref, wg_ref, bg_ref,
             dhs_ref, dy0_ref, dzg_ref, dwo_out, dwg_out, dbg_ref, dwo_ref, dwg_ref):
        @pl.when(pl.program_id(0) == 0)
        def _():
            for ref in (dwo_ref, dwg_ref, dbg_ref):
                ref[...] = jnp.zeros_like(ref)

        dt1b = dt1_ref[...].astype(MXU)
        dm = _mm_nt(dt1b, wo_ref[...])
        d_rgy, d_s5y = dm[:, :RG_W], dm[:, RG_W:]
        rg_gate = zg_ref[...]
        s5_gate = jnp.concatenate([zs0_ref[...], zs1_ref[...], zs2_ref[...]], axis=1)
        hs = hs_ref[...]
        sl, dsl = _silu_and_grad(rg_gate)
        dhs_ref[...] = d_rgy * sl
        dzg_ref[:, :RG_W] = d_rgy * hs * dsl
        y0 = y0_ref[...]
        y1 = _gelu(y0)
        gl = _sigmoid(_mm(y1, wg_ref[...]) + _pick(bg_ref, layer))
        y2 = y1 * gl
        sl2, dsl = _silu_and_grad(s5_gate)
        m = jnp.concatenate([(hs * sl).astype(MXU), (y2 * sl2).astype(MXU)], axis=1)
        dwo_ref[...] += _mm_tn(m, dt1b)
        dy2 = d_s5y * sl2
        dzg_ref[:, RG_W:] = d_s5y * y2 * dsl
        dglpre = (dy2 * y1) * gl * (1.0 - gl)
        dwg_ref[...] += _mm_tn(y1, dglpre)
        dbg_ref[...] += _colsum(dglpre)
        dy1 = dy2 * gl + _mm_nt(dglpre, wg_ref[...])
        dy0_ref[...] = dy1 * _gelu_grad(y0)

        @pl.when(pl.program_id(0) == L // TM - 1)
        def _():
            dwo_out[...] = dwo_ref[...].astype(WIRE)
            dwg_out[...] = dwg_ref[...].astype(WIRE)

    return pl.pallas_call(
        body, name="post_bwd_b", grid=(L // TM,),
        in_specs=[_row(D_MODEL), *_gate_rows(), _row(RG_W), _row(S5_W), _full((D_MODEL, D_MODEL)),
                  _full((S5_W, S5_W)), _lrow(layer, S5_W)],
        out_specs=[_row(RG_W), _row(S5_W), _row(D_MODEL), _full((D_MODEL, D_MODEL)), _full((S5_W, S5_W)), _full((1, S5_W))],
        out_shape=[_S((L, RG_W)), _S((L, S5_W)), _S((L, D_MODEL)), _S((D_MODEL, D_MODEL), WIRE), _S((S5_W, S5_W), WIRE),
                   _S((1, S5_W))],
        scratch_shapes=[pltpu.VMEM((D_MODEL, D_MODEL), F32), pltpu.VMEM((S5_W, S5_W), F32)],
        compiler_params=_params(1))(dt1, z, z, z, z, hs, y0, w_out, w_glu, b_glu)


def _adamw(parts, w, m, v, token=None):
    nl = len(parts)
    extra, extra_specs = _after(token)
    n, R, C = parts[0].shape
    tr = R
    for cand in (512, 256, 128, 64, 32, 16, 8):
        if R % cand == 0 and n * cand * C * 4 <= 4 * 1024 * 1024:
            tr = cand
            break
    nblk = R // tr

    def body(*refs):
        p_refs = refs[:nl]
        w_ref, m_ref, v_ref = refs[nl:nl + 3]
        g_ref, d_ref, nm_ref, nv_ref = refs[nl + 3 + len(extra):]
        layer = pl.program_id(0)
        g = None
        for li, p_ref in enumerate(p_refs):
            s = p_ref[0].astype(F32)
            for k in range(1, n):
                s = s + p_ref[k].astype(F32)
            g = s if g is None else jnp.where(layer == li, s, g)
        nm = B1 * m_ref[...] + (1.0 - B1) * g
        nv = B2 * v_ref[...] + (1.0 - B2) * (g * g)
        d_ref[...] = (-LR) * ((nm / BC1) / (jnp.sqrt(nv / BC2) + EPS) + WD * w_ref[...])
        g_ref[...], nm_ref[...], nv_ref[...] = g, nm, nv

    def part_spec(li):
        return pl.BlockSpec((n, tr, C), lambda l, i: (0, jnp.where(l == li, i, jnp.where(l < li, 0, nblk - 1)), 0))

    blk = pl.BlockSpec((tr, C), lambda l, i: (l * nblk + i, 0))
    return pl.pallas_call(
        body, name="adamw", grid=(nl, nblk),
        in_specs=[part_spec(li) for li in range(nl)] + [blk, blk, blk] + extra_specs,
        out_specs=[blk] * 4, out_shape=[_S((nl * R, C))] * 4, compiler_params=_params(2))(*parts, w, m, v, *extra)


def _adamw_sharded(names, recv, w, m, v, name, token=None):
    n, nl = len(names), len(recv)
    extra, extra_specs = _after(token)
    n_in = n * (nl + 3)

    def body(*refs):
        outs = refs[n_in + len(extra):]
        for j in range(n):
            w_ref, m_ref, v_ref = (refs[(nl + t) * n + j] for t in range(3))
            g_ref, d_ref, nm_ref, nv_ref = (outs[t * n + j] for t in range(4))
            for l in range(nl):
                p_ref = refs[l * n + j]
                g = p_ref[0].astype(F32)
                for q in range(1, N_DEV):
                    g = g + p_ref[q].astype(F32)
                nm = B1 * m_ref[l] + (1.0 - B1) * g
                nv = B2 * v_ref[l] + (1.0 - B2) * (g * g)
                d_ref[l] = (-LR) * ((nm / BC1) / (jnp.sqrt(nv / BC2) + EPS) + WD * w_ref[l])
                g_ref[l], nm_ref[l], nv_ref[l] = g, nm, nv

    ins = [r[k] for r in recv for k in names] + [t[k] for t in (w, m, v) for k in names]
    vm = pl.BlockSpec(memory_space=pltpu.VMEM)
    outs = pl.pallas_call(body, name=name, in_specs=[vm] * n_in + extra_specs,
                          out_shape=[_S(w[k].shape) for _ in range(4) for k in names],
                          compiler_params=pltpu.CompilerParams(vmem_limit_bytes=VMEM_LIMIT))(*ins, *extra)
    return [{k: outs[t * n + j] for j, k in enumerate(names)} for t in range(4)]


def _adamw_packed(names, packed, w, m, v, name):
    n = len(names)
    starts, _ = _packed_starts([w[k].shape for k in names] + [(1, 1)])

    def body(p_ref, *refs):
        refs[-1][...] = p_ref[starts[-1]:starts[-1] + 1, :1]
        for j in range(n):
            w_ref, m_ref, v_ref, g_ref, d_ref, nm_ref, nv_ref = (refs[k * n + j] for k in range(7))
            gj = _packed_get(p_ref, starts[j], w_ref.shape)
            nm = B1 * m_ref[...] + (1.0 - B1) * gj
            nv = B2 * v_ref[...] + (1.0 - B2) * (gj * gj)
            d_ref[...] = (-LR) * ((nm / BC1) / (jnp.sqrt(nv / BC2) + EPS) + WD * w_ref[...])
            g_ref[...], nm_ref[...], nv_ref[...] = gj, nm, nv

    ins = [t[k] for t in (w, m, v) for k in names]
    outs = pl.pallas_call(body, name=name, out_shape=[_S(w[k].shape) for _ in range(4) for k in names] + [_S((1, 1))],
                          compiler_params=pltpu.CompilerParams(vmem_limit_bytes=VMEM_LIMIT))(packed, *ins)
    return [{k: outs[t * n + j] for j, k in enumerate(names)} for t in range(4)] + [outs[-1].reshape(())]


def _me():
    return lax.axis_index("x"), lax.axis_index("y"), lax.axis_index("c")


def _lin(dev):
    return 4 * dev[0] + 2 * dev[1] + dev[2]


def _blk(ref, axis, size, idx):
    nd = len(ref.shape)
    start = idx * size
    if axis == nd - 1 and size % LANE == 0:
        start = pl.multiple_of(start, LANE)
    elif axis == nd - 2 and size % 16 == 0:
        start = pl.multiple_of(start, 16)
    ix = [slice(None)] * nd
    ix[axis] = pl.ds(start, size)
    return ref.at[tuple(ix)]


HBM_SPEC = pl.BlockSpec(memory_space=pltpu.HBM)
SEM_SPEC = pl.BlockSpec(memory_space=pltpu.SEMAPHORE)
EFFECT = pltpu.SideEffectType.DATAFLOW_SIDE_EFFECTING


def _peers(x, y, c):
    flip = lambda v, f: 1 - v if f else v
    return [(flip(x, k & 4), flip(y, k & 2), flip(c, k & 1)) for k in range(1, N_DEV)]


def _land_shape(mode, s, axis):
    if mode == "gather":
        return s.shape[:axis] + (N_DEV * s.shape[axis],) + s.shape[axis + 1:]
    return (N_DEV,) + s.shape[:axis] + (s.shape[axis] // N_DEV,) + s.shape[axis + 1:]


def _src_view(mode, ref, axis, peer):
    return ref if mode == "gather" else _blk(ref, axis, ref.shape[axis] // N_DEV, peer)


def _dst_view(mode, land, axis, sender):
    return _blk(land, axis, land.shape[axis] // N_DEV, sender) if mode == "gather" else land.at[sender]


def _blocks(mode, land, axis, k):
    if mode == "gather":
        ix = [slice(None)] * len(land.shape)
        ix[axis] = pl.ds(0, k * (land.shape[axis] // N_DEV))
        return land.at[tuple(ix)]
    return land.at[pl.ds(0, k)]


ARRIVALS = {None: N_DEV - 1, "near": 4, "relay": 3}


def _routes(route, x, y, c):
    me, sibling = (x, y, c), (x, y, 1 - c)
    chips = [(1 - x, y), (x, 1 - y), (1 - x, 1 - y)]
    if route == "near":
        return [(me, sibling)] + [(me, (*chip, c)) for chip in chips]
    if route == "relay":
        return [((*chip, c), sibling) for chip in chips]
    return [(me, peer) for peer in _peers(x, y, c)]


def _place_own(mode, srcs, axes, name, after=None):
    n = len(srcs)
    extra, extra_specs = _after(after)

    def body(me_ref, *refs):
        for a in range(n):
            out = refs[n + len(extra) + a]
            out[...] = refs[a][...].reshape(out.shape)

    def at_me(shape, axis):
        return lambda i, me: tuple(me[0] if d == axis else 0 for d in range(len(shape)))

    in_specs, out_specs = [], []
    for s, axis in zip(srcs, axes):
        if mode == "gather":
            in_specs.append(pl.BlockSpec(s.shape, lambda i, me, nd=len(s.shape): (0,) * nd))
            out_specs.append(pl.BlockSpec(s.shape, at_me(s.shape, axis)))
        else:
            blk = s.shape[:axis] + (s.shape[axis] // N_DEV,) + s.shape[axis + 1:]
            in_specs.append(pl.BlockSpec(blk, at_me(blk, axis)))
            out_specs.append(pl.BlockSpec((1,) + blk, at_me((1,) + blk, 0)))
    me = _lin(_me()).astype(jnp.int32).reshape(1)
    return pl.pallas_call(
        body, name=name, out_shape=[_S(_land_shape(mode, s, a), s.dtype) for s, a in zip(srcs, axes)],
        grid_spec=pltpu.PrefetchScalarGridSpec(num_scalar_prefetch=1, grid=(1,), in_specs=in_specs + extra_specs,
                                               out_specs=out_specs),
        compiler_params=_params(1))(me, *srcs, *extra)


def _place_shards(shards, layers, axes, dtypes, name, after=None):
    n = len(shards)
    extra, extra_specs = _after(after)

    def body(me_ref, *refs):
        for a in range(n):
            out = refs[n + len(extra) + a]
            out[...] = refs[a][...].astype(out.dtype)

    in_specs, out_specs, out_shape = [], [], []
    for s, layer, axis, dt in zip(shards, layers, axes, dtypes):
        shape = s.shape if layer is None else s.shape[1:]
        nd = len(shape)
        if layer is None:
            in_specs.append(pl.BlockSpec(shape, lambda i, me, nd=nd: (0,) * nd))
        else:
            in_specs.append(pl.BlockSpec((None,) + shape, lambda i, me, nd=nd, layer=layer: (layer,) + (0,) * nd))
        out_specs.append(pl.BlockSpec(shape, lambda i, me, nd=nd, axis=axis: tuple(me[0] if d == axis else 0 for d in range(nd))))
        out_shape.append(_S(shape[:axis] + (N_DEV * shape[axis],) + shape[axis + 1:], dt))
    me = _lin(_me()).astype(jnp.int32).reshape(1)
    return pl.pallas_call(
        body, name=name, out_shape=out_shape,
        grid_spec=pltpu.PrefetchScalarGridSpec(num_scalar_prefetch=1, grid=(1,), in_specs=in_specs + extra_specs,
                                               out_specs=out_specs),
        compiler_params=_params(1))(me, *shards, *extra)


def _push_start(mode, srcs, lands, axes, name, route=None):
    n, ns = len(lands), len(srcs)

    def body(*refs):
        src_refs, land_refs = refs[:ns], refs[ns:ns + n]
        send_sems, recv_sems = refs[ns + n], refs[ns + n + 1]
        token = refs[-1]
        x, y, c = _me()
        for a in range(n):
            for block, peer in _routes(route, x, y, c):
                there = _dst_view(mode, land_refs[a], axes[a], _lin(block))
                pltpu.make_async_remote_copy(
                    src_ref=_src_view(mode, src_refs[a], axes[a], _lin(peer)) if ns else there, dst_ref=there,
                    send_sem=send_sems.at[a], recv_sem=recv_sems.at[a], device_id=peer, device_id_type=MESH).start()
        token[...] = jnp.zeros_like(token)

    hbm = lambda s: pltpu.HBM(s.shape, s.dtype)
    outs = pl.pallas_call(
        body, name=name,
        out_shape=(pltpu.SemaphoreType.DMA((n,)), pltpu.SemaphoreType.DMA((n,)), *[hbm(s) for s in srcs], *[hbm(s) for s in lands],
                   _S((SUB, LANE))),
        in_specs=[HBM_SPEC] * (ns + n),
        out_specs=(SEM_SPEC, SEM_SPEC, *[HBM_SPEC] * (ns + n), pl.BlockSpec(memory_space=pltpu.VMEM)),
        input_output_aliases={i: 2 + i for i in range(ns + n)},
        compiler_params=pltpu.CompilerParams(has_side_effects=EFFECT),
    )(*[pltpu.with_memory_space_constraint(s, pltpu.HBM) for s in list(srcs) + list(lands)])
    return outs[0], outs[1], outs[2:2 + ns], outs[2 + ns:2 + ns + n], outs[-1]


def _push_wait(mode, send_sems, recv_sems, srcs, lands, axes, after, name, first=0, route=None):
    n, ns = len(lands), len(srcs)
    after = list(after) if isinstance(after, (list, tuple)) else [after]

    def body(*refs):
        land_refs = refs[ns:ns + n]
        send_sems, recv_sems = refs[ns + n], refs[ns + n + 1]
        x, y, c = _me()
        for a in range(n):
            seven = _blocks(mode, land_refs[a], axes[a], ARRIVALS[route])
            cp = pltpu.make_async_remote_copy(src_ref=seven, dst_ref=seven, send_sem=send_sems.at[first + a],
                                              recv_sem=recv_sems.at[first + a],
                                              device_id=(x, y, 1 - c), device_id_type=MESH)
            cp.wait_send()
            cp.wait_recv()

    hbm = lambda s: pltpu.HBM(s.shape, s.dtype)
    outs = pl.pallas_call(
        body, name=name, out_shape=tuple(hbm(s) for s in list(srcs) + list(lands)),
        in_specs=[HBM_SPEC] * (ns + n) + [SEM_SPEC, SEM_SPEC] + [ANY] * len(after), out_specs=tuple([HBM_SPEC] * (ns + n)),
        input_output_aliases={i: i for i in range(ns + n)},
        compiler_params=pltpu.CompilerParams(has_side_effects=EFFECT),
    )(*srcs, *lands, send_sems, recv_sems, *after)
    return outs[ns:]


def _sum_parts(parts):
    n, R, C = parts.shape

    def body(p_ref, o_ref):
        g = p_ref[0]
        for k in range(1, n):
            g = g + p_ref[k]
        o_ref[...] = g

    return pl.pallas_call(body, name="sum_parts", out_shape=_S((R, C)))(parts)


SMALL =['conv_b', 'rg_wa', 'rg_ba', 'rg_wx', 'rg_bx', 'rg_lambda', 's5_a_re', 's5_a_im', 's5_b_re', 's5_b_im',
         's5_c_re', 's5_c_im', 's5_d', 's5_log_step', 's5_b_glu', 'ln1_g', 'ln1_b', 'ple_gate_b', 'ln2_g', 'ln2_b']
WEIGHTS = ['w_in', 'conv_w', 'conv_b', 'rg_wa', 'rg_ba', 'rg_wx', 'rg_bx', 'rg_lambda', 's5_a_re', 's5_a_im', 's5_b_re',
           's5_b_im', 's5_c_re', 's5_c_im', 's5_d', 's5_log_step', 's5_w_glu', 's5_b_glu', 'w_out', 'ln1_g', 'ln1_b',
           'ple_w', 'ple_gate_w', 'ple_gate_b', 'ln2_g', 'ln2_b']
PACK_ROWS_MULT = 64


STORED = {'s5_b_re': (2, 3), 's5_b_im': (2, 3), 's5_d': (1, 2)}


def _stored(k, a):
    return jnp.swapaxes(a, *STORED[k]) if k in STORED else a


def _two_d(a):
    return a.reshape(-1, a.shape[-1])


def _up8(n):
    return -(-n // SUB) * SUB


def _halves_fit(shape):
    return 2 * shape[1] == LANE and shape[0] % (2 * SUB) == 0


def _packed_rows(shape):
    R, C = shape
    if C % LANE == 0:
        return (C // LANE) * _up8(R)
    return R // 2 if _halves_fit(shape) else _up8(R)


def _packed_put(out_ref, r0, pieces):
    R, C = sum(a.shape[0] for a in pieces), pieces[0].shape[1]
    if _halves_fit((R, C)):
        lo, hi = pieces if len(pieces) == 2 else (pieces[0][:R // 2], pieces[0][R // 2:])
        out_ref[r0:r0 + R // 2, :] = jnp.concatenate([lo, hi], axis=1)
        return
    for a in pieces:
        rows = a.shape[0]
        if C % LANE == 0:
            for j in range(C // LANE):
                out_ref[r0 + j * _up8(R):r0 + j * _up8(R) + rows, :] = a[:, j * LANE:(j + 1) * LANE]
        else:
            out_ref[r0:r0 + rows, :C] = a
        r0 += rows


def _packed_get(ref, r0, shape):
    R, C = shape
    if C % LANE == 0:
        return jnp.concatenate([ref[r0 + j * _up8(R):r0 + j * _up8(R) + R, :] for j in range(C // LANE)], axis=1)
    if _halves_fit(shape):
        both = ref[r0:r0 + R // 2, :]
        return jnp.concatenate([both[:, :C], both[:, C:]], axis=0)
    return ref[r0:r0 + R, :C]


def _packed_starts(shapes):
    starts = [0]
    for s in shapes:
        starts.append(starts[-1] + _packed_rows(s))
    return starts[:-1], starts[-1] + (-starts[-1] % PACK_ROWS_MULT)


def _pack(tree, row, scale):
    groups = [[_two_d(a) for a in (tree[k] if isinstance(tree[k], list) else [tree[k]])] for k in SMALL]
    starts, rows = _packed_starts([(sum(a.shape[0] for a in g), g[0].shape[1]) for g in groups] + [(1, 1)])

    def body(*refs):
        row_ref, out_ref, refs = refs[-2], refs[-1], list(refs[:-2])
        out_ref[...] = jnp.zeros_like(out_ref)
        for r0, g in zip(starts, groups):
            _packed_put(out_ref, r0, [refs.pop(0)[...] for _ in g])
        out_ref[starts[-1]:starts[-1] + 1, :1] = scale * jnp.sum(row_ref[...], axis=1, keepdims=True)

    return pl.pallas_call(body, name="pack_small", out_shape=_S((rows, LANE)))(*sum(groups, []), row)


class _NoHooks:
    token = None
    first_token = None

    def first_weights(self, full, after):
        return full

    def layer_start(self, i, W, after):
        return W

    def late_weights(self, i, W, after):
        return W

    def post_done(self, i, g):
        return None

    def smalls_done(self, grads, loss_row):
        self.small = _small_grads(grads, self.res)
        return None

    def w_in_done(self, i, g):
        return None

    def layer_done(self, i, g, dx):
        return None


def _local_grads(x, p, target, W, disc, hooks):
    depth = 2
    saved = []
    for i in range(depth):
        if i > 0:
            W = hooks.layer_start(i, W, x)
        w = W[i]
        z = _inproj_fwd(x, w['w_in'], hooks.token if i == 0 else None)
        hs, *gates = _rg_fwd(z, w['conv_w'], w['conv_b'], w['wa_bd'], w['wx_bd'], w['rg_ba'], w['rg_bx'], w['rg_lambda'], i)
        d = disc[i]
        y0, s_re, s_im = _s5_fwd(z, d['bb_re'], d['bb_im'], d['lb_re'], d['lb_im'], d['c_re'], d['c_im'], w['s5_d'], i)
        W = hooks.late_weights(i, W, y0)
        w = W[i]
        x2, *norms = _post_fwd(x, hs, z, y0, p, w['s5_w_glu'], w['s5_b_glu'], w['w_out'], w['ln1_g'], w['ln1_b'],
                               w['ple_w'], w['ple_gate_w'], w['ple_gate_b'], w['ln2_g'], w['ln2_b'], i)
        saved.append((x, z, hs, gates, y0, s_re, s_im, norms))
        x = x2

    grads = [None] * depth
    dx = target
    loss = None
    token = None
    for i in reversed(range(depth)):
        w, d = W[i], disc[i]
        xin, z, hs, gates, y0, s_re, s_im, (xh1, xh2, gt, rstd1, rstd2) = saved[i]
        g = {}
        (dt1, g['ple_w'], g['ple_gate_w'], g['ple_gate_b'], g['ln1_g'], g['ln1_b'], g['ln2_g'], g['ln2_b'], lrow) = _post_bwd_a(
            dx, i == depth - 1, xh2, xh1, rstd2, rstd1, gt, p, w['ple_w'], w['ple_gate_w'], w['ln1_g'], w['ln1_b'],
            w['ln2_g'], w['ln2_b'], i, token)
        if i == depth - 1:
            loss_row, loss = lrow, 0.5 / D_MODEL * jnp.sum(lrow)
        dhs, dy0, dzg, g['w_out'], g['s5_w_glu'], g['s5_b_glu'] = _post_bwd_b(dt1, z, hs, y0, w['w_out'], w['s5_w_glu'],
                                                                           w['s5_b_glu'], i)
        (dzu, g['bb_re'], g['bb_im'], g['lb_re'], g['lb_im'], g['c_re'], g['c_im'], g['s5_d']) = _s5_bwd(
            dy0, z, s_re, s_im, d['bb_re'], d['bb_im'], d['lb_re'], d['lb_im'], d['c_re'], d['c_im'], w['s5_d'], i,
            hooks.post_done(i, g))
        (dzx, g['conv_w'], g['conv_b'], g['wa_bd'], g['wx_bd'], g['rg_ba'], g['rg_bx'], g['rg_lambda']) = _rg_bwd(
            dhs, z, hs, gates, w['conv_w'], w['wa_bd'], w['wx_bd'], w['rg_lambda'], i)
        if i == 0:
            g['w_in'] = _inproj_bwd_dw(xin, dzx, dzg, dzu, hooks.smalls_done([g, grads[1]], loss_row))
            dx = _inproj_bwd_dx(dt1, dzx, dzg, dzu, w['w_in'], hooks.w_in_done(i, g))
        else:
            dx, g['w_in'] = _inproj_bwd(dt1, xin, dzx, dzg, dzu, w['w_in'])
        grads[i] = g
        token = hooks.layer_done(i, g, dx)
    return loss, dx, grads


def _s5_layouts_fwd(s5_a_re, s5_a_im, s5_log_step, s5_b_re, s5_b_im, s5_c_re, s5_c_im, token=None):
    depth = s5_a_re.shape[0]
    ar, ai = s5_a_re.reshape(depth * 24, S5_P), s5_a_im.reshape(depth * 24, S5_P)
    ls = s5_log_step.reshape(depth * 24, 1)
    lr, li, cr, ci = _s5_disc_fwd(ar, ai, ls, token)
    per_group = lambda a: a.reshape(depth * 24, 1, S5_P)
    as_c = lambda b: jnp.swapaxes(b, 2, 3).reshape(depth * 24, S5_H, S5_P)
    res = (ar, ai, ls, per_group(cr), per_group(ci), as_c(s5_b_re), as_c(s5_b_im))
    bbr, bbi = _s5_bscale_fwd(*res[3:])
    tiles = lambda a: a.reshape(depth * N_S5_T, S5_GT, S5_H, S5_P)
    rows = lambda a: a.reshape(depth * N_S5_T, S5_GT, S5_P)
    disc = dict(bb_re=tiles(bbr), bb_im=tiles(bbi), lb_re=rows(lr), lb_im=rows(li), c_re=tiles(s5_c_re), c_im=tiles(s5_c_im))
    return [disc] * depth, res


def _s5_layouts_bwd(grads, res):
    ar, ai, ls, cr, ci, br, bi = res
    depth = len(grads)
    layers = lambda k, *shape: [g[k].reshape(shape) for g in grads]
    shape_c = (depth, 24, S5_H, S5_P)
    dbr, dbi, dcr, dci = _s5_bscale_bwd(cr, ci, br, bi, layers('bb_re', 24, S5_H, S5_P), layers('bb_im', 24, S5_H, S5_P))
    gp = (depth * 24, S5_P)
    dar, dai, dls = _s5_disc_bwd(ar, ai, ls, layers('lb_re', 24, S5_P), layers('lb_im', 24, S5_P), dcr.reshape(gp),
                                 dci.reshape(gp))
    return dict(
        s5_a_re=dar.reshape(depth, 24, S5_P), s5_a_im=dai.reshape(depth, 24, S5_P), s5_log_step=dls.reshape(depth, 24),
        s5_b_re=dbr.reshape(shape_c), s5_b_im=dbi.reshape(shape_c),
        s5_c_re=layers('c_re', 24, S5_H, S5_P), s5_c_im=layers('c_im', 24, S5_H, S5_P))


LATE = ('w_out', 'ple_w', 'ple_gate_w', 's5_w_glu')


ROWS = ('conv_b', 'rg_ba', 'rg_bx', 'rg_lambda', 's5_d', 's5_b_glu', 'ln1_g', 'ln1_b', 'ple_gate_b', 'ln2_g', 'ln2_b')


def _shared_weights(full):
    shared = {k: full[k] for k in ROWS}
    shared.update(conv_w=full['conv_w'], wa_bd=full['rg_wa'], wx_bd=full['rg_wx'], s5_d=full['s5_d'].reshape(DEPTH, 1, S5_W))
    return shared


def _layer_weights(full, shared, i):
    return dict(shared, w_in=full['w_in'][i])


class _AllLocal(_NoHooks):
    def __init__(self, full):
        self.full = full

    def late_weights(self, i, W, after):
        W[i].update({k: self.full[k][i] for k in LATE})
        return W


def _full_grads(full, x, p, target, hooks=None):
    hooks = hooks or _AllLocal(full)
    disc, res = _s5_layouts_fwd(full['s5_a_re'], full['s5_a_im'], full['s5_log_step'], full['s5_b_re'], full['s5_b_im'],
                                full['s5_c_re'], full['s5_c_im'], hooks.first_token)
    full = hooks.first_weights(full, disc[-1]['bb_im'])
    shared = _shared_weights(full)
    W = [_layer_weights(full, shared, i) for i in range(2)]
    hooks.res = res
    loss, gx, grads = _local_grads(x, p, target, W, disc, hooks)
    out = dict(hooks.small)
    for k in SHARD_AXIS:
        out[k] = [g[k] for g in grads]
    return loss, gx, out


def _small_grads(grads, res):
    stack = lambda f: jnp.stack([f(g) for g in grads])
    out = _s5_layouts_bwd(grads, res)
    out['conv_w'] = stack(lambda g: g['conv_w'])
    for k in ('conv_b', 'rg_ba', 'rg_bx', 'rg_lambda', 's5_b_glu', 'ln1_g', 'ln1_b', 'ple_gate_b', 'ln2_g', 'ln2_b'):
        out[k] = [g[k] for g in grads]
    out['s5_d'] = _stored('s5_d', stack(lambda g: g['s5_d'][0]).reshape(2, 24, 16))
    out['rg_wa'] = [g['wa_bd'] for g in grads]
    out['rg_wx'] = [g['wx_bd'] for g in grads]
    return out


SHARD_AXIS = {'w_in': 2, 'w_out': 1, 'ple_w': 2, 'ple_gate_w': 1, 's5_w_glu': 1}


def kernel(x, p, w_in, conv_w, conv_b, rg_wa, rg_ba, rg_wx, rg_bx, rg_lambda, s5_a_re, s5_a_im, s5_b_re, s5_b_im, s5_c_re, s5_c_im, s5_d, s5_log_step, s5_w_glu, s5_b_glu, w_out, ln1_g, ln1_b, ple_w, ple_gate_w, ple_gate_b, ln2_g, ln2_b, loss_target, m_w_in, m_conv_w, m_conv_b, m_rg_wa, m_rg_ba, m_rg_wx, m_rg_bx, m_rg_lambda, m_s5_a_re, m_s5_a_im, m_s5_b_re, m_s5_b_im, m_s5_c_re, m_s5_c_im, m_s5_d, m_s5_log_step, m_s5_w_glu, m_s5_b_glu, m_w_out, m_ln1_g, m_ln1_b, m_ple_w, m_ple_gate_w, m_ple_gate_b, m_ln2_g, m_ln2_b, v_w_in, v_conv_w, v_conv_b, v_rg_wa, v_rg_ba, v_rg_wx, v_rg_bx, v_rg_lambda, v_s5_a_re, v_s5_a_im, v_s5_b_re, v_s5_b_im, v_s5_c_re, v_s5_c_im, v_s5_d, v_s5_log_step, v_s5_w_glu, v_s5_b_glu, v_w_out, v_ln1_g, v_ln1_b, v_ple_w, v_ple_gate_w, v_ple_gate_b, v_ln2_g, v_ln2_b):
    local = dict(locals())
    w = {k: local[k] for k in WEIGHTS}
    mom = {k: local['m_' + k] for k in WEIGHTS}
    var = {k: local['v_' + k] for k in WEIGHTS}

    big = list(SHARD_AXIS)
    late_axes = [SHARD_AXIS[k] - 1 for k in LATE]
    pushed = {}

    groups = dict(first=(['w_in', 'conv_w'], [0, None], [1, 0]), l0=(list(LATE), [0] * len(LATE), late_axes),
                  l1=(['w_in'] + list(LATE), [1] * (1 + len(LATE)), [1] + late_axes))
    token = None
    for key, members in (("first", ["first"]), ("rest", ["l0", "l1"])):
        names, layers, axes = (sum((groups[m][j] for m in members), []) for j in range(3))
        shards = [w[k] if layer is not None else w[k][None] for k, layer in zip(names, layers)]
        lands = _place_shards(shards, layers, axes, [WIRE if k in big else w[k].dtype for k in names],
                              "place_weights_" + key, token)
        pushed[key] = _push_start("gather", [], lands, axes, "push_weights_" + key, "near" if key == "first" else None)
        token = pushed[key][4]

    def await_weights(key, axes, after):
        s, first = pushed["rest"], 0 if key == "l0" else len(LATE)
        return _push_wait("gather", s[0], s[1], [], s[3][first:first + len(axes)], axes, after, "await_weights_" + key, first)

    def push_grads(key, g, names, axes):
        srcs = [g[k] for k in names]
        pushed[key] = _push_start("scatter", srcs, _place_own("scatter", srcs, axes, "place_grads_" + key), axes,
                                  "push_grads_" + key)
        return pushed[key][4]

    def await_grads(key, axes, after):
        s = pushed[key]
        return _push_wait("scatter", s[0], s[1], s[2], s[3], axes, after, "await_grads_" + key)

    class Overlap(_NoHooks):
        token = pushed["rest"][4]
        first_token = token

        def first_weights(self, full, after):
            s, axes = pushed["first"], [1, 0]
            near = _push_wait("gather", s[0], s[1], [], s[3], axes, after, "await_weights_near", route="near")
            s = _push_start("gather", [], near, axes, "relay_weights", "relay")
            w_in0, conv = _push_wait("gather", s[0], s[1], [], s[3], axes, s[4], "await_weights_relay", route="relay")
            return dict(full, w_in=[w_in0, None], conv_w=jnp.moveaxis(conv, 0, 2).reshape(2, 4, RG_W))

        def late_weights(self, i, W, after):
            if i == 0:
                W[0].update(zip(LATE, await_weights("l0", late_axes, after)))
            return W

        def layer_start(self, i, W, after):
            lands = await_weights("l1", [1] + late_axes, after)
            W[1].update(zip(LATE, lands[1:]), w_in=lands[0])
            return W

        def post_done(self, i, g):
            return push_grads("late0", g, LATE, late_axes) if i == 0 else None

        def smalls_done(self, grads, loss_row):
            super().smalls_done(grads, loss_row)
            conv = jnp.moveaxis(self.small['conv_w'].reshape(2, 4, N_DEV, RG_W // N_DEV), 2, 0)
            self.packed = _pack(self.small, loss_row, 0.5 / D_MODEL)
            return push_grads("small", dict(conv_w=conv.reshape(N_DEV, 8, RG_W // N_DEV), small=self.packed),
                              ['conv_w', 'small'], [0, 0])

        def w_in_done(self, i, g):
            return push_grads("w_in0", g, ['w_in'], [0])

        def layer_done(self, i, g, dx):
            return push_grads("all1", g, ['w_in'] + list(LATE), [0] + late_axes) if i == 1 else None

    hooks = Overlap()
    _, grad_x, g = _full_grads(dict(w), x[0], p, loss_target[0], hooks)

    recv1 = dict(zip(['w_in'] + list(LATE), await_grads("all1", [0] + late_axes, grad_x)))
    recv0 = dict(zip(LATE, await_grads("late0", late_axes, grad_x)))
    outs = {}

    def update(k, parts, token=None):
        shard = w[k].shape
        c = shard[-1]
        two = lambda a: a.reshape(-1, c)
        res = _adamw([r.reshape(N_DEV, -1, c) for r in parts], two(w[k]), two(mom[k]), two(var[k]), token)
        outs[k] = [o.reshape(shard) for o in res]

    conv_parts, small_parts = await_grads("small", [0, 0], grad_x)
    rows = hooks.packed.shape[0] // N_DEV
    mine = _sum_parts(small_parts.reshape(N_DEV, rows, LANE))
    sums = _push_start("gather", [mine], _place_own("gather", [mine], [0], "place_small_sums"), [0], "push_small_sums")
    late = _adamw_sharded(list(LATE), [recv0, recv1], w, mom, var, "adamw_late", sums[4])
    for k in LATE:
        outs[k] = [t[k] for t in late]
    w_in0, = await_grads("w_in0", [0], [outs[k][1] for k in LATE])
    update('w_in', [w_in0, recv1['w_in']])
    update('conv_w', [conv_parts])
    gathered, = _push_wait("gather", sums[0], sums[1], sums[2], sums[3], [0], [outs['w_in'][1], outs['conv_w'][1]],
                           "await_small_sums")
    stored = [{k: _two_d(_stored(k, t[k])) for k in SMALL} for t in (w, mom, var)]
    *updated, loss = _adamw_packed(SMALL, gathered, *stored, "adamw_small")
    for k in SMALL:
        shape = _stored(k, w[k]).shape
        outs[k] = [_stored(k, o[k].reshape(shape)) for o in updated]

    res = [loss, grad_x[None]]
    for j in range(4):
        res += [outs[k][j] for k in WEIGHTS]
    return tuple(res)
```
